```python
import jax, jax.numpy as jnp
from jax import lax
import numpy as np

D_MODEL = 1024
BATCH = 16
SEQ = 2048
DEPTH = 1

EPS = 1e-6
POOL_WINDOWS = (2, 4, 8, 16)
N_POOL_GROUPS = len(POOL_WINDOWS)
POOL_WIDTH = D_MODEL // 2
POOL_GC = POOL_WIDTH // N_POOL_GROUPS
HEAD_DIM = 64
N_Q_HEADS = (D_MODEL // 2) // HEAD_DIM
N_KV_HEADS = 2
GROUP = N_Q_HEADS // N_KV_HEADS
ATTN_WIDTH = N_Q_HEADS * HEAD_DIM
KV_WIDTH = N_KV_HEADS * HEAD_DIM
WINDOW = 128
BLOCK = 128
NEG_INF = -1e30
ROPE_THETA = 500000.0
ROT_DIM = HEAD_DIM // 4
N_BRANCHES = 2
GATE_WIDTH = N_BRANCHES * D_MODEL
IN_SPLITS = (POOL_WIDTH, POOL_WIDTH + ATTN_WIDTH, POOL_WIDTH + ATTN_WIDTH + KV_WIDTH,
             POOL_WIDTH + ATTN_WIDTH + 2 * KV_WIDTH)
IN_WIDTH = POOL_WIDTH + ATTN_WIDTH + 2 * KV_WIDTH + GATE_WIDTH
D_FF = 4 * D_MODEL

kernel_name = "hybrid_pool_swa_sink_gated_block"


def _rmsnorm(x, g):
    xf = x.astype(jnp.float32)
    r = lax.rsqrt(jnp.mean(xf * xf, axis=-1, keepdims=True) + EPS)
    return (xf * r * g.astype(jnp.float32)).astype(x.dtype)


def _partial_rotary(t, cos, sin):
    half = ROT_DIM // 2
    t1 = t[..., :half]
    t2 = t[..., half:ROT_DIM]
    c = cos[None, :, None, :].astype(t.dtype)
    s = sin[None, :, None, :].astype(t.dtype)
    return jnp.concatenate([t1 * c - t2 * s, t2 * c + t1 * s, t[..., ROT_DIM:]], axis=-1)


def _multiscale_pool(u, w_pool, pool_scale):
    B, S, _ = u.shape
    uf = u.astype(jnp.float32)
    cs = jnp.pad(jnp.cumsum(uf, axis=1), ((0, 0), (1, 0), (0, 0)))
    t = jnp.arange(S)
    pooled = []
    for gi, w in enumerate(POOL_WINDOWS):
        c = cs[..., gi * POOL_GC:(gi + 1) * POOL_GC]
        upper = c[:, 1:]
        lower = jnp.pad(c[:, :S + 1 - w], ((0, 0), (w - 1, 0), (0, 0)))
        count = jnp.minimum(t + 1, w).astype(jnp.float32)[None, :, None]
        pooled.append((upper - lower) / count)
    pooled = jnp.stack(pooled, axis=2)
    diff = (pooled - uf.reshape(B, S, N_POOL_GROUPS, POOL_GC)).astype(u.dtype)
    mixed = jnp.einsum('bsgc,gcd->bsgd', diff, w_pool)
    return mixed.reshape(B, S, POOL_WIDTH) * pool_scale


def _sliding_window_sink_attention(q, k, v, sinks):
    B, S = q.shape[0], q.shape[1]
    nb = S // BLOCK
    qb = q.reshape(B, nb, BLOCK, N_KV_HEADS, GROUP, HEAD_DIM)

    def with_prev(t):
        tb = t.reshape(B, nb, BLOCK, N_KV_HEADS, HEAD_DIM)
        prev = jnp.pad(tb[:, :-1], ((0, 0), (1, 0), (0, 0), (0, 0), (0, 0)))
        return jnp.concatenate([prev, tb], axis=2)

    kk = with_prev(k)
    vv = with_prev(v)
    scale = HEAD_DIM ** -0.5
    s = jnp.einsum('bnqhgd,bnkhd->bnhgqk', qb, kk).astype(jnp.float32) * scale
    qi = jnp.arange(BLOCK)[:, None]
    kj = jnp.arange(2 * BLOCK)[None, :]
    rel = qi + BLOCK - kj
    band = (rel >= 0) & (rel < WINDOW)
    has_prev = (jnp.arange(nb) > 0)[:, None, None] | (kj >= BLOCK)[None]
    valid = band[None] & has_prev
    s = jnp.where(valid[None, :, None, None], s, NEG_INF)
    sink = jnp.broadcast_to(sinks.astype(jnp.float32).reshape(1, 1, N_KV_HEADS, GROUP, 1, 1),
                            s.shape[:-1] + (1,))
    p = jax.nn.softmax(jnp.concatenate([s, sink], axis=-1), axis=-1)[..., :-1]
    o = jnp.einsum('bnhgqk,bnkhd->bnqhgd', p.astype(v.dtype), vv)
    return o.reshape(B, S, ATTN_WIDTH)


def _fwd_setup_inputs(seed: int = 0) -> dict:
    key = jax.random.key(seed)
    ks = jax.random.split(key, 16)
    nrm = jax.random.normal
    f32 = jnp.float32

    def gain(k):
        return 1.0 + 0.1 * nrm(k, (DEPTH, D_MODEL), f32)

    return {
        "x": nrm(ks[0], (BATCH, SEQ, D_MODEL), f32),
        "g_mix_pre": gain(ks[1]),
        "w_in": nrm(ks[2], (DEPTH, D_MODEL, IN_WIDTH), f32) * D_MODEL ** -0.5,
        "b_in": 0.02 * nrm(ks[3], (DEPTH, IN_WIDTH), f32),
        "w_pool": nrm(ks[4], (DEPTH, N_POOL_GROUPS, POOL_GC, POOL_GC), f32) * POOL_GC ** -0.5,
        "pool_scale": 1.0 + 0.1 * nrm(ks[5], (DEPTH, POOL_WIDTH), f32),
        "attn_sinks": 0.5 * nrm(ks[6], (DEPTH, N_Q_HEADS), f32),
        "w_branch_pool": nrm(ks[7], (DEPTH, POOL_WIDTH, D_MODEL), f32) * POOL_WIDTH ** -0.5,
        "w_branch_attn": nrm(ks[8], (DEPTH, ATTN_WIDTH, D_MODEL), f32) * ATTN_WIDTH ** -0.5,
        "w_out": nrm(ks[9], (DEPTH, D_MODEL, D_MODEL), f32) * D_MODEL ** -0.5,
        "g_mix_post": gain(ks[10]),
        "g_mlp_pre": gain(ks[11]),
        "w_up": nrm(ks[12], (DEPTH, D_MODEL, D_FF), f32) * D_MODEL ** -0.5,
        "w_down": nrm(ks[13], (DEPTH, D_FF, D_MODEL), f32) * D_FF ** -0.5,
        "g_mlp_post": gain(ks[14]),
    }


def _fwd_reference(x, g_mix_pre, w_in, b_in, w_pool, pool_scale, attn_sinks, w_branch_pool,
              w_branch_attn, w_out, g_mix_post, g_mlp_pre, w_up, w_down, g_mlp_post):
    B, S, _ = x.shape
    pos = jnp.arange(S, dtype=jnp.float32)
    inv_freq = ROPE_THETA ** (-jnp.arange(0, ROT_DIM, 2, dtype=jnp.float32) / ROT_DIM)
    ang = pos[:, None] * inv_freq[None, :]
    cos, sin = jnp.cos(ang), jnp.sin(ang)

    for l in range(DEPTH):
        h = _rmsnorm(x, g_mix_pre[l])
        proj = jnp.einsum('bsd,de->bse', h, w_in[l]) + b_in[l]
        u_pool, q, k, v, gates = jnp.split(proj, IN_SPLITS, axis=-1)

        y_pool = _multiscale_pool(u_pool, w_pool[l], pool_scale[l])

        q = _partial_rotary(q.reshape(B, S, N_Q_HEADS, HEAD_DIM), cos, sin)
        k = _partial_rotary(k.reshape(B, S, N_KV_HEADS, HEAD_DIM), cos, sin)
        v = v.reshape(B, S, N_KV_HEADS, HEAD_DIM)
        y_attn = _sliding_window_sink_attention(
            q.reshape(B, S, N_KV_HEADS, GROUP, HEAD_DIM), k, v, attn_sinks[l])

        g = jax.nn.sigmoid(gates.astype(jnp.float32)).astype(x.dtype)
        g_pool, g_attn = g[..., :D_MODEL], g[..., D_MODEL:]
        merged = (g_pool * jnp.einsum('bsc,cd->bsd', y_pool, w_branch_pool[l])
                  + g_attn * jnp.einsum('bsc,cd->bsd', y_attn, w_branch_attn[l]))
        mix = jnp.einsum('bsd,de->bse', merged, w_out[l])
        x = x + _rmsnorm(mix, g_mix_post[l])

        h2 = _rmsnorm(x, g_mlp_pre[l])
        ff = jnp.einsum('bsf,fd->bsd',
                        jnp.square(jax.nn.relu(jnp.einsum('bsd,df->bsf', h2, w_up[l]))), w_down[l])
        x = x + _rmsnorm(ff, g_mlp_post[l])
    return x


import jax as _jax
import jax.numpy as _jnp

TWIN_FORMAT = 'train_step'
FWD_PARAMS = ['x', 'g_mix_pre', 'w_in', 'b_in', 'w_pool', 'pool_scale', 'attn_sinks', 'w_branch_pool', 'w_branch_attn', 'w_out', 'g_mix_post', 'g_mlp_pre', 'w_up', 'w_down', 'g_mlp_post']
TWIN_WEIGHTS = ['g_mix_pre', 'w_in', 'b_in', 'w_pool', 'pool_scale', 'attn_sinks', 'w_branch_pool', 'w_branch_attn', 'w_out', 'g_mix_post', 'g_mlp_pre', 'w_up', 'w_down', 'g_mlp_post']
TWIN_DIFF_INPUT = 'x'
TWIN_INPUTS = ['x', 'g_mix_pre', 'w_in', 'b_in', 'w_pool', 'pool_scale', 'attn_sinks', 'w_branch_pool', 'w_branch_attn', 'w_out', 'g_mix_post', 'g_mlp_pre', 'w_up', 'w_down', 'g_mlp_post', 'loss_target', 'm_g_mix_pre', 'm_w_in', 'm_b_in', 'm_w_pool', 'm_pool_scale', 'm_attn_sinks', 'm_w_branch_pool', 'm_w_branch_attn', 'm_w_out', 'm_g_mix_post', 'm_g_mlp_pre', 'm_w_up', 'm_w_down', 'm_g_mlp_post', 'v_g_mix_pre', 'v_w_in', 'v_b_in', 'v_w_pool', 'v_pool_scale', 'v_attn_sinks', 'v_w_branch_pool', 'v_w_branch_attn', 'v_w_out', 'v_g_mix_post', 'v_g_mlp_pre', 'v_w_up', 'v_w_down', 'v_g_mlp_post']
TWIN_OUTPUTS = ['loss', 'grad_x', 'grad_g_mix_pre', 'grad_w_in', 'grad_b_in', 'grad_w_pool', 'grad_pool_scale', 'grad_attn_sinks', 'grad_w_branch_pool', 'grad_w_branch_attn', 'grad_w_out', 'grad_g_mix_post', 'grad_g_mlp_pre', 'grad_w_up', 'grad_w_down', 'grad_g_mlp_post', 'delta_g_mix_pre', 'delta_w_in', 'delta_b_in', 'delta_w_pool', 'delta_pool_scale', 'delta_attn_sinks', 'delta_w_branch_pool', 'delta_w_branch_attn', 'delta_w_out', 'delta_g_mix_post', 'delta_g_mlp_pre', 'delta_w_up', 'delta_w_down', 'delta_g_mlp_post', 'new_m_g_mix_pre', 'new_m_w_in', 'new_m_b_in', 'new_m_w_pool', 'new_m_pool_scale', 'new_m_attn_sinks', 'new_m_w_branch_pool', 'new_m_w_branch_attn', 'new_m_w_out', 'new_m_g_mix_post', 'new_m_g_mlp_pre', 'new_m_w_up', 'new_m_w_down', 'new_m_g_mlp_post', 'new_v_g_mix_pre', 'new_v_w_in', 'new_v_b_in', 'new_v_w_pool', 'new_v_pool_scale', 'new_v_attn_sinks', 'new_v_w_branch_pool', 'new_v_w_branch_attn', 'new_v_w_out', 'new_v_g_mix_post', 'new_v_g_mlp_pre', 'new_v_w_up', 'new_v_w_down', 'new_v_g_mlp_post']
TWIN_LEAF_KINDS = {'loss': 'loss', 'grad_x': 'grad_x', 'grad_g_mix_pre': 'grad_w', 'grad_w_in': 'grad_w', 'grad_b_in': 'grad_w', 'grad_w_pool': 'grad_w', 'grad_pool_scale': 'grad_w', 'grad_attn_sinks': 'grad_w', 'grad_w_branch_pool': 'grad_w', 'grad_w_branch_attn': 'grad_w', 'grad_w_out': 'grad_w', 'grad_g_mix_post': 'grad_w', 'grad_g_mlp_pre': 'grad_w', 'grad_w_up': 'grad_w', 'grad_w_down': 'grad_w', 'grad_g_mlp_post': 'grad_w', 'delta_g_mix_pre': 'delta_w', 'delta_w_in': 'delta_w', 'delta_b_in': 'delta_w', 'delta_w_pool': 'delta_w', 'delta_pool_scale': 'delta_w', 'delta_attn_sinks': 'delta_w', 'delta_w_branch_pool': 'delta_w', 'delta_w_branch_attn': 'delta_w', 'delta_w_out': 'delta_w', 'delta_g_mix_post': 'delta_w', 'delta_g_mlp_pre': 'delta_w', 'delta_w_up': 'delta_w', 'delta_w_down': 'delta_w', 'delta_g_mlp_post': 'delta_w', 'new_m_g_mix_pre': 'new_m', 'new_m_w_in': 'new_m', 'new_m_b_in': 'new_m', 'new_m_w_pool': 'new_m', 'new_m_pool_scale': 'new_m', 'new_m_attn_sinks': 'new_m', 'new_m_w_branch_pool': 'new_m', 'new_m_w_branch_attn': 'new_m', 'new_m_w_out': 'new_m', 'new_m_g_mix_post': 'new_m', 'new_m_g_mlp_pre': 'new_m', 'new_m_w_up': 'new_m', 'new_m_w_down': 'new_m', 'new_m_g_mlp_post': 'new_m', 'new_v_g_mix_pre': 'new_v', 'new_v_w_in': 'new_v', 'new_v_b_in': 'new_v', 'new_v_w_pool': 'new_v', 'new_v_pool_scale': 'new_v', 'new_v_attn_sinks': 'new_v', 'new_v_w_branch_pool': 'new_v', 'new_v_w_branch_attn': 'new_v', 'new_v_w_out': 'new_v', 'new_v_g_mix_post': 'new_v', 'new_v_g_mlp_pre': 'new_v', 'new_v_w_up': 'new_v', 'new_v_w_down': 'new_v', 'new_v_g_mlp_post': 'new_v'}


def _forward(args):
    return _fwd_reference(*[args[k] for k in FWD_PARAMS])


def _output_shape():
    out = _jax.eval_shape(lambda: _forward(_fwd_setup_inputs(0)))
    return out.shape, out.dtype

N_MICROBATCH = 1
ADAM_LR = 0.001
ADAM_B1 = 0.9
ADAM_B2 = 0.999
ADAM_EPS = 1e-08
ADAM_WD = 0.01
ADAM_STEP = 10
PER_EXAMPLE_BATCH_AXIS = {'x': 0, 'loss_target': 0}
SHARED_INPUTS = []
_WEIGHT_DTYPES = {'g_mix_pre': _jnp.float32, 'w_in': _jnp.float32, 'b_in': _jnp.float32, 'w_pool': _jnp.float32, 'pool_scale': _jnp.float32, 'attn_sinks': _jnp.float32, 'w_branch_pool': _jnp.float32, 'w_branch_attn': _jnp.float32, 'w_out': _jnp.float32, 'g_mix_post': _jnp.float32, 'g_mlp_pre': _jnp.float32, 'w_up': _jnp.float32, 'w_down': _jnp.float32, 'g_mlp_post': _jnp.float32}
MOMENT_SCALE = {'g_mix_pre': 8.692892e-01, 'w_in': 4.435002e-01, 'b_in': 7.036433e+00, 'w_pool': 1.429632e+00, 'pool_scale': 1.914983e+00, 'attn_sinks': 1.045618e-01, 'w_branch_pool': 1.222384e+00, 'w_branch_attn': 1.396273e-01, 'w_out': 1.331106e+00, 'g_mix_post': 3.237432e+01, 'g_mlp_pre': 1.149702e+00, 'w_up': 5.487525e-01, 'w_down': 1.421781e+00, 'g_mlp_post': 3.299142e+01}


def _to_microbatches(a, axis):
    t = _jnp.moveaxis(a, axis, 0)
    t = t.reshape((N_MICROBATCH, t.shape[0] // N_MICROBATCH) + t.shape[1:])
    return _jnp.moveaxis(t, 1, axis + 1)


def setup_inputs(seed: int = 0) -> dict:
    inp = _fwd_setup_inputs(seed)
    key = _jax.random.fold_in(_jax.random.key(seed), 7919)
    shape, _ = _output_shape()
    out = dict(inp)
    out["loss_target"] = _jax.random.normal(_jax.random.fold_in(key, 0), shape, _jnp.float32)
    for i, name in enumerate(TWIN_WEIGHTS):
        w = inp[name].astype(_jnp.float32)
        if MOMENT_SCALE is None:
            s = _jnp.sqrt(_jnp.mean(_jnp.square(w)) + 1e-30)
        else:
            s = MOMENT_SCALE[name]
        km, kv = _jax.random.split(_jax.random.fold_in(key, i + 1))
        out[name] = w
        out["m_" + name] = s * _jax.random.normal(km, w.shape, _jnp.float32)
        out["v_" + name] = (s * s) * _jax.random.uniform(kv, w.shape, _jnp.float32, 0.5, 1.5)
    if N_MICROBATCH > 1:
        for name, axis in PER_EXAMPLE_BATCH_AXIS.items():
            out[name] = _to_microbatches(out[name], axis)
    return {'x': out['x'], 'g_mix_pre': out['g_mix_pre'], 'w_in': out['w_in'], 'b_in': out['b_in'], 'w_pool': out['w_pool'], 'pool_scale': out['pool_scale'], 'attn_sinks': out['attn_sinks'], 'w_branch_pool': out['w_branch_pool'], 'w_branch_attn': out['w_branch_attn'], 'w_out': out['w_out'], 'g_mix_post': out['g_mix_post'], 'g_mlp_pre': out['g_mlp_pre'], 'w_up': out['w_up'], 'w_down': out['w_down'], 'g_mlp_post': out['g_mlp_post'], 'loss_target': out['loss_target'], 'm_g_mix_pre': out['m_g_mix_pre'], 'm_w_in': out['m_w_in'], 'm_b_in': out['m_b_in'], 'm_w_pool': out['m_w_pool'], 'm_pool_scale': out['m_pool_scale'], 'm_attn_sinks': out['m_attn_sinks'], 'm_w_branch_pool': out['m_w_branch_pool'], 'm_w_branch_attn': out['m_w_branch_attn'], 'm_w_out': out['m_w_out'], 'm_g_mix_post': out['m_g_mix_post'], 'm_g_mlp_pre': out['m_g_mlp_pre'], 'm_w_up': out['m_w_up'], 'm_w_down': out['m_w_down'], 'm_g_mlp_post': out['m_g_mlp_post'], 'v_g_mix_pre': out['v_g_mix_pre'], 'v_w_in': out['v_w_in'], 'v_b_in': out['v_b_in'], 'v_w_pool': out['v_w_pool'], 'v_pool_scale': out['v_pool_scale'], 'v_attn_sinks': out['v_attn_sinks'], 'v_w_branch_pool': out['v_w_branch_pool'], 'v_w_branch_attn': out['v_w_branch_attn'], 'v_w_out': out['v_w_out'], 'v_g_mix_post': out['v_g_mix_post'], 'v_g_mlp_pre': out['v_g_mlp_pre'], 'v_w_up': out['v_w_up'], 'v_w_down': out['v_w_down'], 'v_g_mlp_post': out['v_g_mlp_post']}


def _loss(weights, diff, rest, loss_target):
    with _jax.named_scope("forward"):
        args = {**rest, TWIN_DIFF_INPUT: diff, **{k: w.astype(_WEIGHT_DTYPES[k]) for k, w in weights.items()}}
        y = _forward(args)
    with _jax.named_scope("loss_head"):
        err = _jnp.square(y.astype(_jnp.float32) - loss_target)
        return 0.5 * _jnp.sum(_jnp.mean(err, axis=-1)) if err.ndim else 0.5 * err


def _adamw(w, g, m, v):
    m = ADAM_B1 * m + (1.0 - ADAM_B1) * g
    v = ADAM_B2 * v + (1.0 - ADAM_B2) * _jnp.square(g)
    m_hat = m / (1.0 - ADAM_B1 ** ADAM_STEP)
    v_hat = v / (1.0 - ADAM_B2 ** ADAM_STEP)
    delta = -ADAM_LR * (m_hat / (_jnp.sqrt(v_hat) + ADAM_EPS) + ADAM_WD * w)
    return delta, m, v


def reference(x, g_mix_pre, w_in, b_in, w_pool, pool_scale, attn_sinks, w_branch_pool, w_branch_attn, w_out, g_mix_post, g_mlp_pre, w_up, w_down, g_mlp_post, loss_target, m_g_mix_pre, m_w_in, m_b_in, m_w_pool, m_pool_scale, m_attn_sinks, m_w_branch_pool, m_w_branch_attn, m_w_out, m_g_mix_post, m_g_mlp_pre, m_w_up, m_w_down, m_g_mlp_post, v_g_mix_pre, v_w_in, v_b_in, v_w_pool, v_pool_scale, v_attn_sinks, v_w_branch_pool, v_w_branch_attn, v_w_out, v_g_mix_post, v_g_mlp_pre, v_w_up, v_w_down, v_g_mlp_post):
    given = dict(x=x, g_mix_pre=g_mix_pre, w_in=w_in, b_in=b_in, w_pool=w_pool, pool_scale=pool_scale, attn_sinks=attn_sinks, w_branch_pool=w_branch_pool, w_branch_attn=w_branch_attn, w_out=w_out, g_mix_post=g_mix_post, g_mlp_pre=g_mlp_pre, w_up=w_up, w_down=w_down, g_mlp_post=g_mlp_post, loss_target=loss_target, m_g_mix_pre=m_g_mix_pre, m_w_in=m_w_in, m_b_in=m_b_in, m_w_pool=m_w_pool, m_pool_scale=m_pool_scale, m_attn_sinks=m_attn_sinks, m_w_branch_pool=m_w_branch_pool, m_w_branch_attn=m_w_branch_attn, m_w_out=m_w_out, m_g_mix_post=m_g_mix_post, m_g_mlp_pre=m_g_mlp_pre, m_w_up=m_w_up, m_w_down=m_w_down, m_g_mlp_post=m_g_mlp_post, v_g_mix_pre=v_g_mix_pre, v_w_in=v_w_in, v_b_in=v_b_in, v_w_pool=v_w_pool, v_pool_scale=v_pool_scale, v_attn_sinks=v_attn_sinks, v_w_branch_pool=v_w_branch_pool, v_w_branch_attn=v_w_branch_attn, v_w_out=v_w_out, v_g_mix_post=v_g_mix_post, v_g_mlp_pre=v_g_mlp_pre, v_w_up=v_w_up, v_w_down=v_w_down, v_g_mlp_post=v_g_mlp_post)
    weights = {n: given[n] for n in TWIN_WEIGHTS}
    shared = {n: given[n] for n in SHARED_INPUTS}
    per_example = {n: given[n] for n in ['x']}
    grad_fn = _jax.value_and_grad(_loss, argnums=(0, 1))

    def one_microbatch(ex, loss_target):
        ex = dict(ex)
        diff = ex.pop(TWIN_DIFF_INPUT)
        return grad_fn(weights, diff, {**shared, **ex}, loss_target)

    if N_MICROBATCH == 1:
        loss, (grad_w, grad_x) = one_microbatch(per_example, given["loss_target"])
    else:
        def body(carry, xs):
            loss_sum, grad_sum = carry
            l_k, (gw_k, gx_k) = one_microbatch(xs[0], xs[1])
            with _jax.named_scope("update"):
                return (loss_sum + l_k, _jax.tree.map(_jnp.add, grad_sum, gw_k)), gx_k

        init = (_jnp.zeros((), _jnp.float32), _jax.tree.map(_jnp.zeros_like, weights))
        (loss, grad_w), grad_x = _jax.lax.scan(body, init, (per_example, given["loss_target"]))
    with _jax.named_scope("update"):
        delta_w, new_m, new_v = {}, {}, {}
        for n in TWIN_WEIGHTS:
            delta_w[n], new_m[n], new_v[n] = _adamw(weights[n], grad_w[n], given["m_" + n], given["v_" + n])
    return (loss, grad_x, *[grad_w[n] for n in TWIN_WEIGHTS], *[delta_w[n] for n in TWIN_WEIGHTS],
            *[new_m[n] for n in TWIN_WEIGHTS], *[new_v[n] for n in TWIN_WEIGHTS])
```

```python
import jax
import jax.numpy as jnp
from jax import lax
from jax.experimental import pallas as pl
from jax.experimental.pallas import tpu as pltpu

F32 = jnp.float32
BF16 = jnp.bfloat16

D_MODEL = 1024
POOL_WINDOWS = (2, 4, 8, 16)
POOL_WIDTH = 512
POOL_GC = 128
HALO = 16
HEAD_DIM = 64
N_Q_HEADS = 8
ATTN_WIDTH = 512
KV_WIDTH = 128
BLOCK = 128
NEG_INF = -1e30
ROPE_THETA = 500000.0
ROT_DIM = 16
GATE_WIDTH = 2048
IN_WIDTH = 3328
D_FF = 4096
EPS = 1e-6
SCALE = HEAD_DIM ** -0.5
C_Q, C_K, C_V, C_G = 512, 1024, 1152, 1280

ADAM_LR, ADAM_B1, ADAM_B2, ADAM_EPS, ADAM_WD, ADAM_STEP = 0.001, 0.9, 0.999, 1e-08, 0.01, 10

N_CHIPS = 4
LANES = 128
TM = 512
TP = 256
VMEM_MB = 56

MESH = pl.DeviceIdType.MESH
ANY = pl.BlockSpec(memory_space=pl.ANY)


def _cp(*sem, vmem=VMEM_MB):
    return pltpu.CompilerParams(dimension_semantics=sem, vmem_limit_bytes=vmem * 1024 * 1024)


def _rows(tile, cols):
    return pl.BlockSpec((tile, cols), lambda i: (i, 0))


def _const(shape):
    nd = len(shape)
    return pl.BlockSpec(shape, lambda i: (0,) * nd)


def _sds(shape, dtype):
    return jax.ShapeDtypeStruct(shape, dtype)


def _dot(a, b):
    return jnp.dot(a, b, preferred_element_type=F32)


def _dot_nt(a, b):
    return lax.dot_general(a, b, (((1,), (1,)), ((), ())), preferred_element_type=F32)


def _dot_tn(a, b):
    return lax.dot_general(a, b, (((0,), (0,)), ((), ())), preferred_element_type=F32)


def _rms(x):
    return lax.rsqrt(jnp.mean(x * x, axis=-1, keepdims=True) + EPS)


def _norm_bwd(x, g, dout):
    r = _rms(x)
    n = x * r
    dn = dout * g
    dx = r * (dn - n * jnp.mean(dn * n, axis=-1, keepdims=True))
    return dx, jnp.sum(dout * n, axis=0, keepdims=True)


def _rot_fwd(t, c, a, bt):
    return t * c + pltpu.roll(t, LANES - 8, 1) * a + pltpu.roll(t, 8, 1) * bt


def _rot_bwd(d, c, a, bt):
    return d * c + pltpu.roll(d * a, 8, 1) + pltpu.roll(d * bt, LANES - 8, 1)


def _rope_tables(seq):
    pos = jnp.arange(seq, dtype=F32)
    inv_freq = ROPE_THETA ** (-jnp.arange(0, ROT_DIM, 2, dtype=F32) / ROT_DIM)
    ang = pos[:, None] * inv_freq[None, :]
    cos, sin = jnp.cos(ang), jnp.sin(ang)
    ones = jnp.ones((seq, HEAD_DIM - ROT_DIM), F32)
    zeros8 = jnp.zeros((seq, 8), F32)
    zrest = jnp.zeros((seq, HEAD_DIM - ROT_DIM), F32)
    c = jnp.concatenate([cos, cos, ones], axis=1)
    a = jnp.concatenate([-sin, zeros8, zrest], axis=1)
    bt = jnp.concatenate([zeros8, sin, zrest], axis=1)
    return tuple(jnp.tile(t, (1, 2)) for t in (c, a, bt))


def _inproj(x2, g1, w_in, b_in, tabs, seq):
    T = x2.shape[0]
    tm = min(TM, seq)
    nseq = seq // tm

    def body(x_ref, g_ref, w_ref, b_ref, c_ref, a_ref, bt_ref, h_ref, u_ref, q_ref, k_ref, v_ref, gate_ref):
        x = x_ref[...]
        h = (x * _rms(x) * g_ref[...]).astype(BF16)
        h_ref[...] = h

        def proj(lo, hi):
            return _dot(h, w_ref[:, lo:hi]) + b_ref[:, lo:hi]

        c, a, bt = c_ref[...], a_ref[...], bt_ref[...]
        u_ref[...] = proj(0, C_Q)
        q = proj(C_Q, C_K)
        for p in range(4):
            sl = slice(LANES * p, LANES * (p + 1))
            q_ref[:, sl] = _rot_fwd(q[:, sl], c, a, bt).astype(BF16)
        kv = proj(C_K, C_G)
        k_ref[...] = _rot_fwd(kv[:, :KV_WIDTH], c, a, bt).astype(BF16)
        v_ref[...] = kv[:, KV_WIDTH:].astype(BF16)
        for j in range(2):
            lo = C_G + D_MODEL * j
            gate_ref[:, D_MODEL * j:D_MODEL * (j + 1)] = jax.nn.sigmoid(proj(lo, lo + D_MODEL)).astype(BF16)

    tab = pl.BlockSpec((tm, LANES), lambda i: (i % nseq, 0))
    return pl.pallas_call(
        body, name="inproj", grid=(T // tm,),
        in_specs=[_rows(tm, D_MODEL), _const((1, D_MODEL)), _const((D_MODEL, IN_WIDTH)), _const((1, IN_WIDTH)),
                  tab, tab, tab],
        out_specs=[_rows(tm, D_MODEL), _rows(tm, POOL_WIDTH), _rows(tm, ATTN_WIDTH), _rows(tm, KV_WIDTH),
                   _rows(tm, KV_WIDTH), _rows(tm, GATE_WIDTH)],
        out_shape=[_sds((T, D_MODEL), BF16), _sds((T, POOL_WIDTH), F32), _sds((T, ATTN_WIDTH), BF16),
                   _sds((T, KV_WIDTH), BF16), _sds((T, KV_WIDTH), BF16), _sds((T, GATE_WIDTH), BF16)],
        compiler_params=_cp("parallel"),
    )(x2, g1, w_in, b_in, *tabs)


def _inv_count(pos, w):
    return 1.0 / jnp.minimum(pos + 1, w).astype(F32)


def _pool_fwd(u, w_pool, pool_scale, seq):
    T = u.shape[0]
    tp = min(TP, seq)
    nseq = seq // tp
    per = tp // HALO

    def body(u_ref, prev_ref, w_ref, s_ref, diff_ref, y_ref):
        i = pl.program_id(0)
        first = (i % nseq) == 0
        prev = jnp.where(first, 0.0, prev_ref[...])
        ext = jnp.concatenate([prev, u_ref[...]], axis=0)
        pos = (i % nseq) * tp + lax.broadcasted_iota(jnp.int32, (tp, 1), 0)
        for gi, w in enumerate(POOL_WINDOWS):
            sl = slice(POOL_GC * gi, POOL_GC * (gi + 1))
            xg = ext[:, sl]
            s = xg
            sh = 1
            while sh < w:
                s = s + pltpu.roll(s, sh, 0)
                sh *= 2
            pooled = s[HALO:] * _inv_count(pos, w)
            diff = (pooled - xg[HALO:]).astype(BF16)
            diff_ref[:, sl] = diff
            mixed = _dot(diff, w_ref[gi].astype(BF16))
            y_ref[:, sl] = (mixed * s_ref[:, sl]).astype(BF16)

    return pl.pallas_call(
        body, name="pool_fwd", grid=(T // tp,),
        in_specs=[_rows(tp, POOL_WIDTH),
                  pl.BlockSpec((HALO, POOL_WIDTH), lambda i: (jnp.maximum(i * per - 1, 0), 0)),
                  _const((4, POOL_GC, POOL_GC)), _const((1, POOL_WIDTH))],
        out_specs=[_rows(tp, POOL_WIDTH), _rows(tp, POOL_WIDTH)],
        out_shape=[_sds((T, POOL_WIDTH), BF16), _sds((T, POOL_WIDTH), BF16)],
        compiler_params=_cp("parallel"),
    )(u, u, w_pool, pool_scale)


def _attn_masks(n):
    qi = lax.broadcasted_iota(jnp.int32, (BLOCK, 2 * BLOCK), 0)
    kj = lax.broadcasted_iota(jnp.int32, (BLOCK, 2 * BLOCK), 1)
    rel = qi + BLOCK - kj
    valid = (rel >= 0) & (rel < BLOCK) & (kj >= jnp.where(n > 0, 0, BLOCK))
    lo = lax.broadcasted_iota(jnp.int32, (BLOCK, LANES), 1) < HEAD_DIM
    return valid, lo


def _head_probs(qm, kk, valid, sink):
    s = _dot_nt(qm, kk) * SCALE
    s = jnp.where(valid, s, NEG_INF)
    m = jnp.maximum(jnp.max(s, axis=1, keepdims=True), sink)
    ex = jnp.exp(s - m)
    es = jnp.exp(sink - m)
    inv = 1.0 / (jnp.sum(ex, axis=1, keepdims=True) + es)
    return ex * inv, es * inv


def _attn_fwd(q, k, v, sinks, seq):
    T = q.shape[0]
    nb = seq // BLOCK

    def body(sink_ref, q_ref, kp_ref, kc_ref, vp_ref, vc_ref, o_ref):
        n = pl.program_id(0) % nb
        kk = jnp.concatenate([kp_ref[...], kc_ref[...]], axis=0)
        vv = jnp.concatenate([vp_ref[...], vc_ref[...]], axis=0)
        valid, lo = _attn_masks(n)
        for p in range(4):
            h = p // 2
            keep = lo if h == 0 else jnp.logical_not(lo)
            qp = q_ref[:, LANES * p:LANES * (p + 1)].astype(F32)
            outs = []
            for e in range(2):
                t = qp if e == h else pltpu.roll(qp, HEAD_DIM, 1)
                qm = jnp.where(keep, t, 0.0).astype(BF16)
                pr, _ = _head_probs(qm, kk, valid, sink_ref[2 * p + e])
                o = _dot(pr.astype(BF16), vv)
                outs.append(o if e == h else pltpu.roll(o, HEAD_DIM, 1))
            o_ref[:, LANES * p:LANES * (p + 1)] = jnp.where(lo, outs[0], outs[1]).astype(BF16)

    cur = lambda i: (i, 0)
    prv = lambda i: (jnp.where(i % nb == 0, i, i - 1), 0)
    return pl.pallas_call(
        body, name="attn_fwd", grid=(T // BLOCK,),
        in_specs=[pl.BlockSpec(memory_space=pltpu.SMEM),
                  pl.BlockSpec((BLOCK, ATTN_WIDTH), cur),
                  pl.BlockSpec((BLOCK, KV_WIDTH), prv), pl.BlockSpec((BLOCK, KV_WIDTH), cur),
                  pl.BlockSpec((BLOCK, KV_WIDTH), prv), pl.BlockSpec((BLOCK, KV_WIDTH), cur)],
        out_specs=pl.BlockSpec((BLOCK, ATTN_WIDTH), cur),
        out_shape=_sds((T, ATTN_WIDTH), BF16),
        compiler_params=_cp("parallel"),
    )(sinks, q, k, k, v, v)


def _merge_out(y_pool, y_attn, gate, x2, w_bp, w_ba, w_out, g2, g3):
    T = x2.shape[0]
    tm = min(TM, T)

    def body(yp_ref, ya_ref, gate_ref, x_ref, wbp_ref, wba_ref, wo_ref, g2_ref, g3_ref,
             bp_ref, ba_ref, mg_ref, mix_ref, x1_ref, h2_ref):
        bp = _dot(yp_ref[...], wbp_ref[...])
        ba = _dot(ya_ref[...], wba_ref[...])
        bp_ref[...] = bp.astype(BF16)
        ba_ref[...] = ba.astype(BF16)
        merged = (gate_ref[:, :D_MODEL].astype(F32) * bp + gate_ref[:, D_MODEL:].astype(F32) * ba).astype(BF16)
        mg_ref[...] = merged
        mix = _dot(merged, wo_ref[...])
        mix_ref[...] = mix
        x1 = x_ref[...] + mix * _rms(mix) * g2_ref[...]
        x1_ref[...] = x1
        h2_ref[...] = (x1 * _rms(x1) * g3_ref[...]).astype(BF16)

    return pl.pallas_call(
        body, name="merge_out", grid=(T // tm,),
        in_specs=[_rows(tm, POOL_WIDTH), _rows(tm, ATTN_WIDTH), _rows(tm, GATE_WIDTH), _rows(tm, D_MODEL),
                  _const((POOL_WIDTH, D_MODEL)), _const((ATTN_WIDTH, D_MODEL)), _const((D_MODEL, D_MODEL)),
                  _const((1, D_MODEL)), _const((1, D_MODEL))],
        out_specs=[_rows(tm, D_MODEL)] * 6,
        out_shape=[_sds((T, D_MODEL), BF16), _sds((T, D_MODEL), BF16), _sds((T, D_MODEL), BF16),
                   _sds((T, D_MODEL), F32), _sds((T, D_MODEL), F32), _sds((T, D_MODEL), BF16)],
        compiler_params=_cp("parallel"),
    )(y_pool, y_attn, gate, x2, w_bp, w_ba, w_out, g2, g3)


def _mlp_up(h2, w_up):
    T = h2.shape[0]
    tm = min(TM, T)

    def body(h_ref, w_ref, up_ref, a_ref):
        h = h_ref[...]
        for j in range(D_FF // D_MODEL):
            sl = slice(D_MODEL * j, D_MODEL * (j + 1))
            up = _dot(h, w_ref[:, sl])
            up_ref[:, sl] = up.astype(BF16)
            a_ref[:, sl] = jnp.square(jnp.maximum(up, 0.0)).astype(BF16)

    return pl.pallas_call(
        body, name="mlp_up", grid=(T // tm,),
        in_specs=[_rows(tm, D_MODEL), _const((D_MODEL, D_FF))],
        out_specs=[_rows(tm, D_FF), _rows(tm, D_FF)],
        out_shape=[_sds((T, D_FF), BF16), _sds((T, D_FF), BF16)],
        compiler_params=_cp("parallel"),
    )(h2, w_up)


def _mlp_down_loss(a, x1, tgt, w_down, g4):
    T = a.shape[0]
    tm = min(TM, T)

    def body(a_ref, x1_ref, t_ref, w_ref, g_ref, dff_ref, dy_ref, loss_ref, dg_ref):
        @pl.when(pl.program_id(0) == 0)
        def _():
            loss_ref[...] = jnp.zeros_like(loss_ref)
            dg_ref[...] = jnp.zeros_like(dg_ref)

        ff = _dot(a_ref[...], w_ref[...])
        g = g_ref[...]
        err = x1_ref[...] + ff * _rms(ff) * g - t_ref[...]
        loss_ref[...] += jnp.sum(err * err) * (0.5 / D_MODEL)
        dy = err * (1.0 / D_MODEL)
        dy_ref[...] = dy
        dff, dg = _norm_bwd(ff, g, dy)
        dff_ref[...] = dff.astype(BF16)
        dg_ref[...] += dg

    return pl.pallas_call(
        body, name="mlp_down_loss", grid=(T // tm,),
        in_specs=[_rows(tm, D_FF), _rows(tm, D_MODEL), _rows(tm, D_MODEL), _const((D_FF, D_MODEL)),
                  _const((1, D_MODEL))],
        out_specs=[_rows(tm, D_MODEL), _rows(tm, D_MODEL), _const((8, LANES)), _const((1, D_MODEL))],
        out_shape=[_sds((T, D_MODEL), BF16), _sds((T, D_MODEL), F32), _sds((8, LANES), F32),
                   _sds((1, D_MODEL), F32)],
        compiler_params=_cp("arbitrary"),
    )(a, x1, tgt, w_down, g4)


def _mlp_down_bwd(dff, up, w_down):
    T = dff.shape[0]
    tm = min(TM, T)

    def body(d_ref, up_ref, w_ref, dup_ref):
        d = d_ref[...]
        for j in range(D_FF // D_MODEL):
            sl = slice(D_MODEL * j, D_MODEL * (j + 1))
            da = _dot_nt(d, w_ref[sl, :])
            dup_ref[:, sl] = (da * (2.0 * jnp.maximum(up_ref[:, sl].astype(F32), 0.0))).astype(BF16)

    return pl.pallas_call(
        body, name="mlp_down_bwd", grid=(T // tm,),
        in_specs=[_rows(tm, D_MODEL), _rows(tm, D_FF), _const((D_FF, D_MODEL))],
        out_specs=_rows(tm, D_FF),
        out_shape=_sds((T, D_FF), BF16),
        compiler_params=_cp("parallel"),
    )(dff, up, w_down)


def _mlp_up_bwd(dup, dy, x1, mix, w_up, g3, g2):
    T = dup.shape[0]
    tm = min(TM, T)

    def body(dup_ref, dy_ref, x1_ref, mix_ref, w_ref, g3_ref, g2_ref, dx1_ref, dmix_ref, dg3_ref, dg2_ref):
        @pl.when(pl.program_id(0) == 0)
        def _():
            dg3_ref[...] = jnp.zeros_like(dg3_ref)
            dg2_ref[...] = jnp.zeros_like(dg2_ref)

        dh2 = _dot_nt(dup_ref[...], w_ref[...])
        dx, dg3 = _norm_bwd(x1_ref[...], g3_ref[...], dh2)
        dx1 = dy_ref[...] + dx
        dx1_ref[...] = dx1
        dg3_ref[...] += dg3
        dmix, dg2 = _norm_bwd(mix_ref[...], g2_ref[...], dx1)
        dmix_ref[...] = dmix.astype(BF16)
        dg2_ref[...] += dg2

    return pl.pallas_call(
        body, name="mlp_up_bwd", grid=(T // tm,),
        in_specs=[_rows(tm, D_FF), _rows(tm, D_MODEL), _rows(tm, D_MODEL), _rows(tm, D_MODEL),
                  _const((D_MODEL, D_FF)), _const((1, D_MODEL)), _const((1, D_MODEL))],
        out_specs=[_rows(tm, D_MODEL), _rows(tm, D_MODEL), _const((1, D_MODEL)), _const((1, D_MODEL))],
        out_shape=[_sds((T, D_MODEL), F32), _sds((T, D_MODEL), BF16), _sds((1, D_MODEL), F32),
                   _sds((1, D_MODEL), F32)],
        compiler_params=_cp("arbitrary"),
    )(dup, dy, x1, mix, w_up, g3, g2)


def _dw(tag, a, g, ta, tn, shard_cols=False):
    T, ka = a.shape
    n = g.shape[1]
    tk = min(TM, T)
    nk = T // tk

    def body(a_ref, g_ref, o_ref):
        @pl.when(pl.program_id(2) == 0)
        def _():
            o_ref[...] = jnp.zeros_like(o_ref)

        o_ref[...] += _dot_tn(a_ref[...], g_ref[...])

    if shard_cols:
        per = (n // N_CHIPS) // tn
        out_spec = pl.BlockSpec((None, ta, tn), lambda i, j, k: (j // per, i, j % per))
        out_shape = _sds((N_CHIPS, ka, n // N_CHIPS), F32)
    else:
        out_spec = pl.BlockSpec((ta, tn), lambda i, j, k: (i, j))
        out_shape = _sds((ka, n), F32)
    return pl.pallas_call(
        body, name="dw_" + tag, grid=(ka // ta, n // tn, nk),
        in_specs=[pl.BlockSpec((tk, ta), lambda i, j, k: (k, i)), pl.BlockSpec((tk, tn), lambda i, j, k: (k, j))],
        out_specs=out_spec, out_shape=out_shape,
        compiler_params=_cp("parallel", "parallel", "arbitrary"),
    )(a, g)


def _merge_bwd(dmix, gate, bp, ba, w_out, w_bp, w_ba):
    T = dmix.shape[0]
    tm = min(TM, T)

    def body(dmix_ref, gate_ref, bp_ref, ba_ref, wo_ref, wbp_ref, wba_ref,
             dbp_ref, dba_ref, dgate_ref, dyp_ref, dya_ref):
        dm = _dot_nt(dmix_ref[...], wo_ref[...])
        for j, (b_ref, db_ref, w_ref, dy_ref) in enumerate(
                ((bp_ref, dbp_ref, wbp_ref, dyp_ref), (ba_ref, dba_ref, wba_ref, dya_ref))):
            sl = slice(D_MODEL * j, D_MODEL * (j + 1))
            gt = gate_ref[:, sl].astype(F32)
            db = (dm * gt).astype(BF16)
            db_ref[...] = db
            dgate_ref[:, sl] = (dm * b_ref[...].astype(F32) * gt * (1.0 - gt)).astype(BF16)
            dy_ref[...] = _dot_nt(db, w_ref[...]).astype(dy_ref.dtype)

    return pl.pallas_call(
        body, name="merge_bwd", grid=(T // tm,),
        in_specs=[_rows(tm, D_MODEL), _rows(tm, GATE_WIDTH), _rows(tm, D_MODEL), _rows(tm, D_MODEL),
                  _const((D_MODEL, D_MODEL)), _const((POOL_WIDTH, D_MODEL)), _const((ATTN_WIDTH, D_MODEL))],
        out_specs=[_rows(tm, D_MODEL), _rows(tm, D_MODEL), _rows(tm, GATE_WIDTH), _rows(tm, POOL_WIDTH),
                   _rows(tm, ATTN_WIDTH)],
        out_shape=[_sds((T, D_MODEL), BF16), _sds((T, D_MODEL), BF16), _sds((T, GATE_WIDTH), BF16),
                   _sds((T, POOL_WIDTH), F32), _sds((T, ATTN_WIDTH), BF16)],
        compiler_params=_cp("parallel"),
    )(dmix, gate, bp, ba, w_out, w_bp, w_ba)


def _attn_bwd(q, k, v, do, sinks, tabs, seq):
    T = q.shape[0]
    nb = seq // BLOCK
    steps = nb + 1

    def body(sink_ref, q_ref, do_ref, kp_ref, kc_ref, vp_ref, vc_ref, c_ref, a_ref, bt_ref, cp_ref, ap_ref, btp_ref,
             dq_ref, dk_ref, dv_ref, dsink_ref, ck_ref, cv_ref):
        i = pl.program_id(0)
        n = i % steps

        @pl.when(i == 0)
        def _():
            dsink_ref[...] = jnp.zeros_like(dsink_ref)

        @pl.when(n == 0)
        def _():
            ck_ref[...] = jnp.zeros_like(ck_ref)
            cv_ref[...] = jnp.zeros_like(cv_ref)

        @pl.when(n < nb)
        def _():
            kk = jnp.concatenate([kp_ref[...], kc_ref[...]], axis=0)
            vv = jnp.concatenate([vp_ref[...], vc_ref[...]], axis=0)
            valid, lo = _attn_masks(n)
            dk_acc = jnp.zeros((2 * BLOCK, KV_WIDTH), F32)
            dv_acc = jnp.zeros((2 * BLOCK, KV_WIDTH), F32)
            for p in range(4):
                h = p // 2
                keep = lo if h == 0 else jnp.logical_not(lo)
                sl = slice(LANES * p, LANES * (p + 1))
                qp = q_ref[:, sl].astype(F32)
                dop = do_ref[:, sl].astype(F32)
                parts = []
                for e in range(2):
                    idx = 2 * p + e
                    same = e == h
                    qm = jnp.where(keep, qp if same else pltpu.roll(qp, HEAD_DIM, 1), 0.0).astype(BF16)
                    dom = jnp.where(keep, dop if same else pltpu.roll(dop, HEAD_DIM, 1), 0.0).astype(BF16)
                    sink = sink_ref[idx]
                    pr, ps = _head_probs(qm, kk, valid, sink)
                    dp = _dot_nt(dom, vv)
                    delta = jnp.sum(pr * dp, axis=1, keepdims=True)
                    ds = (pr * (dp - delta) * SCALE).astype(BF16)
                    dsink_ref[idx:idx + 1, :] += jnp.zeros((1, LANES), F32) - jnp.sum(ps * delta)
                    dq = jnp.where(keep, _dot(ds, kk), 0.0)
                    parts.append(dq if same else pltpu.roll(dq, HEAD_DIM, 1))
                    dk_acc = dk_acc + _dot_tn(ds, qm)
                    dv_acc = dv_acc + _dot_tn(pr.astype(BF16), dom)
                dq_pair = jnp.where(lo, parts[0], parts[1])
                dq_ref[:, sl] = _rot_bwd(dq_pair, c_ref[...], a_ref[...], bt_ref[...]).astype(BF16)
            fin_k = ck_ref[...] + dk_acc[:BLOCK]
            dk_ref[...] = _rot_bwd(fin_k, cp_ref[...], ap_ref[...], btp_ref[...]).astype(BF16)
            dv_ref[...] = (cv_ref[...] + dv_acc[:BLOCK]).astype(BF16)
            ck_ref[...] = dk_acc[BLOCK:]
            cv_ref[...] = dv_acc[BLOCK:]

        @pl.when(n == nb)
        def _():
            dk_ref[...] = _rot_bwd(ck_ref[...], cp_ref[...], ap_ref[...], btp_ref[...]).astype(BF16)
            dv_ref[...] = cv_ref[...].astype(BF16)

    def blk(i):
        return (i // steps) * nb

    cur = lambda i: (blk(i) + jnp.minimum(i % steps, nb - 1), 0)
    prv = lambda i: (blk(i) + jnp.clip(i % steps - 1, 0, nb - 1), 0)
    tcur = lambda i: (jnp.minimum(i % steps, nb - 1), 0)
    tprv = lambda i: (jnp.clip(i % steps - 1, 0, nb - 1), 0)
    kv = lambda m: pl.BlockSpec((BLOCK, KV_WIDTH), m)
    return pl.pallas_call(
        body, name="attn_bwd", grid=((T // seq) * steps,),
        in_specs=[pl.BlockSpec(memory_space=pltpu.SMEM),
                  pl.BlockSpec((BLOCK, ATTN_WIDTH), cur), pl.BlockSpec((BLOCK, ATTN_WIDTH), cur),
                  kv(prv), kv(cur), kv(prv), kv(cur),
                  kv(tcur), kv(tcur), kv(tcur), kv(tprv), kv(tprv), kv(tprv)],
        out_specs=[pl.BlockSpec((BLOCK, ATTN_WIDTH), cur), kv(prv), kv(prv), _const((8, LANES))],
        out_shape=[_sds((T, ATTN_WIDTH), BF16), _sds((T, KV_WIDTH), BF16), _sds((T, KV_WIDTH), BF16),
                   _sds((8, LANES), F32)],
        scratch_shapes=[pltpu.VMEM((BLOCK, KV_WIDTH), F32), pltpu.VMEM((BLOCK, KV_WIDTH), F32)],
        compiler_params=_cp("arbitrary"),
    )(sinks, q, do, k, k, v, v, *tabs, *tabs)


def _pool_bwd(dyp, diff, w_pool, pool_scale, seq):
    T = dyp.shape[0]
    tp = min(TP, seq)
    nseq = seq // tp
    per = tp // HALO
    last_halo = T // HALO - 1

    def body(dy_ref, nxt_ref, diff_ref, w_ref, s_ref, du_ref, dw_ref, ds_ref):
        i = pl.program_id(0)

        @pl.when(i == 0)
        def _():
            dw_ref[...] = jnp.zeros_like(dw_ref)
            ds_ref[...] = jnp.zeros_like(ds_ref)

        last = (i % nseq) == nseq - 1
        nxt = jnp.where(last, 0.0, nxt_ref[...])
        ext = jnp.concatenate([dy_ref[...], nxt], axis=0) * s_ref[...]
        pos = (i % nseq) * tp + lax.broadcasted_iota(jnp.int32, (tp + HALO, 1), 0)
        for gi, w in enumerate(POOL_WINDOWS):
            sl = slice(POOL_GC * gi, POOL_GC * (gi + 1))
            wg = w_ref[gi].astype(BF16)
            dmx = ext[:, sl].astype(BF16)
            ddiff = _dot_nt(dmx, wg)
            s = ddiff * _inv_count(pos, w)
            sh = 1
            while sh < w:
                s = s + pltpu.roll(s, tp + HALO - sh, 0)
                sh *= 2
            du_ref[:, sl] = (s[:tp] - ddiff[:tp]).astype(BF16)
            dg = diff_ref[:, sl]
            dw_ref[gi] += _dot_tn(dg, dmx[:tp])
            ds_ref[:, sl] += jnp.sum(dy_ref[:, sl] * _dot(dg, wg), axis=0, keepdims=True)

    return pl.pallas_call(
        body, name="pool_bwd", grid=(T // tp,),
        in_specs=[_rows(tp, POOL_WIDTH),
                  pl.BlockSpec((HALO, POOL_WIDTH), lambda i: (jnp.minimum((i + 1) * per, last_halo), 0)),
                  _rows(tp, POOL_WIDTH), _const((4, POOL_GC, POOL_GC)), _const((1, POOL_WIDTH))],
        out_specs=[_rows(tp, POOL_WIDTH), _const((4, POOL_GC, POOL_GC)), _const((1, POOL_WIDTH))],
        out_shape=[_sds((T, POOL_WIDTH), BF16), _sds((4, POOL_GC, POOL_GC), F32), _sds((1, POOL_WIDTH), F32)],
        compiler_params=_cp("arbitrary"),
    )(dyp, dyp, diff, w_pool, pool_scale)


_PARTS = ((0, C_Q), (C_Q, C_K), (C_K, C_V), (C_V, C_G), (C_G, IN_WIDTH))


def _inproj_bwd(parts, x2, dx1, w_in, g1):
    T = x2.shape[0]
    tm = min(TM, T)

    def body(du_ref, dq_ref, dk_ref, dv_ref, dgt_ref, x_ref, dx1_ref, w_ref, g_ref, gx_ref, dg_ref, db_ref):
        @pl.when(pl.program_id(0) == 0)
        def _():
            dg_ref[...] = jnp.zeros_like(dg_ref)
            db_ref[...] = jnp.zeros_like(db_ref)

        dh = jnp.zeros((tm, D_MODEL), F32)
        for (lo, hi), p_ref in zip(_PARTS, (du_ref, dq_ref, dk_ref, dv_ref, dgt_ref)):
            part = p_ref[...]
            dh = dh + _dot_nt(part, w_ref[:, lo:hi])
            db_ref[:, lo:hi] += jnp.sum(part.astype(F32), axis=0, keepdims=True)
        dx, dg = _norm_bwd(x_ref[...], g_ref[...], dh)
        gx_ref[...] = dx1_ref[...] + dx
        dg_ref[...] += dg

    return pl.pallas_call(
        body, name="inproj_bwd", grid=(T // tm,),
        in_specs=[_rows(tm, hi - lo) for lo, hi in _PARTS]
        + [_rows(tm, D_MODEL), _rows(tm, D_MODEL), _const((D_MODEL, IN_WIDTH)), _const((1, D_MODEL))],
        out_specs=[_rows(tm, D_MODEL), _const((1, D_MODEL)), _const((1, IN_WIDTH))],
        out_shape=[_sds((T, D_MODEL), F32), _sds((1, D_MODEL), F32), _sds((1, IN_WIDTH), F32)],
        compiler_params=_cp("arbitrary"),
    )(*parts, x2, dx1, w_in, g1)


def _dw_in(h, parts):
    T = h.shape[0]
    tk = min(TM, T)

    def body(h_ref, du_ref, dq_ref, dk_ref, dv_ref, dgt_ref, o_ref):
        @pl.when(pl.program_id(0) == 0)
        def _():
            o_ref[...] = jnp.zeros_like(o_ref)

        hh = h_ref[...]
        for (lo, hi), p_ref in zip(_PARTS, (du_ref, dq_ref, dk_ref, dv_ref, dgt_ref)):
            o_ref[:, lo:hi] += _dot_tn(hh, p_ref[...])

    return pl.pallas_call(
        body, name="dw_in", grid=(T // tk,),
        in_specs=[_rows(tk, D_MODEL)] + [_rows(tk, hi - lo) for lo, hi in _PARTS],
        out_specs=_const((D_MODEL, IN_WIDTH)),
        out_shape=_sds((D_MODEL, IN_WIDTH), F32),
        compiler_params=_cp("arbitrary"),
    )(h, *parts)


def _adamw(w, g, m, v):
    r, c = w.shape
    tr = r
    for cand in (256, 128, 64, 32, 16, 8):
        if r % cand == 0:
            tr = cand
            break

    def body(w_ref, g_ref, m_ref, v_ref, d_ref, nm_ref, nv_ref):
        gg = g_ref[...]
        nm = ADAM_B1 * m_ref[...] + (1.0 - ADAM_B1) * gg
        nv = ADAM_B2 * v_ref[...] + (1.0 - ADAM_B2) * jnp.square(gg)
        m_hat = nm / (1.0 - ADAM_B1 ** ADAM_STEP)
        v_hat = nv / (1.0 - ADAM_B2 ** ADAM_STEP)
        d_ref[...] = -ADAM_LR * (m_hat / (jnp.sqrt(v_hat) + ADAM_EPS) + ADAM_WD * w_ref[...])
        nm_ref[...] = nm
        nv_ref[...] = nv

    spec = _rows(tr, c)
    return pl.pallas_call(
        body, name="adamw_%dx%d" % (r, c), grid=(r // tr,),
        in_specs=[spec] * 4, out_specs=[spec] * 3, out_shape=[_sds((r, c), F32)] * 3,
        compiler_params=_cp("parallel"),
    )(w, g, m, v)


def _place():
    x, y, c = lax.axis_index("x"), lax.axis_index("y"), lax.axis_index("c")
    chips = [(1 - x, y), (x, 1 - y), (1 - x, 1 - y)]
    return x, y, c, chips


def _gather_weights(shards):
    nw = len(shards)

    def body(*refs):
        ins, outs = refs[:nw], refs[nw:2 * nw]
        send, recv, fsend, frecv, lsem = refs[2 * nw:]
        x, y, c, chips = _place()
        me = 2 * x + y
        sibling = (x, y, 1 - c)

        def half(ref, chip_idx, core, w):
            hr = shards[w].shape[0] // 2
            return ref.at[chip_idx, pl.ds(core * hr, hr)]

        def src_half(w):
            hr = shards[w].shape[0] // 2
            return ins[w].at[pl.ds(c * hr, hr)]

        locals_, firsts, passed = [], [], []
        for w in range(nw):
            cp = pltpu.make_async_copy(ins[w], outs[w].at[me], lsem.at[w])
            cp.start()
            locals_.append(cp)
            for k, chip in enumerate(chips):
                cp = pltpu.make_async_remote_copy(
                    src_ref=src_half(w), dst_ref=half(outs[w], me, c, w), send_sem=send.at[w, k],
                    recv_sem=recv.at[w, k], device_id=(*chip, c), device_id_type=MESH)
                cp.start()
                firsts.append(cp)
        for w in range(nw):
            for k, (cx, cy) in enumerate(chips):
                piece = half(outs[w], 2 * cx + cy, c, w)
                pltpu.make_async_remote_copy(
                    src_ref=piece, dst_ref=piece, send_sem=send.at[w, k], recv_sem=recv.at[w, k],
                    device_id=(cx, cy, c), device_id_type=MESH).wait_recv()
                cp = pltpu.make_async_remote_copy(
                    src_ref=piece, dst_ref=piece, send_sem=fsend.at[w, k], recv_sem=frecv.at[w, k],
                    device_id=sibling, device_id_type=MESH)
                cp.start()
                passed.append(cp)
        for w in range(nw):
            for k, (cx, cy) in enumerate(chips):
                piece = half(outs[w], 2 * cx + cy, 1 - c, w)
                pltpu.make_async_remote_copy(
                    src_ref=piece, dst_ref=piece, send_sem=fsend.at[w, k], recv_sem=frecv.at[w, k],
                    device_id=sibling, device_id_type=MESH).wait_recv()
        for cp in firsts + passed:
            cp.wait_send()
        for cp in locals_:
            cp.wait()

    return pl.pallas_call(
        body, name="gather_weights",
        in_specs=[ANY] * nw, out_specs=[ANY] * nw,
        out_shape=[_sds((N_CHIPS,) + s.shape, s.dtype) for s in shards],
        scratch_shapes=[pltpu.SemaphoreType.DMA((nw, 3))] * 4 + [pltpu.SemaphoreType.DMA((nw,))],
    )(*shards)


def _pair_exchange(grads):
    nw = len(grads)

    def body(*refs):
        ins, outs = refs[:nw], refs[nw:2 * nw]
        send, recv = refs[2 * nw:]
        x, y, c, _ = _place()
        copies = []
        for w in range(nw):
            hr = grads[w].shape[1] // 2
            cp = pltpu.make_async_remote_copy(
                src_ref=ins[w].at[:, pl.ds((1 - c) * hr, hr)], dst_ref=outs[w], send_sem=send.at[w],
                recv_sem=recv.at[w], device_id=(x, y, 1 - c), device_id_type=MESH)
            cp.start()
            copies.append(cp)
        for cp in copies:
            cp.wait()

    return pl.pallas_call(
        body, name="pair_exchange",
        in_specs=[ANY] * nw, out_specs=[ANY] * nw,
        out_shape=[_sds((N_CHIPS, g.shape[1] // 2, g.shape[2]), F32) for g in grads],
        scratch_shapes=[pltpu.SemaphoreType.DMA((nw,))] * 2,
    )(*grads)


def _pair_sum(core, full, got):
    _, r, c = full.shape
    hr = r // 2
    tr = min(256, hr)
    nblk = hr // tr

    def body(core_ref, a_ref, b_ref, s_ref, sb_ref):
        s = a_ref[...] + b_ref[...]
        s_ref[...] = s
        sb_ref[...] = s.astype(BF16)

    out_spec = pl.BlockSpec((None, tr, c), lambda j, i, core_ref: (j, i, 0))
    return pl.pallas_call(
        body, name="pair_sum_%dx%d" % (r, c),
        grid_spec=pltpu.PrefetchScalarGridSpec(
            num_scalar_prefetch=1, grid=(N_CHIPS, nblk),
            in_specs=[pl.BlockSpec((None, tr, c), lambda j, i, core_ref: (j, core_ref[0] * nblk + i, 0)), out_spec],
            out_specs=[out_spec, out_spec]),
        out_shape=[_sds((N_CHIPS, hr, c), F32), _sds((N_CHIPS, hr, c), BF16)],
        compiler_params=_cp("parallel", "parallel"),
    )(core, full, got)


def _chip_exchange(pieces):
    nw = len(pieces)

    def body(*refs):
        ins, outs = refs[:nw], refs[nw:2 * nw]
        send, recv = refs[2 * nw:]
        x, y, c, chips = _place()
        copies = []
        for w in range(nw):
            for k, (cx, cy) in enumerate(chips):
                cp = pltpu.make_async_remote_copy(
                    src_ref=ins[w].at[2 * cx + cy], dst_ref=outs[w].at[k], send_sem=send.at[w, k],
                    recv_sem=recv.at[w, k], device_id=(cx, cy, c), device_id_type=MESH)
                cp.start()
                copies.append(cp)
        for cp in copies:
            cp.wait()

    return pl.pallas_call(
        body, name="chip_exchange",
        in_specs=[ANY] * nw, out_specs=[ANY] * nw,
        out_shape=[_sds((3,) + p.shape[1:], BF16) for p in pieces],
        scratch_shapes=[pltpu.SemaphoreType.DMA((nw, 3))] * 2,
    )(*pieces)


def _chip_sum(chip, own, got):
    _, hr, c = own.shape
    tr = min(256, hr)

    def body(chip_ref, a_ref, b_ref, o_ref):
        o_ref[...] = ((a_ref[...] + b_ref[0].astype(F32)) + b_ref[1].astype(F32)) + b_ref[2].astype(F32)

    return pl.pallas_call(
        body, name="chip_sum_%dx%d" % (hr, c),
        grid_spec=pltpu.PrefetchScalarGridSpec(
            num_scalar_prefetch=1, grid=(hr // tr,),
            in_specs=[pl.BlockSpec((None, tr, c), lambda i, chip_ref: (chip_ref[0], i, 0)),
                      pl.BlockSpec((3, tr, c), lambda i, chip_ref: (0, i, 0))],
            out_specs=pl.BlockSpec((tr, c), lambda i, chip_ref: (i, 0))),
        out_shape=_sds((hr, c), F32),
        compiler_params=_cp("parallel"),
    )(chip, own, got)


def _pair_swap(halves):
    nw = len(halves)

    def body(*refs):
        ins, outs = refs[:nw], refs[nw:2 * nw]
        send, recv, lsem = refs[2 * nw:]
        x, y, c, _ = _place()
        copies, locals_ = [], []
        for w in range(nw):
            hr = halves[w].shape[0]
            mine = outs[w].at[pl.ds(c * hr, hr)]
            cp = pltpu.make_async_copy(ins[w], mine, lsem.at[w])
            cp.start()
            locals_.append(cp)
            cp = pltpu.make_async_remote_copy(
                src_ref=ins[w], dst_ref=mine, send_sem=send.at[w], recv_sem=recv.at[w],
                device_id=(x, y, 1 - c), device_id_type=MESH)
            cp.start()
            copies.append(cp)
        for w, cp in enumerate(copies):
            hr = halves[w].shape[0]
            theirs = outs[w].at[pl.ds((1 - c) * hr, hr)]
            cp.wait_send()
            pltpu.make_async_remote_copy(
                src_ref=ins[w], dst_ref=theirs, send_sem=send.at[w], recv_sem=recv.at[w],
                device_id=(x, y, 1 - c), device_id_type=MESH).wait_recv()
        for cp in locals_:
            cp.wait()

    return pl.pallas_call(
        body, name="pair_swap",
        in_specs=[ANY] * nw, out_specs=[ANY] * nw,
        out_shape=[_sds((2 * h.shape[0], h.shape[1]), F32) for h in halves],
        scratch_shapes=[pltpu.SemaphoreType.DMA((nw,))] * 3,
    )(*halves)


def _allsum_small(block):
    m_per, n = block.shape
    n_dev = 8

    def body(x_ref, all_ref, sum_ref, send_sems, recv_sems, local_sem):
        x, y, c, chips = _place()
        me, sibling = (x, y, c), (x, y, 1 - c)

        def rows(px, py, pc):
            return all_ref.at[pl.ds((4 * px + 2 * py + pc) * m_per, m_per), :]

        def copy(k, blk, to, src=None):
            return pltpu.make_async_remote_copy(
                src_ref=rows(*blk) if src is None else src, dst_ref=rows(*blk), send_sem=send_sems.at[k],
                recv_sem=recv_sems.at[k], device_id=to, device_id_type=MESH)

        mine = pltpu.make_async_copy(x_ref, rows(*me), local_sem)
        mine.start()
        first = [copy(0, me, sibling, src=x_ref)]
        first += [copy(1 + j, me, (*chip, c), src=x_ref) for j, chip in enumerate(chips)]
        for cp in first:
            cp.start()
        passed = [copy(4 + j, (*chip, c), sibling) for j, chip in enumerate(chips)]
        for j, chip in enumerate(chips):
            copy(1 + j, (*chip, c), me).wait_recv()
            passed[j].start()
        copy(0, sibling, me).wait_recv()
        for j, chip in enumerate(chips):
            copy(4 + j, (*chip, 1 - c), me).wait_recv()
        for cp in first + passed:
            cp.wait_send()
        mine.wait()
        acc = all_ref[0:m_per, :]
        for d in range(1, n_dev):
            acc = acc + all_ref[d * m_per:(d + 1) * m_per, :]
        sum_ref[...] = acc

    vm = pl.BlockSpec(memory_space=pltpu.VMEM)
    return pl.pallas_call(
        body, name="allsum_small",
        in_specs=[vm], out_specs=[vm, vm],
        out_shape=[_sds((n_dev * m_per, n), F32), _sds((m_per, n), F32)],
        scratch_shapes=[pltpu.SemaphoreType.DMA((7,)), pltpu.SemaphoreType.DMA((7,)), pltpu.SemaphoreType.DMA],
    )(block)[1]


_SMALL = (("w_pool", 4 * POOL_GC * POOL_GC), ("b_in", IN_WIDTH), ("g_mix_pre", D_MODEL), ("g_mix_post", D_MODEL),
          ("g_mlp_pre", D_MODEL), ("g_mlp_post", D_MODEL), ("pool_scale", POOL_WIDTH), ("attn_sinks", N_Q_HEADS),
          ("loss", 1))


def _pack_small(vals):
    parts = []
    for name, size in _SMALL:
        flat = vals[name].reshape(-1).astype(F32)
        padded = -(-size // (8 * LANES)) * (8 * LANES)
        parts.append(jnp.pad(flat, (0, padded - size)).reshape(-1, LANES))
    return jnp.concatenate(parts, axis=0)


def _unpack_small(packed, shapes):
    out, row = {}, 0
    for name, size in _SMALL:
        nrows = -(-size // (8 * LANES)) * 8
        out[name] = packed[row:row + nrows].reshape(-1)[:size].reshape(shapes[name])
        row += nrows
    return out


_BIG = ("w_in", "w_branch_pool", "w_branch_attn", "w_out", "w_up", "w_down")
_ORDER = ("g_mix_pre", "w_in", "b_in", "w_pool", "pool_scale", "attn_sinks", "w_branch_pool", "w_branch_attn",
          "w_out", "g_mix_post", "g_mlp_pre", "w_up", "w_down", "g_mlp_post")


def _cols_to_full(g):
    return jnp.transpose(g, (1, 0, 2)).reshape(g.shape[1], N_CHIPS * g.shape[2])


def _full_to_cols(a):
    r, n = a.shape
    return jnp.transpose(a.reshape(r, N_CHIPS, n // N_CHIPS), (1, 0, 2))


def _local_grads(x2, tgt, seq, full, small):
    tabs = _rope_tables(seq)
    g1, g2, g3, g4 = (small[n] for n in ("g_mix_pre", "g_mix_post", "g_mlp_pre", "g_mlp_post"))
    sinks = small["attn_sinks"].reshape(N_Q_HEADS)
    w_pool = small["w_pool"].reshape(4, POOL_GC, POOL_GC)
    w_in, w_bp, w_ba, w_out, w_up, w_down = (full[n] for n in _BIG)

    h, u, q, k, v, gate = _inproj(x2, g1, w_in, small["b_in"], tabs, seq)
    diff, y_pool = _pool_fwd(u, w_pool, small["pool_scale"], seq)
    y_attn = _attn_fwd(q, k, v, sinks, seq)
    bp, ba, merged, mix, x1, h2 = _merge_out(y_pool, y_attn, gate, x2, w_bp, w_ba, w_out, g2, g3)
    up, act = _mlp_up(h2, w_up)
    dff, dy, loss_acc, dg4 = _mlp_down_loss(act, x1, tgt, w_down, g4)

    dup = _mlp_down_bwd(dff, up, w_down)
    dw_down = _dw("down", act, dff, 1024, 1024).reshape(N_CHIPS, D_FF // N_CHIPS, D_MODEL)
    dx1, dmix, dg3, dg2 = _mlp_up_bwd(dup, dy, x1, mix, w_up, g3, g2)
    dw_up = _dw("up", h2, dup, 1024, 1024, shard_cols=True)
    dbp, dba, dgate, dyp, dya = _merge_bwd(dmix, gate, bp, ba, w_out, w_bp, w_ba)
    dw_out = _dw("out", merged, dmix, 1024, 1024).reshape(N_CHIPS, D_MODEL // N_CHIPS, D_MODEL)
    dw_bp = _dw("branch_pool", y_pool, dbp, 512, 256, shard_cols=True)
    dw_ba = _dw("branch_attn", y_attn, dba, 512, 256, shard_cols=True)
    dq, dk, dv, dsink = _attn_bwd(q, k, v, dya, sinks, tabs, seq)
    du, dw_pool, dps = _pool_bwd(dyp, diff, w_pool, small["pool_scale"], seq)
    parts = (du, dq, dk, dv, dgate)
    gx, dg1, db_in = _inproj_bwd(parts, x2, dx1, w_in, g1)
    dw_in = _full_to_cols(_dw_in(h, parts))

    big = dict(w_in=dw_in, w_branch_pool=dw_bp, w_branch_attn=dw_ba, w_out=dw_out, w_up=dw_up, w_down=dw_down)
    little = dict(w_pool=dw_pool, b_in=db_in, g_mix_pre=dg1, g_mix_post=dg2, g_mlp_pre=dg3, g_mlp_post=dg4,
                  pool_scale=dps, attn_sinks=dsink[:, 0], loss=loss_acc[0, 0])
    return gx, big, little


def kernel(x, g_mix_pre, w_in, b_in, w_pool, pool_scale, attn_sinks, w_branch_pool, w_branch_attn, w_out, g_mix_post, g_mlp_pre, w_up, w_down, g_mlp_post, loss_target, m_g_mix_pre, m_w_in, m_b_in, m_w_pool, m_pool_scale, m_attn_sinks, m_w_branch_pool, m_w_branch_attn, m_w_out, m_g_mix_post, m_g_mlp_pre, m_w_up, m_w_down, m_g_mlp_post, v_g_mix_pre, v_w_in, v_b_in, v_w_pool, v_pool_scale, v_attn_sinks, v_w_branch_pool, v_w_branch_attn, v_w_out, v_g_mix_post, v_g_mlp_pre, v_w_up, v_w_down, v_g_mlp_post):
    weights = dict(g_mix_pre=g_mix_pre, w_in=w_in, b_in=b_in, w_pool=w_pool, pool_scale=pool_scale,
                   attn_sinks=attn_sinks, w_branch_pool=w_branch_pool, w_branch_attn=w_branch_attn, w_out=w_out,
                   g_mix_post=g_mix_post, g_mlp_pre=g_mlp_pre, w_up=w_up, w_down=w_down, g_mlp_post=g_mlp_post)
    mom1 = dict(g_mix_pre=m_g_mix_pre, w_in=m_w_in, b_in=m_b_in, w_pool=m_w_pool, pool_scale=m_pool_scale,
                attn_sinks=m_attn_sinks, w_branch_pool=m_w_branch_pool, w_branch_attn=m_w_branch_attn,
                w_out=m_w_out, g_mix_post=m_g_mix_post, g_mlp_pre=m_g_mlp_pre, w_up=m_w_up, w_down=m_w_down,
                g_mlp_post=m_g_mlp_post)
    mom2 = dict(g_mix_pre=v_g_mix_pre, w_in=v_w_in, b_in=v_b_in, w_pool=v_w_pool, pool_scale=v_pool_scale,
                attn_sinks=v_attn_sinks, w_branch_pool=v_w_branch_pool, w_branch_attn=v_w_branch_attn,
                w_out=v_w_out, g_mix_post=v_g_mix_post, g_mlp_pre=v_g_mlp_pre, w_up=v_w_up, w_down=v_w_down,
                g_mlp_post=v_g_mlp_post)
    b_loc, seq, _ = x.shape
    x2 = x.reshape(b_loc * seq, D_MODEL)
    tgt = loss_target.reshape(b_loc * seq, D_MODEL)
    core = lax.axis_index("c").astype(jnp.int32).reshape(1)
    chip = (2 * lax.axis_index("x") + lax.axis_index("y")).astype(jnp.int32).reshape(1)

    shards = [weights[n][0].astype(BF16) for n in _BIG]
    slabs = dict(zip(_BIG, _gather_weights(shards)))
    full = {n: (_cols_to_full(slabs[n]) if n in ("w_in", "w_branch_pool", "w_branch_attn", "w_up")
                else slabs[n].reshape(-1, D_MODEL)) for n in _BIG}
    small = {n: weights[n] for n in _ORDER if n not in _BIG}

    gx, big, little = _local_grads(x2, tgt, seq, full, small)

    names = list(_BIG)
    got = _pair_exchange([big[n] for n in names])
    sums = [_pair_sum(core, big[n], g) for n, g in zip(names, got)]
    recv = _chip_exchange([s[1] for s in sums])
    halves = [_chip_sum(chip, s[0], r) for s, r in zip(sums, recv)]
    grads = dict(zip(names, _pair_swap(halves)))

    small_shapes = {n: weights[n].shape for n, _ in _SMALL if n != "loss"}
    small_shapes["loss"] = ()
    total = _unpack_small(_allsum_small(_pack_small(little)), small_shapes)
    loss = total.pop("loss")
    grads.update(total)

    delta, new_m, new_v = {}, {}, {}
    for n in names:
        shape = weights[n].shape
        d, nm, nv = _adamw(weights[n][0], grads[n], mom1[n][0], mom2[n][0])
        grads[n] = grads[n].reshape(shape)
        delta[n], new_m[n], new_v[n] = d.reshape(shape), nm.reshape(shape), nv.reshape(shape)
    packed = [_pack_small({**src, "loss": jnp.zeros((), F32)}) for src in (weights, grads, mom1, mom2)]
    for dst, res in zip((delta, new_m, new_v), _adamw(*packed)):
        dst.update({n: a for n, a in _unpack_small(res, small_shapes).items() if n != "loss"})

    return (loss, gx.reshape(x.shape), *[grads[n] for n in _ORDER], *[delta[n] for n in _ORDER],
            *[new_m[n] for n in _ORDER], *[new_v[n] for n in _ORDER])
```

```python
import jax
import jax.numpy as jnp
from jax import lax
from jax.experimental import pallas as pl
from jax.experimental.pallas import tpu as pltpu

F32 = jnp.float32
BF16 = jnp.bfloat16

D_MODEL = 1024
POOL_WINDOWS = (2, 4, 8, 16)
POOL_WIDTH = 512
POOL_GC = 128
HALO = 16
HEAD_DIM = 64
N_Q_HEADS = 8
ATTN_WIDTH = 512
KV_WIDTH = 128
BLOCK = 128
NEG_INF = -1e30
ROPE_THETA = 500000.0
ROT_DIM = 16
GATE_WIDTH = 2048
IN_WIDTH = 3328
D_FF = 4096
EPS = 1e-6
SCALE = HEAD_DIM ** -0.5
C_Q, C_K, C_V, C_G = 512, 1024, 1152, 1280

ADAM_LR, ADAM_B1, ADAM_B2, ADAM_EPS, ADAM_WD, ADAM_STEP = 0.001, 0.9, 0.999, 1e-08, 0.01, 10

N_CHIPS = 4
N_DEV = 8
LANES = 128
TM = 512
TP = 256
VMEM_MB = 56

MESH = pl.DeviceIdType.MESH
ANY = pl.BlockSpec(memory_space=pl.ANY)


def _cp(*sem, vmem=VMEM_MB):
    return pltpu.CompilerParams(dimension_semantics=sem, vmem_limit_bytes=vmem * 1024 * 1024)


def _rows(tile, cols):
    return pl.BlockSpec((tile, cols), lambda i: (i, 0))


def _const(shape):
    nd = len(shape)
    return pl.BlockSpec(shape, lambda i: (0,) * nd)


def _sds(shape, dtype):
    return jax.ShapeDtypeStruct(shape, dtype)


def _dot(a, b):
    return jnp.dot(a, b, preferred_element_type=F32)


def _dot_nt(a, b):
    return lax.dot_general(a, b, (((1,), (1,)), ((), ())), preferred_element_type=F32)


def _dot_tn(a, b):
    return lax.dot_general(a, b, (((0,), (0,)), ((), ())), preferred_element_type=F32)


def _rms(x):
    return lax.rsqrt(jnp.mean(x * x, axis=-1, keepdims=True) + EPS)


def _norm_bwd(x, g, dout):
    r = _rms(x)
    n = x * r
    dn = dout * g
    dx = r * (dn - n * jnp.mean(dn * n, axis=-1, keepdims=True))
    return dx, jnp.sum(dout * n, axis=0, keepdims=True)


def _rot_fwd(t, c, a, bt):
    return t * c + pltpu.roll(t, LANES - 8, 1) * a + pltpu.roll(t, 8, 1) * bt


def _rot_bwd(d, c, a, bt):
    return d * c + pltpu.roll(d * a, 8, 1) + pltpu.roll(d * bt, LANES - 8, 1)


def _rope_tables(seq):
    pos = jnp.arange(seq, dtype=F32)
    inv_freq = ROPE_THETA ** (-jnp.arange(0, ROT_DIM, 2, dtype=F32) / ROT_DIM)
    ang = pos[:, None] * inv_freq[None, :]
    cos, sin = jnp.cos(ang), jnp.sin(ang)
    ones = jnp.ones((seq, HEAD_DIM - ROT_DIM), F32)
    zeros8 = jnp.zeros((seq, 8), F32)
    zrest = jnp.zeros((seq, HEAD_DIM - ROT_DIM), F32)
    c = jnp.concatenate([cos, cos, ones], axis=1)
    a = jnp.concatenate([-sin, zeros8, zrest], axis=1)
    bt = jnp.concatenate([zeros8, sin, zrest], axis=1)
    return tuple(jnp.tile(t, (1, 2)) for t in (c, a, bt))


class _Exchange:
    def __init__(self, inputs, out_shapes, sems, start, finish, aliases=None):
        self.inputs, self.out_shapes, self.sems = list(inputs), list(out_shapes), list(sems)
        self.start, self.finish, self.aliases = start, finish, dict(aliases or {})


def _call(body, *, name, grid, in_specs, out_specs, out_shape, args, scratch=(), sem=(), exchanges=()):
    in_specs, out_specs, out_shape, scratch = list(in_specs), list(out_specs), list(out_shape), list(scratch)
    if not exchanges:
        return pl.pallas_call(body, name=name, grid=grid, in_specs=in_specs, out_specs=out_specs,
                              out_shape=out_shape, scratch_shapes=scratch, compiler_params=_cp(*sem))(*args)
    n_in, n_out, n_scr = len(in_specs), len(out_specs), len(scratch)
    x_in = [a for ex in exchanges for a in ex.inputs]
    x_out = [s for ex in exchanges for s in ex.out_shapes]
    x_sem = [s for ex in exchanges for s in ex.sems]
    aliases, i_off, o_off = {}, n_in, n_out
    for ex in exchanges:
        for i, o in ex.aliases.items():
            aliases[i_off + i] = o_off + o
        i_off += len(ex.inputs)
        o_off += len(ex.out_shapes)

    def split(flat):
        out, pos = [], 0
        for ex, n in zip(exchanges, flat[1]):
            out.append(flat[0][pos:pos + n])
            pos += n
        return out

    def carrier(*refs):
        pos = 0
        groups = []
        for n in (n_in, len(x_in), n_out, len(x_out), n_scr, len(x_sem)):
            groups.append(refs[pos:pos + n])
            pos += n
        ins, xin, outs, xout, scr, xsem = groups
        xin = split((xin, [len(ex.inputs) for ex in exchanges]))
        xout = split((xout, [len(ex.out_shapes) for ex in exchanges]))
        xsem = split((xsem, [len(ex.sems) for ex in exchanges]))
        first = pl.program_id(0) == 0
        last = pl.program_id(0) == grid[0] - 1
        for d in range(1, len(grid)):
            first = jnp.logical_and(first, pl.program_id(d) == 0)
            last = jnp.logical_and(last, pl.program_id(d) == grid[d] - 1)

        @pl.when(first)
        def _():
            for ex, i, o, s in zip(exchanges, xin, xout, xsem):
                ex.start(i, o, s)

        body(*ins, *outs, *scr)

        @pl.when(last)
        def _():
            for ex, i, o, s in zip(exchanges, xin, xout, xsem):
                ex.finish(i, o, s)

    res = pl.pallas_call(
        carrier, name=name, grid=grid, in_specs=in_specs + [ANY] * len(x_in),
        out_specs=out_specs + [ANY] * len(x_out), out_shape=out_shape + x_out,
        scratch_shapes=scratch + x_sem, input_output_aliases=aliases,
        compiler_params=_cp(*(["arbitrary"] * len(grid))),
    )(*args, *x_in)
    return res[:n_out], split((res[n_out:], [len(ex.out_shapes) for ex in exchanges]))


def _alone(name, ex):
    def body(*refs):
        ni, no = len(ex.inputs), len(ex.out_shapes)
        ins, outs, sems = refs[:ni], refs[ni:ni + no], refs[ni + no:]
        ex.start(ins, outs, sems)
        ex.finish(ins, outs, sems)

    return pl.pallas_call(
        body, name=name, in_specs=[ANY] * len(ex.inputs), out_specs=[ANY] * len(ex.out_shapes),
        out_shape=ex.out_shapes, scratch_shapes=ex.sems, input_output_aliases=ex.aliases,
    )(*ex.inputs)


def _place():
    x, y, c = lax.axis_index("x"), lax.axis_index("y"), lax.axis_index("c")
    chips = [(1 - x, y), (x, 1 - y), (1 - x, 1 - y)]
    return x, y, c, chips


def _remote(src, dst, send, recv, to):
    return pltpu.make_async_remote_copy(src_ref=src, dst_ref=dst, send_sem=send, recv_sem=recv,
                                        device_id=to, device_id_type=MESH)


def _ex_gather(shards):
    nw = len(shards)
    hrs = [s.shape[0] // 2 for s in shards]

    def copies(ins, outs, sems):
        send, recv, fsend, frecv, lsem = sems
        x, y, c, chips = _place()
        me = 2 * x + y
        sibling = (x, y, 1 - c)

        def piece(w, chip_idx, core):
            return outs[w].at[chip_idx, pl.ds(core * hrs[w], hrs[w])]

        local = [pltpu.make_async_copy(ins[w], outs[w].at[me], lsem.at[w]) for w in range(nw)]
        first = [[_remote(ins[w].at[pl.ds(c * hrs[w], hrs[w])], piece(w, me, c), send.at[w, k], recv.at[w, k],
                          (cx, cy, c)) for k, (cx, cy) in enumerate(chips)] for w in range(nw)]
        landed = [[_remote(piece(w, 2 * cx + cy, c), piece(w, 2 * cx + cy, c), send.at[w, k], recv.at[w, k],
                           (cx, cy, c)) for k, (cx, cy) in enumerate(chips)] for w in range(nw)]
        passed = [[_remote(piece(w, 2 * cx + cy, c), piece(w, 2 * cx + cy, c), fsend.at[w, k], frecv.at[w, k],
                           sibling) for k, (cx, cy) in enumerate(chips)] for w in range(nw)]
        handed = [[_remote(piece(w, 2 * cx + cy, 1 - c), piece(w, 2 * cx + cy, 1 - c), fsend.at[w, k],
                           frecv.at[w, k], sibling) for k, (cx, cy) in enumerate(chips)] for w in range(nw)]
        return local, first, landed, passed, handed

    def start(ins, outs, sems):
        local, first, _, _, _ = copies(ins, outs, sems)
        for w in range(nw):
            local[w].start()
            for cp in first[w]:
                cp.start()

    def finish(ins, outs, sems):
        local, first, landed, passed, handed = copies(ins, outs, sems)
        for w in range(nw):
            for k in range(3):
                landed[w][k].wait_recv()
                passed[w][k].start()
        for w in range(nw):
            for k in range(3):
                handed[w][k].wait_recv()
        for w in range(nw):
            for k in range(3):
                first[w][k].wait_send()
                passed[w][k].wait_send()
            local[w].wait()

    return _Exchange(shards, [_sds((N_CHIPS,) + s.shape, s.dtype) for s in shards],
                     [pltpu.SemaphoreType.DMA((nw, 3))] * 4 + [pltpu.SemaphoreType.DMA((nw,))], start, finish)


def _ex_pair(grads):
    nw = len(grads)

    def copies(ins, outs, sems):
        x, y, c, _ = _place()
        out = []
        for w in range(nw):
            hr = grads[w].shape[1] // 2
            out.append(_remote(ins[w].at[:, pl.ds((1 - c) * hr, hr)], outs[w], sems[0].at[w], sems[1].at[w],
                               (x, y, 1 - c)))
        return out

    def start(ins, outs, sems):
        for cp in copies(ins, outs, sems):
            cp.start()

    def finish(ins, outs, sems):
        for cp in copies(ins, outs, sems):
            cp.wait()

    return _Exchange(grads, [_sds((N_CHIPS, g.shape[1] // 2, g.shape[2]), F32) for g in grads],
                     [pltpu.SemaphoreType.DMA((nw,))] * 2, start, finish)


def _ex_chip(pieces):
    nw = len(pieces)

    def copies(ins, outs, sems):
        x, y, c, chips = _place()
        return [_remote(ins[w].at[2 * cx + cy], outs[w].at[k], sems[0].at[w, k], sems[1].at[w, k], (cx, cy, c))
                for w in range(nw) for k, (cx, cy) in enumerate(chips)]

    def start(ins, outs, sems):
        for cp in copies(ins, outs, sems):
            cp.start()

    def finish(ins, outs, sems):
        for cp in copies(ins, outs, sems):
            cp.wait()

    return _Exchange(pieces, [_sds((3,) + p.shape[1:], BF16) for p in pieces],
                     [pltpu.SemaphoreType.DMA((nw, 3))] * 2, start, finish)


def _ex_swap(fulls):
    nw = len(fulls)

    def start(ins, outs, sems):
        x, y, c, _ = _place()
        for w in range(nw):
            hr = fulls[w].shape[0] // 2
            mine = pl.ds(c * hr, hr)
            _remote(ins[w].at[mine], outs[w].at[mine], sems[0].at[w], sems[1].at[w], (x, y, 1 - c)).start()

    def finish(ins, outs, sems):
        x, y, c, _ = _place()
        for w in range(nw):
            hr = fulls[w].shape[0] // 2
            mine, theirs = pl.ds(c * hr, hr), pl.ds((1 - c) * hr, hr)
            _remote(ins[w].at[mine], outs[w].at[mine], sems[0].at[w], sems[1].at[w], (x, y, 1 - c)).wait_send()
            _remote(ins[w].at[theirs], outs[w].at[theirs], sems[0].at[w], sems[1].at[w], (x, y, 1 - c)).wait_recv()

    return _Exchange(fulls, [_sds(f.shape, F32) for f in fulls], [pltpu.SemaphoreType.DMA((nw,))] * 2,
                     start, finish, aliases={w: w for w in range(nw)})


def _ex_allgather(block):
    m_per, n = block.shape

    def copies(ins, outs, sems):
        send, recv, lsem = sems
        x, y, c, chips = _place()
        me, sibling = (x, y, c), (x, y, 1 - c)

        def rows(px, py, pc):
            return outs[0].at[pl.ds((4 * px + 2 * py + pc) * m_per, m_per), :]

        def copy(k, blk, to, src=None):
            return _remote(rows(*blk) if src is None else src, rows(*blk), send.at[k], recv.at[k], to)

        mine = pltpu.make_async_copy(ins[0], rows(*me), lsem.at[0])
        first = [copy(0, me, sibling, src=ins[0])] + [copy(1 + j, me, (*chip, c), src=ins[0])
                                                     for j, chip in enumerate(chips)]
        passed = [copy(4 + j, (*chip, c), sibling) for j, chip in enumerate(chips)]
        landed = [copy(1 + j, (*chip, c), me) for j, chip in enumerate(chips)]
        handed = [copy(0, sibling, me)] + [copy(4 + j, (*chip, 1 - c), me) for j, chip in enumerate(chips)]
        return mine, first, passed, landed, handed

    def start(ins, outs, sems):
        mine, first, _, _, _ = copies(ins, outs, sems)
        mine.start()
        for cp in first:
            cp.start()

    def finish(ins, outs, sems):
        mine, first, passed, landed, handed = copies(ins, outs, sems)
        for j in range(3):
            landed[j].wait_recv()
            passed[j].start()
        for cp in handed:
            cp.wait_recv()
        for cp in first + passed:
            cp.wait_send()
        mine.wait()

    return _Exchange([block], [_sds((N_DEV * m_per, n), F32)],
                     [pltpu.SemaphoreType.DMA((7,)), pltpu.SemaphoreType.DMA((7,)), pltpu.SemaphoreType.DMA((1,))],
                     start, finish)


def _inproj(x2, g1, w_in, b_in, tabs, seq, exchanges=()):
    T = x2.shape[0]
    tm = min(TM, seq)
    nseq = seq // tm

    def body(x_ref, g_ref, w_ref, b_ref, c_ref, a_ref, bt_ref, h_ref, u_ref, q_ref, k_ref, v_ref, gate_ref):
        x = x_ref[...]
        h = (x * _rms(x) * g_ref[...]).astype(BF16)
        h_ref[...] = h

        def proj(lo, hi):
            return _dot(h, w_ref[:, lo:hi]) + b_ref[:, lo:hi]

        c, a, bt = c_ref[...], a_ref[...], bt_ref[...]
        u_ref[...] = proj(0, C_Q)
        q = proj(C_Q, C_K)
        for p in range(4):
            sl = slice(LANES * p, LANES * (p + 1))
            q_ref[:, sl] = _rot_fwd(q[:, sl], c, a, bt).astype(BF16)
        kv = proj(C_K, C_G)
        k_ref[...] = _rot_fwd(kv[:, :KV_WIDTH], c, a, bt).astype(BF16)
        v_ref[...] = kv[:, KV_WIDTH:].astype(BF16)
        for j in range(2):
            lo = C_G + D_MODEL * j
            gate_ref[:, D_MODEL * j:D_MODEL * (j + 1)] = jax.nn.sigmoid(proj(lo, lo + D_MODEL)).astype(BF16)

    tab = pl.BlockSpec((tm, LANES), lambda i: (i % nseq, 0))
    return _call(
        body, name="inproj", grid=(T // tm,),
        in_specs=[_rows(tm, D_MODEL), _const((1, D_MODEL)), _const((D_MODEL, IN_WIDTH)), _const((1, IN_WIDTH)),
                  tab, tab, tab],
        out_specs=[_rows(tm, D_MODEL), _rows(tm, POOL_WIDTH), _rows(tm, ATTN_WIDTH), _rows(tm, KV_WIDTH),
                   _rows(tm, KV_WIDTH), _rows(tm, GATE_WIDTH)],
        out_shape=[_sds((T, D_MODEL), BF16), _sds((T, POOL_WIDTH), F32), _sds((T, ATTN_WIDTH), BF16),
                   _sds((T, KV_WIDTH), BF16), _sds((T, KV_WIDTH), BF16), _sds((T, GATE_WIDTH), BF16)],
        args=(x2, g1, w_in, b_in, *tabs), sem=("parallel",), exchanges=exchanges)


def _inv_count(pos, w):
    return 1.0 / jnp.minimum(pos + 1, w).astype(F32)


def _pool_fwd(u, w_pool, pool_scale, seq):
    T = u.shape[0]
    tp = min(TP, seq)
    nseq = seq // tp
    per = tp // HALO

    def body(u_ref, prev_ref, w_ref, s_ref, diff_ref, y_ref):
        i = pl.program_id(0)
        first = (i % nseq) == 0
        prev = jnp.where(first, 0.0, prev_ref[...])
        ext = jnp.concatenate([prev, u_ref[...]], axis=0)
        pos = (i % nseq) * tp + lax.broadcasted_iota(jnp.int32, (tp, 1), 0)
        for gi, w in enumerate(POOL_WINDOWS):
            sl = slice(POOL_GC * gi, POOL_GC * (gi + 1))
            xg = ext[:, sl]
            s = xg
            sh = 1
            while sh < w:
                s = s + pltpu.roll(s, sh, 0)
                sh *= 2
            pooled = s[HALO:] * _inv_count(pos, w)
            diff = (pooled - xg[HALO:]).astype(BF16)
            diff_ref[:, sl] = diff
            mixed = _dot(diff, w_ref[gi].astype(BF16))
            y_ref[:, sl] = (mixed * s_ref[:, sl]).astype(BF16)

    return _call(
        body, name="pool_fwd", grid=(T // tp,),
        in_specs=[_rows(tp, POOL_WIDTH),
                  pl.BlockSpec((HALO, POOL_WIDTH), lambda i: (jnp.maximum(i * per - 1, 0), 0)),
                  _const((4, POOL_GC, POOL_GC)), _const((1, POOL_WIDTH))],
        out_specs=[_rows(tp, POOL_WIDTH), _rows(tp, POOL_WIDTH)],
        out_shape=[_sds((T, POOL_WIDTH), BF16), _sds((T, POOL_WIDTH), BF16)],
        args=(u, u, w_pool, pool_scale), sem=("parallel",))


def _attn_masks(n):
    qi = lax.broadcasted_iota(jnp.int32, (BLOCK, 2 * BLOCK), 0)
    kj = lax.broadcasted_iota(jnp.int32, (BLOCK, 2 * BLOCK), 1)
    rel = qi + BLOCK - kj
    valid = (rel >= 0) & (rel < BLOCK) & (kj >= jnp.where(n > 0, 0, BLOCK))
    lo = lax.broadcasted_iota(jnp.int32, (BLOCK, LANES), 1) < HEAD_DIM
    return valid, lo


def _head_probs(qm, kk, valid, sink):
    s = _dot_nt(qm, kk) * SCALE
    s = jnp.where(valid, s, NEG_INF)
    m = jnp.maximum(jnp.max(s, axis=1, keepdims=True), sink)
    ex = jnp.exp(s - m)
    es = jnp.exp(sink - m)
    inv = 1.0 / (jnp.sum(ex, axis=1, keepdims=True) + es)
    return ex * inv, es * inv


def _attn_fwd(q, k, v, sinks, seq, exchanges=()):
    T = q.shape[0]
    nb = seq // BLOCK

    def body(sink_ref, q_ref, kp_ref, kc_ref, vp_ref, vc_ref, o_ref):
        n = pl.program_id(0) % nb
        kk = jnp.concatenate([kp_ref[...], kc_ref[...]], axis=0)
        vv = jnp.concatenate([vp_ref[...], vc_ref[...]], axis=0)
        valid, lo = _attn_masks(n)
        for p in range(4):
            h = p // 2
            keep = lo if h == 0 else jnp.logical_not(lo)
            qp = q_ref[:, LANES * p:LANES * (p + 1)].astype(F32)
            outs = []
            for e in range(2):
                t = qp if e == h else pltpu.roll(qp, HEAD_DIM, 1)
                qm = jnp.where(keep, t, 0.0).astype(BF16)
                pr, _ = _head_probs(qm, kk, valid, sink_ref[2 * p + e])
                o = _dot(pr.astype(BF16), vv)
                outs.append(o if e == h else pltpu.roll(o, HEAD_DIM, 1))
            o_ref[:, LANES * p:LANES * (p + 1)] = jnp.where(lo, outs[0], outs[1]).astype(BF16)

    cur = lambda i: (i, 0)
    prv = lambda i: (jnp.where(i % nb == 0, i, i - 1), 0)
    return _call(
        body, name="attn_fwd", grid=(T // BLOCK,),
        in_specs=[pl.BlockSpec(memory_space=pltpu.SMEM),
                  pl.BlockSpec((BLOCK, ATTN_WIDTH), cur),
                  pl.BlockSpec((BLOCK, KV_WIDTH), prv), pl.BlockSpec((BLOCK, KV_WIDTH), cur),
                  pl.BlockSpec((BLOCK, KV_WIDTH), prv), pl.BlockSpec((BLOCK, KV_WIDTH), cur)],
        out_specs=[pl.BlockSpec((BLOCK, ATTN_WIDTH), cur)],
        out_shape=[_sds((T, ATTN_WIDTH), BF16)],
        args=(sinks, q, k, k, v, v), sem=("parallel",), exchanges=exchanges)


def _merge_out(y_pool, y_attn, gate, x2, w_bp, w_ba, w_out, g2, g3, exchanges=()):
    T = x2.shape[0]
    tm = min(TM, T)

    def body(yp_ref, ya_ref, gate_ref, x_ref, wbp_ref, wba_ref, wo_ref, g2_ref, g3_ref,
             bp_ref, ba_ref, mg_ref, mix_ref, x1_ref, h2_ref):
        bp = _dot(yp_ref[...], wbp_ref[...])
        ba = _dot(ya_ref[...], wba_ref[...])
        bp_ref[...] = bp.astype(BF16)
        ba_ref[...] = ba.astype(BF16)
        merged = (gate_ref[:, :D_MODEL].astype(F32) * bp + gate_ref[:, D_MODEL:].astype(F32) * ba).astype(BF16)
        mg_ref[...] = merged
        mix = _dot(merged, wo_ref[...])
        mix_ref[...] = mix
        x1 = x_ref[...] + mix * _rms(mix) * g2_ref[...]
        x1_ref[...] = x1
        h2_ref[...] = (x1 * _rms(x1) * g3_ref[...]).astype(BF16)

    return _call(
        body, name="merge_out", grid=(T // tm,),
        in_specs=[_rows(tm, POOL_WIDTH), _rows(tm, ATTN_WIDTH), _rows(tm, GATE_WIDTH), _rows(tm, D_MODEL),
                  _const((POOL_WIDTH, D_MODEL)), _const((ATTN_WIDTH, D_MODEL)), _const((D_MODEL, D_MODEL)),
                  _const((1, D_MODEL)), _const((1, D_MODEL))],
        out_specs=[_rows(tm, D_MODEL)] * 6,
        out_shape=[_sds((T, D_MODEL), BF16), _sds((T, D_MODEL), BF16), _sds((T, D_MODEL), BF16),
                   _sds((T, D_MODEL), F32), _sds((T, D_MODEL), F32), _sds((T, D_MODEL), BF16)],
        args=(y_pool, y_attn, gate, x2, w_bp, w_ba, w_out, g2, g3), sem=("parallel",), exchanges=exchanges)


def _mlp_up(h2, w_up):
    T = h2.shape[0]
    tm = min(TM, T)

    def body(h_ref, w_ref, up_ref, a_ref):
        h = h_ref[...]
        for j in range(D_FF // D_MODEL):
            sl = slice(D_MODEL * j, D_MODEL * (j + 1))
            up = _dot(h, w_ref[:, sl])
            up_ref[:, sl] = up.astype(BF16)
            a_ref[:, sl] = jnp.square(jnp.maximum(up, 0.0)).astype(BF16)

    return _call(
        body, name="mlp_up", grid=(T // tm,),
        in_specs=[_rows(tm, D_MODEL), _const((D_MODEL, D_FF))],
        out_specs=[_rows(tm, D_FF), _rows(tm, D_FF)],
        out_shape=[_sds((T, D_FF), BF16), _sds((T, D_FF), BF16)],
        args=(h2, w_up), sem=("parallel",))


def _mlp_down_loss(a, x1, tgt, w_down, g4):
    T = a.shape[0]
    tm = min(TM, T)

    def body(a_ref, x1_ref, t_ref, w_ref, g_ref, dff_ref, dy_ref, loss_ref, dg_ref):
        @pl.when(pl.program_id(0) == 0)
        def _():
            loss_ref[...] = jnp.zeros_like(loss_ref)
            dg_ref[...] = jnp.zeros_like(dg_ref)

        ff = _dot(a_ref[...], w_ref[...])
        g = g_ref[...]
        err = x1_ref[...] + ff * _rms(ff) * g - t_ref[...]
        loss_ref[...] += jnp.sum(err * err) * (0.5 / D_MODEL)
        dy = err * (1.0 / D_MODEL)
        dy_ref[...] = dy
        dff, dg = _norm_bwd(ff, g, dy)
        dff_ref[...] = dff.astype(BF16)
        dg_ref[...] += dg

    return _call(
        body, name="mlp_down_loss", grid=(T // tm,),
        in_specs=[_rows(tm, D_FF), _rows(tm, D_MODEL), _rows(tm, D_MODEL), _const((D_FF, D_MODEL)),
                  _const((1, D_MODEL))],
        out_specs=[_rows(tm, D_MODEL), _rows(tm, D_MODEL), _const((8, LANES)), _const((1, D_MODEL))],
        out_shape=[_sds((T, D_MODEL), BF16), _sds((T, D_MODEL), F32), _sds((8, LANES), F32),
                   _sds((1, D_MODEL), F32)],
        args=(a, x1, tgt, w_down, g4), sem=("arbitrary",))


def _mlp_down_bwd(dff, up, w_down):
    T = dff.shape[0]
    tm = min(TM, T)

    def body(d_ref, up_ref, w_ref, dup_ref):
        d = d_ref[...]
        for j in range(D_FF // D_MODEL):
            sl = slice(D_MODEL * j, D_MODEL * (j + 1))
            da = _dot_nt(d, w_ref[sl, :])
            dup_ref[:, sl] = (da * (2.0 * jnp.maximum(up_ref[:, sl].astype(F32), 0.0))).astype(BF16)

    return _call(
        body, name="mlp_down_bwd", grid=(T // tm,),
        in_specs=[_rows(tm, D_MODEL), _rows(tm, D_FF), _const((D_FF, D_MODEL))],
        out_specs=[_rows(tm, D_FF)],
        out_shape=[_sds((T, D_FF), BF16)],
        args=(dff, up, w_down), sem=("parallel",))[0]


def _mlp_up_bwd(dup, dy, x1, mix, w_up, g3, g2, exchanges=()):
    T = dup.shape[0]
    tm = min(TM, T)

    def body(dup_ref, dy_ref, x1_ref, mix_ref, w_ref, g3_ref, g2_ref, dx1_ref, dmix_ref, dg3_ref, dg2_ref):
        @pl.when(pl.program_id(0) == 0)
        def _():
            dg3_ref[...] = jnp.zeros_like(dg3_ref)
            dg2_ref[...] = jnp.zeros_like(dg2_ref)

        dh2 = _dot_nt(dup_ref[...], w_ref[...])
        dx, dg3 = _norm_bwd(x1_ref[...], g3_ref[...], dh2)
        dx1 = dy_ref[...] + dx
        dx1_ref[...] = dx1
        dg3_ref[...] += dg3
        dmix, dg2 = _norm_bwd(mix_ref[...], g2_ref[...], dx1)
        dmix_ref[...] = dmix.astype(BF16)
        dg2_ref[...] += dg2

    return _call(
        body, name="mlp_up_bwd", grid=(T // tm,),
        in_specs=[_rows(tm, D_FF), _rows(tm, D_MODEL), _rows(tm, D_MODEL), _rows(tm, D_MODEL),
                  _const((D_MODEL, D_FF)), _const((1, D_MODEL)), _const((1, D_MODEL))],
        out_specs=[_rows(tm, D_MODEL), _rows(tm, D_MODEL), _const((1, D_MODEL)), _const((1, D_MODEL))],
        out_shape=[_sds((T, D_MODEL), F32), _sds((T, D_MODEL), BF16), _sds((1, D_MODEL), F32),
                   _sds((1, D_MODEL), F32)],
        args=(dup, dy, x1, mix, w_up, g3, g2), sem=("arbitrary",), exchanges=exchanges)


def _dw(tag, a, g, ta, tn, shard_cols=False, exchanges=()):
    T, ka = a.shape
    n = g.shape[1]
    tk = min(TM, T)
    nk = T // tk

    def body(a_ref, g_ref, o_ref):
        @pl.when(pl.program_id(2) == 0)
        def _():
            o_ref[...] = jnp.zeros_like(o_ref)

        o_ref[...] += _dot_tn(a_ref[...], g_ref[...])

    if shard_cols:
        per = (n // N_CHIPS) // tn
        out_spec = pl.BlockSpec((None, ta, tn), lambda i, j, k: (j // per, i, j % per))
        out_shape = _sds((N_CHIPS, ka, n // N_CHIPS), F32)
    else:
        out_spec = pl.BlockSpec((ta, tn), lambda i, j, k: (i, j))
        out_shape = _sds((ka, n), F32)
    return _call(
        body, name="dw_" + tag, grid=(ka // ta, n // tn, nk),
        in_specs=[pl.BlockSpec((tk, ta), lambda i, j, k: (k, i)), pl.BlockSpec((tk, tn), lambda i, j, k: (k, j))],
        out_specs=[out_spec], out_shape=[out_shape],
        args=(a, g), sem=("parallel", "parallel", "arbitrary"), exchanges=exchanges)


def _dw_slabs(tag, a, g):
    T, ka = a.shape
    n = g.shape[1]
    c = n // N_CHIPS
    tk = min(TM, T)

    def body(a_ref, g_ref, o_ref):
        @pl.when(pl.program_id(0) == 0)
        def _():
            o_ref[...] = jnp.zeros_like(o_ref)

        res = _dot_tn(a_ref[...], g_ref[...])
        for j in range(N_CHIPS):
            o_ref[j] += res[:, c * j:c * (j + 1)]

    return _call(
        body, name="dw_" + tag, grid=(T // tk,),
        in_specs=[_rows(tk, ka), _rows(tk, n)],
        out_specs=[_const((N_CHIPS, ka, c))], out_shape=[_sds((N_CHIPS, ka, c), F32)],
        args=(a, g), sem=("arbitrary",))[0]


def _merge_bwd(dmix, gate, bp, ba, w_out, w_bp, w_ba, exchanges=()):
    T = dmix.shape[0]
    tm = min(TM, T)

    def body(dmix_ref, gate_ref, bp_ref, ba_ref, wo_ref, wbp_ref, wba_ref,
             dbp_ref, dba_ref, dgate_ref, dyp_ref, dya_ref):
        dm = _dot_nt(dmix_ref[...], wo_ref[...])
        for j, (b_ref, db_ref, w_ref, dy_ref) in enumerate(
                ((bp_ref, dbp_ref, wbp_ref, dyp_ref), (ba_ref, dba_ref, wba_ref, dya_ref))):
            sl = slice(D_MODEL * j, D_MODEL * (j + 1))
            gt = gate_ref[:, sl].astype(F32)
            db = (dm * gt).astype(BF16)
            db_ref[...] = db
            dgate_ref[:, sl] = (dm * b_ref[...].astype(F32) * gt * (1.0 - gt)).astype(BF16)
            dy_ref[...] = _dot_nt(db, w_ref[...]).astype(dy_ref.dtype)

    return _call(
        body, name="merge_bwd", grid=(T // tm,),
        in_specs=[_rows(tm, D_MODEL), _rows(tm, GATE_WIDTH), _rows(tm, D_MODEL), _rows(tm, D_MODEL),
                  _const((D_MODEL, D_MODEL)), _const((POOL_WIDTH, D_MODEL)), _const((ATTN_WIDTH, D_MODEL))],
        out_specs=[_rows(tm, D_MODEL), _rows(tm, D_MODEL), _rows(tm, GATE_WIDTH), _rows(tm, POOL_WIDTH),
                   _rows(tm, ATTN_WIDTH)],
        out_shape=[_sds((T, D_MODEL), BF16), _sds((T, D_MODEL), BF16), _sds((T, GATE_WIDTH), BF16),
                   _sds((T, POOL_WIDTH), F32), _sds((T, ATTN_WIDTH), BF16)],
        args=(dmix, gate, bp, ba, w_out, w_bp, w_ba), sem=("parallel",), exchanges=exchanges)


def _attn_bwd(q, k, v, do, sinks, tabs, seq, exchanges=()):
    T = q.shape[0]
    nb = seq // BLOCK
    steps = nb + 1

    def body(sink_ref, q_ref, do_ref, kp_ref, kc_ref, vp_ref, vc_ref, c_ref, a_ref, bt_ref, cp_ref, ap_ref, btp_ref,
             dq_ref, dk_ref, dv_ref, dsink_ref, ck_ref, cv_ref):
        i = pl.program_id(0)
        n = i % steps

        @pl.when(i == 0)
        def _():
            dsink_ref[...] = jnp.zeros_like(dsink_ref)

        @pl.when(n == 0)
        def _():
            ck_ref[...] = jnp.zeros_like(ck_ref)
            cv_ref[...] = jnp.zeros_like(cv_ref)

        @pl.when(n < nb)
        def _():
            kk = jnp.concatenate([kp_ref[...], kc_ref[...]], axis=0)
            vv = jnp.concatenate([vp_ref[...], vc_ref[...]], axis=0)
            valid, lo = _attn_masks(n)
            dk_acc = jnp.zeros((2 * BLOCK, KV_WIDTH), F32)
            dv_acc = jnp.zeros((2 * BLOCK, KV_WIDTH), F32)
            for p in range(4):
                h = p // 2
                keep = lo if h == 0 else jnp.logical_not(lo)
                sl = slice(LANES * p, LANES * (p + 1))
                qp = q_ref[:, sl].astype(F32)
                dop = do_ref[:, sl].astype(F32)
                parts = []
                for e in range(2):
                    idx = 2 * p + e
                    same = e == h
                    qm = jnp.where(keep, qp if same else pltpu.roll(qp, HEAD_DIM, 1), 0.0).astype(BF16)
                    dom = jnp.where(keep, dop if same else pltpu.roll(dop, HEAD_DIM, 1), 0.0).astype(BF16)
                    sink = sink_ref[idx]
                    pr, ps = _head_probs(qm, kk, valid, sink)
                    dp = _dot_nt(dom, vv)
                    delta = jnp.sum(pr * dp, axis=1, keepdims=True)
                    ds = (pr * (dp - delta) * SCALE).astype(BF16)
                    dsink_ref[idx:idx + 1, :] += jnp.zeros((1, LANES), F32) - jnp.sum(ps * delta)
                    dq = jnp.where(keep, _dot(ds, kk), 0.0)
                    parts.append(dq if same else pltpu.roll(dq, HEAD_DIM, 1))
                    dk_acc = dk_acc + _dot_tn(ds, qm)
                    dv_acc = dv_acc + _dot_tn(pr.astype(BF16), dom)
                dq_pair = jnp.where(lo, parts[0], parts[1])
                dq_ref[:, sl] = _rot_bwd(dq_pair, c_ref[...], a_ref[...], bt_ref[...]).astype(BF16)
            fin_k = ck_ref[...] + dk_acc[:BLOCK]
            dk_ref[...] = _rot_bwd(fin_k, cp_ref[...], ap_ref[...], btp_ref[...]).astype(BF16)
            dv_ref[...] = (cv_ref[...] + dv_acc[:BLOCK]).astype(BF16)
            ck_ref[...] = dk_acc[BLOCK:]
            cv_ref[...] = dv_acc[BLOCK:]

        @pl.when(n == nb)
        def _():
            dk_ref[...] = _rot_bwd(ck_ref[...], cp_ref[...], ap_ref[...], btp_ref[...]).astype(BF16)
            dv_ref[...] = cv_ref[...].astype(BF16)

    def blk(i):
        return (i // steps) * nb

    cur = lambda i: (blk(i) + jnp.minimum(i % steps, nb - 1), 0)
    prv = lambda i: (blk(i) + jnp.clip(i % steps - 1, 0, nb - 1), 0)
    tcur = lambda i: (jnp.minimum(i % steps, nb - 1), 0)
    tprv = lambda i: (jnp.clip(i % steps - 1, 0, nb - 1), 0)
    kv = lambda m: pl.BlockSpec((BLOCK, KV_WIDTH), m)
    return _call(
        body, name="attn_bwd", grid=((T // seq) * steps,),
        in_specs=[pl.BlockSpec(memory_space=pltpu.SMEM),
                  pl.BlockSpec((BLOCK, ATTN_WIDTH), cur), pl.BlockSpec((BLOCK, ATTN_WIDTH), cur),
                  kv(prv), kv(cur), kv(prv), kv(cur),
                  kv(tcur), kv(tcur), kv(tcur), kv(tprv), kv(tprv), kv(tprv)],
        out_specs=[pl.BlockSpec((BLOCK, ATTN_WIDTH), cur), kv(prv), kv(prv), _const((8, LANES))],
        out_shape=[_sds((T, ATTN_WIDTH), BF16), _sds((T, KV_WIDTH), BF16), _sds((T, KV_WIDTH), BF16),
                   _sds((8, LANES), F32)],
        scratch=[pltpu.VMEM((BLOCK, KV_WIDTH), F32), pltpu.VMEM((BLOCK, KV_WIDTH), F32)],
        args=(sinks, q, do, k, k, v, v, *tabs, *tabs), sem=("arbitrary",), exchanges=exchanges)


def _pool_bwd(dyp, diff, w_pool, pool_scale, seq, exchanges=()):
    T = dyp.shape[0]
    tp = min(TP, seq)
    nseq = seq // tp
    per = tp // HALO
    last_halo = T // HALO - 1

    def body(dy_ref, nxt_ref, diff_ref, w_ref, s_ref, du_ref, dw_ref, ds_ref):
        i = pl.program_id(0)

        @pl.when(i == 0)
        def _():
            dw_ref[...] = jnp.zeros_like(dw_ref)
            ds_ref[...] = jnp.zeros_like(ds_ref)

        last = (i % nseq) == nseq - 1
        nxt = jnp.where(last, 0.0, nxt_ref[...])
        ext = jnp.concatenate([dy_ref[...], nxt], axis=0) * s_ref[...]
        pos = (i % nseq) * tp + lax.broadcasted_iota(jnp.int32, (tp + HALO, 1), 0)
        for gi, w in enumerate(POOL_WINDOWS):
            sl = slice(POOL_GC * gi, POOL_GC * (gi + 1))
            wg = w_ref[gi].astype(BF16)
            dmx = ext[:, sl].astype(BF16)
            ddiff = _dot_nt(dmx, wg)
            s = ddiff * _inv_count(pos, w)
            sh = 1
            while sh < w:
                s = s + pltpu.roll(s, tp + HALO - sh, 0)
                sh *= 2
            du_ref[:, sl] = (s[:tp] - ddiff[:tp]).astype(BF16)
            dg = diff_ref[:, sl]
            dw_ref[gi] += _dot_tn(dg, dmx[:tp])
            ds_ref[:, sl] += jnp.sum(dy_ref[:, sl] * _dot(dg, wg), axis=0, keepdims=True)

    return _call(
        body, name="pool_bwd", grid=(T // tp,),
        in_specs=[_rows(tp, POOL_WIDTH),
                  pl.BlockSpec((HALO, POOL_WIDTH), lambda i: (jnp.minimum((i + 1) * per, last_halo), 0)),
                  _rows(tp, POOL_WIDTH), _const((4, POOL_GC, POOL_GC)), _const((1, POOL_WIDTH))],
        out_specs=[_rows(tp, POOL_WIDTH), _const((4, POOL_GC, POOL_GC)), _const((1, POOL_WIDTH))],
        out_shape=[_sds((T, POOL_WIDTH), BF16), _sds((4, POOL_GC, POOL_GC), F32), _sds((1, POOL_WIDTH), F32)],
        args=(dyp, dyp, diff, w_pool, pool_scale), sem=("arbitrary",), exchanges=exchanges)


_PARTS = ((0, C_Q), (C_Q, C_K), (C_K, C_V), (C_V, C_G), (C_G, IN_WIDTH))


def _inproj_bwd(parts, x2, dx1, w_in, g1, exchanges=()):
    T = x2.shape[0]
    tm = min(TM, T)

    def body(du_ref, dq_ref, dk_ref, dv_ref, dgt_ref, x_ref, dx1_ref, w_ref, g_ref, gx_ref, dg_ref, db_ref):
        @pl.when(pl.program_id(0) == 0)
        def _():
            dg_ref[...] = jnp.zeros_like(dg_ref)
            db_ref[...] = jnp.zeros_like(db_ref)

        dh = jnp.zeros((tm, D_MODEL), F32)
        for (lo, hi), p_ref in zip(_PARTS, (du_ref, dq_ref, dk_ref, dv_ref, dgt_ref)):
            part = p_ref[...]
            dh = dh + _dot_nt(part, w_ref[:, lo:hi])
            db_ref[:, lo:hi] += jnp.sum(part.astype(F32), axis=0, keepdims=True)
        dx, dg = _norm_bwd(x_ref[...], g_ref[...], dh)
        gx_ref[...] = dx1_ref[...] + dx
        dg_ref[...] += dg

    return _call(
        body, name="inproj_bwd", grid=(T // tm,),
        in_specs=[_rows(tm, hi - lo) for lo, hi in _PARTS]
        + [_rows(tm, D_MODEL), _rows(tm, D_MODEL), _const((D_MODEL, IN_WIDTH)), _const((1, D_MODEL))],
        out_specs=[_rows(tm, D_MODEL), _const((1, D_MODEL)), _const((1, IN_WIDTH))],
        out_shape=[_sds((T, D_MODEL), F32), _sds((1, D_MODEL), F32), _sds((1, IN_WIDTH), F32)],
        args=(*parts, x2, dx1, w_in, g1), sem=("arbitrary",), exchanges=exchanges)


def _dw_in(h, parts, exchanges=()):
    T = h.shape[0]
    tk = min(TM, T)

    def body(h_ref, du_ref, dq_ref, dk_ref, dv_ref, dgt_ref, o_ref):
        @pl.when(pl.program_id(0) == 0)
        def _():
            o_ref[...] = jnp.zeros_like(o_ref)

        hh = h_ref[...]
        for (lo, hi), p_ref in zip(_PARTS, (du_ref, dq_ref, dk_ref, dv_ref, dgt_ref)):
            o_ref[:, lo:hi] += _dot_tn(hh, p_ref[...])

    return _call(
        body, name="dw_in", grid=(T // tk,),
        in_specs=[_rows(tk, D_MODEL)] + [_rows(tk, hi - lo) for lo, hi in _PARTS],
        out_specs=[_const((D_MODEL, IN_WIDTH))],
        out_shape=[_sds((D_MODEL, IN_WIDTH), F32)],
        args=(h, *parts), sem=("arbitrary",), exchanges=exchanges)


def _pair_sum(ids, full, got):
    _, r, c = full.shape
    hr = r // 2
    tr = min(256, hr)
    nblk = hr // tr

    def body(ids_ref, a_ref, b_ref, s_ref, sb_ref):
        s = a_ref[...] + b_ref[...]
        s_ref[...] = s
        sb_ref[...] = s.astype(BF16)

    out_spec = pl.BlockSpec((None, tr, c), lambda j, i, ids_ref: (j, i, 0))
    return pl.pallas_call(
        body, name="pair_sum_%dx%d" % (r, c),
        grid_spec=pltpu.PrefetchScalarGridSpec(
            num_scalar_prefetch=1, grid=(N_CHIPS, nblk),
            in_specs=[pl.BlockSpec((None, tr, c), lambda j, i, ids_ref: (j, ids_ref[1] * nblk + i, 0)), out_spec],
            out_specs=[out_spec, out_spec]),
        out_shape=[_sds((N_CHIPS, hr, c), F32), _sds((N_CHIPS, hr, c), BF16)],
        compiler_params=_cp("parallel", "parallel"),
    )(ids, full, got)


def _chip_sum(ids, own, got):
    _, hr, c = own.shape
    tr = min(256, hr)
    nblk = hr // tr

    def body(ids_ref, a_ref, b_ref, o_ref):
        o_ref[...] = ((a_ref[...] + b_ref[0].astype(F32)) + b_ref[1].astype(F32)) + b_ref[2].astype(F32)

    return pl.pallas_call(
        body, name="chip_sum_%dx%d" % (hr, c),
        grid_spec=pltpu.PrefetchScalarGridSpec(
            num_scalar_prefetch=1, grid=(nblk,),
            in_specs=[pl.BlockSpec((None, tr, c), lambda i, ids_ref: (ids_ref[0], i, 0)),
                      pl.BlockSpec((3, tr, c), lambda i, ids_ref: (0, i, 0))],
            out_specs=pl.BlockSpec((tr, c), lambda i, ids_ref: (ids_ref[1] * nblk + i, 0))),
        out_shape=_sds((2 * hr, c), F32),
        compiler_params=_cp("parallel"),
    )(ids, own, got)


def _sum_blocks(allb, m_per):
    def body(a_ref, o_ref):
        acc = a_ref[0:m_per, :]
        for d in range(1, N_DEV):
            acc = acc + a_ref[d * m_per:(d + 1) * m_per, :]
        o_ref[...] = acc

    return pl.pallas_call(body, name="sum_blocks", out_shape=_sds((m_per, allb.shape[1]), F32))(allb)


def _adamw(w, g, m, v):
    r, c = w.shape
    tr = r
    if r > 1024:
        raise ValueError("row tiling expects at most 1024 rows")
    if r % 256 == 0 and r > 256:
        tr = 256

    def body(w_ref, g_ref, m_ref, v_ref, d_ref, nm_ref, nv_ref):
        gg = g_ref[...]
        nm = ADAM_B1 * m_ref[...] + (1.0 - ADAM_B1) * gg
        nv = ADAM_B2 * v_ref[...] + (1.0 - ADAM_B2) * jnp.square(gg)
        m_hat = nm / (1.0 - ADAM_B1 ** ADAM_STEP)
        v_hat = nv / (1.0 - ADAM_B2 ** ADAM_STEP)
        d_ref[...] = -ADAM_LR * (m_hat / (jnp.sqrt(v_hat) + ADAM_EPS) + ADAM_WD * w_ref[...])
        nm_ref[...] = nm
        nv_ref[...] = nv

    spec = _rows(tr, c)
    return pl.pallas_call(
        body, name="adamw_%dx%d" % (r, c), grid=(r // tr,),
        in_specs=[spec] * 4, out_specs=[spec] * 3, out_shape=[_sds((r, c), F32)] * 3,
        compiler_params=_cp("parallel"),
    )(w, g, m, v)


_SMALL = (("w_pool", 4 * POOL_GC * POOL_GC), ("b_in", IN_WIDTH), ("g_mix_pre", D_MODEL), ("g_mix_post", D_MODEL),
          ("g_mlp_pre", D_MODEL), ("g_mlp_post", D_MODEL), ("pool_scale", POOL_WIDTH), ("attn_sinks", N_Q_HEADS),
          ("loss", 1))


def _pack_small(vals):
    parts = []
    for name, size in _SMALL:
        flat = vals[name].reshape(-1).astype(F32)
        padded = -(-size // (8 * LANES)) * (8 * LANES)
        parts.append(jnp.pad(flat, (0, padded - size)).reshape(-1, LANES))
    return jnp.concatenate(parts, axis=0)


def _unpack_small(packed, shapes):
    out, row = {}, 0
    for name, size in _SMALL:
        nrows = -(-size // (8 * LANES)) * 8
        out[name] = packed[row:row + nrows].reshape(-1)[:size].reshape(shapes[name])
        row += nrows
    return out


_BIG = ("w_in", "w_branch_pool", "w_branch_attn", "w_out", "w_up", "w_down")
_COLS = ("w_in", "w_branch_pool", "w_branch_attn", "w_up")
_ORDER = ("g_mix_pre", "w_in", "b_in", "w_pool", "pool_scale", "attn_sinks", "w_branch_pool", "w_branch_attn",
          "w_out", "g_mix_post", "g_mlp_pre", "w_up", "w_down", "g_mlp_post")


def _cols_to_full(g):
    return jnp.transpose(g, (1, 0, 2)).reshape(g.shape[1], N_CHIPS * g.shape[2])


def _full_to_cols(a):
    r, n = a.shape
    return jnp.transpose(a.reshape(r, N_CHIPS, n // N_CHIPS), (1, 0, 2))


def _whole(name, slab):
    return _cols_to_full(slab) if name in _COLS else slab.reshape(-1, slab.shape[2])


def _step(x2, tgt, seq, shards, small, ids):
    tabs = _rope_tables(seq)
    g1, g2, g3, g4 = (small[n] for n in ("g_mix_pre", "g_mix_post", "g_mlp_pre", "g_mlp_post"))
    sinks = small["attn_sinks"].reshape(N_Q_HEADS)
    w_pool = small["w_pool"].reshape(4, POOL_GC, POOL_GC)
    pool_scale = small["pool_scale"]

    w_in = _whole("w_in", _alone("gather_in", _ex_gather([shards["w_in"]]))[0])
    (h, u, q, k, v, gate), [mixw] = _inproj(
        x2, g1, w_in, small["b_in"], tabs, seq,
        exchanges=[_ex_gather([shards[n] for n in ("w_branch_pool", "w_branch_attn", "w_out")])])
    w_bp, w_ba, w_out = (_whole(n, s) for n, s in zip(("w_branch_pool", "w_branch_attn", "w_out"), mixw))
    diff, y_pool = _pool_fwd(u, w_pool, pool_scale, seq)
    (y_attn,), [[up_slab]] = _attn_fwd(q, k, v, sinks, seq, exchanges=[_ex_gather([shards["w_up"]])])
    w_up = _whole("w_up", up_slab)
    (bp, ba, merged, mix, x1, h2), [[down_slab]] = _merge_out(
        y_pool, y_attn, gate, x2, w_bp, w_ba, w_out, g2, g3, exchanges=[_ex_gather([shards["w_down"]])])
    w_down = _whole("w_down", down_slab)
    up, act = _mlp_up(h2, w_up)
    dff, dy, loss_acc, dg4 = _mlp_down_loss(act, x1, tgt, w_down, g4)

    dup = _mlp_down_bwd(dff, up, w_down)
    dw_down = _dw("down", act, dff, 1024, 1024)[0].reshape(N_CHIPS, D_FF // N_CHIPS, D_MODEL)
    (dx1, dmix, dg3, dg2), [[got]] = _mlp_up_bwd(dup, dy, x1, mix, w_up, g3, g2, exchanges=[_ex_pair([dw_down])])
    ps_down = _pair_sum(ids, dw_down, got)
    (dw_up,), [[got]] = _dw("up", h2, dup, 1024, 1024, shard_cols=True, exchanges=[_ex_chip([ps_down[1]])])
    half_down = _chip_sum(ids, ps_down[0], got)
    (dbp, dba, dgate, dyp, dya), [[got], [g_down]] = _merge_bwd(
        dmix, gate, bp, ba, w_out, w_bp, w_ba, exchanges=[_ex_pair([dw_up]), _ex_swap([half_down])])
    ps_up = _pair_sum(ids, dw_up, got)
    dw_mix = [_dw("out", merged, dmix, 1024, 1024)[0].reshape(N_CHIPS, D_MODEL // N_CHIPS, D_MODEL),
              _dw_slabs("branch_pool", y_pool, dbp), _dw_slabs("branch_attn", y_attn, dba)]
    (dq, dk, dv, dsink), [[got], gots] = _attn_bwd(
        q, k, v, dya, sinks, tabs, seq, exchanges=[_ex_chip([ps_up[1]]), _ex_pair(dw_mix)])
    half_up = _chip_sum(ids, ps_up[0], got)
    ps_mix = [_pair_sum(ids, d, g) for d, g in zip(dw_mix, gots)]
    (du, dw_pool, dps), [[g_up]] = _pool_bwd(dyp, diff, w_pool, pool_scale, seq, exchanges=[_ex_swap([half_up])])
    parts = (du, dq, dk, dv, dgate)
    (gx, dg1, db_in), [gots] = _inproj_bwd(parts, x2, dx1, w_in, g1, exchanges=[_ex_chip([p[1] for p in ps_mix])])
    half_mix = [_chip_sum(ids, p[0], g) for p, g in zip(ps_mix, gots)]

    little = dict(w_pool=dw_pool, b_in=db_in, g_mix_pre=dg1, g_mix_post=dg2, g_mlp_pre=dg3, g_mlp_post=dg4,
                  pool_scale=dps, attn_sinks=dsink[:, 0], loss=loss_acc[0, 0])
    block = _pack_small(little)
    (dw_in_full,), [g_mix, [gathered]] = _dw_in(h, parts, exchanges=[_ex_swap(half_mix), _ex_allgather(block)])
    total = _sum_blocks(gathered, block.shape[0])

    dw_in = _full_to_cols(dw_in_full)
    got = _alone("pair_in", _ex_pair([dw_in]))[0]
    ps_in = _pair_sum(ids, dw_in, got)
    got = _alone("chip_in", _ex_chip([ps_in[1]]))[0]
    g_in = _alone("swap_in", _ex_swap([_chip_sum(ids, ps_in[0], got)]))[0]

    grads = dict(w_in=g_in, w_branch_pool=g_mix[1], w_branch_attn=g_mix[2], w_out=g_mix[0], w_up=g_up, w_down=g_down)
    return total, gx, grads


def kernel(x, g_mix_pre, w_in, b_in, w_pool, pool_scale, attn_sinks, w_branch_pool, w_branch_attn, w_out, g_mix_post, g_mlp_pre, w_up, w_down, g_mlp_post, loss_target, m_g_mix_pre, m_w_in, m_b_in, m_w_pool, m_pool_scale, m_attn_sinks, m_w_branch_pool, m_w_branch_attn, m_w_out, m_g_mix_post, m_g_mlp_pre, m_w_up, m_w_down, m_g_mlp_post, v_g_mix_pre, v_w_in, v_b_in, v_w_pool, v_pool_scale, v_attn_sinks, v_w_branch_pool, v_w_branch_attn, v_w_out, v_g_mix_post, v_g_mlp_pre, v_w_up, v_w_down, v_g_mlp_post):
    weights = dict(g_mix_pre=g_mix_pre, w_in=w_in, b_in=b_in, w_pool=w_pool, pool_scale=pool_scale,
                   attn_sinks=attn_sinks, w_branch_pool=w_branch_pool, w_branch_attn=w_branch_attn, w_out=w_out,
                   g_mix_post=g_mix_post, g_mlp_pre=g_mlp_pre, w_up=w_up, w_down=w_down, g_mlp_post=g_mlp_post)
    mom1 = dict(g_mix_pre=m_g_mix_pre, w_in=m_w_in, b_in=m_b_in, w_pool=m_w_pool, pool_scale=m_pool_scale,
                attn_sinks=m_attn_sinks, w_branch_pool=m_w_branch_pool, w_branch_attn=m_w_branch_attn,
                w_out=m_w_out, g_mix_post=m_g_mix_post, g_mlp_pre=m_g_mlp_pre, w_up=m_w_up, w_down=m_w_down,
                g_mlp_post=m_g_mlp_post)
    mom2 = dict(g_mix_pre=v_g_mix_pre, w_in=v_w_in, b_in=v_b_in, w_pool=v_w_pool, pool_scale=v_pool_scale,
                attn_sinks=v_attn_sinks, w_branch_pool=v_w_branch_pool, w_branch_attn=v_w_branch_attn,
                w_out=v_w_out, g_mix_post=v_g_mix_post, g_mlp_pre=v_g_mlp_pre, w_up=v_w_up, w_down=v_w_down,
                g_mlp_post=v_g_mlp_post)
    b_loc, seq, _ = x.shape
    x2 = x.reshape(b_loc * seq, D_MODEL)
    tgt = loss_target.reshape(b_loc * seq, D_MODEL)
    ids = jnp.stack([2 * lax.axis_index("x") + lax.axis_index("y"), lax.axis_index("c")]).astype(jnp.int32)

    shards = {n: weights[n][0].astype(BF16) for n in _BIG}
    small = {n: weights[n] for n in _ORDER if n not in _BIG}
    total, gx, grads = _step(x2, tgt, seq, shards, small, ids)

    small_shapes = {n: weights[n].shape for n, _ in _SMALL if n != "loss"}
    small_shapes["loss"] = ()
    total = _unpack_small(total, small_shapes)
    loss = total.pop("loss")
    grads.update(total)

    delta, new_m, new_v = {}, {}, {}
    for n in _BIG:
        shape = weights[n].shape
        d, nm, nv = _adamw(weights[n][0], grads[n], mom1[n][0], mom2[n][0])
        grads[n] = grads[n].reshape(shape)
        delta[n], new_m[n], new_v[n] = d.reshape(shape), nm.reshape(shape), nv.reshape(shape)
    packed = [_pack_small({**src, "loss": jnp.zeros((), F32)}) for src in (weights, grads, mom1, mom2)]
    for dst, res in zip((delta, new_m, new_v), _adamw(*packed)):
        dst.update({n: a for n, a in _unpack_small(res, small_shapes).items() if n != "loss"})

    return (loss, gx.reshape(x.shape), *[grads[n] for n in _ORDER], *[delta[n] for n in _ORDER],
            *[new_m[n] for n in _ORDER], *[new_v[n] for n in _ORDER])
```

```python
import jax
import jax.numpy as jnp
from jax import lax
from jax.experimental import pallas as pl
from jax.experimental.pallas import tpu as pltpu

F32 = jnp.float32
BF16 = jnp.bfloat16

D_MODEL = 1024
POOL_WINDOWS = (2, 4, 8, 16)
POOL_WIDTH = 512
POOL_GC = 128
HALO = 16
HEAD_DIM = 64
N_Q_HEADS = 8
ATTN_WIDTH = 512
KV_WIDTH = 128
BLOCK = 128
NEG_INF = -1e30
ROPE_THETA = 500000.0
ROT_DIM = 16
GATE_WIDTH = 2048
IN_WIDTH = 3328
D_FF = 4096
EPS = 1e-6
SCALE = HEAD_DIM ** -0.5
C_Q, C_K, C_V, C_G = 512, 1024, 1152, 1280

ADAM_LR, ADAM_B1, ADAM_B2, ADAM_EPS, ADAM_WD, ADAM_STEP = 0.001, 0.9, 0.999, 1e-08, 0.01, 10

N_CHIPS = 4
N_DEV = 8
LANES = 128
TM = 512
TP = 256
VMEM_MB = 56

MESH = pl.DeviceIdType.MESH
ANY = pl.BlockSpec(memory_space=pl.ANY)


def _cp(*sem, vmem=VMEM_MB):
    return pltpu.CompilerParams(dimension_semantics=sem, vmem_limit_bytes=vmem * 1024 * 1024)


def _rows(tile, cols):
    return pl.BlockSpec((tile, cols), lambda i: (i, 0))


def _const(shape):
    nd = len(shape)
    return pl.BlockSpec(shape, lambda i: (0,) * nd)


def _sds(shape, dtype):
    return jax.ShapeDtypeStruct(shape, dtype)


def _dot(a, b):
    return jnp.dot(a, b, preferred_element_type=F32)


def _dot_nt(a, b):
    return lax.dot_general(a, b, (((1,), (1,)), ((), ())), preferred_element_type=F32)


def _dot_tn(a, b):
    return lax.dot_general(a, b, (((0,), (0,)), ((), ())), preferred_element_type=F32)


def _rms(x):
    return lax.rsqrt(jnp.mean(x * x, axis=-1, keepdims=True) + EPS)


def _norm_bwd(x, g, dout):
    r = _rms(x)
    n = x * r
    dn = dout * g
    dx = r * (dn - n * jnp.mean(dn * n, axis=-1, keepdims=True))
    return dx, jnp.sum(dout * n, axis=0, keepdims=True)


def _rot_fwd(t, c, a, bt):
    return t * c + pltpu.roll(t, LANES - 8, 1) * a + pltpu.roll(t, 8, 1) * bt


def _rot_bwd(d, c, a, bt):
    return d * c + pltpu.roll(d * a, 8, 1) + pltpu.roll(d * bt, LANES - 8, 1)


def _rope_tables(seq):
    pos = jnp.arange(seq, dtype=F32)
    inv_freq = ROPE_THETA ** (-jnp.arange(0, ROT_DIM, 2, dtype=F32) / ROT_DIM)
    ang = pos[:, None] * inv_freq[None, :]
    cos, sin = jnp.cos(ang), jnp.sin(ang)
    ones = jnp.ones((seq, HEAD_DIM - ROT_DIM), F32)
    zeros8 = jnp.zeros((seq, 8), F32)
    zrest = jnp.zeros((seq, HEAD_DIM - ROT_DIM), F32)
    c = jnp.concatenate([cos, cos, ones], axis=1)
    a = jnp.concatenate([-sin, zeros8, zrest], axis=1)
    bt = jnp.concatenate([zeros8, sin, zrest], axis=1)
    return tuple(jnp.tile(t, (1, 2)) for t in (c, a, bt))


class _Exchange:
    def __init__(self, inputs, out_shapes, sems, start, finish, aliases=None):
        self.inputs, self.out_shapes, self.sems = list(inputs), list(out_shapes), list(sems)
        self.start, self.finish, self.aliases = start, finish, dict(aliases or {})


def _call(body, *, name, grid, in_specs, out_specs, out_shape, args, scratch=(), sem=(), exchanges=()):
    in_specs, out_specs, out_shape, scratch = list(in_specs), list(out_specs), list(out_shape), list(scratch)
    if not exchanges:
        return pl.pallas_call(body, name=name, grid=grid, in_specs=in_specs, out_specs=out_specs,
                              out_shape=out_shape, scratch_shapes=scratch, compiler_params=_cp(*sem))(*args)
    n_in, n_out, n_scr = len(in_specs), len(out_specs), len(scratch)
    x_in = [a for ex in exchanges for a in ex.inputs]
    x_out = [s for ex in exchanges for s in ex.out_shapes]
    x_sem = [s for ex in exchanges for s in ex.sems]
    aliases, i_off, o_off = {}, n_in, n_out
    for ex in exchanges:
        for i, o in ex.aliases.items():
            aliases[i_off + i] = o_off + o
        i_off += len(ex.inputs)
        o_off += len(ex.out_shapes)

    def split(flat):
        out, pos = [], 0
        for ex, n in zip(exchanges, flat[1]):
            out.append(flat[0][pos:pos + n])
            pos += n
        return out

    def carrier(*refs):
        pos = 0
        groups = []
        for n in (n_in, len(x_in), n_out, len(x_out), n_scr, len(x_sem)):
            groups.append(refs[pos:pos + n])
            pos += n
        ins, xin, outs, xout, scr, xsem = groups
        xin = split((xin, [len(ex.inputs) for ex in exchanges]))
        xout = split((xout, [len(ex.out_shapes) for ex in exchanges]))
        xsem = split((xsem, [len(ex.sems) for ex in exchanges]))
        first = pl.program_id(0) == 0
        last = pl.program_id(0) == grid[0] - 1
        for d in range(1, len(grid)):
            first = jnp.logical_and(first, pl.program_id(d) == 0)
            last = jnp.logical_and(last, pl.program_id(d) == grid[d] - 1)

        @pl.when(first)
        def _():
            for ex, i, o, s in zip(exchanges, xin, xout, xsem):
                ex.start(i, o, s)

        body(*ins, *outs, *scr)

        @pl.when(last)
        def _():
            for ex, i, o, s in zip(exchanges, xin, xout, xsem):
                ex.finish(i, o, s)

    res = pl.pallas_call(
        carrier, name=name, grid=grid, in_specs=in_specs + [ANY] * len(x_in),
        out_specs=out_specs + [ANY] * len(x_out), out_shape=out_shape + x_out,
        scratch_shapes=scratch + x_sem, input_output_aliases=aliases,
        compiler_params=_cp(*(["arbitrary"] * len(grid))),
    )(*args, *x_in)
    return res[:n_out], split((res[n_out:], [len(ex.out_shapes) for ex in exchanges]))


def _alone(name, *exchanges):
    n_in = [len(ex.inputs) for ex in exchanges]
    n_out = [len(ex.out_shapes) for ex in exchanges]
    n_sem = [len(ex.sems) for ex in exchanges]
    aliases, i_off, o_off = {}, 0, 0
    for ex in exchanges:
        for i, o in ex.aliases.items():
            aliases[i_off + i] = o_off + o
        i_off += len(ex.inputs)
        o_off += len(ex.out_shapes)

    def split(flat, counts):
        out, pos = [], 0
        for n in counts:
            out.append(flat[pos:pos + n])
            pos += n
        return out

    def body(*refs):
        ins, outs, sems = split(refs, [sum(n_in), sum(n_out), sum(n_sem)])
        groups = list(zip(exchanges, split(ins, n_in), split(outs, n_out), split(sems, n_sem)))
        for ex, i, o, s in groups:
            ex.start(i, o, s)
        for ex, i, o, s in groups:
            ex.finish(i, o, s)

    res = pl.pallas_call(
        body, name=name, in_specs=[ANY] * sum(n_in), out_specs=[ANY] * sum(n_out),
        out_shape=[s for ex in exchanges for s in ex.out_shapes],
        scratch_shapes=[s for ex in exchanges for s in ex.sems], input_output_aliases=aliases,
    )(*[a for ex in exchanges for a in ex.inputs])
    return split(res, n_out)


def _place():
    x, y, c = lax.axis_index("x"), lax.axis_index("y"), lax.axis_index("c")
    chips = [(1 - x, y), (x, 1 - y), (1 - x, 1 - y)]
    return x, y, c, chips


def _remote(src, dst, send, recv, to):
    return pltpu.make_async_remote_copy(src_ref=src, dst_ref=dst, send_sem=send, recv_sem=recv,
                                        device_id=to, device_id_type=MESH)


def _ex_gather(shards):
    nw = len(shards)
    hrs = [s.shape[0] // 2 for s in shards]

    def copies(ins, outs, sems):
        send, recv, fsend, frecv, lsem = sems
        x, y, c, chips = _place()
        me = 2 * x + y
        sibling = (x, y, 1 - c)

        def piece(w, chip_idx, core):
            return outs[w].at[chip_idx, pl.ds(core * hrs[w], hrs[w])]

        local = [pltpu.make_async_copy(ins[w], outs[w].at[me], lsem.at[w]) for w in range(nw)]
        first = [[_remote(ins[w].at[pl.ds(c * hrs[w], hrs[w])], piece(w, me, c), send.at[w, k], recv.at[w, k],
                          (cx, cy, c)) for k, (cx, cy) in enumerate(chips)] for w in range(nw)]
        landed = [[_remote(piece(w, 2 * cx + cy, c), piece(w, 2 * cx + cy, c), send.at[w, k], recv.at[w, k],
                           (cx, cy, c)) for k, (cx, cy) in enumerate(chips)] for w in range(nw)]
        passed = [[_remote(piece(w, 2 * cx + cy, c), piece(w, 2 * cx + cy, c), fsend.at[w, k], frecv.at[w, k],
                           sibling) for k, (cx, cy) in enumerate(chips)] for w in range(nw)]
        handed = [[_remote(piece(w, 2 * cx + cy, 1 - c), piece(w, 2 * cx + cy, 1 - c), fsend.at[w, k],
                           frecv.at[w, k], sibling) for k, (cx, cy) in enumerate(chips)] for w in range(nw)]
        return local, first, landed, passed, handed

    def start(ins, outs, sems):
        local, first, _, _, _ = copies(ins, outs, sems)
        for w in range(nw):
            local[w].start()
            for cp in first[w]:
                cp.start()

    def finish(ins, outs, sems):
        local, first, landed, passed, handed = copies(ins, outs, sems)
        for w in range(nw):
            for k in range(3):
                landed[w][k].wait_recv()
                passed[w][k].start()
        for w in range(nw):
            for k in range(3):
                handed[w][k].wait_recv()
        for w in range(nw):
            for k in range(3):
                first[w][k].wait_send()
                passed[w][k].wait_send()
            local[w].wait()

    return _Exchange(shards, [_sds((N_CHIPS,) + s.shape, s.dtype) for s in shards],
                     [pltpu.SemaphoreType.DMA((nw, 3))] * 4 + [pltpu.SemaphoreType.DMA((nw,))], start, finish)


def _ex_pair(grads):
    nw = len(grads)

    def copies(ins, outs, sems):
        x, y, c, _ = _place()
        out = []
        for w in range(nw):
            hr = grads[w].shape[1] // 2
            out.append(_remote(ins[w].at[:, pl.ds((1 - c) * hr, hr)], outs[w], sems[0].at[w], sems[1].at[w],
                               (x, y, 1 - c)))
        return out

    def start(ins, outs, sems):
        for cp in copies(ins, outs, sems):
            cp.start()

    def finish(ins, outs, sems):
        for cp in copies(ins, outs, sems):
            cp.wait()

    return _Exchange(grads, [_sds((N_CHIPS, g.shape[1] // 2, g.shape[2]), F32) for g in grads],
                     [pltpu.SemaphoreType.DMA((nw,))] * 2, start, finish)


def _ex_chip(pieces):
    nw = len(pieces)

    def copies(ins, outs, sems):
        x, y, c, chips = _place()
        return [_remote(ins[w].at[2 * cx + cy], outs[w].at[k], sems[0].at[w, k], sems[1].at[w, k], (cx, cy, c))
                for w in range(nw) for k, (cx, cy) in enumerate(chips)]

    def start(ins, outs, sems):
        for cp in copies(ins, outs, sems):
            cp.start()

    def finish(ins, outs, sems):
        for cp in copies(ins, outs, sems):
            cp.wait()

    return _Exchange(pieces, [_sds((3,) + p.shape[1:], BF16) for p in pieces],
                     [pltpu.SemaphoreType.DMA((nw, 3))] * 2, start, finish)


def _ex_swap(fulls):
    nw = len(fulls)

    def start(ins, outs, sems):
        x, y, c, _ = _place()
        for w in range(nw):
            hr = fulls[w].shape[0] // 2
            mine = pl.ds(c * hr, hr)
            _remote(ins[w].at[mine], outs[w].at[mine], sems[0].at[w], sems[1].at[w], (x, y, 1 - c)).start()

    def finish(ins, outs, sems):
        x, y, c, _ = _place()
        for w in range(nw):
            hr = fulls[w].shape[0] // 2
            mine, theirs = pl.ds(c * hr, hr), pl.ds((1 - c) * hr, hr)
            _remote(ins[w].at[mine], outs[w].at[mine], sems[0].at[w], sems[1].at[w], (x, y, 1 - c)).wait_send()
            _remote(ins[w].at[theirs], outs[w].at[theirs], sems[0].at[w], sems[1].at[w], (x, y, 1 - c)).wait_recv()

    return _Exchange(fulls, [_sds(f.shape, F32) for f in fulls], [pltpu.SemaphoreType.DMA((nw,))] * 2,
                     start, finish, aliases={w: w for w in range(nw)})


def _ex_allgather(block):
    m_per, n = block.shape

    def copies(ins, outs, sems):
        send, recv, lsem = sems
        x, y, c, chips = _place()
        me, sibling = (x, y, c), (x, y, 1 - c)

        def rows(px, py, pc):
            return outs[0].at[pl.ds((4 * px + 2 * py + pc) * m_per, m_per), :]

        def copy(k, blk, to, src=None):
            return _remote(rows(*blk) if src is None else src, rows(*blk), send.at[k], recv.at[k], to)

        mine = pltpu.make_async_copy(ins[0], rows(*me), lsem.at[0])
        first = [copy(0, me, sibling, src=ins[0])] + [copy(1 + j, me, (*chip, c), src=ins[0])
                                                     for j, chip in enumerate(chips)]
        passed = [copy(4 + j, (*chip, c), sibling) for j, chip in enumerate(chips)]
        landed = [copy(1 + j, (*chip, c), me) for j, chip in enumerate(chips)]
        handed = [copy(0, sibling, me)] + [copy(4 + j, (*chip, 1 - c), me) for j, chip in enumerate(chips)]
        return mine, first, passed, landed, handed

    def start(ins, outs, sems):
        mine, first, _, _, _ = copies(ins, outs, sems)
        mine.start()
        for cp in first:
            cp.start()

    def finish(ins, outs, sems):
        mine, first, passed, landed, handed = copies(ins, outs, sems)
        for j in range(3):
            landed[j].wait_recv()
            passed[j].start()
        for cp in handed:
            cp.wait_recv()
        for cp in first + passed:
            cp.wait_send()
        mine.wait()

    return _Exchange([block], [_sds((N_DEV * m_per, n), F32)],
                     [pltpu.SemaphoreType.DMA((7,)), pltpu.SemaphoreType.DMA((7,)), pltpu.SemaphoreType.DMA((1,))],
                     start, finish)


def _inproj(x2, g1, w_in_t, b_in, tabs, seq, exchanges=()):
    T = x2.shape[0]
    tm = min(TM, seq)
    nseq = seq // tm

    def body(x_ref, g_ref, w_ref, b_ref, c_ref, a_ref, bt_ref, h_ref, u_ref, q_ref, k_ref, v_ref, gate_ref):
        x = x_ref[...]
        h = (x * _rms(x) * g_ref[...]).astype(BF16)
        h_ref[...] = h

        def proj(lo, hi):
            return _dot_nt(h, w_ref[lo:hi, :]) + b_ref[:, lo:hi]

        c, a, bt = c_ref[...], a_ref[...], bt_ref[...]
        u_ref[...] = proj(0, C_Q)
        q = proj(C_Q, C_K)
        for p in range(4):
            sl = slice(LANES * p, LANES * (p + 1))
            q_ref[:, sl] = _rot_fwd(q[:, sl], c, a, bt).astype(BF16)
        kv = proj(C_K, C_G)
        k_ref[...] = _rot_fwd(kv[:, :KV_WIDTH], c, a, bt).astype(BF16)
        v_ref[...] = kv[:, KV_WIDTH:].astype(BF16)
        for j in range(2):
            lo = C_G + D_MODEL * j
            gate_ref[:, D_MODEL * j:D_MODEL * (j + 1)] = jax.nn.sigmoid(proj(lo, lo + D_MODEL)).astype(BF16)

    tab = pl.BlockSpec((tm, LANES), lambda i: (i % nseq, 0))
    return _call(
        body, name="inproj", grid=(T // tm,),
        in_specs=[_rows(tm, D_MODEL), _const((1, D_MODEL)), _const((IN_WIDTH, D_MODEL)), _const((1, IN_WIDTH)),
                  tab, tab, tab],
        out_specs=[_rows(tm, D_MODEL), _rows(tm, POOL_WIDTH), _rows(tm, ATTN_WIDTH), _rows(tm, KV_WIDTH),
                   _rows(tm, KV_WIDTH), _rows(tm, GATE_WIDTH)],
        out_shape=[_sds((T, D_MODEL), BF16), _sds((T, POOL_WIDTH), F32), _sds((T, ATTN_WIDTH), BF16),
                   _sds((T, KV_WIDTH), BF16), _sds((T, KV_WIDTH), BF16), _sds((T, GATE_WIDTH), BF16)],
        args=(x2, g1, w_in_t, b_in, *tabs), sem=("parallel",), exchanges=exchanges)


def _inv_count(pos, w):
    return 1.0 / jnp.minimum(pos + 1, w).astype(F32)


def _pool_fwd(u, w_pool, pool_scale, seq):
    T = u.shape[0]
    tp = min(TP, seq)
    nseq = seq // tp
    per = tp // HALO

    def body(u_ref, prev_ref, w_ref, s_ref, diff_ref, y_ref):
        i = pl.program_id(0)
        first = (i % nseq) == 0
        prev = jnp.where(first, 0.0, prev_ref[...])
        ext = jnp.concatenate([prev, u_ref[...]], axis=0)
        pos = (i % nseq) * tp + lax.broadcasted_iota(jnp.int32, (tp, 1), 0)
        for gi, w in enumerate(POOL_WINDOWS):
            sl = slice(POOL_GC * gi, POOL_GC * (gi + 1))
            xg = ext[:, sl]
            s = xg
            sh = 1
            while sh < w:
                s = s + pltpu.roll(s, sh, 0)
                sh *= 2
            pooled = s[HALO:] * _inv_count(pos, w)
            diff = (pooled - xg[HALO:]).astype(BF16)
            diff_ref[:, sl] = diff
            mixed = _dot(diff, w_ref[gi].astype(BF16))
            y_ref[:, sl] = (mixed * s_ref[:, sl]).astype(BF16)

    return _call(
        body, name="pool_fwd", grid=(T // tp,),
        in_specs=[_rows(tp, POOL_WIDTH),
                  pl.BlockSpec((HALO, POOL_WIDTH), lambda i: (jnp.maximum(i * per - 1, 0), 0)),
                  _const((4, POOL_GC, POOL_GC)), _const((1, POOL_WIDTH))],
        out_specs=[_rows(tp, POOL_WIDTH), _rows(tp, POOL_WIDTH)],
        out_shape=[_sds((T, POOL_WIDTH), BF16), _sds((T, POOL_WIDTH), BF16)],
        args=(u, u, w_pool, pool_scale), sem=("parallel",))


def _attn_masks(n):
    qi = lax.broadcasted_iota(jnp.int32, (BLOCK, 2 * BLOCK), 0)
    kj = lax.broadcasted_iota(jnp.int32, (BLOCK, 2 * BLOCK), 1)
    rel = qi + BLOCK - kj
    valid = (rel >= 0) & (rel < BLOCK) & (kj >= jnp.where(n > 0, 0, BLOCK))
    lo = lax.broadcasted_iota(jnp.int32, (BLOCK, LANES), 1) < HEAD_DIM
    return valid, lo


def _head_probs(qm, kk, valid, sink):
    s = _dot_nt(qm, kk) * SCALE
    s = jnp.where(valid, s, NEG_INF)
    m = jnp.maximum(jnp.max(s, axis=1, keepdims=True), sink)
    ex = jnp.exp(s - m)
    es = jnp.exp(sink - m)
    inv = 1.0 / (jnp.sum(ex, axis=1, keepdims=True) + es)
    return ex * inv, es * inv


def _attn_fwd(q, k, v, sinks, seq, exchanges=()):
    T = q.shape[0]
    nb = seq // BLOCK

    def body(sink_ref, q_ref, kp_ref, kc_ref, vp_ref, vc_ref, o_ref):
        n = pl.program_id(0) % nb
        kk = jnp.concatenate([kp_ref[...], kc_ref[...]], axis=0)
        vv = jnp.concatenate([vp_ref[...], vc_ref[...]], axis=0)
        valid, lo = _attn_masks(n)
        for p in range(4):
            h = p // 2
            keep = lo if h == 0 else jnp.logical_not(lo)
            qp = q_ref[:, LANES * p:LANES * (p + 1)].astype(F32)
            outs = []
            for e in range(2):
                t = qp if e == h else pltpu.roll(qp, HEAD_DIM, 1)
                qm = jnp.where(keep, t, 0.0).astype(BF16)
                pr, _ = _head_probs(qm, kk, valid, sink_ref[2 * p + e])
                o = _dot(pr.astype(BF16), vv)
                outs.append(o if e == h else pltpu.roll(o, HEAD_DIM, 1))
            o_ref[:, LANES * p:LANES * (p + 1)] = jnp.where(lo, outs[0], outs[1]).astype(BF16)

    cur = lambda i: (i, 0)
    prv = lambda i: (jnp.where(i % nb == 0, i, i - 1), 0)
    return _call(
        body, name="attn_fwd", grid=(T // BLOCK,),
        in_specs=[pl.BlockSpec(memory_space=pltpu.SMEM),
                  pl.BlockSpec((BLOCK, ATTN_WIDTH), cur),
                  pl.BlockSpec((BLOCK, KV_WIDTH), prv), pl.BlockSpec((BLOCK, KV_WIDTH), cur),
                  pl.BlockSpec((BLOCK, KV_WIDTH), prv), pl.BlockSpec((BLOCK, KV_WIDTH), cur)],
        out_specs=[pl.BlockSpec((BLOCK, ATTN_WIDTH), cur)],
        out_shape=[_sds((T, ATTN_WIDTH), BF16)],
        args=(sinks, q, k, k, v, v), sem=("parallel",), exchanges=exchanges)


def _merge_out(y_pool, y_attn, gate, x2, w_bp, w_ba, w_out, g2, g3, exchanges=()):
    T = x2.shape[0]
    tm = min(TM, T)

    def body(yp_ref, ya_ref, gate_ref, x_ref, wbp_ref, wba_ref, wo_ref, g2_ref, g3_ref,
             bp_ref, ba_ref, mg_ref, mix_ref, x1_ref, h2_ref):
        yp, ya = yp_ref[...], ya_ref[...]
        bp = jnp.concatenate([_dot(yp, wbp_ref[j]) for j in range(N_CHIPS)], axis=1)
        ba = jnp.concatenate([_dot(ya, wba_ref[j]) for j in range(N_CHIPS)], axis=1)
        bp_ref[...] = bp.astype(BF16)
        ba_ref[...] = ba.astype(BF16)
        merged = (gate_ref[:, :D_MODEL].astype(F32) * bp + gate_ref[:, D_MODEL:].astype(F32) * ba).astype(BF16)
        mg_ref[...] = merged
        mix = _dot(merged, wo_ref[...])
        mix_ref[...] = mix
        x1 = x_ref[...] + mix * _rms(mix) * g2_ref[...]
        x1_ref[...] = x1
        h2_ref[...] = (x1 * _rms(x1) * g3_ref[...]).astype(BF16)

    return _call(
        body, name="merge_out", grid=(T // tm,),
        in_specs=[_rows(tm, POOL_WIDTH), _rows(tm, ATTN_WIDTH), _rows(tm, GATE_WIDTH), _rows(tm, D_MODEL),
                  _const(w_bp.shape), _const(w_ba.shape), _const((D_MODEL, D_MODEL)),
                  _const((1, D_MODEL)), _const((1, D_MODEL))],
        out_specs=[_rows(tm, D_MODEL)] * 6,
        out_shape=[_sds((T, D_MODEL), BF16), _sds((T, D_MODEL), BF16), _sds((T, D_MODEL), BF16),
                   _sds((T, D_MODEL), F32), _sds((T, D_MODEL), F32), _sds((T, D_MODEL), BF16)],
        args=(y_pool, y_attn, gate, x2, w_bp, w_ba, w_out, g2, g3), sem=("parallel",), exchanges=exchanges)


def _mlp_up(h2, w_up):
    T = h2.shape[0]
    tm = min(TM, T)

    def body(h_ref, w_ref, up_ref, a_ref):
        h = h_ref[...]
        for j in range(N_CHIPS):
            sl = slice(D_MODEL * j, D_MODEL * (j + 1))
            up = _dot(h, w_ref[j])
            up_ref[:, sl] = up.astype(BF16)
            a_ref[:, sl] = jnp.square(jnp.maximum(up, 0.0)).astype(BF16)

    return _call(
        body, name="mlp_up", grid=(T // tm,),
        in_specs=[_rows(tm, D_MODEL), _const((N_CHIPS, D_MODEL, D_MODEL))],
        out_specs=[_rows(tm, D_FF), _rows(tm, D_FF)],
        out_shape=[_sds((T, D_FF), BF16), _sds((T, D_FF), BF16)],
        args=(h2, w_up), sem=("parallel",))


def _mlp_down_loss(a, x1, tgt, w_down, g4):
    T = a.shape[0]
    tm = min(TM, T)

    def body(a_ref, x1_ref, t_ref, w_ref, g_ref, dff_ref, dy_ref, loss_ref, dg_ref):
        @pl.when(pl.program_id(0) == 0)
        def _():
            loss_ref[...] = jnp.zeros_like(loss_ref)
            dg_ref[...] = jnp.zeros_like(dg_ref)

        ff = _dot(a_ref[...], w_ref[...])
        g = g_ref[...]
        err = x1_ref[...] + ff * _rms(ff) * g - t_ref[...]
        loss_ref[...] += jnp.sum(err * err) * (0.5 / D_MODEL)
        dy = err * (1.0 / D_MODEL)
        dy_ref[...] = dy
        dff, dg = _norm_bwd(ff, g, dy)
        dff_ref[...] = dff.astype(BF16)
        dg_ref[...] += dg

    return _call(
        body, name="mlp_down_loss", grid=(T // tm,),
        in_specs=[_rows(tm, D_FF), _rows(tm, D_MODEL), _rows(tm, D_MODEL), _const((D_FF, D_MODEL)),
                  _const((1, D_MODEL))],
        out_specs=[_rows(tm, D_MODEL), _rows(tm, D_MODEL), _const((8, LANES)), _const((1, D_MODEL))],
        out_shape=[_sds((T, D_MODEL), BF16), _sds((T, D_MODEL), F32), _sds((8, LANES), F32),
                   _sds((1, D_MODEL), F32)],
        args=(a, x1, tgt, w_down, g4), sem=("arbitrary",))


def _mlp_down_bwd(dff, up, w_down):
    T = dff.shape[0]
    tm = min(TM, T)

    def body(d_ref, up_ref, w_ref, dup_ref):
        d = d_ref[...]
        for j in range(D_FF // D_MODEL):
            sl = slice(D_MODEL * j, D_MODEL * (j + 1))
            da = _dot_nt(d, w_ref[sl, :])
            dup_ref[:, sl] = (da * (2.0 * jnp.maximum(up_ref[:, sl].astype(F32), 0.0))).astype(BF16)

    return _call(
        body, name="mlp_down_bwd", grid=(T // tm,),
        in_specs=[_rows(tm, D_MODEL), _rows(tm, D_FF), _const((D_FF, D_MODEL))],
        out_specs=[_rows(tm, D_FF)],
        out_shape=[_sds((T, D_FF), BF16)],
        args=(dff, up, w_down), sem=("parallel",))[0]


def _mlp_up_bwd(dup, dy, x1, mix, w_up, g3, g2, exchanges=()):
    T = dup.shape[0]
    tm = min(TM, T)

    def body(dup_ref, dy_ref, x1_ref, mix_ref, w_ref, g3_ref, g2_ref, dx1_ref, dmix_ref, dg3_ref, dg2_ref):
        @pl.when(pl.program_id(0) == 0)
        def _():
            dg3_ref[...] = jnp.zeros_like(dg3_ref)
            dg2_ref[...] = jnp.zeros_like(dg2_ref)

        dh2 = _dot_nt(dup_ref[:, :D_MODEL], w_ref[0])
        for j in range(1, N_CHIPS):
            dh2 = dh2 + _dot_nt(dup_ref[:, D_MODEL * j:D_MODEL * (j + 1)], w_ref[j])
        dx, dg3 = _norm_bwd(x1_ref[...], g3_ref[...], dh2)
        dx1 = dy_ref[...] + dx
        dx1_ref[...] = dx1
        dg3_ref[...] += dg3
        dmix, dg2 = _norm_bwd(mix_ref[...], g2_ref[...], dx1)
        dmix_ref[...] = dmix.astype(BF16)
        dg2_ref[...] += dg2

    return _call(
        body, name="mlp_up_bwd", grid=(T // tm,),
        in_specs=[_rows(tm, D_FF), _rows(tm, D_MODEL), _rows(tm, D_MODEL), _rows(tm, D_MODEL),
                  _const((N_CHIPS, D_MODEL, D_MODEL)), _const((1, D_MODEL)), _const((1, D_MODEL))],
        out_specs=[_rows(tm, D_MODEL), _rows(tm, D_MODEL), _const((1, D_MODEL)), _const((1, D_MODEL))],
        out_shape=[_sds((T, D_MODEL), F32), _sds((T, D_MODEL), BF16), _sds((1, D_MODEL), F32),
                   _sds((1, D_MODEL), F32)],
        args=(dup, dy, x1, mix, w_up, g3, g2), sem=("arbitrary",), exchanges=exchanges)


def _dw(tag, a, g, ta, tn, shard_cols=False, exchanges=()):
    T, ka = a.shape
    n = g.shape[1]
    tk = min(TM, T)
    nk = T // tk

    def body(a_ref, g_ref, o_ref):
        @pl.when(pl.program_id(2) == 0)
        def _():
            o_ref[...] = jnp.zeros_like(o_ref)

        o_ref[...] += _dot_tn(a_ref[...], g_ref[...])

    if shard_cols:
        per = (n // N_CHIPS) // tn
        out_spec = pl.BlockSpec((None, ta, tn), lambda i, j, k: (j // per, i, j % per))
        out_shape = _sds((N_CHIPS, ka, n // N_CHIPS), F32)
    else:
        out_spec = pl.BlockSpec((ta, tn), lambda i, j, k: (i, j))
        out_shape = _sds((ka, n), F32)
    return _call(
        body, name="dw_" + tag, grid=(ka // ta, n // tn, nk),
        in_specs=[pl.BlockSpec((tk, ta), lambda i, j, k: (k, i)), pl.BlockSpec((tk, tn), lambda i, j, k: (k, j))],
        out_specs=[out_spec], out_shape=[out_shape],
        args=(a, g), sem=("parallel", "parallel", "arbitrary"), exchanges=exchanges)


def _dw_slabs(tag, a, g):
    T, ka = a.shape
    n = g.shape[1]
    c = n // N_CHIPS
    tk = min(TM, T)

    def body(a_ref, g_ref, o_ref):
        @pl.when(pl.program_id(0) == 0)
        def _():
            o_ref[...] = jnp.zeros_like(o_ref)

        res = _dot_tn(a_ref[...], g_ref[...])
        for j in range(N_CHIPS):
            o_ref[j] += res[:, c * j:c * (j + 1)]

    return _call(
        body, name="dw_" + tag, grid=(T // tk,),
        in_specs=[_rows(tk, ka), _rows(tk, n)],
        out_specs=[_const((N_CHIPS, ka, c))], out_shape=[_sds((N_CHIPS, ka, c), F32)],
        args=(a, g), sem=("arbitrary",))[0]


def _merge_bwd(dmix, gate, bp, ba, w_out, w_bp, w_ba, exchanges=()):
    T = dmix.shape[0]
    tm = min(TM, T)

    def body(dmix_ref, gate_ref, bp_ref, ba_ref, wo_ref, wbp_ref, wba_ref,
             dbp_ref, dba_ref, dgate_ref, dyp_ref, dya_ref):
        dm = _dot_nt(dmix_ref[...], wo_ref[...])
        for j, (b_ref, db_ref, w_ref, dy_ref) in enumerate(
                ((bp_ref, dbp_ref, wbp_ref, dyp_ref), (ba_ref, dba_ref, wba_ref, dya_ref))):
            sl = slice(D_MODEL * j, D_MODEL * (j + 1))
            gt = gate_ref[:, sl].astype(F32)
            db = (dm * gt).astype(BF16)
            db_ref[...] = db
            dgate_ref[:, sl] = (dm * b_ref[...].astype(F32) * gt * (1.0 - gt)).astype(BF16)
            cw = D_MODEL // N_CHIPS
            dy = _dot_nt(db[:, :cw], w_ref[0])
            for c in range(1, N_CHIPS):
                dy = dy + _dot_nt(db[:, cw * c:cw * (c + 1)], w_ref[c])
            dy_ref[...] = dy.astype(dy_ref.dtype)

    return _call(
        body, name="merge_bwd", grid=(T // tm,),
        in_specs=[_rows(tm, D_MODEL), _rows(tm, GATE_WIDTH), _rows(tm, D_MODEL), _rows(tm, D_MODEL),
                  _const((D_MODEL, D_MODEL)), _const(w_bp.shape), _const(w_ba.shape)],
        out_specs=[_rows(tm, D_MODEL), _rows(tm, D_MODEL), _rows(tm, GATE_WIDTH), _rows(tm, POOL_WIDTH),
                   _rows(tm, ATTN_WIDTH)],
        out_shape=[_sds((T, D_MODEL), BF16), _sds((T, D_MODEL), BF16), _sds((T, GATE_WIDTH), BF16),
                   _sds((T, POOL_WIDTH), F32), _sds((T, ATTN_WIDTH), BF16)],
        args=(dmix, gate, bp, ba, w_out, w_bp, w_ba), sem=("parallel",), exchanges=exchanges)


def _attn_bwd(q, k, v, do, sinks, tabs, seq, exchanges=()):
    T = q.shape[0]
    nb = seq // BLOCK
    steps = nb + 1

    def body(sink_ref, q_ref, do_ref, kp_ref, kc_ref, vp_ref, vc_ref, c_ref, a_ref, bt_ref, cp_ref, ap_ref, btp_ref,
             dq_ref, dk_ref, dv_ref, dsink_ref, ck_ref, cv_ref):
        i = pl.program_id(0)
        n = i % steps

        @pl.when(i == 0)
        def _():
            dsink_ref[...] = jnp.zeros_like(dsink_ref)

        @pl.when(n == 0)
        def _():
            ck_ref[...] = jnp.zeros_like(ck_ref)
            cv_ref[...] = jnp.zeros_like(cv_ref)

        @pl.when(n < nb)
        def _():
            kk = jnp.concatenate([kp_ref[...], kc_ref[...]], axis=0)
            vv = jnp.concatenate([vp_ref[...], vc_ref[...]], axis=0)
            valid, lo = _attn_masks(n)
            dk_acc = jnp.zeros((2 * BLOCK, KV_WIDTH), F32)
            dv_acc = jnp.zeros((2 * BLOCK, KV_WIDTH), F32)
            for p in range(4):
                h = p // 2
                keep = lo if h == 0 else jnp.logical_not(lo)
                sl = slice(LANES * p, LANES * (p + 1))
                qp = q_ref[:, sl].astype(F32)
                dop = do_ref[:, sl].astype(F32)
                parts = []
                for e in range(2):
                    idx = 2 * p + e
                    same = e == h
                    qm = jnp.where(keep, qp if same else pltpu.roll(qp, HEAD_DIM, 1), 0.0).astype(BF16)
                    dom = jnp.where(keep, dop if same else pltpu.roll(dop, HEAD_DIM, 1), 0.0).astype(BF16)
                    sink = sink_ref[idx]
                    pr, ps = _head_probs(qm, kk, valid, sink)
                    dp = _dot_nt(dom, vv)
                    delta = jnp.sum(pr * dp, axis=1, keepdims=True)
                    ds = (pr * (dp - delta) * SCALE).astype(BF16)
                    dsink_ref[idx:idx + 1, :] += jnp.zeros((1, LANES), F32) - jnp.sum(ps * delta)
                    dq = jnp.where(keep, _dot(ds, kk), 0.0)
                    parts.append(dq if same else pltpu.roll(dq, HEAD_DIM, 1))
                    dk_acc = dk_acc + _dot_tn(ds, qm)
                    dv_acc = dv_acc + _dot_tn(pr.astype(BF16), dom)
                dq_pair = jnp.where(lo, parts[0], parts[1])
                dq_ref[:, sl] = _rot_bwd(dq_pair, c_ref[...], a_ref[...], bt_ref[...]).astype(BF16)
            fin_k = ck_ref[...] + dk_acc[:BLOCK]
            dk_ref[...] = _rot_bwd(fin_k, cp_ref[...], ap_ref[...], btp_ref[...]).astype(BF16)
            dv_ref[...] = (cv_ref[...] + dv_acc[:BLOCK]).astype(BF16)
            ck_ref[...] = dk_acc[BLOCK:]
            cv_ref[...] = dv_acc[BLOCK:]

        @pl.when(n == nb)
        def _():
            dk_ref[...] = _rot_bwd(ck_ref[...], cp_ref[...], ap_ref[...], btp_ref[...]).astype(BF16)
            dv_ref[...] = cv_ref[...].astype(BF16)

    def blk(i):
        return (i // steps) * nb

    cur = lambda i: (blk(i) + jnp.minimum(i % steps, nb - 1), 0)
    prv = lambda i: (blk(i) + jnp.clip(i % steps - 1, 0, nb - 1), 0)
    tcur = lambda i: (jnp.minimum(i % steps, nb - 1), 0)
    tprv = lambda i: (jnp.clip(i % steps - 1, 0, nb - 1), 0)
    kv = lambda m: pl.BlockSpec((BLOCK, KV_WIDTH), m)
    return _call(
        body, name="attn_bwd", grid=((T // seq) * steps,),
        in_specs=[pl.BlockSpec(memory_space=pltpu.SMEM),
                  pl.BlockSpec((BLOCK, ATTN_WIDTH), cur), pl.BlockSpec((BLOCK, ATTN_WIDTH), cur),
                  kv(prv), kv(cur), kv(prv), kv(cur),
                  kv(tcur), kv(tcur), kv(tcur), kv(tprv), kv(tprv), kv(tprv)],
        out_specs=[pl.BlockSpec((BLOCK, ATTN_WIDTH), cur), kv(prv), kv(prv), _const((8, LANES))],
        out_shape=[_sds((T, ATTN_WIDTH), BF16), _sds((T, KV_WIDTH), BF16), _sds((T, KV_WIDTH), BF16),
                   _sds((8, LANES), F32)],
        scratch=[pltpu.VMEM((BLOCK, KV_WIDTH), F32), pltpu.VMEM((BLOCK, KV_WIDTH), F32)],
        args=(sinks, q, do, k, k, v, v, *tabs, *tabs), sem=("arbitrary",), exchanges=exchanges)


def _pool_bwd(dyp, diff, w_pool, pool_scale, seq, exchanges=()):
    T = dyp.shape[0]
    tp = min(TP, seq)
    nseq = seq // tp
    per = tp // HALO
    last_halo = T // HALO - 1

    def body(dy_ref, nxt_ref, diff_ref, w_ref, s_ref, du_ref, dw_ref, ds_ref):
        i = pl.program_id(0)

        @pl.when(i == 0)
        def _():
            dw_ref[...] = jnp.zeros_like(dw_ref)
            ds_ref[...] = jnp.zeros_like(ds_ref)

        last = (i % nseq) == nseq - 1
        nxt = jnp.where(last, 0.0, nxt_ref[...])
        ext = jnp.concatenate([dy_ref[...], nxt], axis=0) * s_ref[...]
        pos = (i % nseq) * tp + lax.broadcasted_iota(jnp.int32, (tp + HALO, 1), 0)
        for gi, w in enumerate(POOL_WINDOWS):
            sl = slice(POOL_GC * gi, POOL_GC * (gi + 1))
            wg = w_ref[gi].astype(BF16)
            dmx = ext[:, sl].astype(BF16)
            ddiff = _dot_nt(dmx, wg)
            s = ddiff * _inv_count(pos, w)
            sh = 1
            while sh < w:
                s = s + pltpu.roll(s, tp + HALO - sh, 0)
                sh *= 2
            du_ref[:, sl] = (s[:tp] - ddiff[:tp]).astype(BF16)
            dg = diff_ref[:, sl]
            dw_ref[gi] += _dot_tn(dg, dmx[:tp])
            ds_ref[:, sl] += jnp.sum(dy_ref[:, sl] * _dot(dg, wg), axis=0, keepdims=True)

    return _call(
        body, name="pool_bwd", grid=(T // tp,),
        in_specs=[_rows(tp, POOL_WIDTH),
                  pl.BlockSpec((HALO, POOL_WIDTH), lambda i: (jnp.minimum((i + 1) * per, last_halo), 0)),
                  _rows(tp, POOL_WIDTH), _const((4, POOL_GC, POOL_GC)), _const((1, POOL_WIDTH))],
        out_specs=[_rows(tp, POOL_WIDTH), _const((4, POOL_GC, POOL_GC)), _const((1, POOL_WIDTH))],
        out_shape=[_sds((T, POOL_WIDTH), BF16), _sds((4, POOL_GC, POOL_GC), F32), _sds((1, POOL_WIDTH), F32)],
        args=(dyp, dyp, diff, w_pool, pool_scale), sem=("arbitrary",), exchanges=exchanges)


_PARTS = ((0, C_Q), (C_Q, C_K), (C_K, C_V), (C_V, C_G), (C_G, IN_WIDTH))


def _inproj_bwd(parts, x2, dx1, w_in_t, g1, exchanges=()):
    T = x2.shape[0]
    tm = min(TM, T)

    def body(du_ref, dq_ref, dk_ref, dv_ref, dgt_ref, x_ref, dx1_ref, w_ref, g_ref, gx_ref, dg_ref, db_ref):
        @pl.when(pl.program_id(0) == 0)
        def _():
            dg_ref[...] = jnp.zeros_like(dg_ref)
            db_ref[...] = jnp.zeros_like(db_ref)

        dh = jnp.zeros((tm, D_MODEL), F32)
        for (lo, hi), p_ref in zip(_PARTS, (du_ref, dq_ref, dk_ref, dv_ref, dgt_ref)):
            part = p_ref[...]
            dh = dh + _dot(part, w_ref[lo:hi, :])
            db_ref[:, lo:hi] += jnp.sum(part.astype(F32), axis=0, keepdims=True)
        dx, dg = _norm_bwd(x_ref[...], g_ref[...], dh)
        gx_ref[...] = dx1_ref[...] + dx
        dg_ref[...] += dg

    return _call(
        body, name="inproj_bwd", grid=(T // tm,),
        in_specs=[_rows(tm, hi - lo) for lo, hi in _PARTS]
        + [_rows(tm, D_MODEL), _rows(tm, D_MODEL), _const((IN_WIDTH, D_MODEL)), _const((1, D_MODEL))],
        out_specs=[_rows(tm, D_MODEL), _const((1, D_MODEL)), _const((1, IN_WIDTH))],
        out_shape=[_sds((T, D_MODEL), F32), _sds((1, D_MODEL), F32), _sds((1, IN_WIDTH), F32)],
        args=(*parts, x2, dx1, w_in_t, g1), sem=("arbitrary",), exchanges=exchanges)


def _dw_in(h, parts, exchanges=()):
    T = h.shape[0]
    tk = min(TM, T)

    def body(h_ref, du_ref, dq_ref, dk_ref, dv_ref, dgt_ref, o_ref):
        @pl.when(pl.program_id(0) == 0)
        def _():
            o_ref[...] = jnp.zeros_like(o_ref)

        hh = h_ref[...]
        for (lo, hi), p_ref in zip(_PARTS, (du_ref, dq_ref, dk_ref, dv_ref, dgt_ref)):
            o_ref[lo:hi, :] += _dot_tn(p_ref[...], hh)

    return _call(
        body, name="dw_in", grid=(T // tk,),
        in_specs=[_rows(tk, D_MODEL)] + [_rows(tk, hi - lo) for lo, hi in _PARTS],
        out_specs=[_const((IN_WIDTH, D_MODEL))],
        out_shape=[_sds((IN_WIDTH, D_MODEL), F32)],
        args=(h, *parts), sem=("arbitrary",), exchanges=exchanges)


def _row_tile(rows, cap=256, mult=16):
    best = None
    for t in range(mult, min(rows, cap) + 1, mult):
        if rows % t == 0:
            best = t
    if best is None:
        raise ValueError("no row tile for %d rows" % rows)
    return best


def _pair_sum(ids, full, got):
    _, r, c = full.shape
    hr = r // 2
    tr = _row_tile(hr)
    nblk = hr // tr

    def body(ids_ref, a_ref, b_ref, s_ref, sb_ref):
        s = a_ref[...] + b_ref[...]
        s_ref[...] = s
        sb_ref[...] = s.astype(BF16)

    out_spec = pl.BlockSpec((None, tr, c), lambda j, i, ids_ref: (j, i, 0))
    return pl.pallas_call(
        body, name="pair_sum_%dx%d" % (r, c),
        grid_spec=pltpu.PrefetchScalarGridSpec(
            num_scalar_prefetch=1, grid=(N_CHIPS, nblk),
            in_specs=[pl.BlockSpec((None, tr, c), lambda j, i, ids_ref: (j, ids_ref[1] * nblk + i, 0)), out_spec],
            out_specs=[out_spec, out_spec]),
        out_shape=[_sds((N_CHIPS, hr, c), F32), _sds((N_CHIPS, hr, c), BF16)],
        compiler_params=_cp("parallel", "parallel"),
    )(ids, full, got)


def _chip_sum(ids, own, got):
    _, hr, c = own.shape
    tr = _row_tile(hr)
    nblk = hr // tr

    def body(ids_ref, a_ref, b_ref, o_ref):
        o_ref[...] = ((a_ref[...] + b_ref[0].astype(F32)) + b_ref[1].astype(F32)) + b_ref[2].astype(F32)

    return pl.pallas_call(
        body, name="chip_sum_%dx%d" % (hr, c),
        grid_spec=pltpu.PrefetchScalarGridSpec(
            num_scalar_prefetch=1, grid=(nblk,),
            in_specs=[pl.BlockSpec((None, tr, c), lambda i, ids_ref: (ids_ref[0], i, 0)),
                      pl.BlockSpec((3, tr, c), lambda i, ids_ref: (0, i, 0))],
            out_specs=pl.BlockSpec((tr, c), lambda i, ids_ref: (ids_ref[1] * nblk + i, 0))),
        out_shape=_sds((2 * hr, c), F32),
        compiler_params=_cp("parallel"),
    )(ids, own, got)


def _sum_blocks(allb, m_per):
    def body(a_ref, o_ref):
        acc = a_ref[0:m_per, :]
        for d in range(1, N_DEV):
            acc = acc + a_ref[d * m_per:(d + 1) * m_per, :]
        o_ref[...] = acc

    return pl.pallas_call(body, name="sum_blocks", out_shape=_sds((m_per, allb.shape[1]), F32))(allb)


def _adamw(w, g, m, v):
    r, c = w.shape
    tr = _row_tile(r, mult=8)

    def body(w_ref, g_ref, m_ref, v_ref, d_ref, nm_ref, nv_ref):
        gg = g_ref[...]
        nm = ADAM_B1 * m_ref[...] + (1.0 - ADAM_B1) * gg
        nv = ADAM_B2 * v_ref[...] + (1.0 - ADAM_B2) * jnp.square(gg)
        m_hat = nm / (1.0 - ADAM_B1 ** ADAM_STEP)
        v_hat = nv / (1.0 - ADAM_B2 ** ADAM_STEP)
        d_ref[...] = -ADAM_LR * (m_hat / (jnp.sqrt(v_hat) + ADAM_EPS) + ADAM_WD * w_ref[...])
        nm_ref[...] = nm
        nv_ref[...] = nv

    spec = _rows(tr, c)
    return pl.pallas_call(
        body, name="adamw_%dx%d" % (r, c), grid=(r // tr,),
        in_specs=[spec] * 4, out_specs=[spec] * 3, out_shape=[_sds((r, c), F32)] * 3,
        compiler_params=_cp("parallel"),
    )(w, g, m, v)


_SMALL = (("w_pool", 4 * POOL_GC * POOL_GC), ("b_in", IN_WIDTH), ("g_mix_pre", D_MODEL), ("g_mix_post", D_MODEL),
          ("g_mlp_pre", D_MODEL), ("g_mlp_post", D_MODEL), ("pool_scale", POOL_WIDTH), ("attn_sinks", N_Q_HEADS),
          ("loss", 1))


def _pack_small(vals):
    parts = []
    for name, size in _SMALL:
        flat = vals[name].reshape(-1).astype(F32)
        padded = -(-size // (8 * LANES)) * (8 * LANES)
        parts.append(jnp.pad(flat, (0, padded - size)).reshape(-1, LANES))
    return jnp.concatenate(parts, axis=0)


def _unpack_small(packed, shapes):
    out, row = {}, 0
    for name, size in _SMALL:
        nrows = -(-size // (8 * LANES)) * 8
        out[name] = packed[row:row + nrows].reshape(-1)[:size].reshape(shapes[name])
        row += nrows
    return out


_BIG = ("w_in", "w_branch_pool", "w_branch_attn", "w_out", "w_up", "w_down")
_ORDER = ("g_mix_pre", "w_in", "b_in", "w_pool", "pool_scale", "attn_sinks", "w_branch_pool", "w_branch_attn",
          "w_out", "g_mix_post", "g_mlp_pre", "w_up", "w_down", "g_mlp_post")


def _stack_rows(slab):
    return slab.reshape(-1, slab.shape[2])


def _step(x2, tgt, seq, shards, small, ids):
    tabs = _rope_tables(seq)
    g1, g2, g3, g4 = (small[n] for n in ("g_mix_pre", "g_mix_post", "g_mlp_pre", "g_mlp_post"))
    sinks = small["attn_sinks"].reshape(N_Q_HEADS)
    w_pool = small["w_pool"].reshape(4, POOL_GC, POOL_GC)
    pool_scale = small["pool_scale"]

    w_in = _stack_rows(_alone("gather_in", _ex_gather([shards["w_in"]]))[0][0])
    (h, u, q, k, v, gate), [(w_bp, w_ba, out_slab)] = _inproj(
        x2, g1, w_in, small["b_in"], tabs, seq,
        exchanges=[_ex_gather([shards[n] for n in ("w_branch_pool", "w_branch_attn", "w_out")])])
    w_out = _stack_rows(out_slab)
    diff, y_pool = _pool_fwd(u, w_pool, pool_scale, seq)
    (y_attn,), [[w_up]] = _attn_fwd(q, k, v, sinks, seq, exchanges=[_ex_gather([shards["w_up"]])])
    (bp, ba, merged, mix, x1, h2), [[down_slab]] = _merge_out(
        y_pool, y_attn, gate, x2, w_bp, w_ba, w_out, g2, g3, exchanges=[_ex_gather([shards["w_down"]])])
    w_down = _stack_rows(down_slab)
    up, act = _mlp_up(h2, w_up)
    dff, dy, loss_acc, dg4 = _mlp_down_loss(act, x1, tgt, w_down, g4)

    dup = _mlp_down_bwd(dff, up, w_down)
    dw_down = _dw("down", act, dff, 1024, 1024)[0].reshape(N_CHIPS, D_FF // N_CHIPS, D_MODEL)
    (dx1, dmix, dg3, dg2), [[got]] = _mlp_up_bwd(dup, dy, x1, mix, w_up, g3, g2, exchanges=[_ex_pair([dw_down])])
    ps_down = _pair_sum(ids, dw_down, got)
    (dw_up,), [[got]] = _dw("up", h2, dup, 1024, 1024, shard_cols=True, exchanges=[_ex_chip([ps_down[1]])])
    half_down = _chip_sum(ids, ps_down[0], got)
    (dbp, dba, dgate, dyp, dya), [[got], [g_down]] = _merge_bwd(
        dmix, gate, bp, ba, w_out, w_bp, w_ba, exchanges=[_ex_pair([dw_up]), _ex_swap([half_down])])
    ps_up = _pair_sum(ids, dw_up, got)
    dw_mix = [_dw("out", merged, dmix, 1024, 1024)[0].reshape(N_CHIPS, D_MODEL // N_CHIPS, D_MODEL),
              _dw_slabs("branch_pool", y_pool, dbp), _dw_slabs("branch_attn", y_attn, dba)]
    (dq, dk, dv, dsink), [[got], gots] = _attn_bwd(
        q, k, v, dya, sinks, tabs, seq, exchanges=[_ex_chip([ps_up[1]]), _ex_pair(dw_mix)])
    half_up = _chip_sum(ids, ps_up[0], got)
    ps_mix = [_pair_sum(ids, d, g) for d, g in zip(dw_mix, gots)]
    (du, dw_pool, dps), [[g_up]] = _pool_bwd(dyp, diff, w_pool, pool_scale, seq, exchanges=[_ex_swap([half_up])])
    parts = (du, dq, dk, dv, dgate)
    gx, dg1, db_in = _inproj_bwd(parts, x2, dx1, w_in, g1)
    little = dict(w_pool=dw_pool, b_in=db_in, g_mix_pre=dg1, g_mix_post=dg2, g_mlp_pre=dg3, g_mlp_post=dg4,
                  pool_scale=dps, attn_sinks=dsink[:, 0], loss=loss_acc[0, 0])
    block = _pack_small(little)
    (dw_in_t,), [gots, [gathered]] = _dw_in(
        h, parts, exchanges=[_ex_chip([p[1] for p in ps_mix]), _ex_allgather(block)])
    half_mix = [_chip_sum(ids, p[0], g) for p, g in zip(ps_mix, gots)]
    total = _sum_blocks(gathered, block.shape[0])

    dw_in = dw_in_t.reshape(N_CHIPS, IN_WIDTH // N_CHIPS, D_MODEL)
    g_mix, [got] = _alone("swap_mix_pair_in", _ex_swap(half_mix), _ex_pair([dw_in]))
    ps_in = _pair_sum(ids, dw_in, got)
    [[got]] = _alone("chip_in", _ex_chip([ps_in[1]]))
    [[g_in]] = _alone("swap_in", _ex_swap([_chip_sum(ids, ps_in[0], got)]))

    grads = dict(w_in=g_in, w_branch_pool=g_mix[1], w_branch_attn=g_mix[2], w_out=g_mix[0], w_up=g_up, w_down=g_down)
    return total, gx, grads


def kernel(x, g_mix_pre, w_in, b_in, w_pool, pool_scale, attn_sinks, w_branch_pool, w_branch_attn, w_out, g_mix_post, g_mlp_pre, w_up, w_down, g_mlp_post, loss_target, m_g_mix_pre, m_w_in, m_b_in, m_w_pool, m_pool_scale, m_attn_sinks, m_w_branch_pool, m_w_branch_attn, m_w_out, m_g_mix_post, m_g_mlp_pre, m_w_up, m_w_down, m_g_mlp_post, v_g_mix_pre, v_w_in, v_b_in, v_w_pool, v_pool_scale, v_attn_sinks, v_w_branch_pool, v_w_branch_attn, v_w_out, v_g_mix_post, v_g_mlp_pre, v_w_up, v_w_down, v_g_mlp_post):
    weights = dict(g_mix_pre=g_mix_pre, w_in=w_in, b_in=b_in, w_pool=w_pool, pool_scale=pool_scale,
                   attn_sinks=attn_sinks, w_branch_pool=w_branch_pool, w_branch_attn=w_branch_attn, w_out=w_out,
                   g_mix_post=g_mix_post, g_mlp_pre=g_mlp_pre, w_up=w_up, w_down=w_down, g_mlp_post=g_mlp_post)
    mom1 = dict(g_mix_pre=m_g_mix_pre, w_in=m_w_in, b_in=m_b_in, w_pool=m_w_pool, pool_scale=m_pool_scale,
                attn_sinks=m_attn_sinks, w_branch_pool=m_w_branch_pool, w_branch_attn=m_w_branch_attn,
                w_out=m_w_out, g_mix_post=m_g_mix_post, g_mlp_pre=m_g_mlp_pre, w_up=m_w_up, w_down=m_w_down,
                g_mlp_post=m_g_mlp_post)
    mom2 = dict(g_mix_pre=v_g_mix_pre, w_in=v_w_in, b_in=v_b_in, w_pool=v_w_pool, pool_scale=v_pool_scale,
                attn_sinks=v_attn_sinks, w_branch_pool=v_w_branch_pool, w_branch_attn=v_w_branch_attn,
                w_out=v_w_out, g_mix_post=v_g_mix_post, g_mlp_pre=v_g_mlp_pre, w_up=v_w_up, w_down=v_w_down,
                g_mlp_post=v_g_mlp_post)
    b_loc, seq, _ = x.shape
    x2 = x.reshape(b_loc * seq, D_MODEL)
    tgt = loss_target.reshape(b_loc * seq, D_MODEL)
    ids = jnp.stack([2 * lax.axis_index("x") + lax.axis_index("y"), lax.axis_index("c")]).astype(jnp.int32)

    def flat(n, a):
        return a[0].T if n == "w_in" else a[0]

    def unflat(n, a):
        return (a.T if n == "w_in" else a)[None]

    shards = {n: flat(n, weights[n]).astype(BF16) for n in _BIG}
    small = {n: weights[n] for n in _ORDER if n not in _BIG}
    total, gx, grads = _step(x2, tgt, seq, shards, small, ids)

    small_shapes = {n: weights[n].shape for n, _ in _SMALL if n != "loss"}
    small_shapes["loss"] = ()
    total = _unpack_small(total, small_shapes)
    loss = total.pop("loss")
    grads.update(total)

    delta, new_m, new_v = {}, {}, {}
    for n in _BIG:
        d, nm, nv = _adamw(flat(n, weights[n]), grads[n], flat(n, mom1[n]), flat(n, mom2[n]))
        grads[n] = unflat(n, grads[n])
        delta[n], new_m[n], new_v[n] = unflat(n, d), unflat(n, nm), unflat(n, nv)
    packed = [_pack_small({**src, "loss": jnp.zeros((), F32)}) for src in (weights, grads, mom1, mom2)]
    for dst, res in zip((delta, new_m, new_v), _adamw(*packed)):
        dst.update({n: a for n, a in _unpack_small(res, small_shapes).items() if n != "loss"})

    return (loss, gx.reshape(x.shape), *[grads[n] for n in _ORDER], *[delta[n] for n in _ORDER],
            *[new_m[n] for n in _ORDER], *[new_v[n] for n in _ORDER])
```

```python
import jax
import jax.numpy as jnp
from jax import lax
from jax.experimental import pallas as pl
from jax.experimental.pallas import tpu as pltpu

F32 = jnp.float32
BF16 = jnp.bfloat16

D_MODEL = 1024
POOL_WINDOWS = (2, 4, 8, 16)
POOL_WIDTH = 512
POOL_GC = 128
HALO = 16
HEAD_DIM = 64
N_Q_HEADS = 8
ATTN_WIDTH = 512
KV_WIDTH = 128
BLOCK = 128
NEG_INF = -1e30
ROPE_THETA = 500000.0
ROT_DIM = 16
GATE_WIDTH = 2048
IN_WIDTH = 3328
D_FF = 4096
EPS = 1e-6
SCALE = HEAD_DIM ** -0.5
C_Q, C_K, C_V, C_G = 512, 1024, 1152, 1280

ADAM_LR, ADAM_B1, ADAM_B2, ADAM_EPS, ADAM_WD, ADAM_STEP = 0.001, 0.9, 0.999, 1e-08, 0.01, 10

N_CHIPS = 4
N_DEV = 8
LANES = 128
TM = 512
TP = 256
VMEM_MB = 56

MESH = pl.DeviceIdType.MESH
ANY = pl.BlockSpec(memory_space=pl.ANY)


def _cp(*sem, vmem=VMEM_MB):
    return pltpu.CompilerParams(dimension_semantics=sem, vmem_limit_bytes=vmem * 1024 * 1024)


def _rows(tile, cols):
    return pl.BlockSpec((tile, cols), lambda i: (i, 0))


def _const(shape):
    nd = len(shape)
    return pl.BlockSpec(shape, lambda i: (0,) * nd)


def _sds(shape, dtype):
    return jax.ShapeDtypeStruct(shape, dtype)


def _dot(a, b):
    return jnp.dot(a, b, preferred_element_type=F32)


def _dot_nt(a, b):
    return lax.dot_general(a, b, (((1,), (1,)), ((), ())), preferred_element_type=F32)


def _dot_tn(a, b):
    return lax.dot_general(a, b, (((0,), (0,)), ((), ())), preferred_element_type=F32)


def _rms(x):
    return lax.rsqrt(jnp.mean(x * x, axis=-1, keepdims=True) + EPS)


def _norm_bwd(x, g, dout):
    r = _rms(x)
    n = x * r
    dn = dout * g
    dx = r * (dn - n * jnp.mean(dn * n, axis=-1, keepdims=True))
    return dx, jnp.sum(dout * n, axis=0, keepdims=True)


def _rot_fwd(t, c, a, bt):
    return t * c + pltpu.roll(t, LANES - 8, 1) * a + pltpu.roll(t, 8, 1) * bt


def _rot_bwd(d, c, a, bt):
    return d * c + pltpu.roll(d * a, 8, 1) + pltpu.roll(d * bt, LANES - 8, 1)


def _rope_tables(seq):
    pos = jnp.arange(seq, dtype=F32)
    inv_freq = ROPE_THETA ** (-jnp.arange(0, ROT_DIM, 2, dtype=F32) / ROT_DIM)
    ang = pos[:, None] * inv_freq[None, :]
    cos, sin = jnp.cos(ang), jnp.sin(ang)
    ones = jnp.ones((seq, HEAD_DIM - ROT_DIM), F32)
    zeros8 = jnp.zeros((seq, 8), F32)
    zrest = jnp.zeros((seq, HEAD_DIM - ROT_DIM), F32)
    c = jnp.concatenate([cos, cos, ones], axis=1)
    a = jnp.concatenate([-sin, zeros8, zrest], axis=1)
    bt = jnp.concatenate([zeros8, sin, zrest], axis=1)
    return tuple(jnp.tile(t, (1, 2)) for t in (c, a, bt))


class _Exchange:
    def __init__(self, inputs, out_shapes, sems, start, finish, aliases=None):
        self.inputs, self.out_shapes, self.sems = list(inputs), list(out_shapes), list(sems)
        self.start, self.finish, self.aliases = start, finish, dict(aliases or {})


def _call(body, *, name, grid, in_specs, out_specs, out_shape, args, scratch=(), sem=(), exchanges=()):
    in_specs, out_specs, out_shape, scratch = list(in_specs), list(out_specs), list(out_shape), list(scratch)
    if not exchanges:
        return pl.pallas_call(body, name=name, grid=grid, in_specs=in_specs, out_specs=out_specs,
                              out_shape=out_shape, scratch_shapes=scratch, compiler_params=_cp(*sem))(*args)
    n_in, n_out, n_scr = len(in_specs), len(out_specs), len(scratch)
    x_in = [a for ex in exchanges for a in ex.inputs]
    x_out = [s for ex in exchanges for s in ex.out_shapes]
    x_sem = [s for ex in exchanges for s in ex.sems]
    aliases, i_off, o_off = {}, n_in, n_out
    for ex in exchanges:
        for i, o in ex.aliases.items():
            aliases[i_off + i] = o_off + o
        i_off += len(ex.inputs)
        o_off += len(ex.out_shapes)

    def split(flat):
        out, pos = [], 0
        for ex, n in zip(exchanges, flat[1]):
            out.append(flat[0][pos:pos + n])
            pos += n
        return out

    def carrier(*refs):
        pos = 0
        groups = []
        for n in (n_in, len(x_in), n_out, len(x_out), n_scr, len(x_sem)):
            groups.append(refs[pos:pos + n])
            pos += n
        ins, xin, outs, xout, scr, xsem = groups
        xin = split((xin, [len(ex.inputs) for ex in exchanges]))
        xout = split((xout, [len(ex.out_shapes) for ex in exchanges]))
        xsem = split((xsem, [len(ex.sems) for ex in exchanges]))
        first = pl.program_id(0) == 0
        last = pl.program_id(0) == grid[0] - 1
        for d in range(1, len(grid)):
            first = jnp.logical_and(first, pl.program_id(d) == 0)
            last = jnp.logical_and(last, pl.program_id(d) == grid[d] - 1)

        @pl.when(first)
        def _():
            for ex, i, o, s in zip(exchanges, xin, xout, xsem):
                ex.start(i, o, s)

        body(*ins, *outs, *scr)

        @pl.when(last)
        def _():
            for ex, i, o, s in zip(exchanges, xin, xout, xsem):
                ex.finish(i, o, s)

    res = pl.pallas_call(
        carrier, name=name, grid=grid, in_specs=in_specs + [ANY] * len(x_in),
        out_specs=out_specs + [ANY] * len(x_out), out_shape=out_shape + x_out,
        scratch_shapes=scratch + x_sem, input_output_aliases=aliases,
        compiler_params=_cp(*(["arbitrary"] * len(grid))),
    )(*args, *x_in)
    return res[:n_out], split((res[n_out:], [len(ex.out_shapes) for ex in exchanges]))


def _alone(name, *exchanges):
    n_in = [len(ex.inputs) for ex in exchanges]
    n_out = [len(ex.out_shapes) for ex in exchanges]
    n_sem = [len(ex.sems) for ex in exchanges]
    aliases, i_off, o_off = {}, 0, 0
    for ex in exchanges:
        for i, o in ex.aliases.items():
            aliases[i_off + i] = o_off + o
        i_off += len(ex.inputs)
        o_off += len(ex.out_shapes)

    def split(flat, counts):
        out, pos = [], 0
        for n in counts:
            out.append(flat[pos:pos + n])
            pos += n
        return out

    def body(*refs):
        ins, outs, sems = split(refs, [sum(n_in), sum(n_out), sum(n_sem)])
        groups = list(zip(exchanges, split(ins, n_in), split(outs, n_out), split(sems, n_sem)))
        for ex, i, o, s in groups:
            ex.start(i, o, s)
        for ex, i, o, s in groups:
            ex.finish(i, o, s)

    res = pl.pallas_call(
        body, name=name, in_specs=[ANY] * sum(n_in), out_specs=[ANY] * sum(n_out),
        out_shape=[s for ex in exchanges for s in ex.out_shapes],
        scratch_shapes=[s for ex in exchanges for s in ex.sems], input_output_aliases=aliases,
    )(*[a for ex in exchanges for a in ex.inputs])
    return split(res, n_out)


def _place():
    x, y, c = lax.axis_index("x"), lax.axis_index("y"), lax.axis_index("c")
    chips = [(1 - x, y), (x, 1 - y), (1 - x, 1 - y)]
    return x, y, c, chips


def _remote(src, dst, send, recv, to):
    return pltpu.make_async_remote_copy(src_ref=src, dst_ref=dst, send_sem=send, recv_sem=recv,
                                        device_id=to, device_id_type=MESH)


def _ex_gather(shards):
    nw = len(shards)
    hrs = [s.shape[0] // 2 for s in shards]

    def copies(ins, outs, sems):
        send, recv, fsend, frecv, lsem = sems
        x, y, c, chips = _place()
        me = 2 * x + y
        sibling = (x, y, 1 - c)

        def piece(w, chip_idx, core):
            return outs[w].at[chip_idx, pl.ds(core * hrs[w], hrs[w])]

        def from_chip(w, k, core):
            cx, cy = chips[k]
            return piece(w, 2 * cx + cy, core)

        def local(w):
            return pltpu.make_async_copy(ins[w], outs[w].at[me], lsem.at[w])

        def first(w, k):
            return _remote(ins[w].at[pl.ds(c * hrs[w], hrs[w])], piece(w, me, c), send.at[w, k], recv.at[w, k],
                           (*chips[k], c))

        def landed(w, k):
            return _remote(from_chip(w, k, c), from_chip(w, k, c), send.at[w, k], recv.at[w, k], (*chips[k], c))

        def passed(w, k):
            return _remote(from_chip(w, k, c), from_chip(w, k, c), fsend.at[w, k], frecv.at[w, k], sibling)

        def handed(w, k):
            return _remote(from_chip(w, k, 1 - c), from_chip(w, k, 1 - c), fsend.at[w, k], frecv.at[w, k], sibling)

        return local, first, landed, passed, handed

    def start(ins, outs, sems):
        local, first, _, _, _ = copies(ins, outs, sems)
        for w in range(nw):
            local(w).start()
            for k in range(3):
                first(w, k).start()

    def finish(ins, outs, sems):
        local, first, landed, passed, handed = copies(ins, outs, sems)
        sent = []
        for w in range(nw):
            for k in range(3):
                landed(w, k).wait_recv()
                cp = passed(w, k)
                cp.start()
                sent.append(cp)
        for w in range(nw):
            for k in range(3):
                handed(w, k).wait_recv()
        for cp in sent:
            cp.wait_send()
        for w in range(nw):
            for k in range(3):
                first(w, k).wait_send()
            local(w).wait()

    return _Exchange(shards, [_sds((N_CHIPS,) + s.shape, s.dtype) for s in shards],
                     [pltpu.SemaphoreType.DMA((nw, 3))] * 4 + [pltpu.SemaphoreType.DMA((nw,))], start, finish)


def _ex_pair(grads):
    nw = len(grads)

    def copies(ins, outs, sems):
        x, y, c, _ = _place()
        out = []
        for w in range(nw):
            hr = grads[w].shape[1] // 2
            out.append(_remote(ins[w].at[:, pl.ds((1 - c) * hr, hr)], outs[w], sems[0].at[w], sems[1].at[w],
                               (x, y, 1 - c)))
        return out

    def start(ins, outs, sems):
        for cp in copies(ins, outs, sems):
            cp.start()

    def finish(ins, outs, sems):
        for cp in copies(ins, outs, sems):
            cp.wait()

    return _Exchange(grads, [_sds((N_CHIPS, g.shape[1] // 2, g.shape[2]), F32) for g in grads],
                     [pltpu.SemaphoreType.DMA((nw,))] * 2, start, finish)


def _ex_chip(pieces):
    nw = len(pieces)

    def copies(ins, outs, sems):
        x, y, c, chips = _place()
        return [_remote(ins[w].at[2 * cx + cy], outs[w].at[k], sems[0].at[w, k], sems[1].at[w, k], (cx, cy, c))
                for w in range(nw) for k, (cx, cy) in enumerate(chips)]

    def start(ins, outs, sems):
        for cp in copies(ins, outs, sems):
            cp.start()

    def finish(ins, outs, sems):
        for cp in copies(ins, outs, sems):
            cp.wait()

    return _Exchange(pieces, [_sds((3,) + p.shape[1:], BF16) for p in pieces],
                     [pltpu.SemaphoreType.DMA((nw, 3))] * 2, start, finish)


def _ex_swap(fulls):
    nw = len(fulls)

    def start(ins, outs, sems):
        x, y, c, _ = _place()
        for w in range(nw):
            hr = fulls[w].shape[0] // 2
            mine = pl.ds(c * hr, hr)
            _remote(ins[w].at[mine], outs[w].at[mine], sems[0].at[w], sems[1].at[w], (x, y, 1 - c)).start()

    def finish(ins, outs, sems):
        x, y, c, _ = _place()
        for w in range(nw):
            hr = fulls[w].shape[0] // 2
            mine, theirs = pl.ds(c * hr, hr), pl.ds((1 - c) * hr, hr)
            _remote(ins[w].at[mine], outs[w].at[mine], sems[0].at[w], sems[1].at[w], (x, y, 1 - c)).wait_send()
            _remote(ins[w].at[theirs], outs[w].at[theirs], sems[0].at[w], sems[1].at[w], (x, y, 1 - c)).wait_recv()

    return _Exchange(fulls, [_sds(f.shape, F32) for f in fulls], [pltpu.SemaphoreType.DMA((nw,))] * 2,
                     start, finish, aliases={w: w for w in range(nw)})


def _ex_allgather(block):
    m_per, n = block.shape

    def copies(ins, outs, sems):
        send, recv, lsem = sems
        x, y, c, chips = _place()
        me, sibling = (x, y, c), (x, y, 1 - c)

        def rows(px, py, pc):
            return outs[0].at[pl.ds((4 * px + 2 * py + pc) * m_per, m_per), :]

        def copy(k, blk, to, src=None):
            return _remote(rows(*blk) if src is None else src, rows(*blk), send.at[k], recv.at[k], to)

        def mine():
            return pltpu.make_async_copy(ins[0], rows(*me), lsem.at[0])

        def first(k):
            return copy(k, me, sibling if k == 0 else (*chips[k - 1], c), src=ins[0])

        def passed(j):
            return copy(4 + j, (*chips[j], c), sibling)

        def landed(j):
            return copy(1 + j, (*chips[j], c), me)

        def handed(k):
            return copy(0, sibling, me) if k == 0 else copy(3 + k, (*chips[k - 1], 1 - c), me)

        return mine, first, passed, landed, handed

    def start(ins, outs, sems):
        mine, first, _, _, _ = copies(ins, outs, sems)
        mine().start()
        for k in range(4):
            first(k).start()

    def finish(ins, outs, sems):
        mine, first, passed, landed, handed = copies(ins, outs, sems)
        sent = []
        for j in range(3):
            landed(j).wait_recv()
            cp = passed(j)
            cp.start()
            sent.append(cp)
        for k in range(4):
            handed(k).wait_recv()
        for k in range(4):
            first(k).wait_send()
        for cp in sent:
            cp.wait_send()
        mine().wait()

    return _Exchange([block], [_sds((N_DEV * m_per, n), F32)],
                     [pltpu.SemaphoreType.DMA((7,)), pltpu.SemaphoreType.DMA((7,)), pltpu.SemaphoreType.DMA((1,))],
                     start, finish)


def _inproj(x2, g1, w_in_t, b_in, tabs, seq, exchanges=()):
    T = x2.shape[0]
    tm = min(TM, seq)
    nseq = seq // tm

    def body(x_ref, g_ref, w_ref, b_ref, c_ref, a_ref, bt_ref, h_ref, u_ref, q_ref, k_ref, v_ref, gate_ref):
        x = x_ref[...]
        h = (x * _rms(x) * g_ref[...]).astype(BF16)
        h_ref[...] = h

        def proj(lo, hi):
            return _dot_nt(h, w_ref[lo:hi, :]) + b_ref[:, lo:hi]

        c, a, bt = c_ref[...], a_ref[...], bt_ref[...]
        u_ref[...] = proj(0, C_Q)
        q = proj(C_Q, C_K)
        for p in range(4):
            sl = slice(LANES * p, LANES * (p + 1))
            q_ref[:, sl] = (_rot_fwd(q[:, sl], c, a, bt) * SCALE).astype(BF16)
        kv = proj(C_K, C_G)
        k_ref[...] = _rot_fwd(kv[:, :KV_WIDTH], c, a, bt).astype(BF16)
        v_ref[...] = kv[:, KV_WIDTH:].astype(BF16)
        for j in range(2):
            lo = C_G + D_MODEL * j
            gate_ref[:, D_MODEL * j:D_MODEL * (j + 1)] = jax.nn.sigmoid(proj(lo, lo + D_MODEL)).astype(BF16)

    tab = pl.BlockSpec((tm, LANES), lambda i: (i % nseq, 0))
    return _call(
        body, name="inproj", grid=(T // tm,),
        in_specs=[_rows(tm, D_MODEL), _const((1, D_MODEL)), _const((IN_WIDTH, D_MODEL)), _const((1, IN_WIDTH)),
                  tab, tab, tab],
        out_specs=[_rows(tm, D_MODEL), _rows(tm, POOL_WIDTH), _rows(tm, ATTN_WIDTH), _rows(tm, KV_WIDTH),
                   _rows(tm, KV_WIDTH), _rows(tm, GATE_WIDTH)],
        out_shape=[_sds((T, D_MODEL), BF16), _sds((T, POOL_WIDTH), F32), _sds((T, ATTN_WIDTH), BF16),
                   _sds((T, KV_WIDTH), BF16), _sds((T, KV_WIDTH), BF16), _sds((T, GATE_WIDTH), BF16)],
        args=(x2, g1, w_in_t, b_in, *tabs), sem=("parallel",), exchanges=exchanges)


def _inv_count(pos, w):
    return 1.0 / jnp.minimum(pos + 1, w).astype(F32)


def _pool_fwd(u, w_pool, pool_scale, seq):
    T = u.shape[0]
    tp = min(TP, seq)
    nseq = seq // tp
    per = tp // HALO

    def body(u_ref, prev_ref, w_ref, s_ref, diff_ref, y_ref):
        i = pl.program_id(0)
        first = (i % nseq) == 0
        prev = jnp.where(first, 0.0, prev_ref[...])
        ext = jnp.concatenate([prev, u_ref[...]], axis=0)
        pos = (i % nseq) * tp + lax.broadcasted_iota(jnp.int32, (tp, 1), 0)
        for gi, w in enumerate(POOL_WINDOWS):
            sl = slice(POOL_GC * gi, POOL_GC * (gi + 1))
            xg = ext[:, sl]
            s = xg
            sh = 1
            while sh < w:
                s = s + pltpu.roll(s, sh, 0)
                sh *= 2
            pooled = s[HALO:] * _inv_count(pos, w)
            diff = (pooled - xg[HALO:]).astype(BF16)
            diff_ref[:, sl] = diff
            mixed = _dot(diff, w_ref[gi].astype(BF16))
            y_ref[:, sl] = (mixed * s_ref[:, sl]).astype(BF16)

    return _call(
        body, name="pool_fwd", grid=(T // tp,),
        in_specs=[_rows(tp, POOL_WIDTH),
                  pl.BlockSpec((HALO, POOL_WIDTH), lambda i: (jnp.maximum(i * per - 1, 0), 0)),
                  _const((4, POOL_GC, POOL_GC)), _const((1, POOL_WIDTH))],
        out_specs=[_rows(tp, POOL_WIDTH), _rows(tp, POOL_WIDTH)],
        out_shape=[_sds((T, POOL_WIDTH), BF16), _sds((T, POOL_WIDTH), BF16)],
        args=(u, u, w_pool, pool_scale), sem=("parallel",))


GROUP = 4
GROWS = GROUP * BLOCK


def _attn_masks(n):
    qi = lax.broadcasted_iota(jnp.int32, (GROWS, 2 * BLOCK), 0) % BLOCK
    kj = lax.broadcasted_iota(jnp.int32, (GROWS, 2 * BLOCK), 1)
    rel = qi + BLOCK - kj
    valid = (rel >= 0) & (rel < BLOCK) & (kj >= jnp.where(n > 0, 0, BLOCK))
    lo = lax.broadcasted_iota(jnp.int32, (BLOCK, LANES), 1) < HEAD_DIM
    return valid, lo


def _stack_heads(ref, h, lo):
    keep = lo if h == 0 else jnp.logical_not(lo)
    pieces = []
    for p in (2 * h, 2 * h + 1):
        xp = ref[:, LANES * p:LANES * (p + 1)].astype(F32)
        for e in range(2):
            t = xp if e == h else pltpu.roll(xp, HEAD_DIM, 1)
            pieces.append(jnp.where(keep, t, 0.0).astype(BF16))
    return jnp.concatenate(pieces, axis=0)


def _unstack_heads(stacked, h, lo):
    pairs = []
    for j in range(2):
        parts = []
        for e in range(2):
            t = stacked[BLOCK * (2 * j + e):BLOCK * (2 * j + e + 1)]
            parts.append(t if e == h else pltpu.roll(t, HEAD_DIM, 1))
        pairs.append(jnp.where(lo, parts[0], parts[1]))
    return pairs


def _sink_rows(sink_ref, h):
    head = lax.broadcasted_iota(jnp.int32, (GROWS, 1), 0) // BLOCK
    col = jnp.zeros((GROWS, 1), F32) + sink_ref[GROUP * h]
    for g in range(1, GROUP):
        col = jnp.where(head == g, sink_ref[GROUP * h + g], col)
    return col


def _group_probs(qs, kk, valid, sink):
    s = jnp.where(valid, _dot_nt(qs, kk), NEG_INF)
    m = jnp.maximum(jnp.max(s, axis=1, keepdims=True), sink)
    ex = jnp.exp(s - m)
    es = jnp.exp(sink - m)
    inv = 1.0 / (jnp.sum(ex, axis=1, keepdims=True) + es)
    return ex * inv, es * inv


def _attn_fwd(q, k, v, sinks, seq, exchanges=()):
    T = q.shape[0]
    nb = seq // BLOCK

    def body(sink_ref, q_ref, kp_ref, kc_ref, vp_ref, vc_ref, o_ref):
        n = pl.program_id(0) % nb
        kk = jnp.concatenate([kp_ref[...], kc_ref[...]], axis=0)
        vv = jnp.concatenate([vp_ref[...], vc_ref[...]], axis=0)
        valid, lo = _attn_masks(n)
        for h in range(2):
            qs = _stack_heads(q_ref, h, lo)
            pr, _ = _group_probs(qs, kk, valid, _sink_rows(sink_ref, h))
            o = _dot(pr.astype(BF16), vv)
            for j, pair in enumerate(_unstack_heads(o, h, lo)):
                p = 2 * h + j
                o_ref[:, LANES * p:LANES * (p + 1)] = pair.astype(BF16)

    cur = lambda i: (i, 0)
    prv = lambda i: (jnp.where(i % nb == 0, i, i - 1), 0)
    return _call(
        body, name="attn_fwd", grid=(T // BLOCK,),
        in_specs=[pl.BlockSpec(memory_space=pltpu.SMEM),
                  pl.BlockSpec((BLOCK, ATTN_WIDTH), cur),
                  pl.BlockSpec((BLOCK, KV_WIDTH), prv), pl.BlockSpec((BLOCK, KV_WIDTH), cur),
                  pl.BlockSpec((BLOCK, KV_WIDTH), prv), pl.BlockSpec((BLOCK, KV_WIDTH), cur)],
        out_specs=[pl.BlockSpec((BLOCK, ATTN_WIDTH), cur)],
        out_shape=[_sds((T, ATTN_WIDTH), BF16)],
        args=(sinks, q, k, k, v, v), sem=("parallel",), exchanges=exchanges)


def _merge_out(y_pool, y_attn, gate, x2, w_bp, w_ba, w_out, g2, g3, exchanges=()):
    T = x2.shape[0]
    tm = min(TM, T)

    def body(yp_ref, ya_ref, gate_ref, x_ref, wbp_ref, wba_ref, wo_ref, g2_ref, g3_ref,
             bp_ref, ba_ref, mg_ref, mix_ref, x1_ref, h2_ref):
        yp, ya = yp_ref[...], ya_ref[...]
        bp = jnp.concatenate([_dot(yp, wbp_ref[j]) for j in range(N_CHIPS)], axis=1)
        ba = jnp.concatenate([_dot(ya, wba_ref[j]) for j in range(N_CHIPS)], axis=1)
        bp_ref[...] = bp.astype(BF16)
        ba_ref[...] = ba.astype(BF16)
        merged = (gate_ref[:, :D_MODEL].astype(F32) * bp + gate_ref[:, D_MODEL:].astype(F32) * ba).astype(BF16)
        mg_ref[...] = merged
        mix = _dot(merged, wo_ref[...])
        mix_ref[...] = mix
        x1 = x_ref[...] + mix * _rms(mix) * g2_ref[...]
        x1_ref[...] = x1
        h2_ref[...] = (x1 * _rms(x1) * g3_ref[...]).astype(BF16)

    return _call(
        body, name="merge_out", grid=(T // tm,),
        in_specs=[_rows(tm, POOL_WIDTH), _rows(tm, ATTN_WIDTH), _rows(tm, GATE_WIDTH), _rows(tm, D_MODEL),
                  _const(w_bp.shape), _const(w_ba.shape), _const((D_MODEL, D_MODEL)),
                  _const((1, D_MODEL)), _const((1, D_MODEL))],
        out_specs=[_rows(tm, D_MODEL)] * 6,
        out_shape=[_sds((T, D_MODEL), BF16), _sds((T, D_MODEL), BF16), _sds((T, D_MODEL), BF16),
                   _sds((T, D_MODEL), F32), _sds((T, D_MODEL), F32), _sds((T, D_MODEL), BF16)],
        args=(y_pool, y_attn, gate, x2, w_bp, w_ba, w_out, g2, g3), sem=("parallel",), exchanges=exchanges)


def _mlp_up(h2, w_up):
    T = h2.shape[0]
    tm = min(TM, T)

    def body(h_ref, w_ref, up_ref, a_ref):
        h = h_ref[...]
        for j in range(N_CHIPS):
            sl = slice(D_MODEL * j, D_MODEL * (j + 1))
            up = _dot(h, w_ref[j])
            up_ref[:, sl] = up.astype(BF16)
            a_ref[:, sl] = jnp.square(jnp.maximum(up, 0.0)).astype(BF16)

    return _call(
        body, name="mlp_up", grid=(T // tm,),
        in_specs=[_rows(tm, D_MODEL), _const((N_CHIPS, D_MODEL, D_MODEL))],
        out_specs=[_rows(tm, D_FF), _rows(tm, D_FF)],
        out_shape=[_sds((T, D_FF), BF16), _sds((T, D_FF), BF16)],
        args=(h2, w_up), sem=("parallel",))


def _mlp_down_loss(a, x1, tgt, w_down, g4):
    T = a.shape[0]
    tm = min(TM, T)

    def body(a_ref, x1_ref, t_ref, w_ref, g_ref, dff_ref, dy_ref, loss_ref, dg_ref):
        @pl.when(pl.program_id(0) == 0)
        def _():
            loss_ref[...] = jnp.zeros_like(loss_ref)
            dg_ref[...] = jnp.zeros_like(dg_ref)

        ff = _dot(a_ref[...], w_ref[...])
        g = g_ref[...]
        err = x1_ref[...] + ff * _rms(ff) * g - t_ref[...]
        loss_ref[...] += jnp.sum(err * err) * (0.5 / D_MODEL)
        dy = err * (1.0 / D_MODEL)
        dy_ref[...] = dy
        dff, dg = _norm_bwd(ff, g, dy)
        dff_ref[...] = dff.astype(BF16)
        dg_ref[...] += dg

    return _call(
        body, name="mlp_down_loss", grid=(T // tm,),
        in_specs=[_rows(tm, D_FF), _rows(tm, D_MODEL), _rows(tm, D_MODEL), _const((D_FF, D_MODEL)),
                  _const((1, D_MODEL))],
        out_specs=[_rows(tm, D_MODEL), _rows(tm, D_MODEL), _const((8, LANES)), _const((1, D_MODEL))],
        out_shape=[_sds((T, D_MODEL), BF16), _sds((T, D_MODEL), F32), _sds((8, LANES), F32),
                   _sds((1, D_MODEL), F32)],
        args=(a, x1, tgt, w_down, g4), sem=("arbitrary",))


def _mlp_down_bwd(dff, up, w_down):
    T = dff.shape[0]
    tm = min(TM, T)

    def body(d_ref, up_ref, w_ref, dup_ref):
        d = d_ref[...]
        for j in range(D_FF // D_MODEL):
            sl = slice(D_MODEL * j, D_MODEL * (j + 1))
            da = _dot_nt(d, w_ref[sl, :])
            dup_ref[:, sl] = (da * (2.0 * jnp.maximum(up_ref[:, sl].astype(F32), 0.0))).astype(BF16)

    return _call(
        body, name="mlp_down_bwd", grid=(T // tm,),
        in_specs=[_rows(tm, D_MODEL), _rows(tm, D_FF), _const((D_FF, D_MODEL))],
        out_specs=[_rows(tm, D_FF)],
        out_shape=[_sds((T, D_FF), BF16)],
        args=(dff, up, w_down), sem=("parallel",))[0]


def _mlp_up_bwd(dup, dy, x1, mix, w_up, g3, g2, exchanges=()):
    T = dup.shape[0]
    tm = min(TM, T)

    def body(dup_ref, dy_ref, x1_ref, mix_ref, w_ref, g3_ref, g2_ref, dx1_ref, dmix_ref, dg3_ref, dg2_ref):
        @pl.when(pl.program_id(0) == 0)
        def _():
            dg3_ref[...] = jnp.zeros_like(dg3_ref)
            dg2_ref[...] = jnp.zeros_like(dg2_ref)

        dh2 = _dot_nt(dup_ref[:, :D_MODEL], w_ref[0])
        for j in range(1, N_CHIPS):
            dh2 = dh2 + _dot_nt(dup_ref[:, D_MODEL * j:D_MODEL * (j + 1)], w_ref[j])
        dx, dg3 = _norm_bwd(x1_ref[...], g3_ref[...], dh2)
        dx1 = dy_ref[...] + dx
        dx1_ref[...] = dx1
        dg3_ref[...] += dg3
        dmix, dg2 = _norm_bwd(mix_ref[...], g2_ref[...], dx1)
        dmix_ref[...] = dmix.astype(BF16)
        dg2_ref[...] += dg2

    return _call(
        body, name="mlp_up_bwd", grid=(T // tm,),
        in_specs=[_rows(tm, D_FF), _rows(tm, D_MODEL), _rows(tm, D_MODEL), _rows(tm, D_MODEL),
                  _const((N_CHIPS, D_MODEL, D_MODEL)), _const((1, D_MODEL)), _const((1, D_MODEL))],
        out_specs=[_rows(tm, D_MODEL), _rows(tm, D_MODEL), _const((1, D_MODEL)), _const((1, D_MODEL))],
        out_shape=[_sds((T, D_MODEL), F32), _sds((T, D_MODEL), BF16), _sds((1, D_MODEL), F32),
                   _sds((1, D_MODEL), F32)],
        args=(dup, dy, x1, mix, w_up, g3, g2), sem=("arbitrary",), exchanges=exchanges)


def _dw(tag, a, g, ta, tn, shard_cols=False, exchanges=()):
    T, ka = a.shape
    n = g.shape[1]
    tk = min(TM, T)
    nk = T // tk

    def body(a_ref, g_ref, o_ref):
        @pl.when(pl.program_id(2) == 0)
        def _():
            o_ref[...] = jnp.zeros_like(o_ref)

        o_ref[...] += _dot_tn(a_ref[...], g_ref[...])

    if shard_cols:
        per = (n // N_CHIPS) // tn
        out_spec = pl.BlockSpec((None, ta, tn), lambda i, j, k: (j // per, i, j % per))
        out_shape = _sds((N_CHIPS, ka, n // N_CHIPS), F32)
    else:
        out_spec = pl.BlockSpec((ta, tn), lambda i, j, k: (i, j))
        out_shape = _sds((ka, n), F32)
    return _call(
        body, name="dw_" + tag, grid=(ka // ta, n // tn, nk),
        in_specs=[pl.BlockSpec((tk, ta), lambda i, j, k: (k, i)), pl.BlockSpec((tk, tn), lambda i, j, k: (k, j))],
        out_specs=[out_spec], out_shape=[out_shape],
        args=(a, g), sem=("parallel", "parallel", "arbitrary"), exchanges=exchanges)


def _dw_slabs(tag, a, g):
    T, ka = a.shape
    n = g.shape[1]
    c = n // N_CHIPS
    tk = min(TM, T)

    def body(a_ref, g_ref, o_ref):
        @pl.when(pl.program_id(0) == 0)
        def _():
            o_ref[...] = jnp.zeros_like(o_ref)

        res = _dot_tn(a_ref[...], g_ref[...])
        for j in range(N_CHIPS):
            o_ref[j] += res[:, c * j:c * (j + 1)]

    return _call(
        body, name="dw_" + tag, grid=(T // tk,),
        in_specs=[_rows(tk, ka), _rows(tk, n)],
        out_specs=[_const((N_CHIPS, ka, c))], out_shape=[_sds((N_CHIPS, ka, c), F32)],
        args=(a, g), sem=("arbitrary",))[0]


def _merge_bwd(dmix, gate, bp, ba, w_out, w_bp, w_ba, exchanges=()):
    T = dmix.shape[0]
    tm = min(TM, T)

    def body(dmix_ref, gate_ref, bp_ref, ba_ref, wo_ref, wbp_ref, wba_ref,
             dbp_ref, dba_ref, dgate_ref, dyp_ref, dya_ref):
        dm = _dot_nt(dmix_ref[...], wo_ref[...])
        for j, (b_ref, db_ref, w_ref, dy_ref) in enumerate(
                ((bp_ref, dbp_ref, wbp_ref, dyp_ref), (ba_ref, dba_ref, wba_ref, dya_ref))):
            sl = slice(D_MODEL * j, D_MODEL * (j + 1))
            gt = gate_ref[:, sl].astype(F32)
            db = (dm * gt).astype(BF16)
            db_ref[...] = db
            dgate_ref[:, sl] = (dm * b_ref[...].astype(F32) * gt * (1.0 - gt)).astype(BF16)
            cw = D_MODEL // N_CHIPS
            dy = _dot_nt(db[:, :cw], w_ref[0])
            for c in range(1, N_CHIPS):
                dy = dy + _dot_nt(db[:, cw * c:cw * (c + 1)], w_ref[c])
            dy_ref[...] = dy.astype(dy_ref.dtype)

    return _call(
        body, name="merge_bwd", grid=(T // tm,),
        in_specs=[_rows(tm, D_MODEL), _rows(tm, GATE_WIDTH), _rows(tm, D_MODEL), _rows(tm, D_MODEL),
                  _const((D_MODEL, D_MODEL)), _const(w_bp.shape), _const(w_ba.shape)],
        out_specs=[_rows(tm, D_MODEL), _rows(tm, D_MODEL), _rows(tm, GATE_WIDTH), _rows(tm, POOL_WIDTH),
                   _rows(tm, ATTN_WIDTH)],
        out_shape=[_sds((T, D_MODEL), BF16), _sds((T, D_MODEL), BF16), _sds((T, GATE_WIDTH), BF16),
                   _sds((T, POOL_WIDTH), F32), _sds((T, ATTN_WIDTH), BF16)],
        args=(dmix, gate, bp, ba, w_out, w_bp, w_ba), sem=("parallel",), exchanges=exchanges)


def _attn_bwd(q, k, v, do, sinks, tabs, seq, exchanges=()):
    T = q.shape[0]
    nb = seq // BLOCK
    steps = nb + 1

    def body(sink_ref, q_ref, do_ref, kp_ref, kc_ref, vp_ref, vc_ref, c_ref, a_ref, bt_ref, cp_ref, ap_ref, btp_ref,
             dq_ref, dk_ref, dv_ref, dsink_ref, ck_ref, cv_ref):
        i = pl.program_id(0)
        n = i % steps

        @pl.when(i == 0)
        def _():
            dsink_ref[...] = jnp.zeros_like(dsink_ref)

        @pl.when(n == 0)
        def _():
            ck_ref[...] = jnp.zeros_like(ck_ref)
            cv_ref[...] = jnp.zeros_like(cv_ref)

        @pl.when(n < nb)
        def _():
            kk = jnp.concatenate([kp_ref[...], kc_ref[...]], axis=0)
            vv = jnp.concatenate([vp_ref[...], vc_ref[...]], axis=0)
            valid, lo = _attn_masks(n)
            dk_acc = jnp.zeros((2 * BLOCK, KV_WIDTH), F32)
            dv_acc = jnp.zeros((2 * BLOCK, KV_WIDTH), F32)
            for h in range(2):
                qs = _stack_heads(q_ref, h, lo)
                dos = _stack_heads(do_ref, h, lo)
                pr, ps = _group_probs(qs, kk, valid, _sink_rows(sink_ref, h))
                dp = _dot_nt(dos, vv)
                delta = jnp.sum(pr * dp, axis=1, keepdims=True)
                ds = (pr * (dp - delta)).astype(BF16)
                dsk = ps * delta
                for g in range(GROUP):
                    idx = GROUP * h + g
                    dsink_ref[idx:idx + 1, :] += jnp.zeros((1, LANES), F32) - jnp.sum(dsk[BLOCK * g:BLOCK * (g + 1)])
                dk_acc = dk_acc + _dot_tn(ds, qs)
                dv_acc = dv_acc + _dot_tn(pr.astype(BF16), dos)
                for j, pair in enumerate(_unstack_heads(_dot(ds, kk) * SCALE, h, lo)):
                    sl = slice(LANES * (2 * h + j), LANES * (2 * h + j + 1))
                    dq_ref[:, sl] = _rot_bwd(pair, c_ref[...], a_ref[...], bt_ref[...]).astype(BF16)
            fin_k = ck_ref[...] + dk_acc[:BLOCK]
            dk_ref[...] = _rot_bwd(fin_k, cp_ref[...], ap_ref[...], btp_ref[...]).astype(BF16)
            dv_ref[...] = (cv_ref[...] + dv_acc[:BLOCK]).astype(BF16)
            ck_ref[...] = dk_acc[BLOCK:]
            cv_ref[...] = dv_acc[BLOCK:]

        @pl.when(n == nb)
        def _():
            dk_ref[...] = _rot_bwd(ck_ref[...], cp_ref[...], ap_ref[...], btp_ref[...]).astype(BF16)
            dv_ref[...] = cv_ref[...].astype(BF16)

    def blk(i):
        return (i // steps) * nb

    cur = lambda i: (blk(i) + jnp.minimum(i % steps, nb - 1), 0)
    prv = lambda i: (blk(i) + jnp.clip(i % steps - 1, 0, nb - 1), 0)
    tcur = lambda i: (jnp.minimum(i % steps, nb - 1), 0)
    tprv = lambda i: (jnp.clip(i % steps - 1, 0, nb - 1), 0)
    kv = lambda m: pl.BlockSpec((BLOCK, KV_WIDTH), m)
    return _call(
        body, name="attn_bwd", grid=((T // seq) * steps,),
        in_specs=[pl.BlockSpec(memory_space=pltpu.SMEM),
                  pl.BlockSpec((BLOCK, ATTN_WIDTH), cur), pl.BlockSpec((BLOCK, ATTN_WIDTH), cur),
                  kv(prv), kv(cur), kv(prv), kv(cur),
                  kv(tcur), kv(tcur), kv(tcur), kv(tprv), kv(tprv), kv(tprv)],
        out_specs=[pl.BlockSpec((BLOCK, ATTN_WIDTH), cur), kv(prv), kv(prv), _const((8, LANES))],
        out_shape=[_sds((T, ATTN_WIDTH), BF16), _sds((T, KV_WIDTH), BF16), _sds((T, KV_WIDTH), BF16),
                   _sds((8, LANES), F32)],
        scratch=[pltpu.VMEM((BLOCK, KV_WIDTH), F32), pltpu.VMEM((BLOCK, KV_WIDTH), F32)],
        args=(sinks, q, do, k, k, v, v, *tabs, *tabs), sem=("arbitrary",), exchanges=exchanges)


def _pool_bwd(dyp, diff, w_pool, pool_scale, seq, exchanges=()):
    T = dyp.shape[0]
    tp = min(TP, seq)
    nseq = seq // tp
    per = tp // HALO
    last_halo = T // HALO - 1

    def body(dy_ref, nxt_ref, diff_ref, w_ref, s_ref, du_ref, dw_ref, ds_ref):
        i = pl.program_id(0)

        @pl.when(i == 0)
        def _():
            dw_ref[...] = jnp.zeros_like(dw_ref)
            ds_ref[...] = jnp.zeros_like(ds_ref)

        last = (i % nseq) == nseq - 1
        nxt = jnp.where(last, 0.0, nxt_ref[...])
        ext = jnp.concatenate([dy_ref[...], nxt], axis=0) * s_ref[...]
        pos = (i % nseq) * tp + lax.broadcasted_iota(jnp.int32, (tp + HALO, 1), 0)
        for gi, w in enumerate(POOL_WINDOWS):
            sl = slice(POOL_GC * gi, POOL_GC * (gi + 1))
            wg = w_ref[gi].astype(BF16)
            dmx = ext[:, sl].astype(BF16)
            ddiff = _dot_nt(dmx, wg)
            s = ddiff * _inv_count(pos, w)
            sh = 1
            while sh < w:
                s = s + pltpu.roll(s, tp + HALO - sh, 0)
                sh *= 2
            du_ref[:, sl] = (s[:tp] - ddiff[:tp]).astype(BF16)
            dg = diff_ref[:, sl]
            dw_ref[gi] += _dot_tn(dg, dmx[:tp])
            ds_ref[:, sl] += jnp.sum(dy_ref[:, sl] * _dot(dg, wg), axis=0, keepdims=True)

    return _call(
        body, name="pool_bwd", grid=(T // tp,),
        in_specs=[_rows(tp, POOL_WIDTH),
                  pl.BlockSpec((HALO, POOL_WIDTH), lambda i: (jnp.minimum((i + 1) * per, last_halo), 0)),
                  _rows(tp, POOL_WIDTH), _const((4, POOL_GC, POOL_GC)), _const((1, POOL_WIDTH))],
        out_specs=[_rows(tp, POOL_WIDTH), _const((4, POOL_GC, POOL_GC)), _const((1, POOL_WIDTH))],
        out_shape=[_sds((T, POOL_WIDTH), BF16), _sds((4, POOL_GC, POOL_GC), F32), _sds((1, POOL_WIDTH), F32)],
        args=(dyp, dyp, diff, w_pool, pool_scale), sem=("arbitrary",), exchanges=exchanges)


_PARTS = ((0, C_Q), (C_Q, C_K), (C_K, C_V), (C_V, C_G), (C_G, IN_WIDTH))


def _inproj_bwd(parts, x2, dx1, w_in_t, g1, exchanges=()):
    T = x2.shape[0]
    tm = min(TM, T)

    def body(du_ref, dq_ref, dk_ref, dv_ref, dgt_ref, x_ref, dx1_ref, w_ref, g_ref, gx_ref, dg_ref, db_ref):
        @pl.when(pl.program_id(0) == 0)
        def _():
            dg_ref[...] = jnp.zeros_like(dg_ref)
            db_ref[...] = jnp.zeros_like(db_ref)

        dh = jnp.zeros((tm, D_MODEL), F32)
        for (lo, hi), p_ref in zip(_PARTS, (du_ref, dq_ref, dk_ref, dv_ref, dgt_ref)):
            part = p_ref[...]
            dh = dh + _dot(part, w_ref[lo:hi, :])
            db_ref[:, lo:hi] += jnp.sum(part.astype(F32), axis=0, keepdims=True)
        dx, dg = _norm_bwd(x_ref[...], g_ref[...], dh)
        gx_ref[...] = dx1_ref[...] + dx
        dg_ref[...] += dg

    return _call(
        body, name="inproj_bwd", grid=(T // tm,),
        in_specs=[_rows(tm, hi - lo) for lo, hi in _PARTS]
        + [_rows(tm, D_MODEL), _rows(tm, D_MODEL), _const((IN_WIDTH, D_MODEL)), _const((1, D_MODEL))],
        out_specs=[_rows(tm, D_MODEL), _const((1, D_MODEL)), _const((1, IN_WIDTH))],
        out_shape=[_sds((T, D_MODEL), F32), _sds((1, D_MODEL), F32), _sds((1, IN_WIDTH), F32)],
        args=(*parts, x2, dx1, w_in_t, g1), sem=("arbitrary",), exchanges=exchanges)


def _dw_in(h, parts, exchanges=()):
    T = h.shape[0]
    tk = min(TM, T)

    def body(h_ref, du_ref, dq_ref, dk_ref, dv_ref, dgt_ref, o_ref):
        @pl.when(pl.program_id(0) == 0)
        def _():
            o_ref[...] = jnp.zeros_like(o_ref)

        hh = h_ref[...]
        for (lo, hi), p_ref in zip(_PARTS, (du_ref, dq_ref, dk_ref, dv_ref, dgt_ref)):
            o_ref[lo:hi, :] += _dot_tn(p_ref[...], hh)

    return _call(
        body, name="dw_in", grid=(T // tk,),
        in_specs=[_rows(tk, D_MODEL)] + [_rows(tk, hi - lo) for lo, hi in _PARTS],
        out_specs=[_const((IN_WIDTH, D_MODEL))],
        out_shape=[_sds((IN_WIDTH, D_MODEL), F32)],
        args=(h, *parts), sem=("arbitrary",), exchanges=exchanges)


def _row_tile(rows, cap=256, mult=16):
    best = None
    for t in range(mult, min(rows, cap) + 1, mult):
        if rows % t == 0:
            best = t
    if best is None:
        raise ValueError("no row tile for %d rows" % rows)
    return best


def _pair_sum(ids, full, got):
    _, r, c = full.shape
    hr = r // 2
    tr = _row_tile(hr)
    nblk = hr // tr

    def body(ids_ref, a_ref, b_ref, s_ref, sb_ref):
        s = a_ref[...] + b_ref[...]
        s_ref[...] = s
        sb_ref[...] = s.astype(BF16)

    out_spec = pl.BlockSpec((None, tr, c), lambda j, i, ids_ref: (j, i, 0))
    return pl.pallas_call(
        body, name="pair_sum_%dx%d" % (r, c),
        grid_spec=pltpu.PrefetchScalarGridSpec(
            num_scalar_prefetch=1, grid=(N_CHIPS, nblk),
            in_specs=[pl.BlockSpec((None, tr, c), lambda j, i, ids_ref: (j, ids_ref[1] * nblk + i, 0)), out_spec],
            out_specs=[out_spec, out_spec]),
        out_shape=[_sds((N_CHIPS, hr, c), F32), _sds((N_CHIPS, hr, c), BF16)],
        compiler_params=_cp("parallel", "parallel"),
    )(ids, full, got)


def _chip_sum(ids, own, got):
    _, hr, c = own.shape
    tr = _row_tile(hr)
    nblk = hr // tr

    def body(ids_ref, a_ref, b_ref, o_ref):
        o_ref[...] = ((a_ref[...] + b_ref[0].astype(F32)) + b_ref[1].astype(F32)) + b_ref[2].astype(F32)

    return pl.pallas_call(
        body, name="chip_sum_%dx%d" % (hr, c),
        grid_spec=pltpu.PrefetchScalarGridSpec(
            num_scalar_prefetch=1, grid=(nblk,),
            in_specs=[pl.BlockSpec((None, tr, c), lambda i, ids_ref: (ids_ref[0], i, 0)),
                      pl.BlockSpec((3, tr, c), lambda i, ids_ref: (0, i, 0))],
            out_specs=pl.BlockSpec((tr, c), lambda i, ids_ref: (ids_ref[1] * nblk + i, 0))),
        out_shape=_sds((2 * hr, c), F32),
        compiler_params=_cp("parallel"),
    )(ids, own, got)


def _sum_blocks(allb, m_per):
    def body(a_ref, o_ref):
        acc = a_ref[0:m_per, :]
        for d in range(1, N_DEV):
            acc = acc + a_ref[d * m_per:(d + 1) * m_per, :]
        o_ref[...] = acc

    return pl.pallas_call(body, name="sum_blocks", out_shape=_sds((m_per, allb.shape[1]), F32))(allb)


def _adamw(w, g, m, v):
    r, c = w.shape
    tr = _row_tile(r, mult=8)

    def body(w_ref, g_ref, m_ref, v_ref, d_ref, nm_ref, nv_ref):
        gg = g_ref[...]
        nm = ADAM_B1 * m_ref[...] + (1.0 - ADAM_B1) * gg
        nv = ADAM_B2 * v_ref[...] + (1.0 - ADAM_B2) * jnp.square(gg)
        m_hat = nm / (1.0 - ADAM_B1 ** ADAM_STEP)
        v_hat = nv / (1.0 - ADAM_B2 ** ADAM_STEP)
        d_ref[...] = -ADAM_LR * (m_hat / (jnp.sqrt(v_hat) + ADAM_EPS) + ADAM_WD * w_ref[...])
        nm_ref[...] = nm
        nv_ref[...] = nv

    spec = _rows(tr, c)
    return pl.pallas_call(
        body, name="adamw_%dx%d" % (r, c), grid=(r // tr,),
        in_specs=[spec] * 4, out_specs=[spec] * 3, out_shape=[_sds((r, c), F32)] * 3,
        compiler_params=_cp("parallel"),
    )(w, g, m, v)


_SMALL = (("w_pool", 4 * POOL_GC * POOL_GC), ("b_in", IN_WIDTH), ("g_mix_pre", D_MODEL), ("g_mix_post", D_MODEL),
          ("g_mlp_pre", D_MODEL), ("g_mlp_post", D_MODEL), ("pool_scale", POOL_WIDTH), ("attn_sinks", N_Q_HEADS),
          ("loss", 1))


def _pack_small(vals):
    parts = []
    for name, size in _SMALL:
        flat = vals[name].reshape(-1).astype(F32)
        padded = -(-size // (8 * LANES)) * (8 * LANES)
        parts.append(jnp.pad(flat, (0, padded - size)).reshape(-1, LANES))
    return jnp.concatenate(parts, axis=0)


def _unpack_small(packed, shapes):
    out, row = {}, 0
    for name, size in _SMALL:
        nrows = -(-size // (8 * LANES)) * 8
        out[name] = packed[row:row + nrows].reshape(-1)[:size].reshape(shapes[name])
        row += nrows
    return out


_BIG = ("w_in", "w_branch_pool", "w_branch_attn", "w_out", "w_up", "w_down")
_ORDER = ("g_mix_pre", "w_in", "b_in", "w_pool", "pool_scale", "attn_sinks", "w_branch_pool", "w_branch_attn",
          "w_out", "g_mix_post", "g_mlp_pre", "w_up", "w_down", "g_mlp_post")


def _stack_rows(slab):
    return slab.reshape(-1, slab.shape[2])


def _step(x2, tgt, seq, shards, small, ids):
    tabs = _rope_tables(seq)
    g1, g2, g3, g4 = (small[n] for n in ("g_mix_pre", "g_mix_post", "g_mlp_pre", "g_mlp_post"))
    sinks = small["attn_sinks"].reshape(N_Q_HEADS)
    w_pool = small["w_pool"].reshape(4, POOL_GC, POOL_GC)
    pool_scale = small["pool_scale"]

    w_in = _stack_rows(_alone("gather_in", _ex_gather([shards["w_in"]]))[0][0])
    (h, u, q, k, v, gate), [(w_bp, w_ba, out_slab)] = _inproj(
        x2, g1, w_in, small["b_in"], tabs, seq,
        exchanges=[_ex_gather([shards[n] for n in ("w_branch_pool", "w_branch_attn", "w_out")])])
    w_out = _stack_rows(out_slab)
    diff, y_pool = _pool_fwd(u, w_pool, pool_scale, seq)
    (y_attn,), [[w_up]] = _attn_fwd(q, k, v, sinks, seq, exchanges=[_ex_gather([shards["w_up"]])])
    (bp, ba, merged, mix, x1, h2), [[down_slab]] = _merge_out(
        y_pool, y_attn, gate, x2, w_bp, w_ba, w_out, g2, g3, exchanges=[_ex_gather([shards["w_down"]])])
    w_down = _stack_rows(down_slab)
    up, act = _mlp_up(h2, w_up)
    dff, dy, loss_acc, dg4 = _mlp_down_loss(act, x1, tgt, w_down, g4)

    dup = _mlp_down_bwd(dff, up, w_down)
    dw_down = _dw("down", act, dff, 1024, 1024)[0].reshape(N_CHIPS, D_FF // N_CHIPS, D_MODEL)
    (dx1, dmix, dg3, dg2), [[got]] = _mlp_up_bwd(dup, dy, x1, mix, w_up, g3, g2, exchanges=[_ex_pair([dw_down])])
    ps_down = _pair_sum(ids, dw_down, got)
    (dw_up,), [[got]] = _dw("up", h2, dup, 1024, 1024, shard_cols=True, exchanges=[_ex_chip([ps_down[1]])])
    half_down = _chip_sum(ids, ps_down[0], got)
    (dbp, dba, dgate, dyp, dya), [[got], [g_down]] = _merge_bwd(
        dmix, gate, bp, ba, w_out, w_bp, w_ba, exchanges=[_ex_pair([dw_up]), _ex_swap([half_down])])
    ps_up = _pair_sum(ids, dw_up, got)
    dw_mix = [_dw("out", merged, dmix, 1024, 1024)[0].reshape(N_CHIPS, D_MODEL // N_CHIPS, D_MODEL),
              _dw_slabs("branch_pool", y_pool, dbp), _dw_slabs("branch_attn", y_attn, dba)]
    (dq, dk, dv, dsink), [[got], gots] = _attn_bwd(
        q, k, v, dya, sinks, tabs, seq, exchanges=[_ex_chip([ps_up[1]]), _ex_pair(dw_mix)])
    half_up = _chip_sum(ids, ps_up[0], got)
    ps_mix = [_pair_sum(ids, d, g) for d, g in zip(dw_mix, gots)]
    (du, dw_pool, dps), [[g_up]] = _pool_bwd(dyp, diff, w_pool, pool_scale, seq, exchanges=[_ex_swap([half_up])])
    parts = (du, dq, dk, dv, dgate)
    gx, dg1, db_in = _inproj_bwd(parts, x2, dx1, w_in, g1)
    little = dict(w_pool=dw_pool, b_in=db_in, g_mix_pre=dg1, g_mix_post=dg2, g_mlp_pre=dg3, g_mlp_post=dg4,
                  pool_scale=dps, attn_sinks=dsink[:, 0], loss=loss_acc[0, 0])
    block = _pack_small(little)
    (dw_in_t,), [gots, [gathered]] = _dw_in(
        h, parts, exchanges=[_ex_chip([p[1] for p in ps_mix]), _ex_allgather(block)])
    half_mix = [_chip_sum(ids, p[0], g) for p, g in zip(ps_mix, gots)]
    total = _sum_blocks(gathered, block.shape[0])

    dw_in = dw_in_t.reshape(N_CHIPS, IN_WIDTH // N_CHIPS, D_MODEL)
    g_mix, [got] = _alone("swap_mix_pair_in", _ex_swap(half_mix), _ex_pair([dw_in]))
    ps_in = _pair_sum(ids, dw_in, got)
    [[got]] = _alone("chip_in", _ex_chip([ps_in[1]]))
    [[g_in]] = _alone("swap_in", _ex_swap([_chip_sum(ids, ps_in[0], got)]))

    grads = dict(w_in=g_in, w_branch_pool=g_mix[1], w_branch_attn=g_mix[2], w_out=g_mix[0], w_up=g_up, w_down=g_down)
    return total, gx, grads


def kernel(x, g_mix_pre, w_in, b_in, w_pool, pool_scale, attn_sinks, w_branch_pool, w_branch_attn, w_out, g_mix_post, g_mlp_pre, w_up, w_down, g_mlp_post, loss_target, m_g_mix_pre, m_w_in, m_b_in, m_w_pool, m_pool_scale, m_attn_sinks, m_w_branch_pool, m_w_branch_attn, m_w_out, m_g_mix_post, m_g_mlp_pre, m_w_up, m_w_down, m_g_mlp_post, v_g_mix_pre, v_w_in, v_b_in, v_w_pool, v_pool_scale, v_attn_sinks, v_w_branch_pool, v_w_branch_attn, v_w_out, v_g_mix_post, v_g_mlp_pre, v_w_up, v_w_down, v_g_mlp_post):
    weights = dict(g_mix_pre=g_mix_pre, w_in=w_in, b_in=b_in, w_pool=w_pool, pool_scale=pool_scale,
                   attn_sinks=attn_sinks, w_branch_pool=w_branch_pool, w_branch_attn=w_branch_attn, w_out=w_out,
                   g_mix_post=g_mix_post, g_mlp_pre=g_mlp_pre, w_up=w_up, w_down=w_down, g_mlp_post=g_mlp_post)
    mom1 = dict(g_mix_pre=m_g_mix_pre, w_in=m_w_in, b_in=m_b_in, w_pool=m_w_pool, pool_scale=m_pool_scale,
                attn_sinks=m_attn_sinks, w_branch_pool=m_w_branch_pool, w_branch_attn=m_w_branch_attn,
                w_out=m_w_out, g_mix_post=m_g_mix_post, g_mlp_pre=m_g_mlp_pre, w_up=m_w_up, w_down=m_w_down,
                g_mlp_post=m_g_mlp_post)
    mom2 = dict(g_mix_pre=v_g_mix_pre, w_in=v_w_in, b_in=v_b_in, w_pool=v_w_pool, pool_scale=v_pool_scale,
                attn_sinks=v_attn_sinks, w_branch_pool=v_w_branch_pool, w_branch_attn=v_w_branch_attn,
                w_out=v_w_out, g_mix_post=v_g_mix_post, g_mlp_pre=v_g_mlp_pre, w_up=v_w_up, w_down=v_w_down,
                g_mlp_post=v_g_mlp_post)
    b_loc, seq, _ = x.shape
    x2 = x.reshape(b_loc * seq, D_MODEL)
    tgt = loss_target.reshape(b_loc * seq, D_MODEL)
    ids = jnp.stack([2 * lax.axis_index("x") + lax.axis_index("y"), lax.axis_index("c")]).astype(jnp.int32)

    def flat(n, a):
        return a[0].T if n == "w_in" else a[0]

    def unflat(n, a):
        return (a.T if n == "w_in" else a)[None]

    shards = {n: flat(n, weights[n]).astype(BF16) for n in _BIG}
    small = {n: weights[n] for n in _ORDER if n not in _BIG}
    total, gx, grads = _step(x2, tgt, seq, shards, small, ids)

    small_shapes = {n: weights[n].shape for n, _ in _SMALL if n != "loss"}
    small_shapes["loss"] = ()
    total = _unpack_small(total, small_shapes)
    loss = total.pop("loss")
    grads.update(total)

    delta, new_m, new_v = {}, {}, {}
    for n in _BIG:
        d, nm, nv = _adamw(flat(n, weights[n]), grads[n], flat(n, mom1[n]), flat(n, mom2[n]))
        grads[n] = unflat(n, grads[n])
        delta[n], new_m[n], new_v[n] = unflat(n, d), unflat(n, nm), unflat(n, nv)
    packed = [_pack_small({**src, "loss": jnp.zeros((), F32)}) for src in (weights, grads, mom1, mom2)]
    for dst, res in zip((delta, new_m, new_v), _adamw(*packed)):
        dst.update({n: a for n, a in _unpack_small(res, small_shapes).items() if n != "loss"})

    return (loss, gx.reshape(x.shape), *[grads[n] for n in _ORDER], *[delta[n] for n in _ORDER],
            *[new_m[n] for n in _ORDER], *[new_v[n] for n in _ORDER])
```

```python
import jax
import jax.numpy as jnp
from jax import lax
from jax.experimental import pallas as pl
from jax.experimental.pallas import tpu as pltpu

F32 = jnp.float32
BF16 = jnp.bfloat16

D_MODEL = 1024
POOL_WINDOWS = (2, 4, 8, 16)
POOL_WIDTH = 512
POOL_GC = 128
HALO = 16
HEAD_DIM = 64
N_Q_HEADS = 8
ATTN_WIDTH = 512
KV_WIDTH = 128
BLOCK = 128
NEG_INF = -1e30
ROPE_THETA = 500000.0
ROT_DIM = 16
GATE_WIDTH = 2048
IN_WIDTH = 3328
D_FF = 4096
EPS = 1e-6
SCALE = HEAD_DIM ** -0.5
C_Q, C_K, C_V, C_G = 512, 1024, 1152, 1280

ADAM_LR, ADAM_B1, ADAM_B2, ADAM_EPS, ADAM_WD, ADAM_STEP = 0.001, 0.9, 0.999, 1e-08, 0.01, 10

N_CHIPS = 4
N_DEV = 8
LANES = 128
TM = 512
TP = 256
VMEM_MB = 56

MESH = pl.DeviceIdType.MESH
ANY = pl.BlockSpec(memory_space=pl.ANY)


def _cp(*sem, vmem=VMEM_MB):
    return pltpu.CompilerParams(dimension_semantics=sem, vmem_limit_bytes=vmem * 1024 * 1024)


def _rows(tile, cols):
    return pl.BlockSpec((tile, cols), lambda i: (i, 0))


def _const(shape):
    nd = len(shape)
    return pl.BlockSpec(shape, lambda i: (0,) * nd)


def _sds(shape, dtype):
    return jax.ShapeDtypeStruct(shape, dtype)


def _dot(a, b):
    return jnp.dot(a, b, preferred_element_type=F32)


def _dot_nt(a, b):
    return lax.dot_general(a, b, (((1,), (1,)), ((), ())), preferred_element_type=F32)


def _dot_tn(a, b):
    return lax.dot_general(a, b, (((0,), (0,)), ((), ())), preferred_element_type=F32)


def _rms(x):
    return lax.rsqrt(jnp.mean(x * x, axis=-1, keepdims=True) + EPS)


def _norm_bwd(x, g, dout):
    r = _rms(x)
    n = x * r
    dn = dout * g
    dx = r * (dn - n * jnp.mean(dn * n, axis=-1, keepdims=True))
    return dx, jnp.sum(dout * n, axis=0, keepdims=True)


def _rot_fwd(t, c, a, bt):
    return t * c + pltpu.roll(t, LANES - 8, 1) * a + pltpu.roll(t, 8, 1) * bt


def _rot_bwd(d, c, a, bt):
    return d * c + pltpu.roll(d * a, 8, 1) + pltpu.roll(d * bt, LANES - 8, 1)


def _rope_tables(seq):
    pos = jnp.arange(seq, dtype=F32)
    inv_freq = ROPE_THETA ** (-jnp.arange(0, ROT_DIM, 2, dtype=F32) / ROT_DIM)
    ang = pos[:, None] * inv_freq[None, :]
    cos, sin = jnp.cos(ang), jnp.sin(ang)
    ones = jnp.ones((seq, HEAD_DIM - ROT_DIM), F32)
    zeros8 = jnp.zeros((seq, 8), F32)
    zrest = jnp.zeros((seq, HEAD_DIM - ROT_DIM), F32)
    c = jnp.concatenate([cos, cos, ones], axis=1)
    a = jnp.concatenate([-sin, zeros8, zrest], axis=1)
    bt = jnp.concatenate([zeros8, sin, zrest], axis=1)
    return tuple(jnp.tile(t, (1, 2)) for t in (c, a, bt))


class _Exchange:
    def __init__(self, inputs, out_shapes, sems, start, finish, aliases=None, middle=None):
        self.inputs, self.out_shapes, self.sems = list(inputs), list(out_shapes), list(sems)
        self.start, self.finish, self.aliases = start, finish, dict(aliases or {})
        self.middle = middle


def _call(body, *, name, grid, in_specs, out_specs, out_shape, args, scratch=(), sem=(), exchanges=()):
    in_specs, out_specs, out_shape, scratch = list(in_specs), list(out_specs), list(out_shape), list(scratch)
    if not exchanges:
        return pl.pallas_call(body, name=name, grid=grid, in_specs=in_specs, out_specs=out_specs,
                              out_shape=out_shape, scratch_shapes=scratch, compiler_params=_cp(*sem))(*args)
    n_in, n_out, n_scr = len(in_specs), len(out_specs), len(scratch)
    x_in = [a for ex in exchanges for a in ex.inputs]
    x_out = [s for ex in exchanges for s in ex.out_shapes]
    x_sem = [s for ex in exchanges for s in ex.sems]
    aliases, i_off, o_off = {}, n_in, n_out
    for ex in exchanges:
        for i, o in ex.aliases.items():
            aliases[i_off + i] = o_off + o
        i_off += len(ex.inputs)
        o_off += len(ex.out_shapes)

    def split(flat):
        out, pos = [], 0
        for ex, n in zip(exchanges, flat[1]):
            out.append(flat[0][pos:pos + n])
            pos += n
        return out

    def carrier(*refs):
        pos = 0
        groups = []
        for n in (n_in, len(x_in), n_out, len(x_out), n_scr, len(x_sem)):
            groups.append(refs[pos:pos + n])
            pos += n
        ins, xin, outs, xout, scr, xsem = groups
        xin = split((xin, [len(ex.inputs) for ex in exchanges]))
        xout = split((xout, [len(ex.out_shapes) for ex in exchanges]))
        xsem = split((xsem, [len(ex.sems) for ex in exchanges]))
        first = pl.program_id(0) == 0
        last = pl.program_id(0) == grid[0] - 1
        for d in range(1, len(grid)):
            first = jnp.logical_and(first, pl.program_id(d) == 0)
            last = jnp.logical_and(last, pl.program_id(d) == grid[d] - 1)

        @pl.when(first)
        def _():
            for ex, i, o, s in zip(exchanges, xin, xout, xsem):
                ex.start(i, o, s)

        if any(ex.middle for ex in exchanges):
            half = pl.program_id(0) == grid[0] // 2
            for d in range(1, len(grid)):
                half = jnp.logical_and(half, pl.program_id(d) == 0)

            @pl.when(half)
            def _():
                for ex, i, o, s in zip(exchanges, xin, xout, xsem):
                    if ex.middle:
                        ex.middle(i, o, s)

        body(*ins, *outs, *scr)

        @pl.when(last)
        def _():
            for ex, i, o, s in zip(exchanges, xin, xout, xsem):
                ex.finish(i, o, s)

    res = pl.pallas_call(
        carrier, name=name, grid=grid, in_specs=in_specs + [ANY] * len(x_in),
        out_specs=out_specs + [ANY] * len(x_out), out_shape=out_shape + x_out,
        scratch_shapes=scratch + x_sem, input_output_aliases=aliases,
        compiler_params=_cp(*(["arbitrary"] * len(grid))),
    )(*args, *x_in)
    return res[:n_out], split((res[n_out:], [len(ex.out_shapes) for ex in exchanges]))


def _alone(name, *exchanges):
    n_in = [len(ex.inputs) for ex in exchanges]
    n_out = [len(ex.out_shapes) for ex in exchanges]
    n_sem = [len(ex.sems) for ex in exchanges]
    aliases, i_off, o_off = {}, 0, 0
    for ex in exchanges:
        for i, o in ex.aliases.items():
            aliases[i_off + i] = o_off + o
        i_off += len(ex.inputs)
        o_off += len(ex.out_shapes)

    def split(flat, counts):
        out, pos = [], 0
        for n in counts:
            out.append(flat[pos:pos + n])
            pos += n
        return out

    def body(*refs):
        ins, outs, sems = split(refs, [sum(n_in), sum(n_out), sum(n_sem)])
        groups = list(zip(exchanges, split(ins, n_in), split(outs, n_out), split(sems, n_sem)))
        for ex, i, o, s in groups:
            ex.start(i, o, s)
        for ex, i, o, s in groups:
            if ex.middle:
                ex.middle(i, o, s)
        for ex, i, o, s in groups:
            ex.finish(i, o, s)

    res = pl.pallas_call(
        body, name=name, in_specs=[ANY] * sum(n_in), out_specs=[ANY] * sum(n_out),
        out_shape=[s for ex in exchanges for s in ex.out_shapes],
        scratch_shapes=[s for ex in exchanges for s in ex.sems], input_output_aliases=aliases,
    )(*[a for ex in exchanges for a in ex.inputs])
    return split(res, n_out)


def _place():
    x, y, c = lax.axis_index("x"), lax.axis_index("y"), lax.axis_index("c")
    chips = [(1 - x, y), (x, 1 - y), (1 - x, 1 - y)]
    return x, y, c, chips


def _remote(src, dst, send, recv, to):
    return pltpu.make_async_remote_copy(src_ref=src, dst_ref=dst, send_sem=send, recv_sem=recv,
                                        device_id=to, device_id_type=MESH)


def _ex_gather(shards):
    nw = len(shards)
    hrs = [s.shape[0] // 2 for s in shards]

    def copies(ins, outs, sems):
        s1, r1, s2, r2, fs, fr, lsem = sems
        x, y, c, _ = _place()
        me, xn, yn, dg = (x, y), (1 - x, y), (x, 1 - y), (1 - x, 1 - y)
        nbr = (xn, yn)
        sibling = (x, y, 1 - c)

        def piece(w, chip, core, part=None):
            hr = hrs[w]
            rows = pl.ds(core * hr, hr) if part is None else pl.ds(core * hr + part * (hr // 2), hr // 2)
            return outs[w].at[2 * chip[0] + chip[1], rows]

        def local(w):
            return pltpu.make_async_copy(ins[w], outs[w].at[2 * x + y], lsem.at[w])

        def first(w, k):
            return _remote(ins[w].at[pl.ds(c * hrs[w], hrs[w])], piece(w, me, c), s1.at[w, k], r1.at[w, k],
                           (*nbr[k], c))

        def landed(w, k):
            return _remote(piece(w, nbr[k], c), piece(w, nbr[k], c), s1.at[w, k], r1.at[w, k], (*nbr[k], c))

        def onward(w, k):
            return _remote(piece(w, nbr[k], c, k), piece(w, nbr[k], c, k), s2.at[w, k], r2.at[w, k],
                           (*nbr[1 - k], c))

        def arrived(w, k):
            return _remote(piece(w, dg, c, k), piece(w, dg, c, k), s2.at[w, k], r2.at[w, k], (*nbr[1 - k], c))

        def passed(w, j):
            chip = (xn, yn, dg)[j]
            return _remote(piece(w, chip, c), piece(w, chip, c), fs.at[w, j], fr.at[w, j], sibling)

        def handed(w, j):
            chip = (xn, yn, dg)[j]
            return _remote(piece(w, chip, 1 - c), piece(w, chip, 1 - c), fs.at[w, j], fr.at[w, j], sibling)

        return local, first, landed, onward, arrived, passed, handed

    def start(ins, outs, sems):
        local, first = copies(ins, outs, sems)[:2]
        for w in range(nw):
            local(w).start()
            for k in range(2):
                first(w, k).start()

    def middle(ins, outs, sems):
        _, _, landed, onward, _, passed, _ = copies(ins, outs, sems)
        for w in range(nw):
            for k in range(2):
                landed(w, k).wait_recv()
                onward(w, k).start()
                passed(w, k).start()

    def finish(ins, outs, sems):
        local, first, _, onward, arrived, passed, handed = copies(ins, outs, sems)
        for w in range(nw):
            for k in range(2):
                arrived(w, k).wait_recv()
            passed(w, 2).start()
        for w in range(nw):
            for j in range(3):
                handed(w, j).wait_recv()
        for w in range(nw):
            for k in range(2):
                first(w, k).wait_send()
                onward(w, k).wait_send()
            for j in range(3):
                passed(w, j).wait_send()
            local(w).wait()

    return _Exchange(shards, [_sds((N_CHIPS,) + s.shape, s.dtype) for s in shards],
                     [pltpu.SemaphoreType.DMA((nw, 2))] * 4 + [pltpu.SemaphoreType.DMA((nw, 3))] * 2
                     + [pltpu.SemaphoreType.DMA((nw,))], start, finish, middle=middle)


def _ex_pair(grads):
    nw = len(grads)

    def copies(ins, outs, sems):
        x, y, c, _ = _place()
        out = []
        for w in range(nw):
            hr = grads[w].shape[1] // 2
            out.append(_remote(ins[w].at[:, pl.ds((1 - c) * hr, hr)], outs[w], sems[0].at[w], sems[1].at[w],
                               (x, y, 1 - c)))
        return out

    def start(ins, outs, sems):
        for cp in copies(ins, outs, sems):
            cp.start()

    def finish(ins, outs, sems):
        for cp in copies(ins, outs, sems):
            cp.wait()

    return _Exchange(grads, [_sds((N_CHIPS, g.shape[1] // 2, g.shape[2]), F32) for g in grads],
                     [pltpu.SemaphoreType.DMA((nw,))] * 2, start, finish)


def _ex_chip(pieces):
    nw = len(pieces)

    def copies(ins, outs, sems):
        x, y, c, chips = _place()
        return [_remote(ins[w].at[2 * cx + cy], outs[w].at[k], sems[0].at[w, k], sems[1].at[w, k], (cx, cy, c))
                for w in range(nw) for k, (cx, cy) in enumerate(chips)]

    def start(ins, outs, sems):
        for cp in copies(ins, outs, sems):
            cp.start()

    def finish(ins, outs, sems):
        for cp in copies(ins, outs, sems):
            cp.wait()

    return _Exchange(pieces, [_sds((3,) + p.shape[1:], BF16) for p in pieces],
                     [pltpu.SemaphoreType.DMA((nw, 3))] * 2, start, finish)


def _ex_swap(fulls):
    nw = len(fulls)

    def start(ins, outs, sems):
        x, y, c, _ = _place()
        for w in range(nw):
            hr = fulls[w].shape[0] // 2
            mine = pl.ds(c * hr, hr)
            _remote(ins[w].at[mine], outs[w].at[mine], sems[0].at[w], sems[1].at[w], (x, y, 1 - c)).start()

    def finish(ins, outs, sems):
        x, y, c, _ = _place()
        for w in range(nw):
            hr = fulls[w].shape[0] // 2
            mine, theirs = pl.ds(c * hr, hr), pl.ds((1 - c) * hr, hr)
            _remote(ins[w].at[mine], outs[w].at[mine], sems[0].at[w], sems[1].at[w], (x, y, 1 - c)).wait_send()
            _remote(ins[w].at[theirs], outs[w].at[theirs], sems[0].at[w], sems[1].at[w], (x, y, 1 - c)).wait_recv()

    return _Exchange(fulls, [_sds(f.shape, F32) for f in fulls], [pltpu.SemaphoreType.DMA((nw,))] * 2,
                     start, finish, aliases={w: w for w in range(nw)})


def _ex_allgather(block):
    m_per, n = block.shape

    def copies(ins, outs, sems):
        send, recv, lsem = sems
        x, y, c, chips = _place()
        me, sibling = (x, y, c), (x, y, 1 - c)

        def rows(px, py, pc):
            return outs[0].at[pl.ds((4 * px + 2 * py + pc) * m_per, m_per), :]

        def copy(k, blk, to, src=None):
            return _remote(rows(*blk) if src is None else src, rows(*blk), send.at[k], recv.at[k], to)

        def mine():
            return pltpu.make_async_copy(ins[0], rows(*me), lsem.at[0])

        def first(k):
            return copy(k, me, sibling if k == 0 else (*chips[k - 1], c), src=ins[0])

        def passed(j):
            return copy(4 + j, (*chips[j], c), sibling)

        def landed(j):
            return copy(1 + j, (*chips[j], c), me)

        def handed(k):
            return copy(0, sibling, me) if k == 0 else copy(3 + k, (*chips[k - 1], 1 - c), me)

        return mine, first, passed, landed, handed

    def start(ins, outs, sems):
        mine, first, _, _, _ = copies(ins, outs, sems)
        mine().start()
        for k in range(4):
            first(k).start()

    def finish(ins, outs, sems):
        mine, first, passed, landed, handed = copies(ins, outs, sems)
        sent = []
        for j in range(3):
            landed(j).wait_recv()
            cp = passed(j)
            cp.start()
            sent.append(cp)
        for k in range(4):
            handed(k).wait_recv()
        for k in range(4):
            first(k).wait_send()
        for cp in sent:
            cp.wait_send()
        mine().wait()

    return _Exchange([block], [_sds((N_DEV * m_per, n), F32)],
                     [pltpu.SemaphoreType.DMA((7,)), pltpu.SemaphoreType.DMA((7,)), pltpu.SemaphoreType.DMA((1,))],
                     start, finish)


def _inproj(x2, g1, w_in_t, b_in, tabs, seq, exchanges=()):
    T = x2.shape[0]
    tm = min(TM, seq)
    nseq = seq // tm

    def body(x_ref, g_ref, w_ref, b_ref, c_ref, a_ref, bt_ref, h_ref, u_ref, q_ref, k_ref, v_ref, gate_ref):
        x = x_ref[...]
        h = (x * _rms(x) * g_ref[...]).astype(BF16)
        h_ref[...] = h

        def proj(lo, hi):
            return _dot_nt(h, w_ref[lo:hi, :]) + b_ref[:, lo:hi]

        c, a, bt = c_ref[...], a_ref[...], bt_ref[...]
        u_ref[...] = proj(0, C_Q)
        q = proj(C_Q, C_K)
        for p in range(4):
            sl = slice(LANES * p, LANES * (p + 1))
            q_ref[:, sl] = (_rot_fwd(q[:, sl], c, a, bt) * SCALE).astype(BF16)
        kv = proj(C_K, C_G)
        k_ref[...] = _rot_fwd(kv[:, :KV_WIDTH], c, a, bt).astype(BF16)
        v_ref[...] = kv[:, KV_WIDTH:].astype(BF16)
        for j in range(2):
            lo = C_G + D_MODEL * j
            gate_ref[:, D_MODEL * j:D_MODEL * (j + 1)] = jax.nn.sigmoid(proj(lo, lo + D_MODEL)).astype(BF16)

    tab = pl.BlockSpec((tm, LANES), lambda i: (i % nseq, 0))
    return _call(
        body, name="inproj", grid=(T // tm,),
        in_specs=[_rows(tm, D_MODEL), _const((1, D_MODEL)), _const((IN_WIDTH, D_MODEL)), _const((1, IN_WIDTH)),
                  tab, tab, tab],
        out_specs=[_rows(tm, D_MODEL), _rows(tm, POOL_WIDTH), _rows(tm, ATTN_WIDTH), _rows(tm, KV_WIDTH),
                   _rows(tm, KV_WIDTH), _rows(tm, GATE_WIDTH)],
        out_shape=[_sds((T, D_MODEL), BF16), _sds((T, POOL_WIDTH), F32), _sds((T, ATTN_WIDTH), BF16),
                   _sds((T, KV_WIDTH), BF16), _sds((T, KV_WIDTH), BF16), _sds((T, GATE_WIDTH), BF16)],
        args=(x2, g1, w_in_t, b_in, *tabs), sem=("parallel",), exchanges=exchanges)


def _inv_count(pos, w):
    return 1.0 / jnp.minimum(pos + 1, w).astype(F32)


def _pool_fwd(u, w_pool, pool_scale, seq):
    T = u.shape[0]
    tp = min(TP, seq)
    nseq = seq // tp
    per = tp // HALO

    def body(u_ref, prev_ref, w_ref, s_ref, diff_ref, y_ref):
        i = pl.program_id(0)
        first = (i % nseq) == 0
        prev = jnp.where(first, 0.0, prev_ref[...])
        ext = jnp.concatenate([prev, u_ref[...]], axis=0)
        pos = (i % nseq) * tp + lax.broadcasted_iota(jnp.int32, (tp, 1), 0)
        for gi, w in enumerate(POOL_WINDOWS):
            sl = slice(POOL_GC * gi, POOL_GC * (gi + 1))
            xg = ext[:, sl]
            s = xg
            sh = 1
            while sh < w:
                s = s + pltpu.roll(s, sh, 0)
                sh *= 2
            pooled = s[HALO:] * _inv_count(pos, w)
            diff = (pooled - xg[HALO:]).astype(BF16)
            diff_ref[:, sl] = diff
            mixed = _dot(diff, w_ref[gi].astype(BF16))
            y_ref[:, sl] = (mixed * s_ref[:, sl]).astype(BF16)

    return _call(
        body, name="pool_fwd", grid=(T // tp,),
        in_specs=[_rows(tp, POOL_WIDTH),
                  pl.BlockSpec((HALO, POOL_WIDTH), lambda i: (jnp.maximum(i * per - 1, 0), 0)),
                  _const((4, POOL_GC, POOL_GC)), _const((1, POOL_WIDTH))],
        out_specs=[_rows(tp, POOL_WIDTH), _rows(tp, POOL_WIDTH)],
        out_shape=[_sds((T, POOL_WIDTH), BF16), _sds((T, POOL_WIDTH), BF16)],
        args=(u, u, w_pool, pool_scale), sem=("parallel",))


GROUP = 4
GROWS = GROUP * BLOCK


def _attn_masks(n):
    qi = lax.broadcasted_iota(jnp.int32, (GROWS, 2 * BLOCK), 0) % BLOCK
    kj = lax.broadcasted_iota(jnp.int32, (GROWS, 2 * BLOCK), 1)
    rel = qi + BLOCK - kj
    valid = (rel >= 0) & (rel < BLOCK) & (kj >= jnp.where(n > 0, 0, BLOCK))
    lo = lax.broadcasted_iota(jnp.int32, (BLOCK, LANES), 1) < HEAD_DIM
    return valid, lo


def _stack_heads(ref, h, lo):
    keep = lo if h == 0 else jnp.logical_not(lo)
    pieces = []
    for p in (2 * h, 2 * h + 1):
        xp = ref[:, LANES * p:LANES * (p + 1)].astype(F32)
        for e in range(2):
            t = xp if e == h else pltpu.roll(xp, HEAD_DIM, 1)
            pieces.append(jnp.where(keep, t, 0.0).astype(BF16))
    return jnp.concatenate(pieces, axis=0)


def _unstack_heads(stacked, h, lo):
    pairs = []
    for j in range(2):
        parts = []
        for e in range(2):
            t = stacked[BLOCK * (2 * j + e):BLOCK * (2 * j + e + 1)]
            parts.append(t if e == h else pltpu.roll(t, HEAD_DIM, 1))
        pairs.append(jnp.where(lo, parts[0], parts[1]))
    return pairs


def _sink_rows(sink_ref, h):
    head = lax.broadcasted_iota(jnp.int32, (GROWS, 1), 0) // BLOCK
    col = jnp.zeros((GROWS, 1), F32) + sink_ref[GROUP * h]
    for g in range(1, GROUP):
        col = jnp.where(head == g, sink_ref[GROUP * h + g], col)
    return col


def _group_probs(qs, kk, valid, sink):
    s = jnp.where(valid, _dot_nt(qs, kk), NEG_INF)
    m = jnp.maximum(jnp.max(s, axis=1, keepdims=True), sink)
    ex = jnp.exp(s - m)
    es = jnp.exp(sink - m)
    inv = 1.0 / (jnp.sum(ex, axis=1, keepdims=True) + es)
    return ex * inv, es * inv


def _attn_fwd(q, k, v, sinks, seq, exchanges=()):
    T = q.shape[0]
    nb = seq // BLOCK

    def body(sink_ref, q_ref, kp_ref, kc_ref, vp_ref, vc_ref, o_ref):
        n = pl.program_id(0) % nb
        kk = jnp.concatenate([kp_ref[...], kc_ref[...]], axis=0)
        vv = jnp.concatenate([vp_ref[...], vc_ref[...]], axis=0)
        valid, lo = _attn_masks(n)
        for h in range(2):
            qs = _stack_heads(q_ref, h, lo)
            pr, _ = _group_probs(qs, kk, valid, _sink_rows(sink_ref, h))
            o = _dot(pr.astype(BF16), vv)
            for j, pair in enumerate(_unstack_heads(o, h, lo)):
                p = 2 * h + j
                o_ref[:, LANES * p:LANES * (p + 1)] = pair.astype(BF16)

    cur = lambda i: (i, 0)
    prv = lambda i: (jnp.where(i % nb == 0, i, i - 1), 0)
    return _call(
        body, name="attn_fwd", grid=(T // BLOCK,),
        in_specs=[pl.BlockSpec(memory_space=pltpu.SMEM),
                  pl.BlockSpec((BLOCK, ATTN_WIDTH), cur),
                  pl.BlockSpec((BLOCK, KV_WIDTH), prv), pl.BlockSpec((BLOCK, KV_WIDTH), cur),
                  pl.BlockSpec((BLOCK, KV_WIDTH), prv), pl.BlockSpec((BLOCK, KV_WIDTH), cur)],
        out_specs=[pl.BlockSpec((BLOCK, ATTN_WIDTH), cur)],
        out_shape=[_sds((T, ATTN_WIDTH), BF16)],
        args=(sinks, q, k, k, v, v), sem=("parallel",), exchanges=exchanges)


def _merge_out(y_pool, y_attn, gate, x2, w_bp, w_ba, w_out, g2, g3, exchanges=()):
    T = x2.shape[0]
    tm = min(TM, T)

    def body(yp_ref, ya_ref, gate_ref, x_ref, wbp_ref, wba_ref, wo_ref, g2_ref, g3_ref,
             bp_ref, ba_ref, mg_ref, mix_ref, x1_ref, h2_ref):
        yp, ya = yp_ref[...], ya_ref[...]
        bp = jnp.concatenate([_dot(yp, wbp_ref[j]) for j in range(N_CHIPS)], axis=1)
        ba = jnp.concatenate([_dot(ya, wba_ref[j]) for j in range(N_CHIPS)], axis=1)
        bp_ref[...] = bp.astype(BF16)
        ba_ref[...] = ba.astype(BF16)
        merged = (gate_ref[:, :D_MODEL].astype(F32) * bp + gate_ref[:, D_MODEL:].astype(F32) * ba).astype(BF16)
        mg_ref[...] = merged
        mix = _dot(merged, wo_ref[...])
        mix_ref[...] = mix
        x1 = x_ref[...] + mix * _rms(mix) * g2_ref[...]
        x1_ref[...] = x1
        h2_ref[...] = (x1 * _rms(x1) * g3_ref[...]).astype(BF16)

    return _call(
        body, name="merge_out", grid=(T // tm,),
        in_specs=[_rows(tm, POOL_WIDTH), _rows(tm, ATTN_WIDTH), _rows(tm, GATE_WIDTH), _rows(tm, D_MODEL),
                  _const(w_bp.shape), _const(w_ba.shape), _const((D_MODEL, D_MODEL)),
                  _const((1, D_MODEL)), _const((1, D_MODEL))],
        out_specs=[_rows(tm, D_MODEL)] * 6,
        out_shape=[_sds((T, D_MODEL), BF16), _sds((T, D_MODEL), BF16), _sds((T, D_MODEL), BF16),
                   _sds((T, D_MODEL), F32), _sds((T, D_MODEL), F32), _sds((T, D_MODEL), BF16)],
        args=(y_pool, y_attn, gate, x2, w_bp, w_ba, w_out, g2, g3), sem=("parallel",), exchanges=exchanges)


def _mlp_up(h2, w_up):
    T = h2.shape[0]
    tm = min(TM, T)

    def body(h_ref, w_ref, up_ref, a_ref):
        h = h_ref[...]
        for j in range(N_CHIPS):
            sl = slice(D_MODEL * j, D_MODEL * (j + 1))
            up = _dot(h, w_ref[j])
            up_ref[:, sl] = up.astype(BF16)
            a_ref[:, sl] = jnp.square(jnp.maximum(up, 0.0)).astype(BF16)

    return _call(
        body, name="mlp_up", grid=(T // tm,),
        in_specs=[_rows(tm, D_MODEL), _const((N_CHIPS, D_MODEL, D_MODEL))],
        out_specs=[_rows(tm, D_FF), _rows(tm, D_FF)],
        out_shape=[_sds((T, D_FF), BF16), _sds((T, D_FF), BF16)],
        args=(h2, w_up), sem=("parallel",))


def _mlp_down_loss(a, x1, tgt, w_down, g4):
    T = a.shape[0]
    tm = min(TM, T)

    def body(a_ref, x1_ref, t_ref, w_ref, g_ref, dff_ref, dy_ref, loss_ref, dg_ref):
        @pl.when(pl.program_id(0) == 0)
        def _():
            loss_ref[...] = jnp.zeros_like(loss_ref)
            dg_ref[...] = jnp.zeros_like(dg_ref)

        ff = _dot(a_ref[...], w_ref[...])
        g = g_ref[...]
        err = x1_ref[...] + ff * _rms(ff) * g - t_ref[...]
        loss_ref[...] += jnp.sum(err * err) * (0.5 / D_MODEL)
        dy = err * (1.0 / D_MODEL)
        dy_ref[...] = dy
        dff, dg = _norm_bwd(ff, g, dy)
        dff_ref[...] = dff.astype(BF16)
        dg_ref[...] += dg

    return _call(
        body, name="mlp_down_loss", grid=(T // tm,),
        in_specs=[_rows(tm, D_FF), _rows(tm, D_MODEL), _rows(tm, D_MODEL), _const((D_FF, D_MODEL)),
                  _const((1, D_MODEL))],
        out_specs=[_rows(tm, D_MODEL), _rows(tm, D_MODEL), _const((8, LANES)), _const((1, D_MODEL))],
        out_shape=[_sds((T, D_MODEL), BF16), _sds((T, D_MODEL), F32), _sds((8, LANES), F32),
                   _sds((1, D_MODEL), F32)],
        args=(a, x1, tgt, w_down, g4), sem=("arbitrary",))


def _mlp_down_bwd(dff, up, w_down):
    T = dff.shape[0]
    tm = min(TM, T)

    def body(d_ref, up_ref, w_ref, dup_ref):
        d = d_ref[...]
        for j in range(D_FF // D_MODEL):
            sl = slice(D_MODEL * j, D_MODEL * (j + 1))
            da = _dot_nt(d, w_ref[sl, :])
            dup_ref[:, sl] = (da * (2.0 * jnp.maximum(up_ref[:, sl].astype(F32), 0.0))).astype(BF16)

    return _call(
        body, name="mlp_down_bwd", grid=(T // tm,),
        in_specs=[_rows(tm, D_MODEL), _rows(tm, D_FF), _const((D_FF, D_MODEL))],
        out_specs=[_rows(tm, D_FF)],
        out_shape=[_sds((T, D_FF), BF16)],
        args=(dff, up, w_down), sem=("parallel",))[0]


def _mlp_up_bwd(dup, dy, x1, mix, w_up, g3, g2, exchanges=()):
    T = dup.shape[0]
    tm = min(TM, T)

    def body(dup_ref, dy_ref, x1_ref, mix_ref, w_ref, g3_ref, g2_ref, dx1_ref, dmix_ref, dg3_ref, dg2_ref):
        @pl.when(pl.program_id(0) == 0)
        def _():
            dg3_ref[...] = jnp.zeros_like(dg3_ref)
            dg2_ref[...] = jnp.zeros_like(dg2_ref)

        dh2 = _dot_nt(dup_ref[:, :D_MODEL], w_ref[0])
        for j in range(1, N_CHIPS):
            dh2 = dh2 + _dot_nt(dup_ref[:, D_MODEL * j:D_MODEL * (j + 1)], w_ref[j])
        dx, dg3 = _norm_bwd(x1_ref[...], g3_ref[...], dh2)
        dx1 = dy_ref[...] + dx
        dx1_ref[...] = dx1
        dg3_ref[...] += dg3
        dmix, dg2 = _norm_bwd(mix_ref[...], g2_ref[...], dx1)
        dmix_ref[...] = dmix.astype(BF16)
        dg2_ref[...] += dg2

    return _call(
        body, name="mlp_up_bwd", grid=(T // tm,),
        in_specs=[_rows(tm, D_FF), _rows(tm, D_MODEL), _rows(tm, D_MODEL), _rows(tm, D_MODEL),
                  _const((N_CHIPS, D_MODEL, D_MODEL)), _const((1, D_MODEL)), _const((1, D_MODEL))],
        out_specs=[_rows(tm, D_MODEL), _rows(tm, D_MODEL), _const((1, D_MODEL)), _const((1, D_MODEL))],
        out_shape=[_sds((T, D_MODEL), F32), _sds((T, D_MODEL), BF16), _sds((1, D_MODEL), F32),
                   _sds((1, D_MODEL), F32)],
        args=(dup, dy, x1, mix, w_up, g3, g2), sem=("arbitrary",), exchanges=exchanges)


def _dw(tag, a, g, ta, tn, shard_cols=False, exchanges=()):
    T, ka = a.shape
    n = g.shape[1]
    tk = min(TM, T)
    nk = T // tk

    def body(a_ref, g_ref, o_ref):
        @pl.when(pl.program_id(2) == 0)
        def _():
            o_ref[...] = jnp.zeros_like(o_ref)

        o_ref[...] += _dot_tn(a_ref[...], g_ref[...])

    if shard_cols:
        per = (n // N_CHIPS) // tn
        out_spec = pl.BlockSpec((None, ta, tn), lambda i, j, k: (j // per, i, j % per))
        out_shape = _sds((N_CHIPS, ka, n // N_CHIPS), F32)
    else:
        out_spec = pl.BlockSpec((ta, tn), lambda i, j, k: (i, j))
        out_shape = _sds((ka, n), F32)
    return _call(
        body, name="dw_" + tag, grid=(ka // ta, n // tn, nk),
        in_specs=[pl.BlockSpec((tk, ta), lambda i, j, k: (k, i)), pl.BlockSpec((tk, tn), lambda i, j, k: (k, j))],
        out_specs=[out_spec], out_shape=[out_shape],
        args=(a, g), sem=("parallel", "parallel", "arbitrary"), exchanges=exchanges)


def _dw_slabs(tag, a, g):
    T, ka = a.shape
    n = g.shape[1]
    c = n // N_CHIPS
    tk = min(TM, T)

    def body(a_ref, g_ref, o_ref):
        @pl.when(pl.program_id(0) == 0)
        def _():
            o_ref[...] = jnp.zeros_like(o_ref)

        res = _dot_tn(a_ref[...], g_ref[...])
        for j in range(N_CHIPS):
            o_ref[j] += res[:, c * j:c * (j + 1)]

    return _call(
        body, name="dw_" + tag, grid=(T // tk,),
        in_specs=[_rows(tk, ka), _rows(tk, n)],
        out_specs=[_const((N_CHIPS, ka, c))], out_shape=[_sds((N_CHIPS, ka, c), F32)],
        args=(a, g), sem=("arbitrary",))[0]


def _merge_bwd(dmix, gate, bp, ba, w_out, w_bp, w_ba, exchanges=()):
    T = dmix.shape[0]
    tm = min(TM, T)

    def body(dmix_ref, gate_ref, bp_ref, ba_ref, wo_ref, wbp_ref, wba_ref,
             dbp_ref, dba_ref, dgate_ref, dyp_ref, dya_ref):
        dm = _dot_nt(dmix_ref[...], wo_ref[...])
        for j, (b_ref, db_ref, w_ref, dy_ref) in enumerate(
                ((bp_ref, dbp_ref, wbp_ref, dyp_ref), (ba_ref, dba_ref, wba_ref, dya_ref))):
            sl = slice(D_MODEL * j, D_MODEL * (j + 1))
            gt = gate_ref[:, sl].astype(F32)
            db = (dm * gt).astype(BF16)
            db_ref[...] = db
            dgate_ref[:, sl] = (dm * b_ref[...].astype(F32) * gt * (1.0 - gt)).astype(BF16)
            cw = D_MODEL // N_CHIPS
            dy = _dot_nt(db[:, :cw], w_ref[0])
            for c in range(1, N_CHIPS):
                dy = dy + _dot_nt(db[:, cw * c:cw * (c + 1)], w_ref[c])
            dy_ref[...] = dy.astype(dy_ref.dtype)

    return _call(
        body, name="merge_bwd", grid=(T // tm,),
        in_specs=[_rows(tm, D_MODEL), _rows(tm, GATE_WIDTH), _rows(tm, D_MODEL), _rows(tm, D_MODEL),
                  _const((D_MODEL, D_MODEL)), _const(w_bp.shape), _const(w_ba.shape)],
        out_specs=[_rows(tm, D_MODEL), _rows(tm, D_MODEL), _rows(tm, GATE_WIDTH), _rows(tm, POOL_WIDTH),
                   _rows(tm, ATTN_WIDTH)],
        out_shape=[_sds((T, D_MODEL), BF16), _sds((T, D_MODEL), BF16), _sds((T, GATE_WIDTH), BF16),
                   _sds((T, POOL_WIDTH), F32), _sds((T, ATTN_WIDTH), BF16)],
        args=(dmix, gate, bp, ba, w_out, w_bp, w_ba), sem=("parallel",), exchanges=exchanges)


def _attn_bwd(q, k, v, do, sinks, tabs, seq, exchanges=()):
    T = q.shape[0]
    nb = seq // BLOCK
    steps = nb + 1

    def body(sink_ref, q_ref, do_ref, kp_ref, kc_ref, vp_ref, vc_ref, c_ref, a_ref, bt_ref, cp_ref, ap_ref, btp_ref,
             dq_ref, dk_ref, dv_ref, dsink_ref, ck_ref, cv_ref):
        i = pl.program_id(0)
        n = i % steps

        @pl.when(i == 0)
        def _():
            dsink_ref[...] = jnp.zeros_like(dsink_ref)

        @pl.when(n == 0)
        def _():
            ck_ref[...] = jnp.zeros_like(ck_ref)
            cv_ref[...] = jnp.zeros_like(cv_ref)

        @pl.when(n < nb)
        def _():
            kk = jnp.concatenate([kp_ref[...], kc_ref[...]], axis=0)
            vv = jnp.concatenate([vp_ref[...], vc_ref[...]], axis=0)
            valid, lo = _attn_masks(n)
            dk_acc = jnp.zeros((2 * BLOCK, KV_WIDTH), F32)
            dv_acc = jnp.zeros((2 * BLOCK, KV_WIDTH), F32)
            for h in range(2):
                qs = _stack_heads(q_ref, h, lo)
                dos = _stack_heads(do_ref, h, lo)
                pr, ps = _group_probs(qs, kk, valid, _sink_rows(sink_ref, h))
                dp = _dot_nt(dos, vv)
                delta = jnp.sum(pr * dp, axis=1, keepdims=True)
                ds = (pr * (dp - delta)).astype(BF16)
                dsk = ps * delta
                for g in range(GROUP):
                    idx = GROUP * h + g
                    dsink_ref[idx:idx + 1, :] += jnp.zeros((1, LANES), F32) - jnp.sum(dsk[BLOCK * g:BLOCK * (g + 1)])
                dk_acc = dk_acc + _dot_tn(ds, qs)
                dv_acc = dv_acc + _dot_tn(pr.astype(BF16), dos)
                for j, pair in enumerate(_unstack_heads(_dot(ds, kk) * SCALE, h, lo)):
                    sl = slice(LANES * (2 * h + j), LANES * (2 * h + j + 1))
                    dq_ref[:, sl] = _rot_bwd(pair, c_ref[...], a_ref[...], bt_ref[...]).astype(BF16)
            fin_k = ck_ref[...] + dk_acc[:BLOCK]
            dk_ref[...] = _rot_bwd(fin_k, cp_ref[...], ap_ref[...], btp_ref[...]).astype(BF16)
            dv_ref[...] = (cv_ref[...] + dv_acc[:BLOCK]).astype(BF16)
            ck_ref[...] = dk_acc[BLOCK:]
            cv_ref[...] = dv_acc[BLOCK:]

        @pl.when(n == nb)
        def _():
            dk_ref[...] = _rot_bwd(ck_ref[...], cp_ref[...], ap_ref[...], btp_ref[...]).astype(BF16)
            dv_ref[...] = cv_ref[...].astype(BF16)

    def blk(i):
        return (i // steps) * nb

    cur = lambda i: (blk(i) + jnp.minimum(i % steps, nb - 1), 0)
    prv = lambda i: (blk(i) + jnp.clip(i % steps - 1, 0, nb - 1), 0)
    tcur = lambda i: (jnp.minimum(i % steps, nb - 1), 0)
    tprv = lambda i: (jnp.clip(i % steps - 1, 0, nb - 1), 0)
    kv = lambda m: pl.BlockSpec((BLOCK, KV_WIDTH), m)
    return _call(
        body, name="attn_bwd", grid=((T // seq) * steps,),
        in_specs=[pl.BlockSpec(memory_space=pltpu.SMEM),
                  pl.BlockSpec((BLOCK, ATTN_WIDTH), cur), pl.BlockSpec((BLOCK, ATTN_WIDTH), cur),
                  kv(prv), kv(cur), kv(prv), kv(cur),
                  kv(tcur), kv(tcur), kv(tcur), kv(tprv), kv(tprv), kv(tprv)],
        out_specs=[pl.BlockSpec((BLOCK, ATTN_WIDTH), cur), kv(prv), kv(prv), _const((8, LANES))],
        out_shape=[_sds((T, ATTN_WIDTH), BF16), _sds((T, KV_WIDTH), BF16), _sds((T, KV_WIDTH), BF16),
                   _sds((8, LANES), F32)],
        scratch=[pltpu.VMEM((BLOCK, KV_WIDTH), F32), pltpu.VMEM((BLOCK, KV_WIDTH), F32)],
        args=(sinks, q, do, k, k, v, v, *tabs, *tabs), sem=("arbitrary",), exchanges=exchanges)


def _pool_bwd(dyp, diff, w_pool, pool_scale, seq, exchanges=()):
    T = dyp.shape[0]
    tp = min(TP, seq)
    nseq = seq // tp
    per = tp // HALO
    last_halo = T // HALO - 1

    def body(dy_ref, nxt_ref, diff_ref, w_ref, s_ref, du_ref, dw_ref, ds_ref):
        i = pl.program_id(0)

        @pl.when(i == 0)
        def _():
            dw_ref[...] = jnp.zeros_like(dw_ref)
            ds_ref[...] = jnp.zeros_like(ds_ref)

        last = (i % nseq) == nseq - 1
        nxt = jnp.where(last, 0.0, nxt_ref[...])
        ext = jnp.concatenate([dy_ref[...], nxt], axis=0) * s_ref[...]
        pos = (i % nseq) * tp + lax.broadcasted_iota(jnp.int32, (tp + HALO, 1), 0)
        for gi, w in enumerate(POOL_WINDOWS):
            sl = slice(POOL_GC * gi, POOL_GC * (gi + 1))
            wg = w_ref[gi].astype(BF16)
            dmx = ext[:, sl].astype(BF16)
            ddiff = _dot_nt(dmx, wg)
            s = ddiff * _inv_count(pos, w)
            sh = 1
            while sh < w:
                s = s + pltpu.roll(s, tp + HALO - sh, 0)
                sh *= 2
            du_ref[:, sl] = (s[:tp] - ddiff[:tp]).astype(BF16)
            dg = diff_ref[:, sl]
            dw_ref[gi] += _dot_tn(dg, dmx[:tp])
            ds_ref[:, sl] += jnp.sum(dy_ref[:, sl] * _dot(dg, wg), axis=0, keepdims=True)

    return _call(
        body, name="pool_bwd", grid=(T // tp,),
        in_specs=[_rows(tp, POOL_WIDTH),
                  pl.BlockSpec((HALO, POOL_WIDTH), lambda i: (jnp.minimum((i + 1) * per, last_halo), 0)),
                  _rows(tp, POOL_WIDTH), _const((4, POOL_GC, POOL_GC)), _const((1, POOL_WIDTH))],
        out_specs=[_rows(tp, POOL_WIDTH), _const((4, POOL_GC, POOL_GC)), _const((1, POOL_WIDTH))],
        out_shape=[_sds((T, POOL_WIDTH), BF16), _sds((4, POOL_GC, POOL_GC), F32), _sds((1, POOL_WIDTH), F32)],
        args=(dyp, dyp, diff, w_pool, pool_scale), sem=("arbitrary",), exchanges=exchanges)


_PARTS = ((0, C_Q), (C_Q, C_K), (C_K, C_V), (C_V, C_G), (C_G, IN_WIDTH))


def _inproj_bwd(parts, x2, dx1, w_in_t, g1, exchanges=()):
    T = x2.shape[0]
    tm = min(TM, T)

    def body(du_ref, dq_ref, dk_ref, dv_ref, dgt_ref, x_ref, dx1_ref, w_ref, g_ref, gx_ref, dg_ref):
        @pl.when(pl.program_id(0) == 0)
        def _():
            dg_ref[...] = jnp.zeros_like(dg_ref)

        dh = jnp.zeros((tm, D_MODEL), F32)
        for (lo, hi), p_ref in zip(_PARTS, (du_ref, dq_ref, dk_ref, dv_ref, dgt_ref)):
            dh = dh + _dot(p_ref[...], w_ref[lo:hi, :])
        dx, dg = _norm_bwd(x_ref[...], g_ref[...], dh)
        gx_ref[...] = dx1_ref[...] + dx
        dg_ref[...] += dg

    return _call(
        body, name="inproj_bwd", grid=(T // tm,),
        in_specs=[_rows(tm, hi - lo) for lo, hi in _PARTS]
        + [_rows(tm, D_MODEL), _rows(tm, D_MODEL), _const((IN_WIDTH, D_MODEL)), _const((1, D_MODEL))],
        out_specs=[_rows(tm, D_MODEL), _const((1, D_MODEL))],
        out_shape=[_sds((T, D_MODEL), F32), _sds((1, D_MODEL), F32)],
        args=(*parts, x2, dx1, w_in_t, g1), sem=("arbitrary",), exchanges=exchanges)


def _dw_in(h, parts, exchanges=()):
    T = h.shape[0]
    tk = min(TM, T)

    def body(h_ref, du_ref, dq_ref, dk_ref, dv_ref, dgt_ref, o_ref, db_ref):
        @pl.when(pl.program_id(0) == 0)
        def _():
            o_ref[...] = jnp.zeros_like(o_ref)
            db_ref[...] = jnp.zeros_like(db_ref)

        hh = h_ref[...]
        for (lo, hi), p_ref in zip(_PARTS, (du_ref, dq_ref, dk_ref, dv_ref, dgt_ref)):
            part = p_ref[...]
            o_ref[lo:hi, :] += _dot_tn(part, hh)
            db_ref[:, lo:hi] += jnp.sum(part.astype(F32), axis=0, keepdims=True)

    return _call(
        body, name="dw_in", grid=(T // tk,),
        in_specs=[_rows(tk, D_MODEL)] + [_rows(tk, hi - lo) for lo, hi in _PARTS],
        out_specs=[_const((IN_WIDTH, D_MODEL)), _const((1, IN_WIDTH))],
        out_shape=[_sds((IN_WIDTH, D_MODEL), F32), _sds((1, IN_WIDTH), F32)],
        args=(h, *parts), sem=("arbitrary",), exchanges=exchanges)


def _row_tile(rows, cap=256, mult=16):
    best = None
    for t in range(mult, min(rows, cap) + 1, mult):
        if rows % t == 0:
            best = t
    if best is None:
        raise ValueError("no row tile for %d rows" % rows)
    return best


def _pair_sum(ids, full, got):
    _, r, c = full.shape
    hr = r // 2
    tr = _row_tile(hr)
    nblk = hr // tr

    def body(ids_ref, a_ref, b_ref, s_ref, sb_ref):
        s = a_ref[...] + b_ref[...]
        s_ref[...] = s
        sb_ref[...] = s.astype(BF16)

    out_spec = pl.BlockSpec((None, tr, c), lambda j, i, ids_ref: (j, i, 0))
    return pl.pallas_call(
        body, name="pair_sum_%dx%d" % (r, c),
        grid_spec=pltpu.PrefetchScalarGridSpec(
            num_scalar_prefetch=1, grid=(N_CHIPS, nblk),
            in_specs=[pl.BlockSpec((None, tr, c), lambda j, i, ids_ref: (j, ids_ref[1] * nblk + i, 0)), out_spec],
            out_specs=[out_spec, out_spec]),
        out_shape=[_sds((N_CHIPS, hr, c), F32), _sds((N_CHIPS, hr, c), BF16)],
        compiler_params=_cp("parallel", "parallel"),
    )(ids, full, got)


def _chip_sum(ids, own, got):
    _, hr, c = own.shape
    tr = _row_tile(hr)
    nblk = hr // tr

    def body(ids_ref, a_ref, b_ref, o_ref):
        o_ref[...] = ((a_ref[...] + b_ref[0].astype(F32)) + b_ref[1].astype(F32)) + b_ref[2].astype(F32)

    return pl.pallas_call(
        body, name="chip_sum_%dx%d" % (hr, c),
        grid_spec=pltpu.PrefetchScalarGridSpec(
            num_scalar_prefetch=1, grid=(nblk,),
            in_specs=[pl.BlockSpec((None, tr, c), lambda i, ids_ref: (ids_ref[0], i, 0)),
                      pl.BlockSpec((3, tr, c), lambda i, ids_ref: (0, i, 0))],
            out_specs=pl.BlockSpec((tr, c), lambda i, ids_ref: (ids_ref[1] * nblk + i, 0))),
        out_shape=_sds((2 * hr, c), F32),
        compiler_params=_cp("parallel"),
    )(ids, own, got)


def _sum_blocks(tag, allb, m_per):
    def body(a_ref, o_ref):
        acc = a_ref[0:m_per, :]
        for d in range(1, N_DEV):
            acc = acc + a_ref[d * m_per:(d + 1) * m_per, :]
        o_ref[...] = acc

    return pl.pallas_call(body, name="sum_blocks_" + tag, out_shape=_sds((m_per, allb.shape[1]), F32))(allb)


def _adamw(w, g, m, v):
    r, c = w.shape
    tr = _row_tile(r, mult=8)

    def body(w_ref, g_ref, m_ref, v_ref, d_ref, nm_ref, nv_ref):
        gg = g_ref[...]
        nm = ADAM_B1 * m_ref[...] + (1.0 - ADAM_B1) * gg
        nv = ADAM_B2 * v_ref[...] + (1.0 - ADAM_B2) * jnp.square(gg)
        m_hat = nm / (1.0 - ADAM_B1 ** ADAM_STEP)
        v_hat = nv / (1.0 - ADAM_B2 ** ADAM_STEP)
        d_ref[...] = -ADAM_LR * (m_hat / (jnp.sqrt(v_hat) + ADAM_EPS) + ADAM_WD * w_ref[...])
        nm_ref[...] = nm
        nv_ref[...] = nv

    spec = _rows(tr, c)
    return pl.pallas_call(
        body, name="adamw_%dx%d" % (r, c), grid=(r // tr,),
        in_specs=[spec] * 4, out_specs=[spec] * 3, out_shape=[_sds((r, c), F32)] * 3,
        compiler_params=_cp("parallel"),
    )(w, g, m, v)


_SMALL = (("w_pool", 4 * POOL_GC * POOL_GC), ("b_in", IN_WIDTH), ("g_mix_pre", D_MODEL), ("g_mix_post", D_MODEL),
          ("g_mlp_pre", D_MODEL), ("g_mlp_post", D_MODEL), ("pool_scale", POOL_WIDTH), ("attn_sinks", N_Q_HEADS),
          ("loss", 1))


def _pack_small(vals):
    parts = []
    for name, size in _SMALL:
        flat = vals[name].reshape(-1).astype(F32)
        padded = -(-size // (8 * LANES)) * (8 * LANES)
        parts.append(jnp.pad(flat, (0, padded - size)).reshape(-1, LANES))
    return jnp.concatenate(parts, axis=0)


def _small_row(name):
    row = 0
    for part, size in _SMALL:
        if part == name:
            return row
        row += -(-size // (8 * LANES)) * 8
    raise KeyError(name)


def _unpack_small(packed, shapes):
    out, row = {}, 0
    for name, size in _SMALL:
        nrows = -(-size // (8 * LANES)) * 8
        out[name] = packed[row:row + nrows].reshape(-1)[:size].reshape(shapes[name])
        row += nrows
    return out


_BIG = ("w_in", "w_branch_pool", "w_branch_attn", "w_out", "w_up", "w_down")
_ORDER = ("g_mix_pre", "w_in", "b_in", "w_pool", "pool_scale", "attn_sinks", "w_branch_pool", "w_branch_attn",
          "w_out", "g_mix_post", "g_mlp_pre", "w_up", "w_down", "g_mlp_post")


def _stack_rows(slab):
    return slab.reshape(-1, slab.shape[2])


def _step(x2, tgt, seq, shards, small, ids):
    tabs = _rope_tables(seq)
    g1, g2, g3, g4 = (small[n] for n in ("g_mix_pre", "g_mix_post", "g_mlp_pre", "g_mlp_post"))
    sinks = small["attn_sinks"].reshape(N_Q_HEADS)
    w_pool = small["w_pool"].reshape(4, POOL_GC, POOL_GC)
    pool_scale = small["pool_scale"]

    w_in = _stack_rows(_alone("gather_in", _ex_gather([shards["w_in"]]))[0][0])
    (h, u, q, k, v, gate), [(w_bp, w_ba, out_slab)] = _inproj(
        x2, g1, w_in, small["b_in"], tabs, seq,
        exchanges=[_ex_gather([shards[n] for n in ("w_branch_pool", "w_branch_attn", "w_out")])])
    w_out = _stack_rows(out_slab)
    diff, y_pool = _pool_fwd(u, w_pool, pool_scale, seq)
    (y_attn,), [[w_up]] = _attn_fwd(q, k, v, sinks, seq, exchanges=[_ex_gather([shards["w_up"]])])
    (bp, ba, merged, mix, x1, h2), [[down_slab]] = _merge_out(
        y_pool, y_attn, gate, x2, w_bp, w_ba, w_out, g2, g3, exchanges=[_ex_gather([shards["w_down"]])])
    w_down = _stack_rows(down_slab)
    up, act = _mlp_up(h2, w_up)
    dff, dy, loss_acc, dg4 = _mlp_down_loss(act, x1, tgt, w_down, g4)

    dup = _mlp_down_bwd(dff, up, w_down)
    dw_down = _dw("down", act, dff, 1024, 1024)[0].reshape(N_CHIPS, D_FF // N_CHIPS, D_MODEL)
    (dx1, dmix, dg3, dg2), [[got]] = _mlp_up_bwd(dup, dy, x1, mix, w_up, g3, g2, exchanges=[_ex_pair([dw_down])])
    ps_down = _pair_sum(ids, dw_down, got)
    (dw_up,), [[got]] = _dw("up", h2, dup, 1024, 1024, shard_cols=True, exchanges=[_ex_chip([ps_down[1]])])
    half_down = _chip_sum(ids, ps_down[0], got)
    (dbp, dba, dgate, dyp, dya), [[got], [g_down]] = _merge_bwd(
        dmix, gate, bp, ba, w_out, w_bp, w_ba, exchanges=[_ex_pair([dw_up]), _ex_swap([half_down])])
    ps_up = _pair_sum(ids, dw_up, got)
    dw_mix = [_dw("out", merged, dmix, 1024, 1024)[0].reshape(N_CHIPS, D_MODEL // N_CHIPS, D_MODEL),
              _dw_slabs("branch_pool", y_pool, dbp), _dw_slabs("branch_attn", y_attn, dba)]
    (dq, dk, dv, dsink), [[got], gots] = _attn_bwd(
        q, k, v, dya, sinks, tabs, seq, exchanges=[_ex_chip([ps_up[1]]), _ex_pair(dw_mix)])
    half_up = _chip_sum(ids, ps_up[0], got)
    ps_mix = [_pair_sum(ids, d, g) for d, g in zip(dw_mix, gots)]
    (du, dw_pool, dps), [[g_up]] = _pool_bwd(dyp, diff, w_pool, pool_scale, seq, exchanges=[_ex_swap([half_up])])
    parts = (du, dq, dk, dv, dgate)
    (dw_in_t, db_in), [gots] = _dw_in(h, parts, exchanges=[_ex_chip([p[1] for p in ps_mix])])
    half_mix = [_chip_sum(ids, p[0], g) for p, g in zip(ps_mix, gots)]
    dw_in = dw_in_t.reshape(N_CHIPS, IN_WIDTH // N_CHIPS, D_MODEL)
    g_mix, [got] = _alone("swap_mix_pair_in", _ex_swap(half_mix), _ex_pair([dw_in]))
    ps_in = _pair_sum(ids, dw_in, got)
    little = dict(w_pool=dw_pool, b_in=db_in, g_mix_pre=jnp.zeros_like(dg2), g_mix_post=dg2, g_mlp_pre=dg3,
                  g_mlp_post=dg4, pool_scale=dps, attn_sinks=dsink[:, 0], loss=loss_acc[0, 0])
    block = _pack_small(little)
    (gx, dg1), [[got], [gathered]] = _inproj_bwd(
        parts, x2, dx1, w_in, g1, exchanges=[_ex_chip([ps_in[1]]), _ex_allgather(block)])
    gain = dg1.reshape(8, LANES)
    [g_in], [gains] = _alone("swap_in_allgather", _ex_swap([_chip_sum(ids, ps_in[0], got)]), _ex_allgather(gain))
    total = _sum_blocks("small", gathered, block.shape[0])
    row = _small_row("g_mix_pre")
    total = jnp.concatenate([total[:row], _sum_blocks("gain", gains, 8), total[row + 8:]], axis=0)

    grads = dict(w_in=g_in, w_branch_pool=g_mix[1], w_branch_attn=g_mix[2], w_out=g_mix[0], w_up=g_up, w_down=g_down)
    return total, gx, grads


def kernel(x, g_mix_pre, w_in, b_in, w_pool, pool_scale, attn_sinks, w_branch_pool, w_branch_attn, w_out, g_mix_post, g_mlp_pre, w_up, w_down, g_mlp_post, loss_target, m_g_mix_pre, m_w_in, m_b_in, m_w_pool, m_pool_scale, m_attn_sinks, m_w_branch_pool, m_w_branch_attn, m_w_out, m_g_mix_post, m_g_mlp_pre, m_w_up, m_w_down, m_g_mlp_post, v_g_mix_pre, v_w_in, v_b_in, v_w_pool, v_pool_scale, v_attn_sinks, v_w_branch_pool, v_w_branch_attn, v_w_out, v_g_mix_post, v_g_mlp_pre, v_w_up, v_w_down, v_g_mlp_post):
    weights = dict(g_mix_pre=g_mix_pre, w_in=w_in, b_in=b_in, w_pool=w_pool, pool_scale=pool_scale,
                   attn_sinks=attn_sinks, w_branch_pool=w_branch_pool, w_branch_attn=w_branch_attn, w_out=w_out,
                   g_mix_post=g_mix_post, g_mlp_pre=g_mlp_pre, w_up=w_up, w_down=w_down, g_mlp_post=g_mlp_post)
    mom1 = dict(g_mix_pre=m_g_mix_pre, w_in=m_w_in, b_in=m_b_in, w_pool=m_w_pool, pool_scale=m_pool_scale,
                attn_sinks=m_attn_sinks, w_branch_pool=m_w_branch_pool, w_branch_attn=m_w_branch_attn,
                w_out=m_w_out, g_mix_post=m_g_mix_post, g_mlp_pre=m_g_mlp_pre, w_up=m_w_up, w_down=m_w_down,
                g_mlp_post=m_g_mlp_post)
    mom2 = dict(g_mix_pre=v_g_mix_pre, w_in=v_w_in, b_in=v_b_in, w_pool=v_w_pool, pool_scale=v_pool_scale,
                attn_sinks=v_attn_sinks, w_branch_pool=v_w_branch_pool, w_branch_attn=v_w_branch_attn,
                w_out=v_w_out, g_mix_post=v_g_mix_post, g_mlp_pre=v_g_mlp_pre, w_up=v_w_up, w_down=v_w_down,
                g_mlp_post=v_g_mlp_post)
    b_loc, seq, _ = x.shape
    x2 = x.reshape(b_loc * seq, D_MODEL)
    tgt = loss_target.reshape(b_loc * seq, D_MODEL)
    ids = jnp.stack([2 * lax.axis_index("x") + lax.axis_index("y"), lax.axis_index("c")]).astype(jnp.int32)

    def flat(n, a):
        return a[0].T if n == "w_in" else a[0]

    def unflat(n, a):
        return (a.T if n == "w_in" else a)[None]

    shards = {n: flat(n, weights[n]).astype(BF16) for n in _BIG}
    small = {n: weights[n] for n in _ORDER if n not in _BIG}
    total, gx, grads = _step(x2, tgt, seq, shards, small, ids)

    small_shapes = {n: weights[n].shape for n, _ in _SMALL if n != "loss"}
    small_shapes["loss"] = ()
    total = _unpack_small(total, small_shapes)
    loss = total.pop("loss")
    grads.update(total)

    delta, new_m, new_v = {}, {}, {}
    for n in _BIG:
        d, nm, nv = _adamw(flat(n, weights[n]), grads[n], flat(n, mom1[n]), flat(n, mom2[n]))
        grads[n] = unflat(n, grads[n])
        delta[n], new_m[n], new_v[n] = unflat(n, d), unflat(n, nm), unflat(n, nv)
    packed = [_pack_small({**src, "loss": jnp.zeros((), F32)}) for src in (weights, grads, mom1, mom2)]
    for dst, res in zip((delta, new_m, new_v), _adamw(*packed)):
        dst.update({n: a for n, a in _unpack_small(res, small_shapes).items() if n != "loss"})

    return (loss, gx.reshape(x.shape), *[grads[n] for n in _ORDER], *[delta[n] for n in _ORDER],
            *[new_m[n] for n in _ORDER], *[new_v[n] for n in _ORDER])
```

```python
import jax
import jax.numpy as jnp
from jax import lax
from jax.experimental import pallas as pl
from jax.experimental.pallas import tpu as pltpu

F32 = jnp.float32
BF16 = jnp.bfloat16

D_MODEL = 1024
POOL_WINDOWS = (2, 4, 8, 16)
POOL_WIDTH = 512
POOL_GC = 128
HALO = 16
HEAD_DIM = 64
N_Q_HEADS = 8
ATTN_WIDTH = 512
KV_WIDTH = 128
BLOCK = 128
NEG_INF = -1e30
ROPE_THETA = 500000.0
ROT_DIM = 16
GATE_WIDTH = 2048
IN_WIDTH = 3328
D_FF = 4096
EPS = 1e-6
SCALE = HEAD_DIM ** -0.5
C_Q, C_K, C_V, C_G = 512, 1024, 1152, 1280

ADAM_LR, ADAM_B1, ADAM_B2, ADAM_EPS, ADAM_WD, ADAM_STEP = 0.001, 0.9, 0.999, 1e-08, 0.01, 10

N_CHIPS = 4
N_DEV = 8
LANES = 128
TM = 512
TP = 256
VMEM_MB = 56

MESH = pl.DeviceIdType.MESH
ANY = pl.BlockSpec(memory_space=pl.ANY)


def _cp(*sem, vmem=VMEM_MB):
    return pltpu.CompilerParams(dimension_semantics=sem, vmem_limit_bytes=vmem * 1024 * 1024)


def _rows(tile, cols):
    return pl.BlockSpec((tile, cols), lambda i: (i, 0))


def _const(shape):
    nd = len(shape)
    return pl.BlockSpec(shape, lambda i: (0,) * nd)


def _sds(shape, dtype):
    return jax.ShapeDtypeStruct(shape, dtype)


def _dot(a, b):
    return jnp.dot(a, b, preferred_element_type=F32)


def _dot_nt(a, b):
    return lax.dot_general(a, b, (((1,), (1,)), ((), ())), preferred_element_type=F32)


def _dot_tn(a, b):
    return lax.dot_general(a, b, (((0,), (0,)), ((), ())), preferred_element_type=F32)


def _rms(x):
    return lax.rsqrt(jnp.mean(x * x, axis=-1, keepdims=True) + EPS)


def _norm_bwd(x, g, dout):
    r = _rms(x)
    n = x * r
    dn = dout * g
    dx = r * (dn - n * jnp.mean(dn * n, axis=-1, keepdims=True))
    return dx, jnp.sum(dout * n, axis=0, keepdims=True)


def _rot_fwd(t, c, a, bt):
    return t * c + pltpu.roll(t, LANES - 8, 1) * a + pltpu.roll(t, 8, 1) * bt


def _rot_bwd(d, c, a, bt):
    return d * c + pltpu.roll(d * a, 8, 1) + pltpu.roll(d * bt, LANES - 8, 1)


def _rope_tables(seq):
    pos = jnp.arange(seq, dtype=F32)
    inv_freq = ROPE_THETA ** (-jnp.arange(0, ROT_DIM, 2, dtype=F32) / ROT_DIM)
    ang = pos[:, None] * inv_freq[None, :]
    cos, sin = jnp.cos(ang), jnp.sin(ang)
    ones = jnp.ones((seq, HEAD_DIM - ROT_DIM), F32)
    zeros8 = jnp.zeros((seq, 8), F32)
    zrest = jnp.zeros((seq, HEAD_DIM - ROT_DIM), F32)
    c = jnp.concatenate([cos, cos, ones], axis=1)
    a = jnp.concatenate([-sin, zeros8, zrest], axis=1)
    bt = jnp.concatenate([zeros8, sin, zrest], axis=1)
    return tuple(jnp.tile(t, (1, 2)) for t in (c, a, bt))


class _Exchange:
    def __init__(self, inputs, out_shapes, sems, start, finish, aliases=None, middle=None):
        self.inputs, self.out_shapes, self.sems = list(inputs), list(out_shapes), list(sems)
        self.start, self.finish, self.aliases = start, finish, dict(aliases or {})
        self.middle = middle


def _call(body, *, name, grid, in_specs, out_specs, out_shape, args, scratch=(), sem=(), exchanges=()):
    in_specs, out_specs, out_shape, scratch = list(in_specs), list(out_specs), list(out_shape), list(scratch)
    if not exchanges:
        return pl.pallas_call(body, name=name, grid=grid, in_specs=in_specs, out_specs=out_specs,
                              out_shape=out_shape, scratch_shapes=scratch, compiler_params=_cp(*sem))(*args)
    n_in, n_out, n_scr = len(in_specs), len(out_specs), len(scratch)
    x_in = [a for ex in exchanges for a in ex.inputs]
    x_out = [s for ex in exchanges for s in ex.out_shapes]
    x_sem = [s for ex in exchanges for s in ex.sems]
    aliases, i_off, o_off = {}, n_in, n_out
    for ex in exchanges:
        for i, o in ex.aliases.items():
            aliases[i_off + i] = o_off + o
        i_off += len(ex.inputs)
        o_off += len(ex.out_shapes)

    def split(flat):
        out, pos = [], 0
        for ex, n in zip(exchanges, flat[1]):
            out.append(flat[0][pos:pos + n])
            pos += n
        return out

    def carrier(*refs):
        pos = 0
        groups = []
        for n in (n_in, len(x_in), n_out, len(x_out), n_scr, len(x_sem)):
            groups.append(refs[pos:pos + n])
            pos += n
        ins, xin, outs, xout, scr, xsem = groups
        xin = split((xin, [len(ex.inputs) for ex in exchanges]))
        xout = split((xout, [len(ex.out_shapes) for ex in exchanges]))
        xsem = split((xsem, [len(ex.sems) for ex in exchanges]))
        first = pl.program_id(0) == 0
        last = pl.program_id(0) == grid[0] - 1
        for d in range(1, len(grid)):
            first = jnp.logical_and(first, pl.program_id(d) == 0)
            last = jnp.logical_and(last, pl.program_id(d) == grid[d] - 1)

        @pl.when(first)
        def _():
            for ex, i, o, s in zip(exchanges, xin, xout, xsem):
                ex.start(i, o, s)

        if any(ex.middle for ex in exchanges):
            half = pl.program_id(0) == grid[0] // 2
            for d in range(1, len(grid)):
                half = jnp.logical_and(half, pl.program_id(d) == 0)

            @pl.when(half)
            def _():
                for ex, i, o, s in zip(exchanges, xin, xout, xsem):
                    if ex.middle:
                        ex.middle(i, o, s)

        body(*ins, *outs, *scr)

        @pl.when(last)
        def _():
            for ex, i, o, s in zip(exchanges, xin, xout, xsem):
                ex.finish(i, o, s)

    res = pl.pallas_call(
        carrier, name=name, grid=grid, in_specs=in_specs + [ANY] * len(x_in),
        out_specs=out_specs + [ANY] * len(x_out), out_shape=out_shape + x_out,
        scratch_shapes=scratch + x_sem, input_output_aliases=aliases,
        compiler_params=_cp(*(["arbitrary"] * len(grid))),
    )(*args, *x_in)
    return res[:n_out], split((res[n_out:], [len(ex.out_shapes) for ex in exchanges]))


def _alone(name, *exchanges):
    n_in = [len(ex.inputs) for ex in exchanges]
    n_out = [len(ex.out_shapes) for ex in exchanges]
    n_sem = [len(ex.sems) for ex in exchanges]
    aliases, i_off, o_off = {}, 0, 0
    for ex in exchanges:
        for i, o in ex.aliases.items():
            aliases[i_off + i] = o_off + o
        i_off += len(ex.inputs)
        o_off += len(ex.out_shapes)

    def split(flat, counts):
        out, pos = [], 0
        for n in counts:
            out.append(flat[pos:pos + n])
            pos += n
        return out

    def body(*refs):
        ins, outs, sems = split(refs, [sum(n_in), sum(n_out), sum(n_sem)])
        groups = list(zip(exchanges, split(ins, n_in), split(outs, n_out), split(sems, n_sem)))
        for ex, i, o, s in groups:
            ex.start(i, o, s)
        for ex, i, o, s in groups:
            if ex.middle:
                ex.middle(i, o, s)
        for ex, i, o, s in groups:
            ex.finish(i, o, s)

    res = pl.pallas_call(
        body, name=name, in_specs=[ANY] * sum(n_in), out_specs=[ANY] * sum(n_out),
        out_shape=[s for ex in exchanges for s in ex.out_shapes],
        scratch_shapes=[s for ex in exchanges for s in ex.sems], input_output_aliases=aliases,
    )(*[a for ex in exchanges for a in ex.inputs])
    return split(res, n_out)


def _place():
    x, y, c = lax.axis_index("x"), lax.axis_index("y"), lax.axis_index("c")
    chips = [(1 - x, y), (x, 1 - y), (1 - x, 1 - y)]
    return x, y, c, chips


def _remote(src, dst, send, recv, to):
    return pltpu.make_async_remote_copy(src_ref=src, dst_ref=dst, send_sem=send, recv_sem=recv,
                                        device_id=to, device_id_type=MESH)


def _ex_gather(shards):
    nw = len(shards)
    hrs = [s.shape[0] // 2 for s in shards]

    def copies(ins, outs, sems):
        s1, r1, s2, r2, fs, fr = sems
        x, y, c, _ = _place()
        me, xn, yn, dg = (x, y), (1 - x, y), (x, 1 - y), (1 - x, 1 - y)
        nbr = (xn, yn)
        sibling = (x, y, 1 - c)

        def piece(w, chip, core, part=None):
            hr = hrs[w]
            rows = pl.ds(core * hr, hr) if part is None else pl.ds(core * hr + part * (hr // 2), hr // 2)
            return outs[w].at[2 * chip[0] + chip[1], rows]

        def first(w, k):
            return _remote(ins[w].at[pl.ds(c * hrs[w], hrs[w])], piece(w, me, c), s1.at[w, k], r1.at[w, k],
                           (*nbr[k], c))

        def landed(w, k):
            return _remote(piece(w, nbr[k], c), piece(w, nbr[k], c), s1.at[w, k], r1.at[w, k], (*nbr[k], c))

        def onward(w, k):
            return _remote(piece(w, nbr[k], c, k), piece(w, nbr[k], c, k), s2.at[w, k], r2.at[w, k],
                           (*nbr[1 - k], c))

        def arrived(w, k):
            return _remote(piece(w, dg, c, k), piece(w, dg, c, k), s2.at[w, k], r2.at[w, k], (*nbr[1 - k], c))

        def passed(w, j):
            chip = (xn, yn, dg)[j]
            return _remote(piece(w, chip, c), piece(w, chip, c), fs.at[w, j], fr.at[w, j], sibling)

        def handed(w, j):
            chip = (xn, yn, dg)[j]
            return _remote(piece(w, chip, 1 - c), piece(w, chip, 1 - c), fs.at[w, j], fr.at[w, j], sibling)

        return first, landed, onward, arrived, passed, handed

    def start(ins, outs, sems):
        first = copies(ins, outs, sems)[0]
        for w in range(nw):
            for k in range(2):
                first(w, k).start()

    def middle(ins, outs, sems):
        _, landed, onward, _, passed, _ = copies(ins, outs, sems)
        for w in range(nw):
            for k in range(2):
                landed(w, k).wait_recv()
                onward(w, k).start()
                passed(w, k).start()

    def finish(ins, outs, sems):
        first, _, onward, arrived, passed, handed = copies(ins, outs, sems)
        for w in range(nw):
            for k in range(2):
                arrived(w, k).wait_recv()
            passed(w, 2).start()
        for w in range(nw):
            for j in range(3):
                handed(w, j).wait_recv()
        for w in range(nw):
            for k in range(2):
                first(w, k).wait_send()
                onward(w, k).wait_send()
            for j in range(3):
                passed(w, j).wait_send()

    return _Exchange(shards, [_sds((N_CHIPS,) + s.shape, s.dtype) for s in shards],
                     [pltpu.SemaphoreType.DMA((nw, 2))] * 4 + [pltpu.SemaphoreType.DMA((nw, 3))] * 2,
                     start, finish, middle=middle)


def _ex_pair(grads):
    nw = len(grads)

    def copies(ins, outs, sems):
        x, y, c, _ = _place()
        out = []
        for w in range(nw):
            hr = grads[w].shape[1] // 2
            out.append(_remote(ins[w].at[:, pl.ds((1 - c) * hr, hr)], outs[w], sems[0].at[w], sems[1].at[w],
                               (x, y, 1 - c)))
        return out

    def start(ins, outs, sems):
        for cp in copies(ins, outs, sems):
            cp.start()

    def finish(ins, outs, sems):
        for cp in copies(ins, outs, sems):
            cp.wait()

    return _Exchange(grads, [_sds((N_CHIPS, g.shape[1] // 2, g.shape[2]), F32) for g in grads],
                     [pltpu.SemaphoreType.DMA((nw,))] * 2, start, finish)


def _ex_chip(pieces):
    nw = len(pieces)

    def copies(ins, outs, sems):
        x, y, c, chips = _place()
        return [_remote(ins[w].at[2 * cx + cy], outs[w].at[k], sems[0].at[w, k], sems[1].at[w, k], (cx, cy, c))
                for w in range(nw) for k, (cx, cy) in enumerate(chips)]

    def start(ins, outs, sems):
        for cp in copies(ins, outs, sems):
            cp.start()

    def finish(ins, outs, sems):
        for cp in copies(ins, outs, sems):
            cp.wait()

    return _Exchange(pieces, [_sds((3,) + p.shape[1:], BF16) for p in pieces],
                     [pltpu.SemaphoreType.DMA((nw, 3))] * 2, start, finish)


def _ex_swap(fulls):
    nw = len(fulls)

    def start(ins, outs, sems):
        x, y, c, _ = _place()
        for w in range(nw):
            hr = fulls[w].shape[0] // 2
            mine = pl.ds(c * hr, hr)
            _remote(ins[w].at[mine], outs[w].at[mine], sems[0].at[w], sems[1].at[w], (x, y, 1 - c)).start()

    def finish(ins, outs, sems):
        x, y, c, _ = _place()
        for w in range(nw):
            hr = fulls[w].shape[0] // 2
            mine, theirs = pl.ds(c * hr, hr), pl.ds((1 - c) * hr, hr)
            _remote(ins[w].at[mine], outs[w].at[mine], sems[0].at[w], sems[1].at[w], (x, y, 1 - c)).wait_send()
            _remote(ins[w].at[theirs], outs[w].at[theirs], sems[0].at[w], sems[1].at[w], (x, y, 1 - c)).wait_recv()

    return _Exchange(fulls, [_sds(f.shape, F32) for f in fulls], [pltpu.SemaphoreType.DMA((nw,))] * 2,
                     start, finish, aliases={w: w for w in range(nw)})


def _ex_allgather(block):
    m_per, n = block.shape

    def copies(ins, outs, sems):
        send, recv, lsem = sems
        x, y, c, chips = _place()
        me, sibling = (x, y, c), (x, y, 1 - c)

        def rows(px, py, pc):
            return outs[0].at[pl.ds((4 * px + 2 * py + pc) * m_per, m_per), :]

        def copy(k, blk, to, src=None):
            return _remote(rows(*blk) if src is None else src, rows(*blk), send.at[k], recv.at[k], to)

        def mine():
            return pltpu.make_async_copy(ins[0], rows(*me), lsem.at[0])

        def first(k):
            return copy(k, me, sibling if k == 0 else (*chips[k - 1], c), src=ins[0])

        def passed(j):
            return copy(4 + j, (*chips[j], c), sibling)

        def landed(j):
            return copy(1 + j, (*chips[j], c), me)

        def handed(k):
            return copy(0, sibling, me) if k == 0 else copy(3 + k, (*chips[k - 1], 1 - c), me)

        return mine, first, passed, landed, handed

    def start(ins, outs, sems):
        mine, first, _, _, _ = copies(ins, outs, sems)
        mine().start()
        for k in range(4):
            first(k).start()

    def finish(ins, outs, sems):
        mine, first, passed, landed, handed = copies(ins, outs, sems)
        sent = []
        for j in range(3):
            landed(j).wait_recv()
            cp = passed(j)
            cp.start()
            sent.append(cp)
        for k in range(4):
            handed(k).wait_recv()
        for k in range(4):
            first(k).wait_send()
        for cp in sent:
            cp.wait_send()
        mine().wait()

    return _Exchange([block], [_sds((N_DEV * m_per, n), F32)],
                     [pltpu.SemaphoreType.DMA((7,)), pltpu.SemaphoreType.DMA((7,)), pltpu.SemaphoreType.DMA((1,))],
                     start, finish)


def _inproj(x2, g1, w_in_t, b_in, tabs, seq, exchanges=()):
    T = x2.shape[0]
    tm = min(TM, seq)
    nseq = seq // tm

    def body(x_ref, g_ref, w_ref, b_ref, c_ref, a_ref, bt_ref, h_ref, u_ref, q_ref, k_ref, v_ref, gate_ref):
        x = x_ref[...]
        h = (x * _rms(x) * g_ref[...]).astype(BF16)
        h_ref[...] = h

        def proj(lo, hi):
            return _dot_nt(h, w_ref[lo:hi, :]) + b_ref[:, lo:hi]

        c, a, bt = c_ref[...], a_ref[...], bt_ref[...]
        u_ref[...] = proj(0, C_Q)
        q = proj(C_Q, C_K)
        for p in range(4):
            sl = slice(LANES * p, LANES * (p + 1))
            q_ref[:, sl] = (_rot_fwd(q[:, sl], c, a, bt) * SCALE).astype(BF16)
        kv = proj(C_K, C_G)
        k_ref[...] = _rot_fwd(kv[:, :KV_WIDTH], c, a, bt).astype(BF16)
        v_ref[...] = kv[:, KV_WIDTH:].astype(BF16)
        for j in range(2):
            lo = C_G + D_MODEL * j
            gate_ref[:, D_MODEL * j:D_MODEL * (j + 1)] = jax.nn.sigmoid(proj(lo, lo + D_MODEL)).astype(BF16)

    tab = pl.BlockSpec((tm, LANES), lambda i: (i % nseq, 0))
    return _call(
        body, name="inproj", grid=(T // tm,),
        in_specs=[_rows(tm, D_MODEL), _const((1, D_MODEL)), _const((IN_WIDTH, D_MODEL)), _const((1, IN_WIDTH)),
                  tab, tab, tab],
        out_specs=[_rows(tm, D_MODEL), _rows(tm, POOL_WIDTH), _rows(tm, ATTN_WIDTH), _rows(tm, KV_WIDTH),
                   _rows(tm, KV_WIDTH), _rows(tm, GATE_WIDTH)],
        out_shape=[_sds((T, D_MODEL), BF16), _sds((T, POOL_WIDTH), F32), _sds((T, ATTN_WIDTH), BF16),
                   _sds((T, KV_WIDTH), BF16), _sds((T, KV_WIDTH), BF16), _sds((T, GATE_WIDTH), BF16)],
        args=(x2, g1, w_in_t, b_in, *tabs), sem=("parallel",), exchanges=exchanges)


def _inv_count(pos, w):
    return 1.0 / jnp.minimum(pos + 1, w).astype(F32)


def _pool_fwd(u, w_pool, pool_scale, seq):
    T = u.shape[0]
    tp = min(TP, seq)
    nseq = seq // tp
    per = tp // HALO

    def body(u_ref, prev_ref, w_ref, s_ref, diff_ref, y_ref):
        i = pl.program_id(0)
        first = (i % nseq) == 0
        prev = jnp.where(first, 0.0, prev_ref[...])
        ext = jnp.concatenate([prev, u_ref[...]], axis=0)
        pos = (i % nseq) * tp + lax.broadcasted_iota(jnp.int32, (tp, 1), 0)
        for gi, w in enumerate(POOL_WINDOWS):
            sl = slice(POOL_GC * gi, POOL_GC * (gi + 1))
            xg = ext[:, sl]
            s = xg
            sh = 1
            while sh < w:
                s = s + pltpu.roll(s, sh, 0)
                sh *= 2
            pooled = s[HALO:] * _inv_count(pos, w)
            diff = (pooled - xg[HALO:]).astype(BF16)
            diff_ref[:, sl] = diff
            mixed = _dot(diff, w_ref[gi].astype(BF16))
            y_ref[:, sl] = (mixed * s_ref[:, sl]).astype(BF16)

    return _call(
        body, name="pool_fwd", grid=(T // tp,),
        in_specs=[_rows(tp, POOL_WIDTH),
                  pl.BlockSpec((HALO, POOL_WIDTH), lambda i: (jnp.maximum(i * per - 1, 0), 0)),
                  _const((4, POOL_GC, POOL_GC)), _const((1, POOL_WIDTH))],
        out_specs=[_rows(tp, POOL_WIDTH), _rows(tp, POOL_WIDTH)],
        out_shape=[_sds((T, POOL_WIDTH), BF16), _sds((T, POOL_WIDTH), BF16)],
        args=(u, u, w_pool, pool_scale), sem=("parallel",))


GROUP = 4
GROWS = GROUP * BLOCK


def _attn_masks(n):
    qi = lax.broadcasted_iota(jnp.int32, (GROWS, 2 * BLOCK), 0) % BLOCK
    kj = lax.broadcasted_iota(jnp.int32, (GROWS, 2 * BLOCK), 1)
    rel = qi + BLOCK - kj
    valid = (rel >= 0) & (rel < BLOCK) & (kj >= jnp.where(n > 0, 0, BLOCK))
    lo = lax.broadcasted_iota(jnp.int32, (BLOCK, LANES), 1) < HEAD_DIM
    return valid, lo


def _stack_heads(ref, h, lo):
    keep = lo if h == 0 else jnp.logical_not(lo)
    pieces = []
    for p in (2 * h, 2 * h + 1):
        xp = ref[:, LANES * p:LANES * (p + 1)].astype(F32)
        for e in range(2):
            t = xp if e == h else pltpu.roll(xp, HEAD_DIM, 1)
            pieces.append(jnp.where(keep, t, 0.0).astype(BF16))
    return jnp.concatenate(pieces, axis=0)


def _unstack_heads(stacked, h, lo):
    pairs = []
    for j in range(2):
        parts = []
        for e in range(2):
            t = stacked[BLOCK * (2 * j + e):BLOCK * (2 * j + e + 1)]
            parts.append(t if e == h else pltpu.roll(t, HEAD_DIM, 1))
        pairs.append(jnp.where(lo, parts[0], parts[1]))
    return pairs


def _sink_rows(sink_ref, h):
    head = lax.broadcasted_iota(jnp.int32, (GROWS, 1), 0) // BLOCK
    col = jnp.zeros((GROWS, 1), F32) + sink_ref[GROUP * h]
    for g in range(1, GROUP):
        col = jnp.where(head == g, sink_ref[GROUP * h + g], col)
    return col


def _group_probs(qs, kk, valid, sink):
    s = jnp.where(valid, _dot_nt(qs, kk), NEG_INF)
    m = jnp.maximum(jnp.max(s, axis=1, keepdims=True), sink)
    ex = jnp.exp(s - m)
    es = jnp.exp(sink - m)
    inv = 1.0 / (jnp.sum(ex, axis=1, keepdims=True) + es)
    return ex * inv, es * inv


def _attn_fwd(q, k, v, sinks, seq, exchanges=()):
    T = q.shape[0]
    nb = seq // BLOCK

    def body(sink_ref, q_ref, kp_ref, kc_ref, vp_ref, vc_ref, o_ref):
        n = pl.program_id(0) % nb
        kk = jnp.concatenate([kp_ref[...], kc_ref[...]], axis=0)
        vv = jnp.concatenate([vp_ref[...], vc_ref[...]], axis=0)
        valid, lo = _attn_masks(n)
        for h in range(2):
            qs = _stack_heads(q_ref, h, lo)
            pr, _ = _group_probs(qs, kk, valid, _sink_rows(sink_ref, h))
            o = _dot(pr.astype(BF16), vv)
            for j, pair in enumerate(_unstack_heads(o, h, lo)):
                p = 2 * h + j
                o_ref[:, LANES * p:LANES * (p + 1)] = pair.astype(BF16)

    cur = lambda i: (i, 0)
    prv = lambda i: (jnp.where(i % nb == 0, i, i - 1), 0)
    return _call(
        body, name="attn_fwd", grid=(T // BLOCK,),
        in_specs=[pl.BlockSpec(memory_space=pltpu.SMEM),
                  pl.BlockSpec((BLOCK, ATTN_WIDTH), cur),
                  pl.BlockSpec((BLOCK, KV_WIDTH), prv), pl.BlockSpec((BLOCK, KV_WIDTH), cur),
                  pl.BlockSpec((BLOCK, KV_WIDTH), prv), pl.BlockSpec((BLOCK, KV_WIDTH), cur)],
        out_specs=[pl.BlockSpec((BLOCK, ATTN_WIDTH), cur)],
        out_shape=[_sds((T, ATTN_WIDTH), BF16)],
        args=(sinks, q, k, k, v, v), sem=("parallel",), exchanges=exchanges)


def _merge_out(y_pool, y_attn, gate, x2, w_bp, w_ba, w_out, g2, g3, exchanges=()):
    T = x2.shape[0]
    tm = min(TM, T)

    def body(yp_ref, ya_ref, gate_ref, x_ref, wbp_ref, wba_ref, wo_ref, g2_ref, g3_ref,
             bp_ref, ba_ref, mg_ref, mix_ref, x1_ref, h2_ref):
        yp, ya = yp_ref[...], ya_ref[...]
        bp = jnp.concatenate([_dot(yp, wbp_ref[j]) for j in range(N_CHIPS)], axis=1)
        ba = jnp.concatenate([_dot(ya, wba_ref[j]) for j in range(N_CHIPS)], axis=1)
        bp_ref[...] = bp.astype(BF16)
        ba_ref[...] = ba.astype(BF16)
        merged = (gate_ref[:, :D_MODEL].astype(F32) * bp + gate_ref[:, D_MODEL:].astype(F32) * ba).astype(BF16)
        mg_ref[...] = merged
        mix = _dot(merged, wo_ref[...])
        mix_ref[...] = mix
        x1 = x_ref[...] + mix * _rms(mix) * g2_ref[...]
        x1_ref[...] = x1
        h2_ref[...] = (x1 * _rms(x1) * g3_ref[...]).astype(BF16)

    return _call(
        body, name="merge_out", grid=(T // tm,),
        in_specs=[_rows(tm, POOL_WIDTH), _rows(tm, ATTN_WIDTH), _rows(tm, GATE_WIDTH), _rows(tm, D_MODEL),
                  _const(w_bp.shape), _const(w_ba.shape), _const((D_MODEL, D_MODEL)),
                  _const((1, D_MODEL)), _const((1, D_MODEL))],
        out_specs=[_rows(tm, D_MODEL)] * 6,
        out_shape=[_sds((T, D_MODEL), BF16), _sds((T, D_MODEL), BF16), _sds((T, D_MODEL), BF16),
                   _sds((T, D_MODEL), F32), _sds((T, D_MODEL), F32), _sds((T, D_MODEL), BF16)],
        args=(y_pool, y_attn, gate, x2, w_bp, w_ba, w_out, g2, g3), sem=("parallel",), exchanges=exchanges)


def _mlp_up(h2, w_up):
    T = h2.shape[0]
    tm = min(TM, T)

    def body(h_ref, w_ref, up_ref, a_ref):
        h = h_ref[...]
        for j in range(N_CHIPS):
            sl = slice(D_MODEL * j, D_MODEL * (j + 1))
            up = _dot(h, w_ref[j])
            up_ref[:, sl] = up.astype(BF16)
            a_ref[:, sl] = jnp.square(jnp.maximum(up, 0.0)).astype(BF16)

    return _call(
        body, name="mlp_up", grid=(T // tm,),
        in_specs=[_rows(tm, D_MODEL), _const((N_CHIPS, D_MODEL, D_MODEL))],
        out_specs=[_rows(tm, D_FF), _rows(tm, D_FF)],
        out_shape=[_sds((T, D_FF), BF16), _sds((T, D_FF), BF16)],
        args=(h2, w_up), sem=("parallel",))


def _mlp_down_loss(a, x1, tgt, w_down, g4):
    T = a.shape[0]
    tm = min(TM, T)

    def body(a_ref, x1_ref, t_ref, w_ref, g_ref, dff_ref, dy_ref, loss_ref, dg_ref):
        @pl.when(pl.program_id(0) == 0)
        def _():
            loss_ref[...] = jnp.zeros_like(loss_ref)
            dg_ref[...] = jnp.zeros_like(dg_ref)

        ff = _dot(a_ref[...], w_ref[...])
        g = g_ref[...]
        err = x1_ref[...] + ff * _rms(ff) * g - t_ref[...]
        loss_ref[...] += jnp.sum(err * err) * (0.5 / D_MODEL)
        dy = err * (1.0 / D_MODEL)
        dy_ref[...] = dy
        dff, dg = _norm_bwd(ff, g, dy)
        dff_ref[...] = dff.astype(BF16)
        dg_ref[...] += dg

    return _call(
        body, name="mlp_down_loss", grid=(T // tm,),
        in_specs=[_rows(tm, D_FF), _rows(tm, D_MODEL), _rows(tm, D_MODEL), _const((D_FF, D_MODEL)),
                  _const((1, D_MODEL))],
        out_specs=[_rows(tm, D_MODEL), _rows(tm, D_MODEL), _const((8, LANES)), _const((1, D_MODEL))],
        out_shape=[_sds((T, D_MODEL), BF16), _sds((T, D_MODEL), F32), _sds((8, LANES), F32),
                   _sds((1, D_MODEL), F32)],
        args=(a, x1, tgt, w_down, g4), sem=("arbitrary",))


def _mlp_down_bwd(dff, up, w_down):
    T = dff.shape[0]
    tm = min(TM, T)

    def body(d_ref, up_ref, w_ref, dup_ref):
        d = d_ref[...]
        for j in range(D_FF // D_MODEL):
            sl = slice(D_MODEL * j, D_MODEL * (j + 1))
            da = _dot_nt(d, w_ref[sl, :])
            dup_ref[:, sl] = (da * (2.0 * jnp.maximum(up_ref[:, sl].astype(F32), 0.0))).astype(BF16)

    return _call(
        body, name="mlp_down_bwd", grid=(T // tm,),
        in_specs=[_rows(tm, D_MODEL), _rows(tm, D_FF), _const((D_FF, D_MODEL))],
        out_specs=[_rows(tm, D_FF)],
        out_shape=[_sds((T, D_FF), BF16)],
        args=(dff, up, w_down), sem=("parallel",))[0]


def _mlp_up_bwd(dup, dy, x1, mix, w_up, g3, g2, exchanges=()):
    T = dup.shape[0]
    tm = min(TM, T)

    def body(dup_ref, dy_ref, x1_ref, mix_ref, w_ref, g3_ref, g2_ref, dx1_ref, dmix_ref, dg3_ref, dg2_ref):
        @pl.when(pl.program_id(0) == 0)
        def _():
            dg3_ref[...] = jnp.zeros_like(dg3_ref)
            dg2_ref[...] = jnp.zeros_like(dg2_ref)

        dh2 = _dot_nt(dup_ref[:, :D_MODEL], w_ref[0])
        for j in range(1, N_CHIPS):
            dh2 = dh2 + _dot_nt(dup_ref[:, D_MODEL * j:D_MODEL * (j + 1)], w_ref[j])
        dx, dg3 = _norm_bwd(x1_ref[...], g3_ref[...], dh2)
        dx1 = dy_ref[...] + dx
        dx1_ref[...] = dx1
        dg3_ref[...] += dg3
        dmix, dg2 = _norm_bwd(mix_ref[...], g2_ref[...], dx1)
        dmix_ref[...] = dmix.astype(BF16)
        dg2_ref[...] += dg2

    return _call(
        body, name="mlp_up_bwd", grid=(T // tm,),
        in_specs=[_rows(tm, D_FF), _rows(tm, D_MODEL), _rows(tm, D_MODEL), _rows(tm, D_MODEL),
                  _const((N_CHIPS, D_MODEL, D_MODEL)), _const((1, D_MODEL)), _const((1, D_MODEL))],
        out_specs=[_rows(tm, D_MODEL), _rows(tm, D_MODEL), _const((1, D_MODEL)), _const((1, D_MODEL))],
        out_shape=[_sds((T, D_MODEL), F32), _sds((T, D_MODEL), BF16), _sds((1, D_MODEL), F32),
                   _sds((1, D_MODEL), F32)],
        args=(dup, dy, x1, mix, w_up, g3, g2), sem=("arbitrary",), exchanges=exchanges)


def _dw(tag, a, g, ta, tn, shard_cols=False, exchanges=()):
    T, ka = a.shape
    n = g.shape[1]
    tk = min(TM, T)
    nk = T // tk

    def body(a_ref, g_ref, o_ref):
        @pl.when(pl.program_id(2) == 0)
        def _():
            o_ref[...] = jnp.zeros_like(o_ref)

        o_ref[...] += _dot_tn(a_ref[...], g_ref[...])

    if shard_cols:
        per = (n // N_CHIPS) // tn
        out_spec = pl.BlockSpec((None, ta, tn), lambda i, j, k: (j // per, i, j % per))
        out_shape = _sds((N_CHIPS, ka, n // N_CHIPS), F32)
    else:
        out_spec = pl.BlockSpec((ta, tn), lambda i, j, k: (i, j))
        out_shape = _sds((ka, n), F32)
    return _call(
        body, name="dw_" + tag, grid=(ka // ta, n // tn, nk),
        in_specs=[pl.BlockSpec((tk, ta), lambda i, j, k: (k, i)), pl.BlockSpec((tk, tn), lambda i, j, k: (k, j))],
        out_specs=[out_spec], out_shape=[out_shape],
        args=(a, g), sem=("parallel", "parallel", "arbitrary"), exchanges=exchanges)


def _dw_slabs(tag, a, g):
    T, ka = a.shape
    n = g.shape[1]
    c = n // N_CHIPS
    tk = min(TM, T)

    def body(a_ref, g_ref, o_ref):
        @pl.when(pl.program_id(0) == 0)
        def _():
            o_ref[...] = jnp.zeros_like(o_ref)

        res = _dot_tn(a_ref[...], g_ref[...])
        for j in range(N_CHIPS):
            o_ref[j] += res[:, c * j:c * (j + 1)]

    return _call(
        body, name="dw_" + tag, grid=(T // tk,),
        in_specs=[_rows(tk, ka), _rows(tk, n)],
        out_specs=[_const((N_CHIPS, ka, c))], out_shape=[_sds((N_CHIPS, ka, c), F32)],
        args=(a, g), sem=("arbitrary",))[0]


def _merge_bwd(dmix, gate, bp, ba, w_out, w_bp, w_ba, exchanges=()):
    T = dmix.shape[0]
    tm = min(TM, T)

    def body(dmix_ref, gate_ref, bp_ref, ba_ref, wo_ref, wbp_ref, wba_ref,
             dbp_ref, dba_ref, dgate_ref, dyp_ref, dya_ref):
        dm = _dot_nt(dmix_ref[...], wo_ref[...])
        for j, (b_ref, db_ref, w_ref, dy_ref) in enumerate(
                ((bp_ref, dbp_ref, wbp_ref, dyp_ref), (ba_ref, dba_ref, wba_ref, dya_ref))):
            sl = slice(D_MODEL * j, D_MODEL * (j + 1))
            gt = gate_ref[:, sl].astype(F32)
            db = (dm * gt).astype(BF16)
            db_ref[...] = db
            dgate_ref[:, sl] = (dm * b_ref[...].astype(F32) * gt * (1.0 - gt)).astype(BF16)
            cw = D_MODEL // N_CHIPS
            dy = _dot_nt(db[:, :cw], w_ref[0])
            for c in range(1, N_CHIPS):
                dy = dy + _dot_nt(db[:, cw * c:cw * (c + 1)], w_ref[c])
            dy_ref[...] = dy.astype(dy_ref.dtype)

    return _call(
        body, name="merge_bwd", grid=(T // tm,),
        in_specs=[_rows(tm, D_MODEL), _rows(tm, GATE_WIDTH), _rows(tm, D_MODEL), _rows(tm, D_MODEL),
                  _const((D_MODEL, D_MODEL)), _const(w_bp.shape), _const(w_ba.shape)],
        out_specs=[_rows(tm, D_MODEL), _rows(tm, D_MODEL), _rows(tm, GATE_WIDTH), _rows(tm, POOL_WIDTH),
                   _rows(tm, ATTN_WIDTH)],
        out_shape=[_sds((T, D_MODEL), BF16), _sds((T, D_MODEL), BF16), _sds((T, GATE_WIDTH), BF16),
                   _sds((T, POOL_WIDTH), F32), _sds((T, ATTN_WIDTH), BF16)],
        args=(dmix, gate, bp, ba, w_out, w_bp, w_ba), sem=("parallel",), exchanges=exchanges)


def _attn_bwd(q, k, v, do, sinks, tabs, seq, exchanges=()):
    T = q.shape[0]
    nb = seq // BLOCK
    steps = nb + 1

    def body(sink_ref, q_ref, do_ref, kp_ref, kc_ref, vp_ref, vc_ref, c_ref, a_ref, bt_ref, cp_ref, ap_ref, btp_ref,
             dq_ref, dk_ref, dv_ref, dsink_ref, ck_ref, cv_ref):
        i = pl.program_id(0)
        n = i % steps

        @pl.when(i == 0)
        def _():
            dsink_ref[...] = jnp.zeros_like(dsink_ref)

        @pl.when(n == 0)
        def _():
            ck_ref[...] = jnp.zeros_like(ck_ref)
            cv_ref[...] = jnp.zeros_like(cv_ref)

        @pl.when(n < nb)
        def _():
            kk = jnp.concatenate([kp_ref[...], kc_ref[...]], axis=0)
            vv = jnp.concatenate([vp_ref[...], vc_ref[...]], axis=0)
            valid, lo = _attn_masks(n)
            dk_acc = jnp.zeros((2 * BLOCK, KV_WIDTH), F32)
            dv_acc = jnp.zeros((2 * BLOCK, KV_WIDTH), F32)
            for h in range(2):
                qs = _stack_heads(q_ref, h, lo)
                dos = _stack_heads(do_ref, h, lo)
                pr, ps = _group_probs(qs, kk, valid, _sink_rows(sink_ref, h))
                dp = _dot_nt(dos, vv)
                delta = jnp.sum(pr * dp, axis=1, keepdims=True)
                ds = (pr * (dp - delta)).astype(BF16)
                dsk = ps * delta
                for g in range(GROUP):
                    idx = GROUP * h + g
                    dsink_ref[idx:idx + 1, :] += jnp.zeros((1, LANES), F32) - jnp.sum(dsk[BLOCK * g:BLOCK * (g + 1)])
                dk_acc = dk_acc + _dot_tn(ds, qs)
                dv_acc = dv_acc + _dot_tn(pr.astype(BF16), dos)
                for j, pair in enumerate(_unstack_heads(_dot(ds, kk) * SCALE, h, lo)):
                    sl = slice(LANES * (2 * h + j), LANES * (2 * h + j + 1))
                    dq_ref[:, sl] = _rot_bwd(pair, c_ref[...], a_ref[...], bt_ref[...]).astype(BF16)
            fin_k = ck_ref[...] + dk_acc[:BLOCK]
            dk_ref[...] = _rot_bwd(fin_k, cp_ref[...], ap_ref[...], btp_ref[...]).astype(BF16)
            dv_ref[...] = (cv_ref[...] + dv_acc[:BLOCK]).astype(BF16)
            ck_ref[...] = dk_acc[BLOCK:]
            cv_ref[...] = dv_acc[BLOCK:]

        @pl.when(n == nb)
        def _():
            dk_ref[...] = _rot_bwd(ck_ref[...], cp_ref[...], ap_ref[...], btp_ref[...]).astype(BF16)
            dv_ref[...] = cv_ref[...].astype(BF16)

    def blk(i):
        return (i // steps) * nb

    cur = lambda i: (blk(i) + jnp.minimum(i % steps, nb - 1), 0)
    prv = lambda i: (blk(i) + jnp.clip(i % steps - 1, 0, nb - 1), 0)
    tcur = lambda i: (jnp.minimum(i % steps, nb - 1), 0)
    tprv = lambda i: (jnp.clip(i % steps - 1, 0, nb - 1), 0)
    kv = lambda m: pl.BlockSpec((BLOCK, KV_WIDTH), m)
    return _call(
        body, name="attn_bwd", grid=((T // seq) * steps,),
        in_specs=[pl.BlockSpec(memory_space=pltpu.SMEM),
                  pl.BlockSpec((BLOCK, ATTN_WIDTH), cur), pl.BlockSpec((BLOCK, ATTN_WIDTH), cur),
                  kv(prv), kv(cur), kv(prv), kv(cur),
                  kv(tcur), kv(tcur), kv(tcur), kv(tprv), kv(tprv), kv(tprv)],
        out_specs=[pl.BlockSpec((BLOCK, ATTN_WIDTH), cur), kv(prv), kv(prv), _const((8, LANES))],
        out_shape=[_sds((T, ATTN_WIDTH), BF16), _sds((T, KV_WIDTH), BF16), _sds((T, KV_WIDTH), BF16),
                   _sds((8, LANES), F32)],
        scratch=[pltpu.VMEM((BLOCK, KV_WIDTH), F32), pltpu.VMEM((BLOCK, KV_WIDTH), F32)],
        args=(sinks, q, do, k, k, v, v, *tabs, *tabs), sem=("arbitrary",), exchanges=exchanges)


def _pool_bwd(dyp, diff, w_pool, pool_scale, seq, exchanges=()):
    T = dyp.shape[0]
    tp = min(TP, seq)
    nseq = seq // tp
    per = tp // HALO
    last_halo = T // HALO - 1

    def body(dy_ref, nxt_ref, diff_ref, w_ref, s_ref, du_ref, dw_ref, ds_ref):
        i = pl.program_id(0)

        @pl.when(i == 0)
        def _():
            dw_ref[...] = jnp.zeros_like(dw_ref)
            ds_ref[...] = jnp.zeros_like(ds_ref)

        last = (i % nseq) == nseq - 1
        nxt = jnp.where(last, 0.0, nxt_ref[...])
        ext = jnp.concatenate([dy_ref[...], nxt], axis=0) * s_ref[...]
        pos = (i % nseq) * tp + lax.broadcasted_iota(jnp.int32, (tp + HALO, 1), 0)
        for gi, w in enumerate(POOL_WINDOWS):
            sl = slice(POOL_GC * gi, POOL_GC * (gi + 1))
            wg = w_ref[gi].astype(BF16)
            dmx = ext[:, sl].astype(BF16)
            ddiff = _dot_nt(dmx, wg)
            s = ddiff * _inv_count(pos, w)
            sh = 1
            while sh < w:
                s = s + pltpu.roll(s, tp + HALO - sh, 0)
                sh *= 2
            du_ref[:, sl] = (s[:tp] - ddiff[:tp]).astype(BF16)
            dg = diff_ref[:, sl]
            dw_ref[gi] += _dot_tn(dg, dmx[:tp])
            ds_ref[:, sl] += jnp.sum(dy_ref[:, sl] * _dot(dg, wg), axis=0, keepdims=True)

    return _call(
        body, name="pool_bwd", grid=(T // tp,),
        in_specs=[_rows(tp, POOL_WIDTH),
                  pl.BlockSpec((HALO, POOL_WIDTH), lambda i: (jnp.minimum((i + 1) * per, last_halo), 0)),
                  _rows(tp, POOL_WIDTH), _const((4, POOL_GC, POOL_GC)), _const((1, POOL_WIDTH))],
        out_specs=[_rows(tp, POOL_WIDTH), _const((4, POOL_GC, POOL_GC)), _const((1, POOL_WIDTH))],
        out_shape=[_sds((T, POOL_WIDTH), BF16), _sds((4, POOL_GC, POOL_GC), F32), _sds((1, POOL_WIDTH), F32)],
        args=(dyp, dyp, diff, w_pool, pool_scale), sem=("arbitrary",), exchanges=exchanges)


_PARTS = ((0, C_Q), (C_Q, C_K), (C_K, C_V), (C_V, C_G), (C_G, IN_WIDTH))


def _inproj_bwd(parts, x2, dx1, w_in_t, g1, exchanges=()):
    T = x2.shape[0]
    tm = min(TM, T)

    def body(du_ref, dq_ref, dk_ref, dv_ref, dgt_ref, x_ref, dx1_ref, w_ref, g_ref, gx_ref, dg_ref):
        @pl.when(pl.program_id(0) == 0)
        def _():
            dg_ref[...] = jnp.zeros_like(dg_ref)

        dh = jnp.zeros((tm, D_MODEL), F32)
        for (lo, hi), p_ref in zip(_PARTS, (du_ref, dq_ref, dk_ref, dv_ref, dgt_ref)):
            dh = dh + _dot(p_ref[...], w_ref[lo:hi, :])
        dx, dg = _norm_bwd(x_ref[...], g_ref[...], dh)
        gx_ref[...] = dx1_ref[...] + dx
        dg_ref[...] += dg

    return _call(
        body, name="inproj_bwd", grid=(T // tm,),
        in_specs=[_rows(tm, hi - lo) for lo, hi in _PARTS]
        + [_rows(tm, D_MODEL), _rows(tm, D_MODEL), _const((IN_WIDTH, D_MODEL)), _const((1, D_MODEL))],
        out_specs=[_rows(tm, D_MODEL), _const((1, D_MODEL))],
        out_shape=[_sds((T, D_MODEL), F32), _sds((1, D_MODEL), F32)],
        args=(*parts, x2, dx1, w_in_t, g1), sem=("arbitrary",), exchanges=exchanges)


def _dw_in(h, parts, exchanges=()):
    T = h.shape[0]
    tk = min(TM, T)

    def body(h_ref, du_ref, dq_ref, dk_ref, dv_ref, dgt_ref, o_ref, db_ref):
        @pl.when(pl.program_id(0) == 0)
        def _():
            o_ref[...] = jnp.zeros_like(o_ref)
            db_ref[...] = jnp.zeros_like(db_ref)

        hh = h_ref[...]
        for (lo, hi), p_ref in zip(_PARTS, (du_ref, dq_ref, dk_ref, dv_ref, dgt_ref)):
            part = p_ref[...]
            o_ref[lo:hi, :] += _dot_tn(part, hh)
            db_ref[:, lo:hi] += jnp.sum(part.astype(F32), axis=0, keepdims=True)

    return _call(
        body, name="dw_in", grid=(T // tk,),
        in_specs=[_rows(tk, D_MODEL)] + [_rows(tk, hi - lo) for lo, hi in _PARTS],
        out_specs=[_const((IN_WIDTH, D_MODEL)), _const((1, IN_WIDTH))],
        out_shape=[_sds((IN_WIDTH, D_MODEL), F32), _sds((1, IN_WIDTH), F32)],
        args=(h, *parts), sem=("arbitrary",), exchanges=exchanges)


def _row_tile(rows, cap=256, mult=16):
    best = None
    for t in range(mult, min(rows, cap) + 1, mult):
        if rows % t == 0:
            best = t
    if best is None:
        raise ValueError("no row tile for %d rows" % rows)
    return best


def _pair_sum(ids, full, got):
    _, r, c = full.shape
    hr = r // 2
    tr = _row_tile(hr)
    nblk = hr // tr

    def body(ids_ref, a_ref, b_ref, s_ref, sb_ref):
        s = a_ref[...] + b_ref[...]
        s_ref[...] = s
        sb_ref[...] = s.astype(BF16)

    out_spec = pl.BlockSpec((None, tr, c), lambda j, i, ids_ref: (j, i, 0))
    return pl.pallas_call(
        body, name="pair_sum_%dx%d" % (r, c),
        grid_spec=pltpu.PrefetchScalarGridSpec(
            num_scalar_prefetch=1, grid=(N_CHIPS, nblk),
            in_specs=[pl.BlockSpec((None, tr, c), lambda j, i, ids_ref: (j, ids_ref[1] * nblk + i, 0)), out_spec],
            out_specs=[out_spec, out_spec]),
        out_shape=[_sds((N_CHIPS, hr, c), F32), _sds((N_CHIPS, hr, c), BF16)],
        compiler_params=_cp("parallel", "parallel"),
    )(ids, full, got)


def _chip_sum(ids, own, got):
    _, hr, c = own.shape
    tr = _row_tile(hr)
    nblk = hr // tr

    def body(ids_ref, a_ref, b_ref, o_ref):
        o_ref[...] = ((a_ref[...] + b_ref[0].astype(F32)) + b_ref[1].astype(F32)) + b_ref[2].astype(F32)

    return pl.pallas_call(
        body, name="chip_sum_%dx%d" % (hr, c),
        grid_spec=pltpu.PrefetchScalarGridSpec(
            num_scalar_prefetch=1, grid=(nblk,),
            in_specs=[pl.BlockSpec((None, tr, c), lambda i, ids_ref: (ids_ref[0], i, 0)),
                      pl.BlockSpec((3, tr, c), lambda i, ids_ref: (0, i, 0))],
            out_specs=pl.BlockSpec((tr, c), lambda i, ids_ref: (ids_ref[1] * nblk + i, 0))),
        out_shape=_sds((2 * hr, c), F32),
        compiler_params=_cp("parallel"),
    )(ids, own, got)


def _sum_blocks(tag, allb, m_per):
    def body(a_ref, o_ref):
        acc = a_ref[0:m_per, :]
        for d in range(1, N_DEV):
            acc = acc + a_ref[d * m_per:(d + 1) * m_per, :]
        o_ref[...] = acc

    return pl.pallas_call(body, name="sum_blocks_" + tag, out_shape=_sds((m_per, allb.shape[1]), F32))(allb)


def _adamw(w, g, m, v):
    r, c = w.shape
    tr = _row_tile(r, mult=8)

    def body(w_ref, g_ref, m_ref, v_ref, d_ref, nm_ref, nv_ref):
        gg = g_ref[...]
        nm = ADAM_B1 * m_ref[...] + (1.0 - ADAM_B1) * gg
        nv = ADAM_B2 * v_ref[...] + (1.0 - ADAM_B2) * jnp.square(gg)
        m_hat = nm / (1.0 - ADAM_B1 ** ADAM_STEP)
        v_hat = nv / (1.0 - ADAM_B2 ** ADAM_STEP)
        d_ref[...] = -ADAM_LR * (m_hat / (jnp.sqrt(v_hat) + ADAM_EPS) + ADAM_WD * w_ref[...])
        nm_ref[...] = nm
        nv_ref[...] = nv

    spec = _rows(tr, c)
    return pl.pallas_call(
        body, name="adamw_%dx%d" % (r, c), grid=(r // tr,),
        in_specs=[spec] * 4, out_specs=[spec] * 3, out_shape=[_sds((r, c), F32)] * 3,
        compiler_params=_cp("parallel"),
    )(w, g, m, v)


_SMALL = (("w_pool", 4 * POOL_GC * POOL_GC), ("b_in", IN_WIDTH), ("g_mix_pre", D_MODEL), ("g_mix_post", D_MODEL),
          ("g_mlp_pre", D_MODEL), ("g_mlp_post", D_MODEL), ("pool_scale", POOL_WIDTH), ("attn_sinks", N_Q_HEADS),
          ("loss", 1))


def _pack_small(vals):
    parts = []
    for name, size in _SMALL:
        flat = vals[name].reshape(-1).astype(F32)
        padded = -(-size // (8 * LANES)) * (8 * LANES)
        parts.append(jnp.pad(flat, (0, padded - size)).reshape(-1, LANES))
    return jnp.concatenate(parts, axis=0)


def _small_row(name):
    row = 0
    for part, size in _SMALL:
        if part == name:
            return row
        row += -(-size // (8 * LANES)) * 8
    raise KeyError(name)


def _unpack_small(packed, shapes):
    out, row = {}, 0
    for name, size in _SMALL:
        nrows = -(-size // (8 * LANES)) * 8
        out[name] = packed[row:row + nrows].reshape(-1)[:size].reshape(shapes[name])
        row += nrows
    return out


_BIG = ("w_in", "w_branch_pool", "w_branch_attn", "w_out", "w_up", "w_down")
_ORDER = ("g_mix_pre", "w_in", "b_in", "w_pool", "pool_scale", "attn_sinks", "w_branch_pool", "w_branch_attn",
          "w_out", "g_mix_post", "g_mlp_pre", "w_up", "w_down", "g_mlp_post")


def _stack_rows(slab):
    return slab.reshape(-1, slab.shape[2])


def _step(x2, tgt, seq, shards, small, ids):
    tabs = _rope_tables(seq)
    g1, g2, g3, g4 = (small[n] for n in ("g_mix_pre", "g_mix_post", "g_mlp_pre", "g_mlp_post"))
    sinks = small["attn_sinks"].reshape(N_Q_HEADS)
    w_pool = small["w_pool"].reshape(4, POOL_GC, POOL_GC)
    pool_scale = small["pool_scale"]

    def whole(name, slabs):
        return lax.dynamic_update_slice(slabs, shards[name][None], (ids[0], 0, 0))

    w_in = _stack_rows(whole("w_in", _alone("gather_in", _ex_gather([shards["w_in"]]))[0][0]))
    mix_names = ("w_branch_pool", "w_branch_attn", "w_out")
    (h, u, q, k, v, gate), [mix_slabs] = _inproj(
        x2, g1, w_in, small["b_in"], tabs, seq, exchanges=[_ex_gather([shards[n] for n in mix_names])])
    w_bp, w_ba, out_slab = (whole(n, s) for n, s in zip(mix_names, mix_slabs))
    w_out = _stack_rows(out_slab)
    diff, y_pool = _pool_fwd(u, w_pool, pool_scale, seq)
    (y_attn,), [[w_up]] = _attn_fwd(q, k, v, sinks, seq, exchanges=[_ex_gather([shards["w_up"]])])
    w_up = whole("w_up", w_up)
    (bp, ba, merged, mix, x1, h2), [[down_slab]] = _merge_out(
        y_pool, y_attn, gate, x2, w_bp, w_ba, w_out, g2, g3, exchanges=[_ex_gather([shards["w_down"]])])
    w_down = _stack_rows(whole("w_down", down_slab))
    up, act = _mlp_up(h2, w_up)
    dff, dy, loss_acc, dg4 = _mlp_down_loss(act, x1, tgt, w_down, g4)

    dup = _mlp_down_bwd(dff, up, w_down)
    dw_down = _dw("down", act, dff, 1024, 1024)[0].reshape(N_CHIPS, D_FF // N_CHIPS, D_MODEL)
    (dx1, dmix, dg3, dg2), [[got]] = _mlp_up_bwd(dup, dy, x1, mix, w_up, g3, g2, exchanges=[_ex_pair([dw_down])])
    ps_down = _pair_sum(ids, dw_down, got)
    (dw_up,), [[got]] = _dw("up", h2, dup, 1024, 1024, shard_cols=True, exchanges=[_ex_chip([ps_down[1]])])
    half_down = _chip_sum(ids, ps_down[0], got)
    (dbp, dba, dgate, dyp, dya), [[got], [g_down]] = _merge_bwd(
        dmix, gate, bp, ba, w_out, w_bp, w_ba, exchanges=[_ex_pair([dw_up]), _ex_swap([half_down])])
    ps_up = _pair_sum(ids, dw_up, got)
    dw_mix = [_dw("out", merged, dmix, 1024, 1024)[0].reshape(N_CHIPS, D_MODEL // N_CHIPS, D_MODEL),
              _dw_slabs("branch_pool", y_pool, dbp), _dw_slabs("branch_attn", y_attn, dba)]
    (dq, dk, dv, dsink), [[got], gots] = _attn_bwd(
        q, k, v, dya, sinks, tabs, seq, exchanges=[_ex_chip([ps_up[1]]), _ex_pair(dw_mix)])
    half_up = _chip_sum(ids, ps_up[0], got)
    ps_mix = [_pair_sum(ids, d, g) for d, g in zip(dw_mix, gots)]
    (du, dw_pool, dps), [[g_up]] = _pool_bwd(dyp, diff, w_pool, pool_scale, seq, exchanges=[_ex_swap([half_up])])
    parts = (du, dq, dk, dv, dgate)
    (dw_in_t, db_in), [gots] = _dw_in(h, parts, exchanges=[_ex_chip([p[1] for p in ps_mix])])
    half_mix = [_chip_sum(ids, p[0], g) for p, g in zip(ps_mix, gots)]
    dw_in = dw_in_t.reshape(N_CHIPS, IN_WIDTH // N_CHIPS, D_MODEL)
    g_mix, [got] = _alone("swap_mix_pair_in", _ex_swap(half_mix), _ex_pair([dw_in]))
    ps_in = _pair_sum(ids, dw_in, got)
    little = dict(w_pool=dw_pool, b_in=db_in, g_mix_pre=jnp.zeros_like(dg2), g_mix_post=dg2, g_mlp_pre=dg3,
                  g_mlp_post=dg4, pool_scale=dps, attn_sinks=dsink[:, 0], loss=loss_acc[0, 0])
    block = _pack_small(little)
    (gx, dg1), [[got], [gathered]] = _inproj_bwd(
        parts, x2, dx1, w_in, g1, exchanges=[_ex_chip([ps_in[1]]), _ex_allgather(block)])
    gain = dg1.reshape(8, LANES)
    [g_in], [gains] = _alone("swap_in_allgather", _ex_swap([_chip_sum(ids, ps_in[0], got)]), _ex_allgather(gain))
    total = _sum_blocks("small", gathered, block.shape[0])
    row = _small_row("g_mix_pre")
    total = jnp.concatenate([total[:row], _sum_blocks("gain", gains, 8), total[row + 8:]], axis=0)

    grads = dict(w_in=g_in, w_branch_pool=g_mix[1], w_branch_attn=g_mix[2], w_out=g_mix[0], w_up=g_up, w_down=g_down)
    return total, gx, grads


def kernel(x, g_mix_pre, w_in, b_in, w_pool, pool_scale, attn_sinks, w_branch_pool, w_branch_attn, w_out, g_mix_post, g_mlp_pre, w_up, w_down, g_mlp_post, loss_target, m_g_mix_pre, m_w_in, m_b_in, m_w_pool, m_pool_scale, m_attn_sinks, m_w_branch_pool, m_w_branch_attn, m_w_out, m_g_mix_post, m_g_mlp_pre, m_w_up, m_w_down, m_g_mlp_post, v_g_mix_pre, v_w_in, v_b_in, v_w_pool, v_pool_scale, v_attn_sinks, v_w_branch_pool, v_w_branch_attn, v_w_out, v_g_mix_post, v_g_mlp_pre, v_w_up, v_w_down, v_g_mlp_post):
    weights = dict(g_mix_pre=g_mix_pre, w_in=w_in, b_in=b_in, w_pool=w_pool, pool_scale=pool_scale,
                   attn_sinks=attn_sinks, w_branch_pool=w_branch_pool, w_branch_attn=w_branch_attn, w_out=w_out,
                   g_mix_post=g_mix_post, g_mlp_pre=g_mlp_pre, w_up=w_up, w_down=w_down, g_mlp_post=g_mlp_post)
    mom1 = dict(g_mix_pre=m_g_mix_pre, w_in=m_w_in, b_in=m_b_in, w_pool=m_w_pool, pool_scale=m_pool_scale,
                attn_sinks=m_attn_sinks, w_branch_pool=m_w_branch_pool, w_branch_attn=m_w_branch_attn,
                w_out=m_w_out, g_mix_post=m_g_mix_post, g_mlp_pre=m_g_mlp_pre, w_up=m_w_up, w_down=m_w_down,
                g_mlp_post=m_g_mlp_post)
    mom2 = dict(g_mix_pre=v_g_mix_pre, w_in=v_w_in, b_in=v_b_in, w_pool=v_w_pool, pool_scale=v_pool_scale,
                attn_sinks=v_attn_sinks, w_branch_pool=v_w_branch_pool, w_branch_attn=v_w_branch_attn,
                w_out=v_w_out, g_mix_post=v_g_mix_post, g_mlp_pre=v_g_mlp_pre, w_up=v_w_up, w_down=v_w_down,
                g_mlp_post=v_g_mlp_post)
    b_loc, seq, _ = x.shape
    x2 = x.reshape(b_loc * seq, D_MODEL)
    tgt = loss_target.reshape(b_loc * seq, D_MODEL)
    ids = jnp.stack([2 * lax.axis_index("x") + lax.axis_index("y"), lax.axis_index("c")]).astype(jnp.int32)

    def flat(n, a):
        return a[0].T if n == "w_in" else a[0]

    def unflat(n, a):
        return (a.T if n == "w_in" else a)[None]

    shards = {n: flat(n, weights[n]).astype(BF16) for n in _BIG}
    small = {n: weights[n] for n in _ORDER if n not in _BIG}
    total, gx, grads = _step(x2, tgt, seq, shards, small, ids)

    small_shapes = {n: weights[n].shape for n, _ in _SMALL if n != "loss"}
    small_shapes["loss"] = ()
    total = _unpack_small(total, small_shapes)
    loss = total.pop("loss")
    grads.update(total)

    delta, new_m, new_v = {}, {}, {}
    for n in _BIG:
        d, nm, nv = _adamw(flat(n, weights[n]), grads[n], flat(n, mom1[n]), flat(n, mom2[n]))
        grads[n] = unflat(n, grads[n])
        delta[n], new_m[n], new_v[n] = unflat(n, d), unflat(n, nm), unflat(n, nv)
    packed = [_pack_small({**src, "loss": jnp.zeros((), F32)}) for src in (weights, grads, mom1, mom2)]
    for dst, res in zip((delta, new_m, new_v), _adamw(*packed)):
        dst.update({n: a for n, a in _unpack_small(res, small_shapes).items() if n != "loss"})

    return (loss, gx.reshape(x.shape), *[grads[n] for n in _ORDER], *[delta[n] for n in _ORDER],
            *[new_m[n] for n in _ORDER], *[new_v[n] for n in _ORDER])
```

```python
import jax
import jax.numpy as jnp
from jax import lax
from jax.experimental import pallas as pl
from jax.experimental.pallas import tpu as pltpu

F32 = jnp.float32
BF16 = jnp.bfloat16

D_MODEL = 1024
POOL_WINDOWS = (2, 4, 8, 16)
POOL_WIDTH = 512
POOL_GC = 128
HALO = 16
HEAD_DIM = 64
N_Q_HEADS = 8
ATTN_WIDTH = 512
KV_WIDTH = 128
BLOCK = 128
NEG_INF = -1e30
ROPE_THETA = 500000.0
ROT_DIM = 16
GATE_WIDTH = 2048
IN_WIDTH = 3328
D_FF = 4096
EPS = 1e-6
SCALE = HEAD_DIM ** -0.5
C_Q, C_K, C_V, C_G = 512, 1024, 1152, 1280

ADAM_LR, ADAM_B1, ADAM_B2, ADAM_EPS, ADAM_WD, ADAM_STEP = 0.001, 0.9, 0.999, 1e-08, 0.01, 10

N_CHIPS = 4
N_DEV = 8
LANES = 128
TM = 512
TP = 256
VMEM_MB = 56

MESH = pl.DeviceIdType.MESH
ANY = pl.BlockSpec(memory_space=pl.ANY)


def _cp(*sem, vmem=VMEM_MB):
    return pltpu.CompilerParams(dimension_semantics=sem, vmem_limit_bytes=vmem * 1024 * 1024)


def _rows(tile, cols):
    return pl.BlockSpec((tile, cols), lambda i: (i, 0))


def _const(shape):
    nd = len(shape)
    return pl.BlockSpec(shape, lambda i: (0,) * nd)


def _sds(shape, dtype):
    return jax.ShapeDtypeStruct(shape, dtype)


def _dot(a, b):
    return jnp.dot(a, b, preferred_element_type=F32)


def _dot_nt(a, b):
    return lax.dot_general(a, b, (((1,), (1,)), ((), ())), preferred_element_type=F32)


def _dot_tn(a, b):
    return lax.dot_general(a, b, (((0,), (0,)), ((), ())), preferred_element_type=F32)


def _rms(x):
    return lax.rsqrt(jnp.mean(x * x, axis=-1, keepdims=True) + EPS)


def _norm_bwd(x, g, dout):
    r = _rms(x)
    n = x * r
    dn = dout * g
    dx = r * (dn - n * jnp.mean(dn * n, axis=-1, keepdims=True))
    return dx, jnp.sum(dout * n, axis=0, keepdims=True)


def _rot_fwd(t, c, a, bt):
    return t * c + pltpu.roll(t, LANES - 8, 1) * a + pltpu.roll(t, 8, 1) * bt


def _rot_bwd(d, c, a, bt):
    return d * c + pltpu.roll(d * a, 8, 1) + pltpu.roll(d * bt, LANES - 8, 1)


def _rope_tables(seq):
    pos = jnp.arange(seq, dtype=F32)
    inv_freq = ROPE_THETA ** (-jnp.arange(0, ROT_DIM, 2, dtype=F32) / ROT_DIM)
    ang = pos[:, None] * inv_freq[None, :]
    cos, sin = jnp.cos(ang), jnp.sin(ang)
    ones = jnp.ones((seq, HEAD_DIM - ROT_DIM), F32)
    zeros8 = jnp.zeros((seq, 8), F32)
    zrest = jnp.zeros((seq, HEAD_DIM - ROT_DIM), F32)
    c = jnp.concatenate([cos, cos, ones], axis=1)
    a = jnp.concatenate([-sin, zeros8, zrest], axis=1)
    bt = jnp.concatenate([zeros8, sin, zrest], axis=1)
    return tuple(jnp.tile(t, (1, 2)) for t in (c, a, bt))


class _Exchange:
    def __init__(self, inputs, out_shapes, sems, start, finish, aliases=None, middle=None):
        self.inputs, self.out_shapes, self.sems = list(inputs), list(out_shapes), list(sems)
        self.start, self.finish, self.aliases = start, finish, dict(aliases or {})
        self.middle = middle


def _call(body, *, name, grid, in_specs, out_specs, out_shape, args, scratch=(), sem=(), exchanges=()):
    in_specs, out_specs, out_shape, scratch = list(in_specs), list(out_specs), list(out_shape), list(scratch)
    if not exchanges:
        return pl.pallas_call(body, name=name, grid=grid, in_specs=in_specs, out_specs=out_specs,
                              out_shape=out_shape, scratch_shapes=scratch, compiler_params=_cp(*sem))(*args)
    n_in, n_out, n_scr = len(in_specs), len(out_specs), len(scratch)
    x_in = [a for ex in exchanges for a in ex.inputs]
    x_out = [s for ex in exchanges for s in ex.out_shapes]
    x_sem = [s for ex in exchanges for s in ex.sems]
    aliases, i_off, o_off = {}, n_in, n_out
    for ex in exchanges:
        for i, o in ex.aliases.items():
            aliases[i_off + i] = o_off + o
        i_off += len(ex.inputs)
        o_off += len(ex.out_shapes)

    def split(flat):
        out, pos = [], 0
        for ex, n in zip(exchanges, flat[1]):
            out.append(flat[0][pos:pos + n])
            pos += n
        return out

    def carrier(*refs):
        pos = 0
        groups = []
        for n in (n_in, len(x_in), n_out, len(x_out), n_scr, len(x_sem)):
            groups.append(refs[pos:pos + n])
            pos += n
        ins, xin, outs, xout, scr, xsem = groups
        xin = split((xin, [len(ex.inputs) for ex in exchanges]))
        xout = split((xout, [len(ex.out_shapes) for ex in exchanges]))
        xsem = split((xsem, [len(ex.sems) for ex in exchanges]))
        first = pl.program_id(0) == 0
        last = pl.program_id(0) == grid[0] - 1
        for d in range(1, len(grid)):
            first = jnp.logical_and(first, pl.program_id(d) == 0)
            last = jnp.logical_and(last, pl.program_id(d) == grid[d] - 1)

        @pl.when(first)
        def _():
            for ex, i, o, s in zip(exchanges, xin, xout, xsem):
                ex.start(i, o, s)

        if any(ex.middle for ex in exchanges):
            half = pl.program_id(0) == grid[0] // 2
            for d in range(1, len(grid)):
                half = jnp.logical_and(half, pl.program_id(d) == 0)

            @pl.when(half)
            def _():
                for ex, i, o, s in zip(exchanges, xin, xout, xsem):
                    if ex.middle:
                        ex.middle(i, o, s)

        body(*ins, *outs, *scr)

        @pl.when(last)
        def _():
            for ex, i, o, s in zip(exchanges, xin, xout, xsem):
                ex.finish(i, o, s)

    res = pl.pallas_call(
        carrier, name=name, grid=grid, in_specs=in_specs + [ANY] * len(x_in),
        out_specs=out_specs + [ANY] * len(x_out), out_shape=out_shape + x_out,
        scratch_shapes=scratch + x_sem, input_output_aliases=aliases,
        compiler_params=_cp(*(["arbitrary"] * len(grid))),
    )(*args, *x_in)
    return res[:n_out], split((res[n_out:], [len(ex.out_shapes) for ex in exchanges]))


def _alone(name, *exchanges):
    n_in = [len(ex.inputs) for ex in exchanges]
    n_out = [len(ex.out_shapes) for ex in exchanges]
    n_sem = [len(ex.sems) for ex in exchanges]
    aliases, i_off, o_off = {}, 0, 0
    for ex in exchanges:
        for i, o in ex.aliases.items():
            aliases[i_off + i] = o_off + o
        i_off += len(ex.inputs)
        o_off += len(ex.out_shapes)

    def split(flat, counts):
        out, pos = [], 0
        for n in counts:
            out.append(flat[pos:pos + n])
            pos += n
        return out

    def body(*refs):
        ins, outs, sems = split(refs, [sum(n_in), sum(n_out), sum(n_sem)])
        groups = list(zip(exchanges, split(ins, n_in), split(outs, n_out), split(sems, n_sem)))
        for ex, i, o, s in groups:
            ex.start(i, o, s)
        for ex, i, o, s in groups:
            if ex.middle:
                ex.middle(i, o, s)
        for ex, i, o, s in groups:
            ex.finish(i, o, s)

    res = pl.pallas_call(
        body, name=name, in_specs=[ANY] * sum(n_in), out_specs=[ANY] * sum(n_out),
        out_shape=[s for ex in exchanges for s in ex.out_shapes],
        scratch_shapes=[s for ex in exchanges for s in ex.sems], input_output_aliases=aliases,
    )(*[a for ex in exchanges for a in ex.inputs])
    return split(res, n_out)


def _place():
    x, y, c = lax.axis_index("x"), lax.axis_index("y"), lax.axis_index("c")
    chips = [(1 - x, y), (x, 1 - y), (1 - x, 1 - y)]
    return x, y, c, chips


def _remote(src, dst, send, recv, to):
    return pltpu.make_async_remote_copy(src_ref=src, dst_ref=dst, send_sem=send, recv_sem=recv,
                                        device_id=to, device_id_type=MESH)


def _ex_gather(shards):
    nw = len(shards)
    hrs = [s.shape[0] // 2 for s in shards]

    def copies(ins, outs, sems):
        s1, r1, s2, r2, fs, fr = sems
        x, y, c, _ = _place()
        me, xn, yn, dg = (x, y), (1 - x, y), (x, 1 - y), (1 - x, 1 - y)
        nbr = (xn, yn)
        sibling = (x, y, 1 - c)

        def piece(w, chip, core, part=None):
            hr = hrs[w]
            rows = pl.ds(core * hr, hr) if part is None else pl.ds(core * hr + part * (hr // 2), hr // 2)
            return outs[w].at[2 * chip[0] + chip[1], rows]

        def first(w, k):
            return _remote(ins[w].at[pl.ds(c * hrs[w], hrs[w])], piece(w, me, c), s1.at[w, k], r1.at[w, k],
                           (*nbr[k], c))

        def landed(w, k):
            return _remote(piece(w, nbr[k], c), piece(w, nbr[k], c), s1.at[w, k], r1.at[w, k], (*nbr[k], c))

        def onward(w, k):
            return _remote(piece(w, nbr[k], c, k), piece(w, nbr[k], c, k), s2.at[w, k], r2.at[w, k],
                           (*nbr[1 - k], c))

        def arrived(w, k):
            return _remote(piece(w, dg, c, k), piece(w, dg, c, k), s2.at[w, k], r2.at[w, k], (*nbr[1 - k], c))

        def passed(w, j):
            chip = (xn, yn, dg)[j]
            return _remote(piece(w, chip, c), piece(w, chip, c), fs.at[w, j], fr.at[w, j], sibling)

        def handed(w, j):
            chip = (xn, yn, dg)[j]
            return _remote(piece(w, chip, 1 - c), piece(w, chip, 1 - c), fs.at[w, j], fr.at[w, j], sibling)

        return first, landed, onward, arrived, passed, handed

    def start(ins, outs, sems):
        first = copies(ins, outs, sems)[0]
        for w in range(nw):
            for k in range(2):
                first(w, k).start()

    def middle(ins, outs, sems):
        _, landed, onward, _, passed, _ = copies(ins, outs, sems)
        for w in range(nw):
            for k in range(2):
                landed(w, k).wait_recv()
                onward(w, k).start()
                passed(w, k).start()

    def finish(ins, outs, sems):
        first, _, onward, arrived, passed, handed = copies(ins, outs, sems)
        for w in range(nw):
            for k in range(2):
                arrived(w, k).wait_recv()
            passed(w, 2).start()
        for w in range(nw):
            for j in range(3):
                handed(w, j).wait_recv()
        for w in range(nw):
            for k in range(2):
                first(w, k).wait_send()
                onward(w, k).wait_send()
            for j in range(3):
                passed(w, j).wait_send()

    return _Exchange(shards, [_sds((N_CHIPS,) + s.shape, s.dtype) for s in shards],
                     [pltpu.SemaphoreType.DMA((nw, 2))] * 4 + [pltpu.SemaphoreType.DMA((nw, 3))] * 2,
                     start, finish, middle=middle)


def _ex_pair(grads):
    nw = len(grads)

    def copies(ins, outs, sems):
        x, y, c, _ = _place()
        out = []
        for w in range(nw):
            hr = grads[w].shape[1] // 2
            out.append(_remote(ins[w].at[:, pl.ds((1 - c) * hr, hr)], outs[w], sems[0].at[w], sems[1].at[w],
                               (x, y, 1 - c)))
        return out

    def start(ins, outs, sems):
        for cp in copies(ins, outs, sems):
            cp.start()

    def finish(ins, outs, sems):
        for cp in copies(ins, outs, sems):
            cp.wait()

    return _Exchange(grads, [_sds((N_CHIPS, g.shape[1] // 2, g.shape[2]), F32) for g in grads],
                     [pltpu.SemaphoreType.DMA((nw,))] * 2, start, finish)


def _ex_chip(pieces):
    nw = len(pieces)

    def copies(ins, outs, sems):
        x, y, c, chips = _place()
        return [_remote(ins[w].at[2 * cx + cy], outs[w].at[k], sems[0].at[w, k], sems[1].at[w, k], (cx, cy, c))
                for w in range(nw) for k, (cx, cy) in enumerate(chips)]

    def start(ins, outs, sems):
        for cp in copies(ins, outs, sems):
            cp.start()

    def finish(ins, outs, sems):
        for cp in copies(ins, outs, sems):
            cp.wait()

    return _Exchange(pieces, [_sds((3,) + p.shape[1:], BF16) for p in pieces],
                     [pltpu.SemaphoreType.DMA((nw, 3))] * 2, start, finish)


def _ex_swap(fulls):
    nw = len(fulls)

    def start(ins, outs, sems):
        x, y, c, _ = _place()
        for w in range(nw):
            hr = fulls[w].shape[0] // 2
            mine = pl.ds(c * hr, hr)
            _remote(ins[w].at[mine], outs[w].at[mine], sems[0].at[w], sems[1].at[w], (x, y, 1 - c)).start()

    def finish(ins, outs, sems):
        x, y, c, _ = _place()
        for w in range(nw):
            hr = fulls[w].shape[0] // 2
            mine, theirs = pl.ds(c * hr, hr), pl.ds((1 - c) * hr, hr)
            _remote(ins[w].at[mine], outs[w].at[mine], sems[0].at[w], sems[1].at[w], (x, y, 1 - c)).wait_send()
            _remote(ins[w].at[theirs], outs[w].at[theirs], sems[0].at[w], sems[1].at[w], (x, y, 1 - c)).wait_recv()

    return _Exchange(fulls, [_sds(f.shape, F32) for f in fulls], [pltpu.SemaphoreType.DMA((nw,))] * 2,
                     start, finish, aliases={w: w for w in range(nw)})


def _ex_allgather(blocks):
    nb = len(blocks)

    def copies(ins, outs, sems):
        send, recv, lsem = sems
        x, y, c, chips = _place()
        me, sibling = (x, y, c), (x, y, 1 - c)

        def rows(b, px, py, pc):
            m_per = blocks[b].shape[0]
            return outs[b].at[pl.ds((4 * px + 2 * py + pc) * m_per, m_per), :]

        def copy(b, k, blk, to, src=None):
            return _remote(rows(b, *blk) if src is None else src, rows(b, *blk), send.at[b, k], recv.at[b, k], to)

        def mine(b):
            return pltpu.make_async_copy(ins[b], rows(b, *me), lsem.at[b])

        def first(b, k):
            return copy(b, k, me, sibling if k == 0 else (*chips[k - 1], c), src=ins[b])

        def passed(b, j):
            return copy(b, 4 + j, (*chips[j], c), sibling)

        def landed(b, j):
            return copy(b, 1 + j, (*chips[j], c), me)

        def handed(b, k):
            return copy(b, 0, sibling, me) if k == 0 else copy(b, 3 + k, (*chips[k - 1], 1 - c), me)

        return mine, first, passed, landed, handed

    def start(ins, outs, sems):
        mine, first, _, _, _ = copies(ins, outs, sems)
        for b in range(nb):
            mine(b).start()
            for k in range(4):
                first(b, k).start()

    def finish(ins, outs, sems):
        mine, first, passed, landed, handed = copies(ins, outs, sems)
        sent = []
        for b in range(nb):
            for j in range(3):
                landed(b, j).wait_recv()
                cp = passed(b, j)
                cp.start()
                sent.append(cp)
        for b in range(nb):
            for k in range(4):
                handed(b, k).wait_recv()
            for k in range(4):
                first(b, k).wait_send()
        for cp in sent:
            cp.wait_send()
        for b in range(nb):
            mine(b).wait()

    return _Exchange(blocks, [_sds((N_DEV * b.shape[0], b.shape[1]), F32) for b in blocks],
                     [pltpu.SemaphoreType.DMA((nb, 7)), pltpu.SemaphoreType.DMA((nb, 7)), pltpu.SemaphoreType.DMA((nb,))],
                     start, finish)


def _inproj(x2, g1, w_in_t, b_in, tabs, seq, exchanges=()):
    T = x2.shape[0]
    tm = min(TM, seq)
    nseq = seq // tm

    def body(x_ref, g_ref, w_ref, b_ref, c_ref, a_ref, bt_ref, h_ref, u_ref, q_ref, k_ref, v_ref, gate_ref):
        x = x_ref[...]
        h = (x * _rms(x) * g_ref[...]).astype(BF16)
        h_ref[...] = h

        def proj(lo, hi):
            return _dot_nt(h, w_ref[lo:hi, :]) + b_ref[:, lo:hi]

        c, a, bt = c_ref[...], a_ref[...], bt_ref[...]
        u_ref[...] = proj(0, C_Q)
        q = proj(C_Q, C_K)
        for p in range(4):
            sl = slice(LANES * p, LANES * (p + 1))
            q_ref[:, sl] = (_rot_fwd(q[:, sl], c, a, bt) * SCALE).astype(BF16)
        kv = proj(C_K, C_G)
        k_ref[...] = _rot_fwd(kv[:, :KV_WIDTH], c, a, bt).astype(BF16)
        v_ref[...] = kv[:, KV_WIDTH:].astype(BF16)
        for j in range(2):
            lo = C_G + D_MODEL * j
            gate_ref[:, D_MODEL * j:D_MODEL * (j + 1)] = jax.nn.sigmoid(proj(lo, lo + D_MODEL)).astype(BF16)

    tab = pl.BlockSpec((tm, LANES), lambda i: (i % nseq, 0))
    return _call(
        body, name="inproj", grid=(T // tm,),
        in_specs=[_rows(tm, D_MODEL), _const((1, D_MODEL)), _const((IN_WIDTH, D_MODEL)), _const((1, IN_WIDTH)),
                  tab, tab, tab],
        out_specs=[_rows(tm, D_MODEL), _rows(tm, POOL_WIDTH), _rows(tm, ATTN_WIDTH), _rows(tm, KV_WIDTH),
                   _rows(tm, KV_WIDTH), _rows(tm, GATE_WIDTH)],
        out_shape=[_sds((T, D_MODEL), BF16), _sds((T, POOL_WIDTH), F32), _sds((T, ATTN_WIDTH), BF16),
                   _sds((T, KV_WIDTH), BF16), _sds((T, KV_WIDTH), BF16), _sds((T, GATE_WIDTH), BF16)],
        args=(x2, g1, w_in_t, b_in, *tabs), sem=("parallel",), exchanges=exchanges)


def _inv_count(pos, w):
    return 1.0 / jnp.minimum(pos + 1, w).astype(F32)


def _pool_fwd(u, w_pool, pool_scale, seq):
    T = u.shape[0]
    tp = min(TP, seq)
    nseq = seq // tp
    per = tp // HALO

    def body(u_ref, prev_ref, w_ref, s_ref, diff_ref, y_ref):
        i = pl.program_id(0)
        first = (i % nseq) == 0
        prev = jnp.where(first, 0.0, prev_ref[...])
        ext = jnp.concatenate([prev, u_ref[...]], axis=0)
        pos = (i % nseq) * tp + lax.broadcasted_iota(jnp.int32, (tp, 1), 0)
        for gi, w in enumerate(POOL_WINDOWS):
            sl = slice(POOL_GC * gi, POOL_GC * (gi + 1))
            xg = ext[:, sl]
            s = xg
            sh = 1
            while sh < w:
                s = s + pltpu.roll(s, sh, 0)
                sh *= 2
            pooled = s[HALO:] * _inv_count(pos, w)
            diff = (pooled - xg[HALO:]).astype(BF16)
            diff_ref[:, sl] = diff
            mixed = _dot(diff, w_ref[gi].astype(BF16))
            y_ref[:, sl] = (mixed * s_ref[:, sl]).astype(BF16)

    return _call(
        body, name="pool_fwd", grid=(T // tp,),
        in_specs=[_rows(tp, POOL_WIDTH),
                  pl.BlockSpec((HALO, POOL_WIDTH), lambda i: (jnp.maximum(i * per - 1, 0), 0)),
                  _const((4, POOL_GC, POOL_GC)), _const((1, POOL_WIDTH))],
        out_specs=[_rows(tp, POOL_WIDTH), _rows(tp, POOL_WIDTH)],
        out_shape=[_sds((T, POOL_WIDTH), BF16), _sds((T, POOL_WIDTH), BF16)],
        args=(u, u, w_pool, pool_scale), sem=("parallel",))


GROUP = 4
GROWS = GROUP * BLOCK


def _attn_masks(n):
    qi = lax.broadcasted_iota(jnp.int32, (GROWS, 2 * BLOCK), 0) % BLOCK
    kj = lax.broadcasted_iota(jnp.int32, (GROWS, 2 * BLOCK), 1)
    rel = qi + BLOCK - kj
    valid = (rel >= 0) & (rel < BLOCK) & (kj >= jnp.where(n > 0, 0, BLOCK))
    lo = lax.broadcasted_iota(jnp.int32, (BLOCK, LANES), 1) < HEAD_DIM
    return valid, lo


def _stack_heads(ref, h, lo):
    keep = lo if h == 0 else jnp.logical_not(lo)
    pieces = []
    for p in (2 * h, 2 * h + 1):
        xp = ref[:, LANES * p:LANES * (p + 1)].astype(F32)
        for e in range(2):
            t = xp if e == h else pltpu.roll(xp, HEAD_DIM, 1)
            pieces.append(jnp.where(keep, t, 0.0).astype(BF16))
    return jnp.concatenate(pieces, axis=0)


def _unstack_heads(stacked, h, lo):
    pairs = []
    for j in range(2):
        parts = []
        for e in range(2):
            t = stacked[BLOCK * (2 * j + e):BLOCK * (2 * j + e + 1)]
            parts.append(t if e == h else pltpu.roll(t, HEAD_DIM, 1))
        pairs.append(jnp.where(lo, parts[0], parts[1]))
    return pairs


def _sink_rows(sink_ref, h):
    head = lax.broadcasted_iota(jnp.int32, (GROWS, 1), 0) // BLOCK
    col = jnp.zeros((GROWS, 1), F32) + sink_ref[GROUP * h]
    for g in range(1, GROUP):
        col = jnp.where(head == g, sink_ref[GROUP * h + g], col)
    return col


def _group_probs(qs, kk, valid, sink):
    s = jnp.where(valid, _dot_nt(qs, kk), NEG_INF)
    m = jnp.maximum(jnp.max(s, axis=1, keepdims=True), sink)
    ex = jnp.exp(s - m)
    es = jnp.exp(sink - m)
    inv = 1.0 / (jnp.sum(ex, axis=1, keepdims=True) + es)
    return ex * inv, es * inv


def _attn_fwd(q, k, v, sinks, seq, exchanges=()):
    T = q.shape[0]
    nb = seq // BLOCK

    def body(sink_ref, q_ref, kp_ref, kc_ref, vp_ref, vc_ref, o_ref):
        n = pl.program_id(0) % nb
        kk = jnp.concatenate([kp_ref[...], kc_ref[...]], axis=0)
        vv = jnp.concatenate([vp_ref[...], vc_ref[...]], axis=0)
        valid, lo = _attn_masks(n)
        for h in range(2):
            qs = _stack_heads(q_ref, h, lo)
            pr, _ = _group_probs(qs, kk, valid, _sink_rows(sink_ref, h))
            o = _dot(pr.astype(BF16), vv)
            for j, pair in enumerate(_unstack_heads(o, h, lo)):
                p = 2 * h + j
                o_ref[:, LANES * p:LANES * (p + 1)] = pair.astype(BF16)

    cur = lambda i: (i, 0)
    prv = lambda i: (jnp.where(i % nb == 0, i, i - 1), 0)
    return _call(
        body, name="attn_fwd", grid=(T // BLOCK,),
        in_specs=[pl.BlockSpec(memory_space=pltpu.SMEM),
                  pl.BlockSpec((BLOCK, ATTN_WIDTH), cur),
                  pl.BlockSpec((BLOCK, KV_WIDTH), prv), pl.BlockSpec((BLOCK, KV_WIDTH), cur),
                  pl.BlockSpec((BLOCK, KV_WIDTH), prv), pl.BlockSpec((BLOCK, KV_WIDTH), cur)],
        out_specs=[pl.BlockSpec((BLOCK, ATTN_WIDTH), cur)],
        out_shape=[_sds((T, ATTN_WIDTH), BF16)],
        args=(sinks, q, k, k, v, v), sem=("parallel",), exchanges=exchanges)


def _merge_out(y_pool, y_attn, gate, x2, w_bp, w_ba, w_out, g2, g3, exchanges=()):
    T = x2.shape[0]
    tm = min(TM, T)

    def body(yp_ref, ya_ref, gate_ref, x_ref, wbp_ref, wba_ref, wo_ref, g2_ref, g3_ref,
             bp_ref, ba_ref, mg_ref, mix_ref, x1_ref, h2_ref):
        yp, ya = yp_ref[...], ya_ref[...]
        bp = jnp.concatenate([_dot(yp, wbp_ref[j]) for j in range(N_CHIPS)], axis=1)
        ba = jnp.concatenate([_dot(ya, wba_ref[j]) for j in range(N_CHIPS)], axis=1)
        bp_ref[...] = bp.astype(BF16)
        ba_ref[...] = ba.astype(BF16)
        merged = (gate_ref[:, :D_MODEL].astype(F32) * bp + gate_ref[:, D_MODEL:].astype(F32) * ba).astype(BF16)
        mg_ref[...] = merged
        mix = _dot(merged, wo_ref[...])
        mix_ref[...] = mix
        x1 = x_ref[...] + mix * _rms(mix) * g2_ref[...]
        x1_ref[...] = x1
        h2_ref[...] = (x1 * _rms(x1) * g3_ref[...]).astype(BF16)

    return _call(
        body, name="merge_out", grid=(T // tm,),
        in_specs=[_rows(tm, POOL_WIDTH), _rows(tm, ATTN_WIDTH), _rows(tm, GATE_WIDTH), _rows(tm, D_MODEL),
                  _const(w_bp.shape), _const(w_ba.shape), _const((D_MODEL, D_MODEL)),
                  _const((1, D_MODEL)), _const((1, D_MODEL))],
        out_specs=[_rows(tm, D_MODEL)] * 6,
        out_shape=[_sds((T, D_MODEL), BF16), _sds((T, D_MODEL), BF16), _sds((T, D_MODEL), BF16),
                   _sds((T, D_MODEL), F32), _sds((T, D_MODEL), F32), _sds((T, D_MODEL), BF16)],
        args=(y_pool, y_attn, gate, x2, w_bp, w_ba, w_out, g2, g3), sem=("parallel",), exchanges=exchanges)


def _mlp_up(h2, w_up):
    T = h2.shape[0]
    tm = min(TM, T)

    def body(h_ref, w_ref, up_ref, a_ref):
        h = h_ref[...]
        for j in range(N_CHIPS):
            sl = slice(D_MODEL * j, D_MODEL * (j + 1))
            up = _dot(h, w_ref[j])
            up_ref[:, sl] = up.astype(BF16)
            a_ref[:, sl] = jnp.square(jnp.maximum(up, 0.0)).astype(BF16)

    return _call(
        body, name="mlp_up", grid=(T // tm,),
        in_specs=[_rows(tm, D_MODEL), _const((N_CHIPS, D_MODEL, D_MODEL))],
        out_specs=[_rows(tm, D_FF), _rows(tm, D_FF)],
        out_shape=[_sds((T, D_FF), BF16), _sds((T, D_FF), BF16)],
        args=(h2, w_up), sem=("parallel",))


def _mlp_down_loss(a, x1, tgt, w_down, g4):
    T = a.shape[0]
    tm = min(TM, T)

    def body(a_ref, x1_ref, t_ref, w_ref, g_ref, dff_ref, dy_ref, loss_ref, dg_ref):
        @pl.when(pl.program_id(0) == 0)
        def _():
            loss_ref[...] = jnp.zeros_like(loss_ref)
            dg_ref[...] = jnp.zeros_like(dg_ref)

        ff = _dot(a_ref[...], w_ref[...])
        g = g_ref[...]
        err = x1_ref[...] + ff * _rms(ff) * g - t_ref[...]
        loss_ref[...] += jnp.sum(err * err) * (0.5 / D_MODEL)
        dy = err * (1.0 / D_MODEL)
        dy_ref[...] = dy
        dff, dg = _norm_bwd(ff, g, dy)
        dff_ref[...] = dff.astype(BF16)
        dg_ref[...] += dg

    return _call(
        body, name="mlp_down_loss", grid=(T // tm,),
        in_specs=[_rows(tm, D_FF), _rows(tm, D_MODEL), _rows(tm, D_MODEL), _const((D_FF, D_MODEL)),
                  _const((1, D_MODEL))],
        out_specs=[_rows(tm, D_MODEL), _rows(tm, D_MODEL), _const((8, LANES)), _const((1, D_MODEL))],
        out_shape=[_sds((T, D_MODEL), BF16), _sds((T, D_MODEL), F32), _sds((8, LANES), F32),
                   _sds((1, D_MODEL), F32)],
        args=(a, x1, tgt, w_down, g4), sem=("arbitrary",))


def _mlp_down_bwd(dff, up, w_down):
    T = dff.shape[0]
    tm = min(TM, T)

    def body(d_ref, up_ref, w_ref, dup_ref):
        d = d_ref[...]
        for j in range(D_FF // D_MODEL):
            sl = slice(D_MODEL * j, D_MODEL * (j + 1))
            da = _dot_nt(d, w_ref[sl, :])
            dup_ref[:, sl] = (da * (2.0 * jnp.maximum(up_ref[:, sl].astype(F32), 0.0))).astype(BF16)

    return _call(
        body, name="mlp_down_bwd", grid=(T // tm,),
        in_specs=[_rows(tm, D_MODEL), _rows(tm, D_FF), _const((D_FF, D_MODEL))],
        out_specs=[_rows(tm, D_FF)],
        out_shape=[_sds((T, D_FF), BF16)],
        args=(dff, up, w_down), sem=("parallel",))[0]


def _mlp_up_bwd(dup, dy, x1, mix, w_up, g3, g2, exchanges=()):
    T = dup.shape[0]
    tm = min(TM, T)

    def body(dup_ref, dy_ref, x1_ref, mix_ref, w_ref, g3_ref, g2_ref, dx1_ref, dmix_ref, dg3_ref, dg2_ref):
        @pl.when(pl.program_id(0) == 0)
        def _():
            dg3_ref[...] = jnp.zeros_like(dg3_ref)
            dg2_ref[...] = jnp.zeros_like(dg2_ref)

        dh2 = _dot_nt(dup_ref[:, :D_MODEL], w_ref[0])
        for j in range(1, N_CHIPS):
            dh2 = dh2 + _dot_nt(dup_ref[:, D_MODEL * j:D_MODEL * (j + 1)], w_ref[j])
        dx, dg3 = _norm_bwd(x1_ref[...], g3_ref[...], dh2)
        dx1 = dy_ref[...] + dx
        dx1_ref[...] = dx1
        dg3_ref[...] += dg3
        dmix, dg2 = _norm_bwd(mix_ref[...], g2_ref[...], dx1)
        dmix_ref[...] = dmix.astype(BF16)
        dg2_ref[...] += dg2

    return _call(
        body, name="mlp_up_bwd", grid=(T // tm,),
        in_specs=[_rows(tm, D_FF), _rows(tm, D_MODEL), _rows(tm, D_MODEL), _rows(tm, D_MODEL),
                  _const((N_CHIPS, D_MODEL, D_MODEL)), _const((1, D_MODEL)), _const((1, D_MODEL))],
        out_specs=[_rows(tm, D_MODEL), _rows(tm, D_MODEL), _const((1, D_MODEL)), _const((1, D_MODEL))],
        out_shape=[_sds((T, D_MODEL), F32), _sds((T, D_MODEL), BF16), _sds((1, D_MODEL), F32),
                   _sds((1, D_MODEL), F32)],
        args=(dup, dy, x1, mix, w_up, g3, g2), sem=("arbitrary",), exchanges=exchanges)


def _dw(tag, a, g, ta, tn, shard_cols=False, exchanges=()):
    T, ka = a.shape
    n = g.shape[1]
    tk = min(2 * TM, T)
    nk = T // tk

    def body(a_ref, g_ref, o_ref):
        d = _dot_tn(a_ref[...], g_ref[...])

        @pl.when(pl.program_id(2) == 0)
        def _():
            o_ref[...] = d

        @pl.when(pl.program_id(2) > 0)
        def _():
            o_ref[...] += d

    if shard_cols:
        per = (n // N_CHIPS) // tn
        out_spec = pl.BlockSpec((None, ta, tn), lambda i, j, k: (j // per, i, j % per))
        out_shape = _sds((N_CHIPS, ka, n // N_CHIPS), F32)
    else:
        out_spec = pl.BlockSpec((ta, tn), lambda i, j, k: (i, j))
        out_shape = _sds((ka, n), F32)
    return _call(
        body, name="dw_" + tag, grid=(ka // ta, n // tn, nk),
        in_specs=[pl.BlockSpec((tk, ta), lambda i, j, k: (k, i)), pl.BlockSpec((tk, tn), lambda i, j, k: (k, j))],
        out_specs=[out_spec], out_shape=[out_shape],
        args=(a, g), sem=("parallel", "parallel", "arbitrary"), exchanges=exchanges)


def _dw_slabs(tag, a, g):
    T, ka = a.shape
    n = g.shape[1]
    c = n // N_CHIPS
    tk = min(TM, T)

    def body(a_ref, g_ref, o_ref):
        res = _dot_tn(a_ref[...], g_ref[...])

        @pl.when(pl.program_id(0) == 0)
        def _():
            for j in range(N_CHIPS):
                o_ref[j] = res[:, c * j:c * (j + 1)]

        @pl.when(pl.program_id(0) > 0)
        def _():
            for j in range(N_CHIPS):
                o_ref[j] += res[:, c * j:c * (j + 1)]

    return _call(
        body, name="dw_" + tag, grid=(T // tk,),
        in_specs=[_rows(tk, ka), _rows(tk, n)],
        out_specs=[_const((N_CHIPS, ka, c))], out_shape=[_sds((N_CHIPS, ka, c), F32)],
        args=(a, g), sem=("arbitrary",))[0]


def _merge_bwd(dmix, gate, bp, ba, w_out, w_bp, w_ba, exchanges=()):
    T = dmix.shape[0]
    tm = min(TM, T)

    def body(dmix_ref, gate_ref, bp_ref, ba_ref, wo_ref, wbp_ref, wba_ref,
             dbp_ref, dba_ref, dgate_ref, dyp_ref, dya_ref):
        dm = _dot_nt(dmix_ref[...], wo_ref[...])
        for j, (b_ref, db_ref, w_ref, dy_ref) in enumerate(
                ((bp_ref, dbp_ref, wbp_ref, dyp_ref), (ba_ref, dba_ref, wba_ref, dya_ref))):
            sl = slice(D_MODEL * j, D_MODEL * (j + 1))
            gt = gate_ref[:, sl].astype(F32)
            db = (dm * gt).astype(BF16)
            db_ref[...] = db
            dgate_ref[:, sl] = (dm * b_ref[...].astype(F32) * gt * (1.0 - gt)).astype(BF16)
            cw = D_MODEL // N_CHIPS
            dy = _dot_nt(db[:, :cw], w_ref[0])
            for c in range(1, N_CHIPS):
                dy = dy + _dot_nt(db[:, cw * c:cw * (c + 1)], w_ref[c])
            dy_ref[...] = dy.astype(dy_ref.dtype)

    return _call(
        body, name="merge_bwd", grid=(T // tm,),
        in_specs=[_rows(tm, D_MODEL), _rows(tm, GATE_WIDTH), _rows(tm, D_MODEL), _rows(tm, D_MODEL),
                  _const((D_MODEL, D_MODEL)), _const(w_bp.shape), _const(w_ba.shape)],
        out_specs=[_rows(tm, D_MODEL), _rows(tm, D_MODEL), _rows(tm, GATE_WIDTH), _rows(tm, POOL_WIDTH),
                   _rows(tm, ATTN_WIDTH)],
        out_shape=[_sds((T, D_MODEL), BF16), _sds((T, D_MODEL), BF16), _sds((T, GATE_WIDTH), BF16),
                   _sds((T, POOL_WIDTH), F32), _sds((T, ATTN_WIDTH), BF16)],
        args=(dmix, gate, bp, ba, w_out, w_bp, w_ba), sem=("parallel",), exchanges=exchanges)


def _attn_bwd(q, k, v, do, sinks, tabs, seq, exchanges=()):
    T = q.shape[0]
    nb = seq // BLOCK
    steps = nb + 1

    def body(sink_ref, q_ref, do_ref, kp_ref, kc_ref, vp_ref, vc_ref, c_ref, a_ref, bt_ref, cp_ref, ap_ref, btp_ref,
             dq_ref, dk_ref, dv_ref, dsink_ref, ck_ref, cv_ref):
        i = pl.program_id(0)
        n = i % steps

        @pl.when(i == 0)
        def _():
            dsink_ref[...] = jnp.zeros_like(dsink_ref)

        @pl.when(n == 0)
        def _():
            ck_ref[...] = jnp.zeros_like(ck_ref)
            cv_ref[...] = jnp.zeros_like(cv_ref)

        @pl.when(n < nb)
        def _():
            kk = jnp.concatenate([kp_ref[...], kc_ref[...]], axis=0)
            vv = jnp.concatenate([vp_ref[...], vc_ref[...]], axis=0)
            valid, lo = _attn_masks(n)
            dk_acc = jnp.zeros((2 * BLOCK, KV_WIDTH), F32)
            dv_acc = jnp.zeros((2 * BLOCK, KV_WIDTH), F32)
            for h in range(2):
                qs = _stack_heads(q_ref, h, lo)
                dos = _stack_heads(do_ref, h, lo)
                pr, ps = _group_probs(qs, kk, valid, _sink_rows(sink_ref, h))
                dp = _dot_nt(dos, vv)
                delta = jnp.sum(pr * dp, axis=1, keepdims=True)
                ds = (pr * (dp - delta)).astype(BF16)
                dsk = ps * delta
                for g in range(GROUP):
                    idx = GROUP * h + g
                    dsink_ref[idx:idx + 1, :] += jnp.zeros((1, LANES), F32) - jnp.sum(dsk[BLOCK * g:BLOCK * (g + 1)])
                dk_acc = dk_acc + _dot_tn(ds, qs)
                dv_acc = dv_acc + _dot_tn(pr.astype(BF16), dos)
                for j, pair in enumerate(_unstack_heads(_dot(ds, kk) * SCALE, h, lo)):
                    sl = slice(LANES * (2 * h + j), LANES * (2 * h + j + 1))
                    dq_ref[:, sl] = _rot_bwd(pair, c_ref[...], a_ref[...], bt_ref[...]).astype(BF16)
            fin_k = ck_ref[...] + dk_acc[:BLOCK]
            dk_ref[...] = _rot_bwd(fin_k, cp_ref[...], ap_ref[...], btp_ref[...]).astype(BF16)
            dv_ref[...] = (cv_ref[...] + dv_acc[:BLOCK]).astype(BF16)
            ck_ref[...] = dk_acc[BLOCK:]
            cv_ref[...] = dv_acc[BLOCK:]

        @pl.when(n == nb)
        def _():
            dk_ref[...] = _rot_bwd(ck_ref[...], cp_ref[...], ap_ref[...], btp_ref[...]).astype(BF16)
            dv_ref[...] = cv_ref[...].astype(BF16)

    def blk(i):
        return (i // steps) * nb

    cur = lambda i: (blk(i) + jnp.minimum(i % steps, nb - 1), 0)
    prv = lambda i: (blk(i) + jnp.clip(i % steps - 1, 0, nb - 1), 0)
    tcur = lambda i: (jnp.minimum(i % steps, nb - 1), 0)
    tprv = lambda i: (jnp.clip(i % steps - 1, 0, nb - 1), 0)
    kv = lambda m: pl.BlockSpec((BLOCK, KV_WIDTH), m)
    return _call(
        body, name="attn_bwd", grid=((T // seq) * steps,),
        in_specs=[pl.BlockSpec(memory_space=pltpu.SMEM),
                  pl.BlockSpec((BLOCK, ATTN_WIDTH), cur), pl.BlockSpec((BLOCK, ATTN_WIDTH), cur),
                  kv(prv), kv(cur), kv(prv), kv(cur),
                  kv(tcur), kv(tcur), kv(tcur), kv(tprv), kv(tprv), kv(tprv)],
        out_specs=[pl.BlockSpec((BLOCK, ATTN_WIDTH), cur), kv(prv), kv(prv), _const((8, LANES))],
        out_shape=[_sds((T, ATTN_WIDTH), BF16), _sds((T, KV_WIDTH), BF16), _sds((T, KV_WIDTH), BF16),
                   _sds((8, LANES), F32)],
        scratch=[pltpu.VMEM((BLOCK, KV_WIDTH), F32), pltpu.VMEM((BLOCK, KV_WIDTH), F32)],
        args=(sinks, q, do, k, k, v, v, *tabs, *tabs), sem=("arbitrary",), exchanges=exchanges)


def _pool_bwd(dyp, diff, w_pool, pool_scale, seq, exchanges=()):
    T = dyp.shape[0]
    tp = min(TP, seq)
    nseq = seq // tp
    per = tp // HALO
    last_halo = T // HALO - 1

    def body(dy_ref, nxt_ref, diff_ref, w_ref, s_ref, du_ref, dw_ref, ds_ref):
        i = pl.program_id(0)

        @pl.when(i == 0)
        def _():
            dw_ref[...] = jnp.zeros_like(dw_ref)
            ds_ref[...] = jnp.zeros_like(ds_ref)

        last = (i % nseq) == nseq - 1
        nxt = jnp.where(last, 0.0, nxt_ref[...])
        ext = jnp.concatenate([dy_ref[...], nxt], axis=0) * s_ref[...]
        pos = (i % nseq) * tp + lax.broadcasted_iota(jnp.int32, (tp + HALO, 1), 0)
        for gi, w in enumerate(POOL_WINDOWS):
            sl = slice(POOL_GC * gi, POOL_GC * (gi + 1))
            wg = w_ref[gi].astype(BF16)
            dmx = ext[:, sl].astype(BF16)
            ddiff = _dot_nt(dmx, wg)
            s = ddiff * _inv_count(pos, w)
            sh = 1
            while sh < w:
                s = s + pltpu.roll(s, tp + HALO - sh, 0)
                sh *= 2
            du_ref[:, sl] = (s[:tp] - ddiff[:tp]).astype(BF16)
            dg = diff_ref[:, sl]
            dw_ref[gi] += _dot_tn(dg, dmx[:tp])
            ds_ref[:, sl] += jnp.sum(dy_ref[:, sl] * _dot(dg, wg), axis=0, keepdims=True)

    return _call(
        body, name="pool_bwd", grid=(T // tp,),
        in_specs=[_rows(tp, POOL_WIDTH),
                  pl.BlockSpec((HALO, POOL_WIDTH), lambda i: (jnp.minimum((i + 1) * per, last_halo), 0)),
                  _rows(tp, POOL_WIDTH), _const((4, POOL_GC, POOL_GC)), _const((1, POOL_WIDTH))],
        out_specs=[_rows(tp, POOL_WIDTH), _const((4, POOL_GC, POOL_GC)), _const((1, POOL_WIDTH))],
        out_shape=[_sds((T, POOL_WIDTH), BF16), _sds((4, POOL_GC, POOL_GC), F32), _sds((1, POOL_WIDTH), F32)],
        args=(dyp, dyp, diff, w_pool, pool_scale), sem=("arbitrary",), exchanges=exchanges)


_PARTS = ((0, C_Q), (C_Q, C_K), (C_K, C_V), (C_V, C_G), (C_G, IN_WIDTH))


def _inproj_bwd(parts, x2, dx1, w_in_t, g1, exchanges=()):
    T = x2.shape[0]
    tm = min(TM, T)

    def body(du_ref, dq_ref, dk_ref, dv_ref, dgt_ref, x_ref, dx1_ref, w_ref, g_ref, gx_ref, dg_ref):
        @pl.when(pl.program_id(0) == 0)
        def _():
            dg_ref[...] = jnp.zeros_like(dg_ref)

        dh = jnp.zeros((tm, D_MODEL), F32)
        for (lo, hi), p_ref in zip(_PARTS, (du_ref, dq_ref, dk_ref, dv_ref, dgt_ref)):
            dh = dh + _dot(p_ref[...], w_ref[lo:hi, :])
        dx, dg = _norm_bwd(x_ref[...], g_ref[...], dh)
        gx_ref[...] = dx1_ref[...] + dx
        dg_ref[...] += dg

    return _call(
        body, name="inproj_bwd", grid=(T // tm,),
        in_specs=[_rows(tm, hi - lo) for lo, hi in _PARTS]
        + [_rows(tm, D_MODEL), _rows(tm, D_MODEL), _const((IN_WIDTH, D_MODEL)), _const((1, D_MODEL))],
        out_specs=[_rows(tm, D_MODEL), _const((1, D_MODEL))],
        out_shape=[_sds((T, D_MODEL), F32), _sds((1, D_MODEL), F32)],
        args=(*parts, x2, dx1, w_in_t, g1), sem=("arbitrary",), exchanges=exchanges)


def _dw_in(h, parts, exchanges=()):
    T = h.shape[0]
    tk = min(TM, T)

    def body(h_ref, du_ref, dq_ref, dk_ref, dv_ref, dgt_ref, o_ref, db_ref):
        @pl.when(pl.program_id(0) == 0)
        def _():
            o_ref[...] = jnp.zeros_like(o_ref)
            db_ref[...] = jnp.zeros_like(db_ref)

        hh = h_ref[...]
        for (lo, hi), p_ref in zip(_PARTS, (du_ref, dq_ref, dk_ref, dv_ref, dgt_ref)):
            part = p_ref[...]
            o_ref[lo:hi, :] += _dot_tn(part, hh)
            db_ref[:, lo:hi] += jnp.sum(part.astype(F32), axis=0, keepdims=True)

    return _call(
        body, name="dw_in", grid=(T // tk,),
        in_specs=[_rows(tk, D_MODEL)] + [_rows(tk, hi - lo) for lo, hi in _PARTS],
        out_specs=[_const((IN_WIDTH, D_MODEL)), _const((1, IN_WIDTH))],
        out_shape=[_sds((IN_WIDTH, D_MODEL), F32), _sds((1, IN_WIDTH), F32)],
        args=(h, *parts), sem=("arbitrary",), exchanges=exchanges)


def _row_tile(rows, cap=256, mult=16):
    best = None
    for t in range(mult, min(rows, cap) + 1, mult):
        if rows % t == 0:
            best = t
    if best is None:
        raise ValueError("no row tile for %d rows" % rows)
    return best


def _pair_sum(ids, full, got):
    _, r, c = full.shape
    hr = r // 2
    tr = _row_tile(hr)
    nblk = hr // tr

    def body(ids_ref, a_ref, b_ref, s_ref, sb_ref):
        s = a_ref[...] + b_ref[...]
        s_ref[...] = s
        sb_ref[...] = s.astype(BF16)

    out_spec = pl.BlockSpec((None, tr, c), lambda j, i, ids_ref: (j, i, 0))
    return pl.pallas_call(
        body, name="pair_sum_%dx%d" % (r, c),
        grid_spec=pltpu.PrefetchScalarGridSpec(
            num_scalar_prefetch=1, grid=(N_CHIPS, nblk),
            in_specs=[pl.BlockSpec((None, tr, c), lambda j, i, ids_ref: (j, ids_ref[1] * nblk + i, 0)), out_spec],
            out_specs=[out_spec, out_spec]),
        out_shape=[_sds((N_CHIPS, hr, c), F32), _sds((N_CHIPS, hr, c), BF16)],
        compiler_params=_cp("parallel", "parallel"),
    )(ids, full, got)


def _chip_sum(ids, own, got):
    _, hr, c = own.shape
    tr = _row_tile(hr)
    nblk = hr // tr

    def body(ids_ref, a_ref, b_ref, o_ref):
        o_ref[...] = ((a_ref[...] + b_ref[0].astype(F32)) + b_ref[1].astype(F32)) + b_ref[2].astype(F32)

    return pl.pallas_call(
        body, name="chip_sum_%dx%d" % (hr, c),
        grid_spec=pltpu.PrefetchScalarGridSpec(
            num_scalar_prefetch=1, grid=(nblk,),
            in_specs=[pl.BlockSpec((None, tr, c), lambda i, ids_ref: (ids_ref[0], i, 0)),
                      pl.BlockSpec((3, tr, c), lambda i, ids_ref: (0, i, 0))],
            out_specs=pl.BlockSpec((tr, c), lambda i, ids_ref: (ids_ref[1] * nblk + i, 0))),
        out_shape=_sds((2 * hr, c), F32),
        compiler_params=_cp("parallel"),
    )(ids, own, got)


def _adamw_math(w, g, m, v):
    nm = ADAM_B1 * m + (1.0 - ADAM_B1) * g
    nv = ADAM_B2 * v + (1.0 - ADAM_B2) * jnp.square(g)
    m_hat = nm / (1.0 - ADAM_B1 ** ADAM_STEP)
    v_hat = nv / (1.0 - ADAM_B2 ** ADAM_STEP)
    return -ADAM_LR * (m_hat / (jnp.sqrt(v_hat) + ADAM_EPS) + ADAM_WD * w), nm, nv


def _adamw(w, g, m, v):
    r, c = w.shape
    tr = _row_tile(r, mult=8)

    def body(w_ref, g_ref, m_ref, v_ref, d_ref, nm_ref, nv_ref):
        d_ref[...], nm_ref[...], nv_ref[...] = _adamw_math(w_ref[...], g_ref[...], m_ref[...], v_ref[...])

    spec = _rows(tr, c)
    return pl.pallas_call(
        body, name="adamw_%dx%d" % (r, c), grid=(r // tr,),
        in_specs=[spec] * 4, out_specs=[spec] * 3, out_shape=[_sds((r, c), F32)] * 3,
        compiler_params=_cp("parallel"),
    )(w, g, m, v)


_VEC = (("b_in", IN_WIDTH), ("g_mix_pre", D_MODEL), ("g_mix_post", D_MODEL), ("g_mlp_pre", D_MODEL),
        ("g_mlp_post", D_MODEL), ("pool_scale", POOL_WIDTH), ("attn_sinks", LANES), ("loss", LANES))
VEC_WIDTH = sum(width for _, width in _VEC)
_SMALL_NAMES = ("w_pool", "b_in", "g_mix_pre", "g_mix_post", "g_mlp_pre", "g_mlp_post", "pool_scale", "attn_sinks")


def _vec_at(name):
    lane = 0
    for part, width in _VEC:
        if part == name:
            return lane
        lane += width
    raise KeyError(name)


def _vec_block(vals):
    parts = []
    for name, width in _VEC:
        a = vals[name].reshape(1, -1).astype(F32)
        parts.append(jnp.pad(a, ((0, 0), (0, width - a.shape[1]))))
    return jnp.pad(jnp.concatenate(parts, axis=1), ((0, 7), (0, 0)))


def _small_update(gvec, gmat, ggain, w, m, v):
    names = _SMALL_NAMES
    n = len(names)

    def total(ref, rows):
        acc = ref[0:rows, :]
        for d in range(1, N_DEV):
            acc = acc + ref[d * rows:(d + 1) * rows, :]
        return acc

    def body(*refs):
        gvec_ref, gmat_ref, ggain_ref = refs[:3]
        w_refs, m_refs, v_refs = refs[3:3 + n], refs[3 + n:3 + 2 * n], refs[3 + 2 * n:3 + 3 * n]
        outs = refs[3 + 3 * n:]
        loss_ref, g_refs, d_refs = outs[0], outs[1:1 + n], outs[1 + n:1 + 2 * n]
        nm_refs, nv_refs = outs[1 + 2 * n:1 + 3 * n], outs[1 + 3 * n:1 + 4 * n]
        vec = total(gvec_ref, 8)[0:1, :]
        lane = _vec_at("loss")
        loss_ref[...] = jnp.sum(vec[:, lane:lane + LANES], axis=1, keepdims=True)
        for i, name in enumerate(names):
            if name == "w_pool":
                g = total(gmat_ref, 4 * POOL_GC)
            elif name == "g_mix_pre":
                g = total(ggain_ref, 8)[0:1, :]
            else:
                lane = _vec_at(name)
                g = vec[:, lane:lane + w_refs[i].shape[1]]
            g_refs[i][...] = g
            d_refs[i][...], nm_refs[i][...], nv_refs[i][...] = _adamw_math(
                w_refs[i][...], g, m_refs[i][...], v_refs[i][...])

    shapes = [_sds(w[k].shape, F32) for k in names]
    res = pl.pallas_call(
        body, name="small_update", out_shape=[_sds((1, 1), F32)] + shapes * 4,
        compiler_params=pltpu.CompilerParams(vmem_limit_bytes=VMEM_MB * 1024 * 1024),
    )(gvec, gmat, ggain, *[w[k] for k in names], *[m[k] for k in names], *[v[k] for k in names])
    loss = res[0]
    per = {k: tuple(res[1 + j * n + i] for j in range(4)) for i, k in enumerate(names)}
    return loss, per


_BIG = ("w_in", "w_branch_pool", "w_branch_attn", "w_out", "w_up", "w_down")
_ORDER = ("g_mix_pre", "w_in", "b_in", "w_pool", "pool_scale", "attn_sinks", "w_branch_pool", "w_branch_attn",
          "w_out", "g_mix_post", "g_mlp_pre", "w_up", "w_down", "g_mlp_post")


def _stack_rows(slab):
    return slab.reshape(-1, slab.shape[2])


def _step(x2, tgt, seq, shards, small, ids):
    tabs = _rope_tables(seq)
    g1, g2, g3, g4 = (small[n] for n in ("g_mix_pre", "g_mix_post", "g_mlp_pre", "g_mlp_post"))
    sinks = small["attn_sinks"].reshape(N_Q_HEADS)
    w_pool = small["w_pool"].reshape(4, POOL_GC, POOL_GC)
    pool_scale = small["pool_scale"]

    def whole(name, slabs):
        return lax.dynamic_update_slice(slabs, shards[name][None], (ids[0], 0, 0))

    w_in = _stack_rows(whole("w_in", _alone("gather_in", _ex_gather([shards["w_in"]]))[0][0]))
    early = ("w_branch_pool", "w_branch_attn", "w_out", "w_up")
    (h, u, q, k, v, gate), [early_slabs] = _inproj(
        x2, g1, w_in, small["b_in"], tabs, seq, exchanges=[_ex_gather([shards[n] for n in early])])
    w_bp, w_ba, out_slab, w_up = (whole(n, s) for n, s in zip(early, early_slabs))
    w_out = _stack_rows(out_slab)
    diff, y_pool = _pool_fwd(u, w_pool, pool_scale, seq)
    (y_attn,), [[down_slab]] = _attn_fwd(q, k, v, sinks, seq, exchanges=[_ex_gather([shards["w_down"]])])
    w_down = _stack_rows(whole("w_down", down_slab))
    bp, ba, merged, mix, x1, h2 = _merge_out(y_pool, y_attn, gate, x2, w_bp, w_ba, w_out, g2, g3)
    up, act = _mlp_up(h2, w_up)
    dff, dy, loss_acc, dg4 = _mlp_down_loss(act, x1, tgt, w_down, g4)

    dup = _mlp_down_bwd(dff, up, w_down)
    dw_down = _dw("down", act, dff, 1024, 1024)[0].reshape(N_CHIPS, D_FF // N_CHIPS, D_MODEL)
    (dx1, dmix, dg3, dg2), [[got]] = _mlp_up_bwd(dup, dy, x1, mix, w_up, g3, g2, exchanges=[_ex_pair([dw_down])])
    ps_down = _pair_sum(ids, dw_down, got)
    (dw_up,), [[got]] = _dw("up", h2, dup, 1024, 1024, shard_cols=True, exchanges=[_ex_chip([ps_down[1]])])
    half_down = _chip_sum(ids, ps_down[0], got)
    (dbp, dba, dgate, dyp, dya), [[got], [g_down]] = _merge_bwd(
        dmix, gate, bp, ba, w_out, w_bp, w_ba, exchanges=[_ex_pair([dw_up]), _ex_swap([half_down])])
    ps_up = _pair_sum(ids, dw_up, got)
    dw_mix = [_dw("out", merged, dmix, 1024, 1024)[0].reshape(N_CHIPS, D_MODEL // N_CHIPS, D_MODEL),
              _dw_slabs("branch_pool", y_pool, dbp), _dw_slabs("branch_attn", y_attn, dba)]
    (dq, dk, dv, dsink), [[got], gots] = _attn_bwd(
        q, k, v, dya, sinks, tabs, seq, exchanges=[_ex_chip([ps_up[1]]), _ex_pair(dw_mix)])
    half_up = _chip_sum(ids, ps_up[0], got)
    ps_mix = [_pair_sum(ids, d, g) for d, g in zip(dw_mix, gots)]
    (du, dw_pool, dps), [[g_up]] = _pool_bwd(dyp, diff, w_pool, pool_scale, seq, exchanges=[_ex_swap([half_up])])
    parts = (du, dq, dk, dv, dgate)
    (dw_in_t, db_in), [gots] = _dw_in(h, parts, exchanges=[_ex_chip([p[1] for p in ps_mix])])
    half_mix = [_chip_sum(ids, p[0], g) for p, g in zip(ps_mix, gots)]
    dw_in = dw_in_t.reshape(N_CHIPS, IN_WIDTH // N_CHIPS, D_MODEL)
    g_mix, [got] = _alone("swap_mix_pair_in", _ex_swap(half_mix), _ex_pair([dw_in]))
    ps_in = _pair_sum(ids, dw_in, got)
    vec = _vec_block(dict(b_in=db_in, g_mix_pre=jnp.zeros_like(dg2), g_mix_post=dg2, g_mlp_pre=dg3, g_mlp_post=dg4,
                          pool_scale=dps, attn_sinks=dsink[:, 0], loss=loss_acc[0:1, 0:1]))
    mat = dw_pool.reshape(4 * POOL_GC, POOL_GC)
    (gx, dg1), [[got], [gvec, gmat]] = _inproj_bwd(
        parts, x2, dx1, w_in, g1, exchanges=[_ex_chip([ps_in[1]]), _ex_allgather([vec, mat])])
    gain = jnp.pad(dg1, ((0, 7), (0, 0)))
    [g_in], [ggain] = _alone("swap_in_allgather", _ex_swap([_chip_sum(ids, ps_in[0], got)]), _ex_allgather([gain]))

    grads = dict(w_in=g_in, w_branch_pool=g_mix[1], w_branch_attn=g_mix[2], w_out=g_mix[0], w_up=g_up, w_down=g_down)
    return (gvec, gmat, ggain), gx, grads


def kernel(x, g_mix_pre, w_in, b_in, w_pool, pool_scale, attn_sinks, w_branch_pool, w_branch_attn, w_out, g_mix_post, g_mlp_pre, w_up, w_down, g_mlp_post, loss_target, m_g_mix_pre, m_w_in, m_b_in, m_w_pool, m_pool_scale, m_attn_sinks, m_w_branch_pool, m_w_branch_attn, m_w_out, m_g_mix_post, m_g_mlp_pre, m_w_up, m_w_down, m_g_mlp_post, v_g_mix_pre, v_w_in, v_b_in, v_w_pool, v_pool_scale, v_attn_sinks, v_w_branch_pool, v_w_branch_attn, v_w_out, v_g_mix_post, v_g_mlp_pre, v_w_up, v_w_down, v_g_mlp_post):
    weights = dict(g_mix_pre=g_mix_pre, w_in=w_in, b_in=b_in, w_pool=w_pool, pool_scale=pool_scale,
                   attn_sinks=attn_sinks, w_branch_pool=w_branch_pool, w_branch_attn=w_branch_attn, w_out=w_out,
                   g_mix_post=g_mix_post, g_mlp_pre=g_mlp_pre, w_up=w_up, w_down=w_down, g_mlp_post=g_mlp_post)
    mom1 = dict(g_mix_pre=m_g_mix_pre, w_in=m_w_in, b_in=m_b_in, w_pool=m_w_pool, pool_scale=m_pool_scale,
                attn_sinks=m_attn_sinks, w_branch_pool=m_w_branch_pool, w_branch_attn=m_w_branch_attn,
                w_out=m_w_out, g_mix_post=m_g_mix_post, g_mlp_pre=m_g_mlp_pre, w_up=m_w_up, w_down=m_w_down,
                g_mlp_post=m_g_mlp_post)
    mom2 = dict(g_mix_pre=v_g_mix_pre, w_in=v_w_in, b_in=v_b_in, w_pool=v_w_pool, pool_scale=v_pool_scale,
                attn_sinks=v_attn_sinks, w_branch_pool=v_w_branch_pool, w_branch_attn=v_w_branch_attn,
                w_out=v_w_out, g_mix_post=v_g_mix_post, g_mlp_pre=v_g_mlp_pre, w_up=v_w_up, w_down=v_w_down,
                g_mlp_post=v_g_mlp_post)
    b_loc, seq, _ = x.shape
    x2 = x.reshape(b_loc * seq, D_MODEL)
    tgt = loss_target.reshape(b_loc * seq, D_MODEL)
    ids = jnp.stack([2 * lax.axis_index("x") + lax.axis_index("y"), lax.axis_index("c")]).astype(jnp.int32)

    def flat(n, a):
        return a[0].T if n == "w_in" else a[0]

    def unflat(n, a):
        return (a.T if n == "w_in" else a)[None]

    shards = {n: flat(n, weights[n]).astype(BF16) for n in _BIG}
    small = {n: weights[n] for n in _ORDER if n not in _BIG}
    (gvec, gmat, ggain), gx, grads = _step(x2, tgt, seq, shards, small, ids)

    def two_d(src):
        return {n: src[n].reshape(4 * POOL_GC, POOL_GC) if n == "w_pool" else src[n] for n in _SMALL_NAMES}

    loss, per = _small_update(gvec, gmat, ggain, two_d(weights), two_d(mom1), two_d(mom2))
    delta, new_m, new_v = {}, {}, {}
    for n in _SMALL_NAMES:
        grads[n], delta[n], new_m[n], new_v[n] = (a.reshape(weights[n].shape) for a in per[n])
    for n in _BIG:
        d, nm, nv = _adamw(flat(n, weights[n]), grads[n], flat(n, mom1[n]), flat(n, mom2[n]))
        grads[n] = unflat(n, grads[n])
        delta[n], new_m[n], new_v[n] = unflat(n, d), unflat(n, nm), unflat(n, nv)

    return (loss[0, 0], gx.reshape(x.shape), *[grads[n] for n in _ORDER], *[delta[n] for n in _ORDER],
            *[new_m[n] for n in _ORDER], *[new_v[n] for n in _ORDER])
```

```python
import jax
import jax.numpy as jnp
from jax import lax
from jax.experimental import pallas as pl
from jax.experimental.pallas import tpu as pltpu

F32 = jnp.float32
BF16 = jnp.bfloat16

D_MODEL = 1024
POOL_WINDOWS = (2, 4, 8, 16)
POOL_WIDTH = 512
POOL_GC = 128
HALO = 16
HEAD_DIM = 64
N_Q_HEADS = 8
ATTN_WIDTH = 512
KV_WIDTH = 128
BLOCK = 128
NEG_INF = -1e30
ROPE_THETA = 500000.0
ROT_DIM = 16
GATE_WIDTH = 2048
IN_WIDTH = 3328
D_FF = 4096
EPS = 1e-6
SCALE = HEAD_DIM ** -0.5
C_Q, C_K, C_V, C_G = 512, 1024, 1152, 1280

ADAM_LR, ADAM_B1, ADAM_B2, ADAM_EPS, ADAM_WD, ADAM_STEP = 0.001, 0.9, 0.999, 1e-08, 0.01, 10

N_CHIPS = 4
N_DEV = 8
LANES = 128
TM = 512
TP = 512
VMEM_MB = 56

MESH = pl.DeviceIdType.MESH
ANY = pl.BlockSpec(memory_space=pl.ANY)


def _cp(*sem, vmem=VMEM_MB):
    return pltpu.CompilerParams(dimension_semantics=sem, vmem_limit_bytes=vmem * 1024 * 1024)


def _rows(tile, cols):
    return pl.BlockSpec((tile, cols), lambda i: (i, 0))


def _const(shape):
    nd = len(shape)
    return pl.BlockSpec(shape, lambda i: (0,) * nd)


def _sds(shape, dtype):
    return jax.ShapeDtypeStruct(shape, dtype)


def _dot(a, b):
    return jnp.dot(a, b, preferred_element_type=F32)


def _dot_nt(a, b):
    return lax.dot_general(a, b, (((1,), (1,)), ((), ())), preferred_element_type=F32)


def _dot_tn(a, b):
    return lax.dot_general(a, b, (((0,), (0,)), ((), ())), preferred_element_type=F32)


def _rms(x):
    return lax.rsqrt(jnp.mean(x * x, axis=-1, keepdims=True) + EPS)


def _norm_bwd(x, g, dout):
    r = _rms(x)
    n = x * r
    dn = dout * g
    dx = r * (dn - n * jnp.mean(dn * n, axis=-1, keepdims=True))
    return dx, jnp.sum(dout * n, axis=0, keepdims=True)


def _rot_fwd(t, c, a, bt):
    return t * c + pltpu.roll(t, LANES - 8, 1) * a + pltpu.roll(t, 8, 1) * bt


def _rot_bwd(d, c, a, bt):
    return d * c + pltpu.roll(d * a, 8, 1) + pltpu.roll(d * bt, LANES - 8, 1)


def _rope_tables(seq):
    pos = jnp.arange(seq, dtype=F32)
    inv_freq = ROPE_THETA ** (-jnp.arange(0, ROT_DIM, 2, dtype=F32) / ROT_DIM)
    ang = pos[:, None] * inv_freq[None, :]
    cos, sin = jnp.cos(ang), jnp.sin(ang)
    ones = jnp.ones((seq, HEAD_DIM - ROT_DIM), F32)
    zeros8 = jnp.zeros((seq, 8), F32)
    zrest = jnp.zeros((seq, HEAD_DIM - ROT_DIM), F32)
    c = jnp.concatenate([cos, cos, ones], axis=1)
    a = jnp.concatenate([-sin, zeros8, zrest], axis=1)
    bt = jnp.concatenate([zeros8, sin, zrest], axis=1)
    return tuple(jnp.tile(t, (1, 2)) for t in (c, a, bt))


class _Exchange:
    def __init__(self, inputs, out_shapes, sems, start, finish, aliases=None, middle=None):
        self.inputs, self.out_shapes, self.sems = list(inputs), list(out_shapes), list(sems)
        self.start, self.finish, self.aliases = start, finish, dict(aliases or {})
        self.middle = middle


def _call(body, *, name, grid, in_specs, out_specs, out_shape, args, scratch=(), sem=(), exchanges=()):
    in_specs, out_specs, out_shape, scratch = list(in_specs), list(out_specs), list(out_shape), list(scratch)
    if not exchanges:
        return pl.pallas_call(body, name=name, grid=grid, in_specs=in_specs, out_specs=out_specs,
                              out_shape=out_shape, scratch_shapes=scratch, compiler_params=_cp(*sem))(*args)
    n_in, n_out, n_scr = len(in_specs), len(out_specs), len(scratch)
    x_in = [a for ex in exchanges for a in ex.inputs]
    x_out = [s for ex in exchanges for s in ex.out_shapes]
    x_sem = [s for ex in exchanges for s in ex.sems]
    aliases, i_off, o_off = {}, n_in, n_out
    for ex in exchanges:
        for i, o in ex.aliases.items():
            aliases[i_off + i] = o_off + o
        i_off += len(ex.inputs)
        o_off += len(ex.out_shapes)

    def split(flat):
        out, pos = [], 0
        for ex, n in zip(exchanges, flat[1]):
            out.append(flat[0][pos:pos + n])
            pos += n
        return out

    def carrier(*refs):
        pos = 0
        groups = []
        for n in (n_in, len(x_in), n_out, len(x_out), n_scr, len(x_sem)):
            groups.append(refs[pos:pos + n])
            pos += n
        ins, xin, outs, xout, scr, xsem = groups
        xin = split((xin, [len(ex.inputs) for ex in exchanges]))
        xout = split((xout, [len(ex.out_shapes) for ex in exchanges]))
        xsem = split((xsem, [len(ex.sems) for ex in exchanges]))
        first = pl.program_id(0) == 0
        last = pl.program_id(0) == grid[0] - 1
        for d in range(1, len(grid)):
            first = jnp.logical_and(first, pl.program_id(d) == 0)
            last = jnp.logical_and(last, pl.program_id(d) == grid[d] - 1)

        @pl.when(first)
        def _():
            for ex, i, o, s in zip(exchanges, xin, xout, xsem):
                ex.start(i, o, s)

        if any(ex.middle for ex in exchanges):
            half = pl.program_id(0) == grid[0] // 2
            for d in range(1, len(grid)):
                half = jnp.logical_and(half, pl.program_id(d) == 0)

            @pl.when(half)
            def _():
                for ex, i, o, s in zip(exchanges, xin, xout, xsem):
                    if ex.middle:
                        ex.middle(i, o, s)

        body(*ins, *outs, *scr)

        @pl.when(last)
        def _():
            for ex, i, o, s in zip(exchanges, xin, xout, xsem):
                ex.finish(i, o, s)

    res = pl.pallas_call(
        carrier, name=name, grid=grid, in_specs=in_specs + [ANY] * len(x_in),
        out_specs=out_specs + [ANY] * len(x_out), out_shape=out_shape + x_out,
        scratch_shapes=scratch + x_sem, input_output_aliases=aliases,
        compiler_params=_cp(*(["arbitrary"] * len(grid))),
    )(*args, *x_in)
    return res[:n_out], split((res[n_out:], [len(ex.out_shapes) for ex in exchanges]))


def _alone(name, *exchanges):
    n_in = [len(ex.inputs) for ex in exchanges]
    n_out = [len(ex.out_shapes) for ex in exchanges]
    n_sem = [len(ex.sems) for ex in exchanges]
    aliases, i_off, o_off = {}, 0, 0
    for ex in exchanges:
        for i, o in ex.aliases.items():
            aliases[i_off + i] = o_off + o
        i_off += len(ex.inputs)
        o_off += len(ex.out_shapes)

    def split(flat, counts):
        out, pos = [], 0
        for n in counts:
            out.append(flat[pos:pos + n])
            pos += n
        return out

    def body(*refs):
        ins, outs, sems = split(refs, [sum(n_in), sum(n_out), sum(n_sem)])
        groups = list(zip(exchanges, split(ins, n_in), split(outs, n_out), split(sems, n_sem)))
        for ex, i, o, s in groups:
            ex.start(i, o, s)
        for ex, i, o, s in groups:
            if ex.middle:
                ex.middle(i, o, s)
        for ex, i, o, s in groups:
            ex.finish(i, o, s)

    res = pl.pallas_call(
        body, name=name, in_specs=[ANY] * sum(n_in), out_specs=[ANY] * sum(n_out),
        out_shape=[s for ex in exchanges for s in ex.out_shapes],
        scratch_shapes=[s for ex in exchanges for s in ex.sems], input_output_aliases=aliases,
    )(*[a for ex in exchanges for a in ex.inputs])
    return split(res, n_out)


def _place():
    x, y, c = lax.axis_index("x"), lax.axis_index("y"), lax.axis_index("c")
    chips = [(1 - x, y), (x, 1 - y), (1 - x, 1 - y)]
    return x, y, c, chips


def _remote(src, dst, send, recv, to):
    return pltpu.make_async_remote_copy(src_ref=src, dst_ref=dst, send_sem=send, recv_sem=recv,
                                        device_id=to, device_id_type=MESH)


def _ex_gather(shards):
    nw = len(shards)
    hrs = [s.shape[0] // 2 for s in shards]

    def copies(ins, outs, sems):
        s1, r1, s2, r2, fs, fr = sems
        x, y, c, _ = _place()
        me, xn, yn, dg = (x, y), (1 - x, y), (x, 1 - y), (1 - x, 1 - y)
        nbr = (xn, yn)
        sibling = (x, y, 1 - c)

        def piece(w, chip, core, part=None):
            hr = hrs[w]
            rows = pl.ds(core * hr, hr) if part is None else pl.ds(core * hr + part * (hr // 2), hr // 2)
            return outs[w].at[2 * chip[0] + chip[1], rows]

        def first(w, k):
            return _remote(ins[w].at[pl.ds(c * hrs[w], hrs[w])], piece(w, me, c), s1.at[w, k], r1.at[w, k],
                           (*nbr[k], c))

        def landed(w, k):
            return _remote(piece(w, nbr[k], c), piece(w, nbr[k], c), s1.at[w, k], r1.at[w, k], (*nbr[k], c))

        def onward(w, k):
            return _remote(piece(w, nbr[k], c, k), piece(w, nbr[k], c, k), s2.at[w, k], r2.at[w, k],
                           (*nbr[1 - k], c))

        def arrived(w, k):
            return _remote(piece(w, dg, c, k), piece(w, dg, c, k), s2.at[w, k], r2.at[w, k], (*nbr[1 - k], c))

        def passed(w, j):
            chip = (xn, yn, dg)[j]
            return _remote(piece(w, chip, c), piece(w, chip, c), fs.at[w, j], fr.at[w, j], sibling)

        def handed(w, j):
            chip = (xn, yn, dg)[j]
            return _remote(piece(w, chip, 1 - c), piece(w, chip, 1 - c), fs.at[w, j], fr.at[w, j], sibling)

        return first, landed, onward, arrived, passed, handed

    def start(ins, outs, sems):
        first = copies(ins, outs, sems)[0]
        for w in range(nw):
            for k in range(2):
                first(w, k).start()

    def middle(ins, outs, sems):
        _, landed, onward, _, passed, _ = copies(ins, outs, sems)
        for w in range(nw):
            for k in range(2):
                landed(w, k).wait_recv()
                onward(w, k).start()
                passed(w, k).start()

    def finish(ins, outs, sems):
        first, _, onward, arrived, passed, handed = copies(ins, outs, sems)
        for w in range(nw):
            for k in range(2):
                arrived(w, k).wait_recv()
            passed(w, 2).start()
        for w in range(nw):
            for j in range(3):
                handed(w, j).wait_recv()
        for w in range(nw):
            for k in range(2):
                first(w, k).wait_send()
                onward(w, k).wait_send()
            for j in range(3):
                passed(w, j).wait_send()

    return _Exchange(shards, [_sds((N_CHIPS,) + s.shape, s.dtype) for s in shards],
                     [pltpu.SemaphoreType.DMA((nw, 2))] * 4 + [pltpu.SemaphoreType.DMA((nw, 3))] * 2,
                     start, finish, middle=middle)


def _ex_pair(grads):
    nw = len(grads)

    def copies(ins, outs, sems):
        x, y, c, _ = _place()
        out = []
        for w in range(nw):
            hr = grads[w].shape[1] // 2
            out.append(_remote(ins[w].at[:, pl.ds((1 - c) * hr, hr)], outs[w], sems[0].at[w], sems[1].at[w],
                               (x, y, 1 - c)))
        return out

    def start(ins, outs, sems):
        for cp in copies(ins, outs, sems):
            cp.start()

    def finish(ins, outs, sems):
        for cp in copies(ins, outs, sems):
            cp.wait()

    return _Exchange(grads, [_sds((N_CHIPS, g.shape[1] // 2, g.shape[2]), F32) for g in grads],
                     [pltpu.SemaphoreType.DMA((nw,))] * 2, start, finish)


def _ex_chip(pieces):
    nw = len(pieces)

    def copies(ins, outs, sems):
        x, y, c, chips = _place()
        return [_remote(ins[w].at[2 * cx + cy], outs[w].at[k], sems[0].at[w, k], sems[1].at[w, k], (cx, cy, c))
                for w in range(nw) for k, (cx, cy) in enumerate(chips)]

    def start(ins, outs, sems):
        for cp in copies(ins, outs, sems):
            cp.start()

    def finish(ins, outs, sems):
        for cp in copies(ins, outs, sems):
            cp.wait()

    return _Exchange(pieces, [_sds((3,) + p.shape[1:], BF16) for p in pieces],
                     [pltpu.SemaphoreType.DMA((nw, 3))] * 2, start, finish)


def _ex_swap(fulls):
    nw = len(fulls)

    def start(ins, outs, sems):
        x, y, c, _ = _place()
        for w in range(nw):
            hr = fulls[w].shape[0] // 2
            mine = pl.ds(c * hr, hr)
            _remote(ins[w].at[mine], outs[w].at[mine], sems[0].at[w], sems[1].at[w], (x, y, 1 - c)).start()

    def finish(ins, outs, sems):
        x, y, c, _ = _place()
        for w in range(nw):
            hr = fulls[w].shape[0] // 2
            mine, theirs = pl.ds(c * hr, hr), pl.ds((1 - c) * hr, hr)
            _remote(ins[w].at[mine], outs[w].at[mine], sems[0].at[w], sems[1].at[w], (x, y, 1 - c)).wait_send()
            _remote(ins[w].at[theirs], outs[w].at[theirs], sems[0].at[w], sems[1].at[w], (x, y, 1 - c)).wait_recv()

    return _Exchange(fulls, [_sds(f.shape, F32) for f in fulls], [pltpu.SemaphoreType.DMA((nw,))] * 2,
                     start, finish, aliases={w: w for w in range(nw)})


def _ex_allgather(blocks):
    nb = len(blocks)

    def copies(ins, outs, sems):
        send, recv, lsem = sems
        x, y, c, chips = _place()
        me, sibling = (x, y, c), (x, y, 1 - c)

        def rows(b, px, py, pc):
            m_per = blocks[b].shape[0]
            return outs[b].at[pl.ds((4 * px + 2 * py + pc) * m_per, m_per), :]

        def copy(b, k, blk, to, src=None):
            return _remote(rows(b, *blk) if src is None else src, rows(b, *blk), send.at[b, k], recv.at[b, k], to)

        def mine(b):
            return pltpu.make_async_copy(ins[b], rows(b, *me), lsem.at[b])

        def first(b, k):
            return copy(b, k, me, sibling if k == 0 else (*chips[k - 1], c), src=ins[b])

        def passed(b, j):
            return copy(b, 4 + j, (*chips[j], c), sibling)

        def landed(b, j):
            return copy(b, 1 + j, (*chips[j], c), me)

        def handed(b, k):
            return copy(b, 0, sibling, me) if k == 0 else copy(b, 3 + k, (*chips[k - 1], 1 - c), me)

        return mine, first, passed, landed, handed

    def start(ins, outs, sems):
        mine, first, _, _, _ = copies(ins, outs, sems)
        for b in range(nb):
            mine(b).start()
            for k in range(4):
                first(b, k).start()

    def finish(ins, outs, sems):
        mine, first, passed, landed, handed = copies(ins, outs, sems)
        sent = []
        for b in range(nb):
            for j in range(3):
                landed(b, j).wait_recv()
                cp = passed(b, j)
                cp.start()
                sent.append(cp)
        for b in range(nb):
            for k in range(4):
                handed(b, k).wait_recv()
            for k in range(4):
                first(b, k).wait_send()
        for cp in sent:
            cp.wait_send()
        for b in range(nb):
            mine(b).wait()

    return _Exchange(blocks, [_sds((N_DEV * b.shape[0], b.shape[1]), F32) for b in blocks],
                     [pltpu.SemaphoreType.DMA((nb, 7)), pltpu.SemaphoreType.DMA((nb, 7)), pltpu.SemaphoreType.DMA((nb,))],
                     start, finish)


def _inproj(x2, g1, w_in_t, b_in, tabs, seq, exchanges=()):
    T = x2.shape[0]
    tm = min(TM, seq)
    nseq = seq // tm

    def body(x_ref, g_ref, w_ref, b_ref, c_ref, a_ref, bt_ref, h_ref, u_ref, q_ref, k_ref, v_ref, gate_ref):
        x = x_ref[...]
        h = (x * _rms(x) * g_ref[...]).astype(BF16)
        h_ref[...] = h

        def proj(lo, hi):
            return _dot_nt(h, w_ref[lo:hi, :]) + b_ref[:, lo:hi]

        c, a, bt = c_ref[...], a_ref[...], bt_ref[...]
        u_ref[...] = proj(0, C_Q)
        q = proj(C_Q, C_K)
        for p in range(4):
            sl = slice(LANES * p, LANES * (p + 1))
            q_ref[:, sl] = (_rot_fwd(q[:, sl], c, a, bt) * SCALE).astype(BF16)
        kv = proj(C_K, C_G)
        k_ref[...] = _rot_fwd(kv[:, :KV_WIDTH], c, a, bt).astype(BF16)
        v_ref[...] = kv[:, KV_WIDTH:].astype(BF16)
        for j in range(2):
            lo = C_G + D_MODEL * j
            gate_ref[:, D_MODEL * j:D_MODEL * (j + 1)] = jax.nn.sigmoid(proj(lo, lo + D_MODEL)).astype(BF16)

    tab = pl.BlockSpec((tm, LANES), lambda i: (i % nseq, 0))
    return _call(
        body, name="inproj", grid=(T // tm,),
        in_specs=[_rows(tm, D_MODEL), _const((1, D_MODEL)), _const((IN_WIDTH, D_MODEL)), _const((1, IN_WIDTH)),
                  tab, tab, tab],
        out_specs=[_rows(tm, D_MODEL), _rows(tm, POOL_WIDTH), _rows(tm, ATTN_WIDTH), _rows(tm, KV_WIDTH),
                   _rows(tm, KV_WIDTH), _rows(tm, GATE_WIDTH)],
        out_shape=[_sds((T, D_MODEL), BF16), _sds((T, POOL_WIDTH), F32), _sds((T, ATTN_WIDTH), BF16),
                   _sds((T, KV_WIDTH), BF16), _sds((T, KV_WIDTH), BF16), _sds((T, GATE_WIDTH), BF16)],
        args=(x2, g1, w_in_t, b_in, *tabs), sem=("parallel",), exchanges=exchanges)


def _inv_count(pos, w):
    return 1.0 / jnp.minimum(pos + 1, w).astype(F32)


def _pool_fwd(u, w_pool, pool_scale, seq):
    T = u.shape[0]
    tp = min(TP, seq)
    nseq = seq // tp
    per = tp // HALO

    def body(u_ref, prev_ref, w_ref, s_ref, diff_ref, y_ref):
        i = pl.program_id(0)
        first = (i % nseq) == 0
        prev = jnp.where(first, 0.0, prev_ref[...])
        ext = jnp.concatenate([prev, u_ref[...]], axis=0)
        pos = (i % nseq) * tp + lax.broadcasted_iota(jnp.int32, (tp, 1), 0)
        for gi, w in enumerate(POOL_WINDOWS):
            sl = slice(POOL_GC * gi, POOL_GC * (gi + 1))
            xg = ext[:, sl]
            s = xg
            sh = 1
            while sh < w:
                s = s + pltpu.roll(s, sh, 0)
                sh *= 2
            pooled = s[HALO:] * _inv_count(pos, w)
            diff = (pooled - xg[HALO:]).astype(BF16)
            diff_ref[:, sl] = diff
            mixed = _dot(diff, w_ref[gi].astype(BF16))
            y_ref[:, sl] = (mixed * s_ref[:, sl]).astype(BF16)

    return _call(
        body, name="pool_fwd", grid=(T // tp,),
        in_specs=[_rows(tp, POOL_WIDTH),
                  pl.BlockSpec((HALO, POOL_WIDTH), lambda i: (jnp.maximum(i * per - 1, 0), 0)),
                  _const((4, POOL_GC, POOL_GC)), _const((1, POOL_WIDTH))],
        out_specs=[_rows(tp, POOL_WIDTH), _rows(tp, POOL_WIDTH)],
        out_shape=[_sds((T, POOL_WIDTH), BF16), _sds((T, POOL_WIDTH), BF16)],
        args=(u, u, w_pool, pool_scale), sem=("parallel",))


GROUP = 4
GROWS = GROUP * BLOCK


def _attn_masks(n):
    qi = lax.broadcasted_iota(jnp.int32, (GROWS, 2 * BLOCK), 0) % BLOCK
    kj = lax.broadcasted_iota(jnp.int32, (GROWS, 2 * BLOCK), 1)
    rel = qi + BLOCK - kj
    valid = (rel >= 0) & (rel < BLOCK) & (kj >= jnp.where(n > 0, 0, BLOCK))
    lo = lax.broadcasted_iota(jnp.int32, (BLOCK, LANES), 1) < HEAD_DIM
    return valid, lo


def _stack_heads(ref, h, lo):
    keep = lo if h == 0 else jnp.logical_not(lo)
    pieces = []
    for p in (2 * h, 2 * h + 1):
        xp = ref[:, LANES * p:LANES * (p + 1)].astype(F32)
        for e in range(2):
            t = xp if e == h else pltpu.roll(xp, HEAD_DIM, 1)
            pieces.append(jnp.where(keep, t, 0.0).astype(BF16))
    return jnp.concatenate(pieces, axis=0)


def _unstack_heads(stacked, h, lo):
    pairs = []
    for j in range(2):
        parts = []
        for e in range(2):
            t = stacked[BLOCK * (2 * j + e):BLOCK * (2 * j + e + 1)]
            parts.append(t if e == h else pltpu.roll(t, HEAD_DIM, 1))
        pairs.append(jnp.where(lo, parts[0], parts[1]))
    return pairs


def _sink_rows(sink_ref, h):
    head = lax.broadcasted_iota(jnp.int32, (GROWS, 1), 0) // BLOCK
    col = jnp.zeros((GROWS, 1), F32) + sink_ref[GROUP * h]
    for g in range(1, GROUP):
        col = jnp.where(head == g, sink_ref[GROUP * h + g], col)
    return col


def _group_probs(qs, kk, valid, sink):
    s = jnp.where(valid, _dot_nt(qs, kk), NEG_INF)
    m = jnp.maximum(jnp.max(s, axis=1, keepdims=True), sink)
    ex = jnp.exp(s - m)
    es = jnp.exp(sink - m)
    inv = 1.0 / (jnp.sum(ex, axis=1, keepdims=True) + es)
    return ex * inv, es * inv


def _attn_fwd(q, k, v, sinks, seq, exchanges=()):
    T = q.shape[0]
    nb = seq // BLOCK

    def body(sink_ref, q_ref, kp_ref, kc_ref, vp_ref, vc_ref, o_ref):
        n = pl.program_id(0) % nb
        kk = jnp.concatenate([kp_ref[...], kc_ref[...]], axis=0)
        vv = jnp.concatenate([vp_ref[...], vc_ref[...]], axis=0)
        valid, lo = _attn_masks(n)
        for h in range(2):
            qs = _stack_heads(q_ref, h, lo)
            pr, _ = _group_probs(qs, kk, valid, _sink_rows(sink_ref, h))
            o = _dot(pr.astype(BF16), vv)
            for j, pair in enumerate(_unstack_heads(o, h, lo)):
                p = 2 * h + j
                o_ref[:, LANES * p:LANES * (p + 1)] = pair.astype(BF16)

    cur = lambda i: (i, 0)
    prv = lambda i: (jnp.where(i % nb == 0, i, i - 1), 0)
    return _call(
        body, name="attn_fwd", grid=(T // BLOCK,),
        in_specs=[pl.BlockSpec(memory_space=pltpu.SMEM),
                  pl.BlockSpec((BLOCK, ATTN_WIDTH), cur),
                  pl.BlockSpec((BLOCK, KV_WIDTH), prv), pl.BlockSpec((BLOCK, KV_WIDTH), cur),
                  pl.BlockSpec((BLOCK, KV_WIDTH), prv), pl.BlockSpec((BLOCK, KV_WIDTH), cur)],
        out_specs=[pl.BlockSpec((BLOCK, ATTN_WIDTH), cur)],
        out_shape=[_sds((T, ATTN_WIDTH), BF16)],
        args=(sinks, q, k, k, v, v), sem=("parallel",), exchanges=exchanges)


def _merge_out(y_pool, y_attn, gate, x2, w_bp, w_ba, w_out, g2, g3, exchanges=()):
    T = x2.shape[0]
    tm = min(TM, T)

    def body(yp_ref, ya_ref, gate_ref, x_ref, wbp_ref, wba_ref, wo_ref, g2_ref, g3_ref,
             bp_ref, ba_ref, mg_ref, mix_ref, x1_ref, h2_ref):
        yp, ya = yp_ref[...], ya_ref[...]
        bp = jnp.concatenate([_dot(yp, wbp_ref[j]) for j in range(N_CHIPS)], axis=1)
        ba = jnp.concatenate([_dot(ya, wba_ref[j]) for j in range(N_CHIPS)], axis=1)
        bp_ref[...] = bp.astype(BF16)
        ba_ref[...] = ba.astype(BF16)
        merged = (gate_ref[:, :D_MODEL].astype(F32) * bp + gate_ref[:, D_MODEL:].astype(F32) * ba).astype(BF16)
        mg_ref[...] = merged
        mix = _dot(merged, wo_ref[...])
        mix_ref[...] = mix
        x1 = x_ref[...] + mix * _rms(mix) * g2_ref[...]
        x1_ref[...] = x1
        h2_ref[...] = (x1 * _rms(x1) * g3_ref[...]).astype(BF16)

    return _call(
        body, name="merge_out", grid=(T // tm,),
        in_specs=[_rows(tm, POOL_WIDTH), _rows(tm, ATTN_WIDTH), _rows(tm, GATE_WIDTH), _rows(tm, D_MODEL),
                  _const(w_bp.shape), _const(w_ba.shape), _const((D_MODEL, D_MODEL)),
                  _const((1, D_MODEL)), _const((1, D_MODEL))],
        out_specs=[_rows(tm, D_MODEL)] * 6,
        out_shape=[_sds((T, D_MODEL), BF16), _sds((T, D_MODEL), BF16), _sds((T, D_MODEL), BF16),
                   _sds((T, D_MODEL), F32), _sds((T, D_MODEL), F32), _sds((T, D_MODEL), BF16)],
        args=(y_pool, y_attn, gate, x2, w_bp, w_ba, w_out, g2, g3), sem=("parallel",), exchanges=exchanges)


def _mlp_up(h2, w_up, exchanges=()):
    T = h2.shape[0]
    tm = min(TM, T)

    def body(h_ref, w_ref, up_ref, a_ref):
        h = h_ref[...]
        for j in range(N_CHIPS):
            sl = slice(D_MODEL * j, D_MODEL * (j + 1))
            up = _dot(h, w_ref[j])
            up_ref[:, sl] = up.astype(BF16)
            a_ref[:, sl] = jnp.square(jnp.maximum(up, 0.0)).astype(BF16)

    return _call(
        body, name="mlp_up", grid=(T // tm,),
        in_specs=[_rows(tm, D_MODEL), _const((N_CHIPS, D_MODEL, D_MODEL))],
        out_specs=[_rows(tm, D_FF), _rows(tm, D_FF)],
        out_shape=[_sds((T, D_FF), BF16), _sds((T, D_FF), BF16)],
        args=(h2, w_up), sem=("parallel",), exchanges=exchanges)


def _mlp_down_loss(a, x1, tgt, w_down, g4):
    T = a.shape[0]
    tm = min(TM, T)

    def body(a_ref, x1_ref, t_ref, w_ref, g_ref, dff_ref, dy_ref, loss_ref, dg_ref):
        @pl.when(pl.program_id(0) == 0)
        def _():
            loss_ref[...] = jnp.zeros_like(loss_ref)
            dg_ref[...] = jnp.zeros_like(dg_ref)

        ff = _dot(a_ref[...], w_ref[...])
        g = g_ref[...]
        err = x1_ref[...] + ff * _rms(ff) * g - t_ref[...]
        loss_ref[...] += jnp.sum(err * err) * (0.5 / D_MODEL)
        dy = err * (1.0 / D_MODEL)
        dy_ref[...] = dy
        dff, dg = _norm_bwd(ff, g, dy)
        dff_ref[...] = dff.astype(BF16)
        dg_ref[...] += dg

    return _call(
        body, name="mlp_down_loss", grid=(T // tm,),
        in_specs=[_rows(tm, D_FF), _rows(tm, D_MODEL), _rows(tm, D_MODEL), _const((D_FF, D_MODEL)),
                  _const((1, D_MODEL))],
        out_specs=[_rows(tm, D_MODEL), _rows(tm, D_MODEL), _const((8, LANES)), _const((1, D_MODEL))],
        out_shape=[_sds((T, D_MODEL), BF16), _sds((T, D_MODEL), F32), _sds((8, LANES), F32),
                   _sds((1, D_MODEL), F32)],
        args=(a, x1, tgt, w_down, g4), sem=("arbitrary",))


def _mlp_down_bwd(dff, up, w_down):
    T = dff.shape[0]
    tm = min(TM, T)

    def body(d_ref, up_ref, w_ref, dup_ref):
        d = d_ref[...]
        for j in range(D_FF // D_MODEL):
            sl = slice(D_MODEL * j, D_MODEL * (j + 1))
            da = _dot_nt(d, w_ref[sl, :])
            dup_ref[:, sl] = (da * (2.0 * jnp.maximum(up_ref[:, sl].astype(F32), 0.0))).astype(BF16)

    return _call(
        body, name="mlp_down_bwd", grid=(T // tm,),
        in_specs=[_rows(tm, D_MODEL), _rows(tm, D_FF), _const((D_FF, D_MODEL))],
        out_specs=[_rows(tm, D_FF)],
        out_shape=[_sds((T, D_FF), BF16)],
        args=(dff, up, w_down), sem=("parallel",))[0]


def _mlp_up_bwd(dup, dy, x1, mix, w_up, g3, g2, exchanges=()):
    T = dup.shape[0]
    tm = min(TM, T)

    def body(dup_ref, dy_ref, x1_ref, mix_ref, w_ref, g3_ref, g2_ref, dx1_ref, dmix_ref, dg3_ref, dg2_ref):
        @pl.when(pl.program_id(0) == 0)
        def _():
            dg3_ref[...] = jnp.zeros_like(dg3_ref)
            dg2_ref[...] = jnp.zeros_like(dg2_ref)

        dh2 = _dot_nt(dup_ref[:, :D_MODEL], w_ref[0])
        for j in range(1, N_CHIPS):
            dh2 = dh2 + _dot_nt(dup_ref[:, D_MODEL * j:D_MODEL * (j + 1)], w_ref[j])
        dx, dg3 = _norm_bwd(x1_ref[...], g3_ref[...], dh2)
        dx1 = dy_ref[...] + dx
        dx1_ref[...] = dx1
        dg3_ref[...] += dg3
        dmix, dg2 = _norm_bwd(mix_ref[...], g2_ref[...], dx1)
        dmix_ref[...] = dmix.astype(BF16)
        dg2_ref[...] += dg2

    return _call(
        body, name="mlp_up_bwd", grid=(T // tm,),
        in_specs=[_rows(tm, D_FF), _rows(tm, D_MODEL), _rows(tm, D_MODEL), _rows(tm, D_MODEL),
                  _const((N_CHIPS, D_MODEL, D_MODEL)), _const((1, D_MODEL)), _const((1, D_MODEL))],
        out_specs=[_rows(tm, D_MODEL), _rows(tm, D_MODEL), _const((1, D_MODEL)), _const((1, D_MODEL))],
        out_shape=[_sds((T, D_MODEL), F32), _sds((T, D_MODEL), BF16), _sds((1, D_MODEL), F32),
                   _sds((1, D_MODEL), F32)],
        args=(dup, dy, x1, mix, w_up, g3, g2), sem=("arbitrary",), exchanges=exchanges)


def _dw(tag, a, g, ta, tn, shard_cols=False, exchanges=()):
    T, ka = a.shape
    n = g.shape[1]
    tk = min(TM, T)
    nk = T // tk

    def body(a_ref, g_ref, o_ref):
        d = _dot_tn(a_ref[...], g_ref[...])

        @pl.when(pl.program_id(2) == 0)
        def _():
            o_ref[...] = d

        @pl.when(pl.program_id(2) > 0)
        def _():
            o_ref[...] += d

    if shard_cols:
        per = (n // N_CHIPS) // tn
        out_spec = pl.BlockSpec((None, ta, tn), lambda i, j, k: (j // per, i, j % per))
        out_shape = _sds((N_CHIPS, ka, n // N_CHIPS), F32)
    else:
        out_spec = pl.BlockSpec((ta, tn), lambda i, j, k: (i, j))
        out_shape = _sds((ka, n), F32)
    return _call(
        body, name="dw_" + tag, grid=(ka // ta, n // tn, nk),
        in_specs=[pl.BlockSpec((tk, ta), lambda i, j, k: (k, i)), pl.BlockSpec((tk, tn), lambda i, j, k: (k, j))],
        out_specs=[out_spec], out_shape=[out_shape],
        args=(a, g), sem=("parallel", "parallel", "arbitrary"), exchanges=exchanges)


def _dw_slabs(tag, a, g):
    T, ka = a.shape
    n = g.shape[1]
    c = n // N_CHIPS
    tk = min(TM, T)

    def body(a_ref, g_ref, o_ref):
        res = _dot_tn(a_ref[...], g_ref[...])

        @pl.when(pl.program_id(0) == 0)
        def _():
            for j in range(N_CHIPS):
                o_ref[j] = res[:, c * j:c * (j + 1)]

        @pl.when(pl.program_id(0) > 0)
        def _():
            for j in range(N_CHIPS):
                o_ref[j] += res[:, c * j:c * (j + 1)]

    return _call(
        body, name="dw_" + tag, grid=(T // tk,),
        in_specs=[_rows(tk, ka), _rows(tk, n)],
        out_specs=[_const((N_CHIPS, ka, c))], out_shape=[_sds((N_CHIPS, ka, c), F32)],
        args=(a, g), sem=("arbitrary",))[0]


def _merge_bwd(dmix, gate, bp, ba, w_out, w_bp, w_ba, exchanges=()):
    T = dmix.shape[0]
    tm = min(TM, T)

    def body(dmix_ref, gate_ref, bp_ref, ba_ref, wo_ref, wbp_ref, wba_ref,
             dbp_ref, dba_ref, dgate_ref, dyp_ref, dya_ref):
        dm = _dot_nt(dmix_ref[...], wo_ref[...])
        for j, (b_ref, db_ref, w_ref, dy_ref) in enumerate(
                ((bp_ref, dbp_ref, wbp_ref, dyp_ref), (ba_ref, dba_ref, wba_ref, dya_ref))):
            sl = slice(D_MODEL * j, D_MODEL * (j + 1))
            gt = gate_ref[:, sl].astype(F32)
            db = (dm * gt).astype(BF16)
            db_ref[...] = db
            dgate_ref[:, sl] = (dm * b_ref[...].astype(F32) * gt * (1.0 - gt)).astype(BF16)
            cw = D_MODEL // N_CHIPS
            dy = _dot_nt(db[:, :cw], w_ref[0])
            for c in range(1, N_CHIPS):
                dy = dy + _dot_nt(db[:, cw * c:cw * (c + 1)], w_ref[c])
            dy_ref[...] = dy.astype(dy_ref.dtype)

    return _call(
        body, name="merge_bwd", grid=(T // tm,),
        in_specs=[_rows(tm, D_MODEL), _rows(tm, GATE_WIDTH), _rows(tm, D_MODEL), _rows(tm, D_MODEL),
                  _const((D_MODEL, D_MODEL)), _const(w_bp.shape), _const(w_ba.shape)],
        out_specs=[_rows(tm, D_MODEL), _rows(tm, D_MODEL), _rows(tm, GATE_WIDTH), _rows(tm, POOL_WIDTH),
                   _rows(tm, ATTN_WIDTH)],
        out_shape=[_sds((T, D_MODEL), BF16), _sds((T, D_MODEL), BF16), _sds((T, GATE_WIDTH), BF16),
                   _sds((T, POOL_WIDTH), F32), _sds((T, ATTN_WIDTH), BF16)],
        args=(dmix, gate, bp, ba, w_out, w_bp, w_ba), sem=("parallel",), exchanges=exchanges)


def _attn_bwd(q, k, v, do, sinks, tabs, seq, exchanges=()):
    T = q.shape[0]
    nb = seq // BLOCK
    steps = nb + 1

    def body(sink_ref, q_ref, do_ref, kp_ref, kc_ref, vp_ref, vc_ref, c_ref, a_ref, bt_ref, cp_ref, ap_ref, btp_ref,
             dq_ref, dk_ref, dv_ref, dsink_ref, ck_ref, cv_ref):
        i = pl.program_id(0)
        n = i % steps

        @pl.when(i == 0)
        def _():
            dsink_ref[...] = jnp.zeros_like(dsink_ref)

        @pl.when(n == 0)
        def _():
            ck_ref[...] = jnp.zeros_like(ck_ref)
            cv_ref[...] = jnp.zeros_like(cv_ref)

        @pl.when(n < nb)
        def _():
            kk = jnp.concatenate([kp_ref[...], kc_ref[...]], axis=0)
            vv = jnp.concatenate([vp_ref[...], vc_ref[...]], axis=0)
            valid, lo = _attn_masks(n)
            dk_acc = jnp.zeros((2 * BLOCK, KV_WIDTH), F32)
            dv_acc = jnp.zeros((2 * BLOCK, KV_WIDTH), F32)
            for h in range(2):
                qs = _stack_heads(q_ref, h, lo)
                dos = _stack_heads(do_ref, h, lo)
                pr, ps = _group_probs(qs, kk, valid, _sink_rows(sink_ref, h))
                dp = _dot_nt(dos, vv)
                delta = jnp.sum(pr * dp, axis=1, keepdims=True)
                ds = (pr * (dp - delta)).astype(BF16)
                dsk = ps * delta
                for g in range(GROUP):
                    idx = GROUP * h + g
                    dsink_ref[idx:idx + 1, :] += jnp.zeros((1, LANES), F32) - jnp.sum(dsk[BLOCK * g:BLOCK * (g + 1)])
                dk_acc = dk_acc + _dot_tn(ds, qs)
                dv_acc = dv_acc + _dot_tn(pr.astype(BF16), dos)
                for j, pair in enumerate(_unstack_heads(_dot(ds, kk) * SCALE, h, lo)):
                    sl = slice(LANES * (2 * h + j), LANES * (2 * h + j + 1))
                    dq_ref[:, sl] = _rot_bwd(pair, c_ref[...], a_ref[...], bt_ref[...]).astype(BF16)
            fin_k = ck_ref[...] + dk_acc[:BLOCK]
            dk_ref[...] = _rot_bwd(fin_k, cp_ref[...], ap_ref[...], btp_ref[...]).astype(BF16)
            dv_ref[...] = (cv_ref[...] + dv_acc[:BLOCK]).astype(BF16)
            ck_ref[...] = dk_acc[BLOCK:]
            cv_ref[...] = dv_acc[BLOCK:]

        @pl.when(n == nb)
        def _():
            dk_ref[...] = _rot_bwd(ck_ref[...], cp_ref[...], ap_ref[...], btp_ref[...]).astype(BF16)
            dv_ref[...] = cv_ref[...].astype(BF16)

    def blk(i):
        return (i // steps) * nb

    cur = lambda i: (blk(i) + jnp.minimum(i % steps, nb - 1), 0)
    prv = lambda i: (blk(i) + jnp.clip(i % steps - 1, 0, nb - 1), 0)
    tcur = lambda i: (jnp.minimum(i % steps, nb - 1), 0)
    tprv = lambda i: (jnp.clip(i % steps - 1, 0, nb - 1), 0)
    kv = lambda m: pl.BlockSpec((BLOCK, KV_WIDTH), m)
    return _call(
        body, name="attn_bwd", grid=((T // seq) * steps,),
        in_specs=[pl.BlockSpec(memory_space=pltpu.SMEM),
                  pl.BlockSpec((BLOCK, ATTN_WIDTH), cur), pl.BlockSpec((BLOCK, ATTN_WIDTH), cur),
                  kv(prv), kv(cur), kv(prv), kv(cur),
                  kv(tcur), kv(tcur), kv(tcur), kv(tprv), kv(tprv), kv(tprv)],
        out_specs=[pl.BlockSpec((BLOCK, ATTN_WIDTH), cur), kv(prv), kv(prv), _const((8, LANES))],
        out_shape=[_sds((T, ATTN_WIDTH), BF16), _sds((T, KV_WIDTH), BF16), _sds((T, KV_WIDTH), BF16),
                   _sds((8, LANES), F32)],
        scratch=[pltpu.VMEM((BLOCK, KV_WIDTH), F32), pltpu.VMEM((BLOCK, KV_WIDTH), F32)],
        args=(sinks, q, do, k, k, v, v, *tabs, *tabs), sem=("arbitrary",), exchanges=exchanges)


def _pool_bwd(dyp, diff, w_pool, pool_scale, seq, exchanges=()):
    T = dyp.shape[0]
    tp = min(TP, seq)
    nseq = seq // tp
    per = tp // HALO
    last_halo = T // HALO - 1

    def body(dy_ref, nxt_ref, diff_ref, w_ref, s_ref, du_ref, dw_ref, ds_ref):
        i = pl.program_id(0)

        @pl.when(i == 0)
        def _():
            dw_ref[...] = jnp.zeros_like(dw_ref)
            ds_ref[...] = jnp.zeros_like(ds_ref)

        last = (i % nseq) == nseq - 1
        nxt = jnp.where(last, 0.0, nxt_ref[...])
        ext = jnp.concatenate([dy_ref[...], nxt], axis=0) * s_ref[...]
        pos = (i % nseq) * tp + lax.broadcasted_iota(jnp.int32, (tp + HALO, 1), 0)
        for gi, w in enumerate(POOL_WINDOWS):
            sl = slice(POOL_GC * gi, POOL_GC * (gi + 1))
            wg = w_ref[gi].astype(BF16)
            dmx = ext[:, sl].astype(BF16)
            ddiff = _dot_nt(dmx, wg)
            s = ddiff * _inv_count(pos, w)
            sh = 1
            while sh < w:
                s = s + pltpu.roll(s, tp + HALO - sh, 0)
                sh *= 2
            du_ref[:, sl] = (s[:tp] - ddiff[:tp]).astype(BF16)
            dg = diff_ref[:, sl]
            dw_ref[gi] += _dot_tn(dg, dmx[:tp])
            ds_ref[:, sl] += jnp.sum(dy_ref[:, sl] * _dot(dg, wg), axis=0, keepdims=True)

    return _call(
        body, name="pool_bwd", grid=(T // tp,),
        in_specs=[_rows(tp, POOL_WIDTH),
                  pl.BlockSpec((HALO, POOL_WIDTH), lambda i: (jnp.minimum((i + 1) * per, last_halo), 0)),
                  _rows(tp, POOL_WIDTH), _const((4, POOL_GC, POOL_GC)), _const((1, POOL_WIDTH))],
        out_specs=[_rows(tp, POOL_WIDTH), _const((4, POOL_GC, POOL_GC)), _const((1, POOL_WIDTH))],
        out_shape=[_sds((T, POOL_WIDTH), BF16), _sds((4, POOL_GC, POOL_GC), F32), _sds((1, POOL_WIDTH), F32)],
        args=(dyp, dyp, diff, w_pool, pool_scale), sem=("arbitrary",), exchanges=exchanges)


_PARTS = ((0, C_Q), (C_Q, C_K), (C_K, C_V), (C_V, C_G), (C_G, IN_WIDTH))


def _inproj_bwd(parts, x2, dx1, w_in_t, g1, exchanges=()):
    T = x2.shape[0]
    tm = min(TM, T)

    def body(du_ref, dq_ref, dk_ref, dv_ref, dgt_ref, x_ref, dx1_ref, w_ref, g_ref, gx_ref, dg_ref):
        @pl.when(pl.program_id(0) == 0)
        def _():
            dg_ref[...] = jnp.zeros_like(dg_ref)

        dh = jnp.zeros((tm, D_MODEL), F32)
        for (lo, hi), p_ref in zip(_PARTS, (du_ref, dq_ref, dk_ref, dv_ref, dgt_ref)):
            dh = dh + _dot(p_ref[...], w_ref[lo:hi, :])
        dx, dg = _norm_bwd(x_ref[...], g_ref[...], dh)
        gx_ref[...] = dx1_ref[...] + dx
        dg_ref[...] += dg

    return _call(
        body, name="inproj_bwd", grid=(T // tm,),
        in_specs=[_rows(tm, hi - lo) for lo, hi in _PARTS]
        + [_rows(tm, D_MODEL), _rows(tm, D_MODEL), _const((IN_WIDTH, D_MODEL)), _const((1, D_MODEL))],
        out_specs=[_rows(tm, D_MODEL), _const((1, D_MODEL))],
        out_shape=[_sds((T, D_MODEL), F32), _sds((1, D_MODEL), F32)],
        args=(*parts, x2, dx1, w_in_t, g1), sem=("arbitrary",), exchanges=exchanges)


def _dw_in(h, parts, exchanges=()):
    T = h.shape[0]
    tk = min(TM, T)

    def body(h_ref, du_ref, dq_ref, dk_ref, dv_ref, dgt_ref, o_ref, db_ref):
        @pl.when(pl.program_id(0) == 0)
        def _():
            o_ref[...] = jnp.zeros_like(o_ref)
            db_ref[...] = jnp.zeros_like(db_ref)

        hh = h_ref[...]
        for (lo, hi), p_ref in zip(_PARTS, (du_ref, dq_ref, dk_ref, dv_ref, dgt_ref)):
            part = p_ref[...]
            o_ref[lo:hi, :] += _dot_tn(part, hh)
            db_ref[:, lo:hi] += jnp.sum(part.astype(F32), axis=0, keepdims=True)

    return _call(
        body, name="dw_in", grid=(T // tk,),
        in_specs=[_rows(tk, D_MODEL)] + [_rows(tk, hi - lo) for lo, hi in _PARTS],
        out_specs=[_const((IN_WIDTH, D_MODEL)), _const((1, IN_WIDTH))],
        out_shape=[_sds((IN_WIDTH, D_MODEL), F32), _sds((1, IN_WIDTH), F32)],
        args=(h, *parts), sem=("arbitrary",), exchanges=exchanges)


def _row_tile(rows, cap=256, mult=16):
    best = None
    for t in range(mult, min(rows, cap) + 1, mult):
        if rows % t == 0:
            best = t
    if best is None:
        raise ValueError("no row tile for %d rows" % rows)
    return best


def _pair_sum(ids, full, got):
    _, r, c = full.shape
    hr = r // 2
    tr = _row_tile(hr)
    nblk = hr // tr

    def body(ids_ref, a_ref, b_ref, s_ref, sb_ref):
        s = a_ref[...] + b_ref[...]
        s_ref[...] = s
        sb_ref[...] = s.astype(BF16)

    out_spec = pl.BlockSpec((None, tr, c), lambda j, i, ids_ref: (j, i, 0))
    return pl.pallas_call(
        body, name="pair_sum_%dx%d" % (r, c),
        grid_spec=pltpu.PrefetchScalarGridSpec(
            num_scalar_prefetch=1, grid=(N_CHIPS, nblk),
            in_specs=[pl.BlockSpec((None, tr, c), lambda j, i, ids_ref: (j, ids_ref[1] * nblk + i, 0)), out_spec],
            out_specs=[out_spec, out_spec]),
        out_shape=[_sds((N_CHIPS, hr, c), F32), _sds((N_CHIPS, hr, c), BF16)],
        compiler_params=_cp("parallel", "parallel"),
    )(ids, full, got)


def _chip_sum(ids, own, got):
    _, hr, c = own.shape
    tr = _row_tile(hr)
    nblk = hr // tr

    def body(ids_ref, a_ref, b_ref, o_ref):
        o_ref[...] = ((a_ref[...] + b_ref[0].astype(F32)) + b_ref[1].astype(F32)) + b_ref[2].astype(F32)

    return pl.pallas_call(
        body, name="chip_sum_%dx%d" % (hr, c),
        grid_spec=pltpu.PrefetchScalarGridSpec(
            num_scalar_prefetch=1, grid=(nblk,),
            in_specs=[pl.BlockSpec((None, tr, c), lambda i, ids_ref: (ids_ref[0], i, 0)),
                      pl.BlockSpec((3, tr, c), lambda i, ids_ref: (0, i, 0))],
            out_specs=pl.BlockSpec((tr, c), lambda i, ids_ref: (ids_ref[1] * nblk + i, 0))),
        out_shape=_sds((2 * hr, c), F32),
        compiler_params=_cp("parallel"),
    )(ids, own, got)


def _adamw_math(w, g, m, v):
    nm = ADAM_B1 * m + (1.0 - ADAM_B1) * g
    nv = ADAM_B2 * v + (1.0 - ADAM_B2) * jnp.square(g)
    m_hat = nm / (1.0 - ADAM_B1 ** ADAM_STEP)
    v_hat = nv / (1.0 - ADAM_B2 ** ADAM_STEP)
    return -ADAM_LR * (m_hat / (jnp.sqrt(v_hat) + ADAM_EPS) + ADAM_WD * w), nm, nv


def _adamw(w, g, m, v):
    r, c = w.shape
    tr = _row_tile(r, mult=8)

    def body(w_ref, g_ref, m_ref, v_ref, d_ref, nm_ref, nv_ref):
        d_ref[...], nm_ref[...], nv_ref[...] = _adamw_math(w_ref[...], g_ref[...], m_ref[...], v_ref[...])

    spec = _rows(tr, c)
    return pl.pallas_call(
        body, name="adamw_%dx%d" % (r, c), grid=(r // tr,),
        in_specs=[spec] * 4, out_specs=[spec] * 3, out_shape=[_sds((r, c), F32)] * 3,
        compiler_params=_cp("parallel"),
    )(w, g, m, v)


_SMALL_NAMES = ("w_pool", "b_in", "g_mix_pre", "g_mix_post", "g_mlp_pre", "g_mlp_post", "pool_scale", "attn_sinks")
B_ROWS = -(-IN_WIDTH // D_MODEL)


def _row_block(rows):
    rows = [jnp.pad(r.astype(F32), ((0, 0), (0, D_MODEL - r.shape[1]))) for r in rows]
    return jnp.pad(jnp.concatenate(rows, axis=0), ((0, 8 - len(rows)), (0, 0)))


def _early_block(dg2, dg3, dg4, dps, dsink, loss):
    tail = jnp.concatenate([jnp.pad(dsink.reshape(1, -1), ((0, 0), (0, LANES - dsink.size))),
                            jnp.pad(loss.reshape(1, 1), ((0, 0), (0, LANES - 1)))], axis=1)
    return _row_block([dg2, dg3, dg4, dps, tail])


def _late_block(db_in, dg1):
    b = jnp.pad(db_in, ((0, 0), (0, B_ROWS * D_MODEL - IN_WIDTH))).reshape(B_ROWS, D_MODEL)
    return _row_block([b[r:r + 1] for r in range(B_ROWS)] + [dg1])


def _small_update(gearly, gmat, glate, w, m, v):
    names = _SMALL_NAMES
    n = len(names)

    def total(ref, rows):
        acc = ref[0:rows, :]
        for d in range(1, N_DEV):
            acc = acc + ref[d * rows:(d + 1) * rows, :]
        return acc

    def body(*refs):
        early_ref, gmat_ref, late_ref = refs[:3]
        w_refs, m_refs, v_refs = refs[3:3 + n], refs[3 + n:3 + 2 * n], refs[3 + 2 * n:3 + 3 * n]
        outs = refs[3 + 3 * n:]
        loss_ref, g_refs, d_refs = outs[0], outs[1:1 + n], outs[1 + n:1 + 2 * n]
        nm_refs, nv_refs = outs[1 + 2 * n:1 + 3 * n], outs[1 + 3 * n:1 + 4 * n]
        early, late = total(early_ref, 8), total(late_ref, 8)
        loss_ref[...] = jnp.sum(early[4:5, LANES:2 * LANES], axis=1, keepdims=True)
        bias = jnp.concatenate([late[r:r + 1, :] for r in range(B_ROWS - 1)]
                               + [late[B_ROWS - 1:B_ROWS, :IN_WIDTH - (B_ROWS - 1) * D_MODEL]], axis=1)
        grad = dict(b_in=bias, g_mix_pre=late[B_ROWS:B_ROWS + 1, :], g_mix_post=early[0:1, :],
                    g_mlp_pre=early[1:2, :], g_mlp_post=early[2:3, :], pool_scale=early[3:4, :POOL_WIDTH],
                    attn_sinks=early[4:5, :N_Q_HEADS])
        for i, name in enumerate(names):
            g = total(gmat_ref, 4 * POOL_GC) if name == "w_pool" else grad[name]
            g_refs[i][...] = g
            d_refs[i][...], nm_refs[i][...], nv_refs[i][...] = _adamw_math(
                w_refs[i][...], g, m_refs[i][...], v_refs[i][...])

    shapes = [_sds(w[k].shape, F32) for k in names]
    res = pl.pallas_call(
        body, name="small_update", out_shape=[_sds((1, 1), F32)] + shapes * 4,
        compiler_params=pltpu.CompilerParams(vmem_limit_bytes=VMEM_MB * 1024 * 1024),
    )(gearly, gmat, glate, *[w[k] for k in names], *[m[k] for k in names], *[v[k] for k in names])
    loss = res[0]
    per = {k: tuple(res[1 + j * n + i] for j in range(4)) for i, k in enumerate(names)}
    return loss, per


_BIG = ("w_in", "w_branch_pool", "w_branch_attn", "w_out", "w_up", "w_down")
_ORDER = ("g_mix_pre", "w_in", "b_in", "w_pool", "pool_scale", "attn_sinks", "w_branch_pool", "w_branch_attn",
          "w_out", "g_mix_post", "g_mlp_pre", "w_up", "w_down", "g_mlp_post")


def _stack_rows(slab):
    return slab.reshape(-1, slab.shape[2])


def _step(x2, tgt, seq, shards, small, ids):
    tabs = _rope_tables(seq)
    g1, g2, g3, g4 = (small[n] for n in ("g_mix_pre", "g_mix_post", "g_mlp_pre", "g_mlp_post"))
    sinks = small["attn_sinks"].reshape(N_Q_HEADS)
    w_pool = small["w_pool"].reshape(4, POOL_GC, POOL_GC)
    pool_scale = small["pool_scale"]

    def whole(name, slabs):
        return lax.dynamic_update_slice(slabs, shards[name][None], (ids[0], 0, 0))

    w_in = _stack_rows(whole("w_in", _alone("gather_in", _ex_gather([shards["w_in"]]))[0][0]))
    mix_names = ("w_branch_pool", "w_branch_attn", "w_out")
    (h, u, q, k, v, gate), [mix_slabs] = _inproj(
        x2, g1, w_in, small["b_in"], tabs, seq, exchanges=[_ex_gather([shards[n] for n in mix_names])])
    w_bp, w_ba, out_slab = (whole(n, s) for n, s in zip(mix_names, mix_slabs))
    w_out = _stack_rows(out_slab)
    diff, y_pool = _pool_fwd(u, w_pool, pool_scale, seq)
    (y_attn,), [[w_up]] = _attn_fwd(q, k, v, sinks, seq, exchanges=[_ex_gather([shards["w_up"]])])
    w_up = whole("w_up", w_up)
    bp, ba, merged, mix, x1, h2 = _merge_out(y_pool, y_attn, gate, x2, w_bp, w_ba, w_out, g2, g3)
    (up, act), [[down_slab]] = _mlp_up(h2, w_up, exchanges=[_ex_gather([shards["w_down"]])])
    w_down = _stack_rows(whole("w_down", down_slab))
    dff, dy, loss_acc, dg4 = _mlp_down_loss(act, x1, tgt, w_down, g4)

    dup = _mlp_down_bwd(dff, up, w_down)
    dw_down = _dw("down", act, dff, 1024, 1024)[0].reshape(N_CHIPS, D_FF // N_CHIPS, D_MODEL)
    (dx1, dmix, dg3, dg2), [[got]] = _mlp_up_bwd(dup, dy, x1, mix, w_up, g3, g2, exchanges=[_ex_pair([dw_down])])
    ps_down = _pair_sum(ids, dw_down, got)
    (dw_up,), [[got]] = _dw("up", h2, dup, 1024, 1024, shard_cols=True, exchanges=[_ex_chip([ps_down[1]])])
    half_down = _chip_sum(ids, ps_down[0], got)
    (dbp, dba, dgate, dyp, dya), [[got], [g_down]] = _merge_bwd(
        dmix, gate, bp, ba, w_out, w_bp, w_ba, exchanges=[_ex_pair([dw_up]), _ex_swap([half_down])])
    ps_up = _pair_sum(ids, dw_up, got)
    dw_mix = [_dw("out", merged, dmix, 1024, 1024)[0].reshape(N_CHIPS, D_MODEL // N_CHIPS, D_MODEL),
              _dw_slabs("branch_pool", y_pool, dbp), _dw_slabs("branch_attn", y_attn, dba)]
    (dq, dk, dv, dsink), [[got], gots] = _attn_bwd(
        q, k, v, dya, sinks, tabs, seq, exchanges=[_ex_chip([ps_up[1]]), _ex_pair(dw_mix)])
    half_up = _chip_sum(ids, ps_up[0], got)
    ps_mix = [_pair_sum(ids, d, g) for d, g in zip(dw_mix, gots)]
    (du, dw_pool, dps), [[g_up]] = _pool_bwd(dyp, diff, w_pool, pool_scale, seq, exchanges=[_ex_swap([half_up])])
    parts = (du, dq, dk, dv, dgate)
    early = _early_block(dg2, dg3, dg4, dps, dsink[:, 0], loss_acc[0, 0])
    mat = dw_pool.reshape(4 * POOL_GC, POOL_GC)
    (dw_in_t, db_in), [gots, [gearly, gmat]] = _dw_in(
        h, parts, exchanges=[_ex_chip([p[1] for p in ps_mix]), _ex_allgather([early, mat])])
    half_mix = [_chip_sum(ids, p[0], g) for p, g in zip(ps_mix, gots)]
    dw_in = dw_in_t.reshape(N_CHIPS, IN_WIDTH // N_CHIPS, D_MODEL)
    g_mix, [got] = _alone("swap_mix_pair_in", _ex_swap(half_mix), _ex_pair([dw_in]))
    ps_in = _pair_sum(ids, dw_in, got)
    (gx, dg1), [[got]] = _inproj_bwd(parts, x2, dx1, w_in, g1, exchanges=[_ex_chip([ps_in[1]])])
    [g_in], [glate] = _alone("swap_in_allgather", _ex_swap([_chip_sum(ids, ps_in[0], got)]),
                             _ex_allgather([_late_block(db_in, dg1)]))

    grads = dict(w_in=g_in, w_branch_pool=g_mix[1], w_branch_attn=g_mix[2], w_out=g_mix[0], w_up=g_up, w_down=g_down)
    return (gearly, gmat, glate), gx, grads


def kernel(x, g_mix_pre, w_in, b_in, w_pool, pool_scale, attn_sinks, w_branch_pool, w_branch_attn, w_out, g_mix_post, g_mlp_pre, w_up, w_down, g_mlp_post, loss_target, m_g_mix_pre, m_w_in, m_b_in, m_w_pool, m_pool_scale, m_attn_sinks, m_w_branch_pool, m_w_branch_attn, m_w_out, m_g_mix_post, m_g_mlp_pre, m_w_up, m_w_down, m_g_mlp_post, v_g_mix_pre, v_w_in, v_b_in, v_w_pool, v_pool_scale, v_attn_sinks, v_w_branch_pool, v_w_branch_attn, v_w_out, v_g_mix_post, v_g_mlp_pre, v_w_up, v_w_down, v_g_mlp_post):
    weights = dict(g_mix_pre=g_mix_pre, w_in=w_in, b_in=b_in, w_pool=w_pool, pool_scale=pool_scale,
                   attn_sinks=attn_sinks, w_branch_pool=w_branch_pool, w_branch_attn=w_branch_attn, w_out=w_out,
                   g_mix_post=g_mix_post, g_mlp_pre=g_mlp_pre, w_up=w_up, w_down=w_down, g_mlp_post=g_mlp_post)
    mom1 = dict(g_mix_pre=m_g_mix_pre, w_in=m_w_in, b_in=m_b_in, w_pool=m_w_pool, pool_scale=m_pool_scale,
                attn_sinks=m_attn_sinks, w_branch_pool=m_w_branch_pool, w_branch_attn=m_w_branch_attn,
                w_out=m_w_out, g_mix_post=m_g_mix_post, g_mlp_pre=m_g_mlp_pre, w_up=m_w_up, w_down=m_w_down,
                g_mlp_post=m_g_mlp_post)
    mom2 = dict(g_mix_pre=v_g_mix_pre, w_in=v_w_in, b_in=v_b_in, w_pool=v_w_pool, pool_scale=v_pool_scale,
                attn_sinks=v_attn_sinks, w_branch_pool=v_w_branch_pool, w_branch_attn=v_w_branch_attn,
                w_out=v_w_out, g_mix_post=v_g_mix_post, g_mlp_pre=v_g_mlp_pre, w_up=v_w_up, w_down=v_w_down,
                g_mlp_post=v_g_mlp_post)
    b_loc, seq, _ = x.shape
    x2 = x.reshape(b_loc * seq, D_MODEL)
    tgt = loss_target.reshape(b_loc * seq, D_MODEL)
    ids = jnp.stack([2 * lax.axis_index("x") + lax.axis_index("y"), lax.axis_index("c")]).astype(jnp.int32)

    def flat(n, a):
        return a[0].T if n == "w_in" else a[0]

    def unflat(n, a):
        return (a.T if n == "w_in" else a)[None]

    shards = {n: flat(n, weights[n]).astype(BF16) for n in _BIG}
    small = {n: weights[n] for n in _ORDER if n not in _BIG}
    (gearly, gmat, glate), gx, grads = _step(x2, tgt, seq, shards, small, ids)

    def two_d(src):
        return {n: src[n].reshape(4 * POOL_GC, POOL_GC) if n == "w_pool" else src[n] for n in _SMALL_NAMES}

    loss, per = _small_update(gearly, gmat, glate, two_d(weights), two_d(mom1), two_d(mom2))
    delta, new_m, new_v = {}, {}, {}
    for n in _SMALL_NAMES:
        grads[n], delta[n], new_m[n], new_v[n] = (a.reshape(weights[n].shape) for a in per[n])
    for n in _BIG:
        d, nm, nv = _adamw(flat(n, weights[n]), grads[n], flat(n, mom1[n]), flat(n, mom2[n]))
        grads[n] = unflat(n, grads[n])
        delta[n], new_m[n], new_v[n] = unflat(n, d), unflat(n, nm), unflat(n, nv)

    return (loss[0, 0], gx.reshape(x.shape), *[grads[n] for n in _ORDER], *[delta[n] for n in _ORDER],
            *[new_m[n] for n in _ORDER], *[new_v[n] for n in _ORDER])
```

```python
import jax
import jax.numpy as jnp
from jax import lax
from jax.experimental import pallas as pl
from jax.experimental.pallas import tpu as pltpu

F32 = jnp.float32
BF16 = jnp.bfloat16

D_MODEL = 1024
POOL_WINDOWS = (2, 4, 8, 16)
POOL_WIDTH = 512
POOL_GC = 128
HALO = 16
HEAD_DIM = 64
N_Q_HEADS = 8
ATTN_WIDTH = 512
KV_WIDTH = 128
BLOCK = 128
NEG_INF = -1e30
ROPE_THETA = 500000.0
ROT_DIM = 16
GATE_WIDTH = 2048
IN_WIDTH = 3328
D_FF = 4096
EPS = 1e-6
SCALE = HEAD_DIM ** -0.5
C_Q, C_K, C_V, C_G = 512, 1024, 1152, 1280

ADAM_LR, ADAM_B1, ADAM_B2, ADAM_EPS, ADAM_WD, ADAM_STEP = 0.001, 0.9, 0.999, 1e-08, 0.01, 10

N_CHIPS = 4
N_DEV = 8
LANES = 128
TM = 512
TP = 512
VMEM_MB = 56

MESH = pl.DeviceIdType.MESH
ANY = pl.BlockSpec(memory_space=pl.ANY)


def _cp(*sem, vmem=VMEM_MB):
    return pltpu.CompilerParams(dimension_semantics=sem, vmem_limit_bytes=vmem * 1024 * 1024)


def _rows(tile, cols):
    return pl.BlockSpec((tile, cols), lambda i: (i, 0))


def _const(shape):
    nd = len(shape)
    return pl.BlockSpec(shape, lambda i: (0,) * nd)


def _sds(shape, dtype):
    return jax.ShapeDtypeStruct(shape, dtype)


def _dot(a, b):
    return jnp.dot(a, b, preferred_element_type=F32)


def _dot_nt(a, b):
    return lax.dot_general(a, b, (((1,), (1,)), ((), ())), preferred_element_type=F32)


def _dot_tn(a, b):
    return lax.dot_general(a, b, (((0,), (0,)), ((), ())), preferred_element_type=F32)


def _rms(x):
    return lax.rsqrt(jnp.mean(x * x, axis=-1, keepdims=True) + EPS)


def _norm_bwd(x, g, dout):
    r = _rms(x)
    n = x * r
    dn = dout * g
    dx = r * (dn - n * jnp.mean(dn * n, axis=-1, keepdims=True))
    return dx, jnp.sum(dout * n, axis=0, keepdims=True)


def _rot_fwd(t, c, a, bt):
    return t * c + pltpu.roll(t, LANES - 8, 1) * a + pltpu.roll(t, 8, 1) * bt


def _rot_bwd(d, c, a, bt):
    return d * c + pltpu.roll(d * a, 8, 1) + pltpu.roll(d * bt, LANES - 8, 1)


def _rope_tables(seq):
    pos = jnp.arange(seq, dtype=F32)
    inv_freq = ROPE_THETA ** (-jnp.arange(0, ROT_DIM, 2, dtype=F32) / ROT_DIM)
    ang = pos[:, None] * inv_freq[None, :]
    cos, sin = jnp.cos(ang), jnp.sin(ang)
    ones = jnp.ones((seq, HEAD_DIM - ROT_DIM), F32)
    zeros8 = jnp.zeros((seq, 8), F32)
    zrest = jnp.zeros((seq, HEAD_DIM - ROT_DIM), F32)
    c = jnp.concatenate([cos, cos, ones], axis=1)
    a = jnp.concatenate([-sin, zeros8, zrest], axis=1)
    bt = jnp.concatenate([zeros8, sin, zrest], axis=1)
    return tuple(jnp.tile(t, (1, 2)) for t in (c, a, bt))


class _Exchange:
    def __init__(self, inputs, out_shapes, sems, start, finish, aliases=None, middle=None):
        self.inputs, self.out_shapes, self.sems = list(inputs), list(out_shapes), list(sems)
        self.start, self.finish, self.aliases = start, finish, dict(aliases or {})
        self.middle = middle


def _call(body, *, name, grid, in_specs, out_specs, out_shape, args, scratch=(), sem=(), exchanges=()):
    in_specs, out_specs, out_shape, scratch = list(in_specs), list(out_specs), list(out_shape), list(scratch)
    if not exchanges:
        return pl.pallas_call(body, name=name, grid=grid, in_specs=in_specs, out_specs=out_specs,
                              out_shape=out_shape, scratch_shapes=scratch, compiler_params=_cp(*sem))(*args)
    n_in, n_out, n_scr = len(in_specs), len(out_specs), len(scratch)
    x_in = [a for ex in exchanges for a in ex.inputs]
    x_out = [s for ex in exchanges for s in ex.out_shapes]
    x_sem = [s for ex in exchanges for s in ex.sems]
    aliases, i_off, o_off = {}, n_in, n_out
    for ex in exchanges:
        for i, o in ex.aliases.items():
            aliases[i_off + i] = o_off + o
        i_off += len(ex.inputs)
        o_off += len(ex.out_shapes)

    def split(flat):
        out, pos = [], 0
        for ex, n in zip(exchanges, flat[1]):
            out.append(flat[0][pos:pos + n])
            pos += n
        return out

    def carrier(*refs):
        pos = 0
        groups = []
        for n in (n_in, len(x_in), n_out, len(x_out), n_scr, len(x_sem)):
            groups.append(refs[pos:pos + n])
            pos += n
        ins, xin, outs, xout, scr, xsem = groups
        xin = split((xin, [len(ex.inputs) for ex in exchanges]))
        xout = split((xout, [len(ex.out_shapes) for ex in exchanges]))
        xsem = split((xsem, [len(ex.sems) for ex in exchanges]))
        first = pl.program_id(0) == 0
        last = pl.program_id(0) == grid[0] - 1
        for d in range(1, len(grid)):
            first = jnp.logical_and(first, pl.program_id(d) == 0)
            last = jnp.logical_and(last, pl.program_id(d) == grid[d] - 1)

        @pl.when(first)
        def _():
            for ex, i, o, s in zip(exchanges, xin, xout, xsem):
                ex.start(i, o, s)

        if any(ex.middle for ex in exchanges):
            half = pl.program_id(0) == grid[0] // 2
            for d in range(1, len(grid)):
                half = jnp.logical_and(half, pl.program_id(d) == 0)

            @pl.when(half)
            def _():
                for ex, i, o, s in zip(exchanges, xin, xout, xsem):
                    if ex.middle:
                        ex.middle(i, o, s)

        body(*ins, *outs, *scr)

        @pl.when(last)
        def _():
            for ex, i, o, s in zip(exchanges, xin, xout, xsem):
                ex.finish(i, o, s)

    res = pl.pallas_call(
        carrier, name=name, grid=grid, in_specs=in_specs + [ANY] * len(x_in),
        out_specs=out_specs + [ANY] * len(x_out), out_shape=out_shape + x_out,
        scratch_shapes=scratch + x_sem, input_output_aliases=aliases,
        compiler_params=_cp(*(["arbitrary"] * len(grid))),
    )(*args, *x_in)
    return res[:n_out], split((res[n_out:], [len(ex.out_shapes) for ex in exchanges]))


def _alone(name, *exchanges):
    n_in = [len(ex.inputs) for ex in exchanges]
    n_out = [len(ex.out_shapes) for ex in exchanges]
    n_sem = [len(ex.sems) for ex in exchanges]
    aliases, i_off, o_off = {}, 0, 0
    for ex in exchanges:
        for i, o in ex.aliases.items():
            aliases[i_off + i] = o_off + o
        i_off += len(ex.inputs)
        o_off += len(ex.out_shapes)

    def split(flat, counts):
        out, pos = [], 0
        for n in counts:
            out.append(flat[pos:pos + n])
            pos += n
        return out

    def body(*refs):
        ins, outs, sems = split(refs, [sum(n_in), sum(n_out), sum(n_sem)])
        groups = list(zip(exchanges, split(ins, n_in), split(outs, n_out), split(sems, n_sem)))
        for ex, i, o, s in groups:
            ex.start(i, o, s)
        for ex, i, o, s in groups:
            if ex.middle:
                ex.middle(i, o, s)
        for ex, i, o, s in groups:
            ex.finish(i, o, s)

    res = pl.pallas_call(
        body, name=name, in_specs=[ANY] * sum(n_in), out_specs=[ANY] * sum(n_out),
        out_shape=[s for ex in exchanges for s in ex.out_shapes],
        scratch_shapes=[s for ex in exchanges for s in ex.sems], input_output_aliases=aliases,
    )(*[a for ex in exchanges for a in ex.inputs])
    return split(res, n_out)


def _place():
    x, y, c = lax.axis_index("x"), lax.axis_index("y"), lax.axis_index("c")
    chips = [(1 - x, y), (x, 1 - y), (1 - x, 1 - y)]
    return x, y, c, chips


def _remote(src, dst, send, recv, to):
    return pltpu.make_async_remote_copy(src_ref=src, dst_ref=dst, send_sem=send, recv_sem=recv,
                                        device_id=to, device_id_type=MESH)


def _ex_gather(shards):
    nw = len(shards)
    hrs = [s.shape[0] // 2 for s in shards]

    def copies(ins, outs, sems):
        s1, r1, s2, r2, fs, fr = sems
        x, y, c, _ = _place()
        me, xn, yn, dg = (x, y), (1 - x, y), (x, 1 - y), (1 - x, 1 - y)
        nbr = (xn, yn)
        sibling = (x, y, 1 - c)

        def piece(w, chip, core, part=None):
            hr = hrs[w]
            rows = pl.ds(core * hr, hr) if part is None else pl.ds(core * hr + part * (hr // 2), hr // 2)
            return outs[w].at[2 * chip[0] + chip[1], rows]

        def first(w, k):
            return _remote(ins[w].at[pl.ds(c * hrs[w], hrs[w])], piece(w, me, c), s1.at[w, k], r1.at[w, k],
                           (*nbr[k], c))

        def landed(w, k):
            return _remote(piece(w, nbr[k], c), piece(w, nbr[k], c), s1.at[w, k], r1.at[w, k], (*nbr[k], c))

        def onward(w, k):
            return _remote(piece(w, nbr[k], c, k), piece(w, nbr[k], c, k), s2.at[w, k], r2.at[w, k],
                           (*nbr[1 - k], c))

        def arrived(w, k):
            return _remote(piece(w, dg, c, k), piece(w, dg, c, k), s2.at[w, k], r2.at[w, k], (*nbr[1 - k], c))

        def passed(w, j):
            chip = (xn, yn, dg)[j]
            return _remote(piece(w, chip, c), piece(w, chip, c), fs.at[w, j], fr.at[w, j], sibling)

        def handed(w, j):
            chip = (xn, yn, dg)[j]
            return _remote(piece(w, chip, 1 - c), piece(w, chip, 1 - c), fs.at[w, j], fr.at[w, j], sibling)

        return first, landed, onward, arrived, passed, handed

    def start(ins, outs, sems):
        first = copies(ins, outs, sems)[0]
        for w in range(nw):
            for k in range(2):
                first(w, k).start()

    def middle(ins, outs, sems):
        _, landed, onward, _, passed, _ = copies(ins, outs, sems)
        for w in range(nw):
            for k in range(2):
                landed(w, k).wait_recv()
                onward(w, k).start()
                passed(w, k).start()

    def finish(ins, outs, sems):
        first, _, onward, arrived, passed, handed = copies(ins, outs, sems)
        for w in range(nw):
            for k in range(2):
                arrived(w, k).wait_recv()
            passed(w, 2).start()
        for w in range(nw):
            for j in range(3):
                handed(w, j).wait_recv()
        for w in range(nw):
            for k in range(2):
                first(w, k).wait_send()
                onward(w, k).wait_send()
            for j in range(3):
                passed(w, j).wait_send()

    return _Exchange(shards, [_sds((N_CHIPS,) + s.shape, s.dtype) for s in shards],
                     [pltpu.SemaphoreType.DMA((nw, 2))] * 4 + [pltpu.SemaphoreType.DMA((nw, 3))] * 2,
                     start, finish, middle=middle)


def _ex_pair(grads):
    nw = len(grads)

    def copies(ins, outs, sems):
        x, y, c, _ = _place()
        out = []
        for w in range(nw):
            hr = grads[w].shape[1] // 2
            out.append(_remote(ins[w].at[:, pl.ds((1 - c) * hr, hr)], outs[w], sems[0].at[w], sems[1].at[w],
                               (x, y, 1 - c)))
        return out

    def start(ins, outs, sems):
        for cp in copies(ins, outs, sems):
            cp.start()

    def finish(ins, outs, sems):
        for cp in copies(ins, outs, sems):
            cp.wait()

    return _Exchange(grads, [_sds((N_CHIPS, g.shape[1] // 2, g.shape[2]), F32) for g in grads],
                     [pltpu.SemaphoreType.DMA((nw,))] * 2, start, finish)


def _ex_chip(pieces):
    nw = len(pieces)

    def copies(ins, outs, sems):
        x, y, c, chips = _place()
        return [_remote(ins[w].at[2 * cx + cy], outs[w].at[k], sems[0].at[w, k], sems[1].at[w, k], (cx, cy, c))
                for w in range(nw) for k, (cx, cy) in enumerate(chips)]

    def start(ins, outs, sems):
        for cp in copies(ins, outs, sems):
            cp.start()

    def finish(ins, outs, sems):
        for cp in copies(ins, outs, sems):
            cp.wait()

    return _Exchange(pieces, [_sds((3,) + p.shape[1:], BF16) for p in pieces],
                     [pltpu.SemaphoreType.DMA((nw, 3))] * 2, start, finish)


def _ex_swap(fulls):
    nw = len(fulls)

    def start(ins, outs, sems):
        x, y, c, _ = _place()
        for w in range(nw):
            hr = fulls[w].shape[0] // 2
            mine = pl.ds(c * hr, hr)
            _remote(ins[w].at[mine], outs[w].at[mine], sems[0].at[w], sems[1].at[w], (x, y, 1 - c)).start()

    def finish(ins, outs, sems):
        x, y, c, _ = _place()
        for w in range(nw):
            hr = fulls[w].shape[0] // 2
            mine, theirs = pl.ds(c * hr, hr), pl.ds((1 - c) * hr, hr)
            _remote(ins[w].at[mine], outs[w].at[mine], sems[0].at[w], sems[1].at[w], (x, y, 1 - c)).wait_send()
            _remote(ins[w].at[theirs], outs[w].at[theirs], sems[0].at[w], sems[1].at[w], (x, y, 1 - c)).wait_recv()

    return _Exchange(fulls, [_sds(f.shape, F32) for f in fulls], [pltpu.SemaphoreType.DMA((nw,))] * 2,
                     start, finish, aliases={w: w for w in range(nw)})


def _ex_allgather(blocks):
    nb = len(blocks)

    def copies(ins, outs, sems):
        send, recv, lsem = sems
        x, y, c, chips = _place()
        me, sibling = (x, y, c), (x, y, 1 - c)

        def rows(b, px, py, pc):
            m_per = blocks[b].shape[0]
            return outs[b].at[pl.ds((4 * px + 2 * py + pc) * m_per, m_per), :]

        def copy(b, k, blk, to, src=None):
            return _remote(rows(b, *blk) if src is None else src, rows(b, *blk), send.at[b, k], recv.at[b, k], to)

        def mine(b):
            return pltpu.make_async_copy(ins[b], rows(b, *me), lsem.at[b])

        def first(b, k):
            return copy(b, k, me, sibling if k == 0 else (*chips[k - 1], c), src=ins[b])

        def passed(b, j):
            return copy(b, 4 + j, (*chips[j], c), sibling)

        def landed(b, j):
            return copy(b, 1 + j, (*chips[j], c), me)

        def handed(b, k):
            return copy(b, 0, sibling, me) if k == 0 else copy(b, 3 + k, (*chips[k - 1], 1 - c), me)

        return mine, first, passed, landed, handed

    def start(ins, outs, sems):
        mine, first, _, _, _ = copies(ins, outs, sems)
        for b in range(nb):
            mine(b).start()
            for k in range(4):
                first(b, k).start()

    def finish(ins, outs, sems):
        mine, first, passed, landed, handed = copies(ins, outs, sems)
        sent = []
        for b in range(nb):
            for j in range(3):
                landed(b, j).wait_recv()
                cp = passed(b, j)
                cp.start()
                sent.append(cp)
        for b in range(nb):
            for k in range(4):
                handed(b, k).wait_recv()
            for k in range(4):
                first(b, k).wait_send()
        for cp in sent:
            cp.wait_send()
        for b in range(nb):
            mine(b).wait()

    return _Exchange(blocks, [_sds((N_DEV * b.shape[0], b.shape[1]), F32) for b in blocks],
                     [pltpu.SemaphoreType.DMA((nb, 7)), pltpu.SemaphoreType.DMA((nb, 7)), pltpu.SemaphoreType.DMA((nb,))],
                     start, finish)


def _inproj(x2, g1, w_in_t, b_in, tabs, seq, exchanges=()):
    T = x2.shape[0]
    tm = min(TM, seq)
    nseq = seq // tm

    def body(x_ref, g_ref, w_ref, b_ref, c_ref, a_ref, bt_ref, h_ref, u_ref, q_ref, k_ref, v_ref, gate_ref):
        x = x_ref[...]
        h = (x * _rms(x) * g_ref[...]).astype(BF16)
        h_ref[...] = h

        def proj(lo, hi):
            return _dot_nt(h, w_ref[lo:hi, :]) + b_ref[:, lo:hi]

        c, a, bt = c_ref[...], a_ref[...], bt_ref[...]
        u_ref[...] = proj(0, C_Q)
        q = proj(C_Q, C_K)
        for p in range(4):
            sl = slice(LANES * p, LANES * (p + 1))
            q_ref[:, sl] = (_rot_fwd(q[:, sl], c, a, bt) * SCALE).astype(BF16)
        kv = proj(C_K, C_G)
        k_ref[...] = _rot_fwd(kv[:, :KV_WIDTH], c, a, bt).astype(BF16)
        v_ref[...] = kv[:, KV_WIDTH:].astype(BF16)
        for j in range(2):
            lo = C_G + D_MODEL * j
            gate_ref[:, D_MODEL * j:D_MODEL * (j + 1)] = jax.nn.sigmoid(proj(lo, lo + D_MODEL)).astype(BF16)

    tab = pl.BlockSpec((tm, LANES), lambda i: (i % nseq, 0))
    return _call(
        body, name="inproj", grid=(T // tm,),
        in_specs=[_rows(tm, D_MODEL), _const((1, D_MODEL)), _const((IN_WIDTH, D_MODEL)), _const((1, IN_WIDTH)),
                  tab, tab, tab],
        out_specs=[_rows(tm, D_MODEL), _rows(tm, POOL_WIDTH), _rows(tm, ATTN_WIDTH), _rows(tm, KV_WIDTH),
                   _rows(tm, KV_WIDTH), _rows(tm, GATE_WIDTH)],
        out_shape=[_sds((T, D_MODEL), BF16), _sds((T, POOL_WIDTH), F32), _sds((T, ATTN_WIDTH), BF16),
                   _sds((T, KV_WIDTH), BF16), _sds((T, KV_WIDTH), BF16), _sds((T, GATE_WIDTH), BF16)],
        args=(x2, g1, w_in_t, b_in, *tabs), sem=("parallel",), exchanges=exchanges)


def _inv_count(pos, w):
    return 1.0 / jnp.minimum(pos + 1, w).astype(F32)


def _pool_fwd(u, w_pool, pool_scale, seq):
    T = u.shape[0]
    tp = min(TP, seq)
    nseq = seq // tp
    per = tp // HALO

    def body(u_ref, prev_ref, w_ref, s_ref, diff_ref, y_ref):
        i = pl.program_id(0)
        first = (i % nseq) == 0
        prev = jnp.where(first, 0.0, prev_ref[...])
        ext = jnp.concatenate([prev, u_ref[...]], axis=0)
        pos = (i % nseq) * tp + lax.broadcasted_iota(jnp.int32, (tp, 1), 0)
        for gi, w in enumerate(POOL_WINDOWS):
            sl = slice(POOL_GC * gi, POOL_GC * (gi + 1))
            xg = ext[:, sl]
            s = xg
            sh = 1
            while sh < w:
                s = s + pltpu.roll(s, sh, 0)
                sh *= 2
            pooled = s[HALO:] * _inv_count(pos, w)
            diff = (pooled - xg[HALO:]).astype(BF16)
            diff_ref[:, sl] = diff
            mixed = _dot(diff, w_ref[gi].astype(BF16))
            y_ref[:, sl] = (mixed * s_ref[:, sl]).astype(BF16)

    return _call(
        body, name="pool_fwd", grid=(T // tp,),
        in_specs=[_rows(tp, POOL_WIDTH),
                  pl.BlockSpec((HALO, POOL_WIDTH), lambda i: (jnp.maximum(i * per - 1, 0), 0)),
                  _const((4, POOL_GC, POOL_GC)), _const((1, POOL_WIDTH))],
        out_specs=[_rows(tp, POOL_WIDTH), _rows(tp, POOL_WIDTH)],
        out_shape=[_sds((T, POOL_WIDTH), BF16), _sds((T, POOL_WIDTH), BF16)],
        args=(u, u, w_pool, pool_scale), sem=("parallel",))


GROUP = 4
GROWS = GROUP * BLOCK


def _attn_masks(n):
    qi = lax.broadcasted_iota(jnp.int32, (GROWS, 2 * BLOCK), 0) % BLOCK
    kj = lax.broadcasted_iota(jnp.int32, (GROWS, 2 * BLOCK), 1)
    rel = qi + BLOCK - kj
    valid = (rel >= 0) & (rel < BLOCK) & (kj >= jnp.where(n > 0, 0, BLOCK))
    lo = lax.broadcasted_iota(jnp.int32, (BLOCK, LANES), 1) < HEAD_DIM
    return valid, lo


def _stack_heads(ref, h, lo):
    keep = lo if h == 0 else jnp.logical_not(lo)
    pieces = []
    for p in (2 * h, 2 * h + 1):
        xp = ref[:, LANES * p:LANES * (p + 1)].astype(F32)
        for e in range(2):
            t = xp if e == h else pltpu.roll(xp, HEAD_DIM, 1)
            pieces.append(jnp.where(keep, t, 0.0).astype(BF16))
    return jnp.concatenate(pieces, axis=0)


def _unstack_heads(stacked, h, lo):
    pairs = []
    for j in range(2):
        parts = []
        for e in range(2):
            t = stacked[BLOCK * (2 * j + e):BLOCK * (2 * j + e + 1)]
            parts.append(t if e == h else pltpu.roll(t, HEAD_DIM, 1))
        pairs.append(jnp.where(lo, parts[0], parts[1]))
    return pairs


def _sink_rows(sink_ref, h):
    head = lax.broadcasted_iota(jnp.int32, (GROWS, 1), 0) // BLOCK
    col = jnp.zeros((GROWS, 1), F32) + sink_ref[GROUP * h]
    for g in range(1, GROUP):
        col = jnp.where(head == g, sink_ref[GROUP * h + g], col)
    return col


def _group_probs(qs, kk, valid, sink):
    s = jnp.where(valid, _dot_nt(qs, kk), NEG_INF)
    m = jnp.maximum(jnp.max(s, axis=1, keepdims=True), sink)
    ex = jnp.exp(s - m)
    es = jnp.exp(sink - m)
    inv = 1.0 / (jnp.sum(ex, axis=1, keepdims=True) + es)
    return ex * inv, es * inv


def _attn_fwd(q, k, v, sinks, seq, exchanges=()):
    T = q.shape[0]
    nb = seq // BLOCK

    def body(sink_ref, q_ref, kp_ref, kc_ref, vp_ref, vc_ref, o_ref):
        n = pl.program_id(0) % nb
        kk = jnp.concatenate([kp_ref[...], kc_ref[...]], axis=0)
        vv = jnp.concatenate([vp_ref[...], vc_ref[...]], axis=0)
        valid, lo = _attn_masks(n)
        for h in range(2):
            qs = _stack_heads(q_ref, h, lo)
            pr, _ = _group_probs(qs, kk, valid, _sink_rows(sink_ref, h))
            o = _dot(pr.astype(BF16), vv)
            for j, pair in enumerate(_unstack_heads(o, h, lo)):
                p = 2 * h + j
                o_ref[:, LANES * p:LANES * (p + 1)] = pair.astype(BF16)

    cur = lambda i: (i, 0)
    prv = lambda i: (jnp.where(i % nb == 0, i, i - 1), 0)
    return _call(
        body, name="attn_fwd", grid=(T // BLOCK,),
        in_specs=[pl.BlockSpec(memory_space=pltpu.SMEM),
                  pl.BlockSpec((BLOCK, ATTN_WIDTH), cur),
                  pl.BlockSpec((BLOCK, KV_WIDTH), prv), pl.BlockSpec((BLOCK, KV_WIDTH), cur),
                  pl.BlockSpec((BLOCK, KV_WIDTH), prv), pl.BlockSpec((BLOCK, KV_WIDTH), cur)],
        out_specs=[pl.BlockSpec((BLOCK, ATTN_WIDTH), cur)],
        out_shape=[_sds((T, ATTN_WIDTH), BF16)],
        args=(sinks, q, k, k, v, v), sem=("parallel",), exchanges=exchanges)


def _merge_out(y_pool, y_attn, gate, x2, w_bp, w_ba, w_out, g2, g3, exchanges=()):
    T = x2.shape[0]
    tm = min(TM, T)

    def body(yp_ref, ya_ref, gate_ref, x_ref, wbp_ref, wba_ref, wo_ref, g2_ref, g3_ref,
             bp_ref, ba_ref, mg_ref, mix_ref, x1_ref, h2_ref):
        yp, ya = yp_ref[...], ya_ref[...]
        bp = jnp.concatenate([_dot(yp, wbp_ref[j]) for j in range(N_CHIPS)], axis=1)
        ba = jnp.concatenate([_dot(ya, wba_ref[j]) for j in range(N_CHIPS)], axis=1)
        bp_ref[...] = bp.astype(BF16)
        ba_ref[...] = ba.astype(BF16)
        merged = (gate_ref[:, :D_MODEL].astype(F32) * bp + gate_ref[:, D_MODEL:].astype(F32) * ba).astype(BF16)
        mg_ref[...] = merged
        mix = _dot(merged, wo_ref[...])
        mix_ref[...] = mix
        x1 = x_ref[...] + mix * _rms(mix) * g2_ref[...]
        x1_ref[...] = x1
        h2_ref[...] = (x1 * _rms(x1) * g3_ref[...]).astype(BF16)

    return _call(
        body, name="merge_out", grid=(T // tm,),
        in_specs=[_rows(tm, POOL_WIDTH), _rows(tm, ATTN_WIDTH), _rows(tm, GATE_WIDTH), _rows(tm, D_MODEL),
                  _const(w_bp.shape), _const(w_ba.shape), _const((D_MODEL, D_MODEL)),
                  _const((1, D_MODEL)), _const((1, D_MODEL))],
        out_specs=[_rows(tm, D_MODEL)] * 6,
        out_shape=[_sds((T, D_MODEL), BF16), _sds((T, D_MODEL), BF16), _sds((T, D_MODEL), BF16),
                   _sds((T, D_MODEL), F32), _sds((T, D_MODEL), F32), _sds((T, D_MODEL), BF16)],
        args=(y_pool, y_attn, gate, x2, w_bp, w_ba, w_out, g2, g3), sem=("parallel",), exchanges=exchanges)


HALF = D_MODEL // 2
_HALVES = _const((N_CHIPS, HALF, D_MODEL))


def _mlp_up(h2, w_up, exchanges=()):
    T = h2.shape[0]
    tm = min(TM, T)

    def body(h_ref, wa_ref, wb_ref, up_ref, a_ref):
        ha, hb = h_ref[:, :HALF], h_ref[:, HALF:]
        for j in range(N_CHIPS):
            sl = slice(D_MODEL * j, D_MODEL * (j + 1))
            up = _dot(ha, wa_ref[j]) + _dot(hb, wb_ref[j])
            up_ref[:, sl] = up.astype(BF16)
            a_ref[:, sl] = jnp.square(jnp.maximum(up, 0.0)).astype(BF16)

    return _call(
        body, name="mlp_up", grid=(T // tm,),
        in_specs=[_rows(tm, D_MODEL), _HALVES, _HALVES],
        out_specs=[_rows(tm, D_FF), _rows(tm, D_FF)],
        out_shape=[_sds((T, D_FF), BF16), _sds((T, D_FF), BF16)],
        args=(h2, *w_up), sem=("parallel",), exchanges=exchanges)


def _mlp_down_loss(a, x1, tgt, w_down, g4):
    T = a.shape[0]
    tm = min(TM, T)

    def body(a_ref, x1_ref, t_ref, wa_ref, wb_ref, g_ref, dff_ref, dy_ref, loss_ref, dg_ref):
        @pl.when(pl.program_id(0) == 0)
        def _():
            loss_ref[...] = jnp.zeros_like(loss_ref)
            dg_ref[...] = jnp.zeros_like(dg_ref)

        ff = None
        for j in range(N_CHIPS):
            lo = D_MODEL * j
            t = _dot(a_ref[:, lo:lo + HALF], wa_ref[j]) + _dot(a_ref[:, lo + HALF:lo + D_MODEL], wb_ref[j])
            ff = t if ff is None else ff + t
        g = g_ref[...]
        err = x1_ref[...] + ff * _rms(ff) * g - t_ref[...]
        loss_ref[...] += jnp.sum(err * err) * (0.5 / D_MODEL)
        dy = err * (1.0 / D_MODEL)
        dy_ref[...] = dy
        dff, dg = _norm_bwd(ff, g, dy)
        dff_ref[...] = dff.astype(BF16)
        dg_ref[...] += dg

    return _call(
        body, name="mlp_down_loss", grid=(T // tm,),
        in_specs=[_rows(tm, D_FF), _rows(tm, D_MODEL), _rows(tm, D_MODEL), _HALVES, _HALVES, _const((1, D_MODEL))],
        out_specs=[_rows(tm, D_MODEL), _rows(tm, D_MODEL), _const((8, LANES)), _const((1, D_MODEL))],
        out_shape=[_sds((T, D_MODEL), BF16), _sds((T, D_MODEL), F32), _sds((8, LANES), F32),
                   _sds((1, D_MODEL), F32)],
        args=(a, x1, tgt, *w_down, g4), sem=("arbitrary",))


def _mlp_down_bwd(dff, up, w_down):
    T = dff.shape[0]
    tm = min(TM, T)

    def body(d_ref, up_ref, wa_ref, wb_ref, dup_ref):
        d = d_ref[...]
        for j in range(N_CHIPS):
            for part, w_ref in enumerate((wa_ref, wb_ref)):
                lo = D_MODEL * j + HALF * part
                da = _dot_nt(d, w_ref[j])
                relu = jnp.maximum(up_ref[:, lo:lo + HALF].astype(F32), 0.0)
                dup_ref[:, lo:lo + HALF] = (da * (2.0 * relu)).astype(BF16)

    return _call(
        body, name="mlp_down_bwd", grid=(T // tm,),
        in_specs=[_rows(tm, D_MODEL), _rows(tm, D_FF), _HALVES, _HALVES],
        out_specs=[_rows(tm, D_FF)],
        out_shape=[_sds((T, D_FF), BF16)],
        args=(dff, up, *w_down), sem=("parallel",))[0]


def _mlp_up_bwd(dup, dy, x1, mix, w_up, g3, g2, exchanges=()):
    T = dup.shape[0]
    tm = min(TM, T)

    def body(dup_ref, dy_ref, x1_ref, mix_ref, wa_ref, wb_ref, g3_ref, g2_ref, dx1_ref, dmix_ref, dg3_ref, dg2_ref):
        @pl.when(pl.program_id(0) == 0)
        def _():
            dg3_ref[...] = jnp.zeros_like(dg3_ref)
            dg2_ref[...] = jnp.zeros_like(dg2_ref)

        def back(w_ref):
            acc = _dot_nt(dup_ref[:, :D_MODEL], w_ref[0])
            for j in range(1, N_CHIPS):
                acc = acc + _dot_nt(dup_ref[:, D_MODEL * j:D_MODEL * (j + 1)], w_ref[j])
            return acc

        dh2 = jnp.concatenate([back(wa_ref), back(wb_ref)], axis=1)
        dx, dg3 = _norm_bwd(x1_ref[...], g3_ref[...], dh2)
        dx1 = dy_ref[...] + dx
        dx1_ref[...] = dx1
        dg3_ref[...] += dg3
        dmix, dg2 = _norm_bwd(mix_ref[...], g2_ref[...], dx1)
        dmix_ref[...] = dmix.astype(BF16)
        dg2_ref[...] += dg2

    return _call(
        body, name="mlp_up_bwd", grid=(T // tm,),
        in_specs=[_rows(tm, D_FF), _rows(tm, D_MODEL), _rows(tm, D_MODEL), _rows(tm, D_MODEL),
                  _HALVES, _HALVES, _const((1, D_MODEL)), _const((1, D_MODEL))],
        out_specs=[_rows(tm, D_MODEL), _rows(tm, D_MODEL), _const((1, D_MODEL)), _const((1, D_MODEL))],
        out_shape=[_sds((T, D_MODEL), F32), _sds((T, D_MODEL), BF16), _sds((1, D_MODEL), F32),
                   _sds((1, D_MODEL), F32)],
        args=(dup, dy, x1, mix, *w_up, g3, g2), sem=("arbitrary",), exchanges=exchanges)


def _dw(tag, a, g, ta, tn, shard_cols=False, exchanges=()):
    T, ka = a.shape
    n = g.shape[1]
    tk = min(TM, T)
    nk = T // tk

    def body(a_ref, g_ref, o_ref):
        @pl.when(pl.program_id(2) == 0)
        def _():
            o_ref[...] = jnp.zeros_like(o_ref)

        o_ref[...] += _dot_tn(a_ref[...], g_ref[...])

    if shard_cols:
        per = (n // N_CHIPS) // tn
        out_spec = pl.BlockSpec((None, ta, tn), lambda i, j, k: (j // per, i, j % per))
        out_shape = _sds((N_CHIPS, ka, n // N_CHIPS), F32)
    else:
        out_spec = pl.BlockSpec((ta, tn), lambda i, j, k: (i, j))
        out_shape = _sds((ka, n), F32)
    return _call(
        body, name="dw_" + tag, grid=(ka // ta, n // tn, nk),
        in_specs=[pl.BlockSpec((tk, ta), lambda i, j, k: (k, i)), pl.BlockSpec((tk, tn), lambda i, j, k: (k, j))],
        out_specs=[out_spec], out_shape=[out_shape],
        args=(a, g), sem=("parallel", "parallel", "arbitrary"), exchanges=exchanges)


def _dw_slabs(tag, a, g):
    T, ka = a.shape
    n = g.shape[1]
    c = n // N_CHIPS
    tk = min(TM, T)

    def body(a_ref, g_ref, o_ref):
        @pl.when(pl.program_id(0) == 0)
        def _():
            o_ref[...] = jnp.zeros_like(o_ref)

        res = _dot_tn(a_ref[...], g_ref[...])
        for j in range(N_CHIPS):
            o_ref[j] += res[:, c * j:c * (j + 1)]

    return _call(
        body, name="dw_" + tag, grid=(T // tk,),
        in_specs=[_rows(tk, ka), _rows(tk, n)],
        out_specs=[_const((N_CHIPS, ka, c))], out_shape=[_sds((N_CHIPS, ka, c), F32)],
        args=(a, g), sem=("arbitrary",))[0]


def _merge_bwd(dmix, gate, bp, ba, w_out, w_bp, w_ba, exchanges=()):
    T = dmix.shape[0]
    tm = min(TM, T)

    def body(dmix_ref, gate_ref, bp_ref, ba_ref, wo_ref, wbp_ref, wba_ref,
             dbp_ref, dba_ref, dgate_ref, dyp_ref, dya_ref):
        dm = _dot_nt(dmix_ref[...], wo_ref[...])
        for j, (b_ref, db_ref, w_ref, dy_ref) in enumerate(
                ((bp_ref, dbp_ref, wbp_ref, dyp_ref), (ba_ref, dba_ref, wba_ref, dya_ref))):
            sl = slice(D_MODEL * j, D_MODEL * (j + 1))
            gt = gate_ref[:, sl].astype(F32)
            db = (dm * gt).astype(BF16)
            db_ref[...] = db
            dgate_ref[:, sl] = (dm * b_ref[...].astype(F32) * gt * (1.0 - gt)).astype(BF16)
            cw = D_MODEL // N_CHIPS
            dy = _dot_nt(db[:, :cw], w_ref[0])
            for c in range(1, N_CHIPS):
                dy = dy + _dot_nt(db[:, cw * c:cw * (c + 1)], w_ref[c])
            dy_ref[...] = dy.astype(dy_ref.dtype)

    return _call(
        body, name="merge_bwd", grid=(T // tm,),
        in_specs=[_rows(tm, D_MODEL), _rows(tm, GATE_WIDTH), _rows(tm, D_MODEL), _rows(tm, D_MODEL),
                  _const((D_MODEL, D_MODEL)), _const(w_bp.shape), _const(w_ba.shape)],
        out_specs=[_rows(tm, D_MODEL), _rows(tm, D_MODEL), _rows(tm, GATE_WIDTH), _rows(tm, POOL_WIDTH),
                   _rows(tm, ATTN_WIDTH)],
        out_shape=[_sds((T, D_MODEL), BF16), _sds((T, D_MODEL), BF16), _sds((T, GATE_WIDTH), BF16),
                   _sds((T, POOL_WIDTH), F32), _sds((T, ATTN_WIDTH), BF16)],
        args=(dmix, gate, bp, ba, w_out, w_bp, w_ba), sem=("parallel",), exchanges=exchanges)


def _attn_bwd(q, k, v, do, sinks, tabs, seq, exchanges=()):
    T = q.shape[0]
    nb = seq // BLOCK
    steps = nb + 1

    def body(sink_ref, q_ref, do_ref, kp_ref, kc_ref, vp_ref, vc_ref, c_ref, a_ref, bt_ref, cp_ref, ap_ref, btp_ref,
             dq_ref, dk_ref, dv_ref, dsink_ref, ck_ref, cv_ref):
        i = pl.program_id(0)
        n = i % steps

        @pl.when(i == 0)
        def _():
            dsink_ref[...] = jnp.zeros_like(dsink_ref)

        @pl.when(n == 0)
        def _():
            ck_ref[...] = jnp.zeros_like(ck_ref)
            cv_ref[...] = jnp.zeros_like(cv_ref)

        @pl.when(n < nb)
        def _():
            kk = jnp.concatenate([kp_ref[...], kc_ref[...]], axis=0)
            vv = jnp.concatenate([vp_ref[...], vc_ref[...]], axis=0)
            valid, lo = _attn_masks(n)
            dk_acc = jnp.zeros((2 * BLOCK, KV_WIDTH), F32)
            dv_acc = jnp.zeros((2 * BLOCK, KV_WIDTH), F32)
            for h in range(2):
                qs = _stack_heads(q_ref, h, lo)
                dos = _stack_heads(do_ref, h, lo)
                pr, ps = _group_probs(qs, kk, valid, _sink_rows(sink_ref, h))
                dp = _dot_nt(dos, vv)
                delta = jnp.sum(pr * dp, axis=1, keepdims=True)
                ds = (pr * (dp - delta)).astype(BF16)
                dsk = ps * delta
                for g in range(GROUP):
                    idx = GROUP * h + g
                    dsink_ref[idx:idx + 1, :] += jnp.zeros((1, LANES), F32) - jnp.sum(dsk[BLOCK * g:BLOCK * (g + 1)])
                dk_acc = dk_acc + _dot_tn(ds, qs)
                dv_acc = dv_acc + _dot_tn(pr.astype(BF16), dos)
                for j, pair in enumerate(_unstack_heads(_dot(ds, kk) * SCALE, h, lo)):
                    sl = slice(LANES * (2 * h + j), LANES * (2 * h + j + 1))
                    dq_ref[:, sl] = _rot_bwd(pair, c_ref[...], a_ref[...], bt_ref[...]).astype(BF16)
            fin_k = ck_ref[...] + dk_acc[:BLOCK]
            dk_ref[...] = _rot_bwd(fin_k, cp_ref[...], ap_ref[...], btp_ref[...]).astype(BF16)
            dv_ref[...] = (cv_ref[...] + dv_acc[:BLOCK]).astype(BF16)
            ck_ref[...] = dk_acc[BLOCK:]
            cv_ref[...] = dv_acc[BLOCK:]

        @pl.when(n == nb)
        def _():
            dk_ref[...] = _rot_bwd(ck_ref[...], cp_ref[...], ap_ref[...], btp_ref[...]).astype(BF16)
            dv_ref[...] = cv_ref[...].astype(BF16)

    def blk(i):
        return (i // steps) * nb

    cur = lambda i: (blk(i) + jnp.minimum(i % steps, nb - 1), 0)
    prv = lambda i: (blk(i) + jnp.clip(i % steps - 1, 0, nb - 1), 0)
    tcur = lambda i: (jnp.minimum(i % steps, nb - 1), 0)
    tprv = lambda i: (jnp.clip(i % steps - 1, 0, nb - 1), 0)
    kv = lambda m: pl.BlockSpec((BLOCK, KV_WIDTH), m)
    return _call(
        body, name="attn_bwd", grid=((T // seq) * steps,),
        in_specs=[pl.BlockSpec(memory_space=pltpu.SMEM),
                  pl.BlockSpec((BLOCK, ATTN_WIDTH), cur), pl.BlockSpec((BLOCK, ATTN_WIDTH), cur),
                  kv(prv), kv(cur), kv(prv), kv(cur),
                  kv(tcur), kv(tcur), kv(tcur), kv(tprv), kv(tprv), kv(tprv)],
        out_specs=[pl.BlockSpec((BLOCK, ATTN_WIDTH), cur), kv(prv), kv(prv), _const((8, LANES))],
        out_shape=[_sds((T, ATTN_WIDTH), BF16), _sds((T, KV_WIDTH), BF16), _sds((T, KV_WIDTH), BF16),
                   _sds((8, LANES), F32)],
        scratch=[pltpu.VMEM((BLOCK, KV_WIDTH), F32), pltpu.VMEM((BLOCK, KV_WIDTH), F32)],
        args=(sinks, q, do, k, k, v, v, *tabs, *tabs), sem=("arbitrary",), exchanges=exchanges)


def _pool_bwd(dyp, diff, w_pool, pool_scale, seq, exchanges=()):
    T = dyp.shape[0]
    tp = min(TP, seq)
    nseq = seq // tp
    per = tp // HALO
    last_halo = T // HALO - 1

    def body(dy_ref, nxt_ref, diff_ref, w_ref, s_ref, du_ref, dw_ref, ds_ref):
        i = pl.program_id(0)

        @pl.when(i == 0)
        def _():
            dw_ref[...] = jnp.zeros_like(dw_ref)
            ds_ref[...] = jnp.zeros_like(ds_ref)

        last = (i % nseq) == nseq - 1
        nxt = jnp.where(last, 0.0, nxt_ref[...])
        ext = jnp.concatenate([dy_ref[...], nxt], axis=0) * s_ref[...]
        pos = (i % nseq) * tp + lax.broadcasted_iota(jnp.int32, (tp + HALO, 1), 0)
        for gi, w in enumerate(POOL_WINDOWS):
            sl = slice(POOL_GC * gi, POOL_GC * (gi + 1))
            wg = w_ref[gi].astype(BF16)
            dmx = ext[:, sl].astype(BF16)
            ddiff = _dot_nt(dmx, wg)
            s = ddiff * _inv_count(pos, w)
            sh = 1
            while sh < w:
                s = s + pltpu.roll(s, tp + HALO - sh, 0)
                sh *= 2
            du_ref[:, sl] = (s[:tp] - ddiff[:tp]).astype(BF16)
            dg = diff_ref[:, sl]
            dw_ref[gi] += _dot_tn(dg, dmx[:tp])
            ds_ref[:, sl] += jnp.sum(dy_ref[:, sl] * _dot(dg, wg), axis=0, keepdims=True)

    return _call(
        body, name="pool_bwd", grid=(T // tp,),
        in_specs=[_rows(tp, POOL_WIDTH),
                  pl.BlockSpec((HALO, POOL_WIDTH), lambda i: (jnp.minimum((i + 1) * per, last_halo), 0)),
                  _rows(tp, POOL_WIDTH), _const((4, POOL_GC, POOL_GC)), _const((1, POOL_WIDTH))],
        out_specs=[_rows(tp, POOL_WIDTH), _const((4, POOL_GC, POOL_GC)), _const((1, POOL_WIDTH))],
        out_shape=[_sds((T, POOL_WIDTH), BF16), _sds((4, POOL_GC, POOL_GC), F32), _sds((1, POOL_WIDTH), F32)],
        args=(dyp, dyp, diff, w_pool, pool_scale), sem=("arbitrary",), exchanges=exchanges)


_PARTS = ((0, C_Q), (C_Q, C_K), (C_K, C_V), (C_V, C_G), (C_G, IN_WIDTH))


def _inproj_bwd(parts, x2, dx1, w_in_t, g1, exchanges=()):
    T = x2.shape[0]
    tm = min(TM, T)

    def body(du_ref, dq_ref, dk_ref, dv_ref, dgt_ref, x_ref, dx1_ref, w_ref, g_ref, gx_ref, dg_ref):
        @pl.when(pl.program_id(0) == 0)
        def _():
            dg_ref[...] = jnp.zeros_like(dg_ref)

        dh = jnp.zeros((tm, D_MODEL), F32)
        for (lo, hi), p_ref in zip(_PARTS, (du_ref, dq_ref, dk_ref, dv_ref, dgt_ref)):
            dh = dh + _dot(p_ref[...], w_ref[lo:hi, :])
        dx, dg = _norm_bwd(x_ref[...], g_ref[...], dh)
        gx_ref[...] = dx1_ref[...] + dx
        dg_ref[...] += dg

    return _call(
        body, name="inproj_bwd", grid=(T // tm,),
        in_specs=[_rows(tm, hi - lo) for lo, hi in _PARTS]
        + [_rows(tm, D_MODEL), _rows(tm, D_MODEL), _const((IN_WIDTH, D_MODEL)), _const((1, D_MODEL))],
        out_specs=[_rows(tm, D_MODEL), _const((1, D_MODEL))],
        out_shape=[_sds((T, D_MODEL), F32), _sds((1, D_MODEL), F32)],
        args=(*parts, x2, dx1, w_in_t, g1), sem=("arbitrary",), exchanges=exchanges)


def _dw_in(h, parts, exchanges=()):
    T = h.shape[0]
    tk = min(TM, T)

    def body(h_ref, du_ref, dq_ref, dk_ref, dv_ref, dgt_ref, o_ref, db_ref):
        @pl.when(pl.program_id(0) == 0)
        def _():
            o_ref[...] = jnp.zeros_like(o_ref)
            db_ref[...] = jnp.zeros_like(db_ref)

        hh = h_ref[...]
        for (lo, hi), p_ref in zip(_PARTS, (du_ref, dq_ref, dk_ref, dv_ref, dgt_ref)):
            part = p_ref[...]
            o_ref[lo:hi, :] += _dot_tn(part, hh)
            db_ref[:, lo:hi] += jnp.sum(part.astype(F32), axis=0, keepdims=True)

    return _call(
        body, name="dw_in", grid=(T // tk,),
        in_specs=[_rows(tk, D_MODEL)] + [_rows(tk, hi - lo) for lo, hi in _PARTS],
        out_specs=[_const((IN_WIDTH, D_MODEL)), _const((1, IN_WIDTH))],
        out_shape=[_sds((IN_WIDTH, D_MODEL), F32), _sds((1, IN_WIDTH), F32)],
        args=(h, *parts), sem=("arbitrary",), exchanges=exchanges)


def _row_tile(rows, cap=256, mult=16):
    best = None
    for t in range(mult, min(rows, cap) + 1, mult):
        if rows % t == 0:
            best = t
    if best is None:
        raise ValueError("no row tile for %d rows" % rows)
    return best


def _pair_sum(ids, full, got):
    _, r, c = full.shape
    hr = r // 2
    tr = _row_tile(hr)
    nblk = hr // tr

    def body(ids_ref, a_ref, b_ref, s_ref, sb_ref):
        s = a_ref[...] + b_ref[...]
        s_ref[...] = s
        sb_ref[...] = s.astype(BF16)

    out_spec = pl.BlockSpec((None, tr, c), lambda j, i, ids_ref: (j, i, 0))
    return pl.pallas_call(
        body, name="pair_sum_%dx%d" % (r, c),
        grid_spec=pltpu.PrefetchScalarGridSpec(
            num_scalar_prefetch=1, grid=(N_CHIPS, nblk),
            in_specs=[pl.BlockSpec((None, tr, c), lambda j, i, ids_ref: (j, ids_ref[1] * nblk + i, 0)), out_spec],
            out_specs=[out_spec, out_spec]),
        out_shape=[_sds((N_CHIPS, hr, c), F32), _sds((N_CHIPS, hr, c), BF16)],
        compiler_params=_cp("parallel", "parallel"),
    )(ids, full, got)


def _chip_sum(ids, own, got):
    _, hr, c = own.shape
    tr = _row_tile(hr)
    nblk = hr // tr

    def body(ids_ref, a_ref, b_ref, o_ref):
        o_ref[...] = ((a_ref[...] + b_ref[0].astype(F32)) + b_ref[1].astype(F32)) + b_ref[2].astype(F32)

    return pl.pallas_call(
        body, name="chip_sum_%dx%d" % (hr, c),
        grid_spec=pltpu.PrefetchScalarGridSpec(
            num_scalar_prefetch=1, grid=(nblk,),
            in_specs=[pl.BlockSpec((None, tr, c), lambda i, ids_ref: (ids_ref[0], i, 0)),
                      pl.BlockSpec((3, tr, c), lambda i, ids_ref: (0, i, 0))],
            out_specs=pl.BlockSpec((tr, c), lambda i, ids_ref: (ids_ref[1] * nblk + i, 0))),
        out_shape=_sds((2 * hr, c), F32),
        compiler_params=_cp("parallel"),
    )(ids, own, got)


def _adamw_math(w, g, m, v):
    nm = ADAM_B1 * m + (1.0 - ADAM_B1) * g
    nv = ADAM_B2 * v + (1.0 - ADAM_B2) * jnp.square(g)
    m_hat = nm / (1.0 - ADAM_B1 ** ADAM_STEP)
    v_hat = nv / (1.0 - ADAM_B2 ** ADAM_STEP)
    return -ADAM_LR * (m_hat / (jnp.sqrt(v_hat) + ADAM_EPS) + ADAM_WD * w), nm, nv


def _adamw(w, g, m, v):
    r, c = w.shape
    tr = _row_tile(r, mult=8)

    def body(w_ref, g_ref, m_ref, v_ref, d_ref, nm_ref, nv_ref):
        d_ref[...], nm_ref[...], nv_ref[...] = _adamw_math(w_ref[...], g_ref[...], m_ref[...], v_ref[...])

    spec = _rows(tr, c)
    return pl.pallas_call(
        body, name="adamw_%dx%d" % (r, c), grid=(r // tr,),
        in_specs=[spec] * 4, out_specs=[spec] * 3, out_shape=[_sds((r, c), F32)] * 3,
        compiler_params=_cp("parallel"),
    )(w, g, m, v)


_SMALL_NAMES = ("w_pool", "b_in", "g_mix_pre", "g_mix_post", "g_mlp_pre", "g_mlp_post", "pool_scale", "attn_sinks")
B_ROWS = -(-IN_WIDTH // D_MODEL)


def _row_block(rows):
    rows = [jnp.pad(r.astype(F32), ((0, 0), (0, D_MODEL - r.shape[1]))) for r in rows]
    return jnp.pad(jnp.concatenate(rows, axis=0), ((0, 8 - len(rows)), (0, 0)))


def _early_block(dg2, dg3, dg4, dps, dsink, loss):
    tail = jnp.concatenate([jnp.pad(dsink.reshape(1, -1), ((0, 0), (0, LANES - dsink.size))),
                            jnp.pad(loss.reshape(1, 1), ((0, 0), (0, LANES - 1)))], axis=1)
    return _row_block([dg2, dg3, dg4, dps, tail])


def _late_block(db_in, dg1):
    b = jnp.pad(db_in, ((0, 0), (0, B_ROWS * D_MODEL - IN_WIDTH))).reshape(B_ROWS, D_MODEL)
    return _row_block([b[r:r + 1] for r in range(B_ROWS)] + [dg1])


def _small_update(gearly, gmat, glate, w, m, v):
    names = _SMALL_NAMES
    n = len(names)

    def total(ref, rows):
        acc = ref[0:rows, :]
        for d in range(1, N_DEV):
            acc = acc + ref[d * rows:(d + 1) * rows, :]
        return acc

    def body(*refs):
        early_ref, gmat_ref, late_ref = refs[:3]
        w_refs, m_refs, v_refs = refs[3:3 + n], refs[3 + n:3 + 2 * n], refs[3 + 2 * n:3 + 3 * n]
        outs = refs[3 + 3 * n:]
        loss_ref, g_refs, d_refs = outs[0], outs[1:1 + n], outs[1 + n:1 + 2 * n]
        nm_refs, nv_refs = outs[1 + 2 * n:1 + 3 * n], outs[1 + 3 * n:1 + 4 * n]
        early, late = total(early_ref, 8), total(late_ref, 8)
        loss_ref[...] = jnp.sum(early[4:5, LANES:2 * LANES], axis=1, keepdims=True)
        bias = jnp.concatenate([late[r:r + 1, :] for r in range(B_ROWS - 1)]
                               + [late[B_ROWS - 1:B_ROWS, :IN_WIDTH - (B_ROWS - 1) * D_MODEL]], axis=1)
        grad = dict(b_in=bias, g_mix_pre=late[B_ROWS:B_ROWS + 1, :], g_mix_post=early[0:1, :],
                    g_mlp_pre=early[1:2, :], g_mlp_post=early[2:3, :], pool_scale=early[3:4, :POOL_WIDTH],
                    attn_sinks=early[4:5, :N_Q_HEADS])
        for i, name in enumerate(names):
            g = total(gmat_ref, 4 * POOL_GC) if name == "w_pool" else grad[name]
            g_refs[i][...] = g
            d_refs[i][...], nm_refs[i][...], nv_refs[i][...] = _adamw_math(
                w_refs[i][...], g, m_refs[i][...], v_refs[i][...])

    shapes = [_sds(w[k].shape, F32) for k in names]
    res = pl.pallas_call(
        body, name="small_update", out_shape=[_sds((1, 1), F32)] + shapes * 4,
        compiler_params=pltpu.CompilerParams(vmem_limit_bytes=VMEM_MB * 1024 * 1024),
    )(gearly, gmat, glate, *[w[k] for k in names], *[m[k] for k in names], *[v[k] for k in names])
    loss = res[0]
    per = {k: tuple(res[1 + j * n + i] for j in range(4)) for i, k in enumerate(names)}
    return loss, per


_BIG = ("w_in", "w_branch_pool", "w_branch_attn", "w_out", "w_up", "w_down")
_ORDER = ("g_mix_pre", "w_in", "b_in", "w_pool", "pool_scale", "attn_sinks", "w_branch_pool", "w_branch_attn",
          "w_out", "g_mix_post", "g_mlp_pre", "w_up", "w_down", "g_mlp_post")


def _stack_rows(slab):
    return slab.reshape(-1, slab.shape[2])


def _step(x2, tgt, seq, shards, small, ids):
    tabs = _rope_tables(seq)
    g1, g2, g3, g4 = (small[n] for n in ("g_mix_pre", "g_mix_post", "g_mlp_pre", "g_mlp_post"))
    sinks = small["attn_sinks"].reshape(N_Q_HEADS)
    w_pool = small["w_pool"].reshape(4, POOL_GC, POOL_GC)
    pool_scale = small["pool_scale"]

    def whole(shard, slabs):
        return lax.dynamic_update_slice(slabs, shard[None], (ids[0], 0, 0))

    up_a, up_b = shards["w_up"][:HALF], shards["w_up"][HALF:]
    down_a, down_b = shards["w_down"][:HALF], shards["w_down"][HALF:]
    w_in = _stack_rows(whole(shards["w_in"], _alone("gather_in", _ex_gather([shards["w_in"]]))[0][0]))
    mix_shards = [shards[n] for n in ("w_branch_pool", "w_branch_attn", "w_out")]
    (h, u, q, k, v, gate), [mix_slabs] = _inproj(
        x2, g1, w_in, small["b_in"], tabs, seq, exchanges=[_ex_gather(mix_shards)])
    w_bp, w_ba, out_slab = (whole(s, g) for s, g in zip(mix_shards, mix_slabs))
    w_out = _stack_rows(out_slab)
    diff, y_pool = _pool_fwd(u, w_pool, pool_scale, seq)
    (y_attn,), [[got_a]] = _attn_fwd(q, k, v, sinks, seq, exchanges=[_ex_gather([up_a])])
    (bp, ba, merged, mix, x1, h2), [[got_b, got_c]] = _merge_out(
        y_pool, y_attn, gate, x2, w_bp, w_ba, w_out, g2, g3, exchanges=[_ex_gather([up_b, down_a])])
    w_up = (whole(up_a, got_a), whole(up_b, got_b))
    (up, act), [[got_d]] = _mlp_up(h2, w_up, exchanges=[_ex_gather([down_b])])
    w_down = (whole(down_a, got_c), whole(down_b, got_d))
    dff, dy, loss_acc, dg4 = _mlp_down_loss(act, x1, tgt, w_down, g4)

    dup = _mlp_down_bwd(dff, up, w_down)
    dw_down = _dw("down", act, dff, 1024, 1024)[0].reshape(N_CHIPS, D_FF // N_CHIPS, D_MODEL)
    (dx1, dmix, dg3, dg2), [[got]] = _mlp_up_bwd(dup, dy, x1, mix, w_up, g3, g2, exchanges=[_ex_pair([dw_down])])
    ps_down = _pair_sum(ids, dw_down, got)
    (dw_up,), [[got]] = _dw("up", h2, dup, 1024, 1024, shard_cols=True, exchanges=[_ex_chip([ps_down[1]])])
    half_down = _chip_sum(ids, ps_down[0], got)
    (dbp, dba, dgate, dyp, dya), [[got], [g_down]] = _merge_bwd(
        dmix, gate, bp, ba, w_out, w_bp, w_ba, exchanges=[_ex_pair([dw_up]), _ex_swap([half_down])])
    ps_up = _pair_sum(ids, dw_up, got)
    dw_mix = [_dw("out", merged, dmix, 1024, 1024)[0].reshape(N_CHIPS, D_MODEL // N_CHIPS, D_MODEL),
              _dw_slabs("branch_pool", y_pool, dbp), _dw_slabs("branch_attn", y_attn, dba)]
    (dq, dk, dv, dsink), [[got], gots] = _attn_bwd(
        q, k, v, dya, sinks, tabs, seq, exchanges=[_ex_chip([ps_up[1]]), _ex_pair(dw_mix)])
    half_up = _chip_sum(ids, ps_up[0], got)
    ps_mix = [_pair_sum(ids, d, g) for d, g in zip(dw_mix, gots)]
    (du, dw_pool, dps), [[g_up]] = _pool_bwd(dyp, diff, w_pool, pool_scale, seq, exchanges=[_ex_swap([half_up])])
    parts = (du, dq, dk, dv, dgate)
    early = _early_block(dg2, dg3, dg4, dps, dsink[:, 0], loss_acc[0, 0])
    mat = dw_pool.reshape(4 * POOL_GC, POOL_GC)
    (dw_in_t, db_in), [gots, [gearly, gmat]] = _dw_in(
        h, parts, exchanges=[_ex_chip([p[1] for p in ps_mix]), _ex_allgather([early, mat])])
    half_mix = [_chip_sum(ids, p[0], g) for p, g in zip(ps_mix, gots)]
    dw_in = dw_in_t.reshape(N_CHIPS, IN_WIDTH // N_CHIPS, D_MODEL)
    g_mix, [got] = _alone("swap_mix_pair_in", _ex_swap(half_mix), _ex_pair([dw_in]))
    ps_in = _pair_sum(ids, dw_in, got)
    (gx, dg1), [[got]] = _inproj_bwd(parts, x2, dx1, w_in, g1, exchanges=[_ex_chip([ps_in[1]])])
    [g_in], [glate] = _alone("swap_in_allgather", _ex_swap([_chip_sum(ids, ps_in[0], got)]),
                             _ex_allgather([_late_block(db_in, dg1)]))

    grads = dict(w_in=g_in, w_branch_pool=g_mix[1], w_branch_attn=g_mix[2], w_out=g_mix[0], w_up=g_up, w_down=g_down)
    return (gearly, gmat, glate), gx, grads


def kernel(x, g_mix_pre, w_in, b_in, w_pool, pool_scale, attn_sinks, w_branch_pool, w_branch_attn, w_out, g_mix_post, g_mlp_pre, w_up, w_down, g_mlp_post, loss_target, m_g_mix_pre, m_w_in, m_b_in, m_w_pool, m_pool_scale, m_attn_sinks, m_w_branch_pool, m_w_branch_attn, m_w_out, m_g_mix_post, m_g_mlp_pre, m_w_up, m_w_down, m_g_mlp_post, v_g_mix_pre, v_w_in, v_b_in, v_w_pool, v_pool_scale, v_attn_sinks, v_w_branch_pool, v_w_branch_attn, v_w_out, v_g_mix_post, v_g_mlp_pre, v_w_up, v_w_down, v_g_mlp_post):
    weights = dict(g_mix_pre=g_mix_pre, w_in=w_in, b_in=b_in, w_pool=w_pool, pool_scale=pool_scale,
                   attn_sinks=attn_sinks, w_branch_pool=w_branch_pool, w_branch_attn=w_branch_attn, w_out=w_out,
                   g_mix_post=g_mix_post, g_mlp_pre=g_mlp_pre, w_up=w_up, w_down=w_down, g_mlp_post=g_mlp_post)
    mom1 = dict(g_mix_pre=m_g_mix_pre, w_in=m_w_in, b_in=m_b_in, w_pool=m_w_pool, pool_scale=m_pool_scale,
                attn_sinks=m_attn_sinks, w_branch_pool=m_w_branch_pool, w_branch_attn=m_w_branch_attn,
                w_out=m_w_out, g_mix_post=m_g_mix_post, g_mlp_pre=m_g_mlp_pre, w_up=m_w_up, w_down=m_w_down,
                g_mlp_post=m_g_mlp_post)
    mom2 = dict(g_mix_pre=v_g_mix_pre, w_in=v_w_in, b_in=v_b_in, w_pool=v_w_pool, pool_scale=v_pool_scale,
                attn_sinks=v_attn_sinks, w_branch_pool=v_w_branch_pool, w_branch_attn=v_w_branch_attn,
                w_out=v_w_out, g_mix_post=v_g_mix_post, g_mlp_pre=v_g_mlp_pre, w_up=v_w_up, w_down=v_w_down,
                g_mlp_post=v_g_mlp_post)
    b_loc, seq, _ = x.shape
    x2 = x.reshape(b_loc * seq, D_MODEL)
    tgt = loss_target.reshape(b_loc * seq, D_MODEL)
    ids = jnp.stack([2 * lax.axis_index("x") + lax.axis_index("y"), lax.axis_index("c")]).astype(jnp.int32)

    def flat(n, a):
        return a[0].T if n == "w_in" else a[0]

    def unflat(n, a):
        return (a.T if n == "w_in" else a)[None]

    shards = {n: flat(n, weights[n]).astype(BF16) for n in _BIG}
    small = {n: weights[n] for n in _ORDER if n not in _BIG}
    (gearly, gmat, glate), gx, grads = _step(x2, tgt, seq, shards, small, ids)

    def two_d(src):
        return {n: src[n].reshape(4 * POOL_GC, POOL_GC) if n == "w_pool" else src[n] for n in _SMALL_NAMES}

    loss, per = _small_update(gearly, gmat, glate, two_d(weights), two_d(mom1), two_d(mom2))
    delta, new_m, new_v = {}, {}, {}
    for n in _SMALL_NAMES:
        grads[n], delta[n], new_m[n], new_v[n] = (a.reshape(weights[n].shape) for a in per[n])
    for n in _BIG:
        d, nm, nv = _adamw(flat(n, weights[n]), grads[n], flat(n, mom1[n]), flat(n, mom2[n]))
        grads[n] = unflat(n, grads[n])
        delta[n], new_m[n], new_v[n] = unflat(n, d), unflat(n, nm), unflat(n, nv)

    return (loss[0, 0], gx.reshape(x.shape), *[grads[n] for n in _ORDER], *[delta[n] for n in _ORDER],
            *[new_m[n] for n in _ORDER], *[new_v[n] for n in _ORDER])
```

```python
import jax
import jax.numpy as jnp
from jax import lax
from jax.experimental import pallas as pl
from jax.experimental.pallas import tpu as pltpu

F32 = jnp.float32
BF16 = jnp.bfloat16

D_MODEL = 1024
POOL_WINDOWS = (2, 4, 8, 16)
POOL_WIDTH = 512
POOL_GC = 128
HALO = 16
HEAD_DIM = 64
N_Q_HEADS = 8
ATTN_WIDTH = 512
KV_WIDTH = 128
BLOCK = 128
NEG_INF = -1e30
ROPE_THETA = 500000.0
ROT_DIM = 16
GATE_WIDTH = 2048
IN_WIDTH = 3328
D_FF = 4096
EPS = 1e-6
SCALE = HEAD_DIM ** -0.5
C_Q, C_K, C_V, C_G = 512, 1024, 1152, 1280

ADAM_LR, ADAM_B1, ADAM_B2, ADAM_EPS, ADAM_WD, ADAM_STEP = 0.001, 0.9, 0.999, 1e-08, 0.01, 10

N_CHIPS = 4
N_DEV = 8
LANES = 128
TM = 512
TP = 512
VMEM_MB = 56

MESH = pl.DeviceIdType.MESH
ANY = pl.BlockSpec(memory_space=pl.ANY)


def _cp(*sem, vmem=VMEM_MB):
    return pltpu.CompilerParams(dimension_semantics=sem, vmem_limit_bytes=vmem * 1024 * 1024)


def _rows(tile, cols):
    return pl.BlockSpec((tile, cols), lambda i: (i, 0))


def _const(shape):
    nd = len(shape)
    return pl.BlockSpec(shape, lambda i: (0,) * nd)


def _sds(shape, dtype):
    return jax.ShapeDtypeStruct(shape, dtype)


def _dot(a, b):
    return jnp.dot(a, b, preferred_element_type=F32)


def _dot_nt(a, b):
    return lax.dot_general(a, b, (((1,), (1,)), ((), ())), preferred_element_type=F32)


def _dot_tn(a, b):
    return lax.dot_general(a, b, (((0,), (0,)), ((), ())), preferred_element_type=F32)


def _rms(x):
    return lax.rsqrt(jnp.mean(x * x, axis=-1, keepdims=True) + EPS)


def _norm_bwd(x, g, dout):
    r = _rms(x)
    n = x * r
    dn = dout * g
    dx = r * (dn - n * jnp.mean(dn * n, axis=-1, keepdims=True))
    return dx, jnp.sum(dout * n, axis=0, keepdims=True)


def _rot_fwd(t, c, a, bt):
    return t * c + pltpu.roll(t, LANES - 8, 1) * a + pltpu.roll(t, 8, 1) * bt


def _rot_bwd(d, c, a, bt):
    return d * c + pltpu.roll(d * a, 8, 1) + pltpu.roll(d * bt, LANES - 8, 1)


def _rope_tables(seq):
    pos = jnp.arange(seq, dtype=F32)
    inv_freq = ROPE_THETA ** (-jnp.arange(0, ROT_DIM, 2, dtype=F32) / ROT_DIM)
    ang = pos[:, None] * inv_freq[None, :]
    cos, sin = jnp.cos(ang), jnp.sin(ang)
    ones = jnp.ones((seq, HEAD_DIM - ROT_DIM), F32)
    zeros8 = jnp.zeros((seq, 8), F32)
    zrest = jnp.zeros((seq, HEAD_DIM - ROT_DIM), F32)
    c = jnp.concatenate([cos, cos, ones], axis=1)
    a = jnp.concatenate([-sin, zeros8, zrest], axis=1)
    bt = jnp.concatenate([zeros8, sin, zrest], axis=1)
    return tuple(jnp.tile(t, (1, 2)) for t in (c, a, bt))


class _Exchange:
    def __init__(self, inputs, out_shapes, sems, start, finish, aliases=None, middle=None):
        self.inputs, self.out_shapes, self.sems = list(inputs), list(out_shapes), list(sems)
        self.start, self.finish, self.aliases = start, finish, dict(aliases or {})
        self.middle = middle


def _call(body, *, name, grid, in_specs, out_specs, out_shape, args, scratch=(), sem=(), exchanges=()):
    in_specs, out_specs, out_shape, scratch = list(in_specs), list(out_specs), list(out_shape), list(scratch)
    if not exchanges:
        return pl.pallas_call(body, name=name, grid=grid, in_specs=in_specs, out_specs=out_specs,
                              out_shape=out_shape, scratch_shapes=scratch, compiler_params=_cp(*sem))(*args)
    n_in, n_out, n_scr = len(in_specs), len(out_specs), len(scratch)
    x_in = [a for ex in exchanges for a in ex.inputs]
    x_out = [s for ex in exchanges for s in ex.out_shapes]
    x_sem = [s for ex in exchanges for s in ex.sems]
    aliases, i_off, o_off = {}, n_in, n_out
    for ex in exchanges:
        for i, o in ex.aliases.items():
            aliases[i_off + i] = o_off + o
        i_off += len(ex.inputs)
        o_off += len(ex.out_shapes)

    def split(flat):
        out, pos = [], 0
        for ex, n in zip(exchanges, flat[1]):
            out.append(flat[0][pos:pos + n])
            pos += n
        return out

    def carrier(*refs):
        pos = 0
        groups = []
        for n in (n_in, len(x_in), n_out, len(x_out), n_scr, len(x_sem)):
            groups.append(refs[pos:pos + n])
            pos += n
        ins, xin, outs, xout, scr, xsem = groups
        xin = split((xin, [len(ex.inputs) for ex in exchanges]))
        xout = split((xout, [len(ex.out_shapes) for ex in exchanges]))
        xsem = split((xsem, [len(ex.sems) for ex in exchanges]))
        first = pl.program_id(0) == 0
        last = pl.program_id(0) == grid[0] - 1
        for d in range(1, len(grid)):
            first = jnp.logical_and(first, pl.program_id(d) == 0)
            last = jnp.logical_and(last, pl.program_id(d) == grid[d] - 1)

        @pl.when(first)
        def _():
            for ex, i, o, s in zip(exchanges, xin, xout, xsem):
                ex.start(i, o, s)

        if any(ex.middle for ex in exchanges):
            half = pl.program_id(0) == grid[0] // 2
            for d in range(1, len(grid)):
                half = jnp.logical_and(half, pl.program_id(d) == 0)

            @pl.when(half)
            def _():
                for ex, i, o, s in zip(exchanges, xin, xout, xsem):
                    if ex.middle:
                        ex.middle(i, o, s)

        body(*ins, *outs, *scr)

        @pl.when(last)
        def _():
            for ex, i, o, s in zip(exchanges, xin, xout, xsem):
                ex.finish(i, o, s)

    res = pl.pallas_call(
        carrier, name=name, grid=grid, in_specs=in_specs + [ANY] * len(x_in),
        out_specs=out_specs + [ANY] * len(x_out), out_shape=out_shape + x_out,
        scratch_shapes=scratch + x_sem, input_output_aliases=aliases,
        compiler_params=_cp(*(["arbitrary"] * len(grid))),
    )(*args, *x_in)
    return res[:n_out], split((res[n_out:], [len(ex.out_shapes) for ex in exchanges]))


def _alone(name, *exchanges):
    n_in = [len(ex.inputs) for ex in exchanges]
    n_out = [len(ex.out_shapes) for ex in exchanges]
    n_sem = [len(ex.sems) for ex in exchanges]
    aliases, i_off, o_off = {}, 0, 0
    for ex in exchanges:
        for i, o in ex.aliases.items():
            aliases[i_off + i] = o_off + o
        i_off += len(ex.inputs)
        o_off += len(ex.out_shapes)

    def split(flat, counts):
        out, pos = [], 0
        for n in counts:
            out.append(flat[pos:pos + n])
            pos += n
        return out

    def body(*refs):
        ins, outs, sems = split(refs, [sum(n_in), sum(n_out), sum(n_sem)])
        groups = list(zip(exchanges, split(ins, n_in), split(outs, n_out), split(sems, n_sem)))
        for ex, i, o, s in groups:
            ex.start(i, o, s)
        for ex, i, o, s in groups:
            if ex.middle:
                ex.middle(i, o, s)
        for ex, i, o, s in groups:
            ex.finish(i, o, s)

    res = pl.pallas_call(
        body, name=name, in_specs=[ANY] * sum(n_in), out_specs=[ANY] * sum(n_out),
        out_shape=[s for ex in exchanges for s in ex.out_shapes],
        scratch_shapes=[s for ex in exchanges for s in ex.sems], input_output_aliases=aliases,
    )(*[a for ex in exchanges for a in ex.inputs])
    return split(res, n_out)


def _place():
    x, y, c = lax.axis_index("x"), lax.axis_index("y"), lax.axis_index("c")
    chips = [(1 - x, y), (x, 1 - y), (1 - x, 1 - y)]
    return x, y, c, chips


def _remote(src, dst, send, recv, to):
    return pltpu.make_async_remote_copy(src_ref=src, dst_ref=dst, send_sem=send, recv_sem=recv,
                                        device_id=to, device_id_type=MESH)


def _ex_gather(shards):
    nw = len(shards)
    hrs = [s.shape[0] // 2 for s in shards]

    def copies(ins, outs, sems):
        s1, r1, s2, r2, fs, fr = sems
        x, y, c, _ = _place()
        me, xn, yn, dg = (x, y), (1 - x, y), (x, 1 - y), (1 - x, 1 - y)
        nbr = (xn, yn)
        sibling = (x, y, 1 - c)

        def piece(w, chip, core, part=None):
            hr = hrs[w]
            rows = pl.ds(core * hr, hr) if part is None else pl.ds(core * hr + part * (hr // 2), hr // 2)
            return outs[w].at[2 * chip[0] + chip[1], rows]

        def first(w, k):
            return _remote(ins[w].at[pl.ds(c * hrs[w], hrs[w])], piece(w, me, c), s1.at[w, k], r1.at[w, k],
                           (*nbr[k], c))

        def landed(w, k):
            return _remote(piece(w, nbr[k], c), piece(w, nbr[k], c), s1.at[w, k], r1.at[w, k], (*nbr[k], c))

        def onward(w, k):
            return _remote(piece(w, nbr[k], c, k), piece(w, nbr[k], c, k), s2.at[w, k], r2.at[w, k],
                           (*nbr[1 - k], c))

        def arrived(w, k):
            return _remote(piece(w, dg, c, k), piece(w, dg, c, k), s2.at[w, k], r2.at[w, k], (*nbr[1 - k], c))

        def passed(w, j):
            chip = (xn, yn, dg)[j]
            return _remote(piece(w, chip, c), piece(w, chip, c), fs.at[w, j], fr.at[w, j], sibling)

        def handed(w, j):
            chip = (xn, yn, dg)[j]
            return _remote(piece(w, chip, 1 - c), piece(w, chip, 1 - c), fs.at[w, j], fr.at[w, j], sibling)

        return first, landed, onward, arrived, passed, handed

    def start(ins, outs, sems):
        first = copies(ins, outs, sems)[0]
        for w in range(nw):
            for k in range(2):
                first(w, k).start()

    def middle(ins, outs, sems):
        _, landed, onward, _, passed, _ = copies(ins, outs, sems)
        for w in range(nw):
            for k in range(2):
                landed(w, k).wait_recv()
                onward(w, k).start()
                passed(w, k).start()

    def finish(ins, outs, sems):
        first, _, onward, arrived, passed, handed = copies(ins, outs, sems)
        for w in range(nw):
            for k in range(2):
                arrived(w, k).wait_recv()
            passed(w, 2).start()
        for w in range(nw):
            for j in range(3):
                handed(w, j).wait_recv()
        for w in range(nw):
            for k in range(2):
                first(w, k).wait_send()
                onward(w, k).wait_send()
            for j in range(3):
                passed(w, j).wait_send()

    return _Exchange(shards, [_sds((N_CHIPS,) + s.shape, s.dtype) for s in shards],
                     [pltpu.SemaphoreType.DMA((nw, 2))] * 4 + [pltpu.SemaphoreType.DMA((nw, 3))] * 2,
                     start, finish, middle=middle)


def _ex_pair(grads):
    nw = len(grads)

    def copies(ins, outs, sems):
        x, y, c, _ = _place()
        out = []
        for w in range(nw):
            hr = grads[w].shape[1] // 2
            out.append(_remote(ins[w].at[:, pl.ds((1 - c) * hr, hr)], outs[w], sems[0].at[w], sems[1].at[w],
                               (x, y, 1 - c)))
        return out

    def start(ins, outs, sems):
        for cp in copies(ins, outs, sems):
            cp.start()

    def finish(ins, outs, sems):
        for cp in copies(ins, outs, sems):
            cp.wait()

    return _Exchange(grads, [_sds((N_CHIPS, g.shape[1] // 2, g.shape[2]), F32) for g in grads],
                     [pltpu.SemaphoreType.DMA((nw,))] * 2, start, finish)


def _ex_chip(pieces):
    nw = len(pieces)

    def copies(ins, outs, sems):
        x, y, c, chips = _place()
        return [_remote(ins[w].at[2 * cx + cy], outs[w].at[k], sems[0].at[w, k], sems[1].at[w, k], (cx, cy, c))
                for w in range(nw) for k, (cx, cy) in enumerate(chips)]

    def start(ins, outs, sems):
        for cp in copies(ins, outs, sems):
            cp.start()

    def finish(ins, outs, sems):
        for cp in copies(ins, outs, sems):
            cp.wait()

    return _Exchange(pieces, [_sds((3,) + p.shape[1:], BF16) for p in pieces],
                     [pltpu.SemaphoreType.DMA((nw, 3))] * 2, start, finish)


def _ex_swap(fulls):
    nw = len(fulls)

    def start(ins, outs, sems):
        x, y, c, _ = _place()
        for w in range(nw):
            hr = fulls[w].shape[0] // 2
            mine = pl.ds(c * hr, hr)
            _remote(ins[w].at[mine], outs[w].at[mine], sems[0].at[w], sems[1].at[w], (x, y, 1 - c)).start()

    def finish(ins, outs, sems):
        x, y, c, _ = _place()
        for w in range(nw):
            hr = fulls[w].shape[0] // 2
            mine, theirs = pl.ds(c * hr, hr), pl.ds((1 - c) * hr, hr)
            _remote(ins[w].at[mine], outs[w].at[mine], sems[0].at[w], sems[1].at[w], (x, y, 1 - c)).wait_send()
            _remote(ins[w].at[theirs], outs[w].at[theirs], sems[0].at[w], sems[1].at[w], (x, y, 1 - c)).wait_recv()

    return _Exchange(fulls, [_sds(f.shape, F32) for f in fulls], [pltpu.SemaphoreType.DMA((nw,))] * 2,
                     start, finish, aliases={w: w for w in range(nw)})


def _ex_allgather(blocks):
    nb = len(blocks)

    def copies(ins, outs, sems):
        send, recv, lsem = sems
        x, y, c, chips = _place()
        me, sibling = (x, y, c), (x, y, 1 - c)

        def rows(b, px, py, pc):
            m_per = blocks[b].shape[0]
            return outs[b].at[pl.ds((4 * px + 2 * py + pc) * m_per, m_per), :]

        def copy(b, k, blk, to, src=None):
            return _remote(rows(b, *blk) if src is None else src, rows(b, *blk), send.at[b, k], recv.at[b, k], to)

        def mine(b):
            return pltpu.make_async_copy(ins[b], rows(b, *me), lsem.at[b])

        def first(b, k):
            return copy(b, k, me, sibling if k == 0 else (*chips[k - 1], c), src=ins[b])

        def passed(b, j):
            return copy(b, 4 + j, (*chips[j], c), sibling)

        def landed(b, j):
            return copy(b, 1 + j, (*chips[j], c), me)

        def handed(b, k):
            return copy(b, 0, sibling, me) if k == 0 else copy(b, 3 + k, (*chips[k - 1], 1 - c), me)

        return mine, first, passed, landed, handed

    def start(ins, outs, sems):
        mine, first, _, _, _ = copies(ins, outs, sems)
        for b in range(nb):
            mine(b).start()
            for k in range(4):
                first(b, k).start()

    def finish(ins, outs, sems):
        mine, first, passed, landed, handed = copies(ins, outs, sems)
        sent = []
        for b in range(nb):
            for j in range(3):
                landed(b, j).wait_recv()
                cp = passed(b, j)
                cp.start()
                sent.append(cp)
        for b in range(nb):
            for k in range(4):
                handed(b, k).wait_recv()
            for k in range(4):
                first(b, k).wait_send()
        for cp in sent:
            cp.wait_send()
        for b in range(nb):
            mine(b).wait()

    return _Exchange(blocks, [_sds((N_DEV * b.shape[0], b.shape[1]), F32) for b in blocks],
                     [pltpu.SemaphoreType.DMA((nb, 7)), pltpu.SemaphoreType.DMA((nb, 7)), pltpu.SemaphoreType.DMA((nb,))],
                     start, finish)


def _inproj(x2, g1, w_in_t, b_in, tabs, seq, exchanges=()):
    T = x2.shape[0]
    tm = min(TM, seq)
    nseq = seq // tm

    def body(x_ref, g_ref, w_ref, b_ref, c_ref, a_ref, bt_ref, h_ref, u_ref, q_ref, k_ref, v_ref, gate_ref):
        x = x_ref[...]
        h = (x * _rms(x) * g_ref[...]).astype(BF16)
        h_ref[...] = h

        def proj(lo, hi):
            return _dot_nt(h, w_ref[lo:hi, :]) + b_ref[:, lo:hi]

        c, a, bt = c_ref[...], a_ref[...], bt_ref[...]
        u_ref[...] = proj(0, C_Q)
        q = proj(C_Q, C_K)
        for p in range(4):
            sl = slice(LANES * p, LANES * (p + 1))
            q_ref[:, sl] = (_rot_fwd(q[:, sl], c, a, bt) * SCALE).astype(BF16)
        kv = proj(C_K, C_G)
        k_ref[...] = _rot_fwd(kv[:, :KV_WIDTH], c, a, bt).astype(BF16)
        v_ref[...] = kv[:, KV_WIDTH:].astype(BF16)
        for j in range(2):
            lo = C_G + D_MODEL * j
            gate_ref[:, D_MODEL * j:D_MODEL * (j + 1)] = jax.nn.sigmoid(proj(lo, lo + D_MODEL)).astype(BF16)

    tab = pl.BlockSpec((tm, LANES), lambda i: (i % nseq, 0))
    return _call(
        body, name="inproj", grid=(T // tm,),
        in_specs=[_rows(tm, D_MODEL), _const((1, D_MODEL)), _const((IN_WIDTH, D_MODEL)), _const((1, IN_WIDTH)),
                  tab, tab, tab],
        out_specs=[_rows(tm, D_MODEL), _rows(tm, POOL_WIDTH), _rows(tm, ATTN_WIDTH), _rows(tm, KV_WIDTH),
                   _rows(tm, KV_WIDTH), _rows(tm, GATE_WIDTH)],
        out_shape=[_sds((T, D_MODEL), BF16), _sds((T, POOL_WIDTH), F32), _sds((T, ATTN_WIDTH), BF16),
                   _sds((T, KV_WIDTH), BF16), _sds((T, KV_WIDTH), BF16), _sds((T, GATE_WIDTH), BF16)],
        args=(x2, g1, w_in_t, b_in, *tabs), sem=("parallel",), exchanges=exchanges)


def _inv_count(pos, w):
    return 1.0 / jnp.minimum(pos + 1, w).astype(F32)


def _pool_fwd(u, w_pool, pool_scale, seq):
    T = u.shape[0]
    tp = min(TP, seq)
    nseq = seq // tp
    per = tp // HALO

    def body(u_ref, prev_ref, w_ref, s_ref, diff_ref, y_ref):
        i = pl.program_id(0)
        first = (i % nseq) == 0
        prev = jnp.where(first, 0.0, prev_ref[...])
        ext = jnp.concatenate([prev, u_ref[...]], axis=0)
        pos = (i % nseq) * tp + lax.broadcasted_iota(jnp.int32, (tp, 1), 0)
        for gi, w in enumerate(POOL_WINDOWS):
            sl = slice(POOL_GC * gi, POOL_GC * (gi + 1))
            xg = ext[:, sl]
            s = xg
            sh = 1
            while sh < w:
                s = s + pltpu.roll(s, sh, 0)
                sh *= 2
            pooled = s[HALO:] * _inv_count(pos, w)
            diff = (pooled - xg[HALO:]).astype(BF16)
            diff_ref[:, sl] = diff
            mixed = _dot(diff, w_ref[gi].astype(BF16))
            y_ref[:, sl] = (mixed * s_ref[:, sl]).astype(BF16)

    return _call(
        body, name="pool_fwd", grid=(T // tp,),
        in_specs=[_rows(tp, POOL_WIDTH),
                  pl.BlockSpec((HALO, POOL_WIDTH), lambda i: (jnp.maximum(i * per - 1, 0), 0)),
                  _const((4, POOL_GC, POOL_GC)), _const((1, POOL_WIDTH))],
        out_specs=[_rows(tp, POOL_WIDTH), _rows(tp, POOL_WIDTH)],
        out_shape=[_sds((T, POOL_WIDTH), BF16), _sds((T, POOL_WIDTH), BF16)],
        args=(u, u, w_pool, pool_scale), sem=("parallel",))


GROUP = 4
GROWS = GROUP * BLOCK


def _attn_masks(n):
    qi = lax.broadcasted_iota(jnp.int32, (GROWS, 2 * BLOCK), 0) % BLOCK
    kj = lax.broadcasted_iota(jnp.int32, (GROWS, 2 * BLOCK), 1)
    rel = qi + BLOCK - kj
    valid = (rel >= 0) & (rel < BLOCK) & (kj >= jnp.where(n > 0, 0, BLOCK))
    lo = lax.broadcasted_iota(jnp.int32, (BLOCK, LANES), 1) < HEAD_DIM
    return valid, lo


def _by_example(bl, *arrays):
    return [a.reshape(bl, a.shape[0] // bl, a.shape[1]) for a in arrays]


def _stack_heads(ref, h, lo):
    keep = lo if h == 0 else jnp.logical_not(lo)
    pieces = []
    for p in (2 * h, 2 * h + 1):
        xp = ref[:, LANES * p:LANES * (p + 1)].astype(F32)
        for e in range(2):
            t = xp if e == h else pltpu.roll(xp, HEAD_DIM, 1)
            pieces.append(jnp.where(keep, t, 0.0).astype(BF16))
    return jnp.concatenate(pieces, axis=0)


def _unstack_heads(stacked, h, lo):
    pairs = []
    for j in range(2):
        parts = []
        for e in range(2):
            t = stacked[BLOCK * (2 * j + e):BLOCK * (2 * j + e + 1)]
            parts.append(t if e == h else pltpu.roll(t, HEAD_DIM, 1))
        pairs.append(jnp.where(lo, parts[0], parts[1]))
    return pairs


def _sink_rows(sink_ref, h):
    head = lax.broadcasted_iota(jnp.int32, (GROWS, 1), 0) // BLOCK
    col = jnp.zeros((GROWS, 1), F32) + sink_ref[GROUP * h]
    for g in range(1, GROUP):
        col = jnp.where(head == g, sink_ref[GROUP * h + g], col)
    return col


def _group_probs(qs, kk, valid, sink):
    s = jnp.where(valid, _dot_nt(qs, kk), NEG_INF)
    m = jnp.maximum(jnp.max(s, axis=1, keepdims=True), sink)
    ex = jnp.exp(s - m)
    es = jnp.exp(sink - m)
    inv = 1.0 / (jnp.sum(ex, axis=1, keepdims=True) + es)
    return ex * inv, es * inv


def _attn_fwd(q, k, v, sinks, seq, exchanges=()):
    T = q.shape[0]
    nb = seq // BLOCK
    bl = T // seq

    def body(sink_ref, q_ref, kp_ref, kc_ref, vp_ref, vc_ref, o_ref):
        valid, lo = _attn_masks(pl.program_id(0))
        for b in range(bl):
            kk = jnp.concatenate([kp_ref[b], kc_ref[b]], axis=0)
            vv = jnp.concatenate([vp_ref[b], vc_ref[b]], axis=0)
            for h in range(2):
                qs = _stack_heads(q_ref.at[b], h, lo)
                pr, _ = _group_probs(qs, kk, valid, _sink_rows(sink_ref, h))
                o = _dot(pr.astype(BF16), vv)
                for j, pair in enumerate(_unstack_heads(o, h, lo)):
                    p = 2 * h + j
                    o_ref[b, :, LANES * p:LANES * (p + 1)] = pair.astype(BF16)

    cur = lambda n: (0, n, 0)
    prv = lambda n: (0, jnp.maximum(n - 1, 0), 0)
    kv = lambda m: pl.BlockSpec((bl, BLOCK, KV_WIDTH), m)
    res = _call(
        body, name="attn_fwd", grid=(nb,),
        in_specs=[pl.BlockSpec(memory_space=pltpu.SMEM), pl.BlockSpec((bl, BLOCK, ATTN_WIDTH), cur),
                  kv(prv), kv(cur), kv(prv), kv(cur)],
        out_specs=[pl.BlockSpec((bl, BLOCK, ATTN_WIDTH), cur)],
        out_shape=[_sds((bl, seq, ATTN_WIDTH), BF16)],
        args=(sinks, *_by_example(bl, q, k, k, v, v)), sem=("parallel",), exchanges=exchanges)
    if exchanges:
        return [res[0][0].reshape(T, ATTN_WIDTH)], res[1]
    return [res[0].reshape(T, ATTN_WIDTH)]


def _merge_out(y_pool, y_attn, gate, x2, w_bp, w_ba, w_out, g2, g3, exchanges=()):
    T = x2.shape[0]
    tm = min(TM, T)

    def body(yp_ref, ya_ref, gate_ref, x_ref, wbp_ref, wba_ref, wo_ref, g2_ref, g3_ref,
             bp_ref, ba_ref, mg_ref, mix_ref, x1_ref, h2_ref):
        yp, ya = yp_ref[...], ya_ref[...]
        bp = jnp.concatenate([_dot(yp, wbp_ref[j]) for j in range(N_CHIPS)], axis=1)
        ba = jnp.concatenate([_dot(ya, wba_ref[j]) for j in range(N_CHIPS)], axis=1)
        bp_ref[...] = bp.astype(BF16)
        ba_ref[...] = ba.astype(BF16)
        merged = (gate_ref[:, :D_MODEL].astype(F32) * bp + gate_ref[:, D_MODEL:].astype(F32) * ba).astype(BF16)
        mg_ref[...] = merged
        mix = _dot(merged, wo_ref[...])
        mix_ref[...] = mix
        x1 = x_ref[...] + mix * _rms(mix) * g2_ref[...]
        x1_ref[...] = x1
        h2_ref[...] = (x1 * _rms(x1) * g3_ref[...]).astype(BF16)

    return _call(
        body, name="merge_out", grid=(T // tm,),
        in_specs=[_rows(tm, POOL_WIDTH), _rows(tm, ATTN_WIDTH), _rows(tm, GATE_WIDTH), _rows(tm, D_MODEL),
                  _const(w_bp.shape), _const(w_ba.shape), _const((D_MODEL, D_MODEL)),
                  _const((1, D_MODEL)), _const((1, D_MODEL))],
        out_specs=[_rows(tm, D_MODEL)] * 6,
        out_shape=[_sds((T, D_MODEL), BF16), _sds((T, D_MODEL), BF16), _sds((T, D_MODEL), BF16),
                   _sds((T, D_MODEL), F32), _sds((T, D_MODEL), F32), _sds((T, D_MODEL), BF16)],
        args=(y_pool, y_attn, gate, x2, w_bp, w_ba, w_out, g2, g3), sem=("parallel",), exchanges=exchanges)


HALF = D_MODEL // 2
_HALVES = _const((N_CHIPS, HALF, D_MODEL))


def _mlp_up(h2, w_up, exchanges=()):
    T = h2.shape[0]
    tm = min(TM, T)

    def body(h_ref, wa_ref, wb_ref, up_ref, a_ref):
        ha, hb = h_ref[:, :HALF], h_ref[:, HALF:]
        for j in range(N_CHIPS):
            sl = slice(D_MODEL * j, D_MODEL * (j + 1))
            up = _dot(ha, wa_ref[j]) + _dot(hb, wb_ref[j])
            up_ref[:, sl] = up.astype(BF16)
            a_ref[:, sl] = jnp.square(jnp.maximum(up, 0.0)).astype(BF16)

    return _call(
        body, name="mlp_up", grid=(T // tm,),
        in_specs=[_rows(tm, D_MODEL), _HALVES, _HALVES],
        out_specs=[_rows(tm, D_FF), _rows(tm, D_FF)],
        out_shape=[_sds((T, D_FF), BF16), _sds((T, D_FF), BF16)],
        args=(h2, *w_up), sem=("parallel",), exchanges=exchanges)


def _mlp_down_loss(a, x1, tgt, w_down, g4):
    T = a.shape[0]
    tm = min(TM, T)

    def body(a_ref, x1_ref, t_ref, wa_ref, wb_ref, g_ref, dff_ref, dy_ref, loss_ref, dg_ref):
        @pl.when(pl.program_id(0) == 0)
        def _():
            loss_ref[...] = jnp.zeros_like(loss_ref)
            dg_ref[...] = jnp.zeros_like(dg_ref)

        ff = None
        for j in range(N_CHIPS):
            lo = D_MODEL * j
            t = _dot(a_ref[:, lo:lo + HALF], wa_ref[j]) + _dot(a_ref[:, lo + HALF:lo + D_MODEL], wb_ref[j])
            ff = t if ff is None else ff + t
        g = g_ref[...]
        err = x1_ref[...] + ff * _rms(ff) * g - t_ref[...]
        loss_ref[...] += jnp.sum(err * err) * (0.5 / D_MODEL)
        dy = err * (1.0 / D_MODEL)
        dy_ref[...] = dy
        dff, dg = _norm_bwd(ff, g, dy)
        dff_ref[...] = dff.astype(BF16)
        dg_ref[...] += dg

    return _call(
        body, name="mlp_down_loss", grid=(T // tm,),
        in_specs=[_rows(tm, D_FF), _rows(tm, D_MODEL), _rows(tm, D_MODEL), _HALVES, _HALVES, _const((1, D_MODEL))],
        out_specs=[_rows(tm, D_MODEL), _rows(tm, D_MODEL), _const((8, LANES)), _const((1, D_MODEL))],
        out_shape=[_sds((T, D_MODEL), BF16), _sds((T, D_MODEL), F32), _sds((8, LANES), F32),
                   _sds((1, D_MODEL), F32)],
        args=(a, x1, tgt, *w_down, g4), sem=("arbitrary",))


def _mlp_down_bwd(dff, up, w_down):
    T = dff.shape[0]
    tm = min(TM, T)

    def body(d_ref, up_ref, wa_ref, wb_ref, dup_ref):
        d = d_ref[...]
        for j in range(N_CHIPS):
            for part, w_ref in enumerate((wa_ref, wb_ref)):
                lo = D_MODEL * j + HALF * part
                da = _dot_nt(d, w_ref[j])
                relu = jnp.maximum(up_ref[:, lo:lo + HALF].astype(F32), 0.0)
                dup_ref[:, lo:lo + HALF] = (da * (2.0 * relu)).astype(BF16)

    return _call(
        body, name="mlp_down_bwd", grid=(T // tm,),
        in_specs=[_rows(tm, D_MODEL), _rows(tm, D_FF), _HALVES, _HALVES],
        out_specs=[_rows(tm, D_FF)],
        out_shape=[_sds((T, D_FF), BF16)],
        args=(dff, up, *w_down), sem=("parallel",))[0]


def _mlp_up_bwd(dup, dy, x1, mix, w_up, g3, g2, exchanges=()):
    T = dup.shape[0]
    tm = min(TM, T)

    def body(dup_ref, dy_ref, x1_ref, mix_ref, wa_ref, wb_ref, g3_ref, g2_ref, dx1_ref, dmix_ref, dg3_ref, dg2_ref):
        @pl.when(pl.program_id(0) == 0)
        def _():
            dg3_ref[...] = jnp.zeros_like(dg3_ref)
            dg2_ref[...] = jnp.zeros_like(dg2_ref)

        def back(w_ref):
            acc = _dot_nt(dup_ref[:, :D_MODEL], w_ref[0])
            for j in range(1, N_CHIPS):
                acc = acc + _dot_nt(dup_ref[:, D_MODEL * j:D_MODEL * (j + 1)], w_ref[j])
            return acc

        dh2 = jnp.concatenate([back(wa_ref), back(wb_ref)], axis=1)
        dx, dg3 = _norm_bwd(x1_ref[...], g3_ref[...], dh2)
        dx1 = dy_ref[...] + dx
        dx1_ref[...] = dx1
        dg3_ref[...] += dg3
        dmix, dg2 = _norm_bwd(mix_ref[...], g2_ref[...], dx1)
        dmix_ref[...] = dmix.astype(BF16)
        dg2_ref[...] += dg2

    return _call(
        body, name="mlp_up_bwd", grid=(T // tm,),
        in_specs=[_rows(tm, D_FF), _rows(tm, D_MODEL), _rows(tm, D_MODEL), _rows(tm, D_MODEL),
                  _HALVES, _HALVES, _const((1, D_MODEL)), _const((1, D_MODEL))],
        out_specs=[_rows(tm, D_MODEL), _rows(tm, D_MODEL), _const((1, D_MODEL)), _const((1, D_MODEL))],
        out_shape=[_sds((T, D_MODEL), F32), _sds((T, D_MODEL), BF16), _sds((1, D_MODEL), F32),
                   _sds((1, D_MODEL), F32)],
        args=(dup, dy, x1, mix, *w_up, g3, g2), sem=("arbitrary",), exchanges=exchanges)


def _dw(tag, a, g, ta, tn, shard_cols=False, exchanges=()):
    T, ka = a.shape
    n = g.shape[1]
    tk = min(TM, T)
    nk = T // tk

    def body(a_ref, g_ref, o_ref):
        @pl.when(pl.program_id(2) == 0)
        def _():
            o_ref[...] = jnp.zeros_like(o_ref)

        o_ref[...] += _dot_tn(a_ref[...], g_ref[...])

    if shard_cols:
        per = (n // N_CHIPS) // tn
        out_spec = pl.BlockSpec((None, ta, tn), lambda i, j, k: (j // per, i, j % per))
        out_shape = _sds((N_CHIPS, ka, n // N_CHIPS), F32)
    else:
        out_spec = pl.BlockSpec((ta, tn), lambda i, j, k: (i, j))
        out_shape = _sds((ka, n), F32)
    return _call(
        body, name="dw_" + tag, grid=(ka // ta, n // tn, nk),
        in_specs=[pl.BlockSpec((tk, ta), lambda i, j, k: (k, i)), pl.BlockSpec((tk, tn), lambda i, j, k: (k, j))],
        out_specs=[out_spec], out_shape=[out_shape],
        args=(a, g), sem=("parallel", "parallel", "arbitrary"), exchanges=exchanges)


def _dw_slabs(tag, a, g):
    T, ka = a.shape
    n = g.shape[1]
    c = n // N_CHIPS
    tk = min(TM, T)

    def body(a_ref, g_ref, o_ref):
        @pl.when(pl.program_id(0) == 0)
        def _():
            o_ref[...] = jnp.zeros_like(o_ref)

        res = _dot_tn(a_ref[...], g_ref[...])
        for j in range(N_CHIPS):
            o_ref[j] += res[:, c * j:c * (j + 1)]

    return _call(
        body, name="dw_" + tag, grid=(T // tk,),
        in_specs=[_rows(tk, ka), _rows(tk, n)],
        out_specs=[_const((N_CHIPS, ka, c))], out_shape=[_sds((N_CHIPS, ka, c), F32)],
        args=(a, g), sem=("arbitrary",))[0]


def _merge_bwd(dmix, gate, bp, ba, w_out, w_bp, w_ba, exchanges=()):
    T = dmix.shape[0]
    tm = min(TM, T)

    def body(dmix_ref, gate_ref, bp_ref, ba_ref, wo_ref, wbp_ref, wba_ref,
             dbp_ref, dba_ref, dgate_ref, dyp_ref, dya_ref):
        dm = _dot_nt(dmix_ref[...], wo_ref[...])
        for j, (b_ref, db_ref, w_ref, dy_ref) in enumerate(
                ((bp_ref, dbp_ref, wbp_ref, dyp_ref), (ba_ref, dba_ref, wba_ref, dya_ref))):
            sl = slice(D_MODEL * j, D_MODEL * (j + 1))
            gt = gate_ref[:, sl].astype(F32)
            db = (dm * gt).astype(BF16)
            db_ref[...] = db
            dgate_ref[:, sl] = (dm * b_ref[...].astype(F32) * gt * (1.0 - gt)).astype(BF16)
            cw = D_MODEL // N_CHIPS
            dy = _dot_nt(db[:, :cw], w_ref[0])
            for c in range(1, N_CHIPS):
                dy = dy + _dot_nt(db[:, cw * c:cw * (c + 1)], w_ref[c])
            dy_ref[...] = dy.astype(dy_ref.dtype)

    return _call(
        body, name="merge_bwd", grid=(T // tm,),
        in_specs=[_rows(tm, D_MODEL), _rows(tm, GATE_WIDTH), _rows(tm, D_MODEL), _rows(tm, D_MODEL),
                  _const((D_MODEL, D_MODEL)), _const(w_bp.shape), _const(w_ba.shape)],
        out_specs=[_rows(tm, D_MODEL), _rows(tm, D_MODEL), _rows(tm, GATE_WIDTH), _rows(tm, POOL_WIDTH),
                   _rows(tm, ATTN_WIDTH)],
        out_shape=[_sds((T, D_MODEL), BF16), _sds((T, D_MODEL), BF16), _sds((T, GATE_WIDTH), BF16),
                   _sds((T, POOL_WIDTH), F32), _sds((T, ATTN_WIDTH), BF16)],
        args=(dmix, gate, bp, ba, w_out, w_bp, w_ba), sem=("parallel",), exchanges=exchanges)


def _attn_bwd(q, k, v, do, sinks, tabs, seq, exchanges=()):
    T = q.shape[0]
    nb = seq // BLOCK
    bl = T // seq
    steps = nb + 1

    def body(sink_ref, q_ref, do_ref, kp_ref, kc_ref, vp_ref, vc_ref, c_ref, a_ref, bt_ref, cp_ref, ap_ref, btp_ref,
             dq_ref, dk_ref, dv_ref, dsink_ref, ck_ref, cv_ref):
        n = pl.program_id(0)

        @pl.when(n == 0)
        def _():
            dsink_ref[...] = jnp.zeros_like(dsink_ref)
            ck_ref[...] = jnp.zeros_like(ck_ref)
            cv_ref[...] = jnp.zeros_like(cv_ref)

        @pl.when(n < nb)
        def _():
            valid, lo = _attn_masks(n)
            for b in range(bl):
                kk = jnp.concatenate([kp_ref[b], kc_ref[b]], axis=0)
                vv = jnp.concatenate([vp_ref[b], vc_ref[b]], axis=0)
                dk_acc = jnp.zeros((2 * BLOCK, KV_WIDTH), F32)
                dv_acc = jnp.zeros((2 * BLOCK, KV_WIDTH), F32)
                for h in range(2):
                    qs = _stack_heads(q_ref.at[b], h, lo)
                    dos = _stack_heads(do_ref.at[b], h, lo)
                    pr, ps = _group_probs(qs, kk, valid, _sink_rows(sink_ref, h))
                    dp = _dot_nt(dos, vv)
                    delta = jnp.sum(pr * dp, axis=1, keepdims=True)
                    ds = (pr * (dp - delta)).astype(BF16)
                    dsk = ps * delta
                    for g in range(GROUP):
                        idx = GROUP * h + g
                        dsink_ref[idx:idx + 1, :] += (jnp.zeros((1, LANES), F32)
                                                      - jnp.sum(dsk[BLOCK * g:BLOCK * (g + 1)]))
                    dk_acc = dk_acc + _dot_tn(ds, qs)
                    dv_acc = dv_acc + _dot_tn(pr.astype(BF16), dos)
                    for j, pair in enumerate(_unstack_heads(_dot(ds, kk) * SCALE, h, lo)):
                        sl = slice(LANES * (2 * h + j), LANES * (2 * h + j + 1))
                        dq_ref[b, :, sl] = _rot_bwd(pair, c_ref[...], a_ref[...], bt_ref[...]).astype(BF16)
                fin_k = ck_ref[b] + dk_acc[:BLOCK]
                dk_ref[b] = _rot_bwd(fin_k, cp_ref[...], ap_ref[...], btp_ref[...]).astype(BF16)
                dv_ref[b] = (cv_ref[b] + dv_acc[:BLOCK]).astype(BF16)
                ck_ref[b] = dk_acc[BLOCK:]
                cv_ref[b] = dv_acc[BLOCK:]

        @pl.when(n == nb)
        def _():
            for b in range(bl):
                dk_ref[b] = _rot_bwd(ck_ref[b], cp_ref[...], ap_ref[...], btp_ref[...]).astype(BF16)
                dv_ref[b] = cv_ref[b].astype(BF16)

    cur = lambda n: (0, jnp.minimum(n, nb - 1), 0)
    prv = lambda n: (0, jnp.clip(n - 1, 0, nb - 1), 0)
    tcur = lambda n: (jnp.minimum(n, nb - 1), 0)
    tprv = lambda n: (jnp.clip(n - 1, 0, nb - 1), 0)
    wide = lambda m: pl.BlockSpec((bl, BLOCK, ATTN_WIDTH), m)
    kv = lambda m: pl.BlockSpec((bl, BLOCK, KV_WIDTH), m)
    tab = lambda m: pl.BlockSpec((BLOCK, LANES), m)
    res = _call(
        body, name="attn_bwd", grid=(steps,),
        in_specs=[pl.BlockSpec(memory_space=pltpu.SMEM), wide(cur), wide(cur), kv(prv), kv(cur), kv(prv), kv(cur),
                  tab(tcur), tab(tcur), tab(tcur), tab(tprv), tab(tprv), tab(tprv)],
        out_specs=[wide(cur), kv(prv), kv(prv), _const((8, LANES))],
        out_shape=[_sds((bl, seq, ATTN_WIDTH), BF16), _sds((bl, seq, KV_WIDTH), BF16),
                   _sds((bl, seq, KV_WIDTH), BF16), _sds((8, LANES), F32)],
        scratch=[pltpu.VMEM((bl, BLOCK, KV_WIDTH), F32), pltpu.VMEM((bl, BLOCK, KV_WIDTH), F32)],
        args=(sinks, *_by_example(bl, q, do, k, k, v, v), *tabs, *tabs), sem=("arbitrary",), exchanges=exchanges)
    outs, rest = (res if exchanges else (res, None))
    outs = [outs[0].reshape(T, ATTN_WIDTH), outs[1].reshape(T, KV_WIDTH), outs[2].reshape(T, KV_WIDTH), outs[3]]
    return (outs, rest) if exchanges else outs


def _pool_bwd(dyp, diff, w_pool, pool_scale, seq, exchanges=()):
    T = dyp.shape[0]
    tp = min(TP, seq)
    nseq = seq // tp
    per = tp // HALO
    last_halo = T // HALO - 1

    def body(dy_ref, nxt_ref, diff_ref, w_ref, s_ref, du_ref, dw_ref, ds_ref):
        i = pl.program_id(0)

        @pl.when(i == 0)
        def _():
            dw_ref[...] = jnp.zeros_like(dw_ref)
            ds_ref[...] = jnp.zeros_like(ds_ref)

        last = (i % nseq) == nseq - 1
        nxt = jnp.where(last, 0.0, nxt_ref[...])
        ext = jnp.concatenate([dy_ref[...], nxt], axis=0) * s_ref[...]
        pos = (i % nseq) * tp + lax.broadcasted_iota(jnp.int32, (tp + HALO, 1), 0)
        for gi, w in enumerate(POOL_WINDOWS):
            sl = slice(POOL_GC * gi, POOL_GC * (gi + 1))
            wg = w_ref[gi].astype(BF16)
            dmx = ext[:, sl].astype(BF16)
            ddiff = _dot_nt(dmx, wg)
            s = ddiff * _inv_count(pos, w)
            sh = 1
            while sh < w:
                s = s + pltpu.roll(s, tp + HALO - sh, 0)
                sh *= 2
            du_ref[:, sl] = (s[:tp] - ddiff[:tp]).astype(BF16)
            dg = diff_ref[:, sl]
            dw_ref[gi] += _dot_tn(dg, dmx[:tp])
            ds_ref[:, sl] += jnp.sum(dy_ref[:, sl] * _dot(dg, wg), axis=0, keepdims=True)

    return _call(
        body, name="pool_bwd", grid=(T // tp,),
        in_specs=[_rows(tp, POOL_WIDTH),
                  pl.BlockSpec((HALO, POOL_WIDTH), lambda i: (jnp.minimum((i + 1) * per, last_halo), 0)),
                  _rows(tp, POOL_WIDTH), _const((4, POOL_GC, POOL_GC)), _const((1, POOL_WIDTH))],
        out_specs=[_rows(tp, POOL_WIDTH), _const((4, POOL_GC, POOL_GC)), _const((1, POOL_WIDTH))],
        out_shape=[_sds((T, POOL_WIDTH), BF16), _sds((4, POOL_GC, POOL_GC), F32), _sds((1, POOL_WIDTH), F32)],
        args=(dyp, dyp, diff, w_pool, pool_scale), sem=("arbitrary",), exchanges=exchanges)


_PARTS = ((0, C_Q), (C_Q, C_K), (C_K, C_V), (C_V, C_G), (C_G, IN_WIDTH))


def _inproj_bwd(parts, x2, dx1, w_in_t, g1, exchanges=()):
    T = x2.shape[0]
    tm = min(TM, T)

    def body(du_ref, dq_ref, dk_ref, dv_ref, dgt_ref, x_ref, dx1_ref, w_ref, g_ref, gx_ref, dg_ref):
        @pl.when(pl.program_id(0) == 0)
        def _():
            dg_ref[...] = jnp.zeros_like(dg_ref)

        dh = jnp.zeros((tm, D_MODEL), F32)
        for (lo, hi), p_ref in zip(_PARTS, (du_ref, dq_ref, dk_ref, dv_ref, dgt_ref)):
            dh = dh + _dot(p_ref[...], w_ref[lo:hi, :])
        dx, dg = _norm_bwd(x_ref[...], g_ref[...], dh)
        gx_ref[...] = dx1_ref[...] + dx
        dg_ref[...] += dg

    return _call(
        body, name="inproj_bwd", grid=(T // tm,),
        in_specs=[_rows(tm, hi - lo) for lo, hi in _PARTS]
        + [_rows(tm, D_MODEL), _rows(tm, D_MODEL), _const((IN_WIDTH, D_MODEL)), _const((1, D_MODEL))],
        out_specs=[_rows(tm, D_MODEL), _const((1, D_MODEL))],
        out_shape=[_sds((T, D_MODEL), F32), _sds((1, D_MODEL), F32)],
        args=(*parts, x2, dx1, w_in_t, g1), sem=("arbitrary",), exchanges=exchanges)


def _dw_in(h, parts, exchanges=()):
    T = h.shape[0]
    tk = min(TM, T)

    def body(h_ref, du_ref, dq_ref, dk_ref, dv_ref, dgt_ref, o_ref, db_ref):
        @pl.when(pl.program_id(0) == 0)
        def _():
            o_ref[...] = jnp.zeros_like(o_ref)
            db_ref[...] = jnp.zeros_like(db_ref)

        hh = h_ref[...]
        for (lo, hi), p_ref in zip(_PARTS, (du_ref, dq_ref, dk_ref, dv_ref, dgt_ref)):
            part = p_ref[...]
            o_ref[lo:hi, :] += _dot_tn(part, hh)
            db_ref[:, lo:hi] += jnp.sum(part.astype(F32), axis=0, keepdims=True)

    return _call(
        body, name="dw_in", grid=(T // tk,),
        in_specs=[_rows(tk, D_MODEL)] + [_rows(tk, hi - lo) for lo, hi in _PARTS],
        out_specs=[_const((IN_WIDTH, D_MODEL)), _const((1, IN_WIDTH))],
        out_shape=[_sds((IN_WIDTH, D_MODEL), F32), _sds((1, IN_WIDTH), F32)],
        args=(h, *parts), sem=("arbitrary",), exchanges=exchanges)


def _row_tile(rows, cap=256, mult=16):
    best = None
    for t in range(mult, min(rows, cap) + 1, mult):
        if rows % t == 0:
            best = t
    if best is None:
        raise ValueError("no row tile for %d rows" % rows)
    return best


def _pair_sum(ids, full, got):
    _, r, c = full.shape
    hr = r // 2
    tr = _row_tile(hr)
    nblk = hr // tr

    def body(ids_ref, a_ref, b_ref, s_ref, sb_ref):
        s = a_ref[...] + b_ref[...]
        s_ref[...] = s
        sb_ref[...] = s.astype(BF16)

    out_spec = pl.BlockSpec((None, tr, c), lambda j, i, ids_ref: (j, i, 0))
    return pl.pallas_call(
        body, name="pair_sum_%dx%d" % (r, c),
        grid_spec=pltpu.PrefetchScalarGridSpec(
            num_scalar_prefetch=1, grid=(N_CHIPS, nblk),
            in_specs=[pl.BlockSpec((None, tr, c), lambda j, i, ids_ref: (j, ids_ref[1] * nblk + i, 0)), out_spec],
            out_specs=[out_spec, out_spec]),
        out_shape=[_sds((N_CHIPS, hr, c), F32), _sds((N_CHIPS, hr, c), BF16)],
        compiler_params=_cp("parallel", "parallel"),
    )(ids, full, got)


def _chip_sum(ids, own, got):
    _, hr, c = own.shape
    tr = _row_tile(hr)
    nblk = hr // tr

    def body(ids_ref, a_ref, b_ref, o_ref):
        o_ref[...] = ((a_ref[...] + b_ref[0].astype(F32)) + b_ref[1].astype(F32)) + b_ref[2].astype(F32)

    return pl.pallas_call(
        body, name="chip_sum_%dx%d" % (hr, c),
        grid_spec=pltpu.PrefetchScalarGridSpec(
            num_scalar_prefetch=1, grid=(nblk,),
            in_specs=[pl.BlockSpec((None, tr, c), lambda i, ids_ref: (ids_ref[0], i, 0)),
                      pl.BlockSpec((3, tr, c), lambda i, ids_ref: (0, i, 0))],
            out_specs=pl.BlockSpec((tr, c), lambda i, ids_ref: (ids_ref[1] * nblk + i, 0))),
        out_shape=_sds((2 * hr, c), F32),
        compiler_params=_cp("parallel"),
    )(ids, own, got)


def _adamw_math(w, g, m, v):
    nm = ADAM_B1 * m + (1.0 - ADAM_B1) * g
    nv = ADAM_B2 * v + (1.0 - ADAM_B2) * jnp.square(g)
    m_hat = nm / (1.0 - ADAM_B1 ** ADAM_STEP)
    v_hat = nv / (1.0 - ADAM_B2 ** ADAM_STEP)
    return -ADAM_LR * (m_hat / (jnp.sqrt(v_hat) + ADAM_EPS) + ADAM_WD * w), nm, nv


def _adamw(w, g, m, v):
    r, c = w.shape
    tr = _row_tile(r, cap=512, mult=8)

    def body(w_ref, g_ref, m_ref, v_ref, d_ref, nm_ref, nv_ref):
        d_ref[...], nm_ref[...], nv_ref[...] = _adamw_math(w_ref[...], g_ref[...], m_ref[...], v_ref[...])

    spec = _rows(tr, c)
    return pl.pallas_call(
        body, name="adamw_%dx%d" % (r, c), grid=(r // tr,),
        in_specs=[spec] * 4, out_specs=[spec] * 3, out_shape=[_sds((r, c), F32)] * 3,
        compiler_params=_cp("parallel"),
    )(w, g, m, v)


_SMALL_NAMES = ("w_pool", "b_in", "g_mix_pre", "g_mix_post", "g_mlp_pre", "g_mlp_post", "pool_scale", "attn_sinks")
B_ROWS = -(-IN_WIDTH // D_MODEL)


def _row_block(rows):
    rows = [jnp.pad(r.astype(F32), ((0, 0), (0, D_MODEL - r.shape[1]))) for r in rows]
    return jnp.pad(jnp.concatenate(rows, axis=0), ((0, 8 - len(rows)), (0, 0)))


def _early_block(dg2, dg3, dg4, dps, dsink, loss):
    tail = jnp.concatenate([jnp.pad(dsink.reshape(1, -1), ((0, 0), (0, LANES - dsink.size))),
                            jnp.pad(loss.reshape(1, 1), ((0, 0), (0, LANES - 1)))], axis=1)
    return _row_block([dg2, dg3, dg4, dps, tail])


def _late_block(db_in, dg1):
    b = jnp.pad(db_in, ((0, 0), (0, B_ROWS * D_MODEL - IN_WIDTH))).reshape(B_ROWS, D_MODEL)
    return _row_block([b[r:r + 1] for r in range(B_ROWS)] + [dg1])


def _small_update(gearly, gmat, glate, w, m, v):
    names = _SMALL_NAMES
    n = len(names)

    def total(ref, rows):
        acc = ref[0:rows, :]
        for d in range(1, N_DEV):
            acc = acc + ref[d * rows:(d + 1) * rows, :]
        return acc

    def body(*refs):
        early_ref, gmat_ref, late_ref = refs[:3]
        w_refs, m_refs, v_refs = refs[3:3 + n], refs[3 + n:3 + 2 * n], refs[3 + 2 * n:3 + 3 * n]
        outs = refs[3 + 3 * n:]
        loss_ref, g_refs, d_refs = outs[0], outs[1:1 + n], outs[1 + n:1 + 2 * n]
        nm_refs, nv_refs = outs[1 + 2 * n:1 + 3 * n], outs[1 + 3 * n:1 + 4 * n]
        early, late = total(early_ref, 8), total(late_ref, 8)
        loss_ref[...] = jnp.sum(early[4:5, LANES:2 * LANES], axis=1, keepdims=True)
        bias = jnp.concatenate([late[r:r + 1, :] for r in range(B_ROWS - 1)]
                               + [late[B_ROWS - 1:B_ROWS, :IN_WIDTH - (B_ROWS - 1) * D_MODEL]], axis=1)
        grad = dict(b_in=bias, g_mix_pre=late[B_ROWS:B_ROWS + 1, :], g_mix_post=early[0:1, :],
                    g_mlp_pre=early[1:2, :], g_mlp_post=early[2:3, :], pool_scale=early[3:4, :POOL_WIDTH],
                    attn_sinks=early[4:5, :N_Q_HEADS])
        for i, name in enumerate(names):
            g = total(gmat_ref, 4 * POOL_GC) if name == "w_pool" else grad[name]
            g_refs[i][...] = g
            d_refs[i][...], nm_refs[i][...], nv_refs[i][...] = _adamw_math(
                w_refs[i][...], g, m_refs[i][...], v_refs[i][...])

    shapes = [_sds(w[k].shape, F32) for k in names]
    res = pl.pallas_call(
        body, name="small_update", out_shape=[_sds((1, 1), F32)] + shapes * 4,
        compiler_params=pltpu.CompilerParams(vmem_limit_bytes=VMEM_MB * 1024 * 1024),
    )(gearly, gmat, glate, *[w[k] for k in names], *[m[k] for k in names], *[v[k] for k in names])
    loss = res[0]
    per = {k: tuple(res[1 + j * n + i] for j in range(4)) for i, k in enumerate(names)}
    return loss, per


_BIG = ("w_in", "w_branch_pool", "w_branch_attn", "w_out", "w_up", "w_down")
_ORDER = ("g_mix_pre", "w_in", "b_in", "w_pool", "pool_scale", "attn_sinks", "w_branch_pool", "w_branch_attn",
          "w_out", "g_mix_post", "g_mlp_pre", "w_up", "w_down", "g_mlp_post")


def _stack_rows(slab):
    return slab.reshape(-1, slab.shape[2])


def _step(x2, tgt, seq, shards, small, ids):
    tabs = _rope_tables(seq)
    g1, g2, g3, g4 = (small[n] for n in ("g_mix_pre", "g_mix_post", "g_mlp_pre", "g_mlp_post"))
    sinks = small["attn_sinks"].reshape(N_Q_HEADS)
    w_pool = small["w_pool"].reshape(4, POOL_GC, POOL_GC)
    pool_scale = small["pool_scale"]

    def whole(shard, slabs):
        return lax.dynamic_update_slice(slabs, shard[None], (ids[0], 0, 0))

    up_a, up_b = shards["w_up"][:HALF], shards["w_up"][HALF:]
    down_a, down_b = shards["w_down"][:HALF], shards["w_down"][HALF:]
    w_in = _stack_rows(whole(shards["w_in"], _alone("gather_in", _ex_gather([shards["w_in"]]))[0][0]))
    mix_shards = [shards[n] for n in ("w_branch_pool", "w_branch_attn", "w_out")]
    (h, u, q, k, v, gate), [(*mix_slabs, got_a)] = _inproj(
        x2, g1, w_in, small["b_in"], tabs, seq, exchanges=[_ex_gather(mix_shards + [up_a])])
    w_bp, w_ba, out_slab = (whole(s, g) for s, g in zip(mix_shards, mix_slabs))
    w_out = _stack_rows(out_slab)
    diff, y_pool = _pool_fwd(u, w_pool, pool_scale, seq)
    (y_attn,), [[got_b]] = _attn_fwd(q, k, v, sinks, seq, exchanges=[_ex_gather([up_b])])
    (bp, ba, merged, mix, x1, h2), [[got_c]] = _merge_out(
        y_pool, y_attn, gate, x2, w_bp, w_ba, w_out, g2, g3, exchanges=[_ex_gather([down_a])])
    w_up = (whole(up_a, got_a), whole(up_b, got_b))
    (up, act), [[got_d]] = _mlp_up(h2, w_up, exchanges=[_ex_gather([down_b])])
    w_down = (whole(down_a, got_c), whole(down_b, got_d))
    dff, dy, loss_acc, dg4 = _mlp_down_loss(act, x1, tgt, w_down, g4)

    dup = _mlp_down_bwd(dff, up, w_down)
    dw_down = _dw("down", act, dff, 1024, 1024)[0].reshape(N_CHIPS, D_FF // N_CHIPS, D_MODEL)
    (dx1, dmix, dg3, dg2), [[got]] = _mlp_up_bwd(dup, dy, x1, mix, w_up, g3, g2, exchanges=[_ex_pair([dw_down])])
    ps_down = _pair_sum(ids, dw_down, got)
    (dw_up,), [[got]] = _dw("up", h2, dup, 1024, 1024, shard_cols=True, exchanges=[_ex_chip([ps_down[1]])])
    half_down = _chip_sum(ids, ps_down[0], got)
    (dbp, dba, dgate, dyp, dya), [[got], [g_down]] = _merge_bwd(
        dmix, gate, bp, ba, w_out, w_bp, w_ba, exchanges=[_ex_pair([dw_up]), _ex_swap([half_down])])
    ps_up = _pair_sum(ids, dw_up, got)
    dw_mix = [_dw("out", merged, dmix, 1024, 1024)[0].reshape(N_CHIPS, D_MODEL // N_CHIPS, D_MODEL),
              _dw_slabs("branch_pool", y_pool, dbp), _dw_slabs("branch_attn", y_attn, dba)]
    (dq, dk, dv, dsink), [[got], gots] = _attn_bwd(
        q, k, v, dya, sinks, tabs, seq, exchanges=[_ex_chip([ps_up[1]]), _ex_pair(dw_mix)])
    half_up = _chip_sum(ids, ps_up[0], got)
    ps_mix = [_pair_sum(ids, d, g) for d, g in zip(dw_mix, gots)]
    (du, dw_pool, dps), [[g_up]] = _pool_bwd(dyp, diff, w_pool, pool_scale, seq, exchanges=[_ex_swap([half_up])])
    parts = (du, dq, dk, dv, dgate)
    early = _early_block(dg2, dg3, dg4, dps, dsink[:, 0], loss_acc[0, 0])
    mat = dw_pool.reshape(4 * POOL_GC, POOL_GC)
    (dw_in_t, db_in), [gots, [gearly, gmat]] = _dw_in(
        h, parts, exchanges=[_ex_chip([p[1] for p in ps_mix]), _ex_allgather([early, mat])])
    half_mix = [_chip_sum(ids, p[0], g) for p, g in zip(ps_mix, gots)]
    dw_in = dw_in_t.reshape(N_CHIPS, IN_WIDTH // N_CHIPS, D_MODEL)
    g_mix, [got] = _alone("swap_mix_pair_in", _ex_swap(half_mix), _ex_pair([dw_in]))
    ps_in = _pair_sum(ids, dw_in, got)
    (gx, dg1), [[got]] = _inproj_bwd(parts, x2, dx1, w_in, g1, exchanges=[_ex_chip([ps_in[1]])])
    [g_in], [glate] = _alone("swap_in_allgather", _ex_swap([_chip_sum(ids, ps_in[0], got)]),
                             _ex_allgather([_late_block(db_in, dg1)]))

    grads = dict(w_in=g_in, w_branch_pool=g_mix[1], w_branch_attn=g_mix[2], w_out=g_mix[0], w_up=g_up, w_down=g_down)
    return (gearly, gmat, glate), gx, grads


def kernel(x, g_mix_pre, w_in, b_in, w_pool, pool_scale, attn_sinks, w_branch_pool, w_branch_attn, w_out, g_mix_post, g_mlp_pre, w_up, w_down, g_mlp_post, loss_target, m_g_mix_pre, m_w_in, m_b_in, m_w_pool, m_pool_scale, m_attn_sinks, m_w_branch_pool, m_w_branch_attn, m_w_out, m_g_mix_post, m_g_mlp_pre, m_w_up, m_w_down, m_g_mlp_post, v_g_mix_pre, v_w_in, v_b_in, v_w_pool, v_pool_scale, v_attn_sinks, v_w_branch_pool, v_w_branch_attn, v_w_out, v_g_mix_post, v_g_mlp_pre, v_w_up, v_w_down, v_g_mlp_post):
    weights = dict(g_mix_pre=g_mix_pre, w_in=w_in, b_in=b_in, w_pool=w_pool, pool_scale=pool_scale,
                   attn_sinks=attn_sinks, w_branch_pool=w_branch_pool, w_branch_attn=w_branch_attn, w_out=w_out,
                   g_mix_post=g_mix_post, g_mlp_pre=g_mlp_pre, w_up=w_up, w_down=w_down, g_mlp_post=g_mlp_post)
    mom1 = dict(g_mix_pre=m_g_mix_pre, w_in=m_w_in, b_in=m_b_in, w_pool=m_w_pool, pool_scale=m_pool_scale,
                attn_sinks=m_attn_sinks, w_branch_pool=m_w_branch_pool, w_branch_attn=m_w_branch_attn,
                w_out=m_w_out, g_mix_post=m_g_mix_post, g_mlp_pre=m_g_mlp_pre, w_up=m_w_up, w_down=m_w_down,
                g_mlp_post=m_g_mlp_post)
    mom2 = dict(g_mix_pre=v_g_mix_pre, w_in=v_w_in, b_in=v_b_in, w_pool=v_w_pool, pool_scale=v_pool_scale,
                attn_sinks=v_attn_sinks, w_branch_pool=v_w_branch_pool, w_branch_attn=v_w_branch_attn,
                w_out=v_w_out, g_mix_post=v_g_mix_post, g_mlp_pre=v_g_mlp_pre, w_up=v_w_up, w_down=v_w_down,
                g_mlp_post=v_g_mlp_post)
    b_loc, seq, _ = x.shape
    x2 = x.reshape(b_loc * seq, D_MODEL)
    tgt = loss_target.reshape(b_loc * seq, D_MODEL)
    ids = jnp.stack([2 * lax.axis_index("x") + lax.axis_index("y"), lax.axis_index("c")]).astype(jnp.int32)

    def flat(n, a):
        return a[0].T if n == "w_in" else a[0]

    def unflat(n, a):
        return (a.T if n == "w_in" else a)[None]

    shards = {n: flat(n, weights[n]).astype(BF16) for n in _BIG}
    small = {n: weights[n] for n in _ORDER if n not in _BIG}
    (gearly, gmat, glate), gx, grads = _step(x2, tgt, seq, shards, small, ids)

    def two_d(src):
        return {n: src[n].reshape(4 * POOL_GC, POOL_GC) if n == "w_pool" else src[n] for n in _SMALL_NAMES}

    loss, per = _small_update(gearly, gmat, glate, two_d(weights), two_d(mom1), two_d(mom2))
    delta, new_m, new_v = {}, {}, {}
    for n in _SMALL_NAMES:
        grads[n], delta[n], new_m[n], new_v[n] = (a.reshape(weights[n].shape) for a in per[n])
    for n in _BIG:
        d, nm, nv = _adamw(flat(n, weights[n]), grads[n], flat(n, mom1[n]), flat(n, mom2[n]))
        grads[n] = unflat(n, grads[n])
        delta[n], new_m[n], new_v[n] = unflat(n, d), unflat(n, nm), unflat(n, nv)

    return (loss[0, 0], gx.reshape(x.shape), *[grads[n] for n in _ORDER], *[delta[n] for n in _ORDER],
            *[new_m[n] for n in _ORDER], *[new_v[n] for n in _ORDER])
```

```python
import jax
import jax.numpy as jnp
from jax import lax
from jax.experimental import pallas as pl
from jax.experimental.pallas import tpu as pltpu

F32 = jnp.float32
BF16 = jnp.bfloat16

D_MODEL = 1024
POOL_WINDOWS = (2, 4, 8, 16)
POOL_WIDTH = 512
POOL_GC = 128
HALO = 16
HEAD_DIM = 64
N_Q_HEADS = 8
ATTN_WIDTH = 512
KV_WIDTH = 128
BLOCK = 128
NEG_INF = -1e30
ROPE_THETA = 500000.0
ROT_DIM = 16
GATE_WIDTH = 2048
IN_WIDTH = 3328
D_FF = 4096
EPS = 1e-6
SCALE = HEAD_DIM ** -0.5
C_Q, C_K, C_V, C_G = 512, 1024, 1152, 1280

ADAM_LR, ADAM_B1, ADAM_B2, ADAM_EPS, ADAM_WD, ADAM_STEP = 0.001, 0.9, 0.999, 1e-08, 0.01, 10

N_CHIPS = 4
N_DEV = 8
LANES = 128
TM = 512
TP = 512
VMEM_MB = 56

MESH = pl.DeviceIdType.MESH
ANY = pl.BlockSpec(memory_space=pl.ANY)


def _cp(*sem, vmem=VMEM_MB):
    return pltpu.CompilerParams(dimension_semantics=sem, vmem_limit_bytes=vmem * 1024 * 1024)


def _rows(tile, cols):
    return pl.BlockSpec((tile, cols), lambda i: (i, 0))


def _const(shape):
    nd = len(shape)
    return pl.BlockSpec(shape, lambda i: (0,) * nd)


def _sds(shape, dtype):
    return jax.ShapeDtypeStruct(shape, dtype)


def _dot(a, b):
    return jnp.dot(a, b, preferred_element_type=F32)


def _dot_nt(a, b):
    return lax.dot_general(a, b, (((1,), (1,)), ((), ())), preferred_element_type=F32)


def _dot_tn(a, b):
    return lax.dot_general(a, b, (((0,), (0,)), ((), ())), preferred_element_type=F32)


def _rms(x):
    return lax.rsqrt(jnp.mean(x * x, axis=-1, keepdims=True) + EPS)


def _norm_bwd(x, g, dout):
    r = _rms(x)
    n = x * r
    dn = dout * g
    dx = r * (dn - n * jnp.mean(dn * n, axis=-1, keepdims=True))
    return dx, jnp.sum(dout * n, axis=0, keepdims=True)


def _rot_fwd(t, c, a, bt):
    return t * c + pltpu.roll(t, LANES - 8, 1) * a + pltpu.roll(t, 8, 1) * bt


def _rot_bwd(d, c, a, bt):
    return d * c + pltpu.roll(d * a, 8, 1) + pltpu.roll(d * bt, LANES - 8, 1)


def _rope_tables(seq):
    pos = jnp.arange(seq, dtype=F32)
    inv_freq = ROPE_THETA ** (-jnp.arange(0, ROT_DIM, 2, dtype=F32) / ROT_DIM)
    ang = pos[:, None] * inv_freq[None, :]
    cos, sin = jnp.cos(ang), jnp.sin(ang)
    ones = jnp.ones((seq, HEAD_DIM - ROT_DIM), F32)
    zeros8 = jnp.zeros((seq, 8), F32)
    zrest = jnp.zeros((seq, HEAD_DIM - ROT_DIM), F32)
    c = jnp.concatenate([cos, cos, ones], axis=1)
    a = jnp.concatenate([-sin, zeros8, zrest], axis=1)
    bt = jnp.concatenate([zeros8, sin, zrest], axis=1)
    return tuple(jnp.tile(t, (1, 2)) for t in (c, a, bt))


class _Exchange:
    def __init__(self, inputs, out_shapes, sems, start, finish, aliases=None, middle=None):
        self.inputs, self.out_shapes, self.sems = list(inputs), list(out_shapes), list(sems)
        self.start, self.finish, self.aliases = start, finish, dict(aliases or {})
        self.middle = middle


def _call(body, *, name, grid, in_specs, out_specs, out_shape, args, scratch=(), sem=(), exchanges=()):
    in_specs, out_specs, out_shape, scratch = list(in_specs), list(out_specs), list(out_shape), list(scratch)
    if not exchanges:
        return pl.pallas_call(body, name=name, grid=grid, in_specs=in_specs, out_specs=out_specs,
                              out_shape=out_shape, scratch_shapes=scratch, compiler_params=_cp(*sem))(*args)
    n_in, n_out, n_scr = len(in_specs), len(out_specs), len(scratch)
    x_in = [a for ex in exchanges for a in ex.inputs]
    x_out = [s for ex in exchanges for s in ex.out_shapes]
    x_sem = [s for ex in exchanges for s in ex.sems]
    aliases, i_off, o_off = {}, n_in, n_out
    for ex in exchanges:
        for i, o in ex.aliases.items():
            aliases[i_off + i] = o_off + o
        i_off += len(ex.inputs)
        o_off += len(ex.out_shapes)

    def split(flat):
        out, pos = [], 0
        for ex, n in zip(exchanges, flat[1]):
            out.append(flat[0][pos:pos + n])
            pos += n
        return out

    def carrier(*refs):
        pos = 0
        groups = []
        for n in (n_in, len(x_in), n_out, len(x_out), n_scr, len(x_sem)):
            groups.append(refs[pos:pos + n])
            pos += n
        ins, xin, outs, xout, scr, xsem = groups
        xin = split((xin, [len(ex.inputs) for ex in exchanges]))
        xout = split((xout, [len(ex.out_shapes) for ex in exchanges]))
        xsem = split((xsem, [len(ex.sems) for ex in exchanges]))
        first = pl.program_id(0) == 0
        last = pl.program_id(0) == grid[0] - 1
        for d in range(1, len(grid)):
            first = jnp.logical_and(first, pl.program_id(d) == 0)
            last = jnp.logical_and(last, pl.program_id(d) == grid[d] - 1)

        @pl.when(first)
        def _():
            for ex, i, o, s in zip(exchanges, xin, xout, xsem):
                ex.start(i, o, s)

        if any(ex.middle for ex in exchanges):
            half = pl.program_id(0) == grid[0] // 2
            for d in range(1, len(grid)):
                half = jnp.logical_and(half, pl.program_id(d) == 0)

            @pl.when(half)
            def _():
                for ex, i, o, s in zip(exchanges, xin, xout, xsem):
                    if ex.middle:
                        ex.middle(i, o, s)

        body(*ins, *outs, *scr)

        @pl.when(last)
        def _():
            for ex, i, o, s in zip(exchanges, xin, xout, xsem):
                ex.finish(i, o, s)

    res = pl.pallas_call(
        carrier, name=name, grid=grid, in_specs=in_specs + [ANY] * len(x_in),
        out_specs=out_specs + [ANY] * len(x_out), out_shape=out_shape + x_out,
        scratch_shapes=scratch + x_sem, input_output_aliases=aliases,
        compiler_params=_cp(*(["arbitrary"] * len(grid))),
    )(*args, *x_in)
    return res[:n_out], split((res[n_out:], [len(ex.out_shapes) for ex in exchanges]))


def _alone(name, *exchanges):
    n_in = [len(ex.inputs) for ex in exchanges]
    n_out = [len(ex.out_shapes) for ex in exchanges]
    n_sem = [len(ex.sems) for ex in exchanges]
    aliases, i_off, o_off = {}, 0, 0
    for ex in exchanges:
        for i, o in ex.aliases.items():
            aliases[i_off + i] = o_off + o
        i_off += len(ex.inputs)
        o_off += len(ex.out_shapes)

    def split(flat, counts):
        out, pos = [], 0
        for n in counts:
            out.append(flat[pos:pos + n])
            pos += n
        return out

    def body(*refs):
        ins, outs, sems = split(refs, [sum(n_in), sum(n_out), sum(n_sem)])
        groups = list(zip(exchanges, split(ins, n_in), split(outs, n_out), split(sems, n_sem)))
        for ex, i, o, s in groups:
            ex.start(i, o, s)
        for ex, i, o, s in groups:
            if ex.middle:
                ex.middle(i, o, s)
        for ex, i, o, s in groups:
            ex.finish(i, o, s)

    res = pl.pallas_call(
        body, name=name, in_specs=[ANY] * sum(n_in), out_specs=[ANY] * sum(n_out),
        out_shape=[s for ex in exchanges for s in ex.out_shapes],
        scratch_shapes=[s for ex in exchanges for s in ex.sems], input_output_aliases=aliases,
    )(*[a for ex in exchanges for a in ex.inputs])
    return split(res, n_out)


def _place():
    x, y, c = lax.axis_index("x"), lax.axis_index("y"), lax.axis_index("c")
    chips = [(1 - x, y), (x, 1 - y), (1 - x, 1 - y)]
    return x, y, c, chips


def _remote(src, dst, send, recv, to):
    return pltpu.make_async_remote_copy(src_ref=src, dst_ref=dst, send_sem=send, recv_sem=recv,
                                        device_id=to, device_id_type=MESH)


def _ex_gather(shards):
    nw = len(shards)
    hrs = [s.shape[0] // 2 for s in shards]

    def copies(ins, outs, sems):
        s1, r1, s2, r2, fs, fr = sems
        x, y, c, _ = _place()
        me, xn, yn, dg = (x, y), (1 - x, y), (x, 1 - y), (1 - x, 1 - y)
        nbr = (xn, yn)
        sibling = (x, y, 1 - c)

        def piece(w, chip, core, part=None):
            hr = hrs[w]
            rows = pl.ds(core * hr, hr) if part is None else pl.ds(core * hr + part * (hr // 2), hr // 2)
            return outs[w].at[2 * chip[0] + chip[1], rows]

        def first(w, k):
            return _remote(ins[w].at[pl.ds(c * hrs[w], hrs[w])], piece(w, me, c), s1.at[w, k], r1.at[w, k],
                           (*nbr[k], c))

        def landed(w, k):
            return _remote(piece(w, nbr[k], c), piece(w, nbr[k], c), s1.at[w, k], r1.at[w, k], (*nbr[k], c))

        def onward(w, k):
            return _remote(piece(w, nbr[k], c, k), piece(w, nbr[k], c, k), s2.at[w, k], r2.at[w, k],
                           (*nbr[1 - k], c))

        def arrived(w, k):
            return _remote(piece(w, dg, c, k), piece(w, dg, c, k), s2.at[w, k], r2.at[w, k], (*nbr[1 - k], c))

        def passed(w, j):
            chip = (xn, yn, dg)[j]
            return _remote(piece(w, chip, c), piece(w, chip, c), fs.at[w, j], fr.at[w, j], sibling)

        def handed(w, j):
            chip = (xn, yn, dg)[j]
            return _remote(piece(w, chip, 1 - c), piece(w, chip, 1 - c), fs.at[w, j], fr.at[w, j], sibling)

        return first, landed, onward, arrived, passed, handed

    def start(ins, outs, sems):
        first = copies(ins, outs, sems)[0]
        for w in range(nw):
            for k in range(2):
                first(w, k).start()

    def middle(ins, outs, sems):
        _, landed, onward, _, passed, _ = copies(ins, outs, sems)
        for w in range(nw):
            for k in range(2):
                landed(w, k).wait_recv()
                onward(w, k).start()
                passed(w, k).start()

    def finish(ins, outs, sems):
        first, _, onward, arrived, passed, handed = copies(ins, outs, sems)
        for w in range(nw):
            for k in range(2):
                arrived(w, k).wait_recv()
            passed(w, 2).start()
        for w in range(nw):
            for j in range(3):
                handed(w, j).wait_recv()
        for w in range(nw):
            for k in range(2):
                first(w, k).wait_send()
                onward(w, k).wait_send()
            for j in range(3):
                passed(w, j).wait_send()

    return _Exchange(shards, [_sds((N_CHIPS,) + s.shape, s.dtype) for s in shards],
                     [pltpu.SemaphoreType.DMA((nw, 2))] * 4 + [pltpu.SemaphoreType.DMA((nw, 3))] * 2,
                     start, finish, middle=middle)


def _ex_pair(grads):
    nw = len(grads)

    def copies(ins, outs, sems):
        x, y, c, _ = _place()
        out = []
        for w in range(nw):
            hr = grads[w].shape[1] // 2
            out.append(_remote(ins[w].at[:, pl.ds((1 - c) * hr, hr)], outs[w], sems[0].at[w], sems[1].at[w],
                               (x, y, 1 - c)))
        return out

    def start(ins, outs, sems):
        for cp in copies(ins, outs, sems):
            cp.start()

    def finish(ins, outs, sems):
        for cp in copies(ins, outs, sems):
            cp.wait()

    return _Exchange(grads, [_sds((N_CHIPS, g.shape[1] // 2, g.shape[2]), F32) for g in grads],
                     [pltpu.SemaphoreType.DMA((nw,))] * 2, start, finish)


def _ex_chip(pieces):
    nw = len(pieces)

    def copies(ins, outs, sems):
        x, y, c, chips = _place()
        return [_remote(ins[w].at[2 * cx + cy], outs[w].at[k], sems[0].at[w, k], sems[1].at[w, k], (cx, cy, c))
                for w in range(nw) for k, (cx, cy) in enumerate(chips)]

    def start(ins, outs, sems):
        for cp in copies(ins, outs, sems):
            cp.start()

    def finish(ins, outs, sems):
        for cp in copies(ins, outs, sems):
            cp.wait()

    return _Exchange(pieces, [_sds((3,) + p.shape[1:], BF16) for p in pieces],
                     [pltpu.SemaphoreType.DMA((nw, 3))] * 2, start, finish)


def _ex_swap(fulls):
    nw = len(fulls)

    def start(ins, outs, sems):
        x, y, c, _ = _place()
        for w in range(nw):
            hr = fulls[w].shape[0] // 2
            mine = pl.ds(c * hr, hr)
            _remote(ins[w].at[mine], outs[w].at[mine], sems[0].at[w], sems[1].at[w], (x, y, 1 - c)).start()

    def finish(ins, outs, sems):
        x, y, c, _ = _place()
        for w in range(nw):
            hr = fulls[w].shape[0] // 2
            mine, theirs = pl.ds(c * hr, hr), pl.ds((1 - c) * hr, hr)
            _remote(ins[w].at[mine], outs[w].at[mine], sems[0].at[w], sems[1].at[w], (x, y, 1 - c)).wait_send()
            _remote(ins[w].at[theirs], outs[w].at[theirs], sems[0].at[w], sems[1].at[w], (x, y, 1 - c)).wait_recv()

    return _Exchange(fulls, [_sds(f.shape, F32) for f in fulls], [pltpu.SemaphoreType.DMA((nw,))] * 2,
                     start, finish, aliases={w: w for w in range(nw)})


def _ex_allgather(blocks):
    nb = len(blocks)

    def copies(ins, outs, sems):
        send, recv, lsem = sems
        x, y, c, chips = _place()
        me, sibling = (x, y, c), (x, y, 1 - c)

        def rows(b, px, py, pc):
            m_per = blocks[b].shape[0]
            return outs[b].at[pl.ds((4 * px + 2 * py + pc) * m_per, m_per), :]

        def copy(b, k, blk, to, src=None):
            return _remote(rows(b, *blk) if src is None else src, rows(b, *blk), send.at[b, k], recv.at[b, k], to)

        def mine(b):
            return pltpu.make_async_copy(ins[b], rows(b, *me), lsem.at[b])

        def first(b, k):
            return copy(b, k, me, sibling if k == 0 else (*chips[k - 1], c), src=ins[b])

        def passed(b, j):
            return copy(b, 4 + j, (*chips[j], c), sibling)

        def landed(b, j):
            return copy(b, 1 + j, (*chips[j], c), me)

        def handed(b, k):
            return copy(b, 0, sibling, me) if k == 0 else copy(b, 3 + k, (*chips[k - 1], 1 - c), me)

        return mine, first, passed, landed, handed

    def start(ins, outs, sems):
        mine, first, _, _, _ = copies(ins, outs, sems)
        for b in range(nb):
            mine(b).start()
            for k in range(4):
                first(b, k).start()

    def finish(ins, outs, sems):
        mine, first, passed, landed, handed = copies(ins, outs, sems)
        sent = []
        for b in range(nb):
            for j in range(3):
                landed(b, j).wait_recv()
                cp = passed(b, j)
                cp.start()
                sent.append(cp)
        for b in range(nb):
            for k in range(4):
                handed(b, k).wait_recv()
            for k in range(4):
                first(b, k).wait_send()
        for cp in sent:
            cp.wait_send()
        for b in range(nb):
            mine(b).wait()

    return _Exchange(blocks, [_sds((N_DEV * b.shape[0], b.shape[1]), F32) for b in blocks],
                     [pltpu.SemaphoreType.DMA((nb, 7)), pltpu.SemaphoreType.DMA((nb, 7)), pltpu.SemaphoreType.DMA((nb,))],
                     start, finish)


def _inproj(x2, g1, w_in_t, b_in, tabs, seq, exchanges=()):
    T = x2.shape[0]
    tm = min(TM, seq)
    nseq = seq // tm

    def body(x_ref, g_ref, w_ref, b_ref, c_ref, a_ref, bt_ref, h_ref, u_ref, q_ref, k_ref, v_ref, gate_ref):
        x = x_ref[...]
        h = (x * _rms(x) * g_ref[...]).astype(BF16)
        h_ref[...] = h

        def proj(lo, hi):
            return _dot_nt(h, w_ref[lo:hi, :]) + b_ref[:, lo:hi]

        c, a, bt = c_ref[...], a_ref[...], bt_ref[...]
        u_ref[...] = proj(0, C_Q)
        q = proj(C_Q, C_K)
        for p in range(4):
            sl = slice(LANES * p, LANES * (p + 1))
            q_ref[:, sl] = (_rot_fwd(q[:, sl], c, a, bt) * SCALE).astype(BF16)
        kv = proj(C_K, C_G)
        k_ref[...] = _rot_fwd(kv[:, :KV_WIDTH], c, a, bt).astype(BF16)
        v_ref[...] = kv[:, KV_WIDTH:].astype(BF16)
        for j in range(2):
            lo = C_G + D_MODEL * j
            gate_ref[:, D_MODEL * j:D_MODEL * (j + 1)] = jax.nn.sigmoid(proj(lo, lo + D_MODEL)).astype(BF16)

    tab = pl.BlockSpec((tm, LANES), lambda i: (i % nseq, 0))
    return _call(
        body, name="inproj", grid=(T // tm,),
        in_specs=[_rows(tm, D_MODEL), _const((1, D_MODEL)), _const((IN_WIDTH, D_MODEL)), _const((1, IN_WIDTH)),
                  tab, tab, tab],
        out_specs=[_rows(tm, D_MODEL), _rows(tm, POOL_WIDTH), _rows(tm, ATTN_WIDTH), _rows(tm, KV_WIDTH),
                   _rows(tm, KV_WIDTH), _rows(tm, GATE_WIDTH)],
        out_shape=[_sds((T, D_MODEL), BF16), _sds((T, POOL_WIDTH), F32), _sds((T, ATTN_WIDTH), BF16),
                   _sds((T, KV_WIDTH), BF16), _sds((T, KV_WIDTH), BF16), _sds((T, GATE_WIDTH), BF16)],
        args=(x2, g1, w_in_t, b_in, *tabs), sem=("parallel",), exchanges=exchanges)


def _inv_count(pos, w):
    return 1.0 / jnp.minimum(pos + 1, w).astype(F32)


def _pool_fwd(u, w_pool, pool_scale, seq):
    T = u.shape[0]
    tp = min(TP, seq)
    nseq = seq // tp
    per = tp // HALO

    def body(u_ref, prev_ref, w_ref, s_ref, diff_ref, y_ref):
        i = pl.program_id(0)
        first = (i % nseq) == 0
        prev = jnp.where(first, 0.0, prev_ref[...])
        ext = jnp.concatenate([prev, u_ref[...]], axis=0)
        pos = (i % nseq) * tp + lax.broadcasted_iota(jnp.int32, (tp, 1), 0)
        for gi, w in enumerate(POOL_WINDOWS):
            sl = slice(POOL_GC * gi, POOL_GC * (gi + 1))
            xg = ext[:, sl]
            s = xg
            sh = 1
            while sh < w:
                s = s + pltpu.roll(s, sh, 0)
                sh *= 2
            pooled = s[HALO:] * _inv_count(pos, w)
            diff = (pooled - xg[HALO:]).astype(BF16)
            diff_ref[:, sl] = diff
            mixed = _dot(diff, w_ref[gi].astype(BF16))
            y_ref[:, sl] = (mixed * s_ref[:, sl]).astype(BF16)

    return _call(
        body, name="pool_fwd", grid=(T // tp,),
        in_specs=[_rows(tp, POOL_WIDTH),
                  pl.BlockSpec((HALO, POOL_WIDTH), lambda i: (jnp.maximum(i * per - 1, 0), 0)),
                  _const((4, POOL_GC, POOL_GC)), _const((1, POOL_WIDTH))],
        out_specs=[_rows(tp, POOL_WIDTH), _rows(tp, POOL_WIDTH)],
        out_shape=[_sds((T, POOL_WIDTH), BF16), _sds((T, POOL_WIDTH), BF16)],
        args=(u, u, w_pool, pool_scale), sem=("parallel",))


GROUP = 4
GROWS = GROUP * BLOCK


def _attn_masks(n):
    qi = lax.broadcasted_iota(jnp.int32, (GROWS, 2 * BLOCK), 0) % BLOCK
    kj = lax.broadcasted_iota(jnp.int32, (GROWS, 2 * BLOCK), 1)
    rel = qi + BLOCK - kj
    valid = (rel >= 0) & (rel < BLOCK) & (kj >= jnp.where(n > 0, 0, BLOCK))
    lo = lax.broadcasted_iota(jnp.int32, (BLOCK, LANES), 1) < HEAD_DIM
    return valid, lo


def _by_example(bl, *arrays):
    return [a.reshape(bl, a.shape[0] // bl, a.shape[1]) for a in arrays]


def _stack_heads(ref, h, lo):
    keep = lo if h == 0 else jnp.logical_not(lo)
    pieces = []
    for p in (2 * h, 2 * h + 1):
        xp = ref[:, LANES * p:LANES * (p + 1)].astype(F32)
        for e in range(2):
            t = xp if e == h else pltpu.roll(xp, HEAD_DIM, 1)
            pieces.append(jnp.where(keep, t, 0.0).astype(BF16))
    return jnp.concatenate(pieces, axis=0)


def _unstack_heads(stacked, h, lo):
    pairs = []
    for j in range(2):
        parts = []
        for e in range(2):
            t = stacked[BLOCK * (2 * j + e):BLOCK * (2 * j + e + 1)]
            parts.append(t if e == h else pltpu.roll(t, HEAD_DIM, 1))
        pairs.append(jnp.where(lo, parts[0], parts[1]))
    return pairs


def _sink_rows(sink_ref, h):
    head = lax.broadcasted_iota(jnp.int32, (GROWS, 1), 0) // BLOCK
    col = jnp.zeros((GROWS, 1), F32) + sink_ref[GROUP * h]
    for g in range(1, GROUP):
        col = jnp.where(head == g, sink_ref[GROUP * h + g], col)
    return col


def _group_probs(qs, kk, valid, sink):
    s = jnp.where(valid, _dot_nt(qs, kk), NEG_INF)
    m = jnp.maximum(jnp.max(s, axis=1, keepdims=True), sink)
    ex = jnp.exp(s - m)
    es = jnp.exp(sink - m)
    inv = 1.0 / (jnp.sum(ex, axis=1, keepdims=True) + es)
    return ex * inv, es * inv


def _attn_fwd(q, k, v, sinks, seq, exchanges=()):
    T = q.shape[0]
    nb = seq // BLOCK
    bl = T // seq

    def body(sink_ref, q_ref, kp_ref, kc_ref, vp_ref, vc_ref, o_ref):
        valid, lo = _attn_masks(pl.program_id(0))
        for b in range(bl):
            kk = jnp.concatenate([kp_ref[b], kc_ref[b]], axis=0)
            vv = jnp.concatenate([vp_ref[b], vc_ref[b]], axis=0)
            for h in range(2):
                qs = _stack_heads(q_ref.at[b], h, lo)
                pr, _ = _group_probs(qs, kk, valid, _sink_rows(sink_ref, h))
                o = _dot(pr.astype(BF16), vv)
                for j, pair in enumerate(_unstack_heads(o, h, lo)):
                    p = 2 * h + j
                    o_ref[b, :, LANES * p:LANES * (p + 1)] = pair.astype(BF16)

    cur = lambda n: (0, n, 0)
    prv = lambda n: (0, jnp.maximum(n - 1, 0), 0)
    kv = lambda m: pl.BlockSpec((bl, BLOCK, KV_WIDTH), m)
    res = _call(
        body, name="attn_fwd", grid=(nb,),
        in_specs=[pl.BlockSpec(memory_space=pltpu.SMEM), pl.BlockSpec((bl, BLOCK, ATTN_WIDTH), cur),
                  kv(prv), kv(cur), kv(prv), kv(cur)],
        out_specs=[pl.BlockSpec((bl, BLOCK, ATTN_WIDTH), cur)],
        out_shape=[_sds((bl, seq, ATTN_WIDTH), BF16)],
        args=(sinks, *_by_example(bl, q, k, k, v, v)), sem=("parallel",), exchanges=exchanges)
    if exchanges:
        return [res[0][0].reshape(T, ATTN_WIDTH)], res[1]
    return [res[0].reshape(T, ATTN_WIDTH)]


def _merge_out(y_pool, y_attn, gate, x2, w_bp, w_ba, w_out, g2, g3, exchanges=()):
    T = x2.shape[0]
    tm = min(TM, T)

    def body(yp_ref, ya_ref, gate_ref, x_ref, wbp_ref, wba_ref, wo_ref, g2_ref, g3_ref,
             bp_ref, ba_ref, mg_ref, mix_ref, x1_ref, h2_ref):
        yp, ya = yp_ref[...], ya_ref[...]
        bp = jnp.concatenate([_dot(yp, wbp_ref[j]) for j in range(N_CHIPS)], axis=1)
        ba = jnp.concatenate([_dot(ya, wba_ref[j]) for j in range(N_CHIPS)], axis=1)
        bp_ref[...] = bp.astype(BF16)
        ba_ref[...] = ba.astype(BF16)
        merged = (gate_ref[:, :D_MODEL].astype(F32) * bp + gate_ref[:, D_MODEL:].astype(F32) * ba).astype(BF16)
        mg_ref[...] = merged
        mix = _dot(merged, wo_ref[...])
        mix_ref[...] = mix
        x1 = x_ref[...] + mix * _rms(mix) * g2_ref[...]
        x1_ref[...] = x1
        h2_ref[...] = (x1 * _rms(x1) * g3_ref[...]).astype(BF16)

    return _call(
        body, name="merge_out", grid=(T // tm,),
        in_specs=[_rows(tm, POOL_WIDTH), _rows(tm, ATTN_WIDTH), _rows(tm, GATE_WIDTH), _rows(tm, D_MODEL),
                  _const(w_bp.shape), _const(w_ba.shape), _const((D_MODEL, D_MODEL)),
                  _const((1, D_MODEL)), _const((1, D_MODEL))],
        out_specs=[_rows(tm, D_MODEL)] * 6,
        out_shape=[_sds((T, D_MODEL), BF16), _sds((T, D_MODEL), BF16), _sds((T, D_MODEL), BF16),
                   _sds((T, D_MODEL), F32), _sds((T, D_MODEL), F32), _sds((T, D_MODEL), BF16)],
        args=(y_pool, y_attn, gate, x2, w_bp, w_ba, w_out, g2, g3), sem=("parallel",), exchanges=exchanges)


HALF = D_MODEL // 2
_HALVES = _const((N_CHIPS, HALF, D_MODEL))


def _mlp_up(h2, w_up, exchanges=()):
    T = h2.shape[0]
    tm = min(TM, T)

    def body(h_ref, wa_ref, wb_ref, up_ref, a_ref):
        ha, hb = h_ref[:, :HALF], h_ref[:, HALF:]
        for j in range(N_CHIPS):
            sl = slice(D_MODEL * j, D_MODEL * (j + 1))
            up = _dot(ha, wa_ref[j]) + _dot(hb, wb_ref[j])
            up_ref[:, sl] = up.astype(BF16)
            a_ref[:, sl] = jnp.square(jnp.maximum(up, 0.0)).astype(BF16)

    return _call(
        body, name="mlp_up", grid=(T // tm,),
        in_specs=[_rows(tm, D_MODEL), _HALVES, _HALVES],
        out_specs=[_rows(tm, D_FF), _rows(tm, D_FF)],
        out_shape=[_sds((T, D_FF), BF16), _sds((T, D_FF), BF16)],
        args=(h2, *w_up), sem=("parallel",), exchanges=exchanges)


def _mlp_down_loss(a, x1, tgt, w_down, g4):
    T = a.shape[0]
    tm = min(TM, T)

    def body(a_ref, x1_ref, t_ref, wa_ref, wb_ref, g_ref, dff_ref, dy_ref, loss_ref, dg_ref):
        @pl.when(pl.program_id(0) == 0)
        def _():
            loss_ref[...] = jnp.zeros_like(loss_ref)
            dg_ref[...] = jnp.zeros_like(dg_ref)

        ff = None
        for j in range(N_CHIPS):
            lo = D_MODEL * j
            t = _dot(a_ref[:, lo:lo + HALF], wa_ref[j]) + _dot(a_ref[:, lo + HALF:lo + D_MODEL], wb_ref[j])
            ff = t if ff is None else ff + t
        g = g_ref[...]
        err = x1_ref[...] + ff * _rms(ff) * g - t_ref[...]
        loss_ref[...] += jnp.sum(err * err) * (0.5 / D_MODEL)
        dy = err * (1.0 / D_MODEL)
        dy_ref[...] = dy
        dff, dg = _norm_bwd(ff, g, dy)
        dff_ref[...] = dff.astype(BF16)
        dg_ref[...] += dg

    return _call(
        body, name="mlp_down_loss", grid=(T // tm,),
        in_specs=[_rows(tm, D_FF), _rows(tm, D_MODEL), _rows(tm, D_MODEL), _HALVES, _HALVES, _const((1, D_MODEL))],
        out_specs=[_rows(tm, D_MODEL), _rows(tm, D_MODEL), _const((8, LANES)), _const((1, D_MODEL))],
        out_shape=[_sds((T, D_MODEL), BF16), _sds((T, D_MODEL), F32), _sds((8, LANES), F32),
                   _sds((1, D_MODEL), F32)],
        args=(a, x1, tgt, *w_down, g4), sem=("arbitrary",))


def _mlp_down_bwd(dff, up, w_down):
    T = dff.shape[0]
    tm = min(TM, T)

    def body(d_ref, up_ref, wa_ref, wb_ref, dup_ref):
        d = d_ref[...]
        for j in range(N_CHIPS):
            for part, w_ref in enumerate((wa_ref, wb_ref)):
                lo = D_MODEL * j + HALF * part
                da = _dot_nt(d, w_ref[j])
                relu = jnp.maximum(up_ref[:, lo:lo + HALF].astype(F32), 0.0)
                dup_ref[:, lo:lo + HALF] = (da * (2.0 * relu)).astype(BF16)

    return _call(
        body, name="mlp_down_bwd", grid=(T // tm,),
        in_specs=[_rows(tm, D_MODEL), _rows(tm, D_FF), _HALVES, _HALVES],
        out_specs=[_rows(tm, D_FF)],
        out_shape=[_sds((T, D_FF), BF16)],
        args=(dff, up, *w_down), sem=("parallel",))[0]


def _mlp_up_bwd(dup, dy, x1, mix, w_up, g3, g2, exchanges=()):
    T = dup.shape[0]
    tm = min(TM, T)

    def body(dup_ref, dy_ref, x1_ref, mix_ref, wa_ref, wb_ref, g3_ref, g2_ref, dx1_ref, dmix_ref, dg3_ref, dg2_ref):
        @pl.when(pl.program_id(0) == 0)
        def _():
            dg3_ref[...] = jnp.zeros_like(dg3_ref)
            dg2_ref[...] = jnp.zeros_like(dg2_ref)

        def back(w_ref):
            acc = _dot_nt(dup_ref[:, :D_MODEL], w_ref[0])
            for j in range(1, N_CHIPS):
                acc = acc + _dot_nt(dup_ref[:, D_MODEL * j:D_MODEL * (j + 1)], w_ref[j])
            return acc

        dh2 = jnp.concatenate([back(wa_ref), back(wb_ref)], axis=1)
        dx, dg3 = _norm_bwd(x1_ref[...], g3_ref[...], dh2)
        dx1 = dy_ref[...] + dx
        dx1_ref[...] = dx1
        dg3_ref[...] += dg3
        dmix, dg2 = _norm_bwd(mix_ref[...], g2_ref[...], dx1)
        dmix_ref[...] = dmix.astype(BF16)
        dg2_ref[...] += dg2

    return _call(
        body, name="mlp_up_bwd", grid=(T // tm,),
        in_specs=[_rows(tm, D_FF), _rows(tm, D_MODEL), _rows(tm, D_MODEL), _rows(tm, D_MODEL),
                  _HALVES, _HALVES, _const((1, D_MODEL)), _const((1, D_MODEL))],
        out_specs=[_rows(tm, D_MODEL), _rows(tm, D_MODEL), _const((1, D_MODEL)), _const((1, D_MODEL))],
        out_shape=[_sds((T, D_MODEL), F32), _sds((T, D_MODEL), BF16), _sds((1, D_MODEL), F32),
                   _sds((1, D_MODEL), F32)],
        args=(dup, dy, x1, mix, *w_up, g3, g2), sem=("arbitrary",), exchanges=exchanges)


def _dw(tag, a, g, ta, tn, shard_cols=False, exchanges=()):
    T, ka = a.shape
    n = g.shape[1]
    tk = min(2 * TM, T)
    nk = T // tk

    def body(a_ref, g_ref, o_ref):
        @pl.when(pl.program_id(2) == 0)
        def _():
            o_ref[...] = jnp.zeros_like(o_ref)

        o_ref[...] += _dot_tn(a_ref[...], g_ref[...])

    if shard_cols:
        per = (n // N_CHIPS) // tn
        out_spec = pl.BlockSpec((None, ta, tn), lambda i, j, k: (j // per, i, j % per))
        out_shape = _sds((N_CHIPS, ka, n // N_CHIPS), F32)
    else:
        out_spec = pl.BlockSpec((ta, tn), lambda i, j, k: (i, j))
        out_shape = _sds((ka, n), F32)
    return _call(
        body, name="dw_" + tag, grid=(ka // ta, n // tn, nk),
        in_specs=[pl.BlockSpec((tk, ta), lambda i, j, k: (k, i)), pl.BlockSpec((tk, tn), lambda i, j, k: (k, j))],
        out_specs=[out_spec], out_shape=[out_shape],
        args=(a, g), sem=("parallel", "parallel", "arbitrary"), exchanges=exchanges)


def _dw_slabs(tag, a, g):
    T, ka = a.shape
    n = g.shape[1]
    c = n // N_CHIPS
    tk = min(TM, T)

    def body(a_ref, g_ref, o_ref):
        @pl.when(pl.program_id(0) == 0)
        def _():
            o_ref[...] = jnp.zeros_like(o_ref)

        res = _dot_tn(a_ref[...], g_ref[...])
        for j in range(N_CHIPS):
            o_ref[j] += res[:, c * j:c * (j + 1)]

    return _call(
        body, name="dw_" + tag, grid=(T // tk,),
        in_specs=[_rows(tk, ka), _rows(tk, n)],
        out_specs=[_const((N_CHIPS, ka, c))], out_shape=[_sds((N_CHIPS, ka, c), F32)],
        args=(a, g), sem=("arbitrary",))[0]


def _merge_bwd(dmix, gate, bp, ba, w_out, w_bp, w_ba, exchanges=()):
    T = dmix.shape[0]
    tm = min(TM, T)

    def body(dmix_ref, gate_ref, bp_ref, ba_ref, wo_ref, wbp_ref, wba_ref,
             dbp_ref, dba_ref, dgate_ref, dyp_ref, dya_ref):
        dm = _dot_nt(dmix_ref[...], wo_ref[...])
        for j, (b_ref, db_ref, w_ref, dy_ref) in enumerate(
                ((bp_ref, dbp_ref, wbp_ref, dyp_ref), (ba_ref, dba_ref, wba_ref, dya_ref))):
            sl = slice(D_MODEL * j, D_MODEL * (j + 1))
            gt = gate_ref[:, sl].astype(F32)
            db = (dm * gt).astype(BF16)
            db_ref[...] = db
            dgate_ref[:, sl] = (dm * b_ref[...].astype(F32) * gt * (1.0 - gt)).astype(BF16)
            cw = D_MODEL // N_CHIPS
            dy = _dot_nt(db[:, :cw], w_ref[0])
            for c in range(1, N_CHIPS):
                dy = dy + _dot_nt(db[:, cw * c:cw * (c + 1)], w_ref[c])
            dy_ref[...] = dy.astype(dy_ref.dtype)

    return _call(
        body, name="merge_bwd", grid=(T // tm,),
        in_specs=[_rows(tm, D_MODEL), _rows(tm, GATE_WIDTH), _rows(tm, D_MODEL), _rows(tm, D_MODEL),
                  _const((D_MODEL, D_MODEL)), _const(w_bp.shape), _const(w_ba.shape)],
        out_specs=[_rows(tm, D_MODEL), _rows(tm, D_MODEL), _rows(tm, GATE_WIDTH), _rows(tm, POOL_WIDTH),
                   _rows(tm, ATTN_WIDTH)],
        out_shape=[_sds((T, D_MODEL), BF16), _sds((T, D_MODEL), BF16), _sds((T, GATE_WIDTH), BF16),
                   _sds((T, POOL_WIDTH), F32), _sds((T, ATTN_WIDTH), BF16)],
        args=(dmix, gate, bp, ba, w_out, w_bp, w_ba), sem=("parallel",), exchanges=exchanges)


def _attn_bwd(q, k, v, do, sinks, tabs, seq, exchanges=()):
    T = q.shape[0]
    nb = seq // BLOCK
    bl = T // seq
    steps = nb + 1

    def body(sink_ref, q_ref, do_ref, kp_ref, kc_ref, vp_ref, vc_ref, c_ref, a_ref, bt_ref, cp_ref, ap_ref, btp_ref,
             dq_ref, dk_ref, dv_ref, dsink_ref, ck_ref, cv_ref):
        n = pl.program_id(0)

        @pl.when(n == 0)
        def _():
            dsink_ref[...] = jnp.zeros_like(dsink_ref)
            ck_ref[...] = jnp.zeros_like(ck_ref)
            cv_ref[...] = jnp.zeros_like(cv_ref)

        @pl.when(n < nb)
        def _():
            valid, lo = _attn_masks(n)
            for b in range(bl):
                kk = jnp.concatenate([kp_ref[b], kc_ref[b]], axis=0)
                vv = jnp.concatenate([vp_ref[b], vc_ref[b]], axis=0)
                dk_acc = jnp.zeros((2 * BLOCK, KV_WIDTH), F32)
                dv_acc = jnp.zeros((2 * BLOCK, KV_WIDTH), F32)
                for h in range(2):
                    qs = _stack_heads(q_ref.at[b], h, lo)
                    dos = _stack_heads(do_ref.at[b], h, lo)
                    pr, ps = _group_probs(qs, kk, valid, _sink_rows(sink_ref, h))
                    dp = _dot_nt(dos, vv)
                    delta = jnp.sum(pr * dp, axis=1, keepdims=True)
                    ds = (pr * (dp - delta)).astype(BF16)
                    dsk = ps * delta
                    for g in range(GROUP):
                        idx = GROUP * h + g
                        dsink_ref[idx:idx + 1, :] += (jnp.zeros((1, LANES), F32)
                                                      - jnp.sum(dsk[BLOCK * g:BLOCK * (g + 1)]))
                    dk_acc = dk_acc + _dot_tn(ds, qs)
                    dv_acc = dv_acc + _dot_tn(pr.astype(BF16), dos)
                    for j, pair in enumerate(_unstack_heads(_dot(ds, kk) * SCALE, h, lo)):
                        sl = slice(LANES * (2 * h + j), LANES * (2 * h + j + 1))
                        dq_ref[b, :, sl] = _rot_bwd(pair, c_ref[...], a_ref[...], bt_ref[...]).astype(BF16)
                fin_k = ck_ref[b] + dk_acc[:BLOCK]
                dk_ref[b] = _rot_bwd(fin_k, cp_ref[...], ap_ref[...], btp_ref[...]).astype(BF16)
                dv_ref[b] = (cv_ref[b] + dv_acc[:BLOCK]).astype(BF16)
                ck_ref[b] = dk_acc[BLOCK:]
                cv_ref[b] = dv_acc[BLOCK:]

        @pl.when(n == nb)
        def _():
            for b in range(bl):
                dk_ref[b] = _rot_bwd(ck_ref[b], cp_ref[...], ap_ref[...], btp_ref[...]).astype(BF16)
                dv_ref[b] = cv_ref[b].astype(BF16)

    cur = lambda n: (0, jnp.minimum(n, nb - 1), 0)
    prv = lambda n: (0, jnp.clip(n - 1, 0, nb - 1), 0)
    tcur = lambda n: (jnp.minimum(n, nb - 1), 0)
    tprv = lambda n: (jnp.clip(n - 1, 0, nb - 1), 0)
    wide = lambda m: pl.BlockSpec((bl, BLOCK, ATTN_WIDTH), m)
    kv = lambda m: pl.BlockSpec((bl, BLOCK, KV_WIDTH), m)
    tab = lambda m: pl.BlockSpec((BLOCK, LANES), m)
    res = _call(
        body, name="attn_bwd", grid=(steps,),
        in_specs=[pl.BlockSpec(memory_space=pltpu.SMEM), wide(cur), wide(cur), kv(prv), kv(cur), kv(prv), kv(cur),
                  tab(tcur), tab(tcur), tab(tcur), tab(tprv), tab(tprv), tab(tprv)],
        out_specs=[wide(cur), kv(prv), kv(prv), _const((8, LANES))],
        out_shape=[_sds((bl, seq, ATTN_WIDTH), BF16), _sds((bl, seq, KV_WIDTH), BF16),
                   _sds((bl, seq, KV_WIDTH), BF16), _sds((8, LANES), F32)],
        scratch=[pltpu.VMEM((bl, BLOCK, KV_WIDTH), F32), pltpu.VMEM((bl, BLOCK, KV_WIDTH), F32)],
        args=(sinks, *_by_example(bl, q, do, k, k, v, v), *tabs, *tabs), sem=("arbitrary",), exchanges=exchanges)
    outs, rest = (res if exchanges else (res, None))
    outs = [outs[0].reshape(T, ATTN_WIDTH), outs[1].reshape(T, KV_WIDTH), outs[2].reshape(T, KV_WIDTH), outs[3]]
    return (outs, rest) if exchanges else outs


def _pool_bwd(dyp, diff, w_pool, pool_scale, seq, exchanges=()):
    T = dyp.shape[0]
    tp = min(TP, seq)
    nseq = seq // tp
    per = tp // HALO
    last_halo = T // HALO - 1

    def body(dy_ref, nxt_ref, diff_ref, w_ref, s_ref, du_ref, dw_ref, ds_ref):
        i = pl.program_id(0)

        @pl.when(i == 0)
        def _():
            dw_ref[...] = jnp.zeros_like(dw_ref)
            ds_ref[...] = jnp.zeros_like(ds_ref)

        last = (i % nseq) == nseq - 1
        nxt = jnp.where(last, 0.0, nxt_ref[...])
        ext = jnp.concatenate([dy_ref[...], nxt], axis=0) * s_ref[...]
        pos = (i % nseq) * tp + lax.broadcasted_iota(jnp.int32, (tp + HALO, 1), 0)
        for gi, w in enumerate(POOL_WINDOWS):
            sl = slice(POOL_GC * gi, POOL_GC * (gi + 1))
            wg = w_ref[gi].astype(BF16)
            dmx = ext[:, sl].astype(BF16)
            ddiff = _dot_nt(dmx, wg)
            s = ddiff * _inv_count(pos, w)
            sh = 1
            while sh < w:
                s = s + pltpu.roll(s, tp + HALO - sh, 0)
                sh *= 2
            du_ref[:, sl] = (s[:tp] - ddiff[:tp]).astype(BF16)
            dg = diff_ref[:, sl]
            dw_ref[gi] += _dot_tn(dg, dmx[:tp])
            ds_ref[:, sl] += jnp.sum(dy_ref[:, sl] * _dot(dg, wg), axis=0, keepdims=True)

    return _call(
        body, name="pool_bwd", grid=(T // tp,),
        in_specs=[_rows(tp, POOL_WIDTH),
                  pl.BlockSpec((HALO, POOL_WIDTH), lambda i: (jnp.minimum((i + 1) * per, last_halo), 0)),
                  _rows(tp, POOL_WIDTH), _const((4, POOL_GC, POOL_GC)), _const((1, POOL_WIDTH))],
        out_specs=[_rows(tp, POOL_WIDTH), _const((4, POOL_GC, POOL_GC)), _const((1, POOL_WIDTH))],
        out_shape=[_sds((T, POOL_WIDTH), BF16), _sds((4, POOL_GC, POOL_GC), F32), _sds((1, POOL_WIDTH), F32)],
        args=(dyp, dyp, diff, w_pool, pool_scale), sem=("arbitrary",), exchanges=exchanges)


_PARTS = ((0, C_Q), (C_Q, C_K), (C_K, C_V), (C_V, C_G), (C_G, IN_WIDTH))


def _inproj_bwd(parts, x2, dx1, w_in_t, g1, exchanges=()):
    T = x2.shape[0]
    tm = min(TM, T)

    def body(du_ref, dq_ref, dk_ref, dv_ref, dgt_ref, x_ref, dx1_ref, w_ref, g_ref, gx_ref, dg_ref):
        @pl.when(pl.program_id(0) == 0)
        def _():
            dg_ref[...] = jnp.zeros_like(dg_ref)

        dh = jnp.zeros((tm, D_MODEL), F32)
        for (lo, hi), p_ref in zip(_PARTS, (du_ref, dq_ref, dk_ref, dv_ref, dgt_ref)):
            dh = dh + _dot(p_ref[...], w_ref[lo:hi, :])
        dx, dg = _norm_bwd(x_ref[...], g_ref[...], dh)
        gx_ref[...] = dx1_ref[...] + dx
        dg_ref[...] += dg

    return _call(
        body, name="inproj_bwd", grid=(T // tm,),
        in_specs=[_rows(tm, hi - lo) for lo, hi in _PARTS]
        + [_rows(tm, D_MODEL), _rows(tm, D_MODEL), _const((IN_WIDTH, D_MODEL)), _const((1, D_MODEL))],
        out_specs=[_rows(tm, D_MODEL), _const((1, D_MODEL))],
        out_shape=[_sds((T, D_MODEL), F32), _sds((1, D_MODEL), F32)],
        args=(*parts, x2, dx1, w_in_t, g1), sem=("arbitrary",), exchanges=exchanges)


def _dw_in(h, parts, exchanges=()):
    T = h.shape[0]
    tk = min(TM, T)

    def body(h_ref, du_ref, dq_ref, dk_ref, dv_ref, dgt_ref, o_ref, db_ref):
        @pl.when(pl.program_id(0) == 0)
        def _():
            o_ref[...] = jnp.zeros_like(o_ref)
            db_ref[...] = jnp.zeros_like(db_ref)

        hh = h_ref[...]
        for (lo, hi), p_ref in zip(_PARTS, (du_ref, dq_ref, dk_ref, dv_ref, dgt_ref)):
            part = p_ref[...]
            o_ref[lo:hi, :] += _dot_tn(part, hh)
            db_ref[:, lo:hi] += jnp.sum(part.astype(F32), axis=0, keepdims=True)

    return _call(
        body, name="dw_in", grid=(T // tk,),
        in_specs=[_rows(tk, D_MODEL)] + [_rows(tk, hi - lo) for lo, hi in _PARTS],
        out_specs=[_const((IN_WIDTH, D_MODEL)), _const((1, IN_WIDTH))],
        out_shape=[_sds((IN_WIDTH, D_MODEL), F32), _sds((1, IN_WIDTH), F32)],
        args=(h, *parts), sem=("arbitrary",), exchanges=exchanges)


def _row_tile(rows, cap=256, mult=16):
    best = None
    for t in range(mult, min(rows, cap) + 1, mult):
        if rows % t == 0:
            best = t
    if best is None:
        raise ValueError("no row tile for %d rows" % rows)
    return best


def _pair_sum(ids, full, got):
    _, r, c = full.shape
    hr = r // 2
    tr = _row_tile(hr)
    nblk = hr // tr

    def body(ids_ref, a_ref, b_ref, own_ref, sb_ref):
        s = a_ref[...] + b_ref[...]
        sb_ref[...] = s.astype(BF16)

        @pl.when(pl.program_id(1) == ids_ref[0])
        def _():
            own_ref[...] = s

    slab = pl.BlockSpec((None, tr, c), lambda i, j, ids_ref: (j, i, 0))
    return pl.pallas_call(
        body, name="pair_sum_%dx%d" % (r, c),
        grid_spec=pltpu.PrefetchScalarGridSpec(
            num_scalar_prefetch=1, grid=(nblk, N_CHIPS),
            in_specs=[pl.BlockSpec((None, tr, c), lambda i, j, ids_ref: (j, ids_ref[1] * nblk + i, 0)), slab],
            out_specs=[pl.BlockSpec((tr, c), lambda i, j, ids_ref: (i, 0)), slab]),
        out_shape=[_sds((hr, c), F32), _sds((N_CHIPS, hr, c), BF16)],
        compiler_params=_cp("parallel", "arbitrary"),
    )(ids, full, got)


def _chip_sum(ids, own, got):
    hr, c = own.shape
    tr = _row_tile(hr)
    nblk = hr // tr

    def body(ids_ref, a_ref, b_ref, o_ref):
        o_ref[...] = ((a_ref[...] + b_ref[0].astype(F32)) + b_ref[1].astype(F32)) + b_ref[2].astype(F32)

    return pl.pallas_call(
        body, name="chip_sum_%dx%d" % (hr, c),
        grid_spec=pltpu.PrefetchScalarGridSpec(
            num_scalar_prefetch=1, grid=(nblk,),
            in_specs=[pl.BlockSpec((tr, c), lambda i, ids_ref: (i, 0)),
                      pl.BlockSpec((3, tr, c), lambda i, ids_ref: (0, i, 0))],
            out_specs=pl.BlockSpec((tr, c), lambda i, ids_ref: (ids_ref[1] * nblk + i, 0))),
        out_shape=_sds((2 * hr, c), F32),
        compiler_params=_cp("parallel"),
    )(ids, own, got)


def _adamw_math(w, g, m, v):
    nm = ADAM_B1 * m + (1.0 - ADAM_B1) * g
    nv = ADAM_B2 * v + (1.0 - ADAM_B2) * jnp.square(g)
    m_hat = nm / (1.0 - ADAM_B1 ** ADAM_STEP)
    v_hat = nv / (1.0 - ADAM_B2 ** ADAM_STEP)
    return -ADAM_LR * (m_hat / (jnp.sqrt(v_hat) + ADAM_EPS) + ADAM_WD * w), nm, nv


def _adamw(w, g, m, v):
    r, c = w.shape
    tr = _row_tile(r, cap=512, mult=8)

    def body(w_ref, g_ref, m_ref, v_ref, d_ref, nm_ref, nv_ref):
        d_ref[...], nm_ref[...], nv_ref[...] = _adamw_math(w_ref[...], g_ref[...], m_ref[...], v_ref[...])

    spec = _rows(tr, c)
    return pl.pallas_call(
        body, name="adamw_%dx%d" % (r, c), grid=(r // tr,),
        in_specs=[spec] * 4, out_specs=[spec] * 3, out_shape=[_sds((r, c), F32)] * 3,
        compiler_params=_cp("parallel"),
    )(w, g, m, v)


_SMALL_NAMES = ("w_pool", "b_in", "g_mix_pre", "g_mix_post", "g_mlp_pre", "g_mlp_post", "pool_scale", "attn_sinks")
B_ROWS = -(-IN_WIDTH // D_MODEL)


def _row_block(rows):
    rows = [jnp.pad(r.astype(F32), ((0, 0), (0, D_MODEL - r.shape[1]))) for r in rows]
    return jnp.pad(jnp.concatenate(rows, axis=0), ((0, 8 - len(rows)), (0, 0)))


def _early_block(dg2, dg3, dg4, dps, dsink, loss):
    tail = jnp.concatenate([jnp.pad(dsink.reshape(1, -1), ((0, 0), (0, LANES - dsink.size))),
                            jnp.pad(loss.reshape(1, 1), ((0, 0), (0, LANES - 1)))], axis=1)
    return _row_block([dg2, dg3, dg4, dps, tail])


def _late_block(db_in, dg1):
    b = jnp.pad(db_in, ((0, 0), (0, B_ROWS * D_MODEL - IN_WIDTH))).reshape(B_ROWS, D_MODEL)
    return _row_block([b[r:r + 1] for r in range(B_ROWS)] + [dg1])


def _small_update(gearly, gmat, glate, w, m, v):
    names = _SMALL_NAMES
    n = len(names)

    def total(ref, rows):
        acc = ref[0:rows, :]
        for d in range(1, N_DEV):
            acc = acc + ref[d * rows:(d + 1) * rows, :]
        return acc

    def body(*refs):
        early_ref, gmat_ref, late_ref = refs[:3]
        w_refs, m_refs, v_refs = refs[3:3 + n], refs[3 + n:3 + 2 * n], refs[3 + 2 * n:3 + 3 * n]
        outs = refs[3 + 3 * n:]
        loss_ref, g_refs, d_refs = outs[0], outs[1:1 + n], outs[1 + n:1 + 2 * n]
        nm_refs, nv_refs = outs[1 + 2 * n:1 + 3 * n], outs[1 + 3 * n:1 + 4 * n]
        early, late = total(early_ref, 8), total(late_ref, 8)
        loss_ref[...] = jnp.sum(early[4:5, LANES:2 * LANES], axis=1, keepdims=True)
        bias = jnp.concatenate([late[r:r + 1, :] for r in range(B_ROWS - 1)]
                               + [late[B_ROWS - 1:B_ROWS, :IN_WIDTH - (B_ROWS - 1) * D_MODEL]], axis=1)
        grad = dict(b_in=bias, g_mix_pre=late[B_ROWS:B_ROWS + 1, :], g_mix_post=early[0:1, :],
                    g_mlp_pre=early[1:2, :], g_mlp_post=early[2:3, :], pool_scale=early[3:4, :POOL_WIDTH],
                    attn_sinks=early[4:5, :N_Q_HEADS])
        for i, name in enumerate(names):
            g = total(gmat_ref, 4 * POOL_GC) if name == "w_pool" else grad[name]
            g_refs[i][...] = g
            d_refs[i][...], nm_refs[i][...], nv_refs[i][...] = _adamw_math(
                w_refs[i][...], g, m_refs[i][...], v_refs[i][...])

    shapes = [_sds(w[k].shape, F32) for k in names]
    res = pl.pallas_call(
        body, name="small_update", out_shape=[_sds((1, 1), F32)] + shapes * 4,
        compiler_params=pltpu.CompilerParams(vmem_limit_bytes=VMEM_MB * 1024 * 1024),
    )(gearly, gmat, glate, *[w[k] for k in names], *[m[k] for k in names], *[v[k] for k in names])
    loss = res[0]
    per = {k: tuple(res[1 + j * n + i] for j in range(4)) for i, k in enumerate(names)}
    return loss, per


_BIG = ("w_in", "w_branch_pool", "w_branch_attn", "w_out", "w_up", "w_down")
_ORDER = ("g_mix_pre", "w_in", "b_in", "w_pool", "pool_scale", "attn_sinks", "w_branch_pool", "w_branch_attn",
          "w_out", "g_mix_post", "g_mlp_pre", "w_up", "w_down", "g_mlp_post")


def _stack_rows(slab):
    return slab.reshape(-1, slab.shape[2])


def _step(x2, tgt, seq, shards, small, ids):
    tabs = _rope_tables(seq)
    g1, g2, g3, g4 = (small[n] for n in ("g_mix_pre", "g_mix_post", "g_mlp_pre", "g_mlp_post"))
    sinks = small["attn_sinks"].reshape(N_Q_HEADS)
    w_pool = small["w_pool"].reshape(4, POOL_GC, POOL_GC)
    pool_scale = small["pool_scale"]

    def whole(shard, slabs):
        return lax.dynamic_update_slice(slabs, shard[None], (ids[0], 0, 0))

    up_a, up_b = shards["w_up"][:HALF], shards["w_up"][HALF:]
    down_a, down_b = shards["w_down"][:HALF], shards["w_down"][HALF:]
    w_in = _stack_rows(whole(shards["w_in"], _alone("gather_in", _ex_gather([shards["w_in"]]))[0][0]))
    mix_shards = [shards[n] for n in ("w_branch_pool", "w_branch_attn", "w_out")]
    (h, u, q, k, v, gate), [(*mix_slabs, got_a)] = _inproj(
        x2, g1, w_in, small["b_in"], tabs, seq, exchanges=[_ex_gather(mix_shards + [up_a])])
    w_bp, w_ba, out_slab = (whole(s, g) for s, g in zip(mix_shards, mix_slabs))
    w_out = _stack_rows(out_slab)
    diff, y_pool = _pool_fwd(u, w_pool, pool_scale, seq)
    (y_attn,), [[got_b]] = _attn_fwd(q, k, v, sinks, seq, exchanges=[_ex_gather([up_b])])
    (bp, ba, merged, mix, x1, h2), [[got_c]] = _merge_out(
        y_pool, y_attn, gate, x2, w_bp, w_ba, w_out, g2, g3, exchanges=[_ex_gather([down_a])])
    w_up = (whole(up_a, got_a), whole(up_b, got_b))
    (up, act), [[got_d]] = _mlp_up(h2, w_up, exchanges=[_ex_gather([down_b])])
    w_down = (whole(down_a, got_c), whole(down_b, got_d))
    dff, dy, loss_acc, dg4 = _mlp_down_loss(act, x1, tgt, w_down, g4)

    dup = _mlp_down_bwd(dff, up, w_down)
    dw_down = _dw("down", act, dff, 1024, 1024)[0].reshape(N_CHIPS, D_FF // N_CHIPS, D_MODEL)
    (dx1, dmix, dg3, dg2), [[got]] = _mlp_up_bwd(dup, dy, x1, mix, w_up, g3, g2, exchanges=[_ex_pair([dw_down])])
    ps_down = _pair_sum(ids, dw_down, got)
    (dw_up,), [[got]] = _dw("up", h2, dup, 1024, 1024, shard_cols=True, exchanges=[_ex_chip([ps_down[1]])])
    half_down = _chip_sum(ids, ps_down[0], got)
    (dbp, dba, dgate, dyp, dya), [[got], [g_down]] = _merge_bwd(
        dmix, gate, bp, ba, w_out, w_bp, w_ba, exchanges=[_ex_pair([dw_up]), _ex_swap([half_down])])
    ps_up = _pair_sum(ids, dw_up, got)
    dw_mix = [_dw("out", merged, dmix, 1024, 1024)[0].reshape(N_CHIPS, D_MODEL // N_CHIPS, D_MODEL),
              _dw_slabs("branch_pool", y_pool, dbp), _dw_slabs("branch_attn", y_attn, dba)]
    (dq, dk, dv, dsink), [[got], gots] = _attn_bwd(
        q, k, v, dya, sinks, tabs, seq, exchanges=[_ex_chip([ps_up[1]]), _ex_pair(dw_mix)])
    half_up = _chip_sum(ids, ps_up[0], got)
    ps_mix = [_pair_sum(ids, d, g) for d, g in zip(dw_mix, gots)]
    (du, dw_pool, dps), [[g_up]] = _pool_bwd(dyp, diff, w_pool, pool_scale, seq, exchanges=[_ex_swap([half_up])])
    parts = (du, dq, dk, dv, dgate)
    early = _early_block(dg2, dg3, dg4, dps, dsink[:, 0], loss_acc[0, 0])
    mat = dw_pool.reshape(4 * POOL_GC, POOL_GC)
    (dw_in_t, db_in), [gots, [gearly, gmat]] = _dw_in(
        h, parts, exchanges=[_ex_chip([p[1] for p in ps_mix]), _ex_allgather([early, mat])])
    half_mix = [_chip_sum(ids, p[0], g) for p, g in zip(ps_mix, gots)]
    dw_in = dw_in_t.reshape(N_CHIPS, IN_WIDTH // N_CHIPS, D_MODEL)
    g_mix, [got] = _alone("swap_mix_pair_in", _ex_swap(half_mix), _ex_pair([dw_in]))
    ps_in = _pair_sum(ids, dw_in, got)
    (gx, dg1), [[got]] = _inproj_bwd(parts, x2, dx1, w_in, g1, exchanges=[_ex_chip([ps_in[1]])])
    [g_in], [glate] = _alone("swap_in_allgather", _ex_swap([_chip_sum(ids, ps_in[0], got)]),
                             _ex_allgather([_late_block(db_in, dg1)]))

    grads = dict(w_in=g_in, w_branch_pool=g_mix[1], w_branch_attn=g_mix[2], w_out=g_mix[0], w_up=g_up, w_down=g_down)
    return (gearly, gmat, glate), gx, grads


def kernel(x, g_mix_pre, w_in, b_in, w_pool, pool_scale, attn_sinks, w_branch_pool, w_branch_attn, w_out, g_mix_post, g_mlp_pre, w_up, w_down, g_mlp_post, loss_target, m_g_mix_pre, m_w_in, m_b_in, m_w_pool, m_pool_scale, m_attn_sinks, m_w_branch_pool, m_w_branch_attn, m_w_out, m_g_mix_post, m_g_mlp_pre, m_w_up, m_w_down, m_g_mlp_post, v_g_mix_pre, v_w_in, v_b_in, v_w_pool, v_pool_scale, v_attn_sinks, v_w_branch_pool, v_w_branch_attn, v_w_out, v_g_mix_post, v_g_mlp_pre, v_w_up, v_w_down, v_g_mlp_post):
    weights = dict(g_mix_pre=g_mix_pre, w_in=w_in, b_in=b_in, w_pool=w_pool, pool_scale=pool_scale,
                   attn_sinks=attn_sinks, w_branch_pool=w_branch_pool, w_branch_attn=w_branch_attn, w_out=w_out,
                   g_mix_post=g_mix_post, g_mlp_pre=g_mlp_pre, w_up=w_up, w_down=w_down, g_mlp_post=g_mlp_post)
    mom1 = dict(g_mix_pre=m_g_mix_pre, w_in=m_w_in, b_in=m_b_in, w_pool=m_w_pool, pool_scale=m_pool_scale,
                attn_sinks=m_attn_sinks, w_branch_pool=m_w_branch_pool, w_branch_attn=m_w_branch_attn,
                w_out=m_w_out, g_mix_post=m_g_mix_post, g_mlp_pre=m_g_mlp_pre, w_up=m_w_up, w_down=m_w_down,
                g_mlp_post=m_g_mlp_post)
    mom2 = dict(g_mix_pre=v_g_mix_pre, w_in=v_w_in, b_in=v_b_in, w_pool=v_w_pool, pool_scale=v_pool_scale,
                attn_sinks=v_attn_sinks, w_branch_pool=v_w_branch_pool, w_branch_attn=v_w_branch_attn,
                w_out=v_w_out, g_mix_post=v_g_mix_post, g_mlp_pre=v_g_mlp_pre, w_up=v_w_up, w_down=v_w_down,
                g_mlp_post=v_g_mlp_post)
    b_loc, seq, _ = x.shape
    x2 = x.reshape(b_loc * seq, D_MODEL)
    tgt = loss_target.reshape(b_loc * seq, D_MODEL)
    ids = jnp.stack([2 * lax.axis_index("x") + lax.axis_index("y"), lax.axis_index("c")]).astype(jnp.int32)

    def flat(n, a):
        return a[0].T if n == "w_in" else a[0]

    def unflat(n, a):
        return (a.T if n == "w_in" else a)[None]

    shards = {n: flat(n, weights[n]).astype(BF16) for n in _BIG}
    small = {n: weights[n] for n in _ORDER if n not in _BIG}
    (gearly, gmat, glate), gx, grads = _step(x2, tgt, seq, shards, small, ids)

    def two_d(src):
        return {n: src[n].reshape(4 * POOL_GC, POOL_GC) if n == "w_pool" else src[n] for n in _SMALL_NAMES}

    loss, per = _small_update(gearly, gmat, glate, two_d(weights), two_d(mom1), two_d(mom2))
    delta, new_m, new_v = {}, {}, {}
    for n in _SMALL_NAMES:
        grads[n], delta[n], new_m[n], new_v[n] = (a.reshape(weights[n].shape) for a in per[n])
    for n in _BIG:
        d, nm, nv = _adamw(flat(n, weights[n]), grads[n], flat(n, mom1[n]), flat(n, mom2[n]))
        grads[n] = unflat(n, grads[n])
        delta[n], new_m[n], new_v[n] = unflat(n, d), unflat(n, nm), unflat(n, nv)

    return (loss[0, 0], gx.reshape(x.shape), *[grads[n] for n in _ORDER], *[delta[n] for n in _ORDER],
            *[new_m[n] for n in _ORDER], *[new_v[n] for n in _ORDER])
```

```python
import jax
import jax.numpy as jnp
from jax import lax
from jax.experimental import pallas as pl
from jax.experimental.pallas import tpu as pltpu

F32 = jnp.float32
BF16 = jnp.bfloat16

D_MODEL = 1024
POOL_WINDOWS = (2, 4, 8, 16)
POOL_WIDTH = 512
POOL_GC = 128
HALO = 16
HEAD_DIM = 64
N_Q_HEADS = 8
ATTN_WIDTH = 512
KV_WIDTH = 128
BLOCK = 128
NEG_INF = -1e30
ROPE_THETA = 500000.0
ROT_DIM = 16
GATE_WIDTH = 2048
IN_WIDTH = 3328
D_FF = 4096
EPS = 1e-6
SCALE = HEAD_DIM ** -0.5
C_Q, C_K, C_V, C_G = 512, 1024, 1152, 1280

ADAM_LR, ADAM_B1, ADAM_B2, ADAM_EPS, ADAM_WD, ADAM_STEP = 0.001, 0.9, 0.999, 1e-08, 0.01, 10

N_CHIPS = 4
N_DEV = 8
LANES = 128
TM = 512
TP = 512
VMEM_MB = 56

MESH = pl.DeviceIdType.MESH
ANY = pl.BlockSpec(memory_space=pl.ANY)


def _cp(*sem, vmem=VMEM_MB):
    return pltpu.CompilerParams(dimension_semantics=sem, vmem_limit_bytes=vmem * 1024 * 1024)


def _rows(tile, cols):
    return pl.BlockSpec((tile, cols), lambda i: (i, 0))


def _const(shape):
    nd = len(shape)
    return pl.BlockSpec(shape, lambda i: (0,) * nd)


def _sds(shape, dtype):
    return jax.ShapeDtypeStruct(shape, dtype)


def _dot(a, b):
    return jnp.dot(a, b, preferred_element_type=F32)


def _dot_nt(a, b):
    return lax.dot_general(a, b, (((1,), (1,)), ((), ())), preferred_element_type=F32)


def _dot_tn(a, b):
    return lax.dot_general(a, b, (((0,), (0,)), ((), ())), preferred_element_type=F32)


def _rms(x):
    return lax.rsqrt(jnp.mean(x * x, axis=-1, keepdims=True) + EPS)


def _norm_bwd(x, g, dout):
    r = _rms(x)
    n = x * r
    dn = dout * g
    dx = r * (dn - n * jnp.mean(dn * n, axis=-1, keepdims=True))
    return dx, jnp.sum(dout * n, axis=0, keepdims=True)


def _rot_fwd(t, c, a, bt):
    return t * c + pltpu.roll(t, LANES - 8, 1) * a + pltpu.roll(t, 8, 1) * bt


def _rot_bwd(d, c, a, bt):
    return d * c + pltpu.roll(d * a, 8, 1) + pltpu.roll(d * bt, LANES - 8, 1)


def _rope_tables(seq):
    pos = jnp.arange(seq, dtype=F32)
    inv_freq = ROPE_THETA ** (-jnp.arange(0, ROT_DIM, 2, dtype=F32) / ROT_DIM)
    ang = pos[:, None] * inv_freq[None, :]
    cos, sin = jnp.cos(ang), jnp.sin(ang)
    ones = jnp.ones((seq, HEAD_DIM - ROT_DIM), F32)
    zeros8 = jnp.zeros((seq, 8), F32)
    zrest = jnp.zeros((seq, HEAD_DIM - ROT_DIM), F32)
    c = jnp.concatenate([cos, cos, ones], axis=1)
    a = jnp.concatenate([-sin, zeros8, zrest], axis=1)
    bt = jnp.concatenate([zeros8, sin, zrest], axis=1)
    return tuple(jnp.tile(t, (1, 2)) for t in (c, a, bt))


class _Exchange:
    def __init__(self, inputs, out_shapes, sems, start, finish, aliases=None, middle=None):
        self.inputs, self.out_shapes, self.sems = list(inputs), list(out_shapes), list(sems)
        self.start, self.finish, self.aliases = start, finish, dict(aliases or {})
        self.middle = middle


def _call(body, *, name, grid, in_specs, out_specs, out_shape, args, scratch=(), sem=(), exchanges=()):
    in_specs, out_specs, out_shape, scratch = list(in_specs), list(out_specs), list(out_shape), list(scratch)
    if not exchanges:
        return pl.pallas_call(body, name=name, grid=grid, in_specs=in_specs, out_specs=out_specs,
                              out_shape=out_shape, scratch_shapes=scratch, compiler_params=_cp(*sem))(*args)
    n_in, n_out, n_scr = len(in_specs), len(out_specs), len(scratch)
    x_in = [a for ex in exchanges for a in ex.inputs]
    x_out = [s for ex in exchanges for s in ex.out_shapes]
    x_sem = [s for ex in exchanges for s in ex.sems]
    aliases, i_off, o_off = {}, n_in, n_out
    for ex in exchanges:
        for i, o in ex.aliases.items():
            aliases[i_off + i] = o_off + o
        i_off += len(ex.inputs)
        o_off += len(ex.out_shapes)

    def split(flat):
        out, pos = [], 0
        for ex, n in zip(exchanges, flat[1]):
            out.append(flat[0][pos:pos + n])
            pos += n
        return out

    def carrier(*refs):
        pos = 0
        groups = []
        for n in (n_in, len(x_in), n_out, len(x_out), n_scr, len(x_sem)):
            groups.append(refs[pos:pos + n])
            pos += n
        ins, xin, outs, xout, scr, xsem = groups
        xin = split((xin, [len(ex.inputs) for ex in exchanges]))
        xout = split((xout, [len(ex.out_shapes) for ex in exchanges]))
        xsem = split((xsem, [len(ex.sems) for ex in exchanges]))
        first = pl.program_id(0) == 0
        last = pl.program_id(0) == grid[0] - 1
        for d in range(1, len(grid)):
            first = jnp.logical_and(first, pl.program_id(d) == 0)
            last = jnp.logical_and(last, pl.program_id(d) == grid[d] - 1)

        @pl.when(first)
        def _():
            for ex, i, o, s in zip(exchanges, xin, xout, xsem):
                ex.start(i, o, s)

        if any(ex.middle for ex in exchanges):
            half = pl.program_id(0) == grid[0] // 2
            for d in range(1, len(grid)):
                half = jnp.logical_and(half, pl.program_id(d) == 0)

            @pl.when(half)
            def _():
                for ex, i, o, s in zip(exchanges, xin, xout, xsem):
                    if ex.middle:
                        ex.middle(i, o, s)

        body(*ins, *outs, *scr)

        @pl.when(last)
        def _():
            for ex, i, o, s in zip(exchanges, xin, xout, xsem):
                ex.finish(i, o, s)

    res = pl.pallas_call(
        carrier, name=name, grid=grid, in_specs=in_specs + [ANY] * len(x_in),
        out_specs=out_specs + [ANY] * len(x_out), out_shape=out_shape + x_out,
        scratch_shapes=scratch + x_sem, input_output_aliases=aliases,
        compiler_params=_cp(*(["arbitrary"] * len(grid))),
    )(*args, *x_in)
    return res[:n_out], split((res[n_out:], [len(ex.out_shapes) for ex in exchanges]))


def _alone(name, *exchanges):
    n_in = [len(ex.inputs) for ex in exchanges]
    n_out = [len(ex.out_shapes) for ex in exchanges]
    n_sem = [len(ex.sems) for ex in exchanges]
    aliases, i_off, o_off = {}, 0, 0
    for ex in exchanges:
        for i, o in ex.aliases.items():
            aliases[i_off + i] = o_off + o
        i_off += len(ex.inputs)
        o_off += len(ex.out_shapes)

    def split(flat, counts):
        out, pos = [], 0
        for n in counts:
            out.append(flat[pos:pos + n])
            pos += n
        return out

    def body(*refs):
        ins, outs, sems = split(refs, [sum(n_in), sum(n_out), sum(n_sem)])
        groups = list(zip(exchanges, split(ins, n_in), split(outs, n_out), split(sems, n_sem)))
        for ex, i, o, s in groups:
            ex.start(i, o, s)
        for ex, i, o, s in groups:
            if ex.middle:
                ex.middle(i, o, s)
        for ex, i, o, s in groups:
            ex.finish(i, o, s)

    res = pl.pallas_call(
        body, name=name, in_specs=[ANY] * sum(n_in), out_specs=[ANY] * sum(n_out),
        out_shape=[s for ex in exchanges for s in ex.out_shapes],
        scratch_shapes=[s for ex in exchanges for s in ex.sems], input_output_aliases=aliases,
    )(*[a for ex in exchanges for a in ex.inputs])
    return split(res, n_out)


def _place():
    x, y, c = lax.axis_index("x"), lax.axis_index("y"), lax.axis_index("c")
    chips = [(1 - x, y), (x, 1 - y), (1 - x, 1 - y)]
    return x, y, c, chips


def _remote(src, dst, send, recv, to):
    return pltpu.make_async_remote_copy(src_ref=src, dst_ref=dst, send_sem=send, recv_sem=recv,
                                        device_id=to, device_id_type=MESH)


def _ex_gather(shards):
    nw = len(shards)
    hrs = [s.shape[0] // 2 for s in shards]

    def copies(ins, outs, sems):
        s1, r1, s2, r2, fs, fr = sems
        x, y, c, _ = _place()
        me, xn, yn, dg = (x, y), (1 - x, y), (x, 1 - y), (1 - x, 1 - y)
        nbr = (xn, yn)
        sibling = (x, y, 1 - c)

        def piece(w, chip, core, part=None):
            hr = hrs[w]
            rows = pl.ds(core * hr, hr) if part is None else pl.ds(core * hr + part * (hr // 2), hr // 2)
            return outs[w].at[2 * chip[0] + chip[1], rows]

        def first(w, k):
            return _remote(ins[w].at[pl.ds(c * hrs[w], hrs[w])], piece(w, me, c), s1.at[w, k], r1.at[w, k],
                           (*nbr[k], c))

        def landed(w, k):
            return _remote(piece(w, nbr[k], c), piece(w, nbr[k], c), s1.at[w, k], r1.at[w, k], (*nbr[k], c))

        def onward(w, k):
            return _remote(piece(w, nbr[k], c, k), piece(w, nbr[k], c, k), s2.at[w, k], r2.at[w, k],
                           (*nbr[1 - k], c))

        def arrived(w, k):
            return _remote(piece(w, dg, c, k), piece(w, dg, c, k), s2.at[w, k], r2.at[w, k], (*nbr[1 - k], c))

        def passed(w, j):
            chip = (xn, yn, dg)[j]
            return _remote(piece(w, chip, c), piece(w, chip, c), fs.at[w, j], fr.at[w, j], sibling)

        def handed(w, j):
            chip = (xn, yn, dg)[j]
            return _remote(piece(w, chip, 1 - c), piece(w, chip, 1 - c), fs.at[w, j], fr.at[w, j], sibling)

        return first, landed, onward, arrived, passed, handed

    def start(ins, outs, sems):
        first = copies(ins, outs, sems)[0]
        for w in range(nw):
            for k in range(2):
                first(w, k).start()

    def middle(ins, outs, sems):
        _, landed, onward, _, passed, _ = copies(ins, outs, sems)
        for w in range(nw):
            for k in range(2):
                landed(w, k).wait_recv()
                onward(w, k).start()
                passed(w, k).start()

    def finish(ins, outs, sems):
        first, _, onward, arrived, passed, handed = copies(ins, outs, sems)
        for w in range(nw):
            for k in range(2):
                arrived(w, k).wait_recv()
            passed(w, 2).start()
        for w in range(nw):
            for j in range(3):
                handed(w, j).wait_recv()
        for w in range(nw):
            for k in range(2):
                first(w, k).wait_send()
                onward(w, k).wait_send()
            for j in range(3):
                passed(w, j).wait_send()

    return _Exchange(shards, [_sds((N_CHIPS,) + s.shape, s.dtype) for s in shards],
                     [pltpu.SemaphoreType.DMA((nw, 2))] * 4 + [pltpu.SemaphoreType.DMA((nw, 3))] * 2,
                     start, finish, middle=middle)


def _ex_pair(grads):
    nw = len(grads)

    def copies(ins, outs, sems):
        x, y, c, _ = _place()
        out = []
        for w in range(nw):
            hr = grads[w].shape[1] // 2
            out.append(_remote(ins[w].at[:, pl.ds((1 - c) * hr, hr)], outs[w], sems[0].at[w], sems[1].at[w],
                               (x, y, 1 - c)))
        return out

    def start(ins, outs, sems):
        for cp in copies(ins, outs, sems):
            cp.start()

    def finish(ins, outs, sems):
        for cp in copies(ins, outs, sems):
            cp.wait()

    return _Exchange(grads, [_sds((N_CHIPS, g.shape[1] // 2, g.shape[2]), F32) for g in grads],
                     [pltpu.SemaphoreType.DMA((nw,))] * 2, start, finish)


def _ex_chip(pieces):
    nw = len(pieces)

    def copies(ins, outs, sems):
        x, y, c, chips = _place()
        return [_remote(ins[w].at[2 * cx + cy], outs[w].at[k], sems[0].at[w, k], sems[1].at[w, k], (cx, cy, c))
                for w in range(nw) for k, (cx, cy) in enumerate(chips)]

    def start(ins, outs, sems):
        for cp in copies(ins, outs, sems):
            cp.start()

    def finish(ins, outs, sems):
        for cp in copies(ins, outs, sems):
            cp.wait()

    return _Exchange(pieces, [_sds((3,) + p.shape[1:], BF16) for p in pieces],
                     [pltpu.SemaphoreType.DMA((nw, 3))] * 2, start, finish)


def _ex_swap(fulls):
    nw = len(fulls)

    def start(ins, outs, sems):
        x, y, c, _ = _place()
        for w in range(nw):
            hr = fulls[w].shape[0] // 2
            mine = pl.ds(c * hr, hr)
            _remote(ins[w].at[mine], outs[w].at[mine], sems[0].at[w], sems[1].at[w], (x, y, 1 - c)).start()

    def finish(ins, outs, sems):
        x, y, c, _ = _place()
        for w in range(nw):
            hr = fulls[w].shape[0] // 2
            mine, theirs = pl.ds(c * hr, hr), pl.ds((1 - c) * hr, hr)
            _remote(ins[w].at[mine], outs[w].at[mine], sems[0].at[w], sems[1].at[w], (x, y, 1 - c)).wait_send()
            _remote(ins[w].at[theirs], outs[w].at[theirs], sems[0].at[w], sems[1].at[w], (x, y, 1 - c)).wait_recv()

    return _Exchange(fulls, [_sds(f.shape, F32) for f in fulls], [pltpu.SemaphoreType.DMA((nw,))] * 2,
                     start, finish, aliases={w: w for w in range(nw)})


def _ex_allgather(blocks):
    nb = len(blocks)

    def copies(ins, outs, sems):
        send, recv, lsem = sems
        x, y, c, chips = _place()
        me, sibling = (x, y, c), (x, y, 1 - c)

        def rows(b, px, py, pc):
            m_per = blocks[b].shape[0]
            return outs[b].at[pl.ds((4 * px + 2 * py + pc) * m_per, m_per), :]

        def copy(b, k, blk, to, src=None):
            return _remote(rows(b, *blk) if src is None else src, rows(b, *blk), send.at[b, k], recv.at[b, k], to)

        def mine(b):
            return pltpu.make_async_copy(ins[b], rows(b, *me), lsem.at[b])

        def first(b, k):
            return copy(b, k, me, sibling if k == 0 else (*chips[k - 1], c), src=ins[b])

        def passed(b, j):
            return copy(b, 4 + j, (*chips[j], c), sibling)

        def landed(b, j):
            return copy(b, 1 + j, (*chips[j], c), me)

        def handed(b, k):
            return copy(b, 0, sibling, me) if k == 0 else copy(b, 3 + k, (*chips[k - 1], 1 - c), me)

        return mine, first, passed, landed, handed

    def start(ins, outs, sems):
        mine, first, _, _, _ = copies(ins, outs, sems)
        for b in range(nb):
            mine(b).start()
            for k in range(4):
                first(b, k).start()

    def finish(ins, outs, sems):
        mine, first, passed, landed, handed = copies(ins, outs, sems)
        sent = []
        for b in range(nb):
            for j in range(3):
                landed(b, j).wait_recv()
                cp = passed(b, j)
                cp.start()
                sent.append(cp)
        for b in range(nb):
            for k in range(4):
                handed(b, k).wait_recv()
            for k in range(4):
                first(b, k).wait_send()
        for cp in sent:
            cp.wait_send()
        for b in range(nb):
            mine(b).wait()

    return _Exchange(blocks, [_sds((N_DEV * b.shape[0], b.shape[1]), F32) for b in blocks],
                     [pltpu.SemaphoreType.DMA((nb, 7)), pltpu.SemaphoreType.DMA((nb, 7)), pltpu.SemaphoreType.DMA((nb,))],
                     start, finish)


def _inproj(x2, g1, w_in_t, b_in, tabs, seq, exchanges=()):
    T = x2.shape[0]
    tm = min(TM, seq)
    nseq = seq // tm

    def body(x_ref, g_ref, w_ref, b_ref, c_ref, a_ref, bt_ref, h_ref, u_ref, q_ref, k_ref, v_ref, gate_ref):
        x = x_ref[...]
        h = (x * _rms(x) * g_ref[...]).astype(BF16)
        h_ref[...] = h

        def proj(lo, hi):
            return _dot_nt(h, w_ref[lo:hi, :]) + b_ref[:, lo:hi]

        c, a, bt = c_ref[...], a_ref[...], bt_ref[...]
        u_ref[...] = proj(0, C_Q)
        q = proj(C_Q, C_K)
        for p in range(4):
            sl = slice(LANES * p, LANES * (p + 1))
            q_ref[:, sl] = (_rot_fwd(q[:, sl], c, a, bt) * SCALE).astype(BF16)
        kv = proj(C_K, C_G)
        k_ref[...] = _rot_fwd(kv[:, :KV_WIDTH], c, a, bt).astype(BF16)
        v_ref[...] = kv[:, KV_WIDTH:].astype(BF16)
        for j in range(2):
            lo = C_G + D_MODEL * j
            gate_ref[:, D_MODEL * j:D_MODEL * (j + 1)] = jax.nn.sigmoid(proj(lo, lo + D_MODEL)).astype(BF16)

    tab = pl.BlockSpec((tm, LANES), lambda i: (i % nseq, 0))
    return _call(
        body, name="inproj", grid=(T // tm,),
        in_specs=[_rows(tm, D_MODEL), _const((1, D_MODEL)), _const((IN_WIDTH, D_MODEL)), _const((1, IN_WIDTH)),
                  tab, tab, tab],
        out_specs=[_rows(tm, D_MODEL), _rows(tm, POOL_WIDTH), _rows(tm, ATTN_WIDTH), _rows(tm, KV_WIDTH),
                   _rows(tm, KV_WIDTH), _rows(tm, GATE_WIDTH)],
        out_shape=[_sds((T, D_MODEL), BF16), _sds((T, POOL_WIDTH), F32), _sds((T, ATTN_WIDTH), BF16),
                   _sds((T, KV_WIDTH), BF16), _sds((T, KV_WIDTH), BF16), _sds((T, GATE_WIDTH), BF16)],
        args=(x2, g1, w_in_t, b_in, *tabs), sem=("parallel",), exchanges=exchanges)


def _inv_count(pos, w):
    return 1.0 / jnp.minimum(pos + 1, w).astype(F32)


def _pool_fwd(u, w_pool, pool_scale, seq):
    T = u.shape[0]
    tp = min(TP, seq)
    nseq = seq // tp
    per = tp // HALO

    def body(u_ref, prev_ref, w_ref, s_ref, diff_ref, y_ref):
        i = pl.program_id(0)
        first = (i % nseq) == 0
        prev = jnp.where(first, 0.0, prev_ref[...])
        ext = jnp.concatenate([prev, u_ref[...]], axis=0)
        pos = (i % nseq) * tp + lax.broadcasted_iota(jnp.int32, (tp, 1), 0)
        for gi, w in enumerate(POOL_WINDOWS):
            sl = slice(POOL_GC * gi, POOL_GC * (gi + 1))
            xg = ext[:, sl]
            s = xg
            sh = 1
            while sh < w:
                s = s + pltpu.roll(s, sh, 0)
                sh *= 2
            pooled = s[HALO:] * _inv_count(pos, w)
            diff = (pooled - xg[HALO:]).astype(BF16)
            diff_ref[:, sl] = diff
            mixed = _dot(diff, w_ref[gi].astype(BF16))
            y_ref[:, sl] = (mixed * s_ref[:, sl]).astype(BF16)

    return _call(
        body, name="pool_fwd", grid=(T // tp,),
        in_specs=[_rows(tp, POOL_WIDTH),
                  pl.BlockSpec((HALO, POOL_WIDTH), lambda i: (jnp.maximum(i * per - 1, 0), 0)),
                  _const((4, POOL_GC, POOL_GC)), _const((1, POOL_WIDTH))],
        out_specs=[_rows(tp, POOL_WIDTH), _rows(tp, POOL_WIDTH)],
        out_shape=[_sds((T, POOL_WIDTH), BF16), _sds((T, POOL_WIDTH), BF16)],
        args=(u, u, w_pool, pool_scale), sem=("parallel",))


GROUP = 4
GROWS = GROUP * BLOCK


def _attn_masks(n):
    qi = lax.broadcasted_iota(jnp.int32, (GROWS, 2 * BLOCK), 0) % BLOCK
    kj = lax.broadcasted_iota(jnp.int32, (GROWS, 2 * BLOCK), 1)
    rel = qi + BLOCK - kj
    valid = (rel >= 0) & (rel < BLOCK) & (kj >= jnp.where(n > 0, 0, BLOCK))
    lo = lax.broadcasted_iota(jnp.int32, (BLOCK, LANES), 1) < HEAD_DIM
    return valid, lo


def _by_example(bl, *arrays):
    return [a.reshape(bl, a.shape[0] // bl, a.shape[1]) for a in arrays]


def _stack_heads(ref, h, lo):
    keep = lo if h == 0 else jnp.logical_not(lo)
    pieces = []
    for p in (2 * h, 2 * h + 1):
        xp = ref[:, LANES * p:LANES * (p + 1)].astype(F32)
        for e in range(2):
            t = xp if e == h else pltpu.roll(xp, HEAD_DIM, 1)
            pieces.append(jnp.where(keep, t, 0.0).astype(BF16))
    return jnp.concatenate(pieces, axis=0)


def _unstack_heads(stacked, h, lo):
    pairs = []
    for j in range(2):
        parts = []
        for e in range(2):
            t = stacked[BLOCK * (2 * j + e):BLOCK * (2 * j + e + 1)]
            parts.append(t if e == h else pltpu.roll(t, HEAD_DIM, 1))
        pairs.append(jnp.where(lo, parts[0], parts[1]))
    return pairs


def _sink_rows(sink_ref, h):
    head = lax.broadcasted_iota(jnp.int32, (GROWS, 1), 0) // BLOCK
    col = jnp.zeros((GROWS, 1), F32) + sink_ref[GROUP * h]
    for g in range(1, GROUP):
        col = jnp.where(head == g, sink_ref[GROUP * h + g], col)
    return col


def _group_probs(qs, kk, valid, sink):
    s = jnp.where(valid, _dot_nt(qs, kk), NEG_INF)
    m = jnp.maximum(jnp.max(s, axis=1, keepdims=True), sink)
    ex = jnp.exp(s - m)
    es = jnp.exp(sink - m)
    inv = 1.0 / (jnp.sum(ex, axis=1, keepdims=True) + es)
    return ex * inv, es * inv


def _attn_fwd(q, k, v, sinks, seq, exchanges=()):
    T = q.shape[0]
    nb = seq // BLOCK
    bl = T // seq

    def body(sink_ref, q_ref, kp_ref, kc_ref, vp_ref, vc_ref, o_ref):
        valid, lo = _attn_masks(pl.program_id(0))
        for b in range(bl):
            kk = jnp.concatenate([kp_ref[b], kc_ref[b]], axis=0)
            vv = jnp.concatenate([vp_ref[b], vc_ref[b]], axis=0)
            for h in range(2):
                qs = _stack_heads(q_ref.at[b], h, lo)
                pr, _ = _group_probs(qs, kk, valid, _sink_rows(sink_ref, h))
                o = _dot(pr.astype(BF16), vv)
                for j, pair in enumerate(_unstack_heads(o, h, lo)):
                    p = 2 * h + j
                    o_ref[b, :, LANES * p:LANES * (p + 1)] = pair.astype(BF16)

    cur = lambda n: (0, n, 0)
    prv = lambda n: (0, jnp.maximum(n - 1, 0), 0)
    kv = lambda m: pl.BlockSpec((bl, BLOCK, KV_WIDTH), m)
    res = _call(
        body, name="attn_fwd", grid=(nb,),
        in_specs=[pl.BlockSpec(memory_space=pltpu.SMEM), pl.BlockSpec((bl, BLOCK, ATTN_WIDTH), cur),
                  kv(prv), kv(cur), kv(prv), kv(cur)],
        out_specs=[pl.BlockSpec((bl, BLOCK, ATTN_WIDTH), cur)],
        out_shape=[_sds((bl, seq, ATTN_WIDTH), BF16)],
        args=(sinks, *_by_example(bl, q, k, k, v, v)), sem=("parallel",), exchanges=exchanges)
    if exchanges:
        return [res[0][0].reshape(T, ATTN_WIDTH)], res[1]
    return [res[0].reshape(T, ATTN_WIDTH)]


def _branch(y, w_ref):
    return jnp.concatenate([_dot(y, w_ref[j]) for j in range(N_CHIPS)], axis=1)


def _merge_out(y_pool, y_attn, gate, x2, w_bp, w_ba, w_out, g2, g3, exchanges=()):
    T = x2.shape[0]
    tm = min(TM, T)

    def body(yp_ref, ya_ref, gate_ref, x_ref, wbp_ref, wba_ref, wo_ref, g2_ref, g3_ref,
             mg_ref, mix_ref, x1_ref, h2_ref):
        bp, ba = _branch(yp_ref[...], wbp_ref), _branch(ya_ref[...], wba_ref)
        merged = (gate_ref[:, :D_MODEL].astype(F32) * bp + gate_ref[:, D_MODEL:].astype(F32) * ba).astype(BF16)
        mg_ref[...] = merged
        mix = _dot(merged, wo_ref[...])
        mix_ref[...] = mix
        x1 = x_ref[...] + mix * _rms(mix) * g2_ref[...]
        x1_ref[...] = x1
        h2_ref[...] = (x1 * _rms(x1) * g3_ref[...]).astype(BF16)

    return _call(
        body, name="merge_out", grid=(T // tm,),
        in_specs=[_rows(tm, POOL_WIDTH), _rows(tm, ATTN_WIDTH), _rows(tm, GATE_WIDTH), _rows(tm, D_MODEL),
                  _const(w_bp.shape), _const(w_ba.shape), _const((D_MODEL, D_MODEL)),
                  _const((1, D_MODEL)), _const((1, D_MODEL))],
        out_specs=[_rows(tm, D_MODEL)] * 4,
        out_shape=[_sds((T, D_MODEL), BF16), _sds((T, D_MODEL), F32), _sds((T, D_MODEL), F32),
                   _sds((T, D_MODEL), BF16)],
        args=(y_pool, y_attn, gate, x2, w_bp, w_ba, w_out, g2, g3), sem=("parallel",), exchanges=exchanges)


HALF = D_MODEL // 2
_HALVES = _const((N_CHIPS, HALF, D_MODEL))


def _mlp_up(h2, w_up, exchanges=()):
    T = h2.shape[0]
    tm = min(TM, T)

    def body(h_ref, wa_ref, wb_ref, up_ref, a_ref):
        ha, hb = h_ref[:, :HALF], h_ref[:, HALF:]
        for j in range(N_CHIPS):
            sl = slice(D_MODEL * j, D_MODEL * (j + 1))
            up = _dot(ha, wa_ref[j]) + _dot(hb, wb_ref[j])
            up_ref[:, sl] = up.astype(BF16)
            a_ref[:, sl] = jnp.square(jnp.maximum(up, 0.0)).astype(BF16)

    return _call(
        body, name="mlp_up", grid=(T // tm,),
        in_specs=[_rows(tm, D_MODEL), _HALVES, _HALVES],
        out_specs=[_rows(tm, D_FF), _rows(tm, D_FF)],
        out_shape=[_sds((T, D_FF), BF16), _sds((T, D_FF), BF16)],
        args=(h2, *w_up), sem=("parallel",), exchanges=exchanges)


def _mlp_down_loss(a, x1, tgt, w_down, g4):
    T = a.shape[0]
    tm = min(TM, T)

    def body(a_ref, x1_ref, t_ref, wa_ref, wb_ref, g_ref, dff_ref, dy_ref, loss_ref, dg_ref):
        @pl.when(pl.program_id(0) == 0)
        def _():
            loss_ref[...] = jnp.zeros_like(loss_ref)
            dg_ref[...] = jnp.zeros_like(dg_ref)

        ff = None
        for j in range(N_CHIPS):
            lo = D_MODEL * j
            t = _dot(a_ref[:, lo:lo + HALF], wa_ref[j]) + _dot(a_ref[:, lo + HALF:lo + D_MODEL], wb_ref[j])
            ff = t if ff is None else ff + t
        g = g_ref[...]
        err = x1_ref[...] + ff * _rms(ff) * g - t_ref[...]
        loss_ref[...] += jnp.sum(err * err) * (0.5 / D_MODEL)
        dy = err * (1.0 / D_MODEL)
        dy_ref[...] = dy
        dff, dg = _norm_bwd(ff, g, dy)
        dff_ref[...] = dff.astype(BF16)
        dg_ref[...] += dg

    return _call(
        body, name="mlp_down_loss", grid=(T // tm,),
        in_specs=[_rows(tm, D_FF), _rows(tm, D_MODEL), _rows(tm, D_MODEL), _HALVES, _HALVES, _const((1, D_MODEL))],
        out_specs=[_rows(tm, D_MODEL), _rows(tm, D_MODEL), _const((8, LANES)), _const((1, D_MODEL))],
        out_shape=[_sds((T, D_MODEL), BF16), _sds((T, D_MODEL), F32), _sds((8, LANES), F32),
                   _sds((1, D_MODEL), F32)],
        args=(a, x1, tgt, *w_down, g4), sem=("arbitrary",))


def _mlp_down_bwd(dff, up, w_down):
    T = dff.shape[0]
    tm = min(TM, T)

    def body(d_ref, up_ref, wa_ref, wb_ref, dup_ref):
        d = d_ref[...]
        for j in range(N_CHIPS):
            for part, w_ref in enumerate((wa_ref, wb_ref)):
                lo = D_MODEL * j + HALF * part
                da = _dot_nt(d, w_ref[j])
                relu = jnp.maximum(up_ref[:, lo:lo + HALF].astype(F32), 0.0)
                dup_ref[:, lo:lo + HALF] = (da * (2.0 * relu)).astype(BF16)

    return _call(
        body, name="mlp_down_bwd", grid=(T // tm,),
        in_specs=[_rows(tm, D_MODEL), _rows(tm, D_FF), _HALVES, _HALVES],
        out_specs=[_rows(tm, D_FF)],
        out_shape=[_sds((T, D_FF), BF16)],
        args=(dff, up, *w_down), sem=("parallel",))[0]


def _mlp_up_bwd(dup, dy, x1, mix, w_up, g3, g2, exchanges=()):
    T = dup.shape[0]
    tm = min(TM, T)

    def body(dup_ref, dy_ref, x1_ref, mix_ref, wa_ref, wb_ref, g3_ref, g2_ref, dx1_ref, dmix_ref, dg3_ref, dg2_ref):
        @pl.when(pl.program_id(0) == 0)
        def _():
            dg3_ref[...] = jnp.zeros_like(dg3_ref)
            dg2_ref[...] = jnp.zeros_like(dg2_ref)

        def back(w_ref):
            acc = _dot_nt(dup_ref[:, :D_MODEL], w_ref[0])
            for j in range(1, N_CHIPS):
                acc = acc + _dot_nt(dup_ref[:, D_MODEL * j:D_MODEL * (j + 1)], w_ref[j])
            return acc

        dh2 = jnp.concatenate([back(wa_ref), back(wb_ref)], axis=1)
        dx, dg3 = _norm_bwd(x1_ref[...], g3_ref[...], dh2)
        dx1 = dy_ref[...] + dx
        dx1_ref[...] = dx1
        dg3_ref[...] += dg3
        dmix, dg2 = _norm_bwd(mix_ref[...], g2_ref[...], dx1)
        dmix_ref[...] = dmix.astype(BF16)
        dg2_ref[...] += dg2

    return _call(
        body, name="mlp_up_bwd", grid=(T // tm,),
        in_specs=[_rows(tm, D_FF), _rows(tm, D_MODEL), _rows(tm, D_MODEL), _rows(tm, D_MODEL),
                  _HALVES, _HALVES, _const((1, D_MODEL)), _const((1, D_MODEL))],
        out_specs=[_rows(tm, D_MODEL), _rows(tm, D_MODEL), _const((1, D_MODEL)), _const((1, D_MODEL))],
        out_shape=[_sds((T, D_MODEL), F32), _sds((T, D_MODEL), BF16), _sds((1, D_MODEL), F32),
                   _sds((1, D_MODEL), F32)],
        args=(dup, dy, x1, mix, *w_up, g3, g2), sem=("arbitrary",), exchanges=exchanges)


def _dw(tag, a, g, ta, tn, shard_cols=False, exchanges=()):
    T, ka = a.shape
    n = g.shape[1]
    tk = min(2 * TM, T)
    nk = T // tk

    def body(a_ref, g_ref, o_ref):
        @pl.when(pl.program_id(2) == 0)
        def _():
            o_ref[...] = jnp.zeros_like(o_ref)

        o_ref[...] += _dot_tn(a_ref[...], g_ref[...])

    if shard_cols:
        per = (n // N_CHIPS) // tn
        out_spec = pl.BlockSpec((None, ta, tn), lambda i, j, k: (j // per, i, j % per))
        out_shape = _sds((N_CHIPS, ka, n // N_CHIPS), F32)
    else:
        out_spec = pl.BlockSpec((ta, tn), lambda i, j, k: (i, j))
        out_shape = _sds((ka, n), F32)
    return _call(
        body, name="dw_" + tag, grid=(ka // ta, n // tn, nk),
        in_specs=[pl.BlockSpec((tk, ta), lambda i, j, k: (k, i)), pl.BlockSpec((tk, tn), lambda i, j, k: (k, j))],
        out_specs=[out_spec], out_shape=[out_shape],
        args=(a, g), sem=("parallel", "parallel", "arbitrary"), exchanges=exchanges)


def _dw_slabs(tag, a, g):
    T, ka = a.shape
    n = g.shape[1]
    c = n // N_CHIPS
    tk = min(TM, T)

    def body(a_ref, g_ref, o_ref):
        @pl.when(pl.program_id(0) == 0)
        def _():
            o_ref[...] = jnp.zeros_like(o_ref)

        res = _dot_tn(a_ref[...], g_ref[...])
        for j in range(N_CHIPS):
            o_ref[j] += res[:, c * j:c * (j + 1)]

    return _call(
        body, name="dw_" + tag, grid=(T // tk,),
        in_specs=[_rows(tk, ka), _rows(tk, n)],
        out_specs=[_const((N_CHIPS, ka, c))], out_shape=[_sds((N_CHIPS, ka, c), F32)],
        args=(a, g), sem=("arbitrary",))[0]


def _merge_bwd(dmix, gate, y_pool, y_attn, w_out, w_bp, w_ba, exchanges=()):
    T = dmix.shape[0]
    tm = min(TM, T)

    def body(dmix_ref, gate_ref, yp_ref, ya_ref, wo_ref, wbp_ref, wba_ref,
             dbp_ref, dba_ref, dgate_ref, dyp_ref, dya_ref):
        dm = _dot_nt(dmix_ref[...], wo_ref[...])
        for j, (y_ref, db_ref, w_ref, dy_ref) in enumerate(
                ((yp_ref, dbp_ref, wbp_ref, dyp_ref), (ya_ref, dba_ref, wba_ref, dya_ref))):
            sl = slice(D_MODEL * j, D_MODEL * (j + 1))
            gt = gate_ref[:, sl].astype(F32)
            db = (dm * gt).astype(BF16)
            db_ref[...] = db
            dgate_ref[:, sl] = (dm * _branch(y_ref[...], w_ref) * gt * (1.0 - gt)).astype(BF16)
            cw = D_MODEL // N_CHIPS
            dy = _dot_nt(db[:, :cw], w_ref[0])
            for c in range(1, N_CHIPS):
                dy = dy + _dot_nt(db[:, cw * c:cw * (c + 1)], w_ref[c])
            dy_ref[...] = dy.astype(dy_ref.dtype)

    return _call(
        body, name="merge_bwd", grid=(T // tm,),
        in_specs=[_rows(tm, D_MODEL), _rows(tm, GATE_WIDTH), _rows(tm, POOL_WIDTH), _rows(tm, ATTN_WIDTH),
                  _const((D_MODEL, D_MODEL)), _const(w_bp.shape), _const(w_ba.shape)],
        out_specs=[_rows(tm, D_MODEL), _rows(tm, D_MODEL), _rows(tm, GATE_WIDTH), _rows(tm, POOL_WIDTH),
                   _rows(tm, ATTN_WIDTH)],
        out_shape=[_sds((T, D_MODEL), BF16), _sds((T, D_MODEL), BF16), _sds((T, GATE_WIDTH), BF16),
                   _sds((T, POOL_WIDTH), F32), _sds((T, ATTN_WIDTH), BF16)],
        args=(dmix, gate, y_pool, y_attn, w_out, w_bp, w_ba), sem=("parallel",), exchanges=exchanges)


def _attn_bwd(q, k, v, do, sinks, tabs, seq, exchanges=()):
    T = q.shape[0]
    nb = seq // BLOCK
    bl = T // seq
    steps = nb + 1

    def body(sink_ref, q_ref, do_ref, kp_ref, kc_ref, vp_ref, vc_ref, c_ref, a_ref, bt_ref, cp_ref, ap_ref, btp_ref,
             dq_ref, dk_ref, dv_ref, dsink_ref, ck_ref, cv_ref):
        n = pl.program_id(0)

        @pl.when(n == 0)
        def _():
            dsink_ref[...] = jnp.zeros_like(dsink_ref)
            ck_ref[...] = jnp.zeros_like(ck_ref)
            cv_ref[...] = jnp.zeros_like(cv_ref)

        @pl.when(n < nb)
        def _():
            valid, lo = _attn_masks(n)
            for b in range(bl):
                kk = jnp.concatenate([kp_ref[b], kc_ref[b]], axis=0)
                vv = jnp.concatenate([vp_ref[b], vc_ref[b]], axis=0)
                dk_acc = jnp.zeros((2 * BLOCK, KV_WIDTH), F32)
                dv_acc = jnp.zeros((2 * BLOCK, KV_WIDTH), F32)
                for h in range(2):
                    qs = _stack_heads(q_ref.at[b], h, lo)
                    dos = _stack_heads(do_ref.at[b], h, lo)
                    pr, ps = _group_probs(qs, kk, valid, _sink_rows(sink_ref, h))
                    dp = _dot_nt(dos, vv)
                    delta = jnp.sum(pr * dp, axis=1, keepdims=True)
                    ds = (pr * (dp - delta)).astype(BF16)
                    dsk = ps * delta
                    for g in range(GROUP):
                        idx = GROUP * h + g
                        dsink_ref[idx:idx + 1, :] += (jnp.zeros((1, LANES), F32)
                                                      - jnp.sum(dsk[BLOCK * g:BLOCK * (g + 1)]))
                    dk_acc = dk_acc + _dot_tn(ds, qs)
                    dv_acc = dv_acc + _dot_tn(pr.astype(BF16), dos)
                    for j, pair in enumerate(_unstack_heads(_dot(ds, kk) * SCALE, h, lo)):
                        sl = slice(LANES * (2 * h + j), LANES * (2 * h + j + 1))
                        dq_ref[b, :, sl] = _rot_bwd(pair, c_ref[...], a_ref[...], bt_ref[...]).astype(BF16)
                fin_k = ck_ref[b] + dk_acc[:BLOCK]
                dk_ref[b] = _rot_bwd(fin_k, cp_ref[...], ap_ref[...], btp_ref[...]).astype(BF16)
                dv_ref[b] = (cv_ref[b] + dv_acc[:BLOCK]).astype(BF16)
                ck_ref[b] = dk_acc[BLOCK:]
                cv_ref[b] = dv_acc[BLOCK:]

        @pl.when(n == nb)
        def _():
            for b in range(bl):
                dk_ref[b] = _rot_bwd(ck_ref[b], cp_ref[...], ap_ref[...], btp_ref[...]).astype(BF16)
                dv_ref[b] = cv_ref[b].astype(BF16)

    cur = lambda n: (0, jnp.minimum(n, nb - 1), 0)
    prv = lambda n: (0, jnp.clip(n - 1, 0, nb - 1), 0)
    tcur = lambda n: (jnp.minimum(n, nb - 1), 0)
    tprv = lambda n: (jnp.clip(n - 1, 0, nb - 1), 0)
    wide = lambda m: pl.BlockSpec((bl, BLOCK, ATTN_WIDTH), m)
    kv = lambda m: pl.BlockSpec((bl, BLOCK, KV_WIDTH), m)
    tab = lambda m: pl.BlockSpec((BLOCK, LANES), m)
    res = _call(
        body, name="attn_bwd", grid=(steps,),
        in_specs=[pl.BlockSpec(memory_space=pltpu.SMEM), wide(cur), wide(cur), kv(prv), kv(cur), kv(prv), kv(cur),
                  tab(tcur), tab(tcur), tab(tcur), tab(tprv), tab(tprv), tab(tprv)],
        out_specs=[wide(cur), kv(prv), kv(prv), _const((8, LANES))],
        out_shape=[_sds((bl, seq, ATTN_WIDTH), BF16), _sds((bl, seq, KV_WIDTH), BF16),
                   _sds((bl, seq, KV_WIDTH), BF16), _sds((8, LANES), F32)],
        scratch=[pltpu.VMEM((bl, BLOCK, KV_WIDTH), F32), pltpu.VMEM((bl, BLOCK, KV_WIDTH), F32)],
        args=(sinks, *_by_example(bl, q, do, k, k, v, v), *tabs, *tabs), sem=("arbitrary",), exchanges=exchanges)
    outs, rest = (res if exchanges else (res, None))
    outs = [outs[0].reshape(T, ATTN_WIDTH), outs[1].reshape(T, KV_WIDTH), outs[2].reshape(T, KV_WIDTH), outs[3]]
    return (outs, rest) if exchanges else outs


def _pool_bwd(dyp, diff, w_pool, pool_scale, seq, exchanges=()):
    T = dyp.shape[0]
    tp = min(TP, seq)
    nseq = seq // tp
    per = tp // HALO
    last_halo = T // HALO - 1

    def body(dy_ref, nxt_ref, diff_ref, w_ref, s_ref, du_ref, dw_ref, ds_ref):
        i = pl.program_id(0)

        @pl.when(i == 0)
        def _():
            dw_ref[...] = jnp.zeros_like(dw_ref)
            ds_ref[...] = jnp.zeros_like(ds_ref)

        last = (i % nseq) == nseq - 1
        nxt = jnp.where(last, 0.0, nxt_ref[...])
        ext = jnp.concatenate([dy_ref[...], nxt], axis=0) * s_ref[...]
        pos = (i % nseq) * tp + lax.broadcasted_iota(jnp.int32, (tp + HALO, 1), 0)
        for gi, w in enumerate(POOL_WINDOWS):
            sl = slice(POOL_GC * gi, POOL_GC * (gi + 1))
            wg = w_ref[gi].astype(BF16)
            dmx = ext[:, sl].astype(BF16)
            ddiff = _dot_nt(dmx, wg)
            s = ddiff * _inv_count(pos, w)
            sh = 1
            while sh < w:
                s = s + pltpu.roll(s, tp + HALO - sh, 0)
                sh *= 2
            du_ref[:, sl] = (s[:tp] - ddiff[:tp]).astype(BF16)
            dg = diff_ref[:, sl]
            dw_ref[gi] += _dot_tn(dg, dmx[:tp])
            ds_ref[:, sl] += jnp.sum(dy_ref[:, sl] * _dot(dg, wg), axis=0, keepdims=True)

    return _call(
        body, name="pool_bwd", grid=(T // tp,),
        in_specs=[_rows(tp, POOL_WIDTH),
                  pl.BlockSpec((HALO, POOL_WIDTH), lambda i: (jnp.minimum((i + 1) * per, last_halo), 0)),
                  _rows(tp, POOL_WIDTH), _const((4, POOL_GC, POOL_GC)), _const((1, POOL_WIDTH))],
        out_specs=[_rows(tp, POOL_WIDTH), _const((4, POOL_GC, POOL_GC)), _const((1, POOL_WIDTH))],
        out_shape=[_sds((T, POOL_WIDTH), BF16), _sds((4, POOL_GC, POOL_GC), F32), _sds((1, POOL_WIDTH), F32)],
        args=(dyp, dyp, diff, w_pool, pool_scale), sem=("arbitrary",), exchanges=exchanges)


_PARTS = ((0, C_Q), (C_Q, C_K), (C_K, C_V), (C_V, C_G), (C_G, IN_WIDTH))


def _inproj_bwd(parts, x2, dx1, w_in_t, g1, exchanges=()):
    T = x2.shape[0]
    tm = min(TM, T)

    def body(du_ref, dq_ref, dk_ref, dv_ref, dgt_ref, x_ref, dx1_ref, w_ref, g_ref, gx_ref, dg_ref):
        @pl.when(pl.program_id(0) == 0)
        def _():
            dg_ref[...] = jnp.zeros_like(dg_ref)

        dh = jnp.zeros((tm, D_MODEL), F32)
        for (lo, hi), p_ref in zip(_PARTS, (du_ref, dq_ref, dk_ref, dv_ref, dgt_ref)):
            dh = dh + _dot(p_ref[...], w_ref[lo:hi, :])
        dx, dg = _norm_bwd(x_ref[...], g_ref[...], dh)
        gx_ref[...] = dx1_ref[...] + dx
        dg_ref[...] += dg

    return _call(
        body, name="inproj_bwd", grid=(T // tm,),
        in_specs=[_rows(tm, hi - lo) for lo, hi in _PARTS]
        + [_rows(tm, D_MODEL), _rows(tm, D_MODEL), _const((IN_WIDTH, D_MODEL)), _const((1, D_MODEL))],
        out_specs=[_rows(tm, D_MODEL), _const((1, D_MODEL))],
        out_shape=[_sds((T, D_MODEL), F32), _sds((1, D_MODEL), F32)],
        args=(*parts, x2, dx1, w_in_t, g1), sem=("arbitrary",), exchanges=exchanges)


def _dw_in(h, parts, exchanges=()):
    T = h.shape[0]
    tk = min(TM, T)

    def body(h_ref, du_ref, dq_ref, dk_ref, dv_ref, dgt_ref, o_ref, db_ref):
        @pl.when(pl.program_id(0) == 0)
        def _():
            o_ref[...] = jnp.zeros_like(o_ref)
            db_ref[...] = jnp.zeros_like(db_ref)

        hh = h_ref[...]
        for (lo, hi), p_ref in zip(_PARTS, (du_ref, dq_ref, dk_ref, dv_ref, dgt_ref)):
            part = p_ref[...]
            o_ref[lo:hi, :] += _dot_tn(part, hh)
            db_ref[:, lo:hi] += jnp.sum(part.astype(F32), axis=0, keepdims=True)

    return _call(
        body, name="dw_in", grid=(T // tk,),
        in_specs=[_rows(tk, D_MODEL)] + [_rows(tk, hi - lo) for lo, hi in _PARTS],
        out_specs=[_const((IN_WIDTH, D_MODEL)), _const((1, IN_WIDTH))],
        out_shape=[_sds((IN_WIDTH, D_MODEL), F32), _sds((1, IN_WIDTH), F32)],
        args=(h, *parts), sem=("arbitrary",), exchanges=exchanges)


def _row_tile(rows, cap=256, mult=16):
    best = None
    for t in range(mult, min(rows, cap) + 1, mult):
        if rows % t == 0:
            best = t
    if best is None:
        raise ValueError("no row tile for %d rows" % rows)
    return best


def _pair_sum(ids, full, got):
    _, r, c = full.shape
    hr = r // 2
    tr = _row_tile(hr)
    nblk = hr // tr

    def body(ids_ref, a_ref, b_ref, own_ref, sb_ref):
        s = a_ref[...] + b_ref[...]
        sb_ref[...] = s.astype(BF16)

        @pl.when(pl.program_id(1) == ids_ref[0])
        def _():
            own_ref[...] = s

    slab = pl.BlockSpec((None, tr, c), lambda i, j, ids_ref: (j, i, 0))
    return pl.pallas_call(
        body, name="pair_sum_%dx%d" % (r, c),
        grid_spec=pltpu.PrefetchScalarGridSpec(
            num_scalar_prefetch=1, grid=(nblk, N_CHIPS),
            in_specs=[pl.BlockSpec((None, tr, c), lambda i, j, ids_ref: (j, ids_ref[1] * nblk + i, 0)), slab],
            out_specs=[pl.BlockSpec((tr, c), lambda i, j, ids_ref: (i, 0)), slab]),
        out_shape=[_sds((hr, c), F32), _sds((N_CHIPS, hr, c), BF16)],
        compiler_params=_cp("parallel", "arbitrary"),
    )(ids, full, got)


def _chip_sum(ids, own, got):
    hr, c = own.shape
    tr = _row_tile(hr)
    nblk = hr // tr

    def body(ids_ref, a_ref, b_ref, o_ref):
        o_ref[...] = ((a_ref[...] + b_ref[0].astype(F32)) + b_ref[1].astype(F32)) + b_ref[2].astype(F32)

    return pl.pallas_call(
        body, name="chip_sum_%dx%d" % (hr, c),
        grid_spec=pltpu.PrefetchScalarGridSpec(
            num_scalar_prefetch=1, grid=(nblk,),
            in_specs=[pl.BlockSpec((tr, c), lambda i, ids_ref: (i, 0)),
                      pl.BlockSpec((3, tr, c), lambda i, ids_ref: (0, i, 0))],
            out_specs=pl.BlockSpec((tr, c), lambda i, ids_ref: (ids_ref[1] * nblk + i, 0))),
        out_shape=_sds((2 * hr, c), F32),
        compiler_params=_cp("parallel"),
    )(ids, own, got)


def _adamw_math(w, g, m, v):
    nm = ADAM_B1 * m + (1.0 - ADAM_B1) * g
    nv = ADAM_B2 * v + (1.0 - ADAM_B2) * jnp.square(g)
    m_hat = nm / (1.0 - ADAM_B1 ** ADAM_STEP)
    v_hat = nv / (1.0 - ADAM_B2 ** ADAM_STEP)
    return -ADAM_LR * (m_hat / (jnp.sqrt(v_hat) + ADAM_EPS) + ADAM_WD * w), nm, nv


def _adamw(w, g, m, v):
    r, c = w.shape
    tr = _row_tile(r, cap=512, mult=8)

    def body(w_ref, g_ref, m_ref, v_ref, d_ref, nm_ref, nv_ref):
        d_ref[...], nm_ref[...], nv_ref[...] = _adamw_math(w_ref[...], g_ref[...], m_ref[...], v_ref[...])

    spec = _rows(tr, c)
    return pl.pallas_call(
        body, name="adamw_%dx%d" % (r, c), grid=(r // tr,),
        in_specs=[spec] * 4, out_specs=[spec] * 3, out_shape=[_sds((r, c), F32)] * 3,
        compiler_params=_cp("parallel"),
    )(w, g, m, v)


_SMALL_NAMES = ("w_pool", "b_in", "g_mix_pre", "g_mix_post", "g_mlp_pre", "g_mlp_post", "pool_scale", "attn_sinks")
B_ROWS = -(-IN_WIDTH // D_MODEL)


def _row_block(rows):
    rows = [jnp.pad(r.astype(F32), ((0, 0), (0, D_MODEL - r.shape[1]))) for r in rows]
    return jnp.pad(jnp.concatenate(rows, axis=0), ((0, 8 - len(rows)), (0, 0)))


def _early_block(dg2, dg3, dg4, dps, dsink, loss):
    tail = jnp.concatenate([jnp.pad(dsink.reshape(1, -1), ((0, 0), (0, LANES - dsink.size))),
                            jnp.pad(loss.reshape(1, 1), ((0, 0), (0, LANES - 1)))], axis=1)
    return _row_block([dg2, dg3, dg4, dps, tail])


def _late_block(db_in, dg1):
    b = jnp.pad(db_in, ((0, 0), (0, B_ROWS * D_MODEL - IN_WIDTH))).reshape(B_ROWS, D_MODEL)
    return _row_block([b[r:r + 1] for r in range(B_ROWS)] + [dg1])


def _small_update(gearly, gmat, glate, w, m, v):
    names = _SMALL_NAMES
    n = len(names)

    def total(ref, rows):
        acc = ref[0:rows, :]
        for d in range(1, N_DEV):
            acc = acc + ref[d * rows:(d + 1) * rows, :]
        return acc

    def body(*refs):
        early_ref, gmat_ref, late_ref = refs[:3]
        w_refs, m_refs, v_refs = refs[3:3 + n], refs[3 + n:3 + 2 * n], refs[3 + 2 * n:3 + 3 * n]
        outs = refs[3 + 3 * n:]
        loss_ref, g_refs, d_refs = outs[0], outs[1:1 + n], outs[1 + n:1 + 2 * n]
        nm_refs, nv_refs = outs[1 + 2 * n:1 + 3 * n], outs[1 + 3 * n:1 + 4 * n]
        early, late = total(early_ref, 8), total(late_ref, 8)
        loss_ref[...] = jnp.sum(early[4:5, LANES:2 * LANES], axis=1, keepdims=True)
        bias = jnp.concatenate([late[r:r + 1, :] for r in range(B_ROWS - 1)]
                               + [late[B_ROWS - 1:B_ROWS, :IN_WIDTH - (B_ROWS - 1) * D_MODEL]], axis=1)
        grad = dict(b_in=bias, g_mix_pre=late[B_ROWS:B_ROWS + 1, :], g_mix_post=early[0:1, :],
                    g_mlp_pre=early[1:2, :], g_mlp_post=early[2:3, :], pool_scale=early[3:4, :POOL_WIDTH],
                    attn_sinks=early[4:5, :N_Q_HEADS])
        for i, name in enumerate(names):
            g = total(gmat_ref, 4 * POOL_GC) if name == "w_pool" else grad[name]
            g_refs[i][...] = g
            d_refs[i][...], nm_refs[i][...], nv_refs[i][...] = _adamw_math(
                w_refs[i][...], g, m_refs[i][...], v_refs[i][...])

    shapes = [_sds(w[k].shape, F32) for k in names]
    res = pl.pallas_call(
        body, name="small_update", out_shape=[_sds((1, 1), F32)] + shapes * 4,
        compiler_params=pltpu.CompilerParams(vmem_limit_bytes=VMEM_MB * 1024 * 1024),
    )(gearly, gmat, glate, *[w[k] for k in names], *[m[k] for k in names], *[v[k] for k in names])
    loss = res[0]
    per = {k: tuple(res[1 + j * n + i] for j in range(4)) for i, k in enumerate(names)}
    return loss, per


_BIG = ("w_in", "w_branch_pool", "w_branch_attn", "w_out", "w_up", "w_down")
_ORDER = ("g_mix_pre", "w_in", "b_in", "w_pool", "pool_scale", "attn_sinks", "w_branch_pool", "w_branch_attn",
          "w_out", "g_mix_post", "g_mlp_pre", "w_up", "w_down", "g_mlp_post")


def _stack_rows(slab):
    return slab.reshape(-1, slab.shape[2])


def _step(x2, tgt, seq, shards, small, ids):
    tabs = _rope_tables(seq)
    g1, g2, g3, g4 = (small[n] for n in ("g_mix_pre", "g_mix_post", "g_mlp_pre", "g_mlp_post"))
    sinks = small["attn_sinks"].reshape(N_Q_HEADS)
    w_pool = small["w_pool"].reshape(4, POOL_GC, POOL_GC)
    pool_scale = small["pool_scale"]

    def whole(shard, slabs):
        return lax.dynamic_update_slice(slabs, shard[None], (ids[0], 0, 0))

    up_a, up_b = shards["w_up"][:HALF], shards["w_up"][HALF:]
    down_a, down_b = shards["w_down"][:HALF], shards["w_down"][HALF:]
    w_in = _stack_rows(whole(shards["w_in"], _alone("gather_in", _ex_gather([shards["w_in"]]))[0][0]))
    mix_shards = [shards[n] for n in ("w_branch_pool", "w_branch_attn", "w_out")]
    (h, u, q, k, v, gate), [mix_slabs] = _inproj(
        x2, g1, w_in, small["b_in"], tabs, seq, exchanges=[_ex_gather(mix_shards)])
    w_bp, w_ba, out_slab = (whole(s, g) for s, g in zip(mix_shards, mix_slabs))
    w_out = _stack_rows(out_slab)
    diff, y_pool = _pool_fwd(u, w_pool, pool_scale, seq)
    (y_attn,), [[got_a]] = _attn_fwd(q, k, v, sinks, seq, exchanges=[_ex_gather([up_a])])
    (merged, mix, x1, h2), [[got_b]] = _merge_out(
        y_pool, y_attn, gate, x2, w_bp, w_ba, w_out, g2, g3, exchanges=[_ex_gather([up_b])])
    w_up = (whole(up_a, got_a), whole(up_b, got_b))
    (up, act), [[got_c, got_d]] = _mlp_up(h2, w_up, exchanges=[_ex_gather([down_a, down_b])])
    w_down = (whole(down_a, got_c), whole(down_b, got_d))
    dff, dy, loss_acc, dg4 = _mlp_down_loss(act, x1, tgt, w_down, g4)

    dup = _mlp_down_bwd(dff, up, w_down)
    dw_down = _dw("down", act, dff, 1024, 1024)[0].reshape(N_CHIPS, D_FF // N_CHIPS, D_MODEL)
    (dx1, dmix, dg3, dg2), [[got]] = _mlp_up_bwd(dup, dy, x1, mix, w_up, g3, g2, exchanges=[_ex_pair([dw_down])])
    ps_down = _pair_sum(ids, dw_down, got)
    (dw_up,), [[got]] = _dw("up", h2, dup, 1024, 1024, shard_cols=True, exchanges=[_ex_chip([ps_down[1]])])
    half_down = _chip_sum(ids, ps_down[0], got)
    (dbp, dba, dgate, dyp, dya), [[got], [g_down]] = _merge_bwd(
        dmix, gate, y_pool, y_attn, w_out, w_bp, w_ba, exchanges=[_ex_pair([dw_up]), _ex_swap([half_down])])
    ps_up = _pair_sum(ids, dw_up, got)
    dw_mix = [_dw("out", merged, dmix, 1024, 1024)[0].reshape(N_CHIPS, D_MODEL // N_CHIPS, D_MODEL),
              _dw_slabs("branch_pool", y_pool, dbp), _dw_slabs("branch_attn", y_attn, dba)]
    (dq, dk, dv, dsink), [[got], gots] = _attn_bwd(
        q, k, v, dya, sinks, tabs, seq, exchanges=[_ex_chip([ps_up[1]]), _ex_pair(dw_mix)])
    half_up = _chip_sum(ids, ps_up[0], got)
    ps_mix = [_pair_sum(ids, d, g) for d, g in zip(dw_mix, gots)]
    (du, dw_pool, dps), [[g_up]] = _pool_bwd(dyp, diff, w_pool, pool_scale, seq, exchanges=[_ex_swap([half_up])])
    parts = (du, dq, dk, dv, dgate)
    early = _early_block(dg2, dg3, dg4, dps, dsink[:, 0], loss_acc[0, 0])
    mat = dw_pool.reshape(4 * POOL_GC, POOL_GC)
    (dw_in_t, db_in), [gots, [gearly, gmat]] = _dw_in(
        h, parts, exchanges=[_ex_chip([p[1] for p in ps_mix]), _ex_allgather([early, mat])])
    half_mix = [_chip_sum(ids, p[0], g) for p, g in zip(ps_mix, gots)]
    dw_in = dw_in_t.reshape(N_CHIPS, IN_WIDTH // N_CHIPS, D_MODEL)
    g_mix, [got] = _alone("swap_mix_pair_in", _ex_swap(half_mix), _ex_pair([dw_in]))
    ps_in = _pair_sum(ids, dw_in, got)
    (gx, dg1), [[got]] = _inproj_bwd(parts, x2, dx1, w_in, g1, exchanges=[_ex_chip([ps_in[1]])])
    [g_in], [glate] = _alone("swap_in_allgather", _ex_swap([_chip_sum(ids, ps_in[0], got)]),
                             _ex_allgather([_late_block(db_in, dg1)]))

    grads = dict(w_in=g_in, w_branch_pool=g_mix[1], w_branch_attn=g_mix[2], w_out=g_mix[0], w_up=g_up, w_down=g_down)
    return (gearly, gmat, glate), gx, grads


def kernel(x, g_mix_pre, w_in, b_in, w_pool, pool_scale, attn_sinks, w_branch_pool, w_branch_attn, w_out, g_mix_post, g_mlp_pre, w_up, w_down, g_mlp_post, loss_target, m_g_mix_pre, m_w_in, m_b_in, m_w_pool, m_pool_scale, m_attn_sinks, m_w_branch_pool, m_w_branch_attn, m_w_out, m_g_mix_post, m_g_mlp_pre, m_w_up, m_w_down, m_g_mlp_post, v_g_mix_pre, v_w_in, v_b_in, v_w_pool, v_pool_scale, v_attn_sinks, v_w_branch_pool, v_w_branch_attn, v_w_out, v_g_mix_post, v_g_mlp_pre, v_w_up, v_w_down, v_g_mlp_post):
    weights = dict(g_mix_pre=g_mix_pre, w_in=w_in, b_in=b_in, w_pool=w_pool, pool_scale=pool_scale,
                   attn_sinks=attn_sinks, w_branch_pool=w_branch_pool, w_branch_attn=w_branch_attn, w_out=w_out,
                   g_mix_post=g_mix_post, g_mlp_pre=g_mlp_pre, w_up=w_up, w_down=w_down, g_mlp_post=g_mlp_post)
    mom1 = dict(g_mix_pre=m_g_mix_pre, w_in=m_w_in, b_in=m_b_in, w_pool=m_w_pool, pool_scale=m_pool_scale,
                attn_sinks=m_attn_sinks, w_branch_pool=m_w_branch_pool, w_branch_attn=m_w_branch_attn,
                w_out=m_w_out, g_mix_post=m_g_mix_post, g_mlp_pre=m_g_mlp_pre, w_up=m_w_up, w_down=m_w_down,
                g_mlp_post=m_g_mlp_post)
    mom2 = dict(g_mix_pre=v_g_mix_pre, w_in=v_w_in, b_in=v_b_in, w_pool=v_w_pool, pool_scale=v_pool_scale,
                attn_sinks=v_attn_sinks, w_branch_pool=v_w_branch_pool, w_branch_attn=v_w_branch_attn,
                w_out=v_w_out, g_mix_post=v_g_mix_post, g_mlp_pre=v_g_mlp_pre, w_up=v_w_up, w_down=v_w_down,
                g_mlp_post=v_g_mlp_post)
    b_loc, seq, _ = x.shape
    x2 = x.reshape(b_loc * seq, D_MODEL)
    tgt = loss_target.reshape(b_loc * seq, D_MODEL)
    ids = jnp.stack([2 * lax.axis_index("x") + lax.axis_index("y"), lax.axis_index("c")]).astype(jnp.int32)

    def flat(n, a):
        return a[0].T if n == "w_in" else a[0]

    def unflat(n, a):
        return (a.T if n == "w_in" else a)[None]

    shards = {n: flat(n, weights[n]).astype(BF16) for n in _BIG}
    small = {n: weights[n] for n in _ORDER if n not in _BIG}
    (gearly, gmat, glate), gx, grads = _step(x2, tgt, seq, shards, small, ids)

    def two_d(src):
        return {n: src[n].reshape(4 * POOL_GC, POOL_GC) if n == "w_pool" else src[n] for n in _SMALL_NAMES}

    loss, per = _small_update(gearly, gmat, glate, two_d(weights), two_d(mom1), two_d(mom2))
    delta, new_m, new_v = {}, {}, {}
    for n in _SMALL_NAMES:
        grads[n], delta[n], new_m[n], new_v[n] = (a.reshape(weights[n].shape) for a in per[n])
    for n in _BIG:
        d, nm, nv = _adamw(flat(n, weights[n]), grads[n], flat(n, mom1[n]), flat(n, mom2[n]))
        grads[n] = unflat(n, grads[n])
        delta[n], new_m[n], new_v[n] = unflat(n, d), unflat(n, nm), unflat(n, nv)

    return (loss[0, 0], gx.reshape(x.shape), *[grads[n] for n in _ORDER], *[delta[n] for n in _ORDER],
            *[new_m[n] for n in _ORDER], *[new_v[n] for n in _ORDER])
```

```python
import jax
import jax.numpy as jnp
from jax import lax
from jax.experimental import pallas as pl
from jax.experimental.pallas import tpu as pltpu

F32 = jnp.float32
BF16 = jnp.bfloat16

D_MODEL = 1024
POOL_WINDOWS = (2, 4, 8, 16)
POOL_WIDTH = 512
POOL_GC = 128
HALO = 16
HEAD_DIM = 64
N_Q_HEADS = 8
ATTN_WIDTH = 512
KV_WIDTH = 128
BLOCK = 128
NEG_INF = -1e30
ROPE_THETA = 500000.0
ROT_DIM = 16
GATE_WIDTH = 2048
IN_WIDTH = 3328
D_FF = 4096
EPS = 1e-6
SCALE = HEAD_DIM ** -0.5
C_Q, C_K, C_V, C_G = 512, 1024, 1152, 1280

ADAM_LR, ADAM_B1, ADAM_B2, ADAM_EPS, ADAM_WD, ADAM_STEP = 0.001, 0.9, 0.999, 1e-08, 0.01, 10

N_CHIPS = 4
N_DEV = 8
LANES = 128
TM = 512
TP = 512
VMEM_MB = 56

MESH = pl.DeviceIdType.MESH
ANY = pl.BlockSpec(memory_space=pl.ANY)


def _cp(*sem, vmem=VMEM_MB):
    return pltpu.CompilerParams(dimension_semantics=sem, vmem_limit_bytes=vmem * 1024 * 1024)


def _rows(tile, cols):
    return pl.BlockSpec((tile, cols), lambda i: (i, 0))


def _const(shape):
    nd = len(shape)
    return pl.BlockSpec(shape, lambda i: (0,) * nd)


def _sds(shape, dtype):
    return jax.ShapeDtypeStruct(shape, dtype)


def _dot(a, b):
    return jnp.dot(a, b, preferred_element_type=F32)


def _dot_nt(a, b):
    return lax.dot_general(a, b, (((1,), (1,)), ((), ())), preferred_element_type=F32)


def _dot_tn(a, b):
    return lax.dot_general(a, b, (((0,), (0,)), ((), ())), preferred_element_type=F32)


def _rms(x):
    return lax.rsqrt(jnp.mean(x * x, axis=-1, keepdims=True) + EPS)


def _norm_bwd(x, g, dout):
    r = _rms(x)
    n = x * r
    dn = dout * g
    dx = r * (dn - n * jnp.mean(dn * n, axis=-1, keepdims=True))
    return dx, jnp.sum(dout * n, axis=0, keepdims=True)


def _rot_fwd(t, c, a, bt):
    return t * c + pltpu.roll(t, LANES - 8, 1) * a + pltpu.roll(t, 8, 1) * bt


def _rot_bwd(d, c, a, bt):
    return d * c + pltpu.roll(d * a, 8, 1) + pltpu.roll(d * bt, LANES - 8, 1)


def _rope_tables(seq):
    pos = jnp.arange(seq, dtype=F32)
    inv_freq = ROPE_THETA ** (-jnp.arange(0, ROT_DIM, 2, dtype=F32) / ROT_DIM)
    ang = pos[:, None] * inv_freq[None, :]
    cos, sin = jnp.cos(ang), jnp.sin(ang)
    ones = jnp.ones((seq, HEAD_DIM - ROT_DIM), F32)
    zeros8 = jnp.zeros((seq, 8), F32)
    zrest = jnp.zeros((seq, HEAD_DIM - ROT_DIM), F32)
    c = jnp.concatenate([cos, cos, ones], axis=1)
    a = jnp.concatenate([-sin, zeros8, zrest], axis=1)
    bt = jnp.concatenate([zeros8, sin, zrest], axis=1)
    return tuple(jnp.tile(t, (1, 2)) for t in (c, a, bt))


class _Exchange:
    def __init__(self, inputs, out_shapes, sems, start, finish, aliases=None, middle=None):
        self.inputs, self.out_shapes, self.sems = list(inputs), list(out_shapes), list(sems)
        self.start, self.finish, self.aliases = start, finish, dict(aliases or {})
        self.middle = middle


def _call(body, *, name, grid, in_specs, out_specs, out_shape, args, scratch=(), sem=(), exchanges=()):
    in_specs, out_specs, out_shape, scratch = list(in_specs), list(out_specs), list(out_shape), list(scratch)
    if not exchanges:
        return pl.pallas_call(body, name=name, grid=grid, in_specs=in_specs, out_specs=out_specs,
                              out_shape=out_shape, scratch_shapes=scratch, compiler_params=_cp(*sem))(*args)
    n_in, n_out, n_scr = len(in_specs), len(out_specs), len(scratch)
    x_in = [a for ex in exchanges for a in ex.inputs]
    x_out = [s for ex in exchanges for s in ex.out_shapes]
    x_sem = [s for ex in exchanges for s in ex.sems]
    aliases, i_off, o_off = {}, n_in, n_out
    for ex in exchanges:
        for i, o in ex.aliases.items():
            aliases[i_off + i] = o_off + o
        i_off += len(ex.inputs)
        o_off += len(ex.out_shapes)

    def split(flat):
        out, pos = [], 0
        for ex, n in zip(exchanges, flat[1]):
            out.append(flat[0][pos:pos + n])
            pos += n
        return out

    def carrier(*refs):
        pos = 0
        groups = []
        for n in (n_in, len(x_in), n_out, len(x_out), n_scr, len(x_sem)):
            groups.append(refs[pos:pos + n])
            pos += n
        ins, xin, outs, xout, scr, xsem = groups
        xin = split((xin, [len(ex.inputs) for ex in exchanges]))
        xout = split((xout, [len(ex.out_shapes) for ex in exchanges]))
        xsem = split((xsem, [len(ex.sems) for ex in exchanges]))
        first = pl.program_id(0) == 0
        last = pl.program_id(0) == grid[0] - 1
        for d in range(1, len(grid)):
            first = jnp.logical_and(first, pl.program_id(d) == 0)
            last = jnp.logical_and(last, pl.program_id(d) == grid[d] - 1)

        @pl.when(first)
        def _():
            for ex, i, o, s in zip(exchanges, xin, xout, xsem):
                ex.start(i, o, s)

        if any(ex.middle for ex in exchanges):
            half = pl.program_id(0) == grid[0] // 2
            for d in range(1, len(grid)):
                half = jnp.logical_and(half, pl.program_id(d) == 0)

            @pl.when(half)
            def _():
                for ex, i, o, s in zip(exchanges, xin, xout, xsem):
                    if ex.middle:
                        ex.middle(i, o, s)

        body(*ins, *outs, *scr)

        @pl.when(last)
        def _():
            for ex, i, o, s in zip(exchanges, xin, xout, xsem):
                ex.finish(i, o, s)

    res = pl.pallas_call(
        carrier, name=name, grid=grid, in_specs=in_specs + [ANY] * len(x_in),
        out_specs=out_specs + [ANY] * len(x_out), out_shape=out_shape + x_out,
        scratch_shapes=scratch + x_sem, input_output_aliases=aliases,
        compiler_params=_cp(*(["arbitrary"] * len(grid))),
    )(*args, *x_in)
    return res[:n_out], split((res[n_out:], [len(ex.out_shapes) for ex in exchanges]))


def _alone(name, *exchanges):
    n_in = [len(ex.inputs) for ex in exchanges]
    n_out = [len(ex.out_shapes) for ex in exchanges]
    n_sem = [len(ex.sems) for ex in exchanges]
    aliases, i_off, o_off = {}, 0, 0
    for ex in exchanges:
        for i, o in ex.aliases.items():
            aliases[i_off + i] = o_off + o
        i_off += len(ex.inputs)
        o_off += len(ex.out_shapes)

    def split(flat, counts):
        out, pos = [], 0
        for n in counts:
            out.append(flat[pos:pos + n])
            pos += n
        return out

    def body(*refs):
        ins, outs, sems = split(refs, [sum(n_in), sum(n_out), sum(n_sem)])
        groups = list(zip(exchanges, split(ins, n_in), split(outs, n_out), split(sems, n_sem)))
        for ex, i, o, s in groups:
            ex.start(i, o, s)
        for ex, i, o, s in groups:
            if ex.middle:
                ex.middle(i, o, s)
        for ex, i, o, s in groups:
            ex.finish(i, o, s)

    res = pl.pallas_call(
        body, name=name, in_specs=[ANY] * sum(n_in), out_specs=[ANY] * sum(n_out),
        out_shape=[s for ex in exchanges for s in ex.out_shapes],
        scratch_shapes=[s for ex in exchanges for s in ex.sems], input_output_aliases=aliases,
    )(*[a for ex in exchanges for a in ex.inputs])
    return split(res, n_out)


def _place():
    x, y, c = lax.axis_index("x"), lax.axis_index("y"), lax.axis_index("c")
    chips = [(1 - x, y), (x, 1 - y), (1 - x, 1 - y)]
    return x, y, c, chips


def _remote(src, dst, send, recv, to):
    return pltpu.make_async_remote_copy(src_ref=src, dst_ref=dst, send_sem=send, recv_sem=recv,
                                        device_id=to, device_id_type=MESH)


def _ex_gather(shards):
    nw = len(shards)
    hrs = [s.shape[0] // 2 for s in shards]

    def copies(ins, outs, sems):
        s1, r1, s2, r2, fs, fr = sems
        x, y, c, _ = _place()
        me, xn, yn, dg = (x, y), (1 - x, y), (x, 1 - y), (1 - x, 1 - y)
        nbr = (xn, yn)
        sibling = (x, y, 1 - c)

        def piece(w, chip, core, part=None):
            hr = hrs[w]
            rows = pl.ds(core * hr, hr) if part is None else pl.ds(core * hr + part * (hr // 2), hr // 2)
            return outs[w].at[2 * chip[0] + chip[1], rows]

        def first(w, k):
            return _remote(ins[w].at[pl.ds(c * hrs[w], hrs[w])], piece(w, me, c), s1.at[w, k], r1.at[w, k],
                           (*nbr[k], c))

        def landed(w, k):
            return _remote(piece(w, nbr[k], c), piece(w, nbr[k], c), s1.at[w, k], r1.at[w, k], (*nbr[k], c))

        def onward(w, k):
            return _remote(piece(w, nbr[k], c, k), piece(w, nbr[k], c, k), s2.at[w, k], r2.at[w, k],
                           (*nbr[1 - k], c))

        def arrived(w, k):
            return _remote(piece(w, dg, c, k), piece(w, dg, c, k), s2.at[w, k], r2.at[w, k], (*nbr[1 - k], c))

        def passed(w, j):
            chip = (xn, yn, dg)[j]
            return _remote(piece(w, chip, c), piece(w, chip, c), fs.at[w, j], fr.at[w, j], sibling)

        def handed(w, j):
            chip = (xn, yn, dg)[j]
            return _remote(piece(w, chip, 1 - c), piece(w, chip, 1 - c), fs.at[w, j], fr.at[w, j], sibling)

        return first, landed, onward, arrived, passed, handed

    def start(ins, outs, sems):
        first = copies(ins, outs, sems)[0]
        for w in range(nw):
            for k in range(2):
                first(w, k).start()

    def middle(ins, outs, sems):
        _, landed, onward, _, passed, _ = copies(ins, outs, sems)
        for w in range(nw):
            for k in range(2):
                landed(w, k).wait_recv()
                onward(w, k).start()
                passed(w, k).start()

    def finish(ins, outs, sems):
        first, _, onward, arrived, passed, handed = copies(ins, outs, sems)
        for w in range(nw):
            for k in range(2):
                arrived(w, k).wait_recv()
            passed(w, 2).start()
        for w in range(nw):
            for j in range(3):
                handed(w, j).wait_recv()
        for w in range(nw):
            for k in range(2):
                first(w, k).wait_send()
                onward(w, k).wait_send()
            for j in range(3):
                passed(w, j).wait_send()

    return _Exchange(shards, [_sds((N_CHIPS,) + s.shape, s.dtype) for s in shards],
                     [pltpu.SemaphoreType.DMA((nw, 2))] * 4 + [pltpu.SemaphoreType.DMA((nw, 3))] * 2,
                     start, finish, middle=middle)


def _ex_pair(grads):
    nw = len(grads)

    def copies(ins, outs, sems):
        x, y, c, _ = _place()
        out = []
        for w in range(nw):
            hr = grads[w].shape[1] // 2
            out.append(_remote(ins[w].at[:, pl.ds((1 - c) * hr, hr)], outs[w], sems[0].at[w], sems[1].at[w],
                               (x, y, 1 - c)))
        return out

    def start(ins, outs, sems):
        for cp in copies(ins, outs, sems):
            cp.start()

    def finish(ins, outs, sems):
        for cp in copies(ins, outs, sems):
            cp.wait()

    return _Exchange(grads, [_sds((N_CHIPS, g.shape[1] // 2, g.shape[2]), F32) for g in grads],
                     [pltpu.SemaphoreType.DMA((nw,))] * 2, start, finish)


def _ex_chip(pieces):
    nw = len(pieces)

    def copies(ins, outs, sems):
        x, y, c, chips = _place()
        return [_remote(ins[w].at[2 * cx + cy], outs[w].at[k], sems[0].at[w, k], sems[1].at[w, k], (cx, cy, c))
                for w in range(nw) for k, (cx, cy) in enumerate(chips)]

    def start(ins, outs, sems):
        for cp in copies(ins, outs, sems):
            cp.start()

    def finish(ins, outs, sems):
        for cp in copies(ins, outs, sems):
            cp.wait()

    return _Exchange(pieces, [_sds((3,) + p.shape[1:], BF16) for p in pieces],
                     [pltpu.SemaphoreType.DMA((nw, 3))] * 2, start, finish)


def _ex_swap(fulls):
    nw = len(fulls)

    def start(ins, outs, sems):
        x, y, c, _ = _place()
        for w in range(nw):
            hr = fulls[w].shape[0] // 2
            mine = pl.ds(c * hr, hr)
            _remote(ins[w].at[mine], outs[w].at[mine], sems[0].at[w], sems[1].at[w], (x, y, 1 - c)).start()

    def finish(ins, outs, sems):
        x, y, c, _ = _place()
        for w in range(nw):
            hr = fulls[w].shape[0] // 2
            mine, theirs = pl.ds(c * hr, hr), pl.ds((1 - c) * hr, hr)
            _remote(ins[w].at[mine], outs[w].at[mine], sems[0].at[w], sems[1].at[w], (x, y, 1 - c)).wait_send()
            _remote(ins[w].at[theirs], outs[w].at[theirs], sems[0].at[w], sems[1].at[w], (x, y, 1 - c)).wait_recv()

    return _Exchange(fulls, [_sds(f.shape, F32) for f in fulls], [pltpu.SemaphoreType.DMA((nw,))] * 2,
                     start, finish, aliases={w: w for w in range(nw)})


def _ex_allgather(blocks):
    nb = len(blocks)

    def copies(ins, outs, sems):
        send, recv, lsem = sems
        x, y, c, chips = _place()
        me, sibling = (x, y, c), (x, y, 1 - c)

        def rows(b, px, py, pc):
            m_per = blocks[b].shape[0]
            return outs[b].at[pl.ds((4 * px + 2 * py + pc) * m_per, m_per), :]

        def copy(b, k, blk, to, src=None):
            return _remote(rows(b, *blk) if src is None else src, rows(b, *blk), send.at[b, k], recv.at[b, k], to)

        def mine(b):
            return pltpu.make_async_copy(ins[b], rows(b, *me), lsem.at[b])

        def first(b, k):
            return copy(b, k, me, sibling if k == 0 else (*chips[k - 1], c), src=ins[b])

        def passed(b, j):
            return copy(b, 4 + j, (*chips[j], c), sibling)

        def landed(b, j):
            return copy(b, 1 + j, (*chips[j], c), me)

        def handed(b, k):
            return copy(b, 0, sibling, me) if k == 0 else copy(b, 3 + k, (*chips[k - 1], 1 - c), me)

        return mine, first, passed, landed, handed

    def start(ins, outs, sems):
        mine, first, _, _, _ = copies(ins, outs, sems)
        for b in range(nb):
            mine(b).start()
            for k in range(4):
                first(b, k).start()

    def finish(ins, outs, sems):
        mine, first, passed, landed, handed = copies(ins, outs, sems)
        sent = []
        for b in range(nb):
            for j in range(3):
                landed(b, j).wait_recv()
                cp = passed(b, j)
                cp.start()
                sent.append(cp)
        for b in range(nb):
            for k in range(4):
                handed(b, k).wait_recv()
            for k in range(4):
                first(b, k).wait_send()
        for cp in sent:
            cp.wait_send()
        for b in range(nb):
            mine(b).wait()

    return _Exchange(blocks, [_sds((N_DEV * b.shape[0], b.shape[1]), F32) for b in blocks],
                     [pltpu.SemaphoreType.DMA((nb, 7)), pltpu.SemaphoreType.DMA((nb, 7)), pltpu.SemaphoreType.DMA((nb,))],
                     start, finish)


def _inproj(x2, g1, w_in_t, b_in, tabs, seq, exchanges=()):
    T = x2.shape[0]
    tm = min(TM, seq)
    nseq = seq // tm

    def body(x_ref, g_ref, w_ref, b_ref, c_ref, a_ref, bt_ref, h_ref, u_ref, q_ref, k_ref, v_ref, gate_ref):
        x = x_ref[...]
        h = (x * _rms(x) * g_ref[...]).astype(BF16)
        h_ref[...] = h

        def proj(lo, hi):
            return _dot_nt(h, w_ref[lo:hi, :]) + b_ref[:, lo:hi]

        c, a, bt = c_ref[...], a_ref[...], bt_ref[...]
        u_ref[...] = proj(0, C_Q)
        q = proj(C_Q, C_K)
        for p in range(4):
            sl = slice(LANES * p, LANES * (p + 1))
            q_ref[:, sl] = (_rot_fwd(q[:, sl], c, a, bt) * SCALE).astype(BF16)
        kv = proj(C_K, C_G)
        k_ref[...] = _rot_fwd(kv[:, :KV_WIDTH], c, a, bt).astype(BF16)
        v_ref[...] = kv[:, KV_WIDTH:].astype(BF16)
        for j in range(2):
            lo = C_G + D_MODEL * j
            gate_ref[:, D_MODEL * j:D_MODEL * (j + 1)] = jax.nn.sigmoid(proj(lo, lo + D_MODEL)).astype(BF16)

    tab = pl.BlockSpec((tm, LANES), lambda i: (i % nseq, 0))
    return _call(
        body, name="inproj", grid=(T // tm,),
        in_specs=[_rows(tm, D_MODEL), _const((1, D_MODEL)), _const((IN_WIDTH, D_MODEL)), _const((1, IN_WIDTH)),
                  tab, tab, tab],
        out_specs=[_rows(tm, D_MODEL), _rows(tm, POOL_WIDTH), _rows(tm, ATTN_WIDTH), _rows(tm, KV_WIDTH),
                   _rows(tm, KV_WIDTH), _rows(tm, GATE_WIDTH)],
        out_shape=[_sds((T, D_MODEL), BF16), _sds((T, POOL_WIDTH), F32), _sds((T, ATTN_WIDTH), BF16),
                   _sds((T, KV_WIDTH), BF16), _sds((T, KV_WIDTH), BF16), _sds((T, GATE_WIDTH), BF16)],
        args=(x2, g1, w_in_t, b_in, *tabs), sem=("parallel",), exchanges=exchanges)


def _inv_count(pos, w):
    return 1.0 / jnp.minimum(pos + 1, w).astype(F32)


def _pool_fwd(u, w_pool, pool_scale, seq):
    T = u.shape[0]
    tp = min(TP, seq)
    nseq = seq // tp
    per = tp // HALO

    def body(u_ref, prev_ref, w_ref, s_ref, diff_ref, y_ref):
        i = pl.program_id(0)
        first = (i % nseq) == 0
        prev = jnp.where(first, 0.0, prev_ref[...])
        ext = jnp.concatenate([prev, u_ref[...]], axis=0)
        pos = (i % nseq) * tp + lax.broadcasted_iota(jnp.int32, (tp, 1), 0)
        for gi, w in enumerate(POOL_WINDOWS):
            sl = slice(POOL_GC * gi, POOL_GC * (gi + 1))
            xg = ext[:, sl]
            s = xg
            sh = 1
            while sh < w:
                s = s + pltpu.roll(s, sh, 0)
                sh *= 2
            pooled = s[HALO:] * _inv_count(pos, w)
            diff = (pooled - xg[HALO:]).astype(BF16)
            diff_ref[:, sl] = diff
            mixed = _dot(diff, w_ref[gi].astype(BF16))
            y_ref[:, sl] = (mixed * s_ref[:, sl]).astype(BF16)

    return _call(
        body, name="pool_fwd", grid=(T // tp,),
        in_specs=[_rows(tp, POOL_WIDTH),
                  pl.BlockSpec((HALO, POOL_WIDTH), lambda i: (jnp.maximum(i * per - 1, 0), 0)),
                  _const((4, POOL_GC, POOL_GC)), _const((1, POOL_WIDTH))],
        out_specs=[_rows(tp, POOL_WIDTH), _rows(tp, POOL_WIDTH)],
        out_shape=[_sds((T, POOL_WIDTH), BF16), _sds((T, POOL_WIDTH), BF16)],
        args=(u, u, w_pool, pool_scale), sem=("parallel",))


GROUP = 4
GROWS = GROUP * BLOCK


def _attn_masks(n):
    qi = lax.broadcasted_iota(jnp.int32, (GROWS, 2 * BLOCK), 0) % BLOCK
    kj = lax.broadcasted_iota(jnp.int32, (GROWS, 2 * BLOCK), 1)
    rel = qi + BLOCK - kj
    valid = (rel >= 0) & (rel < BLOCK) & (kj >= jnp.where(n > 0, 0, BLOCK))
    lo = lax.broadcasted_iota(jnp.int32, (BLOCK, LANES), 1) < HEAD_DIM
    return valid, lo


def _by_example(bl, *arrays):
    return [a.reshape(bl, a.shape[0] // bl, a.shape[1]) for a in arrays]


def _stack_heads(ref, h, lo):
    keep = lo if h == 0 else jnp.logical_not(lo)
    pieces = []
    for p in (2 * h, 2 * h + 1):
        xp = ref[:, LANES * p:LANES * (p + 1)].astype(F32)
        for e in range(2):
            t = xp if e == h else pltpu.roll(xp, HEAD_DIM, 1)
            pieces.append(jnp.where(keep, t, 0.0).astype(BF16))
    return jnp.concatenate(pieces, axis=0)


def _unstack_heads(stacked, h, lo):
    pairs = []
    for j in range(2):
        parts = []
        for e in range(2):
            t = stacked[BLOCK * (2 * j + e):BLOCK * (2 * j + e + 1)]
            parts.append(t if e == h else pltpu.roll(t, HEAD_DIM, 1))
        pairs.append(jnp.where(lo, parts[0], parts[1]))
    return pairs


def _sink_rows(sink_ref, h):
    head = lax.broadcasted_iota(jnp.int32, (GROWS, 1), 0) // BLOCK
    col = jnp.zeros((GROWS, 1), F32) + sink_ref[GROUP * h]
    for g in range(1, GROUP):
        col = jnp.where(head == g, sink_ref[GROUP * h + g], col)
    return col


def _group_probs(qs, kk, valid, sink):
    s = jnp.where(valid, _dot_nt(qs, kk), NEG_INF)
    m = jnp.maximum(jnp.max(s, axis=1, keepdims=True), sink)
    ex = jnp.exp(s - m)
    es = jnp.exp(sink - m)
    inv = 1.0 / (jnp.sum(ex, axis=1, keepdims=True) + es)
    return ex * inv, es * inv


def _attn_fwd(q, k, v, sinks, seq, exchanges=()):
    T = q.shape[0]
    nb = seq // BLOCK
    bl = T // seq

    def body(sink_ref, q_ref, kp_ref, kc_ref, vp_ref, vc_ref, o_ref):
        valid, lo = _attn_masks(pl.program_id(0))
        for b in range(bl):
            kk = jnp.concatenate([kp_ref[b], kc_ref[b]], axis=0)
            vv = jnp.concatenate([vp_ref[b], vc_ref[b]], axis=0)
            for h in range(2):
                qs = _stack_heads(q_ref.at[b], h, lo)
                pr, _ = _group_probs(qs, kk, valid, _sink_rows(sink_ref, h))
                o = _dot(pr.astype(BF16), vv)
                for j, pair in enumerate(_unstack_heads(o, h, lo)):
                    p = 2 * h + j
                    o_ref[b, :, LANES * p:LANES * (p + 1)] = pair.astype(BF16)

    cur = lambda n: (0, n, 0)
    prv = lambda n: (0, jnp.maximum(n - 1, 0), 0)
    kv = lambda m: pl.BlockSpec((bl, BLOCK, KV_WIDTH), m)
    res = _call(
        body, name="attn_fwd", grid=(nb,),
        in_specs=[pl.BlockSpec(memory_space=pltpu.SMEM), pl.BlockSpec((bl, BLOCK, ATTN_WIDTH), cur),
                  kv(prv), kv(cur), kv(prv), kv(cur)],
        out_specs=[pl.BlockSpec((bl, BLOCK, ATTN_WIDTH), cur)],
        out_shape=[_sds((bl, seq, ATTN_WIDTH), BF16)],
        args=(sinks, *_by_example(bl, q, k, k, v, v)), sem=("parallel",), exchanges=exchanges)
    if exchanges:
        return [res[0][0].reshape(T, ATTN_WIDTH)], res[1]
    return [res[0].reshape(T, ATTN_WIDTH)]


def _branch(y, w_ref):
    return jnp.concatenate([_dot(y, w_ref[j]) for j in range(N_CHIPS)], axis=1)


def _merge_out(y_pool, y_attn, gate, x2, w_bp, w_ba, w_out, g2, g3, exchanges=()):
    T = x2.shape[0]
    tm = min(TM, T)

    def body(yp_ref, ya_ref, gate_ref, x_ref, wbp_ref, wba_ref, wo_ref, g2_ref, g3_ref,
             mg_ref, mix_ref, x1_ref, h2_ref):
        bp, ba = _branch(yp_ref[...], wbp_ref), _branch(ya_ref[...], wba_ref)
        merged = (gate_ref[:, :D_MODEL].astype(F32) * bp + gate_ref[:, D_MODEL:].astype(F32) * ba).astype(BF16)
        mg_ref[...] = merged
        mix = _dot(merged, wo_ref[...])
        mix_ref[...] = mix
        x1 = x_ref[...] + mix * _rms(mix) * g2_ref[...]
        x1_ref[...] = x1
        h2_ref[...] = (x1 * _rms(x1) * g3_ref[...]).astype(BF16)

    return _call(
        body, name="merge_out", grid=(T // tm,),
        in_specs=[_rows(tm, POOL_WIDTH), _rows(tm, ATTN_WIDTH), _rows(tm, GATE_WIDTH), _rows(tm, D_MODEL),
                  _const(w_bp.shape), _const(w_ba.shape), _const((D_MODEL, D_MODEL)),
                  _const((1, D_MODEL)), _const((1, D_MODEL))],
        out_specs=[_rows(tm, D_MODEL)] * 4,
        out_shape=[_sds((T, D_MODEL), BF16), _sds((T, D_MODEL), F32), _sds((T, D_MODEL), F32),
                   _sds((T, D_MODEL), BF16)],
        args=(y_pool, y_attn, gate, x2, w_bp, w_ba, w_out, g2, g3), sem=("parallel",), exchanges=exchanges)


HALF = D_MODEL // 2
_HALVES = _const((N_CHIPS, HALF, D_MODEL))


def _mlp_up(h2, w_up, exchanges=()):
    T = h2.shape[0]
    tm = min(TM, T)

    def body(h_ref, wa_ref, wb_ref, up_ref, a_ref):
        ha, hb = h_ref[:, :HALF], h_ref[:, HALF:]
        for j in range(N_CHIPS):
            sl = slice(D_MODEL * j, D_MODEL * (j + 1))
            up = _dot(ha, wa_ref[j]) + _dot(hb, wb_ref[j])
            up_ref[:, sl] = up.astype(BF16)
            a_ref[:, sl] = jnp.square(jnp.maximum(up, 0.0)).astype(BF16)

    return _call(
        body, name="mlp_up", grid=(T // tm,),
        in_specs=[_rows(tm, D_MODEL), _HALVES, _HALVES],
        out_specs=[_rows(tm, D_FF), _rows(tm, D_FF)],
        out_shape=[_sds((T, D_FF), BF16), _sds((T, D_FF), BF16)],
        args=(h2, *w_up), sem=("parallel",), exchanges=exchanges)


def _mlp_down_loss(a, x1, tgt, w_down, g4):
    T = a.shape[0]
    tm = min(TM, T)

    def body(a_ref, x1_ref, t_ref, wa_ref, wb_ref, g_ref, dff_ref, dy_ref, loss_ref, dg_ref):
        @pl.when(pl.program_id(0) == 0)
        def _():
            loss_ref[...] = jnp.zeros_like(loss_ref)
            dg_ref[...] = jnp.zeros_like(dg_ref)

        ff = None
        for j in range(N_CHIPS):
            lo = D_MODEL * j
            t = _dot(a_ref[:, lo:lo + HALF], wa_ref[j]) + _dot(a_ref[:, lo + HALF:lo + D_MODEL], wb_ref[j])
            ff = t if ff is None else ff + t
        g = g_ref[...]
        err = x1_ref[...] + ff * _rms(ff) * g - t_ref[...]
        loss_ref[...] += jnp.sum(err * err) * (0.5 / D_MODEL)
        dy = err * (1.0 / D_MODEL)
        dy_ref[...] = dy
        dff, dg = _norm_bwd(ff, g, dy)
        dff_ref[...] = dff.astype(BF16)
        dg_ref[...] += dg

    return _call(
        body, name="mlp_down_loss", grid=(T // tm,),
        in_specs=[_rows(tm, D_FF), _rows(tm, D_MODEL), _rows(tm, D_MODEL), _HALVES, _HALVES, _const((1, D_MODEL))],
        out_specs=[_rows(tm, D_MODEL), _rows(tm, D_MODEL), _const((8, LANES)), _const((1, D_MODEL))],
        out_shape=[_sds((T, D_MODEL), BF16), _sds((T, D_MODEL), F32), _sds((8, LANES), F32),
                   _sds((1, D_MODEL), F32)],
        args=(a, x1, tgt, *w_down, g4), sem=("arbitrary",))


TM_MLP = 256


def _mlp_core(h2, x1, tgt, w_up, w_down, g4):
    T = h2.shape[0]
    tm = min(TM_MLP, T)

    def body(h_ref, x1_ref, t_ref, g_ref, ua_hbm, ub_hbm, da_hbm, db_hbm,
             act_ref, dff_ref, dy_ref, dup_ref, loss_ref, dg_ref, ua, ub, da, db, relu_scr, sems):
        @pl.when(pl.program_id(0) == 0)
        def _():
            copies = [pltpu.make_async_copy(src, dst, sems.at[i])
                      for i, (src, dst) in enumerate(((ua_hbm, ua), (ub_hbm, ub), (da_hbm, da), (db_hbm, db)))]
            for cp in copies:
                cp.start()
            loss_ref[...] = jnp.zeros_like(loss_ref)
            dg_ref[...] = jnp.zeros_like(dg_ref)
            for cp in copies:
                cp.wait()

        ha, hb = h_ref[:, :HALF], h_ref[:, HALF:]
        ff = None
        for j in range(N_CHIPS):
            lo = D_MODEL * j
            relu = jnp.maximum(_dot(ha, ua[j]) + _dot(hb, ub[j]), 0.0)
            relu_scr[:, lo:lo + D_MODEL] = relu
            act = jnp.square(relu).astype(BF16)
            act_ref[:, lo:lo + D_MODEL] = act
            t = _dot(act[:, :HALF], da[j]) + _dot(act[:, HALF:], db[j])
            ff = t if ff is None else ff + t
        g = g_ref[...]
        err = x1_ref[...] + ff * _rms(ff) * g - t_ref[...]
        loss_ref[...] += jnp.sum(err * err) * (0.5 / D_MODEL)
        dy = err * (1.0 / D_MODEL)
        dy_ref[...] = dy
        dff, dg = _norm_bwd(ff, g, dy)
        dg_ref[...] += dg
        dff = dff.astype(BF16)
        dff_ref[...] = dff
        for j in range(N_CHIPS):
            for part, w in enumerate((da, db)):
                lo = D_MODEL * j + HALF * part
                dact = _dot_nt(dff, w[j])
                dup_ref[:, lo:lo + HALF] = (dact * (2.0 * relu_scr[:, lo:lo + HALF])).astype(BF16)

    halves = pltpu.VMEM((N_CHIPS, HALF, D_MODEL), BF16)
    return pl.pallas_call(
        body, name="mlp_core", grid=(T // tm,),
        in_specs=[_rows(tm, D_MODEL), _rows(tm, D_MODEL), _rows(tm, D_MODEL), _const((1, D_MODEL))] + [ANY] * 4,
        out_specs=[_rows(tm, D_FF), _rows(tm, D_MODEL), _rows(tm, D_MODEL), _rows(tm, D_FF), _const((8, LANES)),
                   _const((1, D_MODEL))],
        out_shape=[_sds((T, D_FF), BF16), _sds((T, D_MODEL), BF16), _sds((T, D_MODEL), F32), _sds((T, D_FF), BF16),
                   _sds((8, LANES), F32), _sds((1, D_MODEL), F32)],
        scratch_shapes=[halves] * 4 + [pltpu.VMEM((tm, D_FF), F32), pltpu.SemaphoreType.DMA((4,))],
        compiler_params=_cp("arbitrary"),
    )(h2, x1, tgt, g4, *w_up, *w_down)


def _mlp_down_bwd(dff, up, w_down):
    T = dff.shape[0]
    tm = min(TM, T)

    def body(d_ref, up_ref, wa_ref, wb_ref, dup_ref):
        d = d_ref[...]
        for j in range(N_CHIPS):
            for part, w_ref in enumerate((wa_ref, wb_ref)):
                lo = D_MODEL * j + HALF * part
                da = _dot_nt(d, w_ref[j])
                relu = jnp.maximum(up_ref[:, lo:lo + HALF].astype(F32), 0.0)
                dup_ref[:, lo:lo + HALF] = (da * (2.0 * relu)).astype(BF16)

    return _call(
        body, name="mlp_down_bwd", grid=(T // tm,),
        in_specs=[_rows(tm, D_MODEL), _rows(tm, D_FF), _HALVES, _HALVES],
        out_specs=[_rows(tm, D_FF)],
        out_shape=[_sds((T, D_FF), BF16)],
        args=(dff, up, *w_down), sem=("parallel",))[0]


def _mlp_up_bwd(dup, dy, x1, mix, w_up, g3, g2, exchanges=()):
    T = dup.shape[0]
    tm = min(TM, T)

    def body(dup_ref, dy_ref, x1_ref, mix_ref, wa_ref, wb_ref, g3_ref, g2_ref, dx1_ref, dmix_ref, dg3_ref, dg2_ref):
        @pl.when(pl.program_id(0) == 0)
        def _():
            dg3_ref[...] = jnp.zeros_like(dg3_ref)
            dg2_ref[...] = jnp.zeros_like(dg2_ref)

        def back(w_ref):
            acc = _dot_nt(dup_ref[:, :D_MODEL], w_ref[0])
            for j in range(1, N_CHIPS):
                acc = acc + _dot_nt(dup_ref[:, D_MODEL * j:D_MODEL * (j + 1)], w_ref[j])
            return acc

        dh2 = jnp.concatenate([back(wa_ref), back(wb_ref)], axis=1)
        dx, dg3 = _norm_bwd(x1_ref[...], g3_ref[...], dh2)
        dx1 = dy_ref[...] + dx
        dx1_ref[...] = dx1
        dg3_ref[...] += dg3
        dmix, dg2 = _norm_bwd(mix_ref[...], g2_ref[...], dx1)
        dmix_ref[...] = dmix.astype(BF16)
        dg2_ref[...] += dg2

    return _call(
        body, name="mlp_up_bwd", grid=(T // tm,),
        in_specs=[_rows(tm, D_FF), _rows(tm, D_MODEL), _rows(tm, D_MODEL), _rows(tm, D_MODEL),
                  _HALVES, _HALVES, _const((1, D_MODEL)), _const((1, D_MODEL))],
        out_specs=[_rows(tm, D_MODEL), _rows(tm, D_MODEL), _const((1, D_MODEL)), _const((1, D_MODEL))],
        out_shape=[_sds((T, D_MODEL), F32), _sds((T, D_MODEL), BF16), _sds((1, D_MODEL), F32),
                   _sds((1, D_MODEL), F32)],
        args=(dup, dy, x1, mix, *w_up, g3, g2), sem=("arbitrary",), exchanges=exchanges)


def _dw(tag, a, g, ta, tn, shard_cols=False, exchanges=()):
    T, ka = a.shape
    n = g.shape[1]
    tk = min(2 * TM, T)
    nk = T // tk

    def body(a_ref, g_ref, o_ref):
        @pl.when(pl.program_id(2) == 0)
        def _():
            o_ref[...] = jnp.zeros_like(o_ref)

        o_ref[...] += _dot_tn(a_ref[...], g_ref[...])

    if shard_cols:
        per = (n // N_CHIPS) // tn
        out_spec = pl.BlockSpec((None, ta, tn), lambda i, j, k: (j // per, i, j % per))
        out_shape = _sds((N_CHIPS, ka, n // N_CHIPS), F32)
    else:
        out_spec = pl.BlockSpec((ta, tn), lambda i, j, k: (i, j))
        out_shape = _sds((ka, n), F32)
    return _call(
        body, name="dw_" + tag, grid=(ka // ta, n // tn, nk),
        in_specs=[pl.BlockSpec((tk, ta), lambda i, j, k: (k, i)), pl.BlockSpec((tk, tn), lambda i, j, k: (k, j))],
        out_specs=[out_spec], out_shape=[out_shape],
        args=(a, g), sem=("parallel", "parallel", "arbitrary"), exchanges=exchanges)


def _dw_slabs(tag, a, g):
    T, ka = a.shape
    n = g.shape[1]
    c = n // N_CHIPS
    tk = min(TM, T)

    def body(a_ref, g_ref, o_ref):
        @pl.when(pl.program_id(0) == 0)
        def _():
            o_ref[...] = jnp.zeros_like(o_ref)

        res = _dot_tn(a_ref[...], g_ref[...])
        for j in range(N_CHIPS):
            o_ref[j] += res[:, c * j:c * (j + 1)]

    return _call(
        body, name="dw_" + tag, grid=(T // tk,),
        in_specs=[_rows(tk, ka), _rows(tk, n)],
        out_specs=[_const((N_CHIPS, ka, c))], out_shape=[_sds((N_CHIPS, ka, c), F32)],
        args=(a, g), sem=("arbitrary",))[0]


def _merge_bwd(dmix, gate, y_pool, y_attn, w_out, w_bp, w_ba, exchanges=()):
    T = dmix.shape[0]
    tm = min(TM, T)

    def body(dmix_ref, gate_ref, yp_ref, ya_ref, wo_ref, wbp_ref, wba_ref,
             dbp_ref, dba_ref, dgate_ref, dyp_ref, dya_ref):
        dm = _dot_nt(dmix_ref[...], wo_ref[...])
        for j, (y_ref, db_ref, w_ref, dy_ref) in enumerate(
                ((yp_ref, dbp_ref, wbp_ref, dyp_ref), (ya_ref, dba_ref, wba_ref, dya_ref))):
            sl = slice(D_MODEL * j, D_MODEL * (j + 1))
            gt = gate_ref[:, sl].astype(F32)
            db = (dm * gt).astype(BF16)
            db_ref[...] = db
            dgate_ref[:, sl] = (dm * _branch(y_ref[...], w_ref) * gt * (1.0 - gt)).astype(BF16)
            cw = D_MODEL // N_CHIPS
            dy = _dot_nt(db[:, :cw], w_ref[0])
            for c in range(1, N_CHIPS):
                dy = dy + _dot_nt(db[:, cw * c:cw * (c + 1)], w_ref[c])
            dy_ref[...] = dy.astype(dy_ref.dtype)

    return _call(
        body, name="merge_bwd", grid=(T // tm,),
        in_specs=[_rows(tm, D_MODEL), _rows(tm, GATE_WIDTH), _rows(tm, POOL_WIDTH), _rows(tm, ATTN_WIDTH),
                  _const((D_MODEL, D_MODEL)), _const(w_bp.shape), _const(w_ba.shape)],
        out_specs=[_rows(tm, D_MODEL), _rows(tm, D_MODEL), _rows(tm, GATE_WIDTH), _rows(tm, POOL_WIDTH),
                   _rows(tm, ATTN_WIDTH)],
        out_shape=[_sds((T, D_MODEL), BF16), _sds((T, D_MODEL), BF16), _sds((T, GATE_WIDTH), BF16),
                   _sds((T, POOL_WIDTH), F32), _sds((T, ATTN_WIDTH), BF16)],
        args=(dmix, gate, y_pool, y_attn, w_out, w_bp, w_ba), sem=("parallel",), exchanges=exchanges)


def _attn_bwd(q, k, v, do, sinks, tabs, seq, exchanges=()):
    T = q.shape[0]
    nb = seq // BLOCK
    bl = T // seq
    steps = nb + 1

    def body(sink_ref, q_ref, do_ref, kp_ref, kc_ref, vp_ref, vc_ref, c_ref, a_ref, bt_ref, cp_ref, ap_ref, btp_ref,
             dq_ref, dk_ref, dv_ref, dsink_ref, ck_ref, cv_ref):
        n = pl.program_id(0)

        @pl.when(n == 0)
        def _():
            dsink_ref[...] = jnp.zeros_like(dsink_ref)
            ck_ref[...] = jnp.zeros_like(ck_ref)
            cv_ref[...] = jnp.zeros_like(cv_ref)

        @pl.when(n < nb)
        def _():
            valid, lo = _attn_masks(n)
            for b in range(bl):
                kk = jnp.concatenate([kp_ref[b], kc_ref[b]], axis=0)
                vv = jnp.concatenate([vp_ref[b], vc_ref[b]], axis=0)
                dk_acc = jnp.zeros((2 * BLOCK, KV_WIDTH), F32)
                dv_acc = jnp.zeros((2 * BLOCK, KV_WIDTH), F32)
                for h in range(2):
                    qs = _stack_heads(q_ref.at[b], h, lo)
                    dos = _stack_heads(do_ref.at[b], h, lo)
                    pr, ps = _group_probs(qs, kk, valid, _sink_rows(sink_ref, h))
                    dp = _dot_nt(dos, vv)
                    delta = jnp.sum(pr * dp, axis=1, keepdims=True)
                    ds = (pr * (dp - delta)).astype(BF16)
                    dsk = ps * delta
                    for g in range(GROUP):
                        idx = GROUP * h + g
                        dsink_ref[idx:idx + 1, :] += (jnp.zeros((1, LANES), F32)
                                                      - jnp.sum(dsk[BLOCK * g:BLOCK * (g + 1)]))
                    dk_acc = dk_acc + _dot_tn(ds, qs)
                    dv_acc = dv_acc + _dot_tn(pr.astype(BF16), dos)
                    for j, pair in enumerate(_unstack_heads(_dot(ds, kk) * SCALE, h, lo)):
                        sl = slice(LANES * (2 * h + j), LANES * (2 * h + j + 1))
                        dq_ref[b, :, sl] = _rot_bwd(pair, c_ref[...], a_ref[...], bt_ref[...]).astype(BF16)
                fin_k = ck_ref[b] + dk_acc[:BLOCK]
                dk_ref[b] = _rot_bwd(fin_k, cp_ref[...], ap_ref[...], btp_ref[...]).astype(BF16)
                dv_ref[b] = (cv_ref[b] + dv_acc[:BLOCK]).astype(BF16)
                ck_ref[b] = dk_acc[BLOCK:]
                cv_ref[b] = dv_acc[BLOCK:]

        @pl.when(n == nb)
        def _():
            for b in range(bl):
                dk_ref[b] = _rot_bwd(ck_ref[b], cp_ref[...], ap_ref[...], btp_ref[...]).astype(BF16)
                dv_ref[b] = cv_ref[b].astype(BF16)

    cur = lambda n: (0, jnp.minimum(n, nb - 1), 0)
    prv = lambda n: (0, jnp.clip(n - 1, 0, nb - 1), 0)
    tcur = lambda n: (jnp.minimum(n, nb - 1), 0)
    tprv = lambda n: (jnp.clip(n - 1, 0, nb - 1), 0)
    wide = lambda m: pl.BlockSpec((bl, BLOCK, ATTN_WIDTH), m)
    kv = lambda m: pl.BlockSpec((bl, BLOCK, KV_WIDTH), m)
    tab = lambda m: pl.BlockSpec((BLOCK, LANES), m)
    res = _call(
        body, name="attn_bwd", grid=(steps,),
        in_specs=[pl.BlockSpec(memory_space=pltpu.SMEM), wide(cur), wide(cur), kv(prv), kv(cur), kv(prv), kv(cur),
                  tab(tcur), tab(tcur), tab(tcur), tab(tprv), tab(tprv), tab(tprv)],
        out_specs=[wide(cur), kv(prv), kv(prv), _const((8, LANES))],
        out_shape=[_sds((bl, seq, ATTN_WIDTH), BF16), _sds((bl, seq, KV_WIDTH), BF16),
                   _sds((bl, seq, KV_WIDTH), BF16), _sds((8, LANES), F32)],
        scratch=[pltpu.VMEM((bl, BLOCK, KV_WIDTH), F32), pltpu.VMEM((bl, BLOCK, KV_WIDTH), F32)],
        args=(sinks, *_by_example(bl, q, do, k, k, v, v), *tabs, *tabs), sem=("arbitrary",), exchanges=exchanges)
    outs, rest = (res if exchanges else (res, None))
    outs = [outs[0].reshape(T, ATTN_WIDTH), outs[1].reshape(T, KV_WIDTH), outs[2].reshape(T, KV_WIDTH), outs[3]]
    return (outs, rest) if exchanges else outs


def _pool_bwd(dyp, diff, w_pool, pool_scale, seq, exchanges=()):
    T = dyp.shape[0]
    tp = min(TP, seq)
    nseq = seq // tp
    per = tp // HALO
    last_halo = T // HALO - 1

    def body(dy_ref, nxt_ref, diff_ref, w_ref, s_ref, du_ref, dw_ref, ds_ref):
        i = pl.program_id(0)

        @pl.when(i == 0)
        def _():
            dw_ref[...] = jnp.zeros_like(dw_ref)
            ds_ref[...] = jnp.zeros_like(ds_ref)

        last = (i % nseq) == nseq - 1
        nxt = jnp.where(last, 0.0, nxt_ref[...])
        ext = jnp.concatenate([dy_ref[...], nxt], axis=0) * s_ref[...]
        pos = (i % nseq) * tp + lax.broadcasted_iota(jnp.int32, (tp + HALO, 1), 0)
        for gi, w in enumerate(POOL_WINDOWS):
            sl = slice(POOL_GC * gi, POOL_GC * (gi + 1))
            wg = w_ref[gi].astype(BF16)
            dmx = ext[:, sl].astype(BF16)
            ddiff = _dot_nt(dmx, wg)
            s = ddiff * _inv_count(pos, w)
            sh = 1
            while sh < w:
                s = s + pltpu.roll(s, tp + HALO - sh, 0)
                sh *= 2
            du_ref[:, sl] = (s[:tp] - ddiff[:tp]).astype(BF16)
            dg = diff_ref[:, sl]
            dw_ref[gi] += _dot_tn(dg, dmx[:tp])
            ds_ref[:, sl] += jnp.sum(dy_ref[:, sl] * _dot(dg, wg), axis=0, keepdims=True)

    return _call(
        body, name="pool_bwd", grid=(T // tp,),
        in_specs=[_rows(tp, POOL_WIDTH),
                  pl.BlockSpec((HALO, POOL_WIDTH), lambda i: (jnp.minimum((i + 1) * per, last_halo), 0)),
                  _rows(tp, POOL_WIDTH), _const((4, POOL_GC, POOL_GC)), _const((1, POOL_WIDTH))],
        out_specs=[_rows(tp, POOL_WIDTH), _const((4, POOL_GC, POOL_GC)), _const((1, POOL_WIDTH))],
        out_shape=[_sds((T, POOL_WIDTH), BF16), _sds((4, POOL_GC, POOL_GC), F32), _sds((1, POOL_WIDTH), F32)],
        args=(dyp, dyp, diff, w_pool, pool_scale), sem=("arbitrary",), exchanges=exchanges)


_PARTS = ((0, C_Q), (C_Q, C_K), (C_K, C_V), (C_V, C_G), (C_G, IN_WIDTH))


def _inproj_bwd(parts, x2, dx1, w_in_t, g1, exchanges=()):
    T = x2.shape[0]
    tm = min(TM, T)

    def body(du_ref, dq_ref, dk_ref, dv_ref, dgt_ref, x_ref, dx1_ref, w_ref, g_ref, gx_ref, dg_ref):
        @pl.when(pl.program_id(0) == 0)
        def _():
            dg_ref[...] = jnp.zeros_like(dg_ref)

        dh = jnp.zeros((tm, D_MODEL), F32)
        for (lo, hi), p_ref in zip(_PARTS, (du_ref, dq_ref, dk_ref, dv_ref, dgt_ref)):
            dh = dh + _dot(p_ref[...], w_ref[lo:hi, :])
        dx, dg = _norm_bwd(x_ref[...], g_ref[...], dh)
        gx_ref[...] = dx1_ref[...] + dx
        dg_ref[...] += dg

    return _call(
        body, name="inproj_bwd", grid=(T // tm,),
        in_specs=[_rows(tm, hi - lo) for lo, hi in _PARTS]
        + [_rows(tm, D_MODEL), _rows(tm, D_MODEL), _const((IN_WIDTH, D_MODEL)), _const((1, D_MODEL))],
        out_specs=[_rows(tm, D_MODEL), _const((1, D_MODEL))],
        out_shape=[_sds((T, D_MODEL), F32), _sds((1, D_MODEL), F32)],
        args=(*parts, x2, dx1, w_in_t, g1), sem=("arbitrary",), exchanges=exchanges)


def _dw_in(h, parts, exchanges=()):
    T = h.shape[0]
    tk = min(TM, T)

    def body(h_ref, du_ref, dq_ref, dk_ref, dv_ref, dgt_ref, o_ref, db_ref):
        @pl.when(pl.program_id(0) == 0)
        def _():
            o_ref[...] = jnp.zeros_like(o_ref)
            db_ref[...] = jnp.zeros_like(db_ref)

        hh = h_ref[...]
        for (lo, hi), p_ref in zip(_PARTS, (du_ref, dq_ref, dk_ref, dv_ref, dgt_ref)):
            part = p_ref[...]
            o_ref[lo:hi, :] += _dot_tn(part, hh)
            db_ref[:, lo:hi] += jnp.sum(part.astype(F32), axis=0, keepdims=True)

    return _call(
        body, name="dw_in", grid=(T // tk,),
        in_specs=[_rows(tk, D_MODEL)] + [_rows(tk, hi - lo) for lo, hi in _PARTS],
        out_specs=[_const((IN_WIDTH, D_MODEL)), _const((1, IN_WIDTH))],
        out_shape=[_sds((IN_WIDTH, D_MODEL), F32), _sds((1, IN_WIDTH), F32)],
        args=(h, *parts), sem=("arbitrary",), exchanges=exchanges)


def _row_tile(rows, cap=256, mult=16):
    best = None
    for t in range(mult, min(rows, cap) + 1, mult):
        if rows % t == 0:
            best = t
    if best is None:
        raise ValueError("no row tile for %d rows" % rows)
    return best


def _pair_sum(ids, full, got):
    _, r, c = full.shape
    hr = r // 2
    tr = _row_tile(hr)
    nblk = hr // tr

    def body(ids_ref, a_ref, b_ref, own_ref, sb_ref):
        s = a_ref[...] + b_ref[...]
        sb_ref[...] = s.astype(BF16)

        @pl.when(pl.program_id(1) == ids_ref[0])
        def _():
            own_ref[...] = s

    slab = pl.BlockSpec((None, tr, c), lambda i, j, ids_ref: (j, i, 0))
    return pl.pallas_call(
        body, name="pair_sum_%dx%d" % (r, c),
        grid_spec=pltpu.PrefetchScalarGridSpec(
            num_scalar_prefetch=1, grid=(nblk, N_CHIPS),
            in_specs=[pl.BlockSpec((None, tr, c), lambda i, j, ids_ref: (j, ids_ref[1] * nblk + i, 0)), slab],
            out_specs=[pl.BlockSpec((tr, c), lambda i, j, ids_ref: (i, 0)), slab]),
        out_shape=[_sds((hr, c), F32), _sds((N_CHIPS, hr, c), BF16)],
        compiler_params=_cp("parallel", "arbitrary"),
    )(ids, full, got)


def _chip_sum(ids, own, got):
    hr, c = own.shape
    tr = _row_tile(hr)
    nblk = hr // tr

    def body(ids_ref, a_ref, b_ref, o_ref):
        o_ref[...] = ((a_ref[...] + b_ref[0].astype(F32)) + b_ref[1].astype(F32)) + b_ref[2].astype(F32)

    return pl.pallas_call(
        body, name="chip_sum_%dx%d" % (hr, c),
        grid_spec=pltpu.PrefetchScalarGridSpec(
            num_scalar_prefetch=1, grid=(nblk,),
            in_specs=[pl.BlockSpec((tr, c), lambda i, ids_ref: (i, 0)),
                      pl.BlockSpec((3, tr, c), lambda i, ids_ref: (0, i, 0))],
            out_specs=pl.BlockSpec((tr, c), lambda i, ids_ref: (ids_ref[1] * nblk + i, 0))),
        out_shape=_sds((2 * hr, c), F32),
        compiler_params=_cp("parallel"),
    )(ids, own, got)


def _adamw_math(w, g, m, v):
    nm = ADAM_B1 * m + (1.0 - ADAM_B1) * g
    nv = ADAM_B2 * v + (1.0 - ADAM_B2) * jnp.square(g)
    m_hat = nm / (1.0 - ADAM_B1 ** ADAM_STEP)
    v_hat = nv / (1.0 - ADAM_B2 ** ADAM_STEP)
    return -ADAM_LR * (m_hat / (jnp.sqrt(v_hat) + ADAM_EPS) + ADAM_WD * w), nm, nv


def _adamw(w, g, m, v):
    r, c = w.shape
    tr = _row_tile(r, cap=512, mult=8)

    def body(w_ref, g_ref, m_ref, v_ref, d_ref, nm_ref, nv_ref):
        d_ref[...], nm_ref[...], nv_ref[...] = _adamw_math(w_ref[...], g_ref[...], m_ref[...], v_ref[...])

    spec = _rows(tr, c)
    return pl.pallas_call(
        body, name="adamw_%dx%d" % (r, c), grid=(r // tr,),
        in_specs=[spec] * 4, out_specs=[spec] * 3, out_shape=[_sds((r, c), F32)] * 3,
        compiler_params=_cp("parallel"),
    )(w, g, m, v)


_SMALL_NAMES = ("w_pool", "b_in", "g_mix_pre", "g_mix_post", "g_mlp_pre", "g_mlp_post", "pool_scale", "attn_sinks")
B_ROWS = -(-IN_WIDTH // D_MODEL)


def _row_block(rows):
    rows = [jnp.pad(r.astype(F32), ((0, 0), (0, D_MODEL - r.shape[1]))) for r in rows]
    return jnp.pad(jnp.concatenate(rows, axis=0), ((0, 8 - len(rows)), (0, 0)))


def _early_block(dg2, dg3, dg4, dps, dsink, loss):
    tail = jnp.concatenate([jnp.pad(dsink.reshape(1, -1), ((0, 0), (0, LANES - dsink.size))),
                            jnp.pad(loss.reshape(1, 1), ((0, 0), (0, LANES - 1)))], axis=1)
    return _row_block([dg2, dg3, dg4, dps, tail])


def _late_block(db_in, dg1):
    b = jnp.pad(db_in, ((0, 0), (0, B_ROWS * D_MODEL - IN_WIDTH))).reshape(B_ROWS, D_MODEL)
    return _row_block([b[r:r + 1] for r in range(B_ROWS)] + [dg1])


def _small_update(gearly, gmat, glate, w, m, v):
    names = _SMALL_NAMES
    n = len(names)

    def total(ref, rows):
        acc = ref[0:rows, :]
        for d in range(1, N_DEV):
            acc = acc + ref[d * rows:(d + 1) * rows, :]
        return acc

    def body(*refs):
        early_ref, gmat_ref, late_ref = refs[:3]
        w_refs, m_refs, v_refs = refs[3:3 + n], refs[3 + n:3 + 2 * n], refs[3 + 2 * n:3 + 3 * n]
        outs = refs[3 + 3 * n:]
        loss_ref, g_refs, d_refs = outs[0], outs[1:1 + n], outs[1 + n:1 + 2 * n]
        nm_refs, nv_refs = outs[1 + 2 * n:1 + 3 * n], outs[1 + 3 * n:1 + 4 * n]
        early, late = total(early_ref, 8), total(late_ref, 8)
        loss_ref[...] = jnp.sum(early[4:5, LANES:2 * LANES], axis=1, keepdims=True)
        bias = jnp.concatenate([late[r:r + 1, :] for r in range(B_ROWS - 1)]
                               + [late[B_ROWS - 1:B_ROWS, :IN_WIDTH - (B_ROWS - 1) * D_MODEL]], axis=1)
        grad = dict(b_in=bias, g_mix_pre=late[B_ROWS:B_ROWS + 1, :], g_mix_post=early[0:1, :],
                    g_mlp_pre=early[1:2, :], g_mlp_post=early[2:3, :], pool_scale=early[3:4, :POOL_WIDTH],
                    attn_sinks=early[4:5, :N_Q_HEADS])
        for i, name in enumerate(names):
            g = total(gmat_ref, 4 * POOL_GC) if name == "w_pool" else grad[name]
            g_refs[i][...] = g
            d_refs[i][...], nm_refs[i][...], nv_refs[i][...] = _adamw_math(
                w_refs[i][...], g, m_refs[i][...], v_refs[i][...])

    shapes = [_sds(w[k].shape, F32) for k in names]
    res = pl.pallas_call(
        body, name="small_update", out_shape=[_sds((1, 1), F32)] + shapes * 4,
        compiler_params=pltpu.CompilerParams(vmem_limit_bytes=VMEM_MB * 1024 * 1024),
    )(gearly, gmat, glate, *[w[k] for k in names], *[m[k] for k in names], *[v[k] for k in names])
    loss = res[0]
    per = {k: tuple(res[1 + j * n + i] for j in range(4)) for i, k in enumerate(names)}
    return loss, per


_BIG = ("w_in", "w_branch_pool", "w_branch_attn", "w_out", "w_up", "w_down")
_ORDER = ("g_mix_pre", "w_in", "b_in", "w_pool", "pool_scale", "attn_sinks", "w_branch_pool", "w_branch_attn",
          "w_out", "g_mix_post", "g_mlp_pre", "w_up", "w_down", "g_mlp_post")


def _stack_rows(slab):
    return slab.reshape(-1, slab.shape[2])


def _step(x2, tgt, seq, shards, small, ids):
    tabs = _rope_tables(seq)
    g1, g2, g3, g4 = (small[n] for n in ("g_mix_pre", "g_mix_post", "g_mlp_pre", "g_mlp_post"))
    sinks = small["attn_sinks"].reshape(N_Q_HEADS)
    w_pool = small["w_pool"].reshape(4, POOL_GC, POOL_GC)
    pool_scale = small["pool_scale"]

    def whole(shard, slabs):
        return lax.dynamic_update_slice(slabs, shard[None], (ids[0], 0, 0))

    up_a, up_b = shards["w_up"][:HALF], shards["w_up"][HALF:]
    down_a, down_b = shards["w_down"][:HALF], shards["w_down"][HALF:]
    w_in = _stack_rows(whole(shards["w_in"], _alone("gather_in", _ex_gather([shards["w_in"]]))[0][0]))
    mix_shards = [shards[n] for n in ("w_branch_pool", "w_branch_attn", "w_out")]
    (h, u, q, k, v, gate), [(*mix_slabs, got_c)] = _inproj(
        x2, g1, w_in, small["b_in"], tabs, seq, exchanges=[_ex_gather(mix_shards + [down_a])])
    w_bp, w_ba, out_slab = (whole(s, g) for s, g in zip(mix_shards, mix_slabs))
    w_out = _stack_rows(out_slab)
    diff, y_pool = _pool_fwd(u, w_pool, pool_scale, seq)
    (y_attn,), [[got_a]] = _attn_fwd(q, k, v, sinks, seq, exchanges=[_ex_gather([up_a])])
    (merged, mix, x1, h2), [[got_b, got_d]] = _merge_out(
        y_pool, y_attn, gate, x2, w_bp, w_ba, w_out, g2, g3, exchanges=[_ex_gather([up_b, down_b])])
    w_up = (whole(up_a, got_a), whole(up_b, got_b))
    w_down = (whole(down_a, got_c), whole(down_b, got_d))
    act, dff, dy, dup, loss_acc, dg4 = _mlp_core(h2, x1, tgt, w_up, w_down, g4)

    dw_down =_dw("down", act, dff, 1024, 1024)[0].reshape(N_CHIPS, D_FF // N_CHIPS, D_MODEL)
    (dx1, dmix, dg3, dg2), [[got]] = _mlp_up_bwd(dup, dy, x1, mix, w_up, g3, g2, exchanges=[_ex_pair([dw_down])])
    ps_down = _pair_sum(ids, dw_down, got)
    (dw_up,), [[got]] = _dw("up", h2, dup, 1024, 1024, shard_cols=True, exchanges=[_ex_chip([ps_down[1]])])
    half_down = _chip_sum(ids, ps_down[0], got)
    (dbp, dba, dgate, dyp, dya), [[got], [g_down]] = _merge_bwd(
        dmix, gate, y_pool, y_attn, w_out, w_bp, w_ba, exchanges=[_ex_pair([dw_up]), _ex_swap([half_down])])
    ps_up = _pair_sum(ids, dw_up, got)
    dw_mix = [_dw("out", merged, dmix, 1024, 1024)[0].reshape(N_CHIPS, D_MODEL // N_CHIPS, D_MODEL),
              _dw_slabs("branch_pool", y_pool, dbp), _dw_slabs("branch_attn", y_attn, dba)]
    (dq, dk, dv, dsink), [[got], gots] = _attn_bwd(
        q, k, v, dya, sinks, tabs, seq, exchanges=[_ex_chip([ps_up[1]]), _ex_pair(dw_mix)])
    half_up = _chip_sum(ids, ps_up[0], got)
    ps_mix = [_pair_sum(ids, d, g) for d, g in zip(dw_mix, gots)]
    (du, dw_pool, dps), [[g_up]] = _pool_bwd(dyp, diff, w_pool, pool_scale, seq, exchanges=[_ex_swap([half_up])])
    parts = (du, dq, dk, dv, dgate)
    early = _early_block(dg2, dg3, dg4, dps, dsink[:, 0], loss_acc[0, 0])
    mat = dw_pool.reshape(4 * POOL_GC, POOL_GC)
    (dw_in_t, db_in), [gots, [gearly, gmat]] = _dw_in(
        h, parts, exchanges=[_ex_chip([p[1] for p in ps_mix]), _ex_allgather([early, mat])])
    half_mix = [_chip_sum(ids, p[0], g) for p, g in zip(ps_mix, gots)]
    dw_in = dw_in_t.reshape(N_CHIPS, IN_WIDTH // N_CHIPS, D_MODEL)
    g_mix, [got] = _alone("swap_mix_pair_in", _ex_swap(half_mix), _ex_pair([dw_in]))
    ps_in = _pair_sum(ids, dw_in, got)
    (gx, dg1), [[got]] = _inproj_bwd(parts, x2, dx1, w_in, g1, exchanges=[_ex_chip([ps_in[1]])])
    [g_in], [glate] = _alone("swap_in_allgather", _ex_swap([_chip_sum(ids, ps_in[0], got)]),
                             _ex_allgather([_late_block(db_in, dg1)]))

    grads = dict(w_in=g_in, w_branch_pool=g_mix[1], w_branch_attn=g_mix[2], w_out=g_mix[0], w_up=g_up, w_down=g_down)
    return (gearly, gmat, glate), gx, grads


def kernel(x, g_mix_pre, w_in, b_in, w_pool, pool_scale, attn_sinks, w_branch_pool, w_branch_attn, w_out, g_mix_post, g_mlp_pre, w_up, w_down, g_mlp_post, loss_target, m_g_mix_pre, m_w_in, m_b_in, m_w_pool, m_pool_scale, m_attn_sinks, m_w_branch_pool, m_w_branch_attn, m_w_out, m_g_mix_post, m_g_mlp_pre, m_w_up, m_w_down, m_g_mlp_post, v_g_mix_pre, v_w_in, v_b_in, v_w_pool, v_pool_scale, v_attn_sinks, v_w_branch_pool, v_w_branch_attn, v_w_out, v_g_mix_post, v_g_mlp_pre, v_w_up, v_w_down, v_g_mlp_post):
    weights = dict(g_mix_pre=g_mix_pre, w_in=w_in, b_in=b_in, w_pool=w_pool, pool_scale=pool_scale,
                   attn_sinks=attn_sinks, w_branch_pool=w_branch_pool, w_branch_attn=w_branch_attn, w_out=w_out,
                   g_mix_post=g_mix_post, g_mlp_pre=g_mlp_pre, w_up=w_up, w_down=w_down, g_mlp_post=g_mlp_post)
    mom1 = dict(g_mix_pre=m_g_mix_pre, w_in=m_w_in, b_in=m_b_in, w_pool=m_w_pool, pool_scale=m_pool_scale,
                attn_sinks=m_attn_sinks, w_branch_pool=m_w_branch_pool, w_branch_attn=m_w_branch_attn,
                w_out=m_w_out, g_mix_post=m_g_mix_post, g_mlp_pre=m_g_mlp_pre, w_up=m_w_up, w_down=m_w_down,
                g_mlp_post=m_g_mlp_post)
    mom2 = dict(g_mix_pre=v_g_mix_pre, w_in=v_w_in, b_in=v_b_in, w_pool=v_w_pool, pool_scale=v_pool_scale,
                attn_sinks=v_attn_sinks, w_branch_pool=v_w_branch_pool, w_branch_attn=v_w_branch_attn,
                w_out=v_w_out, g_mix_post=v_g_mix_post, g_mlp_pre=v_g_mlp_pre, w_up=v_w_up, w_down=v_w_down,
                g_mlp_post=v_g_mlp_post)
    b_loc, seq, _ = x.shape
    x2 = x.reshape(b_loc * seq, D_MODEL)
    tgt = loss_target.reshape(b_loc * seq, D_MODEL)
    ids = jnp.stack([2 * lax.axis_index("x") + lax.axis_index("y"), lax.axis_index("c")]).astype(jnp.int32)

    def flat(n, a):
        return a[0].T if n == "w_in" else a[0]

    def unflat(n, a):
        return (a.T if n == "w_in" else a)[None]

    shards = {n: flat(n, weights[n]).astype(BF16) for n in _BIG}
    small = {n: weights[n] for n in _ORDER if n not in _BIG}
    (gearly, gmat, glate), gx, grads = _step(x2, tgt, seq, shards, small, ids)

    def two_d(src):
        return {n: src[n].reshape(4 * POOL_GC, POOL_GC) if n == "w_pool" else src[n] for n in _SMALL_NAMES}

    loss, per = _small_update(gearly, gmat, glate, two_d(weights), two_d(mom1), two_d(mom2))
    delta, new_m, new_v = {}, {}, {}
    for n in _SMALL_NAMES:
        grads[n], delta[n], new_m[n], new_v[n] = (a.reshape(weights[n].shape) for a in per[n])
    for n in _BIG:
        d, nm, nv = _adamw(flat(n, weights[n]), grads[n], flat(n, mom1[n]), flat(n, mom2[n]))
        grads[n] = unflat(n, grads[n])
        delta[n], new_m[n], new_v[n] = unflat(n, d), unflat(n, nm), unflat(n, nv)

    return (loss[0, 0], gx.reshape(x.shape), *[grads[n] for n in _ORDER], *[delta[n] for n in _ORDER],
            *[new_m[n] for n in _ORDER], *[new_v[n] for n in _ORDER])
```

```python
import jax
import jax.numpy as jnp
from jax import lax
from jax.experimental import pallas as pl
from jax.experimental.pallas import tpu as pltpu

F32 = jnp.float32
BF16 = jnp.bfloat16

D_MODEL = 1024
POOL_WINDOWS = (2, 4, 8, 16)
POOL_WIDTH = 512
POOL_GC = 128
HALO = 16
HEAD_DIM = 64
N_Q_HEADS = 8
ATTN_WIDTH = 512
KV_WIDTH = 128
BLOCK = 128
NEG_INF = -1e30
ROPE_THETA = 500000.0
ROT_DIM = 16
GATE_WIDTH = 2048
IN_WIDTH = 3328
D_FF = 4096
EPS = 1e-6
SCALE = HEAD_DIM ** -0.5
C_Q, C_K, C_V, C_G = 512, 1024, 1152, 1280

ADAM_LR, ADAM_B1, ADAM_B2, ADAM_EPS, ADAM_WD, ADAM_STEP = 0.001, 0.9, 0.999, 1e-08, 0.01, 10

N_CHIPS = 4
N_DEV = 8
LANES = 128
TM = 512
TP = 512
VMEM_MB = 56

MESH = pl.DeviceIdType.MESH
ANY = pl.BlockSpec(memory_space=pl.ANY)


def _cp(*sem, vmem=VMEM_MB):
    return pltpu.CompilerParams(dimension_semantics=sem, vmem_limit_bytes=vmem * 1024 * 1024)


def _rows(tile, cols):
    return pl.BlockSpec((tile, cols), lambda i: (i, 0))


def _const(shape):
    nd = len(shape)
    return pl.BlockSpec(shape, lambda i: (0,) * nd)


def _sds(shape, dtype):
    return jax.ShapeDtypeStruct(shape, dtype)


def _dot(a, b):
    return jnp.dot(a, b, preferred_element_type=F32)


def _dot_nt(a, b):
    return lax.dot_general(a, b, (((1,), (1,)), ((), ())), preferred_element_type=F32)


def _dot_tn(a, b):
    return lax.dot_general(a, b, (((0,), (0,)), ((), ())), preferred_element_type=F32)


def _rms(x):
    return lax.rsqrt(jnp.mean(x * x, axis=-1, keepdims=True) + EPS)


def _norm_bwd(x, g, dout):
    r = _rms(x)
    n = x * r
    dn = dout * g
    dx = r * (dn - n * jnp.mean(dn * n, axis=-1, keepdims=True))
    return dx, jnp.sum(dout * n, axis=0, keepdims=True)


def _rot_fwd(t, c, a, bt):
    return t * c + pltpu.roll(t, LANES - 8, 1) * a + pltpu.roll(t, 8, 1) * bt


def _rot_bwd(d, c, a, bt):
    return d * c + pltpu.roll(d * a, 8, 1) + pltpu.roll(d * bt, LANES - 8, 1)


def _rope_tables(seq):
    pos = jnp.arange(seq, dtype=F32)
    inv_freq = ROPE_THETA ** (-jnp.arange(0, ROT_DIM, 2, dtype=F32) / ROT_DIM)
    ang = pos[:, None] * inv_freq[None, :]
    cos, sin = jnp.cos(ang), jnp.sin(ang)
    ones = jnp.ones((seq, HEAD_DIM - ROT_DIM), F32)
    zeros8 = jnp.zeros((seq, 8), F32)
    zrest = jnp.zeros((seq, HEAD_DIM - ROT_DIM), F32)
    c = jnp.concatenate([cos, cos, ones], axis=1)
    a = jnp.concatenate([-sin, zeros8, zrest], axis=1)
    bt = jnp.concatenate([zeros8, sin, zrest], axis=1)
    return tuple(jnp.tile(t, (1, 2)) for t in (c, a, bt))


class _Exchange:
    def __init__(self, inputs, out_shapes, sems, start, finish, aliases=None, middle=None):
        self.inputs, self.out_shapes, self.sems = list(inputs), list(out_shapes), list(sems)
        self.start, self.finish, self.aliases = start, finish, dict(aliases or {})
        self.middle = middle


def _call(body, *, name, grid, in_specs, out_specs, out_shape, args, scratch=(), sem=(), exchanges=()):
    in_specs, out_specs, out_shape, scratch = list(in_specs), list(out_specs), list(out_shape), list(scratch)
    if not exchanges:
        return pl.pallas_call(body, name=name, grid=grid, in_specs=in_specs, out_specs=out_specs,
                              out_shape=out_shape, scratch_shapes=scratch, compiler_params=_cp(*sem))(*args)
    n_in, n_out, n_scr = len(in_specs), len(out_specs), len(scratch)
    x_in = [a for ex in exchanges for a in ex.inputs]
    x_out = [s for ex in exchanges for s in ex.out_shapes]
    x_sem = [s for ex in exchanges for s in ex.sems]
    aliases, i_off, o_off = {}, n_in, n_out
    for ex in exchanges:
        for i, o in ex.aliases.items():
            aliases[i_off + i] = o_off + o
        i_off += len(ex.inputs)
        o_off += len(ex.out_shapes)

    def split(flat):
        out, pos = [], 0
        for ex, n in zip(exchanges, flat[1]):
            out.append(flat[0][pos:pos + n])
            pos += n
        return out

    def carrier(*refs):
        pos = 0
        groups = []
        for n in (n_in, len(x_in), n_out, len(x_out), n_scr, len(x_sem)):
            groups.append(refs[pos:pos + n])
            pos += n
        ins, xin, outs, xout, scr, xsem = groups
        xin = split((xin, [len(ex.inputs) for ex in exchanges]))
        xout = split((xout, [len(ex.out_shapes) for ex in exchanges]))
        xsem = split((xsem, [len(ex.sems) for ex in exchanges]))
        first = pl.program_id(0) == 0
        last = pl.program_id(0) == grid[0] - 1
        for d in range(1, len(grid)):
            first = jnp.logical_and(first, pl.program_id(d) == 0)
            last = jnp.logical_and(last, pl.program_id(d) == grid[d] - 1)

        @pl.when(first)
        def _():
            for ex, i, o, s in zip(exchanges, xin, xout, xsem):
                ex.start(i, o, s)

        if any(ex.middle for ex in exchanges):
            half = pl.program_id(0) == grid[0] // 2
            for d in range(1, len(grid)):
                half = jnp.logical_and(half, pl.program_id(d) == 0)

            @pl.when(half)
            def _():
                for ex, i, o, s in zip(exchanges, xin, xout, xsem):
                    if ex.middle:
                        ex.middle(i, o, s)

        body(*ins, *outs, *scr)

        @pl.when(last)
        def _():
            for ex, i, o, s in zip(exchanges, xin, xout, xsem):
                ex.finish(i, o, s)

    res = pl.pallas_call(
        carrier, name=name, grid=grid, in_specs=in_specs + [ANY] * len(x_in),
        out_specs=out_specs + [ANY] * len(x_out), out_shape=out_shape + x_out,
        scratch_shapes=scratch + x_sem, input_output_aliases=aliases,
        compiler_params=_cp(*(["arbitrary"] * len(grid))),
    )(*args, *x_in)
    return res[:n_out], split((res[n_out:], [len(ex.out_shapes) for ex in exchanges]))


def _alone(name, *exchanges):
    n_in = [len(ex.inputs) for ex in exchanges]
    n_out = [len(ex.out_shapes) for ex in exchanges]
    n_sem = [len(ex.sems) for ex in exchanges]
    aliases, i_off, o_off = {}, 0, 0
    for ex in exchanges:
        for i, o in ex.aliases.items():
            aliases[i_off + i] = o_off + o
        i_off += len(ex.inputs)
        o_off += len(ex.out_shapes)

    def split(flat, counts):
        out, pos = [], 0
        for n in counts:
            out.append(flat[pos:pos + n])
            pos += n
        return out

    def body(*refs):
        ins, outs, sems = split(refs, [sum(n_in), sum(n_out), sum(n_sem)])
        groups = list(zip(exchanges, split(ins, n_in), split(outs, n_out), split(sems, n_sem)))
        for ex, i, o, s in groups:
            ex.start(i, o, s)
        for ex, i, o, s in groups:
            if ex.middle:
                ex.middle(i, o, s)
        for ex, i, o, s in groups:
            ex.finish(i, o, s)

    res = pl.pallas_call(
        body, name=name, in_specs=[ANY] * sum(n_in), out_specs=[ANY] * sum(n_out),
        out_shape=[s for ex in exchanges for s in ex.out_shapes],
        scratch_shapes=[s for ex in exchanges for s in ex.sems], input_output_aliases=aliases,
    )(*[a for ex in exchanges for a in ex.inputs])
    return split(res, n_out)


def _place():
    x, y, c = lax.axis_index("x"), lax.axis_index("y"), lax.axis_index("c")
    chips = [(1 - x, y), (x, 1 - y), (1 - x, 1 - y)]
    return x, y, c, chips


def _remote(src, dst, send, recv, to):
    return pltpu.make_async_remote_copy(src_ref=src, dst_ref=dst, send_sem=send, recv_sem=recv,
                                        device_id=to, device_id_type=MESH)


def _ex_gather(shards):
    nw = len(shards)
    hrs = [s.shape[0] // 2 for s in shards]

    def copies(ins, outs, sems):
        s1, r1, s2, r2, fs, fr = sems
        x, y, c, _ = _place()
        me, xn, yn, dg = (x, y), (1 - x, y), (x, 1 - y), (1 - x, 1 - y)
        nbr = (xn, yn)
        sibling = (x, y, 1 - c)

        def piece(w, chip, core, part=None):
            hr = hrs[w]
            rows = pl.ds(core * hr, hr) if part is None else pl.ds(core * hr + part * (hr // 2), hr // 2)
            return outs[w].at[2 * chip[0] + chip[1], rows]

        def first(w, k):
            return _remote(ins[w].at[pl.ds(c * hrs[w], hrs[w])], piece(w, me, c), s1.at[w, k], r1.at[w, k],
                           (*nbr[k], c))

        def landed(w, k):
            return _remote(piece(w, nbr[k], c), piece(w, nbr[k], c), s1.at[w, k], r1.at[w, k], (*nbr[k], c))

        def onward(w, k):
            return _remote(piece(w, nbr[k], c, k), piece(w, nbr[k], c, k), s2.at[w, k], r2.at[w, k],
                           (*nbr[1 - k], c))

        def arrived(w, k):
            return _remote(piece(w, dg, c, k), piece(w, dg, c, k), s2.at[w, k], r2.at[w, k], (*nbr[1 - k], c))

        def passed(w, j):
            chip = (xn, yn, dg)[j]
            return _remote(piece(w, chip, c), piece(w, chip, c), fs.at[w, j], fr.at[w, j], sibling)

        def handed(w, j):
            chip = (xn, yn, dg)[j]
            return _remote(piece(w, chip, 1 - c), piece(w, chip, 1 - c), fs.at[w, j], fr.at[w, j], sibling)

        return first, landed, onward, arrived, passed, handed

    def start(ins, outs, sems):
        first = copies(ins, outs, sems)[0]
        for w in range(nw):
            for k in range(2):
                first(w, k).start()

    def middle(ins, outs, sems):
        _, landed, onward, _, passed, _ = copies(ins, outs, sems)
        for w in range(nw):
            for k in range(2):
                landed(w, k).wait_recv()
                onward(w, k).start()
                passed(w, k).start()

    def finish(ins, outs, sems):
        first, _, onward, arrived, passed, handed = copies(ins, outs, sems)
        for w in range(nw):
            for k in range(2):
                arrived(w, k).wait_recv()
            passed(w, 2).start()
        for w in range(nw):
            for j in range(3):
                handed(w, j).wait_recv()
        for w in range(nw):
            for k in range(2):
                first(w, k).wait_send()
                onward(w, k).wait_send()
            for j in range(3):
                passed(w, j).wait_send()

    return _Exchange(shards, [_sds((N_CHIPS,) + s.shape, s.dtype) for s in shards],
                     [pltpu.SemaphoreType.DMA((nw, 2))] * 4 + [pltpu.SemaphoreType.DMA((nw, 3))] * 2,
                     start, finish, middle=middle)


def _ex_pair(grads):
    nw = len(grads)

    def copies(ins, outs, sems):
        x, y, c, _ = _place()
        out = []
        for w in range(nw):
            hr = grads[w].shape[1] // 2
            out.append(_remote(ins[w].at[:, pl.ds((1 - c) * hr, hr)], outs[w], sems[0].at[w], sems[1].at[w],
                               (x, y, 1 - c)))
        return out

    def start(ins, outs, sems):
        for cp in copies(ins, outs, sems):
            cp.start()

    def finish(ins, outs, sems):
        for cp in copies(ins, outs, sems):
            cp.wait()

    return _Exchange(grads, [_sds((N_CHIPS, g.shape[1] // 2, g.shape[2]), F32) for g in grads],
                     [pltpu.SemaphoreType.DMA((nw,))] * 2, start, finish)


def _ex_chip(pieces):
    nw = len(pieces)

    def copies(ins, outs, sems):
        x, y, c, chips = _place()
        return [_remote(ins[w].at[2 * cx + cy], outs[w].at[k], sems[0].at[w, k], sems[1].at[w, k], (cx, cy, c))
                for w in range(nw) for k, (cx, cy) in enumerate(chips)]

    def start(ins, outs, sems):
        for cp in copies(ins, outs, sems):
            cp.start()

    def finish(ins, outs, sems):
        for cp in copies(ins, outs, sems):
            cp.wait()

    return _Exchange(pieces, [_sds((3,) + p.shape[1:], BF16) for p in pieces],
                     [pltpu.SemaphoreType.DMA((nw, 3))] * 2, start, finish)


def _ex_swap(fulls):
    nw = len(fulls)

    def start(ins, outs, sems):
        x, y, c, _ = _place()
        for w in range(nw):
            hr = fulls[w].shape[0] // 2
            mine = pl.ds(c * hr, hr)
            _remote(ins[w].at[mine], outs[w].at[mine], sems[0].at[w], sems[1].at[w], (x, y, 1 - c)).start()

    def finish(ins, outs, sems):
        x, y, c, _ = _place()
        for w in range(nw):
            hr = fulls[w].shape[0] // 2
            mine, theirs = pl.ds(c * hr, hr), pl.ds((1 - c) * hr, hr)
            _remote(ins[w].at[mine], outs[w].at[mine], sems[0].at[w], sems[1].at[w], (x, y, 1 - c)).wait_send()
            _remote(ins[w].at[theirs], outs[w].at[theirs], sems[0].at[w], sems[1].at[w], (x, y, 1 - c)).wait_recv()

    return _Exchange(fulls, [_sds(f.shape, F32) for f in fulls], [pltpu.SemaphoreType.DMA((nw,))] * 2,
                     start, finish, aliases={w: w for w in range(nw)})


def _ex_allgather(blocks):
    nb = len(blocks)

    def copies(ins, outs, sems):
        send, recv, lsem = sems
        x, y, c, chips = _place()
        me, sibling = (x, y, c), (x, y, 1 - c)

        def rows(b, px, py, pc):
            m_per = blocks[b].shape[0]
            return outs[b].at[pl.ds((4 * px + 2 * py + pc) * m_per, m_per), :]

        def copy(b, k, blk, to, src=None):
            return _remote(rows(b, *blk) if src is None else src, rows(b, *blk), send.at[b, k], recv.at[b, k], to)

        def mine(b):
            return pltpu.make_async_copy(ins[b], rows(b, *me), lsem.at[b])

        def first(b, k):
            return copy(b, k, me, sibling if k == 0 else (*chips[k - 1], c), src=ins[b])

        def passed(b, j):
            return copy(b, 4 + j, (*chips[j], c), sibling)

        def landed(b, j):
            return copy(b, 1 + j, (*chips[j], c), me)

        def handed(b, k):
            return copy(b, 0, sibling, me) if k == 0 else copy(b, 3 + k, (*chips[k - 1], 1 - c), me)

        return mine, first, passed, landed, handed

    def start(ins, outs, sems):
        mine, first, _, _, _ = copies(ins, outs, sems)
        for b in range(nb):
            mine(b).start()
            for k in range(4):
                first(b, k).start()

    def finish(ins, outs, sems):
        mine, first, passed, landed, handed = copies(ins, outs, sems)
        sent = []
        for b in range(nb):
            for j in range(3):
                landed(b, j).wait_recv()
                cp = passed(b, j)
                cp.start()
                sent.append(cp)
        for b in range(nb):
            for k in range(4):
                handed(b, k).wait_recv()
            for k in range(4):
                first(b, k).wait_send()
        for cp in sent:
            cp.wait_send()
        for b in range(nb):
            mine(b).wait()

    return _Exchange(blocks, [_sds((N_DEV * b.shape[0], b.shape[1]), F32) for b in blocks],
                     [pltpu.SemaphoreType.DMA((nb, 7)), pltpu.SemaphoreType.DMA((nb, 7)), pltpu.SemaphoreType.DMA((nb,))],
                     start, finish)


def _inproj(x2, g1, w_in_t, b_in, tabs, seq, exchanges=()):
    T = x2.shape[0]
    tm = min(TM, seq)
    nseq = seq // tm

    def body(x_ref, g_ref, w_ref, b_ref, c_ref, a_ref, bt_ref, h_ref, u_ref, q_ref, k_ref, v_ref, gate_ref):
        x = x_ref[...]
        h = (x * _rms(x) * g_ref[...]).astype(BF16)
        h_ref[...] = h

        def proj(lo, hi):
            return _dot_nt(h, w_ref[lo:hi, :]) + b_ref[:, lo:hi]

        c, a, bt = c_ref[...], a_ref[...], bt_ref[...]
        u_ref[...] = proj(0, C_Q)
        q = proj(C_Q, C_K)
        for p in range(4):
            sl = slice(LANES * p, LANES * (p + 1))
            q_ref[:, sl] = (_rot_fwd(q[:, sl], c, a, bt) * SCALE).astype(BF16)
        kv = proj(C_K, C_G)
        k_ref[...] = _rot_fwd(kv[:, :KV_WIDTH], c, a, bt).astype(BF16)
        v_ref[...] = kv[:, KV_WIDTH:].astype(BF16)
        for j in range(2):
            lo = C_G + D_MODEL * j
            gate_ref[:, D_MODEL * j:D_MODEL * (j + 1)] = jax.nn.sigmoid(proj(lo, lo + D_MODEL)).astype(BF16)

    tab = pl.BlockSpec((tm, LANES), lambda i: (i % nseq, 0))
    return _call(
        body, name="inproj", grid=(T // tm,),
        in_specs=[_rows(tm, D_MODEL), _const((1, D_MODEL)), _const((IN_WIDTH, D_MODEL)), _const((1, IN_WIDTH)),
                  tab, tab, tab],
        out_specs=[_rows(tm, D_MODEL), _rows(tm, POOL_WIDTH), _rows(tm, ATTN_WIDTH), _rows(tm, KV_WIDTH),
                   _rows(tm, KV_WIDTH), _rows(tm, GATE_WIDTH)],
        out_shape=[_sds((T, D_MODEL), BF16), _sds((T, POOL_WIDTH), F32), _sds((T, ATTN_WIDTH), BF16),
                   _sds((T, KV_WIDTH), BF16), _sds((T, KV_WIDTH), BF16), _sds((T, GATE_WIDTH), BF16)],
        args=(x2, g1, w_in_t, b_in, *tabs), sem=("parallel",), exchanges=exchanges)


def _inv_count(pos, w):
    return 1.0 / jnp.minimum(pos + 1, w).astype(F32)


def _pool_fwd(u, w_pool, pool_scale, seq):
    T = u.shape[0]
    tp = min(TP, seq)
    nseq = seq // tp
    per = tp // HALO

    def body(u_ref, prev_ref, w_ref, s_ref, diff_ref, y_ref):
        i = pl.program_id(0)
        first = (i % nseq) == 0
        prev = jnp.where(first, 0.0, prev_ref[...])
        ext = jnp.concatenate([prev, u_ref[...]], axis=0)
        pos = (i % nseq) * tp + lax.broadcasted_iota(jnp.int32, (tp, 1), 0)
        for gi, w in enumerate(POOL_WINDOWS):
            sl = slice(POOL_GC * gi, POOL_GC * (gi + 1))
            xg = ext[:, sl]
            s = xg
            sh = 1
            while sh < w:
                s = s + pltpu.roll(s, sh, 0)
                sh *= 2
            pooled = s[HALO:] * _inv_count(pos, w)
            diff = (pooled - xg[HALO:]).astype(BF16)
            diff_ref[:, sl] = diff
            mixed = _dot(diff, w_ref[gi].astype(BF16))
            y_ref[:, sl] = (mixed * s_ref[:, sl]).astype(BF16)

    return _call(
        body, name="pool_fwd", grid=(T // tp,),
        in_specs=[_rows(tp, POOL_WIDTH),
                  pl.BlockSpec((HALO, POOL_WIDTH), lambda i: (jnp.maximum(i * per - 1, 0), 0)),
                  _const((4, POOL_GC, POOL_GC)), _const((1, POOL_WIDTH))],
        out_specs=[_rows(tp, POOL_WIDTH), _rows(tp, POOL_WIDTH)],
        out_shape=[_sds((T, POOL_WIDTH), BF16), _sds((T, POOL_WIDTH), BF16)],
        args=(u, u, w_pool, pool_scale), sem=("parallel",))


GROUP = 4
GROWS = GROUP * BLOCK


def _attn_masks(n):
    qi = lax.broadcasted_iota(jnp.int32, (GROWS, 2 * BLOCK), 0) % BLOCK
    kj = lax.broadcasted_iota(jnp.int32, (GROWS, 2 * BLOCK), 1)
    rel = qi + BLOCK - kj
    valid = (rel >= 0) & (rel < BLOCK) & (kj >= jnp.where(n > 0, 0, BLOCK))
    lo = lax.broadcasted_iota(jnp.int32, (BLOCK, LANES), 1) < HEAD_DIM
    return valid, lo


def _by_example(bl, *arrays):
    return [a.reshape(bl, a.shape[0] // bl, a.shape[1]) for a in arrays]


def _stack_heads(ref, h, lo):
    keep = lo if h == 0 else jnp.logical_not(lo)
    pieces = []
    for p in (2 * h, 2 * h + 1):
        xp = ref[:, LANES * p:LANES * (p + 1)].astype(F32)
        for e in range(2):
            t = xp if e == h else pltpu.roll(xp, HEAD_DIM, 1)
            pieces.append(jnp.where(keep, t, 0.0).astype(BF16))
    return jnp.concatenate(pieces, axis=0)


def _unstack_heads(stacked, h, lo):
    pairs = []
    for j in range(2):
        parts = []
        for e in range(2):
            t = stacked[BLOCK * (2 * j + e):BLOCK * (2 * j + e + 1)]
            parts.append(t if e == h else pltpu.roll(t, HEAD_DIM, 1))
        pairs.append(jnp.where(lo, parts[0], parts[1]))
    return pairs


def _sink_rows(sink_ref, h):
    head = lax.broadcasted_iota(jnp.int32, (GROWS, 1), 0) // BLOCK
    col = jnp.zeros((GROWS, 1), F32) + sink_ref[GROUP * h]
    for g in range(1, GROUP):
        col = jnp.where(head == g, sink_ref[GROUP * h + g], col)
    return col


def _group_probs(qs, kk, valid, sink):
    s = jnp.where(valid, _dot_nt(qs, kk), NEG_INF)
    m = jnp.maximum(jnp.max(s, axis=1, keepdims=True), sink)
    ex = jnp.exp(s - m)
    es = jnp.exp(sink - m)
    inv = 1.0 / (jnp.sum(ex, axis=1, keepdims=True) + es)
    return ex * inv, es * inv


def _attn_fwd(q, k, v, sinks, seq, exchanges=()):
    T = q.shape[0]
    nb = seq // BLOCK
    bl = T // seq

    def body(sink_ref, q_ref, kp_ref, kc_ref, vp_ref, vc_ref, o_ref):
        valid, lo = _attn_masks(pl.program_id(0))
        for b in range(bl):
            kk = jnp.concatenate([kp_ref[b], kc_ref[b]], axis=0)
            vv = jnp.concatenate([vp_ref[b], vc_ref[b]], axis=0)
            for h in range(2):
                qs = _stack_heads(q_ref.at[b], h, lo)
                pr, _ = _group_probs(qs, kk, valid, _sink_rows(sink_ref, h))
                o = _dot(pr.astype(BF16), vv)
                for j, pair in enumerate(_unstack_heads(o, h, lo)):
                    p = 2 * h + j
                    o_ref[b, :, LANES * p:LANES * (p + 1)] = pair.astype(BF16)

    cur = lambda n: (0, n, 0)
    prv = lambda n: (0, jnp.maximum(n - 1, 0), 0)
    kv = lambda m: pl.BlockSpec((bl, BLOCK, KV_WIDTH), m)
    res = _call(
        body, name="attn_fwd", grid=(nb,),
        in_specs=[pl.BlockSpec(memory_space=pltpu.SMEM), pl.BlockSpec((bl, BLOCK, ATTN_WIDTH), cur),
                  kv(prv), kv(cur), kv(prv), kv(cur)],
        out_specs=[pl.BlockSpec((bl, BLOCK, ATTN_WIDTH), cur)],
        out_shape=[_sds((bl, seq, ATTN_WIDTH), BF16)],
        args=(sinks, *_by_example(bl, q, k, k, v, v)), sem=("parallel",), exchanges=exchanges)
    if exchanges:
        return [res[0][0].reshape(T, ATTN_WIDTH)], res[1]
    return [res[0].reshape(T, ATTN_WIDTH)]


def _branch(y, w_ref):
    return jnp.concatenate([_dot(y, w_ref[j]) for j in range(N_CHIPS)], axis=1)


def _merge_out(y_pool, y_attn, gate, x2, w_bp, w_ba, w_out, g2, g3, exchanges=()):
    T = x2.shape[0]
    tm = min(TM, T)

    def body(yp_ref, ya_ref, gate_ref, x_ref, wbp_ref, wba_ref, wo_ref, g2_ref, g3_ref,
             mg_ref, mix_ref, x1_ref, h2_ref):
        bp, ba = _branch(yp_ref[...], wbp_ref), _branch(ya_ref[...], wba_ref)
        merged = (gate_ref[:, :D_MODEL].astype(F32) * bp + gate_ref[:, D_MODEL:].astype(F32) * ba).astype(BF16)
        mg_ref[...] = merged
        mix = _dot(merged, wo_ref[...])
        mix_ref[...] = mix
        x1 = x_ref[...] + mix * _rms(mix) * g2_ref[...]
        x1_ref[...] = x1
        h2_ref[...] = (x1 * _rms(x1) * g3_ref[...]).astype(BF16)

    return _call(
        body, name="merge_out", grid=(T // tm,),
        in_specs=[_rows(tm, POOL_WIDTH), _rows(tm, ATTN_WIDTH), _rows(tm, GATE_WIDTH), _rows(tm, D_MODEL),
                  _const(w_bp.shape), _const(w_ba.shape), _const((D_MODEL, D_MODEL)),
                  _const((1, D_MODEL)), _const((1, D_MODEL))],
        out_specs=[_rows(tm, D_MODEL)] * 4,
        out_shape=[_sds((T, D_MODEL), BF16), _sds((T, D_MODEL), F32), _sds((T, D_MODEL), F32),
                   _sds((T, D_MODEL), BF16)],
        args=(y_pool, y_attn, gate, x2, w_bp, w_ba, w_out, g2, g3), sem=("parallel",), exchanges=exchanges)


HALF = D_MODEL // 2
_HALVES = _const((N_CHIPS, HALF, D_MODEL))


def _mlp_up(h2, w_up, exchanges=()):
    T = h2.shape[0]
    tm = min(TM, T)

    def body(h_ref, wa_ref, wb_ref, up_ref, a_ref):
        ha, hb = h_ref[:, :HALF], h_ref[:, HALF:]
        for j in range(N_CHIPS):
            sl = slice(D_MODEL * j, D_MODEL * (j + 1))
            up = _dot(ha, wa_ref[j]) + _dot(hb, wb_ref[j])
            up_ref[:, sl] = up.astype(BF16)
            a_ref[:, sl] = jnp.square(jnp.maximum(up, 0.0)).astype(BF16)

    return _call(
        body, name="mlp_up", grid=(T // tm,),
        in_specs=[_rows(tm, D_MODEL), _HALVES, _HALVES],
        out_specs=[_rows(tm, D_FF), _rows(tm, D_FF)],
        out_shape=[_sds((T, D_FF), BF16), _sds((T, D_FF), BF16)],
        args=(h2, *w_up), sem=("parallel",), exchanges=exchanges)


def _mlp_down_loss(a, x1, tgt, w_down, g4):
    T = a.shape[0]
    tm = min(TM, T)

    def body(a_ref, x1_ref, t_ref, wa_ref, wb_ref, g_ref, dff_ref, dy_ref, loss_ref, dg_ref):
        @pl.when(pl.program_id(0) == 0)
        def _():
            loss_ref[...] = jnp.zeros_like(loss_ref)
            dg_ref[...] = jnp.zeros_like(dg_ref)

        ff = None
        for j in range(N_CHIPS):
            lo = D_MODEL * j
            t = _dot(a_ref[:, lo:lo + HALF], wa_ref[j]) + _dot(a_ref[:, lo + HALF:lo + D_MODEL], wb_ref[j])
            ff = t if ff is None else ff + t
        g = g_ref[...]
        err = x1_ref[...] + ff * _rms(ff) * g - t_ref[...]
        loss_ref[...] += jnp.sum(err * err) * (0.5 / D_MODEL)
        dy = err * (1.0 / D_MODEL)
        dy_ref[...] = dy
        dff, dg = _norm_bwd(ff, g, dy)
        dff_ref[...] = dff.astype(BF16)
        dg_ref[...] += dg

    return _call(
        body, name="mlp_down_loss", grid=(T // tm,),
        in_specs=[_rows(tm, D_FF), _rows(tm, D_MODEL), _rows(tm, D_MODEL), _HALVES, _HALVES, _const((1, D_MODEL))],
        out_specs=[_rows(tm, D_MODEL), _rows(tm, D_MODEL), _const((8, LANES)), _const((1, D_MODEL))],
        out_shape=[_sds((T, D_MODEL), BF16), _sds((T, D_MODEL), F32), _sds((8, LANES), F32),
                   _sds((1, D_MODEL), F32)],
        args=(a, x1, tgt, *w_down, g4), sem=("arbitrary",))


TM_MLP = 256


def _mlp_core(h2, x1, mix, tgt, w_up, w_down, g4, g3, g2):
    T = h2.shape[0]
    tm = min(TM_MLP, T)

    def body(h_ref, x1_ref, mix_ref, t_ref, g_ref, g3_ref, g2_ref, ua_hbm, ub_hbm, da_hbm, db_hbm,
             act_ref, dff_ref, dup_ref, dx1_ref, dmix_ref, loss_ref, dg_ref, dg3_ref, dg2_ref,
             ua, ub, da, db, relu_scr, sems):
        @pl.when(pl.program_id(0) == 0)
        def _():
            copies = [pltpu.make_async_copy(src, dst, sems.at[i])
                      for i, (src, dst) in enumerate(((ua_hbm, ua), (ub_hbm, ub), (da_hbm, da), (db_hbm, db)))]
            for cp in copies:
                cp.start()
            loss_ref[...] = jnp.zeros_like(loss_ref)
            for ref in (dg_ref, dg3_ref, dg2_ref):
                ref[...] = jnp.zeros_like(ref)
            for cp in copies:
                cp.wait()

        ha, hb = h_ref[:, :HALF], h_ref[:, HALF:]
        ff = None
        for j in range(N_CHIPS):
            lo = D_MODEL * j
            relu = jnp.maximum(_dot(ha, ua[j]) + _dot(hb, ub[j]), 0.0)
            relu_scr[:, lo:lo + D_MODEL] = relu
            act = jnp.square(relu).astype(BF16)
            act_ref[:, lo:lo + D_MODEL] = act
            t = _dot(act[:, :HALF], da[j]) + _dot(act[:, HALF:], db[j])
            ff = t if ff is None else ff + t
        g = g_ref[...]
        x1 = x1_ref[...]
        err = x1 + ff * _rms(ff) * g - t_ref[...]
        loss_ref[...] += jnp.sum(err * err) * (0.5 / D_MODEL)
        dy = err * (1.0 / D_MODEL)
        dff, dg = _norm_bwd(ff, g, dy)
        dg_ref[...] += dg
        dff = dff.astype(BF16)
        dff_ref[...] = dff
        for j in range(N_CHIPS):
            for part, w in enumerate((da, db)):
                lo = D_MODEL * j + HALF * part
                dact = _dot_nt(dff, w[j])
                dup_ref[:, lo:lo + HALF] = (dact * (2.0 * relu_scr[:, lo:lo + HALF])).astype(BF16)

        def back(w):
            acc = _dot_nt(dup_ref[:, :D_MODEL], w[0])
            for j in range(1, N_CHIPS):
                acc = acc + _dot_nt(dup_ref[:, D_MODEL * j:D_MODEL * (j + 1)], w[j])
            return acc

        dh2 = jnp.concatenate([back(ua), back(ub)], axis=1)
        dx, dg3 = _norm_bwd(x1, g3_ref[...], dh2)
        dx1 = dy + dx
        dx1_ref[...] = dx1
        dg3_ref[...] += dg3
        dmix, dg2 = _norm_bwd(mix_ref[...], g2_ref[...], dx1)
        dmix_ref[...] = dmix.astype(BF16)
        dg2_ref[...] += dg2

    halves = pltpu.VMEM((N_CHIPS, HALF, D_MODEL), BF16)
    gain = _const((1, D_MODEL))
    return pl.pallas_call(
        body, name="mlp_core", grid=(T // tm,),
        in_specs=[_rows(tm, D_MODEL)] * 4 + [gain] * 3 + [ANY] * 4,
        out_specs=[_rows(tm, D_FF), _rows(tm, D_MODEL), _rows(tm, D_FF), _rows(tm, D_MODEL), _rows(tm, D_MODEL),
                   _const((8, LANES)), gain, gain, gain],
        out_shape=[_sds((T, D_FF), BF16), _sds((T, D_MODEL), BF16), _sds((T, D_FF), BF16), _sds((T, D_MODEL), F32),
                   _sds((T, D_MODEL), BF16), _sds((8, LANES), F32)] + [_sds((1, D_MODEL), F32)] * 3,
        scratch_shapes=[halves] * 4 + [pltpu.VMEM((tm, D_FF), F32), pltpu.SemaphoreType.DMA((4,))],
        compiler_params=_cp("arbitrary"),
    )(h2, x1, mix, tgt, g4, g3, g2, *w_up, *w_down)


def _mlp_down_bwd(dff, up, w_down):
    T = dff.shape[0]
    tm = min(TM, T)

    def body(d_ref, up_ref, wa_ref, wb_ref, dup_ref):
        d = d_ref[...]
        for j in range(N_CHIPS):
            for part, w_ref in enumerate((wa_ref, wb_ref)):
                lo = D_MODEL * j + HALF * part
                da = _dot_nt(d, w_ref[j])
                relu = jnp.maximum(up_ref[:, lo:lo + HALF].astype(F32), 0.0)
                dup_ref[:, lo:lo + HALF] = (da * (2.0 * relu)).astype(BF16)

    return _call(
        body, name="mlp_down_bwd", grid=(T // tm,),
        in_specs=[_rows(tm, D_MODEL), _rows(tm, D_FF), _HALVES, _HALVES],
        out_specs=[_rows(tm, D_FF)],
        out_shape=[_sds((T, D_FF), BF16)],
        args=(dff, up, *w_down), sem=("parallel",))[0]


def _mlp_up_bwd(dup, dy, x1, mix, w_up, g3, g2, exchanges=()):
    T = dup.shape[0]
    tm = min(TM, T)

    def body(dup_ref, dy_ref, x1_ref, mix_ref, wa_ref, wb_ref, g3_ref, g2_ref, dx1_ref, dmix_ref, dg3_ref, dg2_ref):
        @pl.when(pl.program_id(0) == 0)
        def _():
            dg3_ref[...] = jnp.zeros_like(dg3_ref)
            dg2_ref[...] = jnp.zeros_like(dg2_ref)

        def back(w_ref):
            acc = _dot_nt(dup_ref[:, :D_MODEL], w_ref[0])
            for j in range(1, N_CHIPS):
                acc = acc + _dot_nt(dup_ref[:, D_MODEL * j:D_MODEL * (j + 1)], w_ref[j])
            return acc

        dh2 = jnp.concatenate([back(wa_ref), back(wb_ref)], axis=1)
        dx, dg3 = _norm_bwd(x1_ref[...], g3_ref[...], dh2)
        dx1 = dy_ref[...] + dx
        dx1_ref[...] = dx1
        dg3_ref[...] += dg3
        dmix, dg2 = _norm_bwd(mix_ref[...], g2_ref[...], dx1)
        dmix_ref[...] = dmix.astype(BF16)
        dg2_ref[...] += dg2

    return _call(
        body, name="mlp_up_bwd", grid=(T // tm,),
        in_specs=[_rows(tm, D_FF), _rows(tm, D_MODEL), _rows(tm, D_MODEL), _rows(tm, D_MODEL),
                  _HALVES, _HALVES, _const((1, D_MODEL)), _const((1, D_MODEL))],
        out_specs=[_rows(tm, D_MODEL), _rows(tm, D_MODEL), _const((1, D_MODEL)), _const((1, D_MODEL))],
        out_shape=[_sds((T, D_MODEL), F32), _sds((T, D_MODEL), BF16), _sds((1, D_MODEL), F32),
                   _sds((1, D_MODEL), F32)],
        args=(dup, dy, x1, mix, *w_up, g3, g2), sem=("arbitrary",), exchanges=exchanges)


def _dw(tag, a, g, ta, tn, shard_cols=False, exchanges=()):
    T, ka = a.shape
    n = g.shape[1]
    tk = min(2 * TM, T)
    nk = T // tk

    def body(a_ref, g_ref, o_ref):
        @pl.when(pl.program_id(2) == 0)
        def _():
            o_ref[...] = jnp.zeros_like(o_ref)

        o_ref[...] += _dot_tn(a_ref[...], g_ref[...])

    if shard_cols:
        per = (n // N_CHIPS) // tn
        out_spec = pl.BlockSpec((None, ta, tn), lambda i, j, k: (j // per, i, j % per))
        out_shape = _sds((N_CHIPS, ka, n // N_CHIPS), F32)
    else:
        out_spec = pl.BlockSpec((ta, tn), lambda i, j, k: (i, j))
        out_shape = _sds((ka, n), F32)
    return _call(
        body, name="dw_" + tag, grid=(ka // ta, n // tn, nk),
        in_specs=[pl.BlockSpec((tk, ta), lambda i, j, k: (k, i)), pl.BlockSpec((tk, tn), lambda i, j, k: (k, j))],
        out_specs=[out_spec], out_shape=[out_shape],
        args=(a, g), sem=("parallel", "parallel", "arbitrary"), exchanges=exchanges)


def _dw_slabs(tag, a, g):
    T, ka = a.shape
    n = g.shape[1]
    c = n // N_CHIPS
    tk = min(TM, T)

    def body(a_ref, g_ref, o_ref):
        @pl.when(pl.program_id(0) == 0)
        def _():
            o_ref[...] = jnp.zeros_like(o_ref)

        res = _dot_tn(a_ref[...], g_ref[...])
        for j in range(N_CHIPS):
            o_ref[j] += res[:, c * j:c * (j + 1)]

    return _call(
        body, name="dw_" + tag, grid=(T // tk,),
        in_specs=[_rows(tk, ka), _rows(tk, n)],
        out_specs=[_const((N_CHIPS, ka, c))], out_shape=[_sds((N_CHIPS, ka, c), F32)],
        args=(a, g), sem=("arbitrary",))[0]


def _merge_bwd(dmix, gate, y_pool, y_attn, w_out, w_bp, w_ba, exchanges=()):
    T = dmix.shape[0]
    tm = min(TM, T)

    def body(dmix_ref, gate_ref, yp_ref, ya_ref, wo_ref, wbp_ref, wba_ref,
             dbp_ref, dba_ref, dgate_ref, dyp_ref, dya_ref):
        dm = _dot_nt(dmix_ref[...], wo_ref[...])
        for j, (y_ref, db_ref, w_ref, dy_ref) in enumerate(
                ((yp_ref, dbp_ref, wbp_ref, dyp_ref), (ya_ref, dba_ref, wba_ref, dya_ref))):
            sl = slice(D_MODEL * j, D_MODEL * (j + 1))
            gt = gate_ref[:, sl].astype(F32)
            db = (dm * gt).astype(BF16)
            db_ref[...] = db
            dgate_ref[:, sl] = (dm * _branch(y_ref[...], w_ref) * gt * (1.0 - gt)).astype(BF16)
            cw = D_MODEL // N_CHIPS
            dy = _dot_nt(db[:, :cw], w_ref[0])
            for c in range(1, N_CHIPS):
                dy = dy + _dot_nt(db[:, cw * c:cw * (c + 1)], w_ref[c])
            dy_ref[...] = dy.astype(dy_ref.dtype)

    return _call(
        body, name="merge_bwd", grid=(T // tm,),
        in_specs=[_rows(tm, D_MODEL), _rows(tm, GATE_WIDTH), _rows(tm, POOL_WIDTH), _rows(tm, ATTN_WIDTH),
                  _const((D_MODEL, D_MODEL)), _const(w_bp.shape), _const(w_ba.shape)],
        out_specs=[_rows(tm, D_MODEL), _rows(tm, D_MODEL), _rows(tm, GATE_WIDTH), _rows(tm, POOL_WIDTH),
                   _rows(tm, ATTN_WIDTH)],
        out_shape=[_sds((T, D_MODEL), BF16), _sds((T, D_MODEL), BF16), _sds((T, GATE_WIDTH), BF16),
                   _sds((T, POOL_WIDTH), F32), _sds((T, ATTN_WIDTH), BF16)],
        args=(dmix, gate, y_pool, y_attn, w_out, w_bp, w_ba), sem=("parallel",), exchanges=exchanges)


def _attn_bwd(q, k, v, do, sinks, tabs, seq, exchanges=()):
    T = q.shape[0]
    nb = seq // BLOCK
    bl = T // seq
    steps = nb + 1

    def body(sink_ref, q_ref, do_ref, kp_ref, kc_ref, vp_ref, vc_ref, c_ref, a_ref, bt_ref, cp_ref, ap_ref, btp_ref,
             dq_ref, dk_ref, dv_ref, dsink_ref, ck_ref, cv_ref):
        n = pl.program_id(0)

        @pl.when(n == 0)
        def _():
            dsink_ref[...] = jnp.zeros_like(dsink_ref)
            ck_ref[...] = jnp.zeros_like(ck_ref)
            cv_ref[...] = jnp.zeros_like(cv_ref)

        @pl.when(n < nb)
        def _():
            valid, lo = _attn_masks(n)
            for b in range(bl):
                kk = jnp.concatenate([kp_ref[b], kc_ref[b]], axis=0)
                vv = jnp.concatenate([vp_ref[b], vc_ref[b]], axis=0)
                dk_acc = jnp.zeros((2 * BLOCK, KV_WIDTH), F32)
                dv_acc = jnp.zeros((2 * BLOCK, KV_WIDTH), F32)
                for h in range(2):
                    qs = _stack_heads(q_ref.at[b], h, lo)
                    dos = _stack_heads(do_ref.at[b], h, lo)
                    pr, ps = _group_probs(qs, kk, valid, _sink_rows(sink_ref, h))
                    dp = _dot_nt(dos, vv)
                    delta = jnp.sum(pr * dp, axis=1, keepdims=True)
                    ds = (pr * (dp - delta)).astype(BF16)
                    dsk = ps * delta
                    for g in range(GROUP):
                        idx = GROUP * h + g
                        dsink_ref[idx:idx + 1, :] += (jnp.zeros((1, LANES), F32)
                                                      - jnp.sum(dsk[BLOCK * g:BLOCK * (g + 1)]))
                    dk_acc = dk_acc + _dot_tn(ds, qs)
                    dv_acc = dv_acc + _dot_tn(pr.astype(BF16), dos)
                    for j, pair in enumerate(_unstack_heads(_dot(ds, kk) * SCALE, h, lo)):
                        sl = slice(LANES * (2 * h + j), LANES * (2 * h + j + 1))
                        dq_ref[b, :, sl] = _rot_bwd(pair, c_ref[...], a_ref[...], bt_ref[...]).astype(BF16)
                fin_k = ck_ref[b] + dk_acc[:BLOCK]
                dk_ref[b] = _rot_bwd(fin_k, cp_ref[...], ap_ref[...], btp_ref[...]).astype(BF16)
                dv_ref[b] = (cv_ref[b] + dv_acc[:BLOCK]).astype(BF16)
                ck_ref[b] = dk_acc[BLOCK:]
                cv_ref[b] = dv_acc[BLOCK:]

        @pl.when(n == nb)
        def _():
            for b in range(bl):
                dk_ref[b] = _rot_bwd(ck_ref[b], cp_ref[...], ap_ref[...], btp_ref[...]).astype(BF16)
                dv_ref[b] = cv_ref[b].astype(BF16)

    cur = lambda n: (0, jnp.minimum(n, nb - 1), 0)
    prv = lambda n: (0, jnp.clip(n - 1, 0, nb - 1), 0)
    tcur = lambda n: (jnp.minimum(n, nb - 1), 0)
    tprv = lambda n: (jnp.clip(n - 1, 0, nb - 1), 0)
    wide = lambda m: pl.BlockSpec((bl, BLOCK, ATTN_WIDTH), m)
    kv = lambda m: pl.BlockSpec((bl, BLOCK, KV_WIDTH), m)
    tab = lambda m: pl.BlockSpec((BLOCK, LANES), m)
    res = _call(
        body, name="attn_bwd", grid=(steps,),
        in_specs=[pl.BlockSpec(memory_space=pltpu.SMEM), wide(cur), wide(cur), kv(prv), kv(cur), kv(prv), kv(cur),
                  tab(tcur), tab(tcur), tab(tcur), tab(tprv), tab(tprv), tab(tprv)],
        out_specs=[wide(cur), kv(prv), kv(prv), _const((8, LANES))],
        out_shape=[_sds((bl, seq, ATTN_WIDTH), BF16), _sds((bl, seq, KV_WIDTH), BF16),
                   _sds((bl, seq, KV_WIDTH), BF16), _sds((8, LANES), F32)],
        scratch=[pltpu.VMEM((bl, BLOCK, KV_WIDTH), F32), pltpu.VMEM((bl, BLOCK, KV_WIDTH), F32)],
        args=(sinks, *_by_example(bl, q, do, k, k, v, v), *tabs, *tabs), sem=("arbitrary",), exchanges=exchanges)
    outs, rest = (res if exchanges else (res, None))
    outs = [outs[0].reshape(T, ATTN_WIDTH), outs[1].reshape(T, KV_WIDTH), outs[2].reshape(T, KV_WIDTH), outs[3]]
    return (outs, rest) if exchanges else outs


def _pool_bwd(dyp, diff, w_pool, pool_scale, seq, exchanges=()):
    T = dyp.shape[0]
    tp = min(TP, seq)
    nseq = seq // tp
    per = tp // HALO
    last_halo = T // HALO - 1

    def body(dy_ref, nxt_ref, diff_ref, w_ref, s_ref, du_ref, dw_ref, ds_ref):
        i = pl.program_id(0)

        @pl.when(i == 0)
        def _():
            dw_ref[...] = jnp.zeros_like(dw_ref)
            ds_ref[...] = jnp.zeros_like(ds_ref)

        last = (i % nseq) == nseq - 1
        nxt = jnp.where(last, 0.0, nxt_ref[...])
        ext = jnp.concatenate([dy_ref[...], nxt], axis=0) * s_ref[...]
        pos = (i % nseq) * tp + lax.broadcasted_iota(jnp.int32, (tp + HALO, 1), 0)
        for gi, w in enumerate(POOL_WINDOWS):
            sl = slice(POOL_GC * gi, POOL_GC * (gi + 1))
            wg = w_ref[gi].astype(BF16)
            dmx = ext[:, sl].astype(BF16)
            ddiff = _dot_nt(dmx, wg)
            s = ddiff * _inv_count(pos, w)
            sh = 1
            while sh < w:
                s = s + pltpu.roll(s, tp + HALO - sh, 0)
                sh *= 2
            du_ref[:, sl] = (s[:tp] - ddiff[:tp]).astype(BF16)
            dg = diff_ref[:, sl]
            dw_ref[gi] += _dot_tn(dg, dmx[:tp])
            ds_ref[:, sl] += jnp.sum(dy_ref[:, sl] * _dot(dg, wg), axis=0, keepdims=True)

    return _call(
        body, name="pool_bwd", grid=(T // tp,),
        in_specs=[_rows(tp, POOL_WIDTH),
                  pl.BlockSpec((HALO, POOL_WIDTH), lambda i: (jnp.minimum((i + 1) * per, last_halo), 0)),
                  _rows(tp, POOL_WIDTH), _const((4, POOL_GC, POOL_GC)), _const((1, POOL_WIDTH))],
        out_specs=[_rows(tp, POOL_WIDTH), _const((4, POOL_GC, POOL_GC)), _const((1, POOL_WIDTH))],
        out_shape=[_sds((T, POOL_WIDTH), BF16), _sds((4, POOL_GC, POOL_GC), F32), _sds((1, POOL_WIDTH), F32)],
        args=(dyp, dyp, diff, w_pool, pool_scale), sem=("arbitrary",), exchanges=exchanges)


_PARTS = ((0, C_Q), (C_Q, C_K), (C_K, C_V), (C_V, C_G), (C_G, IN_WIDTH))


def _inproj_bwd(parts, x2, dx1, w_in_t, g1, exchanges=()):
    T = x2.shape[0]
    tm = min(TM, T)

    def body(du_ref, dq_ref, dk_ref, dv_ref, dgt_ref, x_ref, dx1_ref, w_ref, g_ref, gx_ref, dg_ref):
        @pl.when(pl.program_id(0) == 0)
        def _():
            dg_ref[...] = jnp.zeros_like(dg_ref)

        dh = jnp.zeros((tm, D_MODEL), F32)
        for (lo, hi), p_ref in zip(_PARTS, (du_ref, dq_ref, dk_ref, dv_ref, dgt_ref)):
            dh = dh + _dot(p_ref[...], w_ref[lo:hi, :])
        dx, dg = _norm_bwd(x_ref[...], g_ref[...], dh)
        gx_ref[...] = dx1_ref[...] + dx
        dg_ref[...] += dg

    return _call(
        body, name="inproj_bwd", grid=(T // tm,),
        in_specs=[_rows(tm, hi - lo) for lo, hi in _PARTS]
        + [_rows(tm, D_MODEL), _rows(tm, D_MODEL), _const((IN_WIDTH, D_MODEL)), _const((1, D_MODEL))],
        out_specs=[_rows(tm, D_MODEL), _const((1, D_MODEL))],
        out_shape=[_sds((T, D_MODEL), F32), _sds((1, D_MODEL), F32)],
        args=(*parts, x2, dx1, w_in_t, g1), sem=("arbitrary",), exchanges=exchanges)


def _dw_in(h, parts, exchanges=()):
    T = h.shape[0]
    tk = min(TM, T)

    def body(h_ref, du_ref, dq_ref, dk_ref, dv_ref, dgt_ref, o_ref, db_ref):
        @pl.when(pl.program_id(0) == 0)
        def _():
            o_ref[...] = jnp.zeros_like(o_ref)
            db_ref[...] = jnp.zeros_like(db_ref)

        hh = h_ref[...]
        for (lo, hi), p_ref in zip(_PARTS, (du_ref, dq_ref, dk_ref, dv_ref, dgt_ref)):
            part = p_ref[...]
            o_ref[lo:hi, :] += _dot_tn(part, hh)
            db_ref[:, lo:hi] += jnp.sum(part.astype(F32), axis=0, keepdims=True)

    return _call(
        body, name="dw_in", grid=(T // tk,),
        in_specs=[_rows(tk, D_MODEL)] + [_rows(tk, hi - lo) for lo, hi in _PARTS],
        out_specs=[_const((IN_WIDTH, D_MODEL)), _const((1, IN_WIDTH))],
        out_shape=[_sds((IN_WIDTH, D_MODEL), F32), _sds((1, IN_WIDTH), F32)],
        args=(h, *parts), sem=("arbitrary",), exchanges=exchanges)


def _row_tile(rows, cap=256, mult=16):
    best = None
    for t in range(mult, min(rows, cap) + 1, mult):
        if rows % t == 0:
            best = t
    if best is None:
        raise ValueError("no row tile for %d rows" % rows)
    return best


def _pair_sum(ids, full, got):
    _, r, c = full.shape
    hr = r // 2
    tr = _row_tile(hr)
    nblk = hr // tr

    def body(ids_ref, a_ref, b_ref, own_ref, sb_ref):
        s = a_ref[...] + b_ref[...]
        sb_ref[...] = s.astype(BF16)

        @pl.when(pl.program_id(1) == ids_ref[0])
        def _():
            own_ref[...] = s

    slab = pl.BlockSpec((None, tr, c), lambda i, j, ids_ref: (j, i, 0))
    return pl.pallas_call(
        body, name="pair_sum_%dx%d" % (r, c),
        grid_spec=pltpu.PrefetchScalarGridSpec(
            num_scalar_prefetch=1, grid=(nblk, N_CHIPS),
            in_specs=[pl.BlockSpec((None, tr, c), lambda i, j, ids_ref: (j, ids_ref[1] * nblk + i, 0)), slab],
            out_specs=[pl.BlockSpec((tr, c), lambda i, j, ids_ref: (i, 0)), slab]),
        out_shape=[_sds((hr, c), F32), _sds((N_CHIPS, hr, c), BF16)],
        compiler_params=_cp("parallel", "arbitrary"),
    )(ids, full, got)


def _chip_sum(ids, own, got):
    hr, c = own.shape
    tr = _row_tile(hr)
    nblk = hr // tr

    def body(ids_ref, a_ref, b_ref, o_ref):
        o_ref[...] = ((a_ref[...] + b_ref[0].astype(F32)) + b_ref[1].astype(F32)) + b_ref[2].astype(F32)

    return pl.pallas_call(
        body, name="chip_sum_%dx%d" % (hr, c),
        grid_spec=pltpu.PrefetchScalarGridSpec(
            num_scalar_prefetch=1, grid=(nblk,),
            in_specs=[pl.BlockSpec((tr, c), lambda i, ids_ref: (i, 0)),
                      pl.BlockSpec((3, tr, c), lambda i, ids_ref: (0, i, 0))],
            out_specs=pl.BlockSpec((tr, c), lambda i, ids_ref: (ids_ref[1] * nblk + i, 0))),
        out_shape=_sds((2 * hr, c), F32),
        compiler_params=_cp("parallel"),
    )(ids, own, got)


def _adamw_math(w, g, m, v):
    nm = ADAM_B1 * m + (1.0 - ADAM_B1) * g
    nv = ADAM_B2 * v + (1.0 - ADAM_B2) * jnp.square(g)
    m_hat = nm / (1.0 - ADAM_B1 ** ADAM_STEP)
    v_hat = nv / (1.0 - ADAM_B2 ** ADAM_STEP)
    return -ADAM_LR * (m_hat / (jnp.sqrt(v_hat) + ADAM_EPS) + ADAM_WD * w), nm, nv


def _adamw(w, g, m, v):
    r, c = w.shape
    tr = _row_tile(r, cap=512, mult=8)

    def body(w_ref, g_ref, m_ref, v_ref, d_ref, nm_ref, nv_ref):
        d_ref[...], nm_ref[...], nv_ref[...] = _adamw_math(w_ref[...], g_ref[...], m_ref[...], v_ref[...])

    spec = _rows(tr, c)
    return pl.pallas_call(
        body, name="adamw_%dx%d" % (r, c), grid=(r // tr,),
        in_specs=[spec] * 4, out_specs=[spec] * 3, out_shape=[_sds((r, c), F32)] * 3,
        compiler_params=_cp("parallel"),
    )(w, g, m, v)


_SMALL_NAMES = ("w_pool", "b_in", "g_mix_pre", "g_mix_post", "g_mlp_pre", "g_mlp_post", "pool_scale", "attn_sinks")
B_ROWS = -(-IN_WIDTH // D_MODEL)


def _row_block(rows):
    rows = [jnp.pad(r.astype(F32), ((0, 0), (0, D_MODEL - r.shape[1]))) for r in rows]
    return jnp.pad(jnp.concatenate(rows, axis=0), ((0, 8 - len(rows)), (0, 0)))


def _early_block(dg2, dg3, dg4, dps, dsink, loss):
    tail = jnp.concatenate([jnp.pad(dsink.reshape(1, -1), ((0, 0), (0, LANES - dsink.size))),
                            jnp.pad(loss.reshape(1, 1), ((0, 0), (0, LANES - 1)))], axis=1)
    return _row_block([dg2, dg3, dg4, dps, tail])


def _late_block(db_in, dg1):
    b = jnp.pad(db_in, ((0, 0), (0, B_ROWS * D_MODEL - IN_WIDTH))).reshape(B_ROWS, D_MODEL)
    return _row_block([b[r:r + 1] for r in range(B_ROWS)] + [dg1])


def _small_update(gearly, gmat, glate, w, m, v):
    names = _SMALL_NAMES
    n = len(names)

    def total(ref, rows):
        acc = ref[0:rows, :]
        for d in range(1, N_DEV):
            acc = acc + ref[d * rows:(d + 1) * rows, :]
        return acc

    def body(*refs):
        early_ref, gmat_ref, late_ref = refs[:3]
        w_refs, m_refs, v_refs = refs[3:3 + n], refs[3 + n:3 + 2 * n], refs[3 + 2 * n:3 + 3 * n]
        outs = refs[3 + 3 * n:]
        loss_ref, g_refs, d_refs = outs[0], outs[1:1 + n], outs[1 + n:1 + 2 * n]
        nm_refs, nv_refs = outs[1 + 2 * n:1 + 3 * n], outs[1 + 3 * n:1 + 4 * n]
        early, late = total(early_ref, 8), total(late_ref, 8)
        loss_ref[...] = jnp.sum(early[4:5, LANES:2 * LANES], axis=1, keepdims=True)
        bias = jnp.concatenate([late[r:r + 1, :] for r in range(B_ROWS - 1)]
                               + [late[B_ROWS - 1:B_ROWS, :IN_WIDTH - (B_ROWS - 1) * D_MODEL]], axis=1)
        grad = dict(b_in=bias, g_mix_pre=late[B_ROWS:B_ROWS + 1, :], g_mix_post=early[0:1, :],
                    g_mlp_pre=early[1:2, :], g_mlp_post=early[2:3, :], pool_scale=early[3:4, :POOL_WIDTH],
                    attn_sinks=early[4:5, :N_Q_HEADS])
        for i, name in enumerate(names):
            g = total(gmat_ref, 4 * POOL_GC) if name == "w_pool" else grad[name]
            g_refs[i][...] = g
            d_refs[i][...], nm_refs[i][...], nv_refs[i][...] = _adamw_math(
                w_refs[i][...], g, m_refs[i][...], v_refs[i][...])

    shapes = [_sds(w[k].shape, F32) for k in names]
    res = pl.pallas_call(
        body, name="small_update", out_shape=[_sds((1, 1), F32)] + shapes * 4,
        compiler_params=pltpu.CompilerParams(vmem_limit_bytes=VMEM_MB * 1024 * 1024),
    )(gearly, gmat, glate, *[w[k] for k in names], *[m[k] for k in names], *[v[k] for k in names])
    loss = res[0]
    per = {k: tuple(res[1 + j * n + i] for j in range(4)) for i, k in enumerate(names)}
    return loss, per


_BIG = ("w_in", "w_branch_pool", "w_branch_attn", "w_out", "w_up", "w_down")
_ORDER = ("g_mix_pre", "w_in", "b_in", "w_pool", "pool_scale", "attn_sinks", "w_branch_pool", "w_branch_attn",
          "w_out", "g_mix_post", "g_mlp_pre", "w_up", "w_down", "g_mlp_post")


def _stack_rows(slab):
    return slab.reshape(-1, slab.shape[2])


def _step(x2, tgt, seq, shards, small, ids):
    tabs = _rope_tables(seq)
    g1, g2, g3, g4 = (small[n] for n in ("g_mix_pre", "g_mix_post", "g_mlp_pre", "g_mlp_post"))
    sinks = small["attn_sinks"].reshape(N_Q_HEADS)
    w_pool = small["w_pool"].reshape(4, POOL_GC, POOL_GC)
    pool_scale = small["pool_scale"]

    def whole(shard, slabs):
        return lax.dynamic_update_slice(slabs, shard[None], (ids[0], 0, 0))

    up_a, up_b = shards["w_up"][:HALF], shards["w_up"][HALF:]
    down_a, down_b = shards["w_down"][:HALF], shards["w_down"][HALF:]
    w_in = _stack_rows(whole(shards["w_in"], _alone("gather_in", _ex_gather([shards["w_in"]]))[0][0]))
    mix_shards = [shards[n] for n in ("w_branch_pool", "w_branch_attn", "w_out")]
    (h, u, q, k, v, gate), [(*mix_slabs, got_c)] = _inproj(
        x2, g1, w_in, small["b_in"], tabs, seq, exchanges=[_ex_gather(mix_shards + [down_a])])
    w_bp, w_ba, out_slab = (whole(s, g) for s, g in zip(mix_shards, mix_slabs))
    w_out = _stack_rows(out_slab)
    diff, y_pool = _pool_fwd(u, w_pool, pool_scale, seq)
    (y_attn,), [[got_a]] = _attn_fwd(q, k, v, sinks, seq, exchanges=[_ex_gather([up_a])])
    (merged, mix, x1, h2), [[got_b, got_d]] = _merge_out(
        y_pool, y_attn, gate, x2, w_bp, w_ba, w_out, g2, g3, exchanges=[_ex_gather([up_b, down_b])])
    w_up = (whole(up_a, got_a), whole(up_b, got_b))
    w_down = (whole(down_a, got_c), whole(down_b, got_d))
    act, dff, dup, dx1, dmix, loss_acc, dg4, dg3, dg2 = _mlp_core(h2, x1, mix, tgt, w_up, w_down, g4, g3, g2)

    dw_down = _dw("down", act, dff, 1024, 1024)[0].reshape(N_CHIPS, D_FF // N_CHIPS, D_MODEL)
    (dw_up,), [[got]] = _dw("up", h2, dup, 1024, 1024, shard_cols=True, exchanges=[_ex_pair([dw_down])])
    ps_down = _pair_sum(ids, dw_down, got)
    (dbp, dba, dgate, dyp, dya), [[got_chip], [got_pair]] = _merge_bwd(
        dmix, gate, y_pool, y_attn, w_out, w_bp, w_ba, exchanges=[_ex_chip([ps_down[1]]), _ex_pair([dw_up])])
    half_down = _chip_sum(ids, ps_down[0], got_chip)
    ps_up = _pair_sum(ids, dw_up, got_pair)
    dw_mix = [_dw("out", merged, dmix, 1024, 1024)[0].reshape(N_CHIPS, D_MODEL // N_CHIPS, D_MODEL),
              _dw_slabs("branch_pool", y_pool, dbp), _dw_slabs("branch_attn", y_attn, dba)]
    (dq, dk, dv, dsink), [[got], gots, [g_down]] = _attn_bwd(
        q, k, v, dya, sinks, tabs, seq, exchanges=[_ex_chip([ps_up[1]]), _ex_pair(dw_mix), _ex_swap([half_down])])
    half_up = _chip_sum(ids, ps_up[0], got)
    ps_mix = [_pair_sum(ids, d, g) for d, g in zip(dw_mix, gots)]
    (du, dw_pool, dps), [[g_up]] = _pool_bwd(dyp, diff, w_pool, pool_scale, seq, exchanges=[_ex_swap([half_up])])
    parts = (du, dq, dk, dv, dgate)
    early = _early_block(dg2, dg3, dg4, dps, dsink[:, 0], loss_acc[0, 0])
    mat = dw_pool.reshape(4 * POOL_GC, POOL_GC)
    (dw_in_t, db_in), [gots, [gearly, gmat]] = _dw_in(
        h, parts, exchanges=[_ex_chip([p[1] for p in ps_mix]), _ex_allgather([early, mat])])
    half_mix = [_chip_sum(ids, p[0], g) for p, g in zip(ps_mix, gots)]
    dw_in = dw_in_t.reshape(N_CHIPS, IN_WIDTH // N_CHIPS, D_MODEL)
    g_mix, [got] = _alone("swap_mix_pair_in", _ex_swap(half_mix), _ex_pair([dw_in]))
    ps_in = _pair_sum(ids, dw_in, got)
    (gx, dg1), [[got]] = _inproj_bwd(parts, x2, dx1, w_in, g1, exchanges=[_ex_chip([ps_in[1]])])
    [g_in], [glate] = _alone("swap_in_allgather", _ex_swap([_chip_sum(ids, ps_in[0], got)]),
                             _ex_allgather([_late_block(db_in, dg1)]))

    grads = dict(w_in=g_in, w_branch_pool=g_mix[1], w_branch_attn=g_mix[2], w_out=g_mix[0], w_up=g_up, w_down=g_down)
    return (gearly, gmat, glate), gx, grads


def kernel(x, g_mix_pre, w_in, b_in, w_pool, pool_scale, attn_sinks, w_branch_pool, w_branch_attn, w_out, g_mix_post, g_mlp_pre, w_up, w_down, g_mlp_post, loss_target, m_g_mix_pre, m_w_in, m_b_in, m_w_pool, m_pool_scale, m_attn_sinks, m_w_branch_pool, m_w_branch_attn, m_w_out, m_g_mix_post, m_g_mlp_pre, m_w_up, m_w_down, m_g_mlp_post, v_g_mix_pre, v_w_in, v_b_in, v_w_pool, v_pool_scale, v_attn_sinks, v_w_branch_pool, v_w_branch_attn, v_w_out, v_g_mix_post, v_g_mlp_pre, v_w_up, v_w_down, v_g_mlp_post):
    weights = dict(g_mix_pre=g_mix_pre, w_in=w_in, b_in=b_in, w_pool=w_pool, pool_scale=pool_scale,
                   attn_sinks=attn_sinks, w_branch_pool=w_branch_pool, w_branch_attn=w_branch_attn, w_out=w_out,
                   g_mix_post=g_mix_post, g_mlp_pre=g_mlp_pre, w_up=w_up, w_down=w_down, g_mlp_post=g_mlp_post)
    mom1 = dict(g_mix_pre=m_g_mix_pre, w_in=m_w_in, b_in=m_b_in, w_pool=m_w_pool, pool_scale=m_pool_scale,
                attn_sinks=m_attn_sinks, w_branch_pool=m_w_branch_pool, w_branch_attn=m_w_branch_attn,
                w_out=m_w_out, g_mix_post=m_g_mix_post, g_mlp_pre=m_g_mlp_pre, w_up=m_w_up, w_down=m_w_down,
                g_mlp_post=m_g_mlp_post)
    mom2 = dict(g_mix_pre=v_g_mix_pre, w_in=v_w_in, b_in=v_b_in, w_pool=v_w_pool, pool_scale=v_pool_scale,
                attn_sinks=v_attn_sinks, w_branch_pool=v_w_branch_pool, w_branch_attn=v_w_branch_attn,
                w_out=v_w_out, g_mix_post=v_g_mix_post, g_mlp_pre=v_g_mlp_pre, w_up=v_w_up, w_down=v_w_down,
                g_mlp_post=v_g_mlp_post)
    b_loc, seq, _ = x.shape
    x2 = x.reshape(b_loc * seq, D_MODEL)
    tgt = loss_target.reshape(b_loc * seq, D_MODEL)
    ids = jnp.stack([2 * lax.axis_index("x") + lax.axis_index("y"), lax.axis_index("c")]).astype(jnp.int32)

    def flat(n, a):
        return a[0].T if n == "w_in" else a[0]

    def unflat(n, a):
        return (a.T if n == "w_in" else a)[None]

    shards = {n: flat(n, weights[n]).astype(BF16) for n in _BIG}
    small = {n: weights[n] for n in _ORDER if n not in _BIG}
    (gearly, gmat, glate), gx, grads = _step(x2, tgt, seq, shards, small, ids)

    def two_d(src):
        return {n: src[n].reshape(4 * POOL_GC, POOL_GC) if n == "w_pool" else src[n] for n in _SMALL_NAMES}

    loss, per = _small_update(gearly, gmat, glate, two_d(weights), two_d(mom1), two_d(mom2))
    delta, new_m, new_v = {}, {}, {}
    for n in _SMALL_NAMES:
        grads[n], delta[n], new_m[n], new_v[n] = (a.reshape(weights[n].shape) for a in per[n])
    for n in _BIG:
        d, nm, nv = _adamw(flat(n, weights[n]), grads[n], flat(n, mom1[n]), flat(n, mom2[n]))
        grads[n] = unflat(n, grads[n])
        delta[n], new_m[n], new_v[n] = unflat(n, d), unflat(n, nm), unflat(n, nv)

    return (loss[0, 0], gx.reshape(x.shape), *[grads[n] for n in _ORDER], *[delta[n] for n in _ORDER],
            *[new_m[n] for n in _ORDER], *[new_v[n] for n in _ORDER])
```

```python
import jax
import jax.numpy as jnp
from jax import lax
from jax.experimental import pallas as pl
from jax.experimental.pallas import tpu as pltpu

F32 = jnp.float32
BF16 = jnp.bfloat16

D_MODEL = 1024
POOL_WINDOWS = (2, 4, 8, 16)
POOL_WIDTH = 512
POOL_GC = 128
HALO = 16
HEAD_DIM = 64
N_Q_HEADS = 8
ATTN_WIDTH = 512
KV_WIDTH = 128
BLOCK = 128
NEG_INF = -1e30
ROPE_THETA = 500000.0
ROT_DIM = 16
GATE_WIDTH = 2048
IN_WIDTH = 3328
D_FF = 4096
EPS = 1e-6
SCALE = HEAD_DIM ** -0.5
C_Q, C_K, C_V, C_G = 512, 1024, 1152, 1280

ADAM_LR, ADAM_B1, ADAM_B2, ADAM_EPS, ADAM_WD, ADAM_STEP = 0.001, 0.9, 0.999, 1e-08, 0.01, 10

N_CHIPS = 4
N_DEV = 8
LANES = 128
TM = 512
TP = 512
VMEM_MB = 56

MESH = pl.DeviceIdType.MESH
ANY = pl.BlockSpec(memory_space=pl.ANY)


def _cp(*sem, vmem=VMEM_MB):
    return pltpu.CompilerParams(dimension_semantics=sem, vmem_limit_bytes=vmem * 1024 * 1024)


def _rows(tile, cols):
    return pl.BlockSpec((tile, cols), lambda i: (i, 0))


def _const(shape):
    nd = len(shape)
    return pl.BlockSpec(shape, lambda i: (0,) * nd)


def _sds(shape, dtype):
    return jax.ShapeDtypeStruct(shape, dtype)


def _dot(a, b):
    return jnp.dot(a, b, preferred_element_type=F32)


def _dot_nt(a, b):
    return lax.dot_general(a, b, (((1,), (1,)), ((), ())), preferred_element_type=F32)


def _dot_tn(a, b):
    return lax.dot_general(a, b, (((0,), (0,)), ((), ())), preferred_element_type=F32)


def _rms(x):
    return lax.rsqrt(jnp.mean(x * x, axis=-1, keepdims=True) + EPS)


def _norm_bwd(x, g, dout):
    r = _rms(x)
    n = x * r
    dn = dout * g
    dx = r * (dn - n * jnp.mean(dn * n, axis=-1, keepdims=True))
    return dx, jnp.sum(dout * n, axis=0, keepdims=True)


def _rot_fwd(t, c, a, bt):
    return t * c + pltpu.roll(t, LANES - 8, 1) * a + pltpu.roll(t, 8, 1) * bt


def _rot_bwd(d, c, a, bt):
    return d * c + pltpu.roll(d * a, 8, 1) + pltpu.roll(d * bt, LANES - 8, 1)


def _rope_tables(seq):
    pos = jnp.arange(seq, dtype=F32)
    inv_freq = ROPE_THETA ** (-jnp.arange(0, ROT_DIM, 2, dtype=F32) / ROT_DIM)
    ang = pos[:, None] * inv_freq[None, :]
    cos, sin = jnp.cos(ang), jnp.sin(ang)
    ones = jnp.ones((seq, HEAD_DIM - ROT_DIM), F32)
    zeros8 = jnp.zeros((seq, 8), F32)
    zrest = jnp.zeros((seq, HEAD_DIM - ROT_DIM), F32)
    c = jnp.concatenate([cos, cos, ones], axis=1)
    a = jnp.concatenate([-sin, zeros8, zrest], axis=1)
    bt = jnp.concatenate([zeros8, sin, zrest], axis=1)
    return tuple(jnp.tile(t, (1, 2)) for t in (c, a, bt))


class _Exchange:
    def __init__(self, inputs, out_shapes, sems, start, finish, aliases=None, middle=None):
        self.inputs, self.out_shapes, self.sems = list(inputs), list(out_shapes), list(sems)
        self.start, self.finish, self.aliases = start, finish, dict(aliases or {})
        self.middle = middle


def _call(body, *, name, grid, in_specs, out_specs, out_shape, args, scratch=(), sem=(), exchanges=()):
    in_specs, out_specs, out_shape, scratch = list(in_specs), list(out_specs), list(out_shape), list(scratch)
    if not exchanges:
        return pl.pallas_call(body, name=name, grid=grid, in_specs=in_specs, out_specs=out_specs,
                              out_shape=out_shape, scratch_shapes=scratch, compiler_params=_cp(*sem))(*args)
    n_in, n_out, n_scr = len(in_specs), len(out_specs), len(scratch)
    x_in = [a for ex in exchanges for a in ex.inputs]
    x_out = [s for ex in exchanges for s in ex.out_shapes]
    x_sem = [s for ex in exchanges for s in ex.sems]
    aliases, i_off, o_off = {}, n_in, n_out
    for ex in exchanges:
        for i, o in ex.aliases.items():
            aliases[i_off + i] = o_off + o
        i_off += len(ex.inputs)
        o_off += len(ex.out_shapes)

    def split(flat):
        out, pos = [], 0
        for ex, n in zip(exchanges, flat[1]):
            out.append(flat[0][pos:pos + n])
            pos += n
        return out

    def carrier(*refs):
        pos = 0
        groups = []
        for n in (n_in, len(x_in), n_out, len(x_out), n_scr, len(x_sem)):
            groups.append(refs[pos:pos + n])
            pos += n
        ins, xin, outs, xout, scr, xsem = groups
        xin = split((xin, [len(ex.inputs) for ex in exchanges]))
        xout = split((xout, [len(ex.out_shapes) for ex in exchanges]))
        xsem = split((xsem, [len(ex.sems) for ex in exchanges]))
        first = pl.program_id(0) == 0
        last = pl.program_id(0) == grid[0] - 1
        for d in range(1, len(grid)):
            first = jnp.logical_and(first, pl.program_id(d) == 0)
            last = jnp.logical_and(last, pl.program_id(d) == grid[d] - 1)

        @pl.when(first)
        def _():
            for ex, i, o, s in zip(exchanges, xin, xout, xsem):
                ex.start(i, o, s)

        if any(ex.middle for ex in exchanges):
            half = pl.program_id(0) == grid[0] // 2
            for d in range(1, len(grid)):
                half = jnp.logical_and(half, pl.program_id(d) == 0)

            @pl.when(half)
            def _():
                for ex, i, o, s in zip(exchanges, xin, xout, xsem):
                    if ex.middle:
                        ex.middle(i, o, s)

        body(*ins, *outs, *scr)

        @pl.when(last)
        def _():
            for ex, i, o, s in zip(exchanges, xin, xout, xsem):
                ex.finish(i, o, s)

    res = pl.pallas_call(
        carrier, name=name, grid=grid, in_specs=in_specs + [ANY] * len(x_in),
        out_specs=out_specs + [ANY] * len(x_out), out_shape=out_shape + x_out,
        scratch_shapes=scratch + x_sem, input_output_aliases=aliases,
        compiler_params=_cp(*(["arbitrary"] * len(grid))),
    )(*args, *x_in)
    return res[:n_out], split((res[n_out:], [len(ex.out_shapes) for ex in exchanges]))


def _alone(name, *exchanges):
    n_in = [len(ex.inputs) for ex in exchanges]
    n_out = [len(ex.out_shapes) for ex in exchanges]
    n_sem = [len(ex.sems) for ex in exchanges]
    aliases, i_off, o_off = {}, 0, 0
    for ex in exchanges:
        for i, o in ex.aliases.items():
            aliases[i_off + i] = o_off + o
        i_off += len(ex.inputs)
        o_off += len(ex.out_shapes)

    def split(flat, counts):
        out, pos = [], 0
        for n in counts:
            out.append(flat[pos:pos + n])
            pos += n
        return out

    def body(*refs):
        ins, outs, sems = split(refs, [sum(n_in), sum(n_out), sum(n_sem)])
        groups = list(zip(exchanges, split(ins, n_in), split(outs, n_out), split(sems, n_sem)))
        for ex, i, o, s in groups:
            ex.start(i, o, s)
        for ex, i, o, s in groups:
            if ex.middle:
                ex.middle(i, o, s)
        for ex, i, o, s in groups:
            ex.finish(i, o, s)

    res = pl.pallas_call(
        body, name=name, in_specs=[ANY] * sum(n_in), out_specs=[ANY] * sum(n_out),
        out_shape=[s for ex in exchanges for s in ex.out_shapes],
        scratch_shapes=[s for ex in exchanges for s in ex.sems], input_output_aliases=aliases,
    )(*[a for ex in exchanges for a in ex.inputs])
    return split(res, n_out)


def _place():
    x, y, c = lax.axis_index("x"), lax.axis_index("y"), lax.axis_index("c")
    chips = [(1 - x, y), (x, 1 - y), (1 - x, 1 - y)]
    return x, y, c, chips


def _remote(src, dst, send, recv, to):
    return pltpu.make_async_remote_copy(src_ref=src, dst_ref=dst, send_sem=send, recv_sem=recv,
                                        device_id=to, device_id_type=MESH)


def _ex_gather(shards):
    nw = len(shards)
    hrs = [s.shape[0] // 2 for s in shards]

    def copies(ins, outs, sems):
        s1, r1, s2, r2, fs, fr = sems
        x, y, c, _ = _place()
        me, xn, yn, dg = (x, y), (1 - x, y), (x, 1 - y), (1 - x, 1 - y)
        nbr = (xn, yn)
        sibling = (x, y, 1 - c)

        def piece(w, chip, core, part=None):
            hr = hrs[w]
            rows = pl.ds(core * hr, hr) if part is None else pl.ds(core * hr + part * (hr // 2), hr // 2)
            return outs[w].at[2 * chip[0] + chip[1], rows]

        def first(w, k):
            return _remote(ins[w].at[pl.ds(c * hrs[w], hrs[w])], piece(w, me, c), s1.at[w, k], r1.at[w, k],
                           (*nbr[k], c))

        def landed(w, k):
            return _remote(piece(w, nbr[k], c), piece(w, nbr[k], c), s1.at[w, k], r1.at[w, k], (*nbr[k], c))

        def onward(w, k):
            return _remote(piece(w, nbr[k], c, k), piece(w, nbr[k], c, k), s2.at[w, k], r2.at[w, k],
                           (*nbr[1 - k], c))

        def arrived(w, k):
            return _remote(piece(w, dg, c, k), piece(w, dg, c, k), s2.at[w, k], r2.at[w, k], (*nbr[1 - k], c))

        def passed(w, j):
            chip = (xn, yn, dg)[j]
            return _remote(piece(w, chip, c), piece(w, chip, c), fs.at[w, j], fr.at[w, j], sibling)

        def handed(w, j):
            chip = (xn, yn, dg)[j]
            return _remote(piece(w, chip, 1 - c), piece(w, chip, 1 - c), fs.at[w, j], fr.at[w, j], sibling)

        return first, landed, onward, arrived, passed, handed

    def start(ins, outs, sems):
        first = copies(ins, outs, sems)[0]
        for w in range(nw):
            for k in range(2):
                first(w, k).start()

    def middle(ins, outs, sems):
        _, landed, onward, _, passed, _ = copies(ins, outs, sems)
        for w in range(nw):
            for k in range(2):
                landed(w, k).wait_recv()
                onward(w, k).start()
                passed(w, k).start()

    def finish(ins, outs, sems):
        first, _, onward, arrived, passed, handed = copies(ins, outs, sems)
        for w in range(nw):
            for k in range(2):
                arrived(w, k).wait_recv()
            passed(w, 2).start()
        for w in range(nw):
            for j in range(3):
                handed(w, j).wait_recv()
        for w in range(nw):
            for k in range(2):
                first(w, k).wait_send()
                onward(w, k).wait_send()
            for j in range(3):
                passed(w, j).wait_send()

    return _Exchange(shards, [_sds((N_CHIPS,) + s.shape, s.dtype) for s in shards],
                     [pltpu.SemaphoreType.DMA((nw, 2))] * 4 + [pltpu.SemaphoreType.DMA((nw, 3))] * 2,
                     start, finish, middle=middle)


def _ex_pair(grads):
    nw = len(grads)

    def copies(ins, outs, sems):
        x, y, c, _ = _place()
        out = []
        for w in range(nw):
            hr = grads[w].shape[1] // 2
            out.append(_remote(ins[w].at[:, pl.ds((1 - c) * hr, hr)], outs[w], sems[0].at[w], sems[1].at[w],
                               (x, y, 1 - c)))
        return out

    def start(ins, outs, sems):
        for cp in copies(ins, outs, sems):
            cp.start()

    def finish(ins, outs, sems):
        for cp in copies(ins, outs, sems):
            cp.wait()

    return _Exchange(grads, [_sds((N_CHIPS, g.shape[1] // 2, g.shape[2]), F32) for g in grads],
                     [pltpu.SemaphoreType.DMA((nw,))] * 2, start, finish)


def _ex_chip(pieces):
    nw = len(pieces)

    def copies(ins, outs, sems):
        x, y, c, chips = _place()
        return [_remote(ins[w].at[2 * cx + cy], outs[w].at[k], sems[0].at[w, k], sems[1].at[w, k], (cx, cy, c))
                for w in range(nw) for k, (cx, cy) in enumerate(chips)]

    def start(ins, outs, sems):
        for cp in copies(ins, outs, sems):
            cp.start()

    def finish(ins, outs, sems):
        for cp in copies(ins, outs, sems):
            cp.wait()

    return _Exchange(pieces, [_sds((3,) + p.shape[1:], BF16) for p in pieces],
                     [pltpu.SemaphoreType.DMA((nw, 3))] * 2, start, finish)


def _ex_swap(fulls):
    nw = len(fulls)

    def start(ins, outs, sems):
        x, y, c, _ = _place()
        for w in range(nw):
            hr = fulls[w].shape[0] // 2
            mine = pl.ds(c * hr, hr)
            _remote(ins[w].at[mine], outs[w].at[mine], sems[0].at[w], sems[1].at[w], (x, y, 1 - c)).start()

    def finish(ins, outs, sems):
        x, y, c, _ = _place()
        for w in range(nw):
            hr = fulls[w].shape[0] // 2
            mine, theirs = pl.ds(c * hr, hr), pl.ds((1 - c) * hr, hr)
            _remote(ins[w].at[mine], outs[w].at[mine], sems[0].at[w], sems[1].at[w], (x, y, 1 - c)).wait_send()
            _remote(ins[w].at[theirs], outs[w].at[theirs], sems[0].at[w], sems[1].at[w], (x, y, 1 - c)).wait_recv()

    return _Exchange(fulls, [_sds(f.shape, F32) for f in fulls], [pltpu.SemaphoreType.DMA((nw,))] * 2,
                     start, finish, aliases={w: w for w in range(nw)})


def _ex_allgather(blocks):
    nb = len(blocks)

    def copies(ins, outs, sems):
        send, recv, lsem = sems
        x, y, c, chips = _place()
        me, sibling = (x, y, c), (x, y, 1 - c)

        def rows(b, px, py, pc):
            m_per = blocks[b].shape[0]
            return outs[b].at[pl.ds((4 * px + 2 * py + pc) * m_per, m_per), :]

        def copy(b, k, blk, to, src=None):
            return _remote(rows(b, *blk) if src is None else src, rows(b, *blk), send.at[b, k], recv.at[b, k], to)

        def mine(b):
            return pltpu.make_async_copy(ins[b], rows(b, *me), lsem.at[b])

        def first(b, k):
            return copy(b, k, me, sibling if k == 0 else (*chips[k - 1], c), src=ins[b])

        def passed(b, j):
            return copy(b, 4 + j, (*chips[j], c), sibling)

        def landed(b, j):
            return copy(b, 1 + j, (*chips[j], c), me)

        def handed(b, k):
            return copy(b, 0, sibling, me) if k == 0 else copy(b, 3 + k, (*chips[k - 1], 1 - c), me)

        return mine, first, passed, landed, handed

    def start(ins, outs, sems):
        mine, first, _, _, _ = copies(ins, outs, sems)
        for b in range(nb):
            mine(b).start()
            for k in range(4):
                first(b, k).start()

    def finish(ins, outs, sems):
        mine, first, passed, landed, handed = copies(ins, outs, sems)
        sent = []
        for b in range(nb):
            for j in range(3):
                landed(b, j).wait_recv()
                cp = passed(b, j)
                cp.start()
                sent.append(cp)
        for b in range(nb):
            for k in range(4):
                handed(b, k).wait_recv()
            for k in range(4):
                first(b, k).wait_send()
        for cp in sent:
            cp.wait_send()
        for b in range(nb):
            mine(b).wait()

    return _Exchange(blocks, [_sds((N_DEV * b.shape[0], b.shape[1]), F32) for b in blocks],
                     [pltpu.SemaphoreType.DMA((nb, 7)), pltpu.SemaphoreType.DMA((nb, 7)), pltpu.SemaphoreType.DMA((nb,))],
                     start, finish)


def _inproj(x2, g1, w_in_t, b_in, tabs, seq, exchanges=()):
    T = x2.shape[0]
    tm = min(TM, seq)
    nseq = seq // tm

    def body(x_ref, g_ref, w_ref, b_ref, c_ref, a_ref, bt_ref, h_ref, u_ref, q_ref, k_ref, v_ref, gate_ref):
        x = x_ref[...]
        h = (x * _rms(x) * g_ref[...]).astype(BF16)
        h_ref[...] = h

        def proj(lo, hi):
            return _dot_nt(h, w_ref[lo:hi, :]) + b_ref[:, lo:hi]

        c, a, bt = c_ref[...], a_ref[...], bt_ref[...]
        u_ref[...] = proj(0, C_Q)
        q = proj(C_Q, C_K)
        for p in range(4):
            sl = slice(LANES * p, LANES * (p + 1))
            q_ref[:, sl] = (_rot_fwd(q[:, sl], c, a, bt) * SCALE).astype(BF16)
        kv = proj(C_K, C_G)
        k_ref[...] = _rot_fwd(kv[:, :KV_WIDTH], c, a, bt).astype(BF16)
        v_ref[...] = kv[:, KV_WIDTH:].astype(BF16)
        for j in range(2):
            lo = C_G + D_MODEL * j
            gate_ref[:, D_MODEL * j:D_MODEL * (j + 1)] = jax.nn.sigmoid(proj(lo, lo + D_MODEL)).astype(BF16)

    tab = pl.BlockSpec((tm, LANES), lambda i: (i % nseq, 0))
    return _call(
        body, name="inproj", grid=(T // tm,),
        in_specs=[_rows(tm, D_MODEL), _const((1, D_MODEL)), _const((IN_WIDTH, D_MODEL)), _const((1, IN_WIDTH)),
                  tab, tab, tab],
        out_specs=[_rows(tm, D_MODEL), _rows(tm, POOL_WIDTH), _rows(tm, ATTN_WIDTH), _rows(tm, KV_WIDTH),
                   _rows(tm, KV_WIDTH), _rows(tm, GATE_WIDTH)],
        out_shape=[_sds((T, D_MODEL), BF16), _sds((T, POOL_WIDTH), F32), _sds((T, ATTN_WIDTH), BF16),
                   _sds((T, KV_WIDTH), BF16), _sds((T, KV_WIDTH), BF16), _sds((T, GATE_WIDTH), BF16)],
        args=(x2, g1, w_in_t, b_in, *tabs), sem=("parallel",), exchanges=exchanges)


def _inv_count(pos, w):
    return 1.0 / jnp.minimum(pos + 1, w).astype(F32)


def _pool_fwd(u, w_pool, pool_scale, seq):
    T = u.shape[0]
    tp = min(TP, seq)
    nseq = seq // tp
    per = tp // HALO

    def body(u_ref, prev_ref, w_ref, s_ref, diff_ref, y_ref):
        i = pl.program_id(0)
        first = (i % nseq) == 0
        prev = jnp.where(first, 0.0, prev_ref[...])
        ext = jnp.concatenate([prev, u_ref[...]], axis=0)
        pos = (i % nseq) * tp + lax.broadcasted_iota(jnp.int32, (tp, 1), 0)
        for gi, w in enumerate(POOL_WINDOWS):
            sl = slice(POOL_GC * gi, POOL_GC * (gi + 1))
            xg = ext[:, sl]
            s = xg
            sh = 1
            while sh < w:
                s = s + pltpu.roll(s, sh, 0)
                sh *= 2
            pooled = s[HALO:] * _inv_count(pos, w)
            diff = (pooled - xg[HALO:]).astype(BF16)
            diff_ref[:, sl] = diff
            mixed = _dot(diff, w_ref[gi].astype(BF16))
            y_ref[:, sl] = (mixed * s_ref[:, sl]).astype(BF16)

    return _call(
        body, name="pool_fwd", grid=(T // tp,),
        in_specs=[_rows(tp, POOL_WIDTH),
                  pl.BlockSpec((HALO, POOL_WIDTH), lambda i: (jnp.maximum(i * per - 1, 0), 0)),
                  _const((4, POOL_GC, POOL_GC)), _const((1, POOL_WIDTH))],
        out_specs=[_rows(tp, POOL_WIDTH), _rows(tp, POOL_WIDTH)],
        out_shape=[_sds((T, POOL_WIDTH), BF16), _sds((T, POOL_WIDTH), BF16)],
        args=(u, u, w_pool, pool_scale), sem=("parallel",))


GROUP = 4
GROWS = GROUP * BLOCK


def _attn_masks(n):
    qi = lax.broadcasted_iota(jnp.int32, (GROWS, 2 * BLOCK), 0) % BLOCK
    kj = lax.broadcasted_iota(jnp.int32, (GROWS, 2 * BLOCK), 1)
    rel = qi + BLOCK - kj
    valid = (rel >= 0) & (rel < BLOCK) & (kj >= jnp.where(n > 0, 0, BLOCK))
    lo = lax.broadcasted_iota(jnp.int32, (BLOCK, LANES), 1) < HEAD_DIM
    return valid, lo


def _by_example(bl, *arrays):
    return [a.reshape(bl, a.shape[0] // bl, a.shape[1]) for a in arrays]


def _stack_heads(ref, h, lo):
    keep = lo if h == 0 else jnp.logical_not(lo)
    pieces = []
    for p in (2 * h, 2 * h + 1):
        xp = ref[:, LANES * p:LANES * (p + 1)].astype(F32)
        for e in range(2):
            t = xp if e == h else pltpu.roll(xp, HEAD_DIM, 1)
            pieces.append(jnp.where(keep, t, 0.0).astype(BF16))
    return jnp.concatenate(pieces, axis=0)


def _unstack_heads(stacked, h, lo):
    pairs = []
    for j in range(2):
        parts = []
        for e in range(2):
            t = stacked[BLOCK * (2 * j + e):BLOCK * (2 * j + e + 1)]
            parts.append(t if e == h else pltpu.roll(t, HEAD_DIM, 1))
        pairs.append(jnp.where(lo, parts[0], parts[1]))
    return pairs


def _sink_rows(sink_ref, h):
    head = lax.broadcasted_iota(jnp.int32, (GROWS, 1), 0) // BLOCK
    col = jnp.zeros((GROWS, 1), F32) + sink_ref[GROUP * h]
    for g in range(1, GROUP):
        col = jnp.where(head == g, sink_ref[GROUP * h + g], col)
    return col


def _group_probs(qs, kk, valid, sink):
    s = jnp.where(valid, _dot_nt(qs, kk), NEG_INF)
    m = jnp.maximum(jnp.max(s, axis=1, keepdims=True), sink)
    ex = jnp.exp(s - m)
    es = jnp.exp(sink - m)
    inv = 1.0 / (jnp.sum(ex, axis=1, keepdims=True) + es)
    return ex * inv, es * inv


def _attn_fwd(q, k, v, sinks, seq, exchanges=()):
    T = q.shape[0]
    nb = seq // BLOCK
    bl = T // seq

    def body(sink_ref, q_ref, kp_ref, kc_ref, vp_ref, vc_ref, o_ref):
        valid, lo = _attn_masks(pl.program_id(0))
        for b in range(bl):
            kk = jnp.concatenate([kp_ref[b], kc_ref[b]], axis=0)
            vv = jnp.concatenate([vp_ref[b], vc_ref[b]], axis=0)
            for h in range(2):
                qs = _stack_heads(q_ref.at[b], h, lo)
                pr, _ = _group_probs(qs, kk, valid, _sink_rows(sink_ref, h))
                o = _dot(pr.astype(BF16), vv)
                for j, pair in enumerate(_unstack_heads(o, h, lo)):
                    p = 2 * h + j
                    o_ref[b, :, LANES * p:LANES * (p + 1)] = pair.astype(BF16)

    cur = lambda n: (0, n, 0)
    prv = lambda n: (0, jnp.maximum(n - 1, 0), 0)
    kv = lambda m: pl.BlockSpec((bl, BLOCK, KV_WIDTH), m)
    res = _call(
        body, name="attn_fwd", grid=(nb,),
        in_specs=[pl.BlockSpec(memory_space=pltpu.SMEM), pl.BlockSpec((bl, BLOCK, ATTN_WIDTH), cur),
                  kv(prv), kv(cur), kv(prv), kv(cur)],
        out_specs=[pl.BlockSpec((bl, BLOCK, ATTN_WIDTH), cur)],
        out_shape=[_sds((bl, seq, ATTN_WIDTH), BF16)],
        args=(sinks, *_by_example(bl, q, k, k, v, v)), sem=("parallel",), exchanges=exchanges)
    if exchanges:
        return [res[0][0].reshape(T, ATTN_WIDTH)], res[1]
    return [res[0].reshape(T, ATTN_WIDTH)]


def _branch(y, w_ref):
    return jnp.concatenate([_dot(y, w_ref[j]) for j in range(N_CHIPS)], axis=1)


def _merge_out(y_pool, y_attn, gate, x2, w_bp, w_ba, w_out, g2, g3, exchanges=()):
    T = x2.shape[0]
    tm = min(TM, T)

    def body(yp_ref, ya_ref, gate_ref, x_ref, wbp_ref, wba_ref, wo_ref, g2_ref, g3_ref,
             mg_ref, mix_ref, x1_ref, h2_ref):
        bp, ba = _branch(yp_ref[...], wbp_ref), _branch(ya_ref[...], wba_ref)
        merged = (gate_ref[:, :D_MODEL].astype(F32) * bp + gate_ref[:, D_MODEL:].astype(F32) * ba).astype(BF16)
        mg_ref[...] = merged
        mix = _dot(merged, wo_ref[...])
        mix_ref[...] = mix
        x1 = x_ref[...] + mix * _rms(mix) * g2_ref[...]
        x1_ref[...] = x1
        h2_ref[...] = (x1 * _rms(x1) * g3_ref[...]).astype(BF16)

    return _call(
        body, name="merge_out", grid=(T // tm,),
        in_specs=[_rows(tm, POOL_WIDTH), _rows(tm, ATTN_WIDTH), _rows(tm, GATE_WIDTH), _rows(tm, D_MODEL),
                  _const(w_bp.shape), _const(w_ba.shape), _const((D_MODEL, D_MODEL)),
                  _const((1, D_MODEL)), _const((1, D_MODEL))],
        out_specs=[_rows(tm, D_MODEL)] * 4,
        out_shape=[_sds((T, D_MODEL), BF16), _sds((T, D_MODEL), F32), _sds((T, D_MODEL), F32),
                   _sds((T, D_MODEL), BF16)],
        args=(y_pool, y_attn, gate, x2, w_bp, w_ba, w_out, g2, g3), sem=("parallel",), exchanges=exchanges)


HALF = D_MODEL // 2
_HALVES = _const((N_CHIPS, HALF, D_MODEL))


def _mlp_up(h2, w_up, exchanges=()):
    T = h2.shape[0]
    tm = min(TM, T)

    def body(h_ref, wa_ref, wb_ref, up_ref, a_ref):
        ha, hb = h_ref[:, :HALF], h_ref[:, HALF:]
        for j in range(N_CHIPS):
            sl = slice(D_MODEL * j, D_MODEL * (j + 1))
            up = _dot(ha, wa_ref[j]) + _dot(hb, wb_ref[j])
            up_ref[:, sl] = up.astype(BF16)
            a_ref[:, sl] = jnp.square(jnp.maximum(up, 0.0)).astype(BF16)

    return _call(
        body, name="mlp_up", grid=(T // tm,),
        in_specs=[_rows(tm, D_MODEL), _HALVES, _HALVES],
        out_specs=[_rows(tm, D_FF), _rows(tm, D_FF)],
        out_shape=[_sds((T, D_FF), BF16), _sds((T, D_FF), BF16)],
        args=(h2, *w_up), sem=("parallel",), exchanges=exchanges)


def _mlp_down_loss(a, x1, tgt, w_down, g4):
    T = a.shape[0]
    tm = min(TM, T)

    def body(a_ref, x1_ref, t_ref, wa_ref, wb_ref, g_ref, dff_ref, dy_ref, loss_ref, dg_ref):
        @pl.when(pl.program_id(0) == 0)
        def _():
            loss_ref[...] = jnp.zeros_like(loss_ref)
            dg_ref[...] = jnp.zeros_like(dg_ref)

        ff = None
        for j in range(N_CHIPS):
            lo = D_MODEL * j
            t = _dot(a_ref[:, lo:lo + HALF], wa_ref[j]) + _dot(a_ref[:, lo + HALF:lo + D_MODEL], wb_ref[j])
            ff = t if ff is None else ff + t
        g = g_ref[...]
        err = x1_ref[...] + ff * _rms(ff) * g - t_ref[...]
        loss_ref[...] += jnp.sum(err * err) * (0.5 / D_MODEL)
        dy = err * (1.0 / D_MODEL)
        dy_ref[...] = dy
        dff, dg = _norm_bwd(ff, g, dy)
        dff_ref[...] = dff.astype(BF16)
        dg_ref[...] += dg

    return _call(
        body, name="mlp_down_loss", grid=(T // tm,),
        in_specs=[_rows(tm, D_FF), _rows(tm, D_MODEL), _rows(tm, D_MODEL), _HALVES, _HALVES, _const((1, D_MODEL))],
        out_specs=[_rows(tm, D_MODEL), _rows(tm, D_MODEL), _const((8, LANES)), _const((1, D_MODEL))],
        out_shape=[_sds((T, D_MODEL), BF16), _sds((T, D_MODEL), F32), _sds((8, LANES), F32),
                   _sds((1, D_MODEL), F32)],
        args=(a, x1, tgt, *w_down, g4), sem=("arbitrary",))


TM_MLP = 256


def _mlp_core(h2, x1, mix, tgt, w_up, w_down, g4, g3, g2):
    T = h2.shape[0]
    tm = min(TM_MLP, T)

    def body(h_ref, x1_ref, mix_ref, t_ref, g_ref, g3_ref, g2_ref, ua_hbm, ub_hbm, da_hbm, db_hbm,
             act_ref, dff_ref, dup_ref, dx1_ref, dmix_ref, loss_ref, dg_ref, dg3_ref, dg2_ref,
             ua, ub, da, db, relu_scr, sems):
        def weight_copy(i):
            src, dst = ((ua_hbm, ua), (ub_hbm, ub), (da_hbm, da), (db_hbm, db))[i]
            return pltpu.make_async_copy(src, dst, sems.at[i])

        @pl.when(pl.program_id(0) == 0)
        def _():
            for i in range(4):
                weight_copy(i).start()
            loss_ref[...] = jnp.zeros_like(loss_ref)
            for ref in (dg_ref, dg3_ref, dg2_ref):
                ref[...] = jnp.zeros_like(ref)
            weight_copy(0).wait()
            weight_copy(1).wait()

        ha, hb = h_ref[:, :HALF], h_ref[:, HALF:]
        ff = None
        for j in range(N_CHIPS):
            lo = D_MODEL * j
            relu = jnp.maximum(_dot(ha, ua[j]) + _dot(hb, ub[j]), 0.0)
            if j == 0:
                @pl.when(pl.program_id(0) == 0)
                def _():
                    weight_copy(2).wait()
                    weight_copy(3).wait()
            relu_scr[:, lo:lo + D_MODEL] = relu
            act = jnp.square(relu).astype(BF16)
            act_ref[:, lo:lo + D_MODEL] = act
            t = _dot(act[:, :HALF], da[j]) + _dot(act[:, HALF:], db[j])
            ff = t if ff is None else ff + t
        g = g_ref[...]
        x1 = x1_ref[...]
        err = x1 + ff * _rms(ff) * g - t_ref[...]
        loss_ref[...] += jnp.sum(err * err) * (0.5 / D_MODEL)
        dy = err * (1.0 / D_MODEL)
        dff, dg = _norm_bwd(ff, g, dy)
        dg_ref[...] += dg
        dff = dff.astype(BF16)
        dff_ref[...] = dff
        for j in range(N_CHIPS):
            for part, w in enumerate((da, db)):
                lo = D_MODEL * j + HALF * part
                dact = _dot_nt(dff, w[j])
                dup_ref[:, lo:lo + HALF] = (dact * (2.0 * relu_scr[:, lo:lo + HALF])).astype(BF16)

        def back(w):
            acc = _dot_nt(dup_ref[:, :D_MODEL], w[0])
            for j in range(1, N_CHIPS):
                acc = acc + _dot_nt(dup_ref[:, D_MODEL * j:D_MODEL * (j + 1)], w[j])
            return acc

        dh2 = jnp.concatenate([back(ua), back(ub)], axis=1)
        dx, dg3 = _norm_bwd(x1, g3_ref[...], dh2)
        dx1 = dy + dx
        dx1_ref[...] = dx1
        dg3_ref[...] += dg3
        dmix, dg2 = _norm_bwd(mix_ref[...], g2_ref[...], dx1)
        dmix_ref[...] = dmix.astype(BF16)
        dg2_ref[...] += dg2

    halves = pltpu.VMEM((N_CHIPS, HALF, D_MODEL), BF16)
    gain = _const((1, D_MODEL))
    return pl.pallas_call(
        body, name="mlp_core", grid=(T // tm,),
        in_specs=[_rows(tm, D_MODEL)] * 4 + [gain] * 3 + [ANY] * 4,
        out_specs=[_rows(tm, D_FF), _rows(tm, D_MODEL), _rows(tm, D_FF), _rows(tm, D_MODEL), _rows(tm, D_MODEL),
                   _const((8, LANES)), gain, gain, gain],
        out_shape=[_sds((T, D_FF), BF16), _sds((T, D_MODEL), BF16), _sds((T, D_FF), BF16), _sds((T, D_MODEL), F32),
                   _sds((T, D_MODEL), BF16), _sds((8, LANES), F32)] + [_sds((1, D_MODEL), F32)] * 3,
        scratch_shapes=[halves] * 4 + [pltpu.VMEM((tm, D_FF), F32), pltpu.SemaphoreType.DMA((4,))],
        compiler_params=_cp("arbitrary"),
    )(h2, x1, mix, tgt, g4, g3, g2, *w_up, *w_down)


def _mlp_down_bwd(dff, up, w_down):
    T = dff.shape[0]
    tm = min(TM, T)

    def body(d_ref, up_ref, wa_ref, wb_ref, dup_ref):
        d = d_ref[...]
        for j in range(N_CHIPS):
            for part, w_ref in enumerate((wa_ref, wb_ref)):
                lo = D_MODEL * j + HALF * part
                da = _dot_nt(d, w_ref[j])
                relu = jnp.maximum(up_ref[:, lo:lo + HALF].astype(F32), 0.0)
                dup_ref[:, lo:lo + HALF] = (da * (2.0 * relu)).astype(BF16)

    return _call(
        body, name="mlp_down_bwd", grid=(T // tm,),
        in_specs=[_rows(tm, D_MODEL), _rows(tm, D_FF), _HALVES, _HALVES],
        out_specs=[_rows(tm, D_FF)],
        out_shape=[_sds((T, D_FF), BF16)],
        args=(dff, up, *w_down), sem=("parallel",))[0]


def _mlp_up_bwd(dup, dy, x1, mix, w_up, g3, g2, exchanges=()):
    T = dup.shape[0]
    tm = min(TM, T)

    def body(dup_ref, dy_ref, x1_ref, mix_ref, wa_ref, wb_ref, g3_ref, g2_ref, dx1_ref, dmix_ref, dg3_ref, dg2_ref):
        @pl.when(pl.program_id(0) == 0)
        def _():
            dg3_ref[...] = jnp.zeros_like(dg3_ref)
            dg2_ref[...] = jnp.zeros_like(dg2_ref)

        def back(w_ref):
            acc = _dot_nt(dup_ref[:, :D_MODEL], w_ref[0])
            for j in range(1, N_CHIPS):
                acc = acc + _dot_nt(dup_ref[:, D_MODEL * j:D_MODEL * (j + 1)], w_ref[j])
            return acc

        dh2 = jnp.concatenate([back(wa_ref), back(wb_ref)], axis=1)
        dx, dg3 = _norm_bwd(x1_ref[...], g3_ref[...], dh2)
        dx1 = dy_ref[...] + dx
        dx1_ref[...] = dx1
        dg3_ref[...] += dg3
        dmix, dg2 = _norm_bwd(mix_ref[...], g2_ref[...], dx1)
        dmix_ref[...] = dmix.astype(BF16)
        dg2_ref[...] += dg2

    return _call(
        body, name="mlp_up_bwd", grid=(T // tm,),
        in_specs=[_rows(tm, D_FF), _rows(tm, D_MODEL), _rows(tm, D_MODEL), _rows(tm, D_MODEL),
                  _HALVES, _HALVES, _const((1, D_MODEL)), _const((1, D_MODEL))],
        out_specs=[_rows(tm, D_MODEL), _rows(tm, D_MODEL), _const((1, D_MODEL)), _const((1, D_MODEL))],
        out_shape=[_sds((T, D_MODEL), F32), _sds((T, D_MODEL), BF16), _sds((1, D_MODEL), F32),
                   _sds((1, D_MODEL), F32)],
        args=(dup, dy, x1, mix, *w_up, g3, g2), sem=("arbitrary",), exchanges=exchanges)


def _dw(tag, a, g, ta, tn, shard_cols=False, exchanges=()):
    T, ka = a.shape
    n = g.shape[1]
    tk = min(2 * TM, T)
    nk = T // tk

    def body(a_ref, g_ref, o_ref):
        @pl.when(pl.program_id(2) == 0)
        def _():
            o_ref[...] = jnp.zeros_like(o_ref)

        o_ref[...] += _dot_tn(a_ref[...], g_ref[...])

    if shard_cols:
        per = (n // N_CHIPS) // tn
        out_spec = pl.BlockSpec((None, ta, tn), lambda i, j, k: (j // per, i, j % per))
        out_shape = _sds((N_CHIPS, ka, n // N_CHIPS), F32)
    else:
        out_spec = pl.BlockSpec((ta, tn), lambda i, j, k: (i, j))
        out_shape = _sds((ka, n), F32)
    return _call(
        body, name="dw_" + tag, grid=(ka // ta, n // tn, nk),
        in_specs=[pl.BlockSpec((tk, ta), lambda i, j, k: (k, i)), pl.BlockSpec((tk, tn), lambda i, j, k: (k, j))],
        out_specs=[out_spec], out_shape=[out_shape],
        args=(a, g), sem=("parallel", "parallel", "arbitrary"), exchanges=exchanges)


def _dw_slabs(tag, a, g):
    T, ka = a.shape
    n = g.shape[1]
    c = n // N_CHIPS
    tk = min(TM, T)

    def body(a_ref, g_ref, o_ref):
        @pl.when(pl.program_id(0) == 0)
        def _():
            o_ref[...] = jnp.zeros_like(o_ref)

        res = _dot_tn(a_ref[...], g_ref[...])
        for j in range(N_CHIPS):
            o_ref[j] += res[:, c * j:c * (j + 1)]

    return _call(
        body, name="dw_" + tag, grid=(T // tk,),
        in_specs=[_rows(tk, ka), _rows(tk, n)],
        out_specs=[_const((N_CHIPS, ka, c))], out_shape=[_sds((N_CHIPS, ka, c), F32)],
        args=(a, g), sem=("arbitrary",))[0]


def _merge_bwd(dmix, gate, y_pool, y_attn, w_out, w_bp, w_ba, exchanges=()):
    T = dmix.shape[0]
    tm = min(TM, T)

    def body(dmix_ref, gate_ref, yp_ref, ya_ref, wo_ref, wbp_ref, wba_ref,
             dbp_ref, dba_ref, dgate_ref, dyp_ref, dya_ref):
        dm = _dot_nt(dmix_ref[...], wo_ref[...])
        for j, (y_ref, db_ref, w_ref, dy_ref) in enumerate(
                ((yp_ref, dbp_ref, wbp_ref, dyp_ref), (ya_ref, dba_ref, wba_ref, dya_ref))):
            sl = slice(D_MODEL * j, D_MODEL * (j + 1))
            gt = gate_ref[:, sl].astype(F32)
            db = (dm * gt).astype(BF16)
            db_ref[...] = db
            dgate_ref[:, sl] = (dm * _branch(y_ref[...], w_ref) * gt * (1.0 - gt)).astype(BF16)
            cw = D_MODEL // N_CHIPS
            dy = _dot_nt(db[:, :cw], w_ref[0])
            for c in range(1, N_CHIPS):
                dy = dy + _dot_nt(db[:, cw * c:cw * (c + 1)], w_ref[c])
            dy_ref[...] = dy.astype(dy_ref.dtype)

    return _call(
        body, name="merge_bwd", grid=(T // tm,),
        in_specs=[_rows(tm, D_MODEL), _rows(tm, GATE_WIDTH), _rows(tm, POOL_WIDTH), _rows(tm, ATTN_WIDTH),
                  _const((D_MODEL, D_MODEL)), _const(w_bp.shape), _const(w_ba.shape)],
        out_specs=[_rows(tm, D_MODEL), _rows(tm, D_MODEL), _rows(tm, GATE_WIDTH), _rows(tm, POOL_WIDTH),
                   _rows(tm, ATTN_WIDTH)],
        out_shape=[_sds((T, D_MODEL), BF16), _sds((T, D_MODEL), BF16), _sds((T, GATE_WIDTH), BF16),
                   _sds((T, POOL_WIDTH), F32), _sds((T, ATTN_WIDTH), BF16)],
        args=(dmix, gate, y_pool, y_attn, w_out, w_bp, w_ba), sem=("parallel",), exchanges=exchanges)


def _attn_bwd(q, k, v, do, sinks, tabs, seq, exchanges=()):
    T = q.shape[0]
    nb = seq // BLOCK
    bl = T // seq
    steps = nb + 1

    def body(sink_ref, q_ref, do_ref, kp_ref, kc_ref, vp_ref, vc_ref, c_ref, a_ref, bt_ref, cp_ref, ap_ref, btp_ref,
             dq_ref, dk_ref, dv_ref, dsink_ref, ck_ref, cv_ref):
        n = pl.program_id(0)

        @pl.when(n == 0)
        def _():
            dsink_ref[...] = jnp.zeros_like(dsink_ref)
            ck_ref[...] = jnp.zeros_like(ck_ref)
            cv_ref[...] = jnp.zeros_like(cv_ref)

        @pl.when(n < nb)
        def _():
            valid, lo = _attn_masks(n)
            for b in range(bl):
                kk = jnp.concatenate([kp_ref[b], kc_ref[b]], axis=0)
                vv = jnp.concatenate([vp_ref[b], vc_ref[b]], axis=0)
                dk_acc = jnp.zeros((2 * BLOCK, KV_WIDTH), F32)
                dv_acc = jnp.zeros((2 * BLOCK, KV_WIDTH), F32)
                for h in range(2):
                    qs = _stack_heads(q_ref.at[b], h, lo)
                    dos = _stack_heads(do_ref.at[b], h, lo)
                    pr, ps = _group_probs(qs, kk, valid, _sink_rows(sink_ref, h))
                    dp = _dot_nt(dos, vv)
                    delta = jnp.sum(pr * dp, axis=1, keepdims=True)
                    ds = (pr * (dp - delta)).astype(BF16)
                    dsk = ps * delta
                    for g in range(GROUP):
                        idx = GROUP * h + g
                        dsink_ref[idx:idx + 1, :] += (jnp.zeros((1, LANES), F32)
                                                      - jnp.sum(dsk[BLOCK * g:BLOCK * (g + 1)]))
                    dk_acc = dk_acc + _dot_tn(ds, qs)
                    dv_acc = dv_acc + _dot_tn(pr.astype(BF16), dos)
                    for j, pair in enumerate(_unstack_heads(_dot(ds, kk) * SCALE, h, lo)):
                        sl = slice(LANES * (2 * h + j), LANES * (2 * h + j + 1))
                        dq_ref[b, :, sl] = _rot_bwd(pair, c_ref[...], a_ref[...], bt_ref[...]).astype(BF16)
                fin_k = ck_ref[b] + dk_acc[:BLOCK]
                dk_ref[b] = _rot_bwd(fin_k, cp_ref[...], ap_ref[...], btp_ref[...]).astype(BF16)
                dv_ref[b] = (cv_ref[b] + dv_acc[:BLOCK]).astype(BF16)
                ck_ref[b] = dk_acc[BLOCK:]
                cv_ref[b] = dv_acc[BLOCK:]

        @pl.when(n == nb)
        def _():
            for b in range(bl):
                dk_ref[b] = _rot_bwd(ck_ref[b], cp_ref[...], ap_ref[...], btp_ref[...]).astype(BF16)
                dv_ref[b] = cv_ref[b].astype(BF16)

    cur = lambda n: (0, jnp.minimum(n, nb - 1), 0)
    prv = lambda n: (0, jnp.clip(n - 1, 0, nb - 1), 0)
    tcur = lambda n: (jnp.minimum(n, nb - 1), 0)
    tprv = lambda n: (jnp.clip(n - 1, 0, nb - 1), 0)
    wide = lambda m: pl.BlockSpec((bl, BLOCK, ATTN_WIDTH), m)
    kv = lambda m: pl.BlockSpec((bl, BLOCK, KV_WIDTH), m)
    tab = lambda m: pl.BlockSpec((BLOCK, LANES), m)
    res = _call(
        body, name="attn_bwd", grid=(steps,),
        in_specs=[pl.BlockSpec(memory_space=pltpu.SMEM), wide(cur), wide(cur), kv(prv), kv(cur), kv(prv), kv(cur),
                  tab(tcur), tab(tcur), tab(tcur), tab(tprv), tab(tprv), tab(tprv)],
        out_specs=[wide(cur), kv(prv), kv(prv), _const((8, LANES))],
        out_shape=[_sds((bl, seq, ATTN_WIDTH), BF16), _sds((bl, seq, KV_WIDTH), BF16),
                   _sds((bl, seq, KV_WIDTH), BF16), _sds((8, LANES), F32)],
        scratch=[pltpu.VMEM((bl, BLOCK, KV_WIDTH), F32), pltpu.VMEM((bl, BLOCK, KV_WIDTH), F32)],
        args=(sinks, *_by_example(bl, q, do, k, k, v, v), *tabs, *tabs), sem=("arbitrary",), exchanges=exchanges)
    outs, rest = (res if exchanges else (res, None))
    outs = [outs[0].reshape(T, ATTN_WIDTH), outs[1].reshape(T, KV_WIDTH), outs[2].reshape(T, KV_WIDTH), outs[3]]
    return (outs, rest) if exchanges else outs


def _pool_bwd(dyp, diff, w_pool, pool_scale, seq, exchanges=()):
    T = dyp.shape[0]
    tp = min(TP, seq)
    nseq = seq // tp
    per = tp // HALO
    last_halo = T // HALO - 1

    def body(dy_ref, nxt_ref, diff_ref, w_ref, s_ref, du_ref, dw_ref, ds_ref):
        i = pl.program_id(0)

        @pl.when(i == 0)
        def _():
            dw_ref[...] = jnp.zeros_like(dw_ref)
            ds_ref[...] = jnp.zeros_like(ds_ref)

        last = (i % nseq) == nseq - 1
        nxt = jnp.where(last, 0.0, nxt_ref[...])
        ext = jnp.concatenate([dy_ref[...], nxt], axis=0) * s_ref[...]
        pos = (i % nseq) * tp + lax.broadcasted_iota(jnp.int32, (tp + HALO, 1), 0)
        for gi, w in enumerate(POOL_WINDOWS):
            sl = slice(POOL_GC * gi, POOL_GC * (gi + 1))
            wg = w_ref[gi].astype(BF16)
            dmx = ext[:, sl].astype(BF16)
            ddiff = _dot_nt(dmx, wg)
            s = ddiff * _inv_count(pos, w)
            sh = 1
            while sh < w:
                s = s + pltpu.roll(s, tp + HALO - sh, 0)
                sh *= 2
            du_ref[:, sl] = (s[:tp] - ddiff[:tp]).astype(BF16)
            dg = diff_ref[:, sl]
            dw_ref[gi] += _dot_tn(dg, dmx[:tp])
            ds_ref[:, sl] += jnp.sum(dy_ref[:, sl] * _dot(dg, wg), axis=0, keepdims=True)

    return _call(
        body, name="pool_bwd", grid=(T // tp,),
        in_specs=[_rows(tp, POOL_WIDTH),
                  pl.BlockSpec((HALO, POOL_WIDTH), lambda i: (jnp.minimum((i + 1) * per, last_halo), 0)),
                  _rows(tp, POOL_WIDTH), _const((4, POOL_GC, POOL_GC)), _const((1, POOL_WIDTH))],
        out_specs=[_rows(tp, POOL_WIDTH), _const((4, POOL_GC, POOL_GC)), _const((1, POOL_WIDTH))],
        out_shape=[_sds((T, POOL_WIDTH), BF16), _sds((4, POOL_GC, POOL_GC), F32), _sds((1, POOL_WIDTH), F32)],
        args=(dyp, dyp, diff, w_pool, pool_scale), sem=("arbitrary",), exchanges=exchanges)


_PARTS = ((0, C_Q), (C_Q, C_K), (C_K, C_V), (C_V, C_G), (C_G, IN_WIDTH))


def _inproj_bwd(parts, x2, dx1, w_in_t, g1, exchanges=()):
    T = x2.shape[0]
    tm = min(TM, T)

    def body(du_ref, dq_ref, dk_ref, dv_ref, dgt_ref, x_ref, dx1_ref, w_ref, g_ref, gx_ref, dg_ref):
        @pl.when(pl.program_id(0) == 0)
        def _():
            dg_ref[...] = jnp.zeros_like(dg_ref)

        dh = jnp.zeros((tm, D_MODEL), F32)
        for (lo, hi), p_ref in zip(_PARTS, (du_ref, dq_ref, dk_ref, dv_ref, dgt_ref)):
            dh = dh + _dot(p_ref[...], w_ref[lo:hi, :])
        dx, dg = _norm_bwd(x_ref[...], g_ref[...], dh)
        gx_ref[...] = dx1_ref[...] + dx
        dg_ref[...] += dg

    return _call(
        body, name="inproj_bwd", grid=(T // tm,),
        in_specs=[_rows(tm, hi - lo) for lo, hi in _PARTS]
        + [_rows(tm, D_MODEL), _rows(tm, D_MODEL), _const((IN_WIDTH, D_MODEL)), _const((1, D_MODEL))],
        out_specs=[_rows(tm, D_MODEL), _const((1, D_MODEL))],
        out_shape=[_sds((T, D_MODEL), F32), _sds((1, D_MODEL), F32)],
        args=(*parts, x2, dx1, w_in_t, g1), sem=("arbitrary",), exchanges=exchanges)


def _dw_in(h, parts, exchanges=()):
    T = h.shape[0]
    tk = min(TM, T)

    def body(h_ref, du_ref, dq_ref, dk_ref, dv_ref, dgt_ref, o_ref, db_ref):
        @pl.when(pl.program_id(0) == 0)
        def _():
            o_ref[...] = jnp.zeros_like(o_ref)
            db_ref[...] = jnp.zeros_like(db_ref)

        hh = h_ref[...]
        for (lo, hi), p_ref in zip(_PARTS, (du_ref, dq_ref, dk_ref, dv_ref, dgt_ref)):
            part = p_ref[...]
            o_ref[lo:hi, :] += _dot_tn(part, hh)
            db_ref[:, lo:hi] += jnp.sum(part.astype(F32), axis=0, keepdims=True)

    return _call(
        body, name="dw_in", grid=(T // tk,),
        in_specs=[_rows(tk, D_MODEL)] + [_rows(tk, hi - lo) for lo, hi in _PARTS],
        out_specs=[_const((IN_WIDTH, D_MODEL)), _const((1, IN_WIDTH))],
        out_shape=[_sds((IN_WIDTH, D_MODEL), F32), _sds((1, IN_WIDTH), F32)],
        args=(h, *parts), sem=("arbitrary",), exchanges=exchanges)


def _row_tile(rows, cap=256, mult=16):
    best = None
    for t in range(mult, min(rows, cap) + 1, mult):
        if rows % t == 0:
            best = t
    if best is None:
        raise ValueError("no row tile for %d rows" % rows)
    return best


def _pair_sum(ids, full, got):
    _, r, c = full.shape
    hr = r // 2
    tr = _row_tile(hr)
    nblk = hr // tr

    def body(ids_ref, a_ref, b_ref, own_ref, sb_ref):
        s = a_ref[...] + b_ref[...]
        sb_ref[...] = s.astype(BF16)

        @pl.when(pl.program_id(1) == ids_ref[0])
        def _():
            own_ref[...] = s

    slab = pl.BlockSpec((None, tr, c), lambda i, j, ids_ref: (j, i, 0))
    return pl.pallas_call(
        body, name="pair_sum_%dx%d" % (r, c),
        grid_spec=pltpu.PrefetchScalarGridSpec(
            num_scalar_prefetch=1, grid=(nblk, N_CHIPS),
            in_specs=[pl.BlockSpec((None, tr, c), lambda i, j, ids_ref: (j, ids_ref[1] * nblk + i, 0)), slab],
            out_specs=[pl.BlockSpec((tr, c), lambda i, j, ids_ref: (i, 0)), slab]),
        out_shape=[_sds((hr, c), F32), _sds((N_CHIPS, hr, c), BF16)],
        compiler_params=_cp("parallel", "arbitrary"),
    )(ids, full, got)


def _chip_sum(ids, own, got):
    hr, c = own.shape
    tr = _row_tile(hr)
    nblk = hr // tr

    def body(ids_ref, a_ref, b_ref, o_ref):
        o_ref[...] = ((a_ref[...] + b_ref[0].astype(F32)) + b_ref[1].astype(F32)) + b_ref[2].astype(F32)

    return pl.pallas_call(
        body, name="chip_sum_%dx%d" % (hr, c),
        grid_spec=pltpu.PrefetchScalarGridSpec(
            num_scalar_prefetch=1, grid=(nblk,),
            in_specs=[pl.BlockSpec((tr, c), lambda i, ids_ref: (i, 0)),
                      pl.BlockSpec((3, tr, c), lambda i, ids_ref: (0, i, 0))],
            out_specs=pl.BlockSpec((tr, c), lambda i, ids_ref: (ids_ref[1] * nblk + i, 0))),
        out_shape=_sds((2 * hr, c), F32),
        compiler_params=_cp("parallel"),
    )(ids, own, got)


def _adamw_math(w, g, m, v):
    nm = ADAM_B1 * m + (1.0 - ADAM_B1) * g
    nv = ADAM_B2 * v + (1.0 - ADAM_B2) * jnp.square(g)
    m_hat = nm / (1.0 - ADAM_B1 ** ADAM_STEP)
    v_hat = nv / (1.0 - ADAM_B2 ** ADAM_STEP)
    return -ADAM_LR * (m_hat / (jnp.sqrt(v_hat) + ADAM_EPS) + ADAM_WD * w), nm, nv


def _adamw(w, g, m, v):
    r, c = w.shape
    tr = _row_tile(r, cap=512, mult=8)

    def body(w_ref, g_ref, m_ref, v_ref, d_ref, nm_ref, nv_ref):
        d_ref[...], nm_ref[...], nv_ref[...] = _adamw_math(w_ref[...], g_ref[...], m_ref[...], v_ref[...])

    spec = _rows(tr, c)
    return pl.pallas_call(
        body, name="adamw_%dx%d" % (r, c), grid=(r // tr,),
        in_specs=[spec] * 4, out_specs=[spec] * 3, out_shape=[_sds((r, c), F32)] * 3,
        compiler_params=_cp("parallel"),
    )(w, g, m, v)


_SMALL_NAMES = ("w_pool", "b_in", "g_mix_pre", "g_mix_post", "g_mlp_pre", "g_mlp_post", "pool_scale", "attn_sinks")
B_ROWS = -(-IN_WIDTH // D_MODEL)


def _row_block(rows):
    rows = [jnp.pad(r.astype(F32), ((0, 0), (0, D_MODEL - r.shape[1]))) for r in rows]
    return jnp.pad(jnp.concatenate(rows, axis=0), ((0, 8 - len(rows)), (0, 0)))


def _early_block(dg2, dg3, dg4, dps, dsink, loss):
    tail = jnp.concatenate([jnp.pad(dsink.reshape(1, -1), ((0, 0), (0, LANES - dsink.size))),
                            jnp.pad(loss.reshape(1, 1), ((0, 0), (0, LANES - 1)))], axis=1)
    return _row_block([dg2, dg3, dg4, dps, tail])


def _late_block(db_in, dg1):
    b = jnp.pad(db_in, ((0, 0), (0, B_ROWS * D_MODEL - IN_WIDTH))).reshape(B_ROWS, D_MODEL)
    return _row_block([b[r:r + 1] for r in range(B_ROWS)] + [dg1])


def _small_update(gearly, gmat, glate, w, m, v):
    names = _SMALL_NAMES
    n = len(names)

    def total(ref, rows):
        acc = ref[0:rows, :]
        for d in range(1, N_DEV):
            acc = acc + ref[d * rows:(d + 1) * rows, :]
        return acc

    def body(*refs):
        early_ref, gmat_ref, late_ref = refs[:3]
        w_refs, m_refs, v_refs = refs[3:3 + n], refs[3 + n:3 + 2 * n], refs[3 + 2 * n:3 + 3 * n]
        outs = refs[3 + 3 * n:]
        loss_ref, g_refs, d_refs = outs[0], outs[1:1 + n], outs[1 + n:1 + 2 * n]
        nm_refs, nv_refs = outs[1 + 2 * n:1 + 3 * n], outs[1 + 3 * n:1 + 4 * n]
        early, late = total(early_ref, 8), total(late_ref, 8)
        loss_ref[...] = jnp.sum(early[4:5, LANES:2 * LANES], axis=1, keepdims=True)
        bias = jnp.concatenate([late[r:r + 1, :] for r in range(B_ROWS - 1)]
                               + [late[B_ROWS - 1:B_ROWS, :IN_WIDTH - (B_ROWS - 1) * D_MODEL]], axis=1)
        grad = dict(b_in=bias, g_mix_pre=late[B_ROWS:B_ROWS + 1, :], g_mix_post=early[0:1, :],
                    g_mlp_pre=early[1:2, :], g_mlp_post=early[2:3, :], pool_scale=early[3:4, :POOL_WIDTH],
                    attn_sinks=early[4:5, :N_Q_HEADS])
        for i, name in enumerate(names):
            g = total(gmat_ref, 4 * POOL_GC) if name == "w_pool" else grad[name]
            g_refs[i][...] = g
            d_refs[i][...], nm_refs[i][...], nv_refs[i][...] = _adamw_math(
                w_refs[i][...], g, m_refs[i][...], v_refs[i][...])

    shapes = [_sds(w[k].shape, F32) for k in names]
    res = pl.pallas_call(
        body, name="small_update", out_shape=[_sds((1, 1), F32)] + shapes * 4,
        compiler_params=pltpu.CompilerParams(vmem_limit_bytes=VMEM_MB * 1024 * 1024),
    )(gearly, gmat, glate, *[w[k] for k in names], *[m[k] for k in names], *[v[k] for k in names])
    loss = res[0]
    per = {k: tuple(res[1 + j * n + i] for j in range(4)) for i, k in enumerate(names)}
    return loss, per


_BIG = ("w_in", "w_branch_pool", "w_branch_attn", "w_out", "w_up", "w_down")
_ORDER = ("g_mix_pre", "w_in", "b_in", "w_pool", "pool_scale", "attn_sinks", "w_branch_pool", "w_branch_attn",
          "w_out", "g_mix_post", "g_mlp_pre", "w_up", "w_down", "g_mlp_post")


def _stack_rows(slab):
    return slab.reshape(-1, slab.shape[2])


def _step(x2, tgt, seq, shards, small, ids):
    tabs = _rope_tables(seq)
    g1, g2, g3, g4 = (small[n] for n in ("g_mix_pre", "g_mix_post", "g_mlp_pre", "g_mlp_post"))
    sinks = small["attn_sinks"].reshape(N_Q_HEADS)
    w_pool = small["w_pool"].reshape(4, POOL_GC, POOL_GC)
    pool_scale = small["pool_scale"]

    def whole(shard, slabs):
        return lax.dynamic_update_slice(slabs, shard[None], (ids[0], 0, 0))

    up_a, up_b = shards["w_up"][:HALF], shards["w_up"][HALF:]
    down_a, down_b = shards["w_down"][:HALF], shards["w_down"][HALF:]
    w_in = _stack_rows(whole(shards["w_in"], _alone("gather_in", _ex_gather([shards["w_in"]]))[0][0]))
    mix_shards = [shards[n] for n in ("w_branch_pool", "w_branch_attn", "w_out")]
    (h, u, q, k, v, gate), [(*mix_slabs, got_c)] = _inproj(
        x2, g1, w_in, small["b_in"], tabs, seq, exchanges=[_ex_gather(mix_shards + [down_a])])
    w_bp, w_ba, out_slab = (whole(s, g) for s, g in zip(mix_shards, mix_slabs))
    w_out = _stack_rows(out_slab)
    diff, y_pool = _pool_fwd(u, w_pool, pool_scale, seq)
    (y_attn,), [[got_a]] = _attn_fwd(q, k, v, sinks, seq, exchanges=[_ex_gather([up_a])])
    (merged, mix, x1, h2), [[got_b, got_d]] = _merge_out(
        y_pool, y_attn, gate, x2, w_bp, w_ba, w_out, g2, g3, exchanges=[_ex_gather([up_b, down_b])])
    w_up = (whole(up_a, got_a), whole(up_b, got_b))
    w_down = (whole(down_a, got_c), whole(down_b, got_d))
    act, dff, dup, dx1, dmix, loss_acc, dg4, dg3, dg2 = _mlp_core(h2, x1, mix, tgt, w_up, w_down, g4, g3, g2)

    dw_down = _dw("down", act, dff, 1024, 1024)[0].reshape(N_CHIPS, D_FF // N_CHIPS, D_MODEL)
    (dbp, dba, dgate, dyp, dya), [[got]] = _merge_bwd(
        dmix, gate, y_pool, y_attn, w_out, w_bp, w_ba, exchanges=[_ex_pair([dw_down])])
    ps_down = _pair_sum(ids, dw_down, got)
    (dw_up,), [[got]] = _dw("up", h2, dup, 1024, 1024, shard_cols=True, exchanges=[_ex_chip([ps_down[1]])])
    half_down = _chip_sum(ids, ps_down[0], got)
    (dw_out,), [[got]] = _dw("out", merged, dmix, 1024, 1024, exchanges=[_ex_pair([dw_up])])
    ps_up = _pair_sum(ids, dw_up, got)
    dw_mix = [dw_out.reshape(N_CHIPS, D_MODEL // N_CHIPS, D_MODEL),
              _dw_slabs("branch_pool", y_pool, dbp), _dw_slabs("branch_attn", y_attn, dba)]
    (dq, dk, dv, dsink), [[got], gots, [g_down]] = _attn_bwd(
        q, k, v, dya, sinks, tabs, seq, exchanges=[_ex_chip([ps_up[1]]), _ex_pair(dw_mix), _ex_swap([half_down])])
    half_up = _chip_sum(ids, ps_up[0], got)
    ps_mix = [_pair_sum(ids, d, g) for d, g in zip(dw_mix, gots)]
    (du, dw_pool, dps), [[g_up]] = _pool_bwd(dyp, diff, w_pool, pool_scale, seq, exchanges=[_ex_swap([half_up])])
    parts = (du, dq, dk, dv, dgate)
    early = _early_block(dg2, dg3, dg4, dps, dsink[:, 0], loss_acc[0, 0])
    mat = dw_pool.reshape(4 * POOL_GC, POOL_GC)
    (dw_in_t, db_in), [gots, [gearly, gmat]] = _dw_in(
        h, parts, exchanges=[_ex_chip([p[1] for p in ps_mix]), _ex_allgather([early, mat])])
    half_mix = [_chip_sum(ids, p[0], g) for p, g in zip(ps_mix, gots)]
    dw_in = dw_in_t.reshape(N_CHIPS, IN_WIDTH // N_CHIPS, D_MODEL)
    g_mix, [got] = _alone("swap_mix_pair_in", _ex_swap(half_mix), _ex_pair([dw_in]))
    ps_in = _pair_sum(ids, dw_in, got)
    (gx, dg1), [[got]] = _inproj_bwd(parts, x2, dx1, w_in, g1, exchanges=[_ex_chip([ps_in[1]])])
    [g_in], [glate] = _alone("swap_in_allgather", _ex_swap([_chip_sum(ids, ps_in[0], got)]),
                             _ex_allgather([_late_block(db_in, dg1)]))

    grads = dict(w_in=g_in, w_branch_pool=g_mix[1], w_branch_attn=g_mix[2], w_out=g_mix[0], w_up=g_up, w_down=g_down)
    return (gearly, gmat, glate), gx, grads


def kernel(x, g_mix_pre, w_in, b_in, w_pool, pool_scale, attn_sinks, w_branch_pool, w_branch_attn, w_out, g_mix_post, g_mlp_pre, w_up, w_down, g_mlp_post, loss_target, m_g_mix_pre, m_w_in, m_b_in, m_w_pool, m_pool_scale, m_attn_sinks, m_w_branch_pool, m_w_branch_attn, m_w_out, m_g_mix_post, m_g_mlp_pre, m_w_up, m_w_down, m_g_mlp_post, v_g_mix_pre, v_w_in, v_b_in, v_w_pool, v_pool_scale, v_attn_sinks, v_w_branch_pool, v_w_branch_attn, v_w_out, v_g_mix_post, v_g_mlp_pre, v_w_up, v_w_down, v_g_mlp_post):
    weights = dict(g_mix_pre=g_mix_pre, w_in=w_in, b_in=b_in, w_pool=w_pool, pool_scale=pool_scale,
                   attn_sinks=attn_sinks, w_branch_pool=w_branch_pool, w_branch_attn=w_branch_attn, w_out=w_out,
                   g_mix_post=g_mix_post, g_mlp_pre=g_mlp_pre, w_up=w_up, w_down=w_down, g_mlp_post=g_mlp_post)
    mom1 = dict(g_mix_pre=m_g_mix_pre, w_in=m_w_in, b_in=m_b_in, w_pool=m_w_pool, pool_scale=m_pool_scale,
                attn_sinks=m_attn_sinks, w_branch_pool=m_w_branch_pool, w_branch_attn=m_w_branch_attn,
                w_out=m_w_out, g_mix_post=m_g_mix_post, g_mlp_pre=m_g_mlp_pre, w_up=m_w_up, w_down=m_w_down,
                g_mlp_post=m_g_mlp_post)
    mom2 = dict(g_mix_pre=v_g_mix_pre, w_in=v_w_in, b_in=v_b_in, w_pool=v_w_pool, pool_scale=v_pool_scale,
                attn_sinks=v_attn_sinks, w_branch_pool=v_w_branch_pool, w_branch_attn=v_w_branch_attn,
                w_out=v_w_out, g_mix_post=v_g_mix_post, g_mlp_pre=v_g_mlp_pre, w_up=v_w_up, w_down=v_w_down,
                g_mlp_post=v_g_mlp_post)
    b_loc, seq, _ = x.shape
    x2 = x.reshape(b_loc * seq, D_MODEL)
    tgt = loss_target.reshape(b_loc * seq, D_MODEL)
    ids = jnp.stack([2 * lax.axis_index("x") + lax.axis_index("y"), lax.axis_index("c")]).astype(jnp.int32)

    def flat(n, a):
        return a[0].T if n == "w_in" else a[0]

    def unflat(n, a):
        return (a.T if n == "w_in" else a)[None]

    shards = {n: flat(n, weights[n]).astype(BF16) for n in _BIG}
    small = {n: weights[n] for n in _ORDER if n not in _BIG}
    (gearly, gmat, glate), gx, grads = _step(x2, tgt, seq, shards, small, ids)

    def two_d(src):
        return {n: src[n].reshape(4 * POOL_GC, POOL_GC) if n == "w_pool" else src[n] for n in _SMALL_NAMES}

    loss, per = _small_update(gearly, gmat, glate, two_d(weights), two_d(mom1), two_d(mom2))
    delta, new_m, new_v = {}, {}, {}
    for n in _SMALL_NAMES:
        grads[n], delta[n], new_m[n], new_v[n] = (a.reshape(weights[n].shape) for a in per[n])
    for n in _BIG:
        d, nm, nv = _adamw(flat(n, weights[n]), grads[n], flat(n, mom1[n]), flat(n, mom2[n]))
        grads[n] = unflat(n, grads[n])
        delta[n], new_m[n], new_v[n] = unflat(n, d), unflat(n, nm), unflat(n, nv)

    return (loss[0, 0], gx.reshape(x.shape), *[grads[n] for n in _ORDER], *[delta[n] for n in _ORDER],
            *[new_m[n] for n in _ORDER], *[new_v[n] for n in _ORDER])
```

```python
import jax
import jax.numpy as jnp
from jax import lax
from jax.experimental import pallas as pl
from jax.experimental.pallas import tpu as pltpu

F32 = jnp.float32
BF16 = jnp.bfloat16

D_MODEL = 1024
POOL_WINDOWS = (2, 4, 8, 16)
POOL_WIDTH = 512
POOL_GC = 128
HALO = 16
HEAD_DIM = 64
N_Q_HEADS = 8
ATTN_WIDTH = 512
KV_WIDTH = 128
BLOCK = 128
NEG_INF = -1e30
ROPE_THETA = 500000.0
ROT_DIM = 16
GATE_WIDTH = 2048
IN_WIDTH = 3328
D_FF = 4096
EPS = 1e-6
SCALE = HEAD_DIM ** -0.5
C_Q, C_K, C_V, C_G = 512, 1024, 1152, 1280

ADAM_LR, ADAM_B1, ADAM_B2, ADAM_EPS, ADAM_WD, ADAM_STEP = 0.001, 0.9, 0.999, 1e-08, 0.01, 10

N_CHIPS = 4
N_DEV = 8
LANES = 128
TM = 512
TP = 512
VMEM_MB = 56

MESH = pl.DeviceIdType.MESH
ANY = pl.BlockSpec(memory_space=pl.ANY)


def _cp(*sem, vmem=VMEM_MB):
    return pltpu.CompilerParams(dimension_semantics=sem, vmem_limit_bytes=vmem * 1024 * 1024)


def _rows(tile, cols):
    return pl.BlockSpec((tile, cols), lambda i: (i, 0))


def _const(shape):
    nd = len(shape)
    return pl.BlockSpec(shape, lambda i: (0,) * nd)


def _sds(shape, dtype):
    return jax.ShapeDtypeStruct(shape, dtype)


def _dot(a, b):
    return jnp.dot(a, b, preferred_element_type=F32)


def _dot_nt(a, b):
    return lax.dot_general(a, b, (((1,), (1,)), ((), ())), preferred_element_type=F32)


def _dot_tn(a, b):
    return lax.dot_general(a, b, (((0,), (0,)), ((), ())), preferred_element_type=F32)


def _rms(x):
    return lax.rsqrt(jnp.mean(x * x, axis=-1, keepdims=True) + EPS)


def _norm_bwd(x, g, dout):
    r = _rms(x)
    n = x * r
    dn = dout * g
    dx = r * (dn - n * jnp.mean(dn * n, axis=-1, keepdims=True))
    return dx, jnp.sum(dout * n, axis=0, keepdims=True)


def _rot_fwd(t, c, a, bt):
    return t * c + pltpu.roll(t, LANES - 8, 1) * a + pltpu.roll(t, 8, 1) * bt


def _rot_bwd(d, c, a, bt):
    return d * c + pltpu.roll(d * a, 8, 1) + pltpu.roll(d * bt, LANES - 8, 1)


def _rope_tables(seq):
    pos = jnp.arange(seq, dtype=F32)
    inv_freq = ROPE_THETA ** (-jnp.arange(0, ROT_DIM, 2, dtype=F32) / ROT_DIM)
    ang = pos[:, None] * inv_freq[None, :]
    cos, sin = jnp.cos(ang), jnp.sin(ang)
    ones = jnp.ones((seq, HEAD_DIM - ROT_DIM), F32)
    zeros8 = jnp.zeros((seq, 8), F32)
    zrest = jnp.zeros((seq, HEAD_DIM - ROT_DIM), F32)
    c = jnp.concatenate([cos, cos, ones], axis=1)
    a = jnp.concatenate([-sin, zeros8, zrest], axis=1)
    bt = jnp.concatenate([zeros8, sin, zrest], axis=1)
    return tuple(jnp.tile(t, (1, 2)) for t in (c, a, bt))


class _Exchange:
    def __init__(self, inputs, out_shapes, sems, start, finish, aliases=None, middle=None):
        self.inputs, self.out_shapes, self.sems = list(inputs), list(out_shapes), list(sems)
        self.start, self.finish, self.aliases = start, finish, dict(aliases or {})
        self.middle = middle


def _call(body, *, name, grid, in_specs, out_specs, out_shape, args, scratch=(), sem=(), exchanges=()):
    in_specs, out_specs, out_shape, scratch = list(in_specs), list(out_specs), list(out_shape), list(scratch)
    if not exchanges:
        return pl.pallas_call(body, name=name, grid=grid, in_specs=in_specs, out_specs=out_specs,
                              out_shape=out_shape, scratch_shapes=scratch, compiler_params=_cp(*sem))(*args)
    n_in, n_out, n_scr = len(in_specs), len(out_specs), len(scratch)
    x_in = [a for ex in exchanges for a in ex.inputs]
    x_out = [s for ex in exchanges for s in ex.out_shapes]
    x_sem = [s for ex in exchanges for s in ex.sems]
    aliases, i_off, o_off = {}, n_in, n_out
    for ex in exchanges:
        for i, o in ex.aliases.items():
            aliases[i_off + i] = o_off + o
        i_off += len(ex.inputs)
        o_off += len(ex.out_shapes)

    def split(flat):
        out, pos = [], 0
        for ex, n in zip(exchanges, flat[1]):
            out.append(flat[0][pos:pos + n])
            pos += n
        return out

    def carrier(*refs):
        pos = 0
        groups = []
        for n in (n_in, len(x_in), n_out, len(x_out), n_scr, len(x_sem)):
            groups.append(refs[pos:pos + n])
            pos += n
        ins, xin, outs, xout, scr, xsem = groups
        xin = split((xin, [len(ex.inputs) for ex in exchanges]))
        xout = split((xout, [len(ex.out_shapes) for ex in exchanges]))
        xsem = split((xsem, [len(ex.sems) for ex in exchanges]))
        first = pl.program_id(0) == 0
        last = pl.program_id(0) == grid[0] - 1
        for d in range(1, len(grid)):
            first = jnp.logical_and(first, pl.program_id(d) == 0)
            last = jnp.logical_and(last, pl.program_id(d) == grid[d] - 1)

        @pl.when(first)
        def _():
            for ex, i, o, s in zip(exchanges, xin, xout, xsem):
                ex.start(i, o, s)

        if any(ex.middle for ex in exchanges):
            half = pl.program_id(0) == grid[0] // 2
            for d in range(1, len(grid)):
                half = jnp.logical_and(half, pl.program_id(d) == 0)

            @pl.when(half)
            def _():
                for ex, i, o, s in zip(exchanges, xin, xout, xsem):
                    if ex.middle:
                        ex.middle(i, o, s)

        body(*ins, *outs, *scr)

        @pl.when(last)
        def _():
            for ex, i, o, s in zip(exchanges, xin, xout, xsem):
                ex.finish(i, o, s)

    res = pl.pallas_call(
        carrier, name=name, grid=grid, in_specs=in_specs + [ANY] * len(x_in),
        out_specs=out_specs + [ANY] * len(x_out), out_shape=out_shape + x_out,
        scratch_shapes=scratch + x_sem, input_output_aliases=aliases,
        compiler_params=_cp(*(["arbitrary"] * len(grid))),
    )(*args, *x_in)
    return res[:n_out], split((res[n_out:], [len(ex.out_shapes) for ex in exchanges]))


def _alone(name, *exchanges):
    n_in = [len(ex.inputs) for ex in exchanges]
    n_out = [len(ex.out_shapes) for ex in exchanges]
    n_sem = [len(ex.sems) for ex in exchanges]
    aliases, i_off, o_off = {}, 0, 0
    for ex in exchanges:
        for i, o in ex.aliases.items():
            aliases[i_off + i] = o_off + o
        i_off += len(ex.inputs)
        o_off += len(ex.out_shapes)

    def split(flat, counts):
        out, pos = [], 0
        for n in counts:
            out.append(flat[pos:pos + n])
            pos += n
        return out

    def body(*refs):
        ins, outs, sems = split(refs, [sum(n_in), sum(n_out), sum(n_sem)])
        groups = list(zip(exchanges, split(ins, n_in), split(outs, n_out), split(sems, n_sem)))
        for ex, i, o, s in groups:
            ex.start(i, o, s)
        for ex, i, o, s in groups:
            if ex.middle:
                ex.middle(i, o, s)
        for ex, i, o, s in groups:
            ex.finish(i, o, s)

    res = pl.pallas_call(
        body, name=name, in_specs=[ANY] * sum(n_in), out_specs=[ANY] * sum(n_out),
        out_shape=[s for ex in exchanges for s in ex.out_shapes],
        scratch_shapes=[s for ex in exchanges for s in ex.sems], input_output_aliases=aliases,
    )(*[a for ex in exchanges for a in ex.inputs])
    return split(res, n_out)


def _place():
    x, y, c = lax.axis_index("x"), lax.axis_index("y"), lax.axis_index("c")
    chips = [(1 - x, y), (x, 1 - y), (1 - x, 1 - y)]
    return x, y, c, chips


def _remote(src, dst, send, recv, to):
    return pltpu.make_async_remote_copy(src_ref=src, dst_ref=dst, send_sem=send, recv_sem=recv,
                                        device_id=to, device_id_type=MESH)


def _ex_gather(shards):
    nw = len(shards)
    hrs = [s.shape[0] // 2 for s in shards]

    def copies(ins, outs, sems):
        s1, r1, s2, r2, fs, fr = sems
        x, y, c, _ = _place()
        me, xn, yn, dg = (x, y), (1 - x, y), (x, 1 - y), (1 - x, 1 - y)
        nbr = (xn, yn)
        sibling = (x, y, 1 - c)

        def piece(w, chip, core, part=None):
            hr = hrs[w]
            rows = pl.ds(core * hr, hr) if part is None else pl.ds(core * hr + part * (hr // 2), hr // 2)
            return outs[w].at[2 * chip[0] + chip[1], rows]

        def first(w, k):
            return _remote(ins[w].at[pl.ds(c * hrs[w], hrs[w])], piece(w, me, c), s1.at[w, k], r1.at[w, k],
                           (*nbr[k], c))

        def landed(w, k):
            return _remote(piece(w, nbr[k], c), piece(w, nbr[k], c), s1.at[w, k], r1.at[w, k], (*nbr[k], c))

        def onward(w, k):
            return _remote(piece(w, nbr[k], c, k), piece(w, nbr[k], c, k), s2.at[w, k], r2.at[w, k],
                           (*nbr[1 - k], c))

        def arrived(w, k):
            return _remote(piece(w, dg, c, k), piece(w, dg, c, k), s2.at[w, k], r2.at[w, k], (*nbr[1 - k], c))

        def passed(w, j):
            chip = (xn, yn, dg)[j]
            return _remote(piece(w, chip, c), piece(w, chip, c), fs.at[w, j], fr.at[w, j], sibling)

        def handed(w, j):
            chip = (xn, yn, dg)[j]
            return _remote(piece(w, chip, 1 - c), piece(w, chip, 1 - c), fs.at[w, j], fr.at[w, j], sibling)

        return first, landed, onward, arrived, passed, handed

    def start(ins, outs, sems):
        first = copies(ins, outs, sems)[0]
        for w in range(nw):
            for k in range(2):
                first(w, k).start()

    def middle(ins, outs, sems):
        _, landed, onward, _, passed, _ = copies(ins, outs, sems)
        for w in range(nw):
            for k in range(2):
                landed(w, k).wait_recv()
                onward(w, k).start()
                passed(w, k).start()

    def finish(ins, outs, sems):
        first, _, onward, arrived, passed, handed = copies(ins, outs, sems)
        for w in range(nw):
            for k in range(2):
                arrived(w, k).wait_recv()
            passed(w, 2).start()
        for w in range(nw):
            for j in range(3):
                handed(w, j).wait_recv()
        for w in range(nw):
            for k in range(2):
                first(w, k).wait_send()
                onward(w, k).wait_send()
            for j in range(3):
                passed(w, j).wait_send()

    return _Exchange(shards, [_sds((N_CHIPS,) + s.shape, s.dtype) for s in shards],
                     [pltpu.SemaphoreType.DMA((nw, 2))] * 4 + [pltpu.SemaphoreType.DMA((nw, 3))] * 2,
                     start, finish, middle=middle)


def _ex_pair(grads):
    nw = len(grads)

    def copies(ins, outs, sems):
        x, y, c, _ = _place()
        out = []
        for w in range(nw):
            hr = grads[w].shape[1] // 2
            out.append(_remote(ins[w].at[:, pl.ds((1 - c) * hr, hr)], outs[w], sems[0].at[w], sems[1].at[w],
                               (x, y, 1 - c)))
        return out

    def start(ins, outs, sems):
        for cp in copies(ins, outs, sems):
            cp.start()

    def finish(ins, outs, sems):
        for cp in copies(ins, outs, sems):
            cp.wait()

    return _Exchange(grads, [_sds((N_CHIPS, g.shape[1] // 2, g.shape[2]), F32) for g in grads],
                     [pltpu.SemaphoreType.DMA((nw,))] * 2, start, finish)


def _ex_chip(pieces):
    nw = len(pieces)

    def copies(ins, outs, sems):
        x, y, c, chips = _place()
        return [_remote(ins[w].at[2 * cx + cy], outs[w].at[k], sems[0].at[w, k], sems[1].at[w, k], (cx, cy, c))
                for w in range(nw) for k, (cx, cy) in enumerate(chips)]

    def start(ins, outs, sems):
        for cp in copies(ins, outs, sems):
            cp.start()

    def finish(ins, outs, sems):
        for cp in copies(ins, outs, sems):
            cp.wait()

    return _Exchange(pieces, [_sds((3,) + p.shape[1:], BF16) for p in pieces],
                     [pltpu.SemaphoreType.DMA((nw, 3))] * 2, start, finish)


def _ex_swap(fulls):
    nw = len(fulls)

    def start(ins, outs, sems):
        x, y, c, _ = _place()
        for w in range(nw):
            hr = fulls[w].shape[0] // 2
            mine = pl.ds(c * hr, hr)
            _remote(ins[w].at[mine], outs[w].at[mine], sems[0].at[w], sems[1].at[w], (x, y, 1 - c)).start()

    def finish(ins, outs, sems):
        x, y, c, _ = _place()
        for w in range(nw):
            hr = fulls[w].shape[0] // 2
            mine, theirs = pl.ds(c * hr, hr), pl.ds((1 - c) * hr, hr)
            _remote(ins[w].at[mine], outs[w].at[mine], sems[0].at[w], sems[1].at[w], (x, y, 1 - c)).wait_send()
            _remote(ins[w].at[theirs], outs[w].at[theirs], sems[0].at[w], sems[1].at[w], (x, y, 1 - c)).wait_recv()

    return _Exchange(fulls, [_sds(f.shape, F32) for f in fulls], [pltpu.SemaphoreType.DMA((nw,))] * 2,
                     start, finish, aliases={w: w for w in range(nw)})


def _ex_allgather(blocks):
    nb = len(blocks)

    def copies(ins, outs, sems):
        send, recv, lsem = sems
        x, y, c, chips = _place()
        me, sibling = (x, y, c), (x, y, 1 - c)

        def rows(b, px, py, pc):
            m_per = blocks[b].shape[0]
            return outs[b].at[pl.ds((4 * px + 2 * py + pc) * m_per, m_per), :]

        def copy(b, k, blk, to, src=None):
            return _remote(rows(b, *blk) if src is None else src, rows(b, *blk), send.at[b, k], recv.at[b, k], to)

        def mine(b):
            return pltpu.make_async_copy(ins[b], rows(b, *me), lsem.at[b])

        def first(b, k):
            return copy(b, k, me, sibling if k == 0 else (*chips[k - 1], c), src=ins[b])

        def passed(b, j):
            return copy(b, 4 + j, (*chips[j], c), sibling)

        def landed(b, j):
            return copy(b, 1 + j, (*chips[j], c), me)

        def handed(b, k):
            return copy(b, 0, sibling, me) if k == 0 else copy(b, 3 + k, (*chips[k - 1], 1 - c), me)

        return mine, first, passed, landed, handed

    def start(ins, outs, sems):
        mine, first, _, _, _ = copies(ins, outs, sems)
        for b in range(nb):
            mine(b).start()
            for k in range(4):
                first(b, k).start()

    def finish(ins, outs, sems):
        mine, first, passed, landed, handed = copies(ins, outs, sems)
        sent = []
        for b in range(nb):
            for j in range(3):
                landed(b, j).wait_recv()
                cp = passed(b, j)
                cp.start()
                sent.append(cp)
        for b in range(nb):
            for k in range(4):
                handed(b, k).wait_recv()
            for k in range(4):
                first(b, k).wait_send()
        for cp in sent:
            cp.wait_send()
        for b in range(nb):
            mine(b).wait()

    return _Exchange(blocks, [_sds((N_DEV * b.shape[0], b.shape[1]), F32) for b in blocks],
                     [pltpu.SemaphoreType.DMA((nb, 7)), pltpu.SemaphoreType.DMA((nb, 7)), pltpu.SemaphoreType.DMA((nb,))],
                     start, finish)


def _inproj(x2, g1, w_in_t, b_in, tabs, seq, exchanges=()):
    T = x2.shape[0]
    tm = min(TM, seq)
    nseq = seq // tm

    def body(x_ref, g_ref, w_ref, b_ref, c_ref, a_ref, bt_ref, h_ref, u_ref, q_ref, k_ref, v_ref, gate_ref):
        x = x_ref[...]
        h = (x * _rms(x) * g_ref[...]).astype(BF16)
        h_ref[...] = h

        def proj(lo, hi):
            return _dot_nt(h, w_ref[lo:hi, :]) + b_ref[:, lo:hi]

        c, a, bt = c_ref[...], a_ref[...], bt_ref[...]
        u_ref[...] = proj(0, C_Q)
        q = proj(C_Q, C_K)
        for p in range(4):
            sl = slice(LANES * p, LANES * (p + 1))
            q_ref[:, sl] = (_rot_fwd(q[:, sl], c, a, bt) * SCALE).astype(BF16)
        kv = proj(C_K, C_G)
        k_ref[...] = _rot_fwd(kv[:, :KV_WIDTH], c, a, bt).astype(BF16)
        v_ref[...] = kv[:, KV_WIDTH:].astype(BF16)
        for j in range(2):
            lo = C_G + D_MODEL * j
            gate_ref[:, D_MODEL * j:D_MODEL * (j + 1)] = jax.nn.sigmoid(proj(lo, lo + D_MODEL)).astype(BF16)

    tab = pl.BlockSpec((tm, LANES), lambda i: (i % nseq, 0))
    return _call(
        body, name="inproj", grid=(T // tm,),
        in_specs=[_rows(tm, D_MODEL), _const((1, D_MODEL)), _const((IN_WIDTH, D_MODEL)), _const((1, IN_WIDTH)),
                  tab, tab, tab],
        out_specs=[_rows(tm, D_MODEL), _rows(tm, POOL_WIDTH), _rows(tm, ATTN_WIDTH), _rows(tm, KV_WIDTH),
                   _rows(tm, KV_WIDTH), _rows(tm, GATE_WIDTH)],
        out_shape=[_sds((T, D_MODEL), BF16), _sds((T, POOL_WIDTH), F32), _sds((T, ATTN_WIDTH), BF16),
                   _sds((T, KV_WIDTH), BF16), _sds((T, KV_WIDTH), BF16), _sds((T, GATE_WIDTH), BF16)],
        args=(x2, g1, w_in_t, b_in, *tabs), sem=("parallel",), exchanges=exchanges)


def _inv_count(pos, w):
    return 1.0 / jnp.minimum(pos + 1, w).astype(F32)


def _pool_fwd(u, w_pool, pool_scale, seq):
    T = u.shape[0]
    tp = min(TP, seq)
    nseq = seq // tp
    per = tp // HALO

    def body(u_ref, prev_ref, w_ref, s_ref, diff_ref, y_ref):
        i = pl.program_id(0)
        first = (i % nseq) == 0
        prev = jnp.where(first, 0.0, prev_ref[...])
        ext = jnp.concatenate([prev, u_ref[...]], axis=0)
        pos = (i % nseq) * tp + lax.broadcasted_iota(jnp.int32, (tp, 1), 0)
        for gi, w in enumerate(POOL_WINDOWS):
            sl = slice(POOL_GC * gi, POOL_GC * (gi + 1))
            xg = ext[:, sl]
            s = xg
            sh = 1
            while sh < w:
                s = s + pltpu.roll(s, sh, 0)
                sh *= 2
            pooled = s[HALO:] * _inv_count(pos, w)
            diff = (pooled - xg[HALO:]).astype(BF16)
            diff_ref[:, sl] = diff
            mixed = _dot(diff, w_ref[gi].astype(BF16))
            y_ref[:, sl] = (mixed * s_ref[:, sl]).astype(BF16)

    return _call(
        body, name="pool_fwd", grid=(T // tp,),
        in_specs=[_rows(tp, POOL_WIDTH),
                  pl.BlockSpec((HALO, POOL_WIDTH), lambda i: (jnp.maximum(i * per - 1, 0), 0)),
                  _const((4, POOL_GC, POOL_GC)), _const((1, POOL_WIDTH))],
        out_specs=[_rows(tp, POOL_WIDTH), _rows(tp, POOL_WIDTH)],
        out_shape=[_sds((T, POOL_WIDTH), BF16), _sds((T, POOL_WIDTH), BF16)],
        args=(u, u, w_pool, pool_scale), sem=("parallel",))


GROUP = 4
GROWS = GROUP * BLOCK


def _attn_masks(n):
    qi = lax.broadcasted_iota(jnp.int32, (GROWS, 2 * BLOCK), 0) % BLOCK
    kj = lax.broadcasted_iota(jnp.int32, (GROWS, 2 * BLOCK), 1)
    rel = qi + BLOCK - kj
    valid = (rel >= 0) & (rel < BLOCK) & (kj >= jnp.where(n > 0, 0, BLOCK))
    lo = lax.broadcasted_iota(jnp.int32, (BLOCK, LANES), 1) < HEAD_DIM
    return valid, lo


def _by_example(bl, *arrays):
    return [a.reshape(bl, a.shape[0] // bl, a.shape[1]) for a in arrays]


def _stack_heads(ref, h, lo):
    keep = lo if h == 0 else jnp.logical_not(lo)
    pieces = []
    for p in (2 * h, 2 * h + 1):
        xp = ref[:, LANES * p:LANES * (p + 1)].astype(F32)
        for e in range(2):
            t = xp if e == h else pltpu.roll(xp, HEAD_DIM, 1)
            pieces.append(jnp.where(keep, t, 0.0).astype(BF16))
    return jnp.concatenate(pieces, axis=0)


def _unstack_heads(stacked, h, lo):
    pairs = []
    for j in range(2):
        parts = []
        for e in range(2):
            t = stacked[BLOCK * (2 * j + e):BLOCK * (2 * j + e + 1)]
            parts.append(t if e == h else pltpu.roll(t, HEAD_DIM, 1))
        pairs.append(jnp.where(lo, parts[0], parts[1]))
    return pairs


def _sink_rows(sink_ref, h):
    head = lax.broadcasted_iota(jnp.int32, (GROWS, 1), 0) // BLOCK
    col = jnp.zeros((GROWS, 1), F32) + sink_ref[GROUP * h]
    for g in range(1, GROUP):
        col = jnp.where(head == g, sink_ref[GROUP * h + g], col)
    return col


def _group_probs(qs, kk, valid, sink):
    s = jnp.where(valid, _dot_nt(qs, kk), NEG_INF)
    m = jnp.maximum(jnp.max(s, axis=1, keepdims=True), sink)
    ex = jnp.exp(s - m)
    es = jnp.exp(sink - m)
    inv = 1.0 / (jnp.sum(ex, axis=1, keepdims=True) + es)
    return ex * inv, es * inv


def _attn_fwd(q, k, v, sinks, seq, exchanges=()):
    T = q.shape[0]
    nb = seq // BLOCK
    bl = T // seq

    def body(sink_ref, q_ref, kp_ref, kc_ref, vp_ref, vc_ref, o_ref):
        valid, lo = _attn_masks(pl.program_id(0))
        for b in range(bl):
            kk = jnp.concatenate([kp_ref[b], kc_ref[b]], axis=0)
            vv = jnp.concatenate([vp_ref[b], vc_ref[b]], axis=0)
            for h in range(2):
                qs = _stack_heads(q_ref.at[b], h, lo)
                pr, _ = _group_probs(qs, kk, valid, _sink_rows(sink_ref, h))
                o = _dot(pr.astype(BF16), vv)
                for j, pair in enumerate(_unstack_heads(o, h, lo)):
                    p = 2 * h + j
                    o_ref[b, :, LANES * p:LANES * (p + 1)] = pair.astype(BF16)

    cur = lambda n: (0, n, 0)
    prv = lambda n: (0, jnp.maximum(n - 1, 0), 0)
    kv = lambda m: pl.BlockSpec((bl, BLOCK, KV_WIDTH), m)
    res = _call(
        body, name="attn_fwd", grid=(nb,),
        in_specs=[pl.BlockSpec(memory_space=pltpu.SMEM), pl.BlockSpec((bl, BLOCK, ATTN_WIDTH), cur),
                  kv(prv), kv(cur), kv(prv), kv(cur)],
        out_specs=[pl.BlockSpec((bl, BLOCK, ATTN_WIDTH), cur)],
        out_shape=[_sds((bl, seq, ATTN_WIDTH), BF16)],
        args=(sinks, *_by_example(bl, q, k, k, v, v)), sem=("parallel",), exchanges=exchanges)
    if exchanges:
        return [res[0][0].reshape(T, ATTN_WIDTH)], res[1]
    return [res[0].reshape(T, ATTN_WIDTH)]


def _branch(y, w_ref):
    return jnp.concatenate([_dot(y, w_ref[j]) for j in range(N_CHIPS)], axis=1)


def _merge_out(y_pool, y_attn, gate, x2, w_bp, w_ba, w_out, g2, g3, exchanges=()):
    T = x2.shape[0]
    tm = min(TM, T)

    def body(yp_ref, ya_ref, gate_ref, x_ref, wbp_ref, wba_ref, wo_ref, g2_ref, g3_ref,
             mg_ref, mix_ref, x1_ref, h2_ref):
        bp, ba = _branch(yp_ref[...], wbp_ref), _branch(ya_ref[...], wba_ref)
        merged = (gate_ref[:, :D_MODEL].astype(F32) * bp + gate_ref[:, D_MODEL:].astype(F32) * ba).astype(BF16)
        mg_ref[...] = merged
        mix = _dot(merged, wo_ref[...])
        mix_ref[...] = mix
        x1 = x_ref[...] + mix * _rms(mix) * g2_ref[...]
        x1_ref[...] = x1
        h2_ref[...] = (x1 * _rms(x1) * g3_ref[...]).astype(BF16)

    return _call(
        body, name="merge_out", grid=(T // tm,),
        in_specs=[_rows(tm, POOL_WIDTH), _rows(tm, ATTN_WIDTH), _rows(tm, GATE_WIDTH), _rows(tm, D_MODEL),
                  _const(w_bp.shape), _const(w_ba.shape), _const((D_MODEL, D_MODEL)),
                  _const((1, D_MODEL)), _const((1, D_MODEL))],
        out_specs=[_rows(tm, D_MODEL)] * 4,
        out_shape=[_sds((T, D_MODEL), BF16), _sds((T, D_MODEL), F32), _sds((T, D_MODEL), F32),
                   _sds((T, D_MODEL), BF16)],
        args=(y_pool, y_attn, gate, x2, w_bp, w_ba, w_out, g2, g3), sem=("parallel",), exchanges=exchanges)


HALF = D_MODEL // 2
TM_MLP = 256


def _mlp_core(h2, x1, mix, tgt, w_up, w_down, g4, g3, g2):
    T = h2.shape[0]
    tm = min(TM_MLP, T)

    def body(h_ref, x1_ref, mix_ref, t_ref, g_ref, g3_ref, g2_ref, ua_hbm, ub_hbm, da_hbm, db_hbm,
             act_ref, dff_ref, dup_ref, dx1_ref, dmix_ref, loss_ref, dg_ref, dg3_ref, dg2_ref,
             wu, wd, relu_scr, sems):
        def weight_copy(i):
            src, dst = ((ua_hbm, wu.at[:, :HALF]), (ub_hbm, wu.at[:, HALF:]),
                        (da_hbm, wd.at[:, :HALF]), (db_hbm, wd.at[:, HALF:]))[i]
            return pltpu.make_async_copy(src, dst, sems.at[i])

        @pl.when(pl.program_id(0) == 0)
        def _():
            for i in range(4):
                weight_copy(i).start()
            loss_ref[...] = jnp.zeros_like(loss_ref)
            for ref in (dg_ref, dg3_ref, dg2_ref):
                ref[...] = jnp.zeros_like(ref)
            weight_copy(0).wait()
            weight_copy(1).wait()

        h = h_ref[...]
        ff = None
        for j in range(N_CHIPS):
            lo = D_MODEL * j
            relu = jnp.maximum(_dot(h, wu[j]), 0.0)
            if j == 0:
                @pl.when(pl.program_id(0) == 0)
                def _():
                    weight_copy(2).wait()
                    weight_copy(3).wait()
            relu_scr[:, lo:lo + D_MODEL] = relu
            act = jnp.square(relu).astype(BF16)
            act_ref[:, lo:lo + D_MODEL] = act
            t = _dot(act, wd[j])
            ff = t if ff is None else ff + t
        g = g_ref[...]
        x1 = x1_ref[...]
        err = x1 + ff * _rms(ff) * g - t_ref[...]
        loss_ref[...] += jnp.sum(err * err) * (0.5 / D_MODEL)
        dy = err * (1.0 / D_MODEL)
        dff, dg = _norm_bwd(ff, g, dy)
        dg_ref[...] += dg
        dff = dff.astype(BF16)
        dff_ref[...] = dff
        dh2 = None
        for j in range(N_CHIPS):
            lo = D_MODEL * j
            dup = (_dot_nt(dff, wd[j]) * (2.0 * relu_scr[:, lo:lo + D_MODEL])).astype(BF16)
            dup_ref[:, lo:lo + D_MODEL] = dup
            t = _dot_nt(dup, wu[j])
            dh2 = t if dh2 is None else dh2 + t
        dx, dg3 = _norm_bwd(x1, g3_ref[...], dh2)
        dx1 = dy + dx
        dx1_ref[...] = dx1
        dg3_ref[...] += dg3
        dmix, dg2 = _norm_bwd(mix_ref[...], g2_ref[...], dx1)
        dmix_ref[...] = dmix.astype(BF16)
        dg2_ref[...] += dg2

    slabs = pltpu.VMEM((N_CHIPS, D_MODEL, D_MODEL), BF16)
    gain = _const((1, D_MODEL))
    return pl.pallas_call(
        body, name="mlp_core", grid=(T // tm,),
        in_specs=[_rows(tm, D_MODEL)] * 4 + [gain] * 3 + [ANY] * 4,
        out_specs=[_rows(tm, D_FF), _rows(tm, D_MODEL), _rows(tm, D_FF), _rows(tm, D_MODEL), _rows(tm, D_MODEL),
                   _const((8, LANES)), gain, gain, gain],
        out_shape=[_sds((T, D_FF), BF16), _sds((T, D_MODEL), BF16), _sds((T, D_FF), BF16), _sds((T, D_MODEL), F32),
                   _sds((T, D_MODEL), BF16), _sds((8, LANES), F32)] + [_sds((1, D_MODEL), F32)] * 3,
        scratch_shapes=[slabs] * 2 + [pltpu.VMEM((tm, D_FF), F32), pltpu.SemaphoreType.DMA((4,))],
        compiler_params=_cp("arbitrary"),
    )(h2, x1, mix, tgt, g4, g3, g2, *w_up, *w_down)


def _dw(tag, a, g, ta, tn, shard_cols=False, exchanges=()):
    T, ka = a.shape
    n = g.shape[1]
    tk = min(2 * TM, T)
    nk = T // tk

    def body(a_ref, g_ref, o_ref):
        @pl.when(pl.program_id(2) == 0)
        def _():
            o_ref[...] = jnp.zeros_like(o_ref)

        o_ref[...] += _dot_tn(a_ref[...], g_ref[...])

    if shard_cols:
        per = (n // N_CHIPS) // tn
        out_spec = pl.BlockSpec((None, ta, tn), lambda i, j, k: (j // per, i, j % per))
        out_shape = _sds((N_CHIPS, ka, n // N_CHIPS), F32)
    else:
        out_spec = pl.BlockSpec((ta, tn), lambda i, j, k: (i, j))
        out_shape = _sds((ka, n), F32)
    return _call(
        body, name="dw_" + tag, grid=(ka // ta, n // tn, nk),
        in_specs=[pl.BlockSpec((tk, ta), lambda i, j, k: (k, i)), pl.BlockSpec((tk, tn), lambda i, j, k: (k, j))],
        out_specs=[out_spec], out_shape=[out_shape],
        args=(a, g), sem=("parallel", "parallel", "arbitrary"), exchanges=exchanges)


def _dw_mix(merged, dmix, y_pool, dbp, y_attn, dba, exchanges=()):
    T = merged.shape[0]
    tk = min(2 * TM, T)
    c = D_MODEL // N_CHIPS

    def body(mg_ref, dmix_ref, yp_ref, dbp_ref, ya_ref, dba_ref, out_ref, bp_ref, ba_ref):
        @pl.when(pl.program_id(0) == 0)
        def _():
            for ref in (out_ref, bp_ref, ba_ref):
                ref[...] = jnp.zeros_like(ref)

        out_ref[...] += _dot_tn(mg_ref[...], dmix_ref[...])
        for y_ref, d_ref, o_ref in ((yp_ref, dbp_ref, bp_ref), (ya_ref, dba_ref, ba_ref)):
            res = _dot_tn(y_ref[...], d_ref[...])
            for j in range(N_CHIPS):
                o_ref[j] += res[:, c * j:c * (j + 1)]

    slabs = (N_CHIPS, POOL_WIDTH, c)
    return _call(
        body, name="dw_mix", grid=(T // tk,),
        in_specs=[_rows(tk, D_MODEL), _rows(tk, D_MODEL), _rows(tk, POOL_WIDTH), _rows(tk, D_MODEL),
                  _rows(tk, ATTN_WIDTH), _rows(tk, D_MODEL)],
        out_specs=[_const((D_MODEL, D_MODEL)), _const(slabs), _const(slabs)],
        out_shape=[_sds((D_MODEL, D_MODEL), F32), _sds(slabs, F32), _sds(slabs, F32)],
        args=(merged, dmix, y_pool, dbp, y_attn, dba), sem=("arbitrary",), exchanges=exchanges)


def _merge_bwd(dmix, gate, y_pool, y_attn, w_out, w_bp, w_ba, exchanges=()):
    T = dmix.shape[0]
    tm = min(TM, T)

    def body(dmix_ref, gate_ref, yp_ref, ya_ref, wo_ref, wbp_ref, wba_ref,
             dbp_ref, dba_ref, dgate_ref, dyp_ref, dya_ref):
        dm = _dot_nt(dmix_ref[...], wo_ref[...])
        for j, (y_ref, db_ref, w_ref, dy_ref) in enumerate(
                ((yp_ref, dbp_ref, wbp_ref, dyp_ref), (ya_ref, dba_ref, wba_ref, dya_ref))):
            sl = slice(D_MODEL * j, D_MODEL * (j + 1))
            gt = gate_ref[:, sl].astype(F32)
            db = (dm * gt).astype(BF16)
            db_ref[...] = db
            dgate_ref[:, sl] = (dm * _branch(y_ref[...], w_ref) * gt * (1.0 - gt)).astype(BF16)
            cw = D_MODEL // N_CHIPS
            dy = _dot_nt(db[:, :cw], w_ref[0])
            for c in range(1, N_CHIPS):
                dy = dy + _dot_nt(db[:, cw * c:cw * (c + 1)], w_ref[c])
            dy_ref[...] = dy.astype(dy_ref.dtype)

    return _call(
        body, name="merge_bwd", grid=(T // tm,),
        in_specs=[_rows(tm, D_MODEL), _rows(tm, GATE_WIDTH), _rows(tm, POOL_WIDTH), _rows(tm, ATTN_WIDTH),
                  _const((D_MODEL, D_MODEL)), _const(w_bp.shape), _const(w_ba.shape)],
        out_specs=[_rows(tm, D_MODEL), _rows(tm, D_MODEL), _rows(tm, GATE_WIDTH), _rows(tm, POOL_WIDTH),
                   _rows(tm, ATTN_WIDTH)],
        out_shape=[_sds((T, D_MODEL), BF16), _sds((T, D_MODEL), BF16), _sds((T, GATE_WIDTH), BF16),
                   _sds((T, POOL_WIDTH), F32), _sds((T, ATTN_WIDTH), BF16)],
        args=(dmix, gate, y_pool, y_attn, w_out, w_bp, w_ba), sem=("parallel",), exchanges=exchanges)


def _attn_bwd(q, k, v, do, sinks, tabs, seq, exchanges=()):
    T = q.shape[0]
    nb = seq // BLOCK
    bl = T // seq
    steps = nb + 1

    def body(sink_ref, q_ref, do_ref, kp_ref, kc_ref, vp_ref, vc_ref, c_ref, a_ref, bt_ref, cp_ref, ap_ref, btp_ref,
             dq_ref, dk_ref, dv_ref, dsink_ref, ck_ref, cv_ref):
        n = pl.program_id(0)

        @pl.when(n == 0)
        def _():
            dsink_ref[...] = jnp.zeros_like(dsink_ref)
            ck_ref[...] = jnp.zeros_like(ck_ref)
            cv_ref[...] = jnp.zeros_like(cv_ref)

        @pl.when(n < nb)
        def _():
            valid, lo = _attn_masks(n)
            for b in range(bl):
                kk = jnp.concatenate([kp_ref[b], kc_ref[b]], axis=0)
                vv = jnp.concatenate([vp_ref[b], vc_ref[b]], axis=0)
                dk_acc = jnp.zeros((2 * BLOCK, KV_WIDTH), F32)
                dv_acc = jnp.zeros((2 * BLOCK, KV_WIDTH), F32)
                for h in range(2):
                    qs = _stack_heads(q_ref.at[b], h, lo)
                    dos = _stack_heads(do_ref.at[b], h, lo)
                    pr, ps = _group_probs(qs, kk, valid, _sink_rows(sink_ref, h))
                    dp = _dot_nt(dos, vv)
                    delta = jnp.sum(pr * dp, axis=1, keepdims=True)
                    ds = (pr * (dp - delta)).astype(BF16)
                    dsk = ps * delta
                    for g in range(GROUP):
                        idx = GROUP * h + g
                        dsink_ref[idx:idx + 1, :] += (jnp.zeros((1, LANES), F32)
                                                      - jnp.sum(dsk[BLOCK * g:BLOCK * (g + 1)]))
                    dk_acc = dk_acc + _dot_tn(ds, qs)
                    dv_acc = dv_acc + _dot_tn(pr.astype(BF16), dos)
                    for j, pair in enumerate(_unstack_heads(_dot(ds, kk) * SCALE, h, lo)):
                        sl = slice(LANES * (2 * h + j), LANES * (2 * h + j + 1))
                        dq_ref[b, :, sl] = _rot_bwd(pair, c_ref[...], a_ref[...], bt_ref[...]).astype(BF16)
                fin_k = ck_ref[b] + dk_acc[:BLOCK]
                dk_ref[b] = _rot_bwd(fin_k, cp_ref[...], ap_ref[...], btp_ref[...]).astype(BF16)
                dv_ref[b] = (cv_ref[b] + dv_acc[:BLOCK]).astype(BF16)
                ck_ref[b] = dk_acc[BLOCK:]
                cv_ref[b] = dv_acc[BLOCK:]

        @pl.when(n == nb)
        def _():
            for b in range(bl):
                dk_ref[b] = _rot_bwd(ck_ref[b], cp_ref[...], ap_ref[...], btp_ref[...]).astype(BF16)
                dv_ref[b] = cv_ref[b].astype(BF16)

    cur = lambda n: (0, jnp.minimum(n, nb - 1), 0)
    prv = lambda n: (0, jnp.clip(n - 1, 0, nb - 1), 0)
    tcur = lambda n: (jnp.minimum(n, nb - 1), 0)
    tprv = lambda n: (jnp.clip(n - 1, 0, nb - 1), 0)
    wide = lambda m: pl.BlockSpec((bl, BLOCK, ATTN_WIDTH), m)
    kv = lambda m: pl.BlockSpec((bl, BLOCK, KV_WIDTH), m)
    tab = lambda m: pl.BlockSpec((BLOCK, LANES), m)
    res = _call(
        body, name="attn_bwd", grid=(steps,),
        in_specs=[pl.BlockSpec(memory_space=pltpu.SMEM), wide(cur), wide(cur), kv(prv), kv(cur), kv(prv), kv(cur),
                  tab(tcur), tab(tcur), tab(tcur), tab(tprv), tab(tprv), tab(tprv)],
        out_specs=[wide(cur), kv(prv), kv(prv), _const((8, LANES))],
        out_shape=[_sds((bl, seq, ATTN_WIDTH), BF16), _sds((bl, seq, KV_WIDTH), BF16),
                   _sds((bl, seq, KV_WIDTH), BF16), _sds((8, LANES), F32)],
        scratch=[pltpu.VMEM((bl, BLOCK, KV_WIDTH), F32), pltpu.VMEM((bl, BLOCK, KV_WIDTH), F32)],
        args=(sinks, *_by_example(bl, q, do, k, k, v, v), *tabs, *tabs), sem=("arbitrary",), exchanges=exchanges)
    outs, rest = (res if exchanges else (res, None))
    outs = [outs[0].reshape(T, ATTN_WIDTH), outs[1].reshape(T, KV_WIDTH), outs[2].reshape(T, KV_WIDTH), outs[3]]
    return (outs, rest) if exchanges else outs


def _pool_bwd(dyp, diff, w_pool, pool_scale, seq, exchanges=()):
    T = dyp.shape[0]
    tp = min(TP, seq)
    nseq = seq // tp
    per = tp // HALO
    last_halo = T // HALO - 1

    def body(dy_ref, nxt_ref, diff_ref, w_ref, s_ref, du_ref, dw_ref, ds_ref):
        i = pl.program_id(0)

        @pl.when(i == 0)
        def _():
            dw_ref[...] = jnp.zeros_like(dw_ref)
            ds_ref[...] = jnp.zeros_like(ds_ref)

        last = (i % nseq) == nseq - 1
        nxt = jnp.where(last, 0.0, nxt_ref[...])
        ext = jnp.concatenate([dy_ref[...], nxt], axis=0) * s_ref[...]
        pos = (i % nseq) * tp + lax.broadcasted_iota(jnp.int32, (tp + HALO, 1), 0)
        for gi, w in enumerate(POOL_WINDOWS):
            sl = slice(POOL_GC * gi, POOL_GC * (gi + 1))
            wg = w_ref[gi].astype(BF16)
            dmx = ext[:, sl].astype(BF16)
            ddiff = _dot_nt(dmx, wg)
            s = ddiff * _inv_count(pos, w)
            sh = 1
            while sh < w:
                s = s + pltpu.roll(s, tp + HALO - sh, 0)
                sh *= 2
            du_ref[:, sl] = (s[:tp] - ddiff[:tp]).astype(BF16)
            dg = diff_ref[:, sl]
            dw_ref[gi] += _dot_tn(dg, dmx[:tp])
            ds_ref[:, sl] += jnp.sum(dy_ref[:, sl] * _dot(dg, wg), axis=0, keepdims=True)

    return _call(
        body, name="pool_bwd", grid=(T // tp,),
        in_specs=[_rows(tp, POOL_WIDTH),
                  pl.BlockSpec((HALO, POOL_WIDTH), lambda i: (jnp.minimum((i + 1) * per, last_halo), 0)),
                  _rows(tp, POOL_WIDTH), _const((4, POOL_GC, POOL_GC)), _const((1, POOL_WIDTH))],
        out_specs=[_rows(tp, POOL_WIDTH), _const((4, POOL_GC, POOL_GC)), _const((1, POOL_WIDTH))],
        out_shape=[_sds((T, POOL_WIDTH), BF16), _sds((4, POOL_GC, POOL_GC), F32), _sds((1, POOL_WIDTH), F32)],
        args=(dyp, dyp, diff, w_pool, pool_scale), sem=("arbitrary",), exchanges=exchanges)


_PARTS = ((0, C_Q), (C_Q, C_K), (C_K, C_V), (C_V, C_G), (C_G, IN_WIDTH))


def _inproj_bwd(parts, x2, dx1, w_in_t, g1, exchanges=()):
    T = x2.shape[0]
    tm = min(TM, T)

    def body(du_ref, dq_ref, dk_ref, dv_ref, dgt_ref, x_ref, dx1_ref, w_ref, g_ref, gx_ref, dg_ref):
        @pl.when(pl.program_id(0) == 0)
        def _():
            dg_ref[...] = jnp.zeros_like(dg_ref)

        dh = jnp.zeros((tm, D_MODEL), F32)
        for (lo, hi), p_ref in zip(_PARTS, (du_ref, dq_ref, dk_ref, dv_ref, dgt_ref)):
            dh = dh + _dot(p_ref[...], w_ref[lo:hi, :])
        dx, dg = _norm_bwd(x_ref[...], g_ref[...], dh)
        gx_ref[...] = dx1_ref[...] + dx
        dg_ref[...] += dg

    return _call(
        body, name="inproj_bwd", grid=(T // tm,),
        in_specs=[_rows(tm, hi - lo) for lo, hi in _PARTS]
        + [_rows(tm, D_MODEL), _rows(tm, D_MODEL), _const((IN_WIDTH, D_MODEL)), _const((1, D_MODEL))],
        out_specs=[_rows(tm, D_MODEL), _const((1, D_MODEL))],
        out_shape=[_sds((T, D_MODEL), F32), _sds((1, D_MODEL), F32)],
        args=(*parts, x2, dx1, w_in_t, g1), sem=("arbitrary",), exchanges=exchanges)


def _dw_in(h, parts, exchanges=()):
    T = h.shape[0]
    tk = min(TM, T)

    def body(h_ref, du_ref, dq_ref, dk_ref, dv_ref, dgt_ref, o_ref, db_ref):
        @pl.when(pl.program_id(0) == 0)
        def _():
            o_ref[...] = jnp.zeros_like(o_ref)
            db_ref[...] = jnp.zeros_like(db_ref)

        hh = h_ref[...]
        for (lo, hi), p_ref in zip(_PARTS, (du_ref, dq_ref, dk_ref, dv_ref, dgt_ref)):
            part = p_ref[...]
            o_ref[lo:hi, :] += _dot_tn(part, hh)
            db_ref[:, lo:hi] += jnp.sum(part.astype(F32), axis=0, keepdims=True)

    return _call(
        body, name="dw_in", grid=(T // tk,),
        in_specs=[_rows(tk, D_MODEL)] + [_rows(tk, hi - lo) for lo, hi in _PARTS],
        out_specs=[_const((IN_WIDTH, D_MODEL)), _const((1, IN_WIDTH))],
        out_shape=[_sds((IN_WIDTH, D_MODEL), F32), _sds((1, IN_WIDTH), F32)],
        args=(h, *parts), sem=("arbitrary",), exchanges=exchanges)


def _row_tile(rows, cap=256, mult=16):
    best = None
    for t in range(mult, min(rows, cap) + 1, mult):
        if rows % t == 0:
            best = t
    if best is None:
        raise ValueError("no row tile for %d rows" % rows)
    return best


def _pair_sum(ids, full, got):
    _, r, c = full.shape
    hr = r // 2
    tr = _row_tile(hr)
    nblk = hr // tr

    def body(ids_ref, a_ref, b_ref, own_ref, sb_ref):
        s = a_ref[...] + b_ref[...]
        sb_ref[...] = s.astype(BF16)

        @pl.when(pl.program_id(1) == ids_ref[0])
        def _():
            own_ref[...] = s

    slab = pl.BlockSpec((None, tr, c), lambda i, j, ids_ref: (j, i, 0))
    return pl.pallas_call(
        body, name="pair_sum_%dx%d" % (r, c),
        grid_spec=pltpu.PrefetchScalarGridSpec(
            num_scalar_prefetch=1, grid=(nblk, N_CHIPS),
            in_specs=[pl.BlockSpec((None, tr, c), lambda i, j, ids_ref: (j, ids_ref[1] * nblk + i, 0)), slab],
            out_specs=[pl.BlockSpec((tr, c), lambda i, j, ids_ref: (i, 0)), slab]),
        out_shape=[_sds((hr, c), F32), _sds((N_CHIPS, hr, c), BF16)],
        compiler_params=_cp("parallel", "arbitrary"),
    )(ids, full, got)


def _chip_sum(ids, own, got):
    hr, c = own.shape
    tr = _row_tile(hr)
    nblk = hr // tr

    def body(ids_ref, a_ref, b_ref, o_ref):
        o_ref[...] = ((a_ref[...] + b_ref[0].astype(F32)) + b_ref[1].astype(F32)) + b_ref[2].astype(F32)

    return pl.pallas_call(
        body, name="chip_sum_%dx%d" % (hr, c),
        grid_spec=pltpu.PrefetchScalarGridSpec(
            num_scalar_prefetch=1, grid=(nblk,),
            in_specs=[pl.BlockSpec((tr, c), lambda i, ids_ref: (i, 0)),
                      pl.BlockSpec((3, tr, c), lambda i, ids_ref: (0, i, 0))],
            out_specs=pl.BlockSpec((tr, c), lambda i, ids_ref: (ids_ref[1] * nblk + i, 0))),
        out_shape=_sds((2 * hr, c), F32),
        compiler_params=_cp("parallel"),
    )(ids, own, got)


def _adamw_math(w, g, m, v):
    nm = ADAM_B1 * m + (1.0 - ADAM_B1) * g
    nv = ADAM_B2 * v + (1.0 - ADAM_B2) * jnp.square(g)
    m_hat = nm / (1.0 - ADAM_B1 ** ADAM_STEP)
    v_hat = nv / (1.0 - ADAM_B2 ** ADAM_STEP)
    return -ADAM_LR * (m_hat / (jnp.sqrt(v_hat) + ADAM_EPS) + ADAM_WD * w), nm, nv


def _adamw(w, g, m, v):
    r, c = w.shape
    tr = _row_tile(r, cap=512, mult=8)

    def body(w_ref, g_ref, m_ref, v_ref, d_ref, nm_ref, nv_ref):
        d_ref[...], nm_ref[...], nv_ref[...] = _adamw_math(w_ref[...], g_ref[...], m_ref[...], v_ref[...])

    spec = _rows(tr, c)
    return pl.pallas_call(
        body, name="adamw_%dx%d" % (r, c), grid=(r // tr,),
        in_specs=[spec] * 4, out_specs=[spec] * 3, out_shape=[_sds((r, c), F32)] * 3,
        compiler_params=_cp("parallel"),
    )(w, g, m, v)


_SMALL_NAMES = ("w_pool", "b_in", "g_mix_pre", "g_mix_post", "g_mlp_pre", "g_mlp_post", "pool_scale", "attn_sinks")
B_ROWS = -(-IN_WIDTH // D_MODEL)


def _row_block(rows):
    rows = [jnp.pad(r.astype(F32), ((0, 0), (0, D_MODEL - r.shape[1]))) for r in rows]
    return jnp.pad(jnp.concatenate(rows, axis=0), ((0, 8 - len(rows)), (0, 0)))


def _early_block(dg2, dg3, dg4, dps, dsink, loss):
    tail = jnp.concatenate([jnp.pad(dsink.reshape(1, -1), ((0, 0), (0, LANES - dsink.size))),
                            jnp.pad(loss.reshape(1, 1), ((0, 0), (0, LANES - 1)))], axis=1)
    return _row_block([dg2, dg3, dg4, dps, tail])


def _late_block(db_in, dg1):
    b = jnp.pad(db_in, ((0, 0), (0, B_ROWS * D_MODEL - IN_WIDTH))).reshape(B_ROWS, D_MODEL)
    return _row_block([b[r:r + 1] for r in range(B_ROWS)] + [dg1])


def _small_update(gearly, gmat, glate, w, m, v):
    names = _SMALL_NAMES
    n = len(names)

    def total(ref, rows):
        acc = ref[0:rows, :]
        for d in range(1, N_DEV):
            acc = acc + ref[d * rows:(d + 1) * rows, :]
        return acc

    def body(*refs):
        early_ref, gmat_ref, late_ref = refs[:3]
        w_refs, m_refs, v_refs = refs[3:3 + n], refs[3 + n:3 + 2 * n], refs[3 + 2 * n:3 + 3 * n]
        outs = refs[3 + 3 * n:]
        loss_ref, g_refs, d_refs = outs[0], outs[1:1 + n], outs[1 + n:1 + 2 * n]
        nm_refs, nv_refs = outs[1 + 2 * n:1 + 3 * n], outs[1 + 3 * n:1 + 4 * n]
        early, late = total(early_ref, 8), total(late_ref, 8)
        loss_ref[...] = jnp.sum(early[4:5, LANES:2 * LANES], axis=1, keepdims=True)
        bias = jnp.concatenate([late[r:r + 1, :] for r in range(B_ROWS - 1)]
                               + [late[B_ROWS - 1:B_ROWS, :IN_WIDTH - (B_ROWS - 1) * D_MODEL]], axis=1)
        grad = dict(b_in=bias, g_mix_pre=late[B_ROWS:B_ROWS + 1, :], g_mix_post=early[0:1, :],
                    g_mlp_pre=early[1:2, :], g_mlp_post=early[2:3, :], pool_scale=early[3:4, :POOL_WIDTH],
                    attn_sinks=early[4:5, :N_Q_HEADS])
        for i, name in enumerate(names):
            g = total(gmat_ref, 4 * POOL_GC) if name == "w_pool" else grad[name]
            g_refs[i][...] = g
            d_refs[i][...], nm_refs[i][...], nv_refs[i][...] = _adamw_math(
                w_refs[i][...], g, m_refs[i][...], v_refs[i][...])

    shapes = [_sds(w[k].shape, F32) for k in names]
    res = pl.pallas_call(
        body, name="small_update", out_shape=[_sds((1, 1), F32)] + shapes * 4,
        compiler_params=pltpu.CompilerParams(vmem_limit_bytes=VMEM_MB * 1024 * 1024),
    )(gearly, gmat, glate, *[w[k] for k in names], *[m[k] for k in names], *[v[k] for k in names])
    loss = res[0]
    per = {k: tuple(res[1 + j * n + i] for j in range(4)) for i, k in enumerate(names)}
    return loss, per


_BIG = ("w_in", "w_branch_pool", "w_branch_attn", "w_out", "w_up", "w_down")
_ORDER = ("g_mix_pre", "w_in", "b_in", "w_pool", "pool_scale", "attn_sinks", "w_branch_pool", "w_branch_attn",
          "w_out", "g_mix_post", "g_mlp_pre", "w_up", "w_down", "g_mlp_post")


def _stack_rows(slab):
    return slab.reshape(-1, slab.shape[2])


def _step(x2, tgt, seq, shards, small, ids):
    tabs = _rope_tables(seq)
    g1, g2, g3, g4 = (small[n] for n in ("g_mix_pre", "g_mix_post", "g_mlp_pre", "g_mlp_post"))
    sinks = small["attn_sinks"].reshape(N_Q_HEADS)
    w_pool = small["w_pool"].reshape(4, POOL_GC, POOL_GC)
    pool_scale = small["pool_scale"]

    def whole(shard, slabs):
        return lax.dynamic_update_slice(slabs, shard[None], (ids[0], 0, 0))

    up_a, up_b = shards["w_up"][:HALF], shards["w_up"][HALF:]
    down_a, down_b = shards["w_down"][:HALF], shards["w_down"][HALF:]
    w_in = _stack_rows(whole(shards["w_in"], _alone("gather_in", _ex_gather([shards["w_in"]]))[0][0]))
    mix_shards = [shards[n] for n in ("w_branch_pool", "w_branch_attn", "w_out")]
    (h, u, q, k, v, gate), [(*mix_slabs, got_c)] = _inproj(
        x2, g1, w_in, small["b_in"], tabs, seq, exchanges=[_ex_gather(mix_shards + [down_a])])
    w_bp, w_ba, out_slab = (whole(s, g) for s, g in zip(mix_shards, mix_slabs))
    w_out = _stack_rows(out_slab)
    diff, y_pool = _pool_fwd(u, w_pool, pool_scale, seq)
    (y_attn,), [[got_a]] = _attn_fwd(q, k, v, sinks, seq, exchanges=[_ex_gather([up_a])])
    (merged, mix, x1, h2), [[got_b, got_d]] = _merge_out(
        y_pool, y_attn, gate, x2, w_bp, w_ba, w_out, g2, g3, exchanges=[_ex_gather([up_b, down_b])])
    w_up = (whole(up_a, got_a), whole(up_b, got_b))
    w_down = (whole(down_a, got_c), whole(down_b, got_d))
    act, dff, dup, dx1, dmix, loss_acc, dg4, dg3, dg2 = _mlp_core(h2, x1, mix, tgt, w_up, w_down, g4, g3, g2)

    dw_down = _dw("down", act, dff, 1024, 1024)[0].reshape(N_CHIPS, D_FF // N_CHIPS, D_MODEL)
    (dbp, dba, dgate, dyp, dya), [[got]] = _merge_bwd(
        dmix, gate, y_pool, y_attn, w_out, w_bp, w_ba, exchanges=[_ex_pair([dw_down])])
    ps_down = _pair_sum(ids, dw_down, got)
    (dw_up,), [[got]] = _dw("up", h2, dup, 1024, 1024, shard_cols=True, exchanges=[_ex_chip([ps_down[1]])])
    half_down = _chip_sum(ids, ps_down[0], got)
    (dw_out, dw_bp, dw_ba), [[got]] = _dw_mix(merged, dmix, y_pool, dbp, y_attn, dba, exchanges=[_ex_pair([dw_up])])
    ps_up = _pair_sum(ids, dw_up, got)
    dw_mix = [dw_out.reshape(N_CHIPS, D_MODEL // N_CHIPS, D_MODEL), dw_bp, dw_ba]
    (dq, dk, dv, dsink), [[got], gots, [g_down]] = _attn_bwd(
        q, k, v, dya, sinks, tabs, seq, exchanges=[_ex_chip([ps_up[1]]), _ex_pair(dw_mix), _ex_swap([half_down])])
    half_up = _chip_sum(ids, ps_up[0], got)
    ps_mix = [_pair_sum(ids, d, g) for d, g in zip(dw_mix, gots)]
    (du, dw_pool, dps), [[g_up]] = _pool_bwd(dyp, diff, w_pool, pool_scale, seq, exchanges=[_ex_swap([half_up])])
    parts = (du, dq, dk, dv, dgate)
    early = _early_block(dg2, dg3, dg4, dps, dsink[:, 0], loss_acc[0, 0])
    mat = dw_pool.reshape(4 * POOL_GC, POOL_GC)
    (dw_in_t, db_in), [gots, [gearly, gmat]] = _dw_in(
        h, parts, exchanges=[_ex_chip([p[1] for p in ps_mix]), _ex_allgather([early, mat])])
    half_mix = [_chip_sum(ids, p[0], g) for p, g in zip(ps_mix, gots)]
    dw_in = dw_in_t.reshape(N_CHIPS, IN_WIDTH // N_CHIPS, D_MODEL)
    g_mix, [got] = _alone("swap_mix_pair_in", _ex_swap(half_mix), _ex_pair([dw_in]))
    ps_in = _pair_sum(ids, dw_in, got)
    (gx, dg1), [[got]] = _inproj_bwd(parts, x2, dx1, w_in, g1, exchanges=[_ex_chip([ps_in[1]])])
    [g_in], [glate] = _alone("swap_in_allgather", _ex_swap([_chip_sum(ids, ps_in[0], got)]),
                             _ex_allgather([_late_block(db_in, dg1)]))

    grads = dict(w_in=g_in, w_branch_pool=g_mix[1], w_branch_attn=g_mix[2], w_out=g_mix[0], w_up=g_up, w_down=g_down)
    return (gearly, gmat, glate), gx, grads


def kernel(x, g_mix_pre, w_in, b_in, w_pool, pool_scale, attn_sinks, w_branch_pool, w_branch_attn, w_out, g_mix_post, g_mlp_pre, w_up, w_down, g_mlp_post, loss_target, m_g_mix_pre, m_w_in, m_b_in, m_w_pool, m_pool_scale, m_attn_sinks, m_w_branch_pool, m_w_branch_attn, m_w_out, m_g_mix_post, m_g_mlp_pre, m_w_up, m_w_down, m_g_mlp_post, v_g_mix_pre, v_w_in, v_b_in, v_w_pool, v_pool_scale, v_attn_sinks, v_w_branch_pool, v_w_branch_attn, v_w_out, v_g_mix_post, v_g_mlp_pre, v_w_up, v_w_down, v_g_mlp_post):
    weights = dict(g_mix_pre=g_mix_pre, w_in=w_in, b_in=b_in, w_pool=w_pool, pool_scale=pool_scale,
                   attn_sinks=attn_sinks, w_branch_pool=w_branch_pool, w_branch_attn=w_branch_attn, w_out=w_out,
                   g_mix_post=g_mix_post, g_mlp_pre=g_mlp_pre, w_up=w_up, w_down=w_down, g_mlp_post=g_mlp_post)
    mom1 = dict(g_mix_pre=m_g_mix_pre, w_in=m_w_in, b_in=m_b_in, w_pool=m_w_pool, pool_scale=m_pool_scale,
                attn_sinks=m_attn_sinks, w_branch_pool=m_w_branch_pool, w_branch_attn=m_w_branch_attn,
                w_out=m_w_out, g_mix_post=m_g_mix_post, g_mlp_pre=m_g_mlp_pre, w_up=m_w_up, w_down=m_w_down,
                g_mlp_post=m_g_mlp_post)
    mom2 = dict(g_mix_pre=v_g_mix_pre, w_in=v_w_in, b_in=v_b_in, w_pool=v_w_pool, pool_scale=v_pool_scale,
                attn_sinks=v_attn_sinks, w_branch_pool=v_w_branch_pool, w_branch_attn=v_w_branch_attn,
                w_out=v_w_out, g_mix_post=v_g_mix_post, g_mlp_pre=v_g_mlp_pre, w_up=v_w_up, w_down=v_w_down,
                g_mlp_post=v_g_mlp_post)
    b_loc, seq, _ = x.shape
    x2 = x.reshape(b_loc * seq, D_MODEL)
    tgt = loss_target.reshape(b_loc * seq, D_MODEL)
    ids = jnp.stack([2 * lax.axis_index("x") + lax.axis_index("y"), lax.axis_index("c")]).astype(jnp.int32)

    def flat(n, a):
        return a[0].T if n == "w_in" else a[0]

    def unflat(n, a):
        return (a.T if n == "w_in" else a)[None]

    shards = {n: flat(n, weights[n]).astype(BF16) for n in _BIG}
    small = {n: weights[n] for n in _ORDER if n not in _BIG}
    (gearly, gmat, glate), gx, grads = _step(x2, tgt, seq, shards, small, ids)

    def two_d(src):
        return {n: src[n].reshape(4 * POOL_GC, POOL_GC) if n == "w_pool" else src[n] for n in _SMALL_NAMES}

    loss, per = _small_update(gearly, gmat, glate, two_d(weights), two_d(mom1), two_d(mom2))
    delta, new_m, new_v = {}, {}, {}
    for n in _SMALL_NAMES:
        grads[n], delta[n], new_m[n], new_v[n] = (a.reshape(weights[n].shape) for a in per[n])
    for n in _BIG:
        d, nm, nv = _adamw(flat(n, weights[n]), grads[n], flat(n, mom1[n]), flat(n, mom2[n]))
        grads[n] = unflat(n, grads[n])
        delta[n], new_m[n], new_v[n] = unflat(n, d), unflat(n, nm), unflat(n, nv)

    return (loss[0, 0], gx.reshape(x.shape), *[grads[n] for n in _ORDER], *[delta[n] for n in _ORDER],
            *[new_m[n] for n in _ORDER], *[new_v[n] for n in _ORDER])
```

```python
import jax
import jax.numpy as jnp
from jax import lax
from jax.experimental import pallas as pl
from jax.experimental.pallas import tpu as pltpu
from jax.experimental.pallas import tpu_sc as plsc

F32 = jnp.float32
BF16 = jnp.bfloat16

D_MODEL = 1024
POOL_WINDOWS = (2, 4, 8, 16)
POOL_WIDTH = 512
POOL_GC = 128
HALO = 16
HEAD_DIM = 64
N_Q_HEADS = 8
ATTN_WIDTH = 512
KV_WIDTH = 128
BLOCK = 128
NEG_INF = -1e30
ROPE_THETA = 500000.0
ROT_DIM = 16
GATE_WIDTH = 2048
IN_WIDTH = 3328
D_FF = 4096
EPS = 1e-6
SCALE = HEAD_DIM ** -0.5
C_Q, C_K, C_V, C_G = 512, 1024, 1152, 1280

ADAM_LR, ADAM_B1, ADAM_B2, ADAM_EPS, ADAM_WD, ADAM_STEP = 0.001, 0.9, 0.999, 1e-08, 0.01, 10

N_CHIPS = 4
N_DEV = 8
LANES = 128
TM = 512
TP = 512
VMEM_MB = 56

MESH = pl.DeviceIdType.MESH
ANY = pl.BlockSpec(memory_space=pl.ANY)


def _cp(*sem, vmem=VMEM_MB):
    return pltpu.CompilerParams(dimension_semantics=sem, vmem_limit_bytes=vmem * 1024 * 1024)


def _rows(tile, cols):
    return pl.BlockSpec((tile, cols), lambda i: (i, 0))


def _const(shape):
    nd = len(shape)
    return pl.BlockSpec(shape, lambda i: (0,) * nd)


def _sds(shape, dtype):
    return jax.ShapeDtypeStruct(shape, dtype)


def _dot(a, b):
    return jnp.dot(a, b, preferred_element_type=F32)


def _dot_nt(a, b):
    return lax.dot_general(a, b, (((1,), (1,)), ((), ())), preferred_element_type=F32)


def _dot_tn(a, b):
    return lax.dot_general(a, b, (((0,), (0,)), ((), ())), preferred_element_type=F32)


def _rms(x):
    return lax.rsqrt(jnp.mean(x * x, axis=-1, keepdims=True) + EPS)


def _norm_bwd(x, g, dout):
    r = _rms(x)
    n = x * r
    dn = dout * g
    dx = r * (dn - n * jnp.mean(dn * n, axis=-1, keepdims=True))
    return dx, jnp.sum(dout * n, axis=0, keepdims=True)


def _rot_fwd(t, c, a, bt):
    return t * c + pltpu.roll(t, LANES - 8, 1) * a + pltpu.roll(t, 8, 1) * bt


def _rot_bwd(d, c, a, bt):
    return d * c + pltpu.roll(d * a, 8, 1) + pltpu.roll(d * bt, LANES - 8, 1)


def _rope_tables(seq):
    pos = jnp.arange(seq, dtype=F32)
    inv_freq = ROPE_THETA ** (-jnp.arange(0, ROT_DIM, 2, dtype=F32) / ROT_DIM)
    ang = pos[:, None] * inv_freq[None, :]
    cos, sin = jnp.cos(ang), jnp.sin(ang)
    ones = jnp.ones((seq, HEAD_DIM - ROT_DIM), F32)
    zeros8 = jnp.zeros((seq, 8), F32)
    zrest = jnp.zeros((seq, HEAD_DIM - ROT_DIM), F32)
    c = jnp.concatenate([cos, cos, ones], axis=1)
    a = jnp.concatenate([-sin, zeros8, zrest], axis=1)
    bt = jnp.concatenate([zeros8, sin, zrest], axis=1)
    return tuple(jnp.tile(t, (1, 2)) for t in (c, a, bt))


class _Exchange:
    def __init__(self, inputs, out_shapes, sems, start, finish, aliases=None, middle=None):
        self.inputs, self.out_shapes, self.sems = list(inputs), list(out_shapes), list(sems)
        self.start, self.finish, self.aliases = start, finish, dict(aliases or {})
        self.middle = middle


def _call(body, *, name, grid, in_specs, out_specs, out_shape, args, scratch=(), sem=(), exchanges=()):
    in_specs, out_specs, out_shape, scratch = list(in_specs), list(out_specs), list(out_shape), list(scratch)
    if not exchanges:
        return pl.pallas_call(body, name=name, grid=grid, in_specs=in_specs, out_specs=out_specs,
                              out_shape=out_shape, scratch_shapes=scratch, compiler_params=_cp(*sem))(*args)
    n_in, n_out, n_scr = len(in_specs), len(out_specs), len(scratch)
    x_in = [a for ex in exchanges for a in ex.inputs]
    x_out = [s for ex in exchanges for s in ex.out_shapes]
    x_sem = [s for ex in exchanges for s in ex.sems]
    aliases, i_off, o_off = {}, n_in, n_out
    for ex in exchanges:
        for i, o in ex.aliases.items():
            aliases[i_off + i] = o_off + o
        i_off += len(ex.inputs)
        o_off += len(ex.out_shapes)

    def split(flat):
        out, pos = [], 0
        for ex, n in zip(exchanges, flat[1]):
            out.append(flat[0][pos:pos + n])
            pos += n
        return out

    def carrier(*refs):
        pos = 0
        groups = []
        for n in (n_in, len(x_in), n_out, len(x_out), n_scr, len(x_sem)):
            groups.append(refs[pos:pos + n])
            pos += n
        ins, xin, outs, xout, scr, xsem = groups
        xin = split((xin, [len(ex.inputs) for ex in exchanges]))
        xout = split((xout, [len(ex.out_shapes) for ex in exchanges]))
        xsem = split((xsem, [len(ex.sems) for ex in exchanges]))
        first = pl.program_id(0) == 0
        last = pl.program_id(0) == grid[0] - 1
        for d in range(1, len(grid)):
            first = jnp.logical_and(first, pl.program_id(d) == 0)
            last = jnp.logical_and(last, pl.program_id(d) == grid[d] - 1)

        @pl.when(first)
        def _():
            for ex, i, o, s in zip(exchanges, xin, xout, xsem):
                ex.start(i, o, s)

        if any(ex.middle for ex in exchanges):
            half = pl.program_id(0) == grid[0] // 2
            for d in range(1, len(grid)):
                half = jnp.logical_and(half, pl.program_id(d) == 0)

            @pl.when(half)
            def _():
                for ex, i, o, s in zip(exchanges, xin, xout, xsem):
                    if ex.middle:
                        ex.middle(i, o, s)

        body(*ins, *outs, *scr)

        @pl.when(last)
        def _():
            for ex, i, o, s in zip(exchanges, xin, xout, xsem):
                ex.finish(i, o, s)

    res = pl.pallas_call(
        carrier, name=name, grid=grid, in_specs=in_specs + [ANY] * len(x_in),
        out_specs=out_specs + [ANY] * len(x_out), out_shape=out_shape + x_out,
        scratch_shapes=scratch + x_sem, input_output_aliases=aliases,
        compiler_params=_cp(*(["arbitrary"] * len(grid))),
    )(*args, *x_in)
    return res[:n_out], split((res[n_out:], [len(ex.out_shapes) for ex in exchanges]))


def _alone(name, *exchanges):
    n_in = [len(ex.inputs) for ex in exchanges]
    n_out = [len(ex.out_shapes) for ex in exchanges]
    n_sem = [len(ex.sems) for ex in exchanges]
    aliases, i_off, o_off = {}, 0, 0
    for ex in exchanges:
        for i, o in ex.aliases.items():
            aliases[i_off + i] = o_off + o
        i_off += len(ex.inputs)
        o_off += len(ex.out_shapes)

    def split(flat, counts):
        out, pos = [], 0
        for n in counts:
            out.append(flat[pos:pos + n])
            pos += n
        return out

    def body(*refs):
        ins, outs, sems = split(refs, [sum(n_in), sum(n_out), sum(n_sem)])
        groups = list(zip(exchanges, split(ins, n_in), split(outs, n_out), split(sems, n_sem)))
        for ex, i, o, s in groups:
            ex.start(i, o, s)
        for ex, i, o, s in groups:
            if ex.middle:
                ex.middle(i, o, s)
        for ex, i, o, s in groups:
            ex.finish(i, o, s)

    res = pl.pallas_call(
        body, name=name, in_specs=[ANY] * sum(n_in), out_specs=[ANY] * sum(n_out),
        out_shape=[s for ex in exchanges for s in ex.out_shapes],
        scratch_shapes=[s for ex in exchanges for s in ex.sems], input_output_aliases=aliases,
    )(*[a for ex in exchanges for a in ex.inputs])
    return split(res, n_out)


def _place():
    x, y, c = lax.axis_index("x"), lax.axis_index("y"), lax.axis_index("c")
    chips = [(1 - x, y), (x, 1 - y), (1 - x, 1 - y)]
    return x, y, c, chips


def _remote(src, dst, send, recv, to):
    return pltpu.make_async_remote_copy(src_ref=src, dst_ref=dst, send_sem=send, recv_sem=recv,
                                        device_id=to, device_id_type=MESH)


def _ex_gather(shards):
    nw = len(shards)
    hrs = [s.shape[0] // 2 for s in shards]

    def copies(ins, outs, sems):
        s1, r1, s2, r2, fs, fr = sems
        x, y, c, _ = _place()
        me, xn, yn, dg = (x, y), (1 - x, y), (x, 1 - y), (1 - x, 1 - y)
        nbr = (xn, yn)
        sibling = (x, y, 1 - c)

        def piece(w, chip, core, part=None):
            hr = hrs[w]
            rows = pl.ds(core * hr, hr) if part is None else pl.ds(core * hr + part * (hr // 2), hr // 2)
            return outs[w].at[2 * chip[0] + chip[1], rows]

        def first(w, k):
            return _remote(ins[w].at[pl.ds(c * hrs[w], hrs[w])], piece(w, me, c), s1.at[w, k], r1.at[w, k],
                           (*nbr[k], c))

        def landed(w, k):
            return _remote(piece(w, nbr[k], c), piece(w, nbr[k], c), s1.at[w, k], r1.at[w, k], (*nbr[k], c))

        def onward(w, k):
            return _remote(piece(w, nbr[k], c, k), piece(w, nbr[k], c, k), s2.at[w, k], r2.at[w, k],
                           (*nbr[1 - k], c))

        def arrived(w, k):
            return _remote(piece(w, dg, c, k), piece(w, dg, c, k), s2.at[w, k], r2.at[w, k], (*nbr[1 - k], c))

        def passed(w, j):
            chip = (xn, yn, dg)[j]
            return _remote(piece(w, chip, c), piece(w, chip, c), fs.at[w, j], fr.at[w, j], sibling)

        def handed(w, j):
            chip = (xn, yn, dg)[j]
            return _remote(piece(w, chip, 1 - c), piece(w, chip, 1 - c), fs.at[w, j], fr.at[w, j], sibling)

        return first, landed, onward, arrived, passed, handed

    def start(ins, outs, sems):
        first = copies(ins, outs, sems)[0]
        for w in range(nw):
            for k in range(2):
                first(w, k).start()

    def middle(ins, outs, sems):
        _, landed, onward, _, passed, _ = copies(ins, outs, sems)
        for w in range(nw):
            for k in range(2):
                landed(w, k).wait_recv()
                onward(w, k).start()
                passed(w, k).start()

    def finish(ins, outs, sems):
        first, _, onward, arrived, passed, handed = copies(ins, outs, sems)
        for w in range(nw):
            for k in range(2):
                arrived(w, k).wait_recv()
            passed(w, 2).start()
        for w in range(nw):
            for j in range(3):
                handed(w, j).wait_recv()
        for w in range(nw):
            for k in range(2):
                first(w, k).wait_send()
                onward(w, k).wait_send()
            for j in range(3):
                passed(w, j).wait_send()

    return _Exchange(shards, [_sds((N_CHIPS,) + s.shape, s.dtype) for s in shards],
                     [pltpu.SemaphoreType.DMA((nw, 2))] * 4 + [pltpu.SemaphoreType.DMA((nw, 3))] * 2,
                     start, finish, middle=middle)


def _ex_pair(grads):
    nw = len(grads)

    def copies(ins, outs, sems):
        x, y, c, _ = _place()
        out = []
        for w in range(nw):
            hr = grads[w].shape[1] // 2
            out.append(_remote(ins[w].at[:, pl.ds((1 - c) * hr, hr)], outs[w], sems[0].at[w], sems[1].at[w],
                               (x, y, 1 - c)))
        return out

    def start(ins, outs, sems):
        for cp in copies(ins, outs, sems):
            cp.start()

    def finish(ins, outs, sems):
        for cp in copies(ins, outs, sems):
            cp.wait()

    return _Exchange(grads, [_sds((N_CHIPS, g.shape[1] // 2, g.shape[2]), F32) for g in grads],
                     [pltpu.SemaphoreType.DMA((nw,))] * 2, start, finish)


def _ex_chip(pieces):
    nw = len(pieces)

    def copies(ins, outs, sems):
        x, y, c, chips = _place()
        return [_remote(ins[w].at[2 * cx + cy], outs[w].at[k], sems[0].at[w, k], sems[1].at[w, k], (cx, cy, c))
                for w in range(nw) for k, (cx, cy) in enumerate(chips)]

    def start(ins, outs, sems):
        for cp in copies(ins, outs, sems):
            cp.start()

    def finish(ins, outs, sems):
        for cp in copies(ins, outs, sems):
            cp.wait()

    return _Exchange(pieces, [_sds((3,) + p.shape[1:], BF16) for p in pieces],
                     [pltpu.SemaphoreType.DMA((nw, 3))] * 2, start, finish)


def _ex_swap(fulls):
    nw = len(fulls)

    def start(ins, outs, sems):
        x, y, c, _ = _place()
        for w in range(nw):
            hr = fulls[w].shape[0] // 2
            mine = pl.ds(c * hr, hr)
            _remote(ins[w].at[mine], outs[w].at[mine], sems[0].at[w], sems[1].at[w], (x, y, 1 - c)).start()

    def finish(ins, outs, sems):
        x, y, c, _ = _place()
        for w in range(nw):
            hr = fulls[w].shape[0] // 2
            mine, theirs = pl.ds(c * hr, hr), pl.ds((1 - c) * hr, hr)
            _remote(ins[w].at[mine], outs[w].at[mine], sems[0].at[w], sems[1].at[w], (x, y, 1 - c)).wait_send()
            _remote(ins[w].at[theirs], outs[w].at[theirs], sems[0].at[w], sems[1].at[w], (x, y, 1 - c)).wait_recv()

    return _Exchange(fulls, [_sds(f.shape, F32) for f in fulls], [pltpu.SemaphoreType.DMA((nw,))] * 2,
                     start, finish, aliases={w: w for w in range(nw)})


def _ex_allgather(blocks):
    nb = len(blocks)

    def copies(ins, outs, sems):
        send, recv, lsem = sems
        x, y, c, chips = _place()
        me, sibling = (x, y, c), (x, y, 1 - c)

        def rows(b, px, py, pc):
            m_per = blocks[b].shape[0]
            return outs[b].at[pl.ds((4 * px + 2 * py + pc) * m_per, m_per), :]

        def copy(b, k, blk, to, src=None):
            return _remote(rows(b, *blk) if src is None else src, rows(b, *blk), send.at[b, k], recv.at[b, k], to)

        def mine(b):
            return pltpu.make_async_copy(ins[b], rows(b, *me), lsem.at[b])

        def first(b, k):
            return copy(b, k, me, sibling if k == 0 else (*chips[k - 1], c), src=ins[b])

        def passed(b, j):
            return copy(b, 4 + j, (*chips[j], c), sibling)

        def landed(b, j):
            return copy(b, 1 + j, (*chips[j], c), me)

        def handed(b, k):
            return copy(b, 0, sibling, me) if k == 0 else copy(b, 3 + k, (*chips[k - 1], 1 - c), me)

        return mine, first, passed, landed, handed

    def start(ins, outs, sems):
        mine, first, _, _, _ = copies(ins, outs, sems)
        for b in range(nb):
            mine(b).start()
            for k in range(4):
                first(b, k).start()

    def finish(ins, outs, sems):
        mine, first, passed, landed, handed = copies(ins, outs, sems)
        sent = []
        for b in range(nb):
            for j in range(3):
                landed(b, j).wait_recv()
                cp = passed(b, j)
                cp.start()
                sent.append(cp)
        for b in range(nb):
            for k in range(4):
                handed(b, k).wait_recv()
            for k in range(4):
                first(b, k).wait_send()
        for cp in sent:
            cp.wait_send()
        for b in range(nb):
            mine(b).wait()

    return _Exchange(blocks, [_sds((N_DEV * b.shape[0], b.shape[1]), F32) for b in blocks],
                     [pltpu.SemaphoreType.DMA((nb, 7)), pltpu.SemaphoreType.DMA((nb, 7)), pltpu.SemaphoreType.DMA((nb,))],
                     start, finish)


def _inproj(x2, g1, w_in_t, b_in, tabs, seq, exchanges=()):
    T = x2.shape[0]
    tm = min(TM, seq)
    nseq = seq // tm

    def body(x_ref, g_ref, w_ref, b_ref, c_ref, a_ref, bt_ref, h_ref, u_ref, q_ref, k_ref, v_ref, gate_ref):
        x = x_ref[...]
        h = (x * _rms(x) * g_ref[...]).astype(BF16)
        h_ref[...] = h

        def proj(lo, hi):
            return _dot_nt(h, w_ref[lo:hi, :]) + b_ref[:, lo:hi]

        c, a, bt = c_ref[...], a_ref[...], bt_ref[...]
        u_ref[...] = proj(0, C_Q)
        q = proj(C_Q, C_K)
        for p in range(4):
            sl = slice(LANES * p, LANES * (p + 1))
            q_ref[:, sl] = (_rot_fwd(q[:, sl], c, a, bt) * SCALE).astype(BF16)
        kv = proj(C_K, C_G)
        k_ref[...] = _rot_fwd(kv[:, :KV_WIDTH], c, a, bt).astype(BF16)
        v_ref[...] = kv[:, KV_WIDTH:].astype(BF16)
        for j in range(2):
            lo = C_G + D_MODEL * j
            gate_ref[:, D_MODEL * j:D_MODEL * (j + 1)] = jax.nn.sigmoid(proj(lo, lo + D_MODEL)).astype(BF16)

    tab = pl.BlockSpec((tm, LANES), lambda i: (i % nseq, 0))
    return _call(
        body, name="inproj", grid=(T // tm,),
        in_specs=[_rows(tm, D_MODEL), _const((1, D_MODEL)), _const((IN_WIDTH, D_MODEL)), _const((1, IN_WIDTH)),
                  tab, tab, tab],
        out_specs=[_rows(tm, D_MODEL), _rows(tm, POOL_WIDTH), _rows(tm, ATTN_WIDTH), _rows(tm, KV_WIDTH),
                   _rows(tm, KV_WIDTH), _rows(tm, GATE_WIDTH)],
        out_shape=[_sds((T, D_MODEL), BF16), _sds((T, POOL_WIDTH), F32), _sds((T, ATTN_WIDTH), BF16),
                   _sds((T, KV_WIDTH), BF16), _sds((T, KV_WIDTH), BF16), _sds((T, GATE_WIDTH), BF16)],
        args=(x2, g1, w_in_t, b_in, *tabs), sem=("parallel",), exchanges=exchanges)


def _inv_count(pos, w):
    return 1.0 / jnp.minimum(pos + 1, w).astype(F32)


def _pool_fwd(u, w_pool, pool_scale, seq):
    T = u.shape[0]
    tp = min(TP, seq)
    nseq = seq // tp
    per = tp // HALO

    def body(u_ref, prev_ref, w_ref, s_ref, diff_ref, y_ref):
        i = pl.program_id(0)
        first = (i % nseq) == 0
        prev = jnp.where(first, 0.0, prev_ref[...])
        ext = jnp.concatenate([prev, u_ref[...]], axis=0)
        pos = (i % nseq) * tp + lax.broadcasted_iota(jnp.int32, (tp, 1), 0)
        for gi, w in enumerate(POOL_WINDOWS):
            sl = slice(POOL_GC * gi, POOL_GC * (gi + 1))
            xg = ext[:, sl]
            s = xg
            sh = 1
            while sh < w:
                s = s + pltpu.roll(s, sh, 0)
                sh *= 2
            pooled = s[HALO:] * _inv_count(pos, w)
            diff = (pooled - xg[HALO:]).astype(BF16)
            diff_ref[:, sl] = diff
            mixed = _dot(diff, w_ref[gi].astype(BF16))
            y_ref[:, sl] = (mixed * s_ref[:, sl]).astype(BF16)

    return _call(
        body, name="pool_fwd", grid=(T // tp,),
        in_specs=[_rows(tp, POOL_WIDTH),
                  pl.BlockSpec((HALO, POOL_WIDTH), lambda i: (jnp.maximum(i * per - 1, 0), 0)),
                  _const((4, POOL_GC, POOL_GC)), _const((1, POOL_WIDTH))],
        out_specs=[_rows(tp, POOL_WIDTH), _rows(tp, POOL_WIDTH)],
        out_shape=[_sds((T, POOL_WIDTH), BF16), _sds((T, POOL_WIDTH), BF16)],
        args=(u, u, w_pool, pool_scale), sem=("parallel",))


GROUP = 4
GROWS = GROUP * BLOCK


def _attn_masks(n):
    qi = lax.broadcasted_iota(jnp.int32, (GROWS, 2 * BLOCK), 0) % BLOCK
    kj = lax.broadcasted_iota(jnp.int32, (GROWS, 2 * BLOCK), 1)
    rel = qi + BLOCK - kj
    valid = (rel >= 0) & (rel < BLOCK) & (kj >= jnp.where(n > 0, 0, BLOCK))
    lo = lax.broadcasted_iota(jnp.int32, (BLOCK, LANES), 1) < HEAD_DIM
    return valid, lo


def _by_example(bl, *arrays):
    return [a.reshape(bl, a.shape[0] // bl, a.shape[1]) for a in arrays]


def _stack_heads(ref, h, lo):
    keep = lo if h == 0 else jnp.logical_not(lo)
    pieces = []
    for p in (2 * h, 2 * h + 1):
        xp = ref[:, LANES * p:LANES * (p + 1)].astype(F32)
        for e in range(2):
            t = xp if e == h else pltpu.roll(xp, HEAD_DIM, 1)
            pieces.append(jnp.where(keep, t, 0.0).astype(BF16))
    return jnp.concatenate(pieces, axis=0)


def _unstack_heads(stacked, h, lo):
    pairs = []
    for j in range(2):
        parts = []
        for e in range(2):
            t = stacked[BLOCK * (2 * j + e):BLOCK * (2 * j + e + 1)]
            parts.append(t if e == h else pltpu.roll(t, HEAD_DIM, 1))
        pairs.append(jnp.where(lo, parts[0], parts[1]))
    return pairs


def _sink_rows(sink_ref, h):
    head = lax.broadcasted_iota(jnp.int32, (GROWS, 1), 0) // BLOCK
    col = jnp.zeros((GROWS, 1), F32) + sink_ref[GROUP * h]
    for g in range(1, GROUP):
        col = jnp.where(head == g, sink_ref[GROUP * h + g], col)
    return col


def _group_probs(qs, kk, valid, sink):
    s = jnp.where(valid, _dot_nt(qs, kk), NEG_INF)
    m = jnp.maximum(jnp.max(s, axis=1, keepdims=True), sink)
    ex = jnp.exp(s - m)
    es = jnp.exp(sink - m)
    inv = 1.0 / (jnp.sum(ex, axis=1, keepdims=True) + es)
    return ex * inv, es * inv


def _attn_fwd(q, k, v, sinks, seq, exchanges=()):
    T = q.shape[0]
    nb = seq // BLOCK
    bl = T // seq

    def body(sink_ref, q_ref, kp_ref, kc_ref, vp_ref, vc_ref, o_ref):
        valid, lo = _attn_masks(pl.program_id(0))
        for b in range(bl):
            kk = jnp.concatenate([kp_ref[b], kc_ref[b]], axis=0)
            vv = jnp.concatenate([vp_ref[b], vc_ref[b]], axis=0)
            for h in range(2):
                qs = _stack_heads(q_ref.at[b], h, lo)
                pr, _ = _group_probs(qs, kk, valid, _sink_rows(sink_ref, h))
                o = _dot(pr.astype(BF16), vv)
                for j, pair in enumerate(_unstack_heads(o, h, lo)):
                    p = 2 * h + j
                    o_ref[b, :, LANES * p:LANES * (p + 1)] = pair.astype(BF16)

    cur = lambda n: (0, n, 0)
    prv = lambda n: (0, jnp.maximum(n - 1, 0), 0)
    kv = lambda m: pl.BlockSpec((bl, BLOCK, KV_WIDTH), m)
    res = _call(
        body, name="attn_fwd", grid=(nb,),
        in_specs=[pl.BlockSpec(memory_space=pltpu.SMEM), pl.BlockSpec((bl, BLOCK, ATTN_WIDTH), cur),
                  kv(prv), kv(cur), kv(prv), kv(cur)],
        out_specs=[pl.BlockSpec((bl, BLOCK, ATTN_WIDTH), cur)],
        out_shape=[_sds((bl, seq, ATTN_WIDTH), BF16)],
        args=(sinks, *_by_example(bl, q, k, k, v, v)), sem=("parallel",), exchanges=exchanges)
    if exchanges:
        return [res[0][0].reshape(T, ATTN_WIDTH)], res[1]
    return [res[0].reshape(T, ATTN_WIDTH)]


def _branch(y, w_ref):
    return jnp.concatenate([_dot(y, w_ref[j]) for j in range(N_CHIPS)], axis=1)


def _merge_out(y_pool, y_attn, gate, x2, w_bp, w_ba, w_out, g2, g3, exchanges=()):
    T = x2.shape[0]
    tm = min(TM, T)

    def body(yp_ref, ya_ref, gate_ref, x_ref, wbp_ref, wba_ref, wo_ref, g2_ref, g3_ref,
             mg_ref, mix_ref, x1_ref, h2_ref):
        bp, ba = _branch(yp_ref[...], wbp_ref), _branch(ya_ref[...], wba_ref)
        merged = (gate_ref[:, :D_MODEL].astype(F32) * bp + gate_ref[:, D_MODEL:].astype(F32) * ba).astype(BF16)
        mg_ref[...] = merged
        mix = _dot(merged, wo_ref[...])
        mix_ref[...] = mix
        x1 = x_ref[...] + mix * _rms(mix) * g2_ref[...]
        x1_ref[...] = x1
        h2_ref[...] = (x1 * _rms(x1) * g3_ref[...]).astype(BF16)

    return _call(
        body, name="merge_out", grid=(T // tm,),
        in_specs=[_rows(tm, POOL_WIDTH), _rows(tm, ATTN_WIDTH), _rows(tm, GATE_WIDTH), _rows(tm, D_MODEL),
                  _const(w_bp.shape), _const(w_ba.shape), _const((D_MODEL, D_MODEL)),
                  _const((1, D_MODEL)), _const((1, D_MODEL))],
        out_specs=[_rows(tm, D_MODEL)] * 4,
        out_shape=[_sds((T, D_MODEL), BF16), _sds((T, D_MODEL), F32), _sds((T, D_MODEL), F32),
                   _sds((T, D_MODEL), BF16)],
        args=(y_pool, y_attn, gate, x2, w_bp, w_ba, w_out, g2, g3), sem=("parallel",), exchanges=exchanges)


HALF = D_MODEL // 2
TM_MLP = 256


def _mlp_core(h2, x1, mix, tgt, w_up, w_down, g4, g3, g2):
    T = h2.shape[0]
    tm = min(TM_MLP, T)

    def body(h_ref, x1_ref, mix_ref, t_ref, g_ref, g3_ref, g2_ref, ua_hbm, ub_hbm, da_hbm, db_hbm,
             act_ref, dff_ref, dup_ref, dx1_ref, dmix_ref, loss_ref, dg_ref, dg3_ref, dg2_ref,
             wu, wd, relu_scr, sems):
        def weight_copy(i):
            src, dst = ((ua_hbm, wu.at[:, :HALF]), (ub_hbm, wu.at[:, HALF:]),
                        (da_hbm, wd.at[:, :HALF]), (db_hbm, wd.at[:, HALF:]))[i]
            return pltpu.make_async_copy(src, dst, sems.at[i])

        @pl.when(pl.program_id(0) == 0)
        def _():
            for i in range(4):
                weight_copy(i).start()
            loss_ref[...] = jnp.zeros_like(loss_ref)
            for ref in (dg_ref, dg3_ref, dg2_ref):
                ref[...] = jnp.zeros_like(ref)
            weight_copy(0).wait()
            weight_copy(1).wait()

        h = h_ref[...]
        ff = None
        for j in range(N_CHIPS):
            lo = D_MODEL * j
            relu = jnp.maximum(_dot(h, wu[j]), 0.0)
            if j == 0:
                @pl.when(pl.program_id(0) == 0)
                def _():
                    weight_copy(2).wait()
                    weight_copy(3).wait()
            relu_scr[:, lo:lo + D_MODEL] = relu
            act = jnp.square(relu).astype(BF16)
            act_ref[:, lo:lo + D_MODEL] = act
            t = _dot(act, wd[j])
            ff = t if ff is None else ff + t
        g = g_ref[...]
        x1 = x1_ref[...]
        err = x1 + ff * _rms(ff) * g - t_ref[...]
        loss_ref[...] += jnp.sum(err * err) * (0.5 / D_MODEL)
        dy = err * (1.0 / D_MODEL)
        dff, dg = _norm_bwd(ff, g, dy)
        dg_ref[...] += dg
        dff = dff.astype(BF16)
        dff_ref[...] = dff
        dh2 = None
        for j in range(N_CHIPS):
            lo = D_MODEL * j
            dup = (_dot_nt(dff, wd[j]) * (2.0 * relu_scr[:, lo:lo + D_MODEL])).astype(BF16)
            dup_ref[:, lo:lo + D_MODEL] = dup
            t = _dot_nt(dup, wu[j])
            dh2 = t if dh2 is None else dh2 + t
        dx, dg3 = _norm_bwd(x1, g3_ref[...], dh2)
        dx1 = dy + dx
        dx1_ref[...] = dx1
        dg3_ref[...] += dg3
        dmix, dg2 = _norm_bwd(mix_ref[...], g2_ref[...], dx1)
        dmix_ref[...] = dmix.astype(BF16)
        dg2_ref[...] += dg2

    slabs = pltpu.VMEM((N_CHIPS, D_MODEL, D_MODEL), BF16)
    gain = _const((1, D_MODEL))
    return pl.pallas_call(
        body, name="mlp_core", grid=(T // tm,),
        in_specs=[_rows(tm, D_MODEL)] * 4 + [gain] * 3 + [ANY] * 4,
        out_specs=[_rows(tm, D_FF), _rows(tm, D_MODEL), _rows(tm, D_FF), _rows(tm, D_MODEL), _rows(tm, D_MODEL),
                   _const((8, LANES)), gain, gain, gain],
        out_shape=[_sds((T, D_FF), BF16), _sds((T, D_MODEL), BF16), _sds((T, D_FF), BF16), _sds((T, D_MODEL), F32),
                   _sds((T, D_MODEL), BF16), _sds((8, LANES), F32)] + [_sds((1, D_MODEL), F32)] * 3,
        scratch_shapes=[slabs] * 2 + [pltpu.VMEM((tm, D_FF), F32), pltpu.SemaphoreType.DMA((4,))],
        compiler_params=_cp("arbitrary"),
    )(h2, x1, mix, tgt, g4, g3, g2, *w_up, *w_down)


def _dw(tag, a, g, ta, tn, shard_cols=False, exchanges=()):
    T, ka = a.shape
    n = g.shape[1]
    tk = min(2 * TM, T)
    nk = T // tk

    def body(a_ref, g_ref, o_ref):
        @pl.when(pl.program_id(2) == 0)
        def _():
            o_ref[...] = jnp.zeros_like(o_ref)

        o_ref[...] += _dot_tn(a_ref[...], g_ref[...])

    if shard_cols:
        per = (n // N_CHIPS) // tn
        out_spec = pl.BlockSpec((None, ta, tn), lambda i, j, k: (j // per, i, j % per))
        out_shape = _sds((N_CHIPS, ka, n // N_CHIPS), F32)
    else:
        out_spec = pl.BlockSpec((ta, tn), lambda i, j, k: (i, j))
        out_shape = _sds((ka, n), F32)
    return _call(
        body, name="dw_" + tag, grid=(ka // ta, n // tn, nk),
        in_specs=[pl.BlockSpec((tk, ta), lambda i, j, k: (k, i)), pl.BlockSpec((tk, tn), lambda i, j, k: (k, j))],
        out_specs=[out_spec], out_shape=[out_shape],
        args=(a, g), sem=("parallel", "parallel", "arbitrary"), exchanges=exchanges)


def _dw_mix(merged, dmix, y_pool, dbp, y_attn, dba, exchanges=()):
    T = merged.shape[0]
    tk = min(2 * TM, T)
    c = D_MODEL // N_CHIPS

    def body(mg_ref, dmix_ref, yp_ref, dbp_ref, ya_ref, dba_ref, out_ref, bp_ref, ba_ref):
        @pl.when(pl.program_id(0) == 0)
        def _():
            for ref in (out_ref, bp_ref, ba_ref):
                ref[...] = jnp.zeros_like(ref)

        out_ref[...] += _dot_tn(mg_ref[...], dmix_ref[...])
        for y_ref, d_ref, o_ref in ((yp_ref, dbp_ref, bp_ref), (ya_ref, dba_ref, ba_ref)):
            res = _dot_tn(y_ref[...], d_ref[...])
            for j in range(N_CHIPS):
                o_ref[j] += res[:, c * j:c * (j + 1)]

    slabs = (N_CHIPS, POOL_WIDTH, c)
    return _call(
        body, name="dw_mix", grid=(T // tk,),
        in_specs=[_rows(tk, D_MODEL), _rows(tk, D_MODEL), _rows(tk, POOL_WIDTH), _rows(tk, D_MODEL),
                  _rows(tk, ATTN_WIDTH), _rows(tk, D_MODEL)],
        out_specs=[_const((D_MODEL, D_MODEL)), _const(slabs), _const(slabs)],
        out_shape=[_sds((D_MODEL, D_MODEL), F32), _sds(slabs, F32), _sds(slabs, F32)],
        args=(merged, dmix, y_pool, dbp, y_attn, dba), sem=("arbitrary",), exchanges=exchanges)


def _merge_bwd(dmix, gate, y_pool, y_attn, w_out, w_bp, w_ba, exchanges=()):
    T = dmix.shape[0]
    tm = min(TM, T)

    def body(dmix_ref, gate_ref, yp_ref, ya_ref, wo_ref, wbp_ref, wba_ref,
             dbp_ref, dba_ref, dgate_ref, dyp_ref, dya_ref):
        dm = _dot_nt(dmix_ref[...], wo_ref[...])
        for j, (y_ref, db_ref, w_ref, dy_ref) in enumerate(
                ((yp_ref, dbp_ref, wbp_ref, dyp_ref), (ya_ref, dba_ref, wba_ref, dya_ref))):
            sl = slice(D_MODEL * j, D_MODEL * (j + 1))
            gt = gate_ref[:, sl].astype(F32)
            db = (dm * gt).astype(BF16)
            db_ref[...] = db
            dgate_ref[:, sl] = (dm * _branch(y_ref[...], w_ref) * gt * (1.0 - gt)).astype(BF16)
            cw = D_MODEL // N_CHIPS
            dy = _dot_nt(db[:, :cw], w_ref[0])
            for c in range(1, N_CHIPS):
                dy = dy + _dot_nt(db[:, cw * c:cw * (c + 1)], w_ref[c])
            dy_ref[...] = dy.astype(dy_ref.dtype)

    return _call(
        body, name="merge_bwd", grid=(T // tm,),
        in_specs=[_rows(tm, D_MODEL), _rows(tm, GATE_WIDTH), _rows(tm, POOL_WIDTH), _rows(tm, ATTN_WIDTH),
                  _const((D_MODEL, D_MODEL)), _const(w_bp.shape), _const(w_ba.shape)],
        out_specs=[_rows(tm, D_MODEL), _rows(tm, D_MODEL), _rows(tm, GATE_WIDTH), _rows(tm, POOL_WIDTH),
                   _rows(tm, ATTN_WIDTH)],
        out_shape=[_sds((T, D_MODEL), BF16), _sds((T, D_MODEL), BF16), _sds((T, GATE_WIDTH), BF16),
                   _sds((T, POOL_WIDTH), F32), _sds((T, ATTN_WIDTH), BF16)],
        args=(dmix, gate, y_pool, y_attn, w_out, w_bp, w_ba), sem=("parallel",), exchanges=exchanges)


def _attn_bwd(q, k, v, do, sinks, tabs, seq, exchanges=()):
    T = q.shape[0]
    nb = seq // BLOCK
    bl = T // seq
    steps = nb + 1

    def body(sink_ref, q_ref, do_ref, kp_ref, kc_ref, vp_ref, vc_ref, c_ref, a_ref, bt_ref, cp_ref, ap_ref, btp_ref,
             dq_ref, dk_ref, dv_ref, dsink_ref, ck_ref, cv_ref):
        n = pl.program_id(0)

        @pl.when(n == 0)
        def _():
            dsink_ref[...] = jnp.zeros_like(dsink_ref)
            ck_ref[...] = jnp.zeros_like(ck_ref)
            cv_ref[...] = jnp.zeros_like(cv_ref)

        @pl.when(n < nb)
        def _():
            valid, lo = _attn_masks(n)
            for b in range(bl):
                kk = jnp.concatenate([kp_ref[b], kc_ref[b]], axis=0)
                vv = jnp.concatenate([vp_ref[b], vc_ref[b]], axis=0)
                dk_acc = jnp.zeros((2 * BLOCK, KV_WIDTH), F32)
                dv_acc = jnp.zeros((2 * BLOCK, KV_WIDTH), F32)
                for h in range(2):
                    qs = _stack_heads(q_ref.at[b], h, lo)
                    dos = _stack_heads(do_ref.at[b], h, lo)
                    pr, ps = _group_probs(qs, kk, valid, _sink_rows(sink_ref, h))
                    dp = _dot_nt(dos, vv)
                    delta = jnp.sum(pr * dp, axis=1, keepdims=True)
                    ds = (pr * (dp - delta)).astype(BF16)
                    dsk = ps * delta
                    for g in range(GROUP):
                        idx = GROUP * h + g
                        dsink_ref[idx:idx + 1, :] += (jnp.zeros((1, LANES), F32)
                                                      - jnp.sum(dsk[BLOCK * g:BLOCK * (g + 1)]))
                    dk_acc = dk_acc + _dot_tn(ds, qs)
                    dv_acc = dv_acc + _dot_tn(pr.astype(BF16), dos)
                    for j, pair in enumerate(_unstack_heads(_dot(ds, kk) * SCALE, h, lo)):
                        sl = slice(LANES * (2 * h + j), LANES * (2 * h + j + 1))
                        dq_ref[b, :, sl] = _rot_bwd(pair, c_ref[...], a_ref[...], bt_ref[...]).astype(BF16)
                fin_k = ck_ref[b] + dk_acc[:BLOCK]
                dk_ref[b] = _rot_bwd(fin_k, cp_ref[...], ap_ref[...], btp_ref[...]).astype(BF16)
                dv_ref[b] = (cv_ref[b] + dv_acc[:BLOCK]).astype(BF16)
                ck_ref[b] = dk_acc[BLOCK:]
                cv_ref[b] = dv_acc[BLOCK:]

        @pl.when(n == nb)
        def _():
            for b in range(bl):
                dk_ref[b] = _rot_bwd(ck_ref[b], cp_ref[...], ap_ref[...], btp_ref[...]).astype(BF16)
                dv_ref[b] = cv_ref[b].astype(BF16)

    cur = lambda n: (0, jnp.minimum(n, nb - 1), 0)
    prv = lambda n: (0, jnp.clip(n - 1, 0, nb - 1), 0)
    tcur = lambda n: (jnp.minimum(n, nb - 1), 0)
    tprv = lambda n: (jnp.clip(n - 1, 0, nb - 1), 0)
    wide = lambda m: pl.BlockSpec((bl, BLOCK, ATTN_WIDTH), m)
    kv = lambda m: pl.BlockSpec((bl, BLOCK, KV_WIDTH), m)
    tab = lambda m: pl.BlockSpec((BLOCK, LANES), m)
    res = _call(
        body, name="attn_bwd", grid=(steps,),
        in_specs=[pl.BlockSpec(memory_space=pltpu.SMEM), wide(cur), wide(cur), kv(prv), kv(cur), kv(prv), kv(cur),
                  tab(tcur), tab(tcur), tab(tcur), tab(tprv), tab(tprv), tab(tprv)],
        out_specs=[wide(cur), kv(prv), kv(prv), _const((8, LANES))],
        out_shape=[_sds((bl, seq, ATTN_WIDTH), BF16), _sds((bl, seq, KV_WIDTH), BF16),
                   _sds((bl, seq, KV_WIDTH), BF16), _sds((8, LANES), F32)],
        scratch=[pltpu.VMEM((bl, BLOCK, KV_WIDTH), F32), pltpu.VMEM((bl, BLOCK, KV_WIDTH), F32)],
        args=(sinks, *_by_example(bl, q, do, k, k, v, v), *tabs, *tabs), sem=("arbitrary",), exchanges=exchanges)
    outs, rest = (res if exchanges else (res, None))
    outs = [outs[0].reshape(T, ATTN_WIDTH), outs[1].reshape(T, KV_WIDTH), outs[2].reshape(T, KV_WIDTH), outs[3]]
    return (outs, rest) if exchanges else outs


def _pool_bwd(dyp, diff, w_pool, pool_scale, seq, exchanges=()):
    T = dyp.shape[0]
    tp = min(TP, seq)
    nseq = seq // tp
    per = tp // HALO
    last_halo = T // HALO - 1

    def body(dy_ref, nxt_ref, diff_ref, w_ref, s_ref, du_ref, dw_ref, ds_ref):
        i = pl.program_id(0)

        @pl.when(i == 0)
        def _():
            dw_ref[...] = jnp.zeros_like(dw_ref)
            ds_ref[...] = jnp.zeros_like(ds_ref)

        last = (i % nseq) == nseq - 1
        nxt = jnp.where(last, 0.0, nxt_ref[...])
        ext = jnp.concatenate([dy_ref[...], nxt], axis=0) * s_ref[...]
        pos = (i % nseq) * tp + lax.broadcasted_iota(jnp.int32, (tp + HALO, 1), 0)
        for gi, w in enumerate(POOL_WINDOWS):
            sl = slice(POOL_GC * gi, POOL_GC * (gi + 1))
            wg = w_ref[gi].astype(BF16)
            dmx = ext[:, sl].astype(BF16)
            ddiff = _dot_nt(dmx, wg)
            s = ddiff * _inv_count(pos, w)
            sh = 1
            while sh < w:
                s = s + pltpu.roll(s, tp + HALO - sh, 0)
                sh *= 2
            du_ref[:, sl] = (s[:tp] - ddiff[:tp]).astype(BF16)
            dg = diff_ref[:, sl]
            dw_ref[gi] += _dot_tn(dg, dmx[:tp])
            ds_ref[:, sl] += jnp.sum(dy_ref[:, sl] * _dot(dg, wg), axis=0, keepdims=True)

    return _call(
        body, name="pool_bwd", grid=(T // tp,),
        in_specs=[_rows(tp, POOL_WIDTH),
                  pl.BlockSpec((HALO, POOL_WIDTH), lambda i: (jnp.minimum((i + 1) * per, last_halo), 0)),
                  _rows(tp, POOL_WIDTH), _const((4, POOL_GC, POOL_GC)), _const((1, POOL_WIDTH))],
        out_specs=[_rows(tp, POOL_WIDTH), _const((4, POOL_GC, POOL_GC)), _const((1, POOL_WIDTH))],
        out_shape=[_sds((T, POOL_WIDTH), BF16), _sds((4, POOL_GC, POOL_GC), F32), _sds((1, POOL_WIDTH), F32)],
        args=(dyp, dyp, diff, w_pool, pool_scale), sem=("arbitrary",), exchanges=exchanges)


_PARTS = ((0, C_Q), (C_Q, C_K), (C_K, C_V), (C_V, C_G), (C_G, IN_WIDTH))


def _inproj_bwd(parts, x2, dx1, w_in_t, g1, exchanges=()):
    T = x2.shape[0]
    tm = min(TM, T)

    def body(du_ref, dq_ref, dk_ref, dv_ref, dgt_ref, x_ref, dx1_ref, w_ref, g_ref, gx_ref, dg_ref):
        @pl.when(pl.program_id(0) == 0)
        def _():
            dg_ref[...] = jnp.zeros_like(dg_ref)

        dh = jnp.zeros((tm, D_MODEL), F32)
        for (lo, hi), p_ref in zip(_PARTS, (du_ref, dq_ref, dk_ref, dv_ref, dgt_ref)):
            dh = dh + _dot(p_ref[...], w_ref[lo:hi, :])
        dx, dg = _norm_bwd(x_ref[...], g_ref[...], dh)
        gx_ref[...] = dx1_ref[...] + dx
        dg_ref[...] += dg

    return _call(
        body, name="inproj_bwd", grid=(T // tm,),
        in_specs=[_rows(tm, hi - lo) for lo, hi in _PARTS]
        + [_rows(tm, D_MODEL), _rows(tm, D_MODEL), _const((IN_WIDTH, D_MODEL)), _const((1, D_MODEL))],
        out_specs=[_rows(tm, D_MODEL), _const((1, D_MODEL))],
        out_shape=[_sds((T, D_MODEL), F32), _sds((1, D_MODEL), F32)],
        args=(*parts, x2, dx1, w_in_t, g1), sem=("arbitrary",), exchanges=exchanges)


def _dw_in(h, parts, exchanges=()):
    T = h.shape[0]
    tk = min(TM, T)

    def body(h_ref, du_ref, dq_ref, dk_ref, dv_ref, dgt_ref, o_ref, db_ref):
        @pl.when(pl.program_id(0) == 0)
        def _():
            o_ref[...] = jnp.zeros_like(o_ref)
            db_ref[...] = jnp.zeros_like(db_ref)

        hh = h_ref[...]
        for (lo, hi), p_ref in zip(_PARTS, (du_ref, dq_ref, dk_ref, dv_ref, dgt_ref)):
            part = p_ref[...]
            o_ref[lo:hi, :] += _dot_tn(part, hh)
            db_ref[:, lo:hi] += jnp.sum(part.astype(F32), axis=0, keepdims=True)

    return _call(
        body, name="dw_in", grid=(T // tk,),
        in_specs=[_rows(tk, D_MODEL)] + [_rows(tk, hi - lo) for lo, hi in _PARTS],
        out_specs=[_const((IN_WIDTH, D_MODEL)), _const((1, IN_WIDTH))],
        out_shape=[_sds((IN_WIDTH, D_MODEL), F32), _sds((1, IN_WIDTH), F32)],
        args=(h, *parts), sem=("arbitrary",), exchanges=exchanges)


def _row_tile(rows, cap=256, mult=16):
    best = None
    for t in range(mult, min(rows, cap) + 1, mult):
        if rows % t == 0:
            best = t
    if best is None:
        raise ValueError("no row tile for %d rows" % rows)
    return best


def _pair_sum(ids, full, got):
    _, r, c = full.shape
    hr = r // 2
    tr = _row_tile(hr)
    nblk = hr // tr

    def body(ids_ref, a_ref, b_ref, own_ref, sb_ref):
        s = a_ref[...] + b_ref[...]
        sb_ref[...] = s.astype(BF16)

        @pl.when(pl.program_id(1) == ids_ref[0])
        def _():
            own_ref[...] = s

    slab = pl.BlockSpec((None, tr, c), lambda i, j, ids_ref: (j, i, 0))
    return pl.pallas_call(
        body, name="pair_sum_%dx%d" % (r, c),
        grid_spec=pltpu.PrefetchScalarGridSpec(
            num_scalar_prefetch=1, grid=(nblk, N_CHIPS),
            in_specs=[pl.BlockSpec((None, tr, c), lambda i, j, ids_ref: (j, ids_ref[1] * nblk + i, 0)), slab],
            out_specs=[pl.BlockSpec((tr, c), lambda i, j, ids_ref: (i, 0)), slab]),
        out_shape=[_sds((hr, c), F32), _sds((N_CHIPS, hr, c), BF16)],
        compiler_params=_cp("parallel", "arbitrary"),
    )(ids, full, got)


def _chip_sum(ids, own, got):
    hr, c = own.shape
    tr = _row_tile(hr)
    nblk = hr // tr

    def body(ids_ref, a_ref, b_ref, o_ref):
        o_ref[...] = ((a_ref[...] + b_ref[0].astype(F32)) + b_ref[1].astype(F32)) + b_ref[2].astype(F32)

    return pl.pallas_call(
        body, name="chip_sum_%dx%d" % (hr, c),
        grid_spec=pltpu.PrefetchScalarGridSpec(
            num_scalar_prefetch=1, grid=(nblk,),
            in_specs=[pl.BlockSpec((tr, c), lambda i, ids_ref: (i, 0)),
                      pl.BlockSpec((3, tr, c), lambda i, ids_ref: (0, i, 0))],
            out_specs=pl.BlockSpec((tr, c), lambda i, ids_ref: (ids_ref[1] * nblk + i, 0))),
        out_shape=_sds((2 * hr, c), F32),
        compiler_params=_cp("parallel"),
    )(ids, own, got)


def _adamw_math(w, g, m, v):
    nm = ADAM_B1 * m + (1.0 - ADAM_B1) * g
    nv = ADAM_B2 * v + (1.0 - ADAM_B2) * (g * g)
    m_hat = nm / (1.0 - ADAM_B1 ** ADAM_STEP)
    v_hat = nv / (1.0 - ADAM_B2 ** ADAM_STEP)
    return -ADAM_LR * (m_hat / (jnp.sqrt(v_hat) + ADAM_EPS) + ADAM_WD * w), nm, nv


def _adamw(w, g, m, v):
    r, c = w.shape
    tr = _row_tile(r, cap=512, mult=8)

    def body(w_ref, g_ref, m_ref, v_ref, d_ref, nm_ref, nv_ref):
        d_ref[...], nm_ref[...], nv_ref[...] = _adamw_math(w_ref[...], g_ref[...], m_ref[...], v_ref[...])

    spec = _rows(tr, c)
    return pl.pallas_call(
        body, name="adamw_%dx%d" % (r, c), grid=(r // tr,),
        in_specs=[spec] * 4, out_specs=[spec] * 3, out_shape=[_sds((r, c), F32)] * 3,
        compiler_params=_cp("parallel"),
    )(w, g, m, v)


SC_TILES = 32
SC_LANES = 16
SC_ROWS = 8


def _adamw_sparse(w, g, m, v):
    r, c = w.shape
    rows = r // SC_TILES
    step = min(rows, SC_ROWS)

    def body(w_hbm, g_hbm, m_hbm, v_hbm, d_hbm, nm_hbm, nv_hbm, wb, gb, mb, vb):
        tile = lax.axis_index("sc_subcore") * 2 + lax.axis_index("sc_core")

        @pl.loop(0, rows, step=step)
        def _(r0):
            mine = pl.ds(tile * rows + r0, step)
            for src, dst in ((w_hbm, wb), (g_hbm, gb), (m_hbm, mb), (v_hbm, vb)):
                pltpu.sync_copy(src.at[mine], dst)

            @pl.loop(0, step)
            def _(row):
                @pl.loop(0, c, step=SC_LANES)
                def _(i):
                    at = (row, pl.ds(i, SC_LANES))
                    wb[at], mb[at], vb[at] = _adamw_math(wb[at], gb[at], mb[at], vb[at])

            for src, dst in ((wb, d_hbm), (mb, nm_hbm), (vb, nv_hbm)):
                pltpu.sync_copy(src, dst.at[mine])

    return pl.kernel(
        body, name="adamw_sparse_%dx%d" % (r, c), out_type=[_sds((r, c), F32)] * 3,
        mesh=plsc.VectorSubcoreMesh(core_axis_name="sc_core", subcore_axis_name="sc_subcore"),
        scratch_types=[pltpu.VMEM((step, c), F32)] * 4,
    )(w, g, m, v)


_SMALL_NAMES = ("w_pool", "b_in", "g_mix_pre", "g_mix_post", "g_mlp_pre", "g_mlp_post", "pool_scale", "attn_sinks")
B_ROWS = -(-IN_WIDTH // D_MODEL)


def _row_block(rows):
    rows = [jnp.pad(r.astype(F32), ((0, 0), (0, D_MODEL - r.shape[1]))) for r in rows]
    return jnp.pad(jnp.concatenate(rows, axis=0), ((0, 8 - len(rows)), (0, 0)))


def _early_block(dg2, dg3, dg4, dps, dsink, loss):
    tail = jnp.concatenate([jnp.pad(dsink.reshape(1, -1), ((0, 0), (0, LANES - dsink.size))),
                            jnp.pad(loss.reshape(1, 1), ((0, 0), (0, LANES - 1)))], axis=1)
    return _row_block([dg2, dg3, dg4, dps, tail])


def _late_block(db_in, dg1):
    b = jnp.pad(db_in, ((0, 0), (0, B_ROWS * D_MODEL - IN_WIDTH))).reshape(B_ROWS, D_MODEL)
    return _row_block([b[r:r + 1] for r in range(B_ROWS)] + [dg1])


def _small_update(gearly, gmat, glate, w, m, v):
    names = _SMALL_NAMES
    n = len(names)

    def total(ref, rows):
        acc = ref[0:rows, :]
        for d in range(1, N_DEV):
            acc = acc + ref[d * rows:(d + 1) * rows, :]
        return acc

    def body(*refs):
        early_ref, gmat_ref, late_ref = refs[:3]
        w_refs, m_refs, v_refs = refs[3:3 + n], refs[3 + n:3 + 2 * n], refs[3 + 2 * n:3 + 3 * n]
        outs = refs[3 + 3 * n:]
        loss_ref, g_refs, d_refs = outs[0], outs[1:1 + n], outs[1 + n:1 + 2 * n]
        nm_refs, nv_refs = outs[1 + 2 * n:1 + 3 * n], outs[1 + 3 * n:1 + 4 * n]
        early, late = total(early_ref, 8), total(late_ref, 8)
        loss_ref[...] = jnp.sum(early[4:5, LANES:2 * LANES], axis=1, keepdims=True)
        bias = jnp.concatenate([late[r:r + 1, :] for r in range(B_ROWS - 1)]
                               + [late[B_ROWS - 1:B_ROWS, :IN_WIDTH - (B_ROWS - 1) * D_MODEL]], axis=1)
        grad = dict(b_in=bias, g_mix_pre=late[B_ROWS:B_ROWS + 1, :], g_mix_post=early[0:1, :],
                    g_mlp_pre=early[1:2, :], g_mlp_post=early[2:3, :], pool_scale=early[3:4, :POOL_WIDTH],
                    attn_sinks=early[4:5, :N_Q_HEADS])
        for i, name in enumerate(names):
            g = total(gmat_ref, 4 * POOL_GC) if name == "w_pool" else grad[name]
            g_refs[i][...] = g
            d_refs[i][...], nm_refs[i][...], nv_refs[i][...] = _adamw_math(
                w_refs[i][...], g, m_refs[i][...], v_refs[i][...])

    shapes = [_sds(w[k].shape, F32) for k in names]
    res = pl.pallas_call(
        body, name="small_update", out_shape=[_sds((1, 1), F32)] + shapes * 4,
        compiler_params=pltpu.CompilerParams(vmem_limit_bytes=VMEM_MB * 1024 * 1024),
    )(gearly, gmat, glate, *[w[k] for k in names], *[m[k] for k in names], *[v[k] for k in names])
    loss = res[0]
    per = {k: tuple(res[1 + j * n + i] for j in range(4)) for i, k in enumerate(names)}
    return loss, per


_BIG = ("w_in", "w_branch_pool", "w_branch_attn", "w_out", "w_up", "w_down")
_ORDER = ("g_mix_pre", "w_in", "b_in", "w_pool", "pool_scale", "attn_sinks", "w_branch_pool", "w_branch_attn",
          "w_out", "g_mix_post", "g_mlp_pre", "w_up", "w_down", "g_mlp_post")


def _stack_rows(slab):
    return slab.reshape(-1, slab.shape[2])


def _step(x2, tgt, seq, shards, small, ids):
    tabs = _rope_tables(seq)
    g1, g2, g3, g4 = (small[n] for n in ("g_mix_pre", "g_mix_post", "g_mlp_pre", "g_mlp_post"))
    sinks = small["attn_sinks"].reshape(N_Q_HEADS)
    w_pool = small["w_pool"].reshape(4, POOL_GC, POOL_GC)
    pool_scale = small["pool_scale"]

    def whole(shard, slabs):
        return lax.dynamic_update_slice(slabs, shard[None], (ids[0], 0, 0))

    up_a, up_b = shards["w_up"][:HALF], shards["w_up"][HALF:]
    down_a, down_b = shards["w_down"][:HALF], shards["w_down"][HALF:]
    w_in = _stack_rows(whole(shards["w_in"], _alone("gather_in", _ex_gather([shards["w_in"]]))[0][0]))
    mix_shards = [shards[n] for n in ("w_branch_pool", "w_branch_attn", "w_out")]
    (h, u, q, k, v, gate), [(*mix_slabs, got_c)] = _inproj(
        x2, g1, w_in, small["b_in"], tabs, seq, exchanges=[_ex_gather(mix_shards + [down_a])])
    w_bp, w_ba, out_slab = (whole(s, g) for s, g in zip(mix_shards, mix_slabs))
    w_out = _stack_rows(out_slab)
    diff, y_pool = _pool_fwd(u, w_pool, pool_scale, seq)
    (y_attn,), [[got_a]] = _attn_fwd(q, k, v, sinks, seq, exchanges=[_ex_gather([up_a])])
    (merged, mix, x1, h2), [[got_b, got_d]] = _merge_out(
        y_pool, y_attn, gate, x2, w_bp, w_ba, w_out, g2, g3, exchanges=[_ex_gather([up_b, down_b])])
    w_up = (whole(up_a, got_a), whole(up_b, got_b))
    w_down = (whole(down_a, got_c), whole(down_b, got_d))
    act, dff, dup, dx1, dmix, loss_acc, dg4, dg3, dg2 = _mlp_core(h2, x1, mix, tgt, w_up, w_down, g4, g3, g2)

    dw_down = _dw("down", act, dff, 1024, 1024)[0].reshape(N_CHIPS, D_FF // N_CHIPS, D_MODEL)
    (dbp, dba, dgate, dyp, dya), [[got]] = _merge_bwd(
        dmix, gate, y_pool, y_attn, w_out, w_bp, w_ba, exchanges=[_ex_pair([dw_down])])
    ps_down = _pair_sum(ids, dw_down, got)
    (dw_up,), [[got]] = _dw("up", h2, dup, 1024, 1024, shard_cols=True, exchanges=[_ex_chip([ps_down[1]])])
    half_down = _chip_sum(ids, ps_down[0], got)
    (dw_out, dw_bp, dw_ba), [[got]] = _dw_mix(merged, dmix, y_pool, dbp, y_attn, dba, exchanges=[_ex_pair([dw_up])])
    ps_up = _pair_sum(ids, dw_up, got)
    dw_mix = [dw_out.reshape(N_CHIPS, D_MODEL // N_CHIPS, D_MODEL), dw_bp, dw_ba]
    (dq, dk, dv, dsink), [[got], gots, [g_down]] = _attn_bwd(
        q, k, v, dya, sinks, tabs, seq, exchanges=[_ex_chip([ps_up[1]]), _ex_pair(dw_mix), _ex_swap([half_down])])
    half_up = _chip_sum(ids, ps_up[0], got)
    ps_mix = [_pair_sum(ids, d, g) for d, g in zip(dw_mix, gots)]
    (du, dw_pool, dps), [[g_up]] = _pool_bwd(dyp, diff, w_pool, pool_scale, seq, exchanges=[_ex_swap([half_up])])
    parts = (du, dq, dk, dv, dgate)
    early = _early_block(dg2, dg3, dg4, dps, dsink[:, 0], loss_acc[0, 0])
    mat = dw_pool.reshape(4 * POOL_GC, POOL_GC)
    (dw_in_t, db_in), [gots, [gearly, gmat]] = _dw_in(
        h, parts, exchanges=[_ex_chip([p[1] for p in ps_mix]), _ex_allgather([early, mat])])
    half_mix = [_chip_sum(ids, p[0], g) for p, g in zip(ps_mix, gots)]
    dw_in = dw_in_t.reshape(N_CHIPS, IN_WIDTH // N_CHIPS, D_MODEL)
    g_mix, [got] = _alone("swap_mix_pair_in", _ex_swap(half_mix), _ex_pair([dw_in]))
    ps_in = _pair_sum(ids, dw_in, got)
    (gx, dg1), [[got]] = _inproj_bwd(parts, x2, dx1, w_in, g1, exchanges=[_ex_chip([ps_in[1]])])
    [g_in], [glate] = _alone("swap_in_allgather", _ex_swap([_chip_sum(ids, ps_in[0], got)]),
                             _ex_allgather([_late_block(db_in, dg1)]))

    grads = dict(w_in=g_in, w_branch_pool=g_mix[1], w_branch_attn=g_mix[2], w_out=g_mix[0], w_up=g_up, w_down=g_down)
    return (gearly, gmat, glate), gx, grads


def kernel(x, g_mix_pre, w_in, b_in, w_pool, pool_scale, attn_sinks, w_branch_pool, w_branch_attn, w_out, g_mix_post, g_mlp_pre, w_up, w_down, g_mlp_post, loss_target, m_g_mix_pre, m_w_in, m_b_in, m_w_pool, m_pool_scale, m_attn_sinks, m_w_branch_pool, m_w_branch_attn, m_w_out, m_g_mix_post, m_g_mlp_pre, m_w_up, m_w_down, m_g_mlp_post, v_g_mix_pre, v_w_in, v_b_in, v_w_pool, v_pool_scale, v_attn_sinks, v_w_branch_pool, v_w_branch_attn, v_w_out, v_g_mix_post, v_g_mlp_pre, v_w_up, v_w_down, v_g_mlp_post):
    weights = dict(g_mix_pre=g_mix_pre, w_in=w_in, b_in=b_in, w_pool=w_pool, pool_scale=pool_scale,
                   attn_sinks=attn_sinks, w_branch_pool=w_branch_pool, w_branch_attn=w_branch_attn, w_out=w_out,
                   g_mix_post=g_mix_post, g_mlp_pre=g_mlp_pre, w_up=w_up, w_down=w_down, g_mlp_post=g_mlp_post)
    mom1 = dict(g_mix_pre=m_g_mix_pre, w_in=m_w_in, b_in=m_b_in, w_pool=m_w_pool, pool_scale=m_pool_scale,
                attn_sinks=m_attn_sinks, w_branch_pool=m_w_branch_pool, w_branch_attn=m_w_branch_attn,
                w_out=m_w_out, g_mix_post=m_g_mix_post, g_mlp_pre=m_g_mlp_pre, w_up=m_w_up, w_down=m_w_down,
                g_mlp_post=m_g_mlp_post)
    mom2 = dict(g_mix_pre=v_g_mix_pre, w_in=v_w_in, b_in=v_b_in, w_pool=v_w_pool, pool_scale=v_pool_scale,
                attn_sinks=v_attn_sinks, w_branch_pool=v_w_branch_pool, w_branch_attn=v_w_branch_attn,
                w_out=v_w_out, g_mix_post=v_g_mix_post, g_mlp_pre=v_g_mlp_pre, w_up=v_w_up, w_down=v_w_down,
                g_mlp_post=v_g_mlp_post)
    b_loc, seq, _ = x.shape
    x2 = x.reshape(b_loc * seq, D_MODEL)
    tgt = loss_target.reshape(b_loc * seq, D_MODEL)
    ids = jnp.stack([2 * lax.axis_index("x") + lax.axis_index("y"), lax.axis_index("c")]).astype(jnp.int32)

    def flat(n, a):
        return a[0].T if n == "w_in" else a[0]

    def unflat(n, a):
        return (a.T if n == "w_in" else a)[None]

    shards = {n: flat(n, weights[n]).astype(BF16) for n in _BIG}
    small = {n: weights[n] for n in _ORDER if n not in _BIG}
    (gearly, gmat, glate), gx, grads = _step(x2, tgt, seq, shards, small, ids)

    def two_d(src):
        return {n: src[n].reshape(4 * POOL_GC, POOL_GC) if n == "w_pool" else src[n] for n in _SMALL_NAMES}

    loss, per = _small_update(gearly, gmat, glate, two_d(weights), two_d(mom1), two_d(mom2))
    delta, new_m, new_v = {}, {}, {}
    for n in _SMALL_NAMES:
        grads[n], delta[n], new_m[n], new_v[n] = (a.reshape(weights[n].shape) for a in per[n])
    for n in _BIG:
        update = _adamw_sparse if n in ("w_up", "w_down") else _adamw
        d, nm, nv = update(flat(n, weights[n]), grads[n], flat(n, mom1[n]), flat(n, mom2[n]))
        grads[n] = unflat(n, grads[n])
        delta[n], new_m[n], new_v[n] = unflat(n, d), unflat(n, nm), unflat(n, nv)

    return (loss[0, 0], gx.reshape(x.shape), *[grads[n] for n in _ORDER], *[delta[n] for n in _ORDER],
            *[new_m[n] for n in _ORDER], *[new_v[n] for n in _ORDER])
```

```python
import jax
import jax.numpy as jnp
from jax import lax
from jax.experimental import pallas as pl
from jax.experimental.pallas import tpu as pltpu
from jax.experimental.pallas import tpu_sc as plsc

F32 = jnp.float32
BF16 = jnp.bfloat16

D_MODEL = 1024
POOL_WINDOWS = (2, 4, 8, 16)
POOL_WIDTH = 512
POOL_GC = 128
HALO = 16
HEAD_DIM = 64
N_Q_HEADS = 8
ATTN_WIDTH = 512
KV_WIDTH = 128
BLOCK = 128
NEG_INF = -1e30
ROPE_THETA = 500000.0
ROT_DIM = 16
GATE_WIDTH = 2048
IN_WIDTH = 3328
D_FF = 4096
EPS = 1e-6
SCALE = HEAD_DIM ** -0.5
C_Q, C_K, C_V, C_G = 512, 1024, 1152, 1280

ADAM_LR, ADAM_B1, ADAM_B2, ADAM_EPS, ADAM_WD, ADAM_STEP = 0.001, 0.9, 0.999, 1e-08, 0.01, 10

N_CHIPS = 4
N_DEV = 8
LANES = 128
TM = 512
TP = 512
VMEM_MB = 56

MESH = pl.DeviceIdType.MESH
ANY = pl.BlockSpec(memory_space=pl.ANY)


def _cp(*sem, vmem=VMEM_MB):
    return pltpu.CompilerParams(dimension_semantics=sem, vmem_limit_bytes=vmem * 1024 * 1024)


def _rows(tile, cols):
    return pl.BlockSpec((tile, cols), lambda i: (i, 0))


def _const(shape):
    nd = len(shape)
    return pl.BlockSpec(shape, lambda i: (0,) * nd)


def _sds(shape, dtype):
    return jax.ShapeDtypeStruct(shape, dtype)


def _dot(a, b):
    return jnp.dot(a, b, preferred_element_type=F32)


def _dot_nt(a, b):
    return lax.dot_general(a, b, (((1,), (1,)), ((), ())), preferred_element_type=F32)


def _dot_tn(a, b):
    return lax.dot_general(a, b, (((0,), (0,)), ((), ())), preferred_element_type=F32)


def _rms(x):
    return lax.rsqrt(jnp.mean(x * x, axis=-1, keepdims=True) + EPS)


def _norm_bwd(x, g, dout):
    r = _rms(x)
    n = x * r
    dn = dout * g
    dx = r * (dn - n * jnp.mean(dn * n, axis=-1, keepdims=True))
    return dx, jnp.sum(dout * n, axis=0, keepdims=True)


def _rot_fwd(t, c, a, bt):
    return t * c + pltpu.roll(t, LANES - 8, 1) * a + pltpu.roll(t, 8, 1) * bt


def _rot_bwd(d, c, a, bt):
    return d * c + pltpu.roll(d * a, 8, 1) + pltpu.roll(d * bt, LANES - 8, 1)


def _rope_tables(seq):
    pos = jnp.arange(seq, dtype=F32)
    inv_freq = ROPE_THETA ** (-jnp.arange(0, ROT_DIM, 2, dtype=F32) / ROT_DIM)
    ang = pos[:, None] * inv_freq[None, :]
    cos, sin = jnp.cos(ang), jnp.sin(ang)
    ones = jnp.ones((seq, HEAD_DIM - ROT_DIM), F32)
    zeros8 = jnp.zeros((seq, 8), F32)
    zrest = jnp.zeros((seq, HEAD_DIM - ROT_DIM), F32)
    c = jnp.concatenate([cos, cos, ones], axis=1)
    a = jnp.concatenate([-sin, zeros8, zrest], axis=1)
    bt = jnp.concatenate([zeros8, sin, zrest], axis=1)
    return tuple(jnp.tile(t, (1, 2)) for t in (c, a, bt))


class _Exchange:
    def __init__(self, inputs, out_shapes, sems, start, finish, aliases=None, middle=None):
        self.inputs, self.out_shapes, self.sems = list(inputs), list(out_shapes), list(sems)
        self.start, self.finish, self.aliases = start, finish, dict(aliases or {})
        self.middle = middle


def _call(body, *, name, grid, in_specs, out_specs, out_shape, args, scratch=(), sem=(), exchanges=()):
    in_specs, out_specs, out_shape, scratch = list(in_specs), list(out_specs), list(out_shape), list(scratch)
    if not exchanges:
        return pl.pallas_call(body, name=name, grid=grid, in_specs=in_specs, out_specs=out_specs,
                              out_shape=out_shape, scratch_shapes=scratch, compiler_params=_cp(*sem))(*args)
    n_in, n_out, n_scr = len(in_specs), len(out_specs), len(scratch)
    x_in = [a for ex in exchanges for a in ex.inputs]
    x_out = [s for ex in exchanges for s in ex.out_shapes]
    x_sem = [s for ex in exchanges for s in ex.sems]
    aliases, i_off, o_off = {}, n_in, n_out
    for ex in exchanges:
        for i, o in ex.aliases.items():
            aliases[i_off + i] = o_off + o
        i_off += len(ex.inputs)
        o_off += len(ex.out_shapes)

    def split(flat):
        out, pos = [], 0
        for ex, n in zip(exchanges, flat[1]):
            out.append(flat[0][pos:pos + n])
            pos += n
        return out

    def carrier(*refs):
        pos = 0
        groups = []
        for n in (n_in, len(x_in), n_out, len(x_out), n_scr, len(x_sem)):
            groups.append(refs[pos:pos + n])
            pos += n
        ins, xin, outs, xout, scr, xsem = groups
        xin = split((xin, [len(ex.inputs) for ex in exchanges]))
        xout = split((xout, [len(ex.out_shapes) for ex in exchanges]))
        xsem = split((xsem, [len(ex.sems) for ex in exchanges]))
        first = pl.program_id(0) == 0
        last = pl.program_id(0) == grid[0] - 1
        for d in range(1, len(grid)):
            first = jnp.logical_and(first, pl.program_id(d) == 0)
            last = jnp.logical_and(last, pl.program_id(d) == grid[d] - 1)

        @pl.when(first)
        def _():
            for ex, i, o, s in zip(exchanges, xin, xout, xsem):
                ex.start(i, o, s)

        if any(ex.middle for ex in exchanges):
            half = pl.program_id(0) == grid[0] // 2
            for d in range(1, len(grid)):
                half = jnp.logical_and(half, pl.program_id(d) == 0)

            @pl.when(half)
            def _():
                for ex, i, o, s in zip(exchanges, xin, xout, xsem):
                    if ex.middle:
                        ex.middle(i, o, s)

        body(*ins, *outs, *scr)

        @pl.when(last)
        def _():
            for ex, i, o, s in zip(exchanges, xin, xout, xsem):
                ex.finish(i, o, s)

    res = pl.pallas_call(
        carrier, name=name, grid=grid, in_specs=in_specs + [ANY] * len(x_in),
        out_specs=out_specs + [ANY] * len(x_out), out_shape=out_shape + x_out,
        scratch_shapes=scratch + x_sem, input_output_aliases=aliases,
        compiler_params=_cp(*(["arbitrary"] * len(grid))),
    )(*args, *x_in)
    return res[:n_out], split((res[n_out:], [len(ex.out_shapes) for ex in exchanges]))


def _alone(name, *exchanges):
    n_in = [len(ex.inputs) for ex in exchanges]
    n_out = [len(ex.out_shapes) for ex in exchanges]
    n_sem = [len(ex.sems) for ex in exchanges]
    aliases, i_off, o_off = {}, 0, 0
    for ex in exchanges:
        for i, o in ex.aliases.items():
            aliases[i_off + i] = o_off + o
        i_off += len(ex.inputs)
        o_off += len(ex.out_shapes)

    def split(flat, counts):
        out, pos = [], 0
        for n in counts:
            out.append(flat[pos:pos + n])
            pos += n
        return out

    def body(*refs):
        ins, outs, sems = split(refs, [sum(n_in), sum(n_out), sum(n_sem)])
        groups = list(zip(exchanges, split(ins, n_in), split(outs, n_out), split(sems, n_sem)))
        for ex, i, o, s in groups:
            ex.start(i, o, s)
        for ex, i, o, s in groups:
            if ex.middle:
                ex.middle(i, o, s)
        for ex, i, o, s in groups:
            ex.finish(i, o, s)

    res = pl.pallas_call(
        body, name=name, in_specs=[ANY] * sum(n_in), out_specs=[ANY] * sum(n_out),
        out_shape=[s for ex in exchanges for s in ex.out_shapes],
        scratch_shapes=[s for ex in exchanges for s in ex.sems], input_output_aliases=aliases,
    )(*[a for ex in exchanges for a in ex.inputs])
    return split(res, n_out)


def _place():
    x, y, c = lax.axis_index("x"), lax.axis_index("y"), lax.axis_index("c")
    chips = [(1 - x, y), (x, 1 - y), (1 - x, 1 - y)]
    return x, y, c, chips


def _remote(src, dst, send, recv, to):
    return pltpu.make_async_remote_copy(src_ref=src, dst_ref=dst, send_sem=send, recv_sem=recv,
                                        device_id=to, device_id_type=MESH)


def _ex_gather(shards):
    nw = len(shards)
    hrs = [s.shape[0] // 2 for s in shards]

    def copies(ins, outs, sems):
        s1, r1, s2, r2, fs, fr = sems
        x, y, c, _ = _place()
        me, xn, yn, dg = (x, y), (1 - x, y), (x, 1 - y), (1 - x, 1 - y)
        nbr = (xn, yn)
        sibling = (x, y, 1 - c)

        def piece(w, chip, core, part=None):
            hr = hrs[w]
            rows = pl.ds(core * hr, hr) if part is None else pl.ds(core * hr + part * (hr // 2), hr // 2)
            return outs[w].at[2 * chip[0] + chip[1], rows]

        def first(w, k):
            return _remote(ins[w].at[pl.ds(c * hrs[w], hrs[w])], piece(w, me, c), s1.at[w, k], r1.at[w, k],
                           (*nbr[k], c))

        def landed(w, k):
            return _remote(piece(w, nbr[k], c), piece(w, nbr[k], c), s1.at[w, k], r1.at[w, k], (*nbr[k], c))

        def onward(w, k):
            return _remote(piece(w, nbr[k], c, k), piece(w, nbr[k], c, k), s2.at[w, k], r2.at[w, k],
                           (*nbr[1 - k], c))

        def arrived(w, k):
            return _remote(piece(w, dg, c, k), piece(w, dg, c, k), s2.at[w, k], r2.at[w, k], (*nbr[1 - k], c))

        def passed(w, j):
            chip = (xn, yn, dg)[j]
            return _remote(piece(w, chip, c), piece(w, chip, c), fs.at[w, j], fr.at[w, j], sibling)

        def handed(w, j):
            chip = (xn, yn, dg)[j]
            return _remote(piece(w, chip, 1 - c), piece(w, chip, 1 - c), fs.at[w, j], fr.at[w, j], sibling)

        return first, landed, onward, arrived, passed, handed

    def start(ins, outs, sems):
        first = copies(ins, outs, sems)[0]
        for w in range(nw):
            for k in range(2):
                first(w, k).start()

    def middle(ins, outs, sems):
        _, landed, onward, _, passed, _ = copies(ins, outs, sems)
        for w in range(nw):
            for k in range(2):
                landed(w, k).wait_recv()
                onward(w, k).start()
                passed(w, k).start()

    def finish(ins, outs, sems):
        first, _, onward, arrived, passed, handed = copies(ins, outs, sems)
        for w in range(nw):
            for k in range(2):
                arrived(w, k).wait_recv()
            passed(w, 2).start()
        for w in range(nw):
            for j in range(3):
                handed(w, j).wait_recv()
        for w in range(nw):
            for k in range(2):
                first(w, k).wait_send()
                onward(w, k).wait_send()
            for j in range(3):
                passed(w, j).wait_send()

    return _Exchange(shards, [_sds((N_CHIPS,) + s.shape, s.dtype) for s in shards],
                     [pltpu.SemaphoreType.DMA((nw, 2))] * 4 + [pltpu.SemaphoreType.DMA((nw, 3))] * 2,
                     start, finish, middle=middle)


def _ex_pair(grads):
    nw = len(grads)

    def copies(ins, outs, sems):
        x, y, c, _ = _place()
        out = []
        for w in range(nw):
            hr = grads[w].shape[1] // 2
            out.append(_remote(ins[w].at[:, pl.ds((1 - c) * hr, hr)], outs[w], sems[0].at[w], sems[1].at[w],
                               (x, y, 1 - c)))
        return out

    def start(ins, outs, sems):
        for cp in copies(ins, outs, sems):
            cp.start()

    def finish(ins, outs, sems):
        for cp in copies(ins, outs, sems):
            cp.wait()

    return _Exchange(grads, [_sds((N_CHIPS, g.shape[1] // 2, g.shape[2]), F32) for g in grads],
                     [pltpu.SemaphoreType.DMA((nw,))] * 2, start, finish)


def _ex_chip(pieces):
    nw = len(pieces)

    def copies(ins, outs, sems):
        x, y, c, chips = _place()
        return [_remote(ins[w].at[2 * cx + cy], outs[w].at[k], sems[0].at[w, k], sems[1].at[w, k], (cx, cy, c))
                for w in range(nw) for k, (cx, cy) in enumerate(chips)]

    def start(ins, outs, sems):
        for cp in copies(ins, outs, sems):
            cp.start()

    def finish(ins, outs, sems):
        for cp in copies(ins, outs, sems):
            cp.wait()

    return _Exchange(pieces, [_sds((3,) + p.shape[1:], BF16) for p in pieces],
                     [pltpu.SemaphoreType.DMA((nw, 3))] * 2, start, finish)


def _ex_swap(fulls):
    nw = len(fulls)

    def start(ins, outs, sems):
        x, y, c, _ = _place()
        for w in range(nw):
            hr = fulls[w].shape[0] // 2
            mine = pl.ds(c * hr, hr)
            _remote(ins[w].at[mine], outs[w].at[mine], sems[0].at[w], sems[1].at[w], (x, y, 1 - c)).start()

    def finish(ins, outs, sems):
        x, y, c, _ = _place()
        for w in range(nw):
            hr = fulls[w].shape[0] // 2
            mine, theirs = pl.ds(c * hr, hr), pl.ds((1 - c) * hr, hr)
            _remote(ins[w].at[mine], outs[w].at[mine], sems[0].at[w], sems[1].at[w], (x, y, 1 - c)).wait_send()
            _remote(ins[w].at[theirs], outs[w].at[theirs], sems[0].at[w], sems[1].at[w], (x, y, 1 - c)).wait_recv()

    return _Exchange(fulls, [_sds(f.shape, F32) for f in fulls], [pltpu.SemaphoreType.DMA((nw,))] * 2,
                     start, finish, aliases={w: w for w in range(nw)})


def _ex_allgather(blocks):
    nb = len(blocks)

    def copies(ins, outs, sems):
        send, recv, lsem = sems
        x, y, c, chips = _place()
        me, sibling = (x, y, c), (x, y, 1 - c)

        def rows(b, px, py, pc):
            m_per = blocks[b].shape[0]
            return outs[b].at[pl.ds((4 * px + 2 * py + pc) * m_per, m_per), :]

        def copy(b, k, blk, to, src=None):
            return _remote(rows(b, *blk) if src is None else src, rows(b, *blk), send.at[b, k], recv.at[b, k], to)

        def mine(b):
            return pltpu.make_async_copy(ins[b], rows(b, *me), lsem.at[b])

        def first(b, k):
            return copy(b, k, me, sibling if k == 0 else (*chips[k - 1], c), src=ins[b])

        def passed(b, j):
            return copy(b, 4 + j, (*chips[j], c), sibling)

        def landed(b, j):
            return copy(b, 1 + j, (*chips[j], c), me)

        def handed(b, k):
            return copy(b, 0, sibling, me) if k == 0 else copy(b, 3 + k, (*chips[k - 1], 1 - c), me)

        return mine, first, passed, landed, handed

    def start(ins, outs, sems):
        mine, first, _, _, _ = copies(ins, outs, sems)
        for b in range(nb):
            mine(b).start()
            for k in range(4):
                first(b, k).start()

    def finish(ins, outs, sems):
        mine, first, passed, landed, handed = copies(ins, outs, sems)
        sent = []
        for b in range(nb):
            for j in range(3):
                landed(b, j).wait_recv()
                cp = passed(b, j)
                cp.start()
                sent.append(cp)
        for b in range(nb):
            for k in range(4):
                handed(b, k).wait_recv()
            for k in range(4):
                first(b, k).wait_send()
        for cp in sent:
            cp.wait_send()
        for b in range(nb):
            mine(b).wait()

    return _Exchange(blocks, [_sds((N_DEV * b.shape[0], b.shape[1]), F32) for b in blocks],
                     [pltpu.SemaphoreType.DMA((nb, 7)), pltpu.SemaphoreType.DMA((nb, 7)), pltpu.SemaphoreType.DMA((nb,))],
                     start, finish)


def _inproj(x2, g1, w_in_t, b_in, tabs, seq, exchanges=()):
    T = x2.shape[0]
    tm = min(TM, seq)
    nseq = seq // tm

    def body(x_ref, g_ref, w_ref, b_ref, c_ref, a_ref, bt_ref, h_ref, u_ref, q_ref, k_ref, v_ref, gate_ref):
        x = x_ref[...]
        h = (x * _rms(x) * g_ref[...]).astype(BF16)
        h_ref[...] = h

        def proj(lo, hi):
            return _dot_nt(h, w_ref[lo:hi, :]) + b_ref[:, lo:hi]

        c, a, bt = c_ref[...], a_ref[...], bt_ref[...]
        u_ref[...] = proj(0, C_Q)
        q = proj(C_Q, C_K)
        for p in range(4):
            sl = slice(LANES * p, LANES * (p + 1))
            q_ref[:, sl] = (_rot_fwd(q[:, sl], c, a, bt) * SCALE).astype(BF16)
        kv = proj(C_K, C_G)
        k_ref[...] = _rot_fwd(kv[:, :KV_WIDTH], c, a, bt).astype(BF16)
        v_ref[...] = kv[:, KV_WIDTH:].astype(BF16)
        for j in range(2):
            lo = C_G + D_MODEL * j
            gate_ref[:, D_MODEL * j:D_MODEL * (j + 1)] = jax.nn.sigmoid(proj(lo, lo + D_MODEL)).astype(BF16)

    tab = pl.BlockSpec((tm, LANES), lambda i: (i % nseq, 0))
    return _call(
        body, name="inproj", grid=(T // tm,),
        in_specs=[_rows(tm, D_MODEL), _const((1, D_MODEL)), _const((IN_WIDTH, D_MODEL)), _const((1, IN_WIDTH)),
                  tab, tab, tab],
        out_specs=[_rows(tm, D_MODEL), _rows(tm, POOL_WIDTH), _rows(tm, ATTN_WIDTH), _rows(tm, KV_WIDTH),
                   _rows(tm, KV_WIDTH), _rows(tm, GATE_WIDTH)],
        out_shape=[_sds((T, D_MODEL), BF16), _sds((T, POOL_WIDTH), F32), _sds((T, ATTN_WIDTH), BF16),
                   _sds((T, KV_WIDTH), BF16), _sds((T, KV_WIDTH), BF16), _sds((T, GATE_WIDTH), BF16)],
        args=(x2, g1, w_in_t, b_in, *tabs), sem=("parallel",), exchanges=exchanges)


def _inv_count(pos, w):
    return 1.0 / jnp.minimum(pos + 1, w).astype(F32)


def _pool_fwd(u, w_pool, pool_scale, seq):
    T = u.shape[0]
    tp = min(TP, seq)
    nseq = seq // tp
    per = tp // HALO

    def body(u_ref, prev_ref, w_ref, s_ref, diff_ref, y_ref):
        i = pl.program_id(0)
        first = (i % nseq) == 0
        prev = jnp.where(first, 0.0, prev_ref[...])
        ext = jnp.concatenate([prev, u_ref[...]], axis=0)
        pos = (i % nseq) * tp + lax.broadcasted_iota(jnp.int32, (tp, 1), 0)
        for gi, w in enumerate(POOL_WINDOWS):
            sl = slice(POOL_GC * gi, POOL_GC * (gi + 1))
            xg = ext[:, sl]
            s = xg
            sh = 1
            while sh < w:
                s = s + pltpu.roll(s, sh, 0)
                sh *= 2
            pooled = s[HALO:] * _inv_count(pos, w)
            diff = (pooled - xg[HALO:]).astype(BF16)
            diff_ref[:, sl] = diff
            mixed = _dot(diff, w_ref[gi].astype(BF16))
            y_ref[:, sl] = (mixed * s_ref[:, sl]).astype(BF16)

    return _call(
        body, name="pool_fwd", grid=(T // tp,),
        in_specs=[_rows(tp, POOL_WIDTH),
                  pl.BlockSpec((HALO, POOL_WIDTH), lambda i: (jnp.maximum(i * per - 1, 0), 0)),
                  _const((4, POOL_GC, POOL_GC)), _const((1, POOL_WIDTH))],
        out_specs=[_rows(tp, POOL_WIDTH), _rows(tp, POOL_WIDTH)],
        out_shape=[_sds((T, POOL_WIDTH), BF16), _sds((T, POOL_WIDTH), BF16)],
        args=(u, u, w_pool, pool_scale), sem=("parallel",))


GROUP = 4
GROWS = GROUP * BLOCK


def _attn_masks(n):
    qi = lax.broadcasted_iota(jnp.int32, (GROWS, 2 * BLOCK), 0) % BLOCK
    kj = lax.broadcasted_iota(jnp.int32, (GROWS, 2 * BLOCK), 1)
    rel = qi + BLOCK - kj
    valid = (rel >= 0) & (rel < BLOCK) & (kj >= jnp.where(n > 0, 0, BLOCK))
    lo = lax.broadcasted_iota(jnp.int32, (BLOCK, LANES), 1) < HEAD_DIM
    return valid, lo


def _by_example(bl, *arrays):
    return [a.reshape(bl, a.shape[0] // bl, a.shape[1]) for a in arrays]


def _stack_heads(ref, h, lo):
    keep = lo if h == 0 else jnp.logical_not(lo)
    pieces = []
    for p in (2 * h, 2 * h + 1):
        xp = ref[:, LANES * p:LANES * (p + 1)].astype(F32)
        for e in range(2):
            t = xp if e == h else pltpu.roll(xp, HEAD_DIM, 1)
            pieces.append(jnp.where(keep, t, 0.0).astype(BF16))
    return jnp.concatenate(pieces, axis=0)


def _unstack_heads(stacked, h, lo):
    pairs = []
    for j in range(2):
        parts = []
        for e in range(2):
            t = stacked[BLOCK * (2 * j + e):BLOCK * (2 * j + e + 1)]
            parts.append(t if e == h else pltpu.roll(t, HEAD_DIM, 1))
        pairs.append(jnp.where(lo, parts[0], parts[1]))
    return pairs


def _sink_rows(sink_ref, h):
    head = lax.broadcasted_iota(jnp.int32, (GROWS, 1), 0) // BLOCK
    col = jnp.zeros((GROWS, 1), F32) + sink_ref[GROUP * h]
    for g in range(1, GROUP):
        col = jnp.where(head == g, sink_ref[GROUP * h + g], col)
    return col


def _group_probs(qs, kk, valid, sink):
    s = jnp.where(valid, _dot_nt(qs, kk), NEG_INF)
    m = jnp.maximum(jnp.max(s, axis=1, keepdims=True), sink)
    ex = jnp.exp(s - m)
    es = jnp.exp(sink - m)
    inv = 1.0 / (jnp.sum(ex, axis=1, keepdims=True) + es)
    return ex * inv, es * inv


def _attn_fwd(q, k, v, sinks, seq, exchanges=()):
    T = q.shape[0]
    nb = seq // BLOCK
    bl = T // seq

    def body(sink_ref, q_ref, kp_ref, kc_ref, vp_ref, vc_ref, o_ref):
        valid, lo = _attn_masks(pl.program_id(0))
        for b in range(bl):
            kk = jnp.concatenate([kp_ref[b], kc_ref[b]], axis=0)
            vv = jnp.concatenate([vp_ref[b], vc_ref[b]], axis=0)
            for h in range(2):
                qs = _stack_heads(q_ref.at[b], h, lo)
                pr, _ = _group_probs(qs, kk, valid, _sink_rows(sink_ref, h))
                o = _dot(pr.astype(BF16), vv)
                for j, pair in enumerate(_unstack_heads(o, h, lo)):
                    p = 2 * h + j
                    o_ref[b, :, LANES * p:LANES * (p + 1)] = pair.astype(BF16)

    cur = lambda n: (0, n, 0)
    prv = lambda n: (0, jnp.maximum(n - 1, 0), 0)
    kv = lambda m: pl.BlockSpec((bl, BLOCK, KV_WIDTH), m)
    res = _call(
        body, name="attn_fwd", grid=(nb,),
        in_specs=[pl.BlockSpec(memory_space=pltpu.SMEM), pl.BlockSpec((bl, BLOCK, ATTN_WIDTH), cur),
                  kv(prv), kv(cur), kv(prv), kv(cur)],
        out_specs=[pl.BlockSpec((bl, BLOCK, ATTN_WIDTH), cur)],
        out_shape=[_sds((bl, seq, ATTN_WIDTH), BF16)],
        args=(sinks, *_by_example(bl, q, k, k, v, v)), sem=("parallel",), exchanges=exchanges)
    if exchanges:
        return [res[0][0].reshape(T, ATTN_WIDTH)], res[1]
    return [res[0].reshape(T, ATTN_WIDTH)]


def _branch(y, w_ref):
    return jnp.concatenate([_dot(y, w_ref[j]) for j in range(N_CHIPS)], axis=1)


def _merge_out(y_pool, y_attn, gate, x2, w_bp, w_ba, w_out, g2, g3, exchanges=()):
    T = x2.shape[0]
    tm = min(TM, T)

    def body(yp_ref, ya_ref, gate_ref, x_ref, wbp_ref, wba_ref, wo_ref, g2_ref, g3_ref,
             mg_ref, mix_ref, x1_ref, h2_ref):
        bp, ba = _branch(yp_ref[...], wbp_ref), _branch(ya_ref[...], wba_ref)
        merged = (gate_ref[:, :D_MODEL].astype(F32) * bp + gate_ref[:, D_MODEL:].astype(F32) * ba).astype(BF16)
        mg_ref[...] = merged
        mix = _dot(merged, wo_ref[...])
        mix_ref[...] = mix
        x1 = x_ref[...] + mix * _rms(mix) * g2_ref[...]
        x1_ref[...] = x1
        h2_ref[...] = (x1 * _rms(x1) * g3_ref[...]).astype(BF16)

    return _call(
        body, name="merge_out", grid=(T // tm,),
        in_specs=[_rows(tm, POOL_WIDTH), _rows(tm, ATTN_WIDTH), _rows(tm, GATE_WIDTH), _rows(tm, D_MODEL),
                  _const(w_bp.shape), _const(w_ba.shape), _const((D_MODEL, D_MODEL)),
                  _const((1, D_MODEL)), _const((1, D_MODEL))],
        out_specs=[_rows(tm, D_MODEL)] * 4,
        out_shape=[_sds((T, D_MODEL), BF16), _sds((T, D_MODEL), F32), _sds((T, D_MODEL), F32),
                   _sds((T, D_MODEL), BF16)],
        args=(y_pool, y_attn, gate, x2, w_bp, w_ba, w_out, g2, g3), sem=("parallel",), exchanges=exchanges)


HALF = D_MODEL // 2
TM_MLP = 256


def _mlp_core(h2, x1, mix, tgt, w_up, w_down, g4, g3, g2):
    T = h2.shape[0]
    tm = min(TM_MLP, T)

    def body(h_ref, x1_ref, mix_ref, t_ref, g_ref, g3_ref, g2_ref, ua_hbm, ub_hbm, da_hbm, db_hbm,
             act_ref, dff_ref, dup_ref, dx1_ref, dmix_ref, loss_ref, dg_ref, dg3_ref, dg2_ref,
             wu, wd, relu_scr, sems):
        def weight_copy(i):
            src, dst = ((ua_hbm, wu.at[:, :HALF]), (ub_hbm, wu.at[:, HALF:]),
                        (da_hbm, wd.at[:, :HALF]), (db_hbm, wd.at[:, HALF:]))[i]
            return pltpu.make_async_copy(src, dst, sems.at[i])

        @pl.when(pl.program_id(0) == 0)
        def _():
            for i in range(4):
                weight_copy(i).start()
            loss_ref[...] = jnp.zeros_like(loss_ref)
            for ref in (dg_ref, dg3_ref, dg2_ref):
                ref[...] = jnp.zeros_like(ref)
            weight_copy(0).wait()
            weight_copy(1).wait()

        h = h_ref[...]
        ff = None
        for j in range(N_CHIPS):
            lo = D_MODEL * j
            relu = jnp.maximum(_dot(h, wu[j]), 0.0)
            if j == 0:
                @pl.when(pl.program_id(0) == 0)
                def _():
                    weight_copy(2).wait()
                    weight_copy(3).wait()
            relu_scr[:, lo:lo + D_MODEL] = relu
            act = jnp.square(relu).astype(BF16)
            act_ref[:, lo:lo + D_MODEL] = act
            t = _dot(act, wd[j])
            ff = t if ff is None else ff + t
        g = g_ref[...]
        x1 = x1_ref[...]
        err = x1 + ff * _rms(ff) * g - t_ref[...]
        loss_ref[...] += jnp.sum(err * err) * (0.5 / D_MODEL)
        dy = err * (1.0 / D_MODEL)
        dff, dg = _norm_bwd(ff, g, dy)
        dg_ref[...] += dg
        dff = dff.astype(BF16)
        dff_ref[...] = dff
        dh2 = None
        for j in range(N_CHIPS):
            lo = D_MODEL * j
            dup = (_dot_nt(dff, wd[j]) * (2.0 * relu_scr[:, lo:lo + D_MODEL])).astype(BF16)
            dup_ref[:, lo:lo + D_MODEL] = dup
            t = _dot_nt(dup, wu[j])
            dh2 = t if dh2 is None else dh2 + t
        dx, dg3 = _norm_bwd(x1, g3_ref[...], dh2)
        dx1 = dy + dx
        dx1_ref[...] = dx1
        dg3_ref[...] += dg3
        dmix, dg2 = _norm_bwd(mix_ref[...], g2_ref[...], dx1)
        dmix_ref[...] = dmix.astype(BF16)
        dg2_ref[...] += dg2

    slabs = pltpu.VMEM((N_CHIPS, D_MODEL, D_MODEL), BF16)
    gain = _const((1, D_MODEL))
    return pl.pallas_call(
        body, name="mlp_core", grid=(T // tm,),
        in_specs=[_rows(tm, D_MODEL)] * 4 + [gain] * 3 + [ANY] * 4,
        out_specs=[_rows(tm, D_FF), _rows(tm, D_MODEL), _rows(tm, D_FF), _rows(tm, D_MODEL), _rows(tm, D_MODEL),
                   _const((8, LANES)), gain, gain, gain],
        out_shape=[_sds((T, D_FF), BF16), _sds((T, D_MODEL), BF16), _sds((T, D_FF), BF16), _sds((T, D_MODEL), F32),
                   _sds((T, D_MODEL), BF16), _sds((8, LANES), F32)] + [_sds((1, D_MODEL), F32)] * 3,
        scratch_shapes=[slabs] * 2 + [pltpu.VMEM((tm, D_FF), F32), pltpu.SemaphoreType.DMA((4,))],
        compiler_params=_cp("arbitrary"),
    )(h2, x1, mix, tgt, g4, g3, g2, *w_up, *w_down)


def _dw(tag, a, g, ta, tn, shard_cols=False, exchanges=()):
    T, ka = a.shape
    n = g.shape[1]
    tk = min(2 * TM, T)
    nk = T // tk

    def body(a_ref, g_ref, o_ref):
        @pl.when(pl.program_id(2) == 0)
        def _():
            o_ref[...] = jnp.zeros_like(o_ref)

        o_ref[...] += _dot_tn(a_ref[...], g_ref[...])

    if shard_cols:
        per = (n // N_CHIPS) // tn
        out_spec = pl.BlockSpec((None, ta, tn), lambda i, j, k: (j // per, i, j % per))
        out_shape = _sds((N_CHIPS, ka, n // N_CHIPS), F32)
    else:
        out_spec = pl.BlockSpec((ta, tn), lambda i, j, k: (i, j))
        out_shape = _sds((ka, n), F32)
    return _call(
        body, name="dw_" + tag, grid=(ka // ta, n // tn, nk),
        in_specs=[pl.BlockSpec((tk, ta), lambda i, j, k: (k, i)), pl.BlockSpec((tk, tn), lambda i, j, k: (k, j))],
        out_specs=[out_spec], out_shape=[out_shape],
        args=(a, g), sem=("parallel", "parallel", "arbitrary"), exchanges=exchanges)


def _dw_mix(merged, dmix, y_pool, dbp, y_attn, dba, exchanges=()):
    T = merged.shape[0]
    tk = min(2 * TM, T)
    c = D_MODEL // N_CHIPS

    def body(mg_ref, dmix_ref, yp_ref, dbp_ref, ya_ref, dba_ref, out_ref, bp_ref, ba_ref):
        @pl.when(pl.program_id(0) == 0)
        def _():
            for ref in (out_ref, bp_ref, ba_ref):
                ref[...] = jnp.zeros_like(ref)

        out_ref[...] += _dot_tn(mg_ref[...], dmix_ref[...])
        for y_ref, d_ref, o_ref in ((yp_ref, dbp_ref, bp_ref), (ya_ref, dba_ref, ba_ref)):
            res = _dot_tn(y_ref[...], d_ref[...])
            for j in range(N_CHIPS):
                o_ref[j] += res[:, c * j:c * (j + 1)]

    slabs = (N_CHIPS, POOL_WIDTH, c)
    return _call(
        body, name="dw_mix", grid=(T // tk,),
        in_specs=[_rows(tk, D_MODEL), _rows(tk, D_MODEL), _rows(tk, POOL_WIDTH), _rows(tk, D_MODEL),
                  _rows(tk, ATTN_WIDTH), _rows(tk, D_MODEL)],
        out_specs=[_const((D_MODEL, D_MODEL)), _const(slabs), _const(slabs)],
        out_shape=[_sds((D_MODEL, D_MODEL), F32), _sds(slabs, F32), _sds(slabs, F32)],
        args=(merged, dmix, y_pool, dbp, y_attn, dba), sem=("arbitrary",), exchanges=exchanges)


def _merge_bwd(dmix, gate, y_pool, y_attn, w_out, w_bp, w_ba, exchanges=()):
    T = dmix.shape[0]
    tm = min(TM, T)

    def body(dmix_ref, gate_ref, yp_ref, ya_ref, wo_ref, wbp_ref, wba_ref,
             dbp_ref, dba_ref, dgate_ref, dyp_ref, dya_ref):
        dm = _dot_nt(dmix_ref[...], wo_ref[...])
        for j, (y_ref, db_ref, w_ref, dy_ref) in enumerate(
                ((yp_ref, dbp_ref, wbp_ref, dyp_ref), (ya_ref, dba_ref, wba_ref, dya_ref))):
            sl = slice(D_MODEL * j, D_MODEL * (j + 1))
            gt = gate_ref[:, sl].astype(F32)
            db = (dm * gt).astype(BF16)
            db_ref[...] = db
            dgate_ref[:, sl] = (dm * _branch(y_ref[...], w_ref) * gt * (1.0 - gt)).astype(BF16)
            cw = D_MODEL // N_CHIPS
            dy = _dot_nt(db[:, :cw], w_ref[0])
            for c in range(1, N_CHIPS):
                dy = dy + _dot_nt(db[:, cw * c:cw * (c + 1)], w_ref[c])
            dy_ref[...] = dy.astype(dy_ref.dtype)

    return _call(
        body, name="merge_bwd", grid=(T // tm,),
        in_specs=[_rows(tm, D_MODEL), _rows(tm, GATE_WIDTH), _rows(tm, POOL_WIDTH), _rows(tm, ATTN_WIDTH),
                  _const((D_MODEL, D_MODEL)), _const(w_bp.shape), _const(w_ba.shape)],
        out_specs=[_rows(tm, D_MODEL), _rows(tm, D_MODEL), _rows(tm, GATE_WIDTH), _rows(tm, POOL_WIDTH),
                   _rows(tm, ATTN_WIDTH)],
        out_shape=[_sds((T, D_MODEL), BF16), _sds((T, D_MODEL), BF16), _sds((T, GATE_WIDTH), BF16),
                   _sds((T, POOL_WIDTH), F32), _sds((T, ATTN_WIDTH), BF16)],
        args=(dmix, gate, y_pool, y_attn, w_out, w_bp, w_ba), sem=("parallel",), exchanges=exchanges)


def _attn_bwd(q, k, v, do, sinks, tabs, seq, exchanges=()):
    T = q.shape[0]
    nb = seq // BLOCK
    bl = T // seq
    steps = nb + 1

    def body(sink_ref, q_ref, do_ref, kp_ref, kc_ref, vp_ref, vc_ref, c_ref, a_ref, bt_ref, cp_ref, ap_ref, btp_ref,
             dq_ref, dk_ref, dv_ref, dsink_ref, ck_ref, cv_ref):
        n = pl.program_id(0)

        @pl.when(n == 0)
        def _():
            dsink_ref[...] = jnp.zeros_like(dsink_ref)
            ck_ref[...] = jnp.zeros_like(ck_ref)
            cv_ref[...] = jnp.zeros_like(cv_ref)

        @pl.when(n < nb)
        def _():
            valid, lo = _attn_masks(n)
            for b in range(bl):
                kk = jnp.concatenate([kp_ref[b], kc_ref[b]], axis=0)
                vv = jnp.concatenate([vp_ref[b], vc_ref[b]], axis=0)
                dk_acc = jnp.zeros((2 * BLOCK, KV_WIDTH), F32)
                dv_acc = jnp.zeros((2 * BLOCK, KV_WIDTH), F32)
                for h in range(2):
                    qs = _stack_heads(q_ref.at[b], h, lo)
                    dos = _stack_heads(do_ref.at[b], h, lo)
                    pr, ps = _group_probs(qs, kk, valid, _sink_rows(sink_ref, h))
                    dp = _dot_nt(dos, vv)
                    delta = jnp.sum(pr * dp, axis=1, keepdims=True)
                    ds = (pr * (dp - delta)).astype(BF16)
                    dsk = ps * delta
                    for g in range(GROUP):
                        idx = GROUP * h + g
                        dsink_ref[idx:idx + 1, :] += (jnp.zeros((1, LANES), F32)
                                                      - jnp.sum(dsk[BLOCK * g:BLOCK * (g + 1)]))
                    dk_acc = dk_acc + _dot_tn(ds, qs)
                    dv_acc = dv_acc + _dot_tn(pr.astype(BF16), dos)
                    for j, pair in enumerate(_unstack_heads(_dot(ds, kk) * SCALE, h, lo)):
                        sl = slice(LANES * (2 * h + j), LANES * (2 * h + j + 1))
                        dq_ref[b, :, sl] = _rot_bwd(pair, c_ref[...], a_ref[...], bt_ref[...]).astype(BF16)
                fin_k = ck_ref[b] + dk_acc[:BLOCK]
                dk_ref[b] = _rot_bwd(fin_k, cp_ref[...], ap_ref[...], btp_ref[...]).astype(BF16)
                dv_ref[b] = (cv_ref[b] + dv_acc[:BLOCK]).astype(BF16)
                ck_ref[b] = dk_acc[BLOCK:]
                cv_ref[b] = dv_acc[BLOCK:]

        @pl.when(n == nb)
        def _():
            for b in range(bl):
                dk_ref[b] = _rot_bwd(ck_ref[b], cp_ref[...], ap_ref[...], btp_ref[...]).astype(BF16)
                dv_ref[b] = cv_ref[b].astype(BF16)

    cur = lambda n: (0, jnp.minimum(n, nb - 1), 0)
    prv = lambda n: (0, jnp.clip(n - 1, 0, nb - 1), 0)
    tcur = lambda n: (jnp.minimum(n, nb - 1), 0)
    tprv = lambda n: (jnp.clip(n - 1, 0, nb - 1), 0)
    wide = lambda m: pl.BlockSpec((bl, BLOCK, ATTN_WIDTH), m)
    kv = lambda m: pl.BlockSpec((bl, BLOCK, KV_WIDTH), m)
    tab = lambda m: pl.BlockSpec((BLOCK, LANES), m)
    res = _call(
        body, name="attn_bwd", grid=(steps,),
        in_specs=[pl.BlockSpec(memory_space=pltpu.SMEM), wide(cur), wide(cur), kv(prv), kv(cur), kv(prv), kv(cur),
                  tab(tcur), tab(tcur), tab(tcur), tab(tprv), tab(tprv), tab(tprv)],
        out_specs=[wide(cur), kv(prv), kv(prv), _const((8, LANES))],
        out_shape=[_sds((bl, seq, ATTN_WIDTH), BF16), _sds((bl, seq, KV_WIDTH), BF16),
                   _sds((bl, seq, KV_WIDTH), BF16), _sds((8, LANES), F32)],
        scratch=[pltpu.VMEM((bl, BLOCK, KV_WIDTH), F32), pltpu.VMEM((bl, BLOCK, KV_WIDTH), F32)],
        args=(sinks, *_by_example(bl, q, do, k, k, v, v), *tabs, *tabs), sem=("arbitrary",), exchanges=exchanges)
    outs, rest = (res if exchanges else (res, None))
    outs = [outs[0].reshape(T, ATTN_WIDTH), outs[1].reshape(T, KV_WIDTH), outs[2].reshape(T, KV_WIDTH), outs[3]]
    return (outs, rest) if exchanges else outs


def _pool_bwd(dyp, diff, w_pool, pool_scale, seq, exchanges=()):
    T = dyp.shape[0]
    tp = min(TP, seq)
    nseq = seq // tp
    per = tp // HALO
    last_halo = T // HALO - 1

    def body(dy_ref, nxt_ref, diff_ref, w_ref, s_ref, du_ref, dw_ref, ds_ref):
        i = pl.program_id(0)

        @pl.when(i == 0)
        def _():
            dw_ref[...] = jnp.zeros_like(dw_ref)
            ds_ref[...] = jnp.zeros_like(ds_ref)

        last = (i % nseq) == nseq - 1
        nxt = jnp.where(last, 0.0, nxt_ref[...])
        ext = jnp.concatenate([dy_ref[...], nxt], axis=0) * s_ref[...]
        pos = (i % nseq) * tp + lax.broadcasted_iota(jnp.int32, (tp + HALO, 1), 0)
        for gi, w in enumerate(POOL_WINDOWS):
            sl = slice(POOL_GC * gi, POOL_GC * (gi + 1))
            wg = w_ref[gi].astype(BF16)
            dmx = ext[:, sl].astype(BF16)
            ddiff = _dot_nt(dmx, wg)
            s = ddiff * _inv_count(pos, w)
            sh = 1
            while sh < w:
                s = s + pltpu.roll(s, tp + HALO - sh, 0)
                sh *= 2
            du_ref[:, sl] = (s[:tp] - ddiff[:tp]).astype(BF16)
            dg = diff_ref[:, sl]
            dw_ref[gi] += _dot_tn(dg, dmx[:tp])
            ds_ref[:, sl] += jnp.sum(dy_ref[:, sl] * _dot(dg, wg), axis=0, keepdims=True)

    return _call(
        body, name="pool_bwd", grid=(T // tp,),
        in_specs=[_rows(tp, POOL_WIDTH),
                  pl.BlockSpec((HALO, POOL_WIDTH), lambda i: (jnp.minimum((i + 1) * per, last_halo), 0)),
                  _rows(tp, POOL_WIDTH), _const((4, POOL_GC, POOL_GC)), _const((1, POOL_WIDTH))],
        out_specs=[_rows(tp, POOL_WIDTH), _const((4, POOL_GC, POOL_GC)), _const((1, POOL_WIDTH))],
        out_shape=[_sds((T, POOL_WIDTH), BF16), _sds((4, POOL_GC, POOL_GC), F32), _sds((1, POOL_WIDTH), F32)],
        args=(dyp, dyp, diff, w_pool, pool_scale), sem=("arbitrary",), exchanges=exchanges)


_PARTS = ((0, C_Q), (C_Q, C_K), (C_K, C_V), (C_V, C_G), (C_G, IN_WIDTH))


def _inproj_bwd(parts, x2, dx1, w_in_t, g1, exchanges=()):
    T = x2.shape[0]
    tm = min(TM, T)

    def body(du_ref, dq_ref, dk_ref, dv_ref, dgt_ref, x_ref, dx1_ref, w_ref, g_ref, gx_ref, dg_ref):
        @pl.when(pl.program_id(0) == 0)
        def _():
            dg_ref[...] = jnp.zeros_like(dg_ref)

        dh = jnp.zeros((tm, D_MODEL), F32)
        for (lo, hi), p_ref in zip(_PARTS, (du_ref, dq_ref, dk_ref, dv_ref, dgt_ref)):
            dh = dh + _dot(p_ref[...], w_ref[lo:hi, :])
        dx, dg = _norm_bwd(x_ref[...], g_ref[...], dh)
        gx_ref[...] = dx1_ref[...] + dx
        dg_ref[...] += dg

    return _call(
        body, name="inproj_bwd", grid=(T // tm,),
        in_specs=[_rows(tm, hi - lo) for lo, hi in _PARTS]
        + [_rows(tm, D_MODEL), _rows(tm, D_MODEL), _const((IN_WIDTH, D_MODEL)), _const((1, D_MODEL))],
        out_specs=[_rows(tm, D_MODEL), _const((1, D_MODEL))],
        out_shape=[_sds((T, D_MODEL), F32), _sds((1, D_MODEL), F32)],
        args=(*parts, x2, dx1, w_in_t, g1), sem=("arbitrary",), exchanges=exchanges)


def _dw_in(h, parts, exchanges=()):
    T = h.shape[0]
    tk = min(TM, T)

    def body(h_ref, du_ref, dq_ref, dk_ref, dv_ref, dgt_ref, o_ref, db_ref):
        @pl.when(pl.program_id(0) == 0)
        def _():
            o_ref[...] = jnp.zeros_like(o_ref)
            db_ref[...] = jnp.zeros_like(db_ref)

        hh = h_ref[...]
        for (lo, hi), p_ref in zip(_PARTS, (du_ref, dq_ref, dk_ref, dv_ref, dgt_ref)):
            part = p_ref[...]
            o_ref[lo:hi, :] += _dot_tn(part, hh)
            db_ref[:, lo:hi] += jnp.sum(part.astype(F32), axis=0, keepdims=True)

    return _call(
        body, name="dw_in", grid=(T // tk,),
        in_specs=[_rows(tk, D_MODEL)] + [_rows(tk, hi - lo) for lo, hi in _PARTS],
        out_specs=[_const((IN_WIDTH, D_MODEL)), _const((1, IN_WIDTH))],
        out_shape=[_sds((IN_WIDTH, D_MODEL), F32), _sds((1, IN_WIDTH), F32)],
        args=(h, *parts), sem=("arbitrary",), exchanges=exchanges)


def _row_tile(rows, cap=256, mult=16):
    best = None
    for t in range(mult, min(rows, cap) + 1, mult):
        if rows % t == 0:
            best = t
    if best is None:
        raise ValueError("no row tile for %d rows" % rows)
    return best


def _pair_sum(ids, full, got):
    _, r, c = full.shape
    hr = r // 2
    tr = _row_tile(hr)
    nblk = hr // tr

    def body(ids_ref, a_ref, b_ref, own_ref, sb_ref):
        s = a_ref[...] + b_ref[...]
        sb_ref[...] = s.astype(BF16)

        @pl.when(pl.program_id(1) == ids_ref[0])
        def _():
            own_ref[...] = s

    slab = pl.BlockSpec((None, tr, c), lambda i, j, ids_ref: (j, i, 0))
    return pl.pallas_call(
        body, name="pair_sum_%dx%d" % (r, c),
        grid_spec=pltpu.PrefetchScalarGridSpec(
            num_scalar_prefetch=1, grid=(nblk, N_CHIPS),
            in_specs=[pl.BlockSpec((None, tr, c), lambda i, j, ids_ref: (j, ids_ref[1] * nblk + i, 0)), slab],
            out_specs=[pl.BlockSpec((tr, c), lambda i, j, ids_ref: (i, 0)), slab]),
        out_shape=[_sds((hr, c), F32), _sds((N_CHIPS, hr, c), BF16)],
        compiler_params=_cp("parallel", "arbitrary"),
    )(ids, full, got)


def _chip_sum(ids, own, got):
    hr, c = own.shape
    tr = _row_tile(hr)
    nblk = hr // tr

    def body(ids_ref, a_ref, b_ref, o_ref):
        o_ref[...] = ((a_ref[...] + b_ref[0].astype(F32)) + b_ref[1].astype(F32)) + b_ref[2].astype(F32)

    return pl.pallas_call(
        body, name="chip_sum_%dx%d" % (hr, c),
        grid_spec=pltpu.PrefetchScalarGridSpec(
            num_scalar_prefetch=1, grid=(nblk,),
            in_specs=[pl.BlockSpec((tr, c), lambda i, ids_ref: (i, 0)),
                      pl.BlockSpec((3, tr, c), lambda i, ids_ref: (0, i, 0))],
            out_specs=pl.BlockSpec((tr, c), lambda i, ids_ref: (ids_ref[1] * nblk + i, 0))),
        out_shape=_sds((2 * hr, c), F32),
        compiler_params=_cp("parallel"),
    )(ids, own, got)


def _adamw_math(w, g, m, v):
    nm = ADAM_B1 * m + (1.0 - ADAM_B1) * g
    nv = ADAM_B2 * v + (1.0 - ADAM_B2) * (g * g)
    m_hat = nm / (1.0 - ADAM_B1 ** ADAM_STEP)
    v_hat = nv / (1.0 - ADAM_B2 ** ADAM_STEP)
    return -ADAM_LR * (m_hat / (jnp.sqrt(v_hat) + ADAM_EPS) + ADAM_WD * w), nm, nv


def _adamw(w, g, m, v):
    r, c = w.shape
    tr = _row_tile(r, cap=512, mult=8)

    def body(w_ref, g_ref, m_ref, v_ref, d_ref, nm_ref, nv_ref):
        d_ref[...], nm_ref[...], nv_ref[...] = _adamw_math(w_ref[...], g_ref[...], m_ref[...], v_ref[...])

    spec = _rows(tr, c)
    return pl.pallas_call(
        body, name="adamw_%dx%d" % (r, c), grid=(r // tr,),
        in_specs=[spec] * 4, out_specs=[spec] * 3, out_shape=[_sds((r, c), F32)] * 3,
        compiler_params=_cp("parallel"),
    )(w, g, m, v)


SC_TILES = 32
SC_LANES = 16
SC_ROWS = 8


def _adamw_sparse(w, g, m, v):
    r, c = w.shape
    rows = r // SC_TILES
    step = min(rows, SC_ROWS)

    def body(w_hbm, g_hbm, m_hbm, v_hbm, d_hbm, nm_hbm, nv_hbm, wb, gb, mb, vb):
        tile = lax.axis_index("sc_subcore") * 2 + lax.axis_index("sc_core")

        @pl.loop(0, rows, step=step)
        def _(r0):
            mine = pl.ds(tile * rows + r0, step)
            for src, dst in ((w_hbm, wb), (g_hbm, gb), (m_hbm, mb), (v_hbm, vb)):
                pltpu.sync_copy(src.at[mine], dst)

            @pl.loop(0, step)
            def _(row):
                @pl.loop(0, c, step=SC_LANES)
                def _(i):
                    at = (row, pl.ds(i, SC_LANES))
                    wb[at], mb[at], vb[at] = _adamw_math(wb[at], gb[at], mb[at], vb[at])

            for src, dst in ((wb, d_hbm), (mb, nm_hbm), (vb, nv_hbm)):
                pltpu.sync_copy(src, dst.at[mine])

    return pl.kernel(
        body, name="adamw_sparse_%dx%d" % (r, c), out_type=[_sds((r, c), F32)] * 3,
        mesh=plsc.VectorSubcoreMesh(core_axis_name="sc_core", subcore_axis_name="sc_subcore"),
        scratch_types=[pltpu.VMEM((step, c), F32)] * 4,
    )(w, g, m, v)


_SMALL_NAMES = ("w_pool", "b_in", "g_mix_pre", "g_mix_post", "g_mlp_pre", "g_mlp_post", "pool_scale", "attn_sinks")
B_ROWS = -(-IN_WIDTH // D_MODEL)


def _row_block(rows):
    rows = [jnp.pad(r.astype(F32), ((0, 0), (0, D_MODEL - r.shape[1]))) for r in rows]
    return jnp.pad(jnp.concatenate(rows, axis=0), ((0, 8 - len(rows)), (0, 0)))


def _early_block(dg2, dg3, dg4, dps, dsink, loss):
    tail = jnp.concatenate([jnp.pad(dsink.reshape(1, -1), ((0, 0), (0, LANES - dsink.size))),
                            jnp.pad(loss.reshape(1, 1), ((0, 0), (0, LANES - 1)))], axis=1)
    return _row_block([dg2, dg3, dg4, dps, tail])


def _late_block(db_in, dg1):
    b = jnp.pad(db_in, ((0, 0), (0, B_ROWS * D_MODEL - IN_WIDTH))).reshape(B_ROWS, D_MODEL)
    return _row_block([b[r:r + 1] for r in range(B_ROWS)] + [dg1])


def _small_update(gearly, gmat, glate, w, m, v):
    names = _SMALL_NAMES
    n = len(names)

    def total(ref, rows):
        acc = ref[0:rows, :]
        for d in range(1, N_DEV):
            acc = acc + ref[d * rows:(d + 1) * rows, :]
        return acc

    def body(*refs):
        early_ref, gmat_ref, late_ref = refs[:3]
        w_refs, m_refs, v_refs = refs[3:3 + n], refs[3 + n:3 + 2 * n], refs[3 + 2 * n:3 + 3 * n]
        outs = refs[3 + 3 * n:]
        loss_ref, g_refs, d_refs = outs[0], outs[1:1 + n], outs[1 + n:1 + 2 * n]
        nm_refs, nv_refs = outs[1 + 2 * n:1 + 3 * n], outs[1 + 3 * n:1 + 4 * n]
        early, late = total(early_ref, 8), total(late_ref, 8)
        loss_ref[...] = jnp.sum(early[4:5, LANES:2 * LANES], axis=1, keepdims=True)
        bias = jnp.concatenate([late[r:r + 1, :] for r in range(B_ROWS - 1)]
                               + [late[B_ROWS - 1:B_ROWS, :IN_WIDTH - (B_ROWS - 1) * D_MODEL]], axis=1)
        grad = dict(b_in=bias, g_mix_pre=late[B_ROWS:B_ROWS + 1, :], g_mix_post=early[0:1, :],
                    g_mlp_pre=early[1:2, :], g_mlp_post=early[2:3, :], pool_scale=early[3:4, :POOL_WIDTH],
                    attn_sinks=early[4:5, :N_Q_HEADS])
        for i, name in enumerate(names):
            g = total(gmat_ref, 4 * POOL_GC) if name == "w_pool" else grad[name]
            g_refs[i][...] = g
            d_refs[i][...], nm_refs[i][...], nv_refs[i][...] = _adamw_math(
                w_refs[i][...], g, m_refs[i][...], v_refs[i][...])

    shapes = [_sds(w[k].shape, F32) for k in names]
    res = pl.pallas_call(
        body, name="small_update", out_shape=[_sds((1, 1), F32)] + shapes * 4,
        compiler_params=pltpu.CompilerParams(vmem_limit_bytes=VMEM_MB * 1024 * 1024),
    )(gearly, gmat, glate, *[w[k] for k in names], *[m[k] for k in names], *[v[k] for k in names])
    loss = res[0]
    per = {k: tuple(res[1 + j * n + i] for j in range(4)) for i, k in enumerate(names)}
    return loss, per


_BIG = ("w_in", "w_branch_pool", "w_branch_attn", "w_out", "w_up", "w_down")
_ORDER = ("g_mix_pre", "w_in", "b_in", "w_pool", "pool_scale", "attn_sinks", "w_branch_pool", "w_branch_attn",
          "w_out", "g_mix_post", "g_mlp_pre", "w_up", "w_down", "g_mlp_post")


def _stack_rows(slab):
    return slab.reshape(-1, slab.shape[2])


def _step(x2, tgt, seq, shards, small, ids):
    tabs = _rope_tables(seq)
    g1, g2, g3, g4 = (small[n] for n in ("g_mix_pre", "g_mix_post", "g_mlp_pre", "g_mlp_post"))
    sinks = small["attn_sinks"].reshape(N_Q_HEADS)
    w_pool = small["w_pool"].reshape(4, POOL_GC, POOL_GC)
    pool_scale = small["pool_scale"]

    def whole(shard, slabs):
        return lax.dynamic_update_slice(slabs, shard[None], (ids[0], 0, 0))

    up_a, up_b = shards["w_up"][:HALF], shards["w_up"][HALF:]
    down_a, down_b = shards["w_down"][:HALF], shards["w_down"][HALF:]
    w_in = _stack_rows(whole(shards["w_in"], _alone("gather_in", _ex_gather([shards["w_in"]]))[0][0]))
    mix_shards = [shards[n] for n in ("w_branch_pool", "w_branch_attn", "w_out")]
    (h, u, q, k, v, gate), [(*mix_slabs, got_c)] = _inproj(
        x2, g1, w_in, small["b_in"], tabs, seq, exchanges=[_ex_gather(mix_shards + [down_a])])
    w_bp, w_ba, out_slab = (whole(s, g) for s, g in zip(mix_shards, mix_slabs))
    w_out = _stack_rows(out_slab)
    diff, y_pool = _pool_fwd(u, w_pool, pool_scale, seq)
    (y_attn,), [[got_a]] = _attn_fwd(q, k, v, sinks, seq, exchanges=[_ex_gather([up_a])])
    (merged, mix, x1, h2), [[got_b, got_d]] = _merge_out(
        y_pool, y_attn, gate, x2, w_bp, w_ba, w_out, g2, g3, exchanges=[_ex_gather([up_b, down_b])])
    w_up = (whole(up_a, got_a), whole(up_b, got_b))
    w_down = (whole(down_a, got_c), whole(down_b, got_d))
    act, dff, dup, dx1, dmix, loss_acc, dg4, dg3, dg2 = _mlp_core(h2, x1, mix, tgt, w_up, w_down, g4, g3, g2)

    dw_down = _dw("down", act, dff, 1024, 1024)[0].reshape(N_CHIPS, D_FF // N_CHIPS, D_MODEL)
    (dbp, dba, dgate, dyp, dya), [[got]] = _merge_bwd(
        dmix, gate, y_pool, y_attn, w_out, w_bp, w_ba, exchanges=[_ex_pair([dw_down])])
    ps_down = _pair_sum(ids, dw_down, got)
    (dw_up,), [[got]] = _dw("up", h2, dup, 1024, 1024, shard_cols=True, exchanges=[_ex_chip([ps_down[1]])])
    half_down = _chip_sum(ids, ps_down[0], got)
    (dw_out, dw_bp, dw_ba), [[got]] = _dw_mix(merged, dmix, y_pool, dbp, y_attn, dba, exchanges=[_ex_pair([dw_up])])
    ps_up = _pair_sum(ids, dw_up, got)
    dw_mix = [dw_out.reshape(N_CHIPS, D_MODEL // N_CHIPS, D_MODEL), dw_bp, dw_ba]
    (dq, dk, dv, dsink), [[got], gots, [g_down]] = _attn_bwd(
        q, k, v, dya, sinks, tabs, seq, exchanges=[_ex_chip([ps_up[1]]), _ex_pair(dw_mix), _ex_swap([half_down])])
    half_up = _chip_sum(ids, ps_up[0], got)
    ps_mix = [_pair_sum(ids, d, g) for d, g in zip(dw_mix, gots)]
    (du, dw_pool, dps), [[g_up]] = _pool_bwd(dyp, diff, w_pool, pool_scale, seq, exchanges=[_ex_swap([half_up])])
    parts = (du, dq, dk, dv, dgate)
    early = _early_block(dg2, dg3, dg4, dps, dsink[:, 0], loss_acc[0, 0])
    mat = dw_pool.reshape(4 * POOL_GC, POOL_GC)
    (dw_in_t, db_in), [gots, [gearly, gmat]] = _dw_in(
        h, parts, exchanges=[_ex_chip([p[1] for p in ps_mix]), _ex_allgather([early, mat])])
    half_mix = [_chip_sum(ids, p[0], g) for p, g in zip(ps_mix, gots)]
    dw_in = dw_in_t.reshape(N_CHIPS, IN_WIDTH // N_CHIPS, D_MODEL)
    g_mix, [got] = _alone("swap_mix_pair_in", _ex_swap(half_mix), _ex_pair([dw_in]))
    ps_in = _pair_sum(ids, dw_in, got)
    (gx, dg1), [[got]] = _inproj_bwd(parts, x2, dx1, w_in, g1, exchanges=[_ex_chip([ps_in[1]])])
    [g_in], [glate] = _alone("swap_in_allgather", _ex_swap([_chip_sum(ids, ps_in[0], got)]),
                             _ex_allgather([_late_block(db_in, dg1)]))

    grads = dict(w_in=g_in, w_branch_pool=g_mix[1], w_branch_attn=g_mix[2], w_out=g_mix[0], w_up=g_up, w_down=g_down)
    return (gearly, gmat, glate), gx, grads


def kernel(x, g_mix_pre, w_in, b_in, w_pool, pool_scale, attn_sinks, w_branch_pool, w_branch_attn, w_out, g_mix_post, g_mlp_pre, w_up, w_down, g_mlp_post, loss_target, m_g_mix_pre, m_w_in, m_b_in, m_w_pool, m_pool_scale, m_attn_sinks, m_w_branch_pool, m_w_branch_attn, m_w_out, m_g_mix_post, m_g_mlp_pre, m_w_up, m_w_down, m_g_mlp_post, v_g_mix_pre, v_w_in, v_b_in, v_w_pool, v_pool_scale, v_attn_sinks, v_w_branch_pool, v_w_branch_attn, v_w_out, v_g_mix_post, v_g_mlp_pre, v_w_up, v_w_down, v_g_mlp_post):
    weights = dict(g_mix_pre=g_mix_pre, w_in=w_in, b_in=b_in, w_pool=w_pool, pool_scale=pool_scale,
                   attn_sinks=attn_sinks, w_branch_pool=w_branch_pool, w_branch_attn=w_branch_attn, w_out=w_out,
                   g_mix_post=g_mix_post, g_mlp_pre=g_mlp_pre, w_up=w_up, w_down=w_down, g_mlp_post=g_mlp_post)
    mom1 = dict(g_mix_pre=m_g_mix_pre, w_in=m_w_in, b_in=m_b_in, w_pool=m_w_pool, pool_scale=m_pool_scale,
                attn_sinks=m_attn_sinks, w_branch_pool=m_w_branch_pool, w_branch_attn=m_w_branch_attn,
                w_out=m_w_out, g_mix_post=m_g_mix_post, g_mlp_pre=m_g_mlp_pre, w_up=m_w_up, w_down=m_w_down,
                g_mlp_post=m_g_mlp_post)
    mom2 = dict(g_mix_pre=v_g_mix_pre, w_in=v_w_in, b_in=v_b_in, w_pool=v_w_pool, pool_scale=v_pool_scale,
                attn_sinks=v_attn_sinks, w_branch_pool=v_w_branch_pool, w_branch_attn=v_w_branch_attn,
                w_out=v_w_out, g_mix_post=v_g_mix_post, g_mlp_pre=v_g_mlp_pre, w_up=v_w_up, w_down=v_w_down,
                g_mlp_post=v_g_mlp_post)
    b_loc, seq, _ = x.shape
    x2 = x.reshape(b_loc * seq, D_MODEL)
    tgt = loss_target.reshape(b_loc * seq, D_MODEL)
    ids = jnp.stack([2 * lax.axis_index("x") + lax.axis_index("y"), lax.axis_index("c")]).astype(jnp.int32)

    def flat(n, a):
        return a[0].T if n == "w_in" else a[0]

    def unflat(n, a):
        return (a.T if n == "w_in" else a)[None]

    shards = {n: flat(n, weights[n]).astype(BF16) for n in _BIG}
    small = {n: weights[n] for n in _ORDER if n not in _BIG}
    (gearly, gmat, glate), gx, grads = _step(x2, tgt, seq, shards, small, ids)

    def two_d(src):
        return {n: src[n].reshape(4 * POOL_GC, POOL_GC) if n == "w_pool" else src[n] for n in _SMALL_NAMES}

    loss, per = _small_update(gearly, gmat, glate, two_d(weights), two_d(mom1), two_d(mom2))
    delta, new_m, new_v = {}, {}, {}
    for n in _SMALL_NAMES:
        grads[n], delta[n], new_m[n], new_v[n] = (a.reshape(weights[n].shape) for a in per[n])
    for n in _BIG:
        update = _adamw if n == "w_in" else _adamw_sparse
        d, nm, nv = update(flat(n, weights[n]), grads[n], flat(n, mom1[n]), flat(n, mom2[n]))
        grads[n] = unflat(n, grads[n])
        delta[n], new_m[n], new_v[n] = unflat(n, d), unflat(n, nm), unflat(n, nv)

    return (loss[0, 0], gx.reshape(x.shape), *[grads[n] for n in _ORDER], *[delta[n] for n in _ORDER],
            *[new_m[n] for n in _ORDER], *[new_v[n] for n in _ORDER])
```

```python
import jax
import jax.numpy as jnp
from jax import lax
from jax.experimental import pallas as pl
from jax.experimental.pallas import tpu as pltpu
from jax.experimental.pallas import tpu_sc as plsc

F32 = jnp.float32
BF16 = jnp.bfloat16

D_MODEL = 1024
POOL_WINDOWS = (2, 4, 8, 16)
POOL_WIDTH = 512
POOL_GC = 128
HALO = 16
HEAD_DIM = 64
N_Q_HEADS = 8
ATTN_WIDTH = 512
KV_WIDTH = 128
BLOCK = 128
NEG_INF = -1e30
ROPE_THETA = 500000.0
ROT_DIM = 16
GATE_WIDTH = 2048
IN_WIDTH = 3328
D_FF = 4096
EPS = 1e-6
SCALE = HEAD_DIM ** -0.5
C_Q, C_K, C_V, C_G = 512, 1024, 1152, 1280

ADAM_LR, ADAM_B1, ADAM_B2, ADAM_EPS, ADAM_WD, ADAM_STEP = 0.001, 0.9, 0.999, 1e-08, 0.01, 10

N_CHIPS = 4
N_DEV = 8
LANES = 128
TM = 512
TP = 512
VMEM_MB = 56

MESH = pl.DeviceIdType.MESH
ANY = pl.BlockSpec(memory_space=pl.ANY)


def _cp(*sem, vmem=VMEM_MB):
    return pltpu.CompilerParams(dimension_semantics=sem, vmem_limit_bytes=vmem * 1024 * 1024)


def _rows(tile, cols):
    return pl.BlockSpec((tile, cols), lambda i: (i, 0))


def _const(shape):
    nd = len(shape)
    return pl.BlockSpec(shape, lambda i: (0,) * nd)


def _sds(shape, dtype):
    return jax.ShapeDtypeStruct(shape, dtype)


def _dot(a, b):
    return jnp.dot(a, b, preferred_element_type=F32)


def _dot_nt(a, b):
    return lax.dot_general(a, b, (((1,), (1,)), ((), ())), preferred_element_type=F32)


def _dot_tn(a, b):
    return lax.dot_general(a, b, (((0,), (0,)), ((), ())), preferred_element_type=F32)


def _rms(x):
    return lax.rsqrt(jnp.mean(x * x, axis=-1, keepdims=True) + EPS)


def _norm_bwd(x, g, dout):
    r = _rms(x)
    n = x * r
    dn = dout * g
    dx = r * (dn - n * jnp.mean(dn * n, axis=-1, keepdims=True))
    return dx, jnp.sum(dout * n, axis=0, keepdims=True)


def _rot_fwd(t, c, a, bt):
    return t * c + pltpu.roll(t, LANES - 8, 1) * a + pltpu.roll(t, 8, 1) * bt


def _rot_bwd(d, c, a, bt):
    return d * c + pltpu.roll(d * a, 8, 1) + pltpu.roll(d * bt, LANES - 8, 1)


def _rope_tables(seq):
    pos = jnp.arange(seq, dtype=F32)
    inv_freq = ROPE_THETA ** (-jnp.arange(0, ROT_DIM, 2, dtype=F32) / ROT_DIM)
    ang = pos[:, None] * inv_freq[None, :]
    cos, sin = jnp.cos(ang), jnp.sin(ang)
    ones = jnp.ones((seq, HEAD_DIM - ROT_DIM), F32)
    zeros8 = jnp.zeros((seq, 8), F32)
    zrest = jnp.zeros((seq, HEAD_DIM - ROT_DIM), F32)
    c = jnp.concatenate([cos, cos, ones], axis=1)
    a = jnp.concatenate([-sin, zeros8, zrest], axis=1)
    bt = jnp.concatenate([zeros8, sin, zrest], axis=1)
    return tuple(jnp.tile(t, (1, 2)) for t in (c, a, bt))


class _Exchange:
    def __init__(self, inputs, out_shapes, sems, start, finish, aliases=None, middle=None):
        self.inputs, self.out_shapes, self.sems = list(inputs), list(out_shapes), list(sems)
        self.start, self.finish, self.aliases = start, finish, dict(aliases or {})
        self.middle = middle


def _call(body, *, name, grid, in_specs, out_specs, out_shape, args, scratch=(), sem=(), exchanges=()):
    in_specs, out_specs, out_shape, scratch = list(in_specs), list(out_specs), list(out_shape), list(scratch)
    if not exchanges:
        return pl.pallas_call(body, name=name, grid=grid, in_specs=in_specs, out_specs=out_specs,
                              out_shape=out_shape, scratch_shapes=scratch, compiler_params=_cp(*sem))(*args)
    n_in, n_out, n_scr = len(in_specs), len(out_specs), len(scratch)
    x_in = [a for ex in exchanges for a in ex.inputs]
    x_out = [s for ex in exchanges for s in ex.out_shapes]
    x_sem = [s for ex in exchanges for s in ex.sems]
    aliases, i_off, o_off = {}, n_in, n_out
    for ex in exchanges:
        for i, o in ex.aliases.items():
            aliases[i_off + i] = o_off + o
        i_off += len(ex.inputs)
        o_off += len(ex.out_shapes)

    def split(flat):
        out, pos = [], 0
        for ex, n in zip(exchanges, flat[1]):
            out.append(flat[0][pos:pos + n])
            pos += n
        return out

    def carrier(*refs):
        pos = 0
        groups = []
        for n in (n_in, len(x_in), n_out, len(x_out), n_scr, len(x_sem)):
            groups.append(refs[pos:pos + n])
            pos += n
        ins, xin, outs, xout, scr, xsem = groups
        xin = split((xin, [len(ex.inputs) for ex in exchanges]))
        xout = split((xout, [len(ex.out_shapes) for ex in exchanges]))
        xsem = split((xsem, [len(ex.sems) for ex in exchanges]))
        first = pl.program_id(0) == 0
        last = pl.program_id(0) == grid[0] - 1
        for d in range(1, len(grid)):
            first = jnp.logical_and(first, pl.program_id(d) == 0)
            last = jnp.logical_and(last, pl.program_id(d) == grid[d] - 1)

        @pl.when(first)
        def _():
            for ex, i, o, s in zip(exchanges, xin, xout, xsem):
                ex.start(i, o, s)

        if any(ex.middle for ex in exchanges):
            half = pl.program_id(0) == grid[0] // 2
            for d in range(1, len(grid)):
                half = jnp.logical_and(half, pl.program_id(d) == 0)

            @pl.when(half)
            def _():
                for ex, i, o, s in zip(exchanges, xin, xout, xsem):
                    if ex.middle:
                        ex.middle(i, o, s)

        body(*ins, *outs, *scr)

        @pl.when(last)
        def _():
            for ex, i, o, s in zip(exchanges, xin, xout, xsem):
                ex.finish(i, o, s)

    res = pl.pallas_call(
        carrier, name=name, grid=grid, in_specs=in_specs + [ANY] * len(x_in),
        out_specs=out_specs + [ANY] * len(x_out), out_shape=out_shape + x_out,
        scratch_shapes=scratch + x_sem, input_output_aliases=aliases,
        compiler_params=_cp(*(["arbitrary"] * len(grid))),
    )(*args, *x_in)
    return res[:n_out], split((res[n_out:], [len(ex.out_shapes) for ex in exchanges]))


def _alone(name, *exchanges):
    n_in = [len(ex.inputs) for ex in exchanges]
    n_out = [len(ex.out_shapes) for ex in exchanges]
    n_sem = [len(ex.sems) for ex in exchanges]
    aliases, i_off, o_off = {}, 0, 0
    for ex in exchanges:
        for i, o in ex.aliases.items():
            aliases[i_off + i] = o_off + o
        i_off += len(ex.inputs)
        o_off += len(ex.out_shapes)

    def split(flat, counts):
        out, pos = [], 0
        for n in counts:
            out.append(flat[pos:pos + n])
            pos += n
        return out

    def body(*refs):
        ins, outs, sems = split(refs, [sum(n_in), sum(n_out), sum(n_sem)])
        groups = list(zip(exchanges, split(ins, n_in), split(outs, n_out), split(sems, n_sem)))
        for ex, i, o, s in groups:
            ex.start(i, o, s)
        for ex, i, o, s in groups:
            if ex.middle:
                ex.middle(i, o, s)
        for ex, i, o, s in groups:
            ex.finish(i, o, s)

    res = pl.pallas_call(
        body, name=name, in_specs=[ANY] * sum(n_in), out_specs=[ANY] * sum(n_out),
        out_shape=[s for ex in exchanges for s in ex.out_shapes],
        scratch_shapes=[s for ex in exchanges for s in ex.sems], input_output_aliases=aliases,
    )(*[a for ex in exchanges for a in ex.inputs])
    return split(res, n_out)


def _place():
    x, y, c = lax.axis_index("x"), lax.axis_index("y"), lax.axis_index("c")
    chips = [(1 - x, y), (x, 1 - y), (1 - x, 1 - y)]
    return x, y, c, chips


def _remote(src, dst, send, recv, to):
    return pltpu.make_async_remote_copy(src_ref=src, dst_ref=dst, send_sem=send, recv_sem=recv,
                                        device_id=to, device_id_type=MESH)


def _ex_gather(shards):
    nw = len(shards)
    hrs = [s.shape[0] // 2 for s in shards]

    def copies(ins, outs, sems):
        s1, r1, s2, r2, fs, fr = sems
        x, y, c, _ = _place()
        me, xn, yn, dg = (x, y), (1 - x, y), (x, 1 - y), (1 - x, 1 - y)
        nbr = (xn, yn)
        sibling = (x, y, 1 - c)

        def piece(w, chip, core, part=None):
            hr = hrs[w]
            rows = pl.ds(core * hr, hr) if part is None else pl.ds(core * hr + part * (hr // 2), hr // 2)
            return outs[w].at[2 * chip[0] + chip[1], rows]

        def first(w, k):
            return _remote(ins[w].at[pl.ds(c * hrs[w], hrs[w])], piece(w, me, c), s1.at[w, k], r1.at[w, k],
                           (*nbr[k], c))

        def landed(w, k):
            return _remote(piece(w, nbr[k], c), piece(w, nbr[k], c), s1.at[w, k], r1.at[w, k], (*nbr[k], c))

        def onward(w, k):
            return _remote(piece(w, nbr[k], c, k), piece(w, nbr[k], c, k), s2.at[w, k], r2.at[w, k],
                           (*nbr[1 - k], c))

        def arrived(w, k):
            return _remote(piece(w, dg, c, k), piece(w, dg, c, k), s2.at[w, k], r2.at[w, k], (*nbr[1 - k], c))

        def passed(w, j):
            chip = (xn, yn, dg)[j]
            return _remote(piece(w, chip, c), piece(w, chip, c), fs.at[w, j], fr.at[w, j], sibling)

        def handed(w, j):
            chip = (xn, yn, dg)[j]
            return _remote(piece(w, chip, 1 - c), piece(w, chip, 1 - c), fs.at[w, j], fr.at[w, j], sibling)

        return first, landed, onward, arrived, passed, handed

    def start(ins, outs, sems):
        first = copies(ins, outs, sems)[0]
        for w in range(nw):
            for k in range(2):
                first(w, k).start()

    def middle(ins, outs, sems):
        _, landed, onward, _, passed, _ = copies(ins, outs, sems)
        for w in range(nw):
            for k in range(2):
                landed(w, k).wait_recv()
                onward(w, k).start()
                passed(w, k).start()

    def finish(ins, outs, sems):
        first, _, onward, arrived, passed, handed = copies(ins, outs, sems)
        for w in range(nw):
            for k in range(2):
                arrived(w, k).wait_recv()
            passed(w, 2).start()
        for w in range(nw):
            for j in range(3):
                handed(w, j).wait_recv()
        for w in range(nw):
            for k in range(2):
                first(w, k).wait_send()
                onward(w, k).wait_send()
            for j in range(3):
                passed(w, j).wait_send()

    return _Exchange(shards, [_sds((N_CHIPS,) + s.shape, s.dtype) for s in shards],
                     [pltpu.SemaphoreType.DMA((nw, 2))] * 4 + [pltpu.SemaphoreType.DMA((nw, 3))] * 2,
                     start, finish, middle=middle)


def _ex_pair(grads):
    nw = len(grads)

    def copies(ins, outs, sems):
        x, y, c, _ = _place()
        out = []
        for w in range(nw):
            hr = grads[w].shape[1] // 2
            out.append(_remote(ins[w].at[:, pl.ds((1 - c) * hr, hr)], outs[w], sems[0].at[w], sems[1].at[w],
                               (x, y, 1 - c)))
        return out

    def start(ins, outs, sems):
        for cp in copies(ins, outs, sems):
            cp.start()

    def finish(ins, outs, sems):
        for cp in copies(ins, outs, sems):
            cp.wait()

    return _Exchange(grads, [_sds((N_CHIPS, g.shape[1] // 2, g.shape[2]), F32) for g in grads],
                     [pltpu.SemaphoreType.DMA((nw,))] * 2, start, finish)


def _ex_chip(pieces):
    nw = len(pieces)

    def copies(ins, outs, sems):
        x, y, c, chips = _place()
        return [_remote(ins[w].at[2 * cx + cy], outs[w].at[k], sems[0].at[w, k], sems[1].at[w, k], (cx, cy, c))
                for w in range(nw) for k, (cx, cy) in enumerate(chips)]

    def start(ins, outs, sems):
        for cp in copies(ins, outs, sems):
            cp.start()

    def finish(ins, outs, sems):
        for cp in copies(ins, outs, sems):
            cp.wait()

    return _Exchange(pieces, [_sds((3,) + p.shape[1:], BF16) for p in pieces],
                     [pltpu.SemaphoreType.DMA((nw, 3))] * 2, start, finish)


def _ex_swap(fulls):
    nw = len(fulls)

    def start(ins, outs, sems):
        x, y, c, _ = _place()
        for w in range(nw):
            hr = fulls[w].shape[0] // 2
            mine = pl.ds(c * hr, hr)
            _remote(ins[w].at[mine], outs[w].at[mine], sems[0].at[w], sems[1].at[w], (x, y, 1 - c)).start()

    def finish(ins, outs, sems):
        x, y, c, _ = _place()
        for w in range(nw):
            hr = fulls[w].shape[0] // 2
            mine, theirs = pl.ds(c * hr, hr), pl.ds((1 - c) * hr, hr)
            _remote(ins[w].at[mine], outs[w].at[mine], sems[0].at[w], sems[1].at[w], (x, y, 1 - c)).wait_send()
            _remote(ins[w].at[theirs], outs[w].at[theirs], sems[0].at[w], sems[1].at[w], (x, y, 1 - c)).wait_recv()

    return _Exchange(fulls, [_sds(f.shape, F32) for f in fulls], [pltpu.SemaphoreType.DMA((nw,))] * 2,
                     start, finish, aliases={w: w for w in range(nw)})


def _ex_allgather(blocks):
    nb = len(blocks)

    def copies(ins, outs, sems):
        send, recv, lsem = sems
        x, y, c, chips = _place()
        me, sibling = (x, y, c), (x, y, 1 - c)

        def rows(b, px, py, pc):
            m_per = blocks[b].shape[0]
            return outs[b].at[pl.ds((4 * px + 2 * py + pc) * m_per, m_per), :]

        def copy(b, k, blk, to, src=None):
            return _remote(rows(b, *blk) if src is None else src, rows(b, *blk), send.at[b, k], recv.at[b, k], to)

        def mine(b):
            return pltpu.make_async_copy(ins[b], rows(b, *me), lsem.at[b])

        def first(b, k):
            return copy(b, k, me, sibling if k == 0 else (*chips[k - 1], c), src=ins[b])

        def passed(b, j):
            return copy(b, 4 + j, (*chips[j], c), sibling)

        def landed(b, j):
            return copy(b, 1 + j, (*chips[j], c), me)

        def handed(b, k):
            return copy(b, 0, sibling, me) if k == 0 else copy(b, 3 + k, (*chips[k - 1], 1 - c), me)

        return mine, first, passed, landed, handed

    def start(ins, outs, sems):
        mine, first, _, _, _ = copies(ins, outs, sems)
        for b in range(nb):
            mine(b).start()
            for k in range(4):
                first(b, k).start()

    def finish(ins, outs, sems):
        mine, first, passed, landed, handed = copies(ins, outs, sems)
        sent = []
        for b in range(nb):
            for j in range(3):
                landed(b, j).wait_recv()
                cp = passed(b, j)
                cp.start()
                sent.append(cp)
        for b in range(nb):
            for k in range(4):
                handed(b, k).wait_recv()
            for k in range(4):
                first(b, k).wait_send()
        for cp in sent:
            cp.wait_send()
        for b in range(nb):
            mine(b).wait()

    return _Exchange(blocks, [_sds((N_DEV * b.shape[0], b.shape[1]), F32) for b in blocks],
                     [pltpu.SemaphoreType.DMA((nb, 7)), pltpu.SemaphoreType.DMA((nb, 7)), pltpu.SemaphoreType.DMA((nb,))],
                     start, finish)


def _inproj(x2, g1, w_in_t, b_in, tabs, seq, exchanges=()):
    T = x2.shape[0]
    tm = min(TM, seq)
    nseq = seq // tm

    def body(x_ref, g_ref, w_ref, b_ref, c_ref, a_ref, bt_ref, h_ref, u_ref, q_ref, k_ref, v_ref, gate_ref):
        x = x_ref[...]
        h = (x * _rms(x) * g_ref[...]).astype(BF16)
        h_ref[...] = h

        def proj(lo, hi):
            return _dot_nt(h, w_ref[lo:hi, :]) + b_ref[:, lo:hi]

        c, a, bt = c_ref[...], a_ref[...], bt_ref[...]
        u_ref[...] = proj(0, C_Q)
        q = proj(C_Q, C_K)
        for p in range(4):
            sl = slice(LANES * p, LANES * (p + 1))
            q_ref[:, sl] = (_rot_fwd(q[:, sl], c, a, bt) * SCALE).astype(BF16)
        kv = proj(C_K, C_G)
        k_ref[...] = _rot_fwd(kv[:, :KV_WIDTH], c, a, bt).astype(BF16)
        v_ref[...] = kv[:, KV_WIDTH:].astype(BF16)
        for j in range(2):
            lo = C_G + D_MODEL * j
            gate_ref[:, D_MODEL * j:D_MODEL * (j + 1)] = jax.nn.sigmoid(proj(lo, lo + D_MODEL)).astype(BF16)

    tab = pl.BlockSpec((tm, LANES), lambda i: (i % nseq, 0))
    return _call(
        body, name="inproj", grid=(T // tm,),
        in_specs=[_rows(tm, D_MODEL), _const((1, D_MODEL)), _const((IN_WIDTH, D_MODEL)), _const((1, IN_WIDTH)),
                  tab, tab, tab],
        out_specs=[_rows(tm, D_MODEL), _rows(tm, POOL_WIDTH), _rows(tm, ATTN_WIDTH), _rows(tm, KV_WIDTH),
                   _rows(tm, KV_WIDTH), _rows(tm, GATE_WIDTH)],
        out_shape=[_sds((T, D_MODEL), BF16), _sds((T, POOL_WIDTH), F32), _sds((T, ATTN_WIDTH), BF16),
                   _sds((T, KV_WIDTH), BF16), _sds((T, KV_WIDTH), BF16), _sds((T, GATE_WIDTH), BF16)],
        args=(x2, g1, w_in_t, b_in, *tabs), sem=("parallel",), exchanges=exchanges)


def _inv_count(pos, w):
    return 1.0 / jnp.minimum(pos + 1, w).astype(F32)


def _pool_fwd(u, w_pool, pool_scale, seq):
    T = u.shape[0]
    tp = min(TP, seq)
    nseq = seq // tp
    per = tp // HALO

    def body(u_ref, prev_ref, w_ref, s_ref, diff_ref, y_ref):
        i = pl.program_id(0)
        first = (i % nseq) == 0
        prev = jnp.where(first, 0.0, prev_ref[...])
        ext = jnp.concatenate([prev, u_ref[...]], axis=0)
        pos = (i % nseq) * tp + lax.broadcasted_iota(jnp.int32, (tp, 1), 0)
        for gi, w in enumerate(POOL_WINDOWS):
            sl = slice(POOL_GC * gi, POOL_GC * (gi + 1))
            xg = ext[:, sl]
            s = xg
            sh = 1
            while sh < w:
                s = s + pltpu.roll(s, sh, 0)
                sh *= 2
            pooled = s[HALO:] * _inv_count(pos, w)
            diff = (pooled - xg[HALO:]).astype(BF16)
            diff_ref[:, sl] = diff
            mixed = _dot(diff, w_ref[gi].astype(BF16))
            y_ref[:, sl] = (mixed * s_ref[:, sl]).astype(BF16)

    return _call(
        body, name="pool_fwd", grid=(T // tp,),
        in_specs=[_rows(tp, POOL_WIDTH),
                  pl.BlockSpec((HALO, POOL_WIDTH), lambda i: (jnp.maximum(i * per - 1, 0), 0)),
                  _const((4, POOL_GC, POOL_GC)), _const((1, POOL_WIDTH))],
        out_specs=[_rows(tp, POOL_WIDTH), _rows(tp, POOL_WIDTH)],
        out_shape=[_sds((T, POOL_WIDTH), BF16), _sds((T, POOL_WIDTH), BF16)],
        args=(u, u, w_pool, pool_scale), sem=("parallel",))


GROUP = 4
GROWS = GROUP * BLOCK


def _attn_masks(n):
    qi = lax.broadcasted_iota(jnp.int32, (GROWS, 2 * BLOCK), 0) % BLOCK
    kj = lax.broadcasted_iota(jnp.int32, (GROWS, 2 * BLOCK), 1)
    rel = qi + BLOCK - kj
    valid = (rel >= 0) & (rel < BLOCK) & (kj >= jnp.where(n > 0, 0, BLOCK))
    lo = lax.broadcasted_iota(jnp.int32, (BLOCK, LANES), 1) < HEAD_DIM
    return valid, lo


def _by_example(bl, *arrays):
    return [a.reshape(bl, a.shape[0] // bl, a.shape[1]) for a in arrays]


def _stack_heads(ref, h, lo):
    keep = lo if h == 0 else jnp.logical_not(lo)
    pieces = []
    for p in (2 * h, 2 * h + 1):
        xp = ref[:, LANES * p:LANES * (p + 1)].astype(F32)
        for e in range(2):
            t = xp if e == h else pltpu.roll(xp, HEAD_DIM, 1)
            pieces.append(jnp.where(keep, t, 0.0).astype(BF16))
    return jnp.concatenate(pieces, axis=0)


def _unstack_heads(stacked, h, lo):
    pairs = []
    for j in range(2):
        parts = []
        for e in range(2):
            t = stacked[BLOCK * (2 * j + e):BLOCK * (2 * j + e + 1)]
            parts.append(t if e == h else pltpu.roll(t, HEAD_DIM, 1))
        pairs.append(jnp.where(lo, parts[0], parts[1]))
    return pairs


def _sink_rows(sink_ref, h):
    head = lax.broadcasted_iota(jnp.int32, (GROWS, 1), 0) // BLOCK
    col = jnp.zeros((GROWS, 1), F32) + sink_ref[GROUP * h]
    for g in range(1, GROUP):
        col = jnp.where(head == g, sink_ref[GROUP * h + g], col)
    return col


def _group_probs(qs, kk, valid, sink):
    s = jnp.where(valid, _dot_nt(qs, kk), NEG_INF)
    m = jnp.maximum(jnp.max(s, axis=1, keepdims=True), sink)
    ex = jnp.exp(s - m)
    es = jnp.exp(sink - m)
    inv = 1.0 / (jnp.sum(ex, axis=1, keepdims=True) + es)
    return ex * inv, es * inv


def _attn_fwd(q, k, v, sinks, seq, exchanges=()):
    T = q.shape[0]
    nb = seq // BLOCK
    bl = T // seq

    def body(sink_ref, q_ref, kp_ref, kc_ref, vp_ref, vc_ref, o_ref):
        valid, lo = _attn_masks(pl.program_id(0))
        for b in range(bl):
            kk = jnp.concatenate([kp_ref[b], kc_ref[b]], axis=0)
            vv = jnp.concatenate([vp_ref[b], vc_ref[b]], axis=0)
            for h in range(2):
                qs = _stack_heads(q_ref.at[b], h, lo)
                pr, _ = _group_probs(qs, kk, valid, _sink_rows(sink_ref, h))
                o = _dot(pr.astype(BF16), vv)
                for j, pair in enumerate(_unstack_heads(o, h, lo)):
                    p = 2 * h + j
                    o_ref[b, :, LANES * p:LANES * (p + 1)] = pair.astype(BF16)

    cur = lambda n: (0, n, 0)
    prv = lambda n: (0, jnp.maximum(n - 1, 0), 0)
    kv = lambda m: pl.BlockSpec((bl, BLOCK, KV_WIDTH), m)
    res = _call(
        body, name="attn_fwd", grid=(nb,),
        in_specs=[pl.BlockSpec(memory_space=pltpu.SMEM), pl.BlockSpec((bl, BLOCK, ATTN_WIDTH), cur),
                  kv(prv), kv(cur), kv(prv), kv(cur)],
        out_specs=[pl.BlockSpec((bl, BLOCK, ATTN_WIDTH), cur)],
        out_shape=[_sds((bl, seq, ATTN_WIDTH), BF16)],
        args=(sinks, *_by_example(bl, q, k, k, v, v)), sem=("parallel",), exchanges=exchanges)
    if exchanges:
        return [res[0][0].reshape(T, ATTN_WIDTH)], res[1]
    return [res[0].reshape(T, ATTN_WIDTH)]


def _branch(y, w_ref):
    return jnp.concatenate([_dot(y, w_ref[j]) for j in range(N_CHIPS)], axis=1)


def _merge_out(y_pool, y_attn, gate, x2, w_bp, w_ba, w_out, g2, g3, exchanges=()):
    T = x2.shape[0]
    tm = min(TM, T)

    def body(yp_ref, ya_ref, gate_ref, x_ref, wbp_ref, wba_ref, wo_ref, g2_ref, g3_ref,
             mg_ref, mix_ref, x1_ref, h2_ref):
        bp, ba = _branch(yp_ref[...], wbp_ref), _branch(ya_ref[...], wba_ref)
        merged = (gate_ref[:, :D_MODEL].astype(F32) * bp + gate_ref[:, D_MODEL:].astype(F32) * ba).astype(BF16)
        mg_ref[...] = merged
        mix = _dot(merged, wo_ref[...])
        mix_ref[...] = mix
        x1 = x_ref[...] + mix * _rms(mix) * g2_ref[...]
        x1_ref[...] = x1
        h2_ref[...] = (x1 * _rms(x1) * g3_ref[...]).astype(BF16)

    return _call(
        body, name="merge_out", grid=(T // tm,),
        in_specs=[_rows(tm, POOL_WIDTH), _rows(tm, ATTN_WIDTH), _rows(tm, GATE_WIDTH), _rows(tm, D_MODEL),
                  _const(w_bp.shape), _const(w_ba.shape), _const((D_MODEL, D_MODEL)),
                  _const((1, D_MODEL)), _const((1, D_MODEL))],
        out_specs=[_rows(tm, D_MODEL)] * 4,
        out_shape=[_sds((T, D_MODEL), BF16), _sds((T, D_MODEL), F32), _sds((T, D_MODEL), F32),
                   _sds((T, D_MODEL), BF16)],
        args=(y_pool, y_attn, gate, x2, w_bp, w_ba, w_out, g2, g3), sem=("parallel",), exchanges=exchanges)


HALF = D_MODEL // 2
TM_MLP = 256


def _mlp_core(h2, x1, mix, tgt, w_up, w_down, g4, g3, g2):
    T = h2.shape[0]
    tm = min(TM_MLP, T)

    def body(h_ref, x1_ref, mix_ref, t_ref, g_ref, g3_ref, g2_ref, ua_hbm, ub_hbm, da_hbm, db_hbm,
             act_ref, dff_ref, dup_ref, dx1_ref, dmix_ref, loss_ref, dg_ref, dg3_ref, dg2_ref,
             wu, wd, relu_scr, sems):
        def weight_copy(i):
            src, dst = ((ua_hbm, wu.at[:, :HALF]), (ub_hbm, wu.at[:, HALF:]),
                        (da_hbm, wd.at[:, :HALF]), (db_hbm, wd.at[:, HALF:]))[i]
            return pltpu.make_async_copy(src, dst, sems.at[i])

        @pl.when(pl.program_id(0) == 0)
        def _():
            for i in range(4):
                weight_copy(i).start()
            loss_ref[...] = jnp.zeros_like(loss_ref)
            for ref in (dg_ref, dg3_ref, dg2_ref):
                ref[...] = jnp.zeros_like(ref)
            weight_copy(0).wait()
            weight_copy(1).wait()

        h = h_ref[...]
        ff = None
        for j in range(N_CHIPS):
            lo = D_MODEL * j
            relu = jnp.maximum(_dot(h, wu[j]), 0.0)
            if j == 0:
                @pl.when(pl.program_id(0) == 0)
                def _():
                    weight_copy(2).wait()
                    weight_copy(3).wait()
            relu_scr[:, lo:lo + D_MODEL] = relu
            act = jnp.square(relu).astype(BF16)
            act_ref[:, lo:lo + D_MODEL] = act
            t = _dot(act, wd[j])
            ff = t if ff is None else ff + t
        g = g_ref[...]
        x1 = x1_ref[...]
        err = x1 + ff * _rms(ff) * g - t_ref[...]
        loss_ref[...] += jnp.sum(err * err) * (0.5 / D_MODEL)
        dy = err * (1.0 / D_MODEL)
        dff, dg = _norm_bwd(ff, g, dy)
        dg_ref[...] += dg
        dff = dff.astype(BF16)
        dff_ref[...] = dff
        dh2 = None
        for j in range(N_CHIPS):
            lo = D_MODEL * j
            dup = (_dot_nt(dff, wd[j]) * (2.0 * relu_scr[:, lo:lo + D_MODEL])).astype(BF16)
            dup_ref[:, lo:lo + D_MODEL] = dup
            t = _dot_nt(dup, wu[j])
            dh2 = t if dh2 is None else dh2 + t
        dx, dg3 = _norm_bwd(x1, g3_ref[...], dh2)
        dx1 = dy + dx
        dx1_ref[...] = dx1
        dg3_ref[...] += dg3
        dmix, dg2 = _norm_bwd(mix_ref[...], g2_ref[...], dx1)
        dmix_ref[...] = dmix.astype(BF16)
        dg2_ref[...] += dg2

    slabs = pltpu.VMEM((N_CHIPS, D_MODEL, D_MODEL), BF16)
    gain = _const((1, D_MODEL))
    return pl.pallas_call(
        body, name="mlp_core", grid=(T // tm,),
        in_specs=[_rows(tm, D_MODEL)] * 4 + [gain] * 3 + [ANY] * 4,
        out_specs=[_rows(tm, D_FF), _rows(tm, D_MODEL), _rows(tm, D_FF), _rows(tm, D_MODEL), _rows(tm, D_MODEL),
                   _const((8, LANES)), gain, gain, gain],
        out_shape=[_sds((T, D_FF), BF16), _sds((T, D_MODEL), BF16), _sds((T, D_FF), BF16), _sds((T, D_MODEL), F32),
                   _sds((T, D_MODEL), BF16), _sds((8, LANES), F32)] + [_sds((1, D_MODEL), F32)] * 3,
        scratch_shapes=[slabs] * 2 + [pltpu.VMEM((tm, D_FF), F32), pltpu.SemaphoreType.DMA((4,))],
        compiler_params=_cp("arbitrary"),
    )(h2, x1, mix, tgt, g4, g3, g2, *w_up, *w_down)


def _dw(tag, a, g, ta, tn, shard_cols=False, exchanges=()):
    T, ka = a.shape
    n = g.shape[1]
    tk = min(2 * TM, T)
    nk = T // tk

    def body(a_ref, g_ref, o_ref):
        @pl.when(pl.program_id(2) == 0)
        def _():
            o_ref[...] = jnp.zeros_like(o_ref)

        o_ref[...] += _dot_tn(a_ref[...], g_ref[...])

    if shard_cols:
        per = (n // N_CHIPS) // tn
        out_spec = pl.BlockSpec((None, ta, tn), lambda i, j, k: (j // per, i, j % per))
        out_shape = _sds((N_CHIPS, ka, n // N_CHIPS), F32)
    else:
        out_spec = pl.BlockSpec((ta, tn), lambda i, j, k: (i, j))
        out_shape = _sds((ka, n), F32)
    return _call(
        body, name="dw_" + tag, grid=(ka // ta, n // tn, nk),
        in_specs=[pl.BlockSpec((tk, ta), lambda i, j, k: (k, i)), pl.BlockSpec((tk, tn), lambda i, j, k: (k, j))],
        out_specs=[out_spec], out_shape=[out_shape],
        args=(a, g), sem=("parallel", "parallel", "arbitrary"), exchanges=exchanges)


def _dw_mix(merged, dmix, y_pool, dbp, y_attn, dba, exchanges=()):
    T = merged.shape[0]
    tk = min(2 * TM, T)
    c = D_MODEL // N_CHIPS

    def body(mg_ref, dmix_ref, yp_ref, dbp_ref, ya_ref, dba_ref, out_ref, bp_ref, ba_ref):
        @pl.when(pl.program_id(0) == 0)
        def _():
            for ref in (out_ref, bp_ref, ba_ref):
                ref[...] = jnp.zeros_like(ref)

        out_ref[...] += _dot_tn(mg_ref[...], dmix_ref[...])
        for y_ref, d_ref, o_ref in ((yp_ref, dbp_ref, bp_ref), (ya_ref, dba_ref, ba_ref)):
            res = _dot_tn(y_ref[...], d_ref[...])
            for j in range(N_CHIPS):
                o_ref[j] += res[:, c * j:c * (j + 1)]

    slabs = (N_CHIPS, POOL_WIDTH, c)
    return _call(
        body, name="dw_mix", grid=(T // tk,),
        in_specs=[_rows(tk, D_MODEL), _rows(tk, D_MODEL), _rows(tk, POOL_WIDTH), _rows(tk, D_MODEL),
                  _rows(tk, ATTN_WIDTH), _rows(tk, D_MODEL)],
        out_specs=[_const((D_MODEL, D_MODEL)), _const(slabs), _const(slabs)],
        out_shape=[_sds((D_MODEL, D_MODEL), F32), _sds(slabs, F32), _sds(slabs, F32)],
        args=(merged, dmix, y_pool, dbp, y_attn, dba), sem=("arbitrary",), exchanges=exchanges)


def _merge_bwd(dmix, gate, y_pool, y_attn, w_out, w_bp, w_ba, exchanges=()):
    T = dmix.shape[0]
    tm = min(TM, T)

    def body(dmix_ref, gate_ref, yp_ref, ya_ref, wo_ref, wbp_ref, wba_ref,
             dbp_ref, dba_ref, dgate_ref, dyp_ref, dya_ref):
        dm = _dot_nt(dmix_ref[...], wo_ref[...])
        for j, (y_ref, db_ref, w_ref, dy_ref) in enumerate(
                ((yp_ref, dbp_ref, wbp_ref, dyp_ref), (ya_ref, dba_ref, wba_ref, dya_ref))):
            sl = slice(D_MODEL * j, D_MODEL * (j + 1))
            gt = gate_ref[:, sl].astype(F32)
            db = (dm * gt).astype(BF16)
            db_ref[...] = db
            dgate_ref[:, sl] = (dm * _branch(y_ref[...], w_ref) * gt * (1.0 - gt)).astype(BF16)
            cw = D_MODEL // N_CHIPS
            dy = _dot_nt(db[:, :cw], w_ref[0])
            for c in range(1, N_CHIPS):
                dy = dy + _dot_nt(db[:, cw * c:cw * (c + 1)], w_ref[c])
            dy_ref[...] = dy.astype(dy_ref.dtype)

    return _call(
        body, name="merge_bwd", grid=(T // tm,),
        in_specs=[_rows(tm, D_MODEL), _rows(tm, GATE_WIDTH), _rows(tm, POOL_WIDTH), _rows(tm, ATTN_WIDTH),
                  _const((D_MODEL, D_MODEL)), _const(w_bp.shape), _const(w_ba.shape)],
        out_specs=[_rows(tm, D_MODEL), _rows(tm, D_MODEL), _rows(tm, GATE_WIDTH), _rows(tm, POOL_WIDTH),
                   _rows(tm, ATTN_WIDTH)],
        out_shape=[_sds((T, D_MODEL), BF16), _sds((T, D_MODEL), BF16), _sds((T, GATE_WIDTH), BF16),
                   _sds((T, POOL_WIDTH), F32), _sds((T, ATTN_WIDTH), BF16)],
        args=(dmix, gate, y_pool, y_attn, w_out, w_bp, w_ba), sem=("parallel",), exchanges=exchanges)


def _attn_bwd(q, k, v, do, sinks, tabs, seq, exchanges=()):
    T = q.shape[0]
    nb = seq // BLOCK
    bl = T // seq
    steps = nb + 1

    def body(sink_ref, q_ref, do_ref, kp_ref, kc_ref, vp_ref, vc_ref, c_ref, a_ref, bt_ref, cp_ref, ap_ref, btp_ref,
             dq_ref, dk_ref, dv_ref, dsink_ref, ck_ref, cv_ref):
        n = pl.program_id(0)

        @pl.when(n == 0)
        def _():
            dsink_ref[...] = jnp.zeros_like(dsink_ref)
            ck_ref[...] = jnp.zeros_like(ck_ref)
            cv_ref[...] = jnp.zeros_like(cv_ref)

        @pl.when(n < nb)
        def _():
            valid, lo = _attn_masks(n)
            for b in range(bl):
                kk = jnp.concatenate([kp_ref[b], kc_ref[b]], axis=0)
                vv = jnp.concatenate([vp_ref[b], vc_ref[b]], axis=0)
                dk_acc = jnp.zeros((2 * BLOCK, KV_WIDTH), F32)
                dv_acc = jnp.zeros((2 * BLOCK, KV_WIDTH), F32)
                for h in range(2):
                    qs = _stack_heads(q_ref.at[b], h, lo)
                    dos = _stack_heads(do_ref.at[b], h, lo)
                    pr, ps = _group_probs(qs, kk, valid, _sink_rows(sink_ref, h))
                    dp = _dot_nt(dos, vv)
                    delta = jnp.sum(pr * dp, axis=1, keepdims=True)
                    ds = (pr * (dp - delta)).astype(BF16)
                    dsk = ps * delta
                    for g in range(GROUP):
                        idx = GROUP * h + g
                        dsink_ref[idx:idx + 1, :] += (jnp.zeros((1, LANES), F32)
                                                      - jnp.sum(dsk[BLOCK * g:BLOCK * (g + 1)]))
                    dk_acc = dk_acc + _dot_tn(ds, qs)
                    dv_acc = dv_acc + _dot_tn(pr.astype(BF16), dos)
                    for j, pair in enumerate(_unstack_heads(_dot(ds, kk) * SCALE, h, lo)):
                        sl = slice(LANES * (2 * h + j), LANES * (2 * h + j + 1))
                        dq_ref[b, :, sl] = _rot_bwd(pair, c_ref[...], a_ref[...], bt_ref[...]).astype(BF16)
                fin_k = ck_ref[b] + dk_acc[:BLOCK]
                dk_ref[b] = _rot_bwd(fin_k, cp_ref[...], ap_ref[...], btp_ref[...]).astype(BF16)
                dv_ref[b] = (cv_ref[b] + dv_acc[:BLOCK]).astype(BF16)
                ck_ref[b] = dk_acc[BLOCK:]
                cv_ref[b] = dv_acc[BLOCK:]

        @pl.when(n == nb)
        def _():
            for b in range(bl):
                dk_ref[b] = _rot_bwd(ck_ref[b], cp_ref[...], ap_ref[...], btp_ref[...]).astype(BF16)
                dv_ref[b] = cv_ref[b].astype(BF16)

    cur = lambda n: (0, jnp.minimum(n, nb - 1), 0)
    prv = lambda n: (0, jnp.clip(n - 1, 0, nb - 1), 0)
    tcur = lambda n: (jnp.minimum(n, nb - 1), 0)
    tprv = lambda n: (jnp.clip(n - 1, 0, nb - 1), 0)
    wide = lambda m: pl.BlockSpec((bl, BLOCK, ATTN_WIDTH), m)
    kv = lambda m: pl.BlockSpec((bl, BLOCK, KV_WIDTH), m)
    tab = lambda m: pl.BlockSpec((BLOCK, LANES), m)
    res = _call(
        body, name="attn_bwd", grid=(steps,),
        in_specs=[pl.BlockSpec(memory_space=pltpu.SMEM), wide(cur), wide(cur), kv(prv), kv(cur), kv(prv), kv(cur),
                  tab(tcur), tab(tcur), tab(tcur), tab(tprv), tab(tprv), tab(tprv)],
        out_specs=[wide(cur), kv(prv), kv(prv), _const((8, LANES))],
        out_shape=[_sds((bl, seq, ATTN_WIDTH), BF16), _sds((bl, seq, KV_WIDTH), BF16),
                   _sds((bl, seq, KV_WIDTH), BF16), _sds((8, LANES), F32)],
        scratch=[pltpu.VMEM((bl, BLOCK, KV_WIDTH), F32), pltpu.VMEM((bl, BLOCK, KV_WIDTH), F32)],
        args=(sinks, *_by_example(bl, q, do, k, k, v, v), *tabs, *tabs), sem=("arbitrary",), exchanges=exchanges)
    outs, rest = (res if exchanges else (res, None))
    outs = [outs[0].reshape(T, ATTN_WIDTH), outs[1].reshape(T, KV_WIDTH), outs[2].reshape(T, KV_WIDTH), outs[3]]
    return (outs, rest) if exchanges else outs


def _pool_bwd(dyp, diff, w_pool, pool_scale, seq, exchanges=()):
    T = dyp.shape[0]
    tp = min(TP, seq)
    nseq = seq // tp
    per = tp // HALO
    last_halo = T // HALO - 1

    def body(dy_ref, nxt_ref, diff_ref, w_ref, s_ref, du_ref, dw_ref, ds_ref):
        i = pl.program_id(0)

        @pl.when(i == 0)
        def _():
            dw_ref[...] = jnp.zeros_like(dw_ref)
            ds_ref[...] = jnp.zeros_like(ds_ref)

        last = (i % nseq) == nseq - 1
        nxt = jnp.where(last, 0.0, nxt_ref[...])
        ext = jnp.concatenate([dy_ref[...], nxt], axis=0) * s_ref[...]
        pos = (i % nseq) * tp + lax.broadcasted_iota(jnp.int32, (tp + HALO, 1), 0)
        for gi, w in enumerate(POOL_WINDOWS):
            sl = slice(POOL_GC * gi, POOL_GC * (gi + 1))
            wg = w_ref[gi].astype(BF16)
            dmx = ext[:, sl].astype(BF16)
            ddiff = _dot_nt(dmx, wg)
            s = ddiff * _inv_count(pos, w)
            sh = 1
            while sh < w:
                s = s + pltpu.roll(s, tp + HALO - sh, 0)
                sh *= 2
            du_ref[:, sl] = (s[:tp] - ddiff[:tp]).astype(BF16)
            dg = diff_ref[:, sl]
            dw_ref[gi] += _dot_tn(dg, dmx[:tp])
            ds_ref[:, sl] += jnp.sum(dy_ref[:, sl] * _dot(dg, wg), axis=0, keepdims=True)

    return _call(
        body, name="pool_bwd", grid=(T // tp,),
        in_specs=[_rows(tp, POOL_WIDTH),
                  pl.BlockSpec((HALO, POOL_WIDTH), lambda i: (jnp.minimum((i + 1) * per, last_halo), 0)),
                  _rows(tp, POOL_WIDTH), _const((4, POOL_GC, POOL_GC)), _const((1, POOL_WIDTH))],
        out_specs=[_rows(tp, POOL_WIDTH), _const((4, POOL_GC, POOL_GC)), _const((1, POOL_WIDTH))],
        out_shape=[_sds((T, POOL_WIDTH), BF16), _sds((4, POOL_GC, POOL_GC), F32), _sds((1, POOL_WIDTH), F32)],
        args=(dyp, dyp, diff, w_pool, pool_scale), sem=("arbitrary",), exchanges=exchanges)


_PARTS = ((0, C_Q), (C_Q, C_K), (C_K, C_V), (C_V, C_G), (C_G, IN_WIDTH))


def _inproj_bwd(parts, x2, dx1, w_in_t, g1, exchanges=()):
    T = x2.shape[0]
    tm = min(TM, T)

    def body(du_ref, dq_ref, dk_ref, dv_ref, dgt_ref, x_ref, dx1_ref, w_ref, g_ref, gx_ref, dg_ref):
        @pl.when(pl.program_id(0) == 0)
        def _():
            dg_ref[...] = jnp.zeros_like(dg_ref)

        dh = jnp.zeros((tm, D_MODEL), F32)
        for (lo, hi), p_ref in zip(_PARTS, (du_ref, dq_ref, dk_ref, dv_ref, dgt_ref)):
            dh = dh + _dot(p_ref[...], w_ref[lo:hi, :])
        dx, dg = _norm_bwd(x_ref[...], g_ref[...], dh)
        gx_ref[...] = dx1_ref[...] + dx
        dg_ref[...] += dg

    return _call(
        body, name="inproj_bwd", grid=(T // tm,),
        in_specs=[_rows(tm, hi - lo) for lo, hi in _PARTS]
        + [_rows(tm, D_MODEL), _rows(tm, D_MODEL), _const((IN_WIDTH, D_MODEL)), _const((1, D_MODEL))],
        out_specs=[_rows(tm, D_MODEL), _const((1, D_MODEL))],
        out_shape=[_sds((T, D_MODEL), F32), _sds((1, D_MODEL), F32)],
        args=(*parts, x2, dx1, w_in_t, g1), sem=("arbitrary",), exchanges=exchanges)


def _dw_in(h, parts, exchanges=()):
    T = h.shape[0]
    tk = min(TM, T)

    def body(h_ref, du_ref, dq_ref, dk_ref, dv_ref, dgt_ref, o_ref, db_ref):
        @pl.when(pl.program_id(0) == 0)
        def _():
            o_ref[...] = jnp.zeros_like(o_ref)
            db_ref[...] = jnp.zeros_like(db_ref)

        hh = h_ref[...]
        for (lo, hi), p_ref in zip(_PARTS, (du_ref, dq_ref, dk_ref, dv_ref, dgt_ref)):
            part = p_ref[...]
            o_ref[lo:hi, :] += _dot_tn(part, hh)
            db_ref[:, lo:hi] += jnp.sum(part.astype(F32), axis=0, keepdims=True)

    return _call(
        body, name="dw_in", grid=(T // tk,),
        in_specs=[_rows(tk, D_MODEL)] + [_rows(tk, hi - lo) for lo, hi in _PARTS],
        out_specs=[_const((IN_WIDTH, D_MODEL)), _const((1, IN_WIDTH))],
        out_shape=[_sds((IN_WIDTH, D_MODEL), F32), _sds((1, IN_WIDTH), F32)],
        args=(h, *parts), sem=("arbitrary",), exchanges=exchanges)


def _row_tile(rows, cap=256, mult=16):
    best = None
    for t in range(mult, min(rows, cap) + 1, mult):
        if rows % t == 0:
            best = t
    if best is None:
        raise ValueError("no row tile for %d rows" % rows)
    return best


def _pair_sum(ids, full, got):
    _, r, c = full.shape
    hr = r // 2
    tr = _row_tile(hr)
    nblk = hr // tr

    def body(ids_ref, a_ref, b_ref, own_ref, sb_ref):
        s = a_ref[...] + b_ref[...]
        sb_ref[...] = s.astype(BF16)

        @pl.when(pl.program_id(1) == ids_ref[0])
        def _():
            own_ref[...] = s

    slab = pl.BlockSpec((None, tr, c), lambda i, j, ids_ref: (j, i, 0))
    return pl.pallas_call(
        body, name="pair_sum_%dx%d" % (r, c),
        grid_spec=pltpu.PrefetchScalarGridSpec(
            num_scalar_prefetch=1, grid=(nblk, N_CHIPS),
            in_specs=[pl.BlockSpec((None, tr, c), lambda i, j, ids_ref: (j, ids_ref[1] * nblk + i, 0)), slab],
            out_specs=[pl.BlockSpec((tr, c), lambda i, j, ids_ref: (i, 0)), slab]),
        out_shape=[_sds((hr, c), F32), _sds((N_CHIPS, hr, c), BF16)],
        compiler_params=_cp("parallel", "arbitrary"),
    )(ids, full, got)


def _chip_sum(ids, own, got):
    hr, c = own.shape
    tr = _row_tile(hr)
    nblk = hr // tr

    def body(ids_ref, a_ref, b_ref, o_ref):
        o_ref[...] = ((a_ref[...] + b_ref[0].astype(F32)) + b_ref[1].astype(F32)) + b_ref[2].astype(F32)

    return pl.pallas_call(
        body, name="chip_sum_%dx%d" % (hr, c),
        grid_spec=pltpu.PrefetchScalarGridSpec(
            num_scalar_prefetch=1, grid=(nblk,),
            in_specs=[pl.BlockSpec((tr, c), lambda i, ids_ref: (i, 0)),
                      pl.BlockSpec((3, tr, c), lambda i, ids_ref: (0, i, 0))],
            out_specs=pl.BlockSpec((tr, c), lambda i, ids_ref: (ids_ref[1] * nblk + i, 0))),
        out_shape=_sds((2 * hr, c), F32),
        compiler_params=_cp("parallel"),
    )(ids, own, got)


def _adamw_math(w, g, m, v):
    nm = ADAM_B1 * m + (1.0 - ADAM_B1) * g
    nv = ADAM_B2 * v + (1.0 - ADAM_B2) * (g * g)
    m_hat = nm / (1.0 - ADAM_B1 ** ADAM_STEP)
    v_hat = nv / (1.0 - ADAM_B2 ** ADAM_STEP)
    return -ADAM_LR * (m_hat / (jnp.sqrt(v_hat) + ADAM_EPS) + ADAM_WD * w), nm, nv


def _adamw(w, g, m, v):
    r, c = w.shape
    tr = _row_tile(r, cap=512, mult=8)

    def body(w_ref, g_ref, m_ref, v_ref, d_ref, nm_ref, nv_ref):
        d_ref[...], nm_ref[...], nv_ref[...] = _adamw_math(w_ref[...], g_ref[...], m_ref[...], v_ref[...])

    spec = _rows(tr, c)
    return pl.pallas_call(
        body, name="adamw_%dx%d" % (r, c), grid=(r // tr,),
        in_specs=[spec] * 4, out_specs=[spec] * 3, out_shape=[_sds((r, c), F32)] * 3,
        compiler_params=_cp("parallel"),
    )(w, g, m, v)


SC_TILES = 32
SC_LANES = 16
SC_ROWS = 8


def _sparse_mesh():
    return plsc.VectorSubcoreMesh(core_axis_name="sc_core", subcore_axis_name="sc_subcore")


def _sparse_tile():
    return lax.axis_index("sc_subcore") * 2 + lax.axis_index("sc_core")


def _adamw_sparse(w, g, m, v):
    r, c = w.shape
    rows = r // SC_TILES
    step = min(rows, SC_ROWS)

    def body(w_hbm, g_hbm, m_hbm, v_hbm, d_hbm, nm_hbm, nv_hbm, g_out, wb, gb, mb, vb):
        tile = _sparse_tile()

        @pl.loop(0, rows, step=step)
        def _(r0):
            mine = pl.ds(tile * rows + r0, step)
            for src, dst in ((w_hbm, wb), (g_hbm, gb), (m_hbm, mb), (v_hbm, vb)):
                pltpu.sync_copy(src.at[mine], dst)

            @pl.loop(0, step)
            def _(row):
                @pl.loop(0, c, step=SC_LANES)
                def _(i):
                    at = (row, pl.ds(i, SC_LANES))
                    wb[at], mb[at], vb[at] = _adamw_math(wb[at], gb[at], mb[at], vb[at])

            for src, dst in ((wb, d_hbm), (mb, nm_hbm), (vb, nv_hbm), (gb, g_out)):
                pltpu.sync_copy(src, dst.at[mine])

    return pl.kernel(
        body, name="adamw_sparse_%dx%d" % (r, c), out_type=[_sds((r, c), F32)] * 4, mesh=_sparse_mesh(),
        scratch_types=[pltpu.VMEM((step, c), F32)] * 4,
    )(w, g, m, v)


def _copy_sparse(a):
    r, c = a.shape
    rows = r // SC_TILES

    def body(a_hbm, o_hbm, buf):
        tile = _sparse_tile()

        @pl.loop(0, rows, step=SC_ROWS)
        def _(r0):
            mine = pl.ds(tile * rows + r0, SC_ROWS)
            pltpu.sync_copy(a_hbm.at[mine], buf)
            pltpu.sync_copy(buf, o_hbm.at[mine])

    return pl.kernel(
        body, name="copy_sparse_%dx%d" % (r, c), out_type=_sds((r, c), F32), mesh=_sparse_mesh(),
        scratch_types=[pltpu.VMEM((SC_ROWS, c), F32)],
    )(a)


_SMALL_NAMES = ("w_pool", "b_in", "g_mix_pre", "g_mix_post", "g_mlp_pre", "g_mlp_post", "pool_scale", "attn_sinks")
B_ROWS = -(-IN_WIDTH // D_MODEL)


def _row_block(rows):
    rows = [jnp.pad(r.astype(F32), ((0, 0), (0, D_MODEL - r.shape[1]))) for r in rows]
    return jnp.pad(jnp.concatenate(rows, axis=0), ((0, 8 - len(rows)), (0, 0)))


def _early_block(dg2, dg3, dg4, dps, dsink, loss):
    tail = jnp.concatenate([jnp.pad(dsink.reshape(1, -1), ((0, 0), (0, LANES - dsink.size))),
                            jnp.pad(loss.reshape(1, 1), ((0, 0), (0, LANES - 1)))], axis=1)
    return _row_block([dg2, dg3, dg4, dps, tail])


def _late_block(db_in, dg1):
    b = jnp.pad(db_in, ((0, 0), (0, B_ROWS * D_MODEL - IN_WIDTH))).reshape(B_ROWS, D_MODEL)
    return _row_block([b[r:r + 1] for r in range(B_ROWS)] + [dg1])


def _small_update(gearly, gmat, glate, w, m, v):
    names = _SMALL_NAMES
    n = len(names)

    def total(ref, rows):
        acc = ref[0:rows, :]
        for d in range(1, N_DEV):
            acc = acc + ref[d * rows:(d + 1) * rows, :]
        return acc

    def body(*refs):
        early_ref, gmat_ref, late_ref = refs[:3]
        w_refs, m_refs, v_refs = refs[3:3 + n], refs[3 + n:3 + 2 * n], refs[3 + 2 * n:3 + 3 * n]
        outs = refs[3 + 3 * n:]
        loss_ref, g_refs, d_refs = outs[0], outs[1:1 + n], outs[1 + n:1 + 2 * n]
        nm_refs, nv_refs = outs[1 + 2 * n:1 + 3 * n], outs[1 + 3 * n:1 + 4 * n]
        early, late = total(early_ref, 8), total(late_ref, 8)
        loss_ref[...] = jnp.sum(early[4:5, LANES:2 * LANES], axis=1, keepdims=True)
        bias = jnp.concatenate([late[r:r + 1, :] for r in range(B_ROWS - 1)]
                               + [late[B_ROWS - 1:B_ROWS, :IN_WIDTH - (B_ROWS - 1) * D_MODEL]], axis=1)
        grad = dict(b_in=bias, g_mix_pre=late[B_ROWS:B_ROWS + 1, :], g_mix_post=early[0:1, :],
                    g_mlp_pre=early[1:2, :], g_mlp_post=early[2:3, :], pool_scale=early[3:4, :POOL_WIDTH],
                    attn_sinks=early[4:5, :N_Q_HEADS])
        for i, name in enumerate(names):
            g = total(gmat_ref, 4 * POOL_GC) if name == "w_pool" else grad[name]
            g_refs[i][...] = g
            d_refs[i][...], nm_refs[i][...], nv_refs[i][...] = _adamw_math(
                w_refs[i][...], g, m_refs[i][...], v_refs[i][...])

    shapes = [_sds(w[k].shape, F32) for k in names]
    res = pl.pallas_call(
        body, name="small_update", out_shape=[_sds((1, 1), F32)] + shapes * 4,
        compiler_params=pltpu.CompilerParams(vmem_limit_bytes=VMEM_MB * 1024 * 1024),
    )(gearly, gmat, glate, *[w[k] for k in names], *[m[k] for k in names], *[v[k] for k in names])
    loss = res[0]
    per = {k: tuple(res[1 + j * n + i] for j in range(4)) for i, k in enumerate(names)}
    return loss, per


_BIG = ("w_in", "w_branch_pool", "w_branch_attn", "w_out", "w_up", "w_down")
_ORDER = ("g_mix_pre", "w_in", "b_in", "w_pool", "pool_scale", "attn_sinks", "w_branch_pool", "w_branch_attn",
          "w_out", "g_mix_post", "g_mlp_pre", "w_up", "w_down", "g_mlp_post")


def _stack_rows(slab):
    return slab.reshape(-1, slab.shape[2])


def _step(x2, tgt, seq, shards, small, ids):
    tabs = _rope_tables(seq)
    g1, g2, g3, g4 = (small[n] for n in ("g_mix_pre", "g_mix_post", "g_mlp_pre", "g_mlp_post"))
    sinks = small["attn_sinks"].reshape(N_Q_HEADS)
    w_pool = small["w_pool"].reshape(4, POOL_GC, POOL_GC)
    pool_scale = small["pool_scale"]

    def whole(shard, slabs):
        return lax.dynamic_update_slice(slabs, shard[None], (ids[0], 0, 0))

    up_a, up_b = shards["w_up"][:HALF], shards["w_up"][HALF:]
    down_a, down_b = shards["w_down"][:HALF], shards["w_down"][HALF:]
    w_in = _stack_rows(whole(shards["w_in"], _alone("gather_in", _ex_gather([shards["w_in"]]))[0][0]))
    mix_shards = [shards[n] for n in ("w_branch_pool", "w_branch_attn", "w_out")]
    (h, u, q, k, v, gate), [(*mix_slabs, got_c)] = _inproj(
        x2, g1, w_in, small["b_in"], tabs, seq, exchanges=[_ex_gather(mix_shards + [down_a])])
    w_bp, w_ba, out_slab = (whole(s, g) for s, g in zip(mix_shards, mix_slabs))
    w_out = _stack_rows(out_slab)
    diff, y_pool = _pool_fwd(u, w_pool, pool_scale, seq)
    (y_attn,), [[got_a]] = _attn_fwd(q, k, v, sinks, seq, exchanges=[_ex_gather([up_a])])
    (merged, mix, x1, h2), [[got_b, got_d]] = _merge_out(
        y_pool, y_attn, gate, x2, w_bp, w_ba, w_out, g2, g3, exchanges=[_ex_gather([up_b, down_b])])
    w_up = (whole(up_a, got_a), whole(up_b, got_b))
    w_down = (whole(down_a, got_c), whole(down_b, got_d))
    act, dff, dup, dx1, dmix, loss_acc, dg4, dg3, dg2 = _mlp_core(h2, x1, mix, tgt, w_up, w_down, g4, g3, g2)

    dw_down = _dw("down", act, dff, 1024, 1024)[0].reshape(N_CHIPS, D_FF // N_CHIPS, D_MODEL)
    (dbp, dba, dgate, dyp, dya), [[got]] = _merge_bwd(
        dmix, gate, y_pool, y_attn, w_out, w_bp, w_ba, exchanges=[_ex_pair([dw_down])])
    ps_down = _pair_sum(ids, dw_down, got)
    (dw_up,), [[got]] = _dw("up", h2, dup, 1024, 1024, shard_cols=True, exchanges=[_ex_chip([ps_down[1]])])
    half_down = _chip_sum(ids, ps_down[0], got)
    (dw_out, dw_bp, dw_ba), [[got]] = _dw_mix(merged, dmix, y_pool, dbp, y_attn, dba, exchanges=[_ex_pair([dw_up])])
    ps_up = _pair_sum(ids, dw_up, got)
    dw_mix = [dw_out.reshape(N_CHIPS, D_MODEL // N_CHIPS, D_MODEL), dw_bp, dw_ba]
    (dq, dk, dv, dsink), [[got], gots, [g_down]] = _attn_bwd(
        q, k, v, dya, sinks, tabs, seq, exchanges=[_ex_chip([ps_up[1]]), _ex_pair(dw_mix), _ex_swap([half_down])])
    half_up = _chip_sum(ids, ps_up[0], got)
    ps_mix = [_pair_sum(ids, d, g) for d, g in zip(dw_mix, gots)]
    (du, dw_pool, dps), [[g_up]] = _pool_bwd(dyp, diff, w_pool, pool_scale, seq, exchanges=[_ex_swap([half_up])])
    parts = (du, dq, dk, dv, dgate)
    early = _early_block(dg2, dg3, dg4, dps, dsink[:, 0], loss_acc[0, 0])
    mat = dw_pool.reshape(4 * POOL_GC, POOL_GC)
    (dw_in_t, db_in), [gots, [gearly, gmat]] = _dw_in(
        h, parts, exchanges=[_ex_chip([p[1] for p in ps_mix]), _ex_allgather([early, mat])])
    half_mix = [_chip_sum(ids, p[0], g) for p, g in zip(ps_mix, gots)]
    dw_in = dw_in_t.reshape(N_CHIPS, IN_WIDTH // N_CHIPS, D_MODEL)
    g_mix, [got] = _alone("swap_mix_pair_in", _ex_swap(half_mix), _ex_pair([dw_in]))
    ps_in = _pair_sum(ids, dw_in, got)
    (gx, dg1), [[got]] = _inproj_bwd(parts, x2, dx1, w_in, g1, exchanges=[_ex_chip([ps_in[1]])])
    [g_in], [glate] = _alone("swap_in_allgather", _ex_swap([_chip_sum(ids, ps_in[0], got)]),
                             _ex_allgather([_late_block(db_in, dg1)]))

    grads = dict(w_in=g_in, w_branch_pool=g_mix[1], w_branch_attn=g_mix[2], w_out=g_mix[0], w_up=g_up, w_down=g_down)
    return (gearly, gmat, glate), gx, grads


def kernel(x, g_mix_pre, w_in, b_in, w_pool, pool_scale, attn_sinks, w_branch_pool, w_branch_attn, w_out, g_mix_post, g_mlp_pre, w_up, w_down, g_mlp_post, loss_target, m_g_mix_pre, m_w_in, m_b_in, m_w_pool, m_pool_scale, m_attn_sinks, m_w_branch_pool, m_w_branch_attn, m_w_out, m_g_mix_post, m_g_mlp_pre, m_w_up, m_w_down, m_g_mlp_post, v_g_mix_pre, v_w_in, v_b_in, v_w_pool, v_pool_scale, v_attn_sinks, v_w_branch_pool, v_w_branch_attn, v_w_out, v_g_mix_post, v_g_mlp_pre, v_w_up, v_w_down, v_g_mlp_post):
    weights = dict(g_mix_pre=g_mix_pre, w_in=w_in, b_in=b_in, w_pool=w_pool, pool_scale=pool_scale,
                   attn_sinks=attn_sinks, w_branch_pool=w_branch_pool, w_branch_attn=w_branch_attn, w_out=w_out,
                   g_mix_post=g_mix_post, g_mlp_pre=g_mlp_pre, w_up=w_up, w_down=w_down, g_mlp_post=g_mlp_post)
    mom1 = dict(g_mix_pre=m_g_mix_pre, w_in=m_w_in, b_in=m_b_in, w_pool=m_w_pool, pool_scale=m_pool_scale,
                attn_sinks=m_attn_sinks, w_branch_pool=m_w_branch_pool, w_branch_attn=m_w_branch_attn,
                w_out=m_w_out, g_mix_post=m_g_mix_post, g_mlp_pre=m_g_mlp_pre, w_up=m_w_up, w_down=m_w_down,
                g_mlp_post=m_g_mlp_post)
    mom2 = dict(g_mix_pre=v_g_mix_pre, w_in=v_w_in, b_in=v_b_in, w_pool=v_w_pool, pool_scale=v_pool_scale,
                attn_sinks=v_attn_sinks, w_branch_pool=v_w_branch_pool, w_branch_attn=v_w_branch_attn,
                w_out=v_w_out, g_mix_post=v_g_mix_post, g_mlp_pre=v_g_mlp_pre, w_up=v_w_up, w_down=v_w_down,
                g_mlp_post=v_g_mlp_post)
    b_loc, seq, _ = x.shape
    x2 = x.reshape(b_loc * seq, D_MODEL)
    tgt = loss_target.reshape(b_loc * seq, D_MODEL)
    ids = jnp.stack([2 * lax.axis_index("x") + lax.axis_index("y"), lax.axis_index("c")]).astype(jnp.int32)

    def flat(n, a):
        return a[0].T if n == "w_in" else a[0]

    def unflat(n, a):
        return (a.T if n == "w_in" else a)[None]

    shards = {n: flat(n, weights[n]).astype(BF16) for n in _BIG}
    small = {n: weights[n] for n in _ORDER if n not in _BIG}
    (gearly, gmat, glate), gx, grads = _step(x2, tgt, seq, shards, small, ids)

    def two_d(src):
        return {n: src[n].reshape(4 * POOL_GC, POOL_GC) if n == "w_pool" else src[n] for n in _SMALL_NAMES}

    loss, per = _small_update(gearly, gmat, glate, two_d(weights), two_d(mom1), two_d(mom2))
    delta, new_m, new_v = {}, {}, {}
    for n in _SMALL_NAMES:
        grads[n], delta[n], new_m[n], new_v[n] = (a.reshape(weights[n].shape) for a in per[n])
    for n in _BIG:
        if n == "w_in":
            d, nm, nv = _adamw(flat(n, weights[n]), grads[n], flat(n, mom1[n]), flat(n, mom2[n]))
        else:
            d, nm, nv, grads[n] = _adamw_sparse(flat(n, weights[n]), grads[n], flat(n, mom1[n]), flat(n, mom2[n]))
        grads[n] = unflat(n, grads[n])
        delta[n], new_m[n], new_v[n] = unflat(n, d), unflat(n, nm), unflat(n, nv)
    gx = _copy_sparse(gx)

    return (loss[0, 0], gx.reshape(x.shape), *[grads[n] for n in _ORDER], *[delta[n] for n in _ORDER],
            *[new_m[n] for n in _ORDER], *[new_v[n] for n in _ORDER])
```

```python
import jax
import jax.numpy as jnp
from jax import lax
from jax.experimental import pallas as pl
from jax.experimental.pallas import tpu as pltpu
from jax.experimental.pallas import tpu_sc as plsc

F32 = jnp.float32
BF16 = jnp.bfloat16

D_MODEL = 1024
POOL_WINDOWS = (2, 4, 8, 16)
POOL_WIDTH = 512
POOL_GC = 128
HALO = 16
HEAD_DIM = 64
N_Q_HEADS = 8
ATTN_WIDTH = 512
KV_WIDTH = 128
BLOCK = 128
NEG_INF = -1e30
ROPE_THETA = 500000.0
ROT_DIM = 16
GATE_WIDTH = 2048
IN_WIDTH = 3328
D_FF = 4096
EPS = 1e-6
SCALE = HEAD_DIM ** -0.5
C_Q, C_K, C_V, C_G = 512, 1024, 1152, 1280

ADAM_LR, ADAM_B1, ADAM_B2, ADAM_EPS, ADAM_WD, ADAM_STEP = 0.001, 0.9, 0.999, 1e-08, 0.01, 10

N_CHIPS = 4
N_DEV = 8
LANES = 128
TM = 512
TP = 512
VMEM_MB = 56

MESH = pl.DeviceIdType.MESH
ANY = pl.BlockSpec(memory_space=pl.ANY)


def _cp(*sem, vmem=VMEM_MB):
    return pltpu.CompilerParams(dimension_semantics=sem, vmem_limit_bytes=vmem * 1024 * 1024)


def _rows(tile, cols):
    return pl.BlockSpec((tile, cols), lambda i: (i, 0))


def _const(shape):
    nd = len(shape)
    return pl.BlockSpec(shape, lambda i: (0,) * nd)


def _sds(shape, dtype):
    return jax.ShapeDtypeStruct(shape, dtype)


def _dot(a, b):
    return jnp.dot(a, b, preferred_element_type=F32)


def _dot_nt(a, b):
    return lax.dot_general(a, b, (((1,), (1,)), ((), ())), preferred_element_type=F32)


def _dot_tn(a, b):
    return lax.dot_general(a, b, (((0,), (0,)), ((), ())), preferred_element_type=F32)


def _rms(x):
    return lax.rsqrt(jnp.mean(x * x, axis=-1, keepdims=True) + EPS)


def _norm_bwd(x, g, dout):
    r = _rms(x)
    n = x * r
    dn = dout * g
    dx = r * (dn - n * jnp.mean(dn * n, axis=-1, keepdims=True))
    return dx, jnp.sum(dout * n, axis=0, keepdims=True)


def _rot_fwd(t, c, a, bt):
    return t * c + pltpu.roll(t, LANES - 8, 1) * a + pltpu.roll(t, 8, 1) * bt


def _rot_bwd(d, c, a, bt):
    return d * c + pltpu.roll(d * a, 8, 1) + pltpu.roll(d * bt, LANES - 8, 1)


def _rope_tables(seq):
    pos = jnp.arange(seq, dtype=F32)
    inv_freq = ROPE_THETA ** (-jnp.arange(0, ROT_DIM, 2, dtype=F32) / ROT_DIM)
    ang = pos[:, None] * inv_freq[None, :]
    cos, sin = jnp.cos(ang), jnp.sin(ang)
    ones = jnp.ones((seq, HEAD_DIM - ROT_DIM), F32)
    zeros8 = jnp.zeros((seq, 8), F32)
    zrest = jnp.zeros((seq, HEAD_DIM - ROT_DIM), F32)
    c = jnp.concatenate([cos, cos, ones], axis=1)
    a = jnp.concatenate([-sin, zeros8, zrest], axis=1)
    bt = jnp.concatenate([zeros8, sin, zrest], axis=1)
    return tuple(jnp.tile(t, (1, 2)) for t in (c, a, bt))


class _Exchange:
    def __init__(self, inputs, out_shapes, sems, start, finish, aliases=None, middle=None):
        self.inputs, self.out_shapes, self.sems = list(inputs), list(out_shapes), list(sems)
        self.start, self.finish, self.aliases = start, finish, dict(aliases or {})
        self.middle = middle


def _call(body, *, name, grid, in_specs, out_specs, out_shape, args, scratch=(), sem=(), exchanges=()):
    in_specs, out_specs, out_shape, scratch = list(in_specs), list(out_specs), list(out_shape), list(scratch)
    if not exchanges:
        return pl.pallas_call(body, name=name, grid=grid, in_specs=in_specs, out_specs=out_specs,
                              out_shape=out_shape, scratch_shapes=scratch, compiler_params=_cp(*sem))(*args)
    n_in, n_out, n_scr = len(in_specs), len(out_specs), len(scratch)
    x_in = [a for ex in exchanges for a in ex.inputs]
    x_out = [s for ex in exchanges for s in ex.out_shapes]
    x_sem = [s for ex in exchanges for s in ex.sems]
    aliases, i_off, o_off = {}, n_in, n_out
    for ex in exchanges:
        for i, o in ex.aliases.items():
            aliases[i_off + i] = o_off + o
        i_off += len(ex.inputs)
        o_off += len(ex.out_shapes)

    def split(flat):
        out, pos = [], 0
        for ex, n in zip(exchanges, flat[1]):
            out.append(flat[0][pos:pos + n])
            pos += n
        return out

    def carrier(*refs):
        pos = 0
        groups = []
        for n in (n_in, len(x_in), n_out, len(x_out), n_scr, len(x_sem)):
            groups.append(refs[pos:pos + n])
            pos += n
        ins, xin, outs, xout, scr, xsem = groups
        xin = split((xin, [len(ex.inputs) for ex in exchanges]))
        xout = split((xout, [len(ex.out_shapes) for ex in exchanges]))
        xsem = split((xsem, [len(ex.sems) for ex in exchanges]))
        first = pl.program_id(0) == 0
        last = pl.program_id(0) == grid[0] - 1
        for d in range(1, len(grid)):
            first = jnp.logical_and(first, pl.program_id(d) == 0)
            last = jnp.logical_and(last, pl.program_id(d) == grid[d] - 1)

        @pl.when(first)
        def _():
            for ex, i, o, s in zip(exchanges, xin, xout, xsem):
                ex.start(i, o, s)

        if any(ex.middle for ex in exchanges):
            half = pl.program_id(0) == grid[0] // 2
            for d in range(1, len(grid)):
                half = jnp.logical_and(half, pl.program_id(d) == 0)

            @pl.when(half)
            def _():
                for ex, i, o, s in zip(exchanges, xin, xout, xsem):
                    if ex.middle:
                        ex.middle(i, o, s)

        body(*ins, *outs, *scr)

        @pl.when(last)
        def _():
            for ex, i, o, s in zip(exchanges, xin, xout, xsem):
                ex.finish(i, o, s)

    res = pl.pallas_call(
        carrier, name=name, grid=grid, in_specs=in_specs + [ANY] * len(x_in),
        out_specs=out_specs + [ANY] * len(x_out), out_shape=out_shape + x_out,
        scratch_shapes=scratch + x_sem, input_output_aliases=aliases,
        compiler_params=_cp(*(["arbitrary"] * len(grid))),
    )(*args, *x_in)
    return res[:n_out], split((res[n_out:], [len(ex.out_shapes) for ex in exchanges]))


def _alone(name, *exchanges):
    n_in = [len(ex.inputs) for ex in exchanges]
    n_out = [len(ex.out_shapes) for ex in exchanges]
    n_sem = [len(ex.sems) for ex in exchanges]
    aliases, i_off, o_off = {}, 0, 0
    for ex in exchanges:
        for i, o in ex.aliases.items():
            aliases[i_off + i] = o_off + o
        i_off += len(ex.inputs)
        o_off += len(ex.out_shapes)

    def split(flat, counts):
        out, pos = [], 0
        for n in counts:
            out.append(flat[pos:pos + n])
            pos += n
        return out

    def body(*refs):
        ins, outs, sems = split(refs, [sum(n_in), sum(n_out), sum(n_sem)])
        groups = list(zip(exchanges, split(ins, n_in), split(outs, n_out), split(sems, n_sem)))
        for ex, i, o, s in groups:
            ex.start(i, o, s)
        for ex, i, o, s in groups:
            if ex.middle:
                ex.middle(i, o, s)
        for ex, i, o, s in groups:
            ex.finish(i, o, s)

    res = pl.pallas_call(
        body, name=name, in_specs=[ANY] * sum(n_in), out_specs=[ANY] * sum(n_out),
        out_shape=[s for ex in exchanges for s in ex.out_shapes],
        scratch_shapes=[s for ex in exchanges for s in ex.sems], input_output_aliases=aliases,
    )(*[a for ex in exchanges for a in ex.inputs])
    return split(res, n_out)


def _place():
    x, y, c = lax.axis_index("x"), lax.axis_index("y"), lax.axis_index("c")
    chips = [(1 - x, y), (x, 1 - y), (1 - x, 1 - y)]
    return x, y, c, chips


def _remote(src, dst, send, recv, to):
    return pltpu.make_async_remote_copy(src_ref=src, dst_ref=dst, send_sem=send, recv_sem=recv,
                                        device_id=to, device_id_type=MESH)


def _ex_gather(shards):
    nw = len(shards)
    hrs = [s.shape[0] // 2 for s in shards]

    def copies(ins, outs, sems):
        s1, r1, s2, r2, fs, fr = sems
        x, y, c, _ = _place()
        me, xn, yn, dg = (x, y), (1 - x, y), (x, 1 - y), (1 - x, 1 - y)
        nbr = (xn, yn)
        sibling = (x, y, 1 - c)

        def piece(w, chip, core, part=None):
            hr = hrs[w]
            rows = pl.ds(core * hr, hr) if part is None else pl.ds(core * hr + part * (hr // 2), hr // 2)
            return outs[w].at[2 * chip[0] + chip[1], rows]

        def first(w, k):
            return _remote(ins[w].at[pl.ds(c * hrs[w], hrs[w])], piece(w, me, c), s1.at[w, k], r1.at[w, k],
                           (*nbr[k], c))

        def landed(w, k):
            return _remote(piece(w, nbr[k], c), piece(w, nbr[k], c), s1.at[w, k], r1.at[w, k], (*nbr[k], c))

        def onward(w, k):
            return _remote(piece(w, nbr[k], c, k), piece(w, nbr[k], c, k), s2.at[w, k], r2.at[w, k],
                           (*nbr[1 - k], c))

        def arrived(w, k):
            return _remote(piece(w, dg, c, k), piece(w, dg, c, k), s2.at[w, k], r2.at[w, k], (*nbr[1 - k], c))

        def passed(w, j):
            chip = (xn, yn, dg)[j]
            return _remote(piece(w, chip, c), piece(w, chip, c), fs.at[w, j], fr.at[w, j], sibling)

        def handed(w, j):
            chip = (xn, yn, dg)[j]
            return _remote(piece(w, chip, 1 - c), piece(w, chip, 1 - c), fs.at[w, j], fr.at[w, j], sibling)

        return first, landed, onward, arrived, passed, handed

    def start(ins, outs, sems):
        first = copies(ins, outs, sems)[0]
        for w in range(nw):
            for k in range(2):
                first(w, k).start()

    def middle(ins, outs, sems):
        _, landed, onward, _, passed, _ = copies(ins, outs, sems)
        for w in range(nw):
            for k in range(2):
                landed(w, k).wait_recv()
                onward(w, k).start()
                passed(w, k).start()

    def finish(ins, outs, sems):
        first, _, onward, arrived, passed, handed = copies(ins, outs, sems)
        for w in range(nw):
            for k in range(2):
                arrived(w, k).wait_recv()
            passed(w, 2).start()
        for w in range(nw):
            for j in range(3):
                handed(w, j).wait_recv()
        for w in range(nw):
            for k in range(2):
                first(w, k).wait_send()
                onward(w, k).wait_send()
            for j in range(3):
                passed(w, j).wait_send()

    return _Exchange(shards, [_sds((N_CHIPS,) + s.shape, s.dtype) for s in shards],
                     [pltpu.SemaphoreType.DMA((nw, 2))] * 4 + [pltpu.SemaphoreType.DMA((nw, 3))] * 2,
                     start, finish, middle=middle)


def _ex_pair(grads):
    nw = len(grads)

    def copies(ins, outs, sems):
        x, y, c, _ = _place()
        out = []
        for w in range(nw):
            hr = grads[w].shape[1] // 2
            out.append(_remote(ins[w].at[:, pl.ds((1 - c) * hr, hr)], outs[w], sems[0].at[w], sems[1].at[w],
                               (x, y, 1 - c)))
        return out

    def start(ins, outs, sems):
        for cp in copies(ins, outs, sems):
            cp.start()

    def finish(ins, outs, sems):
        for cp in copies(ins, outs, sems):
            cp.wait()

    return _Exchange(grads, [_sds((N_CHIPS, g.shape[1] // 2, g.shape[2]), F32) for g in grads],
                     [pltpu.SemaphoreType.DMA((nw,))] * 2, start, finish)


def _ex_chip(pieces):
    nw = len(pieces)

    def copies(ins, outs, sems):
        x, y, c, chips = _place()
        return [_remote(ins[w].at[2 * cx + cy], outs[w].at[k], sems[0].at[w, k], sems[1].at[w, k], (cx, cy, c))
                for w in range(nw) for k, (cx, cy) in enumerate(chips)]

    def start(ins, outs, sems):
        for cp in copies(ins, outs, sems):
            cp.start()

    def finish(ins, outs, sems):
        for cp in copies(ins, outs, sems):
            cp.wait()

    return _Exchange(pieces, [_sds((3,) + p.shape[1:], BF16) for p in pieces],
                     [pltpu.SemaphoreType.DMA((nw, 3))] * 2, start, finish)


def _ex_swap(fulls):
    nw = len(fulls)

    def start(ins, outs, sems):
        x, y, c, _ = _place()
        for w in range(nw):
            hr = fulls[w].shape[0] // 2
            mine = pl.ds(c * hr, hr)
            _remote(ins[w].at[mine], outs[w].at[mine], sems[0].at[w], sems[1].at[w], (x, y, 1 - c)).start()

    def finish(ins, outs, sems):
        x, y, c, _ = _place()
        for w in range(nw):
            hr = fulls[w].shape[0] // 2
            mine, theirs = pl.ds(c * hr, hr), pl.ds((1 - c) * hr, hr)
            _remote(ins[w].at[mine], outs[w].at[mine], sems[0].at[w], sems[1].at[w], (x, y, 1 - c)).wait_send()
            _remote(ins[w].at[theirs], outs[w].at[theirs], sems[0].at[w], sems[1].at[w], (x, y, 1 - c)).wait_recv()

    return _Exchange(fulls, [_sds(f.shape, F32) for f in fulls], [pltpu.SemaphoreType.DMA((nw,))] * 2,
                     start, finish, aliases={w: w for w in range(nw)})


def _ex_allgather(blocks):
    nb = len(blocks)

    def copies(ins, outs, sems):
        send, recv, lsem = sems
        x, y, c, chips = _place()
        me, sibling = (x, y, c), (x, y, 1 - c)

        def rows(b, px, py, pc):
            m_per = blocks[b].shape[0]
            return outs[b].at[pl.ds((4 * px + 2 * py + pc) * m_per, m_per), :]

        def copy(b, k, blk, to, src=None):
            return _remote(rows(b, *blk) if src is None else src, rows(b, *blk), send.at[b, k], recv.at[b, k], to)

        def mine(b):
            return pltpu.make_async_copy(ins[b], rows(b, *me), lsem.at[b])

        def first(b, k):
            return copy(b, k, me, sibling if k == 0 else (*chips[k - 1], c), src=ins[b])

        def passed(b, j):
            return copy(b, 4 + j, (*chips[j], c), sibling)

        def landed(b, j):
            return copy(b, 1 + j, (*chips[j], c), me)

        def handed(b, k):
            return copy(b, 0, sibling, me) if k == 0 else copy(b, 3 + k, (*chips[k - 1], 1 - c), me)

        return mine, first, passed, landed, handed

    def start(ins, outs, sems):
        mine, first, _, _, _ = copies(ins, outs, sems)
        for b in range(nb):
            mine(b).start()
            for k in range(4):
                first(b, k).start()

    def finish(ins, outs, sems):
        mine, first, passed, landed, handed = copies(ins, outs, sems)
        sent = []
        for b in range(nb):
            for j in range(3):
                landed(b, j).wait_recv()
                cp = passed(b, j)
                cp.start()
                sent.append(cp)
        for b in range(nb):
            for k in range(4):
                handed(b, k).wait_recv()
            for k in range(4):
                first(b, k).wait_send()
        for cp in sent:
            cp.wait_send()
        for b in range(nb):
            mine(b).wait()

    return _Exchange(blocks, [_sds((N_DEV * b.shape[0], b.shape[1]), F32) for b in blocks],
                     [pltpu.SemaphoreType.DMA((nb, 7)), pltpu.SemaphoreType.DMA((nb, 7)), pltpu.SemaphoreType.DMA((nb,))],
                     start, finish)


def _inproj(x2, g1, w_in_t, b_in, tabs, seq, exchanges=()):
    T = x2.shape[0]
    tm = min(TM, seq)
    nseq = seq // tm

    def body(x_ref, g_ref, w_ref, b_ref, c_ref, a_ref, bt_ref, h_ref, u_ref, q_ref, k_ref, v_ref, gate_ref):
        x = x_ref[...]
        h = (x * _rms(x) * g_ref[...]).astype(BF16)
        h_ref[...] = h

        def proj(lo, hi):
            return _dot_nt(h, w_ref[lo:hi, :]) + b_ref[:, lo:hi]

        c, a, bt = c_ref[...], a_ref[...], bt_ref[...]
        u_ref[...] = proj(0, C_Q)
        q = proj(C_Q, C_K)
        for p in range(4):
            sl = slice(LANES * p, LANES * (p + 1))
            q_ref[:, sl] = (_rot_fwd(q[:, sl], c, a, bt) * SCALE).astype(BF16)
        kv = proj(C_K, C_G)
        k_ref[...] = _rot_fwd(kv[:, :KV_WIDTH], c, a, bt).astype(BF16)
        v_ref[...] = kv[:, KV_WIDTH:].astype(BF16)
        for j in range(2):
            lo = C_G + D_MODEL * j
            gate_ref[:, D_MODEL * j:D_MODEL * (j + 1)] = jax.nn.sigmoid(proj(lo, lo + D_MODEL)).astype(BF16)

    tab = pl.BlockSpec((tm, LANES), lambda i: (i % nseq, 0))
    return _call(
        body, name="inproj", grid=(T // tm,),
        in_specs=[_rows(tm, D_MODEL), _const((1, D_MODEL)), _const((IN_WIDTH, D_MODEL)), _const((1, IN_WIDTH)),
                  tab, tab, tab],
        out_specs=[_rows(tm, D_MODEL), _rows(tm, POOL_WIDTH), _rows(tm, ATTN_WIDTH), _rows(tm, KV_WIDTH),
                   _rows(tm, KV_WIDTH), _rows(tm, GATE_WIDTH)],
        out_shape=[_sds((T, D_MODEL), BF16), _sds((T, POOL_WIDTH), F32), _sds((T, ATTN_WIDTH), BF16),
                   _sds((T, KV_WIDTH), BF16), _sds((T, KV_WIDTH), BF16), _sds((T, GATE_WIDTH), BF16)],
        args=(x2, g1, w_in_t, b_in, *tabs), sem=("parallel",), exchanges=exchanges)


def _inv_count(pos, w):
    return 1.0 / jnp.minimum(pos + 1, w).astype(F32)


def _pool_fwd(u, w_pool, pool_scale, seq):
    T = u.shape[0]
    tp = min(TP, seq)
    nseq = seq // tp
    per = tp // HALO

    def body(u_ref, prev_ref, w_ref, s_ref, diff_ref, y_ref):
        i = pl.program_id(0)
        first = (i % nseq) == 0
        prev = jnp.where(first, 0.0, prev_ref[...])
        ext = jnp.concatenate([prev, u_ref[...]], axis=0)
        pos = (i % nseq) * tp + lax.broadcasted_iota(jnp.int32, (tp, 1), 0)
        for gi, w in enumerate(POOL_WINDOWS):
            sl = slice(POOL_GC * gi, POOL_GC * (gi + 1))
            xg = ext[:, sl]
            s = xg
            sh = 1
            while sh < w:
                s = s + pltpu.roll(s, sh, 0)
                sh *= 2
            pooled = s[HALO:] * _inv_count(pos, w)
            diff = (pooled - xg[HALO:]).astype(BF16)
            diff_ref[:, sl] = diff
            mixed = _dot(diff, w_ref[gi].astype(BF16))
            y_ref[:, sl] = (mixed * s_ref[:, sl]).astype(BF16)

    return _call(
        body, name="pool_fwd", grid=(T // tp,),
        in_specs=[_rows(tp, POOL_WIDTH),
                  pl.BlockSpec((HALO, POOL_WIDTH), lambda i: (jnp.maximum(i * per - 1, 0), 0)),
                  _const((4, POOL_GC, POOL_GC)), _const((1, POOL_WIDTH))],
        out_specs=[_rows(tp, POOL_WIDTH), _rows(tp, POOL_WIDTH)],
        out_shape=[_sds((T, POOL_WIDTH), BF16), _sds((T, POOL_WIDTH), BF16)],
        args=(u, u, w_pool, pool_scale), sem=("parallel",))


GROUP = 4
GROWS = GROUP * BLOCK


def _attn_masks(n):
    qi = lax.broadcasted_iota(jnp.int32, (GROWS, 2 * BLOCK), 0) % BLOCK
    kj = lax.broadcasted_iota(jnp.int32, (GROWS, 2 * BLOCK), 1)
    rel = qi + BLOCK - kj
    valid = (rel >= 0) & (rel < BLOCK) & (kj >= jnp.where(n > 0, 0, BLOCK))
    lo = lax.broadcasted_iota(jnp.int32, (BLOCK, LANES), 1) < HEAD_DIM
    return valid, lo


def _by_example(bl, *arrays):
    return [a.reshape(bl, a.shape[0] // bl, a.shape[1]) for a in arrays]


def _stack_heads(ref, h, lo):
    keep = lo if h == 0 else jnp.logical_not(lo)
    pieces = []
    for p in (2 * h, 2 * h + 1):
        xp = ref[:, LANES * p:LANES * (p + 1)].astype(F32)
        for e in range(2):
            t = xp if e == h else pltpu.roll(xp, HEAD_DIM, 1)
            pieces.append(jnp.where(keep, t, 0.0).astype(BF16))
    return jnp.concatenate(pieces, axis=0)


def _unstack_heads(stacked, h, lo):
    pairs = []
    for j in range(2):
        parts = []
        for e in range(2):
            t = stacked[BLOCK * (2 * j + e):BLOCK * (2 * j + e + 1)]
            parts.append(t if e == h else pltpu.roll(t, HEAD_DIM, 1))
        pairs.append(jnp.where(lo, parts[0], parts[1]))
    return pairs


def _sink_rows(sink_ref, h):
    head = lax.broadcasted_iota(jnp.int32, (GROWS, 1), 0) // BLOCK
    col = jnp.zeros((GROWS, 1), F32) + sink_ref[GROUP * h]
    for g in range(1, GROUP):
        col = jnp.where(head == g, sink_ref[GROUP * h + g], col)
    return col


def _group_probs(qs, kk, valid, sink):
    s = jnp.where(valid, _dot_nt(qs, kk), NEG_INF)
    m = jnp.maximum(jnp.max(s, axis=1, keepdims=True), sink)
    ex = jnp.exp(s - m)
    es = jnp.exp(sink - m)
    inv = 1.0 / (jnp.sum(ex, axis=1, keepdims=True) + es)
    return ex * inv, es * inv


def _attn_fwd(q, k, v, sinks, seq, exchanges=()):
    T = q.shape[0]
    nb = seq // BLOCK
    bl = T // seq

    def body(sink_ref, q_ref, kp_ref, kc_ref, vp_ref, vc_ref, o_ref):
        valid, lo = _attn_masks(pl.program_id(0))
        for b in range(bl):
            kk = jnp.concatenate([kp_ref[b], kc_ref[b]], axis=0)
            vv = jnp.concatenate([vp_ref[b], vc_ref[b]], axis=0)
            for h in range(2):
                qs = _stack_heads(q_ref.at[b], h, lo)
                pr, _ = _group_probs(qs, kk, valid, _sink_rows(sink_ref, h))
                o = _dot(pr.astype(BF16), vv)
                for j, pair in enumerate(_unstack_heads(o, h, lo)):
                    p = 2 * h + j
                    o_ref[b, :, LANES * p:LANES * (p + 1)] = pair.astype(BF16)

    cur = lambda n: (0, n, 0)
    prv = lambda n: (0, jnp.maximum(n - 1, 0), 0)
    kv = lambda m: pl.BlockSpec((bl, BLOCK, KV_WIDTH), m)
    res = _call(
        body, name="attn_fwd", grid=(nb,),
        in_specs=[pl.BlockSpec(memory_space=pltpu.SMEM), pl.BlockSpec((bl, BLOCK, ATTN_WIDTH), cur),
                  kv(prv), kv(cur), kv(prv), kv(cur)],
        out_specs=[pl.BlockSpec((bl, BLOCK, ATTN_WIDTH), cur)],
        out_shape=[_sds((bl, seq, ATTN_WIDTH), BF16)],
        args=(sinks, *_by_example(bl, q, k, k, v, v)), sem=("parallel",), exchanges=exchanges)
    if exchanges:
        return [res[0][0].reshape(T, ATTN_WIDTH)], res[1]
    return [res[0].reshape(T, ATTN_WIDTH)]


def _branch(y, w_ref):
    return jnp.concatenate([_dot(y, w_ref[j]) for j in range(N_CHIPS)], axis=1)


def _merge_out(y_pool, y_attn, gate, x2, w_bp, w_ba, w_out, g2, g3, exchanges=()):
    T = x2.shape[0]
    tm = min(TM, T)

    def body(yp_ref, ya_ref, gate_ref, x_ref, wbp_ref, wba_ref, wo_ref, g2_ref, g3_ref,
             mg_ref, mix_ref, x1_ref, h2_ref):
        bp, ba = _branch(yp_ref[...], wbp_ref), _branch(ya_ref[...], wba_ref)
        merged = (gate_ref[:, :D_MODEL].astype(F32) * bp + gate_ref[:, D_MODEL:].astype(F32) * ba).astype(BF16)
        mg_ref[...] = merged
        mix = _dot(merged, wo_ref[...])
        mix_ref[...] = mix
        x1 = x_ref[...] + mix * _rms(mix) * g2_ref[...]
        x1_ref[...] = x1
        h2_ref[...] = (x1 * _rms(x1) * g3_ref[...]).astype(BF16)

    return _call(
        body, name="merge_out", grid=(T // tm,),
        in_specs=[_rows(tm, POOL_WIDTH), _rows(tm, ATTN_WIDTH), _rows(tm, GATE_WIDTH), _rows(tm, D_MODEL),
                  _const(w_bp.shape), _const(w_ba.shape), _const((D_MODEL, D_MODEL)),
                  _const((1, D_MODEL)), _const((1, D_MODEL))],
        out_specs=[_rows(tm, D_MODEL)] * 4,
        out_shape=[_sds((T, D_MODEL), BF16), _sds((T, D_MODEL), F32), _sds((T, D_MODEL), F32),
                   _sds((T, D_MODEL), BF16)],
        args=(y_pool, y_attn, gate, x2, w_bp, w_ba, w_out, g2, g3), sem=("parallel",), exchanges=exchanges)


HALF = D_MODEL // 2
TM_MLP = 256


def _mlp_core(h2, x1, mix, tgt, w_up, w_down, g4, g3, g2):
    T = h2.shape[0]
    tm = min(TM_MLP, T)

    def body(h_ref, x1_ref, mix_ref, t_ref, g_ref, g3_ref, g2_ref, ua_hbm, ub_hbm, da_hbm, db_hbm,
             act_ref, dff_ref, dup_ref, dx1_ref, dmix_ref, loss_ref, dg_ref, dg3_ref, dg2_ref,
             wu, wd, relu_scr, sems):
        def weight_copy(i):
            src, dst = ((ua_hbm, wu.at[:, :HALF]), (ub_hbm, wu.at[:, HALF:]),
                        (da_hbm, wd.at[:, :HALF]), (db_hbm, wd.at[:, HALF:]))[i]
            return pltpu.make_async_copy(src, dst, sems.at[i])

        @pl.when(pl.program_id(0) == 0)
        def _():
            for i in range(4):
                weight_copy(i).start()
            loss_ref[...] = jnp.zeros_like(loss_ref)
            for ref in (dg_ref, dg3_ref, dg2_ref):
                ref[...] = jnp.zeros_like(ref)
            weight_copy(0).wait()
            weight_copy(1).wait()

        h = h_ref[...]
        ff = None
        for j in range(N_CHIPS):
            lo = D_MODEL * j
            relu = jnp.maximum(_dot(h, wu[j]), 0.0)
            if j == 0:
                @pl.when(pl.program_id(0) == 0)
                def _():
                    weight_copy(2).wait()
                    weight_copy(3).wait()
            relu_scr[:, lo:lo + D_MODEL] = relu
            act = jnp.square(relu).astype(BF16)
            act_ref[:, lo:lo + D_MODEL] = act
            t = _dot(act, wd[j])
            ff = t if ff is None else ff + t
        g = g_ref[...]
        x1 = x1_ref[...]
        err = x1 + ff * _rms(ff) * g - t_ref[...]
        loss_ref[...] += jnp.sum(err * err) * (0.5 / D_MODEL)
        dy = err * (1.0 / D_MODEL)
        dff, dg = _norm_bwd(ff, g, dy)
        dg_ref[...] += dg
        dff = dff.astype(BF16)
        dff_ref[...] = dff
        dh2 = None
        for j in range(N_CHIPS):
            lo = D_MODEL * j
            dup = (_dot_nt(dff, wd[j]) * (2.0 * relu_scr[:, lo:lo + D_MODEL])).astype(BF16)
            dup_ref[:, lo:lo + D_MODEL] = dup
            t = _dot_nt(dup, wu[j])
            dh2 = t if dh2 is None else dh2 + t
        dx, dg3 = _norm_bwd(x1, g3_ref[...], dh2)
        dx1 = dy + dx
        dx1_ref[...] = dx1
        dg3_ref[...] += dg3
        dmix, dg2 = _norm_bwd(mix_ref[...], g2_ref[...], dx1)
        dmix_ref[...] = dmix.astype(BF16)
        dg2_ref[...] += dg2

    slabs = pltpu.VMEM((N_CHIPS, D_MODEL, D_MODEL), BF16)
    gain = _const((1, D_MODEL))
    return pl.pallas_call(
        body, name="mlp_core", grid=(T // tm,),
        in_specs=[_rows(tm, D_MODEL)] * 4 + [gain] * 3 + [ANY] * 4,
        out_specs=[_rows(tm, D_FF), _rows(tm, D_MODEL), _rows(tm, D_FF), _rows(tm, D_MODEL), _rows(tm, D_MODEL),
                   _const((8, LANES)), gain, gain, gain],
        out_shape=[_sds((T, D_FF), BF16), _sds((T, D_MODEL), BF16), _sds((T, D_FF), BF16), _sds((T, D_MODEL), F32),
                   _sds((T, D_MODEL), BF16), _sds((8, LANES), F32)] + [_sds((1, D_MODEL), F32)] * 3,
        scratch_shapes=[slabs] * 2 + [pltpu.VMEM((tm, D_FF), F32), pltpu.SemaphoreType.DMA((4,))],
        compiler_params=_cp("arbitrary"),
    )(h2, x1, mix, tgt, g4, g3, g2, *w_up, *w_down)


def _dw(tag, a, g, ta, tn, shard_cols=False, exchanges=()):
    T, ka = a.shape
    n = g.shape[1]
    tk = min(2 * TM, T)
    nk = T // tk

    def body(a_ref, g_ref, o_ref):
        @pl.when(pl.program_id(2) == 0)
        def _():
            o_ref[...] = jnp.zeros_like(o_ref)

        o_ref[...] += _dot_tn(a_ref[...], g_ref[...])

    if shard_cols:
        per = (n // N_CHIPS) // tn
        out_spec = pl.BlockSpec((None, ta, tn), lambda i, j, k: (j // per, i, j % per))
        out_shape = _sds((N_CHIPS, ka, n // N_CHIPS), F32)
    else:
        out_spec = pl.BlockSpec((ta, tn), lambda i, j, k: (i, j))
        out_shape = _sds((ka, n), F32)
    return _call(
        body, name="dw_" + tag, grid=(ka // ta, n // tn, nk),
        in_specs=[pl.BlockSpec((tk, ta), lambda i, j, k: (k, i)), pl.BlockSpec((tk, tn), lambda i, j, k: (k, j))],
        out_specs=[out_spec], out_shape=[out_shape],
        args=(a, g), sem=("parallel", "parallel", "arbitrary"), exchanges=exchanges)


def _dw_mix(merged, dmix, y_pool, dbp, y_attn, dba, exchanges=()):
    T = merged.shape[0]
    tk = min(2 * TM, T)
    c = D_MODEL // N_CHIPS

    def body(mg_ref, dmix_ref, yp_ref, dbp_ref, ya_ref, dba_ref, out_ref, bp_ref, ba_ref):
        @pl.when(pl.program_id(0) == 0)
        def _():
            for ref in (out_ref, bp_ref, ba_ref):
                ref[...] = jnp.zeros_like(ref)

        out_ref[...] += _dot_tn(mg_ref[...], dmix_ref[...])
        for y_ref, d_ref, o_ref in ((yp_ref, dbp_ref, bp_ref), (ya_ref, dba_ref, ba_ref)):
            res = _dot_tn(y_ref[...], d_ref[...])
            for j in range(N_CHIPS):
                o_ref[j] += res[:, c * j:c * (j + 1)]

    slabs = (N_CHIPS, POOL_WIDTH, c)
    return _call(
        body, name="dw_mix", grid=(T // tk,),
        in_specs=[_rows(tk, D_MODEL), _rows(tk, D_MODEL), _rows(tk, POOL_WIDTH), _rows(tk, D_MODEL),
                  _rows(tk, ATTN_WIDTH), _rows(tk, D_MODEL)],
        out_specs=[_const((D_MODEL, D_MODEL)), _const(slabs), _const(slabs)],
        out_shape=[_sds((D_MODEL, D_MODEL), F32), _sds(slabs, F32), _sds(slabs, F32)],
        args=(merged, dmix, y_pool, dbp, y_attn, dba), sem=("arbitrary",), exchanges=exchanges)


def _merge_bwd(dmix, gate, y_pool, y_attn, w_out, w_bp, w_ba, exchanges=()):
    T = dmix.shape[0]
    tm = min(TM, T)

    def body(dmix_ref, gate_ref, yp_ref, ya_ref, wo_ref, wbp_ref, wba_ref,
             dbp_ref, dba_ref, dgate_ref, dyp_ref, dya_ref):
        dm = _dot_nt(dmix_ref[...], wo_ref[...])
        for j, (y_ref, db_ref, w_ref, dy_ref) in enumerate(
                ((yp_ref, dbp_ref, wbp_ref, dyp_ref), (ya_ref, dba_ref, wba_ref, dya_ref))):
            sl = slice(D_MODEL * j, D_MODEL * (j + 1))
            gt = gate_ref[:, sl].astype(F32)
            db = (dm * gt).astype(BF16)
            db_ref[...] = db
            dgate_ref[:, sl] = (dm * _branch(y_ref[...], w_ref) * gt * (1.0 - gt)).astype(BF16)
            cw = D_MODEL // N_CHIPS
            dy = _dot_nt(db[:, :cw], w_ref[0])
            for c in range(1, N_CHIPS):
                dy = dy + _dot_nt(db[:, cw * c:cw * (c + 1)], w_ref[c])
            dy_ref[...] = dy.astype(dy_ref.dtype)

    return _call(
        body, name="merge_bwd", grid=(T // tm,),
        in_specs=[_rows(tm, D_MODEL), _rows(tm, GATE_WIDTH), _rows(tm, POOL_WIDTH), _rows(tm, ATTN_WIDTH),
                  _const((D_MODEL, D_MODEL)), _const(w_bp.shape), _const(w_ba.shape)],
        out_specs=[_rows(tm, D_MODEL), _rows(tm, D_MODEL), _rows(tm, GATE_WIDTH), _rows(tm, POOL_WIDTH),
                   _rows(tm, ATTN_WIDTH)],
        out_shape=[_sds((T, D_MODEL), BF16), _sds((T, D_MODEL), BF16), _sds((T, GATE_WIDTH), BF16),
                   _sds((T, POOL_WIDTH), F32), _sds((T, ATTN_WIDTH), BF16)],
        args=(dmix, gate, y_pool, y_attn, w_out, w_bp, w_ba), sem=("parallel",), exchanges=exchanges)


def _attn_bwd(q, k, v, do, sinks, tabs, seq, exchanges=()):
    T = q.shape[0]
    nb = seq // BLOCK
    bl = T // seq
    steps = nb + 1

    def body(sink_ref, q_ref, do_ref, kp_ref, kc_ref, vp_ref, vc_ref, c_ref, a_ref, bt_ref, cp_ref, ap_ref, btp_ref,
             dq_ref, dk_ref, dv_ref, dsink_ref, ck_ref, cv_ref):
        n = pl.program_id(0)

        @pl.when(n == 0)
        def _():
            dsink_ref[...] = jnp.zeros_like(dsink_ref)
            ck_ref[...] = jnp.zeros_like(ck_ref)
            cv_ref[...] = jnp.zeros_like(cv_ref)

        @pl.when(n < nb)
        def _():
            valid, lo = _attn_masks(n)
            for b in range(bl):
                kk = jnp.concatenate([kp_ref[b], kc_ref[b]], axis=0)
                vv = jnp.concatenate([vp_ref[b], vc_ref[b]], axis=0)
                dk_acc = jnp.zeros((2 * BLOCK, KV_WIDTH), F32)
                dv_acc = jnp.zeros((2 * BLOCK, KV_WIDTH), F32)
                for h in range(2):
                    qs = _stack_heads(q_ref.at[b], h, lo)
                    dos = _stack_heads(do_ref.at[b], h, lo)
                    pr, ps = _group_probs(qs, kk, valid, _sink_rows(sink_ref, h))
                    dp = _dot_nt(dos, vv)
                    delta = jnp.sum(pr * dp, axis=1, keepdims=True)
                    ds = (pr * (dp - delta)).astype(BF16)
                    dsk = ps * delta
                    for g in range(GROUP):
                        idx = GROUP * h + g
                        dsink_ref[idx:idx + 1, :] += (jnp.zeros((1, LANES), F32)
                                                      - jnp.sum(dsk[BLOCK * g:BLOCK * (g + 1)]))
                    dk_acc = dk_acc + _dot_tn(ds, qs)
                    dv_acc = dv_acc + _dot_tn(pr.astype(BF16), dos)
                    for j, pair in enumerate(_unstack_heads(_dot(ds, kk) * SCALE, h, lo)):
                        sl = slice(LANES * (2 * h + j), LANES * (2 * h + j + 1))
                        dq_ref[b, :, sl] = _rot_bwd(pair, c_ref[...], a_ref[...], bt_ref[...]).astype(BF16)
                fin_k = ck_ref[b] + dk_acc[:BLOCK]
                dk_ref[b] = _rot_bwd(fin_k, cp_ref[...], ap_ref[...], btp_ref[...]).astype(BF16)
                dv_ref[b] = (cv_ref[b] + dv_acc[:BLOCK]).astype(BF16)
                ck_ref[b] = dk_acc[BLOCK:]
                cv_ref[b] = dv_acc[BLOCK:]

        @pl.when(n == nb)
        def _():
            for b in range(bl):
                dk_ref[b] = _rot_bwd(ck_ref[b], cp_ref[...], ap_ref[...], btp_ref[...]).astype(BF16)
                dv_ref[b] = cv_ref[b].astype(BF16)

    cur = lambda n: (0, jnp.minimum(n, nb - 1), 0)
    prv = lambda n: (0, jnp.clip(n - 1, 0, nb - 1), 0)
    tcur = lambda n: (jnp.minimum(n, nb - 1), 0)
    tprv = lambda n: (jnp.clip(n - 1, 0, nb - 1), 0)
    wide = lambda m: pl.BlockSpec((bl, BLOCK, ATTN_WIDTH), m)
    kv = lambda m: pl.BlockSpec((bl, BLOCK, KV_WIDTH), m)
    tab = lambda m: pl.BlockSpec((BLOCK, LANES), m)
    res = _call(
        body, name="attn_bwd", grid=(steps,),
        in_specs=[pl.BlockSpec(memory_space=pltpu.SMEM), wide(cur), wide(cur), kv(prv), kv(cur), kv(prv), kv(cur),
                  tab(tcur), tab(tcur), tab(tcur), tab(tprv), tab(tprv), tab(tprv)],
        out_specs=[wide(cur), kv(prv), kv(prv), _const((8, LANES))],
        out_shape=[_sds((bl, seq, ATTN_WIDTH), BF16), _sds((bl, seq, KV_WIDTH), BF16),
                   _sds((bl, seq, KV_WIDTH), BF16), _sds((8, LANES), F32)],
        scratch=[pltpu.VMEM((bl, BLOCK, KV_WIDTH), F32), pltpu.VMEM((bl, BLOCK, KV_WIDTH), F32)],
        args=(sinks, *_by_example(bl, q, do, k, k, v, v), *tabs, *tabs), sem=("arbitrary",), exchanges=exchanges)
    outs, rest = (res if exchanges else (res, None))
    outs = [outs[0].reshape(T, ATTN_WIDTH), outs[1].reshape(T, KV_WIDTH), outs[2].reshape(T, KV_WIDTH), outs[3]]
    return (outs, rest) if exchanges else outs


def _pool_bwd(dyp, diff, w_pool, pool_scale, seq, exchanges=()):
    T = dyp.shape[0]
    tp = min(TP, seq)
    nseq = seq // tp
    per = tp // HALO
    last_halo = T // HALO - 1

    def body(dy_ref, nxt_ref, diff_ref, w_ref, s_ref, du_ref, dw_ref, ds_ref):
        i = pl.program_id(0)

        @pl.when(i == 0)
        def _():
            dw_ref[...] = jnp.zeros_like(dw_ref)
            ds_ref[...] = jnp.zeros_like(ds_ref)

        last = (i % nseq) == nseq - 1
        nxt = jnp.where(last, 0.0, nxt_ref[...])
        ext = jnp.concatenate([dy_ref[...], nxt], axis=0) * s_ref[...]
        pos = (i % nseq) * tp + lax.broadcasted_iota(jnp.int32, (tp + HALO, 1), 0)
        for gi, w in enumerate(POOL_WINDOWS):
            sl = slice(POOL_GC * gi, POOL_GC * (gi + 1))
            wg = w_ref[gi].astype(BF16)
            dmx = ext[:, sl].astype(BF16)
            ddiff = _dot_nt(dmx, wg)
            s = ddiff * _inv_count(pos, w)
            sh = 1
            while sh < w:
                s = s + pltpu.roll(s, tp + HALO - sh, 0)
                sh *= 2
            du_ref[:, sl] = (s[:tp] - ddiff[:tp]).astype(BF16)
            dg = diff_ref[:, sl]
            dw_ref[gi] += _dot_tn(dg, dmx[:tp])
            ds_ref[:, sl] += jnp.sum(dy_ref[:, sl] * _dot(dg, wg), axis=0, keepdims=True)

    return _call(
        body, name="pool_bwd", grid=(T // tp,),
        in_specs=[_rows(tp, POOL_WIDTH),
                  pl.BlockSpec((HALO, POOL_WIDTH), lambda i: (jnp.minimum((i + 1) * per, last_halo), 0)),
                  _rows(tp, POOL_WIDTH), _const((4, POOL_GC, POOL_GC)), _const((1, POOL_WIDTH))],
        out_specs=[_rows(tp, POOL_WIDTH), _const((4, POOL_GC, POOL_GC)), _const((1, POOL_WIDTH))],
        out_shape=[_sds((T, POOL_WIDTH), BF16), _sds((4, POOL_GC, POOL_GC), F32), _sds((1, POOL_WIDTH), F32)],
        args=(dyp, dyp, diff, w_pool, pool_scale), sem=("arbitrary",), exchanges=exchanges)


_PARTS = ((0, C_Q), (C_Q, C_K), (C_K, C_V), (C_V, C_G), (C_G, IN_WIDTH))


def _inproj_bwd(parts, x2, dx1, w_in_t, g1, exchanges=()):
    T = x2.shape[0]
    tm = min(TM, T)

    def body(du_ref, dq_ref, dk_ref, dv_ref, dgt_ref, x_ref, dx1_ref, w_ref, g_ref, gx_ref, dg_ref):
        @pl.when(pl.program_id(0) == 0)
        def _():
            dg_ref[...] = jnp.zeros_like(dg_ref)

        dh = jnp.zeros((tm, D_MODEL), F32)
        for (lo, hi), p_ref in zip(_PARTS, (du_ref, dq_ref, dk_ref, dv_ref, dgt_ref)):
            dh = dh + _dot(p_ref[...], w_ref[lo:hi, :])
        dx, dg = _norm_bwd(x_ref[...], g_ref[...], dh)
        gx_ref[...] = dx1_ref[...] + dx
        dg_ref[...] += dg

    return _call(
        body, name="inproj_bwd", grid=(T // tm,),
        in_specs=[_rows(tm, hi - lo) for lo, hi in _PARTS]
        + [_rows(tm, D_MODEL), _rows(tm, D_MODEL), _const((IN_WIDTH, D_MODEL)), _const((1, D_MODEL))],
        out_specs=[_rows(tm, D_MODEL), _const((1, D_MODEL))],
        out_shape=[_sds((T, D_MODEL), F32), _sds((1, D_MODEL), F32)],
        args=(*parts, x2, dx1, w_in_t, g1), sem=("arbitrary",), exchanges=exchanges)


def _dw_in(h, parts, exchanges=()):
    T = h.shape[0]
    tk = min(TM, T)

    def body(h_ref, du_ref, dq_ref, dk_ref, dv_ref, dgt_ref, o_ref, db_ref):
        @pl.when(pl.program_id(0) == 0)
        def _():
            o_ref[...] = jnp.zeros_like(o_ref)
            db_ref[...] = jnp.zeros_like(db_ref)

        hh = h_ref[...]
        for (lo, hi), p_ref in zip(_PARTS, (du_ref, dq_ref, dk_ref, dv_ref, dgt_ref)):
            part = p_ref[...]
            o_ref[lo:hi, :] += _dot_tn(part, hh)
            db_ref[:, lo:hi] += jnp.sum(part.astype(F32), axis=0, keepdims=True)

    return _call(
        body, name="dw_in", grid=(T // tk,),
        in_specs=[_rows(tk, D_MODEL)] + [_rows(tk, hi - lo) for lo, hi in _PARTS],
        out_specs=[_const((IN_WIDTH, D_MODEL)), _const((1, IN_WIDTH))],
        out_shape=[_sds((IN_WIDTH, D_MODEL), F32), _sds((1, IN_WIDTH), F32)],
        args=(h, *parts), sem=("arbitrary",), exchanges=exchanges)


def _row_tile(rows, cap=256, mult=16):
    best = None
    for t in range(mult, min(rows, cap) + 1, mult):
        if rows % t == 0:
            best = t
    if best is None:
        raise ValueError("no row tile for %d rows" % rows)
    return best


def _pair_sum(ids, full, got):
    _, r, c = full.shape
    hr = r // 2
    tr = _row_tile(hr)
    nblk = hr // tr

    def body(ids_ref, a_ref, b_ref, own_ref, sb_ref):
        s = a_ref[...] + b_ref[...]
        sb_ref[...] = s.astype(BF16)

        @pl.when(pl.program_id(1) == ids_ref[0])
        def _():
            own_ref[...] = s

    slab = pl.BlockSpec((None, tr, c), lambda i, j, ids_ref: (j, i, 0))
    return pl.pallas_call(
        body, name="pair_sum_%dx%d" % (r, c),
        grid_spec=pltpu.PrefetchScalarGridSpec(
            num_scalar_prefetch=1, grid=(nblk, N_CHIPS),
            in_specs=[pl.BlockSpec((None, tr, c), lambda i, j, ids_ref: (j, ids_ref[1] * nblk + i, 0)), slab],
            out_specs=[pl.BlockSpec((tr, c), lambda i, j, ids_ref: (i, 0)), slab]),
        out_shape=[_sds((hr, c), F32), _sds((N_CHIPS, hr, c), BF16)],
        compiler_params=_cp("parallel", "arbitrary"),
    )(ids, full, got)


def _chip_sum(ids, own, got):
    hr, c = own.shape
    tr = _row_tile(hr)
    nblk = hr // tr

    def body(ids_ref, a_ref, b_ref, o_ref):
        o_ref[...] = ((a_ref[...] + b_ref[0].astype(F32)) + b_ref[1].astype(F32)) + b_ref[2].astype(F32)

    return pl.pallas_call(
        body, name="chip_sum_%dx%d" % (hr, c),
        grid_spec=pltpu.PrefetchScalarGridSpec(
            num_scalar_prefetch=1, grid=(nblk,),
            in_specs=[pl.BlockSpec((tr, c), lambda i, ids_ref: (i, 0)),
                      pl.BlockSpec((3, tr, c), lambda i, ids_ref: (0, i, 0))],
            out_specs=pl.BlockSpec((tr, c), lambda i, ids_ref: (ids_ref[1] * nblk + i, 0))),
        out_shape=_sds((2 * hr, c), F32),
        compiler_params=_cp("parallel"),
    )(ids, own, got)


def _adamw_math(w, g, m, v):
    nm = ADAM_B1 * m + (1.0 - ADAM_B1) * g
    nv = ADAM_B2 * v + (1.0 - ADAM_B2) * (g * g)
    m_hat = nm / (1.0 - ADAM_B1 ** ADAM_STEP)
    v_hat = nv / (1.0 - ADAM_B2 ** ADAM_STEP)
    return -ADAM_LR * (m_hat / (jnp.sqrt(v_hat) + ADAM_EPS) + ADAM_WD * w), nm, nv


def _adamw(w, g, m, v):
    r, c = w.shape
    tr = _row_tile(r, cap=512, mult=8)

    def body(w_ref, g_ref, m_ref, v_ref, d_ref, nm_ref, nv_ref):
        d_ref[...], nm_ref[...], nv_ref[...] = _adamw_math(w_ref[...], g_ref[...], m_ref[...], v_ref[...])

    spec = _rows(tr, c)
    return pl.pallas_call(
        body, name="adamw_%dx%d" % (r, c), grid=(r // tr,),
        in_specs=[spec] * 4, out_specs=[spec] * 3, out_shape=[_sds((r, c), F32)] * 3,
        compiler_params=_cp("parallel"),
    )(w, g, m, v)


SC_TILES = 32
SC_LANES = 16
SC_ROWS = 8


def _sparse_mesh():
    return plsc.VectorSubcoreMesh(core_axis_name="sc_core", subcore_axis_name="sc_subcore")


def _sparse_tile():
    return lax.axis_index("sc_subcore") * 2 + lax.axis_index("sc_core")


def _adamw_sparse(tag, items):
    n = len(items)
    shapes = [w.shape for w, _, _, _ in items]

    def body(*refs):
        ins, outs, bufs = refs[:4 * n], refs[4 * n:8 * n], refs[8 * n:]
        tile = _sparse_tile()
        for k, (r, c) in enumerate(shapes):
            rows = r // SC_TILES
            step = min(rows, SC_ROWS)
            wb, gb, mb, vb = bufs[4 * k:4 * k + 4]

            @pl.loop(0, rows, step=step)
            def _(r0):
                mine = pl.ds(tile * rows + r0, step)
                for src, dst in zip(ins[4 * k:4 * k + 4], (wb, gb, mb, vb)):
                    pltpu.sync_copy(src.at[mine], dst)

                @pl.loop(0, step)
                def _(row):
                    @pl.loop(0, c, step=SC_LANES)
                    def _(i):
                        at = (row, pl.ds(i, SC_LANES))
                        wb[at], mb[at], vb[at] = _adamw_math(wb[at], gb[at], mb[at], vb[at])

                for src, dst in zip((wb, mb, vb, gb), outs[4 * k:4 * k + 4]):
                    pltpu.sync_copy(src, dst.at[mine])

    res = pl.kernel(
        body, name="adamw_sparse_" + tag, out_type=[_sds(s, F32) for s in shapes for _ in range(4)],
        mesh=_sparse_mesh(),
        scratch_types=[pltpu.VMEM((min(r // SC_TILES, SC_ROWS), c), F32) for r, c in shapes for _ in range(4)],
    )(*[a for item in items for a in item])
    return [tuple(res[4 * k:4 * k + 4]) for k in range(n)]


def _copy_sparse(a):
    r, c = a.shape
    rows = r // SC_TILES

    def body(a_hbm, o_hbm, buf):
        tile = _sparse_tile()

        @pl.loop(0, rows, step=SC_ROWS)
        def _(r0):
            mine = pl.ds(tile * rows + r0, SC_ROWS)
            pltpu.sync_copy(a_hbm.at[mine], buf)
            pltpu.sync_copy(buf, o_hbm.at[mine])

    return pl.kernel(
        body, name="copy_sparse_%dx%d" % (r, c), out_type=_sds((r, c), F32), mesh=_sparse_mesh(),
        scratch_types=[pltpu.VMEM((SC_ROWS, c), F32)],
    )(a)


_SMALL_NAMES = ("w_pool", "b_in", "g_mix_pre", "g_mix_post", "g_mlp_pre", "g_mlp_post", "pool_scale", "attn_sinks")
B_ROWS = -(-IN_WIDTH // D_MODEL)


def _row_block(rows):
    rows = [jnp.pad(r.astype(F32), ((0, 0), (0, D_MODEL - r.shape[1]))) for r in rows]
    return jnp.pad(jnp.concatenate(rows, axis=0), ((0, 8 - len(rows)), (0, 0)))


def _early_block(dg2, dg3, dg4, dps, dsink, loss):
    tail = jnp.concatenate([jnp.pad(dsink.reshape(1, -1), ((0, 0), (0, LANES - dsink.size))),
                            jnp.pad(loss.reshape(1, 1), ((0, 0), (0, LANES - 1)))], axis=1)
    return _row_block([dg2, dg3, dg4, dps, tail])


def _late_block(db_in, dg1):
    b = jnp.pad(db_in, ((0, 0), (0, B_ROWS * D_MODEL - IN_WIDTH))).reshape(B_ROWS, D_MODEL)
    return _row_block([b[r:r + 1] for r in range(B_ROWS)] + [dg1])


def _small_update(gearly, gmat, glate, w, m, v):
    names = _SMALL_NAMES
    n = len(names)

    def total(ref, rows):
        acc = ref[0:rows, :]
        for d in range(1, N_DEV):
            acc = acc + ref[d * rows:(d + 1) * rows, :]
        return acc

    def body(*refs):
        early_ref, gmat_ref, late_ref = refs[:3]
        w_refs, m_refs, v_refs = refs[3:3 + n], refs[3 + n:3 + 2 * n], refs[3 + 2 * n:3 + 3 * n]
        outs = refs[3 + 3 * n:]
        loss_ref, g_refs, d_refs = outs[0], outs[1:1 + n], outs[1 + n:1 + 2 * n]
        nm_refs, nv_refs = outs[1 + 2 * n:1 + 3 * n], outs[1 + 3 * n:1 + 4 * n]
        early, late = total(early_ref, 8), total(late_ref, 8)
        loss_ref[...] = jnp.sum(early[4:5, LANES:2 * LANES], axis=1, keepdims=True)
        bias = jnp.concatenate([late[r:r + 1, :] for r in range(B_ROWS - 1)]
                               + [late[B_ROWS - 1:B_ROWS, :IN_WIDTH - (B_ROWS - 1) * D_MODEL]], axis=1)
        grad = dict(b_in=bias, g_mix_pre=late[B_ROWS:B_ROWS + 1, :], g_mix_post=early[0:1, :],
                    g_mlp_pre=early[1:2, :], g_mlp_post=early[2:3, :], pool_scale=early[3:4, :POOL_WIDTH],
                    attn_sinks=early[4:5, :N_Q_HEADS])
        for i, name in enumerate(names):
            g = total(gmat_ref, 4 * POOL_GC) if name == "w_pool" else grad[name]
            g_refs[i][...] = g
            d_refs[i][...], nm_refs[i][...], nv_refs[i][...] = _adamw_math(
                w_refs[i][...], g, m_refs[i][...], v_refs[i][...])

    shapes = [_sds(w[k].shape, F32) for k in names]
    res = pl.pallas_call(
        body, name="small_update", out_shape=[_sds((1, 1), F32)] + shapes * 4,
        compiler_params=pltpu.CompilerParams(vmem_limit_bytes=VMEM_MB * 1024 * 1024),
    )(gearly, gmat, glate, *[w[k] for k in names], *[m[k] for k in names], *[v[k] for k in names])
    loss = res[0]
    per = {k: tuple(res[1 + j * n + i] for j in range(4)) for i, k in enumerate(names)}
    return loss, per


_BIG = ("w_in", "w_branch_pool", "w_branch_attn", "w_out", "w_up", "w_down")
_ORDER = ("g_mix_pre", "w_in", "b_in", "w_pool", "pool_scale", "attn_sinks", "w_branch_pool", "w_branch_attn",
          "w_out", "g_mix_post", "g_mlp_pre", "w_up", "w_down", "g_mlp_post")


def _stack_rows(slab):
    return slab.reshape(-1, slab.shape[2])


def _step(x2, tgt, seq, shards, small, ids):
    tabs = _rope_tables(seq)
    g1, g2, g3, g4 = (small[n] for n in ("g_mix_pre", "g_mix_post", "g_mlp_pre", "g_mlp_post"))
    sinks = small["attn_sinks"].reshape(N_Q_HEADS)
    w_pool = small["w_pool"].reshape(4, POOL_GC, POOL_GC)
    pool_scale = small["pool_scale"]

    def whole(shard, slabs):
        return lax.dynamic_update_slice(slabs, shard[None], (ids[0], 0, 0))

    up_a, up_b = shards["w_up"][:HALF], shards["w_up"][HALF:]
    down_a, down_b = shards["w_down"][:HALF], shards["w_down"][HALF:]
    w_in = _stack_rows(whole(shards["w_in"], _alone("gather_in", _ex_gather([shards["w_in"]]))[0][0]))
    mix_shards = [shards[n] for n in ("w_branch_pool", "w_branch_attn", "w_out")]
    (h, u, q, k, v, gate), [(*mix_slabs, got_c)] = _inproj(
        x2, g1, w_in, small["b_in"], tabs, seq, exchanges=[_ex_gather(mix_shards + [down_a])])
    w_bp, w_ba, out_slab = (whole(s, g) for s, g in zip(mix_shards, mix_slabs))
    w_out = _stack_rows(out_slab)
    diff, y_pool = _pool_fwd(u, w_pool, pool_scale, seq)
    (y_attn,), [[got_a]] = _attn_fwd(q, k, v, sinks, seq, exchanges=[_ex_gather([up_a])])
    (merged, mix, x1, h2), [[got_b, got_d]] = _merge_out(
        y_pool, y_attn, gate, x2, w_bp, w_ba, w_out, g2, g3, exchanges=[_ex_gather([up_b, down_b])])
    w_up = (whole(up_a, got_a), whole(up_b, got_b))
    w_down = (whole(down_a, got_c), whole(down_b, got_d))
    act, dff, dup, dx1, dmix, loss_acc, dg4, dg3, dg2 = _mlp_core(h2, x1, mix, tgt, w_up, w_down, g4, g3, g2)

    dw_down = _dw("down", act, dff, 1024, 1024)[0].reshape(N_CHIPS, D_FF // N_CHIPS, D_MODEL)
    (dbp, dba, dgate, dyp, dya), [[got]] = _merge_bwd(
        dmix, gate, y_pool, y_attn, w_out, w_bp, w_ba, exchanges=[_ex_pair([dw_down])])
    ps_down = _pair_sum(ids, dw_down, got)
    (dw_up,), [[got]] = _dw("up", h2, dup, 1024, 1024, shard_cols=True, exchanges=[_ex_chip([ps_down[1]])])
    half_down = _chip_sum(ids, ps_down[0], got)
    (dw_out, dw_bp, dw_ba), [[got]] = _dw_mix(merged, dmix, y_pool, dbp, y_attn, dba, exchanges=[_ex_pair([dw_up])])
    ps_up = _pair_sum(ids, dw_up, got)
    dw_mix = [dw_out.reshape(N_CHIPS, D_MODEL // N_CHIPS, D_MODEL), dw_bp, dw_ba]
    (dq, dk, dv, dsink), [[got], gots, [g_down]] = _attn_bwd(
        q, k, v, dya, sinks, tabs, seq, exchanges=[_ex_chip([ps_up[1]]), _ex_pair(dw_mix), _ex_swap([half_down])])
    half_up = _chip_sum(ids, ps_up[0], got)
    ps_mix = [_pair_sum(ids, d, g) for d, g in zip(dw_mix, gots)]
    (du, dw_pool, dps), [[g_up]] = _pool_bwd(dyp, diff, w_pool, pool_scale, seq, exchanges=[_ex_swap([half_up])])
    parts = (du, dq, dk, dv, dgate)
    early = _early_block(dg2, dg3, dg4, dps, dsink[:, 0], loss_acc[0, 0])
    mat = dw_pool.reshape(4 * POOL_GC, POOL_GC)
    (dw_in_t, db_in), [gots, [gearly, gmat]] = _dw_in(
        h, parts, exchanges=[_ex_chip([p[1] for p in ps_mix]), _ex_allgather([early, mat])])
    half_mix = [_chip_sum(ids, p[0], g) for p, g in zip(ps_mix, gots)]
    dw_in = dw_in_t.reshape(N_CHIPS, IN_WIDTH // N_CHIPS, D_MODEL)
    g_mix, [got] = _alone("swap_mix_pair_in", _ex_swap(half_mix), _ex_pair([dw_in]))
    ps_in = _pair_sum(ids, dw_in, got)
    (gx, dg1), [[got]] = _inproj_bwd(parts, x2, dx1, w_in, g1, exchanges=[_ex_chip([ps_in[1]])])
    [g_in], [glate] = _alone("swap_in_allgather", _ex_swap([_chip_sum(ids, ps_in[0], got)]),
                             _ex_allgather([_late_block(db_in, dg1)]))

    grads = dict(w_in=g_in, w_branch_pool=g_mix[1], w_branch_attn=g_mix[2], w_out=g_mix[0], w_up=g_up, w_down=g_down)
    return (gearly, gmat, glate), gx, grads


def kernel(x, g_mix_pre, w_in, b_in, w_pool, pool_scale, attn_sinks, w_branch_pool, w_branch_attn, w_out, g_mix_post, g_mlp_pre, w_up, w_down, g_mlp_post, loss_target, m_g_mix_pre, m_w_in, m_b_in, m_w_pool, m_pool_scale, m_attn_sinks, m_w_branch_pool, m_w_branch_attn, m_w_out, m_g_mix_post, m_g_mlp_pre, m_w_up, m_w_down, m_g_mlp_post, v_g_mix_pre, v_w_in, v_b_in, v_w_pool, v_pool_scale, v_attn_sinks, v_w_branch_pool, v_w_branch_attn, v_w_out, v_g_mix_post, v_g_mlp_pre, v_w_up, v_w_down, v_g_mlp_post):
    weights = dict(g_mix_pre=g_mix_pre, w_in=w_in, b_in=b_in, w_pool=w_pool, pool_scale=pool_scale,
                   attn_sinks=attn_sinks, w_branch_pool=w_branch_pool, w_branch_attn=w_branch_attn, w_out=w_out,
                   g_mix_post=g_mix_post, g_mlp_pre=g_mlp_pre, w_up=w_up, w_down=w_down, g_mlp_post=g_mlp_post)
    mom1 = dict(g_mix_pre=m_g_mix_pre, w_in=m_w_in, b_in=m_b_in, w_pool=m_w_pool, pool_scale=m_pool_scale,
                attn_sinks=m_attn_sinks, w_branch_pool=m_w_branch_pool, w_branch_attn=m_w_branch_attn,
                w_out=m_w_out, g_mix_post=m_g_mix_post, g_mlp_pre=m_g_mlp_pre, w_up=m_w_up, w_down=m_w_down,
                g_mlp_post=m_g_mlp_post)
    mom2 = dict(g_mix_pre=v_g_mix_pre, w_in=v_w_in, b_in=v_b_in, w_pool=v_w_pool, pool_scale=v_pool_scale,
                attn_sinks=v_attn_sinks, w_branch_pool=v_w_branch_pool, w_branch_attn=v_w_branch_attn,
                w_out=v_w_out, g_mix_post=v_g_mix_post, g_mlp_pre=v_g_mlp_pre, w_up=v_w_up, w_down=v_w_down,
                g_mlp_post=v_g_mlp_post)
    b_loc, seq, _ = x.shape
    x2 = x.reshape(b_loc * seq, D_MODEL)
    tgt = loss_target.reshape(b_loc * seq, D_MODEL)
    ids = jnp.stack([2 * lax.axis_index("x") + lax.axis_index("y"), lax.axis_index("c")]).astype(jnp.int32)

    def flat(n, a):
        return a[0].T if n == "w_in" else a[0]

    def unflat(n, a):
        return (a.T if n == "w_in" else a)[None]

    shards = {n: flat(n, weights[n]).astype(BF16) for n in _BIG}
    small = {n: weights[n] for n in _ORDER if n not in _BIG}
    (gearly, gmat, glate), gx, grads = _step(x2, tgt, seq, shards, small, ids)

    def two_d(src):
        return {n: src[n].reshape(4 * POOL_GC, POOL_GC) if n == "w_pool" else src[n] for n in _SMALL_NAMES}

    loss, per = _small_update(gearly, gmat, glate, two_d(weights), two_d(mom1), two_d(mom2))
    delta, new_m, new_v = {}, {}, {}
    for n in _SMALL_NAMES:
        grads[n], delta[n], new_m[n], new_v[n] = (a.reshape(weights[n].shape) for a in per[n])
    def operands(n):
        return flat(n, weights[n]), grads[n], flat(n, mom1[n]), flat(n, mom2[n])

    done = {"w_in": (*_adamw(*operands("w_in")), grads["w_in"])}
    for tag, group in (("down", ("w_down",)), ("up", ("w_up",)), ("mix", ("w_out", "w_branch_pool", "w_branch_attn"))):
        done.update(zip(group, _adamw_sparse(tag, [operands(n) for n in group])))
    for n in _BIG:
        delta[n], new_m[n], new_v[n], grads[n] = (unflat(n, a) for a in done[n])
    gx = _copy_sparse(gx)

    return (loss[0, 0], gx.reshape(x.shape), *[grads[n] for n in _ORDER], *[delta[n] for n in _ORDER],
            *[new_m[n] for n in _ORDER], *[new_v[n] for n in _ORDER])
```

```python
import jax
import jax.numpy as jnp
from jax import lax
from jax.experimental import pallas as pl
from jax.experimental.pallas import tpu as pltpu
from jax.experimental.pallas import tpu_sc as plsc

F32 = jnp.float32
BF16 = jnp.bfloat16

D_MODEL = 1024
POOL_WINDOWS = (2, 4, 8, 16)
POOL_WIDTH = 512
POOL_GC = 128
HALO = 16
HEAD_DIM = 64
N_Q_HEADS = 8
ATTN_WIDTH = 512
KV_WIDTH = 128
BLOCK = 128
NEG_INF = -1e30
ROPE_THETA = 500000.0
ROT_DIM = 16
GATE_WIDTH = 2048
IN_WIDTH = 3328
D_FF = 4096
EPS = 1e-6
SCALE = HEAD_DIM ** -0.5
C_Q, C_K, C_V, C_G = 512, 1024, 1152, 1280

ADAM_LR, ADAM_B1, ADAM_B2, ADAM_EPS, ADAM_WD, ADAM_STEP = 0.001, 0.9, 0.999, 1e-08, 0.01, 10

N_CHIPS = 4
N_DEV = 8
LANES = 128
TM = 512
TP = 512
VMEM_MB = 56

MESH = pl.DeviceIdType.MESH
ANY = pl.BlockSpec(memory_space=pl.ANY)


def _cp(*sem, vmem=VMEM_MB):
    return pltpu.CompilerParams(dimension_semantics=sem, vmem_limit_bytes=vmem * 1024 * 1024)


def _rows(tile, cols):
    return pl.BlockSpec((tile, cols), lambda i: (i, 0))


def _const(shape):
    nd = len(shape)
    return pl.BlockSpec(shape, lambda i: (0,) * nd)


def _sds(shape, dtype):
    return jax.ShapeDtypeStruct(shape, dtype)


def _dot(a, b):
    return jnp.dot(a, b, preferred_element_type=F32)


def _dot_nt(a, b):
    return lax.dot_general(a, b, (((1,), (1,)), ((), ())), preferred_element_type=F32)


def _dot_tn(a, b):
    return lax.dot_general(a, b, (((0,), (0,)), ((), ())), preferred_element_type=F32)


def _rms(x):
    return lax.rsqrt(jnp.mean(x * x, axis=-1, keepdims=True) + EPS)


def _norm_bwd(x, g, dout):
    r = _rms(x)
    n = x * r
    dn = dout * g
    dx = r * (dn - n * jnp.mean(dn * n, axis=-1, keepdims=True))
    return dx, jnp.sum(dout * n, axis=0, keepdims=True)


def _rot_fwd(t, c, a, bt):
    return t * c + pltpu.roll(t, LANES - 8, 1) * a + pltpu.roll(t, 8, 1) * bt


def _rot_bwd(d, c, a, bt):
    return d * c + pltpu.roll(d * a, 8, 1) + pltpu.roll(d * bt, LANES - 8, 1)


def _rope_tables(seq):
    pos = jnp.arange(seq, dtype=F32)
    inv_freq = ROPE_THETA ** (-jnp.arange(0, ROT_DIM, 2, dtype=F32) / ROT_DIM)
    ang = pos[:, None] * inv_freq[None, :]
    cos, sin = jnp.cos(ang), jnp.sin(ang)
    ones = jnp.ones((seq, HEAD_DIM - ROT_DIM), F32)
    zeros8 = jnp.zeros((seq, 8), F32)
    zrest = jnp.zeros((seq, HEAD_DIM - ROT_DIM), F32)
    c = jnp.concatenate([cos, cos, ones], axis=1)
    a = jnp.concatenate([-sin, zeros8, zrest], axis=1)
    bt = jnp.concatenate([zeros8, sin, zrest], axis=1)
    return tuple(jnp.tile(t, (1, 2)) for t in (c, a, bt))


class _Exchange:
    def __init__(self, inputs, out_shapes, sems, start, finish, aliases=None, middle=None):
        self.inputs, self.out_shapes, self.sems = list(inputs), list(out_shapes), list(sems)
        self.start, self.finish, self.aliases = start, finish, dict(aliases or {})
        self.middle = middle


def _call(body, *, name, grid, in_specs, out_specs, out_shape, args, scratch=(), sem=(), exchanges=()):
    in_specs, out_specs, out_shape, scratch = list(in_specs), list(out_specs), list(out_shape), list(scratch)
    if not exchanges:
        return pl.pallas_call(body, name=name, grid=grid, in_specs=in_specs, out_specs=out_specs,
                              out_shape=out_shape, scratch_shapes=scratch, compiler_params=_cp(*sem))(*args)
    n_in, n_out, n_scr = len(in_specs), len(out_specs), len(scratch)
    x_in = [a for ex in exchanges for a in ex.inputs]
    x_out = [s for ex in exchanges for s in ex.out_shapes]
    x_sem = [s for ex in exchanges for s in ex.sems]
    aliases, i_off, o_off = {}, n_in, n_out
    for ex in exchanges:
        for i, o in ex.aliases.items():
            aliases[i_off + i] = o_off + o
        i_off += len(ex.inputs)
        o_off += len(ex.out_shapes)

    def split(flat):
        out, pos = [], 0
        for ex, n in zip(exchanges, flat[1]):
            out.append(flat[0][pos:pos + n])
            pos += n
        return out

    def carrier(*refs):
        pos = 0
        groups = []
        for n in (n_in, len(x_in), n_out, len(x_out), n_scr, len(x_sem)):
            groups.append(refs[pos:pos + n])
            pos += n
        ins, xin, outs, xout, scr, xsem = groups
        xin = split((xin, [len(ex.inputs) for ex in exchanges]))
        xout = split((xout, [len(ex.out_shapes) for ex in exchanges]))
        xsem = split((xsem, [len(ex.sems) for ex in exchanges]))
        first = pl.program_id(0) == 0
        last = pl.program_id(0) == grid[0] - 1
        for d in range(1, len(grid)):
            first = jnp.logical_and(first, pl.program_id(d) == 0)
            last = jnp.logical_and(last, pl.program_id(d) == grid[d] - 1)

        @pl.when(first)
        def _():
            for ex, i, o, s in zip(exchanges, xin, xout, xsem):
                ex.start(i, o, s)

        if any(ex.middle for ex in exchanges):
            half = pl.program_id(0) == grid[0] // 2
            for d in range(1, len(grid)):
                half = jnp.logical_and(half, pl.program_id(d) == 0)

            @pl.when(half)
            def _():
                for ex, i, o, s in zip(exchanges, xin, xout, xsem):
                    if ex.middle:
                        ex.middle(i, o, s)

        body(*ins, *outs, *scr)

        @pl.when(last)
        def _():
            for ex, i, o, s in zip(exchanges, xin, xout, xsem):
                ex.finish(i, o, s)

    res = pl.pallas_call(
        carrier, name=name, grid=grid, in_specs=in_specs + [ANY] * len(x_in),
        out_specs=out_specs + [ANY] * len(x_out), out_shape=out_shape + x_out,
        scratch_shapes=scratch + x_sem, input_output_aliases=aliases,
        compiler_params=_cp(*(["arbitrary"] * len(grid))),
    )(*args, *x_in)
    return res[:n_out], split((res[n_out:], [len(ex.out_shapes) for ex in exchanges]))


def _alone(name, *exchanges):
    n_in = [len(ex.inputs) for ex in exchanges]
    n_out = [len(ex.out_shapes) for ex in exchanges]
    n_sem = [len(ex.sems) for ex in exchanges]
    aliases, i_off, o_off = {}, 0, 0
    for ex in exchanges:
        for i, o in ex.aliases.items():
            aliases[i_off + i] = o_off + o
        i_off += len(ex.inputs)
        o_off += len(ex.out_shapes)

    def split(flat, counts):
        out, pos = [], 0
        for n in counts:
            out.append(flat[pos:pos + n])
            pos += n
        return out

    def body(*refs):
        ins, outs, sems = split(refs, [sum(n_in), sum(n_out), sum(n_sem)])
        groups = list(zip(exchanges, split(ins, n_in), split(outs, n_out), split(sems, n_sem)))
        for ex, i, o, s in groups:
            ex.start(i, o, s)
        for ex, i, o, s in groups:
            if ex.middle:
                ex.middle(i, o, s)
        for ex, i, o, s in groups:
            ex.finish(i, o, s)

    res = pl.pallas_call(
        body, name=name, in_specs=[ANY] * sum(n_in), out_specs=[ANY] * sum(n_out),
        out_shape=[s for ex in exchanges for s in ex.out_shapes],
        scratch_shapes=[s for ex in exchanges for s in ex.sems], input_output_aliases=aliases,
    )(*[a for ex in exchanges for a in ex.inputs])
    return split(res, n_out)


def _place():
    x, y, c = lax.axis_index("x"), lax.axis_index("y"), lax.axis_index("c")
    chips = [(1 - x, y), (x, 1 - y), (1 - x, 1 - y)]
    return x, y, c, chips


def _remote(src, dst, send, recv, to):
    return pltpu.make_async_remote_copy(src_ref=src, dst_ref=dst, send_sem=send, recv_sem=recv,
                                        device_id=to, device_id_type=MESH)


def _ex_gather(shards):
    nw = len(shards)
    hrs = [s.shape[0] // 2 for s in shards]

    def copies(ins, outs, sems):
        s1, r1, s2, r2, fs, fr = sems
        x, y, c, _ = _place()
        me, xn, yn, dg = (x, y), (1 - x, y), (x, 1 - y), (1 - x, 1 - y)
        nbr = (xn, yn)
        sibling = (x, y, 1 - c)

        def piece(w, chip, core, part=None):
            hr = hrs[w]
            rows = pl.ds(core * hr, hr) if part is None else pl.ds(core * hr + part * (hr // 2), hr // 2)
            return outs[w].at[2 * chip[0] + chip[1], rows]

        def first(w, k):
            return _remote(ins[w].at[pl.ds(c * hrs[w], hrs[w])], piece(w, me, c), s1.at[w, k], r1.at[w, k],
                           (*nbr[k], c))

        def landed(w, k):
            return _remote(piece(w, nbr[k], c), piece(w, nbr[k], c), s1.at[w, k], r1.at[w, k], (*nbr[k], c))

        def onward(w, k):
            return _remote(piece(w, nbr[k], c, k), piece(w, nbr[k], c, k), s2.at[w, k], r2.at[w, k],
                           (*nbr[1 - k], c))

        def arrived(w, k):
            return _remote(piece(w, dg, c, k), piece(w, dg, c, k), s2.at[w, k], r2.at[w, k], (*nbr[1 - k], c))

        def passed(w, j):
            chip = (xn, yn, dg)[j]
            return _remote(piece(w, chip, c), piece(w, chip, c), fs.at[w, j], fr.at[w, j], sibling)

        def handed(w, j):
            chip = (xn, yn, dg)[j]
            return _remote(piece(w, chip, 1 - c), piece(w, chip, 1 - c), fs.at[w, j], fr.at[w, j], sibling)

        return first, landed, onward, arrived, passed, handed

    def start(ins, outs, sems):
        first = copies(ins, outs, sems)[0]
        for w in range(nw):
            for k in range(2):
                first(w, k).start()

    def middle(ins, outs, sems):
        _, landed, onward, _, passed, _ = copies(ins, outs, sems)
        for w in range(nw):
            for k in range(2):
                landed(w, k).wait_recv()
                onward(w, k).start()
                passed(w, k).start()

    def finish(ins, outs, sems):
        first, _, onward, arrived, passed, handed = copies(ins, outs, sems)
        for w in range(nw):
            for k in range(2):
                arrived(w, k).wait_recv()
            passed(w, 2).start()
        for w in range(nw):
            for j in range(3):
                handed(w, j).wait_recv()
        for w in range(nw):
            for k in range(2):
                first(w, k).wait_send()
                onward(w, k).wait_send()
            for j in range(3):
                passed(w, j).wait_send()

    return _Exchange(shards, [_sds((N_CHIPS,) + s.shape, s.dtype) for s in shards],
                     [pltpu.SemaphoreType.DMA((nw, 2))] * 4 + [pltpu.SemaphoreType.DMA((nw, 3))] * 2,
                     start, finish, middle=middle)


def _ex_pair(grads):
    nw = len(grads)

    def copies(ins, outs, sems):
        x, y, c, _ = _place()
        out = []
        for w in range(nw):
            hr = grads[w].shape[1] // 2
            out.append(_remote(ins[w].at[:, pl.ds((1 - c) * hr, hr)], outs[w], sems[0].at[w], sems[1].at[w],
                               (x, y, 1 - c)))
        return out

    def start(ins, outs, sems):
        for cp in copies(ins, outs, sems):
            cp.start()

    def finish(ins, outs, sems):
        for cp in copies(ins, outs, sems):
            cp.wait()

    return _Exchange(grads, [_sds((N_CHIPS, g.shape[1] // 2, g.shape[2]), F32) for g in grads],
                     [pltpu.SemaphoreType.DMA((nw,))] * 2, start, finish)


def _ex_chip(pieces):
    nw = len(pieces)

    def copies(ins, outs, sems):
        x, y, c, chips = _place()
        return [_remote(ins[w].at[2 * cx + cy], outs[w].at[k], sems[0].at[w, k], sems[1].at[w, k], (cx, cy, c))
                for w in range(nw) for k, (cx, cy) in enumerate(chips)]

    def start(ins, outs, sems):
        for cp in copies(ins, outs, sems):
            cp.start()

    def finish(ins, outs, sems):
        for cp in copies(ins, outs, sems):
            cp.wait()

    return _Exchange(pieces, [_sds((3,) + p.shape[1:], BF16) for p in pieces],
                     [pltpu.SemaphoreType.DMA((nw, 3))] * 2, start, finish)


def _ex_swap(fulls):
    nw = len(fulls)

    def start(ins, outs, sems):
        x, y, c, _ = _place()
        for w in range(nw):
            hr = fulls[w].shape[0] // 2
            mine = pl.ds(c * hr, hr)
            _remote(ins[w].at[mine], outs[w].at[mine], sems[0].at[w], sems[1].at[w], (x, y, 1 - c)).start()

    def finish(ins, outs, sems):
        x, y, c, _ = _place()
        for w in range(nw):
            hr = fulls[w].shape[0] // 2
            mine, theirs = pl.ds(c * hr, hr), pl.ds((1 - c) * hr, hr)
            _remote(ins[w].at[mine], outs[w].at[mine], sems[0].at[w], sems[1].at[w], (x, y, 1 - c)).wait_send()
            _remote(ins[w].at[theirs], outs[w].at[theirs], sems[0].at[w], sems[1].at[w], (x, y, 1 - c)).wait_recv()

    return _Exchange(fulls, [_sds(f.shape, F32) for f in fulls], [pltpu.SemaphoreType.DMA((nw,))] * 2,
                     start, finish, aliases={w: w for w in range(nw)})


def _ex_allgather(blocks):
    nb = len(blocks)

    def copies(ins, outs, sems):
        send, recv, lsem = sems
        x, y, c, chips = _place()
        me, sibling = (x, y, c), (x, y, 1 - c)

        def rows(b, px, py, pc):
            m_per = blocks[b].shape[0]
            return outs[b].at[pl.ds((4 * px + 2 * py + pc) * m_per, m_per), :]

        def copy(b, k, blk, to, src=None):
            return _remote(rows(b, *blk) if src is None else src, rows(b, *blk), send.at[b, k], recv.at[b, k], to)

        def mine(b):
            return pltpu.make_async_copy(ins[b], rows(b, *me), lsem.at[b])

        def first(b, k):
            return copy(b, k, me, sibling if k == 0 else (*chips[k - 1], c), src=ins[b])

        def passed(b, j):
            return copy(b, 4 + j, (*chips[j], c), sibling)

        def landed(b, j):
            return copy(b, 1 + j, (*chips[j], c), me)

        def handed(b, k):
            return copy(b, 0, sibling, me) if k == 0 else copy(b, 3 + k, (*chips[k - 1], 1 - c), me)

        return mine, first, passed, landed, handed

    def start(ins, outs, sems):
        mine, first, _, _, _ = copies(ins, outs, sems)
        for b in range(nb):
            mine(b).start()
            for k in range(4):
                first(b, k).start()

    def finish(ins, outs, sems):
        mine, first, passed, landed, handed = copies(ins, outs, sems)
        sent = []
        for b in range(nb):
            for j in range(3):
                landed(b, j).wait_recv()
                cp = passed(b, j)
                cp.start()
                sent.append(cp)
        for b in range(nb):
            for k in range(4):
                handed(b, k).wait_recv()
            for k in range(4):
                first(b, k).wait_send()
        for cp in sent:
            cp.wait_send()
        for b in range(nb):
            mine(b).wait()

    return _Exchange(blocks, [_sds((N_DEV * b.shape[0], b.shape[1]), F32) for b in blocks],
                     [pltpu.SemaphoreType.DMA((nb, 7)), pltpu.SemaphoreType.DMA((nb, 7)), pltpu.SemaphoreType.DMA((nb,))],
                     start, finish)


def _inproj(x2, g1, w_in_t, b_in, tabs, seq, exchanges=()):
    T = x2.shape[0]
    tm = min(TM, seq)
    nseq = seq // tm

    def body(x_ref, g_ref, w_ref, b_ref, c_ref, a_ref, bt_ref, h_ref, u_ref, q_ref, k_ref, v_ref, gate_ref):
        x = x_ref[...]
        h = (x * _rms(x) * g_ref[...]).astype(BF16)
        h_ref[...] = h

        def proj(lo, hi):
            return _dot_nt(h, w_ref[lo:hi, :]) + b_ref[:, lo:hi]

        c, a, bt = c_ref[...], a_ref[...], bt_ref[...]
        u_ref[...] = proj(0, C_Q)
        q = proj(C_Q, C_K)
        for p in range(4):
            sl = slice(LANES * p, LANES * (p + 1))
            q_ref[:, sl] = (_rot_fwd(q[:, sl], c, a, bt) * SCALE).astype(BF16)
        kv = proj(C_K, C_G)
        k_ref[...] = _rot_fwd(kv[:, :KV_WIDTH], c, a, bt).astype(BF16)
        v_ref[...] = kv[:, KV_WIDTH:].astype(BF16)
        for j in range(2):
            lo = C_G + D_MODEL * j
            gate_ref[:, D_MODEL * j:D_MODEL * (j + 1)] = jax.nn.sigmoid(proj(lo, lo + D_MODEL)).astype(BF16)

    tab = pl.BlockSpec((tm, LANES), lambda i: (i % nseq, 0))
    return _call(
        body, name="inproj", grid=(T // tm,),
        in_specs=[_rows(tm, D_MODEL), _const((1, D_MODEL)), _const((IN_WIDTH, D_MODEL)), _const((1, IN_WIDTH)),
                  tab, tab, tab],
        out_specs=[_rows(tm, D_MODEL), _rows(tm, POOL_WIDTH), _rows(tm, ATTN_WIDTH), _rows(tm, KV_WIDTH),
                   _rows(tm, KV_WIDTH), _rows(tm, GATE_WIDTH)],
        out_shape=[_sds((T, D_MODEL), BF16), _sds((T, POOL_WIDTH), F32), _sds((T, ATTN_WIDTH), BF16),
                   _sds((T, KV_WIDTH), BF16), _sds((T, KV_WIDTH), BF16), _sds((T, GATE_WIDTH), BF16)],
        args=(x2, g1, w_in_t, b_in, *tabs), sem=("parallel",), exchanges=exchanges)


def _inv_count(pos, w):
    return 1.0 / jnp.minimum(pos + 1, w).astype(F32)


def _pool_fwd(u, w_pool, pool_scale, seq):
    T = u.shape[0]
    tp = min(TP, seq)
    nseq = seq // tp
    per = tp // HALO

    def body(u_ref, prev_ref, w_ref, s_ref, diff_ref, y_ref):
        i = pl.program_id(0)
        first = (i % nseq) == 0
        prev = jnp.where(first, 0.0, prev_ref[...])
        ext = jnp.concatenate([prev, u_ref[...]], axis=0)
        pos = (i % nseq) * tp + lax.broadcasted_iota(jnp.int32, (tp, 1), 0)
        for gi, w in enumerate(POOL_WINDOWS):
            sl = slice(POOL_GC * gi, POOL_GC * (gi + 1))
            xg = ext[:, sl]
            s = xg
            sh = 1
            while sh < w:
                s = s + pltpu.roll(s, sh, 0)
                sh *= 2
            pooled = s[HALO:] * _inv_count(pos, w)
            diff = (pooled - xg[HALO:]).astype(BF16)
            diff_ref[:, sl] = diff
            mixed = _dot(diff, w_ref[gi].astype(BF16))
            y_ref[:, sl] = (mixed * s_ref[:, sl]).astype(BF16)

    return _call(
        body, name="pool_fwd", grid=(T // tp,),
        in_specs=[_rows(tp, POOL_WIDTH),
                  pl.BlockSpec((HALO, POOL_WIDTH), lambda i: (jnp.maximum(i * per - 1, 0), 0)),
                  _const((4, POOL_GC, POOL_GC)), _const((1, POOL_WIDTH))],
        out_specs=[_rows(tp, POOL_WIDTH), _rows(tp, POOL_WIDTH)],
        out_shape=[_sds((T, POOL_WIDTH), BF16), _sds((T, POOL_WIDTH), BF16)],
        args=(u, u, w_pool, pool_scale), sem=("parallel",))


GROUP = 4
GROWS = GROUP * BLOCK


def _attn_masks(n):
    qi = lax.broadcasted_iota(jnp.int32, (GROWS, 2 * BLOCK), 0) % BLOCK
    kj = lax.broadcasted_iota(jnp.int32, (GROWS, 2 * BLOCK), 1)
    rel = qi + BLOCK - kj
    valid = (rel >= 0) & (rel < BLOCK) & (kj >= jnp.where(n > 0, 0, BLOCK))
    lo = lax.broadcasted_iota(jnp.int32, (BLOCK, LANES), 1) < HEAD_DIM
    return valid, lo


def _by_example(bl, *arrays):
    return [a.reshape(bl, a.shape[0] // bl, a.shape[1]) for a in arrays]


def _stack_heads(ref, h, lo):
    keep = lo if h == 0 else jnp.logical_not(lo)
    pieces = []
    for p in (2 * h, 2 * h + 1):
        xp = ref[:, LANES * p:LANES * (p + 1)].astype(F32)
        for e in range(2):
            t = xp if e == h else pltpu.roll(xp, HEAD_DIM, 1)
            pieces.append(jnp.where(keep, t, 0.0).astype(BF16))
    return jnp.concatenate(pieces, axis=0)


def _unstack_heads(stacked, h, lo):
    pairs = []
    for j in range(2):
        parts = []
        for e in range(2):
            t = stacked[BLOCK * (2 * j + e):BLOCK * (2 * j + e + 1)]
            parts.append(t if e == h else pltpu.roll(t, HEAD_DIM, 1))
        pairs.append(jnp.where(lo, parts[0], parts[1]))
    return pairs


def _sink_rows(sink_ref, h):
    head = lax.broadcasted_iota(jnp.int32, (GROWS, 1), 0) // BLOCK
    col = jnp.zeros((GROWS, 1), F32) + sink_ref[GROUP * h]
    for g in range(1, GROUP):
        col = jnp.where(head == g, sink_ref[GROUP * h + g], col)
    return col


def _group_probs(qs, kk, valid, sink):
    s = jnp.where(valid, _dot_nt(qs, kk), NEG_INF)
    m = jnp.maximum(jnp.max(s, axis=1, keepdims=True), sink)
    ex = jnp.exp(s - m)
    es = jnp.exp(sink - m)
    inv = 1.0 / (jnp.sum(ex, axis=1, keepdims=True) + es)
    return ex * inv, es * inv


def _attn_fwd(q, k, v, sinks, seq, exchanges=()):
    T = q.shape[0]
    nb = seq // BLOCK
    bl = T // seq

    def body(sink_ref, q_ref, kp_ref, kc_ref, vp_ref, vc_ref, o_ref):
        valid, lo = _attn_masks(pl.program_id(0))
        for b in range(bl):
            kk = jnp.concatenate([kp_ref[b], kc_ref[b]], axis=0)
            vv = jnp.concatenate([vp_ref[b], vc_ref[b]], axis=0)
            for h in range(2):
                qs = _stack_heads(q_ref.at[b], h, lo)
                pr, _ = _group_probs(qs, kk, valid, _sink_rows(sink_ref, h))
                o = _dot(pr.astype(BF16), vv)
                for j, pair in enumerate(_unstack_heads(o, h, lo)):
                    p = 2 * h + j
                    o_ref[b, :, LANES * p:LANES * (p + 1)] = pair.astype(BF16)

    cur = lambda n: (0, n, 0)
    prv = lambda n: (0, jnp.maximum(n - 1, 0), 0)
    kv = lambda m: pl.BlockSpec((bl, BLOCK, KV_WIDTH), m)
    res = _call(
        body, name="attn_fwd", grid=(nb,),
        in_specs=[pl.BlockSpec(memory_space=pltpu.SMEM), pl.BlockSpec((bl, BLOCK, ATTN_WIDTH), cur),
                  kv(prv), kv(cur), kv(prv), kv(cur)],
        out_specs=[pl.BlockSpec((bl, BLOCK, ATTN_WIDTH), cur)],
        out_shape=[_sds((bl, seq, ATTN_WIDTH), BF16)],
        args=(sinks, *_by_example(bl, q, k, k, v, v)), sem=("parallel",), exchanges=exchanges)
    if exchanges:
        return [res[0][0].reshape(T, ATTN_WIDTH)], res[1]
    return [res[0].reshape(T, ATTN_WIDTH)]


def _branch(y, w_ref):
    return jnp.concatenate([_dot(y, w_ref[j]) for j in range(N_CHIPS)], axis=1)


def _merge_out(y_pool, y_attn, gate, x2, w_bp, w_ba, w_out, g2, g3, exchanges=()):
    T = x2.shape[0]
    tm = min(TM, T)

    def body(yp_ref, ya_ref, gate_ref, x_ref, wbp_ref, wba_ref, wo_ref, g2_ref, g3_ref,
             mg_ref, mix_ref, x1_ref, h2_ref):
        bp, ba = _branch(yp_ref[...], wbp_ref), _branch(ya_ref[...], wba_ref)
        merged = (gate_ref[:, :D_MODEL].astype(F32) * bp + gate_ref[:, D_MODEL:].astype(F32) * ba).astype(BF16)
        mg_ref[...] = merged
        mix = _dot(merged, wo_ref[...])
        mix_ref[...] = mix
        x1 = x_ref[...] + mix * _rms(mix) * g2_ref[...]
        x1_ref[...] = x1
        h2_ref[...] = (x1 * _rms(x1) * g3_ref[...]).astype(BF16)

    return _call(
        body, name="merge_out", grid=(T // tm,),
        in_specs=[_rows(tm, POOL_WIDTH), _rows(tm, ATTN_WIDTH), _rows(tm, GATE_WIDTH), _rows(tm, D_MODEL),
                  _const(w_bp.shape), _const(w_ba.shape), _const((D_MODEL, D_MODEL)),
                  _const((1, D_MODEL)), _const((1, D_MODEL))],
        out_specs=[_rows(tm, D_MODEL)] * 4,
        out_shape=[_sds((T, D_MODEL), BF16), _sds((T, D_MODEL), F32), _sds((T, D_MODEL), F32),
                   _sds((T, D_MODEL), BF16)],
        args=(y_pool, y_attn, gate, x2, w_bp, w_ba, w_out, g2, g3), sem=("parallel",), exchanges=exchanges)


HALF = D_MODEL // 2
TM_MLP = 256


def _mlp_core(h2, x1, mix, tgt, w_up, w_down, g4, g3, g2):
    T = h2.shape[0]
    tm = min(TM_MLP, T)

    def body(h_ref, x1_ref, mix_ref, t_ref, g_ref, g3_ref, g2_ref, ua_hbm, ub_hbm, da_hbm, db_hbm,
             act_ref, dff_ref, dup_ref, dx1_ref, dmix_ref, loss_ref, dg_ref, dg3_ref, dg2_ref,
             wu, wd, relu_scr, sems):
        def weight_copy(i):
            src, dst = ((ua_hbm, wu.at[:, :HALF]), (ub_hbm, wu.at[:, HALF:]),
                        (da_hbm, wd.at[:, :HALF]), (db_hbm, wd.at[:, HALF:]))[i]
            return pltpu.make_async_copy(src, dst, sems.at[i])

        @pl.when(pl.program_id(0) == 0)
        def _():
            for i in range(4):
                weight_copy(i).start()
            loss_ref[...] = jnp.zeros_like(loss_ref)
            for ref in (dg_ref, dg3_ref, dg2_ref):
                ref[...] = jnp.zeros_like(ref)
            weight_copy(0).wait()
            weight_copy(1).wait()

        h = h_ref[...]
        ff = None
        for j in range(N_CHIPS):
            lo = D_MODEL * j
            relu = jnp.maximum(_dot(h, wu[j]), 0.0)
            if j == 0:
                @pl.when(pl.program_id(0) == 0)
                def _():
                    weight_copy(2).wait()
                    weight_copy(3).wait()
            relu_scr[:, lo:lo + D_MODEL] = relu
            act = jnp.square(relu).astype(BF16)
            act_ref[:, lo:lo + D_MODEL] = act
            t = _dot(act, wd[j])
            ff = t if ff is None else ff + t
        g = g_ref[...]
        x1 = x1_ref[...]
        err = x1 + ff * _rms(ff) * g - t_ref[...]
        loss_ref[...] += jnp.sum(err * err) * (0.5 / D_MODEL)
        dy = err * (1.0 / D_MODEL)
        dff, dg = _norm_bwd(ff, g, dy)
        dg_ref[...] += dg
        dff = dff.astype(BF16)
        dff_ref[...] = dff
        dh2 = None
        for j in range(N_CHIPS):
            lo = D_MODEL * j
            dup = (_dot_nt(dff, wd[j]) * (2.0 * relu_scr[:, lo:lo + D_MODEL])).astype(BF16)
            dup_ref[:, lo:lo + D_MODEL] = dup
            t = _dot_nt(dup, wu[j])
            dh2 = t if dh2 is None else dh2 + t
        dx, dg3 = _norm_bwd(x1, g3_ref[...], dh2)
        dx1 = dy + dx
        dx1_ref[...] = dx1
        dg3_ref[...] += dg3
        dmix, dg2 = _norm_bwd(mix_ref[...], g2_ref[...], dx1)
        dmix_ref[...] = dmix.astype(BF16)
        dg2_ref[...] += dg2

    slabs = pltpu.VMEM((N_CHIPS, D_MODEL, D_MODEL), BF16)
    gain = _const((1, D_MODEL))
    return pl.pallas_call(
        body, name="mlp_core", grid=(T // tm,),
        in_specs=[_rows(tm, D_MODEL)] * 4 + [gain] * 3 + [ANY] * 4,
        out_specs=[_rows(tm, D_FF), _rows(tm, D_MODEL), _rows(tm, D_FF), _rows(tm, D_MODEL), _rows(tm, D_MODEL),
                   _const((8, LANES)), gain, gain, gain],
        out_shape=[_sds((T, D_FF), BF16), _sds((T, D_MODEL), BF16), _sds((T, D_FF), BF16), _sds((T, D_MODEL), F32),
                   _sds((T, D_MODEL), BF16), _sds((8, LANES), F32)] + [_sds((1, D_MODEL), F32)] * 3,
        scratch_shapes=[slabs] * 2 + [pltpu.VMEM((tm, D_FF), F32), pltpu.SemaphoreType.DMA((4,))],
        compiler_params=_cp("arbitrary"),
    )(h2, x1, mix, tgt, g4, g3, g2, *w_up, *w_down)


def _dw(tag, a, g, ta, tn, shard_cols=False, exchanges=()):
    T, ka = a.shape
    n = g.shape[1]
    tk = min(2 * TM, T)
    nk = T // tk

    def body(a_ref, g_ref, o_ref):
        @pl.when(pl.program_id(2) == 0)
        def _():
            o_ref[...] = jnp.zeros_like(o_ref)

        o_ref[...] += _dot_tn(a_ref[...], g_ref[...])

    if shard_cols:
        per = (n // N_CHIPS) // tn
        out_spec = pl.BlockSpec((None, ta, tn), lambda i, j, k: (j // per, i, j % per))
        out_shape = _sds((N_CHIPS, ka, n // N_CHIPS), F32)
    else:
        out_spec = pl.BlockSpec((ta, tn), lambda i, j, k: (i, j))
        out_shape = _sds((ka, n), F32)
    return _call(
        body, name="dw_" + tag, grid=(ka // ta, n // tn, nk),
        in_specs=[pl.BlockSpec((tk, ta), lambda i, j, k: (k, i)), pl.BlockSpec((tk, tn), lambda i, j, k: (k, j))],
        out_specs=[out_spec], out_shape=[out_shape],
        args=(a, g), sem=("parallel", "parallel", "arbitrary"), exchanges=exchanges)


def _dw_mix(merged, dmix, y_pool, dbp, y_attn, dba, exchanges=()):
    T = merged.shape[0]
    tk = min(2 * TM, T)
    c = D_MODEL // N_CHIPS

    def body(mg_ref, dmix_ref, yp_ref, dbp_ref, ya_ref, dba_ref, out_ref, bp_ref, ba_ref):
        @pl.when(pl.program_id(0) == 0)
        def _():
            for ref in (out_ref, bp_ref, ba_ref):
                ref[...] = jnp.zeros_like(ref)

        out_ref[...] += _dot_tn(mg_ref[...], dmix_ref[...])
        for y_ref, d_ref, o_ref in ((yp_ref, dbp_ref, bp_ref), (ya_ref, dba_ref, ba_ref)):
            res = _dot_tn(y_ref[...], d_ref[...])
            for j in range(N_CHIPS):
                o_ref[j] += res[:, c * j:c * (j + 1)]

    slabs = (N_CHIPS, POOL_WIDTH, c)
    return _call(
        body, name="dw_mix", grid=(T // tk,),
        in_specs=[_rows(tk, D_MODEL), _rows(tk, D_MODEL), _rows(tk, POOL_WIDTH), _rows(tk, D_MODEL),
                  _rows(tk, ATTN_WIDTH), _rows(tk, D_MODEL)],
        out_specs=[_const((D_MODEL, D_MODEL)), _const(slabs), _const(slabs)],
        out_shape=[_sds((D_MODEL, D_MODEL), F32), _sds(slabs, F32), _sds(slabs, F32)],
        args=(merged, dmix, y_pool, dbp, y_attn, dba), sem=("arbitrary",), exchanges=exchanges)


def _merge_bwd(dmix, gate, y_pool, y_attn, w_out, w_bp, w_ba, exchanges=()):
    T = dmix.shape[0]
    tm = min(TM, T)

    def body(dmix_ref, gate_ref, yp_ref, ya_ref, wo_ref, wbp_ref, wba_ref,
             dbp_ref, dba_ref, dgate_ref, dyp_ref, dya_ref):
        dm = _dot_nt(dmix_ref[...], wo_ref[...])
        for j, (y_ref, db_ref, w_ref, dy_ref) in enumerate(
                ((yp_ref, dbp_ref, wbp_ref, dyp_ref), (ya_ref, dba_ref, wba_ref, dya_ref))):
            sl = slice(D_MODEL * j, D_MODEL * (j + 1))
            gt = gate_ref[:, sl].astype(F32)
            db = (dm * gt).astype(BF16)
            db_ref[...] = db
            dgate_ref[:, sl] = (dm * _branch(y_ref[...], w_ref) * gt * (1.0 - gt)).astype(BF16)
            cw = D_MODEL // N_CHIPS
            dy = _dot_nt(db[:, :cw], w_ref[0])
            for c in range(1, N_CHIPS):
                dy = dy + _dot_nt(db[:, cw * c:cw * (c + 1)], w_ref[c])
            dy_ref[...] = dy.astype(dy_ref.dtype)

    return _call(
        body, name="merge_bwd", grid=(T // tm,),
        in_specs=[_rows(tm, D_MODEL), _rows(tm, GATE_WIDTH), _rows(tm, POOL_WIDTH), _rows(tm, ATTN_WIDTH),
                  _const((D_MODEL, D_MODEL)), _const(w_bp.shape), _const(w_ba.shape)],
        out_specs=[_rows(tm, D_MODEL), _rows(tm, D_MODEL), _rows(tm, GATE_WIDTH), _rows(tm, POOL_WIDTH),
                   _rows(tm, ATTN_WIDTH)],
        out_shape=[_sds((T, D_MODEL), BF16), _sds((T, D_MODEL), BF16), _sds((T, GATE_WIDTH), BF16),
                   _sds((T, POOL_WIDTH), F32), _sds((T, ATTN_WIDTH), BF16)],
        args=(dmix, gate, y_pool, y_attn, w_out, w_bp, w_ba), sem=("parallel",), exchanges=exchanges)


def _attn_bwd(q, k, v, do, sinks, tabs, seq, exchanges=()):
    T = q.shape[0]
    nb = seq // BLOCK
    bl = T // seq
    steps = nb + 1

    def body(sink_ref, q_ref, do_ref, kp_ref, kc_ref, vp_ref, vc_ref, c_ref, a_ref, bt_ref, cp_ref, ap_ref, btp_ref,
             dq_ref, dk_ref, dv_ref, dsink_ref, ck_ref, cv_ref):
        n = pl.program_id(0)

        @pl.when(n == 0)
        def _():
            dsink_ref[...] = jnp.zeros_like(dsink_ref)
            ck_ref[...] = jnp.zeros_like(ck_ref)
            cv_ref[...] = jnp.zeros_like(cv_ref)

        @pl.when(n < nb)
        def _():
            valid, lo = _attn_masks(n)
            for b in range(bl):
                kk = jnp.concatenate([kp_ref[b], kc_ref[b]], axis=0)
                vv = jnp.concatenate([vp_ref[b], vc_ref[b]], axis=0)
                dk_acc = jnp.zeros((2 * BLOCK, KV_WIDTH), F32)
                dv_acc = jnp.zeros((2 * BLOCK, KV_WIDTH), F32)
                for h in range(2):
                    qs = _stack_heads(q_ref.at[b], h, lo)
                    dos = _stack_heads(do_ref.at[b], h, lo)
                    pr, ps = _group_probs(qs, kk, valid, _sink_rows(sink_ref, h))
                    dp = _dot_nt(dos, vv)
                    delta = jnp.sum(pr * dp, axis=1, keepdims=True)
                    ds = (pr * (dp - delta)).astype(BF16)
                    dsk = ps * delta
                    for g in range(GROUP):
                        idx = GROUP * h + g
                        dsink_ref[idx:idx + 1, :] += (jnp.zeros((1, LANES), F32)
                                                      - jnp.sum(dsk[BLOCK * g:BLOCK * (g + 1)]))
                    dk_acc = dk_acc + _dot_tn(ds, qs)
                    dv_acc = dv_acc + _dot_tn(pr.astype(BF16), dos)
                    for j, pair in enumerate(_unstack_heads(_dot(ds, kk) * SCALE, h, lo)):
                        sl = slice(LANES * (2 * h + j), LANES * (2 * h + j + 1))
                        dq_ref[b, :, sl] = _rot_bwd(pair, c_ref[...], a_ref[...], bt_ref[...]).astype(BF16)
                fin_k = ck_ref[b] + dk_acc[:BLOCK]
                dk_ref[b] = _rot_bwd(fin_k, cp_ref[...], ap_ref[...], btp_ref[...]).astype(BF16)
                dv_ref[b] = (cv_ref[b] + dv_acc[:BLOCK]).astype(BF16)
                ck_ref[b] = dk_acc[BLOCK:]
                cv_ref[b] = dv_acc[BLOCK:]

        @pl.when(n == nb)
        def _():
            for b in range(bl):
                dk_ref[b] = _rot_bwd(ck_ref[b], cp_ref[...], ap_ref[...], btp_ref[...]).astype(BF16)
                dv_ref[b] = cv_ref[b].astype(BF16)

    cur = lambda n: (0, jnp.minimum(n, nb - 1), 0)
    prv = lambda n: (0, jnp.clip(n - 1, 0, nb - 1), 0)
    tcur = lambda n: (jnp.minimum(n, nb - 1), 0)
    tprv = lambda n: (jnp.clip(n - 1, 0, nb - 1), 0)
    wide = lambda m: pl.BlockSpec((bl, BLOCK, ATTN_WIDTH), m)
    kv = lambda m: pl.BlockSpec((bl, BLOCK, KV_WIDTH), m)
    tab = lambda m: pl.BlockSpec((BLOCK, LANES), m)
    res = _call(
        body, name="attn_bwd", grid=(steps,),
        in_specs=[pl.BlockSpec(memory_space=pltpu.SMEM), wide(cur), wide(cur), kv(prv), kv(cur), kv(prv), kv(cur),
                  tab(tcur), tab(tcur), tab(tcur), tab(tprv), tab(tprv), tab(tprv)],
        out_specs=[wide(cur), kv(prv), kv(prv), _const((8, LANES))],
        out_shape=[_sds((bl, seq, ATTN_WIDTH), BF16), _sds((bl, seq, KV_WIDTH), BF16),
                   _sds((bl, seq, KV_WIDTH), BF16), _sds((8, LANES), F32)],
        scratch=[pltpu.VMEM((bl, BLOCK, KV_WIDTH), F32), pltpu.VMEM((bl, BLOCK, KV_WIDTH), F32)],
        args=(sinks, *_by_example(bl, q, do, k, k, v, v), *tabs, *tabs), sem=("arbitrary",), exchanges=exchanges)
    outs, rest = (res if exchanges else (res, None))
    outs = [outs[0].reshape(T, ATTN_WIDTH), outs[1].reshape(T, KV_WIDTH), outs[2].reshape(T, KV_WIDTH), outs[3]]
    return (outs, rest) if exchanges else outs


def _pool_bwd(dyp, diff, w_pool, pool_scale, seq, exchanges=()):
    T = dyp.shape[0]
    tp = min(TP, seq)
    nseq = seq // tp
    per = tp // HALO
    last_halo = T // HALO - 1

    def body(dy_ref, nxt_ref, diff_ref, w_ref, s_ref, du_ref, dw_ref, ds_ref):
        i = pl.program_id(0)

        @pl.when(i == 0)
        def _():
            dw_ref[...] = jnp.zeros_like(dw_ref)
            ds_ref[...] = jnp.zeros_like(ds_ref)

        last = (i % nseq) == nseq - 1
        nxt = jnp.where(last, 0.0, nxt_ref[...])
        ext = jnp.concatenate([dy_ref[...], nxt], axis=0) * s_ref[...]
        pos = (i % nseq) * tp + lax.broadcasted_iota(jnp.int32, (tp + HALO, 1), 0)
        for gi, w in enumerate(POOL_WINDOWS):
            sl = slice(POOL_GC * gi, POOL_GC * (gi + 1))
            wg = w_ref[gi].astype(BF16)
            dmx = ext[:, sl].astype(BF16)
            ddiff = _dot_nt(dmx, wg)
            s = ddiff * _inv_count(pos, w)
            sh = 1
            while sh < w:
                s = s + pltpu.roll(s, tp + HALO - sh, 0)
                sh *= 2
            du_ref[:, sl] = (s[:tp] - ddiff[:tp]).astype(BF16)
            dg = diff_ref[:, sl]
            dw_ref[gi] += _dot_tn(dg, dmx[:tp])
            ds_ref[:, sl] += jnp.sum(dy_ref[:, sl] * _dot(dg, wg), axis=0, keepdims=True)

    return _call(
        body, name="pool_bwd", grid=(T // tp,),
        in_specs=[_rows(tp, POOL_WIDTH),
                  pl.BlockSpec((HALO, POOL_WIDTH), lambda i: (jnp.minimum((i + 1) * per, last_halo), 0)),
                  _rows(tp, POOL_WIDTH), _const((4, POOL_GC, POOL_GC)), _const((1, POOL_WIDTH))],
        out_specs=[_rows(tp, POOL_WIDTH), _const((4, POOL_GC, POOL_GC)), _const((1, POOL_WIDTH))],
        out_shape=[_sds((T, POOL_WIDTH), BF16), _sds((4, POOL_GC, POOL_GC), F32), _sds((1, POOL_WIDTH), F32)],
        args=(dyp, dyp, diff, w_pool, pool_scale), sem=("arbitrary",), exchanges=exchanges)


_PARTS = ((0, C_Q), (C_Q, C_K), (C_K, C_V), (C_V, C_G), (C_G, IN_WIDTH))


def _inproj_bwd(parts, x2, dx1, w_in_t, g1, exchanges=()):
    T = x2.shape[0]
    tm = min(TM, T)

    def body(du_ref, dq_ref, dk_ref, dv_ref, dgt_ref, x_ref, dx1_ref, w_ref, g_ref, gx_ref, dg_ref):
        @pl.when(pl.program_id(0) == 0)
        def _():
            dg_ref[...] = jnp.zeros_like(dg_ref)

        dh = jnp.zeros((tm, D_MODEL), F32)
        for (lo, hi), p_ref in zip(_PARTS, (du_ref, dq_ref, dk_ref, dv_ref, dgt_ref)):
            dh = dh + _dot(p_ref[...], w_ref[lo:hi, :])
        dx, dg = _norm_bwd(x_ref[...], g_ref[...], dh)
        gx_ref[...] = dx1_ref[...] + dx
        dg_ref[...] += dg

    return _call(
        body, name="inproj_bwd", grid=(T // tm,),
        in_specs=[_rows(tm, hi - lo) for lo, hi in _PARTS]
        + [_rows(tm, D_MODEL), _rows(tm, D_MODEL), _const((IN_WIDTH, D_MODEL)), _const((1, D_MODEL))],
        out_specs=[_rows(tm, D_MODEL), _const((1, D_MODEL))],
        out_shape=[_sds((T, D_MODEL), F32), _sds((1, D_MODEL), F32)],
        args=(*parts, x2, dx1, w_in_t, g1), sem=("arbitrary",), exchanges=exchanges)


def _dw_in(h, parts, exchanges=()):
    T = h.shape[0]
    tk = min(TM, T)

    def body(h_ref, du_ref, dq_ref, dk_ref, dv_ref, dgt_ref, o_ref, db_ref):
        @pl.when(pl.program_id(0) == 0)
        def _():
            o_ref[...] = jnp.zeros_like(o_ref)
            db_ref[...] = jnp.zeros_like(db_ref)

        hh = h_ref[...]
        for (lo, hi), p_ref in zip(_PARTS, (du_ref, dq_ref, dk_ref, dv_ref, dgt_ref)):
            part = p_ref[...]
            o_ref[lo:hi, :] += _dot_tn(part, hh)
            db_ref[:, lo:hi] += jnp.sum(part.astype(F32), axis=0, keepdims=True)

    return _call(
        body, name="dw_in", grid=(T // tk,),
        in_specs=[_rows(tk, D_MODEL)] + [_rows(tk, hi - lo) for lo, hi in _PARTS],
        out_specs=[_const((IN_WIDTH, D_MODEL)), _const((1, IN_WIDTH))],
        out_shape=[_sds((IN_WIDTH, D_MODEL), F32), _sds((1, IN_WIDTH), F32)],
        args=(h, *parts), sem=("arbitrary",), exchanges=exchanges)


def _row_tile(rows, cap=256, mult=16):
    best = None
    for t in range(mult, min(rows, cap) + 1, mult):
        if rows % t == 0:
            best = t
    if best is None:
        raise ValueError("no row tile for %d rows" % rows)
    return best


def _pair_sum(ids, full, got):
    _, r, c = full.shape
    hr = r // 2
    tr = _row_tile(hr)
    nblk = hr // tr

    def body(ids_ref, a_ref, b_ref, own_ref, sb_ref):
        s = a_ref[...] + b_ref[...]
        sb_ref[...] = s.astype(BF16)

        @pl.when(pl.program_id(1) == ids_ref[0])
        def _():
            own_ref[...] = s

    slab = pl.BlockSpec((None, tr, c), lambda i, j, ids_ref: (j, i, 0))
    return pl.pallas_call(
        body, name="pair_sum_%dx%d" % (r, c),
        grid_spec=pltpu.PrefetchScalarGridSpec(
            num_scalar_prefetch=1, grid=(nblk, N_CHIPS),
            in_specs=[pl.BlockSpec((None, tr, c), lambda i, j, ids_ref: (j, ids_ref[1] * nblk + i, 0)), slab],
            out_specs=[pl.BlockSpec((tr, c), lambda i, j, ids_ref: (i, 0)), slab]),
        out_shape=[_sds((hr, c), F32), _sds((N_CHIPS, hr, c), BF16)],
        compiler_params=_cp("parallel", "arbitrary"),
    )(ids, full, got)


def _chip_sum(ids, own, got):
    hr, c = own.shape
    tr = _row_tile(hr)
    nblk = hr // tr

    def body(ids_ref, a_ref, b_ref, o_ref):
        o_ref[...] = ((a_ref[...] + b_ref[0].astype(F32)) + b_ref[1].astype(F32)) + b_ref[2].astype(F32)

    return pl.pallas_call(
        body, name="chip_sum_%dx%d" % (hr, c),
        grid_spec=pltpu.PrefetchScalarGridSpec(
            num_scalar_prefetch=1, grid=(nblk,),
            in_specs=[pl.BlockSpec((tr, c), lambda i, ids_ref: (i, 0)),
                      pl.BlockSpec((3, tr, c), lambda i, ids_ref: (0, i, 0))],
            out_specs=pl.BlockSpec((tr, c), lambda i, ids_ref: (ids_ref[1] * nblk + i, 0))),
        out_shape=_sds((2 * hr, c), F32),
        compiler_params=_cp("parallel"),
    )(ids, own, got)


def _adamw_math(w, g, m, v):
    nm = ADAM_B1 * m + (1.0 - ADAM_B1) * g
    nv = ADAM_B2 * v + (1.0 - ADAM_B2) * (g * g)
    m_hat = nm / (1.0 - ADAM_B1 ** ADAM_STEP)
    v_hat = nv / (1.0 - ADAM_B2 ** ADAM_STEP)
    return -ADAM_LR * (m_hat / (jnp.sqrt(v_hat) + ADAM_EPS) + ADAM_WD * w), nm, nv


def _adamw(w, g, m, v):
    r, c = w.shape
    tr = _row_tile(r, cap=512, mult=8)

    def body(w_ref, g_ref, m_ref, v_ref, d_ref, nm_ref, nv_ref):
        d_ref[...], nm_ref[...], nv_ref[...] = _adamw_math(w_ref[...], g_ref[...], m_ref[...], v_ref[...])

    spec = _rows(tr, c)
    return pl.pallas_call(
        body, name="adamw_%dx%d" % (r, c), grid=(r // tr,),
        in_specs=[spec] * 4, out_specs=[spec] * 3, out_shape=[_sds((r, c), F32)] * 3,
        compiler_params=_cp("parallel"),
    )(w, g, m, v)


SC_TILES = 32
SC_LANES = 16
SC_ROWS = 8


def _sparse_mesh():
    return plsc.VectorSubcoreMesh(core_axis_name="sc_core", subcore_axis_name="sc_subcore")


def _sparse_tile():
    return lax.axis_index("sc_subcore") * 2 + lax.axis_index("sc_core")


def _adamw_sparse(tag, items, after=()):
    n = len(items)
    shapes = [w.shape for w, _, _, _ in items]

    def body(*refs):
        refs = refs[len(after):]
        ins, outs, bufs = refs[:4 * n], refs[4 * n:8 * n], refs[8 * n:]
        tile = _sparse_tile()
        for k, (r, c) in enumerate(shapes):
            rows = r // SC_TILES
            step = min(rows, SC_ROWS)
            wb, gb, mb, vb = bufs[4 * k:4 * k + 4]

            @pl.loop(0, rows, step=step)
            def _(r0):
                mine = pl.ds(tile * rows + r0, step)
                for src, dst in zip(ins[4 * k:4 * k + 4], (wb, gb, mb, vb)):
                    pltpu.sync_copy(src.at[mine], dst)

                @pl.loop(0, step)
                def _(row):
                    @pl.loop(0, c, step=SC_LANES)
                    def _(i):
                        at = (row, pl.ds(i, SC_LANES))
                        wb[at], mb[at], vb[at] = _adamw_math(wb[at], gb[at], mb[at], vb[at])

                for src, dst in zip((wb, mb, vb, gb), outs[4 * k:4 * k + 4]):
                    pltpu.sync_copy(src, dst.at[mine])

    res = pl.kernel(
        body, name="adamw_sparse_" + tag, out_type=[_sds(s, F32) for s in shapes for _ in range(4)],
        mesh=_sparse_mesh(),
        scratch_types=[pltpu.VMEM((min(r // SC_TILES, SC_ROWS), c), F32) for r, c in shapes for _ in range(4)],
    )(*after, *[a for item in items for a in item])
    return [tuple(res[4 * k:4 * k + 4]) for k in range(n)]


def _copy_sparse(a, after=()):
    r, c = a.shape
    rows = r // SC_TILES

    def body(*refs):
        a_hbm, o_hbm, buf = refs[len(after):]
        tile = _sparse_tile()

        @pl.loop(0, rows, step=SC_ROWS)
        def _(r0):
            mine = pl.ds(tile * rows + r0, SC_ROWS)
            pltpu.sync_copy(a_hbm.at[mine], buf)
            pltpu.sync_copy(buf, o_hbm.at[mine])

    return pl.kernel(
        body, name="copy_sparse_%dx%d" % (r, c), out_type=_sds((r, c), F32), mesh=_sparse_mesh(),
        scratch_types=[pltpu.VMEM((SC_ROWS, c), F32)],
    )(*after, a)


_SMALL_NAMES = ("w_pool", "b_in", "g_mix_pre", "g_mix_post", "g_mlp_pre", "g_mlp_post", "pool_scale", "attn_sinks")
B_ROWS = -(-IN_WIDTH // D_MODEL)


def _row_block(rows):
    rows = [jnp.pad(r.astype(F32), ((0, 0), (0, D_MODEL - r.shape[1]))) for r in rows]
    return jnp.pad(jnp.concatenate(rows, axis=0), ((0, 8 - len(rows)), (0, 0)))


def _early_block(dg2, dg3, dg4, dps, dsink, loss):
    tail = jnp.concatenate([jnp.pad(dsink.reshape(1, -1), ((0, 0), (0, LANES - dsink.size))),
                            jnp.pad(loss.reshape(1, 1), ((0, 0), (0, LANES - 1)))], axis=1)
    return _row_block([dg2, dg3, dg4, dps, tail])


def _late_block(db_in, dg1):
    b = jnp.pad(db_in, ((0, 0), (0, B_ROWS * D_MODEL - IN_WIDTH))).reshape(B_ROWS, D_MODEL)
    return _row_block([b[r:r + 1] for r in range(B_ROWS)] + [dg1])


def _small_update(gearly, gmat, glate, w, m, v):
    names = _SMALL_NAMES
    n = len(names)

    def total(ref, rows):
        acc = ref[0:rows, :]
        for d in range(1, N_DEV):
            acc = acc + ref[d * rows:(d + 1) * rows, :]
        return acc

    def body(*refs):
        early_ref, gmat_ref, late_ref = refs[:3]
        w_refs, m_refs, v_refs = refs[3:3 + n], refs[3 + n:3 + 2 * n], refs[3 + 2 * n:3 + 3 * n]
        outs = refs[3 + 3 * n:]
        loss_ref, g_refs, d_refs = outs[0], outs[1:1 + n], outs[1 + n:1 + 2 * n]
        nm_refs, nv_refs = outs[1 + 2 * n:1 + 3 * n], outs[1 + 3 * n:1 + 4 * n]
        early, late = total(early_ref, 8), total(late_ref, 8)
        loss_ref[...] = jnp.sum(early[4:5, LANES:2 * LANES], axis=1, keepdims=True)
        bias = jnp.concatenate([late[r:r + 1, :] for r in range(B_ROWS - 1)]
                               + [late[B_ROWS - 1:B_ROWS, :IN_WIDTH - (B_ROWS - 1) * D_MODEL]], axis=1)
        grad = dict(b_in=bias, g_mix_pre=late[B_ROWS:B_ROWS + 1, :], g_mix_post=early[0:1, :],
                    g_mlp_pre=early[1:2, :], g_mlp_post=early[2:3, :], pool_scale=early[3:4, :POOL_WIDTH],
                    attn_sinks=early[4:5, :N_Q_HEADS])
        for i, name in enumerate(names):
            g = total(gmat_ref, 4 * POOL_GC) if name == "w_pool" else grad[name]
            g_refs[i][...] = g
            d_refs[i][...], nm_refs[i][...], nv_refs[i][...] = _adamw_math(
                w_refs[i][...], g, m_refs[i][...], v_refs[i][...])

    shapes = [_sds(w[k].shape, F32) for k in names]
    res = pl.pallas_call(
        body, name="small_update", out_shape=[_sds((1, 1), F32)] + shapes * 4,
        compiler_params=pltpu.CompilerParams(vmem_limit_bytes=VMEM_MB * 1024 * 1024),
    )(gearly, gmat, glate, *[w[k] for k in names], *[m[k] for k in names], *[v[k] for k in names])
    loss = res[0]
    per = {k: tuple(res[1 + j * n + i] for j in range(4)) for i, k in enumerate(names)}
    return loss, per


_BIG = ("w_in", "w_branch_pool", "w_branch_attn", "w_out", "w_up", "w_down")
_ORDER = ("g_mix_pre", "w_in", "b_in", "w_pool", "pool_scale", "attn_sinks", "w_branch_pool", "w_branch_attn",
          "w_out", "g_mix_post", "g_mlp_pre", "w_up", "w_down", "g_mlp_post")


def _stack_rows(slab):
    return slab.reshape(-1, slab.shape[2])


def _step(x2, tgt, seq, shards, small, ids):
    tabs = _rope_tables(seq)
    g1, g2, g3, g4 = (small[n] for n in ("g_mix_pre", "g_mix_post", "g_mlp_pre", "g_mlp_post"))
    sinks = small["attn_sinks"].reshape(N_Q_HEADS)
    w_pool = small["w_pool"].reshape(4, POOL_GC, POOL_GC)
    pool_scale = small["pool_scale"]

    def whole(shard, slabs):
        return lax.dynamic_update_slice(slabs, shard[None], (ids[0], 0, 0))

    up_a, up_b = shards["w_up"][:HALF], shards["w_up"][HALF:]
    down_a, down_b = shards["w_down"][:HALF], shards["w_down"][HALF:]
    w_in = _stack_rows(whole(shards["w_in"], _alone("gather_in", _ex_gather([shards["w_in"]]))[0][0]))
    mix_shards = [shards[n] for n in ("w_branch_pool", "w_branch_attn", "w_out")]
    (h, u, q, k, v, gate), [(*mix_slabs, got_c)] = _inproj(
        x2, g1, w_in, small["b_in"], tabs, seq, exchanges=[_ex_gather(mix_shards + [down_a])])
    w_bp, w_ba, out_slab = (whole(s, g) for s, g in zip(mix_shards, mix_slabs))
    w_out = _stack_rows(out_slab)
    diff, y_pool = _pool_fwd(u, w_pool, pool_scale, seq)
    (y_attn,), [[got_a]] = _attn_fwd(q, k, v, sinks, seq, exchanges=[_ex_gather([up_a])])
    (merged, mix, x1, h2), [[got_b, got_d]] = _merge_out(
        y_pool, y_attn, gate, x2, w_bp, w_ba, w_out, g2, g3, exchanges=[_ex_gather([up_b, down_b])])
    w_up = (whole(up_a, got_a), whole(up_b, got_b))
    w_down = (whole(down_a, got_c), whole(down_b, got_d))
    act, dff, dup, dx1, dmix, loss_acc, dg4, dg3, dg2 = _mlp_core(h2, x1, mix, tgt, w_up, w_down, g4, g3, g2)

    dw_down = _dw("down", act, dff, 1024, 1024)[0].reshape(N_CHIPS, D_FF // N_CHIPS, D_MODEL)
    (dbp, dba, dgate, dyp, dya), [[got]] = _merge_bwd(
        dmix, gate, y_pool, y_attn, w_out, w_bp, w_ba, exchanges=[_ex_pair([dw_down])])
    ps_down = _pair_sum(ids, dw_down, got)
    (dw_up,), [[got]] = _dw("up", h2, dup, 1024, 1024, shard_cols=True, exchanges=[_ex_chip([ps_down[1]])])
    half_down = _chip_sum(ids, ps_down[0], got)
    (dw_out, dw_bp, dw_ba), [[got]] = _dw_mix(merged, dmix, y_pool, dbp, y_attn, dba, exchanges=[_ex_pair([dw_up])])
    ps_up = _pair_sum(ids, dw_up, got)
    dw_mix = [dw_out.reshape(N_CHIPS, D_MODEL // N_CHIPS, D_MODEL), dw_bp, dw_ba]
    (dq, dk, dv, dsink), [[got], gots, [g_down]] = _attn_bwd(
        q, k, v, dya, sinks, tabs, seq, exchanges=[_ex_chip([ps_up[1]]), _ex_pair(dw_mix), _ex_swap([half_down])])
    half_up = _chip_sum(ids, ps_up[0], got)
    ps_mix = [_pair_sum(ids, d, g) for d, g in zip(dw_mix, gots)]
    (du, dw_pool, dps), [[g_up]] = _pool_bwd(dyp, diff, w_pool, pool_scale, seq, exchanges=[_ex_swap([half_up])])
    parts = (du, dq, dk, dv, dgate)
    early = _early_block(dg2, dg3, dg4, dps, dsink[:, 0], loss_acc[0, 0])
    mat = dw_pool.reshape(4 * POOL_GC, POOL_GC)
    (dw_in_t, db_in), [gots, [gearly, gmat]] = _dw_in(
        h, parts, exchanges=[_ex_chip([p[1] for p in ps_mix]), _ex_allgather([early, mat])])
    half_mix = [_chip_sum(ids, p[0], g) for p, g in zip(ps_mix, gots)]
    dw_in = dw_in_t.reshape(N_CHIPS, IN_WIDTH // N_CHIPS, D_MODEL)
    g_mix, [got] = _alone("swap_mix_pair_in", _ex_swap(half_mix), _ex_pair([dw_in]))
    ps_in = _pair_sum(ids, dw_in, got)
    (gx, dg1), [[got]] = _inproj_bwd(parts, x2, dx1, w_in, g1, exchanges=[_ex_chip([ps_in[1]])])
    [g_in], [glate] = _alone("swap_in_allgather", _ex_swap([_chip_sum(ids, ps_in[0], got)]),
                             _ex_allgather([_late_block(db_in, dg1)]))

    grads = dict(w_in=g_in, w_branch_pool=g_mix[1], w_branch_attn=g_mix[2], w_out=g_mix[0], w_up=g_up, w_down=g_down)
    return (gearly, gmat, glate), gx, grads


def kernel(x, g_mix_pre, w_in, b_in, w_pool, pool_scale, attn_sinks, w_branch_pool, w_branch_attn, w_out, g_mix_post, g_mlp_pre, w_up, w_down, g_mlp_post, loss_target, m_g_mix_pre, m_w_in, m_b_in, m_w_pool, m_pool_scale, m_attn_sinks, m_w_branch_pool, m_w_branch_attn, m_w_out, m_g_mix_post, m_g_mlp_pre, m_w_up, m_w_down, m_g_mlp_post, v_g_mix_pre, v_w_in, v_b_in, v_w_pool, v_pool_scale, v_attn_sinks, v_w_branch_pool, v_w_branch_attn, v_w_out, v_g_mix_post, v_g_mlp_pre, v_w_up, v_w_down, v_g_mlp_post):
    weights = dict(g_mix_pre=g_mix_pre, w_in=w_in, b_in=b_in, w_pool=w_pool, pool_scale=pool_scale,
                   attn_sinks=attn_sinks, w_branch_pool=w_branch_pool, w_branch_attn=w_branch_attn, w_out=w_out,
                   g_mix_post=g_mix_post, g_mlp_pre=g_mlp_pre, w_up=w_up, w_down=w_down, g_mlp_post=g_mlp_post)
    mom1 = dict(g_mix_pre=m_g_mix_pre, w_in=m_w_in, b_in=m_b_in, w_pool=m_w_pool, pool_scale=m_pool_scale,
                attn_sinks=m_attn_sinks, w_branch_pool=m_w_branch_pool, w_branch_attn=m_w_branch_attn,
                w_out=m_w_out, g_mix_post=m_g_mix_post, g_mlp_pre=m_g_mlp_pre, w_up=m_w_up, w_down=m_w_down,
                g_mlp_post=m_g_mlp_post)
    mom2 = dict(g_mix_pre=v_g_mix_pre, w_in=v_w_in, b_in=v_b_in, w_pool=v_w_pool, pool_scale=v_pool_scale,
                attn_sinks=v_attn_sinks, w_branch_pool=v_w_branch_pool, w_branch_attn=v_w_branch_attn,
                w_out=v_w_out, g_mix_post=v_g_mix_post, g_mlp_pre=v_g_mlp_pre, w_up=v_w_up, w_down=v_w_down,
                g_mlp_post=v_g_mlp_post)
    b_loc, seq, _ = x.shape
    x2 = x.reshape(b_loc * seq, D_MODEL)
    tgt = loss_target.reshape(b_loc * seq, D_MODEL)
    ids = jnp.stack([2 * lax.axis_index("x") + lax.axis_index("y"), lax.axis_index("c")]).astype(jnp.int32)

    def flat(n, a):
        return a[0].T if n == "w_in" else a[0]

    def unflat(n, a):
        return (a.T if n == "w_in" else a)[None]

    shards = {n: flat(n, weights[n]).astype(BF16) for n in _BIG}
    small = {n: weights[n] for n in _ORDER if n not in _BIG}
    (gearly, gmat, glate), gx, grads = _step(x2, tgt, seq, shards, small, ids)

    def two_d(src):
        return {n: src[n].reshape(4 * POOL_GC, POOL_GC) if n == "w_pool" else src[n] for n in _SMALL_NAMES}

    loss, per = _small_update(gearly, gmat, glate, two_d(weights), two_d(mom1), two_d(mom2))
    delta, new_m, new_v = {}, {}, {}
    for n in _SMALL_NAMES:
        grads[n], delta[n], new_m[n], new_v[n] = (a.reshape(weights[n].shape) for a in per[n])
    def operands(n):
        return flat(n, weights[n]), grads[n], flat(n, mom1[n]), flat(n, mom2[n])

    done = {"w_in": (*_adamw(*operands("w_in")), grads["w_in"])}
    last = ()
    for tag, group in (("down", ("w_down",)), ("up", ("w_up",)), ("mix", ("w_out", "w_branch_pool", "w_branch_attn"))):
        res = _adamw_sparse(tag, [operands(n) for n in group], after=last)
        done.update(zip(group, res))
        last = (res[0][0],)
    for n in _BIG:
        delta[n], new_m[n], new_v[n], grads[n] = (unflat(n, a) for a in done[n])
    gx = _copy_sparse(gx, after=last)

    return (loss[0, 0], gx.reshape(x.shape), *[grads[n] for n in _ORDER], *[delta[n] for n in _ORDER],
            *[new_m[n] for n in _ORDER], *[new_v[n] for n in _ORDER])
```

```python
import jax
import jax.numpy as jnp
from jax import lax
from jax.experimental import pallas as pl
from jax.experimental.pallas import tpu as pltpu
from jax.experimental.pallas import tpu_sc as plsc

F32 = jnp.float32
BF16 = jnp.bfloat16

D_MODEL = 1024
POOL_WINDOWS = (2, 4, 8, 16)
POOL_WIDTH = 512
POOL_GC = 128
HALO = 16
HEAD_DIM = 64
N_Q_HEADS = 8
ATTN_WIDTH = 512
KV_WIDTH = 128
BLOCK = 128
NEG_INF = -1e30
ROPE_THETA = 500000.0
ROT_DIM = 16
GATE_WIDTH = 2048
IN_WIDTH = 3328
D_FF = 4096
EPS = 1e-6
SCALE = HEAD_DIM ** -0.5
C_Q, C_K, C_V, C_G = 512, 1024, 1152, 1280

ADAM_LR, ADAM_B1, ADAM_B2, ADAM_EPS, ADAM_WD, ADAM_STEP = 0.001, 0.9, 0.999, 1e-08, 0.01, 10

N_CHIPS = 4
N_DEV = 8
LANES = 128
TM = 512
TP = 512
VMEM_MB = 56

MESH = pl.DeviceIdType.MESH
ANY = pl.BlockSpec(memory_space=pl.ANY)


def _cp(*sem, vmem=VMEM_MB):
    return pltpu.CompilerParams(dimension_semantics=sem, vmem_limit_bytes=vmem * 1024 * 1024)


def _rows(tile, cols):
    return pl.BlockSpec((tile, cols), lambda i: (i, 0))


def _const(shape):
    nd = len(shape)
    return pl.BlockSpec(shape, lambda i: (0,) * nd)


def _sds(shape, dtype):
    return jax.ShapeDtypeStruct(shape, dtype)


def _dot(a, b):
    return jnp.dot(a, b, preferred_element_type=F32)


def _dot_nt(a, b):
    return lax.dot_general(a, b, (((1,), (1,)), ((), ())), preferred_element_type=F32)


def _dot_tn(a, b):
    return lax.dot_general(a, b, (((0,), (0,)), ((), ())), preferred_element_type=F32)


def _rms(x):
    return lax.rsqrt(jnp.mean(x * x, axis=-1, keepdims=True) + EPS)


def _norm_bwd(x, g, dout):
    r = _rms(x)
    n = x * r
    dn = dout * g
    dx = r * (dn - n * jnp.mean(dn * n, axis=-1, keepdims=True))
    return dx, jnp.sum(dout * n, axis=0, keepdims=True)


def _rot_fwd(t, c, a, bt):
    return t * c + pltpu.roll(t, LANES - 8, 1) * a + pltpu.roll(t, 8, 1) * bt


def _rot_bwd(d, c, a, bt):
    return d * c + pltpu.roll(d * a, 8, 1) + pltpu.roll(d * bt, LANES - 8, 1)


def _rope_tables(seq):
    pos = jnp.arange(seq, dtype=F32)
    inv_freq = ROPE_THETA ** (-jnp.arange(0, ROT_DIM, 2, dtype=F32) / ROT_DIM)
    ang = pos[:, None] * inv_freq[None, :]
    cos, sin = jnp.cos(ang), jnp.sin(ang)
    ones = jnp.ones((seq, HEAD_DIM - ROT_DIM), F32)
    zeros8 = jnp.zeros((seq, 8), F32)
    zrest = jnp.zeros((seq, HEAD_DIM - ROT_DIM), F32)
    c = jnp.concatenate([cos, cos, ones], axis=1)
    a = jnp.concatenate([-sin, zeros8, zrest], axis=1)
    bt = jnp.concatenate([zeros8, sin, zrest], axis=1)
    return tuple(jnp.tile(t, (1, 2)) for t in (c, a, bt))


class _Exchange:
    def __init__(self, inputs, out_shapes, sems, start, finish, aliases=None, middle=None):
        self.inputs, self.out_shapes, self.sems = list(inputs), list(out_shapes), list(sems)
        self.start, self.finish, self.aliases = start, finish, dict(aliases or {})
        self.middle = middle


def _call(body, *, name, grid, in_specs, out_specs, out_shape, args, scratch=(), sem=(), exchanges=()):
    in_specs, out_specs, out_shape, scratch = list(in_specs), list(out_specs), list(out_shape), list(scratch)
    if not exchanges:
        return pl.pallas_call(body, name=name, grid=grid, in_specs=in_specs, out_specs=out_specs,
                              out_shape=out_shape, scratch_shapes=scratch, compiler_params=_cp(*sem))(*args)
    n_in, n_out, n_scr = len(in_specs), len(out_specs), len(scratch)
    x_in = [a for ex in exchanges for a in ex.inputs]
    x_out = [s for ex in exchanges for s in ex.out_shapes]
    x_sem = [s for ex in exchanges for s in ex.sems]
    aliases, i_off, o_off = {}, n_in, n_out
    for ex in exchanges:
        for i, o in ex.aliases.items():
            aliases[i_off + i] = o_off + o
        i_off += len(ex.inputs)
        o_off += len(ex.out_shapes)

    def split(flat):
        out, pos = [], 0
        for ex, n in zip(exchanges, flat[1]):
            out.append(flat[0][pos:pos + n])
            pos += n
        return out

    def carrier(*refs):
        pos = 0
        groups = []
        for n in (n_in, len(x_in), n_out, len(x_out), n_scr, len(x_sem)):
            groups.append(refs[pos:pos + n])
            pos += n
        ins, xin, outs, xout, scr, xsem = groups
        xin = split((xin, [len(ex.inputs) for ex in exchanges]))
        xout = split((xout, [len(ex.out_shapes) for ex in exchanges]))
        xsem = split((xsem, [len(ex.sems) for ex in exchanges]))
        first = pl.program_id(0) == 0
        last = pl.program_id(0) == grid[0] - 1
        for d in range(1, len(grid)):
            first = jnp.logical_and(first, pl.program_id(d) == 0)
            last = jnp.logical_and(last, pl.program_id(d) == grid[d] - 1)

        @pl.when(first)
        def _():
            for ex, i, o, s in zip(exchanges, xin, xout, xsem):
                ex.start(i, o, s)

        if any(ex.middle for ex in exchanges):
            half = pl.program_id(0) == grid[0] // 2
            for d in range(1, len(grid)):
                half = jnp.logical_and(half, pl.program_id(d) == 0)

            @pl.when(half)
            def _():
                for ex, i, o, s in zip(exchanges, xin, xout, xsem):
                    if ex.middle:
                        ex.middle(i, o, s)

        body(*ins, *outs, *scr)

        @pl.when(last)
        def _():
            for ex, i, o, s in zip(exchanges, xin, xout, xsem):
                ex.finish(i, o, s)

    res = pl.pallas_call(
        carrier, name=name, grid=grid, in_specs=in_specs + [ANY] * len(x_in),
        out_specs=out_specs + [ANY] * len(x_out), out_shape=out_shape + x_out,
        scratch_shapes=scratch + x_sem, input_output_aliases=aliases,
        compiler_params=_cp(*(["arbitrary"] * len(grid))),
    )(*args, *x_in)
    return res[:n_out], split((res[n_out:], [len(ex.out_shapes) for ex in exchanges]))


def _alone(name, *exchanges):
    n_in = [len(ex.inputs) for ex in exchanges]
    n_out = [len(ex.out_shapes) for ex in exchanges]
    n_sem = [len(ex.sems) for ex in exchanges]
    aliases, i_off, o_off = {}, 0, 0
    for ex in exchanges:
        for i, o in ex.aliases.items():
            aliases[i_off + i] = o_off + o
        i_off += len(ex.inputs)
        o_off += len(ex.out_shapes)

    def split(flat, counts):
        out, pos = [], 0
        for n in counts:
            out.append(flat[pos:pos + n])
            pos += n
        return out

    def body(*refs):
        ins, outs, sems = split(refs, [sum(n_in), sum(n_out), sum(n_sem)])
        groups = list(zip(exchanges, split(ins, n_in), split(outs, n_out), split(sems, n_sem)))
        for ex, i, o, s in groups:
            ex.start(i, o, s)
        for ex, i, o, s in groups:
            if ex.middle:
                ex.middle(i, o, s)
        for ex, i, o, s in groups:
            ex.finish(i, o, s)

    res = pl.pallas_call(
        body, name=name, in_specs=[ANY] * sum(n_in), out_specs=[ANY] * sum(n_out),
        out_shape=[s for ex in exchanges for s in ex.out_shapes],
        scratch_shapes=[s for ex in exchanges for s in ex.sems], input_output_aliases=aliases,
    )(*[a for ex in exchanges for a in ex.inputs])
    return split(res, n_out)


def _place():
    x, y, c = lax.axis_index("x"), lax.axis_index("y"), lax.axis_index("c")
    chips = [(1 - x, y), (x, 1 - y), (1 - x, 1 - y)]
    return x, y, c, chips


def _remote(src, dst, send, recv, to):
    return pltpu.make_async_remote_copy(src_ref=src, dst_ref=dst, send_sem=send, recv_sem=recv,
                                        device_id=to, device_id_type=MESH)


def _ex_gather(shards):
    nw = len(shards)
    hrs = [s.shape[0] // 2 for s in shards]

    def copies(ins, outs, sems):
        s1, r1, s2, r2, fs, fr = sems
        x, y, c, _ = _place()
        me, xn, yn, dg = (x, y), (1 - x, y), (x, 1 - y), (1 - x, 1 - y)
        nbr = (xn, yn)
        sibling = (x, y, 1 - c)

        def piece(w, chip, core, part=None):
            hr = hrs[w]
            rows = pl.ds(core * hr, hr) if part is None else pl.ds(core * hr + part * (hr // 2), hr // 2)
            return outs[w].at[2 * chip[0] + chip[1], rows]

        def first(w, k):
            return _remote(ins[w].at[pl.ds(c * hrs[w], hrs[w])], piece(w, me, c), s1.at[w, k], r1.at[w, k],
                           (*nbr[k], c))

        def landed(w, k):
            return _remote(piece(w, nbr[k], c), piece(w, nbr[k], c), s1.at[w, k], r1.at[w, k], (*nbr[k], c))

        def onward(w, k):
            return _remote(piece(w, nbr[k], c, k), piece(w, nbr[k], c, k), s2.at[w, k], r2.at[w, k],
                           (*nbr[1 - k], c))

        def arrived(w, k):
            return _remote(piece(w, dg, c, k), piece(w, dg, c, k), s2.at[w, k], r2.at[w, k], (*nbr[1 - k], c))

        def passed(w, j):
            chip = (xn, yn, dg)[j]
            return _remote(piece(w, chip, c), piece(w, chip, c), fs.at[w, j], fr.at[w, j], sibling)

        def handed(w, j):
            chip = (xn, yn, dg)[j]
            return _remote(piece(w, chip, 1 - c), piece(w, chip, 1 - c), fs.at[w, j], fr.at[w, j], sibling)

        return first, landed, onward, arrived, passed, handed

    def start(ins, outs, sems):
        first = copies(ins, outs, sems)[0]
        for w in range(nw):
            for k in range(2):
                first(w, k).start()

    def middle(ins, outs, sems):
        _, landed, onward, _, passed, _ = copies(ins, outs, sems)
        for w in range(nw):
            for k in range(2):
                landed(w, k).wait_recv()
                onward(w, k).start()
                passed(w, k).start()

    def finish(ins, outs, sems):
        first, _, onward, arrived, passed, handed = copies(ins, outs, sems)
        for w in range(nw):
            for k in range(2):
                arrived(w, k).wait_recv()
            passed(w, 2).start()
        for w in range(nw):
            for j in range(3):
                handed(w, j).wait_recv()
        for w in range(nw):
            for k in range(2):
                first(w, k).wait_send()
                onward(w, k).wait_send()
            for j in range(3):
                passed(w, j).wait_send()

    return _Exchange(shards, [_sds((N_CHIPS,) + s.shape, s.dtype) for s in shards],
                     [pltpu.SemaphoreType.DMA((nw, 2))] * 4 + [pltpu.SemaphoreType.DMA((nw, 3))] * 2,
                     start, finish, middle=middle)


def _ex_pair(grads):
    nw = len(grads)

    def copies(ins, outs, sems):
        x, y, c, _ = _place()
        out = []
        for w in range(nw):
            hr = grads[w].shape[1] // 2
            out.append(_remote(ins[w].at[:, pl.ds((1 - c) * hr, hr)], outs[w], sems[0].at[w], sems[1].at[w],
                               (x, y, 1 - c)))
        return out

    def start(ins, outs, sems):
        for cp in copies(ins, outs, sems):
            cp.start()

    def finish(ins, outs, sems):
        for cp in copies(ins, outs, sems):
            cp.wait()

    return _Exchange(grads, [_sds((N_CHIPS, g.shape[1] // 2, g.shape[2]), F32) for g in grads],
                     [pltpu.SemaphoreType.DMA((nw,))] * 2, start, finish)


def _ex_chip(pieces):
    nw = len(pieces)

    def copies(ins, outs, sems):
        x, y, c, chips = _place()
        return [_remote(ins[w].at[2 * cx + cy], outs[w].at[k], sems[0].at[w, k], sems[1].at[w, k], (cx, cy, c))
                for w in range(nw) for k, (cx, cy) in enumerate(chips)]

    def start(ins, outs, sems):
        for cp in copies(ins, outs, sems):
            cp.start()

    def finish(ins, outs, sems):
        for cp in copies(ins, outs, sems):
            cp.wait()

    return _Exchange(pieces, [_sds((3,) + p.shape[1:], BF16) for p in pieces],
                     [pltpu.SemaphoreType.DMA((nw, 3))] * 2, start, finish)


def _ex_swap(fulls):
    nw = len(fulls)

    def start(ins, outs, sems):
        x, y, c, _ = _place()
        for w in range(nw):
            hr = fulls[w].shape[0] // 2
            mine = pl.ds(c * hr, hr)
            _remote(ins[w].at[mine], outs[w].at[mine], sems[0].at[w], sems[1].at[w], (x, y, 1 - c)).start()

    def finish(ins, outs, sems):
        x, y, c, _ = _place()
        for w in range(nw):
            hr = fulls[w].shape[0] // 2
            mine, theirs = pl.ds(c * hr, hr), pl.ds((1 - c) * hr, hr)
            _remote(ins[w].at[mine], outs[w].at[mine], sems[0].at[w], sems[1].at[w], (x, y, 1 - c)).wait_send()
            _remote(ins[w].at[theirs], outs[w].at[theirs], sems[0].at[w], sems[1].at[w], (x, y, 1 - c)).wait_recv()

    return _Exchange(fulls, [_sds(f.shape, F32) for f in fulls], [pltpu.SemaphoreType.DMA((nw,))] * 2,
                     start, finish, aliases={w: w for w in range(nw)})


def _ex_allgather(blocks):
    nb = len(blocks)

    def copies(ins, outs, sems):
        send, recv, lsem = sems
        x, y, c, chips = _place()
        me, sibling = (x, y, c), (x, y, 1 - c)

        def rows(b, px, py, pc):
            m_per = blocks[b].shape[0]
            return outs[b].at[pl.ds((4 * px + 2 * py + pc) * m_per, m_per), :]

        def copy(b, k, blk, to, src=None):
            return _remote(rows(b, *blk) if src is None else src, rows(b, *blk), send.at[b, k], recv.at[b, k], to)

        def mine(b):
            return pltpu.make_async_copy(ins[b], rows(b, *me), lsem.at[b])

        def first(b, k):
            return copy(b, k, me, sibling if k == 0 else (*chips[k - 1], c), src=ins[b])

        def passed(b, j):
            return copy(b, 4 + j, (*chips[j], c), sibling)

        def landed(b, j):
            return copy(b, 1 + j, (*chips[j], c), me)

        def handed(b, k):
            return copy(b, 0, sibling, me) if k == 0 else copy(b, 3 + k, (*chips[k - 1], 1 - c), me)

        return mine, first, passed, landed, handed

    def start(ins, outs, sems):
        mine, first, _, _, _ = copies(ins, outs, sems)
        for b in range(nb):
            mine(b).start()
            for k in range(4):
                first(b, k).start()

    def finish(ins, outs, sems):
        mine, first, passed, landed, handed = copies(ins, outs, sems)
        sent = []
        for b in range(nb):
            for j in range(3):
                landed(b, j).wait_recv()
                cp = passed(b, j)
                cp.start()
                sent.append(cp)
        for b in range(nb):
            for k in range(4):
                handed(b, k).wait_recv()
            for k in range(4):
                first(b, k).wait_send()
        for cp in sent:
            cp.wait_send()
        for b in range(nb):
            mine(b).wait()

    return _Exchange(blocks, [_sds((N_DEV * b.shape[0], b.shape[1]), F32) for b in blocks],
                     [pltpu.SemaphoreType.DMA((nb, 7)), pltpu.SemaphoreType.DMA((nb, 7)), pltpu.SemaphoreType.DMA((nb,))],
                     start, finish)


def _inproj(x2, g1, w_in_t, b_in, tabs, seq, exchanges=()):
    T = x2.shape[0]
    tm = min(TM, seq)
    nseq = seq // tm

    def body(x_ref, g_ref, w_ref, b_ref, c_ref, a_ref, bt_ref, h_ref, u_ref, q_ref, k_ref, v_ref, gate_ref):
        x = x_ref[...]
        h = (x * _rms(x) * g_ref[...]).astype(BF16)
        h_ref[...] = h

        def proj(lo, hi):
            return _dot_nt(h, w_ref[lo:hi, :]) + b_ref[:, lo:hi]

        c, a, bt = c_ref[...], a_ref[...], bt_ref[...]
        u_ref[...] = proj(0, C_Q)
        q = proj(C_Q, C_K)
        for p in range(4):
            sl = slice(LANES * p, LANES * (p + 1))
            q_ref[:, sl] = (_rot_fwd(q[:, sl], c, a, bt) * SCALE).astype(BF16)
        kv = proj(C_K, C_G)
        k_ref[...] = _rot_fwd(kv[:, :KV_WIDTH], c, a, bt).astype(BF16)
        v_ref[...] = kv[:, KV_WIDTH:].astype(BF16)
        for j in range(2):
            lo = C_G + D_MODEL * j
            gate_ref[:, D_MODEL * j:D_MODEL * (j + 1)] = jax.nn.sigmoid(proj(lo, lo + D_MODEL)).astype(BF16)

    tab = pl.BlockSpec((tm, LANES), lambda i: (i % nseq, 0))
    return _call(
        body, name="inproj", grid=(T // tm,),
        in_specs=[_rows(tm, D_MODEL), _const((1, D_MODEL)), _const((IN_WIDTH, D_MODEL)), _const((1, IN_WIDTH)),
                  tab, tab, tab],
        out_specs=[_rows(tm, D_MODEL), _rows(tm, POOL_WIDTH), _rows(tm, ATTN_WIDTH), _rows(tm, KV_WIDTH),
                   _rows(tm, KV_WIDTH), _rows(tm, GATE_WIDTH)],
        out_shape=[_sds((T, D_MODEL), BF16), _sds((T, POOL_WIDTH), F32), _sds((T, ATTN_WIDTH), BF16),
                   _sds((T, KV_WIDTH), BF16), _sds((T, KV_WIDTH), BF16), _sds((T, GATE_WIDTH), BF16)],
        args=(x2, g1, w_in_t, b_in, *tabs), sem=("parallel",), exchanges=exchanges)


def _inv_count(pos, w):
    return 1.0 / jnp.minimum(pos + 1, w).astype(F32)


def _pool_fwd(u, w_pool, pool_scale, seq):
    T = u.shape[0]
    tp = min(TP, seq)
    nseq = seq // tp
    per = tp // HALO

    def body(u_ref, prev_ref, w_ref, s_ref, diff_ref, y_ref):
        i = pl.program_id(0)
        first = (i % nseq) == 0
        prev = jnp.where(first, 0.0, prev_ref[...])
        ext = jnp.concatenate([prev, u_ref[...]], axis=0)
        pos = (i % nseq) * tp + lax.broadcasted_iota(jnp.int32, (tp, 1), 0)
        for gi, w in enumerate(POOL_WINDOWS):
            sl = slice(POOL_GC * gi, POOL_GC * (gi + 1))
            xg = ext[:, sl]
            s = xg
            sh = 1
            while sh < w:
                s = s + pltpu.roll(s, sh, 0)
                sh *= 2
            pooled = s[HALO:] * _inv_count(pos, w)
            diff = (pooled - xg[HALO:]).astype(BF16)
            diff_ref[:, sl] = diff
            mixed = _dot(diff, w_ref[gi].astype(BF16))
            y_ref[:, sl] = (mixed * s_ref[:, sl]).astype(BF16)

    return _call(
        body, name="pool_fwd", grid=(T // tp,),
        in_specs=[_rows(tp, POOL_WIDTH),
                  pl.BlockSpec((HALO, POOL_WIDTH), lambda i: (jnp.maximum(i * per - 1, 0), 0)),
                  _const((4, POOL_GC, POOL_GC)), _const((1, POOL_WIDTH))],
        out_specs=[_rows(tp, POOL_WIDTH), _rows(tp, POOL_WIDTH)],
        out_shape=[_sds((T, POOL_WIDTH), BF16), _sds((T, POOL_WIDTH), BF16)],
        args=(u, u, w_pool, pool_scale), sem=("parallel",))


GROUP = 4
GROWS = GROUP * BLOCK


def _attn_masks(n):
    qi = lax.broadcasted_iota(jnp.int32, (GROWS, 2 * BLOCK), 0) % BLOCK
    kj = lax.broadcasted_iota(jnp.int32, (GROWS, 2 * BLOCK), 1)
    rel = qi + BLOCK - kj
    valid = (rel >= 0) & (rel < BLOCK) & (kj >= jnp.where(n > 0, 0, BLOCK))
    lo = lax.broadcasted_iota(jnp.int32, (BLOCK, LANES), 1) < HEAD_DIM
    return valid, lo


def _by_example(bl, *arrays):
    return [a.reshape(bl, a.shape[0] // bl, a.shape[1]) for a in arrays]


def _stack_heads(ref, h, lo):
    keep = lo if h == 0 else jnp.logical_not(lo)
    pieces = []
    for p in (2 * h, 2 * h + 1):
        xp = ref[:, LANES * p:LANES * (p + 1)].astype(F32)
        for e in range(2):
            t = xp if e == h else pltpu.roll(xp, HEAD_DIM, 1)
            pieces.append(jnp.where(keep, t, 0.0).astype(BF16))
    return jnp.concatenate(pieces, axis=0)


def _unstack_heads(stacked, h, lo):
    pairs = []
    for j in range(2):
        parts = []
        for e in range(2):
            t = stacked[BLOCK * (2 * j + e):BLOCK * (2 * j + e + 1)]
            parts.append(t if e == h else pltpu.roll(t, HEAD_DIM, 1))
        pairs.append(jnp.where(lo, parts[0], parts[1]))
    return pairs


def _sink_rows(sink_ref, h):
    head = lax.broadcasted_iota(jnp.int32, (GROWS, 1), 0) // BLOCK
    col = jnp.zeros((GROWS, 1), F32) + sink_ref[GROUP * h]
    for g in range(1, GROUP):
        col = jnp.where(head == g, sink_ref[GROUP * h + g], col)
    return col


def _group_probs(qs, kk, valid, sink):
    s = jnp.where(valid, _dot_nt(qs, kk), NEG_INF)
    m = jnp.maximum(jnp.max(s, axis=1, keepdims=True), sink)
    ex = jnp.exp(s - m)
    es = jnp.exp(sink - m)
    inv = 1.0 / (jnp.sum(ex, axis=1, keepdims=True) + es)
    return ex * inv, es * inv


def _attn_fwd(q, k, v, sinks, seq, exchanges=()):
    T = q.shape[0]
    nb = seq // BLOCK
    bl = T // seq

    def body(sink_ref, q_ref, kp_ref, kc_ref, vp_ref, vc_ref, o_ref):
        valid, lo = _attn_masks(pl.program_id(0))
        for b in range(bl):
            kk = jnp.concatenate([kp_ref[b], kc_ref[b]], axis=0)
            vv = jnp.concatenate([vp_ref[b], vc_ref[b]], axis=0)
            for h in range(2):
                qs = _stack_heads(q_ref.at[b], h, lo)
                pr, _ = _group_probs(qs, kk, valid, _sink_rows(sink_ref, h))
                o = _dot(pr.astype(BF16), vv)
                for j, pair in enumerate(_unstack_heads(o, h, lo)):
                    p = 2 * h + j
                    o_ref[b, :, LANES * p:LANES * (p + 1)] = pair.astype(BF16)

    cur = lambda n: (0, n, 0)
    prv = lambda n: (0, jnp.maximum(n - 1, 0), 0)
    kv = lambda m: pl.BlockSpec((bl, BLOCK, KV_WIDTH), m)
    res = _call(
        body, name="attn_fwd", grid=(nb,),
        in_specs=[pl.BlockSpec(memory_space=pltpu.SMEM), pl.BlockSpec((bl, BLOCK, ATTN_WIDTH), cur),
                  kv(prv), kv(cur), kv(prv), kv(cur)],
        out_specs=[pl.BlockSpec((bl, BLOCK, ATTN_WIDTH), cur)],
        out_shape=[_sds((bl, seq, ATTN_WIDTH), BF16)],
        args=(sinks, *_by_example(bl, q, k, k, v, v)), sem=("parallel",), exchanges=exchanges)
    if exchanges:
        return [res[0][0].reshape(T, ATTN_WIDTH)], res[1]
    return [res[0].reshape(T, ATTN_WIDTH)]


def _branch(y, w_ref):
    return jnp.concatenate([_dot(y, w_ref[j]) for j in range(N_CHIPS)], axis=1)


def _merge_out(y_pool, y_attn, gate, x2, w_bp, w_ba, w_out, g2, g3, exchanges=()):
    T = x2.shape[0]
    tm = min(TM, T)

    def body(yp_ref, ya_ref, gate_ref, x_ref, wbp_ref, wba_ref, wo_ref, g2_ref, g3_ref,
             mg_ref, mix_ref, x1_ref, h2_ref):
        bp, ba = _branch(yp_ref[...], wbp_ref), _branch(ya_ref[...], wba_ref)
        merged = (gate_ref[:, :D_MODEL].astype(F32) * bp + gate_ref[:, D_MODEL:].astype(F32) * ba).astype(BF16)
        mg_ref[...] = merged
        mix = _dot(merged, wo_ref[...])
        mix_ref[...] = mix
        x1 = x_ref[...] + mix * _rms(mix) * g2_ref[...]
        x1_ref[...] = x1
        h2_ref[...] = (x1 * _rms(x1) * g3_ref[...]).astype(BF16)

    return _call(
        body, name="merge_out", grid=(T // tm,),
        in_specs=[_rows(tm, POOL_WIDTH), _rows(tm, ATTN_WIDTH), _rows(tm, GATE_WIDTH), _rows(tm, D_MODEL),
                  _const(w_bp.shape), _const(w_ba.shape), _const((D_MODEL, D_MODEL)),
                  _const((1, D_MODEL)), _const((1, D_MODEL))],
        out_specs=[_rows(tm, D_MODEL)] * 4,
        out_shape=[_sds((T, D_MODEL), BF16), _sds((T, D_MODEL), F32), _sds((T, D_MODEL), F32),
                   _sds((T, D_MODEL), BF16)],
        args=(y_pool, y_attn, gate, x2, w_bp, w_ba, w_out, g2, g3), sem=("parallel",), exchanges=exchanges)


HALF = D_MODEL // 2
TM_MLP = 256


def _mlp_core(h2, x1, mix, tgt, w_up, w_down, g4, g3, g2):
    T = h2.shape[0]
    tm = min(TM_MLP, T)

    def body(h_ref, x1_ref, mix_ref, t_ref, g_ref, g3_ref, g2_ref, ua_hbm, ub_hbm, da_hbm, db_hbm,
             act_ref, dff_ref, dup_ref, dx1_ref, dmix_ref, loss_ref, dg_ref, dg3_ref, dg2_ref,
             wu, wd, relu_scr, sems):
        def weight_copy(i):
            src, dst = ((ua_hbm, wu.at[:, :HALF]), (ub_hbm, wu.at[:, HALF:]),
                        (da_hbm, wd.at[:, :HALF]), (db_hbm, wd.at[:, HALF:]))[i]
            return pltpu.make_async_copy(src, dst, sems.at[i])

        @pl.when(pl.program_id(0) == 0)
        def _():
            for i in range(4):
                weight_copy(i).start()
            loss_ref[...] = jnp.zeros_like(loss_ref)
            for ref in (dg_ref, dg3_ref, dg2_ref):
                ref[...] = jnp.zeros_like(ref)
            weight_copy(0).wait()
            weight_copy(1).wait()

        h = h_ref[...]
        ff = None
        for j in range(N_CHIPS):
            lo = D_MODEL * j
            relu = jnp.maximum(_dot(h, wu[j]), 0.0)
            if j == 0:
                @pl.when(pl.program_id(0) == 0)
                def _():
                    weight_copy(2).wait()
                    weight_copy(3).wait()
            relu_scr[:, lo:lo + D_MODEL] = relu
            act = jnp.square(relu).astype(BF16)
            act_ref[:, lo:lo + D_MODEL] = act
            t = _dot(act, wd[j])
            ff = t if ff is None else ff + t
        g = g_ref[...]
        x1 = x1_ref[...]
        err = x1 + ff * _rms(ff) * g - t_ref[...]
        loss_ref[...] += jnp.sum(err * err) * (0.5 / D_MODEL)
        dy = err * (1.0 / D_MODEL)
        dff, dg = _norm_bwd(ff, g, dy)
        dg_ref[...] += dg
        dff = dff.astype(BF16)
        dff_ref[...] = dff
        dh2 = None
        for j in range(N_CHIPS):
            lo = D_MODEL * j
            dup = (_dot_nt(dff, wd[j]) * (2.0 * relu_scr[:, lo:lo + D_MODEL])).astype(BF16)
            dup_ref[:, lo:lo + D_MODEL] = dup
            t = _dot_nt(dup, wu[j])
            dh2 = t if dh2 is None else dh2 + t
        dx, dg3 = _norm_bwd(x1, g3_ref[...], dh2)
        dx1 = dy + dx
        dx1_ref[...] = dx1
        dg3_ref[...] += dg3
        dmix, dg2 = _norm_bwd(mix_ref[...], g2_ref[...], dx1)
        dmix_ref[...] = dmix.astype(BF16)
        dg2_ref[...] += dg2

    slabs = pltpu.VMEM((N_CHIPS, D_MODEL, D_MODEL), BF16)
    gain = _const((1, D_MODEL))
    return pl.pallas_call(
        body, name="mlp_core", grid=(T // tm,),
        in_specs=[_rows(tm, D_MODEL)] * 4 + [gain] * 3 + [ANY] * 4,
        out_specs=[_rows(tm, D_FF), _rows(tm, D_MODEL), _rows(tm, D_FF), _rows(tm, D_MODEL), _rows(tm, D_MODEL),
                   _const((8, LANES)), gain, gain, gain],
        out_shape=[_sds((T, D_FF), BF16), _sds((T, D_MODEL), BF16), _sds((T, D_FF), BF16), _sds((T, D_MODEL), F32),
                   _sds((T, D_MODEL), BF16), _sds((8, LANES), F32)] + [_sds((1, D_MODEL), F32)] * 3,
        scratch_shapes=[slabs] * 2 + [pltpu.VMEM((tm, D_FF), F32), pltpu.SemaphoreType.DMA((4,))],
        compiler_params=_cp("arbitrary"),
    )(h2, x1, mix, tgt, g4, g3, g2, *w_up, *w_down)


def _dw(tag, a, g, ta, tn, shard_cols=False, exchanges=()):
    T, ka = a.shape
    n = g.shape[1]
    tk = min(2 * TM, T)
    nk = T // tk

    def body(a_ref, g_ref, o_ref):
        @pl.when(pl.program_id(2) == 0)
        def _():
            o_ref[...] = jnp.zeros_like(o_ref)

        o_ref[...] += _dot_tn(a_ref[...], g_ref[...])

    if shard_cols:
        per = (n // N_CHIPS) // tn
        out_spec = pl.BlockSpec((None, ta, tn), lambda i, j, k: (j // per, i, j % per))
        out_shape = _sds((N_CHIPS, ka, n // N_CHIPS), F32)
    else:
        out_spec = pl.BlockSpec((ta, tn), lambda i, j, k: (i, j))
        out_shape = _sds((ka, n), F32)
    return _call(
        body, name="dw_" + tag, grid=(ka // ta, n // tn, nk),
        in_specs=[pl.BlockSpec((tk, ta), lambda i, j, k: (k, i)), pl.BlockSpec((tk, tn), lambda i, j, k: (k, j))],
        out_specs=[out_spec], out_shape=[out_shape],
        args=(a, g), sem=("parallel", "parallel", "arbitrary"), exchanges=exchanges)


def _dw_mix(merged, dmix, y_pool, dbp, y_attn, dba, exchanges=()):
    T = merged.shape[0]
    tk = min(2 * TM, T)
    c = D_MODEL // N_CHIPS

    def body(mg_ref, dmix_ref, yp_ref, dbp_ref, ya_ref, dba_ref, out_ref, bp_ref, ba_ref):
        @pl.when(pl.program_id(0) == 0)
        def _():
            for ref in (out_ref, bp_ref, ba_ref):
                ref[...] = jnp.zeros_like(ref)

        out_ref[...] += _dot_tn(mg_ref[...], dmix_ref[...])
        for y_ref, d_ref, o_ref in ((yp_ref, dbp_ref, bp_ref), (ya_ref, dba_ref, ba_ref)):
            res = _dot_tn(y_ref[...], d_ref[...])
            for j in range(N_CHIPS):
                o_ref[j] += res[:, c * j:c * (j + 1)]

    slabs = (N_CHIPS, POOL_WIDTH, c)
    return _call(
        body, name="dw_mix", grid=(T // tk,),
        in_specs=[_rows(tk, D_MODEL), _rows(tk, D_MODEL), _rows(tk, POOL_WIDTH), _rows(tk, D_MODEL),
                  _rows(tk, ATTN_WIDTH), _rows(tk, D_MODEL)],
        out_specs=[_const((D_MODEL, D_MODEL)), _const(slabs), _const(slabs)],
        out_shape=[_sds((D_MODEL, D_MODEL), F32), _sds(slabs, F32), _sds(slabs, F32)],
        args=(merged, dmix, y_pool, dbp, y_attn, dba), sem=("arbitrary",), exchanges=exchanges)


def _merge_bwd(dmix, gate, y_pool, y_attn, w_out, w_bp, w_ba, exchanges=()):
    T = dmix.shape[0]
    tm = min(TM, T)

    def body(dmix_ref, gate_ref, yp_ref, ya_ref, wo_ref, wbp_ref, wba_ref,
             dbp_ref, dba_ref, dgate_ref, dyp_ref, dya_ref):
        dm = _dot_nt(dmix_ref[...], wo_ref[...])
        for j, (y_ref, db_ref, w_ref, dy_ref) in enumerate(
                ((yp_ref, dbp_ref, wbp_ref, dyp_ref), (ya_ref, dba_ref, wba_ref, dya_ref))):
            sl = slice(D_MODEL * j, D_MODEL * (j + 1))
            gt = gate_ref[:, sl].astype(F32)
            db = (dm * gt).astype(BF16)
            db_ref[...] = db
            dgate_ref[:, sl] = (dm * _branch(y_ref[...], w_ref) * gt * (1.0 - gt)).astype(BF16)
            cw = D_MODEL // N_CHIPS
            dy = _dot_nt(db[:, :cw], w_ref[0])
            for c in range(1, N_CHIPS):
                dy = dy + _dot_nt(db[:, cw * c:cw * (c + 1)], w_ref[c])
            dy_ref[...] = dy.astype(dy_ref.dtype)

    return _call(
        body, name="merge_bwd", grid=(T // tm,),
        in_specs=[_rows(tm, D_MODEL), _rows(tm, GATE_WIDTH), _rows(tm, POOL_WIDTH), _rows(tm, ATTN_WIDTH),
                  _const((D_MODEL, D_MODEL)), _const(w_bp.shape), _const(w_ba.shape)],
        out_specs=[_rows(tm, D_MODEL), _rows(tm, D_MODEL), _rows(tm, GATE_WIDTH), _rows(tm, POOL_WIDTH),
                   _rows(tm, ATTN_WIDTH)],
        out_shape=[_sds((T, D_MODEL), BF16), _sds((T, D_MODEL), BF16), _sds((T, GATE_WIDTH), BF16),
                   _sds((T, POOL_WIDTH), F32), _sds((T, ATTN_WIDTH), BF16)],
        args=(dmix, gate, y_pool, y_attn, w_out, w_bp, w_ba), sem=("parallel",), exchanges=exchanges)


def _attn_bwd(q, k, v, do, sinks, tabs, seq, exchanges=()):
    T = q.shape[0]
    nb = seq // BLOCK
    bl = T // seq
    steps = nb + 1

    def body(sink_ref, q_ref, do_ref, kp_ref, kc_ref, vp_ref, vc_ref, c_ref, a_ref, bt_ref, cp_ref, ap_ref, btp_ref,
             dq_ref, dk_ref, dv_ref, dsink_ref, ck_ref, cv_ref):
        n = pl.program_id(0)

        @pl.when(n == 0)
        def _():
            dsink_ref[...] = jnp.zeros_like(dsink_ref)
            ck_ref[...] = jnp.zeros_like(ck_ref)
            cv_ref[...] = jnp.zeros_like(cv_ref)

        @pl.when(n < nb)
        def _():
            valid, lo = _attn_masks(n)
            for b in range(bl):
                kk = jnp.concatenate([kp_ref[b], kc_ref[b]], axis=0)
                vv = jnp.concatenate([vp_ref[b], vc_ref[b]], axis=0)
                dk_acc = jnp.zeros((2 * BLOCK, KV_WIDTH), F32)
                dv_acc = jnp.zeros((2 * BLOCK, KV_WIDTH), F32)
                for h in range(2):
                    qs = _stack_heads(q_ref.at[b], h, lo)
                    dos = _stack_heads(do_ref.at[b], h, lo)
                    pr, ps = _group_probs(qs, kk, valid, _sink_rows(sink_ref, h))
                    dp = _dot_nt(dos, vv)
                    delta = jnp.sum(pr * dp, axis=1, keepdims=True)
                    ds = (pr * (dp - delta)).astype(BF16)
                    dsk = ps * delta
                    for g in range(GROUP):
                        idx = GROUP * h + g
                        dsink_ref[idx:idx + 1, :] += (jnp.zeros((1, LANES), F32)
                                                      - jnp.sum(dsk[BLOCK * g:BLOCK * (g + 1)]))
                    dk_acc = dk_acc + _dot_tn(ds, qs)
                    dv_acc = dv_acc + _dot_tn(pr.astype(BF16), dos)
                    for j, pair in enumerate(_unstack_heads(_dot(ds, kk) * SCALE, h, lo)):
                        sl = slice(LANES * (2 * h + j), LANES * (2 * h + j + 1))
                        dq_ref[b, :, sl] = _rot_bwd(pair, c_ref[...], a_ref[...], bt_ref[...]).astype(BF16)
                fin_k = ck_ref[b] + dk_acc[:BLOCK]
                dk_ref[b] = _rot_bwd(fin_k, cp_ref[...], ap_ref[...], btp_ref[...]).astype(BF16)
                dv_ref[b] = (cv_ref[b] + dv_acc[:BLOCK]).astype(BF16)
                ck_ref[b] = dk_acc[BLOCK:]
                cv_ref[b] = dv_acc[BLOCK:]

        @pl.when(n == nb)
        def _():
            for b in range(bl):
                dk_ref[b] = _rot_bwd(ck_ref[b], cp_ref[...], ap_ref[...], btp_ref[...]).astype(BF16)
                dv_ref[b] = cv_ref[b].astype(BF16)

    cur = lambda n: (0, jnp.minimum(n, nb - 1), 0)
    prv = lambda n: (0, jnp.clip(n - 1, 0, nb - 1), 0)
    tcur = lambda n: (jnp.minimum(n, nb - 1), 0)
    tprv = lambda n: (jnp.clip(n - 1, 0, nb - 1), 0)
    wide = lambda m: pl.BlockSpec((bl, BLOCK, ATTN_WIDTH), m)
    kv = lambda m: pl.BlockSpec((bl, BLOCK, KV_WIDTH), m)
    tab = lambda m: pl.BlockSpec((BLOCK, LANES), m)
    res = _call(
        body, name="attn_bwd", grid=(steps,),
        in_specs=[pl.BlockSpec(memory_space=pltpu.SMEM), wide(cur), wide(cur), kv(prv), kv(cur), kv(prv), kv(cur),
                  tab(tcur), tab(tcur), tab(tcur), tab(tprv), tab(tprv), tab(tprv)],
        out_specs=[wide(cur), kv(prv), kv(prv), _const((8, LANES))],
        out_shape=[_sds((bl, seq, ATTN_WIDTH), BF16), _sds((bl, seq, KV_WIDTH), BF16),
                   _sds((bl, seq, KV_WIDTH), BF16), _sds((8, LANES), F32)],
        scratch=[pltpu.VMEM((bl, BLOCK, KV_WIDTH), F32), pltpu.VMEM((bl, BLOCK, KV_WIDTH), F32)],
        args=(sinks, *_by_example(bl, q, do, k, k, v, v), *tabs, *tabs), sem=("arbitrary",), exchanges=exchanges)
    outs, rest = (res if exchanges else (res, None))
    outs = [outs[0].reshape(T, ATTN_WIDTH), outs[1].reshape(T, KV_WIDTH), outs[2].reshape(T, KV_WIDTH), outs[3]]
    return (outs, rest) if exchanges else outs


def _pool_bwd(dyp, diff, w_pool, pool_scale, seq, exchanges=()):
    T = dyp.shape[0]
    tp = min(TP, seq)
    nseq = seq // tp
    per = tp // HALO
    last_halo = T // HALO - 1

    def body(dy_ref, nxt_ref, diff_ref, w_ref, s_ref, du_ref, dw_ref, ds_ref):
        i = pl.program_id(0)

        @pl.when(i == 0)
        def _():
            dw_ref[...] = jnp.zeros_like(dw_ref)
            ds_ref[...] = jnp.zeros_like(ds_ref)

        last = (i % nseq) == nseq - 1
        nxt = jnp.where(last, 0.0, nxt_ref[...])
        ext = jnp.concatenate([dy_ref[...], nxt], axis=0) * s_ref[...]
        pos = (i % nseq) * tp + lax.broadcasted_iota(jnp.int32, (tp + HALO, 1), 0)
        for gi, w in enumerate(POOL_WINDOWS):
            sl = slice(POOL_GC * gi, POOL_GC * (gi + 1))
            wg = w_ref[gi].astype(BF16)
            dmx = ext[:, sl].astype(BF16)
            ddiff = _dot_nt(dmx, wg)
            s = ddiff * _inv_count(pos, w)
            sh = 1
            while sh < w:
                s = s + pltpu.roll(s, tp + HALO - sh, 0)
                sh *= 2
            du_ref[:, sl] = (s[:tp] - ddiff[:tp]).astype(BF16)
            dg = diff_ref[:, sl]
            dw_ref[gi] += _dot_tn(dg, dmx[:tp])
            ds_ref[:, sl] += jnp.sum(dy_ref[:, sl] * _dot(dg, wg), axis=0, keepdims=True)

    return _call(
        body, name="pool_bwd", grid=(T // tp,),
        in_specs=[_rows(tp, POOL_WIDTH),
                  pl.BlockSpec((HALO, POOL_WIDTH), lambda i: (jnp.minimum((i + 1) * per, last_halo), 0)),
                  _rows(tp, POOL_WIDTH), _const((4, POOL_GC, POOL_GC)), _const((1, POOL_WIDTH))],
        out_specs=[_rows(tp, POOL_WIDTH), _const((4, POOL_GC, POOL_GC)), _const((1, POOL_WIDTH))],
        out_shape=[_sds((T, POOL_WIDTH), BF16), _sds((4, POOL_GC, POOL_GC), F32), _sds((1, POOL_WIDTH), F32)],
        args=(dyp, dyp, diff, w_pool, pool_scale), sem=("arbitrary",), exchanges=exchanges)


_PARTS = ((0, C_Q), (C_Q, C_K), (C_K, C_V), (C_V, C_G), (C_G, IN_WIDTH))


def _inproj_bwd(parts, x2, dx1, w_in_t, g1, exchanges=()):
    T = x2.shape[0]
    tm = min(TM, T)

    def body(du_ref, dq_ref, dk_ref, dv_ref, dgt_ref, x_ref, dx1_ref, w_ref, g_ref, gx_ref, dg_ref):
        @pl.when(pl.program_id(0) == 0)
        def _():
            dg_ref[...] = jnp.zeros_like(dg_ref)

        dh = jnp.zeros((tm, D_MODEL), F32)
        for (lo, hi), p_ref in zip(_PARTS, (du_ref, dq_ref, dk_ref, dv_ref, dgt_ref)):
            dh = dh + _dot(p_ref[...], w_ref[lo:hi, :])
        dx, dg = _norm_bwd(x_ref[...], g_ref[...], dh)
        gx_ref[...] = dx1_ref[...] + dx
        dg_ref[...] += dg

    return _call(
        body, name="inproj_bwd", grid=(T // tm,),
        in_specs=[_rows(tm, hi - lo) for lo, hi in _PARTS]
        + [_rows(tm, D_MODEL), _rows(tm, D_MODEL), _const((IN_WIDTH, D_MODEL)), _const((1, D_MODEL))],
        out_specs=[_rows(tm, D_MODEL), _const((1, D_MODEL))],
        out_shape=[_sds((T, D_MODEL), F32), _sds((1, D_MODEL), F32)],
        args=(*parts, x2, dx1, w_in_t, g1), sem=("arbitrary",), exchanges=exchanges)


def _dw_in(h, parts, exchanges=()):
    T = h.shape[0]
    tk = min(TM, T)

    def body(h_ref, du_ref, dq_ref, dk_ref, dv_ref, dgt_ref, o_ref, db_ref):
        @pl.when(pl.program_id(0) == 0)
        def _():
            o_ref[...] = jnp.zeros_like(o_ref)
            db_ref[...] = jnp.zeros_like(db_ref)

        hh = h_ref[...]
        for (lo, hi), p_ref in zip(_PARTS, (du_ref, dq_ref, dk_ref, dv_ref, dgt_ref)):
            part = p_ref[...]
            o_ref[lo:hi, :] += _dot_tn(part, hh)
            db_ref[:, lo:hi] += jnp.sum(part.astype(F32), axis=0, keepdims=True)

    return _call(
        body, name="dw_in", grid=(T // tk,),
        in_specs=[_rows(tk, D_MODEL)] + [_rows(tk, hi - lo) for lo, hi in _PARTS],
        out_specs=[_const((IN_WIDTH, D_MODEL)), _const((1, IN_WIDTH))],
        out_shape=[_sds((IN_WIDTH, D_MODEL), F32), _sds((1, IN_WIDTH), F32)],
        args=(h, *parts), sem=("arbitrary",), exchanges=exchanges)


def _row_tile(rows, cap=256, mult=16):
    best = None
    for t in range(mult, min(rows, cap) + 1, mult):
        if rows % t == 0:
            best = t
    if best is None:
        raise ValueError("no row tile for %d rows" % rows)
    return best


def _pair_sum(ids, full, got):
    _, r, c = full.shape
    hr = r // 2
    tr = _row_tile(hr)
    nblk = hr // tr

    def body(ids_ref, a_ref, b_ref, own_ref, sb_ref):
        s = a_ref[...] + b_ref[...]
        sb_ref[...] = s.astype(BF16)

        @pl.when(pl.program_id(1) == ids_ref[0])
        def _():
            own_ref[...] = s

    slab = pl.BlockSpec((None, tr, c), lambda i, j, ids_ref: (j, i, 0))
    return pl.pallas_call(
        body, name="pair_sum_%dx%d" % (r, c),
        grid_spec=pltpu.PrefetchScalarGridSpec(
            num_scalar_prefetch=1, grid=(nblk, N_CHIPS),
            in_specs=[pl.BlockSpec((None, tr, c), lambda i, j, ids_ref: (j, ids_ref[1] * nblk + i, 0)), slab],
            out_specs=[pl.BlockSpec((tr, c), lambda i, j, ids_ref: (i, 0)), slab]),
        out_shape=[_sds((hr, c), F32), _sds((N_CHIPS, hr, c), BF16)],
        compiler_params=_cp("parallel", "arbitrary"),
    )(ids, full, got)


def _chip_sum(ids, own, got):
    hr, c = own.shape
    tr = _row_tile(hr)
    nblk = hr // tr

    def body(ids_ref, a_ref, b_ref, o_ref):
        o_ref[...] = ((a_ref[...] + b_ref[0].astype(F32)) + b_ref[1].astype(F32)) + b_ref[2].astype(F32)

    return pl.pallas_call(
        body, name="chip_sum_%dx%d" % (hr, c),
        grid_spec=pltpu.PrefetchScalarGridSpec(
            num_scalar_prefetch=1, grid=(nblk,),
            in_specs=[pl.BlockSpec((tr, c), lambda i, ids_ref: (i, 0)),
                      pl.BlockSpec((3, tr, c), lambda i, ids_ref: (0, i, 0))],
            out_specs=pl.BlockSpec((tr, c), lambda i, ids_ref: (ids_ref[1] * nblk + i, 0))),
        out_shape=_sds((2 * hr, c), F32),
        compiler_params=_cp("parallel"),
    )(ids, own, got)


def _adamw_math(w, g, m, v):
    nm = ADAM_B1 * m + (1.0 - ADAM_B1) * g
    nv = ADAM_B2 * v + (1.0 - ADAM_B2) * (g * g)
    m_hat = nm / (1.0 - ADAM_B1 ** ADAM_STEP)
    v_hat = nv / (1.0 - ADAM_B2 ** ADAM_STEP)
    return -ADAM_LR * (m_hat / (jnp.sqrt(v_hat) + ADAM_EPS) + ADAM_WD * w), nm, nv


def _adamw(w, g, m, v):
    r, c = w.shape
    tr = _row_tile(r, cap=512, mult=8)

    def body(w_ref, g_ref, m_ref, v_ref, d_ref, nm_ref, nv_ref):
        d_ref[...], nm_ref[...], nv_ref[...] = _adamw_math(w_ref[...], g_ref[...], m_ref[...], v_ref[...])

    spec = _rows(tr, c)
    return pl.pallas_call(
        body, name="adamw_%dx%d" % (r, c), grid=(r // tr,),
        in_specs=[spec] * 4, out_specs=[spec] * 3, out_shape=[_sds((r, c), F32)] * 3,
        compiler_params=_cp("parallel"),
    )(w, g, m, v)


SC_TILES = 32
SC_LANES = 16
SC_ROWS = 8


def _sparse_mesh():
    return plsc.VectorSubcoreMesh(core_axis_name="sc_core", subcore_axis_name="sc_subcore")


def _sparse_tile():
    return lax.axis_index("sc_subcore") * 2 + lax.axis_index("sc_core")


def _adamw_sparse(tag, items, after=()):
    n = len(items)
    shapes = [w.shape for w, _, _, _ in items]

    def body(*refs):
        refs = refs[len(after):]
        ins, outs, bufs = refs[:4 * n], refs[4 * n:8 * n], refs[8 * n:]
        tile = _sparse_tile()
        for k, (r, c) in enumerate(shapes):
            rows = r // SC_TILES
            step = min(rows, SC_ROWS)
            wb, gb, mb, vb = bufs[4 * k:4 * k + 4]

            @pl.loop(0, rows, step=step)
            def _(r0):
                mine = pl.ds(tile * rows + r0, step)
                for src, dst in zip(ins[4 * k:4 * k + 4], (wb, gb, mb, vb)):
                    pltpu.sync_copy(src.at[mine], dst)

                @pl.loop(0, step)
                def _(row):
                    @pl.loop(0, c, step=SC_LANES)
                    def _(i):
                        at = (row, pl.ds(i, SC_LANES))
                        wb[at], mb[at], vb[at] = _adamw_math(wb[at], gb[at], mb[at], vb[at])

                for src, dst in zip((wb, mb, vb, gb), outs[4 * k:4 * k + 4]):
                    pltpu.sync_copy(src, dst.at[mine])

    res = pl.kernel(
        body, name="adamw_sparse_" + tag, out_type=[_sds(s, F32) for s in shapes for _ in range(4)],
        mesh=_sparse_mesh(),
        scratch_types=[pltpu.VMEM((min(r // SC_TILES, SC_ROWS), c), F32) for r, c in shapes for _ in range(4)],
    )(*after, *[a for item in items for a in item])
    return [tuple(res[4 * k:4 * k + 4]) for k in range(n)]


def _copy_sparse(a, after=()):
    r, c = a.shape
    rows = r // SC_TILES

    def body(*refs):
        a_hbm, o_hbm, buf = refs[len(after):]
        tile = _sparse_tile()

        @pl.loop(0, rows, step=SC_ROWS)
        def _(r0):
            mine = pl.ds(tile * rows + r0, SC_ROWS)
            pltpu.sync_copy(a_hbm.at[mine], buf)
            pltpu.sync_copy(buf, o_hbm.at[mine])

    return pl.kernel(
        body, name="copy_sparse_%dx%d" % (r, c), out_type=_sds((r, c), F32), mesh=_sparse_mesh(),
        scratch_types=[pltpu.VMEM((SC_ROWS, c), F32)],
    )(*after, a)


_SMALL_NAMES = ("w_pool", "b_in", "g_mix_pre", "g_mix_post", "g_mlp_pre", "g_mlp_post", "pool_scale", "attn_sinks")
B_ROWS = -(-IN_WIDTH // D_MODEL)


def _row_block(rows):
    rows = [jnp.pad(r.astype(F32), ((0, 0), (0, D_MODEL - r.shape[1]))) for r in rows]
    return jnp.pad(jnp.concatenate(rows, axis=0), ((0, 8 - len(rows)), (0, 0)))


def _early_block(dg2, dg3, dg4, dps, dsink, loss):
    tail = jnp.concatenate([jnp.pad(dsink.reshape(1, -1), ((0, 0), (0, LANES - dsink.size))),
                            jnp.pad(loss.reshape(1, 1), ((0, 0), (0, LANES - 1)))], axis=1)
    return _row_block([dg2, dg3, dg4, dps, tail])


def _late_block(db_in, dg1):
    b = jnp.pad(db_in, ((0, 0), (0, B_ROWS * D_MODEL - IN_WIDTH))).reshape(B_ROWS, D_MODEL)
    return _row_block([b[r:r + 1] for r in range(B_ROWS)] + [dg1])


def _small_update(gearly, gmat, glate, w, m, v):
    names = _SMALL_NAMES
    n = len(names)

    def total(ref, rows):
        acc = ref[0:rows, :]
        for d in range(1, N_DEV):
            acc = acc + ref[d * rows:(d + 1) * rows, :]
        return acc

    def body(*refs):
        early_ref, gmat_ref, late_ref = refs[:3]
        w_refs, m_refs, v_refs = refs[3:3 + n], refs[3 + n:3 + 2 * n], refs[3 + 2 * n:3 + 3 * n]
        outs = refs[3 + 3 * n:]
        loss_ref, g_refs, d_refs = outs[0], outs[1:1 + n], outs[1 + n:1 + 2 * n]
        nm_refs, nv_refs = outs[1 + 2 * n:1 + 3 * n], outs[1 + 3 * n:1 + 4 * n]
        early, late = total(early_ref, 8), total(late_ref, 8)
        loss_ref[...] = jnp.sum(early[4:5, LANES:2 * LANES], axis=1, keepdims=True)
        bias = jnp.concatenate([late[r:r + 1, :] for r in range(B_ROWS - 1)]
                               + [late[B_ROWS - 1:B_ROWS, :IN_WIDTH - (B_ROWS - 1) * D_MODEL]], axis=1)
        grad = dict(b_in=bias, g_mix_pre=late[B_ROWS:B_ROWS + 1, :], g_mix_post=early[0:1, :],
                    g_mlp_pre=early[1:2, :], g_mlp_post=early[2:3, :], pool_scale=early[3:4, :POOL_WIDTH],
                    attn_sinks=early[4:5, :N_Q_HEADS])
        for i, name in enumerate(names):
            g = total(gmat_ref, 4 * POOL_GC) if name == "w_pool" else grad[name]
            g_refs[i][...] = g
            d_refs[i][...], nm_refs[i][...], nv_refs[i][...] = _adamw_math(
                w_refs[i][...], g, m_refs[i][...], v_refs[i][...])

    shapes = [_sds(w[k].shape, F32) for k in names]
    res = pl.pallas_call(
        body, name="small_update", out_shape=[_sds((1, 1), F32)] + shapes * 4,
        compiler_params=pltpu.CompilerParams(vmem_limit_bytes=VMEM_MB * 1024 * 1024),
    )(gearly, gmat, glate, *[w[k] for k in names], *[m[k] for k in names], *[v[k] for k in names])
    loss = res[0]
    per = {k: tuple(res[1 + j * n + i] for j in range(4)) for i, k in enumerate(names)}
    return loss, per


_BIG = ("w_in", "w_branch_pool", "w_branch_attn", "w_out", "w_up", "w_down")
_ORDER = ("g_mix_pre", "w_in", "b_in", "w_pool", "pool_scale", "attn_sinks", "w_branch_pool", "w_branch_attn",
          "w_out", "g_mix_post", "g_mlp_pre", "w_up", "w_down", "g_mlp_post")


def _stack_rows(slab):
    return slab.reshape(-1, slab.shape[2])


def _step(x2, tgt, seq, shards, small, ids):
    tabs = _rope_tables(seq)
    g1, g2, g3, g4 = (small[n] for n in ("g_mix_pre", "g_mix_post", "g_mlp_pre", "g_mlp_post"))
    sinks = small["attn_sinks"].reshape(N_Q_HEADS)
    w_pool = small["w_pool"].reshape(4, POOL_GC, POOL_GC)
    pool_scale = small["pool_scale"]

    def whole(shard, slabs):
        return lax.dynamic_update_slice(slabs, shard[None], (ids[0], 0, 0))

    up_a, up_b = shards["w_up"][:HALF], shards["w_up"][HALF:]
    down_a, down_b = shards["w_down"][:HALF], shards["w_down"][HALF:]
    w_in = _stack_rows(whole(shards["w_in"], _alone("gather_in", _ex_gather([shards["w_in"]]))[0][0]))
    mix_shards = [shards[n] for n in ("w_branch_pool", "w_branch_attn", "w_out")]
    (h, u, q, k, v, gate), [(*mix_slabs, got_c)] = _inproj(
        x2, g1, w_in, small["b_in"], tabs, seq, exchanges=[_ex_gather(mix_shards + [down_a])])
    w_bp, w_ba, out_slab = (whole(s, g) for s, g in zip(mix_shards, mix_slabs))
    w_out = _stack_rows(out_slab)
    diff, y_pool = _pool_fwd(u, w_pool, pool_scale, seq)
    (y_attn,), [[got_a]] = _attn_fwd(q, k, v, sinks, seq, exchanges=[_ex_gather([up_a])])
    (merged, mix, x1, h2), [[got_b, got_d]] = _merge_out(
        y_pool, y_attn, gate, x2, w_bp, w_ba, w_out, g2, g3, exchanges=[_ex_gather([up_b, down_b])])
    w_up = (whole(up_a, got_a), whole(up_b, got_b))
    w_down = (whole(down_a, got_c), whole(down_b, got_d))
    act, dff, dup, dx1, dmix, loss_acc, dg4, dg3, dg2 = _mlp_core(h2, x1, mix, tgt, w_up, w_down, g4, g3, g2)

    dw_down = _dw("down", act, dff, 1024, 1024)[0].reshape(N_CHIPS, D_FF // N_CHIPS, D_MODEL)
    (dbp, dba, dgate, dyp, dya), [[got]] = _merge_bwd(
        dmix, gate, y_pool, y_attn, w_out, w_bp, w_ba, exchanges=[_ex_pair([dw_down])])
    ps_down = _pair_sum(ids, dw_down, got)
    (dw_up,), [[got]] = _dw("up", h2, dup, 1024, 1024, shard_cols=True, exchanges=[_ex_chip([ps_down[1]])])
    half_down = _chip_sum(ids, ps_down[0], got)
    (dw_out, dw_bp, dw_ba), [[got]] = _dw_mix(merged, dmix, y_pool, dbp, y_attn, dba, exchanges=[_ex_pair([dw_up])])
    ps_up = _pair_sum(ids, dw_up, got)
    dw_mix = [dw_out.reshape(N_CHIPS, D_MODEL // N_CHIPS, D_MODEL), dw_bp, dw_ba]
    (dq, dk, dv, dsink), [[got], gots, [g_down]] = _attn_bwd(
        q, k, v, dya, sinks, tabs, seq, exchanges=[_ex_chip([ps_up[1]]), _ex_pair(dw_mix), _ex_swap([half_down])])
    half_up = _chip_sum(ids, ps_up[0], got)
    ps_mix = [_pair_sum(ids, d, g) for d, g in zip(dw_mix, gots)]
    (du, dw_pool, dps), [[g_up]] = _pool_bwd(dyp, diff, w_pool, pool_scale, seq, exchanges=[_ex_swap([half_up])])
    parts = (du, dq, dk, dv, dgate)
    early = _early_block(dg2, dg3, dg4, dps, dsink[:, 0], loss_acc[0, 0])
    mat = dw_pool.reshape(4 * POOL_GC, POOL_GC)
    (dw_in_t, db_in), [gots, [gearly, gmat]] = _dw_in(
        h, parts, exchanges=[_ex_chip([p[1] for p in ps_mix]), _ex_allgather([early, mat])])
    half_mix = [_chip_sum(ids, p[0], g) for p, g in zip(ps_mix, gots)]
    dw_in = dw_in_t.reshape(N_CHIPS, IN_WIDTH // N_CHIPS, D_MODEL)
    g_mix, [got] = _alone("swap_mix_pair_in", _ex_swap(half_mix), _ex_pair([dw_in]))
    ps_in = _pair_sum(ids, dw_in, got)
    (gx, dg1), [[got]] = _inproj_bwd(parts, x2, dx1, w_in, g1, exchanges=[_ex_chip([ps_in[1]])])
    [g_in], [glate] = _alone("swap_in_allgather", _ex_swap([_chip_sum(ids, ps_in[0], got)]),
                             _ex_allgather([_late_block(db_in, dg1)]))

    grads = dict(w_in=g_in, w_branch_pool=g_mix[1], w_branch_attn=g_mix[2], w_out=g_mix[0], w_up=g_up, w_down=g_down)
    return (gearly, gmat, glate), gx, grads


def kernel(x, g_mix_pre, w_in, b_in, w_pool, pool_scale, attn_sinks, w_branch_pool, w_branch_attn, w_out, g_mix_post, g_mlp_pre, w_up, w_down, g_mlp_post, loss_target, m_g_mix_pre, m_w_in, m_b_in, m_w_pool, m_pool_scale, m_attn_sinks, m_w_branch_pool, m_w_branch_attn, m_w_out, m_g_mix_post, m_g_mlp_pre, m_w_up, m_w_down, m_g_mlp_post, v_g_mix_pre, v_w_in, v_b_in, v_w_pool, v_pool_scale, v_attn_sinks, v_w_branch_pool, v_w_branch_attn, v_w_out, v_g_mix_post, v_g_mlp_pre, v_w_up, v_w_down, v_g_mlp_post):
    weights = dict(g_mix_pre=g_mix_pre, w_in=w_in, b_in=b_in, w_pool=w_pool, pool_scale=pool_scale,
                   attn_sinks=attn_sinks, w_branch_pool=w_branch_pool, w_branch_attn=w_branch_attn, w_out=w_out,
                   g_mix_post=g_mix_post, g_mlp_pre=g_mlp_pre, w_up=w_up, w_down=w_down, g_mlp_post=g_mlp_post)
    mom1 = dict(g_mix_pre=m_g_mix_pre, w_in=m_w_in, b_in=m_b_in, w_pool=m_w_pool, pool_scale=m_pool_scale,
                attn_sinks=m_attn_sinks, w_branch_pool=m_w_branch_pool, w_branch_attn=m_w_branch_attn,
                w_out=m_w_out, g_mix_post=m_g_mix_post, g_mlp_pre=m_g_mlp_pre, w_up=m_w_up, w_down=m_w_down,
                g_mlp_post=m_g_mlp_post)
    mom2 = dict(g_mix_pre=v_g_mix_pre, w_in=v_w_in, b_in=v_b_in, w_pool=v_w_pool, pool_scale=v_pool_scale,
                attn_sinks=v_attn_sinks, w_branch_pool=v_w_branch_pool, w_branch_attn=v_w_branch_attn,
                w_out=v_w_out, g_mix_post=v_g_mix_post, g_mlp_pre=v_g_mlp_pre, w_up=v_w_up, w_down=v_w_down,
                g_mlp_post=v_g_mlp_post)
    b_loc, seq, _ = x.shape
    x2 = x.reshape(b_loc * seq, D_MODEL)
    tgt = loss_target.reshape(b_loc * seq, D_MODEL)
    ids = jnp.stack([2 * lax.axis_index("x") + lax.axis_index("y"), lax.axis_index("c")]).astype(jnp.int32)

    def flat(n, a):
        return a[0].T if n == "w_in" else a[0]

    def unflat(n, a):
        return (a.T if n == "w_in" else a)[None]

    shards = {n: flat(n, weights[n]).astype(BF16) for n in _BIG}
    small = {n: weights[n] for n in _ORDER if n not in _BIG}
    (gearly, gmat, glate), gx, grads = _step(x2, tgt, seq, shards, small, ids)

    def two_d(src):
        return {n: src[n].reshape(4 * POOL_GC, POOL_GC) if n == "w_pool" else src[n] for n in _SMALL_NAMES}

    loss, per = _small_update(gearly, gmat, glate, two_d(weights), two_d(mom1), two_d(mom2))
    delta, new_m, new_v = {}, {}, {}
    for n in _SMALL_NAMES:
        grads[n], delta[n], new_m[n], new_v[n] = (a.reshape(weights[n].shape) for a in per[n])
    def operands(n):
        return flat(n, weights[n]), grads[n], flat(n, mom1[n]), flat(n, mom2[n])

    done = {"w_in": (*_adamw(*operands("w_in")), grads["w_in"])}
    for tag, group in (("down", ("w_down",)), ("up", ("w_up",)), ("mix", ("w_out", "w_branch_pool", "w_branch_attn"))):
        done.update(zip(group, _adamw_sparse(tag, [operands(n) for n in group])))
    for n in _BIG:
        delta[n], new_m[n], new_v[n], grads[n] = (unflat(n, a) for a in done[n])
    gx = _copy_sparse(gx, after=(done["w_out"][0],))

    return (loss[0, 0], gx.reshape(x.shape), *[grads[n] for n in _ORDER], *[delta[n] for n in _ORDER],
            *[new_m[n] for n in _ORDER], *[new_v[n] for n in _ORDER])
```

```python
import jax
import jax.numpy as jnp
from jax import lax
from jax.experimental import pallas as pl
from jax.experimental.pallas import tpu as pltpu
from jax.experimental.pallas import tpu_sc as plsc

F32 = jnp.float32
BF16 = jnp.bfloat16

D_MODEL = 1024
POOL_WINDOWS = (2, 4, 8, 16)
POOL_WIDTH = 512
POOL_GC = 128
HALO = 16
HEAD_DIM = 64
N_Q_HEADS = 8
ATTN_WIDTH = 512
KV_WIDTH = 128
BLOCK = 128
NEG_INF = -1e30
ROPE_THETA = 500000.0
ROT_DIM = 16
GATE_WIDTH = 2048
IN_WIDTH = 3328
D_FF = 4096
EPS = 1e-6
SCALE = HEAD_DIM ** -0.5
C_Q, C_K, C_V, C_G = 512, 1024, 1152, 1280

ADAM_LR, ADAM_B1, ADAM_B2, ADAM_EPS, ADAM_WD, ADAM_STEP = 0.001, 0.9, 0.999, 1e-08, 0.01, 10

N_CHIPS = 4
N_DEV = 8
LANES = 128
TM = 512
TP = 512
VMEM_MB = 56

MESH = pl.DeviceIdType.MESH
ANY = pl.BlockSpec(memory_space=pl.ANY)


def _cp(*sem, vmem=VMEM_MB):
    return pltpu.CompilerParams(dimension_semantics=sem, vmem_limit_bytes=vmem * 1024 * 1024)


def _rows(tile, cols):
    return pl.BlockSpec((tile, cols), lambda i: (i, 0))


def _const(shape):
    nd = len(shape)
    return pl.BlockSpec(shape, lambda i: (0,) * nd)


def _sds(shape, dtype):
    return jax.ShapeDtypeStruct(shape, dtype)


def _dot(a, b):
    return jnp.dot(a, b, preferred_element_type=F32)


def _dot_nt(a, b):
    return lax.dot_general(a, b, (((1,), (1,)), ((), ())), preferred_element_type=F32)


def _dot_tn(a, b):
    return lax.dot_general(a, b, (((0,), (0,)), ((), ())), preferred_element_type=F32)


def _rms(x):
    return lax.rsqrt(jnp.mean(x * x, axis=-1, keepdims=True) + EPS)


def _norm_bwd(x, g, dout):
    r = _rms(x)
    n = x * r
    dn = dout * g
    dx = r * (dn - n * jnp.mean(dn * n, axis=-1, keepdims=True))
    return dx, jnp.sum(dout * n, axis=0, keepdims=True)


def _rot_fwd(t, c, a, bt):
    return t * c + pltpu.roll(t, LANES - 8, 1) * a + pltpu.roll(t, 8, 1) * bt


def _rot_bwd(d, c, a, bt):
    return d * c + pltpu.roll(d * a, 8, 1) + pltpu.roll(d * bt, LANES - 8, 1)


def _rope_tables(seq):
    pos = jnp.arange(seq, dtype=F32)
    inv_freq = ROPE_THETA ** (-jnp.arange(0, ROT_DIM, 2, dtype=F32) / ROT_DIM)
    ang = pos[:, None] * inv_freq[None, :]
    cos, sin = jnp.cos(ang), jnp.sin(ang)
    ones = jnp.ones((seq, HEAD_DIM - ROT_DIM), F32)
    zeros8 = jnp.zeros((seq, 8), F32)
    zrest = jnp.zeros((seq, HEAD_DIM - ROT_DIM), F32)
    c = jnp.concatenate([cos, cos, ones], axis=1)
    a = jnp.concatenate([-sin, zeros8, zrest], axis=1)
    bt = jnp.concatenate([zeros8, sin, zrest], axis=1)
    return tuple(jnp.tile(t, (1, 2)) for t in (c, a, bt))


class _Exchange:
    def __init__(self, inputs, out_shapes, sems, start, finish, aliases=None, middle=None):
        self.inputs, self.out_shapes, self.sems = list(inputs), list(out_shapes), list(sems)
        self.start, self.finish, self.aliases = start, finish, dict(aliases or {})
        self.middle = middle


def _call(body, *, name, grid, in_specs, out_specs, out_shape, args, scratch=(), sem=(), exchanges=()):
    in_specs, out_specs, out_shape, scratch = list(in_specs), list(out_specs), list(out_shape), list(scratch)
    if not exchanges:
        return pl.pallas_call(body, name=name, grid=grid, in_specs=in_specs, out_specs=out_specs,
                              out_shape=out_shape, scratch_shapes=scratch, compiler_params=_cp(*sem))(*args)
    n_in, n_out, n_scr = len(in_specs), len(out_specs), len(scratch)
    x_in = [a for ex in exchanges for a in ex.inputs]
    x_out = [s for ex in exchanges for s in ex.out_shapes]
    x_sem = [s for ex in exchanges for s in ex.sems]
    aliases, i_off, o_off = {}, n_in, n_out
    for ex in exchanges:
        for i, o in ex.aliases.items():
            aliases[i_off + i] = o_off + o
        i_off += len(ex.inputs)
        o_off += len(ex.out_shapes)

    def split(flat):
        out, pos = [], 0
        for ex, n in zip(exchanges, flat[1]):
            out.append(flat[0][pos:pos + n])
            pos += n
        return out

    def carrier(*refs):
        pos = 0
        groups = []
        for n in (n_in, len(x_in), n_out, len(x_out), n_scr, len(x_sem)):
            groups.append(refs[pos:pos + n])
            pos += n
        ins, xin, outs, xout, scr, xsem = groups
        xin = split((xin, [len(ex.inputs) for ex in exchanges]))
        xout = split((xout, [len(ex.out_shapes) for ex in exchanges]))
        xsem = split((xsem, [len(ex.sems) for ex in exchanges]))
        first = pl.program_id(0) == 0
        last = pl.program_id(0) == grid[0] - 1
        for d in range(1, len(grid)):
            first = jnp.logical_and(first, pl.program_id(d) == 0)
            last = jnp.logical_and(last, pl.program_id(d) == grid[d] - 1)

        @pl.when(first)
        def _():
            for ex, i, o, s in zip(exchanges, xin, xout, xsem):
                ex.start(i, o, s)

        if any(ex.middle for ex in exchanges):
            half = pl.program_id(0) == grid[0] // 2
            for d in range(1, len(grid)):
                half = jnp.logical_and(half, pl.program_id(d) == 0)

            @pl.when(half)
            def _():
                for ex, i, o, s in zip(exchanges, xin, xout, xsem):
                    if ex.middle:
                        ex.middle(i, o, s)

        body(*ins, *outs, *scr)

        @pl.when(last)
        def _():
            for ex, i, o, s in zip(exchanges, xin, xout, xsem):
                ex.finish(i, o, s)

    res = pl.pallas_call(
        carrier, name=name, grid=grid, in_specs=in_specs + [ANY] * len(x_in),
        out_specs=out_specs + [ANY] * len(x_out), out_shape=out_shape + x_out,
        scratch_shapes=scratch + x_sem, input_output_aliases=aliases,
        compiler_params=_cp(*(["arbitrary"] * len(grid))),
    )(*args, *x_in)
    return res[:n_out], split((res[n_out:], [len(ex.out_shapes) for ex in exchanges]))


def _alone(name, *exchanges):
    n_in = [len(ex.inputs) for ex in exchanges]
    n_out = [len(ex.out_shapes) for ex in exchanges]
    n_sem = [len(ex.sems) for ex in exchanges]
    aliases, i_off, o_off = {}, 0, 0
    for ex in exchanges:
        for i, o in ex.aliases.items():
            aliases[i_off + i] = o_off + o
        i_off += len(ex.inputs)
        o_off += len(ex.out_shapes)

    def split(flat, counts):
        out, pos = [], 0
        for n in counts:
            out.append(flat[pos:pos + n])
            pos += n
        return out

    def body(*refs):
        ins, outs, sems = split(refs, [sum(n_in), sum(n_out), sum(n_sem)])
        groups = list(zip(exchanges, split(ins, n_in), split(outs, n_out), split(sems, n_sem)))
        for ex, i, o, s in groups:
            ex.start(i, o, s)
        for ex, i, o, s in groups:
            if ex.middle:
                ex.middle(i, o, s)
        for ex, i, o, s in groups:
            ex.finish(i, o, s)

    res = pl.pallas_call(
        body, name=name, in_specs=[ANY] * sum(n_in), out_specs=[ANY] * sum(n_out),
        out_shape=[s for ex in exchanges for s in ex.out_shapes],
        scratch_shapes=[s for ex in exchanges for s in ex.sems], input_output_aliases=aliases,
    )(*[a for ex in exchanges for a in ex.inputs])
    return split(res, n_out)


def _place():
    x, y, c = lax.axis_index("x"), lax.axis_index("y"), lax.axis_index("c")
    chips = [(1 - x, y), (x, 1 - y), (1 - x, 1 - y)]
    return x, y, c, chips


def _remote(src, dst, send, recv, to):
    return pltpu.make_async_remote_copy(src_ref=src, dst_ref=dst, send_sem=send, recv_sem=recv,
                                        device_id=to, device_id_type=MESH)


def _ex_gather(shards):
    nw = len(shards)
    hrs = [s.shape[0] // 2 for s in shards]

    def copies(ins, outs, sems):
        s1, r1, s2, r2, fs, fr = sems
        x, y, c, _ = _place()
        me, xn, yn, dg = (x, y), (1 - x, y), (x, 1 - y), (1 - x, 1 - y)
        nbr = (xn, yn)
        sibling = (x, y, 1 - c)

        def piece(w, chip, core, part=None):
            hr = hrs[w]
            rows = pl.ds(core * hr, hr) if part is None else pl.ds(core * hr + part * (hr // 2), hr // 2)
            return outs[w].at[2 * chip[0] + chip[1], rows]

        def first(w, k):
            return _remote(ins[w].at[pl.ds(c * hrs[w], hrs[w])], piece(w, me, c), s1.at[w, k], r1.at[w, k],
                           (*nbr[k], c))

        def landed(w, k):
            return _remote(piece(w, nbr[k], c), piece(w, nbr[k], c), s1.at[w, k], r1.at[w, k], (*nbr[k], c))

        def onward(w, k):
            return _remote(piece(w, nbr[k], c, k), piece(w, nbr[k], c, k), s2.at[w, k], r2.at[w, k],
                           (*nbr[1 - k], c))

        def arrived(w, k):
            return _remote(piece(w, dg, c, k), piece(w, dg, c, k), s2.at[w, k], r2.at[w, k], (*nbr[1 - k], c))

        def passed(w, j):
            chip = (xn, yn, dg)[j]
            return _remote(piece(w, chip, c), piece(w, chip, c), fs.at[w, j], fr.at[w, j], sibling)

        def handed(w, j):
            chip = (xn, yn, dg)[j]
            return _remote(piece(w, chip, 1 - c), piece(w, chip, 1 - c), fs.at[w, j], fr.at[w, j], sibling)

        return first, landed, onward, arrived, passed, handed

    def start(ins, outs, sems):
        first = copies(ins, outs, sems)[0]
        for w in range(nw):
            for k in range(2):
                first(w, k).start()

    def middle(ins, outs, sems):
        _, landed, onward, _, passed, _ = copies(ins, outs, sems)
        for w in range(nw):
            for k in range(2):
                landed(w, k).wait_recv()
                onward(w, k).start()
                passed(w, k).start()

    def finish(ins, outs, sems):
        first, _, onward, arrived, passed, handed = copies(ins, outs, sems)
        for w in range(nw):
            for k in range(2):
                arrived(w, k).wait_recv()
            passed(w, 2).start()
        for w in range(nw):
            for j in range(3):
                handed(w, j).wait_recv()
        for w in range(nw):
            for k in range(2):
                first(w, k).wait_send()
                onward(w, k).wait_send()
            for j in range(3):
                passed(w, j).wait_send()

    return _Exchange(shards, [_sds((N_CHIPS,) + s.shape, s.dtype) for s in shards],
                     [pltpu.SemaphoreType.DMA((nw, 2))] * 4 + [pltpu.SemaphoreType.DMA((nw, 3))] * 2,
                     start, finish, middle=middle)


def _ex_pair(grads):
    nw = len(grads)

    def copies(ins, outs, sems):
        x, y, c, _ = _place()
        out = []
        for w in range(nw):
            hr = grads[w].shape[1] // 2
            out.append(_remote(ins[w].at[:, pl.ds((1 - c) * hr, hr)], outs[w], sems[0].at[w], sems[1].at[w],
                               (x, y, 1 - c)))
        return out

    def start(ins, outs, sems):
        for cp in copies(ins, outs, sems):
            cp.start()

    def finish(ins, outs, sems):
        for cp in copies(ins, outs, sems):
            cp.wait()

    return _Exchange(grads, [_sds((N_CHIPS, g.shape[1] // 2, g.shape[2]), F32) for g in grads],
                     [pltpu.SemaphoreType.DMA((nw,))] * 2, start, finish)


def _ex_chip(pieces):
    nw = len(pieces)

    def copies(ins, outs, sems):
        x, y, c, chips = _place()
        return [_remote(ins[w].at[2 * cx + cy], outs[w].at[k], sems[0].at[w, k], sems[1].at[w, k], (cx, cy, c))
                for w in range(nw) for k, (cx, cy) in enumerate(chips)]

    def start(ins, outs, sems):
        for cp in copies(ins, outs, sems):
            cp.start()

    def finish(ins, outs, sems):
        for cp in copies(ins, outs, sems):
            cp.wait()

    return _Exchange(pieces, [_sds((3,) + p.shape[1:], BF16) for p in pieces],
                     [pltpu.SemaphoreType.DMA((nw, 3))] * 2, start, finish)


def _ex_swap(fulls):
    nw = len(fulls)

    def start(ins, outs, sems):
        x, y, c, _ = _place()
        for w in range(nw):
            hr = fulls[w].shape[0] // 2
            mine = pl.ds(c * hr, hr)
            _remote(ins[w].at[mine], outs[w].at[mine], sems[0].at[w], sems[1].at[w], (x, y, 1 - c)).start()

    def finish(ins, outs, sems):
        x, y, c, _ = _place()
        for w in range(nw):
            hr = fulls[w].shape[0] // 2
            mine, theirs = pl.ds(c * hr, hr), pl.ds((1 - c) * hr, hr)
            _remote(ins[w].at[mine], outs[w].at[mine], sems[0].at[w], sems[1].at[w], (x, y, 1 - c)).wait_send()
            _remote(ins[w].at[theirs], outs[w].at[theirs], sems[0].at[w], sems[1].at[w], (x, y, 1 - c)).wait_recv()

    return _Exchange(fulls, [_sds(f.shape, F32) for f in fulls], [pltpu.SemaphoreType.DMA((nw,))] * 2,
                     start, finish, aliases={w: w for w in range(nw)})


def _ex_allgather(blocks):
    nb = len(blocks)

    def copies(ins, outs, sems):
        send, recv, lsem = sems
        x, y, c, chips = _place()
        me, sibling = (x, y, c), (x, y, 1 - c)

        def rows(b, px, py, pc):
            m_per = blocks[b].shape[0]
            return outs[b].at[pl.ds((4 * px + 2 * py + pc) * m_per, m_per), :]

        def copy(b, k, blk, to, src=None):
            return _remote(rows(b, *blk) if src is None else src, rows(b, *blk), send.at[b, k], recv.at[b, k], to)

        def mine(b):
            return pltpu.make_async_copy(ins[b], rows(b, *me), lsem.at[b])

        def first(b, k):
            return copy(b, k, me, sibling if k == 0 else (*chips[k - 1], c), src=ins[b])

        def passed(b, j):
            return copy(b, 4 + j, (*chips[j], c), sibling)

        def landed(b, j):
            return copy(b, 1 + j, (*chips[j], c), me)

        def handed(b, k):
            return copy(b, 0, sibling, me) if k == 0 else copy(b, 3 + k, (*chips[k - 1], 1 - c), me)

        return mine, first, passed, landed, handed

    def start(ins, outs, sems):
        mine, first, _, _, _ = copies(ins, outs, sems)
        for b in range(nb):
            mine(b).start()
            for k in range(4):
                first(b, k).start()

    def finish(ins, outs, sems):
        mine, first, passed, landed, handed = copies(ins, outs, sems)
        sent = []
        for b in range(nb):
            for j in range(3):
                landed(b, j).wait_recv()
                cp = passed(b, j)
                cp.start()
                sent.append(cp)
        for b in range(nb):
            for k in range(4):
                handed(b, k).wait_recv()
            for k in range(4):
                first(b, k).wait_send()
        for cp in sent:
            cp.wait_send()
        for b in range(nb):
            mine(b).wait()

    return _Exchange(blocks, [_sds((N_DEV * b.shape[0], b.shape[1]), F32) for b in blocks],
                     [pltpu.SemaphoreType.DMA((nb, 7)), pltpu.SemaphoreType.DMA((nb, 7)), pltpu.SemaphoreType.DMA((nb,))],
                     start, finish)


def _inproj(x2, g1, w_in_t, b_in, tabs, seq, exchanges=()):
    T = x2.shape[0]
    tm = min(TM, seq)
    nseq = seq // tm

    def body(x_ref, g_ref, w_ref, b_ref, c_ref, a_ref, bt_ref, h_ref, u_ref, q_ref, k_ref, v_ref, gate_ref):
        x = x_ref[...]
        h = (x * _rms(x) * g_ref[...]).astype(BF16)
        h_ref[...] = h

        def proj(lo, hi):
            return _dot_nt(h, w_ref[lo:hi, :]) + b_ref[:, lo:hi]

        c, a, bt = c_ref[...], a_ref[...], bt_ref[...]
        u_ref[...] = proj(0, C_Q)
        q = proj(C_Q, C_K)
        for p in range(4):
            sl = slice(LANES * p, LANES * (p + 1))
            q_ref[:, sl] = (_rot_fwd(q[:, sl], c, a, bt) * SCALE).astype(BF16)
        kv = proj(C_K, C_G)
        k_ref[...] = _rot_fwd(kv[:, :KV_WIDTH], c, a, bt).astype(BF16)
        v_ref[...] = kv[:, KV_WIDTH:].astype(BF16)
        for j in range(2):
            lo = C_G + D_MODEL * j
            gate_ref[:, D_MODEL * j:D_MODEL * (j + 1)] = jax.nn.sigmoid(proj(lo, lo + D_MODEL)).astype(BF16)

    tab = pl.BlockSpec((tm, LANES), lambda i: (i % nseq, 0))
    return _call(
        body, name="inproj", grid=(T // tm,),
        in_specs=[_rows(tm, D_MODEL), _const((1, D_MODEL)), _const((IN_WIDTH, D_MODEL)), _const((1, IN_WIDTH)),
                  tab, tab, tab],
        out_specs=[_rows(tm, D_MODEL), _rows(tm, POOL_WIDTH), _rows(tm, ATTN_WIDTH), _rows(tm, KV_WIDTH),
                   _rows(tm, KV_WIDTH), _rows(tm, GATE_WIDTH)],
        out_shape=[_sds((T, D_MODEL), BF16), _sds((T, POOL_WIDTH), F32), _sds((T, ATTN_WIDTH), BF16),
                   _sds((T, KV_WIDTH), BF16), _sds((T, KV_WIDTH), BF16), _sds((T, GATE_WIDTH), BF16)],
        args=(x2, g1, w_in_t, b_in, *tabs), sem=("parallel",), exchanges=exchanges)


def _inv_count(pos, w):
    return 1.0 / jnp.minimum(pos + 1, w).astype(F32)


def _pool_tile(i, tp, nseq, u_ref, prev_ref, w_ref, s_ref, diff_ref, y_ref):
    first = (i % nseq) == 0
    prev = jnp.where(first, 0.0, prev_ref[...])
    ext = jnp.concatenate([prev, u_ref[...]], axis=0)
    pos = (i % nseq) * tp + lax.broadcasted_iota(jnp.int32, (tp, 1), 0)
    for gi, w in enumerate(POOL_WINDOWS):
        sl = slice(POOL_GC * gi, POOL_GC * (gi + 1))
        xg = ext[:, sl]
        s = xg
        sh = 1
        while sh < w:
            s = s + pltpu.roll(s, sh, 0)
            sh *= 2
        pooled = s[HALO:] * _inv_count(pos, w)
        diff = (pooled - xg[HALO:]).astype(BF16)
        diff_ref[:, sl] = diff
        mixed = _dot(diff, w_ref[gi].astype(BF16))
        y_ref[:, sl] = (mixed * s_ref[:, sl]).astype(BF16)


def _pool_specs(tp):
    per = tp // HALO
    return [_rows(tp, POOL_WIDTH), pl.BlockSpec((HALO, POOL_WIDTH), lambda i: (jnp.maximum(i * per - 1, 0), 0)),
            _const((4, POOL_GC, POOL_GC)), _const((1, POOL_WIDTH))]


GROUP = 4
GROWS = GROUP * BLOCK


def _attn_masks(n):
    qi = lax.broadcasted_iota(jnp.int32, (GROWS, 2 * BLOCK), 0) % BLOCK
    kj = lax.broadcasted_iota(jnp.int32, (GROWS, 2 * BLOCK), 1)
    rel = qi + BLOCK - kj
    valid = (rel >= 0) & (rel < BLOCK) & (kj >= jnp.where(n > 0, 0, BLOCK))
    lo = lax.broadcasted_iota(jnp.int32, (BLOCK, LANES), 1) < HEAD_DIM
    return valid, lo


def _by_example(bl, *arrays):
    return [a.reshape(bl, a.shape[0] // bl, a.shape[1]) for a in arrays]


def _stack_heads(ref, h, lo):
    keep = lo if h == 0 else jnp.logical_not(lo)
    pieces = []
    for p in (2 * h, 2 * h + 1):
        xp = ref[:, LANES * p:LANES * (p + 1)].astype(F32)
        for e in range(2):
            t = xp if e == h else pltpu.roll(xp, HEAD_DIM, 1)
            pieces.append(jnp.where(keep, t, 0.0).astype(BF16))
    return jnp.concatenate(pieces, axis=0)


def _unstack_heads(stacked, h, lo):
    pairs = []
    for j in range(2):
        parts = []
        for e in range(2):
            t = stacked[BLOCK * (2 * j + e):BLOCK * (2 * j + e + 1)]
            parts.append(t if e == h else pltpu.roll(t, HEAD_DIM, 1))
        pairs.append(jnp.where(lo, parts[0], parts[1]))
    return pairs


def _sink_rows(sink_ref, h):
    head = lax.broadcasted_iota(jnp.int32, (GROWS, 1), 0) // BLOCK
    col = jnp.zeros((GROWS, 1), F32) + sink_ref[GROUP * h]
    for g in range(1, GROUP):
        col = jnp.where(head == g, sink_ref[GROUP * h + g], col)
    return col


def _group_probs(qs, kk, valid, sink):
    s = jnp.where(valid, _dot_nt(qs, kk), NEG_INF)
    m = jnp.maximum(jnp.max(s, axis=1, keepdims=True), sink)
    ex = jnp.exp(s - m)
    es = jnp.exp(sink - m)
    inv = 1.0 / (jnp.sum(ex, axis=1, keepdims=True) + es)
    return ex * inv, es * inv


def _mixers_fwd(q, k, v, sinks, u, w_pool, pool_scale, seq, exchanges=()):
    T = q.shape[0]
    nb = seq // BLOCK
    bl = T // seq
    tp = T // nb
    nseq = seq // tp

    def body(sink_ref, q_ref, kp_ref, kc_ref, vp_ref, vc_ref, u_ref, prev_ref, w_ref, s_ref, o_ref, diff_ref, y_ref):
        n = pl.program_id(0)
        valid, lo = _attn_masks(n)
        for b in range(bl):
            kk = jnp.concatenate([kp_ref[b], kc_ref[b]], axis=0)
            vv = jnp.concatenate([vp_ref[b], vc_ref[b]], axis=0)
            for h in range(2):
                qs = _stack_heads(q_ref.at[b], h, lo)
                pr, _ = _group_probs(qs, kk, valid, _sink_rows(sink_ref, h))
                o = _dot(pr.astype(BF16), vv)
                for j, pair in enumerate(_unstack_heads(o, h, lo)):
                    p = 2 * h + j
                    o_ref[b, :, LANES * p:LANES * (p + 1)] = pair.astype(BF16)
        _pool_tile(n, tp, nseq, u_ref, prev_ref, w_ref, s_ref, diff_ref, y_ref)

    cur = lambda n: (0, n, 0)
    prv = lambda n: (0, jnp.maximum(n - 1, 0), 0)
    kv = lambda m: pl.BlockSpec((bl, BLOCK, KV_WIDTH), m)
    res = _call(
        body, name="mixers_fwd", grid=(nb,),
        in_specs=[pl.BlockSpec(memory_space=pltpu.SMEM), pl.BlockSpec((bl, BLOCK, ATTN_WIDTH), cur),
                  kv(prv), kv(cur), kv(prv), kv(cur)] + _pool_specs(tp),
        out_specs=[pl.BlockSpec((bl, BLOCK, ATTN_WIDTH), cur), _rows(tp, POOL_WIDTH), _rows(tp, POOL_WIDTH)],
        out_shape=[_sds((bl, seq, ATTN_WIDTH), BF16), _sds((T, POOL_WIDTH), BF16), _sds((T, POOL_WIDTH), BF16)],
        args=(sinks, *_by_example(bl, q, k, k, v, v), u, u, w_pool, pool_scale), sem=("parallel",),
        exchanges=exchanges)
    outs, rest = res if exchanges else (res, None)
    return [outs[0].reshape(T, ATTN_WIDTH), outs[1], outs[2]], rest


def _branch(y, w_ref):
    return jnp.concatenate([_dot(y, w_ref[j]) for j in range(N_CHIPS)], axis=1)


def _merge_out(y_pool, y_attn, gate, x2, w_bp, w_ba, w_out, g2, g3, exchanges=()):
    T = x2.shape[0]
    tm = min(TM, T)

    def body(yp_ref, ya_ref, gate_ref, x_ref, wbp_ref, wba_ref, wo_ref, g2_ref, g3_ref,
             mg_ref, mix_ref, x1_ref, h2_ref):
        bp, ba = _branch(yp_ref[...], wbp_ref), _branch(ya_ref[...], wba_ref)
        merged = (gate_ref[:, :D_MODEL].astype(F32) * bp + gate_ref[:, D_MODEL:].astype(F32) * ba).astype(BF16)
        mg_ref[...] = merged
        mix = _dot(merged, wo_ref[...])
        mix_ref[...] = mix
        x1 = x_ref[...] + mix * _rms(mix) * g2_ref[...]
        x1_ref[...] = x1
        h2_ref[...] = (x1 * _rms(x1) * g3_ref[...]).astype(BF16)

    return _call(
        body, name="merge_out", grid=(T // tm,),
        in_specs=[_rows(tm, POOL_WIDTH), _rows(tm, ATTN_WIDTH), _rows(tm, GATE_WIDTH), _rows(tm, D_MODEL),
                  _const(w_bp.shape), _const(w_ba.shape), _const((D_MODEL, D_MODEL)),
                  _const((1, D_MODEL)), _const((1, D_MODEL))],
        out_specs=[_rows(tm, D_MODEL)] * 4,
        out_shape=[_sds((T, D_MODEL), BF16), _sds((T, D_MODEL), F32), _sds((T, D_MODEL), F32),
                   _sds((T, D_MODEL), BF16)],
        args=(y_pool, y_attn, gate, x2, w_bp, w_ba, w_out, g2, g3), sem=("parallel",), exchanges=exchanges)


HALF = D_MODEL // 2
TM_MLP = 256


def _mlp_core(h2, x1, mix, tgt, w_up, w_down, g4, g3, g2):
    T = h2.shape[0]
    tm = min(TM_MLP, T)

    def body(h_ref, x1_ref, mix_ref, t_ref, g_ref, g3_ref, g2_ref, ua_hbm, ub_hbm, da_hbm, db_hbm,
             act_ref, dff_ref, dup_ref, dx1_ref, dmix_ref, loss_ref, dg_ref, dg3_ref, dg2_ref,
             wu, wd, relu_scr, sems):
        def weight_copy(i):
            src, dst = ((ua_hbm, wu.at[:, :HALF]), (ub_hbm, wu.at[:, HALF:]),
                        (da_hbm, wd.at[:, :HALF]), (db_hbm, wd.at[:, HALF:]))[i]
            return pltpu.make_async_copy(src, dst, sems.at[i])

        @pl.when(pl.program_id(0) == 0)
        def _():
            for i in range(4):
                weight_copy(i).start()
            loss_ref[...] = jnp.zeros_like(loss_ref)
            for ref in (dg_ref, dg3_ref, dg2_ref):
                ref[...] = jnp.zeros_like(ref)
            weight_copy(0).wait()
            weight_copy(1).wait()

        h = h_ref[...]
        ff = None
        for j in range(N_CHIPS):
            lo = D_MODEL * j
            relu = jnp.maximum(_dot(h, wu[j]), 0.0)
            if j == 0:
                @pl.when(pl.program_id(0) == 0)
                def _():
                    weight_copy(2).wait()
                    weight_copy(3).wait()
            relu_scr[:, lo:lo + D_MODEL] = relu
            act = jnp.square(relu).astype(BF16)
            act_ref[:, lo:lo + D_MODEL] = act
            t = _dot(act, wd[j])
            ff = t if ff is None else ff + t
        g = g_ref[...]
        x1 = x1_ref[...]
        err = x1 + ff * _rms(ff) * g - t_ref[...]
        loss_ref[...] += jnp.sum(err * err) * (0.5 / D_MODEL)
        dy = err * (1.0 / D_MODEL)
        dff, dg = _norm_bwd(ff, g, dy)
        dg_ref[...] += dg
        dff = dff.astype(BF16)
        dff_ref[...] = dff
        dh2 = None
        for j in range(N_CHIPS):
            lo = D_MODEL * j
            dup = (_dot_nt(dff, wd[j]) * (2.0 * relu_scr[:, lo:lo + D_MODEL])).astype(BF16)
            dup_ref[:, lo:lo + D_MODEL] = dup
            t = _dot_nt(dup, wu[j])
            dh2 = t if dh2 is None else dh2 + t
        dx, dg3 = _norm_bwd(x1, g3_ref[...], dh2)
        dx1 = dy + dx
        dx1_ref[...] = dx1
        dg3_ref[...] += dg3
        dmix, dg2 = _norm_bwd(mix_ref[...], g2_ref[...], dx1)
        dmix_ref[...] = dmix.astype(BF16)
        dg2_ref[...] += dg2

    slabs = pltpu.VMEM((N_CHIPS, D_MODEL, D_MODEL), BF16)
    gain = _const((1, D_MODEL))
    return pl.pallas_call(
        body, name="mlp_core", grid=(T // tm,),
        in_specs=[_rows(tm, D_MODEL)] * 4 + [gain] * 3 + [ANY] * 4,
        out_specs=[_rows(tm, D_FF), _rows(tm, D_MODEL), _rows(tm, D_FF), _rows(tm, D_MODEL), _rows(tm, D_MODEL),
                   _const((8, LANES)), gain, gain, gain],
        out_shape=[_sds((T, D_FF), BF16), _sds((T, D_MODEL), BF16), _sds((T, D_FF), BF16), _sds((T, D_MODEL), F32),
                   _sds((T, D_MODEL), BF16), _sds((8, LANES), F32)] + [_sds((1, D_MODEL), F32)] * 3,
        scratch_shapes=[slabs] * 2 + [pltpu.VMEM((tm, D_FF), F32), pltpu.SemaphoreType.DMA((4,))],
        compiler_params=_cp("arbitrary"),
    )(h2, x1, mix, tgt, g4, g3, g2, *w_up, *w_down)


def _dw(tag, a, g, ta, tn, shard_cols=False, exchanges=()):
    T, ka = a.shape
    n = g.shape[1]
    tk = min(2 * TM, T)
    nk = T // tk

    def body(a_ref, g_ref, o_ref):
        @pl.when(pl.program_id(2) == 0)
        def _():
            o_ref[...] = jnp.zeros_like(o_ref)

        o_ref[...] += _dot_tn(a_ref[...], g_ref[...])

    if shard_cols:
        per = (n // N_CHIPS) // tn
        out_spec = pl.BlockSpec((None, ta, tn), lambda i, j, k: (j // per, i, j % per))
        out_shape = _sds((N_CHIPS, ka, n // N_CHIPS), F32)
    else:
        out_spec = pl.BlockSpec((ta, tn), lambda i, j, k: (i, j))
        out_shape = _sds((ka, n), F32)
    return _call(
        body, name="dw_" + tag, grid=(ka // ta, n // tn, nk),
        in_specs=[pl.BlockSpec((tk, ta), lambda i, j, k: (k, i)), pl.BlockSpec((tk, tn), lambda i, j, k: (k, j))],
        out_specs=[out_spec], out_shape=[out_shape],
        args=(a, g), sem=("parallel", "parallel", "arbitrary"), exchanges=exchanges)


def _dw_mix(merged, dmix, y_pool, dbp, y_attn, dba, exchanges=()):
    T = merged.shape[0]
    tk = min(2 * TM, T)
    c = D_MODEL // N_CHIPS

    def body(mg_ref, dmix_ref, yp_ref, dbp_ref, ya_ref, dba_ref, out_ref, bp_ref, ba_ref):
        @pl.when(pl.program_id(0) == 0)
        def _():
            for ref in (out_ref, bp_ref, ba_ref):
                ref[...] = jnp.zeros_like(ref)

        out_ref[...] += _dot_tn(mg_ref[...], dmix_ref[...])
        for y_ref, d_ref, o_ref in ((yp_ref, dbp_ref, bp_ref), (ya_ref, dba_ref, ba_ref)):
            res = _dot_tn(y_ref[...], d_ref[...])
            for j in range(N_CHIPS):
                o_ref[j] += res[:, c * j:c * (j + 1)]

    slabs = (N_CHIPS, POOL_WIDTH, c)
    return _call(
        body, name="dw_mix", grid=(T // tk,),
        in_specs=[_rows(tk, D_MODEL), _rows(tk, D_MODEL), _rows(tk, POOL_WIDTH), _rows(tk, D_MODEL),
                  _rows(tk, ATTN_WIDTH), _rows(tk, D_MODEL)],
        out_specs=[_const((D_MODEL, D_MODEL)), _const(slabs), _const(slabs)],
        out_shape=[_sds((D_MODEL, D_MODEL), F32), _sds(slabs, F32), _sds(slabs, F32)],
        args=(merged, dmix, y_pool, dbp, y_attn, dba), sem=("arbitrary",), exchanges=exchanges)


def _merge_bwd(dmix, gate, y_pool, y_attn, w_out, w_bp, w_ba, exchanges=()):
    T = dmix.shape[0]
    tm = min(TM, T)

    def body(dmix_ref, gate_ref, yp_ref, ya_ref, wo_ref, wbp_ref, wba_ref,
             dbp_ref, dba_ref, dgate_ref, dyp_ref, dya_ref):
        dm = _dot_nt(dmix_ref[...], wo_ref[...])
        for j, (y_ref, db_ref, w_ref, dy_ref) in enumerate(
                ((yp_ref, dbp_ref, wbp_ref, dyp_ref), (ya_ref, dba_ref, wba_ref, dya_ref))):
            sl = slice(D_MODEL * j, D_MODEL * (j + 1))
            gt = gate_ref[:, sl].astype(F32)
            db = (dm * gt).astype(BF16)
            db_ref[...] = db
            dgate_ref[:, sl] = (dm * _branch(y_ref[...], w_ref) * gt * (1.0 - gt)).astype(BF16)
            cw = D_MODEL // N_CHIPS
            dy = _dot_nt(db[:, :cw], w_ref[0])
            for c in range(1, N_CHIPS):
                dy = dy + _dot_nt(db[:, cw * c:cw * (c + 1)], w_ref[c])
            dy_ref[...] = dy.astype(dy_ref.dtype)

    return _call(
        body, name="merge_bwd", grid=(T // tm,),
        in_specs=[_rows(tm, D_MODEL), _rows(tm, GATE_WIDTH), _rows(tm, POOL_WIDTH), _rows(tm, ATTN_WIDTH),
                  _const((D_MODEL, D_MODEL)), _const(w_bp.shape), _const(w_ba.shape)],
        out_specs=[_rows(tm, D_MODEL), _rows(tm, D_MODEL), _rows(tm, GATE_WIDTH), _rows(tm, POOL_WIDTH),
                   _rows(tm, ATTN_WIDTH)],
        out_shape=[_sds((T, D_MODEL), BF16), _sds((T, D_MODEL), BF16), _sds((T, GATE_WIDTH), BF16),
                   _sds((T, POOL_WIDTH), F32), _sds((T, ATTN_WIDTH), BF16)],
        args=(dmix, gate, y_pool, y_attn, w_out, w_bp, w_ba), sem=("parallel",), exchanges=exchanges)


def _attn_bwd(q, k, v, do, sinks, tabs, seq, exchanges=()):
    T = q.shape[0]
    nb = seq // BLOCK
    bl = T // seq
    steps = nb + 1

    def body(sink_ref, q_ref, do_ref, kp_ref, kc_ref, vp_ref, vc_ref, c_ref, a_ref, bt_ref, cp_ref, ap_ref, btp_ref,
             dq_ref, dk_ref, dv_ref, dsink_ref, ck_ref, cv_ref):
        n = pl.program_id(0)

        @pl.when(n == 0)
        def _():
            dsink_ref[...] = jnp.zeros_like(dsink_ref)
            ck_ref[...] = jnp.zeros_like(ck_ref)
            cv_ref[...] = jnp.zeros_like(cv_ref)

        @pl.when(n < nb)
        def _():
            valid, lo = _attn_masks(n)
            for b in range(bl):
                kk = jnp.concatenate([kp_ref[b], kc_ref[b]], axis=0)
                vv = jnp.concatenate([vp_ref[b], vc_ref[b]], axis=0)
                dk_acc = jnp.zeros((2 * BLOCK, KV_WIDTH), F32)
                dv_acc = jnp.zeros((2 * BLOCK, KV_WIDTH), F32)
                for h in range(2):
                    qs = _stack_heads(q_ref.at[b], h, lo)
                    dos = _stack_heads(do_ref.at[b], h, lo)
                    pr, ps = _group_probs(qs, kk, valid, _sink_rows(sink_ref, h))
                    dp = _dot_nt(dos, vv)
                    delta = jnp.sum(pr * dp, axis=1, keepdims=True)
                    ds = (pr * (dp - delta)).astype(BF16)
                    dsk = ps * delta
                    for g in range(GROUP):
                        idx = GROUP * h + g
                        dsink_ref[idx:idx + 1, :] += (jnp.zeros((1, LANES), F32)
                                                      - jnp.sum(dsk[BLOCK * g:BLOCK * (g + 1)]))
                    dk_acc = dk_acc + _dot_tn(ds, qs)
                    dv_acc = dv_acc + _dot_tn(pr.astype(BF16), dos)
                    for j, pair in enumerate(_unstack_heads(_dot(ds, kk) * SCALE, h, lo)):
                        sl = slice(LANES * (2 * h + j), LANES * (2 * h + j + 1))
                        dq_ref[b, :, sl] = _rot_bwd(pair, c_ref[...], a_ref[...], bt_ref[...]).astype(BF16)
                fin_k = ck_ref[b] + dk_acc[:BLOCK]
                dk_ref[b] = _rot_bwd(fin_k, cp_ref[...], ap_ref[...], btp_ref[...]).astype(BF16)
                dv_ref[b] = (cv_ref[b] + dv_acc[:BLOCK]).astype(BF16)
                ck_ref[b] = dk_acc[BLOCK:]
                cv_ref[b] = dv_acc[BLOCK:]

        @pl.when(n == nb)
        def _():
            for b in range(bl):
                dk_ref[b] = _rot_bwd(ck_ref[b], cp_ref[...], ap_ref[...], btp_ref[...]).astype(BF16)
                dv_ref[b] = cv_ref[b].astype(BF16)

    cur = lambda n: (0, jnp.minimum(n, nb - 1), 0)
    prv = lambda n: (0, jnp.clip(n - 1, 0, nb - 1), 0)
    tcur = lambda n: (jnp.minimum(n, nb - 1), 0)
    tprv = lambda n: (jnp.clip(n - 1, 0, nb - 1), 0)
    wide = lambda m: pl.BlockSpec((bl, BLOCK, ATTN_WIDTH), m)
    kv = lambda m: pl.BlockSpec((bl, BLOCK, KV_WIDTH), m)
    tab = lambda m: pl.BlockSpec((BLOCK, LANES), m)
    res = _call(
        body, name="attn_bwd", grid=(steps,),
        in_specs=[pl.BlockSpec(memory_space=pltpu.SMEM), wide(cur), wide(cur), kv(prv), kv(cur), kv(prv), kv(cur),
                  tab(tcur), tab(tcur), tab(tcur), tab(tprv), tab(tprv), tab(tprv)],
        out_specs=[wide(cur), kv(prv), kv(prv), _const((8, LANES))],
        out_shape=[_sds((bl, seq, ATTN_WIDTH), BF16), _sds((bl, seq, KV_WIDTH), BF16),
                   _sds((bl, seq, KV_WIDTH), BF16), _sds((8, LANES), F32)],
        scratch=[pltpu.VMEM((bl, BLOCK, KV_WIDTH), F32), pltpu.VMEM((bl, BLOCK, KV_WIDTH), F32)],
        args=(sinks, *_by_example(bl, q, do, k, k, v, v), *tabs, *tabs), sem=("arbitrary",), exchanges=exchanges)
    outs, rest = (res if exchanges else (res, None))
    outs = [outs[0].reshape(T, ATTN_WIDTH), outs[1].reshape(T, KV_WIDTH), outs[2].reshape(T, KV_WIDTH), outs[3]]
    return (outs, rest) if exchanges else outs


def _pool_bwd(dyp, diff, w_pool, pool_scale, seq, exchanges=()):
    T = dyp.shape[0]
    tp = min(TP, seq)
    nseq = seq // tp
    per = tp // HALO
    last_halo = T // HALO - 1

    def body(dy_ref, nxt_ref, diff_ref, w_ref, s_ref, du_ref, dw_ref, ds_ref):
        i = pl.program_id(0)

        @pl.when(i == 0)
        def _():
            dw_ref[...] = jnp.zeros_like(dw_ref)
            ds_ref[...] = jnp.zeros_like(ds_ref)

        last = (i % nseq) == nseq - 1
        nxt = jnp.where(last, 0.0, nxt_ref[...])
        ext = jnp.concatenate([dy_ref[...], nxt], axis=0) * s_ref[...]
        pos = (i % nseq) * tp + lax.broadcasted_iota(jnp.int32, (tp + HALO, 1), 0)
        for gi, w in enumerate(POOL_WINDOWS):
            sl = slice(POOL_GC * gi, POOL_GC * (gi + 1))
            wg = w_ref[gi].astype(BF16)
            dmx = ext[:, sl].astype(BF16)
            ddiff = _dot_nt(dmx, wg)
            s = ddiff * _inv_count(pos, w)
            sh = 1
            while sh < w:
                s = s + pltpu.roll(s, tp + HALO - sh, 0)
                sh *= 2
            du_ref[:, sl] = (s[:tp] - ddiff[:tp]).astype(BF16)
            dg = diff_ref[:, sl]
            dw_ref[gi] += _dot_tn(dg, dmx[:tp])
            ds_ref[:, sl] += jnp.sum(dy_ref[:, sl] * _dot(dg, wg), axis=0, keepdims=True)

    return _call(
        body, name="pool_bwd", grid=(T // tp,),
        in_specs=[_rows(tp, POOL_WIDTH),
                  pl.BlockSpec((HALO, POOL_WIDTH), lambda i: (jnp.minimum((i + 1) * per, last_halo), 0)),
                  _rows(tp, POOL_WIDTH), _const((4, POOL_GC, POOL_GC)), _const((1, POOL_WIDTH))],
        out_specs=[_rows(tp, POOL_WIDTH), _const((4, POOL_GC, POOL_GC)), _const((1, POOL_WIDTH))],
        out_shape=[_sds((T, POOL_WIDTH), BF16), _sds((4, POOL_GC, POOL_GC), F32), _sds((1, POOL_WIDTH), F32)],
        args=(dyp, dyp, diff, w_pool, pool_scale), sem=("arbitrary",), exchanges=exchanges)


_PARTS = ((0, C_Q), (C_Q, C_K), (C_K, C_V), (C_V, C_G), (C_G, IN_WIDTH))


def _inproj_bwd(parts, x2, dx1, w_in_t, g1, exchanges=()):
    T = x2.shape[0]
    tm = min(TM, T)

    def body(du_ref, dq_ref, dk_ref, dv_ref, dgt_ref, x_ref, dx1_ref, w_ref, g_ref, gx_ref, dg_ref):
        @pl.when(pl.program_id(0) == 0)
        def _():
            dg_ref[...] = jnp.zeros_like(dg_ref)

        dh = jnp.zeros((tm, D_MODEL), F32)
        for (lo, hi), p_ref in zip(_PARTS, (du_ref, dq_ref, dk_ref, dv_ref, dgt_ref)):
            dh = dh + _dot(p_ref[...], w_ref[lo:hi, :])
        dx, dg = _norm_bwd(x_ref[...], g_ref[...], dh)
        gx_ref[...] = dx1_ref[...] + dx
        dg_ref[...] += dg

    return _call(
        body, name="inproj_bwd", grid=(T // tm,),
        in_specs=[_rows(tm, hi - lo) for lo, hi in _PARTS]
        + [_rows(tm, D_MODEL), _rows(tm, D_MODEL), _const((IN_WIDTH, D_MODEL)), _const((1, D_MODEL))],
        out_specs=[_rows(tm, D_MODEL), _const((1, D_MODEL))],
        out_shape=[_sds((T, D_MODEL), F32), _sds((1, D_MODEL), F32)],
        args=(*parts, x2, dx1, w_in_t, g1), sem=("arbitrary",), exchanges=exchanges)


def _dw_in(h, parts, exchanges=()):
    T = h.shape[0]
    tk = min(TM, T)

    def body(h_ref, du_ref, dq_ref, dk_ref, dv_ref, dgt_ref, o_ref, db_ref):
        @pl.when(pl.program_id(0) == 0)
        def _():
            o_ref[...] = jnp.zeros_like(o_ref)
            db_ref[...] = jnp.zeros_like(db_ref)

        hh = h_ref[...]
        for (lo, hi), p_ref in zip(_PARTS, (du_ref, dq_ref, dk_ref, dv_ref, dgt_ref)):
            part = p_ref[...]
            o_ref[lo:hi, :] += _dot_tn(part, hh)
            db_ref[:, lo:hi] += jnp.sum(part.astype(F32), axis=0, keepdims=True)

    return _call(
        body, name="dw_in", grid=(T // tk,),
        in_specs=[_rows(tk, D_MODEL)] + [_rows(tk, hi - lo) for lo, hi in _PARTS],
        out_specs=[_const((IN_WIDTH, D_MODEL)), _const((1, IN_WIDTH))],
        out_shape=[_sds((IN_WIDTH, D_MODEL), F32), _sds((1, IN_WIDTH), F32)],
        args=(h, *parts), sem=("arbitrary",), exchanges=exchanges)


def _row_tile(rows, cap=256, mult=16):
    best = None
    for t in range(mult, min(rows, cap) + 1, mult):
        if rows % t == 0:
            best = t
    if best is None:
        raise ValueError("no row tile for %d rows" % rows)
    return best


def _pair_sum(ids, full, got):
    _, r, c = full.shape
    hr = r // 2
    tr = _row_tile(hr)
    nblk = hr // tr

    def body(ids_ref, a_ref, b_ref, own_ref, sb_ref):
        s = a_ref[...] + b_ref[...]
        sb_ref[...] = s.astype(BF16)

        @pl.when(pl.program_id(1) == ids_ref[0])
        def _():
            own_ref[...] = s

    slab = pl.BlockSpec((None, tr, c), lambda i, j, ids_ref: (j, i, 0))
    return pl.pallas_call(
        body, name="pair_sum_%dx%d" % (r, c),
        grid_spec=pltpu.PrefetchScalarGridSpec(
            num_scalar_prefetch=1, grid=(nblk, N_CHIPS),
            in_specs=[pl.BlockSpec((None, tr, c), lambda i, j, ids_ref: (j, ids_ref[1] * nblk + i, 0)), slab],
            out_specs=[pl.BlockSpec((tr, c), lambda i, j, ids_ref: (i, 0)), slab]),
        out_shape=[_sds((hr, c), F32), _sds((N_CHIPS, hr, c), BF16)],
        compiler_params=_cp("parallel", "arbitrary"),
    )(ids, full, got)


def _chip_sum(ids, own, got):
    hr, c = own.shape
    tr = _row_tile(hr)
    nblk = hr // tr

    def body(ids_ref, a_ref, b_ref, o_ref):
        o_ref[...] = ((a_ref[...] + b_ref[0].astype(F32)) + b_ref[1].astype(F32)) + b_ref[2].astype(F32)

    return pl.pallas_call(
        body, name="chip_sum_%dx%d" % (hr, c),
        grid_spec=pltpu.PrefetchScalarGridSpec(
            num_scalar_prefetch=1, grid=(nblk,),
            in_specs=[pl.BlockSpec((tr, c), lambda i, ids_ref: (i, 0)),
                      pl.BlockSpec((3, tr, c), lambda i, ids_ref: (0, i, 0))],
            out_specs=pl.BlockSpec((tr, c), lambda i, ids_ref: (ids_ref[1] * nblk + i, 0))),
        out_shape=_sds((2 * hr, c), F32),
        compiler_params=_cp("parallel"),
    )(ids, own, got)


def _adamw_math(w, g, m, v):
    nm = ADAM_B1 * m + (1.0 - ADAM_B1) * g
    nv = ADAM_B2 * v + (1.0 - ADAM_B2) * (g * g)
    m_hat = nm / (1.0 - ADAM_B1 ** ADAM_STEP)
    v_hat = nv / (1.0 - ADAM_B2 ** ADAM_STEP)
    return -ADAM_LR * (m_hat / (jnp.sqrt(v_hat) + ADAM_EPS) + ADAM_WD * w), nm, nv


def _adamw(w, g, m, v):
    r, c = w.shape
    tr = _row_tile(r, cap=512, mult=8)

    def body(w_ref, g_ref, m_ref, v_ref, d_ref, nm_ref, nv_ref):
        d_ref[...], nm_ref[...], nv_ref[...] = _adamw_math(w_ref[...], g_ref[...], m_ref[...], v_ref[...])

    spec = _rows(tr, c)
    return pl.pallas_call(
        body, name="adamw_%dx%d" % (r, c), grid=(r // tr,),
        in_specs=[spec] * 4, out_specs=[spec] * 3, out_shape=[_sds((r, c), F32)] * 3,
        compiler_params=_cp("parallel"),
    )(w, g, m, v)


SC_TILES = 32
SC_LANES = 16
SC_ROWS = 8


def _adamw_sparse(w, g, m, v):
    r, c = w.shape
    rows = r // SC_TILES
    step = min(rows, SC_ROWS)

    def body(w_hbm, g_hbm, m_hbm, v_hbm, d_hbm, nm_hbm, nv_hbm, wb, gb, mb, vb):
        tile = lax.axis_index("sc_subcore") * 2 + lax.axis_index("sc_core")

        @pl.loop(0, rows, step=step)
        def _(r0):
            mine = pl.ds(tile * rows + r0, step)
            for src, dst in ((w_hbm, wb), (g_hbm, gb), (m_hbm, mb), (v_hbm, vb)):
                pltpu.sync_copy(src.at[mine], dst)

            @pl.loop(0, step)
            def _(row):
                @pl.loop(0, c, step=SC_LANES)
                def _(i):
                    at = (row, pl.ds(i, SC_LANES))
                    wb[at], mb[at], vb[at] = _adamw_math(wb[at], gb[at], mb[at], vb[at])

            for src, dst in ((wb, d_hbm), (mb, nm_hbm), (vb, nv_hbm)):
                pltpu.sync_copy(src, dst.at[mine])

    return pl.kernel(
        body, name="adamw_sparse_%dx%d" % (r, c), out_type=[_sds((r, c), F32)] * 3,
        mesh=plsc.VectorSubcoreMesh(core_axis_name="sc_core", subcore_axis_name="sc_subcore"),
        scratch_types=[pltpu.VMEM((step, c), F32)] * 4,
    )(w, g, m, v)


_SMALL_NAMES = ("w_pool", "b_in", "g_mix_pre", "g_mix_post", "g_mlp_pre", "g_mlp_post", "pool_scale", "attn_sinks")
B_ROWS = -(-IN_WIDTH // D_MODEL)


def _row_block(rows):
    rows = [jnp.pad(r.astype(F32), ((0, 0), (0, D_MODEL - r.shape[1]))) for r in rows]
    return jnp.pad(jnp.concatenate(rows, axis=0), ((0, 8 - len(rows)), (0, 0)))


def _early_block(dg2, dg3, dg4, dps, dsink, loss):
    tail = jnp.concatenate([jnp.pad(dsink.reshape(1, -1), ((0, 0), (0, LANES - dsink.size))),
                            jnp.pad(loss.reshape(1, 1), ((0, 0), (0, LANES - 1)))], axis=1)
    return _row_block([dg2, dg3, dg4, dps, tail])


def _late_block(db_in, dg1):
    b = jnp.pad(db_in, ((0, 0), (0, B_ROWS * D_MODEL - IN_WIDTH))).reshape(B_ROWS, D_MODEL)
    return _row_block([b[r:r + 1] for r in range(B_ROWS)] + [dg1])


def _small_update(gearly, gmat, glate, w, m, v):
    names = _SMALL_NAMES
    n = len(names)

    def total(ref, rows):
        acc = ref[0:rows, :]
        for d in range(1, N_DEV):
            acc = acc + ref[d * rows:(d + 1) * rows, :]
        return acc

    def body(*refs):
        early_ref, gmat_ref, late_ref = refs[:3]
        w_refs, m_refs, v_refs = refs[3:3 + n], refs[3 + n:3 + 2 * n], refs[3 + 2 * n:3 + 3 * n]
        outs = refs[3 + 3 * n:]
        loss_ref, g_refs, d_refs = outs[0], outs[1:1 + n], outs[1 + n:1 + 2 * n]
        nm_refs, nv_refs = outs[1 + 2 * n:1 + 3 * n], outs[1 + 3 * n:1 + 4 * n]
        early, late = total(early_ref, 8), total(late_ref, 8)
        loss_ref[...] = jnp.sum(early[4:5, LANES:2 * LANES], axis=1, keepdims=True)
        bias = jnp.concatenate([late[r:r + 1, :] for r in range(B_ROWS - 1)]
                               + [late[B_ROWS - 1:B_ROWS, :IN_WIDTH - (B_ROWS - 1) * D_MODEL]], axis=1)
        grad = dict(b_in=bias, g_mix_pre=late[B_ROWS:B_ROWS + 1, :], g_mix_post=early[0:1, :],
                    g_mlp_pre=early[1:2, :], g_mlp_post=early[2:3, :], pool_scale=early[3:4, :POOL_WIDTH],
                    attn_sinks=early[4:5, :N_Q_HEADS])
        for i, name in enumerate(names):
            g = total(gmat_ref, 4 * POOL_GC) if name == "w_pool" else grad[name]
            g_refs[i][...] = g
            d_refs[i][...], nm_refs[i][...], nv_refs[i][...] = _adamw_math(
                w_refs[i][...], g, m_refs[i][...], v_refs[i][...])

    shapes = [_sds(w[k].shape, F32) for k in names]
    res = pl.pallas_call(
        body, name="small_update", out_shape=[_sds((1, 1), F32)] + shapes * 4,
        compiler_params=pltpu.CompilerParams(vmem_limit_bytes=VMEM_MB * 1024 * 1024),
    )(gearly, gmat, glate, *[w[k] for k in names], *[m[k] for k in names], *[v[k] for k in names])
    loss = res[0]
    per = {k: tuple(res[1 + j * n + i] for j in range(4)) for i, k in enumerate(names)}
    return loss, per


_BIG = ("w_in", "w_branch_pool", "w_branch_attn", "w_out", "w_up", "w_down")
_ORDER = ("g_mix_pre", "w_in", "b_in", "w_pool", "pool_scale", "attn_sinks", "w_branch_pool", "w_branch_attn",
          "w_out", "g_mix_post", "g_mlp_pre", "w_up", "w_down", "g_mlp_post")


def _stack_rows(slab):
    return slab.reshape(-1, slab.shape[2])


def _step(x2, tgt, seq, shards, small, ids):
    tabs = _rope_tables(seq)
    g1, g2, g3, g4 = (small[n] for n in ("g_mix_pre", "g_mix_post", "g_mlp_pre", "g_mlp_post"))
    sinks = small["attn_sinks"].reshape(N_Q_HEADS)
    w_pool = small["w_pool"].reshape(4, POOL_GC, POOL_GC)
    pool_scale = small["pool_scale"]

    def whole(shard, slabs):
        return lax.dynamic_update_slice(slabs, shard[None], (ids[0], 0, 0))

    up_a, up_b = shards["w_up"][:HALF], shards["w_up"][HALF:]
    down_a, down_b = shards["w_down"][:HALF], shards["w_down"][HALF:]
    w_in = _stack_rows(whole(shards["w_in"], _alone("gather_in", _ex_gather([shards["w_in"]]))[0][0]))
    mix_shards = [shards[n] for n in ("w_branch_pool", "w_branch_attn", "w_out")]
    (h, u, q, k, v, gate), [mix_slabs] = _inproj(
        x2, g1, w_in, small["b_in"], tabs, seq, exchanges=[_ex_gather(mix_shards)])
    w_bp, w_ba, out_slab = (whole(s, g) for s, g in zip(mix_shards, mix_slabs))
    w_out = _stack_rows(out_slab)
    (y_attn, diff, y_pool), [[got_a, got_b]] = _mixers_fwd(
        q, k, v, sinks, u, w_pool, pool_scale, seq, exchanges=[_ex_gather([up_a, up_b])])
    (merged, mix, x1, h2), [[got_c, got_d]] = _merge_out(
        y_pool, y_attn, gate, x2, w_bp, w_ba, w_out, g2, g3, exchanges=[_ex_gather([down_a, down_b])])
    w_up = (whole(up_a, got_a), whole(up_b, got_b))
    w_down = (whole(down_a, got_c), whole(down_b, got_d))
    act, dff, dup, dx1, dmix, loss_acc, dg4, dg3, dg2 = _mlp_core(h2, x1, mix, tgt, w_up, w_down, g4, g3, g2)

    dw_down = _dw("down", act, dff, 1024, 1024)[0].reshape(N_CHIPS, D_FF // N_CHIPS, D_MODEL)
    (dbp, dba, dgate, dyp, dya), [[got]] = _merge_bwd(
        dmix, gate, y_pool, y_attn, w_out, w_bp, w_ba, exchanges=[_ex_pair([dw_down])])
    ps_down = _pair_sum(ids, dw_down, got)
    (dw_up,), [[got]] = _dw("up", h2, dup, 1024, 1024, shard_cols=True, exchanges=[_ex_chip([ps_down[1]])])
    half_down = _chip_sum(ids, ps_down[0], got)
    (dw_out, dw_bp, dw_ba), [[got]] = _dw_mix(merged, dmix, y_pool, dbp, y_attn, dba, exchanges=[_ex_pair([dw_up])])
    ps_up = _pair_sum(ids, dw_up, got)
    dw_mix = [dw_out.reshape(N_CHIPS, D_MODEL // N_CHIPS, D_MODEL), dw_bp, dw_ba]
    (dq, dk, dv, dsink), [[got], gots, [g_down]] = _attn_bwd(
        q, k, v, dya, sinks, tabs, seq, exchanges=[_ex_chip([ps_up[1]]), _ex_pair(dw_mix), _ex_swap([half_down])])
    half_up = _chip_sum(ids, ps_up[0], got)
    ps_mix = [_pair_sum(ids, d, g) for d, g in zip(dw_mix, gots)]
    (du, dw_pool, dps), [[g_up]] = _pool_bwd(dyp, diff, w_pool, pool_scale, seq, exchanges=[_ex_swap([half_up])])
    parts = (du, dq, dk, dv, dgate)
    early = _early_block(dg2, dg3, dg4, dps, dsink[:, 0], loss_acc[0, 0])
    mat = dw_pool.reshape(4 * POOL_GC, POOL_GC)
    (dw_in_t, db_in), [gots, [gearly, gmat]] = _dw_in(
        h, parts, exchanges=[_ex_chip([p[1] for p in ps_mix]), _ex_allgather([early, mat])])
    half_mix = [_chip_sum(ids, p[0], g) for p, g in zip(ps_mix, gots)]
    dw_in = dw_in_t.reshape(N_CHIPS, IN_WIDTH // N_CHIPS, D_MODEL)
    g_mix, [got] = _alone("swap_mix_pair_in", _ex_swap(half_mix), _ex_pair([dw_in]))
    ps_in = _pair_sum(ids, dw_in, got)
    (gx, dg1), [[got]] = _inproj_bwd(parts, x2, dx1, w_in, g1, exchanges=[_ex_chip([ps_in[1]])])
    [g_in], [glate] = _alone("swap_in_allgather", _ex_swap([_chip_sum(ids, ps_in[0], got)]),
                             _ex_allgather([_late_block(db_in, dg1)]))

    grads = dict(w_in=g_in, w_branch_pool=g_mix[1], w_branch_attn=g_mix[2], w_out=g_mix[0], w_up=g_up, w_down=g_down)
    return (gearly, gmat, glate), gx, grads


def kernel(x, g_mix_pre, w_in, b_in, w_pool, pool_scale, attn_sinks, w_branch_pool, w_branch_attn, w_out, g_mix_post, g_mlp_pre, w_up, w_down, g_mlp_post, loss_target, m_g_mix_pre, m_w_in, m_b_in, m_w_pool, m_pool_scale, m_attn_sinks, m_w_branch_pool, m_w_branch_attn, m_w_out, m_g_mix_post, m_g_mlp_pre, m_w_up, m_w_down, m_g_mlp_post, v_g_mix_pre, v_w_in, v_b_in, v_w_pool, v_pool_scale, v_attn_sinks, v_w_branch_pool, v_w_branch_attn, v_w_out, v_g_mix_post, v_g_mlp_pre, v_w_up, v_w_down, v_g_mlp_post):
    weights = dict(g_mix_pre=g_mix_pre, w_in=w_in, b_in=b_in, w_pool=w_pool, pool_scale=pool_scale,
                   attn_sinks=attn_sinks, w_branch_pool=w_branch_pool, w_branch_attn=w_branch_attn, w_out=w_out,
                   g_mix_post=g_mix_post, g_mlp_pre=g_mlp_pre, w_up=w_up, w_down=w_down, g_mlp_post=g_mlp_post)
    mom1 = dict(g_mix_pre=m_g_mix_pre, w_in=m_w_in, b_in=m_b_in, w_pool=m_w_pool, pool_scale=m_pool_scale,
                attn_sinks=m_attn_sinks, w_branch_pool=m_w_branch_pool, w_branch_attn=m_w_branch_attn,
                w_out=m_w_out, g_mix_post=m_g_mix_post, g_mlp_pre=m_g_mlp_pre, w_up=m_w_up, w_down=m_w_down,
                g_mlp_post=m_g_mlp_post)
    mom2 = dict(g_mix_pre=v_g_mix_pre, w_in=v_w_in, b_in=v_b_in, w_pool=v_w_pool, pool_scale=v_pool_scale,
                attn_sinks=v_attn_sinks, w_branch_pool=v_w_branch_pool, w_branch_attn=v_w_branch_attn,
                w_out=v_w_out, g_mix_post=v_g_mix_post, g_mlp_pre=v_g_mlp_pre, w_up=v_w_up, w_down=v_w_down,
                g_mlp_post=v_g_mlp_post)
    b_loc, seq, _ = x.shape
    x2 = x.reshape(b_loc * seq, D_MODEL)
    tgt = loss_target.reshape(b_loc * seq, D_MODEL)
    ids = jnp.stack([2 * lax.axis_index("x") + lax.axis_index("y"), lax.axis_index("c")]).astype(jnp.int32)

    def flat(n, a):
        return a[0].T if n == "w_in" else a[0]

    def unflat(n, a):
        return (a.T if n == "w_in" else a)[None]

    shards = {n: flat(n, weights[n]).astype(BF16) for n in _BIG}
    small = {n: weights[n] for n in _ORDER if n not in _BIG}
    (gearly, gmat, glate), gx, grads = _step(x2, tgt, seq, shards, small, ids)

    def two_d(src):
        return {n: src[n].reshape(4 * POOL_GC, POOL_GC) if n == "w_pool" else src[n] for n in _SMALL_NAMES}

    loss, per = _small_update(gearly, gmat, glate, two_d(weights), two_d(mom1), two_d(mom2))
    delta, new_m, new_v = {}, {}, {}
    for n in _SMALL_NAMES:
        grads[n], delta[n], new_m[n], new_v[n] = (a.reshape(weights[n].shape) for a in per[n])
    for n in _BIG:
        update = _adamw if n == "w_in" else _adamw_sparse
        d, nm, nv = update(flat(n, weights[n]), grads[n], flat(n, mom1[n]), flat(n, mom2[n]))
        grads[n] = unflat(n, grads[n])
        delta[n], new_m[n], new_v[n] = unflat(n, d), unflat(n, nm), unflat(n, nv)

    return (loss[0, 0], gx.reshape(x.shape), *[grads[n] for n in _ORDER], *[delta[n] for n in _ORDER],
            *[new_m[n] for n in _ORDER], *[new_v[n] for n in _ORDER])
```

```python
import jax
import jax.numpy as jnp
from jax import lax
from jax.experimental import pallas as pl
from jax.experimental.pallas import tpu as pltpu
from jax.experimental.pallas import tpu_sc as plsc

F32 = jnp.float32
BF16 = jnp.bfloat16

D_MODEL = 1024
POOL_WINDOWS = (2, 4, 8, 16)
POOL_WIDTH = 512
POOL_GC = 128
HALO = 16
HEAD_DIM = 64
N_Q_HEADS = 8
ATTN_WIDTH = 512
KV_WIDTH = 128
BLOCK = 128
NEG_INF = -1e30
ROPE_THETA = 500000.0
ROT_DIM = 16
GATE_WIDTH = 2048
IN_WIDTH = 3328
D_FF = 4096
EPS = 1e-6
SCALE = HEAD_DIM ** -0.5
C_Q, C_K, C_V, C_G = 512, 1024, 1152, 1280

ADAM_LR, ADAM_B1, ADAM_B2, ADAM_EPS, ADAM_WD, ADAM_STEP = 0.001, 0.9, 0.999, 1e-08, 0.01, 10

N_CHIPS = 4
N_DEV = 8
LANES = 128
TM = 512
TP = 512
VMEM_MB = 56

MESH = pl.DeviceIdType.MESH
ANY = pl.BlockSpec(memory_space=pl.ANY)


def _cp(*sem, vmem=VMEM_MB):
    return pltpu.CompilerParams(dimension_semantics=sem, vmem_limit_bytes=vmem * 1024 * 1024)


def _rows(tile, cols):
    return pl.BlockSpec((tile, cols), lambda i: (i, 0))


def _const(shape):
    nd = len(shape)
    return pl.BlockSpec(shape, lambda i: (0,) * nd)


def _sds(shape, dtype):
    return jax.ShapeDtypeStruct(shape, dtype)


def _dot(a, b):
    return jnp.dot(a, b, preferred_element_type=F32)


def _dot_nt(a, b):
    return lax.dot_general(a, b, (((1,), (1,)), ((), ())), preferred_element_type=F32)


def _dot_tn(a, b):
    return lax.dot_general(a, b, (((0,), (0,)), ((), ())), preferred_element_type=F32)


def _rms(x):
    return lax.rsqrt(jnp.mean(x * x, axis=-1, keepdims=True) + EPS)


def _norm_bwd(x, g, dout):
    r = _rms(x)
    n = x * r
    dn = dout * g
    dx = r * (dn - n * jnp.mean(dn * n, axis=-1, keepdims=True))
    return dx, jnp.sum(dout * n, axis=0, keepdims=True)


def _rot_fwd(t, c, a, bt):
    return t * c + pltpu.roll(t, LANES - 8, 1) * a + pltpu.roll(t, 8, 1) * bt


def _rot_bwd(d, c, a, bt):
    return d * c + pltpu.roll(d * a, 8, 1) + pltpu.roll(d * bt, LANES - 8, 1)


def _rope_tables(seq):
    pos = jnp.arange(seq, dtype=F32)
    inv_freq = ROPE_THETA ** (-jnp.arange(0, ROT_DIM, 2, dtype=F32) / ROT_DIM)
    ang = pos[:, None] * inv_freq[None, :]
    cos, sin = jnp.cos(ang), jnp.sin(ang)
    ones = jnp.ones((seq, HEAD_DIM - ROT_DIM), F32)
    zeros8 = jnp.zeros((seq, 8), F32)
    zrest = jnp.zeros((seq, HEAD_DIM - ROT_DIM), F32)
    c = jnp.concatenate([cos, cos, ones], axis=1)
    a = jnp.concatenate([-sin, zeros8, zrest], axis=1)
    bt = jnp.concatenate([zeros8, sin, zrest], axis=1)
    return tuple(jnp.tile(t, (1, 2)) for t in (c, a, bt))


class _Exchange:
    def __init__(self, inputs, out_shapes, sems, start, finish, aliases=None, middle=None, late=None):
        self.inputs, self.out_shapes, self.sems = list(inputs), list(out_shapes), list(sems)
        self.start, self.finish, self.aliases = start, finish, dict(aliases or {})
        self.middle, self.late = middle, late


def _call(body, *, name, grid, in_specs, out_specs, out_shape, args, scratch=(), sem=(), exchanges=()):
    in_specs, out_specs, out_shape, scratch = list(in_specs), list(out_specs), list(out_shape), list(scratch)
    if not exchanges:
        return pl.pallas_call(body, name=name, grid=grid, in_specs=in_specs, out_specs=out_specs,
                              out_shape=out_shape, scratch_shapes=scratch, compiler_params=_cp(*sem))(*args)
    n_in, n_out, n_scr = len(in_specs), len(out_specs), len(scratch)
    x_in = [a for ex in exchanges for a in ex.inputs]
    x_out = [s for ex in exchanges for s in ex.out_shapes]
    x_sem = [s for ex in exchanges for s in ex.sems]
    aliases, i_off, o_off = {}, n_in, n_out
    for ex in exchanges:
        for i, o in ex.aliases.items():
            aliases[i_off + i] = o_off + o
        i_off += len(ex.inputs)
        o_off += len(ex.out_shapes)

    def split(flat):
        out, pos = [], 0
        for ex, n in zip(exchanges, flat[1]):
            out.append(flat[0][pos:pos + n])
            pos += n
        return out

    def carrier(*refs):
        pos = 0
        groups = []
        for n in (n_in, len(x_in), n_out, len(x_out), n_scr, len(x_sem)):
            groups.append(refs[pos:pos + n])
            pos += n
        ins, xin, outs, xout, scr, xsem = groups
        xin = split((xin, [len(ex.inputs) for ex in exchanges]))
        xout = split((xout, [len(ex.out_shapes) for ex in exchanges]))
        xsem = split((xsem, [len(ex.sems) for ex in exchanges]))
        first = pl.program_id(0) == 0
        last = pl.program_id(0) == grid[0] - 1
        for d in range(1, len(grid)):
            first = jnp.logical_and(first, pl.program_id(d) == 0)
            last = jnp.logical_and(last, pl.program_id(d) == grid[d] - 1)

        @pl.when(first)
        def _():
            for ex, i, o, s in zip(exchanges, xin, xout, xsem):
                ex.start(i, o, s)

        for hook, step in (("middle", grid[0] // 2), ("late", 3 * grid[0] // 4)):
            if any(getattr(ex, hook) for ex in exchanges):
                now = pl.program_id(0) == step
                for d in range(1, len(grid)):
                    now = jnp.logical_and(now, pl.program_id(d) == 0)

                @pl.when(now)
                def _(hook=hook):
                    for ex, i, o, s in zip(exchanges, xin, xout, xsem):
                        if getattr(ex, hook):
                            getattr(ex, hook)(i, o, s)

        body(*ins, *outs, *scr)

        @pl.when(last)
        def _():
            for ex, i, o, s in zip(exchanges, xin, xout, xsem):
                ex.finish(i, o, s)

    res = pl.pallas_call(
        carrier, name=name, grid=grid, in_specs=in_specs + [ANY] * len(x_in),
        out_specs=out_specs + [ANY] * len(x_out), out_shape=out_shape + x_out,
        scratch_shapes=scratch + x_sem, input_output_aliases=aliases,
        compiler_params=_cp(*(["arbitrary"] * len(grid))),
    )(*args, *x_in)
    return res[:n_out], split((res[n_out:], [len(ex.out_shapes) for ex in exchanges]))


def _alone(name, *exchanges):
    n_in = [len(ex.inputs) for ex in exchanges]
    n_out = [len(ex.out_shapes) for ex in exchanges]
    n_sem = [len(ex.sems) for ex in exchanges]
    aliases, i_off, o_off = {}, 0, 0
    for ex in exchanges:
        for i, o in ex.aliases.items():
            aliases[i_off + i] = o_off + o
        i_off += len(ex.inputs)
        o_off += len(ex.out_shapes)

    def split(flat, counts):
        out, pos = [], 0
        for n in counts:
            out.append(flat[pos:pos + n])
            pos += n
        return out

    def body(*refs):
        ins, outs, sems = split(refs, [sum(n_in), sum(n_out), sum(n_sem)])
        groups = list(zip(exchanges, split(ins, n_in), split(outs, n_out), split(sems, n_sem)))
        for ex, i, o, s in groups:
            ex.start(i, o, s)
        for hook in ("middle", "late"):
            for ex, i, o, s in groups:
                if getattr(ex, hook):
                    getattr(ex, hook)(i, o, s)
        for ex, i, o, s in groups:
            ex.finish(i, o, s)

    res = pl.pallas_call(
        body, name=name, in_specs=[ANY] * sum(n_in), out_specs=[ANY] * sum(n_out),
        out_shape=[s for ex in exchanges for s in ex.out_shapes],
        scratch_shapes=[s for ex in exchanges for s in ex.sems], input_output_aliases=aliases,
    )(*[a for ex in exchanges for a in ex.inputs])
    return split(res, n_out)


def _place():
    x, y, c = lax.axis_index("x"), lax.axis_index("y"), lax.axis_index("c")
    chips = [(1 - x, y), (x, 1 - y), (1 - x, 1 - y)]
    return x, y, c, chips


def _remote(src, dst, send, recv, to):
    return pltpu.make_async_remote_copy(src_ref=src, dst_ref=dst, send_sem=send, recv_sem=recv,
                                        device_id=to, device_id_type=MESH)


def _ex_gather(shards):
    nw = len(shards)
    hrs = [s.shape[0] // 2 for s in shards]

    def copies(ins, outs, sems):
        s1, r1, s2, r2, fs, fr = sems
        x, y, c, _ = _place()
        me, xn, yn, dg = (x, y), (1 - x, y), (x, 1 - y), (1 - x, 1 - y)
        nbr = (xn, yn)
        sibling = (x, y, 1 - c)

        def piece(w, chip, core, part=None):
            hr = hrs[w]
            rows = pl.ds(core * hr, hr) if part is None else pl.ds(core * hr + part * (hr // 2), hr // 2)
            return outs[w].at[2 * chip[0] + chip[1], rows]

        def first(w, k):
            return _remote(ins[w].at[pl.ds(c * hrs[w], hrs[w])], piece(w, me, c), s1.at[w, k], r1.at[w, k],
                           (*nbr[k], c))

        def landed(w, k):
            return _remote(piece(w, nbr[k], c), piece(w, nbr[k], c), s1.at[w, k], r1.at[w, k], (*nbr[k], c))

        def onward(w, k):
            return _remote(piece(w, nbr[k], c, k), piece(w, nbr[k], c, k), s2.at[w, k], r2.at[w, k],
                           (*nbr[1 - k], c))

        def arrived(w, k):
            return _remote(piece(w, dg, c, k), piece(w, dg, c, k), s2.at[w, k], r2.at[w, k], (*nbr[1 - k], c))

        def passed(w, j):
            chip = (xn, yn, dg)[j]
            return _remote(piece(w, chip, c), piece(w, chip, c), fs.at[w, j], fr.at[w, j], sibling)

        def handed(w, j):
            chip = (xn, yn, dg)[j]
            return _remote(piece(w, chip, 1 - c), piece(w, chip, 1 - c), fs.at[w, j], fr.at[w, j], sibling)

        return first, landed, onward, arrived, passed, handed

    def start(ins, outs, sems):
        first = copies(ins, outs, sems)[0]
        for w in range(nw):
            for k in range(2):
                first(w, k).start()

    def middle(ins, outs, sems):
        _, landed, onward, _, passed, _ = copies(ins, outs, sems)
        for w in range(nw):
            for k in range(2):
                landed(w, k).wait_recv()
                onward(w, k).start()
                passed(w, k).start()

    def late(ins, outs, sems):
        _, _, _, arrived, passed, _ = copies(ins, outs, sems)
        for w in range(nw):
            for k in range(2):
                arrived(w, k).wait_recv()
            passed(w, 2).start()

    def finish(ins, outs, sems):
        first, _, onward, _, passed, handed = copies(ins, outs, sems)
        for w in range(nw):
            for j in range(3):
                handed(w, j).wait_recv()
        for w in range(nw):
            for k in range(2):
                first(w, k).wait_send()
                onward(w, k).wait_send()
            for j in range(3):
                passed(w, j).wait_send()

    return _Exchange(shards, [_sds((N_CHIPS,) + s.shape, s.dtype) for s in shards],
                     [pltpu.SemaphoreType.DMA((nw, 2))] * 4 + [pltpu.SemaphoreType.DMA((nw, 3))] * 2,
                     start, finish, middle=middle, late=late)


def _ex_pair(grads):
    nw = len(grads)

    def copies(ins, outs, sems):
        x, y, c, _ = _place()
        out = []
        for w in range(nw):
            hr = grads[w].shape[1] // 2
            out.append(_remote(ins[w].at[:, pl.ds((1 - c) * hr, hr)], outs[w], sems[0].at[w], sems[1].at[w],
                               (x, y, 1 - c)))
        return out

    def start(ins, outs, sems):
        for cp in copies(ins, outs, sems):
            cp.start()

    def finish(ins, outs, sems):
        for cp in copies(ins, outs, sems):
            cp.wait()

    return _Exchange(grads, [_sds((N_CHIPS, g.shape[1] // 2, g.shape[2]), F32) for g in grads],
                     [pltpu.SemaphoreType.DMA((nw,))] * 2, start, finish)


def _ex_chip(pieces):
    nw = len(pieces)

    def copies(ins, outs, sems):
        x, y, c, chips = _place()
        return [_remote(ins[w].at[2 * cx + cy], outs[w].at[k], sems[0].at[w, k], sems[1].at[w, k], (cx, cy, c))
                for w in range(nw) for k, (cx, cy) in enumerate(chips)]

    def start(ins, outs, sems):
        for cp in copies(ins, outs, sems):
            cp.start()

    def finish(ins, outs, sems):
        for cp in copies(ins, outs, sems):
            cp.wait()

    return _Exchange(pieces, [_sds((3,) + p.shape[1:], BF16) for p in pieces],
                     [pltpu.SemaphoreType.DMA((nw, 3))] * 2, start, finish)


def _ex_swap(fulls):
    nw = len(fulls)

    def start(ins, outs, sems):
        x, y, c, _ = _place()
        for w in range(nw):
            hr = fulls[w].shape[0] // 2
            mine = pl.ds(c * hr, hr)
            _remote(ins[w].at[mine], outs[w].at[mine], sems[0].at[w], sems[1].at[w], (x, y, 1 - c)).start()

    def finish(ins, outs, sems):
        x, y, c, _ = _place()
        for w in range(nw):
            hr = fulls[w].shape[0] // 2
            mine, theirs = pl.ds(c * hr, hr), pl.ds((1 - c) * hr, hr)
            _remote(ins[w].at[mine], outs[w].at[mine], sems[0].at[w], sems[1].at[w], (x, y, 1 - c)).wait_send()
            _remote(ins[w].at[theirs], outs[w].at[theirs], sems[0].at[w], sems[1].at[w], (x, y, 1 - c)).wait_recv()

    return _Exchange(fulls, [_sds(f.shape, F32) for f in fulls], [pltpu.SemaphoreType.DMA((nw,))] * 2,
                     start, finish, aliases={w: w for w in range(nw)})


def _ex_allgather(blocks):
    nb = len(blocks)

    def copies(ins, outs, sems):
        send, recv, lsem = sems
        x, y, c, chips = _place()
        me, sibling = (x, y, c), (x, y, 1 - c)

        def rows(b, px, py, pc):
            m_per = blocks[b].shape[0]
            return outs[b].at[pl.ds((4 * px + 2 * py + pc) * m_per, m_per), :]

        def copy(b, k, blk, to, src=None):
            return _remote(rows(b, *blk) if src is None else src, rows(b, *blk), send.at[b, k], recv.at[b, k], to)

        def mine(b):
            return pltpu.make_async_copy(ins[b], rows(b, *me), lsem.at[b])

        def first(b, k):
            return copy(b, k, me, sibling if k == 0 else (*chips[k - 1], c), src=ins[b])

        def passed(b, j):
            return copy(b, 4 + j, (*chips[j], c), sibling)

        def landed(b, j):
            return copy(b, 1 + j, (*chips[j], c), me)

        def handed(b, k):
            return copy(b, 0, sibling, me) if k == 0 else copy(b, 3 + k, (*chips[k - 1], 1 - c), me)

        return mine, first, passed, landed, handed

    def start(ins, outs, sems):
        mine, first, _, _, _ = copies(ins, outs, sems)
        for b in range(nb):
            mine(b).start()
            for k in range(4):
                first(b, k).start()

    def late(ins, outs, sems):
        _, _, passed, landed, _ = copies(ins, outs, sems)
        for b in range(nb):
            for j in range(3):
                landed(b, j).wait_recv()
                passed(b, j).start()

    def finish(ins, outs, sems):
        mine, first, passed, _, handed = copies(ins, outs, sems)
        for b in range(nb):
            for k in range(4):
                handed(b, k).wait_recv()
            for k in range(4):
                first(b, k).wait_send()
            for j in range(3):
                passed(b, j).wait_send()
            mine(b).wait()

    return _Exchange(blocks, [_sds((N_DEV * b.shape[0], b.shape[1]), F32) for b in blocks],
                     [pltpu.SemaphoreType.DMA((nb, 7)), pltpu.SemaphoreType.DMA((nb, 7)), pltpu.SemaphoreType.DMA((nb,))],
                     start, finish, late=late)


def _inproj(x2, g1, w_in_t, b_in, tabs, seq, exchanges=()):
    T = x2.shape[0]
    tm = min(TM, seq)
    nseq = seq // tm

    def body(x_ref, g_ref, w_ref, b_ref, c_ref, a_ref, bt_ref, h_ref, u_ref, q_ref, k_ref, v_ref, gate_ref):
        x = x_ref[...]
        h = (x * _rms(x) * g_ref[...]).astype(BF16)
        h_ref[...] = h

        def proj(lo, hi):
            return _dot_nt(h, w_ref[lo:hi, :]) + b_ref[:, lo:hi]

        c, a, bt = c_ref[...], a_ref[...], bt_ref[...]
        u_ref[...] = proj(0, C_Q)
        q = proj(C_Q, C_K)
        for p in range(4):
            sl = slice(LANES * p, LANES * (p + 1))
            q_ref[:, sl] = (_rot_fwd(q[:, sl], c, a, bt) * SCALE).astype(BF16)
        kv = proj(C_K, C_G)
        k_ref[...] = _rot_fwd(kv[:, :KV_WIDTH], c, a, bt).astype(BF16)
        v_ref[...] = kv[:, KV_WIDTH:].astype(BF16)
        for j in range(2):
            lo = C_G + D_MODEL * j
            gate_ref[:, D_MODEL * j:D_MODEL * (j + 1)] = jax.nn.sigmoid(proj(lo, lo + D_MODEL)).astype(BF16)

    tab = pl.BlockSpec((tm, LANES), lambda i: (i % nseq, 0))
    return _call(
        body, name="inproj", grid=(T // tm,),
        in_specs=[_rows(tm, D_MODEL), _const((1, D_MODEL)), _const((IN_WIDTH, D_MODEL)), _const((1, IN_WIDTH)),
                  tab, tab, tab],
        out_specs=[_rows(tm, D_MODEL), _rows(tm, POOL_WIDTH), _rows(tm, ATTN_WIDTH), _rows(tm, KV_WIDTH),
                   _rows(tm, KV_WIDTH), _rows(tm, GATE_WIDTH)],
        out_shape=[_sds((T, D_MODEL), BF16), _sds((T, POOL_WIDTH), F32), _sds((T, ATTN_WIDTH), BF16),
                   _sds((T, KV_WIDTH), BF16), _sds((T, KV_WIDTH), BF16), _sds((T, GATE_WIDTH), BF16)],
        args=(x2, g1, w_in_t, b_in, *tabs), sem=("parallel",), exchanges=exchanges)


def _inv_count(pos, w):
    return 1.0 / jnp.minimum(pos + 1, w).astype(F32)


def _pool_tile(i, tp, nseq, u_ref, prev_ref, w_ref, s_ref, diff_ref, y_ref):
    first = (i % nseq) == 0
    prev = jnp.where(first, 0.0, prev_ref[...])
    ext = jnp.concatenate([prev, u_ref[...]], axis=0)
    pos = (i % nseq) * tp + lax.broadcasted_iota(jnp.int32, (tp, 1), 0)
    for gi, w in enumerate(POOL_WINDOWS):
        sl = slice(POOL_GC * gi, POOL_GC * (gi + 1))
        xg = ext[:, sl]
        s = xg
        sh = 1
        while sh < w:
            s = s + pltpu.roll(s, sh, 0)
            sh *= 2
        pooled = s[HALO:] * _inv_count(pos, w)
        diff = (pooled - xg[HALO:]).astype(BF16)
        diff_ref[:, sl] = diff
        mixed = _dot(diff, w_ref[gi].astype(BF16))
        y_ref[:, sl] = (mixed * s_ref[:, sl]).astype(BF16)


def _pool_specs(tp):
    per = tp // HALO
    return [_rows(tp, POOL_WIDTH), pl.BlockSpec((HALO, POOL_WIDTH), lambda i: (jnp.maximum(i * per - 1, 0), 0)),
            _const((4, POOL_GC, POOL_GC)), _const((1, POOL_WIDTH))]


GROUP = 4
GROWS = GROUP * BLOCK


def _attn_masks(n):
    qi = lax.broadcasted_iota(jnp.int32, (GROWS, 2 * BLOCK), 0) % BLOCK
    kj = lax.broadcasted_iota(jnp.int32, (GROWS, 2 * BLOCK), 1)
    rel = qi + BLOCK - kj
    valid = (rel >= 0) & (rel < BLOCK) & (kj >= jnp.where(n > 0, 0, BLOCK))
    lo = lax.broadcasted_iota(jnp.int32, (BLOCK, LANES), 1) < HEAD_DIM
    return valid, lo


def _by_example(bl, *arrays):
    return [a.reshape(bl, a.shape[0] // bl, a.shape[1]) for a in arrays]


def _stack_heads(ref, h, lo):
    keep = lo if h == 0 else jnp.logical_not(lo)
    pieces = []
    for p in (2 * h, 2 * h + 1):
        xp = ref[:, LANES * p:LANES * (p + 1)].astype(F32)
        for e in range(2):
            t = xp if e == h else pltpu.roll(xp, HEAD_DIM, 1)
            pieces.append(jnp.where(keep, t, 0.0).astype(BF16))
    return jnp.concatenate(pieces, axis=0)


def _unstack_heads(stacked, h, lo):
    pairs = []
    for j in range(2):
        parts = []
        for e in range(2):
            t = stacked[BLOCK * (2 * j + e):BLOCK * (2 * j + e + 1)]
            parts.append(t if e == h else pltpu.roll(t, HEAD_DIM, 1))
        pairs.append(jnp.where(lo, parts[0], parts[1]))
    return pairs


def _sink_rows(sink_ref, h):
    head = lax.broadcasted_iota(jnp.int32, (GROWS, 1), 0) // BLOCK
    col = jnp.zeros((GROWS, 1), F32) + sink_ref[GROUP * h]
    for g in range(1, GROUP):
        col = jnp.where(head == g, sink_ref[GROUP * h + g], col)
    return col


def _group_probs(qs, kk, valid, sink):
    s = jnp.where(valid, _dot_nt(qs, kk), NEG_INF)
    m = jnp.maximum(jnp.max(s, axis=1, keepdims=True), sink)
    ex = jnp.exp(s - m)
    es = jnp.exp(sink - m)
    inv = 1.0 / (jnp.sum(ex, axis=1, keepdims=True) + es)
    return ex * inv, es * inv


def _mixers_fwd(q, k, v, sinks, u, w_pool, pool_scale, seq, exchanges=()):
    T = q.shape[0]
    nb = seq // BLOCK
    bl = T // seq
    tp = T // nb
    nseq = seq // tp

    def body(sink_ref, q_ref, kp_ref, kc_ref, vp_ref, vc_ref, u_ref, prev_ref, w_ref, s_ref, o_ref, diff_ref, y_ref):
        n = pl.program_id(0)
        valid, lo = _attn_masks(n)
        for b in range(bl):
            kk = jnp.concatenate([kp_ref[b], kc_ref[b]], axis=0)
            vv = jnp.concatenate([vp_ref[b], vc_ref[b]], axis=0)
            for h in range(2):
                qs = _stack_heads(q_ref.at[b], h, lo)
                pr, _ = _group_probs(qs, kk, valid, _sink_rows(sink_ref, h))
                o = _dot(pr.astype(BF16), vv)
                for j, pair in enumerate(_unstack_heads(o, h, lo)):
                    p = 2 * h + j
                    o_ref[b, :, LANES * p:LANES * (p + 1)] = pair.astype(BF16)
        _pool_tile(n, tp, nseq, u_ref, prev_ref, w_ref, s_ref, diff_ref, y_ref)

    cur = lambda n: (0, n, 0)
    prv = lambda n: (0, jnp.maximum(n - 1, 0), 0)
    kv = lambda m: pl.BlockSpec((bl, BLOCK, KV_WIDTH), m)
    res = _call(
        body, name="mixers_fwd", grid=(nb,),
        in_specs=[pl.BlockSpec(memory_space=pltpu.SMEM), pl.BlockSpec((bl, BLOCK, ATTN_WIDTH), cur),
                  kv(prv), kv(cur), kv(prv), kv(cur)] + _pool_specs(tp),
        out_specs=[pl.BlockSpec((bl, BLOCK, ATTN_WIDTH), cur), _rows(tp, POOL_WIDTH), _rows(tp, POOL_WIDTH)],
        out_shape=[_sds((bl, seq, ATTN_WIDTH), BF16), _sds((T, POOL_WIDTH), BF16), _sds((T, POOL_WIDTH), BF16)],
        args=(sinks, *_by_example(bl, q, k, k, v, v), u, u, w_pool, pool_scale), sem=("parallel",),
        exchanges=exchanges)
    outs, rest = res if exchanges else (res, None)
    return [outs[0].reshape(T, ATTN_WIDTH), outs[1], outs[2]], rest


def _branch(y, w_ref):
    return jnp.concatenate([_dot(y, w_ref[j]) for j in range(N_CHIPS)], axis=1)


def _merge_out(y_pool, y_attn, gate, x2, w_bp, w_ba, w_out, g2, g3, exchanges=()):
    T = x2.shape[0]
    tm = min(TM, T)

    def body(yp_ref, ya_ref, gate_ref, x_ref, wbp_ref, wba_ref, wo_ref, g2_ref, g3_ref,
             mg_ref, mix_ref, x1_ref, h2_ref):
        bp, ba = _branch(yp_ref[...], wbp_ref), _branch(ya_ref[...], wba_ref)
        merged = (gate_ref[:, :D_MODEL].astype(F32) * bp + gate_ref[:, D_MODEL:].astype(F32) * ba).astype(BF16)
        mg_ref[...] = merged
        mix = _dot(merged, wo_ref[...])
        mix_ref[...] = mix
        x1 = x_ref[...] + mix * _rms(mix) * g2_ref[...]
        x1_ref[...] = x1
        h2_ref[...] = (x1 * _rms(x1) * g3_ref[...]).astype(BF16)

    return _call(
        body, name="merge_out", grid=(T // tm,),
        in_specs=[_rows(tm, POOL_WIDTH), _rows(tm, ATTN_WIDTH), _rows(tm, GATE_WIDTH), _rows(tm, D_MODEL),
                  _const(w_bp.shape), _const(w_ba.shape), _const((D_MODEL, D_MODEL)),
                  _const((1, D_MODEL)), _const((1, D_MODEL))],
        out_specs=[_rows(tm, D_MODEL)] * 4,
        out_shape=[_sds((T, D_MODEL), BF16), _sds((T, D_MODEL), F32), _sds((T, D_MODEL), F32),
                   _sds((T, D_MODEL), BF16)],
        args=(y_pool, y_attn, gate, x2, w_bp, w_ba, w_out, g2, g3), sem=("parallel",), exchanges=exchanges)


HALF = D_MODEL // 2
TM_MLP = 256


def _mlp_core(h2, x1, mix, tgt, w_up, w_down, g4, g3, g2):
    T = h2.shape[0]
    tm = min(TM_MLP, T)

    def body(h_ref, x1_ref, mix_ref, t_ref, g_ref, g3_ref, g2_ref, ua_hbm, ub_hbm, da_hbm, db_hbm,
             act_ref, dff_ref, dup_ref, dx1_ref, dmix_ref, loss_ref, dg_ref, dg3_ref, dg2_ref,
             wu, wd, relu_scr, sems):
        def weight_copy(i):
            src, dst = ((ua_hbm, wu.at[:, :HALF]), (ub_hbm, wu.at[:, HALF:]),
                        (da_hbm, wd.at[:, :HALF]), (db_hbm, wd.at[:, HALF:]))[i]
            return pltpu.make_async_copy(src, dst, sems.at[i])

        @pl.when(pl.program_id(0) == 0)
        def _():
            for i in range(4):
                weight_copy(i).start()
            loss_ref[...] = jnp.zeros_like(loss_ref)
            for ref in (dg_ref, dg3_ref, dg2_ref):
                ref[...] = jnp.zeros_like(ref)
            weight_copy(0).wait()
            weight_copy(1).wait()

        h = h_ref[...]
        ff = None
        for j in range(N_CHIPS):
            lo = D_MODEL * j
            relu = jnp.maximum(_dot(h, wu[j]), 0.0)
            if j == 0:
                @pl.when(pl.program_id(0) == 0)
                def _():
                    weight_copy(2).wait()
                    weight_copy(3).wait()
            relu_scr[:, lo:lo + D_MODEL] = relu
            act = jnp.square(relu).astype(BF16)
            act_ref[:, lo:lo + D_MODEL] = act
            t = _dot(act, wd[j])
            ff = t if ff is None else ff + t
        g = g_ref[...]
        x1 = x1_ref[...]
        err = x1 + ff * _rms(ff) * g - t_ref[...]
        loss_ref[...] += jnp.sum(err * err) * (0.5 / D_MODEL)
        dy = err * (1.0 / D_MODEL)
        dff, dg = _norm_bwd(ff, g, dy)
        dg_ref[...] += dg
        dff = dff.astype(BF16)
        dff_ref[...] = dff
        dh2 = None
        for j in range(N_CHIPS):
            lo = D_MODEL * j
            dup = (_dot_nt(dff, wd[j]) * (2.0 * relu_scr[:, lo:lo + D_MODEL])).astype(BF16)
            dup_ref[:, lo:lo + D_MODEL] = dup
            t = _dot_nt(dup, wu[j])
            dh2 = t if dh2 is None else dh2 + t
        dx, dg3 = _norm_bwd(x1, g3_ref[...], dh2)
        dx1 = dy + dx
        dx1_ref[...] = dx1
        dg3_ref[...] += dg3
        dmix, dg2 = _norm_bwd(mix_ref[...], g2_ref[...], dx1)
        dmix_ref[...] = dmix.astype(BF16)
        dg2_ref[...] += dg2

    slabs = pltpu.VMEM((N_CHIPS, D_MODEL, D_MODEL), BF16)
    gain = _const((1, D_MODEL))
    return pl.pallas_call(
        body, name="mlp_core", grid=(T // tm,),
        in_specs=[_rows(tm, D_MODEL)] * 4 + [gain] * 3 + [ANY] * 4,
        out_specs=[_rows(tm, D_FF), _rows(tm, D_MODEL), _rows(tm, D_FF), _rows(tm, D_MODEL), _rows(tm, D_MODEL),
                   _const((8, LANES)), gain, gain, gain],
        out_shape=[_sds((T, D_FF), BF16), _sds((T, D_MODEL), BF16), _sds((T, D_FF), BF16), _sds((T, D_MODEL), F32),
                   _sds((T, D_MODEL), BF16), _sds((8, LANES), F32)] + [_sds((1, D_MODEL), F32)] * 3,
        scratch_shapes=[slabs] * 2 + [pltpu.VMEM((tm, D_FF), F32), pltpu.SemaphoreType.DMA((4,))],
        compiler_params=_cp("arbitrary"),
    )(h2, x1, mix, tgt, g4, g3, g2, *w_up, *w_down)


def _dw(tag, a, g, ta, tn, shard_cols=False, exchanges=()):
    T, ka = a.shape
    n = g.shape[1]
    tk = min(2 * TM, T)
    nk = T // tk

    def body(a_ref, g_ref, o_ref):
        @pl.when(pl.program_id(2) == 0)
        def _():
            o_ref[...] = jnp.zeros_like(o_ref)

        o_ref[...] += _dot_tn(a_ref[...], g_ref[...])

    if shard_cols:
        per = (n // N_CHIPS) // tn
        out_spec = pl.BlockSpec((None, ta, tn), lambda i, j, k: (j // per, i, j % per))
        out_shape = _sds((N_CHIPS, ka, n // N_CHIPS), F32)
    else:
        out_spec = pl.BlockSpec((ta, tn), lambda i, j, k: (i, j))
        out_shape = _sds((ka, n), F32)
    return _call(
        body, name="dw_" + tag, grid=(ka // ta, n // tn, nk),
        in_specs=[pl.BlockSpec((tk, ta), lambda i, j, k: (k, i)), pl.BlockSpec((tk, tn), lambda i, j, k: (k, j))],
        out_specs=[out_spec], out_shape=[out_shape],
        args=(a, g), sem=("parallel", "parallel", "arbitrary"), exchanges=exchanges)


def _dw_mix(merged, dmix, y_pool, dbp, y_attn, dba, exchanges=()):
    T = merged.shape[0]
    tk = min(2 * TM, T)
    c = D_MODEL // N_CHIPS

    def body(mg_ref, dmix_ref, yp_ref, dbp_ref, ya_ref, dba_ref, out_ref, bp_ref, ba_ref):
        @pl.when(pl.program_id(0) == 0)
        def _():
            for ref in (out_ref, bp_ref, ba_ref):
                ref[...] = jnp.zeros_like(ref)

        out_ref[...] += _dot_tn(mg_ref[...], dmix_ref[...])
        for y_ref, d_ref, o_ref in ((yp_ref, dbp_ref, bp_ref), (ya_ref, dba_ref, ba_ref)):
            res = _dot_tn(y_ref[...], d_ref[...])
            for j in range(N_CHIPS):
                o_ref[j] += res[:, c * j:c * (j + 1)]

    slabs = (N_CHIPS, POOL_WIDTH, c)
    return _call(
        body, name="dw_mix", grid=(T // tk,),
        in_specs=[_rows(tk, D_MODEL), _rows(tk, D_MODEL), _rows(tk, POOL_WIDTH), _rows(tk, D_MODEL),
                  _rows(tk, ATTN_WIDTH), _rows(tk, D_MODEL)],
        out_specs=[_const((D_MODEL, D_MODEL)), _const(slabs), _const(slabs)],
        out_shape=[_sds((D_MODEL, D_MODEL), F32), _sds(slabs, F32), _sds(slabs, F32)],
        args=(merged, dmix, y_pool, dbp, y_attn, dba), sem=("arbitrary",), exchanges=exchanges)


def _merge_bwd(dmix, gate, y_pool, y_attn, w_out, w_bp, w_ba, exchanges=()):
    T = dmix.shape[0]
    tm = min(TM, T)

    def body(dmix_ref, gate_ref, yp_ref, ya_ref, wo_ref, wbp_ref, wba_ref,
             dbp_ref, dba_ref, dgate_ref, dyp_ref, dya_ref):
        dm = _dot_nt(dmix_ref[...], wo_ref[...])
        for j, (y_ref, db_ref, w_ref, dy_ref) in enumerate(
                ((yp_ref, dbp_ref, wbp_ref, dyp_ref), (ya_ref, dba_ref, wba_ref, dya_ref))):
            sl = slice(D_MODEL * j, D_MODEL * (j + 1))
            gt = gate_ref[:, sl].astype(F32)
            db = (dm * gt).astype(BF16)
            db_ref[...] = db
            dgate_ref[:, sl] = (dm * _branch(y_ref[...], w_ref) * gt * (1.0 - gt)).astype(BF16)
            cw = D_MODEL // N_CHIPS
            dy = _dot_nt(db[:, :cw], w_ref[0])
            for c in range(1, N_CHIPS):
                dy = dy + _dot_nt(db[:, cw * c:cw * (c + 1)], w_ref[c])
            dy_ref[...] = dy.astype(dy_ref.dtype)

    return _call(
        body, name="merge_bwd", grid=(T // tm,),
        in_specs=[_rows(tm, D_MODEL), _rows(tm, GATE_WIDTH), _rows(tm, POOL_WIDTH), _rows(tm, ATTN_WIDTH),
                  _const((D_MODEL, D_MODEL)), _const(w_bp.shape), _const(w_ba.shape)],
        out_specs=[_rows(tm, D_MODEL), _rows(tm, D_MODEL), _rows(tm, GATE_WIDTH), _rows(tm, POOL_WIDTH),
                   _rows(tm, ATTN_WIDTH)],
        out_shape=[_sds((T, D_MODEL), BF16), _sds((T, D_MODEL), BF16), _sds((T, GATE_WIDTH), BF16),
                   _sds((T, POOL_WIDTH), F32), _sds((T, ATTN_WIDTH), BF16)],
        args=(dmix, gate, y_pool, y_attn, w_out, w_bp, w_ba), sem=("parallel",), exchanges=exchanges)


def _attn_bwd(q, k, v, do, sinks, tabs, seq, exchanges=()):
    T = q.shape[0]
    nb = seq // BLOCK
    bl = T // seq
    steps = nb + 1

    def body(sink_ref, q_ref, do_ref, kp_ref, kc_ref, vp_ref, vc_ref, c_ref, a_ref, bt_ref, cp_ref, ap_ref, btp_ref,
             dq_ref, dk_ref, dv_ref, dsink_ref, ck_ref, cv_ref):
        n = pl.program_id(0)

        @pl.when(n == 0)
        def _():
            dsink_ref[...] = jnp.zeros_like(dsink_ref)
            ck_ref[...] = jnp.zeros_like(ck_ref)
            cv_ref[...] = jnp.zeros_like(cv_ref)

        @pl.when(n < nb)
        def _():
            valid, lo = _attn_masks(n)
            for b in range(bl):
                kk = jnp.concatenate([kp_ref[b], kc_ref[b]], axis=0)
                vv = jnp.concatenate([vp_ref[b], vc_ref[b]], axis=0)
                dk_acc = jnp.zeros((2 * BLOCK, KV_WIDTH), F32)
                dv_acc = jnp.zeros((2 * BLOCK, KV_WIDTH), F32)
                for h in range(2):
                    qs = _stack_heads(q_ref.at[b], h, lo)
                    dos = _stack_heads(do_ref.at[b], h, lo)
                    pr, ps = _group_probs(qs, kk, valid, _sink_rows(sink_ref, h))
                    dp = _dot_nt(dos, vv)
                    delta = jnp.sum(pr * dp, axis=1, keepdims=True)
                    ds = (pr * (dp - delta)).astype(BF16)
                    dsk = ps * delta
                    for g in range(GROUP):
                        idx = GROUP * h + g
                        dsink_ref[idx:idx + 1, :] += (jnp.zeros((1, LANES), F32)
                                                      - jnp.sum(dsk[BLOCK * g:BLOCK * (g + 1)]))
                    dk_acc = dk_acc + _dot_tn(ds, qs)
                    dv_acc = dv_acc + _dot_tn(pr.astype(BF16), dos)
                    for j, pair in enumerate(_unstack_heads(_dot(ds, kk) * SCALE, h, lo)):
                        sl = slice(LANES * (2 * h + j), LANES * (2 * h + j + 1))
                        dq_ref[b, :, sl] = _rot_bwd(pair, c_ref[...], a_ref[...], bt_ref[...]).astype(BF16)
                fin_k = ck_ref[b] + dk_acc[:BLOCK]
                dk_ref[b] = _rot_bwd(fin_k, cp_ref[...], ap_ref[...], btp_ref[...]).astype(BF16)
                dv_ref[b] = (cv_ref[b] + dv_acc[:BLOCK]).astype(BF16)
                ck_ref[b] = dk_acc[BLOCK:]
                cv_ref[b] = dv_acc[BLOCK:]

        @pl.when(n == nb)
        def _():
            for b in range(bl):
                dk_ref[b] = _rot_bwd(ck_ref[b], cp_ref[...], ap_ref[...], btp_ref[...]).astype(BF16)
                dv_ref[b] = cv_ref[b].astype(BF16)

    cur = lambda n: (0, jnp.minimum(n, nb - 1), 0)
    prv = lambda n: (0, jnp.clip(n - 1, 0, nb - 1), 0)
    tcur = lambda n: (jnp.minimum(n, nb - 1), 0)
    tprv = lambda n: (jnp.clip(n - 1, 0, nb - 1), 0)
    wide = lambda m: pl.BlockSpec((bl, BLOCK, ATTN_WIDTH), m)
    kv = lambda m: pl.BlockSpec((bl, BLOCK, KV_WIDTH), m)
    tab = lambda m: pl.BlockSpec((BLOCK, LANES), m)
    res = _call(
        body, name="attn_bwd", grid=(steps,),
        in_specs=[pl.BlockSpec(memory_space=pltpu.SMEM), wide(cur), wide(cur), kv(prv), kv(cur), kv(prv), kv(cur),
                  tab(tcur), tab(tcur), tab(tcur), tab(tprv), tab(tprv), tab(tprv)],
        out_specs=[wide(cur), kv(prv), kv(prv), _const((8, LANES))],
        out_shape=[_sds((bl, seq, ATTN_WIDTH), BF16), _sds((bl, seq, KV_WIDTH), BF16),
                   _sds((bl, seq, KV_WIDTH), BF16), _sds((8, LANES), F32)],
        scratch=[pltpu.VMEM((bl, BLOCK, KV_WIDTH), F32), pltpu.VMEM((bl, BLOCK, KV_WIDTH), F32)],
        args=(sinks, *_by_example(bl, q, do, k, k, v, v), *tabs, *tabs), sem=("arbitrary",), exchanges=exchanges)
    outs, rest = (res if exchanges else (res, None))
    outs = [outs[0].reshape(T, ATTN_WIDTH), outs[1].reshape(T, KV_WIDTH), outs[2].reshape(T, KV_WIDTH), outs[3]]
    return (outs, rest) if exchanges else outs


def _pool_bwd(dyp, diff, w_pool, pool_scale, seq, exchanges=()):
    T = dyp.shape[0]
    tp = min(TP, seq)
    nseq = seq // tp
    per = tp // HALO
    last_halo = T // HALO - 1

    def body(dy_ref, nxt_ref, diff_ref, w_ref, s_ref, du_ref, dw_ref, ds_ref):
        i = pl.program_id(0)

        @pl.when(i == 0)
        def _():
            dw_ref[...] = jnp.zeros_like(dw_ref)
            ds_ref[...] = jnp.zeros_like(ds_ref)

        last = (i % nseq) == nseq - 1
        nxt = jnp.where(last, 0.0, nxt_ref[...])
        ext = jnp.concatenate([dy_ref[...], nxt], axis=0) * s_ref[...]
        pos = (i % nseq) * tp + lax.broadcasted_iota(jnp.int32, (tp + HALO, 1), 0)
        for gi, w in enumerate(POOL_WINDOWS):
            sl = slice(POOL_GC * gi, POOL_GC * (gi + 1))
            wg = w_ref[gi].astype(BF16)
            dmx = ext[:, sl].astype(BF16)
            ddiff = _dot_nt(dmx, wg)
            s = ddiff * _inv_count(pos, w)
            sh = 1
            while sh < w:
                s = s + pltpu.roll(s, tp + HALO - sh, 0)
                sh *= 2
            du_ref[:, sl] = (s[:tp] - ddiff[:tp]).astype(BF16)
            dg = diff_ref[:, sl]
            dw_ref[gi] += _dot_tn(dg, dmx[:tp])
            ds_ref[:, sl] += jnp.sum(dy_ref[:, sl] * _dot(dg, wg), axis=0, keepdims=True)

    return _call(
        body, name="pool_bwd", grid=(T // tp,),
        in_specs=[_rows(tp, POOL_WIDTH),
                  pl.BlockSpec((HALO, POOL_WIDTH), lambda i: (jnp.minimum((i + 1) * per, last_halo), 0)),
                  _rows(tp, POOL_WIDTH), _const((4, POOL_GC, POOL_GC)), _const((1, POOL_WIDTH))],
        out_specs=[_rows(tp, POOL_WIDTH), _const((4, POOL_GC, POOL_GC)), _const((1, POOL_WIDTH))],
        out_shape=[_sds((T, POOL_WIDTH), BF16), _sds((4, POOL_GC, POOL_GC), F32), _sds((1, POOL_WIDTH), F32)],
        args=(dyp, dyp, diff, w_pool, pool_scale), sem=("arbitrary",), exchanges=exchanges)


_PARTS = ((0, C_Q), (C_Q, C_K), (C_K, C_V), (C_V, C_G), (C_G, IN_WIDTH))


def _inproj_bwd(parts, x2, dx1, w_in_t, g1, exchanges=()):
    T = x2.shape[0]
    tm = min(TM, T)

    def body(du_ref, dq_ref, dk_ref, dv_ref, dgt_ref, x_ref, dx1_ref, w_ref, g_ref, gx_ref, dg_ref):
        @pl.when(pl.program_id(0) == 0)
        def _():
            dg_ref[...] = jnp.zeros_like(dg_ref)

        dh = jnp.zeros((tm, D_MODEL), F32)
        for (lo, hi), p_ref in zip(_PARTS, (du_ref, dq_ref, dk_ref, dv_ref, dgt_ref)):
            dh = dh + _dot(p_ref[...], w_ref[lo:hi, :])
        dx, dg = _norm_bwd(x_ref[...], g_ref[...], dh)
        gx_ref[...] = dx1_ref[...] + dx
        dg_ref[...] += dg

    return _call(
        body, name="inproj_bwd", grid=(T // tm,),
        in_specs=[_rows(tm, hi - lo) for lo, hi in _PARTS]
        + [_rows(tm, D_MODEL), _rows(tm, D_MODEL), _const((IN_WIDTH, D_MODEL)), _const((1, D_MODEL))],
        out_specs=[_rows(tm, D_MODEL), _const((1, D_MODEL))],
        out_shape=[_sds((T, D_MODEL), F32), _sds((1, D_MODEL), F32)],
        args=(*parts, x2, dx1, w_in_t, g1), sem=("arbitrary",), exchanges=exchanges)


def _dw_in(h, parts, exchanges=()):
    T = h.shape[0]
    tk = min(TM, T)

    def body(h_ref, du_ref, dq_ref, dk_ref, dv_ref, dgt_ref, o_ref, db_ref):
        @pl.when(pl.program_id(0) == 0)
        def _():
            o_ref[...] = jnp.zeros_like(o_ref)
            db_ref[...] = jnp.zeros_like(db_ref)

        hh = h_ref[...]
        for (lo, hi), p_ref in zip(_PARTS, (du_ref, dq_ref, dk_ref, dv_ref, dgt_ref)):
            part = p_ref[...]
            o_ref[lo:hi, :] += _dot_tn(part, hh)
            db_ref[:, lo:hi] += jnp.sum(part.astype(F32), axis=0, keepdims=True)

    return _call(
        body, name="dw_in", grid=(T // tk,),
        in_specs=[_rows(tk, D_MODEL)] + [_rows(tk, hi - lo) for lo, hi in _PARTS],
        out_specs=[_const((IN_WIDTH, D_MODEL)), _const((1, IN_WIDTH))],
        out_shape=[_sds((IN_WIDTH, D_MODEL), F32), _sds((1, IN_WIDTH), F32)],
        args=(h, *parts), sem=("arbitrary",), exchanges=exchanges)


def _row_tile(rows, cap=256, mult=16):
    best = None
    for t in range(mult, min(rows, cap) + 1, mult):
        if rows % t == 0:
            best = t
    if best is None:
        raise ValueError("no row tile for %d rows" % rows)
    return best


def _pair_sum(ids, full, got):
    _, r, c = full.shape
    hr = r // 2
    tr = _row_tile(hr)
    nblk = hr // tr

    def body(ids_ref, a_ref, b_ref, own_ref, sb_ref):
        s = a_ref[...] + b_ref[...]
        sb_ref[...] = s.astype(BF16)

        @pl.when(pl.program_id(1) == ids_ref[0])
        def _():
            own_ref[...] = s

    slab = pl.BlockSpec((None, tr, c), lambda i, j, ids_ref: (j, i, 0))
    return pl.pallas_call(
        body, name="pair_sum_%dx%d" % (r, c),
        grid_spec=pltpu.PrefetchScalarGridSpec(
            num_scalar_prefetch=1, grid=(nblk, N_CHIPS),
            in_specs=[pl.BlockSpec((None, tr, c), lambda i, j, ids_ref: (j, ids_ref[1] * nblk + i, 0)), slab],
            out_specs=[pl.BlockSpec((tr, c), lambda i, j, ids_ref: (i, 0)), slab]),
        out_shape=[_sds((hr, c), F32), _sds((N_CHIPS, hr, c), BF16)],
        compiler_params=_cp("parallel", "arbitrary"),
    )(ids, full, got)


def _chip_sum(ids, own, got):
    hr, c = own.shape
    tr = _row_tile(hr)
    nblk = hr // tr

    def body(ids_ref, a_ref, b_ref, o_ref):
        o_ref[...] = ((a_ref[...] + b_ref[0].astype(F32)) + b_ref[1].astype(F32)) + b_ref[2].astype(F32)

    return pl.pallas_call(
        body, name="chip_sum_%dx%d" % (hr, c),
        grid_spec=pltpu.PrefetchScalarGridSpec(
            num_scalar_prefetch=1, grid=(nblk,),
            in_specs=[pl.BlockSpec((tr, c), lambda i, ids_ref: (i, 0)),
                      pl.BlockSpec((3, tr, c), lambda i, ids_ref: (0, i, 0))],
            out_specs=pl.BlockSpec((tr, c), lambda i, ids_ref: (ids_ref[1] * nblk + i, 0))),
        out_shape=_sds((2 * hr, c), F32),
        compiler_params=_cp("parallel"),
    )(ids, own, got)


def _adamw_math(w, g, m, v):
    nm = ADAM_B1 * m + (1.0 - ADAM_B1) * g
    nv = ADAM_B2 * v + (1.0 - ADAM_B2) * (g * g)
    m_hat = nm / (1.0 - ADAM_B1 ** ADAM_STEP)
    v_hat = nv / (1.0 - ADAM_B2 ** ADAM_STEP)
    return -ADAM_LR * (m_hat / (jnp.sqrt(v_hat) + ADAM_EPS) + ADAM_WD * w), nm, nv


def _adamw(w, g, m, v):
    r, c = w.shape
    tr = _row_tile(r, cap=512, mult=8)

    def body(w_ref, g_ref, m_ref, v_ref, d_ref, nm_ref, nv_ref):
        d_ref[...], nm_ref[...], nv_ref[...] = _adamw_math(w_ref[...], g_ref[...], m_ref[...], v_ref[...])

    spec = _rows(tr, c)
    return pl.pallas_call(
        body, name="adamw_%dx%d" % (r, c), grid=(r // tr,),
        in_specs=[spec] * 4, out_specs=[spec] * 3, out_shape=[_sds((r, c), F32)] * 3,
        compiler_params=_cp("parallel"),
    )(w, g, m, v)


SC_TILES = 32
SC_LANES = 16
SC_ROWS = 8


def _adamw_sparse(w, g, m, v):
    r, c = w.shape
    rows = r // SC_TILES
    step = min(rows, SC_ROWS)

    def body(w_hbm, g_hbm, m_hbm, v_hbm, d_hbm, nm_hbm, nv_hbm, wb, gb, mb, vb):
        tile = lax.axis_index("sc_subcore") * 2 + lax.axis_index("sc_core")

        @pl.loop(0, rows, step=step)
        def _(r0):
            mine = pl.ds(tile * rows + r0, step)
            for src, dst in ((w_hbm, wb), (g_hbm, gb), (m_hbm, mb), (v_hbm, vb)):
                pltpu.sync_copy(src.at[mine], dst)

            @pl.loop(0, step)
            def _(row):
                @pl.loop(0, c, step=SC_LANES)
                def _(i):
                    at = (row, pl.ds(i, SC_LANES))
                    wb[at], mb[at], vb[at] = _adamw_math(wb[at], gb[at], mb[at], vb[at])

            for src, dst in ((wb, d_hbm), (mb, nm_hbm), (vb, nv_hbm)):
                pltpu.sync_copy(src, dst.at[mine])

    return pl.kernel(
        body, name="adamw_sparse_%dx%d" % (r, c), out_type=[_sds((r, c), F32)] * 3,
        mesh=plsc.VectorSubcoreMesh(core_axis_name="sc_core", subcore_axis_name="sc_subcore"),
        scratch_types=[pltpu.VMEM((step, c), F32)] * 4,
    )(w, g, m, v)


_SMALL_NAMES = ("w_pool", "b_in", "g_mix_pre", "g_mix_post", "g_mlp_pre", "g_mlp_post", "pool_scale", "attn_sinks")
B_ROWS = -(-IN_WIDTH // D_MODEL)


def _row_block(rows):
    rows = [jnp.pad(r.astype(F32), ((0, 0), (0, D_MODEL - r.shape[1]))) for r in rows]
    return jnp.pad(jnp.concatenate(rows, axis=0), ((0, 8 - len(rows)), (0, 0)))


def _early_block(dg2, dg3, dg4, dps, dsink, loss):
    tail = jnp.concatenate([jnp.pad(dsink.reshape(1, -1), ((0, 0), (0, LANES - dsink.size))),
                            jnp.pad(loss.reshape(1, 1), ((0, 0), (0, LANES - 1)))], axis=1)
    return _row_block([dg2, dg3, dg4, dps, tail])


def _late_block(db_in, dg1):
    b = jnp.pad(db_in, ((0, 0), (0, B_ROWS * D_MODEL - IN_WIDTH))).reshape(B_ROWS, D_MODEL)
    return _row_block([b[r:r + 1] for r in range(B_ROWS)] + [dg1])


def _small_update(gearly, gmat, glate, w, m, v):
    names = _SMALL_NAMES
    n = len(names)

    def total(ref, rows):
        acc = ref[0:rows, :]
        for d in range(1, N_DEV):
            acc = acc + ref[d * rows:(d + 1) * rows, :]
        return acc

    def body(*refs):
        early_ref, gmat_ref, late_ref = refs[:3]
        w_refs, m_refs, v_refs = refs[3:3 + n], refs[3 + n:3 + 2 * n], refs[3 + 2 * n:3 + 3 * n]
        outs = refs[3 + 3 * n:]
        loss_ref, g_refs, d_refs = outs[0], outs[1:1 + n], outs[1 + n:1 + 2 * n]
        nm_refs, nv_refs = outs[1 + 2 * n:1 + 3 * n], outs[1 + 3 * n:1 + 4 * n]
        early, late = total(early_ref, 8), total(late_ref, 8)
        loss_ref[...] = jnp.sum(early[4:5, LANES:2 * LANES], axis=1, keepdims=True)
        bias = jnp.concatenate([late[r:r + 1, :] for r in range(B_ROWS - 1)]
                               + [late[B_ROWS - 1:B_ROWS, :IN_WIDTH - (B_ROWS - 1) * D_MODEL]], axis=1)
        grad = dict(b_in=bias, g_mix_pre=late[B_ROWS:B_ROWS + 1, :], g_mix_post=early[0:1, :],
                    g_mlp_pre=early[1:2, :], g_mlp_post=early[2:3, :], pool_scale=early[3:4, :POOL_WIDTH],
                    attn_sinks=early[4:5, :N_Q_HEADS])
        for i, name in enumerate(names):
            g = total(gmat_ref, 4 * POOL_GC) if name == "w_pool" else grad[name]
            g_refs[i][...] = g
            d_refs[i][...], nm_refs[i][...], nv_refs[i][...] = _adamw_math(
                w_refs[i][...], g, m_refs[i][...], v_refs[i][...])

    shapes = [_sds(w[k].shape, F32) for k in names]
    res = pl.pallas_call(
        body, name="small_update", out_shape=[_sds((1, 1), F32)] + shapes * 4,
        compiler_params=pltpu.CompilerParams(vmem_limit_bytes=VMEM_MB * 1024 * 1024),
    )(gearly, gmat, glate, *[w[k] for k in names], *[m[k] for k in names], *[v[k] for k in names])
    loss = res[0]
    per = {k: tuple(res[1 + j * n + i] for j in range(4)) for i, k in enumerate(names)}
    return loss, per


_BIG = ("w_in", "w_branch_pool", "w_branch_attn", "w_out", "w_up", "w_down")
_ORDER = ("g_mix_pre", "w_in", "b_in", "w_pool", "pool_scale", "attn_sinks", "w_branch_pool", "w_branch_attn",
          "w_out", "g_mix_post", "g_mlp_pre", "w_up", "w_down", "g_mlp_post")


def _stack_rows(slab):
    return slab.reshape(-1, slab.shape[2])


def _step(x2, tgt, seq, shards, small, ids):
    tabs = _rope_tables(seq)
    g1, g2, g3, g4 = (small[n] for n in ("g_mix_pre", "g_mix_post", "g_mlp_pre", "g_mlp_post"))
    sinks = small["attn_sinks"].reshape(N_Q_HEADS)
    w_pool = small["w_pool"].reshape(4, POOL_GC, POOL_GC)
    pool_scale = small["pool_scale"]

    def whole(shard, slabs):
        return lax.dynamic_update_slice(slabs, shard[None], (ids[0], 0, 0))

    up_a, up_b = shards["w_up"][:HALF], shards["w_up"][HALF:]
    down_a, down_b = shards["w_down"][:HALF], shards["w_down"][HALF:]
    w_in = _stack_rows(whole(shards["w_in"], _alone("gather_in", _ex_gather([shards["w_in"]]))[0][0]))
    mix_shards = [shards[n] for n in ("w_branch_pool", "w_branch_attn", "w_out")]
    (h, u, q, k, v, gate), [mix_slabs] = _inproj(
        x2, g1, w_in, small["b_in"], tabs, seq, exchanges=[_ex_gather(mix_shards)])
    w_bp, w_ba, out_slab = (whole(s, g) for s, g in zip(mix_shards, mix_slabs))
    w_out = _stack_rows(out_slab)
    (y_attn, diff, y_pool), [[got_a, got_b]] = _mixers_fwd(
        q, k, v, sinks, u, w_pool, pool_scale, seq, exchanges=[_ex_gather([up_a, up_b])])
    (merged, mix, x1, h2), [[got_c, got_d]] = _merge_out(
        y_pool, y_attn, gate, x2, w_bp, w_ba, w_out, g2, g3, exchanges=[_ex_gather([down_a, down_b])])
    w_up = (whole(up_a, got_a), whole(up_b, got_b))
    w_down = (whole(down_a, got_c), whole(down_b, got_d))
    act, dff, dup, dx1, dmix, loss_acc, dg4, dg3, dg2 = _mlp_core(h2, x1, mix, tgt, w_up, w_down, g4, g3, g2)

    dw_down = _dw("down", act, dff, 1024, 1024)[0].reshape(N_CHIPS, D_FF // N_CHIPS, D_MODEL)
    (dbp, dba, dgate, dyp, dya), [[got]] = _merge_bwd(
        dmix, gate, y_pool, y_attn, w_out, w_bp, w_ba, exchanges=[_ex_pair([dw_down])])
    ps_down = _pair_sum(ids, dw_down, got)
    (dw_up,), [[got]] = _dw("up", h2, dup, 1024, 1024, shard_cols=True, exchanges=[_ex_chip([ps_down[1]])])
    half_down = _chip_sum(ids, ps_down[0], got)
    (dw_out, dw_bp, dw_ba), [[got]] = _dw_mix(merged, dmix, y_pool, dbp, y_attn, dba, exchanges=[_ex_pair([dw_up])])
    ps_up = _pair_sum(ids, dw_up, got)
    dw_mix = [dw_out.reshape(N_CHIPS, D_MODEL // N_CHIPS, D_MODEL), dw_bp, dw_ba]
    (dq, dk, dv, dsink), [[got], gots, [g_down]] = _attn_bwd(
        q, k, v, dya, sinks, tabs, seq, exchanges=[_ex_chip([ps_up[1]]), _ex_pair(dw_mix), _ex_swap([half_down])])
    half_up = _chip_sum(ids, ps_up[0], got)
    ps_mix = [_pair_sum(ids, d, g) for d, g in zip(dw_mix, gots)]
    (du, dw_pool, dps), [[g_up]] = _pool_bwd(dyp, diff, w_pool, pool_scale, seq, exchanges=[_ex_swap([half_up])])
    parts = (du, dq, dk, dv, dgate)
    early = _early_block(dg2, dg3, dg4, dps, dsink[:, 0], loss_acc[0, 0])
    mat = dw_pool.reshape(4 * POOL_GC, POOL_GC)
    (dw_in_t, db_in), [gots, [gearly, gmat]] = _dw_in(
        h, parts, exchanges=[_ex_chip([p[1] for p in ps_mix]), _ex_allgather([early, mat])])
    half_mix = [_chip_sum(ids, p[0], g) for p, g in zip(ps_mix, gots)]
    dw_in = dw_in_t.reshape(N_CHIPS, IN_WIDTH // N_CHIPS, D_MODEL)
    g_mix, [got] = _alone("swap_mix_pair_in", _ex_swap(half_mix), _ex_pair([dw_in]))
    ps_in = _pair_sum(ids, dw_in, got)
    (gx, dg1), [[got]] = _inproj_bwd(parts, x2, dx1, w_in, g1, exchanges=[_ex_chip([ps_in[1]])])
    [g_in], [glate] = _alone("swap_in_allgather", _ex_swap([_chip_sum(ids, ps_in[0], got)]),
                             _ex_allgather([_late_block(db_in, dg1)]))

    grads = dict(w_in=g_in, w_branch_pool=g_mix[1], w_branch_attn=g_mix[2], w_out=g_mix[0], w_up=g_up, w_down=g_down)
    return (gearly, gmat, glate), gx, grads


def kernel(x, g_mix_pre, w_in, b_in, w_pool, pool_scale, attn_sinks, w_branch_pool, w_branch_attn, w_out, g_mix_post, g_mlp_pre, w_up, w_down, g_mlp_post, loss_target, m_g_mix_pre, m_w_in, m_b_in, m_w_pool, m_pool_scale, m_attn_sinks, m_w_branch_pool, m_w_branch_attn, m_w_out, m_g_mix_post, m_g_mlp_pre, m_w_up, m_w_down, m_g_mlp_post, v_g_mix_pre, v_w_in, v_b_in, v_w_pool, v_pool_scale, v_attn_sinks, v_w_branch_pool, v_w_branch_attn, v_w_out, v_g_mix_post, v_g_mlp_pre, v_w_up, v_w_down, v_g_mlp_post):
    weights = dict(g_mix_pre=g_mix_pre, w_in=w_in, b_in=b_in, w_pool=w_pool, pool_scale=pool_scale,
                   attn_sinks=attn_sinks, w_branch_pool=w_branch_pool, w_branch_attn=w_branch_attn, w_out=w_out,
                   g_mix_post=g_mix_post, g_mlp_pre=g_mlp_pre, w_up=w_up, w_down=w_down, g_mlp_post=g_mlp_post)
    mom1 = dict(g_mix_pre=m_g_mix_pre, w_in=m_w_in, b_in=m_b_in, w_pool=m_w_pool, pool_scale=m_pool_scale,
                attn_sinks=m_attn_sinks, w_branch_pool=m_w_branch_pool, w_branch_attn=m_w_branch_attn,
                w_out=m_w_out, g_mix_post=m_g_mix_post, g_mlp_pre=m_g_mlp_pre, w_up=m_w_up, w_down=m_w_down,
                g_mlp_post=m_g_mlp_post)
    mom2 = dict(g_mix_pre=v_g_mix_pre, w_in=v_w_in, b_in=v_b_in, w_pool=v_w_pool, pool_scale=v_pool_scale,
                attn_sinks=v_attn_sinks, w_branch_pool=v_w_branch_pool, w_branch_attn=v_w_branch_attn,
                w_out=v_w_out, g_mix_post=v_g_mix_post, g_mlp_pre=v_g_mlp_pre, w_up=v_w_up, w_down=v_w_down,
                g_mlp_post=v_g_mlp_post)
    b_loc, seq, _ = x.shape
    x2 = x.reshape(b_loc * seq, D_MODEL)
    tgt = loss_target.reshape(b_loc * seq, D_MODEL)
    ids = jnp.stack([2 * lax.axis_index("x") + lax.axis_index("y"), lax.axis_index("c")]).astype(jnp.int32)

    def flat(n, a):
        return a[0].T if n == "w_in" else a[0]

    def unflat(n, a):
        return (a.T if n == "w_in" else a)[None]

    shards = {n: flat(n, weights[n]).astype(BF16) for n in _BIG}
    small = {n: weights[n] for n in _ORDER if n not in _BIG}
    (gearly, gmat, glate), gx, grads = _step(x2, tgt, seq, shards, small, ids)

    def two_d(src):
        return {n: src[n].reshape(4 * POOL_GC, POOL_GC) if n == "w_pool" else src[n] for n in _SMALL_NAMES}

    loss, per = _small_update(gearly, gmat, glate, two_d(weights), two_d(mom1), two_d(mom2))
    delta, new_m, new_v = {}, {}, {}
    for n in _SMALL_NAMES:
        grads[n], delta[n], new_m[n], new_v[n] = (a.reshape(weights[n].shape) for a in per[n])
    for n in _BIG:
        update = _adamw if n == "w_in" else _adamw_sparse
        d, nm, nv = update(flat(n, weights[n]), grads[n], flat(n, mom1[n]), flat(n, mom2[n]))
        grads[n] = unflat(n, grads[n])
        delta[n], new_m[n], new_v[n] = unflat(n, d), unflat(n, nm), unflat(n, nv)

    return (loss[0, 0], gx.reshape(x.shape), *[grads[n] for n in _ORDER], *[delta[n] for n in _ORDER],
            *[new_m[n] for n in _ORDER], *[new_v[n] for n in _ORDER])
```

```python
import jax
import jax.numpy as jnp
from jax import lax
from jax.experimental import pallas as pl
from jax.experimental.pallas import tpu as pltpu
from jax.experimental.pallas import tpu_sc as plsc

F32 = jnp.float32
BF16 = jnp.bfloat16

D_MODEL = 1024
POOL_WINDOWS = (2, 4, 8, 16)
POOL_WIDTH = 512
POOL_GC = 128
HALO = 16
HEAD_DIM = 64
N_Q_HEADS = 8
ATTN_WIDTH = 512
KV_WIDTH = 128
BLOCK = 128
NEG_INF = -1e30
ROPE_THETA = 500000.0
ROT_DIM = 16
GATE_WIDTH = 2048
IN_WIDTH = 3328
D_FF = 4096
EPS = 1e-6
SCALE = HEAD_DIM ** -0.5
C_Q, C_K, C_V, C_G = 512, 1024, 1152, 1280

ADAM_LR, ADAM_B1, ADAM_B2, ADAM_EPS, ADAM_WD, ADAM_STEP = 0.001, 0.9, 0.999, 1e-08, 0.01, 10

N_CHIPS = 4
N_DEV = 8
LOCAL_PARTS = 4
LANES = 128
TM = 512
TP = 512
VMEM_MB = 56

MESH = pl.DeviceIdType.MESH
ANY = pl.BlockSpec(memory_space=pl.ANY)


def _cp(*sem, vmem=VMEM_MB):
    return pltpu.CompilerParams(dimension_semantics=sem, vmem_limit_bytes=vmem * 1024 * 1024)


def _rows(tile, cols):
    return pl.BlockSpec((tile, cols), lambda i: (i, 0))


def _const(shape):
    nd = len(shape)
    return pl.BlockSpec(shape, lambda i: (0,) * nd)


def _sds(shape, dtype):
    return jax.ShapeDtypeStruct(shape, dtype)


def _dot(a, b):
    return jnp.dot(a, b, preferred_element_type=F32)


def _dot_nt(a, b):
    return lax.dot_general(a, b, (((1,), (1,)), ((), ())), preferred_element_type=F32)


def _dot_tn(a, b):
    return lax.dot_general(a, b, (((0,), (0,)), ((), ())), preferred_element_type=F32)


def _rms(x):
    return lax.rsqrt(jnp.mean(x * x, axis=-1, keepdims=True) + EPS)


def _norm_bwd(x, g, dout):
    r = _rms(x)
    n = x * r
    dn = dout * g
    dx = r * (dn - n * jnp.mean(dn * n, axis=-1, keepdims=True))
    return dx, jnp.sum(dout * n, axis=0, keepdims=True)


def _rot_fwd(t, c, a, bt):
    return t * c + pltpu.roll(t, LANES - 8, 1) * a + pltpu.roll(t, 8, 1) * bt


def _rot_bwd(d, c, a, bt):
    return d * c + pltpu.roll(d * a, 8, 1) + pltpu.roll(d * bt, LANES - 8, 1)


def _rope_tables(seq):
    pos = jnp.arange(seq, dtype=F32)
    inv_freq = ROPE_THETA ** (-jnp.arange(0, ROT_DIM, 2, dtype=F32) / ROT_DIM)
    ang = pos[:, None] * inv_freq[None, :]
    cos, sin = jnp.cos(ang), jnp.sin(ang)
    ones = jnp.ones((seq, HEAD_DIM - ROT_DIM), F32)
    zeros8 = jnp.zeros((seq, 8), F32)
    zrest = jnp.zeros((seq, HEAD_DIM - ROT_DIM), F32)
    c = jnp.concatenate([cos, cos, ones], axis=1)
    a = jnp.concatenate([-sin, zeros8, zrest], axis=1)
    bt = jnp.concatenate([zeros8, sin, zrest], axis=1)
    return tuple(jnp.tile(t, (1, 2)) for t in (c, a, bt))


class _Exchange:
    def __init__(self, inputs, out_shapes, sems, start, finish, aliases=None, middle=None):
        self.inputs, self.out_shapes, self.sems = list(inputs), list(out_shapes), list(sems)
        self.start, self.finish, self.aliases = start, finish, dict(aliases or {})
        self.middle = middle


def _call(body, *, name, grid, in_specs, out_specs, out_shape, args, scratch=(), sem=(), exchanges=()):
    in_specs, out_specs, out_shape, scratch = list(in_specs), list(out_specs), list(out_shape), list(scratch)
    if not exchanges:
        return pl.pallas_call(body, name=name, grid=grid, in_specs=in_specs, out_specs=out_specs,
                              out_shape=out_shape, scratch_shapes=scratch, compiler_params=_cp(*sem))(*args)
    n_in, n_out, n_scr = len(in_specs), len(out_specs), len(scratch)
    x_in = [a for ex in exchanges for a in ex.inputs]
    x_out = [s for ex in exchanges for s in ex.out_shapes]
    x_sem = [s for ex in exchanges for s in ex.sems]
    aliases, i_off, o_off = {}, n_in, n_out
    for ex in exchanges:
        for i, o in ex.aliases.items():
            aliases[i_off + i] = o_off + o
        i_off += len(ex.inputs)
        o_off += len(ex.out_shapes)

    def split(flat):
        out, pos = [], 0
        for ex, n in zip(exchanges, flat[1]):
            out.append(flat[0][pos:pos + n])
            pos += n
        return out

    def carrier(*refs):
        pos = 0
        groups = []
        for n in (n_in, len(x_in), n_out, len(x_out), n_scr, len(x_sem)):
            groups.append(refs[pos:pos + n])
            pos += n
        ins, xin, outs, xout, scr, xsem = groups
        xin = split((xin, [len(ex.inputs) for ex in exchanges]))
        xout = split((xout, [len(ex.out_shapes) for ex in exchanges]))
        xsem = split((xsem, [len(ex.sems) for ex in exchanges]))
        first = pl.program_id(0) == 0
        last = pl.program_id(0) == grid[0] - 1
        for d in range(1, len(grid)):
            first = jnp.logical_and(first, pl.program_id(d) == 0)
            last = jnp.logical_and(last, pl.program_id(d) == grid[d] - 1)

        @pl.when(first)
        def _():
            for ex, i, o, s in zip(exchanges, xin, xout, xsem):
                ex.start(i, o, s)

        if any(ex.middle for ex in exchanges):
            half = pl.program_id(0) == grid[0] // 2
            for d in range(1, len(grid)):
                half = jnp.logical_and(half, pl.program_id(d) == 0)

            @pl.when(half)
            def _():
                for ex, i, o, s in zip(exchanges, xin, xout, xsem):
                    if ex.middle:
                        ex.middle(i, o, s)

        body(*ins, *outs, *scr)

        @pl.when(last)
        def _():
            for ex, i, o, s in zip(exchanges, xin, xout, xsem):
                ex.finish(i, o, s)

    res = pl.pallas_call(
        carrier, name=name, grid=grid, in_specs=in_specs + [ANY] * len(x_in),
        out_specs=out_specs + [ANY] * len(x_out), out_shape=out_shape + x_out,
        scratch_shapes=scratch + x_sem, input_output_aliases=aliases,
        compiler_params=_cp(*(["arbitrary"] * len(grid))),
    )(*args, *x_in)
    return res[:n_out], split((res[n_out:], [len(ex.out_shapes) for ex in exchanges]))


def _alone(name, *exchanges):
    n_in = [len(ex.inputs) for ex in exchanges]
    n_out = [len(ex.out_shapes) for ex in exchanges]
    n_sem = [len(ex.sems) for ex in exchanges]
    aliases, i_off, o_off = {}, 0, 0
    for ex in exchanges:
        for i, o in ex.aliases.items():
            aliases[i_off + i] = o_off + o
        i_off += len(ex.inputs)
        o_off += len(ex.out_shapes)

    def split(flat, counts):
        out, pos = [], 0
        for n in counts:
            out.append(flat[pos:pos + n])
            pos += n
        return out

    def body(*refs):
        ins, outs, sems = split(refs, [sum(n_in), sum(n_out), sum(n_sem)])
        groups = list(zip(exchanges, split(ins, n_in), split(outs, n_out), split(sems, n_sem)))
        for ex, i, o, s in groups:
            ex.start(i, o, s)
        for ex, i, o, s in groups:
            if ex.middle:
                ex.middle(i, o, s)
        for ex, i, o, s in groups:
            ex.finish(i, o, s)

    res = pl.pallas_call(
        body, name=name, in_specs=[ANY] * sum(n_in), out_specs=[ANY] * sum(n_out),
        out_shape=[s for ex in exchanges for s in ex.out_shapes],
        scratch_shapes=[s for ex in exchanges for s in ex.sems], input_output_aliases=aliases,
    )(*[a for ex in exchanges for a in ex.inputs])
    return split(res, n_out)


def _place():
    x, y, c = lax.axis_index("x"), lax.axis_index("y"), lax.axis_index("c")
    chips = [(1 - x, y), (x, 1 - y), (1 - x, 1 - y)]
    return x, y, c, chips


def _remote(src, dst, send, recv, to):
    return pltpu.make_async_remote_copy(src_ref=src, dst_ref=dst, send_sem=send, recv_sem=recv,
                                        device_id=to, device_id_type=MESH)


def _ex_gather(shards):
    nw = len(shards)
    hrs = [s.shape[0] // 2 for s in shards]

    def copies(ins, outs, sems):
        s1, r1, s2, r2, fs, fr, ls = sems
        x, y, c, _ = _place()
        me, xn, yn, dg = (x, y), (1 - x, y), (x, 1 - y), (1 - x, 1 - y)
        nbr = (xn, yn)
        sibling = (x, y, 1 - c)

        def piece(w, chip, core, part=None):
            hr = hrs[w]
            rows = pl.ds(core * hr, hr) if part is None else pl.ds(core * hr + part * (hr // 2), hr // 2)
            return outs[w].at[2 * chip[0] + chip[1], rows]

        def first(w, k):
            return _remote(ins[w].at[pl.ds(c * hrs[w], hrs[w])], piece(w, me, c), s1.at[w, k], r1.at[w, k],
                           (*nbr[k], c))

        def landed(w, k):
            return _remote(piece(w, nbr[k], c), piece(w, nbr[k], c), s1.at[w, k], r1.at[w, k], (*nbr[k], c))

        def onward(w, k):
            return _remote(piece(w, nbr[k], c, k), piece(w, nbr[k], c, k), s2.at[w, k], r2.at[w, k],
                           (*nbr[1 - k], c))

        def arrived(w, k):
            return _remote(piece(w, dg, c, k), piece(w, dg, c, k), s2.at[w, k], r2.at[w, k], (*nbr[1 - k], c))

        def passed(w, j):
            chip = (xn, yn, dg)[j]
            return _remote(piece(w, chip, c), piece(w, chip, c), fs.at[w, j], fr.at[w, j], sibling)

        def handed(w, j):
            chip = (xn, yn, dg)[j]
            return _remote(piece(w, chip, 1 - c), piece(w, chip, 1 - c), fs.at[w, j], fr.at[w, j], sibling)

        def own(w, q):
            rows = pl.ds(q * (hrs[w] // 2), hrs[w] // 2)
            return pltpu.make_async_copy(ins[w].at[rows], outs[w].at[2 * x + y, rows], ls.at[w, q])

        return first, landed, onward, arrived, passed, handed, own

    def start(ins, outs, sems):
        first, *_, own = copies(ins, outs, sems)
        for w in range(nw):
            for k in range(2):
                first(w, k).start()
            for q in range(LOCAL_PARTS):
                own(w, q).start()

    def middle(ins, outs, sems):
        _, landed, onward, _, passed, _, _ = copies(ins, outs, sems)
        for w in range(nw):
            for k in range(2):
                landed(w, k).wait_recv()
                onward(w, k).start()
                passed(w, k).start()

    def finish(ins, outs, sems):
        first, _, onward, arrived, passed, handed, own = copies(ins, outs, sems)
        for w in range(nw):
            for k in range(2):
                arrived(w, k).wait_recv()
            passed(w, 2).start()
        for w in range(nw):
            for j in range(3):
                handed(w, j).wait_recv()
        for w in range(nw):
            for k in range(2):
                first(w, k).wait_send()
                onward(w, k).wait_send()
            for j in range(3):
                passed(w, j).wait_send()
            for q in range(LOCAL_PARTS):
                own(w, q).wait()

    return _Exchange(shards, [_sds((N_CHIPS,) + s.shape, s.dtype) for s in shards],
                     [pltpu.SemaphoreType.DMA((nw, 2))] * 4 + [pltpu.SemaphoreType.DMA((nw, 3))] * 2
                     + [pltpu.SemaphoreType.DMA((nw, LOCAL_PARTS))], start, finish, middle=middle)


def _ex_pair(grads):
    nw = len(grads)

    def copies(ins, outs, sems):
        x, y, c, _ = _place()
        out = []
        for w in range(nw):
            hr = grads[w].shape[1] // 2
            out.append(_remote(ins[w].at[:, pl.ds((1 - c) * hr, hr)], outs[w], sems[0].at[w], sems[1].at[w],
                               (x, y, 1 - c)))
        return out

    def start(ins, outs, sems):
        for cp in copies(ins, outs, sems):
            cp.start()

    def finish(ins, outs, sems):
        for cp in copies(ins, outs, sems):
            cp.wait()

    return _Exchange(grads, [_sds((N_CHIPS, g.shape[1] // 2, g.shape[2]), F32) for g in grads],
                     [pltpu.SemaphoreType.DMA((nw,))] * 2, start, finish)


def _ex_chip(pieces):
    nw = len(pieces)

    def copies(ins, outs, sems):
        x, y, c, chips = _place()
        return [_remote(ins[w].at[2 * cx + cy], outs[w].at[k], sems[0].at[w, k], sems[1].at[w, k], (cx, cy, c))
                for w in range(nw) for k, (cx, cy) in enumerate(chips)]

    def start(ins, outs, sems):
        for cp in copies(ins, outs, sems):
            cp.start()

    def finish(ins, outs, sems):
        for cp in copies(ins, outs, sems):
            cp.wait()

    return _Exchange(pieces, [_sds((3,) + p.shape[1:], BF16) for p in pieces],
                     [pltpu.SemaphoreType.DMA((nw, 3))] * 2, start, finish)


def _ex_swap(fulls):
    nw = len(fulls)

    def start(ins, outs, sems):
        x, y, c, _ = _place()
        for w in range(nw):
            hr = fulls[w].shape[0] // 2
            mine = pl.ds(c * hr, hr)
            _remote(ins[w].at[mine], outs[w].at[mine], sems[0].at[w], sems[1].at[w], (x, y, 1 - c)).start()

    def finish(ins, outs, sems):
        x, y, c, _ = _place()
        for w in range(nw):
            hr = fulls[w].shape[0] // 2
            mine, theirs = pl.ds(c * hr, hr), pl.ds((1 - c) * hr, hr)
            _remote(ins[w].at[mine], outs[w].at[mine], sems[0].at[w], sems[1].at[w], (x, y, 1 - c)).wait_send()
            _remote(ins[w].at[theirs], outs[w].at[theirs], sems[0].at[w], sems[1].at[w], (x, y, 1 - c)).wait_recv()

    return _Exchange(fulls, [_sds(f.shape, F32) for f in fulls], [pltpu.SemaphoreType.DMA((nw,))] * 2,
                     start, finish, aliases={w: w for w in range(nw)})


def _ex_allgather(blocks):
    nb = len(blocks)

    def copies(ins, outs, sems):
        send, recv, lsem = sems
        x, y, c, chips = _place()
        me, sibling = (x, y, c), (x, y, 1 - c)

        def rows(b, px, py, pc):
            m_per = blocks[b].shape[0]
            return outs[b].at[pl.ds((4 * px + 2 * py + pc) * m_per, m_per), :]

        def copy(b, k, blk, to, src=None):
            return _remote(rows(b, *blk) if src is None else src, rows(b, *blk), send.at[b, k], recv.at[b, k], to)

        def mine(b):
            return pltpu.make_async_copy(ins[b], rows(b, *me), lsem.at[b])

        def first(b, k):
            return copy(b, k, me, sibling if k == 0 else (*chips[k - 1], c), src=ins[b])

        def passed(b, j):
            return copy(b, 4 + j, (*chips[j], c), sibling)

        def landed(b, j):
            return copy(b, 1 + j, (*chips[j], c), me)

        def handed(b, k):
            return copy(b, 0, sibling, me) if k == 0 else copy(b, 3 + k, (*chips[k - 1], 1 - c), me)

        return mine, first, passed, landed, handed

    def start(ins, outs, sems):
        mine, first, _, _, _ = copies(ins, outs, sems)
        for b in range(nb):
            mine(b).start()
            for k in range(4):
                first(b, k).start()

    def finish(ins, outs, sems):
        mine, first, passed, landed, handed = copies(ins, outs, sems)
        sent = []
        for b in range(nb):
            for j in range(3):
                landed(b, j).wait_recv()
                cp = passed(b, j)
                cp.start()
                sent.append(cp)
        for b in range(nb):
            for k in range(4):
                handed(b, k).wait_recv()
            for k in range(4):
                first(b, k).wait_send()
        for cp in sent:
            cp.wait_send()
        for b in range(nb):
            mine(b).wait()

    return _Exchange(blocks, [_sds((N_DEV * b.shape[0], b.shape[1]), F32) for b in blocks],
                     [pltpu.SemaphoreType.DMA((nb, 7)), pltpu.SemaphoreType.DMA((nb, 7)), pltpu.SemaphoreType.DMA((nb,))],
                     start, finish)


def _inproj(x2, g1, w_in_t, b_in, tabs, seq, exchanges=()):
    T = x2.shape[0]
    tm = min(TM, seq)
    nseq = seq // tm

    def body(x_ref, g_ref, w_ref, b_ref, c_ref, a_ref, bt_ref, h_ref, u_ref, q_ref, k_ref, v_ref, gate_ref):
        x = x_ref[...]
        h = (x * _rms(x) * g_ref[...]).astype(BF16)
        h_ref[...] = h

        def proj(lo, hi):
            return _dot_nt(h, w_ref[lo:hi, :]) + b_ref[:, lo:hi]

        c, a, bt = c_ref[...], a_ref[...], bt_ref[...]
        u_ref[...] = proj(0, C_Q)
        q = proj(C_Q, C_K)
        for p in range(4):
            sl = slice(LANES * p, LANES * (p + 1))
            q_ref[:, sl] = (_rot_fwd(q[:, sl], c, a, bt) * SCALE).astype(BF16)
        kv = proj(C_K, C_G)
        k_ref[...] = _rot_fwd(kv[:, :KV_WIDTH], c, a, bt).astype(BF16)
        v_ref[...] = kv[:, KV_WIDTH:].astype(BF16)
        for j in range(2):
            lo = C_G + D_MODEL * j
            gate_ref[:, D_MODEL * j:D_MODEL * (j + 1)] = jax.nn.sigmoid(proj(lo, lo + D_MODEL)).astype(BF16)

    tab = pl.BlockSpec((tm, LANES), lambda i: (i % nseq, 0))
    return _call(
        body, name="inproj", grid=(T // tm,),
        in_specs=[_rows(tm, D_MODEL), _const((1, D_MODEL)), _const((IN_WIDTH, D_MODEL)), _const((1, IN_WIDTH)),
                  tab, tab, tab],
        out_specs=[_rows(tm, D_MODEL), _rows(tm, POOL_WIDTH), _rows(tm, ATTN_WIDTH), _rows(tm, KV_WIDTH),
                   _rows(tm, KV_WIDTH), _rows(tm, GATE_WIDTH)],
        out_shape=[_sds((T, D_MODEL), BF16), _sds((T, POOL_WIDTH), F32), _sds((T, ATTN_WIDTH), BF16),
                   _sds((T, KV_WIDTH), BF16), _sds((T, KV_WIDTH), BF16), _sds((T, GATE_WIDTH), BF16)],
        args=(x2, g1, w_in_t, b_in, *tabs), sem=("parallel",), exchanges=exchanges)


def _inv_count(pos, w):
    return 1.0 / jnp.minimum(pos + 1, w).astype(F32)


def _pool_tile(i, tp, nseq, u_ref, prev_ref, w_ref, s_ref, diff_ref, y_ref):
    first = (i % nseq) == 0
    prev = jnp.where(first, 0.0, prev_ref[...])
    ext = jnp.concatenate([prev, u_ref[...]], axis=0)
    pos = (i % nseq) * tp + lax.broadcasted_iota(jnp.int32, (tp, 1), 0)
    for gi, w in enumerate(POOL_WINDOWS):
        sl = slice(POOL_GC * gi, POOL_GC * (gi + 1))
        xg = ext[:, sl]
        s = xg
        sh = 1
        while sh < w:
            s = s + pltpu.roll(s, sh, 0)
            sh *= 2
        pooled = s[HALO:] * _inv_count(pos, w)
        diff = (pooled - xg[HALO:]).astype(BF16)
        diff_ref[:, sl] = diff
        mixed = _dot(diff, w_ref[gi].astype(BF16))
        y_ref[:, sl] = (mixed * s_ref[:, sl]).astype(BF16)


def _pool_specs(tp):
    per = tp // HALO
    return [_rows(tp, POOL_WIDTH), pl.BlockSpec((HALO, POOL_WIDTH), lambda i: (jnp.maximum(i * per - 1, 0), 0)),
            _const((4, POOL_GC, POOL_GC)), _const((1, POOL_WIDTH))]


GROUP = 4
GROWS = GROUP * BLOCK


def _attn_masks(n):
    qi = lax.broadcasted_iota(jnp.int32, (GROWS, 2 * BLOCK), 0) % BLOCK
    kj = lax.broadcasted_iota(jnp.int32, (GROWS, 2 * BLOCK), 1)
    rel = qi + BLOCK - kj
    valid = (rel >= 0) & (rel < BLOCK) & (kj >= jnp.where(n > 0, 0, BLOCK))
    lo = lax.broadcasted_iota(jnp.int32, (BLOCK, LANES), 1) < HEAD_DIM
    return valid, lo


def _by_example(bl, *arrays):
    return [a.reshape(bl, a.shape[0] // bl, a.shape[1]) for a in arrays]


def _stack_heads(ref, h, lo):
    keep = lo if h == 0 else jnp.logical_not(lo)
    pieces = []
    for p in (2 * h, 2 * h + 1):
        xp = ref[:, LANES * p:LANES * (p + 1)].astype(F32)
        for e in range(2):
            t = xp if e == h else pltpu.roll(xp, HEAD_DIM, 1)
            pieces.append(jnp.where(keep, t, 0.0).astype(BF16))
    return jnp.concatenate(pieces, axis=0)


def _unstack_heads(stacked, h, lo):
    pairs = []
    for j in range(2):
        parts = []
        for e in range(2):
            t = stacked[BLOCK * (2 * j + e):BLOCK * (2 * j + e + 1)]
            parts.append(t if e == h else pltpu.roll(t, HEAD_DIM, 1))
        pairs.append(jnp.where(lo, parts[0], parts[1]))
    return pairs


def _sink_rows(sink_ref, h):
    head = lax.broadcasted_iota(jnp.int32, (GROWS, 1), 0) // BLOCK
    col = jnp.zeros((GROWS, 1), F32) + sink_ref[GROUP * h]
    for g in range(1, GROUP):
        col = jnp.where(head == g, sink_ref[GROUP * h + g], col)
    return col


def _group_probs(qs, kk, valid, sink):
    s = jnp.where(valid, _dot_nt(qs, kk), NEG_INF)
    m = jnp.maximum(jnp.max(s, axis=1, keepdims=True), sink)
    ex = jnp.exp(s - m)
    es = jnp.exp(sink - m)
    inv = 1.0 / (jnp.sum(ex, axis=1, keepdims=True) + es)
    return ex * inv, es * inv


def _mixers_fwd(q, k, v, sinks, u, w_pool, pool_scale, seq, exchanges=()):
    T = q.shape[0]
    nb = seq // BLOCK
    bl = T // seq
    tp = T // nb
    nseq = seq // tp

    def body(sink_ref, q_ref, kp_ref, kc_ref, vp_ref, vc_ref, u_ref, prev_ref, w_ref, s_ref, o_ref, diff_ref, y_ref):
        n = pl.program_id(0)
        valid, lo = _attn_masks(n)
        for b in range(bl):
            kk = jnp.concatenate([kp_ref[b], kc_ref[b]], axis=0)
            vv = jnp.concatenate([vp_ref[b], vc_ref[b]], axis=0)
            for h in range(2):
                qs = _stack_heads(q_ref.at[b], h, lo)
                pr, _ = _group_probs(qs, kk, valid, _sink_rows(sink_ref, h))
                o = _dot(pr.astype(BF16), vv)
                for j, pair in enumerate(_unstack_heads(o, h, lo)):
                    p = 2 * h + j
                    o_ref[b, :, LANES * p:LANES * (p + 1)] = pair.astype(BF16)
        _pool_tile(n, tp, nseq, u_ref, prev_ref, w_ref, s_ref, diff_ref, y_ref)

    cur = lambda n: (0, n, 0)
    prv = lambda n: (0, jnp.maximum(n - 1, 0), 0)
    kv = lambda m: pl.BlockSpec((bl, BLOCK, KV_WIDTH), m)
    res = _call(
        body, name="mixers_fwd", grid=(nb,),
        in_specs=[pl.BlockSpec(memory_space=pltpu.SMEM), pl.BlockSpec((bl, BLOCK, ATTN_WIDTH), cur),
                  kv(prv), kv(cur), kv(prv), kv(cur)] + _pool_specs(tp),
        out_specs=[pl.BlockSpec((bl, BLOCK, ATTN_WIDTH), cur), _rows(tp, POOL_WIDTH), _rows(tp, POOL_WIDTH)],
        out_shape=[_sds((bl, seq, ATTN_WIDTH), BF16), _sds((T, POOL_WIDTH), BF16), _sds((T, POOL_WIDTH), BF16)],
        args=(sinks, *_by_example(bl, q, k, k, v, v), u, u, w_pool, pool_scale), sem=("parallel",),
        exchanges=exchanges)
    outs, rest = res if exchanges else (res, None)
    return [outs[0].reshape(T, ATTN_WIDTH), outs[1], outs[2]], rest


def _branch(y, w_ref):
    return jnp.concatenate([_dot(y, w_ref[j]) for j in range(N_CHIPS)], axis=1)


def _merge_out(y_pool, y_attn, gate, x2, w_bp, w_ba, w_out, g2, g3, exchanges=()):
    T = x2.shape[0]
    tm = min(TM, T)

    def body(yp_ref, ya_ref, gate_ref, x_ref, wbp_ref, wba_ref, wo_ref, g2_ref, g3_ref,
             mg_ref, mix_ref, x1_ref, h2_ref):
        bp, ba = _branch(yp_ref[...], wbp_ref), _branch(ya_ref[...], wba_ref)
        merged = (gate_ref[:, :D_MODEL].astype(F32) * bp + gate_ref[:, D_MODEL:].astype(F32) * ba).astype(BF16)
        mg_ref[...] = merged
        mix = _dot(merged, wo_ref[...])
        mix_ref[...] = mix
        x1 = x_ref[...] + mix * _rms(mix) * g2_ref[...]
        x1_ref[...] = x1
        h2_ref[...] = (x1 * _rms(x1) * g3_ref[...]).astype(BF16)

    return _call(
        body, name="merge_out", grid=(T // tm,),
        in_specs=[_rows(tm, POOL_WIDTH), _rows(tm, ATTN_WIDTH), _rows(tm, GATE_WIDTH), _rows(tm, D_MODEL),
                  _const(w_bp.shape), _const(w_ba.shape), _const((D_MODEL, D_MODEL)),
                  _const((1, D_MODEL)), _const((1, D_MODEL))],
        out_specs=[_rows(tm, D_MODEL)] * 4,
        out_shape=[_sds((T, D_MODEL), BF16), _sds((T, D_MODEL), F32), _sds((T, D_MODEL), F32),
                   _sds((T, D_MODEL), BF16)],
        args=(y_pool, y_attn, gate, x2, w_bp, w_ba, w_out, g2, g3), sem=("parallel",), exchanges=exchanges)


HALF = D_MODEL // 2
TM_MLP = 256


def _mlp_core(h2, x1, mix, tgt, w_up, w_down, g4, g3, g2):
    T = h2.shape[0]
    tm = min(TM_MLP, T)

    def body(h_ref, x1_ref, mix_ref, t_ref, g_ref, g3_ref, g2_ref, ua_hbm, ub_hbm, da_hbm, db_hbm,
             act_ref, dff_ref, dup_ref, dx1_ref, dmix_ref, loss_ref, dg_ref, dg3_ref, dg2_ref,
             wu, wd, relu_scr, sems):
        def weight_copy(i):
            src, dst = ((ua_hbm, wu.at[:, :HALF]), (ub_hbm, wu.at[:, HALF:]),
                        (da_hbm, wd.at[:, :HALF]), (db_hbm, wd.at[:, HALF:]))[i]
            return pltpu.make_async_copy(src, dst, sems.at[i])

        @pl.when(pl.program_id(0) == 0)
        def _():
            for i in range(4):
                weight_copy(i).start()
            loss_ref[...] = jnp.zeros_like(loss_ref)
            for ref in (dg_ref, dg3_ref, dg2_ref):
                ref[...] = jnp.zeros_like(ref)
            weight_copy(0).wait()
            weight_copy(1).wait()

        h = h_ref[...]
        ff = None
        for j in range(N_CHIPS):
            lo = D_MODEL * j
            relu = jnp.maximum(_dot(h, wu[j]), 0.0)
            if j == 0:
                @pl.when(pl.program_id(0) == 0)
                def _():
                    weight_copy(2).wait()
                    weight_copy(3).wait()
            relu_scr[:, lo:lo + D_MODEL] = relu
            act = jnp.square(relu).astype(BF16)
            act_ref[:, lo:lo + D_MODEL] = act
            t = _dot(act, wd[j])
            ff = t if ff is None else ff + t
        g = g_ref[...]
        x1 = x1_ref[...]
        err = x1 + ff * _rms(ff) * g - t_ref[...]
        loss_ref[...] += jnp.sum(err * err) * (0.5 / D_MODEL)
        dy = err * (1.0 / D_MODEL)
        dff, dg = _norm_bwd(ff, g, dy)
        dg_ref[...] += dg
        dff = dff.astype(BF16)
        dff_ref[...] = dff
        dh2 = None
        for j in range(N_CHIPS):
            lo = D_MODEL * j
            dup = (_dot_nt(dff, wd[j]) * (2.0 * relu_scr[:, lo:lo + D_MODEL])).astype(BF16)
            dup_ref[:, lo:lo + D_MODEL] = dup
            t = _dot_nt(dup, wu[j])
            dh2 = t if dh2 is None else dh2 + t
        dx, dg3 = _norm_bwd(x1, g3_ref[...], dh2)
        dx1 = dy + dx
        dx1_ref[...] = dx1
        dg3_ref[...] += dg3
        dmix, dg2 = _norm_bwd(mix_ref[...], g2_ref[...], dx1)
        dmix_ref[...] = dmix.astype(BF16)
        dg2_ref[...] += dg2

    slabs = pltpu.VMEM((N_CHIPS, D_MODEL, D_MODEL), BF16)
    gain = _const((1, D_MODEL))
    return pl.pallas_call(
        body, name="mlp_core", grid=(T // tm,),
        in_specs=[_rows(tm, D_MODEL)] * 4 + [gain] * 3 + [ANY] * 4,
        out_specs=[_rows(tm, D_FF), _rows(tm, D_MODEL), _rows(tm, D_FF), _rows(tm, D_MODEL), _rows(tm, D_MODEL),
                   _const((8, LANES)), gain, gain, gain],
        out_shape=[_sds((T, D_FF), BF16), _sds((T, D_MODEL), BF16), _sds((T, D_FF), BF16), _sds((T, D_MODEL), F32),
                   _sds((T, D_MODEL), BF16), _sds((8, LANES), F32)] + [_sds((1, D_MODEL), F32)] * 3,
        scratch_shapes=[slabs] * 2 + [pltpu.VMEM((tm, D_FF), F32), pltpu.SemaphoreType.DMA((4,))],
        compiler_params=_cp("arbitrary"),
    )(h2, x1, mix, tgt, g4, g3, g2, *w_up, *w_down)


def _dw(tag, a, g, ta, tn, shard_cols=False, exchanges=()):
    T, ka = a.shape
    n = g.shape[1]
    tk = min(2 * TM, T)
    nk = T // tk

    def body(a_ref, g_ref, o_ref):
        @pl.when(pl.program_id(2) == 0)
        def _():
            o_ref[...] = jnp.zeros_like(o_ref)

        o_ref[...] += _dot_tn(a_ref[...], g_ref[...])

    if shard_cols:
        per = (n // N_CHIPS) // tn
        out_spec = pl.BlockSpec((None, ta, tn), lambda i, j, k: (j // per, i, j % per))
        out_shape = _sds((N_CHIPS, ka, n // N_CHIPS), F32)
    else:
        out_spec = pl.BlockSpec((ta, tn), lambda i, j, k: (i, j))
        out_shape = _sds((ka, n), F32)
    return _call(
        body, name="dw_" + tag, grid=(ka // ta, n // tn, nk),
        in_specs=[pl.BlockSpec((tk, ta), lambda i, j, k: (k, i)), pl.BlockSpec((tk, tn), lambda i, j, k: (k, j))],
        out_specs=[out_spec], out_shape=[out_shape],
        args=(a, g), sem=("parallel", "parallel", "arbitrary"), exchanges=exchanges)


def _dw_mix(merged, dmix, y_pool, dbp, y_attn, dba, exchanges=()):
    T = merged.shape[0]
    tk = min(2 * TM, T)
    c = D_MODEL // N_CHIPS

    def body(mg_ref, dmix_ref, yp_ref, dbp_ref, ya_ref, dba_ref, out_ref, bp_ref, ba_ref):
        @pl.when(pl.program_id(0) == 0)
        def _():
            for ref in (out_ref, bp_ref, ba_ref):
                ref[...] = jnp.zeros_like(ref)

        out_ref[...] += _dot_tn(mg_ref[...], dmix_ref[...])
        for y_ref, d_ref, o_ref in ((yp_ref, dbp_ref, bp_ref), (ya_ref, dba_ref, ba_ref)):
            res = _dot_tn(y_ref[...], d_ref[...])
            for j in range(N_CHIPS):
                o_ref[j] += res[:, c * j:c * (j + 1)]

    slabs = (N_CHIPS, POOL_WIDTH, c)
    return _call(
        body, name="dw_mix", grid=(T // tk,),
        in_specs=[_rows(tk, D_MODEL), _rows(tk, D_MODEL), _rows(tk, POOL_WIDTH), _rows(tk, D_MODEL),
                  _rows(tk, ATTN_WIDTH), _rows(tk, D_MODEL)],
        out_specs=[_const((D_MODEL, D_MODEL)), _const(slabs), _const(slabs)],
        out_shape=[_sds((D_MODEL, D_MODEL), F32), _sds(slabs, F32), _sds(slabs, F32)],
        args=(merged, dmix, y_pool, dbp, y_attn, dba), sem=("arbitrary",), exchanges=exchanges)


def _merge_bwd(dmix, gate, y_pool, y_attn, w_out, w_bp, w_ba, exchanges=()):
    T = dmix.shape[0]
    tm = min(TM, T)

    def body(dmix_ref, gate_ref, yp_ref, ya_ref, wo_ref, wbp_ref, wba_ref,
             dbp_ref, dba_ref, dgate_ref, dyp_ref, dya_ref):
        dm = _dot_nt(dmix_ref[...], wo_ref[...])
        for j, (y_ref, db_ref, w_ref, dy_ref) in enumerate(
                ((yp_ref, dbp_ref, wbp_ref, dyp_ref), (ya_ref, dba_ref, wba_ref, dya_ref))):
            sl = slice(D_MODEL * j, D_MODEL * (j + 1))
            gt = gate_ref[:, sl].astype(F32)
            db = (dm * gt).astype(BF16)
            db_ref[...] = db
            dgate_ref[:, sl] = (dm * _branch(y_ref[...], w_ref) * gt * (1.0 - gt)).astype(BF16)
            cw = D_MODEL // N_CHIPS
            dy = _dot_nt(db[:, :cw], w_ref[0])
            for c in range(1, N_CHIPS):
                dy = dy + _dot_nt(db[:, cw * c:cw * (c + 1)], w_ref[c])
            dy_ref[...] = dy.astype(dy_ref.dtype)

    return _call(
        body, name="merge_bwd", grid=(T // tm,),
        in_specs=[_rows(tm, D_MODEL), _rows(tm, GATE_WIDTH), _rows(tm, POOL_WIDTH), _rows(tm, ATTN_WIDTH),
                  _const((D_MODEL, D_MODEL)), _const(w_bp.shape), _const(w_ba.shape)],
        out_specs=[_rows(tm, D_MODEL), _rows(tm, D_MODEL), _rows(tm, GATE_WIDTH), _rows(tm, POOL_WIDTH),
                   _rows(tm, ATTN_WIDTH)],
        out_shape=[_sds((T, D_MODEL), BF16), _sds((T, D_MODEL), BF16), _sds((T, GATE_WIDTH), BF16),
                   _sds((T, POOL_WIDTH), F32), _sds((T, ATTN_WIDTH), BF16)],
        args=(dmix, gate, y_pool, y_attn, w_out, w_bp, w_ba), sem=("parallel",), exchanges=exchanges)


def _attn_bwd(q, k, v, do, sinks, tabs, seq, exchanges=()):
    T = q.shape[0]
    nb = seq // BLOCK
    bl = T // seq
    steps = nb + 1

    def body(sink_ref, q_ref, do_ref, kp_ref, kc_ref, vp_ref, vc_ref, c_ref, a_ref, bt_ref, cp_ref, ap_ref, btp_ref,
             dq_ref, dk_ref, dv_ref, dsink_ref, ck_ref, cv_ref):
        n = pl.program_id(0)

        @pl.when(n == 0)
        def _():
            dsink_ref[...] = jnp.zeros_like(dsink_ref)
            ck_ref[...] = jnp.zeros_like(ck_ref)
            cv_ref[...] = jnp.zeros_like(cv_ref)

        @pl.when(n < nb)
        def _():
            valid, lo = _attn_masks(n)
            for b in range(bl):
                kk = jnp.concatenate([kp_ref[b], kc_ref[b]], axis=0)
                vv = jnp.concatenate([vp_ref[b], vc_ref[b]], axis=0)
                dk_acc = jnp.zeros((2 * BLOCK, KV_WIDTH), F32)
                dv_acc = jnp.zeros((2 * BLOCK, KV_WIDTH), F32)
                for h in range(2):
                    qs = _stack_heads(q_ref.at[b], h, lo)
                    dos = _stack_heads(do_ref.at[b], h, lo)
                    pr, ps = _group_probs(qs, kk, valid, _sink_rows(sink_ref, h))
                    dp = _dot_nt(dos, vv)
                    delta = jnp.sum(pr * dp, axis=1, keepdims=True)
                    ds = (pr * (dp - delta)).astype(BF16)
                    dsk = ps * delta
                    for g in range(GROUP):
                        idx = GROUP * h + g
                        dsink_ref[idx:idx + 1, :] += (jnp.zeros((1, LANES), F32)
                                                      - jnp.sum(dsk[BLOCK * g:BLOCK * (g + 1)]))
                    dk_acc = dk_acc + _dot_tn(ds, qs)
                    dv_acc = dv_acc + _dot_tn(pr.astype(BF16), dos)
                    for j, pair in enumerate(_unstack_heads(_dot(ds, kk) * SCALE, h, lo)):
                        sl = slice(LANES * (2 * h + j), LANES * (2 * h + j + 1))
                        dq_ref[b, :, sl] = _rot_bwd(pair, c_ref[...], a_ref[...], bt_ref[...]).astype(BF16)
                fin_k = ck_ref[b] + dk_acc[:BLOCK]
                dk_ref[b] = _rot_bwd(fin_k, cp_ref[...], ap_ref[...], btp_ref[...]).astype(BF16)
                dv_ref[b] = (cv_ref[b] + dv_acc[:BLOCK]).astype(BF16)
                ck_ref[b] = dk_acc[BLOCK:]
                cv_ref[b] = dv_acc[BLOCK:]

        @pl.when(n == nb)
        def _():
            for b in range(bl):
                dk_ref[b] = _rot_bwd(ck_ref[b], cp_ref[...], ap_ref[...], btp_ref[...]).astype(BF16)
                dv_ref[b] = cv_ref[b].astype(BF16)

    cur = lambda n: (0, jnp.minimum(n, nb - 1), 0)
    prv = lambda n: (0, jnp.clip(n - 1, 0, nb - 1), 0)
    tcur = lambda n: (jnp.minimum(n, nb - 1), 0)
    tprv = lambda n: (jnp.clip(n - 1, 0, nb - 1), 0)
    wide = lambda m: pl.BlockSpec((bl, BLOCK, ATTN_WIDTH), m)
    kv = lambda m: pl.BlockSpec((bl, BLOCK, KV_WIDTH), m)
    tab = lambda m: pl.BlockSpec((BLOCK, LANES), m)
    res = _call(
        body, name="attn_bwd", grid=(steps,),
        in_specs=[pl.BlockSpec(memory_space=pltpu.SMEM), wide(cur), wide(cur), kv(prv), kv(cur), kv(prv), kv(cur),
                  tab(tcur), tab(tcur), tab(tcur), tab(tprv), tab(tprv), tab(tprv)],
        out_specs=[wide(cur), kv(prv), kv(prv), _const((8, LANES))],
        out_shape=[_sds((bl, seq, ATTN_WIDTH), BF16), _sds((bl, seq, KV_WIDTH), BF16),
                   _sds((bl, seq, KV_WIDTH), BF16), _sds((8, LANES), F32)],
        scratch=[pltpu.VMEM((bl, BLOCK, KV_WIDTH), F32), pltpu.VMEM((bl, BLOCK, KV_WIDTH), F32)],
        args=(sinks, *_by_example(bl, q, do, k, k, v, v), *tabs, *tabs), sem=("arbitrary",), exchanges=exchanges)
    outs, rest = (res if exchanges else (res, None))
    outs = [outs[0].reshape(T, ATTN_WIDTH), outs[1].reshape(T, KV_WIDTH), outs[2].reshape(T, KV_WIDTH), outs[3]]
    return (outs, rest) if exchanges else outs


def _pool_bwd(dyp, diff, w_pool, pool_scale, seq, exchanges=()):
    T = dyp.shape[0]
    tp = min(TP, seq)
    nseq = seq // tp
    per = tp // HALO
    last_halo = T // HALO - 1

    def body(dy_ref, nxt_ref, diff_ref, w_ref, s_ref, du_ref, dw_ref, ds_ref):
        i = pl.program_id(0)

        @pl.when(i == 0)
        def _():
            dw_ref[...] = jnp.zeros_like(dw_ref)
            ds_ref[...] = jnp.zeros_like(ds_ref)

        last = (i % nseq) == nseq - 1
        nxt = jnp.where(last, 0.0, nxt_ref[...])
        ext = jnp.concatenate([dy_ref[...], nxt], axis=0) * s_ref[...]
        pos = (i % nseq) * tp + lax.broadcasted_iota(jnp.int32, (tp + HALO, 1), 0)
        for gi, w in enumerate(POOL_WINDOWS):
            sl = slice(POOL_GC * gi, POOL_GC * (gi + 1))
            wg = w_ref[gi].astype(BF16)
            dmx = ext[:, sl].astype(BF16)
            ddiff = _dot_nt(dmx, wg)
            s = ddiff * _inv_count(pos, w)
            sh = 1
            while sh < w:
                s = s + pltpu.roll(s, tp + HALO - sh, 0)
                sh *= 2
            du_ref[:, sl] = (s[:tp] - ddiff[:tp]).astype(BF16)
            dg = diff_ref[:, sl]
            dw_ref[gi] += _dot_tn(dg, dmx[:tp])
            ds_ref[:, sl] += jnp.sum(dy_ref[:, sl] * _dot(dg, wg), axis=0, keepdims=True)

    return _call(
        body, name="pool_bwd", grid=(T // tp,),
        in_specs=[_rows(tp, POOL_WIDTH),
                  pl.BlockSpec((HALO, POOL_WIDTH), lambda i: (jnp.minimum((i + 1) * per, last_halo), 0)),
                  _rows(tp, POOL_WIDTH), _const((4, POOL_GC, POOL_GC)), _const((1, POOL_WIDTH))],
        out_specs=[_rows(tp, POOL_WIDTH), _const((4, POOL_GC, POOL_GC)), _const((1, POOL_WIDTH))],
        out_shape=[_sds((T, POOL_WIDTH), BF16), _sds((4, POOL_GC, POOL_GC), F32), _sds((1, POOL_WIDTH), F32)],
        args=(dyp, dyp, diff, w_pool, pool_scale), sem=("arbitrary",), exchanges=exchanges)


_PARTS = ((0, C_Q), (C_Q, C_K), (C_K, C_V), (C_V, C_G), (C_G, IN_WIDTH))


def _inproj_bwd(parts, x2, dx1, w_in_t, g1, exchanges=()):
    T = x2.shape[0]
    tm = min(TM, T)

    def body(du_ref, dq_ref, dk_ref, dv_ref, dgt_ref, x_ref, dx1_ref, w_ref, g_ref, gx_ref, dg_ref):
        @pl.when(pl.program_id(0) == 0)
        def _():
            dg_ref[...] = jnp.zeros_like(dg_ref)

        dh = jnp.zeros((tm, D_MODEL), F32)
        for (lo, hi), p_ref in zip(_PARTS, (du_ref, dq_ref, dk_ref, dv_ref, dgt_ref)):
            dh = dh + _dot(p_ref[...], w_ref[lo:hi, :])
        dx, dg = _norm_bwd(x_ref[...], g_ref[...], dh)
        gx_ref[...] = dx1_ref[...] + dx
        dg_ref[...] += dg

    return _call(
        body, name="inproj_bwd", grid=(T // tm,),
        in_specs=[_rows(tm, hi - lo) for lo, hi in _PARTS]
        + [_rows(tm, D_MODEL), _rows(tm, D_MODEL), _const((IN_WIDTH, D_MODEL)), _const((1, D_MODEL))],
        out_specs=[_rows(tm, D_MODEL), _const((1, D_MODEL))],
        out_shape=[_sds((T, D_MODEL), F32), _sds((1, D_MODEL), F32)],
        args=(*parts, x2, dx1, w_in_t, g1), sem=("arbitrary",), exchanges=exchanges)


def _dw_in(h, parts, exchanges=()):
    T = h.shape[0]
    tk = min(TM, T)

    def body(h_ref, du_ref, dq_ref, dk_ref, dv_ref, dgt_ref, o_ref, db_ref):
        @pl.when(pl.program_id(0) == 0)
        def _():
            o_ref[...] = jnp.zeros_like(o_ref)
            db_ref[...] = jnp.zeros_like(db_ref)

        hh = h_ref[...]
        for (lo, hi), p_ref in zip(_PARTS, (du_ref, dq_ref, dk_ref, dv_ref, dgt_ref)):
            part = p_ref[...]
            o_ref[lo:hi, :] += _dot_tn(part, hh)
            db_ref[:, lo:hi] += jnp.sum(part.astype(F32), axis=0, keepdims=True)

    return _call(
        body, name="dw_in", grid=(T // tk,),
        in_specs=[_rows(tk, D_MODEL)] + [_rows(tk, hi - lo) for lo, hi in _PARTS],
        out_specs=[_const((IN_WIDTH, D_MODEL)), _const((1, IN_WIDTH))],
        out_shape=[_sds((IN_WIDTH, D_MODEL), F32), _sds((1, IN_WIDTH), F32)],
        args=(h, *parts), sem=("arbitrary",), exchanges=exchanges)


def _row_tile(rows, cap=256, mult=16):
    best = None
    for t in range(mult, min(rows, cap) + 1, mult):
        if rows % t == 0:
            best = t
    if best is None:
        raise ValueError("no row tile for %d rows" % rows)
    return best


def _pair_sum(ids, full, got):
    _, r, c = full.shape
    hr = r // 2
    tr = _row_tile(hr)
    nblk = hr // tr

    def body(ids_ref, a_ref, b_ref, own_ref, sb_ref):
        s = a_ref[...] + b_ref[...]
        sb_ref[...] = s.astype(BF16)

        @pl.when(pl.program_id(1) == ids_ref[0])
        def _():
            own_ref[...] = s

    slab = pl.BlockSpec((None, tr, c), lambda i, j, ids_ref: (j, i, 0))
    return pl.pallas_call(
        body, name="pair_sum_%dx%d" % (r, c),
        grid_spec=pltpu.PrefetchScalarGridSpec(
            num_scalar_prefetch=1, grid=(nblk, N_CHIPS),
            in_specs=[pl.BlockSpec((None, tr, c), lambda i, j, ids_ref: (j, ids_ref[1] * nblk + i, 0)), slab],
            out_specs=[pl.BlockSpec((tr, c), lambda i, j, ids_ref: (i, 0)), slab]),
        out_shape=[_sds((hr, c), F32), _sds((N_CHIPS, hr, c), BF16)],
        compiler_params=_cp("parallel", "arbitrary"),
    )(ids, full, got)


def _chip_sum(ids, own, got):
    hr, c = own.shape
    tr = _row_tile(hr)
    nblk = hr // tr

    def body(ids_ref, a_ref, b_ref, o_ref):
        o_ref[...] = ((a_ref[...] + b_ref[0].astype(F32)) + b_ref[1].astype(F32)) + b_ref[2].astype(F32)

    return pl.pallas_call(
        body, name="chip_sum_%dx%d" % (hr, c),
        grid_spec=pltpu.PrefetchScalarGridSpec(
            num_scalar_prefetch=1, grid=(nblk,),
            in_specs=[pl.BlockSpec((tr, c), lambda i, ids_ref: (i, 0)),
                      pl.BlockSpec((3, tr, c), lambda i, ids_ref: (0, i, 0))],
            out_specs=pl.BlockSpec((tr, c), lambda i, ids_ref: (ids_ref[1] * nblk + i, 0))),
        out_shape=_sds((2 * hr, c), F32),
        compiler_params=_cp("parallel"),
    )(ids, own, got)


def _adamw_math(w, g, m, v):
    nm = ADAM_B1 * m + (1.0 - ADAM_B1) * g
    nv = ADAM_B2 * v + (1.0 - ADAM_B2) * (g * g)
    m_hat = nm / (1.0 - ADAM_B1 ** ADAM_STEP)
    v_hat = nv / (1.0 - ADAM_B2 ** ADAM_STEP)
    return -ADAM_LR * (m_hat / (jnp.sqrt(v_hat) + ADAM_EPS) + ADAM_WD * w), nm, nv


def _adamw(w, g, m, v):
    r, c = w.shape
    tr = _row_tile(r, cap=512, mult=8)

    def body(w_ref, g_ref, m_ref, v_ref, d_ref, nm_ref, nv_ref):
        d_ref[...], nm_ref[...], nv_ref[...] = _adamw_math(w_ref[...], g_ref[...], m_ref[...], v_ref[...])

    spec = _rows(tr, c)
    return pl.pallas_call(
        body, name="adamw_%dx%d" % (r, c), grid=(r // tr,),
        in_specs=[spec] * 4, out_specs=[spec] * 3, out_shape=[_sds((r, c), F32)] * 3,
        compiler_params=_cp("parallel"),
    )(w, g, m, v)


SC_TILES = 32
SC_LANES = 16
SC_ROWS = 8


def _adamw_sparse(w, g, m, v):
    r, c = w.shape
    rows = r // SC_TILES
    step = min(rows, SC_ROWS)

    def body(w_hbm, g_hbm, m_hbm, v_hbm, d_hbm, nm_hbm, nv_hbm, wb, gb, mb, vb):
        tile = lax.axis_index("sc_subcore") * 2 + lax.axis_index("sc_core")

        @pl.loop(0, rows, step=step)
        def _(r0):
            mine = pl.ds(tile * rows + r0, step)
            for src, dst in ((w_hbm, wb), (g_hbm, gb), (m_hbm, mb), (v_hbm, vb)):
                pltpu.sync_copy(src.at[mine], dst)

            @pl.loop(0, step)
            def _(row):
                @pl.loop(0, c, step=SC_LANES)
                def _(i):
                    at = (row, pl.ds(i, SC_LANES))
                    wb[at], mb[at], vb[at] = _adamw_math(wb[at], gb[at], mb[at], vb[at])

            for src, dst in ((wb, d_hbm), (mb, nm_hbm), (vb, nv_hbm)):
                pltpu.sync_copy(src, dst.at[mine])

    return pl.kernel(
        body, name="adamw_sparse_%dx%d" % (r, c), out_type=[_sds((r, c), F32)] * 3,
        mesh=plsc.VectorSubcoreMesh(core_axis_name="sc_core", subcore_axis_name="sc_subcore"),
        scratch_types=[pltpu.VMEM((step, c), F32)] * 4,
    )(w, g, m, v)


_SMALL_NAMES = ("w_pool", "b_in", "g_mix_pre", "g_mix_post", "g_mlp_pre", "g_mlp_post", "pool_scale", "attn_sinks")
B_ROWS = -(-IN_WIDTH // D_MODEL)


def _row_block(rows):
    rows = [jnp.pad(r.astype(F32), ((0, 0), (0, D_MODEL - r.shape[1]))) for r in rows]
    return jnp.pad(jnp.concatenate(rows, axis=0), ((0, 8 - len(rows)), (0, 0)))


def _early_block(dg2, dg3, dg4, dps, dsink, loss):
    tail = jnp.concatenate([jnp.pad(dsink.reshape(1, -1), ((0, 0), (0, LANES - dsink.size))),
                            jnp.pad(loss.reshape(1, 1), ((0, 0), (0, LANES - 1)))], axis=1)
    return _row_block([dg2, dg3, dg4, dps, tail])


def _late_block(db_in, dg1):
    b = jnp.pad(db_in, ((0, 0), (0, B_ROWS * D_MODEL - IN_WIDTH))).reshape(B_ROWS, D_MODEL)
    return _row_block([b[r:r + 1] for r in range(B_ROWS)] + [dg1])


def _small_update(gearly, gmat, glate, w, m, v):
    names = _SMALL_NAMES
    n = len(names)

    def total(ref, rows):
        acc = ref[0:rows, :]
        for d in range(1, N_DEV):
            acc = acc + ref[d * rows:(d + 1) * rows, :]
        return acc

    def body(*refs):
        early_ref, gmat_ref, late_ref = refs[:3]
        w_refs, m_refs, v_refs = refs[3:3 + n], refs[3 + n:3 + 2 * n], refs[3 + 2 * n:3 + 3 * n]
        outs = refs[3 + 3 * n:]
        loss_ref, g_refs, d_refs = outs[0], outs[1:1 + n], outs[1 + n:1 + 2 * n]
        nm_refs, nv_refs = outs[1 + 2 * n:1 + 3 * n], outs[1 + 3 * n:1 + 4 * n]
        early, late = total(early_ref, 8), total(late_ref, 8)
        loss_ref[...] = jnp.sum(early[4:5, LANES:2 * LANES], axis=1, keepdims=True)
        bias = jnp.concatenate([late[r:r + 1, :] for r in range(B_ROWS - 1)]
                               + [late[B_ROWS - 1:B_ROWS, :IN_WIDTH - (B_ROWS - 1) * D_MODEL]], axis=1)
        grad = dict(b_in=bias, g_mix_pre=late[B_ROWS:B_ROWS + 1, :], g_mix_post=early[0:1, :],
                    g_mlp_pre=early[1:2, :], g_mlp_post=early[2:3, :], pool_scale=early[3:4, :POOL_WIDTH],
                    attn_sinks=early[4:5, :N_Q_HEADS])
        for i, name in enumerate(names):
            g = total(gmat_ref, 4 * POOL_GC) if name == "w_pool" else grad[name]
            g_refs[i][...] = g
            d_refs[i][...], nm_refs[i][...], nv_refs[i][...] = _adamw_math(
                w_refs[i][...], g, m_refs[i][...], v_refs[i][...])

    shapes = [_sds(w[k].shape, F32) for k in names]
    res = pl.pallas_call(
        body, name="small_update", out_shape=[_sds((1, 1), F32)] + shapes * 4,
        compiler_params=pltpu.CompilerParams(vmem_limit_bytes=VMEM_MB * 1024 * 1024),
    )(gearly, gmat, glate, *[w[k] for k in names], *[m[k] for k in names], *[v[k] for k in names])
    loss = res[0]
    per = {k: tuple(res[1 + j * n + i] for j in range(4)) for i, k in enumerate(names)}
    return loss, per


_BIG = ("w_in", "w_branch_pool", "w_branch_attn", "w_out", "w_up", "w_down")
_ORDER = ("g_mix_pre", "w_in", "b_in", "w_pool", "pool_scale", "attn_sinks", "w_branch_pool", "w_branch_attn",
          "w_out", "g_mix_post", "g_mlp_pre", "w_up", "w_down", "g_mlp_post")


def _stack_rows(slab):
    return slab.reshape(-1, slab.shape[2])


def _step(x2, tgt, seq, shards, small, ids):
    tabs = _rope_tables(seq)
    g1, g2, g3, g4 = (small[n] for n in ("g_mix_pre", "g_mix_post", "g_mlp_pre", "g_mlp_post"))
    sinks = small["attn_sinks"].reshape(N_Q_HEADS)
    w_pool = small["w_pool"].reshape(4, POOL_GC, POOL_GC)
    pool_scale = small["pool_scale"]

    up_a, up_b = shards["w_up"][:HALF], shards["w_up"][HALF:]
    down_a, down_b = shards["w_down"][:HALF], shards["w_down"][HALF:]
    w_in = _stack_rows(_alone("gather_in", _ex_gather([shards["w_in"]]))[0][0])
    mix_shards = [shards[n] for n in ("w_branch_pool", "w_branch_attn", "w_out")]
    (h, u, q, k, v, gate), [(w_bp, w_ba, out_slab)] = _inproj(
        x2, g1, w_in, small["b_in"], tabs, seq, exchanges=[_ex_gather(mix_shards)])
    w_out = _stack_rows(out_slab)
    (y_attn, diff, y_pool), [w_up] = _mixers_fwd(
        q, k, v, sinks, u, w_pool, pool_scale, seq, exchanges=[_ex_gather([up_a, up_b])])
    (merged, mix, x1, h2), [w_down] = _merge_out(
        y_pool, y_attn, gate, x2, w_bp, w_ba, w_out, g2, g3, exchanges=[_ex_gather([down_a, down_b])])
    act, dff, dup, dx1, dmix, loss_acc, dg4, dg3, dg2 = _mlp_core(h2, x1, mix, tgt, w_up, w_down, g4, g3, g2)

    dw_down = _dw("down", act, dff, 1024, 1024)[0].reshape(N_CHIPS, D_FF // N_CHIPS, D_MODEL)
    (dbp, dba, dgate, dyp, dya), [[got]] = _merge_bwd(
        dmix, gate, y_pool, y_attn, w_out, w_bp, w_ba, exchanges=[_ex_pair([dw_down])])
    ps_down = _pair_sum(ids, dw_down, got)
    (dw_up,), [[got]] = _dw("up", h2, dup, 1024, 1024, shard_cols=True, exchanges=[_ex_chip([ps_down[1]])])
    half_down = _chip_sum(ids, ps_down[0], got)
    (dw_out, dw_bp, dw_ba), [[got]] = _dw_mix(merged, dmix, y_pool, dbp, y_attn, dba, exchanges=[_ex_pair([dw_up])])
    ps_up = _pair_sum(ids, dw_up, got)
    dw_mix = [dw_out.reshape(N_CHIPS, D_MODEL // N_CHIPS, D_MODEL), dw_bp, dw_ba]
    (dq, dk, dv, dsink), [[got], gots, [g_down]] = _attn_bwd(
        q, k, v, dya, sinks, tabs, seq, exchanges=[_ex_chip([ps_up[1]]), _ex_pair(dw_mix), _ex_swap([half_down])])
    half_up = _chip_sum(ids, ps_up[0], got)
    ps_mix = [_pair_sum(ids, d, g) for d, g in zip(dw_mix, gots)]
    (du, dw_pool, dps), [[g_up]] = _pool_bwd(dyp, diff, w_pool, pool_scale, seq, exchanges=[_ex_swap([half_up])])
    parts = (du, dq, dk, dv, dgate)
    early = _early_block(dg2, dg3, dg4, dps, dsink[:, 0], loss_acc[0, 0])
    mat = dw_pool.reshape(4 * POOL_GC, POOL_GC)
    (dw_in_t, db_in), [gots, [gearly, gmat]] = _dw_in(
        h, parts, exchanges=[_ex_chip([p[1] for p in ps_mix]), _ex_allgather([early, mat])])
    half_mix = [_chip_sum(ids, p[0], g) for p, g in zip(ps_mix, gots)]
    dw_in = dw_in_t.reshape(N_CHIPS, IN_WIDTH // N_CHIPS, D_MODEL)
    g_mix, [got] = _alone("swap_mix_pair_in", _ex_swap(half_mix), _ex_pair([dw_in]))
    ps_in = _pair_sum(ids, dw_in, got)
    (gx, dg1), [[got]] = _inproj_bwd(parts, x2, dx1, w_in, g1, exchanges=[_ex_chip([ps_in[1]])])
    [g_in], [glate] = _alone("swap_in_allgather", _ex_swap([_chip_sum(ids, ps_in[0], got)]),
                             _ex_allgather([_late_block(db_in, dg1)]))

    grads = dict(w_in=g_in, w_branch_pool=g_mix[1], w_branch_attn=g_mix[2], w_out=g_mix[0], w_up=g_up, w_down=g_down)
    return (gearly, gmat, glate), gx, grads


def kernel(x, g_mix_pre, w_in, b_in, w_pool, pool_scale, attn_sinks, w_branch_pool, w_branch_attn, w_out, g_mix_post, g_mlp_pre, w_up, w_down, g_mlp_post, loss_target, m_g_mix_pre, m_w_in, m_b_in, m_w_pool, m_pool_scale, m_attn_sinks, m_w_branch_pool, m_w_branch_attn, m_w_out, m_g_mix_post, m_g_mlp_pre, m_w_up, m_w_down, m_g_mlp_post, v_g_mix_pre, v_w_in, v_b_in, v_w_pool, v_pool_scale, v_attn_sinks, v_w_branch_pool, v_w_branch_attn, v_w_out, v_g_mix_post, v_g_mlp_pre, v_w_up, v_w_down, v_g_mlp_post):
    weights = dict(g_mix_pre=g_mix_pre, w_in=w_in, b_in=b_in, w_pool=w_pool, pool_scale=pool_scale,
                   attn_sinks=attn_sinks, w_branch_pool=w_branch_pool, w_branch_attn=w_branch_attn, w_out=w_out,
                   g_mix_post=g_mix_post, g_mlp_pre=g_mlp_pre, w_up=w_up, w_down=w_down, g_mlp_post=g_mlp_post)
    mom1 = dict(g_mix_pre=m_g_mix_pre, w_in=m_w_in, b_in=m_b_in, w_pool=m_w_pool, pool_scale=m_pool_scale,
                attn_sinks=m_attn_sinks, w_branch_pool=m_w_branch_pool, w_branch_attn=m_w_branch_attn,
                w_out=m_w_out, g_mix_post=m_g_mix_post, g_mlp_pre=m_g_mlp_pre, w_up=m_w_up, w_down=m_w_down,
                g_mlp_post=m_g_mlp_post)
    mom2 = dict(g_mix_pre=v_g_mix_pre, w_in=v_w_in, b_in=v_b_in, w_pool=v_w_pool, pool_scale=v_pool_scale,
                attn_sinks=v_attn_sinks, w_branch_pool=v_w_branch_pool, w_branch_attn=v_w_branch_attn,
                w_out=v_w_out, g_mix_post=v_g_mix_post, g_mlp_pre=v_g_mlp_pre, w_up=v_w_up, w_down=v_w_down,
                g_mlp_post=v_g_mlp_post)
    b_loc, seq, _ = x.shape
    x2 = x.reshape(b_loc * seq, D_MODEL)
    tgt = loss_target.reshape(b_loc * seq, D_MODEL)
    ids = jnp.stack([2 * lax.axis_index("x") + lax.axis_index("y"), lax.axis_index("c")]).astype(jnp.int32)

    def flat(n, a):
        return a[0].T if n == "w_in" else a[0]

    def unflat(n, a):
        return (a.T if n == "w_in" else a)[None]

    shards = {n: flat(n, weights[n]).astype(BF16) for n in _BIG}
    small = {n: weights[n] for n in _ORDER if n not in _BIG}
    (gearly, gmat, glate), gx, grads = _step(x2, tgt, seq, shards, small, ids)

    def two_d(src):
        return {n: src[n].reshape(4 * POOL_GC, POOL_GC) if n == "w_pool" else src[n] for n in _SMALL_NAMES}

    loss, per = _small_update(gearly, gmat, glate, two_d(weights), two_d(mom1), two_d(mom2))
    delta, new_m, new_v = {}, {}, {}
    for n in _SMALL_NAMES:
        grads[n], delta[n], new_m[n], new_v[n] = (a.reshape(weights[n].shape) for a in per[n])
    for n in _BIG:
        update = _adamw if n == "w_in" else _adamw_sparse
        d, nm, nv = update(flat(n, weights[n]), grads[n], flat(n, mom1[n]), flat(n, mom2[n]))
        grads[n] = unflat(n, grads[n])
        delta[n], new_m[n], new_v[n] = unflat(n, d), unflat(n, nm), unflat(n, nv)

    return (loss[0, 0], gx.reshape(x.shape), *[grads[n] for n in _ORDER], *[delta[n] for n in _ORDER],
            *[new_m[n] for n in _ORDER], *[new_v[n] for n in _ORDER])
```

```python
import jax
import jax.numpy as jnp
from jax import lax
from jax.experimental import pallas as pl
from jax.experimental.pallas import tpu as pltpu
from jax.experimental.pallas import tpu_sc as plsc

F32 = jnp.float32
BF16 = jnp.bfloat16

D_MODEL = 1024
POOL_WINDOWS = (2, 4, 8, 16)
POOL_WIDTH = 512
POOL_GC = 128
HALO = 16
HEAD_DIM = 64
N_Q_HEADS = 8
ATTN_WIDTH = 512
KV_WIDTH = 128
BLOCK = 128
NEG_INF = -1e30
ROPE_THETA = 500000.0
ROT_DIM = 16
GATE_WIDTH = 2048
IN_WIDTH = 3328
D_FF = 4096
EPS = 1e-6
SCALE = HEAD_DIM ** -0.5
C_Q, C_K, C_V, C_G = 512, 1024, 1152, 1280

ADAM_LR, ADAM_B1, ADAM_B2, ADAM_EPS, ADAM_WD, ADAM_STEP = 0.001, 0.9, 0.999, 1e-08, 0.01, 10

N_CHIPS = 4
N_DEV = 8
LANES = 128
TM = 512
TP = 512
VMEM_MB = 56

MESH = pl.DeviceIdType.MESH
ANY = pl.BlockSpec(memory_space=pl.ANY)


def _cp(*sem, vmem=VMEM_MB):
    return pltpu.CompilerParams(dimension_semantics=sem, vmem_limit_bytes=vmem * 1024 * 1024)


def _rows(tile, cols):
    return pl.BlockSpec((tile, cols), lambda i: (i, 0))


def _const(shape):
    nd = len(shape)
    return pl.BlockSpec(shape, lambda i: (0,) * nd)


def _sds(shape, dtype):
    return jax.ShapeDtypeStruct(shape, dtype)


def _dot(a, b):
    return jnp.dot(a, b, preferred_element_type=F32)


def _dot_nt(a, b):
    return lax.dot_general(a, b, (((1,), (1,)), ((), ())), preferred_element_type=F32)


def _dot_tn(a, b):
    return lax.dot_general(a, b, (((0,), (0,)), ((), ())), preferred_element_type=F32)


def _rms(x):
    return lax.rsqrt(jnp.mean(x * x, axis=-1, keepdims=True) + EPS)


def _norm_bwd(x, g, dout):
    r = _rms(x)
    n = x * r
    dn = dout * g
    dx = r * (dn - n * jnp.mean(dn * n, axis=-1, keepdims=True))
    return dx, jnp.sum(dout * n, axis=0, keepdims=True)


def _rot_fwd(t, c, a, bt):
    return t * c + pltpu.roll(t, LANES - 8, 1) * a + pltpu.roll(t, 8, 1) * bt


def _rot_bwd(d, c, a, bt):
    return d * c + pltpu.roll(d * a, 8, 1) + pltpu.roll(d * bt, LANES - 8, 1)


def _rope_tables(seq):
    pos = jnp.arange(seq, dtype=F32)
    inv_freq = ROPE_THETA ** (-jnp.arange(0, ROT_DIM, 2, dtype=F32) / ROT_DIM)
    ang = pos[:, None] * inv_freq[None, :]
    cos, sin = jnp.cos(ang), jnp.sin(ang)
    ones = jnp.ones((seq, HEAD_DIM - ROT_DIM), F32)
    zeros8 = jnp.zeros((seq, 8), F32)
    zrest = jnp.zeros((seq, HEAD_DIM - ROT_DIM), F32)
    c = jnp.concatenate([cos, cos, ones], axis=1)
    a = jnp.concatenate([-sin, zeros8, zrest], axis=1)
    bt = jnp.concatenate([zeros8, sin, zrest], axis=1)
    return tuple(jnp.tile(t, (1, 2)) for t in (c, a, bt))


class _Exchange:
    def __init__(self, inputs, out_shapes, sems, start, finish, aliases=None, middle=None):
        self.inputs, self.out_shapes, self.sems = list(inputs), list(out_shapes), list(sems)
        self.start, self.finish, self.aliases = start, finish, dict(aliases or {})
        self.middle = middle


def _call(body, *, name, grid, in_specs, out_specs, out_shape, args, scratch=(), sem=(), exchanges=()):
    in_specs, out_specs, out_shape, scratch = list(in_specs), list(out_specs), list(out_shape), list(scratch)
    if not exchanges:
        return pl.pallas_call(body, name=name, grid=grid, in_specs=in_specs, out_specs=out_specs,
                              out_shape=out_shape, scratch_shapes=scratch, compiler_params=_cp(*sem))(*args)
    n_in, n_out, n_scr = len(in_specs), len(out_specs), len(scratch)
    x_in = [a for ex in exchanges for a in ex.inputs]
    x_out = [s for ex in exchanges for s in ex.out_shapes]
    x_sem = [s for ex in exchanges for s in ex.sems]
    aliases, i_off, o_off = {}, n_in, n_out
    for ex in exchanges:
        for i, o in ex.aliases.items():
            aliases[i_off + i] = o_off + o
        i_off += len(ex.inputs)
        o_off += len(ex.out_shapes)

    def split(flat):
        out, pos = [], 0
        for ex, n in zip(exchanges, flat[1]):
            out.append(flat[0][pos:pos + n])
            pos += n
        return out

    def carrier(*refs):
        pos = 0
        groups = []
        for n in (n_in, len(x_in), n_out, len(x_out), n_scr, len(x_sem)):
            groups.append(refs[pos:pos + n])
            pos += n
        ins, xin, outs, xout, scr, xsem = groups
        xin = split((xin, [len(ex.inputs) for ex in exchanges]))
        xout = split((xout, [len(ex.out_shapes) for ex in exchanges]))
        xsem = split((xsem, [len(ex.sems) for ex in exchanges]))
        first = pl.program_id(0) == 0
        last = pl.program_id(0) == grid[0] - 1
        for d in range(1, len(grid)):
            first = jnp.logical_and(first, pl.program_id(d) == 0)
            last = jnp.logical_and(last, pl.program_id(d) == grid[d] - 1)

        @pl.when(first)
        def _():
            for ex, i, o, s in zip(exchanges, xin, xout, xsem):
                ex.start(i, o, s)

        if any(ex.middle for ex in exchanges):
            half = pl.program_id(0) == 5 * grid[0] // 8
            for d in range(1, len(grid)):
                half = jnp.logical_and(half, pl.program_id(d) == 0)

            @pl.when(half)
            def _():
                for ex, i, o, s in zip(exchanges, xin, xout, xsem):
                    if ex.middle:
                        ex.middle(i, o, s)

        body(*ins, *outs, *scr)

        @pl.when(last)
        def _():
            for ex, i, o, s in zip(exchanges, xin, xout, xsem):
                ex.finish(i, o, s)

    res = pl.pallas_call(
        carrier, name=name, grid=grid, in_specs=in_specs + [ANY] * len(x_in),
        out_specs=out_specs + [ANY] * len(x_out), out_shape=out_shape + x_out,
        scratch_shapes=scratch + x_sem, input_output_aliases=aliases,
        compiler_params=_cp(*(["arbitrary"] * len(grid))),
    )(*args, *x_in)
    return res[:n_out], split((res[n_out:], [len(ex.out_shapes) for ex in exchanges]))


def _alone(name, *exchanges):
    n_in = [len(ex.inputs) for ex in exchanges]
    n_out = [len(ex.out_shapes) for ex in exchanges]
    n_sem = [len(ex.sems) for ex in exchanges]
    aliases, i_off, o_off = {}, 0, 0
    for ex in exchanges:
        for i, o in ex.aliases.items():
            aliases[i_off + i] = o_off + o
        i_off += len(ex.inputs)
        o_off += len(ex.out_shapes)

    def split(flat, counts):
        out, pos = [], 0
        for n in counts:
            out.append(flat[pos:pos + n])
            pos += n
        return out

    def body(*refs):
        ins, outs, sems = split(refs, [sum(n_in), sum(n_out), sum(n_sem)])
        groups = list(zip(exchanges, split(ins, n_in), split(outs, n_out), split(sems, n_sem)))
        for ex, i, o, s in groups:
            ex.start(i, o, s)
        for ex, i, o, s in groups:
            if ex.middle:
                ex.middle(i, o, s)
        for ex, i, o, s in groups:
            ex.finish(i, o, s)

    res = pl.pallas_call(
        body, name=name, in_specs=[ANY] * sum(n_in), out_specs=[ANY] * sum(n_out),
        out_shape=[s for ex in exchanges for s in ex.out_shapes],
        scratch_shapes=[s for ex in exchanges for s in ex.sems], input_output_aliases=aliases,
    )(*[a for ex in exchanges for a in ex.inputs])
    return split(res, n_out)


def _place():
    x, y, c = lax.axis_index("x"), lax.axis_index("y"), lax.axis_index("c")
    chips = [(1 - x, y), (x, 1 - y), (1 - x, 1 - y)]
    return x, y, c, chips


def _remote(src, dst, send, recv, to):
    return pltpu.make_async_remote_copy(src_ref=src, dst_ref=dst, send_sem=send, recv_sem=recv,
                                        device_id=to, device_id_type=MESH)


def _ex_gather(shards):
    nw = len(shards)
    hrs = [s.shape[0] // 2 for s in shards]

    def copies(ins, outs, sems):
        s1, r1, s2, r2, fs, fr = sems
        x, y, c, _ = _place()
        me, xn, yn, dg = (x, y), (1 - x, y), (x, 1 - y), (1 - x, 1 - y)
        nbr = (xn, yn)
        sibling = (x, y, 1 - c)

        def piece(w, chip, core, part=None):
            hr = hrs[w]
            rows = pl.ds(core * hr, hr) if part is None else pl.ds(core * hr + part * (hr // 2), hr // 2)
            return outs[w].at[2 * chip[0] + chip[1], rows]

        def first(w, k):
            return _remote(ins[w].at[pl.ds(c * hrs[w], hrs[w])], piece(w, me, c), s1.at[w, k], r1.at[w, k],
                           (*nbr[k], c))

        def landed(w, k):
            return _remote(piece(w, nbr[k], c), piece(w, nbr[k], c), s1.at[w, k], r1.at[w, k], (*nbr[k], c))

        def onward(w, k):
            return _remote(piece(w, nbr[k], c, k), piece(w, nbr[k], c, k), s2.at[w, k], r2.at[w, k],
                           (*nbr[1 - k], c))

        def arrived(w, k):
            return _remote(piece(w, dg, c, k), piece(w, dg, c, k), s2.at[w, k], r2.at[w, k], (*nbr[1 - k], c))

        def passed(w, j):
            chip = (xn, yn, dg)[j]
            return _remote(piece(w, chip, c), piece(w, chip, c), fs.at[w, j], fr.at[w, j], sibling)

        def handed(w, j):
            chip = (xn, yn, dg)[j]
            return _remote(piece(w, chip, 1 - c), piece(w, chip, 1 - c), fs.at[w, j], fr.at[w, j], sibling)

        return first, landed, onward, arrived, passed, handed

    def start(ins, outs, sems):
        first = copies(ins, outs, sems)[0]
        for w in range(nw):
            for k in range(2):
                first(w, k).start()

    def middle(ins, outs, sems):
        _, landed, onward, _, passed, _ = copies(ins, outs, sems)
        for w in range(nw):
            for k in range(2):
                landed(w, k).wait_recv()
                onward(w, k).start()
                passed(w, k).start()

    def finish(ins, outs, sems):
        first, _, onward, arrived, passed, handed = copies(ins, outs, sems)
        for w in range(nw):
            for k in range(2):
                arrived(w, k).wait_recv()
            passed(w, 2).start()
        for w in range(nw):
            for j in range(3):
                handed(w, j).wait_recv()
        for w in range(nw):
            for k in range(2):
                first(w, k).wait_send()
                onward(w, k).wait_send()
            for j in range(3):
                passed(w, j).wait_send()

    return _Exchange(shards, [_sds((N_CHIPS,) + s.shape, s.dtype) for s in shards],
                     [pltpu.SemaphoreType.DMA((nw, 2))] * 4 + [pltpu.SemaphoreType.DMA((nw, 3))] * 2,
                     start, finish, middle=middle)


def _ex_pair(grads):
    nw = len(grads)

    def copies(ins, outs, sems):
        x, y, c, _ = _place()
        out = []
        for w in range(nw):
            hr = grads[w].shape[1] // 2
            out.append(_remote(ins[w].at[:, pl.ds((1 - c) * hr, hr)], outs[w], sems[0].at[w], sems[1].at[w],
                               (x, y, 1 - c)))
        return out

    def start(ins, outs, sems):
        for cp in copies(ins, outs, sems):
            cp.start()

    def finish(ins, outs, sems):
        for cp in copies(ins, outs, sems):
            cp.wait()

    return _Exchange(grads, [_sds((N_CHIPS, g.shape[1] // 2, g.shape[2]), F32) for g in grads],
                     [pltpu.SemaphoreType.DMA((nw,))] * 2, start, finish)


def _ex_chip(pieces):
    nw = len(pieces)

    def copies(ins, outs, sems):
        x, y, c, chips = _place()
        return [_remote(ins[w].at[2 * cx + cy], outs[w].at[k], sems[0].at[w, k], sems[1].at[w, k], (cx, cy, c))
                for w in range(nw) for k, (cx, cy) in enumerate(chips)]

    def start(ins, outs, sems):
        for cp in copies(ins, outs, sems):
            cp.start()

    def finish(ins, outs, sems):
        for cp in copies(ins, outs, sems):
            cp.wait()

    return _Exchange(pieces, [_sds((3,) + p.shape[1:], BF16) for p in pieces],
                     [pltpu.SemaphoreType.DMA((nw, 3))] * 2, start, finish)


def _ex_swap(fulls):
    nw = len(fulls)

    def start(ins, outs, sems):
        x, y, c, _ = _place()
        for w in range(nw):
            hr = fulls[w].shape[0] // 2
            mine = pl.ds(c * hr, hr)
            _remote(ins[w].at[mine], outs[w].at[mine], sems[0].at[w], sems[1].at[w], (x, y, 1 - c)).start()

    def finish(ins, outs, sems):
        x, y, c, _ = _place()
        for w in range(nw):
            hr = fulls[w].shape[0] // 2
            mine, theirs = pl.ds(c * hr, hr), pl.ds((1 - c) * hr, hr)
            _remote(ins[w].at[mine], outs[w].at[mine], sems[0].at[w], sems[1].at[w], (x, y, 1 - c)).wait_send()
            _remote(ins[w].at[theirs], outs[w].at[theirs], sems[0].at[w], sems[1].at[w], (x, y, 1 - c)).wait_recv()

    return _Exchange(fulls, [_sds(f.shape, F32) for f in fulls], [pltpu.SemaphoreType.DMA((nw,))] * 2,
                     start, finish, aliases={w: w for w in range(nw)})


def _ex_allgather(blocks):
    nb = len(blocks)

    def copies(ins, outs, sems):
        send, recv, lsem = sems
        x, y, c, chips = _place()
        me, sibling = (x, y, c), (x, y, 1 - c)

        def rows(b, px, py, pc):
            m_per = blocks[b].shape[0]
            return outs[b].at[pl.ds((4 * px + 2 * py + pc) * m_per, m_per), :]

        def copy(b, k, blk, to, src=None):
            return _remote(rows(b, *blk) if src is None else src, rows(b, *blk), send.at[b, k], recv.at[b, k], to)

        def mine(b):
            return pltpu.make_async_copy(ins[b], rows(b, *me), lsem.at[b])

        def first(b, k):
            return copy(b, k, me, sibling if k == 0 else (*chips[k - 1], c), src=ins[b])

        def passed(b, j):
            return copy(b, 4 + j, (*chips[j], c), sibling)

        def landed(b, j):
            return copy(b, 1 + j, (*chips[j], c), me)

        def handed(b, k):
            return copy(b, 0, sibling, me) if k == 0 else copy(b, 3 + k, (*chips[k - 1], 1 - c), me)

        return mine, first, passed, landed, handed

    def start(ins, outs, sems):
        mine, first, _, _, _ = copies(ins, outs, sems)
        for b in range(nb):
            mine(b).start()
            for k in range(4):
                first(b, k).start()

    def finish(ins, outs, sems):
        mine, first, passed, landed, handed = copies(ins, outs, sems)
        sent = []
        for b in range(nb):
            for j in range(3):
                landed(b, j).wait_recv()
                cp = passed(b, j)
                cp.start()
                sent.append(cp)
        for b in range(nb):
            for k in range(4):
                handed(b, k).wait_recv()
            for k in range(4):
                first(b, k).wait_send()
        for cp in sent:
            cp.wait_send()
        for b in range(nb):
            mine(b).wait()

    return _Exchange(blocks, [_sds((N_DEV * b.shape[0], b.shape[1]), F32) for b in blocks],
                     [pltpu.SemaphoreType.DMA((nb, 7)), pltpu.SemaphoreType.DMA((nb, 7)), pltpu.SemaphoreType.DMA((nb,))],
                     start, finish)


def _inproj(x2, g1, w_in_t, b_in, tabs, seq, exchanges=()):
    T = x2.shape[0]
    tm = min(TM, seq)
    nseq = seq // tm

    def body(x_ref, g_ref, w_ref, b_ref, c_ref, a_ref, bt_ref, h_ref, u_ref, q_ref, k_ref, v_ref, gate_ref):
        x = x_ref[...]
        h = (x * _rms(x) * g_ref[...]).astype(BF16)
        h_ref[...] = h

        def proj(lo, hi):
            return _dot_nt(h, w_ref[lo:hi, :]) + b_ref[:, lo:hi]

        c, a, bt = c_ref[...], a_ref[...], bt_ref[...]
        u_ref[...] = proj(0, C_Q)
        q = proj(C_Q, C_K)
        for p in range(4):
            sl = slice(LANES * p, LANES * (p + 1))
            q_ref[:, sl] = (_rot_fwd(q[:, sl], c, a, bt) * SCALE).astype(BF16)
        kv = proj(C_K, C_G)
        k_ref[...] = _rot_fwd(kv[:, :KV_WIDTH], c, a, bt).astype(BF16)
        v_ref[...] = kv[:, KV_WIDTH:].astype(BF16)
        for j in range(2):
            lo = C_G + D_MODEL * j
            gate_ref[:, D_MODEL * j:D_MODEL * (j + 1)] = jax.nn.sigmoid(proj(lo, lo + D_MODEL)).astype(BF16)

    tab = pl.BlockSpec((tm, LANES), lambda i: (i % nseq, 0))
    return _call(
        body, name="inproj", grid=(T // tm,),
        in_specs=[_rows(tm, D_MODEL), _const((1, D_MODEL)), _const((IN_WIDTH, D_MODEL)), _const((1, IN_WIDTH)),
                  tab, tab, tab],
        out_specs=[_rows(tm, D_MODEL), _rows(tm, POOL_WIDTH), _rows(tm, ATTN_WIDTH), _rows(tm, KV_WIDTH),
                   _rows(tm, KV_WIDTH), _rows(tm, GATE_WIDTH)],
        out_shape=[_sds((T, D_MODEL), BF16), _sds((T, POOL_WIDTH), F32), _sds((T, ATTN_WIDTH), BF16),
                   _sds((T, KV_WIDTH), BF16), _sds((T, KV_WIDTH), BF16), _sds((T, GATE_WIDTH), BF16)],
        args=(x2, g1, w_in_t, b_in, *tabs), sem=("parallel",), exchanges=exchanges)


def _inv_count(pos, w):
    return 1.0 / jnp.minimum(pos + 1, w).astype(F32)


def _pool_tile(i, tp, nseq, u_ref, prev_ref, w_ref, s_ref, diff_ref, y_ref):
    first = (i % nseq) == 0
    prev = jnp.where(first, 0.0, prev_ref[...])
    ext = jnp.concatenate([prev, u_ref[...]], axis=0)
    pos = (i % nseq) * tp + lax.broadcasted_iota(jnp.int32, (tp, 1), 0)
    for gi, w in enumerate(POOL_WINDOWS):
        sl = slice(POOL_GC * gi, POOL_GC * (gi + 1))
        xg = ext[:, sl]
        s = xg
        sh = 1
        while sh < w:
            s = s + pltpu.roll(s, sh, 0)
            sh *= 2
        pooled = s[HALO:] * _inv_count(pos, w)
        diff = (pooled - xg[HALO:]).astype(BF16)
        diff_ref[:, sl] = diff
        mixed = _dot(diff, w_ref[gi].astype(BF16))
        y_ref[:, sl] = (mixed * s_ref[:, sl]).astype(BF16)


def _pool_specs(tp):
    per = tp // HALO
    return [_rows(tp, POOL_WIDTH), pl.BlockSpec((HALO, POOL_WIDTH), lambda i: (jnp.maximum(i * per - 1, 0), 0)),
            _const((4, POOL_GC, POOL_GC)), _const((1, POOL_WIDTH))]


GROUP = 4
GROWS = GROUP * BLOCK


def _attn_masks(n):
    qi = lax.broadcasted_iota(jnp.int32, (GROWS, 2 * BLOCK), 0) % BLOCK
    kj = lax.broadcasted_iota(jnp.int32, (GROWS, 2 * BLOCK), 1)
    rel = qi + BLOCK - kj
    valid = (rel >= 0) & (rel < BLOCK) & (kj >= jnp.where(n > 0, 0, BLOCK))
    lo = lax.broadcasted_iota(jnp.int32, (BLOCK, LANES), 1) < HEAD_DIM
    return valid, lo


def _by_example(bl, *arrays):
    return [a.reshape(bl, a.shape[0] // bl, a.shape[1]) for a in arrays]


def _stack_heads(ref, h, lo):
    keep = lo if h == 0 else jnp.logical_not(lo)
    pieces = []
    for p in (2 * h, 2 * h + 1):
        xp = ref[:, LANES * p:LANES * (p + 1)].astype(F32)
        for e in range(2):
            t = xp if e == h else pltpu.roll(xp, HEAD_DIM, 1)
            pieces.append(jnp.where(keep, t, 0.0).astype(BF16))
    return jnp.concatenate(pieces, axis=0)


def _unstack_heads(stacked, h, lo):
    pairs = []
    for j in range(2):
        parts = []
        for e in range(2):
            t = stacked[BLOCK * (2 * j + e):BLOCK * (2 * j + e + 1)]
            parts.append(t if e == h else pltpu.roll(t, HEAD_DIM, 1))
        pairs.append(jnp.where(lo, parts[0], parts[1]))
    return pairs


def _sink_rows(sink_ref, h):
    head = lax.broadcasted_iota(jnp.int32, (GROWS, 1), 0) // BLOCK
    col = jnp.zeros((GROWS, 1), F32) + sink_ref[GROUP * h]
    for g in range(1, GROUP):
        col = jnp.where(head == g, sink_ref[GROUP * h + g], col)
    return col


def _group_probs(qs, kk, valid, sink):
    s = jnp.where(valid, _dot_nt(qs, kk), NEG_INF)
    m = jnp.maximum(jnp.max(s, axis=1, keepdims=True), sink)
    ex = jnp.exp(s - m)
    es = jnp.exp(sink - m)
    inv = 1.0 / (jnp.sum(ex, axis=1, keepdims=True) + es)
    return ex * inv, es * inv


def _mixers_fwd(q, k, v, sinks, u, w_pool, pool_scale, seq, exchanges=()):
    T = q.shape[0]
    nb = seq // BLOCK
    bl = T // seq
    tp = T // nb
    nseq = seq // tp

    def body(sink_ref, q_ref, kp_ref, kc_ref, vp_ref, vc_ref, u_ref, prev_ref, w_ref, s_ref, o_ref, diff_ref, y_ref):
        n = pl.program_id(0)
        valid, lo = _attn_masks(n)
        for b in range(bl):
            kk = jnp.concatenate([kp_ref[b], kc_ref[b]], axis=0)
            vv = jnp.concatenate([vp_ref[b], vc_ref[b]], axis=0)
            for h in range(2):
                qs = _stack_heads(q_ref.at[b], h, lo)
                pr, _ = _group_probs(qs, kk, valid, _sink_rows(sink_ref, h))
                o = _dot(pr.astype(BF16), vv)
                for j, pair in enumerate(_unstack_heads(o, h, lo)):
                    p = 2 * h + j
                    o_ref[b, :, LANES * p:LANES * (p + 1)] = pair.astype(BF16)
        _pool_tile(n, tp, nseq, u_ref, prev_ref, w_ref, s_ref, diff_ref, y_ref)

    cur = lambda n: (0, n, 0)
    prv = lambda n: (0, jnp.maximum(n - 1, 0), 0)
    kv = lambda m: pl.BlockSpec((bl, BLOCK, KV_WIDTH), m)
    res = _call(
        body, name="mixers_fwd", grid=(nb,),
        in_specs=[pl.BlockSpec(memory_space=pltpu.SMEM), pl.BlockSpec((bl, BLOCK, ATTN_WIDTH), cur),
                  kv(prv), kv(cur), kv(prv), kv(cur)] + _pool_specs(tp),
        out_specs=[pl.BlockSpec((bl, BLOCK, ATTN_WIDTH), cur), _rows(tp, POOL_WIDTH), _rows(tp, POOL_WIDTH)],
        out_shape=[_sds((bl, seq, ATTN_WIDTH), BF16), _sds((T, POOL_WIDTH), BF16), _sds((T, POOL_WIDTH), BF16)],
        args=(sinks, *_by_example(bl, q, k, k, v, v), u, u, w_pool, pool_scale), sem=("parallel",),
        exchanges=exchanges)
    outs, rest = res if exchanges else (res, None)
    return [outs[0].reshape(T, ATTN_WIDTH), outs[1], outs[2]], rest


def _branch(y, w_ref):
    return jnp.concatenate([_dot(y, w_ref[j]) for j in range(N_CHIPS)], axis=1)


def _merge_out(y_pool, y_attn, gate, x2, w_bp, w_ba, w_out, g2, g3, exchanges=()):
    T = x2.shape[0]
    tm = min(TM, T)

    def body(yp_ref, ya_ref, gate_ref, x_ref, wbp_ref, wba_ref, wo_ref, g2_ref, g3_ref,
             mg_ref, mix_ref, x1_ref, h2_ref):
        bp, ba = _branch(yp_ref[...], wbp_ref), _branch(ya_ref[...], wba_ref)
        merged = (gate_ref[:, :D_MODEL].astype(F32) * bp + gate_ref[:, D_MODEL:].astype(F32) * ba).astype(BF16)
        mg_ref[...] = merged
        mix = _dot(merged, wo_ref[...])
        mix_ref[...] = mix
        x1 = x_ref[...] + mix * _rms(mix) * g2_ref[...]
        x1_ref[...] = x1
        h2_ref[...] = (x1 * _rms(x1) * g3_ref[...]).astype(BF16)

    return _call(
        body, name="merge_out", grid=(T // tm,),
        in_specs=[_rows(tm, POOL_WIDTH), _rows(tm, ATTN_WIDTH), _rows(tm, GATE_WIDTH), _rows(tm, D_MODEL),
                  _const(w_bp.shape), _const(w_ba.shape), _const((D_MODEL, D_MODEL)),
                  _const((1, D_MODEL)), _const((1, D_MODEL))],
        out_specs=[_rows(tm, D_MODEL)] * 4,
        out_shape=[_sds((T, D_MODEL), BF16), _sds((T, D_MODEL), F32), _sds((T, D_MODEL), F32),
                   _sds((T, D_MODEL), BF16)],
        args=(y_pool, y_attn, gate, x2, w_bp, w_ba, w_out, g2, g3), sem=("parallel",), exchanges=exchanges)


HALF = D_MODEL // 2
TM_MLP = 256


def _mlp_core(h2, x1, mix, tgt, w_up, w_down, g4, g3, g2):
    T = h2.shape[0]
    tm = min(TM_MLP, T)

    def body(h_ref, x1_ref, mix_ref, t_ref, g_ref, g3_ref, g2_ref, ua_hbm, ub_hbm, da_hbm, db_hbm,
             act_ref, dff_ref, dup_ref, dx1_ref, dmix_ref, loss_ref, dg_ref, dg3_ref, dg2_ref,
             wu, wd, relu_scr, sems):
        def weight_copy(i):
            src, dst = ((ua_hbm, wu.at[:, :HALF]), (ub_hbm, wu.at[:, HALF:]),
                        (da_hbm, wd.at[:, :HALF]), (db_hbm, wd.at[:, HALF:]))[i]
            return pltpu.make_async_copy(src, dst, sems.at[i])

        @pl.when(pl.program_id(0) == 0)
        def _():
            for i in range(4):
                weight_copy(i).start()
            loss_ref[...] = jnp.zeros_like(loss_ref)
            for ref in (dg_ref, dg3_ref, dg2_ref):
                ref[...] = jnp.zeros_like(ref)
            weight_copy(0).wait()
            weight_copy(1).wait()

        h = h_ref[...]
        ff = None
        for j in range(N_CHIPS):
            lo = D_MODEL * j
            relu = jnp.maximum(_dot(h, wu[j]), 0.0)
            if j == 0:
                @pl.when(pl.program_id(0) == 0)
                def _():
                    weight_copy(2).wait()
                    weight_copy(3).wait()
            relu_scr[:, lo:lo + D_MODEL] = relu
            act = jnp.square(relu).astype(BF16)
            act_ref[:, lo:lo + D_MODEL] = act
            t = _dot(act, wd[j])
            ff = t if ff is None else ff + t
        g = g_ref[...]
        x1 = x1_ref[...]
        err = x1 + ff * _rms(ff) * g - t_ref[...]
        loss_ref[...] += jnp.sum(err * err) * (0.5 / D_MODEL)
        dy = err * (1.0 / D_MODEL)
        dff, dg = _norm_bwd(ff, g, dy)
        dg_ref[...] += dg
        dff = dff.astype(BF16)
        dff_ref[...] = dff
        dh2 = None
        for j in range(N_CHIPS):
            lo = D_MODEL * j
            dup = (_dot_nt(dff, wd[j]) * (2.0 * relu_scr[:, lo:lo + D_MODEL])).astype(BF16)
            dup_ref[:, lo:lo + D_MODEL] = dup
            t = _dot_nt(dup, wu[j])
            dh2 = t if dh2 is None else dh2 + t
        dx, dg3 = _norm_bwd(x1, g3_ref[...], dh2)
        dx1 = dy + dx
        dx1_ref[...] = dx1
        dg3_ref[...] += dg3
        dmix, dg2 = _norm_bwd(mix_ref[...], g2_ref[...], dx1)
        dmix_ref[...] = dmix.astype(BF16)
        dg2_ref[...] += dg2

    slabs = pltpu.VMEM((N_CHIPS, D_MODEL, D_MODEL), BF16)
    gain = _const((1, D_MODEL))
    return pl.pallas_call(
        body, name="mlp_core", grid=(T // tm,),
        in_specs=[_rows(tm, D_MODEL)] * 4 + [gain] * 3 + [ANY] * 4,
        out_specs=[_rows(tm, D_FF), _rows(tm, D_MODEL), _rows(tm, D_FF), _rows(tm, D_MODEL), _rows(tm, D_MODEL),
                   _const((8, LANES)), gain, gain, gain],
        out_shape=[_sds((T, D_FF), BF16), _sds((T, D_MODEL), BF16), _sds((T, D_FF), BF16), _sds((T, D_MODEL), F32),
                   _sds((T, D_MODEL), BF16), _sds((8, LANES), F32)] + [_sds((1, D_MODEL), F32)] * 3,
        scratch_shapes=[slabs] * 2 + [pltpu.VMEM((tm, D_FF), F32), pltpu.SemaphoreType.DMA((4,))],
        compiler_params=_cp("arbitrary"),
    )(h2, x1, mix, tgt, g4, g3, g2, *w_up, *w_down)


def _dw(tag, a, g, ta, tn, shard_cols=False, exchanges=()):
    T, ka = a.shape
    n = g.shape[1]
    tk = min(2 * TM, T)
    nk = T // tk

    def body(a_ref, g_ref, o_ref):
        @pl.when(pl.program_id(2) == 0)
        def _():
            o_ref[...] = jnp.zeros_like(o_ref)

        o_ref[...] += _dot_tn(a_ref[...], g_ref[...])

    if shard_cols:
        per = (n // N_CHIPS) // tn
        out_spec = pl.BlockSpec((None, ta, tn), lambda i, j, k: (j // per, i, j % per))
        out_shape = _sds((N_CHIPS, ka, n // N_CHIPS), F32)
    else:
        out_spec = pl.BlockSpec((ta, tn), lambda i, j, k: (i, j))
        out_shape = _sds((ka, n), F32)
    return _call(
        body, name="dw_" + tag, grid=(ka // ta, n // tn, nk),
        in_specs=[pl.BlockSpec((tk, ta), lambda i, j, k: (k, i)), pl.BlockSpec((tk, tn), lambda i, j, k: (k, j))],
        out_specs=[out_spec], out_shape=[out_shape],
        args=(a, g), sem=("parallel", "parallel", "arbitrary"), exchanges=exchanges)


def _dw_mix(merged, dmix, y_pool, dbp, y_attn, dba, exchanges=()):
    T = merged.shape[0]
    tk = min(2 * TM, T)
    c = D_MODEL // N_CHIPS

    def body(mg_ref, dmix_ref, yp_ref, dbp_ref, ya_ref, dba_ref, out_ref, bp_ref, ba_ref):
        @pl.when(pl.program_id(0) == 0)
        def _():
            for ref in (out_ref, bp_ref, ba_ref):
                ref[...] = jnp.zeros_like(ref)

        out_ref[...] += _dot_tn(mg_ref[...], dmix_ref[...])
        for y_ref, d_ref, o_ref in ((yp_ref, dbp_ref, bp_ref), (ya_ref, dba_ref, ba_ref)):
            res = _dot_tn(y_ref[...], d_ref[...])
            for j in range(N_CHIPS):
                o_ref[j] += res[:, c * j:c * (j + 1)]

    slabs = (N_CHIPS, POOL_WIDTH, c)
    return _call(
        body, name="dw_mix", grid=(T // tk,),
        in_specs=[_rows(tk, D_MODEL), _rows(tk, D_MODEL), _rows(tk, POOL_WIDTH), _rows(tk, D_MODEL),
                  _rows(tk, ATTN_WIDTH), _rows(tk, D_MODEL)],
        out_specs=[_const((D_MODEL, D_MODEL)), _const(slabs), _const(slabs)],
        out_shape=[_sds((D_MODEL, D_MODEL), F32), _sds(slabs, F32), _sds(slabs, F32)],
        args=(merged, dmix, y_pool, dbp, y_attn, dba), sem=("arbitrary",), exchanges=exchanges)


def _merge_bwd(dmix, gate, y_pool, y_attn, w_out, w_bp, w_ba, exchanges=()):
    T = dmix.shape[0]
    tm = min(TM, T)

    def body(dmix_ref, gate_ref, yp_ref, ya_ref, wo_ref, wbp_ref, wba_ref,
             dbp_ref, dba_ref, dgate_ref, dyp_ref, dya_ref):
        dm = _dot_nt(dmix_ref[...], wo_ref[...])
        for j, (y_ref, db_ref, w_ref, dy_ref) in enumerate(
                ((yp_ref, dbp_ref, wbp_ref, dyp_ref), (ya_ref, dba_ref, wba_ref, dya_ref))):
            sl = slice(D_MODEL * j, D_MODEL * (j + 1))
            gt = gate_ref[:, sl].astype(F32)
            db = (dm * gt).astype(BF16)
            db_ref[...] = db
            dgate_ref[:, sl] = (dm * _branch(y_ref[...], w_ref) * gt * (1.0 - gt)).astype(BF16)
            cw = D_MODEL // N_CHIPS
            dy = _dot_nt(db[:, :cw], w_ref[0])
            for c in range(1, N_CHIPS):
                dy = dy + _dot_nt(db[:, cw * c:cw * (c + 1)], w_ref[c])
            dy_ref[...] = dy.astype(dy_ref.dtype)

    return _call(
        body, name="merge_bwd", grid=(T // tm,),
        in_specs=[_rows(tm, D_MODEL), _rows(tm, GATE_WIDTH), _rows(tm, POOL_WIDTH), _rows(tm, ATTN_WIDTH),
                  _const((D_MODEL, D_MODEL)), _const(w_bp.shape), _const(w_ba.shape)],
        out_specs=[_rows(tm, D_MODEL), _rows(tm, D_MODEL), _rows(tm, GATE_WIDTH), _rows(tm, POOL_WIDTH),
                   _rows(tm, ATTN_WIDTH)],
        out_shape=[_sds((T, D_MODEL), BF16), _sds((T, D_MODEL), BF16), _sds((T, GATE_WIDTH), BF16),
                   _sds((T, POOL_WIDTH), F32), _sds((T, ATTN_WIDTH), BF16)],
        args=(dmix, gate, y_pool, y_attn, w_out, w_bp, w_ba), sem=("parallel",), exchanges=exchanges)


def _attn_bwd(q, k, v, do, sinks, tabs, seq, exchanges=()):
    T = q.shape[0]
    nb = seq // BLOCK
    bl = T // seq
    steps = nb + 1

    def body(sink_ref, q_ref, do_ref, kp_ref, kc_ref, vp_ref, vc_ref, c_ref, a_ref, bt_ref, cp_ref, ap_ref, btp_ref,
             dq_ref, dk_ref, dv_ref, dsink_ref, ck_ref, cv_ref):
        n = pl.program_id(0)

        @pl.when(n == 0)
        def _():
            dsink_ref[...] = jnp.zeros_like(dsink_ref)
            ck_ref[...] = jnp.zeros_like(ck_ref)
            cv_ref[...] = jnp.zeros_like(cv_ref)

        @pl.when(n < nb)
        def _():
            valid, lo = _attn_masks(n)
            for b in range(bl):
                kk = jnp.concatenate([kp_ref[b], kc_ref[b]], axis=0)
                vv = jnp.concatenate([vp_ref[b], vc_ref[b]], axis=0)
                dk_acc = jnp.zeros((2 * BLOCK, KV_WIDTH), F32)
                dv_acc = jnp.zeros((2 * BLOCK, KV_WIDTH), F32)
                for h in range(2):
                    qs = _stack_heads(q_ref.at[b], h, lo)
                    dos = _stack_heads(do_ref.at[b], h, lo)
                    pr, ps = _group_probs(qs, kk, valid, _sink_rows(sink_ref, h))
                    dp = _dot_nt(dos, vv)
                    delta = jnp.sum(pr * dp, axis=1, keepdims=True)
                    ds = (pr * (dp - delta)).astype(BF16)
                    dsk = ps * delta
                    for g in range(GROUP):
                        idx = GROUP * h + g
                        dsink_ref[idx:idx + 1, :] += (jnp.zeros((1, LANES), F32)
                                                      - jnp.sum(dsk[BLOCK * g:BLOCK * (g + 1)]))
                    dk_acc = dk_acc + _dot_tn(ds, qs)
                    dv_acc = dv_acc + _dot_tn(pr.astype(BF16), dos)
                    for j, pair in enumerate(_unstack_heads(_dot(ds, kk) * SCALE, h, lo)):
                        sl = slice(LANES * (2 * h + j), LANES * (2 * h + j + 1))
                        dq_ref[b, :, sl] = _rot_bwd(pair, c_ref[...], a_ref[...], bt_ref[...]).astype(BF16)
                fin_k = ck_ref[b] + dk_acc[:BLOCK]
                dk_ref[b] = _rot_bwd(fin_k, cp_ref[...], ap_ref[...], btp_ref[...]).astype(BF16)
                dv_ref[b] = (cv_ref[b] + dv_acc[:BLOCK]).astype(BF16)
                ck_ref[b] = dk_acc[BLOCK:]
                cv_ref[b] = dv_acc[BLOCK:]

        @pl.when(n == nb)
        def _():
            for b in range(bl):
                dk_ref[b] = _rot_bwd(ck_ref[b], cp_ref[...], ap_ref[...], btp_ref[...]).astype(BF16)
                dv_ref[b] = cv_ref[b].astype(BF16)

    cur = lambda n: (0, jnp.minimum(n, nb - 1), 0)
    prv = lambda n: (0, jnp.clip(n - 1, 0, nb - 1), 0)
    tcur = lambda n: (jnp.minimum(n, nb - 1), 0)
    tprv = lambda n: (jnp.clip(n - 1, 0, nb - 1), 0)
    wide = lambda m: pl.BlockSpec((bl, BLOCK, ATTN_WIDTH), m)
    kv = lambda m: pl.BlockSpec((bl, BLOCK, KV_WIDTH), m)
    tab = lambda m: pl.BlockSpec((BLOCK, LANES), m)
    res = _call(
        body, name="attn_bwd", grid=(steps,),
        in_specs=[pl.BlockSpec(memory_space=pltpu.SMEM), wide(cur), wide(cur), kv(prv), kv(cur), kv(prv), kv(cur),
                  tab(tcur), tab(tcur), tab(tcur), tab(tprv), tab(tprv), tab(tprv)],
        out_specs=[wide(cur), kv(prv), kv(prv), _const((8, LANES))],
        out_shape=[_sds((bl, seq, ATTN_WIDTH), BF16), _sds((bl, seq, KV_WIDTH), BF16),
                   _sds((bl, seq, KV_WIDTH), BF16), _sds((8, LANES), F32)],
        scratch=[pltpu.VMEM((bl, BLOCK, KV_WIDTH), F32), pltpu.VMEM((bl, BLOCK, KV_WIDTH), F32)],
        args=(sinks, *_by_example(bl, q, do, k, k, v, v), *tabs, *tabs), sem=("arbitrary",), exchanges=exchanges)
    outs, rest = (res if exchanges else (res, None))
    outs = [outs[0].reshape(T, ATTN_WIDTH), outs[1].reshape(T, KV_WIDTH), outs[2].reshape(T, KV_WIDTH), outs[3]]
    return (outs, rest) if exchanges else outs


def _pool_bwd(dyp, diff, w_pool, pool_scale, seq, exchanges=()):
    T = dyp.shape[0]
    tp = min(TP, seq)
    nseq = seq // tp
    per = tp // HALO
    last_halo = T // HALO - 1

    def body(dy_ref, nxt_ref, diff_ref, w_ref, s_ref, du_ref, dw_ref, ds_ref):
        i = pl.program_id(0)

        @pl.when(i == 0)
        def _():
            dw_ref[...] = jnp.zeros_like(dw_ref)
            ds_ref[...] = jnp.zeros_like(ds_ref)

        last = (i % nseq) == nseq - 1
        nxt = jnp.where(last, 0.0, nxt_ref[...])
        ext = jnp.concatenate([dy_ref[...], nxt], axis=0) * s_ref[...]
        pos = (i % nseq) * tp + lax.broadcasted_iota(jnp.int32, (tp + HALO, 1), 0)
        for gi, w in enumerate(POOL_WINDOWS):
            sl = slice(POOL_GC * gi, POOL_GC * (gi + 1))
            wg = w_ref[gi].astype(BF16)
            dmx = ext[:, sl].astype(BF16)
            ddiff = _dot_nt(dmx, wg)
            s = ddiff * _inv_count(pos, w)
            sh = 1
            while sh < w:
                s = s + pltpu.roll(s, tp + HALO - sh, 0)
                sh *= 2
            du_ref[:, sl] = (s[:tp] - ddiff[:tp]).astype(BF16)
            dg = diff_ref[:, sl]
            dw_ref[gi] += _dot_tn(dg, dmx[:tp])
            ds_ref[:, sl] += jnp.sum(dy_ref[:, sl] * _dot(dg, wg), axis=0, keepdims=True)

    return _call(
        body, name="pool_bwd", grid=(T // tp,),
        in_specs=[_rows(tp, POOL_WIDTH),
                  pl.BlockSpec((HALO, POOL_WIDTH), lambda i: (jnp.minimum((i + 1) * per, last_halo), 0)),
                  _rows(tp, POOL_WIDTH), _const((4, POOL_GC, POOL_GC)), _const((1, POOL_WIDTH))],
        out_specs=[_rows(tp, POOL_WIDTH), _const((4, POOL_GC, POOL_GC)), _const((1, POOL_WIDTH))],
        out_shape=[_sds((T, POOL_WIDTH), BF16), _sds((4, POOL_GC, POOL_GC), F32), _sds((1, POOL_WIDTH), F32)],
        args=(dyp, dyp, diff, w_pool, pool_scale), sem=("arbitrary",), exchanges=exchanges)


_PARTS = ((0, C_Q), (C_Q, C_K), (C_K, C_V), (C_V, C_G), (C_G, IN_WIDTH))


def _inproj_bwd(parts, x2, dx1, w_in_t, g1, exchanges=()):
    T = x2.shape[0]
    tm = min(TM, T)

    def body(du_ref, dq_ref, dk_ref, dv_ref, dgt_ref, x_ref, dx1_ref, w_ref, g_ref, gx_ref, dg_ref):
        @pl.when(pl.program_id(0) == 0)
        def _():
            dg_ref[...] = jnp.zeros_like(dg_ref)

        dh = jnp.zeros((tm, D_MODEL), F32)
        for (lo, hi), p_ref in zip(_PARTS, (du_ref, dq_ref, dk_ref, dv_ref, dgt_ref)):
            dh = dh + _dot(p_ref[...], w_ref[lo:hi, :])
        dx, dg = _norm_bwd(x_ref[...], g_ref[...], dh)
        gx_ref[...] = dx1_ref[...] + dx
        dg_ref[...] += dg

    return _call(
        body, name="inproj_bwd", grid=(T // tm,),
        in_specs=[_rows(tm, hi - lo) for lo, hi in _PARTS]
        + [_rows(tm, D_MODEL), _rows(tm, D_MODEL), _const((IN_WIDTH, D_MODEL)), _const((1, D_MODEL))],
        out_specs=[_rows(tm, D_MODEL), _const((1, D_MODEL))],
        out_shape=[_sds((T, D_MODEL), F32), _sds((1, D_MODEL), F32)],
        args=(*parts, x2, dx1, w_in_t, g1), sem=("arbitrary",), exchanges=exchanges)


def _dw_in(h, parts, exchanges=()):
    T = h.shape[0]
    tk = min(TM, T)

    def body(h_ref, du_ref, dq_ref, dk_ref, dv_ref, dgt_ref, o_ref, db_ref):
        @pl.when(pl.program_id(0) == 0)
        def _():
            o_ref[...] = jnp.zeros_like(o_ref)
            db_ref[...] = jnp.zeros_like(db_ref)

        hh = h_ref[...]
        for (lo, hi), p_ref in zip(_PARTS, (du_ref, dq_ref, dk_ref, dv_ref, dgt_ref)):
            part = p_ref[...]
            o_ref[lo:hi, :] += _dot_tn(part, hh)
            db_ref[:, lo:hi] += jnp.sum(part.astype(F32), axis=0, keepdims=True)

    return _call(
        body, name="dw_in", grid=(T // tk,),
        in_specs=[_rows(tk, D_MODEL)] + [_rows(tk, hi - lo) for lo, hi in _PARTS],
        out_specs=[_const((IN_WIDTH, D_MODEL)), _const((1, IN_WIDTH))],
        out_shape=[_sds((IN_WIDTH, D_MODEL), F32), _sds((1, IN_WIDTH), F32)],
        args=(h, *parts), sem=("arbitrary",), exchanges=exchanges)


def _row_tile(rows, cap=256, mult=16):
    best = None
    for t in range(mult, min(rows, cap) + 1, mult):
        if rows % t == 0:
            best = t
    if best is None:
        raise ValueError("no row tile for %d rows" % rows)
    return best


def _pair_sum(ids, full, got):
    _, r, c = full.shape
    hr = r // 2
    tr = _row_tile(hr)
    nblk = hr // tr

    def body(ids_ref, a_ref, b_ref, own_ref, sb_ref):
        s = a_ref[...] + b_ref[...]
        sb_ref[...] = s.astype(BF16)

        @pl.when(pl.program_id(1) == ids_ref[0])
        def _():
            own_ref[...] = s

    slab = pl.BlockSpec((None, tr, c), lambda i, j, ids_ref: (j, i, 0))
    return pl.pallas_call(
        body, name="pair_sum_%dx%d" % (r, c),
        grid_spec=pltpu.PrefetchScalarGridSpec(
            num_scalar_prefetch=1, grid=(nblk, N_CHIPS),
            in_specs=[pl.BlockSpec((None, tr, c), lambda i, j, ids_ref: (j, ids_ref[1] * nblk + i, 0)), slab],
            out_specs=[pl.BlockSpec((tr, c), lambda i, j, ids_ref: (i, 0)), slab]),
        out_shape=[_sds((hr, c), F32), _sds((N_CHIPS, hr, c), BF16)],
        compiler_params=_cp("parallel", "arbitrary"),
    )(ids, full, got)


def _chip_sum(ids, own, got):
    hr, c = own.shape
    tr = _row_tile(hr)
    nblk = hr // tr

    def body(ids_ref, a_ref, b_ref, o_ref):
        o_ref[...] = ((a_ref[...] + b_ref[0].astype(F32)) + b_ref[1].astype(F32)) + b_ref[2].astype(F32)

    return pl.pallas_call(
        body, name="chip_sum_%dx%d" % (hr, c),
        grid_spec=pltpu.PrefetchScalarGridSpec(
            num_scalar_prefetch=1, grid=(nblk,),
            in_specs=[pl.BlockSpec((tr, c), lambda i, ids_ref: (i, 0)),
                      pl.BlockSpec((3, tr, c), lambda i, ids_ref: (0, i, 0))],
            out_specs=pl.BlockSpec((tr, c), lambda i, ids_ref: (ids_ref[1] * nblk + i, 0))),
        out_shape=_sds((2 * hr, c), F32),
        compiler_params=_cp("parallel"),
    )(ids, own, got)


def _adamw_math(w, g, m, v):
    nm = ADAM_B1 * m + (1.0 - ADAM_B1) * g
    nv = ADAM_B2 * v + (1.0 - ADAM_B2) * (g * g)
    m_hat = nm / (1.0 - ADAM_B1 ** ADAM_STEP)
    v_hat = nv / (1.0 - ADAM_B2 ** ADAM_STEP)
    return -ADAM_LR * (m_hat / (jnp.sqrt(v_hat) + ADAM_EPS) + ADAM_WD * w), nm, nv


def _adamw(w, g, m, v):
    r, c = w.shape
    tr = _row_tile(r, cap=512, mult=8)

    def body(w_ref, g_ref, m_ref, v_ref, d_ref, nm_ref, nv_ref):
        d_ref[...], nm_ref[...], nv_ref[...] = _adamw_math(w_ref[...], g_ref[...], m_ref[...], v_ref[...])

    spec = _rows(tr, c)
    return pl.pallas_call(
        body, name="adamw_%dx%d" % (r, c), grid=(r // tr,),
        in_specs=[spec] * 4, out_specs=[spec] * 3, out_shape=[_sds((r, c), F32)] * 3,
        compiler_params=_cp("parallel"),
    )(w, g, m, v)


SC_TILES = 32
SC_LANES = 16
SC_ROWS = 8


def _adamw_sparse(w, g, m, v):
    r, c = w.shape
    rows = r // SC_TILES
    step = min(rows, SC_ROWS)

    def body(w_hbm, g_hbm, m_hbm, v_hbm, d_hbm, nm_hbm, nv_hbm, wb, gb, mb, vb):
        tile = lax.axis_index("sc_subcore") * 2 + lax.axis_index("sc_core")

        @pl.loop(0, rows, step=step)
        def _(r0):
            mine = pl.ds(tile * rows + r0, step)
            for src, dst in ((w_hbm, wb), (g_hbm, gb), (m_hbm, mb), (v_hbm, vb)):
                pltpu.sync_copy(src.at[mine], dst)

            @pl.loop(0, step)
            def _(row):
                @pl.loop(0, c, step=SC_LANES)
                def _(i):
                    at = (row, pl.ds(i, SC_LANES))
                    wb[at], mb[at], vb[at] = _adamw_math(wb[at], gb[at], mb[at], vb[at])

            for src, dst in ((wb, d_hbm), (mb, nm_hbm), (vb, nv_hbm)):
                pltpu.sync_copy(src, dst.at[mine])

    return pl.kernel(
        body, name="adamw_sparse_%dx%d" % (r, c), out_type=[_sds((r, c), F32)] * 3,
        mesh=plsc.VectorSubcoreMesh(core_axis_name="sc_core", subcore_axis_name="sc_subcore"),
        scratch_types=[pltpu.VMEM((step, c), F32)] * 4,
    )(w, g, m, v)


_SMALL_NAMES = ("w_pool", "b_in", "g_mix_pre", "g_mix_post", "g_mlp_pre", "g_mlp_post", "pool_scale", "attn_sinks")
B_ROWS = -(-IN_WIDTH // D_MODEL)


def _row_block(rows):
    rows = [jnp.pad(r.astype(F32), ((0, 0), (0, D_MODEL - r.shape[1]))) for r in rows]
    return jnp.pad(jnp.concatenate(rows, axis=0), ((0, 8 - len(rows)), (0, 0)))


def _early_block(dg2, dg3, dg4, dps, dsink, loss):
    tail = jnp.concatenate([jnp.pad(dsink.reshape(1, -1), ((0, 0), (0, LANES - dsink.size))),
                            jnp.pad(loss.reshape(1, 1), ((0, 0), (0, LANES - 1)))], axis=1)
    return _row_block([dg2, dg3, dg4, dps, tail])


def _late_block(db_in, dg1):
    b = jnp.pad(db_in, ((0, 0), (0, B_ROWS * D_MODEL - IN_WIDTH))).reshape(B_ROWS, D_MODEL)
    return _row_block([b[r:r + 1] for r in range(B_ROWS)] + [dg1])


def _small_update(gearly, gmat, glate, w, m, v):
    names = _SMALL_NAMES
    n = len(names)

    def total(ref, rows):
        acc = ref[0:rows, :]
        for d in range(1, N_DEV):
            acc = acc + ref[d * rows:(d + 1) * rows, :]
        return acc

    def body(*refs):
        early_ref, gmat_ref, late_ref = refs[:3]
        w_refs, m_refs, v_refs = refs[3:3 + n], refs[3 + n:3 + 2 * n], refs[3 + 2 * n:3 + 3 * n]
        outs = refs[3 + 3 * n:]
        loss_ref, g_refs, d_refs = outs[0], outs[1:1 + n], outs[1 + n:1 + 2 * n]
        nm_refs, nv_refs = outs[1 + 2 * n:1 + 3 * n], outs[1 + 3 * n:1 + 4 * n]
        early, late = total(early_ref, 8), total(late_ref, 8)
        loss_ref[...] = jnp.sum(early[4:5, LANES:2 * LANES], axis=1, keepdims=True)
        bias = jnp.concatenate([late[r:r + 1, :] for r in range(B_ROWS - 1)]
                               + [late[B_ROWS - 1:B_ROWS, :IN_WIDTH - (B_ROWS - 1) * D_MODEL]], axis=1)
        grad = dict(b_in=bias, g_mix_pre=late[B_ROWS:B_ROWS + 1, :], g_mix_post=early[0:1, :],
                    g_mlp_pre=early[1:2, :], g_mlp_post=early[2:3, :], pool_scale=early[3:4, :POOL_WIDTH],
                    attn_sinks=early[4:5, :N_Q_HEADS])
        for i, name in enumerate(names):
            g = total(gmat_ref, 4 * POOL_GC) if name == "w_pool" else grad[name]
            g_refs[i][...] = g
            d_refs[i][...], nm_refs[i][...], nv_refs[i][...] = _adamw_math(
                w_refs[i][...], g, m_refs[i][...], v_refs[i][...])

    shapes = [_sds(w[k].shape, F32) for k in names]
    res = pl.pallas_call(
        body, name="small_update", out_shape=[_sds((1, 1), F32)] + shapes * 4,
        compiler_params=pltpu.CompilerParams(vmem_limit_bytes=VMEM_MB * 1024 * 1024),
    )(gearly, gmat, glate, *[w[k] for k in names], *[m[k] for k in names], *[v[k] for k in names])
    loss = res[0]
    per = {k: tuple(res[1 + j * n + i] for j in range(4)) for i, k in enumerate(names)}
    return loss, per


_BIG = ("w_in", "w_branch_pool", "w_branch_attn", "w_out", "w_up", "w_down")
_ORDER = ("g_mix_pre", "w_in", "b_in", "w_pool", "pool_scale", "attn_sinks", "w_branch_pool", "w_branch_attn",
          "w_out", "g_mix_post", "g_mlp_pre", "w_up", "w_down", "g_mlp_post")


def _stack_rows(slab):
    return slab.reshape(-1, slab.shape[2])


def _step(x2, tgt, seq, shards, small, ids):
    tabs = _rope_tables(seq)
    g1, g2, g3, g4 = (small[n] for n in ("g_mix_pre", "g_mix_post", "g_mlp_pre", "g_mlp_post"))
    sinks = small["attn_sinks"].reshape(N_Q_HEADS)
    w_pool = small["w_pool"].reshape(4, POOL_GC, POOL_GC)
    pool_scale = small["pool_scale"]

    def whole(shard, slabs):
        return lax.dynamic_update_slice(slabs, shard[None], (ids[0], 0, 0))

    up_a, up_b = shards["w_up"][:HALF], shards["w_up"][HALF:]
    down_a, down_b = shards["w_down"][:HALF], shards["w_down"][HALF:]
    w_in = _stack_rows(whole(shards["w_in"], _alone("gather_in", _ex_gather([shards["w_in"]]))[0][0]))
    mix_shards = [shards[n] for n in ("w_branch_pool", "w_branch_attn", "w_out")]
    (h, u, q, k, v, gate), [mix_slabs] = _inproj(
        x2, g1, w_in, small["b_in"], tabs, seq, exchanges=[_ex_gather(mix_shards)])
    w_bp, w_ba, out_slab = (whole(s, g) for s, g in zip(mix_shards, mix_slabs))
    w_out = _stack_rows(out_slab)
    (y_attn, diff, y_pool), [[got_a, got_b]] = _mixers_fwd(
        q, k, v, sinks, u, w_pool, pool_scale, seq, exchanges=[_ex_gather([up_a, up_b])])
    (merged, mix, x1, h2), [[got_c, got_d]] = _merge_out(
        y_pool, y_attn, gate, x2, w_bp, w_ba, w_out, g2, g3, exchanges=[_ex_gather([down_a, down_b])])
    w_up = (whole(up_a, got_a), whole(up_b, got_b))
    w_down = (whole(down_a, got_c), whole(down_b, got_d))
    act, dff, dup, dx1, dmix, loss_acc, dg4, dg3, dg2 = _mlp_core(h2, x1, mix, tgt, w_up, w_down, g4, g3, g2)

    dw_down = _dw("down", act, dff, 1024, 1024)[0].reshape(N_CHIPS, D_FF // N_CHIPS, D_MODEL)
    (dbp, dba, dgate, dyp, dya), [[got]] = _merge_bwd(
        dmix, gate, y_pool, y_attn, w_out, w_bp, w_ba, exchanges=[_ex_pair([dw_down])])
    ps_down = _pair_sum(ids, dw_down, got)
    (dw_up,), [[got]] = _dw("up", h2, dup, 1024, 1024, shard_cols=True, exchanges=[_ex_chip([ps_down[1]])])
    half_down = _chip_sum(ids, ps_down[0], got)
    (dw_out, dw_bp, dw_ba), [[got]] = _dw_mix(merged, dmix, y_pool, dbp, y_attn, dba, exchanges=[_ex_pair([dw_up])])
    ps_up = _pair_sum(ids, dw_up, got)
    dw_mix = [dw_out.reshape(N_CHIPS, D_MODEL // N_CHIPS, D_MODEL), dw_bp, dw_ba]
    (dq, dk, dv, dsink), [[got], gots, [g_down]] = _attn_bwd(
        q, k, v, dya, sinks, tabs, seq, exchanges=[_ex_chip([ps_up[1]]), _ex_pair(dw_mix), _ex_swap([half_down])])
    half_up = _chip_sum(ids, ps_up[0], got)
    ps_mix = [_pair_sum(ids, d, g) for d, g in zip(dw_mix, gots)]
    (du, dw_pool, dps), [[g_up]] = _pool_bwd(dyp, diff, w_pool, pool_scale, seq, exchanges=[_ex_swap([half_up])])
    parts = (du, dq, dk, dv, dgate)
    early = _early_block(dg2, dg3, dg4, dps, dsink[:, 0], loss_acc[0, 0])
    mat = dw_pool.reshape(4 * POOL_GC, POOL_GC)
    (dw_in_t, db_in), [gots, [gearly, gmat]] = _dw_in(
        h, parts, exchanges=[_ex_chip([p[1] for p in ps_mix]), _ex_allgather([early, mat])])
    half_mix = [_chip_sum(ids, p[0], g) for p, g in zip(ps_mix, gots)]
    dw_in = dw_in_t.reshape(N_CHIPS, IN_WIDTH // N_CHIPS, D_MODEL)
    g_mix, [got] = _alone("swap_mix_pair_in", _ex_swap(half_mix), _ex_pair([dw_in]))
    ps_in = _pair_sum(ids, dw_in, got)
    (gx, dg1), [[got]] = _inproj_bwd(parts, x2, dx1, w_in, g1, exchanges=[_ex_chip([ps_in[1]])])
    [g_in], [glate] = _alone("swap_in_allgather", _ex_swap([_chip_sum(ids, ps_in[0], got)]),
                             _ex_allgather([_late_block(db_in, dg1)]))

    grads = dict(w_in=g_in, w_branch_pool=g_mix[1], w_branch_attn=g_mix[2], w_out=g_mix[0], w_up=g_up, w_down=g_down)
    return (gearly, gmat, glate), gx, grads


def kernel(x, g_mix_pre, w_in, b_in, w_pool, pool_scale, attn_sinks, w_branch_pool, w_branch_attn, w_out, g_mix_post, g_mlp_pre, w_up, w_down, g_mlp_post, loss_target, m_g_mix_pre, m_w_in, m_b_in, m_w_pool, m_pool_scale, m_attn_sinks, m_w_branch_pool, m_w_branch_attn, m_w_out, m_g_mix_post, m_g_mlp_pre, m_w_up, m_w_down, m_g_mlp_post, v_g_mix_pre, v_w_in, v_b_in, v_w_pool, v_pool_scale, v_attn_sinks, v_w_branch_pool, v_w_branch_attn, v_w_out, v_g_mix_post, v_g_mlp_pre, v_w_up, v_w_down, v_g_mlp_post):
    weights = dict(g_mix_pre=g_mix_pre, w_in=w_in, b_in=b_in, w_pool=w_pool, pool_scale=pool_scale,
                   attn_sinks=attn_sinks, w_branch_pool=w_branch_pool, w_branch_attn=w_branch_attn, w_out=w_out,
                   g_mix_post=g_mix_post, g_mlp_pre=g_mlp_pre, w_up=w_up, w_down=w_down, g_mlp_post=g_mlp_post)
    mom1 = dict(g_mix_pre=m_g_mix_pre, w_in=m_w_in, b_in=m_b_in, w_pool=m_w_pool, pool_scale=m_pool_scale,
                attn_sinks=m_attn_sinks, w_branch_pool=m_w_branch_pool, w_branch_attn=m_w_branch_attn,
                w_out=m_w_out, g_mix_post=m_g_mix_post, g_mlp_pre=m_g_mlp_pre, w_up=m_w_up, w_down=m_w_down,
                g_mlp_post=m_g_mlp_post)
    mom2 = dict(g_mix_pre=v_g_mix_pre, w_in=v_w_in, b_in=v_b_in, w_pool=v_w_pool, pool_scale=v_pool_scale,
                attn_sinks=v_attn_sinks, w_branch_pool=v_w_branch_pool, w_branch_attn=v_w_branch_attn,
                w_out=v_w_out, g_mix_post=v_g_mix_post, g_mlp_pre=v_g_mlp_pre, w_up=v_w_up, w_down=v_w_down,
                g_mlp_post=v_g_mlp_post)
    b_loc, seq, _ = x.shape
    x2 = x.reshape(b_loc * seq, D_MODEL)
    tgt = loss_target.reshape(b_loc * seq, D_MODEL)
    ids = jnp.stack([2 * lax.axis_index("x") + lax.axis_index("y"), lax.axis_index("c")]).astype(jnp.int32)

    def flat(n, a):
        return a[0].T if n == "w_in" else a[0]

    def unflat(n, a):
        return (a.T if n == "w_in" else a)[None]

    shards = {n: flat(n, weights[n]).astype(BF16) for n in _BIG}
    small = {n: weights[n] for n in _ORDER if n not in _BIG}
    (gearly, gmat, glate), gx, grads = _step(x2, tgt, seq, shards, small, ids)

    def two_d(src):
        return {n: src[n].reshape(4 * POOL_GC, POOL_GC) if n == "w_pool" else src[n] for n in _SMALL_NAMES}

    loss, per = _small_update(gearly, gmat, glate, two_d(weights), two_d(mom1), two_d(mom2))
    delta, new_m, new_v = {}, {}, {}
    for n in _SMALL_NAMES:
        grads[n], delta[n], new_m[n], new_v[n] = (a.reshape(weights[n].shape) for a in per[n])
    for n in _BIG:
        update = _adamw if n == "w_in" else _adamw_sparse
        d, nm, nv = update(flat(n, weights[n]), grads[n], flat(n, mom1[n]), flat(n, mom2[n]))
        grads[n] = unflat(n, grads[n])
        delta[n], new_m[n], new_v[n] = unflat(n, d), unflat(n, nm), unflat(n, nv)

    return (loss[0, 0], gx.reshape(x.shape), *[grads[n] for n in _ORDER], *[delta[n] for n in _ORDER],
            *[new_m[n] for n in _ORDER], *[new_v[n] for n in _ORDER])
```

```python
import jax
import jax.numpy as jnp
from jax import lax
from jax.experimental import pallas as pl
from jax.experimental.pallas import tpu as pltpu
from jax.experimental.pallas import tpu_sc as plsc

F32 = jnp.float32
BF16 = jnp.bfloat16

D_MODEL = 1024
POOL_WINDOWS = (2, 4, 8, 16)
POOL_WIDTH = 512
POOL_GC = 128
HALO = 16
HEAD_DIM = 64
N_Q_HEADS = 8
ATTN_WIDTH = 512
KV_WIDTH = 128
BLOCK = 128
NEG_INF = -1e30
ROPE_THETA = 500000.0
ROT_DIM = 16
GATE_WIDTH = 2048
IN_WIDTH = 3328
D_FF = 4096
EPS = 1e-6
SCALE = HEAD_DIM ** -0.5
C_Q, C_K, C_V, C_G = 512, 1024, 1152, 1280

ADAM_LR, ADAM_B1, ADAM_B2, ADAM_EPS, ADAM_WD, ADAM_STEP = 0.001, 0.9, 0.999, 1e-08, 0.01, 10

N_CHIPS = 4
N_DEV = 8
LANES = 128
TM = 512
VMEM_MB = 56

MESH = pl.DeviceIdType.MESH
ANY = pl.BlockSpec(memory_space=pl.ANY)


def _cp(*sem, vmem=VMEM_MB):
    return pltpu.CompilerParams(dimension_semantics=sem, vmem_limit_bytes=vmem * 1024 * 1024)


def _rows(tile, cols):
    return pl.BlockSpec((tile, cols), lambda i: (i, 0))


def _const(shape):
    nd = len(shape)
    return pl.BlockSpec(shape, lambda i: (0,) * nd)


def _sds(shape, dtype):
    return jax.ShapeDtypeStruct(shape, dtype)


def _dot(a, b):
    return jnp.dot(a, b, preferred_element_type=F32)


def _dot_nt(a, b):
    return lax.dot_general(a, b, (((1,), (1,)), ((), ())), preferred_element_type=F32)


def _dot_tn(a, b):
    return lax.dot_general(a, b, (((0,), (0,)), ((), ())), preferred_element_type=F32)


def _rms(x):
    return lax.rsqrt(jnp.mean(x * x, axis=-1, keepdims=True) + EPS)


def _norm_bwd(x, g, dout):
    r = _rms(x)
    n = x * r
    dn = dout * g
    dx = r * (dn - n * jnp.mean(dn * n, axis=-1, keepdims=True))
    return dx, jnp.sum(dout * n, axis=0, keepdims=True)


def _rot_fwd(t, c, a, bt):
    return t * c + pltpu.roll(t, LANES - 8, 1) * a + pltpu.roll(t, 8, 1) * bt


def _rot_bwd(d, c, a, bt):
    return d * c + pltpu.roll(d * a, 8, 1) + pltpu.roll(d * bt, LANES - 8, 1)


def _rope_tables(seq):
    pos = jnp.arange(seq, dtype=F32)
    inv_freq = ROPE_THETA ** (-jnp.arange(0, ROT_DIM, 2, dtype=F32) / ROT_DIM)
    ang = pos[:, None] * inv_freq[None, :]
    cos, sin = jnp.cos(ang), jnp.sin(ang)
    ones = jnp.ones((seq, HEAD_DIM - ROT_DIM), F32)
    zeros8 = jnp.zeros((seq, 8), F32)
    zrest = jnp.zeros((seq, HEAD_DIM - ROT_DIM), F32)
    c = jnp.concatenate([cos, cos, ones], axis=1)
    a = jnp.concatenate([-sin, zeros8, zrest], axis=1)
    bt = jnp.concatenate([zeros8, sin, zrest], axis=1)
    return tuple(jnp.tile(t, (1, 2)) for t in (c, a, bt))


class _Exchange:
    def __init__(self, inputs, out_shapes, sems, start, finish, aliases=None, middle=None):
        self.inputs, self.out_shapes, self.sems = list(inputs), list(out_shapes), list(sems)
        self.start, self.finish, self.aliases = start, finish, dict(aliases or {})
        self.middle = middle


def _call(body, *, name, grid, in_specs, out_specs, out_shape, args, scratch=(), sem=(), exchanges=()):
    in_specs, out_specs, out_shape, scratch = list(in_specs), list(out_specs), list(out_shape), list(scratch)
    if not exchanges:
        return pl.pallas_call(body, name=name, grid=grid, in_specs=in_specs, out_specs=out_specs,
                              out_shape=out_shape, scratch_shapes=scratch, compiler_params=_cp(*sem))(*args)
    n_in, n_out, n_scr = len(in_specs), len(out_specs), len(scratch)
    x_in = [a for ex in exchanges for a in ex.inputs]
    x_out = [s for ex in exchanges for s in ex.out_shapes]
    x_sem = [s for ex in exchanges for s in ex.sems]
    aliases, i_off, o_off = {}, n_in, n_out
    for ex in exchanges:
        for i, o in ex.aliases.items():
            aliases[i_off + i] = o_off + o
        i_off += len(ex.inputs)
        o_off += len(ex.out_shapes)

    def split(flat):
        out, pos = [], 0
        for ex, n in zip(exchanges, flat[1]):
            out.append(flat[0][pos:pos + n])
            pos += n
        return out

    def carrier(*refs):
        pos = 0
        groups = []
        for n in (n_in, len(x_in), n_out, len(x_out), n_scr, len(x_sem)):
            groups.append(refs[pos:pos + n])
            pos += n
        ins, xin, outs, xout, scr, xsem = groups
        xin = split((xin, [len(ex.inputs) for ex in exchanges]))
        xout = split((xout, [len(ex.out_shapes) for ex in exchanges]))
        xsem = split((xsem, [len(ex.sems) for ex in exchanges]))
        first = pl.program_id(0) == 0
        last = pl.program_id(0) == grid[0] - 1
        for d in range(1, len(grid)):
            first = jnp.logical_and(first, pl.program_id(d) == 0)
            last = jnp.logical_and(last, pl.program_id(d) == grid[d] - 1)

        @pl.when(first)
        def _():
            for ex, i, o, s in zip(exchanges, xin, xout, xsem):
                ex.start(i, o, s)

        if any(ex.middle for ex in exchanges):
            half = pl.program_id(0) == 5 * grid[0] // 8
            for d in range(1, len(grid)):
                half = jnp.logical_and(half, pl.program_id(d) == 0)

            @pl.when(half)
            def _():
                for ex, i, o, s in zip(exchanges, xin, xout, xsem):
                    if ex.middle:
                        ex.middle(i, o, s)

        body(*ins, *outs, *scr)

        @pl.when(last)
        def _():
            for ex, i, o, s in zip(exchanges, xin, xout, xsem):
                ex.finish(i, o, s)

    res = pl.pallas_call(
        carrier, name=name, grid=grid, in_specs=in_specs + [ANY] * len(x_in),
        out_specs=out_specs + [ANY] * len(x_out), out_shape=out_shape + x_out,
        scratch_shapes=scratch + x_sem, input_output_aliases=aliases,
        compiler_params=_cp(*(["arbitrary"] * len(grid))),
    )(*args, *x_in)
    return res[:n_out], split((res[n_out:], [len(ex.out_shapes) for ex in exchanges]))


def _alone(name, *exchanges):
    n_in = [len(ex.inputs) for ex in exchanges]
    n_out = [len(ex.out_shapes) for ex in exchanges]
    n_sem = [len(ex.sems) for ex in exchanges]
    aliases, i_off, o_off = {}, 0, 0
    for ex in exchanges:
        for i, o in ex.aliases.items():
            aliases[i_off + i] = o_off + o
        i_off += len(ex.inputs)
        o_off += len(ex.out_shapes)

    def split(flat, counts):
        out, pos = [], 0
        for n in counts:
            out.append(flat[pos:pos + n])
            pos += n
        return out

    def body(*refs):
        ins, outs, sems = split(refs, [sum(n_in), sum(n_out), sum(n_sem)])
        groups = list(zip(exchanges, split(ins, n_in), split(outs, n_out), split(sems, n_sem)))
        for ex, i, o, s in groups:
            ex.start(i, o, s)
        for ex, i, o, s in groups:
            if ex.middle:
                ex.middle(i, o, s)
        for ex, i, o, s in groups:
            ex.finish(i, o, s)

    res = pl.pallas_call(
        body, name=name, in_specs=[ANY] * sum(n_in), out_specs=[ANY] * sum(n_out),
        out_shape=[s for ex in exchanges for s in ex.out_shapes],
        scratch_shapes=[s for ex in exchanges for s in ex.sems], input_output_aliases=aliases,
    )(*[a for ex in exchanges for a in ex.inputs])
    return split(res, n_out)


def _place():
    x, y, c = lax.axis_index("x"), lax.axis_index("y"), lax.axis_index("c")
    chips = [(1 - x, y), (x, 1 - y), (1 - x, 1 - y)]
    return x, y, c, chips


def _remote(src, dst, send, recv, to):
    return pltpu.make_async_remote_copy(src_ref=src, dst_ref=dst, send_sem=send, recv_sem=recv,
                                        device_id=to, device_id_type=MESH)


def _ex_gather(shards):
    nw = len(shards)
    hrs = [s.shape[0] // 2 for s in shards]

    def copies(ins, outs, sems):
        s1, r1, s2, r2, fs, fr = sems
        x, y, c, _ = _place()
        me, xn, yn, dg = (x, y), (1 - x, y), (x, 1 - y), (1 - x, 1 - y)
        nbr = (xn, yn)
        sibling = (x, y, 1 - c)

        def piece(w, chip, core, part=None):
            hr = hrs[w]
            rows = pl.ds(core * hr, hr) if part is None else pl.ds(core * hr + part * (hr // 2), hr // 2)
            return outs[w].at[2 * chip[0] + chip[1], rows]

        def first(w, k):
            return _remote(ins[w].at[pl.ds(c * hrs[w], hrs[w])], piece(w, me, c), s1.at[w, k], r1.at[w, k],
                           (*nbr[k], c))

        def landed(w, k):
            return _remote(piece(w, nbr[k], c), piece(w, nbr[k], c), s1.at[w, k], r1.at[w, k], (*nbr[k], c))

        def onward(w, k):
            return _remote(piece(w, nbr[k], c, k), piece(w, nbr[k], c, k), s2.at[w, k], r2.at[w, k],
                           (*nbr[1 - k], c))

        def arrived(w, k):
            return _remote(piece(w, dg, c, k), piece(w, dg, c, k), s2.at[w, k], r2.at[w, k], (*nbr[1 - k], c))

        def passed(w, j):
            chip = (xn, yn, dg)[j]
            return _remote(piece(w, chip, c), piece(w, chip, c), fs.at[w, j], fr.at[w, j], sibling)

        def handed(w, j):
            chip = (xn, yn, dg)[j]
            return _remote(piece(w, chip, 1 - c), piece(w, chip, 1 - c), fs.at[w, j], fr.at[w, j], sibling)

        return first, landed, onward, arrived, passed, handed

    def start(ins, outs, sems):
        first = copies(ins, outs, sems)[0]
        for w in range(nw):
            for k in range(2):
                first(w, k).start()

    def middle(ins, outs, sems):
        _, landed, onward, _, passed, _ = copies(ins, outs, sems)
        for w in range(nw):
            for k in range(2):
                landed(w, k).wait_recv()
                onward(w, k).start()
                passed(w, k).start()

    def finish(ins, outs, sems):
        first, _, onward, arrived, passed, handed = copies(ins, outs, sems)
        for w in range(nw):
            for k in range(2):
                arrived(w, k).wait_recv()
            passed(w, 2).start()
        for w in range(nw):
            for j in range(3):
                handed(w, j).wait_recv()
        for w in range(nw):
            for k in range(2):
                first(w, k).wait_send()
                onward(w, k).wait_send()
            for j in range(3):
                passed(w, j).wait_send()

    return _Exchange(shards, [_sds((N_CHIPS,) + s.shape, s.dtype) for s in shards],
                     [pltpu.SemaphoreType.DMA((nw, 2))] * 4 + [pltpu.SemaphoreType.DMA((nw, 3))] * 2,
                     start, finish, middle=middle)


def _ex_pair(grads):
    nw = len(grads)

    def copies(ins, outs, sems):
        x, y, c, _ = _place()
        out = []
        for w in range(nw):
            hr = grads[w].shape[1] // 2
            out.append(_remote(ins[w].at[:, pl.ds((1 - c) * hr, hr)], outs[w], sems[0].at[w], sems[1].at[w],
                               (x, y, 1 - c)))
        return out

    def start(ins, outs, sems):
        for cp in copies(ins, outs, sems):
            cp.start()

    def finish(ins, outs, sems):
        for cp in copies(ins, outs, sems):
            cp.wait()

    return _Exchange(grads, [_sds((N_CHIPS, g.shape[1] // 2, g.shape[2]), F32) for g in grads],
                     [pltpu.SemaphoreType.DMA((nw,))] * 2, start, finish)


def _ex_chip(pieces):
    nw = len(pieces)

    def copies(ins, outs, sems):
        x, y, c, chips = _place()
        return [_remote(ins[w].at[2 * cx + cy], outs[w].at[k], sems[0].at[w, k], sems[1].at[w, k], (cx, cy, c))
                for w in range(nw) for k, (cx, cy) in enumerate(chips)]

    def start(ins, outs, sems):
        for cp in copies(ins, outs, sems):
            cp.start()

    def finish(ins, outs, sems):
        for cp in copies(ins, outs, sems):
            cp.wait()

    return _Exchange(pieces, [_sds((3,) + p.shape[1:], BF16) for p in pieces],
                     [pltpu.SemaphoreType.DMA((nw, 3))] * 2, start, finish)


def _ex_swap(fulls):
    nw = len(fulls)

    def start(ins, outs, sems):
        x, y, c, _ = _place()
        for w in range(nw):
            hr = fulls[w].shape[0] // 2
            mine = pl.ds(c * hr, hr)
            _remote(ins[w].at[mine], outs[w].at[mine], sems[0].at[w], sems[1].at[w], (x, y, 1 - c)).start()

    def finish(ins, outs, sems):
        x, y, c, _ = _place()
        for w in range(nw):
            hr = fulls[w].shape[0] // 2
            mine, theirs = pl.ds(c * hr, hr), pl.ds((1 - c) * hr, hr)
            _remote(ins[w].at[mine], outs[w].at[mine], sems[0].at[w], sems[1].at[w], (x, y, 1 - c)).wait_send()
            _remote(ins[w].at[theirs], outs[w].at[theirs], sems[0].at[w], sems[1].at[w], (x, y, 1 - c)).wait_recv()

    return _Exchange(fulls, [_sds(f.shape, F32) for f in fulls], [pltpu.SemaphoreType.DMA((nw,))] * 2,
                     start, finish, aliases={w: w for w in range(nw)})


def _ex_allgather(blocks):
    nb = len(blocks)

    def copies(ins, outs, sems):
        send, recv, lsem = sems
        x, y, c, chips = _place()
        me, sibling = (x, y, c), (x, y, 1 - c)

        def rows(b, px, py, pc):
            m_per = blocks[b].shape[0]
            return outs[b].at[pl.ds((4 * px + 2 * py + pc) * m_per, m_per), :]

        def copy(b, k, blk, to, src=None):
            return _remote(rows(b, *blk) if src is None else src, rows(b, *blk), send.at[b, k], recv.at[b, k], to)

        def mine(b):
            return pltpu.make_async_copy(ins[b], rows(b, *me), lsem.at[b])

        def first(b, k):
            return copy(b, k, me, sibling if k == 0 else (*chips[k - 1], c), src=ins[b])

        def passed(b, j):
            return copy(b, 4 + j, (*chips[j], c), sibling)

        def landed(b, j):
            return copy(b, 1 + j, (*chips[j], c), me)

        def handed(b, k):
            return copy(b, 0, sibling, me) if k == 0 else copy(b, 3 + k, (*chips[k - 1], 1 - c), me)

        return mine, first, passed, landed, handed

    def start(ins, outs, sems):
        mine, first, _, _, _ = copies(ins, outs, sems)
        for b in range(nb):
            mine(b).start()
            for k in range(4):
                first(b, k).start()

    def finish(ins, outs, sems):
        mine, first, passed, landed, handed = copies(ins, outs, sems)
        sent = []
        for b in range(nb):
            for j in range(3):
                landed(b, j).wait_recv()
                cp = passed(b, j)
                cp.start()
                sent.append(cp)
        for b in range(nb):
            for k in range(4):
                handed(b, k).wait_recv()
            for k in range(4):
                first(b, k).wait_send()
        for cp in sent:
            cp.wait_send()
        for b in range(nb):
            mine(b).wait()

    return _Exchange(blocks, [_sds((N_DEV * b.shape[0], b.shape[1]), F32) for b in blocks],
                     [pltpu.SemaphoreType.DMA((nb, 7)), pltpu.SemaphoreType.DMA((nb, 7)), pltpu.SemaphoreType.DMA((nb,))],
                     start, finish)


def _inproj(x2, g1, w_in_t, b_in, tabs, seq, exchanges=()):
    T = x2.shape[0]
    tm = min(TM, seq)
    nseq = seq // tm

    def body(x_ref, g_ref, w_ref, b_ref, c_ref, a_ref, bt_ref, h_ref, u_ref, q_ref, k_ref, v_ref, gate_ref):
        x = x_ref[...]
        h = (x * _rms(x) * g_ref[...]).astype(BF16)
        h_ref[...] = h

        def proj(lo, hi):
            return _dot_nt(h, w_ref[lo:hi, :]) + b_ref[:, lo:hi]

        c, a, bt = c_ref[...], a_ref[...], bt_ref[...]
        u_ref[...] = proj(0, C_Q)
        q = proj(C_Q, C_K)
        for p in range(4):
            sl = slice(LANES * p, LANES * (p + 1))
            q_ref[:, sl] = (_rot_fwd(q[:, sl], c, a, bt) * SCALE).astype(BF16)
        kv = proj(C_K, C_G)
        k_ref[...] = _rot_fwd(kv[:, :KV_WIDTH], c, a, bt).astype(BF16)
        v_ref[...] = kv[:, KV_WIDTH:].astype(BF16)
        for j in range(2):
            lo = C_G + D_MODEL * j
            gate_ref[:, D_MODEL * j:D_MODEL * (j + 1)] = jax.nn.sigmoid(proj(lo, lo + D_MODEL)).astype(BF16)

    tab = pl.BlockSpec((tm, LANES), lambda i: (i % nseq, 0))
    return _call(
        body, name="inproj", grid=(T // tm,),
        in_specs=[_rows(tm, D_MODEL), _const((1, D_MODEL)), _const((IN_WIDTH, D_MODEL)), _const((1, IN_WIDTH)),
                  tab, tab, tab],
        out_specs=[_rows(tm, D_MODEL), _rows(tm, POOL_WIDTH), _rows(tm, ATTN_WIDTH), _rows(tm, KV_WIDTH),
                   _rows(tm, KV_WIDTH), _rows(tm, GATE_WIDTH)],
        out_shape=[_sds((T, D_MODEL), BF16), _sds((T, POOL_WIDTH), F32), _sds((T, ATTN_WIDTH), BF16),
                   _sds((T, KV_WIDTH), BF16), _sds((T, KV_WIDTH), BF16), _sds((T, GATE_WIDTH), BF16)],
        args=(x2, g1, w_in_t, b_in, *tabs), sem=("parallel",), exchanges=exchanges)


def _inv_count(pos, w):
    return 1.0 / jnp.minimum(pos + 1, w).astype(F32)


def _pool_tile(i, tp, nseq, u_ref, prev_ref, w_ref, s_ref, diff_ref, y_ref):
    first = (i % nseq) == 0
    prev = jnp.where(first, 0.0, prev_ref[...])
    ext = jnp.concatenate([prev, u_ref[...]], axis=0)
    pos = (i % nseq) * tp + lax.broadcasted_iota(jnp.int32, (tp, 1), 0)
    for gi, w in enumerate(POOL_WINDOWS):
        sl = slice(POOL_GC * gi, POOL_GC * (gi + 1))
        xg = ext[:, sl]
        s = xg
        sh = 1
        while sh < w:
            s = s + pltpu.roll(s, sh, 0)
            sh *= 2
        pooled = s[HALO:] * _inv_count(pos, w)
        diff = (pooled - xg[HALO:]).astype(BF16)
        diff_ref[:, sl] = diff
        mixed = _dot(diff, w_ref[gi].astype(BF16))
        y_ref[:, sl] = (mixed * s_ref[:, sl]).astype(BF16)


def _pool_specs(tp):
    per = tp // HALO
    return [_rows(tp, POOL_WIDTH), pl.BlockSpec((HALO, POOL_WIDTH), lambda i: (jnp.maximum(i * per - 1, 0), 0)),
            _const((4, POOL_GC, POOL_GC)), _const((1, POOL_WIDTH))]


GROUP = 4
GROWS = GROUP * BLOCK


def _attn_masks(n):
    qi = lax.broadcasted_iota(jnp.int32, (GROWS, 2 * BLOCK), 0) % BLOCK
    kj = lax.broadcasted_iota(jnp.int32, (GROWS, 2 * BLOCK), 1)
    rel = qi + BLOCK - kj
    valid = (rel >= 0) & (rel < BLOCK) & (kj >= jnp.where(n > 0, 0, BLOCK))
    lo = lax.broadcasted_iota(jnp.int32, (BLOCK, LANES), 1) < HEAD_DIM
    return valid, lo


def _by_example(bl, *arrays):
    return [a.reshape(bl, a.shape[0] // bl, a.shape[1]) for a in arrays]


def _stack_heads(ref, h, lo):
    keep = lo if h == 0 else jnp.logical_not(lo)
    pieces = []
    for p in (2 * h, 2 * h + 1):
        xp = ref[:, LANES * p:LANES * (p + 1)].astype(F32)
        for e in range(2):
            t = xp if e == h else pltpu.roll(xp, HEAD_DIM, 1)
            pieces.append(jnp.where(keep, t, 0.0).astype(BF16))
    return jnp.concatenate(pieces, axis=0)


def _unstack_heads(stacked, h, lo):
    pairs = []
    for j in range(2):
        parts = []
        for e in range(2):
            t = stacked[BLOCK * (2 * j + e):BLOCK * (2 * j + e + 1)]
            parts.append(t if e == h else pltpu.roll(t, HEAD_DIM, 1))
        pairs.append(jnp.where(lo, parts[0], parts[1]))
    return pairs


def _sink_rows(sink_ref, h):
    head = lax.broadcasted_iota(jnp.int32, (GROWS, 1), 0) // BLOCK
    col = jnp.zeros((GROWS, 1), F32) + sink_ref[GROUP * h]
    for g in range(1, GROUP):
        col = jnp.where(head == g, sink_ref[GROUP * h + g], col)
    return col


def _group_probs(qs, kk, valid, sink):
    s = jnp.where(valid, _dot_nt(qs, kk), NEG_INF)
    m = jnp.maximum(jnp.max(s, axis=1, keepdims=True), sink)
    ex = jnp.exp(s - m)
    es = jnp.exp(sink - m)
    inv = 1.0 / (jnp.sum(ex, axis=1, keepdims=True) + es)
    return ex * inv, es * inv


def _mixers_fwd(q, k, v, sinks, u, w_pool, pool_scale, seq, exchanges=()):
    T = q.shape[0]
    nb = seq // BLOCK
    bl = T // seq
    tp = T // nb
    nseq = seq // tp

    def body(sink_ref, q_ref, kp_ref, kc_ref, vp_ref, vc_ref, u_ref, prev_ref, w_ref, s_ref, o_ref, diff_ref, y_ref):
        n = pl.program_id(0)
        valid, lo = _attn_masks(n)
        for b in range(bl):
            kk = jnp.concatenate([kp_ref[b], kc_ref[b]], axis=0)
            vv = jnp.concatenate([vp_ref[b], vc_ref[b]], axis=0)
            for h in range(2):
                qs = _stack_heads(q_ref.at[b], h, lo)
                pr, _ = _group_probs(qs, kk, valid, _sink_rows(sink_ref, h))
                o = _dot(pr.astype(BF16), vv)
                for j, pair in enumerate(_unstack_heads(o, h, lo)):
                    p = 2 * h + j
                    o_ref[b, :, LANES * p:LANES * (p + 1)] = pair.astype(BF16)
        _pool_tile(n, tp, nseq, u_ref, prev_ref, w_ref, s_ref, diff_ref, y_ref)

    cur = lambda n: (0, n, 0)
    prv = lambda n: (0, jnp.maximum(n - 1, 0), 0)
    kv = lambda m: pl.BlockSpec((bl, BLOCK, KV_WIDTH), m)
    res = _call(
        body, name="mixers_fwd", grid=(nb,),
        in_specs=[pl.BlockSpec(memory_space=pltpu.SMEM), pl.BlockSpec((bl, BLOCK, ATTN_WIDTH), cur),
                  kv(prv), kv(cur), kv(prv), kv(cur)] + _pool_specs(tp),
        out_specs=[pl.BlockSpec((bl, BLOCK, ATTN_WIDTH), cur), _rows(tp, POOL_WIDTH), _rows(tp, POOL_WIDTH)],
        out_shape=[_sds((bl, seq, ATTN_WIDTH), BF16), _sds((T, POOL_WIDTH), BF16), _sds((T, POOL_WIDTH), BF16)],
        args=(sinks, *_by_example(bl, q, k, k, v, v), u, u, w_pool, pool_scale), sem=("parallel",),
        exchanges=exchanges)
    outs, rest = res if exchanges else (res, None)
    return [outs[0].reshape(T, ATTN_WIDTH), outs[1], outs[2]], rest


def _branch(y, w_ref):
    return jnp.concatenate([_dot(y, w_ref[j]) for j in range(N_CHIPS)], axis=1)


def _merge_out(y_pool, y_attn, gate, x2, w_bp, w_ba, w_out, g2, g3, exchanges=()):
    T = x2.shape[0]
    tm = min(TM, T)

    def body(yp_ref, ya_ref, gate_ref, x_ref, wbp_ref, wba_ref, wo_ref, g2_ref, g3_ref,
             mg_ref, mix_ref, x1_ref, h2_ref):
        bp, ba = _branch(yp_ref[...], wbp_ref), _branch(ya_ref[...], wba_ref)
        merged = (gate_ref[:, :D_MODEL].astype(F32) * bp + gate_ref[:, D_MODEL:].astype(F32) * ba).astype(BF16)
        mg_ref[...] = merged
        mix = _dot(merged, wo_ref[...])
        mix_ref[...] = mix
        x1 = x_ref[...] + mix * _rms(mix) * g2_ref[...]
        x1_ref[...] = x1
        h2_ref[...] = (x1 * _rms(x1) * g3_ref[...]).astype(BF16)

    return _call(
        body, name="merge_out", grid=(T // tm,),
        in_specs=[_rows(tm, POOL_WIDTH), _rows(tm, ATTN_WIDTH), _rows(tm, GATE_WIDTH), _rows(tm, D_MODEL),
                  _const(w_bp.shape), _const(w_ba.shape), _const((D_MODEL, D_MODEL)),
                  _const((1, D_MODEL)), _const((1, D_MODEL))],
        out_specs=[_rows(tm, D_MODEL)] * 4,
        out_shape=[_sds((T, D_MODEL), BF16), _sds((T, D_MODEL), F32), _sds((T, D_MODEL), F32),
                   _sds((T, D_MODEL), BF16)],
        args=(y_pool, y_attn, gate, x2, w_bp, w_ba, w_out, g2, g3), sem=("parallel",), exchanges=exchanges)


HALF = D_MODEL // 2
TM_MLP = 256


def _mlp_core(h2, x1, mix, tgt, w_up, w_down, g4, g3, g2):
    T = h2.shape[0]
    tm = min(TM_MLP, T)

    def body(h_ref, x1_ref, mix_ref, t_ref, g_ref, g3_ref, g2_ref, ua_hbm, ub_hbm, da_hbm, db_hbm,
             act_ref, dff_ref, dup_ref, dx1_ref, dmix_ref, loss_ref, dg_ref, dg3_ref, dg2_ref,
             wu, wd, relu_scr, sems):
        def weight_copy(i):
            src, dst = ((ua_hbm, wu.at[:, :HALF]), (ub_hbm, wu.at[:, HALF:]),
                        (da_hbm, wd.at[:, :HALF]), (db_hbm, wd.at[:, HALF:]))[i]
            return pltpu.make_async_copy(src, dst, sems.at[i])

        @pl.when(pl.program_id(0) == 0)
        def _():
            for i in range(4):
                weight_copy(i).start()
            loss_ref[...] = jnp.zeros_like(loss_ref)
            for ref in (dg_ref, dg3_ref, dg2_ref):
                ref[...] = jnp.zeros_like(ref)
            weight_copy(0).wait()
            weight_copy(1).wait()

        h = h_ref[...]
        ff = None
        for j in range(N_CHIPS):
            lo = D_MODEL * j
            relu = jnp.maximum(_dot(h, wu[j]), 0.0)
            if j == 0:
                @pl.when(pl.program_id(0) == 0)
                def _():
                    weight_copy(2).wait()
                    weight_copy(3).wait()
            relu_scr[:, lo:lo + D_MODEL] = relu
            act = jnp.square(relu).astype(BF16)
            act_ref[:, lo:lo + D_MODEL] = act
            t = _dot(act, wd[j])
            ff = t if ff is None else ff + t
        g = g_ref[...]
        x1 = x1_ref[...]
        err = x1 + ff * _rms(ff) * g - t_ref[...]
        loss_ref[...] += jnp.sum(err * err) * (0.5 / D_MODEL)
        dy = err * (1.0 / D_MODEL)
        dff, dg = _norm_bwd(ff, g, dy)
        dg_ref[...] += dg
        dff = dff.astype(BF16)
        dff_ref[...] = dff
        dh2 = None
        for j in range(N_CHIPS):
            lo = D_MODEL * j
            dup = (_dot_nt(dff, wd[j]) * (2.0 * relu_scr[:, lo:lo + D_MODEL])).astype(BF16)
            dup_ref[:, lo:lo + D_MODEL] = dup
            t = _dot_nt(dup, wu[j])
            dh2 = t if dh2 is None else dh2 + t
        dx, dg3 = _norm_bwd(x1, g3_ref[...], dh2)
        dx1 = dy + dx
        dx1_ref[...] = dx1
        dg3_ref[...] += dg3
        dmix, dg2 = _norm_bwd(mix_ref[...], g2_ref[...], dx1)
        dmix_ref[...] = dmix.astype(BF16)
        dg2_ref[...] += dg2

    slabs = pltpu.VMEM((N_CHIPS, D_MODEL, D_MODEL), BF16)
    gain = _const((1, D_MODEL))
    return pl.pallas_call(
        body, name="mlp_core", grid=(T // tm,),
        in_specs=[_rows(tm, D_MODEL)] * 4 + [gain] * 3 + [ANY] * 4,
        out_specs=[_rows(tm, D_FF), _rows(tm, D_MODEL), _rows(tm, D_FF), _rows(tm, D_MODEL), _rows(tm, D_MODEL),
                   _const((8, LANES)), gain, gain, gain],
        out_shape=[_sds((T, D_FF), BF16), _sds((T, D_MODEL), BF16), _sds((T, D_FF), BF16), _sds((T, D_MODEL), F32),
                   _sds((T, D_MODEL), BF16), _sds((8, LANES), F32)] + [_sds((1, D_MODEL), F32)] * 3,
        scratch_shapes=[slabs] * 2 + [pltpu.VMEM((tm, D_FF), F32), pltpu.SemaphoreType.DMA((4,))],
        compiler_params=_cp("arbitrary"),
    )(h2, x1, mix, tgt, g4, g3, g2, *w_up, *w_down)


def _dw(tag, a, g, ta, tn, shard_cols=False, exchanges=()):
    T, ka = a.shape
    n = g.shape[1]
    tk = min(2 * TM, T)
    nk = T // tk

    def body(a_ref, g_ref, o_ref):
        @pl.when(pl.program_id(2) == 0)
        def _():
            o_ref[...] = jnp.zeros_like(o_ref)

        o_ref[...] += _dot_tn(a_ref[...], g_ref[...])

    if shard_cols:
        per = (n // N_CHIPS) // tn
        out_spec = pl.BlockSpec((None, ta, tn), lambda i, j, k: (j // per, i, j % per))
        out_shape = _sds((N_CHIPS, ka, n // N_CHIPS), F32)
    else:
        out_spec = pl.BlockSpec((ta, tn), lambda i, j, k: (i, j))
        out_shape = _sds((ka, n), F32)
    return _call(
        body, name="dw_" + tag, grid=(ka // ta, n // tn, nk),
        in_specs=[pl.BlockSpec((tk, ta), lambda i, j, k: (k, i)), pl.BlockSpec((tk, tn), lambda i, j, k: (k, j))],
        out_specs=[out_spec], out_shape=[out_shape],
        args=(a, g), sem=("parallel", "parallel", "arbitrary"), exchanges=exchanges)


def _dw_mix(merged, dmix, y_pool, dbp, y_attn, dba, exchanges=()):
    T = merged.shape[0]
    tk = min(2 * TM, T)
    c = D_MODEL // N_CHIPS

    def body(mg_ref, dmix_ref, yp_ref, dbp_ref, ya_ref, dba_ref, out_ref, bp_ref, ba_ref):
        @pl.when(pl.program_id(0) == 0)
        def _():
            for ref in (out_ref, bp_ref, ba_ref):
                ref[...] = jnp.zeros_like(ref)

        out_ref[...] += _dot_tn(mg_ref[...], dmix_ref[...])
        for y_ref, d_ref, o_ref in ((yp_ref, dbp_ref, bp_ref), (ya_ref, dba_ref, ba_ref)):
            res = _dot_tn(y_ref[...], d_ref[...])
            for j in range(N_CHIPS):
                o_ref[j] += res[:, c * j:c * (j + 1)]

    slabs = (N_CHIPS, POOL_WIDTH, c)
    return _call(
        body, name="dw_mix", grid=(T // tk,),
        in_specs=[_rows(tk, D_MODEL), _rows(tk, D_MODEL), _rows(tk, POOL_WIDTH), _rows(tk, D_MODEL),
                  _rows(tk, ATTN_WIDTH), _rows(tk, D_MODEL)],
        out_specs=[_const((D_MODEL, D_MODEL)), _const(slabs), _const(slabs)],
        out_shape=[_sds((D_MODEL, D_MODEL), F32), _sds(slabs, F32), _sds(slabs, F32)],
        args=(merged, dmix, y_pool, dbp, y_attn, dba), sem=("arbitrary",), exchanges=exchanges)


def _merge_bwd(dmix, gate, y_pool, y_attn, w_out, w_bp, w_ba, exchanges=()):
    T = dmix.shape[0]
    tm = min(TM, T)

    def body(dmix_ref, gate_ref, yp_ref, ya_ref, wo_ref, wbp_ref, wba_ref,
             dbp_ref, dba_ref, dgate_ref, dyp_ref, dya_ref):
        dm = _dot_nt(dmix_ref[...], wo_ref[...])
        for j, (y_ref, db_ref, w_ref, dy_ref) in enumerate(
                ((yp_ref, dbp_ref, wbp_ref, dyp_ref), (ya_ref, dba_ref, wba_ref, dya_ref))):
            sl = slice(D_MODEL * j, D_MODEL * (j + 1))
            gt = gate_ref[:, sl].astype(F32)
            db = (dm * gt).astype(BF16)
            db_ref[...] = db
            dgate_ref[:, sl] = (dm * _branch(y_ref[...], w_ref) * gt * (1.0 - gt)).astype(BF16)
            cw = D_MODEL // N_CHIPS
            dy = _dot_nt(db[:, :cw], w_ref[0])
            for c in range(1, N_CHIPS):
                dy = dy + _dot_nt(db[:, cw * c:cw * (c + 1)], w_ref[c])
            dy_ref[...] = dy.astype(dy_ref.dtype)

    return _call(
        body, name="merge_bwd", grid=(T // tm,),
        in_specs=[_rows(tm, D_MODEL), _rows(tm, GATE_WIDTH), _rows(tm, POOL_WIDTH), _rows(tm, ATTN_WIDTH),
                  _const((D_MODEL, D_MODEL)), _const(w_bp.shape), _const(w_ba.shape)],
        out_specs=[_rows(tm, D_MODEL), _rows(tm, D_MODEL), _rows(tm, GATE_WIDTH), _rows(tm, POOL_WIDTH),
                   _rows(tm, ATTN_WIDTH)],
        out_shape=[_sds((T, D_MODEL), BF16), _sds((T, D_MODEL), BF16), _sds((T, GATE_WIDTH), BF16),
                   _sds((T, POOL_WIDTH), F32), _sds((T, ATTN_WIDTH), BF16)],
        args=(dmix, gate, y_pool, y_attn, w_out, w_bp, w_ba), sem=("parallel",), exchanges=exchanges)


def _mixers_bwd(q, k, v, do, sinks, tabs, dyp, diff, w_pool, pool_scale, seq, exchanges=()):
    T = q.shape[0]
    nb = seq // BLOCK
    bl = T // seq
    steps = nb + 1
    tp = T // nb
    nseq = seq // tp
    per = tp // HALO
    last_halo = T // HALO - 1

    def body(sink_ref, q_ref, do_ref, kp_ref, kc_ref, vp_ref, vc_ref, c_ref, a_ref, bt_ref, cp_ref, ap_ref, btp_ref,
             dy_ref, nxt_ref, diff_ref, w_ref, s_ref,
             dq_ref, dk_ref, dv_ref, dsink_ref, du_ref, dw_ref, ds_ref, ck_ref, cv_ref):
        n = pl.program_id(0)

        @pl.when(n == 0)
        def _():
            for ref in (dsink_ref, ck_ref, cv_ref, dw_ref, ds_ref):
                ref[...] = jnp.zeros_like(ref)

        @pl.when(n < nb)
        def _():
            _pool_bwd_tile(n, tp, nseq, dy_ref, nxt_ref, diff_ref, w_ref, s_ref, du_ref, dw_ref, ds_ref)
            valid, lo = _attn_masks(n)
            for b in range(bl):
                kk = jnp.concatenate([kp_ref[b], kc_ref[b]], axis=0)
                vv = jnp.concatenate([vp_ref[b], vc_ref[b]], axis=0)
                dk_acc = jnp.zeros((2 * BLOCK, KV_WIDTH), F32)
                dv_acc = jnp.zeros((2 * BLOCK, KV_WIDTH), F32)
                for h in range(2):
                    qs = _stack_heads(q_ref.at[b], h, lo)
                    dos = _stack_heads(do_ref.at[b], h, lo)
                    pr, ps = _group_probs(qs, kk, valid, _sink_rows(sink_ref, h))
                    dp = _dot_nt(dos, vv)
                    delta = jnp.sum(pr * dp, axis=1, keepdims=True)
                    ds = (pr * (dp - delta)).astype(BF16)
                    dsk = ps * delta
                    for g in range(GROUP):
                        idx = GROUP * h + g
                        dsink_ref[idx:idx + 1, :] += (jnp.zeros((1, LANES), F32)
                                                      - jnp.sum(dsk[BLOCK * g:BLOCK * (g + 1)]))
                    dk_acc = dk_acc + _dot_tn(ds, qs)
                    dv_acc = dv_acc + _dot_tn(pr.astype(BF16), dos)
                    for j, pair in enumerate(_unstack_heads(_dot(ds, kk) * SCALE, h, lo)):
                        sl = slice(LANES * (2 * h + j), LANES * (2 * h + j + 1))
                        dq_ref[b, :, sl] = _rot_bwd(pair, c_ref[...], a_ref[...], bt_ref[...]).astype(BF16)
                fin_k = ck_ref[b] + dk_acc[:BLOCK]
                dk_ref[b] = _rot_bwd(fin_k, cp_ref[...], ap_ref[...], btp_ref[...]).astype(BF16)
                dv_ref[b] = (cv_ref[b] + dv_acc[:BLOCK]).astype(BF16)
                ck_ref[b] = dk_acc[BLOCK:]
                cv_ref[b] = dv_acc[BLOCK:]

        @pl.when(n == nb)
        def _():
            for b in range(bl):
                dk_ref[b] = _rot_bwd(ck_ref[b], cp_ref[...], ap_ref[...], btp_ref[...]).astype(BF16)
                dv_ref[b] = cv_ref[b].astype(BF16)

    cur = lambda n: (0, jnp.minimum(n, nb - 1), 0)
    prv = lambda n: (0, jnp.clip(n - 1, 0, nb - 1), 0)
    tcur = lambda n: (jnp.minimum(n, nb - 1), 0)
    tprv = lambda n: (jnp.clip(n - 1, 0, nb - 1), 0)
    wide = lambda m: pl.BlockSpec((bl, BLOCK, ATTN_WIDTH), m)
    kv = lambda m: pl.BlockSpec((bl, BLOCK, KV_WIDTH), m)
    tab = lambda m: pl.BlockSpec((BLOCK, LANES), m)
    tile = lambda n: (jnp.minimum(n, nb - 1), 0)
    halo = lambda n: (jnp.minimum((jnp.minimum(n, nb - 1) + 1) * per, last_halo), 0)
    rows = pl.BlockSpec((tp, POOL_WIDTH), tile)
    res = _call(
        body, name="mixers_bwd", grid=(steps,),
        in_specs=[pl.BlockSpec(memory_space=pltpu.SMEM), wide(cur), wide(cur), kv(prv), kv(cur), kv(prv), kv(cur),
                  tab(tcur), tab(tcur), tab(tcur), tab(tprv), tab(tprv), tab(tprv),
                  rows, pl.BlockSpec((HALO, POOL_WIDTH), halo), rows, _const((4, POOL_GC, POOL_GC)),
                  _const((1, POOL_WIDTH))],
        out_specs=[wide(cur), kv(prv), kv(prv), _const((8, LANES)), rows, _const((4, POOL_GC, POOL_GC)),
                   _const((1, POOL_WIDTH))],
        out_shape=[_sds((bl, seq, ATTN_WIDTH), BF16), _sds((bl, seq, KV_WIDTH), BF16),
                   _sds((bl, seq, KV_WIDTH), BF16), _sds((8, LANES), F32), _sds((T, POOL_WIDTH), BF16),
                   _sds((4, POOL_GC, POOL_GC), F32), _sds((1, POOL_WIDTH), F32)],
        scratch=[pltpu.VMEM((bl, BLOCK, KV_WIDTH), F32), pltpu.VMEM((bl, BLOCK, KV_WIDTH), F32)],
        args=(sinks, *_by_example(bl, q, do, k, k, v, v), *tabs, *tabs, dyp, dyp, diff, w_pool, pool_scale),
        sem=("arbitrary",), exchanges=exchanges)
    outs, rest = (res if exchanges else (res, None))
    outs = [outs[0].reshape(T, ATTN_WIDTH), outs[1].reshape(T, KV_WIDTH), outs[2].reshape(T, KV_WIDTH), *outs[3:]]
    return (outs, rest) if exchanges else outs


def _pool_bwd_tile(i, tp, nseq, dy_ref, nxt_ref, diff_ref, w_ref, s_ref, du_ref, dw_ref, ds_ref):
    last = (i % nseq) == nseq - 1
    nxt = jnp.where(last, 0.0, nxt_ref[...])
    ext = jnp.concatenate([dy_ref[...], nxt], axis=0) * s_ref[...]
    pos = (i % nseq) * tp + lax.broadcasted_iota(jnp.int32, (tp + HALO, 1), 0)
    for gi, w in enumerate(POOL_WINDOWS):
        sl = slice(POOL_GC * gi, POOL_GC * (gi + 1))
        wg = w_ref[gi].astype(BF16)
        dmx = ext[:, sl].astype(BF16)
        ddiff = _dot_nt(dmx, wg)
        s = ddiff * _inv_count(pos, w)
        sh = 1
        while sh < w:
            s = s + pltpu.roll(s, tp + HALO - sh, 0)
            sh *= 2
        du_ref[:, sl] = (s[:tp] - ddiff[:tp]).astype(BF16)
        dg = diff_ref[:, sl]
        dw_ref[gi] += _dot_tn(dg, dmx[:tp])
        ds_ref[:, sl] += jnp.sum(dy_ref[:, sl] * _dot(dg, wg), axis=0, keepdims=True)


_PARTS = ((0, C_Q), (C_Q, C_K), (C_K, C_V), (C_V, C_G), (C_G, IN_WIDTH))


def _inproj_bwd(parts, x2, dx1, w_in_t, g1, exchanges=()):
    T = x2.shape[0]
    tm = min(TM, T)

    def body(du_ref, dq_ref, dk_ref, dv_ref, dgt_ref, x_ref, dx1_ref, w_ref, g_ref, gx_ref, dg_ref):
        @pl.when(pl.program_id(0) == 0)
        def _():
            dg_ref[...] = jnp.zeros_like(dg_ref)

        dh = jnp.zeros((tm, D_MODEL), F32)
        for (lo, hi), p_ref in zip(_PARTS, (du_ref, dq_ref, dk_ref, dv_ref, dgt_ref)):
            dh = dh + _dot(p_ref[...], w_ref[lo:hi, :])
        dx, dg = _norm_bwd(x_ref[...], g_ref[...], dh)
        gx_ref[...] = dx1_ref[...] + dx
        dg_ref[...] += dg

    return _call(
        body, name="inproj_bwd", grid=(T // tm,),
        in_specs=[_rows(tm, hi - lo) for lo, hi in _PARTS]
        + [_rows(tm, D_MODEL), _rows(tm, D_MODEL), _const((IN_WIDTH, D_MODEL)), _const((1, D_MODEL))],
        out_specs=[_rows(tm, D_MODEL), _const((1, D_MODEL))],
        out_shape=[_sds((T, D_MODEL), F32), _sds((1, D_MODEL), F32)],
        args=(*parts, x2, dx1, w_in_t, g1), sem=("arbitrary",), exchanges=exchanges)


def _dw_in(h, parts, exchanges=()):
    T = h.shape[0]
    tk = min(TM, T)

    def body(h_ref, du_ref, dq_ref, dk_ref, dv_ref, dgt_ref, o_ref, db_ref):
        @pl.when(pl.program_id(0) == 0)
        def _():
            o_ref[...] = jnp.zeros_like(o_ref)
            db_ref[...] = jnp.zeros_like(db_ref)

        hh = h_ref[...]
        for (lo, hi), p_ref in zip(_PARTS, (du_ref, dq_ref, dk_ref, dv_ref, dgt_ref)):
            part = p_ref[...]
            o_ref[lo:hi, :] += _dot_tn(part, hh)
            db_ref[:, lo:hi] += jnp.sum(part.astype(F32), axis=0, keepdims=True)

    return _call(
        body, name="dw_in", grid=(T // tk,),
        in_specs=[_rows(tk, D_MODEL)] + [_rows(tk, hi - lo) for lo, hi in _PARTS],
        out_specs=[_const((IN_WIDTH, D_MODEL)), _const((1, IN_WIDTH))],
        out_shape=[_sds((IN_WIDTH, D_MODEL), F32), _sds((1, IN_WIDTH), F32)],
        args=(h, *parts), sem=("arbitrary",), exchanges=exchanges)


def _row_tile(rows, cap=256, mult=16):
    best = None
    for t in range(mult, min(rows, cap) + 1, mult):
        if rows % t == 0:
            best = t
    if best is None:
        raise ValueError("no row tile for %d rows" % rows)
    return best


def _pair_sum(ids, full, got):
    _, r, c = full.shape
    hr = r // 2
    tr = _row_tile(hr)
    nblk = hr // tr

    def body(ids_ref, a_ref, b_ref, own_ref, sb_ref):
        s = a_ref[...] + b_ref[...]
        sb_ref[...] = s.astype(BF16)

        @pl.when(pl.program_id(1) == ids_ref[0])
        def _():
            own_ref[...] = s

    slab = pl.BlockSpec((None, tr, c), lambda i, j, ids_ref: (j, i, 0))
    return pl.pallas_call(
        body, name="pair_sum_%dx%d" % (r, c),
        grid_spec=pltpu.PrefetchScalarGridSpec(
            num_scalar_prefetch=1, grid=(nblk, N_CHIPS),
            in_specs=[pl.BlockSpec((None, tr, c), lambda i, j, ids_ref: (j, ids_ref[1] * nblk + i, 0)), slab],
            out_specs=[pl.BlockSpec((tr, c), lambda i, j, ids_ref: (i, 0)), slab]),
        out_shape=[_sds((hr, c), F32), _sds((N_CHIPS, hr, c), BF16)],
        compiler_params=_cp("parallel", "arbitrary"),
    )(ids, full, got)


def _chip_sum(ids, own, got):
    hr, c = own.shape
    tr = _row_tile(hr)
    nblk = hr // tr

    def body(ids_ref, a_ref, b_ref, o_ref):
        o_ref[...] = ((a_ref[...] + b_ref[0].astype(F32)) + b_ref[1].astype(F32)) + b_ref[2].astype(F32)

    return pl.pallas_call(
        body, name="chip_sum_%dx%d" % (hr, c),
        grid_spec=pltpu.PrefetchScalarGridSpec(
            num_scalar_prefetch=1, grid=(nblk,),
            in_specs=[pl.BlockSpec((tr, c), lambda i, ids_ref: (i, 0)),
                      pl.BlockSpec((3, tr, c), lambda i, ids_ref: (0, i, 0))],
            out_specs=pl.BlockSpec((tr, c), lambda i, ids_ref: (ids_ref[1] * nblk + i, 0))),
        out_shape=_sds((2 * hr, c), F32),
        compiler_params=_cp("parallel"),
    )(ids, own, got)


def _adamw_math(w, g, m, v):
    nm = ADAM_B1 * m + (1.0 - ADAM_B1) * g
    nv = ADAM_B2 * v + (1.0 - ADAM_B2) * (g * g)
    m_hat = nm / (1.0 - ADAM_B1 ** ADAM_STEP)
    v_hat = nv / (1.0 - ADAM_B2 ** ADAM_STEP)
    return -ADAM_LR * (m_hat / (jnp.sqrt(v_hat) + ADAM_EPS) + ADAM_WD * w), nm, nv


def _adamw(w, g, m, v):
    r, c = w.shape
    tr = _row_tile(r, cap=512, mult=8)

    def body(w_ref, g_ref, m_ref, v_ref, d_ref, nm_ref, nv_ref):
        d_ref[...], nm_ref[...], nv_ref[...] = _adamw_math(w_ref[...], g_ref[...], m_ref[...], v_ref[...])

    spec = _rows(tr, c)
    return pl.pallas_call(
        body, name="adamw_%dx%d" % (r, c), grid=(r // tr,),
        in_specs=[spec] * 4, out_specs=[spec] * 3, out_shape=[_sds((r, c), F32)] * 3,
        compiler_params=_cp("parallel"),
    )(w, g, m, v)


SC_TILES = 32
SC_LANES = 16
SC_ROWS = 8


def _adamw_sparse(w, g, m, v):
    r, c = w.shape
    rows = r // SC_TILES
    step = min(rows, SC_ROWS)

    def body(w_hbm, g_hbm, m_hbm, v_hbm, d_hbm, nm_hbm, nv_hbm, wb, gb, mb, vb):
        tile = lax.axis_index("sc_subcore") * 2 + lax.axis_index("sc_core")

        @pl.loop(0, rows, step=step)
        def _(r0):
            mine = pl.ds(tile * rows + r0, step)
            for src, dst in ((w_hbm, wb), (g_hbm, gb), (m_hbm, mb), (v_hbm, vb)):
                pltpu.sync_copy(src.at[mine], dst)

            @pl.loop(0, step)
            def _(row):
                @pl.loop(0, c, step=SC_LANES)
                def _(i):
                    at = (row, pl.ds(i, SC_LANES))
                    wb[at], mb[at], vb[at] = _adamw_math(wb[at], gb[at], mb[at], vb[at])

            for src, dst in ((wb, d_hbm), (mb, nm_hbm), (vb, nv_hbm)):
                pltpu.sync_copy(src, dst.at[mine])

    return pl.kernel(
        body, name="adamw_sparse_%dx%d" % (r, c), out_type=[_sds((r, c), F32)] * 3,
        mesh=plsc.VectorSubcoreMesh(core_axis_name="sc_core", subcore_axis_name="sc_subcore"),
        scratch_types=[pltpu.VMEM((step, c), F32)] * 4,
    )(w, g, m, v)


_SMALL_NAMES = ("w_pool", "b_in", "g_mix_pre", "g_mix_post", "g_mlp_pre", "g_mlp_post", "pool_scale", "attn_sinks")
B_ROWS = -(-IN_WIDTH // D_MODEL)


def _row_block(rows):
    rows = [jnp.pad(r.astype(F32), ((0, 0), (0, D_MODEL - r.shape[1]))) for r in rows]
    return jnp.pad(jnp.concatenate(rows, axis=0), ((0, 8 - len(rows)), (0, 0)))


def _early_block(dg2, dg3, dg4, dps, dsink, loss):
    tail = jnp.concatenate([jnp.pad(dsink.reshape(1, -1), ((0, 0), (0, LANES - dsink.size))),
                            jnp.pad(loss.reshape(1, 1), ((0, 0), (0, LANES - 1)))], axis=1)
    return _row_block([dg2, dg3, dg4, dps, tail])


def _late_block(db_in, dg1):
    b = jnp.pad(db_in, ((0, 0), (0, B_ROWS * D_MODEL - IN_WIDTH))).reshape(B_ROWS, D_MODEL)
    return _row_block([b[r:r + 1] for r in range(B_ROWS)] + [dg1])


def _small_update(gearly, gmat, glate, w, m, v):
    names = _SMALL_NAMES
    n = len(names)

    def total(ref, rows):
        acc = ref[0:rows, :]
        for d in range(1, N_DEV):
            acc = acc + ref[d * rows:(d + 1) * rows, :]
        return acc

    def body(*refs):
        early_ref, gmat_ref, late_ref = refs[:3]
        w_refs, m_refs, v_refs = refs[3:3 + n], refs[3 + n:3 + 2 * n], refs[3 + 2 * n:3 + 3 * n]
        outs = refs[3 + 3 * n:]
        loss_ref, g_refs, d_refs = outs[0], outs[1:1 + n], outs[1 + n:1 + 2 * n]
        nm_refs, nv_refs = outs[1 + 2 * n:1 + 3 * n], outs[1 + 3 * n:1 + 4 * n]
        early, late = total(early_ref, 8), total(late_ref, 8)
        loss_ref[...] = jnp.sum(early[4:5, LANES:2 * LANES], axis=1, keepdims=True)
        bias = jnp.concatenate([late[r:r + 1, :] for r in range(B_ROWS - 1)]
                               + [late[B_ROWS - 1:B_ROWS, :IN_WIDTH - (B_ROWS - 1) * D_MODEL]], axis=1)
        grad = dict(b_in=bias, g_mix_pre=late[B_ROWS:B_ROWS + 1, :], g_mix_post=early[0:1, :],
                    g_mlp_pre=early[1:2, :], g_mlp_post=early[2:3, :], pool_scale=early[3:4, :POOL_WIDTH],
                    attn_sinks=early[4:5, :N_Q_HEADS])
        for i, name in enumerate(names):
            g = total(gmat_ref, 4 * POOL_GC) if name == "w_pool" else grad[name]
            g_refs[i][...] = g
            d_refs[i][...], nm_refs[i][...], nv_refs[i][...] = _adamw_math(
                w_refs[i][...], g, m_refs[i][...], v_refs[i][...])

    shapes = [_sds(w[k].shape, F32) for k in names]
    res = pl.pallas_call(
        body, name="small_update", out_shape=[_sds((1, 1), F32)] + shapes * 4,
        compiler_params=pltpu.CompilerParams(vmem_limit_bytes=VMEM_MB * 1024 * 1024),
    )(gearly, gmat, glate, *[w[k] for k in names], *[m[k] for k in names], *[v[k] for k in names])
    loss = res[0]
    per = {k: tuple(res[1 + j * n + i] for j in range(4)) for i, k in enumerate(names)}
    return loss, per


_BIG = ("w_in", "w_branch_pool", "w_branch_attn", "w_out", "w_up", "w_down")
_ORDER = ("g_mix_pre", "w_in", "b_in", "w_pool", "pool_scale", "attn_sinks", "w_branch_pool", "w_branch_attn",
          "w_out", "g_mix_post", "g_mlp_pre", "w_up", "w_down", "g_mlp_post")


def _stack_rows(slab):
    return slab.reshape(-1, slab.shape[2])


def _step(x2, tgt, seq, shards, small, ids):
    tabs = _rope_tables(seq)
    g1, g2, g3, g4 = (small[n] for n in ("g_mix_pre", "g_mix_post", "g_mlp_pre", "g_mlp_post"))
    sinks = small["attn_sinks"].reshape(N_Q_HEADS)
    w_pool = small["w_pool"].reshape(4, POOL_GC, POOL_GC)
    pool_scale = small["pool_scale"]

    def whole(shard, slabs):
        return lax.dynamic_update_slice(slabs, shard[None], (ids[0], 0, 0))

    up_a, up_b = shards["w_up"][:HALF], shards["w_up"][HALF:]
    down_a, down_b = shards["w_down"][:HALF], shards["w_down"][HALF:]
    w_in = _stack_rows(whole(shards["w_in"], _alone("gather_in", _ex_gather([shards["w_in"]]))[0][0]))
    mix_shards = [shards[n] for n in ("w_branch_pool", "w_branch_attn", "w_out")]
    (h, u, q, k, v, gate), [mix_slabs] = _inproj(
        x2, g1, w_in, small["b_in"], tabs, seq, exchanges=[_ex_gather(mix_shards)])
    w_bp, w_ba, out_slab = (whole(s, g) for s, g in zip(mix_shards, mix_slabs))
    w_out = _stack_rows(out_slab)
    (y_attn, diff, y_pool), [[got_a, got_b]] = _mixers_fwd(
        q, k, v, sinks, u, w_pool, pool_scale, seq, exchanges=[_ex_gather([up_a, up_b])])
    (merged, mix, x1, h2), [[got_c, got_d]] = _merge_out(
        y_pool, y_attn, gate, x2, w_bp, w_ba, w_out, g2, g3, exchanges=[_ex_gather([down_a, down_b])])
    w_up = (whole(up_a, got_a), whole(up_b, got_b))
    w_down = (whole(down_a, got_c), whole(down_b, got_d))
    act, dff, dup, dx1, dmix, loss_acc, dg4, dg3, dg2 = _mlp_core(h2, x1, mix, tgt, w_up, w_down, g4, g3, g2)

    dw_down = _dw("down", act, dff, 1024, 1024)[0].reshape(N_CHIPS, D_FF // N_CHIPS, D_MODEL)
    (dbp, dba, dgate, dyp, dya), [[got]] = _merge_bwd(
        dmix, gate, y_pool, y_attn, w_out, w_bp, w_ba, exchanges=[_ex_pair([dw_down])])
    ps_down = _pair_sum(ids, dw_down, got)
    (dw_up,), [[got]] = _dw("up", h2, dup, 1024, 1024, shard_cols=True, exchanges=[_ex_chip([ps_down[1]])])
    half_down = _chip_sum(ids, ps_down[0], got)
    (dw_out, dw_bp, dw_ba), [[got]] = _dw_mix(merged, dmix, y_pool, dbp, y_attn, dba, exchanges=[_ex_pair([dw_up])])
    ps_up = _pair_sum(ids, dw_up, got)
    dw_mix = [dw_out.reshape(N_CHIPS, D_MODEL // N_CHIPS, D_MODEL), dw_bp, dw_ba]
    (dq, dk, dv, dsink, du, dw_pool, dps), [[got], gots, [g_down]] = _mixers_bwd(
        q, k, v, dya, sinks, tabs, dyp, diff, w_pool, pool_scale, seq,
        exchanges=[_ex_chip([ps_up[1]]), _ex_pair(dw_mix), _ex_swap([half_down])])
    half_up = _chip_sum(ids, ps_up[0], got)
    ps_mix = [_pair_sum(ids, d, g) for d, g in zip(dw_mix, gots)]
    parts = (du, dq, dk, dv, dgate)
    early = _early_block(dg2, dg3, dg4, dps, dsink[:, 0], loss_acc[0, 0])
    mat = dw_pool.reshape(4 * POOL_GC, POOL_GC)
    (dw_in_t, db_in), [gots, [gearly, gmat], [g_up]] = _dw_in(
        h, parts, exchanges=[_ex_chip([p[1] for p in ps_mix]), _ex_allgather([early, mat]), _ex_swap([half_up])])
    half_mix = [_chip_sum(ids, p[0], g) for p, g in zip(ps_mix, gots)]
    dw_in = dw_in_t.reshape(N_CHIPS, IN_WIDTH // N_CHIPS, D_MODEL)
    g_mix, [got] = _alone("swap_mix_pair_in", _ex_swap(half_mix), _ex_pair([dw_in]))
    ps_in = _pair_sum(ids, dw_in, got)
    (gx, dg1), [[got]] = _inproj_bwd(parts, x2, dx1, w_in, g1, exchanges=[_ex_chip([ps_in[1]])])
    [g_in], [glate] = _alone("swap_in_allgather", _ex_swap([_chip_sum(ids, ps_in[0], got)]),
                             _ex_allgather([_late_block(db_in, dg1)]))

    grads = dict(w_in=g_in, w_branch_pool=g_mix[1], w_branch_attn=g_mix[2], w_out=g_mix[0], w_up=g_up, w_down=g_down)
    return (gearly, gmat, glate), gx, grads


def kernel(x, g_mix_pre, w_in, b_in, w_pool, pool_scale, attn_sinks, w_branch_pool, w_branch_attn, w_out, g_mix_post, g_mlp_pre, w_up, w_down, g_mlp_post, loss_target, m_g_mix_pre, m_w_in, m_b_in, m_w_pool, m_pool_scale, m_attn_sinks, m_w_branch_pool, m_w_branch_attn, m_w_out, m_g_mix_post, m_g_mlp_pre, m_w_up, m_w_down, m_g_mlp_post, v_g_mix_pre, v_w_in, v_b_in, v_w_pool, v_pool_scale, v_attn_sinks, v_w_branch_pool, v_w_branch_attn, v_w_out, v_g_mix_post, v_g_mlp_pre, v_w_up, v_w_down, v_g_mlp_post):
    weights = dict(g_mix_pre=g_mix_pre, w_in=w_in, b_in=b_in, w_pool=w_pool, pool_scale=pool_scale,
                   attn_sinks=attn_sinks, w_branch_pool=w_branch_pool, w_branch_attn=w_branch_attn, w_out=w_out,
                   g_mix_post=g_mix_post, g_mlp_pre=g_mlp_pre, w_up=w_up, w_down=w_down, g_mlp_post=g_mlp_post)
    mom1 = dict(g_mix_pre=m_g_mix_pre, w_in=m_w_in, b_in=m_b_in, w_pool=m_w_pool, pool_scale=m_pool_scale,
                attn_sinks=m_attn_sinks, w_branch_pool=m_w_branch_pool, w_branch_attn=m_w_branch_attn,
                w_out=m_w_out, g_mix_post=m_g_mix_post, g_mlp_pre=m_g_mlp_pre, w_up=m_w_up, w_down=m_w_down,
                g_mlp_post=m_g_mlp_post)
    mom2 = dict(g_mix_pre=v_g_mix_pre, w_in=v_w_in, b_in=v_b_in, w_pool=v_w_pool, pool_scale=v_pool_scale,
                attn_sinks=v_attn_sinks, w_branch_pool=v_w_branch_pool, w_branch_attn=v_w_branch_attn,
                w_out=v_w_out, g_mix_post=v_g_mix_post, g_mlp_pre=v_g_mlp_pre, w_up=v_w_up, w_down=v_w_down,
                g_mlp_post=v_g_mlp_post)
    b_loc, seq, _ = x.shape
    x2 = x.reshape(b_loc * seq, D_MODEL)
    tgt = loss_target.reshape(b_loc * seq, D_MODEL)
    ids = jnp.stack([2 * lax.axis_index("x") + lax.axis_index("y"), lax.axis_index("c")]).astype(jnp.int32)

    def flat(n, a):
        return a[0].T if n == "w_in" else a[0]

    def unflat(n, a):
        return (a.T if n == "w_in" else a)[None]

    shards = {n: flat(n, weights[n]).astype(BF16) for n in _BIG}
    small = {n: weights[n] for n in _ORDER if n not in _BIG}
    (gearly, gmat, glate), gx, grads = _step(x2, tgt, seq, shards, small, ids)

    def two_d(src):
        return {n: src[n].reshape(4 * POOL_GC, POOL_GC) if n == "w_pool" else src[n] for n in _SMALL_NAMES}

    loss, per = _small_update(gearly, gmat, glate, two_d(weights), two_d(mom1), two_d(mom2))
    delta, new_m, new_v = {}, {}, {}
    for n in _SMALL_NAMES:
        grads[n], delta[n], new_m[n], new_v[n] = (a.reshape(weights[n].shape) for a in per[n])
    for n in _BIG:
        update = _adamw if n == "w_in" else _adamw_sparse
        d, nm, nv = update(flat(n, weights[n]), grads[n], flat(n, mom1[n]), flat(n, mom2[n]))
        grads[n] = unflat(n, grads[n])
        delta[n], new_m[n], new_v[n] = unflat(n, d), unflat(n, nm), unflat(n, nv)

    return (loss[0, 0], gx.reshape(x.shape), *[grads[n] for n in _ORDER], *[delta[n] for n in _ORDER],
            *[new_m[n] for n in _ORDER], *[new_v[n] for n in _ORDER])
```

```python
import jax
import jax.numpy as jnp
from jax import lax
from jax.experimental import pallas as pl
from jax.experimental.pallas import tpu as pltpu
from jax.experimental.pallas import tpu_sc as plsc

F32 = jnp.float32
BF16 = jnp.bfloat16

D_MODEL = 1024
POOL_WINDOWS = (2, 4, 8, 16)
POOL_WIDTH = 512
POOL_GC = 128
HALO = 16
HEAD_DIM = 64
N_Q_HEADS = 8
ATTN_WIDTH = 512
KV_WIDTH = 128
BLOCK = 128
NEG_INF = -1e30
ROPE_THETA = 500000.0
ROT_DIM = 16
GATE_WIDTH = 2048
IN_WIDTH = 3328
D_FF = 4096
EPS = 1e-6
SCALE = HEAD_DIM ** -0.5
C_Q, C_K, C_V, C_G = 512, 1024, 1152, 1280

ADAM_LR, ADAM_B1, ADAM_B2, ADAM_EPS, ADAM_WD, ADAM_STEP = 0.001, 0.9, 0.999, 1e-08, 0.01, 10

N_CHIPS = 4
N_DEV = 8
LANES = 128
TM = 512
VMEM_MB = 56

MESH = pl.DeviceIdType.MESH
ANY = pl.BlockSpec(memory_space=pl.ANY)


def _cp(*sem, vmem=VMEM_MB):
    return pltpu.CompilerParams(dimension_semantics=sem, vmem_limit_bytes=vmem * 1024 * 1024)


def _rows(tile, cols):
    return pl.BlockSpec((tile, cols), lambda i: (i, 0))


def _const(shape):
    nd = len(shape)
    return pl.BlockSpec(shape, lambda i: (0,) * nd)


def _sds(shape, dtype):
    return jax.ShapeDtypeStruct(shape, dtype)


def _dot(a, b):
    return jnp.dot(a, b, preferred_element_type=F32)


def _dot_nt(a, b):
    return lax.dot_general(a, b, (((1,), (1,)), ((), ())), preferred_element_type=F32)


def _dot_tn(a, b):
    return lax.dot_general(a, b, (((0,), (0,)), ((), ())), preferred_element_type=F32)


def _rms(x):
    return lax.rsqrt(jnp.mean(x * x, axis=-1, keepdims=True) + EPS)


def _norm_bwd(x, g, dout):
    r = _rms(x)
    n = x * r
    dn = dout * g
    dx = r * (dn - n * jnp.mean(dn * n, axis=-1, keepdims=True))
    return dx, jnp.sum(dout * n, axis=0, keepdims=True)


def _rot_fwd(t, c, a, bt):
    return t * c + pltpu.roll(t, LANES - 8, 1) * a + pltpu.roll(t, 8, 1) * bt


def _rot_bwd(d, c, a, bt):
    return d * c + pltpu.roll(d * a, 8, 1) + pltpu.roll(d * bt, LANES - 8, 1)


def _rope_tables(seq):
    pos = jnp.arange(seq, dtype=F32)
    inv_freq = ROPE_THETA ** (-jnp.arange(0, ROT_DIM, 2, dtype=F32) / ROT_DIM)
    ang = pos[:, None] * inv_freq[None, :]
    cos, sin = jnp.cos(ang), jnp.sin(ang)
    ones = jnp.ones((seq, HEAD_DIM - ROT_DIM), F32)
    zeros8 = jnp.zeros((seq, 8), F32)
    zrest = jnp.zeros((seq, HEAD_DIM - ROT_DIM), F32)
    c = jnp.concatenate([cos, cos, ones], axis=1)
    a = jnp.concatenate([-sin, zeros8, zrest], axis=1)
    bt = jnp.concatenate([zeros8, sin, zrest], axis=1)
    return tuple(jnp.tile(t, (1, 2)) for t in (c, a, bt))


class _Exchange:
    def __init__(self, inputs, out_shapes, sems, start, finish, aliases=None, middle=None):
        self.inputs, self.out_shapes, self.sems = list(inputs), list(out_shapes), list(sems)
        self.start, self.finish, self.aliases = start, finish, dict(aliases or {})
        self.middle = middle


def _call(body, *, name, grid, in_specs, out_specs, out_shape, args, scratch=(), sem=(), exchanges=()):
    in_specs, out_specs, out_shape, scratch = list(in_specs), list(out_specs), list(out_shape), list(scratch)
    if not exchanges:
        return pl.pallas_call(body, name=name, grid=grid, in_specs=in_specs, out_specs=out_specs,
                              out_shape=out_shape, scratch_shapes=scratch, compiler_params=_cp(*sem))(*args)
    n_in, n_out, n_scr = len(in_specs), len(out_specs), len(scratch)
    x_in = [a for ex in exchanges for a in ex.inputs]
    x_out = [s for ex in exchanges for s in ex.out_shapes]
    x_sem = [s for ex in exchanges for s in ex.sems]
    aliases, i_off, o_off = {}, n_in, n_out
    for ex in exchanges:
        for i, o in ex.aliases.items():
            aliases[i_off + i] = o_off + o
        i_off += len(ex.inputs)
        o_off += len(ex.out_shapes)

    def split(flat):
        out, pos = [], 0
        for ex, n in zip(exchanges, flat[1]):
            out.append(flat[0][pos:pos + n])
            pos += n
        return out

    def carrier(*refs):
        pos = 0
        groups = []
        for n in (n_in, len(x_in), n_out, len(x_out), n_scr, len(x_sem)):
            groups.append(refs[pos:pos + n])
            pos += n
        ins, xin, outs, xout, scr, xsem = groups
        xin = split((xin, [len(ex.inputs) for ex in exchanges]))
        xout = split((xout, [len(ex.out_shapes) for ex in exchanges]))
        xsem = split((xsem, [len(ex.sems) for ex in exchanges]))
        first = pl.program_id(0) == 0
        last = pl.program_id(0) == grid[0] - 1
        for d in range(1, len(grid)):
            first = jnp.logical_and(first, pl.program_id(d) == 0)
            last = jnp.logical_and(last, pl.program_id(d) == grid[d] - 1)

        @pl.when(first)
        def _():
            for ex, i, o, s in zip(exchanges, xin, xout, xsem):
                ex.start(i, o, s)

        if any(ex.middle for ex in exchanges):
            half = pl.program_id(0) == 5 * grid[0] // 8
            for d in range(1, len(grid)):
                half = jnp.logical_and(half, pl.program_id(d) == 0)

            @pl.when(half)
            def _():
                for ex, i, o, s in zip(exchanges, xin, xout, xsem):
                    if ex.middle:
                        ex.middle(i, o, s)

        body(*ins, *outs, *scr)

        @pl.when(last)
        def _():
            for ex, i, o, s in zip(exchanges, xin, xout, xsem):
                ex.finish(i, o, s)

    res = pl.pallas_call(
        carrier, name=name, grid=grid, in_specs=in_specs + [ANY] * len(x_in),
        out_specs=out_specs + [ANY] * len(x_out), out_shape=out_shape + x_out,
        scratch_shapes=scratch + x_sem, input_output_aliases=aliases,
        compiler_params=_cp(*(["arbitrary"] * len(grid))),
    )(*args, *x_in)
    return res[:n_out], split((res[n_out:], [len(ex.out_shapes) for ex in exchanges]))


def _alone(name, *exchanges):
    n_in = [len(ex.inputs) for ex in exchanges]
    n_out = [len(ex.out_shapes) for ex in exchanges]
    n_sem = [len(ex.sems) for ex in exchanges]
    aliases, i_off, o_off = {}, 0, 0
    for ex in exchanges:
        for i, o in ex.aliases.items():
            aliases[i_off + i] = o_off + o
        i_off += len(ex.inputs)
        o_off += len(ex.out_shapes)

    def split(flat, counts):
        out, pos = [], 0
        for n in counts:
            out.append(flat[pos:pos + n])
            pos += n
        return out

    def body(*refs):
        ins, outs, sems = split(refs, [sum(n_in), sum(n_out), sum(n_sem)])
        groups = list(zip(exchanges, split(ins, n_in), split(outs, n_out), split(sems, n_sem)))
        for ex, i, o, s in groups:
            ex.start(i, o, s)
        for ex, i, o, s in groups:
            if ex.middle:
                ex.middle(i, o, s)
        for ex, i, o, s in groups:
            ex.finish(i, o, s)

    res = pl.pallas_call(
        body, name=name, in_specs=[ANY] * sum(n_in), out_specs=[ANY] * sum(n_out),
        out_shape=[s for ex in exchanges for s in ex.out_shapes],
        scratch_shapes=[s for ex in exchanges for s in ex.sems], input_output_aliases=aliases,
    )(*[a for ex in exchanges for a in ex.inputs])
    return split(res, n_out)


def _place():
    x, y, c = lax.axis_index("x"), lax.axis_index("y"), lax.axis_index("c")
    chips = [(1 - x, y), (x, 1 - y), (1 - x, 1 - y)]
    return x, y, c, chips


def _remote(src, dst, send, recv, to):
    return pltpu.make_async_remote_copy(src_ref=src, dst_ref=dst, send_sem=send, recv_sem=recv,
                                        device_id=to, device_id_type=MESH)


def _ex_gather(shards):
    nw = len(shards)
    hrs = [s.shape[0] // 2 for s in shards]

    def copies(ins, outs, sems):
        s1, r1, s2, r2, fs, fr = sems
        x, y, c, _ = _place()
        me, xn, yn, dg = (x, y), (1 - x, y), (x, 1 - y), (1 - x, 1 - y)
        nbr = (xn, yn)
        sibling = (x, y, 1 - c)

        def piece(w, chip, core, part=None):
            hr = hrs[w]
            rows = pl.ds(core * hr, hr) if part is None else pl.ds(core * hr + part * (hr // 2), hr // 2)
            return outs[w].at[2 * chip[0] + chip[1], rows]

        def first(w, k):
            return _remote(ins[w].at[pl.ds(c * hrs[w], hrs[w])], piece(w, me, c), s1.at[w, k], r1.at[w, k],
                           (*nbr[k], c))

        def landed(w, k):
            return _remote(piece(w, nbr[k], c), piece(w, nbr[k], c), s1.at[w, k], r1.at[w, k], (*nbr[k], c))

        def onward(w, k):
            return _remote(piece(w, nbr[k], c, k), piece(w, nbr[k], c, k), s2.at[w, k], r2.at[w, k],
                           (*nbr[1 - k], c))

        def arrived(w, k):
            return _remote(piece(w, dg, c, k), piece(w, dg, c, k), s2.at[w, k], r2.at[w, k], (*nbr[1 - k], c))

        def passed(w, j):
            chip = (xn, yn, dg)[j]
            return _remote(piece(w, chip, c), piece(w, chip, c), fs.at[w, j], fr.at[w, j], sibling)

        def handed(w, j):
            chip = (xn, yn, dg)[j]
            return _remote(piece(w, chip, 1 - c), piece(w, chip, 1 - c), fs.at[w, j], fr.at[w, j], sibling)

        return first, landed, onward, arrived, passed, handed

    def start(ins, outs, sems):
        first = copies(ins, outs, sems)[0]
        for w in range(nw):
            for k in range(2):
                first(w, k).start()

    def middle(ins, outs, sems):
        _, landed, onward, _, passed, _ = copies(ins, outs, sems)
        for w in range(nw):
            for k in range(2):
                landed(w, k).wait_recv()
                onward(w, k).start()
                passed(w, k).start()

    def finish(ins, outs, sems):
        first, _, onward, arrived, passed, handed = copies(ins, outs, sems)
        for w in range(nw):
            for k in range(2):
                arrived(w, k).wait_recv()
            passed(w, 2).start()
        for w in range(nw):
            for j in range(3):
                handed(w, j).wait_recv()
        for w in range(nw):
            for k in range(2):
                first(w, k).wait_send()
                onward(w, k).wait_send()
            for j in range(3):
                passed(w, j).wait_send()

    return _Exchange(shards, [_sds((N_CHIPS,) + s.shape, s.dtype) for s in shards],
                     [pltpu.SemaphoreType.DMA((nw, 2))] * 4 + [pltpu.SemaphoreType.DMA((nw, 3))] * 2,
                     start, finish, middle=middle)


def _ex_pair(grads):
    nw = len(grads)

    def copies(ins, outs, sems):
        x, y, c, _ = _place()
        out = []
        for w in range(nw):
            hr = grads[w].shape[1] // 2
            out.append(_remote(ins[w].at[:, pl.ds((1 - c) * hr, hr)], outs[w], sems[0].at[w], sems[1].at[w],
                               (x, y, 1 - c)))
        return out

    def start(ins, outs, sems):
        for cp in copies(ins, outs, sems):
            cp.start()

    def finish(ins, outs, sems):
        for cp in copies(ins, outs, sems):
            cp.wait()

    return _Exchange(grads, [_sds((N_CHIPS, g.shape[1] // 2, g.shape[2]), F32) for g in grads],
                     [pltpu.SemaphoreType.DMA((nw,))] * 2, start, finish)


def _ex_chip(pieces):
    nw = len(pieces)

    def copies(ins, outs, sems):
        x, y, c, chips = _place()
        return [_remote(ins[w].at[2 * cx + cy], outs[w].at[k], sems[0].at[w, k], sems[1].at[w, k], (cx, cy, c))
                for w in range(nw) for k, (cx, cy) in enumerate(chips)]

    def start(ins, outs, sems):
        for cp in copies(ins, outs, sems):
            cp.start()

    def finish(ins, outs, sems):
        for cp in copies(ins, outs, sems):
            cp.wait()

    return _Exchange(pieces, [_sds((3,) + p.shape[1:], BF16) for p in pieces],
                     [pltpu.SemaphoreType.DMA((nw, 3))] * 2, start, finish)


def _ex_swap(fulls):
    nw = len(fulls)

    def start(ins, outs, sems):
        x, y, c, _ = _place()
        for w in range(nw):
            hr = fulls[w].shape[0] // 2
            mine = pl.ds(c * hr, hr)
            _remote(ins[w].at[mine], outs[w].at[mine], sems[0].at[w], sems[1].at[w], (x, y, 1 - c)).start()

    def finish(ins, outs, sems):
        x, y, c, _ = _place()
        for w in range(nw):
            hr = fulls[w].shape[0] // 2
            mine, theirs = pl.ds(c * hr, hr), pl.ds((1 - c) * hr, hr)
            _remote(ins[w].at[mine], outs[w].at[mine], sems[0].at[w], sems[1].at[w], (x, y, 1 - c)).wait_send()
            _remote(ins[w].at[theirs], outs[w].at[theirs], sems[0].at[w], sems[1].at[w], (x, y, 1 - c)).wait_recv()

    return _Exchange(fulls, [_sds(f.shape, F32) for f in fulls], [pltpu.SemaphoreType.DMA((nw,))] * 2,
                     start, finish, aliases={w: w for w in range(nw)})


def _ex_allgather(blocks):
    nb = len(blocks)

    def copies(ins, outs, sems):
        send, recv, lsem = sems
        x, y, c, chips = _place()
        me, sibling = (x, y, c), (x, y, 1 - c)

        def rows(b, px, py, pc):
            m_per = blocks[b].shape[0]
            return outs[b].at[pl.ds((4 * px + 2 * py + pc) * m_per, m_per), :]

        def copy(b, k, blk, to, src=None):
            return _remote(rows(b, *blk) if src is None else src, rows(b, *blk), send.at[b, k], recv.at[b, k], to)

        def mine(b):
            return pltpu.make_async_copy(ins[b], rows(b, *me), lsem.at[b])

        def first(b, k):
            return copy(b, k, me, sibling if k == 0 else (*chips[k - 1], c), src=ins[b])

        def passed(b, j):
            return copy(b, 4 + j, (*chips[j], c), sibling)

        def landed(b, j):
            return copy(b, 1 + j, (*chips[j], c), me)

        def handed(b, k):
            return copy(b, 0, sibling, me) if k == 0 else copy(b, 3 + k, (*chips[k - 1], 1 - c), me)

        return mine, first, passed, landed, handed

    def start(ins, outs, sems):
        mine, first, _, _, _ = copies(ins, outs, sems)
        for b in range(nb):
            mine(b).start()
            for k in range(4):
                first(b, k).start()

    def finish(ins, outs, sems):
        mine, first, passed, landed, handed = copies(ins, outs, sems)
        sent = []
        for b in range(nb):
            for j in range(3):
                landed(b, j).wait_recv()
                cp = passed(b, j)
                cp.start()
                sent.append(cp)
        for b in range(nb):
            for k in range(4):
                handed(b, k).wait_recv()
            for k in range(4):
                first(b, k).wait_send()
        for cp in sent:
            cp.wait_send()
        for b in range(nb):
            mine(b).wait()

    return _Exchange(blocks, [_sds((N_DEV * b.shape[0], b.shape[1]), F32) for b in blocks],
                     [pltpu.SemaphoreType.DMA((nb, 7)), pltpu.SemaphoreType.DMA((nb, 7)), pltpu.SemaphoreType.DMA((nb,))],
                     start, finish)


def _inproj(x2, g1, w_in_t, b_in, tabs, seq, exchanges=()):
    T = x2.shape[0]
    tm = min(TM, seq)
    nseq = seq // tm

    def body(x_ref, g_ref, w_ref, b_ref, c_ref, a_ref, bt_ref, h_ref, u_ref, q_ref, k_ref, v_ref, gate_ref):
        x = x_ref[...]
        h = (x * _rms(x) * g_ref[...]).astype(BF16)
        h_ref[...] = h

        def proj(lo, hi):
            return _dot_nt(h, w_ref[lo:hi, :]) + b_ref[:, lo:hi]

        c, a, bt = c_ref[...], a_ref[...], bt_ref[...]
        u_ref[...] = proj(0, C_Q)
        q = proj(C_Q, C_K)
        for p in range(4):
            sl = slice(LANES * p, LANES * (p + 1))
            q_ref[:, sl] = (_rot_fwd(q[:, sl], c, a, bt) * SCALE).astype(BF16)
        kv = proj(C_K, C_G)
        k_ref[...] = _rot_fwd(kv[:, :KV_WIDTH], c, a, bt).astype(BF16)
        v_ref[...] = kv[:, KV_WIDTH:].astype(BF16)
        for j in range(2):
            lo = C_G + D_MODEL * j
            gate_ref[:, D_MODEL * j:D_MODEL * (j + 1)] = jax.nn.sigmoid(proj(lo, lo + D_MODEL)).astype(BF16)

    tab = pl.BlockSpec((tm, LANES), lambda i: (i % nseq, 0))
    return _call(
        body, name="inproj", grid=(T // tm,),
        in_specs=[_rows(tm, D_MODEL), _const((1, D_MODEL)), _const((IN_WIDTH, D_MODEL)), _const((1, IN_WIDTH)),
                  tab, tab, tab],
        out_specs=[_rows(tm, D_MODEL), _rows(tm, POOL_WIDTH), _rows(tm, ATTN_WIDTH), _rows(tm, KV_WIDTH),
                   _rows(tm, KV_WIDTH), _rows(tm, GATE_WIDTH)],
        out_shape=[_sds((T, D_MODEL), BF16), _sds((T, POOL_WIDTH), F32), _sds((T, ATTN_WIDTH), BF16),
                   _sds((T, KV_WIDTH), BF16), _sds((T, KV_WIDTH), BF16), _sds((T, GATE_WIDTH), BF16)],
        args=(x2, g1, w_in_t, b_in, *tabs), sem=("parallel",), exchanges=exchanges)


def _inv_count(pos, w):
    return 1.0 / jnp.minimum(pos + 1, w).astype(F32)


def _pool_tile(i, tp, nseq, u_ref, prev_ref, w_ref, s_ref, diff_ref, y_ref):
    first = (i % nseq) == 0
    prev = jnp.where(first, 0.0, prev_ref[...])
    ext = jnp.concatenate([prev, u_ref[...]], axis=0)
    pos = (i % nseq) * tp + lax.broadcasted_iota(jnp.int32, (tp, 1), 0)
    for gi, w in enumerate(POOL_WINDOWS):
        sl = slice(POOL_GC * gi, POOL_GC * (gi + 1))
        xg = ext[:, sl]
        s = xg
        sh = 1
        while sh < w:
            s = s + pltpu.roll(s, sh, 0)
            sh *= 2
        pooled = s[HALO:] * _inv_count(pos, w)
        diff = (pooled - xg[HALO:]).astype(BF16)
        diff_ref[:, sl] = diff
        mixed = _dot(diff, w_ref[gi].astype(BF16))
        y_ref[:, sl] = (mixed * s_ref[:, sl]).astype(BF16)


def _pool_specs(tp):
    per = tp // HALO
    return [_rows(tp, POOL_WIDTH), pl.BlockSpec((HALO, POOL_WIDTH), lambda i: (jnp.maximum(i * per - 1, 0), 0)),
            _const((4, POOL_GC, POOL_GC)), _const((1, POOL_WIDTH))]


GROUP = 4
GROWS = GROUP * BLOCK


def _attn_masks(n):
    qi = lax.broadcasted_iota(jnp.int32, (GROWS, 2 * BLOCK), 0) % BLOCK
    kj = lax.broadcasted_iota(jnp.int32, (GROWS, 2 * BLOCK), 1)
    rel = qi + BLOCK - kj
    valid = (rel >= 0) & (rel < BLOCK) & (kj >= jnp.where(n > 0, 0, BLOCK))
    lo = lax.broadcasted_iota(jnp.int32, (BLOCK, LANES), 1) < HEAD_DIM
    return valid, lo


def _by_example(bl, *arrays):
    return [a.reshape(bl, a.shape[0] // bl, a.shape[1]) for a in arrays]


def _stack_heads(ref, h, lo):
    keep = lo if h == 0 else jnp.logical_not(lo)
    pieces = []
    for p in (2 * h, 2 * h + 1):
        xp = ref[:, LANES * p:LANES * (p + 1)].astype(F32)
        for e in range(2):
            t = xp if e == h else pltpu.roll(xp, HEAD_DIM, 1)
            pieces.append(jnp.where(keep, t, 0.0).astype(BF16))
    return jnp.concatenate(pieces, axis=0)


def _unstack_heads(stacked, h, lo):
    pairs = []
    for j in range(2):
        parts = []
        for e in range(2):
            t = stacked[BLOCK * (2 * j + e):BLOCK * (2 * j + e + 1)]
            parts.append(t if e == h else pltpu.roll(t, HEAD_DIM, 1))
        pairs.append(jnp.where(lo, parts[0], parts[1]))
    return pairs


def _sink_rows(sink_ref, h):
    head = lax.broadcasted_iota(jnp.int32, (GROWS, 1), 0) // BLOCK
    col = jnp.zeros((GROWS, 1), F32) + sink_ref[GROUP * h]
    for g in range(1, GROUP):
        col = jnp.where(head == g, sink_ref[GROUP * h + g], col)
    return col


def _group_probs(qs, kk, valid, sink):
    s = jnp.where(valid, _dot_nt(qs, kk), NEG_INF)
    m = jnp.maximum(jnp.max(s, axis=1, keepdims=True), sink)
    ex = jnp.exp(s - m)
    es = jnp.exp(sink - m)
    inv = 1.0 / (jnp.sum(ex, axis=1, keepdims=True) + es)
    return ex * inv, es * inv


def _mixers_fwd(q, k, v, sinks, u, w_pool, pool_scale, seq, exchanges=()):
    T = q.shape[0]
    nb = seq // BLOCK
    bl = T // seq
    tp = T // nb
    nseq = seq // tp

    def body(sink_ref, q_ref, kp_ref, kc_ref, vp_ref, vc_ref, u_ref, prev_ref, w_ref, s_ref, o_ref, diff_ref, y_ref):
        n = pl.program_id(0)
        valid, lo = _attn_masks(n)
        for b in range(bl):
            kk = jnp.concatenate([kp_ref[b], kc_ref[b]], axis=0)
            vv = jnp.concatenate([vp_ref[b], vc_ref[b]], axis=0)
            for h in range(2):
                qs = _stack_heads(q_ref.at[b], h, lo)
                pr, _ = _group_probs(qs, kk, valid, _sink_rows(sink_ref, h))
                o = _dot(pr.astype(BF16), vv)
                for j, pair in enumerate(_unstack_heads(o, h, lo)):
                    p = 2 * h + j
                    o_ref[b, :, LANES * p:LANES * (p + 1)] = pair.astype(BF16)
        _pool_tile(n, tp, nseq, u_ref, prev_ref, w_ref, s_ref, diff_ref, y_ref)

    cur = lambda n: (0, n, 0)
    prv = lambda n: (0, jnp.maximum(n - 1, 0), 0)
    kv = lambda m: pl.BlockSpec((bl, BLOCK, KV_WIDTH), m)
    res = _call(
        body, name="mixers_fwd", grid=(nb,),
        in_specs=[pl.BlockSpec(memory_space=pltpu.SMEM), pl.BlockSpec((bl, BLOCK, ATTN_WIDTH), cur),
                  kv(prv), kv(cur), kv(prv), kv(cur)] + _pool_specs(tp),
        out_specs=[pl.BlockSpec((bl, BLOCK, ATTN_WIDTH), cur), _rows(tp, POOL_WIDTH), _rows(tp, POOL_WIDTH)],
        out_shape=[_sds((bl, seq, ATTN_WIDTH), BF16), _sds((T, POOL_WIDTH), BF16), _sds((T, POOL_WIDTH), BF16)],
        args=(sinks, *_by_example(bl, q, k, k, v, v), u, u, w_pool, pool_scale), sem=("parallel",),
        exchanges=exchanges)
    outs, rest = res if exchanges else (res, None)
    return [outs[0].reshape(T, ATTN_WIDTH), outs[1], outs[2]], rest


def _branch(y, w_ref):
    return jnp.concatenate([_dot(y, w_ref[j]) for j in range(N_CHIPS)], axis=1)


def _merge_out(y_pool, y_attn, gate, x2, w_bp, w_ba, w_out, g2, g3, exchanges=()):
    T = x2.shape[0]
    tm = min(TM, T)

    def body(yp_ref, ya_ref, gate_ref, x_ref, wbp_ref, wba_ref, wo_ref, g2_ref, g3_ref,
             mg_ref, mix_ref, x1_ref, h2_ref):
        bp, ba = _branch(yp_ref[...], wbp_ref), _branch(ya_ref[...], wba_ref)
        merged = (gate_ref[:, :D_MODEL].astype(F32) * bp + gate_ref[:, D_MODEL:].astype(F32) * ba).astype(BF16)
        mg_ref[...] = merged
        mix = _dot(merged, wo_ref[...])
        mix_ref[...] = mix
        x1 = x_ref[...] + mix * _rms(mix) * g2_ref[...]
        x1_ref[...] = x1
        h2_ref[...] = (x1 * _rms(x1) * g3_ref[...]).astype(BF16)

    return _call(
        body, name="merge_out", grid=(T // tm,),
        in_specs=[_rows(tm, POOL_WIDTH), _rows(tm, ATTN_WIDTH), _rows(tm, GATE_WIDTH), _rows(tm, D_MODEL),
                  _const(w_bp.shape), _const(w_ba.shape), _const((D_MODEL, D_MODEL)),
                  _const((1, D_MODEL)), _const((1, D_MODEL))],
        out_specs=[_rows(tm, D_MODEL)] * 4,
        out_shape=[_sds((T, D_MODEL), BF16), _sds((T, D_MODEL), F32), _sds((T, D_MODEL), F32),
                   _sds((T, D_MODEL), BF16)],
        args=(y_pool, y_attn, gate, x2, w_bp, w_ba, w_out, g2, g3), sem=("parallel",), exchanges=exchanges)


HALF = D_MODEL // 2
TM_MLP = 256


def _mlp_core(h2, x1, mix, tgt, w_up, w_down, g4, g3, g2):
    T = h2.shape[0]
    tm = min(TM_MLP, T)

    def body(h_ref, x1_ref, mix_ref, t_ref, g_ref, g3_ref, g2_ref, ua_hbm, ub_hbm, da_hbm, db_hbm,
             act_ref, dff_ref, dup_ref, dx1_ref, dmix_ref, loss_ref, dg_ref, dg3_ref, dg2_ref,
             wu, wd, relu_scr, sems):
        def weight_copy(i):
            src, dst = ((ua_hbm, wu.at[:, :HALF]), (ub_hbm, wu.at[:, HALF:]),
                        (da_hbm, wd.at[:, :HALF]), (db_hbm, wd.at[:, HALF:]))[i]
            return pltpu.make_async_copy(src, dst, sems.at[i])

        @pl.when(pl.program_id(0) == 0)
        def _():
            for i in range(4):
                weight_copy(i).start()
            loss_ref[...] = jnp.zeros_like(loss_ref)
            for ref in (dg_ref, dg3_ref, dg2_ref):
                ref[...] = jnp.zeros_like(ref)
            weight_copy(0).wait()
            weight_copy(1).wait()

        h = h_ref[...]
        ff = None
        for j in range(N_CHIPS):
            lo = D_MODEL * j
            relu = jnp.maximum(_dot(h, wu[j]), 0.0)
            if j == 0:
                @pl.when(pl.program_id(0) == 0)
                def _():
                    weight_copy(2).wait()
                    weight_copy(3).wait()
            relu_scr[:, lo:lo + D_MODEL] = relu
            act = jnp.square(relu).astype(BF16)
            act_ref[:, lo:lo + D_MODEL] = act
            t = _dot(act, wd[j])
            ff = t if ff is None else ff + t
        g = g_ref[...]
        x1 = x1_ref[...]
        err = x1 + ff * _rms(ff) * g - t_ref[...]
        loss_ref[...] += jnp.sum(err * err) * (0.5 / D_MODEL)
        dy = err * (1.0 / D_MODEL)
        dff, dg = _norm_bwd(ff, g, dy)
        dg_ref[...] += dg
        dff = dff.astype(BF16)
        dff_ref[...] = dff
        dh2 = None
        for j in range(N_CHIPS):
            lo = D_MODEL * j
            dup = (_dot_nt(dff, wd[j]) * (2.0 * relu_scr[:, lo:lo + D_MODEL])).astype(BF16)
            dup_ref[:, lo:lo + D_MODEL] = dup
            t = _dot_nt(dup, wu[j])
            dh2 = t if dh2 is None else dh2 + t
        dx, dg3 = _norm_bwd(x1, g3_ref[...], dh2)
        dx1 = dy + dx
        dx1_ref[...] = dx1
        dg3_ref[...] += dg3
        dmix, dg2 = _norm_bwd(mix_ref[...], g2_ref[...], dx1)
        dmix_ref[...] = dmix.astype(BF16)
        dg2_ref[...] += dg2

    slabs = pltpu.VMEM((N_CHIPS, D_MODEL, D_MODEL), BF16)
    gain = _const((1, D_MODEL))
    return pl.pallas_call(
        body, name="mlp_core", grid=(T // tm,),
        in_specs=[_rows(tm, D_MODEL)] * 4 + [gain] * 3 + [ANY] * 4,
        out_specs=[_rows(tm, D_FF), _rows(tm, D_MODEL), _rows(tm, D_FF), _rows(tm, D_MODEL), _rows(tm, D_MODEL),
                   _const((8, LANES)), gain, gain, gain],
        out_shape=[_sds((T, D_FF), BF16), _sds((T, D_MODEL), BF16), _sds((T, D_FF), BF16), _sds((T, D_MODEL), F32),
                   _sds((T, D_MODEL), BF16), _sds((8, LANES), F32)] + [_sds((1, D_MODEL), F32)] * 3,
        scratch_shapes=[slabs] * 2 + [pltpu.VMEM((tm, D_FF), F32), pltpu.SemaphoreType.DMA((4,))],
        compiler_params=_cp("arbitrary"),
    )(h2, x1, mix, tgt, g4, g3, g2, *w_up, *w_down)


def _dw(tag, a, g, ta, tn, shard_cols=False, exchanges=()):
    T, ka = a.shape
    n = g.shape[1]
    tk = min(2 * TM, T)
    nk = T // tk

    def body(a_ref, g_ref, o_ref):
        @pl.when(pl.program_id(2) == 0)
        def _():
            o_ref[...] = jnp.zeros_like(o_ref)

        o_ref[...] += _dot_tn(a_ref[...], g_ref[...])

    if shard_cols:
        per = (n // N_CHIPS) // tn
        out_spec = pl.BlockSpec((None, ta, tn), lambda i, j, k: (j // per, i, j % per))
        out_shape = _sds((N_CHIPS, ka, n // N_CHIPS), F32)
    else:
        out_spec = pl.BlockSpec((ta, tn), lambda i, j, k: (i, j))
        out_shape = _sds((ka, n), F32)
    return _call(
        body, name="dw_" + tag, grid=(ka // ta, n // tn, nk),
        in_specs=[pl.BlockSpec((tk, ta), lambda i, j, k: (k, i)), pl.BlockSpec((tk, tn), lambda i, j, k: (k, j))],
        out_specs=[out_spec], out_shape=[out_shape],
        args=(a, g), sem=("parallel", "parallel", "arbitrary"), exchanges=exchanges)


def _dw_mix(merged, dmix, y_pool, dbp, y_attn, dba, exchanges=()):
    T = merged.shape[0]
    tk = min(2 * TM, T)
    c = D_MODEL // N_CHIPS

    def body(mg_ref, dmix_ref, yp_ref, dbp_ref, ya_ref, dba_ref, out_ref, bp_ref, ba_ref):
        @pl.when(pl.program_id(0) == 0)
        def _():
            for ref in (out_ref, bp_ref, ba_ref):
                ref[...] = jnp.zeros_like(ref)

        out_ref[...] += _dot_tn(mg_ref[...], dmix_ref[...])
        for y_ref, d_ref, o_ref in ((yp_ref, dbp_ref, bp_ref), (ya_ref, dba_ref, ba_ref)):
            res = _dot_tn(y_ref[...], d_ref[...])
            for j in range(N_CHIPS):
                o_ref[j] += res[:, c * j:c * (j + 1)]

    slabs = (N_CHIPS, POOL_WIDTH, c)
    return _call(
        body, name="dw_mix", grid=(T // tk,),
        in_specs=[_rows(tk, D_MODEL), _rows(tk, D_MODEL), _rows(tk, POOL_WIDTH), _rows(tk, D_MODEL),
                  _rows(tk, ATTN_WIDTH), _rows(tk, D_MODEL)],
        out_specs=[_const((D_MODEL, D_MODEL)), _const(slabs), _const(slabs)],
        out_shape=[_sds((D_MODEL, D_MODEL), F32), _sds(slabs, F32), _sds(slabs, F32)],
        args=(merged, dmix, y_pool, dbp, y_attn, dba), sem=("arbitrary",), exchanges=exchanges)


def _merge_bwd(dmix, gate, y_pool, y_attn, w_out, w_bp, w_ba, exchanges=()):
    T = dmix.shape[0]
    tm = min(TM, T)

    def body(dmix_ref, gate_ref, yp_ref, ya_ref, wo_ref, wbp_ref, wba_ref,
             dbp_ref, dba_ref, dgate_ref, dyp_ref, dya_ref):
        dm = _dot_nt(dmix_ref[...], wo_ref[...])
        for j, (y_ref, db_ref, w_ref, dy_ref) in enumerate(
                ((yp_ref, dbp_ref, wbp_ref, dyp_ref), (ya_ref, dba_ref, wba_ref, dya_ref))):
            sl = slice(D_MODEL * j, D_MODEL * (j + 1))
            gt = gate_ref[:, sl].astype(F32)
            db = (dm * gt).astype(BF16)
            db_ref[...] = db
            dgate_ref[:, sl] = (dm * _branch(y_ref[...], w_ref) * gt * (1.0 - gt)).astype(BF16)
            cw = D_MODEL // N_CHIPS
            dy = _dot_nt(db[:, :cw], w_ref[0])
            for c in range(1, N_CHIPS):
                dy = dy + _dot_nt(db[:, cw * c:cw * (c + 1)], w_ref[c])
            dy_ref[...] = dy.astype(dy_ref.dtype)

    return _call(
        body, name="merge_bwd", grid=(T // tm,),
        in_specs=[_rows(tm, D_MODEL), _rows(tm, GATE_WIDTH), _rows(tm, POOL_WIDTH), _rows(tm, ATTN_WIDTH),
                  _const((D_MODEL, D_MODEL)), _const(w_bp.shape), _const(w_ba.shape)],
        out_specs=[_rows(tm, D_MODEL), _rows(tm, D_MODEL), _rows(tm, GATE_WIDTH), _rows(tm, POOL_WIDTH),
                   _rows(tm, ATTN_WIDTH)],
        out_shape=[_sds((T, D_MODEL), BF16), _sds((T, D_MODEL), BF16), _sds((T, GATE_WIDTH), BF16),
                   _sds((T, POOL_WIDTH), F32), _sds((T, ATTN_WIDTH), BF16)],
        args=(dmix, gate, y_pool, y_attn, w_out, w_bp, w_ba), sem=("parallel",), exchanges=exchanges)


def _mixers_bwd(q, k, v, do, sinks, tabs, dyp, diff, w_pool, pool_scale, seq, exchanges=()):
    T = q.shape[0]
    nb = seq // BLOCK
    bl = T // seq
    steps = nb + 1
    tp = T // nb
    nseq = seq // tp
    per = tp // HALO
    last_halo = T // HALO - 1

    def body(sink_ref, q_ref, do_ref, kp_ref, kc_ref, vp_ref, vc_ref, c_ref, a_ref, bt_ref, cp_ref, ap_ref, btp_ref,
             dy_ref, nxt_ref, diff_ref, w_ref, s_ref,
             dq_ref, dk_ref, dv_ref, dsink_ref, du_ref, dw_ref, ds_ref, ck_ref, cv_ref):
        n = pl.program_id(0)

        @pl.when(n == 0)
        def _():
            for ref in (dsink_ref, ck_ref, cv_ref, dw_ref, ds_ref):
                ref[...] = jnp.zeros_like(ref)

        @pl.when(n < nb)
        def _():
            _pool_bwd_tile(n, tp, nseq, dy_ref, nxt_ref, diff_ref, w_ref, s_ref, du_ref, dw_ref, ds_ref)
            valid, lo = _attn_masks(n)
            for b in range(bl):
                kk = jnp.concatenate([kp_ref[b], kc_ref[b]], axis=0)
                vv = jnp.concatenate([vp_ref[b], vc_ref[b]], axis=0)
                dk_acc = jnp.zeros((2 * BLOCK, KV_WIDTH), F32)
                dv_acc = jnp.zeros((2 * BLOCK, KV_WIDTH), F32)
                for h in range(2):
                    qs = _stack_heads(q_ref.at[b], h, lo)
                    dos = _stack_heads(do_ref.at[b], h, lo)
                    pr, ps = _group_probs(qs, kk, valid, _sink_rows(sink_ref, h))
                    dp = _dot_nt(dos, vv)
                    delta = jnp.sum(pr * dp, axis=1, keepdims=True)
                    ds = (pr * (dp - delta)).astype(BF16)
                    dsk = ps * delta
                    for g in range(GROUP):
                        idx = GROUP * h + g
                        dsink_ref[idx:idx + 1, :] += (jnp.zeros((1, LANES), F32)
                                                      - jnp.sum(dsk[BLOCK * g:BLOCK * (g + 1)]))
                    dk_acc = dk_acc + _dot_tn(ds, qs)
                    dv_acc = dv_acc + _dot_tn(pr.astype(BF16), dos)
                    for j, pair in enumerate(_unstack_heads(_dot(ds, kk) * SCALE, h, lo)):
                        sl = slice(LANES * (2 * h + j), LANES * (2 * h + j + 1))
                        dq_ref[b, :, sl] = _rot_bwd(pair, c_ref[...], a_ref[...], bt_ref[...]).astype(BF16)
                fin_k = ck_ref[b] + dk_acc[:BLOCK]
                dk_ref[b] = _rot_bwd(fin_k, cp_ref[...], ap_ref[...], btp_ref[...]).astype(BF16)
                dv_ref[b] = (cv_ref[b] + dv_acc[:BLOCK]).astype(BF16)
                ck_ref[b] = dk_acc[BLOCK:]
                cv_ref[b] = dv_acc[BLOCK:]

        @pl.when(n == nb)
        def _():
            for b in range(bl):
                dk_ref[b] = _rot_bwd(ck_ref[b], cp_ref[...], ap_ref[...], btp_ref[...]).astype(BF16)
                dv_ref[b] = cv_ref[b].astype(BF16)

    cur = lambda n: (0, jnp.minimum(n, nb - 1), 0)
    prv = lambda n: (0, jnp.clip(n - 1, 0, nb - 1), 0)
    tcur = lambda n: (jnp.minimum(n, nb - 1), 0)
    tprv = lambda n: (jnp.clip(n - 1, 0, nb - 1), 0)
    wide = lambda m: pl.BlockSpec((bl, BLOCK, ATTN_WIDTH), m)
    kv = lambda m: pl.BlockSpec((bl, BLOCK, KV_WIDTH), m)
    tab = lambda m: pl.BlockSpec((BLOCK, LANES), m)
    tile = lambda n: (jnp.minimum(n, nb - 1), 0)
    halo = lambda n: (jnp.minimum((jnp.minimum(n, nb - 1) + 1) * per, last_halo), 0)
    rows = pl.BlockSpec((tp, POOL_WIDTH), tile)
    res = _call(
        body, name="mixers_bwd", grid=(steps,),
        in_specs=[pl.BlockSpec(memory_space=pltpu.SMEM), wide(cur), wide(cur), kv(prv), kv(cur), kv(prv), kv(cur),
                  tab(tcur), tab(tcur), tab(tcur), tab(tprv), tab(tprv), tab(tprv),
                  rows, pl.BlockSpec((HALO, POOL_WIDTH), halo), rows, _const((4, POOL_GC, POOL_GC)),
                  _const((1, POOL_WIDTH))],
        out_specs=[wide(cur), kv(prv), kv(prv), _const((8, LANES)), rows, _const((4, POOL_GC, POOL_GC)),
                   _const((1, POOL_WIDTH))],
        out_shape=[_sds((bl, seq, ATTN_WIDTH), BF16), _sds((bl, seq, KV_WIDTH), BF16),
                   _sds((bl, seq, KV_WIDTH), BF16), _sds((8, LANES), F32), _sds((T, POOL_WIDTH), BF16),
                   _sds((4, POOL_GC, POOL_GC), F32), _sds((1, POOL_WIDTH), F32)],
        scratch=[pltpu.VMEM((bl, BLOCK, KV_WIDTH), F32), pltpu.VMEM((bl, BLOCK, KV_WIDTH), F32)],
        args=(sinks, *_by_example(bl, q, do, k, k, v, v), *tabs, *tabs, dyp, dyp, diff, w_pool, pool_scale),
        sem=("arbitrary",), exchanges=exchanges)
    outs, rest = (res if exchanges else (res, None))
    outs = [outs[0].reshape(T, ATTN_WIDTH), outs[1].reshape(T, KV_WIDTH), outs[2].reshape(T, KV_WIDTH), *outs[3:]]
    return (outs, rest) if exchanges else outs


def _pool_bwd_tile(i, tp, nseq, dy_ref, nxt_ref, diff_ref, w_ref, s_ref, du_ref, dw_ref, ds_ref):
    last = (i % nseq) == nseq - 1
    nxt = jnp.where(last, 0.0, nxt_ref[...])
    ext = jnp.concatenate([dy_ref[...], nxt], axis=0) * s_ref[...]
    pos = (i % nseq) * tp + lax.broadcasted_iota(jnp.int32, (tp + HALO, 1), 0)
    for gi, w in enumerate(POOL_WINDOWS):
        sl = slice(POOL_GC * gi, POOL_GC * (gi + 1))
        wg = w_ref[gi].astype(BF16)
        dmx = ext[:, sl].astype(BF16)
        ddiff = _dot_nt(dmx, wg)
        s = ddiff * _inv_count(pos, w)
        sh = 1
        while sh < w:
            s = s + pltpu.roll(s, tp + HALO - sh, 0)
            sh *= 2
        du_ref[:, sl] = (s[:tp] - ddiff[:tp]).astype(BF16)
        dg = diff_ref[:, sl]
        dw_ref[gi] += _dot_tn(dg, dmx[:tp])
        ds_ref[:, sl] += jnp.sum(dy_ref[:, sl] * _dot(dg, wg), axis=0, keepdims=True)


_PARTS = ((0, C_Q), (C_Q, C_K), (C_K, C_V), (C_V, C_G), (C_G, IN_WIDTH))


def _inproj_bwd(parts, x2, dx1, w_in_t, g1, exchanges=()):
    T = x2.shape[0]
    tm = min(TM, T)

    def body(du_ref, dq_ref, dk_ref, dv_ref, dgt_ref, x_ref, dx1_ref, w_ref, g_ref, gx_ref, dg_ref):
        @pl.when(pl.program_id(0) == 0)
        def _():
            dg_ref[...] = jnp.zeros_like(dg_ref)

        dh = jnp.zeros((tm, D_MODEL), F32)
        for (lo, hi), p_ref in zip(_PARTS, (du_ref, dq_ref, dk_ref, dv_ref, dgt_ref)):
            dh = dh + _dot(p_ref[...], w_ref[lo:hi, :])
        dx, dg = _norm_bwd(x_ref[...], g_ref[...], dh)
        gx_ref[...] = dx1_ref[...] + dx
        dg_ref[...] += dg

    return _call(
        body, name="inproj_bwd", grid=(T // tm,),
        in_specs=[_rows(tm, hi - lo) for lo, hi in _PARTS]
        + [_rows(tm, D_MODEL), _rows(tm, D_MODEL), _const((IN_WIDTH, D_MODEL)), _const((1, D_MODEL))],
        out_specs=[_rows(tm, D_MODEL), _const((1, D_MODEL))],
        out_shape=[_sds((T, D_MODEL), F32), _sds((1, D_MODEL), F32)],
        args=(*parts, x2, dx1, w_in_t, g1), sem=("arbitrary",), exchanges=exchanges)


def _dw_in(h, parts, exchanges=()):
    T = h.shape[0]
    tk = min(TM, T)

    def body(h_ref, du_ref, dq_ref, dk_ref, dv_ref, dgt_ref, o_ref, db_ref):
        @pl.when(pl.program_id(0) == 0)
        def _():
            o_ref[...] = jnp.zeros_like(o_ref)
            db_ref[...] = jnp.zeros_like(db_ref)

        hh = h_ref[...]
        for (lo, hi), p_ref in zip(_PARTS, (du_ref, dq_ref, dk_ref, dv_ref, dgt_ref)):
            part = p_ref[...]
            o_ref[lo:hi, :] += _dot_tn(part, hh)
            db_ref[:, lo:hi] += jnp.sum(part.astype(F32), axis=0, keepdims=True)

    return _call(
        body, name="dw_in", grid=(T // tk,),
        in_specs=[_rows(tk, D_MODEL)] + [_rows(tk, hi - lo) for lo, hi in _PARTS],
        out_specs=[_const((IN_WIDTH, D_MODEL)), _const((1, IN_WIDTH))],
        out_shape=[_sds((IN_WIDTH, D_MODEL), F32), _sds((1, IN_WIDTH), F32)],
        args=(h, *parts), sem=("arbitrary",), exchanges=exchanges)


def _row_tile(rows, cap=256, mult=16):
    best = None
    for t in range(mult, min(rows, cap) + 1, mult):
        if rows % t == 0:
            best = t
    if best is None:
        raise ValueError("no row tile for %d rows" % rows)
    return best


def _pair_sum(ids, full, got):
    _, r, c = full.shape
    hr = r // 2
    tr = _row_tile(hr)
    nblk = hr // tr

    def body(ids_ref, a_ref, b_ref, own_ref, sb_ref):
        s = a_ref[...] + b_ref[...]
        sb_ref[...] = s.astype(BF16)

        @pl.when(pl.program_id(1) == ids_ref[0])
        def _():
            own_ref[...] = s

    slab = pl.BlockSpec((None, tr, c), lambda i, j, ids_ref: (j, i, 0))
    return pl.pallas_call(
        body, name="pair_sum_%dx%d" % (r, c),
        grid_spec=pltpu.PrefetchScalarGridSpec(
            num_scalar_prefetch=1, grid=(nblk, N_CHIPS),
            in_specs=[pl.BlockSpec((None, tr, c), lambda i, j, ids_ref: (j, ids_ref[1] * nblk + i, 0)), slab],
            out_specs=[pl.BlockSpec((tr, c), lambda i, j, ids_ref: (i, 0)), slab]),
        out_shape=[_sds((hr, c), F32), _sds((N_CHIPS, hr, c), BF16)],
        compiler_params=_cp("parallel", "arbitrary"),
    )(ids, full, got)


def _pair_sum_small(ids, fulls, gots):
    n = len(fulls)
    dims = [(f.shape[1] // 2, f.shape[2]) for f in fulls]

    def body(ids_ref, *refs):
        ins, outs = refs[:2 * n], refs[2 * n:]
        for k in range(n):
            s = ins[2 * k][...] + ins[2 * k + 1][...]
            outs[2 * k + 1][...] = s.astype(BF16)

            @pl.when(pl.program_id(0) == ids_ref[0])
            def _(k=k, s=s):
                outs[2 * k][...] = s

    in_specs, out_specs, out_shape = [], [], []
    for hr, c in dims:
        slab = pl.BlockSpec((None, hr, c), lambda j, ids_ref: (j, 0, 0))
        in_specs += [pl.BlockSpec((None, hr, c), lambda j, ids_ref: (j, ids_ref[1], 0)), slab]
        out_specs += [pl.BlockSpec((hr, c), lambda j, ids_ref: (0, 0)), slab]
        out_shape += [_sds((hr, c), F32), _sds((N_CHIPS, hr, c), BF16)]
    res = pl.pallas_call(
        body, name="pair_sum_small",
        grid_spec=pltpu.PrefetchScalarGridSpec(num_scalar_prefetch=1, grid=(N_CHIPS,), in_specs=in_specs,
                                               out_specs=out_specs),
        out_shape=out_shape, compiler_params=_cp("arbitrary"),
    )(ids, *[a for pair in zip(fulls, gots) for a in pair])
    return [(res[2 * k], res[2 * k + 1]) for k in range(n)]


def _chip_sum_small(ids, owns, gots):
    n = len(owns)

    def body(ids_ref, *refs):
        ins, outs = refs[:2 * n], refs[2 * n:]
        for k in range(n):
            a, b = ins[2 * k], ins[2 * k + 1]
            outs[k][...] = ((a[...] + b[0].astype(F32)) + b[1].astype(F32)) + b[2].astype(F32)

    in_specs, out_specs, out_shape = [], [], []
    for own in owns:
        hr, c = own.shape
        in_specs += [pl.BlockSpec((hr, c), lambda i, ids_ref: (0, 0)),
                     pl.BlockSpec((3, hr, c), lambda i, ids_ref: (0, 0, 0))]
        out_specs.append(pl.BlockSpec((hr, c), lambda i, ids_ref: (ids_ref[1], 0)))
        out_shape.append(_sds((2 * hr, c), F32))
    return pl.pallas_call(
        body, name="chip_sum_small",
        grid_spec=pltpu.PrefetchScalarGridSpec(num_scalar_prefetch=1, grid=(1,), in_specs=in_specs,
                                               out_specs=out_specs),
        out_shape=out_shape, compiler_params=_cp("arbitrary"),
    )(ids, *[a for pair in zip(owns, gots) for a in pair])


def _chip_sum(ids, own, got):
    hr, c = own.shape
    tr = _row_tile(hr)
    nblk = hr // tr

    def body(ids_ref, a_ref, b_ref, o_ref):
        o_ref[...] = ((a_ref[...] + b_ref[0].astype(F32)) + b_ref[1].astype(F32)) + b_ref[2].astype(F32)

    return pl.pallas_call(
        body, name="chip_sum_%dx%d" % (hr, c),
        grid_spec=pltpu.PrefetchScalarGridSpec(
            num_scalar_prefetch=1, grid=(nblk,),
            in_specs=[pl.BlockSpec((tr, c), lambda i, ids_ref: (i, 0)),
                      pl.BlockSpec((3, tr, c), lambda i, ids_ref: (0, i, 0))],
            out_specs=pl.BlockSpec((tr, c), lambda i, ids_ref: (ids_ref[1] * nblk + i, 0))),
        out_shape=_sds((2 * hr, c), F32),
        compiler_params=_cp("parallel"),
    )(ids, own, got)


def _adamw_math(w, g, m, v):
    nm = ADAM_B1 * m + (1.0 - ADAM_B1) * g
    nv = ADAM_B2 * v + (1.0 - ADAM_B2) * (g * g)
    m_hat = nm / (1.0 - ADAM_B1 ** ADAM_STEP)
    v_hat = nv / (1.0 - ADAM_B2 ** ADAM_STEP)
    return -ADAM_LR * (m_hat / (jnp.sqrt(v_hat) + ADAM_EPS) + ADAM_WD * w), nm, nv


def _adamw(w, g, m, v):
    r, c = w.shape
    tr = _row_tile(r, cap=512, mult=8)

    def body(w_ref, g_ref, m_ref, v_ref, d_ref, nm_ref, nv_ref):
        d_ref[...], nm_ref[...], nv_ref[...] = _adamw_math(w_ref[...], g_ref[...], m_ref[...], v_ref[...])

    spec = _rows(tr, c)
    return pl.pallas_call(
        body, name="adamw_%dx%d" % (r, c), grid=(r // tr,),
        in_specs=[spec] * 4, out_specs=[spec] * 3, out_shape=[_sds((r, c), F32)] * 3,
        compiler_params=_cp("parallel"),
    )(w, g, m, v)


SC_TILES = 32
SC_LANES = 16
SC_ROWS = 8


def _adamw_sparse(w, g, m, v):
    r, c = w.shape
    rows = r // SC_TILES
    step = min(rows, SC_ROWS)

    def body(w_hbm, g_hbm, m_hbm, v_hbm, d_hbm, nm_hbm, nv_hbm, wb, gb, mb, vb):
        tile = lax.axis_index("sc_subcore") * 2 + lax.axis_index("sc_core")

        @pl.loop(0, rows, step=step)
        def _(r0):
            mine = pl.ds(tile * rows + r0, step)
            for src, dst in ((w_hbm, wb), (g_hbm, gb), (m_hbm, mb), (v_hbm, vb)):
                pltpu.sync_copy(src.at[mine], dst)

            @pl.loop(0, step)
            def _(row):
                @pl.loop(0, c, step=SC_LANES)
                def _(i):
                    at = (row, pl.ds(i, SC_LANES))
                    wb[at], mb[at], vb[at] = _adamw_math(wb[at], gb[at], mb[at], vb[at])

            for src, dst in ((wb, d_hbm), (mb, nm_hbm), (vb, nv_hbm)):
                pltpu.sync_copy(src, dst.at[mine])

    return pl.kernel(
        body, name="adamw_sparse_%dx%d" % (r, c), out_type=[_sds((r, c), F32)] * 3,
        mesh=plsc.VectorSubcoreMesh(core_axis_name="sc_core", subcore_axis_name="sc_subcore"),
        scratch_types=[pltpu.VMEM((step, c), F32)] * 4,
    )(w, g, m, v)


_SMALL_NAMES = ("w_pool", "b_in", "g_mix_pre", "g_mix_post", "g_mlp_pre", "g_mlp_post", "pool_scale", "attn_sinks")
B_ROWS = -(-IN_WIDTH // D_MODEL)


def _row_block(rows):
    rows = [jnp.pad(r.astype(F32), ((0, 0), (0, D_MODEL - r.shape[1]))) for r in rows]
    return jnp.pad(jnp.concatenate(rows, axis=0), ((0, 8 - len(rows)), (0, 0)))


def _early_block(dg2, dg3, dg4, dps, dsink, loss):
    tail = jnp.concatenate([jnp.pad(dsink.reshape(1, -1), ((0, 0), (0, LANES - dsink.size))),
                            jnp.pad(loss.reshape(1, 1), ((0, 0), (0, LANES - 1)))], axis=1)
    return _row_block([dg2, dg3, dg4, dps, tail])


def _late_block(db_in, dg1):
    b = jnp.pad(db_in, ((0, 0), (0, B_ROWS * D_MODEL - IN_WIDTH))).reshape(B_ROWS, D_MODEL)
    return _row_block([b[r:r + 1] for r in range(B_ROWS)] + [dg1])


def _small_update(gearly, gmat, glate, w, m, v):
    names = _SMALL_NAMES
    n = len(names)

    def total(ref, rows):
        acc = ref[0:rows, :]
        for d in range(1, N_DEV):
            acc = acc + ref[d * rows:(d + 1) * rows, :]
        return acc

    def body(*refs):
        early_ref, gmat_ref, late_ref = refs[:3]
        w_refs, m_refs, v_refs = refs[3:3 + n], refs[3 + n:3 + 2 * n], refs[3 + 2 * n:3 + 3 * n]
        outs = refs[3 + 3 * n:]
        loss_ref, g_refs, d_refs = outs[0], outs[1:1 + n], outs[1 + n:1 + 2 * n]
        nm_refs, nv_refs = outs[1 + 2 * n:1 + 3 * n], outs[1 + 3 * n:1 + 4 * n]
        early, late = total(early_ref, 8), total(late_ref, 8)
        loss_ref[...] = jnp.sum(early[4:5, LANES:2 * LANES], axis=1, keepdims=True)
        bias = jnp.concatenate([late[r:r + 1, :] for r in range(B_ROWS - 1)]
                               + [late[B_ROWS - 1:B_ROWS, :IN_WIDTH - (B_ROWS - 1) * D_MODEL]], axis=1)
        grad = dict(b_in=bias, g_mix_pre=late[B_ROWS:B_ROWS + 1, :], g_mix_post=early[0:1, :],
                    g_mlp_pre=early[1:2, :], g_mlp_post=early[2:3, :], pool_scale=early[3:4, :POOL_WIDTH],
                    attn_sinks=early[4:5, :N_Q_HEADS])
        for i, name in enumerate(names):
            g = total(gmat_ref, 4 * POOL_GC) if name == "w_pool" else grad[name]
            g_refs[i][...] = g
            d_refs[i][...], nm_refs[i][...], nv_refs[i][...] = _adamw_math(
                w_refs[i][...], g, m_refs[i][...], v_refs[i][...])

    shapes = [_sds(w[k].shape, F32) for k in names]
    res = pl.pallas_call(
        body, name="small_update", out_shape=[_sds((1, 1), F32)] + shapes * 4,
        compiler_params=pltpu.CompilerParams(vmem_limit_bytes=VMEM_MB * 1024 * 1024),
    )(gearly, gmat, glate, *[w[k] for k in names], *[m[k] for k in names], *[v[k] for k in names])
    loss = res[0]
    per = {k: tuple(res[1 + j * n + i] for j in range(4)) for i, k in enumerate(names)}
    return loss, per


_BIG = ("w_in", "w_branch_pool", "w_branch_attn", "w_out", "w_up", "w_down")
_ORDER = ("g_mix_pre", "w_in", "b_in", "w_pool", "pool_scale", "attn_sinks", "w_branch_pool", "w_branch_attn",
          "w_out", "g_mix_post", "g_mlp_pre", "w_up", "w_down", "g_mlp_post")


def _stack_rows(slab):
    return slab.reshape(-1, slab.shape[2])


def _step(x2, tgt, seq, shards, small, ids):
    tabs = _rope_tables(seq)
    g1, g2, g3, g4 = (small[n] for n in ("g_mix_pre", "g_mix_post", "g_mlp_pre", "g_mlp_post"))
    sinks = small["attn_sinks"].reshape(N_Q_HEADS)
    w_pool = small["w_pool"].reshape(4, POOL_GC, POOL_GC)
    pool_scale = small["pool_scale"]

    def whole(shard, slabs):
        return lax.dynamic_update_slice(slabs, shard[None], (ids[0], 0, 0))

    up_a, up_b = shards["w_up"][:HALF], shards["w_up"][HALF:]
    down_a, down_b = shards["w_down"][:HALF], shards["w_down"][HALF:]
    w_in = _stack_rows(whole(shards["w_in"], _alone("gather_in", _ex_gather([shards["w_in"]]))[0][0]))
    mix_shards = [shards[n] for n in ("w_branch_pool", "w_branch_attn", "w_out")]
    (h, u, q, k, v, gate), [mix_slabs] = _inproj(
        x2, g1, w_in, small["b_in"], tabs, seq, exchanges=[_ex_gather(mix_shards)])
    w_bp, w_ba, out_slab = (whole(s, g) for s, g in zip(mix_shards, mix_slabs))
    w_out = _stack_rows(out_slab)
    (y_attn, diff, y_pool), [[got_a, got_b]] = _mixers_fwd(
        q, k, v, sinks, u, w_pool, pool_scale, seq, exchanges=[_ex_gather([up_a, up_b])])
    (merged, mix, x1, h2), [[got_c, got_d]] = _merge_out(
        y_pool, y_attn, gate, x2, w_bp, w_ba, w_out, g2, g3, exchanges=[_ex_gather([down_a, down_b])])
    w_up = (whole(up_a, got_a), whole(up_b, got_b))
    w_down = (whole(down_a, got_c), whole(down_b, got_d))
    act, dff, dup, dx1, dmix, loss_acc, dg4, dg3, dg2 = _mlp_core(h2, x1, mix, tgt, w_up, w_down, g4, g3, g2)

    dw_down = _dw("down", act, dff, 1024, 1024)[0].reshape(N_CHIPS, D_FF // N_CHIPS, D_MODEL)
    (dbp, dba, dgate, dyp, dya), [[got]] = _merge_bwd(
        dmix, gate, y_pool, y_attn, w_out, w_bp, w_ba, exchanges=[_ex_pair([dw_down])])
    ps_down = _pair_sum(ids, dw_down, got)
    (dw_up,), [[got]] = _dw("up", h2, dup, 1024, 1024, shard_cols=True, exchanges=[_ex_chip([ps_down[1]])])
    half_down = _chip_sum(ids, ps_down[0], got)
    (dw_out, dw_bp, dw_ba), [[got]] = _dw_mix(merged, dmix, y_pool, dbp, y_attn, dba, exchanges=[_ex_pair([dw_up])])
    ps_up = _pair_sum(ids, dw_up, got)
    dw_mix = [dw_out.reshape(N_CHIPS, D_MODEL // N_CHIPS, D_MODEL), dw_bp, dw_ba]
    (dq, dk, dv, dsink, du, dw_pool, dps), [[got], gots, [g_down]] = _mixers_bwd(
        q, k, v, dya, sinks, tabs, dyp, diff, w_pool, pool_scale, seq,
        exchanges=[_ex_chip([ps_up[1]]), _ex_pair(dw_mix), _ex_swap([half_down])])
    half_up = _chip_sum(ids, ps_up[0], got)
    ps_mix = _pair_sum_small(ids, dw_mix, gots)
    parts = (du, dq, dk, dv, dgate)
    early = _early_block(dg2, dg3, dg4, dps, dsink[:, 0], loss_acc[0, 0])
    mat = dw_pool.reshape(4 * POOL_GC, POOL_GC)
    (dw_in_t, db_in), [gots, [gearly, gmat], [g_up]] = _dw_in(
        h, parts, exchanges=[_ex_chip([p[1] for p in ps_mix]), _ex_allgather([early, mat]), _ex_swap([half_up])])
    half_mix = _chip_sum_small(ids, [p[0] for p in ps_mix], gots)
    dw_in = dw_in_t.reshape(N_CHIPS, IN_WIDTH // N_CHIPS, D_MODEL)
    g_mix, [got] = _alone("swap_mix_pair_in", _ex_swap(half_mix), _ex_pair([dw_in]))
    ps_in = _pair_sum(ids, dw_in, got)
    (gx, dg1), [[got]] = _inproj_bwd(parts, x2, dx1, w_in, g1, exchanges=[_ex_chip([ps_in[1]])])
    [g_in], [glate] = _alone("swap_in_allgather", _ex_swap([_chip_sum(ids, ps_in[0], got)]),
                             _ex_allgather([_late_block(db_in, dg1)]))

    grads = dict(w_in=g_in, w_branch_pool=g_mix[1], w_branch_attn=g_mix[2], w_out=g_mix[0], w_up=g_up, w_down=g_down)
    return (gearly, gmat, glate), gx, grads


def kernel(x, g_mix_pre, w_in, b_in, w_pool, pool_scale, attn_sinks, w_branch_pool, w_branch_attn, w_out, g_mix_post, g_mlp_pre, w_up, w_down, g_mlp_post, loss_target, m_g_mix_pre, m_w_in, m_b_in, m_w_pool, m_pool_scale, m_attn_sinks, m_w_branch_pool, m_w_branch_attn, m_w_out, m_g_mix_post, m_g_mlp_pre, m_w_up, m_w_down, m_g_mlp_post, v_g_mix_pre, v_w_in, v_b_in, v_w_pool, v_pool_scale, v_attn_sinks, v_w_branch_pool, v_w_branch_attn, v_w_out, v_g_mix_post, v_g_mlp_pre, v_w_up, v_w_down, v_g_mlp_post):
    weights = dict(g_mix_pre=g_mix_pre, w_in=w_in, b_in=b_in, w_pool=w_pool, pool_scale=pool_scale,
                   attn_sinks=attn_sinks, w_branch_pool=w_branch_pool, w_branch_attn=w_branch_attn, w_out=w_out,
                   g_mix_post=g_mix_post, g_mlp_pre=g_mlp_pre, w_up=w_up, w_down=w_down, g_mlp_post=g_mlp_post)
    mom1 = dict(g_mix_pre=m_g_mix_pre, w_in=m_w_in, b_in=m_b_in, w_pool=m_w_pool, pool_scale=m_pool_scale,
                attn_sinks=m_attn_sinks, w_branch_pool=m_w_branch_pool, w_branch_attn=m_w_branch_attn,
                w_out=m_w_out, g_mix_post=m_g_mix_post, g_mlp_pre=m_g_mlp_pre, w_up=m_w_up, w_down=m_w_down,
                g_mlp_post=m_g_mlp_post)
    mom2 = dict(g_mix_pre=v_g_mix_pre, w_in=v_w_in, b_in=v_b_in, w_pool=v_w_pool, pool_scale=v_pool_scale,
                attn_sinks=v_attn_sinks, w_branch_pool=v_w_branch_pool, w_branch_attn=v_w_branch_attn,
                w_out=v_w_out, g_mix_post=v_g_mix_post, g_mlp_pre=v_g_mlp_pre, w_up=v_w_up, w_down=v_w_down,
                g_mlp_post=v_g_mlp_post)
    b_loc, seq, _ = x.shape
    x2 = x.reshape(b_loc * seq, D_MODEL)
    tgt = loss_target.reshape(b_loc * seq, D_MODEL)
    ids = jnp.stack([2 * lax.axis_index("x") + lax.axis_index("y"), lax.axis_index("c")]).astype(jnp.int32)

    def flat(n, a):
        return a[0].T if n == "w_in" else a[0]

    def unflat(n, a):
        return (a.T if n == "w_in" else a)[None]

    shards = {n: flat(n, weights[n]).astype(BF16) for n in _BIG}
    small = {n: weights[n] for n in _ORDER if n not in _BIG}
    (gearly, gmat, glate), gx, grads = _step(x2, tgt, seq, shards, small, ids)

    def two_d(src):
        return {n: src[n].reshape(4 * POOL_GC, POOL_GC) if n == "w_pool" else src[n] for n in _SMALL_NAMES}

    loss, per = _small_update(gearly, gmat, glate, two_d(weights), two_d(mom1), two_d(mom2))
    delta, new_m, new_v = {}, {}, {}
    for n in _SMALL_NAMES:
        grads[n], delta[n], new_m[n], new_v[n] = (a.reshape(weights[n].shape) for a in per[n])
    for n in _BIG:
        update = _adamw if n == "w_in" else _adamw_sparse
        d, nm, nv = update(flat(n, weights[n]), grads[n], flat(n, mom1[n]), flat(n, mom2[n]))
        grads[n] = unflat(n, grads[n])
        delta[n], new_m[n], new_v[n] = unflat(n, d), unflat(n, nm), unflat(n, nv)

    return (loss[0, 0], gx.reshape(x.shape), *[grads[n] for n in _ORDER], *[delta[n] for n in _ORDER],
            *[new_m[n] for n in _ORDER], *[new_v[n] for n in _ORDER])
```

```python
import jax
import jax.numpy as jnp
from jax import lax
from jax.experimental import pallas as pl
from jax.experimental.pallas import tpu as pltpu
from jax.experimental.pallas import tpu_sc as plsc

F32 = jnp.float32
BF16 = jnp.bfloat16

D_MODEL = 1024
POOL_WINDOWS = (2, 4, 8, 16)
POOL_WIDTH = 512
POOL_GC = 128
HALO = 16
HEAD_DIM = 64
N_Q_HEADS = 8
ATTN_WIDTH = 512
KV_WIDTH = 128
BLOCK = 128
NEG_INF = -1e30
ROPE_THETA = 500000.0
ROT_DIM = 16
GATE_WIDTH = 2048
IN_WIDTH = 3328
D_FF = 4096
EPS = 1e-6
SCALE = HEAD_DIM ** -0.5
C_Q, C_K, C_V, C_G = 512, 1024, 1152, 1280

ADAM_LR, ADAM_B1, ADAM_B2, ADAM_EPS, ADAM_WD, ADAM_STEP = 0.001, 0.9, 0.999, 1e-08, 0.01, 10

N_CHIPS = 4
N_DEV = 8
LANES = 128
TM = 512
VMEM_MB = 56

MESH = pl.DeviceIdType.MESH
ANY = pl.BlockSpec(memory_space=pl.ANY)


def _cp(*sem, vmem=VMEM_MB):
    return pltpu.CompilerParams(dimension_semantics=sem, vmem_limit_bytes=vmem * 1024 * 1024)


def _rows(tile, cols):
    return pl.BlockSpec((tile, cols), lambda i: (i, 0))


def _const(shape):
    nd = len(shape)
    return pl.BlockSpec(shape, lambda i: (0,) * nd)


def _sds(shape, dtype):
    return jax.ShapeDtypeStruct(shape, dtype)


def _dot(a, b):
    return jnp.dot(a, b, preferred_element_type=F32)


def _dot_nt(a, b):
    return lax.dot_general(a, b, (((1,), (1,)), ((), ())), preferred_element_type=F32)


def _dot_tn(a, b):
    return lax.dot_general(a, b, (((0,), (0,)), ((), ())), preferred_element_type=F32)


def _rms(x):
    return lax.rsqrt(jnp.mean(x * x, axis=-1, keepdims=True) + EPS)


def _norm_bwd(x, g, dout):
    r = _rms(x)
    n = x * r
    dn = dout * g
    dx = r * (dn - n * jnp.mean(dn * n, axis=-1, keepdims=True))
    return dx, jnp.sum(dout * n, axis=0, keepdims=True)


def _rot_fwd(t, c, a, bt):
    return t * c + pltpu.roll(t, LANES - 8, 1) * a + pltpu.roll(t, 8, 1) * bt


def _rot_bwd(d, c, a, bt):
    return d * c + pltpu.roll(d * a, 8, 1) + pltpu.roll(d * bt, LANES - 8, 1)


def _rope_tables(seq):
    pos = jnp.arange(seq, dtype=F32)
    inv_freq = ROPE_THETA ** (-jnp.arange(0, ROT_DIM, 2, dtype=F32) / ROT_DIM)
    ang = pos[:, None] * inv_freq[None, :]
    cos, sin = jnp.cos(ang), jnp.sin(ang)
    ones = jnp.ones((seq, HEAD_DIM - ROT_DIM), F32)
    zeros8 = jnp.zeros((seq, 8), F32)
    zrest = jnp.zeros((seq, HEAD_DIM - ROT_DIM), F32)
    c = jnp.concatenate([cos, cos, ones], axis=1)
    a = jnp.concatenate([-sin, zeros8, zrest], axis=1)
    bt = jnp.concatenate([zeros8, sin, zrest], axis=1)
    return tuple(jnp.tile(t, (1, 2)) for t in (c, a, bt))


class _Exchange:
    def __init__(self, inputs, out_shapes, sems, start, finish, aliases=None, middle=None):
        self.inputs, self.out_shapes, self.sems = list(inputs), list(out_shapes), list(sems)
        self.start, self.finish, self.aliases = start, finish, dict(aliases or {})
        self.middle = middle


def _call(body, *, name, grid, in_specs, out_specs, out_shape, args, scratch=(), sem=(), exchanges=()):
    in_specs, out_specs, out_shape, scratch = list(in_specs), list(out_specs), list(out_shape), list(scratch)
    if not exchanges:
        return pl.pallas_call(body, name=name, grid=grid, in_specs=in_specs, out_specs=out_specs,
                              out_shape=out_shape, scratch_shapes=scratch, compiler_params=_cp(*sem))(*args)
    n_in, n_out, n_scr = len(in_specs), len(out_specs), len(scratch)
    x_in = [a for ex in exchanges for a in ex.inputs]
    x_out = [s for ex in exchanges for s in ex.out_shapes]
    x_sem = [s for ex in exchanges for s in ex.sems]
    aliases, i_off, o_off = {}, n_in, n_out
    for ex in exchanges:
        for i, o in ex.aliases.items():
            aliases[i_off + i] = o_off + o
        i_off += len(ex.inputs)
        o_off += len(ex.out_shapes)

    def split(flat):
        out, pos = [], 0
        for ex, n in zip(exchanges, flat[1]):
            out.append(flat[0][pos:pos + n])
            pos += n
        return out

    def carrier(*refs):
        pos = 0
        groups = []
        for n in (n_in, len(x_in), n_out, len(x_out), n_scr, len(x_sem)):
            groups.append(refs[pos:pos + n])
            pos += n
        ins, xin, outs, xout, scr, xsem = groups
        xin = split((xin, [len(ex.inputs) for ex in exchanges]))
        xout = split((xout, [len(ex.out_shapes) for ex in exchanges]))
        xsem = split((xsem, [len(ex.sems) for ex in exchanges]))
        first = pl.program_id(0) == 0
        last = pl.program_id(0) == grid[0] - 1
        for d in range(1, len(grid)):
            first = jnp.logical_and(first, pl.program_id(d) == 0)
            last = jnp.logical_and(last, pl.program_id(d) == grid[d] - 1)

        @pl.when(first)
        def _():
            for ex, i, o, s in zip(exchanges, xin, xout, xsem):
                ex.start(i, o, s)

        if any(ex.middle for ex in exchanges):
            half = pl.program_id(0) == 5 * grid[0] // 8
            for d in range(1, len(grid)):
                half = jnp.logical_and(half, pl.program_id(d) == 0)

            @pl.when(half)
            def _():
                for ex, i, o, s in zip(exchanges, xin, xout, xsem):
                    if ex.middle:
                        ex.middle(i, o, s)

        body(*ins, *outs, *scr)

        @pl.when(last)
        def _():
            for ex, i, o, s in zip(exchanges, xin, xout, xsem):
                ex.finish(i, o, s)

    res = pl.pallas_call(
        carrier, name=name, grid=grid, in_specs=in_specs + [ANY] * len(x_in),
        out_specs=out_specs + [ANY] * len(x_out), out_shape=out_shape + x_out,
        scratch_shapes=scratch + x_sem, input_output_aliases=aliases,
        compiler_params=_cp(*(["arbitrary"] * len(grid))),
    )(*args, *x_in)
    return res[:n_out], split((res[n_out:], [len(ex.out_shapes) for ex in exchanges]))


def _alone(name, *exchanges):
    n_in = [len(ex.inputs) for ex in exchanges]
    n_out = [len(ex.out_shapes) for ex in exchanges]
    n_sem = [len(ex.sems) for ex in exchanges]
    aliases, i_off, o_off = {}, 0, 0
    for ex in exchanges:
        for i, o in ex.aliases.items():
            aliases[i_off + i] = o_off + o
        i_off += len(ex.inputs)
        o_off += len(ex.out_shapes)

    def split(flat, counts):
        out, pos = [], 0
        for n in counts:
            out.append(flat[pos:pos + n])
            pos += n
        return out

    def body(*refs):
        ins, outs, sems = split(refs, [sum(n_in), sum(n_out), sum(n_sem)])
        groups = list(zip(exchanges, split(ins, n_in), split(outs, n_out), split(sems, n_sem)))
        for ex, i, o, s in groups:
            ex.start(i, o, s)
        for ex, i, o, s in groups:
            if ex.middle:
                ex.middle(i, o, s)
        for ex, i, o, s in groups:
            ex.finish(i, o, s)

    res = pl.pallas_call(
        body, name=name, in_specs=[ANY] * sum(n_in), out_specs=[ANY] * sum(n_out),
        out_shape=[s for ex in exchanges for s in ex.out_shapes],
        scratch_shapes=[s for ex in exchanges for s in ex.sems], input_output_aliases=aliases,
    )(*[a for ex in exchanges for a in ex.inputs])
    return split(res, n_out)


def _place():
    x, y, c = lax.axis_index("x"), lax.axis_index("y"), lax.axis_index("c")
    chips = [(1 - x, y), (x, 1 - y), (1 - x, 1 - y)]
    return x, y, c, chips


def _remote(src, dst, send, recv, to):
    return pltpu.make_async_remote_copy(src_ref=src, dst_ref=dst, send_sem=send, recv_sem=recv,
                                        device_id=to, device_id_type=MESH)


def _ex_gather(shards):
    nw = len(shards)
    hrs = [s.shape[0] // 2 for s in shards]

    def copies(ins, outs, sems):
        s1, r1, s2, r2, fs, fr = sems
        x, y, c, _ = _place()
        me, xn, yn, dg = (x, y), (1 - x, y), (x, 1 - y), (1 - x, 1 - y)
        nbr = (xn, yn)
        sibling = (x, y, 1 - c)

        def piece(w, chip, core, part=None):
            hr = hrs[w]
            rows = pl.ds(core * hr, hr) if part is None else pl.ds(core * hr + part * (hr // 2), hr // 2)
            return outs[w].at[2 * chip[0] + chip[1], rows]

        def first(w, k):
            return _remote(ins[w].at[pl.ds(c * hrs[w], hrs[w])], piece(w, me, c), s1.at[w, k], r1.at[w, k],
                           (*nbr[k], c))

        def landed(w, k):
            return _remote(piece(w, nbr[k], c), piece(w, nbr[k], c), s1.at[w, k], r1.at[w, k], (*nbr[k], c))

        def onward(w, k):
            return _remote(piece(w, nbr[k], c, k), piece(w, nbr[k], c, k), s2.at[w, k], r2.at[w, k],
                           (*nbr[1 - k], c))

        def arrived(w, k):
            return _remote(piece(w, dg, c, k), piece(w, dg, c, k), s2.at[w, k], r2.at[w, k], (*nbr[1 - k], c))

        def passed(w, j):
            chip = (xn, yn, dg)[j]
            return _remote(piece(w, chip, c), piece(w, chip, c), fs.at[w, j], fr.at[w, j], sibling)

        def handed(w, j):
            chip = (xn, yn, dg)[j]
            return _remote(piece(w, chip, 1 - c), piece(w, chip, 1 - c), fs.at[w, j], fr.at[w, j], sibling)

        return first, landed, onward, arrived, passed, handed

    def start(ins, outs, sems):
        first = copies(ins, outs, sems)[0]
        for w in range(nw):
            for k in range(2):
                first(w, k).start()

    def middle(ins, outs, sems):
        _, landed, onward, _, passed, _ = copies(ins, outs, sems)
        for w in range(nw):
            for k in range(2):
                landed(w, k).wait_recv()
                onward(w, k).start()
                passed(w, k).start()

    def finish(ins, outs, sems):
        first, _, onward, arrived, passed, handed = copies(ins, outs, sems)
        for w in range(nw):
            for k in range(2):
                arrived(w, k).wait_recv()
            passed(w, 2).start()
        for w in range(nw):
            for j in range(3):
                handed(w, j).wait_recv()
        for w in range(nw):
            for k in range(2):
                first(w, k).wait_send()
                onward(w, k).wait_send()
            for j in range(3):
                passed(w, j).wait_send()

    return _Exchange(shards, [_sds((N_CHIPS,) + s.shape, s.dtype) for s in shards],
                     [pltpu.SemaphoreType.DMA((nw, 2))] * 4 + [pltpu.SemaphoreType.DMA((nw, 3))] * 2,
                     start, finish, middle=middle)


def _ex_pair(grads):
    nw = len(grads)

    def copies(ins, outs, sems):
        x, y, c, _ = _place()
        out = []
        for w in range(nw):
            hr = grads[w].shape[1] // 2
            out.append(_remote(ins[w].at[:, pl.ds((1 - c) * hr, hr)], outs[w], sems[0].at[w], sems[1].at[w],
                               (x, y, 1 - c)))
        return out

    def start(ins, outs, sems):
        for cp in copies(ins, outs, sems):
            cp.start()

    def finish(ins, outs, sems):
        for cp in copies(ins, outs, sems):
            cp.wait()

    return _Exchange(grads, [_sds((N_CHIPS, g.shape[1] // 2, g.shape[2]), F32) for g in grads],
                     [pltpu.SemaphoreType.DMA((nw,))] * 2, start, finish)


def _ex_chip(pieces):
    nw = len(pieces)

    def copies(ins, outs, sems):
        x, y, c, chips = _place()
        return [_remote(ins[w].at[2 * cx + cy], outs[w].at[k], sems[0].at[w, k], sems[1].at[w, k], (cx, cy, c))
                for w in range(nw) for k, (cx, cy) in enumerate(chips)]

    def start(ins, outs, sems):
        for cp in copies(ins, outs, sems):
            cp.start()

    def finish(ins, outs, sems):
        for cp in copies(ins, outs, sems):
            cp.wait()

    return _Exchange(pieces, [_sds((3,) + p.shape[1:], BF16) for p in pieces],
                     [pltpu.SemaphoreType.DMA((nw, 3))] * 2, start, finish)


def _ex_swap(fulls):
    nw = len(fulls)

    def start(ins, outs, sems):
        x, y, c, _ = _place()
        for w in range(nw):
            hr = fulls[w].shape[0] // 2
            mine = pl.ds(c * hr, hr)
            _remote(ins[w].at[mine], outs[w].at[mine], sems[0].at[w], sems[1].at[w], (x, y, 1 - c)).start()

    def finish(ins, outs, sems):
        x, y, c, _ = _place()
        for w in range(nw):
            hr = fulls[w].shape[0] // 2
            mine, theirs = pl.ds(c * hr, hr), pl.ds((1 - c) * hr, hr)
            _remote(ins[w].at[mine], outs[w].at[mine], sems[0].at[w], sems[1].at[w], (x, y, 1 - c)).wait_send()
            _remote(ins[w].at[theirs], outs[w].at[theirs], sems[0].at[w], sems[1].at[w], (x, y, 1 - c)).wait_recv()

    return _Exchange(fulls, [_sds(f.shape, F32) for f in fulls], [pltpu.SemaphoreType.DMA((nw,))] * 2,
                     start, finish, aliases={w: w for w in range(nw)})


def _ex_allgather(blocks):
    nb = len(blocks)

    def copies(ins, outs, sems):
        send, recv, lsem = sems
        x, y, c, chips = _place()
        me, sibling = (x, y, c), (x, y, 1 - c)

        def rows(b, px, py, pc):
            m_per = blocks[b].shape[0]
            return outs[b].at[pl.ds((4 * px + 2 * py + pc) * m_per, m_per), :]

        def copy(b, k, blk, to, src=None):
            return _remote(rows(b, *blk) if src is None else src, rows(b, *blk), send.at[b, k], recv.at[b, k], to)

        def mine(b):
            return pltpu.make_async_copy(ins[b], rows(b, *me), lsem.at[b])

        def first(b, k):
            return copy(b, k, me, sibling if k == 0 else (*chips[k - 1], c), src=ins[b])

        def passed(b, j):
            return copy(b, 4 + j, (*chips[j], c), sibling)

        def landed(b, j):
            return copy(b, 1 + j, (*chips[j], c), me)

        def handed(b, k):
            return copy(b, 0, sibling, me) if k == 0 else copy(b, 3 + k, (*chips[k - 1], 1 - c), me)

        return mine, first, passed, landed, handed

    def start(ins, outs, sems):
        mine, first, _, _, _ = copies(ins, outs, sems)
        for b in range(nb):
            mine(b).start()
            for k in range(4):
                first(b, k).start()

    def finish(ins, outs, sems):
        mine, first, passed, landed, handed = copies(ins, outs, sems)
        sent = []
        for b in range(nb):
            for j in range(3):
                landed(b, j).wait_recv()
                cp = passed(b, j)
                cp.start()
                sent.append(cp)
        for b in range(nb):
            for k in range(4):
                handed(b, k).wait_recv()
            for k in range(4):
                first(b, k).wait_send()
        for cp in sent:
            cp.wait_send()
        for b in range(nb):
            mine(b).wait()

    return _Exchange(blocks, [_sds((N_DEV * b.shape[0], b.shape[1]), F32) for b in blocks],
                     [pltpu.SemaphoreType.DMA((nb, 7)), pltpu.SemaphoreType.DMA((nb, 7)), pltpu.SemaphoreType.DMA((nb,))],
                     start, finish)


def _cast_shards(w_up, w_down, w_bp, w_ba, w_out, exchanges=()):
    r, c = w_up.shape
    tr = HALF // 2
    steps = r // tr

    def body(up_ref, down_ref, bp_ref, ba_ref, out_ref, ua_ref, ub_ref, da_ref, db_ref, bpo_ref, bao_ref, outo_ref):
        i = pl.program_id(0)

        @pl.when(i == 0)
        def _():
            for src, dst in ((bp_ref, bpo_ref), (ba_ref, bao_ref), (out_ref, outo_ref)):
                dst[...] = src[...].astype(BF16)

        @pl.when(i < steps // 2)
        def _():
            ua_ref[...] = up_ref[...].astype(BF16)
            da_ref[...] = down_ref[...].astype(BF16)

        @pl.when(i >= steps // 2)
        def _():
            ub_ref[...] = up_ref[...].astype(BF16)
            db_ref[...] = down_ref[...].astype(BF16)

    rows = _rows(tr, c)
    first = pl.BlockSpec((tr, c), lambda i: (jnp.minimum(i, steps // 2 - 1), 0))
    second = pl.BlockSpec((tr, c), lambda i: (jnp.maximum(i - steps // 2, 0), 0))
    half = _sds((HALF, c), BF16)
    return _call(
        body, name="cast_shards", grid=(steps,),
        in_specs=[rows, rows, _const(w_bp.shape), _const(w_ba.shape), _const(w_out.shape)],
        out_specs=[first, second, first, second, _const(w_bp.shape), _const(w_ba.shape), _const(w_out.shape)],
        out_shape=[half, half, half, half, _sds(w_bp.shape, BF16), _sds(w_ba.shape, BF16), _sds(w_out.shape, BF16)],
        args=(w_up, w_down, w_bp, w_ba, w_out), sem=("arbitrary",), exchanges=exchanges)


def _inproj(x2, g1, w_in_t, b_in, tabs, seq, exchanges=()):
    T = x2.shape[0]
    tm = min(TM, seq)
    nseq = seq // tm

    def body(x_ref, g_ref, w_ref, b_ref, c_ref, a_ref, bt_ref, h_ref, u_ref, q_ref, k_ref, v_ref, gate_ref):
        x = x_ref[...]
        h = (x * _rms(x) * g_ref[...]).astype(BF16)
        h_ref[...] = h

        def proj(lo, hi):
            return _dot_nt(h, w_ref[lo:hi, :]) + b_ref[:, lo:hi]

        c, a, bt = c_ref[...], a_ref[...], bt_ref[...]
        u_ref[...] = proj(0, C_Q)
        q = proj(C_Q, C_K)
        for p in range(4):
            sl = slice(LANES * p, LANES * (p + 1))
            q_ref[:, sl] = (_rot_fwd(q[:, sl], c, a, bt) * SCALE).astype(BF16)
        kv = proj(C_K, C_G)
        k_ref[...] = _rot_fwd(kv[:, :KV_WIDTH], c, a, bt).astype(BF16)
        v_ref[...] = kv[:, KV_WIDTH:].astype(BF16)
        for j in range(2):
            lo = C_G + D_MODEL * j
            gate_ref[:, D_MODEL * j:D_MODEL * (j + 1)] = jax.nn.sigmoid(proj(lo, lo + D_MODEL)).astype(BF16)

    tab = pl.BlockSpec((tm, LANES), lambda i: (i % nseq, 0))
    return _call(
        body, name="inproj", grid=(T // tm,),
        in_specs=[_rows(tm, D_MODEL), _const((1, D_MODEL)), _const((IN_WIDTH, D_MODEL)), _const((1, IN_WIDTH)),
                  tab, tab, tab],
        out_specs=[_rows(tm, D_MODEL), _rows(tm, POOL_WIDTH), _rows(tm, ATTN_WIDTH), _rows(tm, KV_WIDTH),
                   _rows(tm, KV_WIDTH), _rows(tm, GATE_WIDTH)],
        out_shape=[_sds((T, D_MODEL), BF16), _sds((T, POOL_WIDTH), F32), _sds((T, ATTN_WIDTH), BF16),
                   _sds((T, KV_WIDTH), BF16), _sds((T, KV_WIDTH), BF16), _sds((T, GATE_WIDTH), BF16)],
        args=(x2, g1, w_in_t, b_in, *tabs), sem=("parallel",), exchanges=exchanges)


def _inv_count(pos, w):
    return 1.0 / jnp.minimum(pos + 1, w).astype(F32)


def _pool_tile(i, tp, nseq, u_ref, prev_ref, w_ref, s_ref, diff_ref, y_ref):
    first = (i % nseq) == 0
    prev = jnp.where(first, 0.0, prev_ref[...])
    ext = jnp.concatenate([prev, u_ref[...]], axis=0)
    pos = (i % nseq) * tp + lax.broadcasted_iota(jnp.int32, (tp, 1), 0)
    for gi, w in enumerate(POOL_WINDOWS):
        sl = slice(POOL_GC * gi, POOL_GC * (gi + 1))
        xg = ext[:, sl]
        s = xg
        sh = 1
        while sh < w:
            s = s + pltpu.roll(s, sh, 0)
            sh *= 2
        pooled = s[HALO:] * _inv_count(pos, w)
        diff = (pooled - xg[HALO:]).astype(BF16)
        diff_ref[:, sl] = diff
        mixed = _dot(diff, w_ref[gi].astype(BF16))
        y_ref[:, sl] = (mixed * s_ref[:, sl]).astype(BF16)


def _pool_specs(tp):
    per = tp // HALO
    return [_rows(tp, POOL_WIDTH), pl.BlockSpec((HALO, POOL_WIDTH), lambda i: (jnp.maximum(i * per - 1, 0), 0)),
            _const((4, POOL_GC, POOL_GC)), _const((1, POOL_WIDTH))]


GROUP = 4
GROWS = GROUP * BLOCK


def _attn_masks(n):
    qi = lax.broadcasted_iota(jnp.int32, (GROWS, 2 * BLOCK), 0) % BLOCK
    kj = lax.broadcasted_iota(jnp.int32, (GROWS, 2 * BLOCK), 1)
    rel = qi + BLOCK - kj
    valid = (rel >= 0) & (rel < BLOCK) & (kj >= jnp.where(n > 0, 0, BLOCK))
    lo = lax.broadcasted_iota(jnp.int32, (BLOCK, LANES), 1) < HEAD_DIM
    return valid, lo


def _by_example(bl, *arrays):
    return [a.reshape(bl, a.shape[0] // bl, a.shape[1]) for a in arrays]


def _stack_heads(ref, h, lo):
    keep = lo if h == 0 else jnp.logical_not(lo)
    pieces = []
    for p in (2 * h, 2 * h + 1):
        xp = ref[:, LANES * p:LANES * (p + 1)].astype(F32)
        for e in range(2):
            t = xp if e == h else pltpu.roll(xp, HEAD_DIM, 1)
            pieces.append(jnp.where(keep, t, 0.0).astype(BF16))
    return jnp.concatenate(pieces, axis=0)


def _unstack_heads(stacked, h, lo):
    pairs = []
    for j in range(2):
        parts = []
        for e in range(2):
            t = stacked[BLOCK * (2 * j + e):BLOCK * (2 * j + e + 1)]
            parts.append(t if e == h else pltpu.roll(t, HEAD_DIM, 1))
        pairs.append(jnp.where(lo, parts[0], parts[1]))
    return pairs


def _sink_rows(sink_ref, h):
    head = lax.broadcasted_iota(jnp.int32, (GROWS, 1), 0) // BLOCK
    col = jnp.zeros((GROWS, 1), F32) + sink_ref[GROUP * h]
    for g in range(1, GROUP):
        col = jnp.where(head == g, sink_ref[GROUP * h + g], col)
    return col


def _group_probs(qs, kk, valid, sink):
    s = jnp.where(valid, _dot_nt(qs, kk), NEG_INF)
    m = jnp.maximum(jnp.max(s, axis=1, keepdims=True), sink)
    ex = jnp.exp(s - m)
    es = jnp.exp(sink - m)
    inv = 1.0 / (jnp.sum(ex, axis=1, keepdims=True) + es)
    return ex * inv, es * inv


def _mixers_fwd(q, k, v, sinks, u, w_pool, pool_scale, seq, exchanges=()):
    T = q.shape[0]
    nb = seq // BLOCK
    bl = T // seq
    tp = T // nb
    nseq = seq // tp

    def body(sink_ref, q_ref, kp_ref, kc_ref, vp_ref, vc_ref, u_ref, prev_ref, w_ref, s_ref, o_ref, diff_ref, y_ref):
        n = pl.program_id(0)
        valid, lo = _attn_masks(n)
        for b in range(bl):
            kk = jnp.concatenate([kp_ref[b], kc_ref[b]], axis=0)
            vv = jnp.concatenate([vp_ref[b], vc_ref[b]], axis=0)
            for h in range(2):
                qs = _stack_heads(q_ref.at[b], h, lo)
                pr, _ = _group_probs(qs, kk, valid, _sink_rows(sink_ref, h))
                o = _dot(pr.astype(BF16), vv)
                for j, pair in enumerate(_unstack_heads(o, h, lo)):
                    p = 2 * h + j
                    o_ref[b, :, LANES * p:LANES * (p + 1)] = pair.astype(BF16)
        _pool_tile(n, tp, nseq, u_ref, prev_ref, w_ref, s_ref, diff_ref, y_ref)

    cur = lambda n: (0, n, 0)
    prv = lambda n: (0, jnp.maximum(n - 1, 0), 0)
    kv = lambda m: pl.BlockSpec((bl, BLOCK, KV_WIDTH), m)
    res = _call(
        body, name="mixers_fwd", grid=(nb,),
        in_specs=[pl.BlockSpec(memory_space=pltpu.SMEM), pl.BlockSpec((bl, BLOCK, ATTN_WIDTH), cur),
                  kv(prv), kv(cur), kv(prv), kv(cur)] + _pool_specs(tp),
        out_specs=[pl.BlockSpec((bl, BLOCK, ATTN_WIDTH), cur), _rows(tp, POOL_WIDTH), _rows(tp, POOL_WIDTH)],
        out_shape=[_sds((bl, seq, ATTN_WIDTH), BF16), _sds((T, POOL_WIDTH), BF16), _sds((T, POOL_WIDTH), BF16)],
        args=(sinks, *_by_example(bl, q, k, k, v, v), u, u, w_pool, pool_scale), sem=("parallel",),
        exchanges=exchanges)
    outs, rest = res if exchanges else (res, None)
    return [outs[0].reshape(T, ATTN_WIDTH), outs[1], outs[2]], rest


def _branch(y, w_ref):
    return jnp.concatenate([_dot(y, w_ref[j]) for j in range(N_CHIPS)], axis=1)


def _merge_out(y_pool, y_attn, gate, x2, w_bp, w_ba, w_out, g2, g3, exchanges=()):
    T = x2.shape[0]
    tm = min(TM, T)

    def body(yp_ref, ya_ref, gate_ref, x_ref, wbp_ref, wba_ref, wo_ref, g2_ref, g3_ref,
             mg_ref, mix_ref, x1_ref, h2_ref):
        bp, ba = _branch(yp_ref[...], wbp_ref), _branch(ya_ref[...], wba_ref)
        merged = (gate_ref[:, :D_MODEL].astype(F32) * bp + gate_ref[:, D_MODEL:].astype(F32) * ba).astype(BF16)
        mg_ref[...] = merged
        mix = _dot(merged, wo_ref[...])
        mix_ref[...] = mix
        x1 = x_ref[...] + mix * _rms(mix) * g2_ref[...]
        x1_ref[...] = x1
        h2_ref[...] = (x1 * _rms(x1) * g3_ref[...]).astype(BF16)

    return _call(
        body, name="merge_out", grid=(T // tm,),
        in_specs=[_rows(tm, POOL_WIDTH), _rows(tm, ATTN_WIDTH), _rows(tm, GATE_WIDTH), _rows(tm, D_MODEL),
                  _const(w_bp.shape), _const(w_ba.shape), _const((D_MODEL, D_MODEL)),
                  _const((1, D_MODEL)), _const((1, D_MODEL))],
        out_specs=[_rows(tm, D_MODEL)] * 4,
        out_shape=[_sds((T, D_MODEL), BF16), _sds((T, D_MODEL), F32), _sds((T, D_MODEL), F32),
                   _sds((T, D_MODEL), BF16)],
        args=(y_pool, y_attn, gate, x2, w_bp, w_ba, w_out, g2, g3), sem=("parallel",), exchanges=exchanges)


HALF = D_MODEL // 2
TM_MLP = 256


def _mlp_core(h2, x1, mix, tgt, w_up, w_down, g4, g3, g2):
    T = h2.shape[0]
    tm = min(TM_MLP, T)

    def body(h_ref, x1_ref, mix_ref, t_ref, g_ref, g3_ref, g2_ref, ua_hbm, ub_hbm, da_hbm, db_hbm,
             act_ref, dff_ref, dup_ref, dx1_ref, dmix_ref, loss_ref, dg_ref, dg3_ref, dg2_ref,
             wu, wd, relu_scr, sems):
        def weight_copy(i):
            src, dst = ((ua_hbm, wu.at[:, :HALF]), (ub_hbm, wu.at[:, HALF:]),
                        (da_hbm, wd.at[:, :HALF]), (db_hbm, wd.at[:, HALF:]))[i]
            return pltpu.make_async_copy(src, dst, sems.at[i])

        @pl.when(pl.program_id(0) == 0)
        def _():
            for i in range(4):
                weight_copy(i).start()
            loss_ref[...] = jnp.zeros_like(loss_ref)
            for ref in (dg_ref, dg3_ref, dg2_ref):
                ref[...] = jnp.zeros_like(ref)
            weight_copy(0).wait()
            weight_copy(1).wait()

        h = h_ref[...]
        ff = None
        for j in range(N_CHIPS):
            lo = D_MODEL * j
            relu = jnp.maximum(_dot(h, wu[j]), 0.0)
            if j == 0:
                @pl.when(pl.program_id(0) == 0)
                def _():
                    weight_copy(2).wait()
                    weight_copy(3).wait()
            relu_scr[:, lo:lo + D_MODEL] = relu
            act = jnp.square(relu).astype(BF16)
            act_ref[:, lo:lo + D_MODEL] = act
            t = _dot(act, wd[j])
            ff = t if ff is None else ff + t
        g = g_ref[...]
        x1 = x1_ref[...]
        err = x1 + ff * _rms(ff) * g - t_ref[...]
        loss_ref[...] += jnp.sum(err * err) * (0.5 / D_MODEL)
        dy = err * (1.0 / D_MODEL)
        dff, dg = _norm_bwd(ff, g, dy)
        dg_ref[...] += dg
        dff = dff.astype(BF16)
        dff_ref[...] = dff
        dh2 = None
        for j in range(N_CHIPS):
            lo = D_MODEL * j
            dup = (_dot_nt(dff, wd[j]) * (2.0 * relu_scr[:, lo:lo + D_MODEL])).astype(BF16)
            dup_ref[:, lo:lo + D_MODEL] = dup
            t = _dot_nt(dup, wu[j])
            dh2 = t if dh2 is None else dh2 + t
        dx, dg3 = _norm_bwd(x1, g3_ref[...], dh2)
        dx1 = dy + dx
        dx1_ref[...] = dx1
        dg3_ref[...] += dg3
        dmix, dg2 = _norm_bwd(mix_ref[...], g2_ref[...], dx1)
        dmix_ref[...] = dmix.astype(BF16)
        dg2_ref[...] += dg2

    slabs = pltpu.VMEM((N_CHIPS, D_MODEL, D_MODEL), BF16)
    gain = _const((1, D_MODEL))
    return pl.pallas_call(
        body, name="mlp_core", grid=(T // tm,),
        in_specs=[_rows(tm, D_MODEL)] * 4 + [gain] * 3 + [ANY] * 4,
        out_specs=[_rows(tm, D_FF), _rows(tm, D_MODEL), _rows(tm, D_FF), _rows(tm, D_MODEL), _rows(tm, D_MODEL),
                   _const((8, LANES)), gain, gain, gain],
        out_shape=[_sds((T, D_FF), BF16), _sds((T, D_MODEL), BF16), _sds((T, D_FF), BF16), _sds((T, D_MODEL), F32),
                   _sds((T, D_MODEL), BF16), _sds((8, LANES), F32)] + [_sds((1, D_MODEL), F32)] * 3,
        scratch_shapes=[slabs] * 2 + [pltpu.VMEM((tm, D_FF), F32), pltpu.SemaphoreType.DMA((4,))],
        compiler_params=_cp("arbitrary"),
    )(h2, x1, mix, tgt, g4, g3, g2, *w_up, *w_down)


def _dw(tag, a, g, ta, tn, shard_cols=False, exchanges=()):
    T, ka = a.shape
    n = g.shape[1]
    tk = min(2 * TM, T)
    nk = T // tk

    def body(a_ref, g_ref, o_ref):
        @pl.when(pl.program_id(2) == 0)
        def _():
            o_ref[...] = jnp.zeros_like(o_ref)

        o_ref[...] += _dot_tn(a_ref[...], g_ref[...])

    if shard_cols:
        per = (n // N_CHIPS) // tn
        out_spec = pl.BlockSpec((None, ta, tn), lambda i, j, k: (j // per, i, j % per))
        out_shape = _sds((N_CHIPS, ka, n // N_CHIPS), F32)
    else:
        out_spec = pl.BlockSpec((ta, tn), lambda i, j, k: (i, j))
        out_shape = _sds((ka, n), F32)
    return _call(
        body, name="dw_" + tag, grid=(ka // ta, n // tn, nk),
        in_specs=[pl.BlockSpec((tk, ta), lambda i, j, k: (k, i)), pl.BlockSpec((tk, tn), lambda i, j, k: (k, j))],
        out_specs=[out_spec], out_shape=[out_shape],
        args=(a, g), sem=("parallel", "parallel", "arbitrary"), exchanges=exchanges)


def _dw_mix(merged, dmix, y_pool, dbp, y_attn, dba, exchanges=()):
    T = merged.shape[0]
    tk = min(2 * TM, T)
    c = D_MODEL // N_CHIPS

    def body(mg_ref, dmix_ref, yp_ref, dbp_ref, ya_ref, dba_ref, out_ref, bp_ref, ba_ref):
        @pl.when(pl.program_id(0) == 0)
        def _():
            for ref in (out_ref, bp_ref, ba_ref):
                ref[...] = jnp.zeros_like(ref)

        out_ref[...] += _dot_tn(mg_ref[...], dmix_ref[...])
        for y_ref, d_ref, o_ref in ((yp_ref, dbp_ref, bp_ref), (ya_ref, dba_ref, ba_ref)):
            res = _dot_tn(y_ref[...], d_ref[...])
            for j in range(N_CHIPS):
                o_ref[j] += res[:, c * j:c * (j + 1)]

    slabs = (N_CHIPS, POOL_WIDTH, c)
    return _call(
        body, name="dw_mix", grid=(T // tk,),
        in_specs=[_rows(tk, D_MODEL), _rows(tk, D_MODEL), _rows(tk, POOL_WIDTH), _rows(tk, D_MODEL),
                  _rows(tk, ATTN_WIDTH), _rows(tk, D_MODEL)],
        out_specs=[_const((D_MODEL, D_MODEL)), _const(slabs), _const(slabs)],
        out_shape=[_sds((D_MODEL, D_MODEL), F32), _sds(slabs, F32), _sds(slabs, F32)],
        args=(merged, dmix, y_pool, dbp, y_attn, dba), sem=("arbitrary",), exchanges=exchanges)


def _merge_bwd(dmix, gate, y_pool, y_attn, w_out, w_bp, w_ba, exchanges=()):
    T = dmix.shape[0]
    tm = min(TM, T)

    def body(dmix_ref, gate_ref, yp_ref, ya_ref, wo_ref, wbp_ref, wba_ref,
             dbp_ref, dba_ref, dgate_ref, dyp_ref, dya_ref):
        dm = _dot_nt(dmix_ref[...], wo_ref[...])
        for j, (y_ref, db_ref, w_ref, dy_ref) in enumerate(
                ((yp_ref, dbp_ref, wbp_ref, dyp_ref), (ya_ref, dba_ref, wba_ref, dya_ref))):
            sl = slice(D_MODEL * j, D_MODEL * (j + 1))
            gt = gate_ref[:, sl].astype(F32)
            db = (dm * gt).astype(BF16)
            db_ref[...] = db
            dgate_ref[:, sl] = (dm * _branch(y_ref[...], w_ref) * gt * (1.0 - gt)).astype(BF16)
            cw = D_MODEL // N_CHIPS
            dy = _dot_nt(db[:, :cw], w_ref[0])
            for c in range(1, N_CHIPS):
                dy = dy + _dot_nt(db[:, cw * c:cw * (c + 1)], w_ref[c])
            dy_ref[...] = dy.astype(dy_ref.dtype)

    return _call(
        body, name="merge_bwd", grid=(T // tm,),
        in_specs=[_rows(tm, D_MODEL), _rows(tm, GATE_WIDTH), _rows(tm, POOL_WIDTH), _rows(tm, ATTN_WIDTH),
                  _const((D_MODEL, D_MODEL)), _const(w_bp.shape), _const(w_ba.shape)],
        out_specs=[_rows(tm, D_MODEL), _rows(tm, D_MODEL), _rows(tm, GATE_WIDTH), _rows(tm, POOL_WIDTH),
                   _rows(tm, ATTN_WIDTH)],
        out_shape=[_sds((T, D_MODEL), BF16), _sds((T, D_MODEL), BF16), _sds((T, GATE_WIDTH), BF16),
                   _sds((T, POOL_WIDTH), F32), _sds((T, ATTN_WIDTH), BF16)],
        args=(dmix, gate, y_pool, y_attn, w_out, w_bp, w_ba), sem=("parallel",), exchanges=exchanges)


def _mixers_bwd(q, k, v, do, sinks, tabs, dyp, diff, w_pool, pool_scale, seq, exchanges=()):
    T = q.shape[0]
    nb = seq // BLOCK
    bl = T // seq
    steps = nb + 1
    tp = T // nb
    nseq = seq // tp
    per = tp // HALO
    last_halo = T // HALO - 1

    def body(sink_ref, q_ref, do_ref, kp_ref, kc_ref, vp_ref, vc_ref, c_ref, a_ref, bt_ref, cp_ref, ap_ref, btp_ref,
             dy_ref, nxt_ref, diff_ref, w_ref, s_ref,
             dq_ref, dk_ref, dv_ref, dsink_ref, du_ref, dw_ref, ds_ref, ck_ref, cv_ref):
        n = pl.program_id(0)

        @pl.when(n == 0)
        def _():
            for ref in (dsink_ref, ck_ref, cv_ref, dw_ref, ds_ref):
                ref[...] = jnp.zeros_like(ref)

        @pl.when(n < nb)
        def _():
            _pool_bwd_tile(n, tp, nseq, dy_ref, nxt_ref, diff_ref, w_ref, s_ref, du_ref, dw_ref, ds_ref)
            valid, lo = _attn_masks(n)
            for b in range(bl):
                kk = jnp.concatenate([kp_ref[b], kc_ref[b]], axis=0)
                vv = jnp.concatenate([vp_ref[b], vc_ref[b]], axis=0)
                dk_acc = jnp.zeros((2 * BLOCK, KV_WIDTH), F32)
                dv_acc = jnp.zeros((2 * BLOCK, KV_WIDTH), F32)
                for h in range(2):
                    qs = _stack_heads(q_ref.at[b], h, lo)
                    dos = _stack_heads(do_ref.at[b], h, lo)
                    pr, ps = _group_probs(qs, kk, valid, _sink_rows(sink_ref, h))
                    dp = _dot_nt(dos, vv)
                    delta = jnp.sum(pr * dp, axis=1, keepdims=True)
                    ds = (pr * (dp - delta)).astype(BF16)
                    dsk = ps * delta
                    for g in range(GROUP):
                        idx = GROUP * h + g
                        dsink_ref[idx:idx + 1, :] += (jnp.zeros((1, LANES), F32)
                                                      - jnp.sum(dsk[BLOCK * g:BLOCK * (g + 1)]))
                    dk_acc = dk_acc + _dot_tn(ds, qs)
                    dv_acc = dv_acc + _dot_tn(pr.astype(BF16), dos)
                    for j, pair in enumerate(_unstack_heads(_dot(ds, kk) * SCALE, h, lo)):
                        sl = slice(LANES * (2 * h + j), LANES * (2 * h + j + 1))
                        dq_ref[b, :, sl] = _rot_bwd(pair, c_ref[...], a_ref[...], bt_ref[...]).astype(BF16)
                fin_k = ck_ref[b] + dk_acc[:BLOCK]
                dk_ref[b] = _rot_bwd(fin_k, cp_ref[...], ap_ref[...], btp_ref[...]).astype(BF16)
                dv_ref[b] = (cv_ref[b] + dv_acc[:BLOCK]).astype(BF16)
                ck_ref[b] = dk_acc[BLOCK:]
                cv_ref[b] = dv_acc[BLOCK:]

        @pl.when(n == nb)
        def _():
            for b in range(bl):
                dk_ref[b] = _rot_bwd(ck_ref[b], cp_ref[...], ap_ref[...], btp_ref[...]).astype(BF16)
                dv_ref[b] = cv_ref[b].astype(BF16)

    cur = lambda n: (0, jnp.minimum(n, nb - 1), 0)
    prv = lambda n: (0, jnp.clip(n - 1, 0, nb - 1), 0)
    tcur = lambda n: (jnp.minimum(n, nb - 1), 0)
    tprv = lambda n: (jnp.clip(n - 1, 0, nb - 1), 0)
    wide = lambda m: pl.BlockSpec((bl, BLOCK, ATTN_WIDTH), m)
    kv = lambda m: pl.BlockSpec((bl, BLOCK, KV_WIDTH), m)
    tab = lambda m: pl.BlockSpec((BLOCK, LANES), m)
    tile = lambda n: (jnp.minimum(n, nb - 1), 0)
    halo = lambda n: (jnp.minimum((jnp.minimum(n, nb - 1) + 1) * per, last_halo), 0)
    rows = pl.BlockSpec((tp, POOL_WIDTH), tile)
    res = _call(
        body, name="mixers_bwd", grid=(steps,),
        in_specs=[pl.BlockSpec(memory_space=pltpu.SMEM), wide(cur), wide(cur), kv(prv), kv(cur), kv(prv), kv(cur),
                  tab(tcur), tab(tcur), tab(tcur), tab(tprv), tab(tprv), tab(tprv),
                  rows, pl.BlockSpec((HALO, POOL_WIDTH), halo), rows, _const((4, POOL_GC, POOL_GC)),
                  _const((1, POOL_WIDTH))],
        out_specs=[wide(cur), kv(prv), kv(prv), _const((8, LANES)), rows, _const((4, POOL_GC, POOL_GC)),
                   _const((1, POOL_WIDTH))],
        out_shape=[_sds((bl, seq, ATTN_WIDTH), BF16), _sds((bl, seq, KV_WIDTH), BF16),
                   _sds((bl, seq, KV_WIDTH), BF16), _sds((8, LANES), F32), _sds((T, POOL_WIDTH), BF16),
                   _sds((4, POOL_GC, POOL_GC), F32), _sds((1, POOL_WIDTH), F32)],
        scratch=[pltpu.VMEM((bl, BLOCK, KV_WIDTH), F32), pltpu.VMEM((bl, BLOCK, KV_WIDTH), F32)],
        args=(sinks, *_by_example(bl, q, do, k, k, v, v), *tabs, *tabs, dyp, dyp, diff, w_pool, pool_scale),
        sem=("arbitrary",), exchanges=exchanges)
    outs, rest = (res if exchanges else (res, None))
    outs = [outs[0].reshape(T, ATTN_WIDTH), outs[1].reshape(T, KV_WIDTH), outs[2].reshape(T, KV_WIDTH), *outs[3:]]
    return (outs, rest) if exchanges else outs


def _pool_bwd_tile(i, tp, nseq, dy_ref, nxt_ref, diff_ref, w_ref, s_ref, du_ref, dw_ref, ds_ref):
    last = (i % nseq) == nseq - 1
    nxt = jnp.where(last, 0.0, nxt_ref[...])
    ext = jnp.concatenate([dy_ref[...], nxt], axis=0) * s_ref[...]
    pos = (i % nseq) * tp + lax.broadcasted_iota(jnp.int32, (tp + HALO, 1), 0)
    for gi, w in enumerate(POOL_WINDOWS):
        sl = slice(POOL_GC * gi, POOL_GC * (gi + 1))
        wg = w_ref[gi].astype(BF16)
        dmx = ext[:, sl].astype(BF16)
        ddiff = _dot_nt(dmx, wg)
        s = ddiff * _inv_count(pos, w)
        sh = 1
        while sh < w:
            s = s + pltpu.roll(s, tp + HALO - sh, 0)
            sh *= 2
        du_ref[:, sl] = (s[:tp] - ddiff[:tp]).astype(BF16)
        dg = diff_ref[:, sl]
        dw_ref[gi] += _dot_tn(dg, dmx[:tp])
        ds_ref[:, sl] += jnp.sum(dy_ref[:, sl] * _dot(dg, wg), axis=0, keepdims=True)


_PARTS = ((0, C_Q), (C_Q, C_K), (C_K, C_V), (C_V, C_G), (C_G, IN_WIDTH))


def _inproj_bwd(parts, x2, dx1, w_in_t, g1, exchanges=()):
    T = x2.shape[0]
    tm = min(TM, T)

    def body(du_ref, dq_ref, dk_ref, dv_ref, dgt_ref, x_ref, dx1_ref, w_ref, g_ref, gx_ref, dg_ref):
        @pl.when(pl.program_id(0) == 0)
        def _():
            dg_ref[...] = jnp.zeros_like(dg_ref)

        dh = jnp.zeros((tm, D_MODEL), F32)
        for (lo, hi), p_ref in zip(_PARTS, (du_ref, dq_ref, dk_ref, dv_ref, dgt_ref)):
            dh = dh + _dot(p_ref[...], w_ref[lo:hi, :])
        dx, dg = _norm_bwd(x_ref[...], g_ref[...], dh)
        gx_ref[...] = dx1_ref[...] + dx
        dg_ref[...] += dg

    return _call(
        body, name="inproj_bwd", grid=(T // tm,),
        in_specs=[_rows(tm, hi - lo) for lo, hi in _PARTS]
        + [_rows(tm, D_MODEL), _rows(tm, D_MODEL), _const((IN_WIDTH, D_MODEL)), _const((1, D_MODEL))],
        out_specs=[_rows(tm, D_MODEL), _const((1, D_MODEL))],
        out_shape=[_sds((T, D_MODEL), F32), _sds((1, D_MODEL), F32)],
        args=(*parts, x2, dx1, w_in_t, g1), sem=("arbitrary",), exchanges=exchanges)


def _dw_in(h, parts, exchanges=()):
    T = h.shape[0]
    tk = min(TM, T)

    def body(h_ref, du_ref, dq_ref, dk_ref, dv_ref, dgt_ref, o_ref, db_ref):
        @pl.when(pl.program_id(0) == 0)
        def _():
            o_ref[...] = jnp.zeros_like(o_ref)
            db_ref[...] = jnp.zeros_like(db_ref)

        hh = h_ref[...]
        for (lo, hi), p_ref in zip(_PARTS, (du_ref, dq_ref, dk_ref, dv_ref, dgt_ref)):
            part = p_ref[...]
            o_ref[lo:hi, :] += _dot_tn(part, hh)
            db_ref[:, lo:hi] += jnp.sum(part.astype(F32), axis=0, keepdims=True)

    return _call(
        body, name="dw_in", grid=(T // tk,),
        in_specs=[_rows(tk, D_MODEL)] + [_rows(tk, hi - lo) for lo, hi in _PARTS],
        out_specs=[_const((IN_WIDTH, D_MODEL)), _const((1, IN_WIDTH))],
        out_shape=[_sds((IN_WIDTH, D_MODEL), F32), _sds((1, IN_WIDTH), F32)],
        args=(h, *parts), sem=("arbitrary",), exchanges=exchanges)


def _row_tile(rows, cap=256, mult=16):
    best = None
    for t in range(mult, min(rows, cap) + 1, mult):
        if rows % t == 0:
            best = t
    if best is None:
        raise ValueError("no row tile for %d rows" % rows)
    return best


def _pair_sum(ids, full, got):
    _, r, c = full.shape
    hr = r // 2
    tr = _row_tile(hr)
    nblk = hr // tr

    def body(ids_ref, a_ref, b_ref, own_ref, sb_ref):
        s = a_ref[...] + b_ref[...]
        sb_ref[...] = s.astype(BF16)

        @pl.when(pl.program_id(1) == ids_ref[0])
        def _():
            own_ref[...] = s

    slab = pl.BlockSpec((None, tr, c), lambda i, j, ids_ref: (j, i, 0))
    return pl.pallas_call(
        body, name="pair_sum_%dx%d" % (r, c),
        grid_spec=pltpu.PrefetchScalarGridSpec(
            num_scalar_prefetch=1, grid=(nblk, N_CHIPS),
            in_specs=[pl.BlockSpec((None, tr, c), lambda i, j, ids_ref: (j, ids_ref[1] * nblk + i, 0)), slab],
            out_specs=[pl.BlockSpec((tr, c), lambda i, j, ids_ref: (i, 0)), slab]),
        out_shape=[_sds((hr, c), F32), _sds((N_CHIPS, hr, c), BF16)],
        compiler_params=_cp("parallel", "arbitrary"),
    )(ids, full, got)


def _pair_sum_small(ids, fulls, gots):
    n = len(fulls)
    dims = [(f.shape[1] // 2, f.shape[2]) for f in fulls]

    def body(ids_ref, *refs):
        ins, outs = refs[:2 * n], refs[2 * n:]
        for k in range(n):
            s = ins[2 * k][...] + ins[2 * k + 1][...]
            outs[2 * k + 1][...] = s.astype(BF16)

            @pl.when(pl.program_id(0) == ids_ref[0])
            def _(k=k, s=s):
                outs[2 * k][...] = s

    in_specs, out_specs, out_shape = [], [], []
    for hr, c in dims:
        slab = pl.BlockSpec((None, hr, c), lambda j, ids_ref: (j, 0, 0))
        in_specs += [pl.BlockSpec((None, hr, c), lambda j, ids_ref: (j, ids_ref[1], 0)), slab]
        out_specs += [pl.BlockSpec((hr, c), lambda j, ids_ref: (0, 0)), slab]
        out_shape += [_sds((hr, c), F32), _sds((N_CHIPS, hr, c), BF16)]
    res = pl.pallas_call(
        body, name="pair_sum_small",
        grid_spec=pltpu.PrefetchScalarGridSpec(num_scalar_prefetch=1, grid=(N_CHIPS,), in_specs=in_specs,
                                               out_specs=out_specs),
        out_shape=out_shape, compiler_params=_cp("arbitrary"),
    )(ids, *[a for pair in zip(fulls, gots) for a in pair])
    return [(res[2 * k], res[2 * k + 1]) for k in range(n)]


def _chip_sum_small(ids, owns, gots):
    n = len(owns)

    def body(ids_ref, *refs):
        ins, outs = refs[:2 * n], refs[2 * n:]
        for k in range(n):
            a, b = ins[2 * k], ins[2 * k + 1]
            outs[k][...] = ((a[...] + b[0].astype(F32)) + b[1].astype(F32)) + b[2].astype(F32)

    in_specs, out_specs, out_shape = [], [], []
    for own in owns:
        hr, c = own.shape
        in_specs += [pl.BlockSpec((hr, c), lambda i, ids_ref: (0, 0)),
                     pl.BlockSpec((3, hr, c), lambda i, ids_ref: (0, 0, 0))]
        out_specs.append(pl.BlockSpec((hr, c), lambda i, ids_ref: (ids_ref[1], 0)))
        out_shape.append(_sds((2 * hr, c), F32))
    return pl.pallas_call(
        body, name="chip_sum_small",
        grid_spec=pltpu.PrefetchScalarGridSpec(num_scalar_prefetch=1, grid=(1,), in_specs=in_specs,
                                               out_specs=out_specs),
        out_shape=out_shape, compiler_params=_cp("arbitrary"),
    )(ids, *[a for pair in zip(owns, gots) for a in pair])


def _chip_sum(ids, own, got):
    hr, c = own.shape
    tr = _row_tile(hr)
    nblk = hr // tr

    def body(ids_ref, a_ref, b_ref, o_ref):
        o_ref[...] = ((a_ref[...] + b_ref[0].astype(F32)) + b_ref[1].astype(F32)) + b_ref[2].astype(F32)

    return pl.pallas_call(
        body, name="chip_sum_%dx%d" % (hr, c),
        grid_spec=pltpu.PrefetchScalarGridSpec(
            num_scalar_prefetch=1, grid=(nblk,),
            in_specs=[pl.BlockSpec((tr, c), lambda i, ids_ref: (i, 0)),
                      pl.BlockSpec((3, tr, c), lambda i, ids_ref: (0, i, 0))],
            out_specs=pl.BlockSpec((tr, c), lambda i, ids_ref: (ids_ref[1] * nblk + i, 0))),
        out_shape=_sds((2 * hr, c), F32),
        compiler_params=_cp("parallel"),
    )(ids, own, got)


def _adamw_math(w, g, m, v):
    nm = ADAM_B1 * m + (1.0 - ADAM_B1) * g
    nv = ADAM_B2 * v + (1.0 - ADAM_B2) * (g * g)
    m_hat = nm / (1.0 - ADAM_B1 ** ADAM_STEP)
    v_hat = nv / (1.0 - ADAM_B2 ** ADAM_STEP)
    return -ADAM_LR * (m_hat / (jnp.sqrt(v_hat) + ADAM_EPS) + ADAM_WD * w), nm, nv


def _adamw(w, g, m, v):
    r, c = w.shape
    tr = _row_tile(r, cap=512, mult=8)

    def body(w_ref, g_ref, m_ref, v_ref, d_ref, nm_ref, nv_ref):
        d_ref[...], nm_ref[...], nv_ref[...] = _adamw_math(w_ref[...], g_ref[...], m_ref[...], v_ref[...])

    spec = _rows(tr, c)
    return pl.pallas_call(
        body, name="adamw_%dx%d" % (r, c), grid=(r // tr,),
        in_specs=[spec] * 4, out_specs=[spec] * 3, out_shape=[_sds((r, c), F32)] * 3,
        compiler_params=_cp("parallel"),
    )(w, g, m, v)


SC_TILES = 32
SC_LANES = 16
SC_ROWS = 8


def _adamw_sparse(w, g, m, v):
    r, c = w.shape
    rows = r // SC_TILES
    step = min(rows, SC_ROWS)

    def body(w_hbm, g_hbm, m_hbm, v_hbm, d_hbm, nm_hbm, nv_hbm, wb, gb, mb, vb):
        tile = lax.axis_index("sc_subcore") * 2 + lax.axis_index("sc_core")

        @pl.loop(0, rows, step=step)
        def _(r0):
            mine = pl.ds(tile * rows + r0, step)
            for src, dst in ((w_hbm, wb), (g_hbm, gb), (m_hbm, mb), (v_hbm, vb)):
                pltpu.sync_copy(src.at[mine], dst)

            @pl.loop(0, step)
            def _(row):
                @pl.loop(0, c, step=SC_LANES)
                def _(i):
                    at = (row, pl.ds(i, SC_LANES))
                    wb[at], mb[at], vb[at] = _adamw_math(wb[at], gb[at], mb[at], vb[at])

            for src, dst in ((wb, d_hbm), (mb, nm_hbm), (vb, nv_hbm)):
                pltpu.sync_copy(src, dst.at[mine])

    return pl.kernel(
        body, name="adamw_sparse_%dx%d" % (r, c), out_type=[_sds((r, c), F32)] * 3,
        mesh=plsc.VectorSubcoreMesh(core_axis_name="sc_core", subcore_axis_name="sc_subcore"),
        scratch_types=[pltpu.VMEM((step, c), F32)] * 4,
    )(w, g, m, v)


_SMALL_NAMES = ("w_pool", "b_in", "g_mix_pre", "g_mix_post", "g_mlp_pre", "g_mlp_post", "pool_scale", "attn_sinks")
B_ROWS = -(-IN_WIDTH // D_MODEL)


def _row_block(rows):
    rows = [jnp.pad(r.astype(F32), ((0, 0), (0, D_MODEL - r.shape[1]))) for r in rows]
    return jnp.pad(jnp.concatenate(rows, axis=0), ((0, 8 - len(rows)), (0, 0)))


def _early_block(dg2, dg3, dg4, dps, dsink, loss):
    tail = jnp.concatenate([jnp.pad(dsink.reshape(1, -1), ((0, 0), (0, LANES - dsink.size))),
                            jnp.pad(loss.reshape(1, 1), ((0, 0), (0, LANES - 1)))], axis=1)
    return _row_block([dg2, dg3, dg4, dps, tail])


def _late_block(db_in, dg1):
    b = jnp.pad(db_in, ((0, 0), (0, B_ROWS * D_MODEL - IN_WIDTH))).reshape(B_ROWS, D_MODEL)
    return _row_block([b[r:r + 1] for r in range(B_ROWS)] + [dg1])


def _small_update(gearly, gmat, glate, w, m, v):
    names = _SMALL_NAMES
    n = len(names)

    def total(ref, rows):
        acc = ref[0:rows, :]
        for d in range(1, N_DEV):
            acc = acc + ref[d * rows:(d + 1) * rows, :]
        return acc

    def body(*refs):
        early_ref, gmat_ref, late_ref = refs[:3]
        w_refs, m_refs, v_refs = refs[3:3 + n], refs[3 + n:3 + 2 * n], refs[3 + 2 * n:3 + 3 * n]
        outs = refs[3 + 3 * n:]
        loss_ref, g_refs, d_refs = outs[0], outs[1:1 + n], outs[1 + n:1 + 2 * n]
        nm_refs, nv_refs = outs[1 + 2 * n:1 + 3 * n], outs[1 + 3 * n:1 + 4 * n]
        early, late = total(early_ref, 8), total(late_ref, 8)
        loss_ref[...] = jnp.sum(early[4:5, LANES:2 * LANES], axis=1, keepdims=True)
        bias = jnp.concatenate([late[r:r + 1, :] for r in range(B_ROWS - 1)]
                               + [late[B_ROWS - 1:B_ROWS, :IN_WIDTH - (B_ROWS - 1) * D_MODEL]], axis=1)
        grad = dict(b_in=bias, g_mix_pre=late[B_ROWS:B_ROWS + 1, :], g_mix_post=early[0:1, :],
                    g_mlp_pre=early[1:2, :], g_mlp_post=early[2:3, :], pool_scale=early[3:4, :POOL_WIDTH],
                    attn_sinks=early[4:5, :N_Q_HEADS])
        for i, name in enumerate(names):
            g = total(gmat_ref, 4 * POOL_GC) if name == "w_pool" else grad[name]
            g_refs[i][...] = g
            d_refs[i][...], nm_refs[i][...], nv_refs[i][...] = _adamw_math(
                w_refs[i][...], g, m_refs[i][...], v_refs[i][...])

    shapes = [_sds(w[k].shape, F32) for k in names]
    res = pl.pallas_call(
        body, name="small_update", out_shape=[_sds((1, 1), F32)] + shapes * 4,
        compiler_params=pltpu.CompilerParams(vmem_limit_bytes=VMEM_MB * 1024 * 1024),
    )(gearly, gmat, glate, *[w[k] for k in names], *[m[k] for k in names], *[v[k] for k in names])
    loss = res[0]
    per = {k: tuple(res[1 + j * n + i] for j in range(4)) for i, k in enumerate(names)}
    return loss, per


_BIG = ("w_in", "w_branch_pool", "w_branch_attn", "w_out", "w_up", "w_down")
_ORDER = ("g_mix_pre", "w_in", "b_in", "w_pool", "pool_scale", "attn_sinks", "w_branch_pool", "w_branch_attn",
          "w_out", "g_mix_post", "g_mlp_pre", "w_up", "w_down", "g_mlp_post")


def _stack_rows(slab):
    return slab.reshape(-1, slab.shape[2])


def _step(x2, tgt, seq, shards, small, ids):
    tabs = _rope_tables(seq)
    g1, g2, g3, g4 = (small[n] for n in ("g_mix_pre", "g_mix_post", "g_mlp_pre", "g_mlp_post"))
    sinks = small["attn_sinks"].reshape(N_Q_HEADS)
    w_pool = small["w_pool"].reshape(4, POOL_GC, POOL_GC)
    pool_scale = small["pool_scale"]

    def whole(shard, slabs):
        return lax.dynamic_update_slice(slabs, shard[None], (ids[0], 0, 0))

    (up_a, up_b, down_a, down_b, *mix_shards), [[in_slab]] = _cast_shards(
        *(shards[n] for n in ("w_up", "w_down", "w_branch_pool", "w_branch_attn", "w_out")),
        exchanges=[_ex_gather([shards["w_in"]])])
    w_in = _stack_rows(whole(shards["w_in"], in_slab))
    (h, u, q, k, v, gate), [mix_slabs] = _inproj(
        x2, g1, w_in, small["b_in"], tabs, seq, exchanges=[_ex_gather(mix_shards)])
    w_bp, w_ba, out_slab = (whole(s, g) for s, g in zip(mix_shards, mix_slabs))
    w_out = _stack_rows(out_slab)
    (y_attn, diff, y_pool), [[got_a, got_b]] = _mixers_fwd(
        q, k, v, sinks, u, w_pool, pool_scale, seq, exchanges=[_ex_gather([up_a, up_b])])
    (merged, mix, x1, h2), [[got_c, got_d]] = _merge_out(
        y_pool, y_attn, gate, x2, w_bp, w_ba, w_out, g2, g3, exchanges=[_ex_gather([down_a, down_b])])
    w_up = (whole(up_a, got_a), whole(up_b, got_b))
    w_down = (whole(down_a, got_c), whole(down_b, got_d))
    act, dff, dup, dx1, dmix, loss_acc, dg4, dg3, dg2 = _mlp_core(h2, x1, mix, tgt, w_up, w_down, g4, g3, g2)

    dw_down = _dw("down", act, dff, 1024, 1024)[0].reshape(N_CHIPS, D_FF // N_CHIPS, D_MODEL)
    (dbp, dba, dgate, dyp, dya), [[got]] = _merge_bwd(
        dmix, gate, y_pool, y_attn, w_out, w_bp, w_ba, exchanges=[_ex_pair([dw_down])])
    ps_down = _pair_sum(ids, dw_down, got)
    (dw_up,), [[got]] = _dw("up", h2, dup, 1024, 1024, shard_cols=True, exchanges=[_ex_chip([ps_down[1]])])
    half_down = _chip_sum(ids, ps_down[0], got)
    (dw_out, dw_bp, dw_ba), [[got]] = _dw_mix(merged, dmix, y_pool, dbp, y_attn, dba, exchanges=[_ex_pair([dw_up])])
    ps_up = _pair_sum(ids, dw_up, got)
    dw_mix = [dw_out.reshape(N_CHIPS, D_MODEL // N_CHIPS, D_MODEL), dw_bp, dw_ba]
    (dq, dk, dv, dsink, du, dw_pool, dps), [[got], gots, [g_down]] = _mixers_bwd(
        q, k, v, dya, sinks, tabs, dyp, diff, w_pool, pool_scale, seq,
        exchanges=[_ex_chip([ps_up[1]]), _ex_pair(dw_mix), _ex_swap([half_down])])
    half_up = _chip_sum(ids, ps_up[0], got)
    ps_mix = _pair_sum_small(ids, dw_mix, gots)
    parts = (du, dq, dk, dv, dgate)
    early = _early_block(dg2, dg3, dg4, dps, dsink[:, 0], loss_acc[0, 0])
    mat = dw_pool.reshape(4 * POOL_GC, POOL_GC)
    (dw_in_t, db_in), [gots, [gearly, gmat], [g_up]] = _dw_in(
        h, parts, exchanges=[_ex_chip([p[1] for p in ps_mix]), _ex_allgather([early, mat]), _ex_swap([half_up])])
    half_mix = _chip_sum_small(ids, [p[0] for p in ps_mix], gots)
    dw_in = dw_in_t.reshape(N_CHIPS, IN_WIDTH // N_CHIPS, D_MODEL)
    g_mix, [got] = _alone("swap_mix_pair_in", _ex_swap(half_mix), _ex_pair([dw_in]))
    ps_in = _pair_sum(ids, dw_in, got)
    (gx, dg1), [[got]] = _inproj_bwd(parts, x2, dx1, w_in, g1, exchanges=[_ex_chip([ps_in[1]])])
    [g_in], [glate] = _alone("swap_in_allgather", _ex_swap([_chip_sum(ids, ps_in[0], got)]),
                             _ex_allgather([_late_block(db_in, dg1)]))

    grads = dict(w_in=g_in, w_branch_pool=g_mix[1], w_branch_attn=g_mix[2], w_out=g_mix[0], w_up=g_up, w_down=g_down)
    return (gearly, gmat, glate), gx, grads


def kernel(x, g_mix_pre, w_in, b_in, w_pool, pool_scale, attn_sinks, w_branch_pool, w_branch_attn, w_out, g_mix_post, g_mlp_pre, w_up, w_down, g_mlp_post, loss_target, m_g_mix_pre, m_w_in, m_b_in, m_w_pool, m_pool_scale, m_attn_sinks, m_w_branch_pool, m_w_branch_attn, m_w_out, m_g_mix_post, m_g_mlp_pre, m_w_up, m_w_down, m_g_mlp_post, v_g_mix_pre, v_w_in, v_b_in, v_w_pool, v_pool_scale, v_attn_sinks, v_w_branch_pool, v_w_branch_attn, v_w_out, v_g_mix_post, v_g_mlp_pre, v_w_up, v_w_down, v_g_mlp_post):
    weights = dict(g_mix_pre=g_mix_pre, w_in=w_in, b_in=b_in, w_pool=w_pool, pool_scale=pool_scale,
                   attn_sinks=attn_sinks, w_branch_pool=w_branch_pool, w_branch_attn=w_branch_attn, w_out=w_out,
                   g_mix_post=g_mix_post, g_mlp_pre=g_mlp_pre, w_up=w_up, w_down=w_down, g_mlp_post=g_mlp_post)
    mom1 = dict(g_mix_pre=m_g_mix_pre, w_in=m_w_in, b_in=m_b_in, w_pool=m_w_pool, pool_scale=m_pool_scale,
                attn_sinks=m_attn_sinks, w_branch_pool=m_w_branch_pool, w_branch_attn=m_w_branch_attn,
                w_out=m_w_out, g_mix_post=m_g_mix_post, g_mlp_pre=m_g_mlp_pre, w_up=m_w_up, w_down=m_w_down,
                g_mlp_post=m_g_mlp_post)
    mom2 = dict(g_mix_pre=v_g_mix_pre, w_in=v_w_in, b_in=v_b_in, w_pool=v_w_pool, pool_scale=v_pool_scale,
                attn_sinks=v_attn_sinks, w_branch_pool=v_w_branch_pool, w_branch_attn=v_w_branch_attn,
                w_out=v_w_out, g_mix_post=v_g_mix_post, g_mlp_pre=v_g_mlp_pre, w_up=v_w_up, w_down=v_w_down,
                g_mlp_post=v_g_mlp_post)
    b_loc, seq, _ = x.shape
    x2 = x.reshape(b_loc * seq, D_MODEL)
    tgt = loss_target.reshape(b_loc * seq, D_MODEL)
    ids = jnp.stack([2 * lax.axis_index("x") + lax.axis_index("y"), lax.axis_index("c")]).astype(jnp.int32)

    def flat(n, a):
        return a[0].T if n == "w_in" else a[0]

    def unflat(n, a):
        return (a.T if n == "w_in" else a)[None]

    shards = {n: flat(n, weights[n]).astype(BF16) if n == "w_in" else flat(n, weights[n]) for n in _BIG}
    small = {n: weights[n] for n in _ORDER if n not in _BIG}
    (gearly, gmat, glate), gx, grads = _step(x2, tgt, seq, shards, small, ids)

    def two_d(src):
        return {n: src[n].reshape(4 * POOL_GC, POOL_GC) if n == "w_pool" else src[n] for n in _SMALL_NAMES}

    loss, per = _small_update(gearly, gmat, glate, two_d(weights), two_d(mom1), two_d(mom2))
    delta, new_m, new_v = {}, {}, {}
    for n in _SMALL_NAMES:
        grads[n], delta[n], new_m[n], new_v[n] = (a.reshape(weights[n].shape) for a in per[n])
    for n in _BIG:
        update = _adamw if n == "w_in" else _adamw_sparse
        d, nm, nv = update(flat(n, weights[n]), grads[n], flat(n, mom1[n]), flat(n, mom2[n]))
        grads[n] = unflat(n, grads[n])
        delta[n], new_m[n], new_v[n] = unflat(n, d), unflat(n, nm), unflat(n, nv)

    return (loss[0, 0], gx.reshape(x.shape), *[grads[n] for n in _ORDER], *[delta[n] for n in _ORDER],
            *[new_m[n] for n in _ORDER], *[new_v[n] for n in _ORDER])
```

```python
import jax
import jax.numpy as jnp
from jax import lax
from jax.experimental import pallas as pl
from jax.experimental.pallas import tpu as pltpu
from jax.experimental.pallas import tpu_sc as plsc

F32 = jnp.float32
BF16 = jnp.bfloat16

D_MODEL = 1024
POOL_WINDOWS = (2, 4, 8, 16)
POOL_WIDTH = 512
POOL_GC = 128
HALO = 16
HEAD_DIM = 64
N_Q_HEADS = 8
ATTN_WIDTH = 512
KV_WIDTH = 128
BLOCK = 128
NEG_INF = -1e30
ROPE_THETA = 500000.0
ROT_DIM = 16
GATE_WIDTH = 2048
IN_WIDTH = 3328
D_FF = 4096
EPS = 1e-6
SCALE = HEAD_DIM ** -0.5
C_Q, C_K, C_V, C_G = 512, 1024, 1152, 1280

ADAM_LR, ADAM_B1, ADAM_B2, ADAM_EPS, ADAM_WD, ADAM_STEP = 0.001, 0.9, 0.999, 1e-08, 0.01, 10

N_CHIPS = 4
N_DEV = 8
LANES = 128
TM = 512
VMEM_MB = 56

MESH = pl.DeviceIdType.MESH
ANY = pl.BlockSpec(memory_space=pl.ANY)


def _cp(*sem, vmem=VMEM_MB):
    return pltpu.CompilerParams(dimension_semantics=sem, vmem_limit_bytes=vmem * 1024 * 1024)


def _rows(tile, cols):
    return pl.BlockSpec((tile, cols), lambda i: (i, 0))


def _const(shape):
    nd = len(shape)
    return pl.BlockSpec(shape, lambda i: (0,) * nd)


def _sds(shape, dtype):
    return jax.ShapeDtypeStruct(shape, dtype)


def _dot(a, b):
    return jnp.dot(a, b, preferred_element_type=F32)


def _dot_nt(a, b):
    return lax.dot_general(a, b, (((1,), (1,)), ((), ())), preferred_element_type=F32)


def _dot_tn(a, b):
    return lax.dot_general(a, b, (((0,), (0,)), ((), ())), preferred_element_type=F32)


def _rms(x):
    return lax.rsqrt(jnp.mean(x * x, axis=-1, keepdims=True) + EPS)


def _norm_bwd(x, g, dout):
    r = _rms(x)
    n = x * r
    dn = dout * g
    dx = r * (dn - n * jnp.mean(dn * n, axis=-1, keepdims=True))
    return dx, jnp.sum(dout * n, axis=0, keepdims=True)


def _rot_fwd(t, c, a, bt):
    return t * c + pltpu.roll(t, LANES - 8, 1) * a + pltpu.roll(t, 8, 1) * bt


def _rot_bwd(d, c, a, bt):
    return d * c + pltpu.roll(d * a, 8, 1) + pltpu.roll(d * bt, LANES - 8, 1)


def _rope_tables(seq):
    pos = jnp.arange(seq, dtype=F32)
    inv_freq = ROPE_THETA ** (-jnp.arange(0, ROT_DIM, 2, dtype=F32) / ROT_DIM)
    ang = pos[:, None] * inv_freq[None, :]
    cos, sin = jnp.cos(ang), jnp.sin(ang)
    ones = jnp.ones((seq, HEAD_DIM - ROT_DIM), F32)
    zeros8 = jnp.zeros((seq, 8), F32)
    zrest = jnp.zeros((seq, HEAD_DIM - ROT_DIM), F32)
    c = jnp.concatenate([cos, cos, ones], axis=1)
    a = jnp.concatenate([-sin, zeros8, zrest], axis=1)
    bt = jnp.concatenate([zeros8, sin, zrest], axis=1)
    return tuple(jnp.tile(t, (1, 2)) for t in (c, a, bt))


class _Exchange:
    def __init__(self, inputs, out_shapes, sems, start, finish, aliases=None, middle=None):
        self.inputs, self.out_shapes, self.sems = list(inputs), list(out_shapes), list(sems)
        self.start, self.finish, self.aliases = start, finish, dict(aliases or {})
        self.middle = middle


def _call(body, *, name, grid, in_specs, out_specs, out_shape, args, scratch=(), sem=(), exchanges=()):
    in_specs, out_specs, out_shape, scratch = list(in_specs), list(out_specs), list(out_shape), list(scratch)
    if not exchanges:
        return pl.pallas_call(body, name=name, grid=grid, in_specs=in_specs, out_specs=out_specs,
                              out_shape=out_shape, scratch_shapes=scratch, compiler_params=_cp(*sem))(*args)
    n_in, n_out, n_scr = len(in_specs), len(out_specs), len(scratch)
    x_in = [a for ex in exchanges for a in ex.inputs]
    x_out = [s for ex in exchanges for s in ex.out_shapes]
    x_sem = [s for ex in exchanges for s in ex.sems]
    aliases, i_off, o_off = {}, n_in, n_out
    for ex in exchanges:
        for i, o in ex.aliases.items():
            aliases[i_off + i] = o_off + o
        i_off += len(ex.inputs)
        o_off += len(ex.out_shapes)

    def split(flat):
        out, pos = [], 0
        for ex, n in zip(exchanges, flat[1]):
            out.append(flat[0][pos:pos + n])
            pos += n
        return out

    def carrier(*refs):
        pos = 0
        groups = []
        for n in (n_in, len(x_in), n_out, len(x_out), n_scr, len(x_sem)):
            groups.append(refs[pos:pos + n])
            pos += n
        ins, xin, outs, xout, scr, xsem = groups
        xin = split((xin, [len(ex.inputs) for ex in exchanges]))
        xout = split((xout, [len(ex.out_shapes) for ex in exchanges]))
        xsem = split((xsem, [len(ex.sems) for ex in exchanges]))
        first = pl.program_id(0) == 0
        last = pl.program_id(0) == grid[0] - 1
        for d in range(1, len(grid)):
            first = jnp.logical_and(first, pl.program_id(d) == 0)
            last = jnp.logical_and(last, pl.program_id(d) == grid[d] - 1)

        @pl.when(first)
        def _():
            for ex, i, o, s in zip(exchanges, xin, xout, xsem):
                ex.start(i, o, s)

        if any(ex.middle for ex in exchanges):
            half = pl.program_id(0) == 5 * grid[0] // 8
            for d in range(1, len(grid)):
                half = jnp.logical_and(half, pl.program_id(d) == 0)

            @pl.when(half)
            def _():
                for ex, i, o, s in zip(exchanges, xin, xout, xsem):
                    if ex.middle:
                        ex.middle(i, o, s)

        body(*ins, *outs, *scr)

        @pl.when(last)
        def _():
            for ex, i, o, s in zip(exchanges, xin, xout, xsem):
                ex.finish(i, o, s)

    res = pl.pallas_call(
        carrier, name=name, grid=grid, in_specs=in_specs + [ANY] * len(x_in),
        out_specs=out_specs + [ANY] * len(x_out), out_shape=out_shape + x_out,
        scratch_shapes=scratch + x_sem, input_output_aliases=aliases,
        compiler_params=_cp(*(["arbitrary"] * len(grid))),
    )(*args, *x_in)
    return res[:n_out], split((res[n_out:], [len(ex.out_shapes) for ex in exchanges]))


def _alone(name, *exchanges):
    n_in = [len(ex.inputs) for ex in exchanges]
    n_out = [len(ex.out_shapes) for ex in exchanges]
    n_sem = [len(ex.sems) for ex in exchanges]
    aliases, i_off, o_off = {}, 0, 0
    for ex in exchanges:
        for i, o in ex.aliases.items():
            aliases[i_off + i] = o_off + o
        i_off += len(ex.inputs)
        o_off += len(ex.out_shapes)

    def split(flat, counts):
        out, pos = [], 0
        for n in counts:
            out.append(flat[pos:pos + n])
            pos += n
        return out

    def body(*refs):
        ins, outs, sems = split(refs, [sum(n_in), sum(n_out), sum(n_sem)])
        groups = list(zip(exchanges, split(ins, n_in), split(outs, n_out), split(sems, n_sem)))
        for ex, i, o, s in groups:
            ex.start(i, o, s)
        for ex, i, o, s in groups:
            if ex.middle:
                ex.middle(i, o, s)
        for ex, i, o, s in groups:
            ex.finish(i, o, s)

    res = pl.pallas_call(
        body, name=name, in_specs=[ANY] * sum(n_in), out_specs=[ANY] * sum(n_out),
        out_shape=[s for ex in exchanges for s in ex.out_shapes],
        scratch_shapes=[s for ex in exchanges for s in ex.sems], input_output_aliases=aliases,
    )(*[a for ex in exchanges for a in ex.inputs])
    return split(res, n_out)


def _place():
    x, y, c = lax.axis_index("x"), lax.axis_index("y"), lax.axis_index("c")
    chips = [(1 - x, y), (x, 1 - y), (1 - x, 1 - y)]
    return x, y, c, chips


def _remote(src, dst, send, recv, to):
    return pltpu.make_async_remote_copy(src_ref=src, dst_ref=dst, send_sem=send, recv_sem=recv,
                                        device_id=to, device_id_type=MESH)


def _ex_gather(shards):
    nw = len(shards)
    hrs = [s.shape[0] // 2 for s in shards]

    def copies(ins, outs, sems):
        s1, r1, s2, r2, fs, fr = sems
        x, y, c, _ = _place()
        me, xn, yn, dg = (x, y), (1 - x, y), (x, 1 - y), (1 - x, 1 - y)
        nbr = (xn, yn)
        sibling = (x, y, 1 - c)

        def piece(w, chip, core, part=None):
            hr = hrs[w]
            rows = pl.ds(core * hr, hr) if part is None else pl.ds(core * hr + part * (hr // 2), hr // 2)
            return outs[w].at[2 * chip[0] + chip[1], rows]

        def first(w, k):
            return _remote(ins[w].at[pl.ds(c * hrs[w], hrs[w])], piece(w, me, c), s1.at[w, k], r1.at[w, k],
                           (*nbr[k], c))

        def landed(w, k):
            return _remote(piece(w, nbr[k], c), piece(w, nbr[k], c), s1.at[w, k], r1.at[w, k], (*nbr[k], c))

        def onward(w, k):
            return _remote(piece(w, nbr[k], c, k), piece(w, nbr[k], c, k), s2.at[w, k], r2.at[w, k],
                           (*nbr[1 - k], c))

        def arrived(w, k):
            return _remote(piece(w, dg, c, k), piece(w, dg, c, k), s2.at[w, k], r2.at[w, k], (*nbr[1 - k], c))

        def passed(w, j):
            chip = (xn, yn, dg)[j]
            return _remote(piece(w, chip, c), piece(w, chip, c), fs.at[w, j], fr.at[w, j], sibling)

        def handed(w, j):
            chip = (xn, yn, dg)[j]
            return _remote(piece(w, chip, 1 - c), piece(w, chip, 1 - c), fs.at[w, j], fr.at[w, j], sibling)

        return first, landed, onward, arrived, passed, handed

    def start(ins, outs, sems):
        first = copies(ins, outs, sems)[0]
        for w in range(nw):
            for k in range(2):
                first(w, k).start()

    def middle(ins, outs, sems):
        _, landed, onward, _, passed, _ = copies(ins, outs, sems)
        for w in range(nw):
            for k in range(2):
                landed(w, k).wait_recv()
                onward(w, k).start()
                passed(w, k).start()

    def finish(ins, outs, sems):
        first, _, onward, arrived, passed, handed = copies(ins, outs, sems)
        for w in range(nw):
            for k in range(2):
                arrived(w, k).wait_recv()
            passed(w, 2).start()
        for w in range(nw):
            for j in range(3):
                handed(w, j).wait_recv()
        for w in range(nw):
            for k in range(2):
                first(w, k).wait_send()
                onward(w, k).wait_send()
            for j in range(3):
                passed(w, j).wait_send()

    return _Exchange(shards, [_sds((N_CHIPS,) + s.shape, s.dtype) for s in shards],
                     [pltpu.SemaphoreType.DMA((nw, 2))] * 4 + [pltpu.SemaphoreType.DMA((nw, 3))] * 2,
                     start, finish, middle=middle)


def _ex_pair(grads):
    nw = len(grads)

    def copies(ins, outs, sems):
        x, y, c, _ = _place()
        out = []
        for w in range(nw):
            hr = grads[w].shape[1] // 2
            out.append(_remote(ins[w].at[:, pl.ds((1 - c) * hr, hr)], outs[w], sems[0].at[w], sems[1].at[w],
                               (x, y, 1 - c)))
        return out

    def start(ins, outs, sems):
        for cp in copies(ins, outs, sems):
            cp.start()

    def finish(ins, outs, sems):
        for cp in copies(ins, outs, sems):
            cp.wait()

    return _Exchange(grads, [_sds((N_CHIPS, g.shape[1] // 2, g.shape[2]), F32) for g in grads],
                     [pltpu.SemaphoreType.DMA((nw,))] * 2, start, finish)


def _ex_chip(pieces):
    nw = len(pieces)

    def copies(ins, outs, sems):
        x, y, c, chips = _place()
        return [_remote(ins[w].at[2 * cx + cy], outs[w].at[k], sems[0].at[w, k], sems[1].at[w, k], (cx, cy, c))
                for w in range(nw) for k, (cx, cy) in enumerate(chips)]

    def start(ins, outs, sems):
        for cp in copies(ins, outs, sems):
            cp.start()

    def finish(ins, outs, sems):
        for cp in copies(ins, outs, sems):
            cp.wait()

    return _Exchange(pieces, [_sds((3,) + p.shape[1:], BF16) for p in pieces],
                     [pltpu.SemaphoreType.DMA((nw, 3))] * 2, start, finish)


def _ex_swap(fulls):
    nw = len(fulls)

    def start(ins, outs, sems):
        x, y, c, _ = _place()
        for w in range(nw):
            hr = fulls[w].shape[0] // 2
            mine = pl.ds(c * hr, hr)
            _remote(ins[w].at[mine], outs[w].at[mine], sems[0].at[w], sems[1].at[w], (x, y, 1 - c)).start()

    def finish(ins, outs, sems):
        x, y, c, _ = _place()
        for w in range(nw):
            hr = fulls[w].shape[0] // 2
            mine, theirs = pl.ds(c * hr, hr), pl.ds((1 - c) * hr, hr)
            _remote(ins[w].at[mine], outs[w].at[mine], sems[0].at[w], sems[1].at[w], (x, y, 1 - c)).wait_send()
            _remote(ins[w].at[theirs], outs[w].at[theirs], sems[0].at[w], sems[1].at[w], (x, y, 1 - c)).wait_recv()

    return _Exchange(fulls, [_sds(f.shape, F32) for f in fulls], [pltpu.SemaphoreType.DMA((nw,))] * 2,
                     start, finish, aliases={w: w for w in range(nw)})


def _ex_allgather(blocks):
    nb = len(blocks)

    def copies(ins, outs, sems):
        send, recv, lsem = sems
        x, y, c, chips = _place()
        me, sibling = (x, y, c), (x, y, 1 - c)

        def rows(b, px, py, pc):
            m_per = blocks[b].shape[0]
            return outs[b].at[pl.ds((4 * px + 2 * py + pc) * m_per, m_per), :]

        def copy(b, k, blk, to, src=None):
            return _remote(rows(b, *blk) if src is None else src, rows(b, *blk), send.at[b, k], recv.at[b, k], to)

        def mine(b):
            return pltpu.make_async_copy(ins[b], rows(b, *me), lsem.at[b])

        def first(b, k):
            return copy(b, k, me, sibling if k == 0 else (*chips[k - 1], c), src=ins[b])

        def passed(b, j):
            return copy(b, 4 + j, (*chips[j], c), sibling)

        def landed(b, j):
            return copy(b, 1 + j, (*chips[j], c), me)

        def handed(b, k):
            return copy(b, 0, sibling, me) if k == 0 else copy(b, 3 + k, (*chips[k - 1], 1 - c), me)

        return mine, first, passed, landed, handed

    def start(ins, outs, sems):
        mine, first, _, _, _ = copies(ins, outs, sems)
        for b in range(nb):
            mine(b).start()
            for k in range(4):
                first(b, k).start()

    def middle(ins, outs, sems):
        _, _, passed, landed, _ = copies(ins, outs, sems)
        for b in range(nb):
            for j in range(3):
                landed(b, j).wait_recv()
                passed(b, j).start()

    def finish(ins, outs, sems):
        mine, first, passed, _, handed = copies(ins, outs, sems)
        for b in range(nb):
            for k in range(4):
                handed(b, k).wait_recv()
            for k in range(4):
                first(b, k).wait_send()
            for j in range(3):
                passed(b, j).wait_send()
            mine(b).wait()

    return _Exchange(blocks, [_sds((N_DEV * b.shape[0], b.shape[1]), F32) for b in blocks],
                     [pltpu.SemaphoreType.DMA((nb, 7)), pltpu.SemaphoreType.DMA((nb, 7)), pltpu.SemaphoreType.DMA((nb,))],
                     start, finish, middle=middle)


def _cast_shards(w_up, w_down, w_bp, w_ba, w_out, exchanges=()):
    r, c = w_up.shape
    tr = HALF // 2
    steps = r // tr

    def body(up_ref, down_ref, bp_ref, ba_ref, out_ref, ua_ref, ub_ref, da_ref, db_ref, bpo_ref, bao_ref, outo_ref):
        i = pl.program_id(0)

        @pl.when(i == 0)
        def _():
            for src, dst in ((bp_ref, bpo_ref), (ba_ref, bao_ref), (out_ref, outo_ref)):
                dst[...] = src[...].astype(BF16)

        @pl.when(i < steps // 2)
        def _():
            ua_ref[...] = up_ref[...].astype(BF16)
            da_ref[...] = down_ref[...].astype(BF16)

        @pl.when(i >= steps // 2)
        def _():
            ub_ref[...] = up_ref[...].astype(BF16)
            db_ref[...] = down_ref[...].astype(BF16)

    rows = _rows(tr, c)
    first = pl.BlockSpec((tr, c), lambda i: (jnp.minimum(i, steps // 2 - 1), 0))
    second = pl.BlockSpec((tr, c), lambda i: (jnp.maximum(i - steps // 2, 0), 0))
    half = _sds((HALF, c), BF16)
    return _call(
        body, name="cast_shards", grid=(steps,),
        in_specs=[rows, rows, _const(w_bp.shape), _const(w_ba.shape), _const(w_out.shape)],
        out_specs=[first, second, first, second, _const(w_bp.shape), _const(w_ba.shape), _const(w_out.shape)],
        out_shape=[half, half, half, half, _sds(w_bp.shape, BF16), _sds(w_ba.shape, BF16), _sds(w_out.shape, BF16)],
        args=(w_up, w_down, w_bp, w_ba, w_out), sem=("arbitrary",), exchanges=exchanges)


def _inproj(x2, g1, w_in_t, b_in, tabs, seq, exchanges=()):
    T = x2.shape[0]
    tm = min(TM, seq)
    nseq = seq // tm

    def body(x_ref, g_ref, w_ref, b_ref, c_ref, a_ref, bt_ref, h_ref, u_ref, q_ref, k_ref, v_ref, gate_ref):
        x = x_ref[...]
        h = (x * _rms(x) * g_ref[...]).astype(BF16)
        h_ref[...] = h

        def proj(lo, hi):
            return _dot_nt(h, w_ref[lo:hi, :]) + b_ref[:, lo:hi]

        c, a, bt = c_ref[...], a_ref[...], bt_ref[...]
        u_ref[...] = proj(0, C_Q)
        q = proj(C_Q, C_K)
        for p in range(4):
            sl = slice(LANES * p, LANES * (p + 1))
            q_ref[:, sl] = (_rot_fwd(q[:, sl], c, a, bt) * SCALE).astype(BF16)
        kv = proj(C_K, C_G)
        k_ref[...] = _rot_fwd(kv[:, :KV_WIDTH], c, a, bt).astype(BF16)
        v_ref[...] = kv[:, KV_WIDTH:].astype(BF16)
        for j in range(2):
            lo = C_G + D_MODEL * j
            gate_ref[:, D_MODEL * j:D_MODEL * (j + 1)] = jax.nn.sigmoid(proj(lo, lo + D_MODEL)).astype(BF16)

    tab = pl.BlockSpec((tm, LANES), lambda i: (i % nseq, 0))
    return _call(
        body, name="inproj", grid=(T // tm,),
        in_specs=[_rows(tm, D_MODEL), _const((1, D_MODEL)), _const((IN_WIDTH, D_MODEL)), _const((1, IN_WIDTH)),
                  tab, tab, tab],
        out_specs=[_rows(tm, D_MODEL), _rows(tm, POOL_WIDTH), _rows(tm, ATTN_WIDTH), _rows(tm, KV_WIDTH),
                   _rows(tm, KV_WIDTH), _rows(tm, GATE_WIDTH)],
        out_shape=[_sds((T, D_MODEL), BF16), _sds((T, POOL_WIDTH), F32), _sds((T, ATTN_WIDTH), BF16),
                   _sds((T, KV_WIDTH), BF16), _sds((T, KV_WIDTH), BF16), _sds((T, GATE_WIDTH), BF16)],
        args=(x2, g1, w_in_t, b_in, *tabs), sem=("parallel",), exchanges=exchanges)


def _inv_count(pos, w):
    return 1.0 / jnp.minimum(pos + 1, w).astype(F32)


def _pool_tile(i, tp, nseq, u_ref, prev_ref, w_ref, s_ref, diff_ref, y_ref):
    first = (i % nseq) == 0
    prev = jnp.where(first, 0.0, prev_ref[...])
    ext = jnp.concatenate([prev, u_ref[...]], axis=0)
    pos = (i % nseq) * tp + lax.broadcasted_iota(jnp.int32, (tp, 1), 0)
    for gi, w in enumerate(POOL_WINDOWS):
        sl = slice(POOL_GC * gi, POOL_GC * (gi + 1))
        xg = ext[:, sl]
        s = xg
        sh = 1
        while sh < w:
            s = s + pltpu.roll(s, sh, 0)
            sh *= 2
        pooled = s[HALO:] * _inv_count(pos, w)
        diff = (pooled - xg[HALO:]).astype(BF16)
        diff_ref[:, sl] = diff
        mixed = _dot(diff, w_ref[gi].astype(BF16))
        y_ref[:, sl] = (mixed * s_ref[:, sl]).astype(BF16)


def _pool_specs(tp):
    per = tp // HALO
    return [_rows(tp, POOL_WIDTH), pl.BlockSpec((HALO, POOL_WIDTH), lambda i: (jnp.maximum(i * per - 1, 0), 0)),
            _const((4, POOL_GC, POOL_GC)), _const((1, POOL_WIDTH))]


GROUP = 4
GROWS = GROUP * BLOCK


def _attn_masks(n):
    qi = lax.broadcasted_iota(jnp.int32, (GROWS, 2 * BLOCK), 0) % BLOCK
    kj = lax.broadcasted_iota(jnp.int32, (GROWS, 2 * BLOCK), 1)
    rel = qi + BLOCK - kj
    valid = (rel >= 0) & (rel < BLOCK) & (kj >= jnp.where(n > 0, 0, BLOCK))
    lo = lax.broadcasted_iota(jnp.int32, (BLOCK, LANES), 1) < HEAD_DIM
    return valid, lo


def _by_example(bl, *arrays):
    return [a.reshape(bl, a.shape[0] // bl, a.shape[1]) for a in arrays]


def _stack_heads(ref, h, lo):
    keep = lo if h == 0 else jnp.logical_not(lo)
    pieces = []
    for p in (2 * h, 2 * h + 1):
        xp = ref[:, LANES * p:LANES * (p + 1)].astype(F32)
        for e in range(2):
            t = xp if e == h else pltpu.roll(xp, HEAD_DIM, 1)
            pieces.append(jnp.where(keep, t, 0.0).astype(BF16))
    return jnp.concatenate(pieces, axis=0)


def _unstack_heads(stacked, h, lo):
    pairs = []
    for j in range(2):
        parts = []
        for e in range(2):
            t = stacked[BLOCK * (2 * j + e):BLOCK * (2 * j + e + 1)]
            parts.append(t if e == h else pltpu.roll(t, HEAD_DIM, 1))
        pairs.append(jnp.where(lo, parts[0], parts[1]))
    return pairs


def _sink_rows(sink_ref, h):
    head = lax.broadcasted_iota(jnp.int32, (GROWS, 1), 0) // BLOCK
    col = jnp.zeros((GROWS, 1), F32) + sink_ref[GROUP * h]
    for g in range(1, GROUP):
        col = jnp.where(head == g, sink_ref[GROUP * h + g], col)
    return col


def _group_probs(qs, kk, valid, sink):
    s = jnp.where(valid, _dot_nt(qs, kk), NEG_INF)
    m = jnp.maximum(jnp.max(s, axis=1, keepdims=True), sink)
    ex = jnp.exp(s - m)
    es = jnp.exp(sink - m)
    inv = 1.0 / (jnp.sum(ex, axis=1, keepdims=True) + es)
    return ex * inv, es * inv


def _mixers_fwd(q, k, v, sinks, u, w_pool, pool_scale, seq, exchanges=()):
    T = q.shape[0]
    nb = seq // BLOCK
    bl = T // seq
    tp = T // nb
    nseq = seq // tp

    def body(sink_ref, q_ref, kp_ref, kc_ref, vp_ref, vc_ref, u_ref, prev_ref, w_ref, s_ref, o_ref, diff_ref, y_ref):
        n = pl.program_id(0)
        valid, lo = _attn_masks(n)
        for b in range(bl):
            kk = jnp.concatenate([kp_ref[b], kc_ref[b]], axis=0)
            vv = jnp.concatenate([vp_ref[b], vc_ref[b]], axis=0)
            for h in range(2):
                qs = _stack_heads(q_ref.at[b], h, lo)
                pr, _ = _group_probs(qs, kk, valid, _sink_rows(sink_ref, h))
                o = _dot(pr.astype(BF16), vv)
                for j, pair in enumerate(_unstack_heads(o, h, lo)):
                    p = 2 * h + j
                    o_ref[b, :, LANES * p:LANES * (p + 1)] = pair.astype(BF16)
        _pool_tile(n, tp, nseq, u_ref, prev_ref, w_ref, s_ref, diff_ref, y_ref)

    cur = lambda n: (0, n, 0)
    prv = lambda n: (0, jnp.maximum(n - 1, 0), 0)
    kv = lambda m: pl.BlockSpec((bl, BLOCK, KV_WIDTH), m)
    res = _call(
        body, name="mixers_fwd", grid=(nb,),
        in_specs=[pl.BlockSpec(memory_space=pltpu.SMEM), pl.BlockSpec((bl, BLOCK, ATTN_WIDTH), cur),
                  kv(prv), kv(cur), kv(prv), kv(cur)] + _pool_specs(tp),
        out_specs=[pl.BlockSpec((bl, BLOCK, ATTN_WIDTH), cur), _rows(tp, POOL_WIDTH), _rows(tp, POOL_WIDTH)],
        out_shape=[_sds((bl, seq, ATTN_WIDTH), BF16), _sds((T, POOL_WIDTH), BF16), _sds((T, POOL_WIDTH), BF16)],
        args=(sinks, *_by_example(bl, q, k, k, v, v), u, u, w_pool, pool_scale), sem=("parallel",),
        exchanges=exchanges)
    outs, rest = res if exchanges else (res, None)
    return [outs[0].reshape(T, ATTN_WIDTH), outs[1], outs[2]], rest


def _branch(y, w_ref):
    return jnp.concatenate([_dot(y, w_ref[j]) for j in range(N_CHIPS)], axis=1)


def _merge_out(y_pool, y_attn, gate, x2, w_bp, w_ba, w_out, g2, g3, exchanges=()):
    T = x2.shape[0]
    tm = min(TM, T)

    def body(yp_ref, ya_ref, gate_ref, x_ref, wbp_ref, wba_ref, wo_ref, g2_ref, g3_ref,
             mg_ref, mix_ref, x1_ref, h2_ref):
        bp, ba = _branch(yp_ref[...], wbp_ref), _branch(ya_ref[...], wba_ref)
        merged = (gate_ref[:, :D_MODEL].astype(F32) * bp + gate_ref[:, D_MODEL:].astype(F32) * ba).astype(BF16)
        mg_ref[...] = merged
        mix = _dot(merged, wo_ref[...])
        mix_ref[...] = mix
        x1 = x_ref[...] + mix * _rms(mix) * g2_ref[...]
        x1_ref[...] = x1
        h2_ref[...] = (x1 * _rms(x1) * g3_ref[...]).astype(BF16)

    return _call(
        body, name="merge_out", grid=(T // tm,),
        in_specs=[_rows(tm, POOL_WIDTH), _rows(tm, ATTN_WIDTH), _rows(tm, GATE_WIDTH), _rows(tm, D_MODEL),
                  _const(w_bp.shape), _const(w_ba.shape), _const((D_MODEL, D_MODEL)),
                  _const((1, D_MODEL)), _const((1, D_MODEL))],
        out_specs=[_rows(tm, D_MODEL)] * 4,
        out_shape=[_sds((T, D_MODEL), BF16), _sds((T, D_MODEL), F32), _sds((T, D_MODEL), F32),
                   _sds((T, D_MODEL), BF16)],
        args=(y_pool, y_attn, gate, x2, w_bp, w_ba, w_out, g2, g3), sem=("parallel",), exchanges=exchanges)


HALF = D_MODEL // 2
TM_MLP = 256


def _mlp_core(h2, x1, mix, tgt, w_up, w_down, g4, g3, g2):
    T = h2.shape[0]
    tm = min(TM_MLP, T)

    def body(h_ref, x1_ref, mix_ref, t_ref, g_ref, g3_ref, g2_ref, ua_hbm, ub_hbm, da_hbm, db_hbm,
             act_ref, dff_ref, dup_ref, dx1_ref, dmix_ref, loss_ref, dg_ref, dg3_ref, dg2_ref,
             wu, wd, relu_scr, sems):
        def weight_copy(i, j):
            src, dst = ((ua_hbm, wu.at[j, :HALF]), (ub_hbm, wu.at[j, HALF:]),
                        (da_hbm, wd.at[j, :HALF]), (db_hbm, wd.at[j, HALF:]))[i]
            return pltpu.make_async_copy(src.at[j], dst, sems.at[i, j])

        def arrived(halves, j):
            @pl.when(pl.program_id(0) == 0)
            def _():
                for i in halves:
                    weight_copy(i, j).wait()

        @pl.when(pl.program_id(0) == 0)
        def _():
            for j in range(N_CHIPS):
                for i in range(4):
                    weight_copy(i, j).start()
            loss_ref[...] = jnp.zeros_like(loss_ref)
            for ref in (dg_ref, dg3_ref, dg2_ref):
                ref[...] = jnp.zeros_like(ref)

        h = h_ref[...]
        ff = None
        for j in range(N_CHIPS):
            lo = D_MODEL * j
            arrived((0, 1), j)
            relu = jnp.maximum(_dot(h, wu[j]), 0.0)
            arrived((2, 3), j)
            relu_scr[:, lo:lo + D_MODEL] = relu
            act = jnp.square(relu).astype(BF16)
            act_ref[:, lo:lo + D_MODEL] = act
            t = _dot(act, wd[j])
            ff = t if ff is None else ff + t
        g = g_ref[...]
        x1 = x1_ref[...]
        err = x1 + ff * _rms(ff) * g - t_ref[...]
        loss_ref[...] += jnp.sum(err * err) * (0.5 / D_MODEL)
        dy = err * (1.0 / D_MODEL)
        dff, dg = _norm_bwd(ff, g, dy)
        dg_ref[...] += dg
        dff = dff.astype(BF16)
        dff_ref[...] = dff
        dh2 = None
        for j in range(N_CHIPS):
            lo = D_MODEL * j
            dup = (_dot_nt(dff, wd[j]) * (2.0 * relu_scr[:, lo:lo + D_MODEL])).astype(BF16)
            dup_ref[:, lo:lo + D_MODEL] = dup
            t = _dot_nt(dup, wu[j])
            dh2 = t if dh2 is None else dh2 + t
        dx, dg3 = _norm_bwd(x1, g3_ref[...], dh2)
        dx1 = dy + dx
        dx1_ref[...] = dx1
        dg3_ref[...] += dg3
        dmix, dg2 = _norm_bwd(mix_ref[...], g2_ref[...], dx1)
        dmix_ref[...] = dmix.astype(BF16)
        dg2_ref[...] += dg2

    slabs = pltpu.VMEM((N_CHIPS, D_MODEL, D_MODEL), BF16)
    gain = _const((1, D_MODEL))
    return pl.pallas_call(
        body, name="mlp_core", grid=(T // tm,),
        in_specs=[_rows(tm, D_MODEL)] * 4 + [gain] * 3 + [ANY] * 4,
        out_specs=[_rows(tm, D_FF), _rows(tm, D_MODEL), _rows(tm, D_FF), _rows(tm, D_MODEL), _rows(tm, D_MODEL),
                   _const((8, LANES)), gain, gain, gain],
        out_shape=[_sds((T, D_FF), BF16), _sds((T, D_MODEL), BF16), _sds((T, D_FF), BF16), _sds((T, D_MODEL), F32),
                   _sds((T, D_MODEL), BF16), _sds((8, LANES), F32)] + [_sds((1, D_MODEL), F32)] * 3,
        scratch_shapes=[slabs] * 2 + [pltpu.VMEM((tm, D_FF), F32), pltpu.SemaphoreType.DMA((4, N_CHIPS))],
        compiler_params=_cp("arbitrary"),
    )(h2, x1, mix, tgt, g4, g3, g2, *w_up, *w_down)


def _dw(tag, a, g, ta, tn, shard_cols=False, exchanges=()):
    T, ka = a.shape
    n = g.shape[1]
    tk = min(2 * TM, T)
    nk = T // tk

    def body(a_ref, g_ref, o_ref):
        @pl.when(pl.program_id(2) == 0)
        def _():
            o_ref[...] = jnp.zeros_like(o_ref)

        o_ref[...] += _dot_tn(a_ref[...], g_ref[...])

    if shard_cols:
        per = (n // N_CHIPS) // tn
        out_spec = pl.BlockSpec((None, ta, tn), lambda i, j, k: (j // per, i, j % per))
        out_shape = _sds((N_CHIPS, ka, n // N_CHIPS), F32)
    else:
        out_spec = pl.BlockSpec((ta, tn), lambda i, j, k: (i, j))
        out_shape = _sds((ka, n), F32)
    return _call(
        body, name="dw_" + tag, grid=(ka // ta, n // tn, nk),
        in_specs=[pl.BlockSpec((tk, ta), lambda i, j, k: (k, i)), pl.BlockSpec((tk, tn), lambda i, j, k: (k, j))],
        out_specs=[out_spec], out_shape=[out_shape],
        args=(a, g), sem=("parallel", "parallel", "arbitrary"), exchanges=exchanges)


def _dw_mix(merged, dmix, y_pool, dbp, y_attn, dba, exchanges=()):
    T = merged.shape[0]
    tk = min(2 * TM, T)
    c = D_MODEL // N_CHIPS

    def body(mg_ref, dmix_ref, yp_ref, dbp_ref, ya_ref, dba_ref, out_ref, bp_ref, ba_ref):
        @pl.when(pl.program_id(0) == 0)
        def _():
            for ref in (out_ref, bp_ref, ba_ref):
                ref[...] = jnp.zeros_like(ref)

        out_ref[...] += _dot_tn(mg_ref[...], dmix_ref[...])
        for y_ref, d_ref, o_ref in ((yp_ref, dbp_ref, bp_ref), (ya_ref, dba_ref, ba_ref)):
            res = _dot_tn(y_ref[...], d_ref[...])
            for j in range(N_CHIPS):
                o_ref[j] += res[:, c * j:c * (j + 1)]

    slabs = (N_CHIPS, POOL_WIDTH, c)
    return _call(
        body, name="dw_mix", grid=(T // tk,),
        in_specs=[_rows(tk, D_MODEL), _rows(tk, D_MODEL), _rows(tk, POOL_WIDTH), _rows(tk, D_MODEL),
                  _rows(tk, ATTN_WIDTH), _rows(tk, D_MODEL)],
        out_specs=[_const((D_MODEL, D_MODEL)), _const(slabs), _const(slabs)],
        out_shape=[_sds((D_MODEL, D_MODEL), F32), _sds(slabs, F32), _sds(slabs, F32)],
        args=(merged, dmix, y_pool, dbp, y_attn, dba), sem=("arbitrary",), exchanges=exchanges)


def _merge_bwd(dmix, gate, y_pool, y_attn, w_out, w_bp, w_ba, exchanges=()):
    T = dmix.shape[0]
    tm = min(TM, T)

    def body(dmix_ref, gate_ref, yp_ref, ya_ref, wo_ref, wbp_ref, wba_ref,
             dbp_ref, dba_ref, dgate_ref, dyp_ref, dya_ref):
        dm = _dot_nt(dmix_ref[...], wo_ref[...])
        for j, (y_ref, db_ref, w_ref, dy_ref) in enumerate(
                ((yp_ref, dbp_ref, wbp_ref, dyp_ref), (ya_ref, dba_ref, wba_ref, dya_ref))):
            sl = slice(D_MODEL * j, D_MODEL * (j + 1))
            gt = gate_ref[:, sl].astype(F32)
            db = (dm * gt).astype(BF16)
            db_ref[...] = db
            dgate_ref[:, sl] = (dm * _branch(y_ref[...], w_ref) * gt * (1.0 - gt)).astype(BF16)
            cw = D_MODEL // N_CHIPS
            dy = _dot_nt(db[:, :cw], w_ref[0])
            for c in range(1, N_CHIPS):
                dy = dy + _dot_nt(db[:, cw * c:cw * (c + 1)], w_ref[c])
            dy_ref[...] = dy.astype(dy_ref.dtype)

    return _call(
        body, name="merge_bwd", grid=(T // tm,),
        in_specs=[_rows(tm, D_MODEL), _rows(tm, GATE_WIDTH), _rows(tm, POOL_WIDTH), _rows(tm, ATTN_WIDTH),
                  _const((D_MODEL, D_MODEL)), _const(w_bp.shape), _const(w_ba.shape)],
        out_specs=[_rows(tm, D_MODEL), _rows(tm, D_MODEL), _rows(tm, GATE_WIDTH), _rows(tm, POOL_WIDTH),
                   _rows(tm, ATTN_WIDTH)],
        out_shape=[_sds((T, D_MODEL), BF16), _sds((T, D_MODEL), BF16), _sds((T, GATE_WIDTH), BF16),
                   _sds((T, POOL_WIDTH), F32), _sds((T, ATTN_WIDTH), BF16)],
        args=(dmix, gate, y_pool, y_attn, w_out, w_bp, w_ba), sem=("parallel",), exchanges=exchanges)


def _mixers_bwd(q, k, v, do, sinks, tabs, dyp, diff, w_pool, pool_scale, seq, exchanges=()):
    T = q.shape[0]
    nb = seq // BLOCK
    bl = T // seq
    steps = nb + 1
    tp = T // nb
    nseq = seq // tp
    per = tp // HALO
    last_halo = T // HALO - 1

    def body(sink_ref, q_ref, do_ref, kp_ref, kc_ref, vp_ref, vc_ref, c_ref, a_ref, bt_ref, cp_ref, ap_ref, btp_ref,
             dy_ref, nxt_ref, diff_ref, w_ref, s_ref,
             dq_ref, dk_ref, dv_ref, dsink_ref, du_ref, dw_ref, ds_ref, ck_ref, cv_ref):
        n = pl.program_id(0)

        @pl.when(n == 0)
        def _():
            for ref in (dsink_ref, ck_ref, cv_ref, dw_ref, ds_ref):
                ref[...] = jnp.zeros_like(ref)

        @pl.when(n < nb)
        def _():
            _pool_bwd_tile(n, tp, nseq, dy_ref, nxt_ref, diff_ref, w_ref, s_ref, du_ref, dw_ref, ds_ref)
            valid, lo = _attn_masks(n)
            for b in range(bl):
                kk = jnp.concatenate([kp_ref[b], kc_ref[b]], axis=0)
                vv = jnp.concatenate([vp_ref[b], vc_ref[b]], axis=0)
                dk_acc = jnp.zeros((2 * BLOCK, KV_WIDTH), F32)
                dv_acc = jnp.zeros((2 * BLOCK, KV_WIDTH), F32)
                for h in range(2):
                    qs = _stack_heads(q_ref.at[b], h, lo)
                    dos = _stack_heads(do_ref.at[b], h, lo)
                    pr, ps = _group_probs(qs, kk, valid, _sink_rows(sink_ref, h))
                    dp = _dot_nt(dos, vv)
                    delta = jnp.sum(pr * dp, axis=1, keepdims=True)
                    ds = (pr * (dp - delta)).astype(BF16)
                    dsk = ps * delta
                    for g in range(GROUP):
                        idx = GROUP * h + g
                        dsink_ref[idx:idx + 1, :] += (jnp.zeros((1, LANES), F32)
                                                      - jnp.sum(dsk[BLOCK * g:BLOCK * (g + 1)]))
                    dk_acc = dk_acc + _dot_tn(ds, qs)
                    dv_acc = dv_acc + _dot_tn(pr.astype(BF16), dos)
                    for j, pair in enumerate(_unstack_heads(_dot(ds, kk) * SCALE, h, lo)):
                        sl = slice(LANES * (2 * h + j), LANES * (2 * h + j + 1))
                        dq_ref[b, :, sl] = _rot_bwd(pair, c_ref[...], a_ref[...], bt_ref[...]).astype(BF16)
                fin_k = ck_ref[b] + dk_acc[:BLOCK]
                dk_ref[b] = _rot_bwd(fin_k, cp_ref[...], ap_ref[...], btp_ref[...]).astype(BF16)
                dv_ref[b] = (cv_ref[b] + dv_acc[:BLOCK]).astype(BF16)
                ck_ref[b] = dk_acc[BLOCK:]
                cv_ref[b] = dv_acc[BLOCK:]

        @pl.when(n == nb)
        def _():
            for b in range(bl):
                dk_ref[b] = _rot_bwd(ck_ref[b], cp_ref[...], ap_ref[...], btp_ref[...]).astype(BF16)
                dv_ref[b] = cv_ref[b].astype(BF16)

    cur = lambda n: (0, jnp.minimum(n, nb - 1), 0)
    prv = lambda n: (0, jnp.clip(n - 1, 0, nb - 1), 0)
    tcur = lambda n: (jnp.minimum(n, nb - 1), 0)
    tprv = lambda n: (jnp.clip(n - 1, 0, nb - 1), 0)
    wide = lambda m: pl.BlockSpec((bl, BLOCK, ATTN_WIDTH), m)
    kv = lambda m: pl.BlockSpec((bl, BLOCK, KV_WIDTH), m)
    tab = lambda m: pl.BlockSpec((BLOCK, LANES), m)
    tile = lambda n: (jnp.minimum(n, nb - 1), 0)
    halo = lambda n: (jnp.minimum((jnp.minimum(n, nb - 1) + 1) * per, last_halo), 0)
    rows = pl.BlockSpec((tp, POOL_WIDTH), tile)
    res = _call(
        body, name="mixers_bwd", grid=(steps,),
        in_specs=[pl.BlockSpec(memory_space=pltpu.SMEM), wide(cur), wide(cur), kv(prv), kv(cur), kv(prv), kv(cur),
                  tab(tcur), tab(tcur), tab(tcur), tab(tprv), tab(tprv), tab(tprv),
                  rows, pl.BlockSpec((HALO, POOL_WIDTH), halo), rows, _const((4, POOL_GC, POOL_GC)),
                  _const((1, POOL_WIDTH))],
        out_specs=[wide(cur), kv(prv), kv(prv), _const((8, LANES)), rows, _const((4, POOL_GC, POOL_GC)),
                   _const((1, POOL_WIDTH))],
        out_shape=[_sds((bl, seq, ATTN_WIDTH), BF16), _sds((bl, seq, KV_WIDTH), BF16),
                   _sds((bl, seq, KV_WIDTH), BF16), _sds((8, LANES), F32), _sds((T, POOL_WIDTH), BF16),
                   _sds((4, POOL_GC, POOL_GC), F32), _sds((1, POOL_WIDTH), F32)],
        scratch=[pltpu.VMEM((bl, BLOCK, KV_WIDTH), F32), pltpu.VMEM((bl, BLOCK, KV_WIDTH), F32)],
        args=(sinks, *_by_example(bl, q, do, k, k, v, v), *tabs, *tabs, dyp, dyp, diff, w_pool, pool_scale),
        sem=("arbitrary",), exchanges=exchanges)
    outs, rest = (res if exchanges else (res, None))
    outs = [outs[0].reshape(T, ATTN_WIDTH), outs[1].reshape(T, KV_WIDTH), outs[2].reshape(T, KV_WIDTH), *outs[3:]]
    return (outs, rest) if exchanges else outs


def _pool_bwd_tile(i, tp, nseq, dy_ref, nxt_ref, diff_ref, w_ref, s_ref, du_ref, dw_ref, ds_ref):
    last = (i % nseq) == nseq - 1
    nxt = jnp.where(last, 0.0, nxt_ref[...])
    ext = jnp.concatenate([dy_ref[...], nxt], axis=0) * s_ref[...]
    pos = (i % nseq) * tp + lax.broadcasted_iota(jnp.int32, (tp + HALO, 1), 0)
    for gi, w in enumerate(POOL_WINDOWS):
        sl = slice(POOL_GC * gi, POOL_GC * (gi + 1))
        wg = w_ref[gi].astype(BF16)
        dmx = ext[:, sl].astype(BF16)
        ddiff = _dot_nt(dmx, wg)
        s = ddiff * _inv_count(pos, w)
        sh = 1
        while sh < w:
            s = s + pltpu.roll(s, tp + HALO - sh, 0)
            sh *= 2
        du_ref[:, sl] = (s[:tp] - ddiff[:tp]).astype(BF16)
        dg = diff_ref[:, sl]
        dw_ref[gi] += _dot_tn(dg, dmx[:tp])
        ds_ref[:, sl] += jnp.sum(dy_ref[:, sl] * _dot(dg, wg), axis=0, keepdims=True)


_PARTS = ((0, C_Q), (C_Q, C_K), (C_K, C_V), (C_V, C_G), (C_G, IN_WIDTH))


def _inproj_bwd(parts, x2, dx1, w_in_t, g1, exchanges=()):
    T = x2.shape[0]
    tm = min(TM, T)

    def body(du_ref, dq_ref, dk_ref, dv_ref, dgt_ref, x_ref, dx1_ref, w_ref, g_ref, gx_ref, dg_ref):
        @pl.when(pl.program_id(0) == 0)
        def _():
            dg_ref[...] = jnp.zeros_like(dg_ref)

        dh = jnp.zeros((tm, D_MODEL), F32)
        for (lo, hi), p_ref in zip(_PARTS, (du_ref, dq_ref, dk_ref, dv_ref, dgt_ref)):
            dh = dh + _dot(p_ref[...], w_ref[lo:hi, :])
        dx, dg = _norm_bwd(x_ref[...], g_ref[...], dh)
        gx_ref[...] = dx1_ref[...] + dx
        dg_ref[...] += dg

    return _call(
        body, name="inproj_bwd", grid=(T // tm,),
        in_specs=[_rows(tm, hi - lo) for lo, hi in _PARTS]
        + [_rows(tm, D_MODEL), _rows(tm, D_MODEL), _const((IN_WIDTH, D_MODEL)), _const((1, D_MODEL))],
        out_specs=[_rows(tm, D_MODEL), _const((1, D_MODEL))],
        out_shape=[_sds((T, D_MODEL), F32), _sds((1, D_MODEL), F32)],
        args=(*parts, x2, dx1, w_in_t, g1), sem=("arbitrary",), exchanges=exchanges)


def _dw_in(h, parts, exchanges=()):
    T = h.shape[0]
    tk = min(TM, T)

    def body(h_ref, du_ref, dq_ref, dk_ref, dv_ref, dgt_ref, o_ref, db_ref):
        @pl.when(pl.program_id(0) == 0)
        def _():
            o_ref[...] = jnp.zeros_like(o_ref)
            db_ref[...] = jnp.zeros_like(db_ref)

        hh = h_ref[...]
        for (lo, hi), p_ref in zip(_PARTS, (du_ref, dq_ref, dk_ref, dv_ref, dgt_ref)):
            part = p_ref[...]
            o_ref[lo:hi, :] += _dot_tn(part, hh)
            db_ref[:, lo:hi] += jnp.sum(part.astype(F32), axis=0, keepdims=True)

    return _call(
        body, name="dw_in", grid=(T // tk,),
        in_specs=[_rows(tk, D_MODEL)] + [_rows(tk, hi - lo) for lo, hi in _PARTS],
        out_specs=[_const((IN_WIDTH, D_MODEL)), _const((1, IN_WIDTH))],
        out_shape=[_sds((IN_WIDTH, D_MODEL), F32), _sds((1, IN_WIDTH), F32)],
        args=(h, *parts), sem=("arbitrary",), exchanges=exchanges)


def _row_tile(rows, cap=256, mult=16):
    best = None
    for t in range(mult, min(rows, cap) + 1, mult):
        if rows % t == 0:
            best = t
    if best is None:
        raise ValueError("no row tile for %d rows" % rows)
    return best


def _pair_sum(ids, full, got):
    _, r, c = full.shape
    hr = r // 2
    tr = _row_tile(hr)
    nblk = hr // tr

    def body(ids_ref, a_ref, b_ref, own_ref, sb_ref):
        s = a_ref[...] + b_ref[...]
        sb_ref[...] = s.astype(BF16)

        @pl.when(pl.program_id(1) == ids_ref[0])
        def _():
            own_ref[...] = s

    slab = pl.BlockSpec((None, tr, c), lambda i, j, ids_ref: (j, i, 0))
    return pl.pallas_call(
        body, name="pair_sum_%dx%d" % (r, c),
        grid_spec=pltpu.PrefetchScalarGridSpec(
            num_scalar_prefetch=1, grid=(nblk, N_CHIPS),
            in_specs=[pl.BlockSpec((None, tr, c), lambda i, j, ids_ref: (j, ids_ref[1] * nblk + i, 0)), slab],
            out_specs=[pl.BlockSpec((tr, c), lambda i, j, ids_ref: (i, 0)), slab]),
        out_shape=[_sds((hr, c), F32), _sds((N_CHIPS, hr, c), BF16)],
        compiler_params=_cp("parallel", "arbitrary"),
    )(ids, full, got)


def _pair_sum_small(ids, fulls, gots):
    n = len(fulls)
    dims = [(f.shape[1] // 2, f.shape[2]) for f in fulls]

    def body(ids_ref, *refs):
        ins, outs = refs[:2 * n], refs[2 * n:]
        for k in range(n):
            s = ins[2 * k][...] + ins[2 * k + 1][...]
            outs[2 * k + 1][...] = s.astype(BF16)

            @pl.when(pl.program_id(0) == ids_ref[0])
            def _(k=k, s=s):
                outs[2 * k][...] = s

    in_specs, out_specs, out_shape = [], [], []
    for hr, c in dims:
        slab = pl.BlockSpec((None, hr, c), lambda j, ids_ref: (j, 0, 0))
        in_specs += [pl.BlockSpec((None, hr, c), lambda j, ids_ref: (j, ids_ref[1], 0)), slab]
        out_specs += [pl.BlockSpec((hr, c), lambda j, ids_ref: (0, 0)), slab]
        out_shape += [_sds((hr, c), F32), _sds((N_CHIPS, hr, c), BF16)]
    res = pl.pallas_call(
        body, name="pair_sum_small",
        grid_spec=pltpu.PrefetchScalarGridSpec(num_scalar_prefetch=1, grid=(N_CHIPS,), in_specs=in_specs,
                                               out_specs=out_specs),
        out_shape=out_shape, compiler_params=_cp("arbitrary"),
    )(ids, *[a for pair in zip(fulls, gots) for a in pair])
    return [(res[2 * k], res[2 * k + 1]) for k in range(n)]


def _chip_sum_small(ids, owns, gots):
    n = len(owns)

    def body(ids_ref, *refs):
        ins, outs = refs[:2 * n], refs[2 * n:]
        for k in range(n):
            a, b = ins[2 * k], ins[2 * k + 1]
            outs[k][...] = ((a[...] + b[0].astype(F32)) + b[1].astype(F32)) + b[2].astype(F32)

    in_specs, out_specs, out_shape = [], [], []
    for own in owns:
        hr, c = own.shape
        in_specs += [pl.BlockSpec((hr, c), lambda i, ids_ref: (0, 0)),
                     pl.BlockSpec((3, hr, c), lambda i, ids_ref: (0, 0, 0))]
        out_specs.append(pl.BlockSpec((hr, c), lambda i, ids_ref: (ids_ref[1], 0)))
        out_shape.append(_sds((2 * hr, c), F32))
    return pl.pallas_call(
        body, name="chip_sum_small",
        grid_spec=pltpu.PrefetchScalarGridSpec(num_scalar_prefetch=1, grid=(1,), in_specs=in_specs,
                                               out_specs=out_specs),
        out_shape=out_shape, compiler_params=_cp("arbitrary"),
    )(ids, *[a for pair in zip(owns, gots) for a in pair])


def _chip_sum(ids, own, got):
    hr, c = own.shape
    tr = _row_tile(hr)
    nblk = hr // tr

    def body(ids_ref, a_ref, b_ref, o_ref):
        o_ref[...] = ((a_ref[...] + b_ref[0].astype(F32)) + b_ref[1].astype(F32)) + b_ref[2].astype(F32)

    return pl.pallas_call(
        body, name="chip_sum_%dx%d" % (hr, c),
        grid_spec=pltpu.PrefetchScalarGridSpec(
            num_scalar_prefetch=1, grid=(nblk,),
            in_specs=[pl.BlockSpec((tr, c), lambda i, ids_ref: (i, 0)),
                      pl.BlockSpec((3, tr, c), lambda i, ids_ref: (0, i, 0))],
            out_specs=pl.BlockSpec((tr, c), lambda i, ids_ref: (ids_ref[1] * nblk + i, 0))),
        out_shape=_sds((2 * hr, c), F32),
        compiler_params=_cp("parallel"),
    )(ids, own, got)


def _adamw_math(w, g, m, v):
    nm = ADAM_B1 * m + (1.0 - ADAM_B1) * g
    nv = ADAM_B2 * v + (1.0 - ADAM_B2) * (g * g)
    m_hat = nm / (1.0 - ADAM_B1 ** ADAM_STEP)
    v_hat = nv / (1.0 - ADAM_B2 ** ADAM_STEP)
    return -ADAM_LR * (m_hat / (jnp.sqrt(v_hat) + ADAM_EPS) + ADAM_WD * w), nm, nv


def _adamw(w, g, m, v):
    r, c = w.shape
    tr = _row_tile(r, cap=512, mult=8)

    def body(w_ref, g_ref, m_ref, v_ref, d_ref, nm_ref, nv_ref):
        d_ref[...], nm_ref[...], nv_ref[...] = _adamw_math(w_ref[...], g_ref[...], m_ref[...], v_ref[...])

    spec = _rows(tr, c)
    return pl.pallas_call(
        body, name="adamw_%dx%d" % (r, c), grid=(r // tr,),
        in_specs=[spec] * 4, out_specs=[spec] * 3, out_shape=[_sds((r, c), F32)] * 3,
        compiler_params=_cp("parallel"),
    )(w, g, m, v)


SC_TILES = 32
SC_LANES = 16
SC_ROWS = 8


def _adamw_sparse(w, g, m, v):
    r, c = w.shape
    rows = r // SC_TILES
    step = min(rows, SC_ROWS)

    def body(w_hbm, g_hbm, m_hbm, v_hbm, d_hbm, nm_hbm, nv_hbm, wb, gb, mb, vb):
        tile = lax.axis_index("sc_subcore") * 2 + lax.axis_index("sc_core")

        @pl.loop(0, rows, step=step)
        def _(r0):
            mine = pl.ds(tile * rows + r0, step)
            for src, dst in ((w_hbm, wb), (g_hbm, gb), (m_hbm, mb), (v_hbm, vb)):
                pltpu.sync_copy(src.at[mine], dst)

            @pl.loop(0, step)
            def _(row):
                @pl.loop(0, c, step=SC_LANES)
                def _(i):
                    at = (row, pl.ds(i, SC_LANES))
                    wb[at], mb[at], vb[at] = _adamw_math(wb[at], gb[at], mb[at], vb[at])

            for src, dst in ((wb, d_hbm), (mb, nm_hbm), (vb, nv_hbm)):
                pltpu.sync_copy(src, dst.at[mine])

    return pl.kernel(
        body, name="adamw_sparse_%dx%d" % (r, c), out_type=[_sds((r, c), F32)] * 3,
        mesh=plsc.VectorSubcoreMesh(core_axis_name="sc_core", subcore_axis_name="sc_subcore"),
        scratch_types=[pltpu.VMEM((step, c), F32)] * 4,
    )(w, g, m, v)


_SMALL_NAMES = ("w_pool", "b_in", "g_mix_pre", "g_mix_post", "g_mlp_pre", "g_mlp_post", "pool_scale", "attn_sinks")
B_ROWS = -(-IN_WIDTH // D_MODEL)


def _row_block(rows):
    rows = [jnp.pad(r.astype(F32), ((0, 0), (0, D_MODEL - r.shape[1]))) for r in rows]
    return jnp.pad(jnp.concatenate(rows, axis=0), ((0, 8 - len(rows)), (0, 0)))


def _early_block(dg2, dg3, dg4, dps, dsink, loss):
    tail = jnp.concatenate([jnp.pad(dsink.reshape(1, -1), ((0, 0), (0, LANES - dsink.size))),
                            jnp.pad(loss.reshape(1, 1), ((0, 0), (0, LANES - 1)))], axis=1)
    return _row_block([dg2, dg3, dg4, dps, tail])


def _late_block(db_in, dg1):
    b = jnp.pad(db_in, ((0, 0), (0, B_ROWS * D_MODEL - IN_WIDTH))).reshape(B_ROWS, D_MODEL)
    return _row_block([b[r:r + 1] for r in range(B_ROWS)] + [dg1])


def _small_update(gearly, gmat, glate, w, m, v):
    names = _SMALL_NAMES
    n = len(names)

    def total(ref, rows):
        acc = ref[0:rows, :]
        for d in range(1, N_DEV):
            acc = acc + ref[d * rows:(d + 1) * rows, :]
        return acc

    def body(*refs):
        early_ref, gmat_ref, late_ref = refs[:3]
        w_refs, m_refs, v_refs = refs[3:3 + n], refs[3 + n:3 + 2 * n], refs[3 + 2 * n:3 + 3 * n]
        outs = refs[3 + 3 * n:]
        loss_ref, g_refs, d_refs = outs[0], outs[1:1 + n], outs[1 + n:1 + 2 * n]
        nm_refs, nv_refs = outs[1 + 2 * n:1 + 3 * n], outs[1 + 3 * n:1 + 4 * n]
        early, late = total(early_ref, 8), total(late_ref, 8)
        loss_ref[...] = jnp.sum(early[4:5, LANES:2 * LANES], axis=1, keepdims=True)
        bias = jnp.concatenate([late[r:r + 1, :] for r in range(B_ROWS - 1)]
                               + [late[B_ROWS - 1:B_ROWS, :IN_WIDTH - (B_ROWS - 1) * D_MODEL]], axis=1)
        grad = dict(b_in=bias, g_mix_pre=late[B_ROWS:B_ROWS + 1, :], g_mix_post=early[0:1, :],
                    g_mlp_pre=early[1:2, :], g_mlp_post=early[2:3, :], pool_scale=early[3:4, :POOL_WIDTH],
                    attn_sinks=early[4:5, :N_Q_HEADS])
        for i, name in enumerate(names):
            g = total(gmat_ref, 4 * POOL_GC) if name == "w_pool" else grad[name]
            g_refs[i][...] = g
            d_refs[i][...], nm_refs[i][...], nv_refs[i][...] = _adamw_math(
                w_refs[i][...], g, m_refs[i][...], v_refs[i][...])

    shapes = [_sds(w[k].shape, F32) for k in names]
    res = pl.pallas_call(
        body, name="small_update", out_shape=[_sds((1, 1), F32)] + shapes * 4,
        compiler_params=pltpu.CompilerParams(vmem_limit_bytes=VMEM_MB * 1024 * 1024),
    )(gearly, gmat, glate, *[w[k] for k in names], *[m[k] for k in names], *[v[k] for k in names])
    loss = res[0]
    per = {k: tuple(res[1 + j * n + i] for j in range(4)) for i, k in enumerate(names)}
    return loss, per


_BIG = ("w_in", "w_branch_pool", "w_branch_attn", "w_out", "w_up", "w_down")
_ORDER = ("g_mix_pre", "w_in", "b_in", "w_pool", "pool_scale", "attn_sinks", "w_branch_pool", "w_branch_attn",
          "w_out", "g_mix_post", "g_mlp_pre", "w_up", "w_down", "g_mlp_post")


def _stack_rows(slab):
    return slab.reshape(-1, slab.shape[2])


def _step(x2, tgt, seq, shards, small, ids):
    tabs = _rope_tables(seq)
    g1, g2, g3, g4 = (small[n] for n in ("g_mix_pre", "g_mix_post", "g_mlp_pre", "g_mlp_post"))
    sinks = small["attn_sinks"].reshape(N_Q_HEADS)
    w_pool = small["w_pool"].reshape(4, POOL_GC, POOL_GC)
    pool_scale = small["pool_scale"]

    def whole(shard, slabs):
        return lax.dynamic_update_slice(slabs, shard[None], (ids[0], 0, 0))

    (up_a, up_b, down_a, down_b, *mix_shards), [[in_slab]] = _cast_shards(
        *(shards[n] for n in ("w_up", "w_down", "w_branch_pool", "w_branch_attn", "w_out")),
        exchanges=[_ex_gather([shards["w_in"]])])
    w_in = _stack_rows(whole(shards["w_in"], in_slab))
    (h, u, q, k, v, gate), [mix_slabs] = _inproj(
        x2, g1, w_in, small["b_in"], tabs, seq, exchanges=[_ex_gather(mix_shards)])
    w_bp, w_ba, out_slab = (whole(s, g) for s, g in zip(mix_shards, mix_slabs))
    w_out = _stack_rows(out_slab)
    (y_attn, diff, y_pool), [[got_a, got_b]] = _mixers_fwd(
        q, k, v, sinks, u, w_pool, pool_scale, seq, exchanges=[_ex_gather([up_a, up_b])])
    (merged, mix, x1, h2), [[got_c, got_d]] = _merge_out(
        y_pool, y_attn, gate, x2, w_bp, w_ba, w_out, g2, g3, exchanges=[_ex_gather([down_a, down_b])])
    w_up = (whole(up_a, got_a), whole(up_b, got_b))
    w_down = (whole(down_a, got_c), whole(down_b, got_d))
    act, dff, dup, dx1, dmix, loss_acc, dg4, dg3, dg2 = _mlp_core(h2, x1, mix, tgt, w_up, w_down, g4, g3, g2)

    dw_down = _dw("down", act, dff, 1024, 1024)[0].reshape(N_CHIPS, D_FF // N_CHIPS, D_MODEL)
    (dbp, dba, dgate, dyp, dya), [[got]] = _merge_bwd(
        dmix, gate, y_pool, y_attn, w_out, w_bp, w_ba, exchanges=[_ex_pair([dw_down])])
    ps_down = _pair_sum(ids, dw_down, got)
    (dw_up,), [[got]] = _dw("up", h2, dup, 1024, 1024, shard_cols=True, exchanges=[_ex_chip([ps_down[1]])])
    half_down = _chip_sum(ids, ps_down[0], got)
    (dw_out, dw_bp, dw_ba), [[got]] = _dw_mix(merged, dmix, y_pool, dbp, y_attn, dba, exchanges=[_ex_pair([dw_up])])
    ps_up = _pair_sum(ids, dw_up, got)
    dw_mix = [dw_out.reshape(N_CHIPS, D_MODEL // N_CHIPS, D_MODEL), dw_bp, dw_ba]
    (dq, dk, dv, dsink, du, dw_pool, dps), [[got], gots, [g_down]] = _mixers_bwd(
        q, k, v, dya, sinks, tabs, dyp, diff, w_pool, pool_scale, seq,
        exchanges=[_ex_chip([ps_up[1]]), _ex_pair(dw_mix), _ex_swap([half_down])])
    half_up = _chip_sum(ids, ps_up[0], got)
    ps_mix = _pair_sum_small(ids, dw_mix, gots)
    parts = (du, dq, dk, dv, dgate)
    early = _early_block(dg2, dg3, dg4, dps, dsink[:, 0], loss_acc[0, 0])
    mat = dw_pool.reshape(4 * POOL_GC, POOL_GC)
    (dw_in_t, db_in), [gots, [gearly, gmat], [g_up]] = _dw_in(
        h, parts, exchanges=[_ex_chip([p[1] for p in ps_mix]), _ex_allgather([early, mat]), _ex_swap([half_up])])
    half_mix = _chip_sum_small(ids, [p[0] for p in ps_mix], gots)
    dw_in = dw_in_t.reshape(N_CHIPS, IN_WIDTH // N_CHIPS, D_MODEL)
    g_mix, [got] = _alone("swap_mix_pair_in", _ex_swap(half_mix), _ex_pair([dw_in]))
    ps_in = _pair_sum(ids, dw_in, got)
    (gx, dg1), [[got]] = _inproj_bwd(parts, x2, dx1, w_in, g1, exchanges=[_ex_chip([ps_in[1]])])
    [g_in], [glate] = _alone("swap_in_allgather", _ex_swap([_chip_sum(ids, ps_in[0], got)]),
                             _ex_allgather([_late_block(db_in, dg1)]))

    grads = dict(w_in=g_in, w_branch_pool=g_mix[1], w_branch_attn=g_mix[2], w_out=g_mix[0], w_up=g_up, w_down=g_down)
    return (gearly, gmat, glate), gx, grads


def kernel(x, g_mix_pre, w_in, b_in, w_pool, pool_scale, attn_sinks, w_branch_pool, w_branch_attn, w_out, g_mix_post, g_mlp_pre, w_up, w_down, g_mlp_post, loss_target, m_g_mix_pre, m_w_in, m_b_in, m_w_pool, m_pool_scale, m_attn_sinks, m_w_branch_pool, m_w_branch_attn, m_w_out, m_g_mix_post, m_g_mlp_pre, m_w_up, m_w_down, m_g_mlp_post, v_g_mix_pre, v_w_in, v_b_in, v_w_pool, v_pool_scale, v_attn_sinks, v_w_branch_pool, v_w_branch_attn, v_w_out, v_g_mix_post, v_g_mlp_pre, v_w_up, v_w_down, v_g_mlp_post):
    weights = dict(g_mix_pre=g_mix_pre, w_in=w_in, b_in=b_in, w_pool=w_pool, pool_scale=pool_scale,
                   attn_sinks=attn_sinks, w_branch_pool=w_branch_pool, w_branch_attn=w_branch_attn, w_out=w_out,
                   g_mix_post=g_mix_post, g_mlp_pre=g_mlp_pre, w_up=w_up, w_down=w_down, g_mlp_post=g_mlp_post)
    mom1 = dict(g_mix_pre=m_g_mix_pre, w_in=m_w_in, b_in=m_b_in, w_pool=m_w_pool, pool_scale=m_pool_scale,
                attn_sinks=m_attn_sinks, w_branch_pool=m_w_branch_pool, w_branch_attn=m_w_branch_attn,
                w_out=m_w_out, g_mix_post=m_g_mix_post, g_mlp_pre=m_g_mlp_pre, w_up=m_w_up, w_down=m_w_down,
                g_mlp_post=m_g_mlp_post)
    mom2 = dict(g_mix_pre=v_g_mix_pre, w_in=v_w_in, b_in=v_b_in, w_pool=v_w_pool, pool_scale=v_pool_scale,
                attn_sinks=v_attn_sinks, w_branch_pool=v_w_branch_pool, w_branch_attn=v_w_branch_attn,
                w_out=v_w_out, g_mix_post=v_g_mix_post, g_mlp_pre=v_g_mlp_pre, w_up=v_w_up, w_down=v_w_down,
                g_mlp_post=v_g_mlp_post)
    b_loc, seq, _ = x.shape
    x2 = x.reshape(b_loc * seq, D_MODEL)
    tgt = loss_target.reshape(b_loc * seq, D_MODEL)
    ids = jnp.stack([2 * lax.axis_index("x") + lax.axis_index("y"), lax.axis_index("c")]).astype(jnp.int32)

    def flat(n, a):
        return a[0].T if n == "w_in" else a[0]

    def unflat(n, a):
        return (a.T if n == "w_in" else a)[None]

    shards = {n: flat(n, weights[n]).astype(BF16) if n == "w_in" else flat(n, weights[n]) for n in _BIG}
    small = {n: weights[n] for n in _ORDER if n not in _BIG}
    (gearly, gmat, glate), gx, grads = _step(x2, tgt, seq, shards, small, ids)

    def two_d(src):
        return {n: src[n].reshape(4 * POOL_GC, POOL_GC) if n == "w_pool" else src[n] for n in _SMALL_NAMES}

    loss, per = _small_update(gearly, gmat, glate, two_d(weights), two_d(mom1), two_d(mom2))
    delta, new_m, new_v = {}, {}, {}
    for n in _SMALL_NAMES:
        grads[n], delta[n], new_m[n], new_v[n] = (a.reshape(weights[n].shape) for a in per[n])
    for n in _BIG:
        update = _adamw if n == "w_in" else _adamw_sparse
        d, nm, nv = update(flat(n, weights[n]), grads[n], flat(n, mom1[n]), flat(n, mom2[n]))
        grads[n] = unflat(n, grads[n])
        delta[n], new_m[n], new_v[n] = unflat(n, d), unflat(n, nm), unflat(n, nv)

    return (loss[0, 0], gx.reshape(x.shape), *[grads[n] for n in _ORDER], *[delta[n] for n in _ORDER],
            *[new_m[n] for n in _ORDER], *[new_v[n] for n in _ORDER])
```

```python
import jax
import jax.numpy as jnp
from jax import lax
from jax.experimental import pallas as pl
from jax.experimental.pallas import tpu as pltpu
from jax.experimental.pallas import tpu_sc as plsc

F32 = jnp.float32
BF16 = jnp.bfloat16

D_MODEL = 1024
POOL_WINDOWS = (2, 4, 8, 16)
POOL_WIDTH = 512
POOL_GC = 128
HALO = 16
HEAD_DIM = 64
N_Q_HEADS = 8
ATTN_WIDTH = 512
KV_WIDTH = 128
BLOCK = 128
NEG_INF = -1e30
ROPE_THETA = 500000.0
ROT_DIM = 16
GATE_WIDTH = 2048
IN_WIDTH = 3328
D_FF = 4096
EPS = 1e-6
SCALE = HEAD_DIM ** -0.5
C_Q, C_K, C_V, C_G = 512, 1024, 1152, 1280

ADAM_LR, ADAM_B1, ADAM_B2, ADAM_EPS, ADAM_WD, ADAM_STEP = 0.001, 0.9, 0.999, 1e-08, 0.01, 10

N_CHIPS = 4
N_DEV = 8
LANES = 128
TM = 512
VMEM_MB = 56

MESH = pl.DeviceIdType.MESH
ANY = pl.BlockSpec(memory_space=pl.ANY)


def _cp(*sem, vmem=VMEM_MB):
    return pltpu.CompilerParams(dimension_semantics=sem, vmem_limit_bytes=vmem * 1024 * 1024)


def _rows(tile, cols):
    return pl.BlockSpec((tile, cols), lambda i: (i, 0))


def _const(shape):
    nd = len(shape)
    return pl.BlockSpec(shape, lambda i: (0,) * nd)


def _sds(shape, dtype):
    return jax.ShapeDtypeStruct(shape, dtype)


def _dot(a, b):
    return jnp.dot(a, b, preferred_element_type=F32)


def _dot_nt(a, b):
    return lax.dot_general(a, b, (((1,), (1,)), ((), ())), preferred_element_type=F32)


def _dot_tn(a, b):
    return lax.dot_general(a, b, (((0,), (0,)), ((), ())), preferred_element_type=F32)


def _rms(x):
    return lax.rsqrt(jnp.mean(x * x, axis=-1, keepdims=True) + EPS)


def _norm_bwd(x, g, dout):
    r = _rms(x)
    n = x * r
    dn = dout * g
    dx = r * (dn - n * jnp.mean(dn * n, axis=-1, keepdims=True))
    return dx, jnp.sum(dout * n, axis=0, keepdims=True)


def _rot_fwd(t, c, a, bt):
    return t * c + pltpu.roll(t, LANES - 8, 1) * a + pltpu.roll(t, 8, 1) * bt


def _rot_bwd(d, c, a, bt):
    return d * c + pltpu.roll(d * a, 8, 1) + pltpu.roll(d * bt, LANES - 8, 1)


def _rope_tables(seq):
    pos = jnp.arange(seq, dtype=F32)
    inv_freq = ROPE_THETA ** (-jnp.arange(0, ROT_DIM, 2, dtype=F32) / ROT_DIM)
    ang = pos[:, None] * inv_freq[None, :]
    cos, sin = jnp.cos(ang), jnp.sin(ang)
    ones = jnp.ones((seq, HEAD_DIM - ROT_DIM), F32)
    zeros8 = jnp.zeros((seq, 8), F32)
    zrest = jnp.zeros((seq, HEAD_DIM - ROT_DIM), F32)
    c = jnp.concatenate([cos, cos, ones], axis=1)
    a = jnp.concatenate([-sin, zeros8, zrest], axis=1)
    bt = jnp.concatenate([zeros8, sin, zrest], axis=1)
    return tuple(jnp.tile(t, (1, 2)) for t in (c, a, bt))


class _Exchange:
    def __init__(self, inputs, out_shapes, sems, start, finish, aliases=None, middle=None):
        self.inputs, self.out_shapes, self.sems = list(inputs), list(out_shapes), list(sems)
        self.start, self.finish, self.aliases = start, finish, dict(aliases or {})
        self.middle = middle


def _call(body, *, name, grid, in_specs, out_specs, out_shape, args, scratch=(), sem=(), exchanges=()):
    in_specs, out_specs, out_shape, scratch = list(in_specs), list(out_specs), list(out_shape), list(scratch)
    if not exchanges:
        return pl.pallas_call(body, name=name, grid=grid, in_specs=in_specs, out_specs=out_specs,
                              out_shape=out_shape, scratch_shapes=scratch, compiler_params=_cp(*sem))(*args)
    n_in, n_out, n_scr = len(in_specs), len(out_specs), len(scratch)
    x_in = [a for ex in exchanges for a in ex.inputs]
    x_out = [s for ex in exchanges for s in ex.out_shapes]
    x_sem = [s for ex in exchanges for s in ex.sems]
    aliases, i_off, o_off = {}, n_in, n_out
    for ex in exchanges:
        for i, o in ex.aliases.items():
            aliases[i_off + i] = o_off + o
        i_off += len(ex.inputs)
        o_off += len(ex.out_shapes)

    def split(flat):
        out, pos = [], 0
        for ex, n in zip(exchanges, flat[1]):
            out.append(flat[0][pos:pos + n])
            pos += n
        return out

    def carrier(*refs):
        pos = 0
        groups = []
        for n in (n_in, len(x_in), n_out, len(x_out), n_scr, len(x_sem)):
            groups.append(refs[pos:pos + n])
            pos += n
        ins, xin, outs, xout, scr, xsem = groups
        xin = split((xin, [len(ex.inputs) for ex in exchanges]))
        xout = split((xout, [len(ex.out_shapes) for ex in exchanges]))
        xsem = split((xsem, [len(ex.sems) for ex in exchanges]))
        first = pl.program_id(0) == 0
        last = pl.program_id(0) == grid[0] - 1
        for d in range(1, len(grid)):
            first = jnp.logical_and(first, pl.program_id(d) == 0)
            last = jnp.logical_and(last, pl.program_id(d) == grid[d] - 1)

        @pl.when(first)
        def _():
            for ex, i, o, s in zip(exchanges, xin, xout, xsem):
                ex.start(i, o, s)

        if any(ex.middle for ex in exchanges):
            half = pl.program_id(0) == 5 * grid[0] // 8
            for d in range(1, len(grid)):
                half = jnp.logical_and(half, pl.program_id(d) == 0)

            @pl.when(half)
            def _():
                for ex, i, o, s in zip(exchanges, xin, xout, xsem):
                    if ex.middle:
                        ex.middle(i, o, s)

        body(*ins, *outs, *scr)

        @pl.when(last)
        def _():
            for ex, i, o, s in zip(exchanges, xin, xout, xsem):
                ex.finish(i, o, s)

    res = pl.pallas_call(
        carrier, name=name, grid=grid, in_specs=in_specs + [ANY] * len(x_in),
        out_specs=out_specs + [ANY] * len(x_out), out_shape=out_shape + x_out,
        scratch_shapes=scratch + x_sem, input_output_aliases=aliases,
        compiler_params=_cp(*(["arbitrary"] * len(grid))),
    )(*args, *x_in)
    return res[:n_out], split((res[n_out:], [len(ex.out_shapes) for ex in exchanges]))


def _alone(name, *exchanges):
    n_in = [len(ex.inputs) for ex in exchanges]
    n_out = [len(ex.out_shapes) for ex in exchanges]
    n_sem = [len(ex.sems) for ex in exchanges]
    aliases, i_off, o_off = {}, 0, 0
    for ex in exchanges:
        for i, o in ex.aliases.items():
            aliases[i_off + i] = o_off + o
        i_off += len(ex.inputs)
        o_off += len(ex.out_shapes)

    def split(flat, counts):
        out, pos = [], 0
        for n in counts:
            out.append(flat[pos:pos + n])
            pos += n
        return out

    def body(*refs):
        ins, outs, sems = split(refs, [sum(n_in), sum(n_out), sum(n_sem)])
        groups = list(zip(exchanges, split(ins, n_in), split(outs, n_out), split(sems, n_sem)))
        for ex, i, o, s in groups:
            ex.start(i, o, s)
        for ex, i, o, s in groups:
            if ex.middle:
                ex.middle(i, o, s)
        for ex, i, o, s in groups:
            ex.finish(i, o, s)

    res = pl.pallas_call(
        body, name=name, in_specs=[ANY] * sum(n_in), out_specs=[ANY] * sum(n_out),
        out_shape=[s for ex in exchanges for s in ex.out_shapes],
        scratch_shapes=[s for ex in exchanges for s in ex.sems], input_output_aliases=aliases,
    )(*[a for ex in exchanges for a in ex.inputs])
    return split(res, n_out)


def _place():
    x, y, c = lax.axis_index("x"), lax.axis_index("y"), lax.axis_index("c")
    chips = [(1 - x, y), (x, 1 - y), (1 - x, 1 - y)]
    return x, y, c, chips


def _remote(src, dst, send, recv, to):
    return pltpu.make_async_remote_copy(src_ref=src, dst_ref=dst, send_sem=send, recv_sem=recv,
                                        device_id=to, device_id_type=MESH)


def _ex_gather(shards):
    nw = len(shards)
    hrs = [s.shape[0] // 2 for s in shards]

    def copies(ins, outs, sems):
        s1, r1, s2, r2, fs, fr = sems
        x, y, c, _ = _place()
        me, xn, yn, dg = (x, y), (1 - x, y), (x, 1 - y), (1 - x, 1 - y)
        nbr = (xn, yn)
        sibling = (x, y, 1 - c)

        def piece(w, chip, core, part=None):
            hr = hrs[w]
            rows = pl.ds(core * hr, hr) if part is None else pl.ds(core * hr + part * (hr // 2), hr // 2)
            return outs[w].at[2 * chip[0] + chip[1], rows]

        def first(w, k):
            return _remote(ins[w].at[pl.ds(c * hrs[w], hrs[w])], piece(w, me, c), s1.at[w, k], r1.at[w, k],
                           (*nbr[k], c))

        def landed(w, k):
            return _remote(piece(w, nbr[k], c), piece(w, nbr[k], c), s1.at[w, k], r1.at[w, k], (*nbr[k], c))

        def onward(w, k):
            return _remote(piece(w, nbr[k], c, k), piece(w, nbr[k], c, k), s2.at[w, k], r2.at[w, k],
                           (*nbr[1 - k], c))

        def arrived(w, k):
            return _remote(piece(w, dg, c, k), piece(w, dg, c, k), s2.at[w, k], r2.at[w, k], (*nbr[1 - k], c))

        def passed(w, j):
            chip = (xn, yn, dg)[j]
            return _remote(piece(w, chip, c), piece(w, chip, c), fs.at[w, j], fr.at[w, j], sibling)

        def handed(w, j):
            chip = (xn, yn, dg)[j]
            return _remote(piece(w, chip, 1 - c), piece(w, chip, 1 - c), fs.at[w, j], fr.at[w, j], sibling)

        return first, landed, onward, arrived, passed, handed

    def start(ins, outs, sems):
        first = copies(ins, outs, sems)[0]
        for w in range(nw):
            for k in range(2):
                first(w, k).start()

    def middle(ins, outs, sems):
        _, landed, onward, _, passed, _ = copies(ins, outs, sems)
        for w in range(nw):
            for k in range(2):
                landed(w, k).wait_recv()
                onward(w, k).start()
                passed(w, k).start()

    def finish(ins, outs, sems):
        first, _, onward, arrived, passed, handed = copies(ins, outs, sems)
        for w in range(nw):
            for k in range(2):
                arrived(w, k).wait_recv()
            passed(w, 2).start()
        for w in range(nw):
            for j in range(3):
                handed(w, j).wait_recv()
        for w in range(nw):
            for k in range(2):
                first(w, k).wait_send()
                onward(w, k).wait_send()
            for j in range(3):
                passed(w, j).wait_send()

    return _Exchange(shards, [_sds((N_CHIPS,) + s.shape, s.dtype) for s in shards],
                     [pltpu.SemaphoreType.DMA((nw, 2))] * 4 + [pltpu.SemaphoreType.DMA((nw, 3))] * 2,
                     start, finish, middle=middle)


def _ex_pair(grads):
    nw = len(grads)

    def copies(ins, outs, sems):
        x, y, c, _ = _place()
        out = []
        for w in range(nw):
            hr = grads[w].shape[1] // 2
            out.append(_remote(ins[w].at[:, pl.ds((1 - c) * hr, hr)], outs[w], sems[0].at[w], sems[1].at[w],
                               (x, y, 1 - c)))
        return out

    def start(ins, outs, sems):
        for cp in copies(ins, outs, sems):
            cp.start()

    def finish(ins, outs, sems):
        for cp in copies(ins, outs, sems):
            cp.wait()

    return _Exchange(grads, [_sds((N_CHIPS, g.shape[1] // 2, g.shape[2]), F32) for g in grads],
                     [pltpu.SemaphoreType.DMA((nw,))] * 2, start, finish)


def _ex_chip(pieces, relations=(0, 1, 2)):
    nw = len(pieces)

    def copies(ins, outs, sems):
        x, y, c, chips = _place()
        return [_remote(ins[w].at[2 * chips[r][0] + chips[r][1]], outs[w].at[k], sems[0].at[w, k], sems[1].at[w, k],
                        (*chips[r], c))
                for w in range(nw) for k, r in enumerate(relations)]

    def start(ins, outs, sems):
        for cp in copies(ins, outs, sems):
            cp.start()

    def finish(ins, outs, sems):
        for cp in copies(ins, outs, sems):
            cp.wait()

    return _Exchange(pieces, [_sds((len(relations),) + p.shape[1:], BF16) for p in pieces],
                     [pltpu.SemaphoreType.DMA((nw, len(relations)))] * 2, start, finish)


def _ex_swap(fulls):
    nw = len(fulls)

    def start(ins, outs, sems):
        x, y, c, _ = _place()
        for w in range(nw):
            hr = fulls[w].shape[0] // 2
            mine = pl.ds(c * hr, hr)
            _remote(ins[w].at[mine], outs[w].at[mine], sems[0].at[w], sems[1].at[w], (x, y, 1 - c)).start()

    def finish(ins, outs, sems):
        x, y, c, _ = _place()
        for w in range(nw):
            hr = fulls[w].shape[0] // 2
            mine, theirs = pl.ds(c * hr, hr), pl.ds((1 - c) * hr, hr)
            _remote(ins[w].at[mine], outs[w].at[mine], sems[0].at[w], sems[1].at[w], (x, y, 1 - c)).wait_send()
            _remote(ins[w].at[theirs], outs[w].at[theirs], sems[0].at[w], sems[1].at[w], (x, y, 1 - c)).wait_recv()

    return _Exchange(fulls, [_sds(f.shape, F32) for f in fulls], [pltpu.SemaphoreType.DMA((nw,))] * 2,
                     start, finish, aliases={w: w for w in range(nw)})


def _ex_allgather(blocks):
    nb = len(blocks)

    def copies(ins, outs, sems):
        send, recv, lsem = sems
        x, y, c, chips = _place()
        me, sibling = (x, y, c), (x, y, 1 - c)

        def rows(b, px, py, pc):
            m_per = blocks[b].shape[0]
            return outs[b].at[pl.ds((4 * px + 2 * py + pc) * m_per, m_per), :]

        def copy(b, k, blk, to, src=None):
            return _remote(rows(b, *blk) if src is None else src, rows(b, *blk), send.at[b, k], recv.at[b, k], to)

        def mine(b):
            return pltpu.make_async_copy(ins[b], rows(b, *me), lsem.at[b])

        def first(b, k):
            return copy(b, k, me, sibling if k == 0 else (*chips[k - 1], c), src=ins[b])

        def passed(b, j):
            return copy(b, 4 + j, (*chips[j], c), sibling)

        def landed(b, j):
            return copy(b, 1 + j, (*chips[j], c), me)

        def handed(b, k):
            return copy(b, 0, sibling, me) if k == 0 else copy(b, 3 + k, (*chips[k - 1], 1 - c), me)

        return mine, first, passed, landed, handed

    def start(ins, outs, sems):
        mine, first, _, _, _ = copies(ins, outs, sems)
        for b in range(nb):
            mine(b).start()
            for k in range(4):
                first(b, k).start()

    def finish(ins, outs, sems):
        mine, first, passed, landed, handed = copies(ins, outs, sems)
        sent = []
        for b in range(nb):
            for j in range(3):
                landed(b, j).wait_recv()
                cp = passed(b, j)
                cp.start()
                sent.append(cp)
        for b in range(nb):
            for k in range(4):
                handed(b, k).wait_recv()
            for k in range(4):
                first(b, k).wait_send()
        for cp in sent:
            cp.wait_send()
        for b in range(nb):
            mine(b).wait()

    return _Exchange(blocks, [_sds((N_DEV * b.shape[0], b.shape[1]), F32) for b in blocks],
                     [pltpu.SemaphoreType.DMA((nb, 7)), pltpu.SemaphoreType.DMA((nb, 7)), pltpu.SemaphoreType.DMA((nb,))],
                     start, finish)


def _cast_shards(w_up, w_down, w_bp, w_ba, w_out, exchanges=()):
    r, c = w_up.shape
    tr = HALF // 2
    steps = r // tr

    def body(up_ref, down_ref, bp_ref, ba_ref, out_ref, ua_ref, ub_ref, da_ref, db_ref, bpo_ref, bao_ref, outo_ref):
        i = pl.program_id(0)

        @pl.when(i == 0)
        def _():
            for src, dst in ((bp_ref, bpo_ref), (ba_ref, bao_ref), (out_ref, outo_ref)):
                dst[...] = src[...].astype(BF16)

        @pl.when(i < steps // 2)
        def _():
            ua_ref[...] = up_ref[...].astype(BF16)
            da_ref[...] = down_ref[...].astype(BF16)

        @pl.when(i >= steps // 2)
        def _():
            ub_ref[...] = up_ref[...].astype(BF16)
            db_ref[...] = down_ref[...].astype(BF16)

    rows = _rows(tr, c)
    first = pl.BlockSpec((tr, c), lambda i: (jnp.minimum(i, steps // 2 - 1), 0))
    second = pl.BlockSpec((tr, c), lambda i: (jnp.maximum(i - steps // 2, 0), 0))
    half = _sds((HALF, c), BF16)
    return _call(
        body, name="cast_shards", grid=(steps,),
        in_specs=[rows, rows, _const(w_bp.shape), _const(w_ba.shape), _const(w_out.shape)],
        out_specs=[first, second, first, second, _const(w_bp.shape), _const(w_ba.shape), _const(w_out.shape)],
        out_shape=[half, half, half, half, _sds(w_bp.shape, BF16), _sds(w_ba.shape, BF16), _sds(w_out.shape, BF16)],
        args=(w_up, w_down, w_bp, w_ba, w_out), sem=("arbitrary",), exchanges=exchanges)


def _inproj(x2, g1, w_in_t, b_in, tabs, seq, exchanges=()):
    T = x2.shape[0]
    tm = min(TM, seq)
    nseq = seq // tm

    def body(x_ref, g_ref, w_ref, b_ref, c_ref, a_ref, bt_ref, h_ref, u_ref, q_ref, k_ref, v_ref, gate_ref):
        x = x_ref[...]
        h = (x * _rms(x) * g_ref[...]).astype(BF16)
        h_ref[...] = h

        def proj(lo, hi):
            return _dot_nt(h, w_ref[lo:hi, :]) + b_ref[:, lo:hi]

        c, a, bt = c_ref[...], a_ref[...], bt_ref[...]
        u_ref[...] = proj(0, C_Q)
        q = proj(C_Q, C_K)
        for p in range(4):
            sl = slice(LANES * p, LANES * (p + 1))
            q_ref[:, sl] = (_rot_fwd(q[:, sl], c, a, bt) * SCALE).astype(BF16)
        kv = proj(C_K, C_G)
        k_ref[...] = _rot_fwd(kv[:, :KV_WIDTH], c, a, bt).astype(BF16)
        v_ref[...] = kv[:, KV_WIDTH:].astype(BF16)
        for j in range(2):
            lo = C_G + D_MODEL * j
            gate_ref[:, D_MODEL * j:D_MODEL * (j + 1)] = jax.nn.sigmoid(proj(lo, lo + D_MODEL)).astype(BF16)

    tab = pl.BlockSpec((tm, LANES), lambda i: (i % nseq, 0))
    return _call(
        body, name="inproj", grid=(T // tm,),
        in_specs=[_rows(tm, D_MODEL), _const((1, D_MODEL)), _const((IN_WIDTH, D_MODEL)), _const((1, IN_WIDTH)),
                  tab, tab, tab],
        out_specs=[_rows(tm, D_MODEL), _rows(tm, POOL_WIDTH), _rows(tm, ATTN_WIDTH), _rows(tm, KV_WIDTH),
                   _rows(tm, KV_WIDTH), _rows(tm, GATE_WIDTH)],
        out_shape=[_sds((T, D_MODEL), BF16), _sds((T, POOL_WIDTH), F32), _sds((T, ATTN_WIDTH), BF16),
                   _sds((T, KV_WIDTH), BF16), _sds((T, KV_WIDTH), BF16), _sds((T, GATE_WIDTH), BF16)],
        args=(x2, g1, w_in_t, b_in, *tabs), sem=("parallel",), exchanges=exchanges)


def _inv_count(pos, w):
    return 1.0 / jnp.minimum(pos + 1, w).astype(F32)


def _pool_tile(i, tp, nseq, u_ref, prev_ref, w_ref, s_ref, diff_ref, y_ref):
    first = (i % nseq) == 0
    prev = jnp.where(first, 0.0, prev_ref[...])
    ext = jnp.concatenate([prev, u_ref[...]], axis=0)
    pos = (i % nseq) * tp + lax.broadcasted_iota(jnp.int32, (tp, 1), 0)
    for gi, w in enumerate(POOL_WINDOWS):
        sl = slice(POOL_GC * gi, POOL_GC * (gi + 1))
        xg = ext[:, sl]
        s = xg
        sh = 1
        while sh < w:
            s = s + pltpu.roll(s, sh, 0)
            sh *= 2
        pooled = s[HALO:] * _inv_count(pos, w)
        diff = (pooled - xg[HALO:]).astype(BF16)
        diff_ref[:, sl] = diff
        mixed = _dot(diff, w_ref[gi].astype(BF16))
        y_ref[:, sl] = (mixed * s_ref[:, sl]).astype(BF16)


def _pool_specs(tp):
    per = tp // HALO
    return [_rows(tp, POOL_WIDTH), pl.BlockSpec((HALO, POOL_WIDTH), lambda i: (jnp.maximum(i * per - 1, 0), 0)),
            _const((4, POOL_GC, POOL_GC)), _const((1, POOL_WIDTH))]


GROUP = 4
GROWS = GROUP * BLOCK


def _attn_masks(n):
    qi = lax.broadcasted_iota(jnp.int32, (GROWS, 2 * BLOCK), 0) % BLOCK
    kj = lax.broadcasted_iota(jnp.int32, (GROWS, 2 * BLOCK), 1)
    rel = qi + BLOCK - kj
    valid = (rel >= 0) & (rel < BLOCK) & (kj >= jnp.where(n > 0, 0, BLOCK))
    lo = lax.broadcasted_iota(jnp.int32, (BLOCK, LANES), 1) < HEAD_DIM
    return valid, lo


def _by_example(bl, *arrays):
    return [a.reshape(bl, a.shape[0] // bl, a.shape[1]) for a in arrays]


def _stack_heads(ref, h, lo):
    keep = lo if h == 0 else jnp.logical_not(lo)
    pieces = []
    for p in (2 * h, 2 * h + 1):
        xp = ref[:, LANES * p:LANES * (p + 1)].astype(F32)
        for e in range(2):
            t = xp if e == h else pltpu.roll(xp, HEAD_DIM, 1)
            pieces.append(jnp.where(keep, t, 0.0).astype(BF16))
    return jnp.concatenate(pieces, axis=0)


def _unstack_heads(stacked, h, lo):
    pairs = []
    for j in range(2):
        parts = []
        for e in range(2):
            t = stacked[BLOCK * (2 * j + e):BLOCK * (2 * j + e + 1)]
            parts.append(t if e == h else pltpu.roll(t, HEAD_DIM, 1))
        pairs.append(jnp.where(lo, parts[0], parts[1]))
    return pairs


def _sink_rows(sink_ref, h):
    head = lax.broadcasted_iota(jnp.int32, (GROWS, 1), 0) // BLOCK
    col = jnp.zeros((GROWS, 1), F32) + sink_ref[GROUP * h]
    for g in range(1, GROUP):
        col = jnp.where(head == g, sink_ref[GROUP * h + g], col)
    return col


def _group_probs(qs, kk, valid, sink):
    s = jnp.where(valid, _dot_nt(qs, kk), NEG_INF)
    m = jnp.maximum(jnp.max(s, axis=1, keepdims=True), sink)
    ex = jnp.exp(s - m)
    es = jnp.exp(sink - m)
    inv = 1.0 / (jnp.sum(ex, axis=1, keepdims=True) + es)
    return ex * inv, es * inv


def _mixers_fwd(q, k, v, sinks, u, w_pool, pool_scale, seq, exchanges=()):
    T = q.shape[0]
    nb = seq // BLOCK
    bl = T // seq
    tp = T // nb
    nseq = seq // tp

    def body(sink_ref, q_ref, kp_ref, kc_ref, vp_ref, vc_ref, u_ref, prev_ref, w_ref, s_ref, o_ref, diff_ref, y_ref):
        n = pl.program_id(0)
        valid, lo = _attn_masks(n)
        for b in range(bl):
            kk = jnp.concatenate([kp_ref[b], kc_ref[b]], axis=0)
            vv = jnp.concatenate([vp_ref[b], vc_ref[b]], axis=0)
            for h in range(2):
                qs = _stack_heads(q_ref.at[b], h, lo)
                pr, _ = _group_probs(qs, kk, valid, _sink_rows(sink_ref, h))
                o = _dot(pr.astype(BF16), vv)
                for j, pair in enumerate(_unstack_heads(o, h, lo)):
                    p = 2 * h + j
                    o_ref[b, :, LANES * p:LANES * (p + 1)] = pair.astype(BF16)
        _pool_tile(n, tp, nseq, u_ref, prev_ref, w_ref, s_ref, diff_ref, y_ref)

    cur = lambda n: (0, n, 0)
    prv = lambda n: (0, jnp.maximum(n - 1, 0), 0)
    kv = lambda m: pl.BlockSpec((bl, BLOCK, KV_WIDTH), m)
    res = _call(
        body, name="mixers_fwd", grid=(nb,),
        in_specs=[pl.BlockSpec(memory_space=pltpu.SMEM), pl.BlockSpec((bl, BLOCK, ATTN_WIDTH), cur),
                  kv(prv), kv(cur), kv(prv), kv(cur)] + _pool_specs(tp),
        out_specs=[pl.BlockSpec((bl, BLOCK, ATTN_WIDTH), cur), _rows(tp, POOL_WIDTH), _rows(tp, POOL_WIDTH)],
        out_shape=[_sds((bl, seq, ATTN_WIDTH), BF16), _sds((T, POOL_WIDTH), BF16), _sds((T, POOL_WIDTH), BF16)],
        args=(sinks, *_by_example(bl, q, k, k, v, v), u, u, w_pool, pool_scale), sem=("parallel",),
        exchanges=exchanges)
    outs, rest = res if exchanges else (res, None)
    return [outs[0].reshape(T, ATTN_WIDTH), outs[1], outs[2]], rest


def _branch(y, w_ref):
    return jnp.concatenate([_dot(y, w_ref[j]) for j in range(N_CHIPS)], axis=1)


def _merge_out(y_pool, y_attn, gate, x2, w_bp, w_ba, w_out, g2, g3, exchanges=()):
    T = x2.shape[0]
    tm = min(TM, T)

    def body(yp_ref, ya_ref, gate_ref, x_ref, wbp_ref, wba_ref, wo_ref, g2_ref, g3_ref,
             mg_ref, mix_ref, x1_ref, h2_ref):
        bp, ba = _branch(yp_ref[...], wbp_ref), _branch(ya_ref[...], wba_ref)
        merged = (gate_ref[:, :D_MODEL].astype(F32) * bp + gate_ref[:, D_MODEL:].astype(F32) * ba).astype(BF16)
        mg_ref[...] = merged
        mix = _dot(merged, wo_ref[...])
        mix_ref[...] = mix
        x1 = x_ref[...] + mix * _rms(mix) * g2_ref[...]
        x1_ref[...] = x1
        h2_ref[...] = (x1 * _rms(x1) * g3_ref[...]).astype(BF16)

    return _call(
        body, name="merge_out", grid=(T // tm,),
        in_specs=[_rows(tm, POOL_WIDTH), _rows(tm, ATTN_WIDTH), _rows(tm, GATE_WIDTH), _rows(tm, D_MODEL),
                  _const(w_bp.shape), _const(w_ba.shape), _const((D_MODEL, D_MODEL)),
                  _const((1, D_MODEL)), _const((1, D_MODEL))],
        out_specs=[_rows(tm, D_MODEL)] * 4,
        out_shape=[_sds((T, D_MODEL), BF16), _sds((T, D_MODEL), F32), _sds((T, D_MODEL), F32),
                   _sds((T, D_MODEL), BF16)],
        args=(y_pool, y_attn, gate, x2, w_bp, w_ba, w_out, g2, g3), sem=("parallel",), exchanges=exchanges)


HALF = D_MODEL // 2
TM_MLP = 256


def _mlp_core(h2, x1, mix, tgt, w_up, w_down, g4, g3, g2):
    T = h2.shape[0]
    tm = min(TM_MLP, T)

    def body(h_ref, x1_ref, mix_ref, t_ref, g_ref, g3_ref, g2_ref, ua_hbm, ub_hbm, da_hbm, db_hbm,
             act_ref, dff_ref, dup_ref, dx1_ref, dmix_ref, loss_ref, dg_ref, dg3_ref, dg2_ref,
             wu, wd, relu_scr, sems):
        def weight_copy(i):
            src, dst = ((ua_hbm, wu.at[:, :HALF]), (ub_hbm, wu.at[:, HALF:]),
                        (da_hbm, wd.at[:, :HALF]), (db_hbm, wd.at[:, HALF:]))[i]
            return pltpu.make_async_copy(src, dst, sems.at[i])

        @pl.when(pl.program_id(0) == 0)
        def _():
            for i in range(4):
                weight_copy(i).start()
            loss_ref[...] = jnp.zeros_like(loss_ref)
            for ref in (dg_ref, dg3_ref, dg2_ref):
                ref[...] = jnp.zeros_like(ref)
            weight_copy(0).wait()
            weight_copy(1).wait()

        h = h_ref[...]
        ff = None
        for j in range(N_CHIPS):
            lo = D_MODEL * j
            relu = jnp.maximum(_dot(h, wu[j]), 0.0)
            if j == 0:
                @pl.when(pl.program_id(0) == 0)
                def _():
                    weight_copy(2).wait()
                    weight_copy(3).wait()
            relu_scr[:, lo:lo + D_MODEL] = relu
            act = jnp.square(relu).astype(BF16)
            act_ref[:, lo:lo + D_MODEL] = act
            t = _dot(act, wd[j])
            ff = t if ff is None else ff + t
        g = g_ref[...]
        x1 = x1_ref[...]
        err = x1 + ff * _rms(ff) * g - t_ref[...]
        loss_ref[...] += jnp.sum(err * err) * (0.5 / D_MODEL)
        dy = err * (1.0 / D_MODEL)
        dff, dg = _norm_bwd(ff, g, dy)
        dg_ref[...] += dg
        dff = dff.astype(BF16)
        dff_ref[...] = dff
        dh2 = None
        for j in range(N_CHIPS):
            lo = D_MODEL * j
            dup = (_dot_nt(dff, wd[j]) * (2.0 * relu_scr[:, lo:lo + D_MODEL])).astype(BF16)
            dup_ref[:, lo:lo + D_MODEL] = dup
            t = _dot_nt(dup, wu[j])
            dh2 = t if dh2 is None else dh2 + t
        dx, dg3 = _norm_bwd(x1, g3_ref[...], dh2)
        dx1 = dy + dx
        dx1_ref[...] = dx1
        dg3_ref[...] += dg3
        dmix, dg2 = _norm_bwd(mix_ref[...], g2_ref[...], dx1)
        dmix_ref[...] = dmix.astype(BF16)
        dg2_ref[...] += dg2

    slabs = pltpu.VMEM((N_CHIPS, D_MODEL, D_MODEL), BF16)
    gain = _const((1, D_MODEL))
    return pl.pallas_call(
        body, name="mlp_core", grid=(T // tm,),
        in_specs=[_rows(tm, D_MODEL)] * 4 + [gain] * 3 + [ANY] * 4,
        out_specs=[_rows(tm, D_FF), _rows(tm, D_MODEL), _rows(tm, D_FF), _rows(tm, D_MODEL), _rows(tm, D_MODEL),
                   _const((8, LANES)), gain, gain, gain],
        out_shape=[_sds((T, D_FF), BF16), _sds((T, D_MODEL), BF16), _sds((T, D_FF), BF16), _sds((T, D_MODEL), F32),
                   _sds((T, D_MODEL), BF16), _sds((8, LANES), F32)] + [_sds((1, D_MODEL), F32)] * 3,
        scratch_shapes=[slabs] * 2 + [pltpu.VMEM((tm, D_FF), F32), pltpu.SemaphoreType.DMA((4,))],
        compiler_params=_cp("arbitrary"),
    )(h2, x1, mix, tgt, g4, g3, g2, *w_up, *w_down)


def _dw(tag, a, g, ta, tn, shard_cols=False, exchanges=()):
    T, ka = a.shape
    n = g.shape[1]
    tk = min(2 * TM, T)
    nk = T // tk

    def body(a_ref, g_ref, o_ref):
        @pl.when(pl.program_id(2) == 0)
        def _():
            o_ref[...] = jnp.zeros_like(o_ref)

        o_ref[...] += _dot_tn(a_ref[...], g_ref[...])

    if shard_cols:
        per = (n // N_CHIPS) // tn
        out_spec = pl.BlockSpec((None, ta, tn), lambda i, j, k: (j // per, i, j % per))
        out_shape = _sds((N_CHIPS, ka, n // N_CHIPS), F32)
    else:
        out_spec = pl.BlockSpec((ta, tn), lambda i, j, k: (i, j))
        out_shape = _sds((ka, n), F32)
    return _call(
        body, name="dw_" + tag, grid=(ka // ta, n // tn, nk),
        in_specs=[pl.BlockSpec((tk, ta), lambda i, j, k: (k, i)), pl.BlockSpec((tk, tn), lambda i, j, k: (k, j))],
        out_specs=[out_spec], out_shape=[out_shape],
        args=(a, g), sem=("parallel", "parallel", "arbitrary"), exchanges=exchanges)


def _dw_mix(merged, dmix, y_pool, dbp, y_attn, dba, exchanges=()):
    T = merged.shape[0]
    tk = min(2 * TM, T)
    c = D_MODEL // N_CHIPS

    def body(mg_ref, dmix_ref, yp_ref, dbp_ref, ya_ref, dba_ref, out_ref, bp_ref, ba_ref):
        @pl.when(pl.program_id(0) == 0)
        def _():
            for ref in (out_ref, bp_ref, ba_ref):
                ref[...] = jnp.zeros_like(ref)

        out_ref[...] += _dot_tn(mg_ref[...], dmix_ref[...])
        for y_ref, d_ref, o_ref in ((yp_ref, dbp_ref, bp_ref), (ya_ref, dba_ref, ba_ref)):
            res = _dot_tn(y_ref[...], d_ref[...])
            for j in range(N_CHIPS):
                o_ref[j] += res[:, c * j:c * (j + 1)]

    slabs = (N_CHIPS, POOL_WIDTH, c)
    return _call(
        body, name="dw_mix", grid=(T // tk,),
        in_specs=[_rows(tk, D_MODEL), _rows(tk, D_MODEL), _rows(tk, POOL_WIDTH), _rows(tk, D_MODEL),
                  _rows(tk, ATTN_WIDTH), _rows(tk, D_MODEL)],
        out_specs=[_const((D_MODEL, D_MODEL)), _const(slabs), _const(slabs)],
        out_shape=[_sds((D_MODEL, D_MODEL), F32), _sds(slabs, F32), _sds(slabs, F32)],
        args=(merged, dmix, y_pool, dbp, y_attn, dba), sem=("arbitrary",), exchanges=exchanges)


def _merge_bwd(dmix, gate, y_pool, y_attn, w_out, w_bp, w_ba, exchanges=()):
    T = dmix.shape[0]
    tm = min(TM, T)

    def body(dmix_ref, gate_ref, yp_ref, ya_ref, wo_ref, wbp_ref, wba_ref,
             dbp_ref, dba_ref, dgate_ref, dyp_ref, dya_ref):
        dm = _dot_nt(dmix_ref[...], wo_ref[...])
        for j, (y_ref, db_ref, w_ref, dy_ref) in enumerate(
                ((yp_ref, dbp_ref, wbp_ref, dyp_ref), (ya_ref, dba_ref, wba_ref, dya_ref))):
            sl = slice(D_MODEL * j, D_MODEL * (j + 1))
            gt = gate_ref[:, sl].astype(F32)
            db = (dm * gt).astype(BF16)
            db_ref[...] = db
            dgate_ref[:, sl] = (dm * _branch(y_ref[...], w_ref) * gt * (1.0 - gt)).astype(BF16)
            cw = D_MODEL // N_CHIPS
            dy = _dot_nt(db[:, :cw], w_ref[0])
            for c in range(1, N_CHIPS):
                dy = dy + _dot_nt(db[:, cw * c:cw * (c + 1)], w_ref[c])
            dy_ref[...] = dy.astype(dy_ref.dtype)

    return _call(
        body, name="merge_bwd", grid=(T // tm,),
        in_specs=[_rows(tm, D_MODEL), _rows(tm, GATE_WIDTH), _rows(tm, POOL_WIDTH), _rows(tm, ATTN_WIDTH),
                  _const((D_MODEL, D_MODEL)), _const(w_bp.shape), _const(w_ba.shape)],
        out_specs=[_rows(tm, D_MODEL), _rows(tm, D_MODEL), _rows(tm, GATE_WIDTH), _rows(tm, POOL_WIDTH),
                   _rows(tm, ATTN_WIDTH)],
        out_shape=[_sds((T, D_MODEL), BF16), _sds((T, D_MODEL), BF16), _sds((T, GATE_WIDTH), BF16),
                   _sds((T, POOL_WIDTH), F32), _sds((T, ATTN_WIDTH), BF16)],
        args=(dmix, gate, y_pool, y_attn, w_out, w_bp, w_ba), sem=("parallel",), exchanges=exchanges)


def _mixers_bwd(q, k, v, do, sinks, tabs, dyp, diff, w_pool, pool_scale, seq, exchanges=()):
    T = q.shape[0]
    nb = seq // BLOCK
    bl = T // seq
    steps = nb + 1
    tp = T // nb
    nseq = seq // tp
    per = tp // HALO
    last_halo = T // HALO - 1

    def body(sink_ref, q_ref, do_ref, kp_ref, kc_ref, vp_ref, vc_ref, c_ref, a_ref, bt_ref, cp_ref, ap_ref, btp_ref,
             dy_ref, nxt_ref, diff_ref, w_ref, s_ref,
             dq_ref, dk_ref, dv_ref, dsink_ref, du_ref, dw_ref, ds_ref, ck_ref, cv_ref):
        n = pl.program_id(0)

        @pl.when(n == 0)
        def _():
            for ref in (dsink_ref, ck_ref, cv_ref, dw_ref, ds_ref):
                ref[...] = jnp.zeros_like(ref)

        @pl.when(n < nb)
        def _():
            _pool_bwd_tile(n, tp, nseq, dy_ref, nxt_ref, diff_ref, w_ref, s_ref, du_ref, dw_ref, ds_ref)
            valid, lo = _attn_masks(n)
            for b in range(bl):
                kk = jnp.concatenate([kp_ref[b], kc_ref[b]], axis=0)
                vv = jnp.concatenate([vp_ref[b], vc_ref[b]], axis=0)
                dk_acc = jnp.zeros((2 * BLOCK, KV_WIDTH), F32)
                dv_acc = jnp.zeros((2 * BLOCK, KV_WIDTH), F32)
                for h in range(2):
                    qs = _stack_heads(q_ref.at[b], h, lo)
                    dos = _stack_heads(do_ref.at[b], h, lo)
                    pr, ps = _group_probs(qs, kk, valid, _sink_rows(sink_ref, h))
                    dp = _dot_nt(dos, vv)
                    delta = jnp.sum(pr * dp, axis=1, keepdims=True)
                    ds = (pr * (dp - delta)).astype(BF16)
                    dsk = ps * delta
                    for g in range(GROUP):
                        idx = GROUP * h + g
                        dsink_ref[idx:idx + 1, :] += (jnp.zeros((1, LANES), F32)
                                                      - jnp.sum(dsk[BLOCK * g:BLOCK * (g + 1)]))
                    dk_acc = dk_acc + _dot_tn(ds, qs)
                    dv_acc = dv_acc + _dot_tn(pr.astype(BF16), dos)
                    for j, pair in enumerate(_unstack_heads(_dot(ds, kk) * SCALE, h, lo)):
                        sl = slice(LANES * (2 * h + j), LANES * (2 * h + j + 1))
                        dq_ref[b, :, sl] = _rot_bwd(pair, c_ref[...], a_ref[...], bt_ref[...]).astype(BF16)
                fin_k = ck_ref[b] + dk_acc[:BLOCK]
                dk_ref[b] = _rot_bwd(fin_k, cp_ref[...], ap_ref[...], btp_ref[...]).astype(BF16)
                dv_ref[b] = (cv_ref[b] + dv_acc[:BLOCK]).astype(BF16)
                ck_ref[b] = dk_acc[BLOCK:]
                cv_ref[b] = dv_acc[BLOCK:]

        @pl.when(n == nb)
        def _():
            for b in range(bl):
                dk_ref[b] = _rot_bwd(ck_ref[b], cp_ref[...], ap_ref[...], btp_ref[...]).astype(BF16)
                dv_ref[b] = cv_ref[b].astype(BF16)

    cur = lambda n: (0, jnp.minimum(n, nb - 1), 0)
    prv = lambda n: (0, jnp.clip(n - 1, 0, nb - 1), 0)
    tcur = lambda n: (jnp.minimum(n, nb - 1), 0)
    tprv = lambda n: (jnp.clip(n - 1, 0, nb - 1), 0)
    wide = lambda m: pl.BlockSpec((bl, BLOCK, ATTN_WIDTH), m)
    kv = lambda m: pl.BlockSpec((bl, BLOCK, KV_WIDTH), m)
    tab = lambda m: pl.BlockSpec((BLOCK, LANES), m)
    tile = lambda n: (jnp.minimum(n, nb - 1), 0)
    halo = lambda n: (jnp.minimum((jnp.minimum(n, nb - 1) + 1) * per, last_halo), 0)
    rows = pl.BlockSpec((tp, POOL_WIDTH), tile)
    res = _call(
        body, name="mixers_bwd", grid=(steps,),
        in_specs=[pl.BlockSpec(memory_space=pltpu.SMEM), wide(cur), wide(cur), kv(prv), kv(cur), kv(prv), kv(cur),
                  tab(tcur), tab(tcur), tab(tcur), tab(tprv), tab(tprv), tab(tprv),
                  rows, pl.BlockSpec((HALO, POOL_WIDTH), halo), rows, _const((4, POOL_GC, POOL_GC)),
                  _const((1, POOL_WIDTH))],
        out_specs=[wide(cur), kv(prv), kv(prv), _const((8, LANES)), rows, _const((4, POOL_GC, POOL_GC)),
                   _const((1, POOL_WIDTH))],
        out_shape=[_sds((bl, seq, ATTN_WIDTH), BF16), _sds((bl, seq, KV_WIDTH), BF16),
                   _sds((bl, seq, KV_WIDTH), BF16), _sds((8, LANES), F32), _sds((T, POOL_WIDTH), BF16),
                   _sds((4, POOL_GC, POOL_GC), F32), _sds((1, POOL_WIDTH), F32)],
        scratch=[pltpu.VMEM((bl, BLOCK, KV_WIDTH), F32), pltpu.VMEM((bl, BLOCK, KV_WIDTH), F32)],
        args=(sinks, *_by_example(bl, q, do, k, k, v, v), *tabs, *tabs, dyp, dyp, diff, w_pool, pool_scale),
        sem=("arbitrary",), exchanges=exchanges)
    outs, rest = (res if exchanges else (res, None))
    outs = [outs[0].reshape(T, ATTN_WIDTH), outs[1].reshape(T, KV_WIDTH), outs[2].reshape(T, KV_WIDTH), *outs[3:]]
    return (outs, rest) if exchanges else outs


def _pool_bwd_tile(i, tp, nseq, dy_ref, nxt_ref, diff_ref, w_ref, s_ref, du_ref, dw_ref, ds_ref):
    last = (i % nseq) == nseq - 1
    nxt = jnp.where(last, 0.0, nxt_ref[...])
    ext = jnp.concatenate([dy_ref[...], nxt], axis=0) * s_ref[...]
    pos = (i % nseq) * tp + lax.broadcasted_iota(jnp.int32, (tp + HALO, 1), 0)
    for gi, w in enumerate(POOL_WINDOWS):
        sl = slice(POOL_GC * gi, POOL_GC * (gi + 1))
        wg = w_ref[gi].astype(BF16)
        dmx = ext[:, sl].astype(BF16)
        ddiff = _dot_nt(dmx, wg)
        s = ddiff * _inv_count(pos, w)
        sh = 1
        while sh < w:
            s = s + pltpu.roll(s, tp + HALO - sh, 0)
            sh *= 2
        du_ref[:, sl] = (s[:tp] - ddiff[:tp]).astype(BF16)
        dg = diff_ref[:, sl]
        dw_ref[gi] += _dot_tn(dg, dmx[:tp])
        ds_ref[:, sl] += jnp.sum(dy_ref[:, sl] * _dot(dg, wg), axis=0, keepdims=True)


_PARTS = ((0, C_Q), (C_Q, C_K), (C_K, C_V), (C_V, C_G), (C_G, IN_WIDTH))


def _inproj_bwd(parts, x2, dx1, w_in_t, g1, exchanges=()):
    T = x2.shape[0]
    tm = min(TM, T)

    def body(du_ref, dq_ref, dk_ref, dv_ref, dgt_ref, x_ref, dx1_ref, w_ref, g_ref, gx_ref, dg_ref):
        @pl.when(pl.program_id(0) == 0)
        def _():
            dg_ref[...] = jnp.zeros_like(dg_ref)

        dh = jnp.zeros((tm, D_MODEL), F32)
        for (lo, hi), p_ref in zip(_PARTS, (du_ref, dq_ref, dk_ref, dv_ref, dgt_ref)):
            dh = dh + _dot(p_ref[...], w_ref[lo:hi, :])
        dx, dg = _norm_bwd(x_ref[...], g_ref[...], dh)
        gx_ref[...] = dx1_ref[...] + dx
        dg_ref[...] += dg

    return _call(
        body, name="inproj_bwd", grid=(T // tm,),
        in_specs=[_rows(tm, hi - lo) for lo, hi in _PARTS]
        + [_rows(tm, D_MODEL), _rows(tm, D_MODEL), _const((IN_WIDTH, D_MODEL)), _const((1, D_MODEL))],
        out_specs=[_rows(tm, D_MODEL), _const((1, D_MODEL))],
        out_shape=[_sds((T, D_MODEL), F32), _sds((1, D_MODEL), F32)],
        args=(*parts, x2, dx1, w_in_t, g1), sem=("arbitrary",), exchanges=exchanges)


def _dw_in(h, parts, exchanges=()):
    T = h.shape[0]
    tk = min(TM, T)

    def body(h_ref, du_ref, dq_ref, dk_ref, dv_ref, dgt_ref, o_ref, db_ref):
        @pl.when(pl.program_id(0) == 0)
        def _():
            o_ref[...] = jnp.zeros_like(o_ref)
            db_ref[...] = jnp.zeros_like(db_ref)

        hh = h_ref[...]
        for (lo, hi), p_ref in zip(_PARTS, (du_ref, dq_ref, dk_ref, dv_ref, dgt_ref)):
            part = p_ref[...]
            o_ref[lo:hi, :] += _dot_tn(part, hh)
            db_ref[:, lo:hi] += jnp.sum(part.astype(F32), axis=0, keepdims=True)

    return _call(
        body, name="dw_in", grid=(T // tk,),
        in_specs=[_rows(tk, D_MODEL)] + [_rows(tk, hi - lo) for lo, hi in _PARTS],
        out_specs=[_const((IN_WIDTH, D_MODEL)), _const((1, IN_WIDTH))],
        out_shape=[_sds((IN_WIDTH, D_MODEL), F32), _sds((1, IN_WIDTH), F32)],
        args=(h, *parts), sem=("arbitrary",), exchanges=exchanges)


def _row_tile(rows, cap=256, mult=16):
    best = None
    for t in range(mult, min(rows, cap) + 1, mult):
        if rows % t == 0:
            best = t
    if best is None:
        raise ValueError("no row tile for %d rows" % rows)
    return best


def _pair_sum(ids, full, got):
    _, r, c = full.shape
    hr = r // 2
    tr = _row_tile(hr)
    nblk = hr // tr

    def body(ids_ref, a_ref, b_ref, own_ref, sb_ref):
        s = a_ref[...] + b_ref[...]
        sb_ref[...] = s.astype(BF16)

        @pl.when(pl.program_id(1) == ids_ref[0])
        def _():
            own_ref[...] = s

    slab = pl.BlockSpec((None, tr, c), lambda i, j, ids_ref: (j, i, 0))
    return pl.pallas_call(
        body, name="pair_sum_%dx%d" % (r, c),
        grid_spec=pltpu.PrefetchScalarGridSpec(
            num_scalar_prefetch=1, grid=(nblk, N_CHIPS),
            in_specs=[pl.BlockSpec((None, tr, c), lambda i, j, ids_ref: (j, ids_ref[1] * nblk + i, 0)), slab],
            out_specs=[pl.BlockSpec((tr, c), lambda i, j, ids_ref: (i, 0)), slab]),
        out_shape=[_sds((hr, c), F32), _sds((N_CHIPS, hr, c), BF16)],
        compiler_params=_cp("parallel", "arbitrary"),
    )(ids, full, got)


def _pair_sum_small(ids, fulls, gots):
    n = len(fulls)
    dims = [(f.shape[1] // 2, f.shape[2]) for f in fulls]

    def body(ids_ref, *refs):
        ins, outs = refs[:2 * n], refs[2 * n:]
        for k in range(n):
            s = ins[2 * k][...] + ins[2 * k + 1][...]
            outs[2 * k + 1][...] = s.astype(BF16)

            @pl.when(pl.program_id(0) == ids_ref[0])
            def _(k=k, s=s):
                outs[2 * k][...] = s

    in_specs, out_specs, out_shape = [], [], []
    for hr, c in dims:
        slab = pl.BlockSpec((None, hr, c), lambda j, ids_ref: (j, 0, 0))
        in_specs += [pl.BlockSpec((None, hr, c), lambda j, ids_ref: (j, ids_ref[1], 0)), slab]
        out_specs += [pl.BlockSpec((hr, c), lambda j, ids_ref: (0, 0)), slab]
        out_shape += [_sds((hr, c), F32), _sds((N_CHIPS, hr, c), BF16)]
    res = pl.pallas_call(
        body, name="pair_sum_small",
        grid_spec=pltpu.PrefetchScalarGridSpec(num_scalar_prefetch=1, grid=(N_CHIPS,), in_specs=in_specs,
                                               out_specs=out_specs),
        out_shape=out_shape, compiler_params=_cp("arbitrary"),
    )(ids, *[a for pair in zip(fulls, gots) for a in pair])
    return [(res[2 * k], res[2 * k + 1]) for k in range(n)]


def _chip_sum_small(ids, owns, gots):
    n = len(owns)

    def body(ids_ref, *refs):
        ins, outs = refs[:2 * n], refs[2 * n:]
        for k in range(n):
            a, b = ins[2 * k], ins[2 * k + 1]
            outs[k][...] = ((a[...] + b[0].astype(F32)) + b[1].astype(F32)) + b[2].astype(F32)

    in_specs, out_specs, out_shape = [], [], []
    for own in owns:
        hr, c = own.shape
        in_specs += [pl.BlockSpec((hr, c), lambda i, ids_ref: (0, 0)),
                     pl.BlockSpec((3, hr, c), lambda i, ids_ref: (0, 0, 0))]
        out_specs.append(pl.BlockSpec((hr, c), lambda i, ids_ref: (ids_ref[1], 0)))
        out_shape.append(_sds((2 * hr, c), F32))
    return pl.pallas_call(
        body, name="chip_sum_small",
        grid_spec=pltpu.PrefetchScalarGridSpec(num_scalar_prefetch=1, grid=(1,), in_specs=in_specs,
                                               out_specs=out_specs),
        out_shape=out_shape, compiler_params=_cp("arbitrary"),
    )(ids, *[a for pair in zip(owns, gots) for a in pair])


def _chip_sum(ids, own, *gots):
    hr, c = own.shape
    tr = _row_tile(hr)
    nblk = hr // tr

    def body(ids_ref, a_ref, *refs):
        s = a_ref[...]
        for b_ref in refs[:-1]:
            for k in range(b_ref.shape[0]):
                s = s + b_ref[k].astype(F32)
        refs[-1][...] = s

    return pl.pallas_call(
        body, name="chip_sum_%dx%d" % (hr, c),
        grid_spec=pltpu.PrefetchScalarGridSpec(
            num_scalar_prefetch=1, grid=(nblk,),
            in_specs=[pl.BlockSpec((tr, c), lambda i, ids_ref: (i, 0))]
            + [pl.BlockSpec((g.shape[0], tr, c), lambda i, ids_ref: (0, i, 0)) for g in gots],
            out_specs=pl.BlockSpec((tr, c), lambda i, ids_ref: (ids_ref[1] * nblk + i, 0))),
        out_shape=_sds((2 * hr, c), F32),
        compiler_params=_cp("parallel"),
    )(ids, own, *gots)


def _adamw_math(w, g, m, v):
    nm = ADAM_B1 * m + (1.0 - ADAM_B1) * g
    nv = ADAM_B2 * v + (1.0 - ADAM_B2) * (g * g)
    m_hat = nm / (1.0 - ADAM_B1 ** ADAM_STEP)
    v_hat = nv / (1.0 - ADAM_B2 ** ADAM_STEP)
    return -ADAM_LR * (m_hat / (jnp.sqrt(v_hat) + ADAM_EPS) + ADAM_WD * w), nm, nv


def _adamw(w, g, m, v):
    r, c = w.shape
    tr = _row_tile(r, cap=512, mult=8)

    def body(w_ref, g_ref, m_ref, v_ref, d_ref, nm_ref, nv_ref):
        d_ref[...], nm_ref[...], nv_ref[...] = _adamw_math(w_ref[...], g_ref[...], m_ref[...], v_ref[...])

    spec = _rows(tr, c)
    return pl.pallas_call(
        body, name="adamw_%dx%d" % (r, c), grid=(r // tr,),
        in_specs=[spec] * 4, out_specs=[spec] * 3, out_shape=[_sds((r, c), F32)] * 3,
        compiler_params=_cp("parallel"),
    )(w, g, m, v)


SC_TILES = 32
SC_LANES = 16
SC_ROWS = 8


def _adamw_sparse(w, g, m, v):
    r, c = w.shape
    rows = r // SC_TILES
    step = min(rows, SC_ROWS)

    def body(w_hbm, g_hbm, m_hbm, v_hbm, d_hbm, nm_hbm, nv_hbm, wb, gb, mb, vb):
        tile = lax.axis_index("sc_subcore") * 2 + lax.axis_index("sc_core")

        @pl.loop(0, rows, step=step)
        def _(r0):
            mine = pl.ds(tile * rows + r0, step)
            for src, dst in ((w_hbm, wb), (g_hbm, gb), (m_hbm, mb), (v_hbm, vb)):
                pltpu.sync_copy(src.at[mine], dst)

            @pl.loop(0, step)
            def _(row):
                @pl.loop(0, c, step=SC_LANES)
                def _(i):
                    at = (row, pl.ds(i, SC_LANES))
                    wb[at], mb[at], vb[at] = _adamw_math(wb[at], gb[at], mb[at], vb[at])

            for src, dst in ((wb, d_hbm), (mb, nm_hbm), (vb, nv_hbm)):
                pltpu.sync_copy(src, dst.at[mine])

    return pl.kernel(
        body, name="adamw_sparse_%dx%d" % (r, c), out_type=[_sds((r, c), F32)] * 3,
        mesh=plsc.VectorSubcoreMesh(core_axis_name="sc_core", subcore_axis_name="sc_subcore"),
        scratch_types=[pltpu.VMEM((step, c), F32)] * 4,
    )(w, g, m, v)


_SMALL_NAMES = ("w_pool", "b_in", "g_mix_pre", "g_mix_post", "g_mlp_pre", "g_mlp_post", "pool_scale", "attn_sinks")
B_ROWS = -(-IN_WIDTH // D_MODEL)


def _row_block(rows):
    rows = [jnp.pad(r.astype(F32), ((0, 0), (0, D_MODEL - r.shape[1]))) for r in rows]
    return jnp.pad(jnp.concatenate(rows, axis=0), ((0, 8 - len(rows)), (0, 0)))


def _early_block(dg2, dg3, dg4, dps, dsink, loss):
    tail = jnp.concatenate([jnp.pad(dsink.reshape(1, -1), ((0, 0), (0, LANES - dsink.size))),
                            jnp.pad(loss.reshape(1, 1), ((0, 0), (0, LANES - 1)))], axis=1)
    return _row_block([dg2, dg3, dg4, dps, tail])


def _late_block(db_in, dg1):
    b = jnp.pad(db_in, ((0, 0), (0, B_ROWS * D_MODEL - IN_WIDTH))).reshape(B_ROWS, D_MODEL)
    return _row_block([b[r:r + 1] for r in range(B_ROWS)] + [dg1])


def _small_update(gearly, gmat, glate, w, m, v):
    names = _SMALL_NAMES
    n = len(names)

    def total(ref, rows):
        acc = ref[0:rows, :]
        for d in range(1, N_DEV):
            acc = acc + ref[d * rows:(d + 1) * rows, :]
        return acc

    def body(*refs):
        early_ref, gmat_ref, late_ref = refs[:3]
        w_refs, m_refs, v_refs = refs[3:3 + n], refs[3 + n:3 + 2 * n], refs[3 + 2 * n:3 + 3 * n]
        outs = refs[3 + 3 * n:]
        loss_ref, g_refs, d_refs = outs[0], outs[1:1 + n], outs[1 + n:1 + 2 * n]
        nm_refs, nv_refs = outs[1 + 2 * n:1 + 3 * n], outs[1 + 3 * n:1 + 4 * n]
        early, late = total(early_ref, 8), total(late_ref, 8)
        loss_ref[...] = jnp.sum(early[4:5, LANES:2 * LANES], axis=1, keepdims=True)
        bias = jnp.concatenate([late[r:r + 1, :] for r in range(B_ROWS - 1)]
                               + [late[B_ROWS - 1:B_ROWS, :IN_WIDTH - (B_ROWS - 1) * D_MODEL]], axis=1)
        grad = dict(b_in=bias, g_mix_pre=late[B_ROWS:B_ROWS + 1, :], g_mix_post=early[0:1, :],
                    g_mlp_pre=early[1:2, :], g_mlp_post=early[2:3, :], pool_scale=early[3:4, :POOL_WIDTH],
                    attn_sinks=early[4:5, :N_Q_HEADS])
        for i, name in enumerate(names):
            g = total(gmat_ref, 4 * POOL_GC) if name == "w_pool" else grad[name]
            g_refs[i][...] = g
            d_refs[i][...], nm_refs[i][...], nv_refs[i][...] = _adamw_math(
                w_refs[i][...], g, m_refs[i][...], v_refs[i][...])

    shapes = [_sds(w[k].shape, F32) for k in names]
    res = pl.pallas_call(
        body, name="small_update", out_shape=[_sds((1, 1), F32)] + shapes * 4,
        compiler_params=pltpu.CompilerParams(vmem_limit_bytes=VMEM_MB * 1024 * 1024),
    )(gearly, gmat, glate, *[w[k] for k in names], *[m[k] for k in names], *[v[k] for k in names])
    loss = res[0]
    per = {k: tuple(res[1 + j * n + i] for j in range(4)) for i, k in enumerate(names)}
    return loss, per


_BIG = ("w_in", "w_branch_pool", "w_branch_attn", "w_out", "w_up", "w_down")
_ORDER = ("g_mix_pre", "w_in", "b_in", "w_pool", "pool_scale", "attn_sinks", "w_branch_pool", "w_branch_attn",
          "w_out", "g_mix_post", "g_mlp_pre", "w_up", "w_down", "g_mlp_post")


def _stack_rows(slab):
    return slab.reshape(-1, slab.shape[2])


def _step(x2, tgt, seq, shards, small, ids):
    tabs = _rope_tables(seq)
    g1, g2, g3, g4 = (small[n] for n in ("g_mix_pre", "g_mix_post", "g_mlp_pre", "g_mlp_post"))
    sinks = small["attn_sinks"].reshape(N_Q_HEADS)
    w_pool = small["w_pool"].reshape(4, POOL_GC, POOL_GC)
    pool_scale = small["pool_scale"]

    def whole(shard, slabs):
        return lax.dynamic_update_slice(slabs, shard[None], (ids[0], 0, 0))

    (up_a, up_b, down_a, down_b, *mix_shards), [[in_slab]] = _cast_shards(
        *(shards[n] for n in ("w_up", "w_down", "w_branch_pool", "w_branch_attn", "w_out")),
        exchanges=[_ex_gather([shards["w_in"]])])
    w_in = _stack_rows(whole(shards["w_in"], in_slab))
    (h, u, q, k, v, gate), [mix_slabs] = _inproj(
        x2, g1, w_in, small["b_in"], tabs, seq, exchanges=[_ex_gather(mix_shards)])
    w_bp, w_ba, out_slab = (whole(s, g) for s, g in zip(mix_shards, mix_slabs))
    w_out = _stack_rows(out_slab)
    (y_attn, diff, y_pool), [[got_a, got_b]] = _mixers_fwd(
        q, k, v, sinks, u, w_pool, pool_scale, seq, exchanges=[_ex_gather([up_a, up_b])])
    (merged, mix, x1, h2), [[got_c, got_d]] = _merge_out(
        y_pool, y_attn, gate, x2, w_bp, w_ba, w_out, g2, g3, exchanges=[_ex_gather([down_a, down_b])])
    w_up = (whole(up_a, got_a), whole(up_b, got_b))
    w_down = (whole(down_a, got_c), whole(down_b, got_d))
    act, dff, dup, dx1, dmix, loss_acc, dg4, dg3, dg2 = _mlp_core(h2, x1, mix, tgt, w_up, w_down, g4, g3, g2)

    dw_down = _dw("down", act, dff, 1024, 1024)[0].reshape(N_CHIPS, D_FF // N_CHIPS, D_MODEL)
    (dbp, dba, dgate, dyp, dya), [[got]] = _merge_bwd(
        dmix, gate, y_pool, y_attn, w_out, w_bp, w_ba, exchanges=[_ex_pair([dw_down])])
    ps_down = _pair_sum(ids, dw_down, got)
    (dw_up,), [[near]] = _dw("up", h2, dup, 1024, 1024, shard_cols=True, exchanges=[_ex_chip([ps_down[1]], (0, 1))])
    (dw_out, dw_bp, dw_ba), [[got], [far]] = _dw_mix(
        merged, dmix, y_pool, dbp, y_attn, dba, exchanges=[_ex_pair([dw_up]), _ex_chip([ps_down[1]], (2,))])
    half_down = _chip_sum(ids, ps_down[0], near, far)
    ps_up = _pair_sum(ids, dw_up, got)
    dw_mix = [dw_out.reshape(N_CHIPS, D_MODEL // N_CHIPS, D_MODEL), dw_bp, dw_ba]
    (dq, dk, dv, dsink, du, dw_pool, dps), [[got], gots, [g_down]] = _mixers_bwd(
        q, k, v, dya, sinks, tabs, dyp, diff, w_pool, pool_scale, seq,
        exchanges=[_ex_chip([ps_up[1]]), _ex_pair(dw_mix), _ex_swap([half_down])])
    half_up = _chip_sum(ids, ps_up[0], got)
    ps_mix = _pair_sum_small(ids, dw_mix, gots)
    parts = (du, dq, dk, dv, dgate)
    early = _early_block(dg2, dg3, dg4, dps, dsink[:, 0], loss_acc[0, 0])
    mat = dw_pool.reshape(4 * POOL_GC, POOL_GC)
    (dw_in_t, db_in), [gots, [gearly, gmat], [g_up]] = _dw_in(
        h, parts, exchanges=[_ex_chip([p[1] for p in ps_mix]), _ex_allgather([early, mat]), _ex_swap([half_up])])
    half_mix = _chip_sum_small(ids, [p[0] for p in ps_mix], gots)
    dw_in = dw_in_t.reshape(N_CHIPS, IN_WIDTH // N_CHIPS, D_MODEL)
    g_mix, [got] = _alone("swap_mix_pair_in", _ex_swap(half_mix), _ex_pair([dw_in]))
    ps_in = _pair_sum(ids, dw_in, got)
    (gx, dg1), [[got]] = _inproj_bwd(parts, x2, dx1, w_in, g1, exchanges=[_ex_chip([ps_in[1]])])
    [g_in], [glate] = _alone("swap_in_allgather", _ex_swap([_chip_sum(ids, ps_in[0], got)]),
                             _ex_allgather([_late_block(db_in, dg1)]))

    grads = dict(w_in=g_in, w_branch_pool=g_mix[1], w_branch_attn=g_mix[2], w_out=g_mix[0], w_up=g_up, w_down=g_down)
    return (gearly, gmat, glate), gx, grads


def kernel(x, g_mix_pre, w_in, b_in, w_pool, pool_scale, attn_sinks, w_branch_pool, w_branch_attn, w_out, g_mix_post, g_mlp_pre, w_up, w_down, g_mlp_post, loss_target, m_g_mix_pre, m_w_in, m_b_in, m_w_pool, m_pool_scale, m_attn_sinks, m_w_branch_pool, m_w_branch_attn, m_w_out, m_g_mix_post, m_g_mlp_pre, m_w_up, m_w_down, m_g_mlp_post, v_g_mix_pre, v_w_in, v_b_in, v_w_pool, v_pool_scale, v_attn_sinks, v_w_branch_pool, v_w_branch_attn, v_w_out, v_g_mix_post, v_g_mlp_pre, v_w_up, v_w_down, v_g_mlp_post):
    weights = dict(g_mix_pre=g_mix_pre, w_in=w_in, b_in=b_in, w_pool=w_pool, pool_scale=pool_scale,
                   attn_sinks=attn_sinks, w_branch_pool=w_branch_pool, w_branch_attn=w_branch_attn, w_out=w_out,
                   g_mix_post=g_mix_post, g_mlp_pre=g_mlp_pre, w_up=w_up, w_down=w_down, g_mlp_post=g_mlp_post)
    mom1 = dict(g_mix_pre=m_g_mix_pre, w_in=m_w_in, b_in=m_b_in, w_pool=m_w_pool, pool_scale=m_pool_scale,
                attn_sinks=m_attn_sinks, w_branch_pool=m_w_branch_pool, w_branch_attn=m_w_branch_attn,
                w_out=m_w_out, g_mix_post=m_g_mix_post, g_mlp_pre=m_g_mlp_pre, w_up=m_w_up, w_down=m_w_down,
                g_mlp_post=m_g_mlp_post)
    mom2 = dict(g_mix_pre=v_g_mix_pre, w_in=v_w_in, b_in=v_b_in, w_pool=v_w_pool, pool_scale=v_pool_scale,
                attn_sinks=v_attn_sinks, w_branch_pool=v_w_branch_pool, w_branch_attn=v_w_branch_attn,
                w_out=v_w_out, g_mix_post=v_g_mix_post, g_mlp_pre=v_g_mlp_pre, w_up=v_w_up, w_down=v_w_down,
                g_mlp_post=v_g_mlp_post)
    b_loc, seq, _ = x.shape
    x2 = x.reshape(b_loc * seq, D_MODEL)
    tgt = loss_target.reshape(b_loc * seq, D_MODEL)
    ids = jnp.stack([2 * lax.axis_index("x") + lax.axis_index("y"), lax.axis_index("c")]).astype(jnp.int32)

    def flat(n, a):
        return a[0].T if n == "w_in" else a[0]

    def unflat(n, a):
        return (a.T if n == "w_in" else a)[None]

    shards = {n: flat(n, weights[n]).astype(BF16) if n == "w_in" else flat(n, weights[n]) for n in _BIG}
    small = {n: weights[n] for n in _ORDER if n not in _BIG}
    (gearly, gmat, glate), gx, grads = _step(x2, tgt, seq, shards, small, ids)

    def two_d(src):
        return {n: src[n].reshape(4 * POOL_GC, POOL_GC) if n == "w_pool" else src[n] for n in _SMALL_NAMES}

    loss, per = _small_update(gearly, gmat, glate, two_d(weights), two_d(mom1), two_d(mom2))
    delta, new_m, new_v = {}, {}, {}
    for n in _SMALL_NAMES:
        grads[n], delta[n], new_m[n], new_v[n] = (a.reshape(weights[n].shape) for a in per[n])
    for n in _BIG:
        update = _adamw if n == "w_in" else _adamw_sparse
        d, nm, nv = update(flat(n, weights[n]), grads[n], flat(n, mom1[n]), flat(n, mom2[n]))
        grads[n] = unflat(n, grads[n])
        delta[n], new_m[n], new_v[n] = unflat(n, d), unflat(n, nm), unflat(n, nv)

    return (loss[0, 0], gx.reshape(x.shape), *[grads[n] for n in _ORDER], *[delta[n] for n in _ORDER],
            *[new_m[n] for n in _ORDER], *[new_v[n] for n in _ORDER])
```

```python
import jax
import jax.numpy as jnp
from jax import lax
from jax.experimental import pallas as pl
from jax.experimental.pallas import tpu as pltpu
from jax.experimental.pallas import tpu_sc as plsc

F32 = jnp.float32
BF16 = jnp.bfloat16

D_MODEL = 1024
POOL_WINDOWS = (2, 4, 8, 16)
POOL_WIDTH = 512
POOL_GC = 128
HALO = 16
HEAD_DIM = 64
N_Q_HEADS = 8
ATTN_WIDTH = 512
KV_WIDTH = 128
BLOCK = 128
NEG_INF = -1e30
ROPE_THETA = 500000.0
ROT_DIM = 16
GATE_WIDTH = 2048
IN_WIDTH = 3328
D_FF = 4096
EPS = 1e-6
SCALE = HEAD_DIM ** -0.5
C_Q, C_K, C_V, C_G = 512, 1024, 1152, 1280

ADAM_LR, ADAM_B1, ADAM_B2, ADAM_EPS, ADAM_WD, ADAM_STEP = 0.001, 0.9, 0.999, 1e-08, 0.01, 10

N_CHIPS = 4
N_DEV = 8
LANES = 128
TM = 512
VMEM_MB = 56

MESH = pl.DeviceIdType.MESH
ANY = pl.BlockSpec(memory_space=pl.ANY)


def _cp(*sem, vmem=VMEM_MB):
    return pltpu.CompilerParams(dimension_semantics=sem, vmem_limit_bytes=vmem * 1024 * 1024)


def _rows(tile, cols):
    return pl.BlockSpec((tile, cols), lambda i: (i, 0))


def _const(shape):
    nd = len(shape)
    return pl.BlockSpec(shape, lambda i: (0,) * nd)


def _sds(shape, dtype):
    return jax.ShapeDtypeStruct(shape, dtype)


def _dot(a, b):
    return jnp.dot(a, b, preferred_element_type=F32)


def _dot_nt(a, b):
    return lax.dot_general(a, b, (((1,), (1,)), ((), ())), preferred_element_type=F32)


def _dot_tn(a, b):
    return lax.dot_general(a, b, (((0,), (0,)), ((), ())), preferred_element_type=F32)


def _rms(x):
    return lax.rsqrt(jnp.mean(x * x, axis=-1, keepdims=True) + EPS)


def _norm_bwd(x, g, dout):
    r = _rms(x)
    n = x * r
    dn = dout * g
    dx = r * (dn - n * jnp.mean(dn * n, axis=-1, keepdims=True))
    return dx, jnp.sum(dout * n, axis=0, keepdims=True)


def _rot_fwd(t, c, a, bt):
    return t * c + pltpu.roll(t, LANES - 8, 1) * a + pltpu.roll(t, 8, 1) * bt


def _rot_bwd(d, c, a, bt):
    return d * c + pltpu.roll(d * a, 8, 1) + pltpu.roll(d * bt, LANES - 8, 1)


def _rope_tables(seq):
    pos = jnp.arange(seq, dtype=F32)
    inv_freq = ROPE_THETA ** (-jnp.arange(0, ROT_DIM, 2, dtype=F32) / ROT_DIM)
    ang = pos[:, None] * inv_freq[None, :]
    cos, sin = jnp.cos(ang), jnp.sin(ang)
    ones = jnp.ones((seq, HEAD_DIM - ROT_DIM), F32)
    zeros8 = jnp.zeros((seq, 8), F32)
    zrest = jnp.zeros((seq, HEAD_DIM - ROT_DIM), F32)
    c = jnp.concatenate([cos, cos, ones], axis=1)
    a = jnp.concatenate([-sin, zeros8, zrest], axis=1)
    bt = jnp.concatenate([zeros8, sin, zrest], axis=1)
    return tuple(jnp.tile(t, (1, 2)) for t in (c, a, bt))


class _Exchange:
    def __init__(self, inputs, out_shapes, sems, start, finish, aliases=None, middle=None):
        self.inputs, self.out_shapes, self.sems = list(inputs), list(out_shapes), list(sems)
        self.start, self.finish, self.aliases = start, finish, dict(aliases or {})
        self.middle = middle


def _call(body, *, name, grid, in_specs, out_specs, out_shape, args, scratch=(), sem=(), exchanges=()):
    in_specs, out_specs, out_shape, scratch = list(in_specs), list(out_specs), list(out_shape), list(scratch)
    if not exchanges:
        return pl.pallas_call(body, name=name, grid=grid, in_specs=in_specs, out_specs=out_specs,
                              out_shape=out_shape, scratch_shapes=scratch, compiler_params=_cp(*sem))(*args)
    n_in, n_out, n_scr = len(in_specs), len(out_specs), len(scratch)
    x_in = [a for ex in exchanges for a in ex.inputs]
    x_out = [s for ex in exchanges for s in ex.out_shapes]
    x_sem = [s for ex in exchanges for s in ex.sems]
    aliases, i_off, o_off = {}, n_in, n_out
    for ex in exchanges:
        for i, o in ex.aliases.items():
            aliases[i_off + i] = o_off + o
        i_off += len(ex.inputs)
        o_off += len(ex.out_shapes)

    def split(flat):
        out, pos = [], 0
        for ex, n in zip(exchanges, flat[1]):
            out.append(flat[0][pos:pos + n])
            pos += n
        return out

    def carrier(*refs):
        pos = 0
        groups = []
        for n in (n_in, len(x_in), n_out, len(x_out), n_scr, len(x_sem)):
            groups.append(refs[pos:pos + n])
            pos += n
        ins, xin, outs, xout, scr, xsem = groups
        xin = split((xin, [len(ex.inputs) for ex in exchanges]))
        xout = split((xout, [len(ex.out_shapes) for ex in exchanges]))
        xsem = split((xsem, [len(ex.sems) for ex in exchanges]))
        first = pl.program_id(0) == 0
        last = pl.program_id(0) == grid[0] - 1
        for d in range(1, len(grid)):
            first = jnp.logical_and(first, pl.program_id(d) == 0)
            last = jnp.logical_and(last, pl.program_id(d) == grid[d] - 1)

        @pl.when(first)
        def _():
            for ex, i, o, s in zip(exchanges, xin, xout, xsem):
                ex.start(i, o, s)

        if any(ex.middle for ex in exchanges):
            step, total = pl.program_id(0), grid[0]
            for d in range(1, len(grid)):
                step, total = step * grid[d] + pl.program_id(d), total * grid[d]
            half = step == 5 * total // 8

            @pl.when(half)
            def _():
                for ex, i, o, s in zip(exchanges, xin, xout, xsem):
                    if ex.middle:
                        ex.middle(i, o, s)

        body(*ins, *outs, *scr)

        @pl.when(last)
        def _():
            for ex, i, o, s in zip(exchanges, xin, xout, xsem):
                ex.finish(i, o, s)

    res = pl.pallas_call(
        carrier, name=name, grid=grid, in_specs=in_specs + [ANY] * len(x_in),
        out_specs=out_specs + [ANY] * len(x_out), out_shape=out_shape + x_out,
        scratch_shapes=scratch + x_sem, input_output_aliases=aliases,
        compiler_params=_cp(*(["arbitrary"] * len(grid))),
    )(*args, *x_in)
    return res[:n_out], split((res[n_out:], [len(ex.out_shapes) for ex in exchanges]))


def _alone(name, *exchanges):
    n_in = [len(ex.inputs) for ex in exchanges]
    n_out = [len(ex.out_shapes) for ex in exchanges]
    n_sem = [len(ex.sems) for ex in exchanges]
    aliases, i_off, o_off = {}, 0, 0
    for ex in exchanges:
        for i, o in ex.aliases.items():
            aliases[i_off + i] = o_off + o
        i_off += len(ex.inputs)
        o_off += len(ex.out_shapes)

    def split(flat, counts):
        out, pos = [], 0
        for n in counts:
            out.append(flat[pos:pos + n])
            pos += n
        return out

    def body(*refs):
        ins, outs, sems = split(refs, [sum(n_in), sum(n_out), sum(n_sem)])
        groups = list(zip(exchanges, split(ins, n_in), split(outs, n_out), split(sems, n_sem)))
        for ex, i, o, s in groups:
            ex.start(i, o, s)
        for ex, i, o, s in groups:
            if ex.middle:
                ex.middle(i, o, s)
        for ex, i, o, s in groups:
            ex.finish(i, o, s)

    res = pl.pallas_call(
        body, name=name, in_specs=[ANY] * sum(n_in), out_specs=[ANY] * sum(n_out),
        out_shape=[s for ex in exchanges for s in ex.out_shapes],
        scratch_shapes=[s for ex in exchanges for s in ex.sems], input_output_aliases=aliases,
    )(*[a for ex in exchanges for a in ex.inputs])
    return split(res, n_out)


def _place():
    x, y, c = lax.axis_index("x"), lax.axis_index("y"), lax.axis_index("c")
    chips = [(1 - x, y), (x, 1 - y), (1 - x, 1 - y)]
    return x, y, c, chips


def _remote(src, dst, send, recv, to):
    return pltpu.make_async_remote_copy(src_ref=src, dst_ref=dst, send_sem=send, recv_sem=recv,
                                        device_id=to, device_id_type=MESH)


def _ex_gather(shards):
    nw = len(shards)
    hrs = [s.shape[0] // 2 for s in shards]

    def copies(ins, outs, sems):
        s1, r1, s2, r2, fs, fr = sems
        x, y, c, _ = _place()
        me, xn, yn, dg = (x, y), (1 - x, y), (x, 1 - y), (1 - x, 1 - y)
        nbr = (xn, yn)
        sibling = (x, y, 1 - c)

        def piece(w, chip, core, part=None):
            hr = hrs[w]
            rows = pl.ds(core * hr, hr) if part is None else pl.ds(core * hr + part * (hr // 2), hr // 2)
            return outs[w].at[2 * chip[0] + chip[1], rows]

        def first(w, k):
            return _remote(ins[w].at[pl.ds(c * hrs[w], hrs[w])], piece(w, me, c), s1.at[w, k], r1.at[w, k],
                           (*nbr[k], c))

        def landed(w, k):
            return _remote(piece(w, nbr[k], c), piece(w, nbr[k], c), s1.at[w, k], r1.at[w, k], (*nbr[k], c))

        def onward(w, k):
            return _remote(piece(w, nbr[k], c, k), piece(w, nbr[k], c, k), s2.at[w, k], r2.at[w, k],
                           (*nbr[1 - k], c))

        def arrived(w, k):
            return _remote(piece(w, dg, c, k), piece(w, dg, c, k), s2.at[w, k], r2.at[w, k], (*nbr[1 - k], c))

        def passed(w, j):
            chip = (xn, yn, dg)[j]
            return _remote(piece(w, chip, c), piece(w, chip, c), fs.at[w, j], fr.at[w, j], sibling)

        def handed(w, j):
            chip = (xn, yn, dg)[j]
            return _remote(piece(w, chip, 1 - c), piece(w, chip, 1 - c), fs.at[w, j], fr.at[w, j], sibling)

        return first, landed, onward, arrived, passed, handed

    def start(ins, outs, sems):
        first = copies(ins, outs, sems)[0]
        for w in range(nw):
            for k in range(2):
                first(w, k).start()

    def middle(ins, outs, sems):
        _, landed, onward, _, passed, _ = copies(ins, outs, sems)
        for w in range(nw):
            for k in range(2):
                landed(w, k).wait_recv()
                onward(w, k).start()
                passed(w, k).start()

    def finish(ins, outs, sems):
        first, _, onward, arrived, passed, handed = copies(ins, outs, sems)
        for w in range(nw):
            for k in range(2):
                arrived(w, k).wait_recv()
            passed(w, 2).start()
        for w in range(nw):
            for j in range(3):
                handed(w, j).wait_recv()
        for w in range(nw):
            for k in range(2):
                first(w, k).wait_send()
                onward(w, k).wait_send()
            for j in range(3):
                passed(w, j).wait_send()

    return _Exchange(shards, [_sds((N_CHIPS,) + s.shape, s.dtype) for s in shards],
                     [pltpu.SemaphoreType.DMA((nw, 2))] * 4 + [pltpu.SemaphoreType.DMA((nw, 3))] * 2,
                     start, finish, middle=middle)


def _ex_pair(grads):
    nw = len(grads)

    def copies(ins, outs, sems):
        x, y, c, _ = _place()
        out = []
        for w in range(nw):
            hr = grads[w].shape[1] // 2
            out.append(_remote(ins[w].at[:, pl.ds((1 - c) * hr, hr)], outs[w], sems[0].at[w], sems[1].at[w],
                               (x, y, 1 - c)))
        return out

    def start(ins, outs, sems):
        for cp in copies(ins, outs, sems):
            cp.start()

    def finish(ins, outs, sems):
        for cp in copies(ins, outs, sems):
            cp.wait()

    return _Exchange(grads, [_sds((N_CHIPS, g.shape[1] // 2, g.shape[2]), F32) for g in grads],
                     [pltpu.SemaphoreType.DMA((nw,))] * 2, start, finish)


def _ex_chip(pieces):
    nw = len(pieces)
    qs = [p.shape[1] // 2 for p in pieces]

    def copies(ins, outs, sems):
        s1, r1, s2, r2, s3, r3 = sems
        x, y, c, chips = _place()
        far = 2 * chips[2][0] + chips[2][1]

        def near(w, k):
            return _remote(ins[w].at[2 * chips[k][0] + chips[k][1]], outs[w].at[k], s1.at[w, k], r1.at[w, k],
                           (*chips[k], c))

        def out_leg(w, k):
            return _remote(ins[w].at[far, pl.ds(k * qs[w], qs[w])], outs[nw + w].at[k], s2.at[w, k], r2.at[w, k],
                           (*chips[k], c))

        def on_leg(w, k):
            return _remote(outs[nw + w].at[k], outs[w].at[2, pl.ds(k * qs[w], qs[w])], s3.at[w, k], r3.at[w, k],
                           (*chips[1 - k], c))

        return near, out_leg, on_leg

    def start(ins, outs, sems):
        near, out_leg, _ = copies(ins, outs, sems)
        for w in range(nw):
            for k in range(2):
                out_leg(w, k).start()
        for w in range(nw):
            for k in range(2):
                near(w, k).start()

    def middle(ins, outs, sems):
        _, out_leg, on_leg = copies(ins, outs, sems)
        for w in range(nw):
            for k in range(2):
                out_leg(w, k).wait_recv()
                on_leg(w, k).start()

    def finish(ins, outs, sems):
        near, out_leg, on_leg = copies(ins, outs, sems)
        for w in range(nw):
            for k in range(2):
                near(w, k).wait()
                out_leg(w, k).wait_send()
                on_leg(w, k).wait()

    return _Exchange(pieces, [_sds((3,) + p.shape[1:], BF16) for p in pieces]
                     + [_sds((2, q, p.shape[2]), BF16) for p, q in zip(pieces, qs)],
                     [pltpu.SemaphoreType.DMA((nw, 2))] * 6, start, finish, middle=middle)


def _ex_swap(fulls):
    nw = len(fulls)

    def start(ins, outs, sems):
        x, y, c, _ = _place()
        for w in range(nw):
            hr = fulls[w].shape[0] // 2
            mine = pl.ds(c * hr, hr)
            _remote(ins[w].at[mine], outs[w].at[mine], sems[0].at[w], sems[1].at[w], (x, y, 1 - c)).start()

    def finish(ins, outs, sems):
        x, y, c, _ = _place()
        for w in range(nw):
            hr = fulls[w].shape[0] // 2
            mine, theirs = pl.ds(c * hr, hr), pl.ds((1 - c) * hr, hr)
            _remote(ins[w].at[mine], outs[w].at[mine], sems[0].at[w], sems[1].at[w], (x, y, 1 - c)).wait_send()
            _remote(ins[w].at[theirs], outs[w].at[theirs], sems[0].at[w], sems[1].at[w], (x, y, 1 - c)).wait_recv()

    return _Exchange(fulls, [_sds(f.shape, F32) for f in fulls], [pltpu.SemaphoreType.DMA((nw,))] * 2,
                     start, finish, aliases={w: w for w in range(nw)})


def _ex_allgather(blocks):
    nb = len(blocks)

    def copies(ins, outs, sems):
        send, recv, lsem = sems
        x, y, c, chips = _place()
        me, sibling = (x, y, c), (x, y, 1 - c)

        def rows(b, px, py, pc):
            m_per = blocks[b].shape[0]
            return outs[b].at[pl.ds((4 * px + 2 * py + pc) * m_per, m_per), :]

        def copy(b, k, blk, to, src=None):
            return _remote(rows(b, *blk) if src is None else src, rows(b, *blk), send.at[b, k], recv.at[b, k], to)

        def mine(b):
            return pltpu.make_async_copy(ins[b], rows(b, *me), lsem.at[b])

        def first(b, k):
            return copy(b, k, me, sibling if k == 0 else (*chips[k - 1], c), src=ins[b])

        def passed(b, j):
            return copy(b, 4 + j, (*chips[j], c), sibling)

        def landed(b, j):
            return copy(b, 1 + j, (*chips[j], c), me)

        def handed(b, k):
            return copy(b, 0, sibling, me) if k == 0 else copy(b, 3 + k, (*chips[k - 1], 1 - c), me)

        return mine, first, passed, landed, handed

    def start(ins, outs, sems):
        mine, first, _, _, _ = copies(ins, outs, sems)
        for b in range(nb):
            mine(b).start()
            for k in range(4):
                first(b, k).start()

    def finish(ins, outs, sems):
        mine, first, passed, landed, handed = copies(ins, outs, sems)
        sent = []
        for b in range(nb):
            for j in range(3):
                landed(b, j).wait_recv()
                cp = passed(b, j)
                cp.start()
                sent.append(cp)
        for b in range(nb):
            for k in range(4):
                handed(b, k).wait_recv()
            for k in range(4):
                first(b, k).wait_send()
        for cp in sent:
            cp.wait_send()
        for b in range(nb):
            mine(b).wait()

    return _Exchange(blocks, [_sds((N_DEV * b.shape[0], b.shape[1]), F32) for b in blocks],
                     [pltpu.SemaphoreType.DMA((nb, 7)), pltpu.SemaphoreType.DMA((nb, 7)), pltpu.SemaphoreType.DMA((nb,))],
                     start, finish)


def _cast_shards(w_up, w_down, w_bp, w_ba, w_out, exchanges=()):
    r, c = w_up.shape
    tr = HALF // 2
    steps = r // tr

    def body(up_ref, down_ref, bp_ref, ba_ref, out_ref, ua_ref, ub_ref, da_ref, db_ref, bpo_ref, bao_ref, outo_ref):
        i = pl.program_id(0)

        @pl.when(i == 0)
        def _():
            for src, dst in ((bp_ref, bpo_ref), (ba_ref, bao_ref), (out_ref, outo_ref)):
                dst[...] = src[...].astype(BF16)

        @pl.when(i < steps // 2)
        def _():
            ua_ref[...] = up_ref[...].astype(BF16)
            da_ref[...] = down_ref[...].astype(BF16)

        @pl.when(i >= steps // 2)
        def _():
            ub_ref[...] = up_ref[...].astype(BF16)
            db_ref[...] = down_ref[...].astype(BF16)

    rows = _rows(tr, c)
    first = pl.BlockSpec((tr, c), lambda i: (jnp.minimum(i, steps // 2 - 1), 0))
    second = pl.BlockSpec((tr, c), lambda i: (jnp.maximum(i - steps // 2, 0), 0))
    half = _sds((HALF, c), BF16)
    return _call(
        body, name="cast_shards", grid=(steps,),
        in_specs=[rows, rows, _const(w_bp.shape), _const(w_ba.shape), _const(w_out.shape)],
        out_specs=[first, second, first, second, _const(w_bp.shape), _const(w_ba.shape), _const(w_out.shape)],
        out_shape=[half, half, half, half, _sds(w_bp.shape, BF16), _sds(w_ba.shape, BF16), _sds(w_out.shape, BF16)],
        args=(w_up, w_down, w_bp, w_ba, w_out), sem=("arbitrary",), exchanges=exchanges)


def _inproj(x2, g1, w_in_t, b_in, tabs, seq, exchanges=()):
    T = x2.shape[0]
    tm = min(TM, seq)
    nseq = seq // tm

    def body(x_ref, g_ref, w_ref, b_ref, c_ref, a_ref, bt_ref, h_ref, u_ref, q_ref, k_ref, v_ref, gate_ref):
        x = x_ref[...]
        h = (x * _rms(x) * g_ref[...]).astype(BF16)
        h_ref[...] = h

        def proj(lo, hi):
            return _dot_nt(h, w_ref[lo:hi, :]) + b_ref[:, lo:hi]

        c, a, bt = c_ref[...], a_ref[...], bt_ref[...]
        u_ref[...] = proj(0, C_Q)
        q = proj(C_Q, C_K)
        for p in range(4):
            sl = slice(LANES * p, LANES * (p + 1))
            q_ref[:, sl] = (_rot_fwd(q[:, sl], c, a, bt) * SCALE).astype(BF16)
        kv = proj(C_K, C_G)
        k_ref[...] = _rot_fwd(kv[:, :KV_WIDTH], c, a, bt).astype(BF16)
        v_ref[...] = kv[:, KV_WIDTH:].astype(BF16)
        for j in range(2):
            lo = C_G + D_MODEL * j
            gate_ref[:, D_MODEL * j:D_MODEL * (j + 1)] = jax.nn.sigmoid(proj(lo, lo + D_MODEL)).astype(BF16)

    tab = pl.BlockSpec((tm, LANES), lambda i: (i % nseq, 0))
    return _call(
        body, name="inproj", grid=(T // tm,),
        in_specs=[_rows(tm, D_MODEL), _const((1, D_MODEL)), _const((IN_WIDTH, D_MODEL)), _const((1, IN_WIDTH)),
                  tab, tab, tab],
        out_specs=[_rows(tm, D_MODEL), _rows(tm, POOL_WIDTH), _rows(tm, ATTN_WIDTH), _rows(tm, KV_WIDTH),
                   _rows(tm, KV_WIDTH), _rows(tm, GATE_WIDTH)],
        out_shape=[_sds((T, D_MODEL), BF16), _sds((T, POOL_WIDTH), F32), _sds((T, ATTN_WIDTH), BF16),
                   _sds((T, KV_WIDTH), BF16), _sds((T, KV_WIDTH), BF16), _sds((T, GATE_WIDTH), BF16)],
        args=(x2, g1, w_in_t, b_in, *tabs), sem=("parallel",), exchanges=exchanges)


def _inv_count(pos, w):
    return 1.0 / jnp.minimum(pos + 1, w).astype(F32)


def _pool_tile(i, tp, nseq, u_ref, prev_ref, w_ref, s_ref, diff_ref, y_ref):
    first = (i % nseq) == 0
    prev = jnp.where(first, 0.0, prev_ref[...])
    ext = jnp.concatenate([prev, u_ref[...]], axis=0)
    pos = (i % nseq) * tp + lax.broadcasted_iota(jnp.int32, (tp, 1), 0)
    for gi, w in enumerate(POOL_WINDOWS):
        sl = slice(POOL_GC * gi, POOL_GC * (gi + 1))
        xg = ext[:, sl]
        s = xg
        sh = 1
        while sh < w:
            s = s + pltpu.roll(s, sh, 0)
            sh *= 2
        pooled = s[HALO:] * _inv_count(pos, w)
        diff = (pooled - xg[HALO:]).astype(BF16)
        diff_ref[:, sl] = diff
        mixed = _dot(diff, w_ref[gi].astype(BF16))
        y_ref[:, sl] = (mixed * s_ref[:, sl]).astype(BF16)


def _pool_specs(tp):
    per = tp // HALO
    return [_rows(tp, POOL_WIDTH), pl.BlockSpec((HALO, POOL_WIDTH), lambda i: (jnp.maximum(i * per - 1, 0), 0)),
            _const((4, POOL_GC, POOL_GC)), _const((1, POOL_WIDTH))]


GROUP = 4
GROWS = GROUP * BLOCK


def _attn_masks(n):
    qi = lax.broadcasted_iota(jnp.int32, (GROWS, 2 * BLOCK), 0) % BLOCK
    kj = lax.broadcasted_iota(jnp.int32, (GROWS, 2 * BLOCK), 1)
    rel = qi + BLOCK - kj
    valid = (rel >= 0) & (rel < BLOCK) & (kj >= jnp.where(n > 0, 0, BLOCK))
    lo = lax.broadcasted_iota(jnp.int32, (BLOCK, LANES), 1) < HEAD_DIM
    return valid, lo


def _by_example(bl, *arrays):
    return [a.reshape(bl, a.shape[0] // bl, a.shape[1]) for a in arrays]


def _stack_heads(ref, h, lo):
    keep = lo if h == 0 else jnp.logical_not(lo)
    pieces = []
    for p in (2 * h, 2 * h + 1):
        xp = ref[:, LANES * p:LANES * (p + 1)].astype(F32)
        for e in range(2):
            t = xp if e == h else pltpu.roll(xp, HEAD_DIM, 1)
            pieces.append(jnp.where(keep, t, 0.0).astype(BF16))
    return jnp.concatenate(pieces, axis=0)


def _unstack_heads(stacked, h, lo):
    pairs = []
    for j in range(2):
        parts = []
        for e in range(2):
            t = stacked[BLOCK * (2 * j + e):BLOCK * (2 * j + e + 1)]
            parts.append(t if e == h else pltpu.roll(t, HEAD_DIM, 1))
        pairs.append(jnp.where(lo, parts[0], parts[1]))
    return pairs


def _sink_rows(sink_ref, h):
    head = lax.broadcasted_iota(jnp.int32, (GROWS, 1), 0) // BLOCK
    col = jnp.zeros((GROWS, 1), F32) + sink_ref[GROUP * h]
    for g in range(1, GROUP):
        col = jnp.where(head == g, sink_ref[GROUP * h + g], col)
    return col


def _group_probs(qs, kk, valid, sink):
    s = jnp.where(valid, _dot_nt(qs, kk), NEG_INF)
    m = jnp.maximum(jnp.max(s, axis=1, keepdims=True), sink)
    ex = jnp.exp(s - m)
    es = jnp.exp(sink - m)
    inv = 1.0 / (jnp.sum(ex, axis=1, keepdims=True) + es)
    return ex * inv, es * inv


def _mixers_fwd(q, k, v, sinks, u, w_pool, pool_scale, seq, exchanges=()):
    T = q.shape[0]
    nb = seq // BLOCK
    bl = T // seq
    tp = T // nb
    nseq = seq // tp

    def body(sink_ref, q_ref, kp_ref, kc_ref, vp_ref, vc_ref, u_ref, prev_ref, w_ref, s_ref, o_ref, diff_ref, y_ref):
        n = pl.program_id(0)
        valid, lo = _attn_masks(n)
        for b in range(bl):
            kk = jnp.concatenate([kp_ref[b], kc_ref[b]], axis=0)
            vv = jnp.concatenate([vp_ref[b], vc_ref[b]], axis=0)
            for h in range(2):
                qs = _stack_heads(q_ref.at[b], h, lo)
                pr, _ = _group_probs(qs, kk, valid, _sink_rows(sink_ref, h))
                o = _dot(pr.astype(BF16), vv)
                for j, pair in enumerate(_unstack_heads(o, h, lo)):
                    p = 2 * h + j
                    o_ref[b, :, LANES * p:LANES * (p + 1)] = pair.astype(BF16)
        _pool_tile(n, tp, nseq, u_ref, prev_ref, w_ref, s_ref, diff_ref, y_ref)

    cur = lambda n: (0, n, 0)
    prv = lambda n: (0, jnp.maximum(n - 1, 0), 0)
    kv = lambda m: pl.BlockSpec((bl, BLOCK, KV_WIDTH), m)
    res = _call(
        body, name="mixers_fwd", grid=(nb,),
        in_specs=[pl.BlockSpec(memory_space=pltpu.SMEM), pl.BlockSpec((bl, BLOCK, ATTN_WIDTH), cur),
                  kv(prv), kv(cur), kv(prv), kv(cur)] + _pool_specs(tp),
        out_specs=[pl.BlockSpec((bl, BLOCK, ATTN_WIDTH), cur), _rows(tp, POOL_WIDTH), _rows(tp, POOL_WIDTH)],
        out_shape=[_sds((bl, seq, ATTN_WIDTH), BF16), _sds((T, POOL_WIDTH), BF16), _sds((T, POOL_WIDTH), BF16)],
        args=(sinks, *_by_example(bl, q, k, k, v, v), u, u, w_pool, pool_scale), sem=("parallel",),
        exchanges=exchanges)
    outs, rest = res if exchanges else (res, None)
    return [outs[0].reshape(T, ATTN_WIDTH), outs[1], outs[2]], rest


def _branch(y, w_ref):
    return jnp.concatenate([_dot(y, w_ref[j]) for j in range(N_CHIPS)], axis=1)


def _merge_out(y_pool, y_attn, gate, x2, w_bp, w_ba, w_out, g2, g3, exchanges=()):
    T = x2.shape[0]
    tm = min(TM, T)

    def body(yp_ref, ya_ref, gate_ref, x_ref, wbp_ref, wba_ref, wo_ref, g2_ref, g3_ref,
             mg_ref, mix_ref, x1_ref, h2_ref):
        bp, ba = _branch(yp_ref[...], wbp_ref), _branch(ya_ref[...], wba_ref)
        merged = (gate_ref[:, :D_MODEL].astype(F32) * bp + gate_ref[:, D_MODEL:].astype(F32) * ba).astype(BF16)
        mg_ref[...] = merged
        mix = _dot(merged, wo_ref[...])
        mix_ref[...] = mix
        x1 = x_ref[...] + mix * _rms(mix) * g2_ref[...]
        x1_ref[...] = x1
        h2_ref[...] = (x1 * _rms(x1) * g3_ref[...]).astype(BF16)

    return _call(
        body, name="merge_out", grid=(T // tm,),
        in_specs=[_rows(tm, POOL_WIDTH), _rows(tm, ATTN_WIDTH), _rows(tm, GATE_WIDTH), _rows(tm, D_MODEL),
                  _const(w_bp.shape), _const(w_ba.shape), _const((D_MODEL, D_MODEL)),
                  _const((1, D_MODEL)), _const((1, D_MODEL))],
        out_specs=[_rows(tm, D_MODEL)] * 4,
        out_shape=[_sds((T, D_MODEL), BF16), _sds((T, D_MODEL), F32), _sds((T, D_MODEL), F32),
                   _sds((T, D_MODEL), BF16)],
        args=(y_pool, y_attn, gate, x2, w_bp, w_ba, w_out, g2, g3), sem=("parallel",), exchanges=exchanges)


HALF = D_MODEL // 2
TM_MLP = 256


def _mlp_core(h2, x1, mix, tgt, w_up, w_down, g4, g3, g2):
    T = h2.shape[0]
    tm = min(TM_MLP, T)

    def body(h_ref, x1_ref, mix_ref, t_ref, g_ref, g3_ref, g2_ref, ua_hbm, ub_hbm, da_hbm, db_hbm,
             act_ref, dff_ref, dup_ref, dx1_ref, dmix_ref, loss_ref, dg_ref, dg3_ref, dg2_ref,
             wu, wd, relu_scr, sems):
        def weight_copy(i):
            src, dst = ((ua_hbm, wu.at[:, :HALF]), (ub_hbm, wu.at[:, HALF:]),
                        (da_hbm, wd.at[:, :HALF]), (db_hbm, wd.at[:, HALF:]))[i]
            return pltpu.make_async_copy(src, dst, sems.at[i])

        @pl.when(pl.program_id(0) == 0)
        def _():
            for i in range(4):
                weight_copy(i).start()
            loss_ref[...] = jnp.zeros_like(loss_ref)
            for ref in (dg_ref, dg3_ref, dg2_ref):
                ref[...] = jnp.zeros_like(ref)
            weight_copy(0).wait()
            weight_copy(1).wait()

        h = h_ref[...]
        ff = None
        for j in range(N_CHIPS):
            lo = D_MODEL * j
            relu = jnp.maximum(_dot(h, wu[j]), 0.0)
            if j == 0:
                @pl.when(pl.program_id(0) == 0)
                def _():
                    weight_copy(2).wait()
                    weight_copy(3).wait()
            relu_scr[:, lo:lo + D_MODEL] = relu
            act = jnp.square(relu).astype(BF16)
            act_ref[:, lo:lo + D_MODEL] = act
            t = _dot(act, wd[j])
            ff = t if ff is None else ff + t
        g = g_ref[...]
        x1 = x1_ref[...]
        err = x1 + ff * _rms(ff) * g - t_ref[...]
        loss_ref[...] += jnp.sum(err * err) * (0.5 / D_MODEL)
        dy = err * (1.0 / D_MODEL)
        dff, dg = _norm_bwd(ff, g, dy)
        dg_ref[...] += dg
        dff = dff.astype(BF16)
        dff_ref[...] = dff
        dh2 = None
        for j in range(N_CHIPS):
            lo = D_MODEL * j
            dup = (_dot_nt(dff, wd[j]) * (2.0 * relu_scr[:, lo:lo + D_MODEL])).astype(BF16)
            dup_ref[:, lo:lo + D_MODEL] = dup
            t = _dot_nt(dup, wu[j])
            dh2 = t if dh2 is None else dh2 + t
        dx, dg3 = _norm_bwd(x1, g3_ref[...], dh2)
        dx1 = dy + dx
        dx1_ref[...] = dx1
        dg3_ref[...] += dg3
        dmix, dg2 = _norm_bwd(mix_ref[...], g2_ref[...], dx1)
        dmix_ref[...] = dmix.astype(BF16)
        dg2_ref[...] += dg2

    slabs = pltpu.VMEM((N_CHIPS, D_MODEL, D_MODEL), BF16)
    gain = _const((1, D_MODEL))
    return pl.pallas_call(
        body, name="mlp_core", grid=(T // tm,),
        in_specs=[_rows(tm, D_MODEL)] * 4 + [gain] * 3 + [ANY] * 4,
        out_specs=[_rows(tm, D_FF), _rows(tm, D_MODEL), _rows(tm, D_FF), _rows(tm, D_MODEL), _rows(tm, D_MODEL),
                   _const((8, LANES)), gain, gain, gain],
        out_shape=[_sds((T, D_FF), BF16), _sds((T, D_MODEL), BF16), _sds((T, D_FF), BF16), _sds((T, D_MODEL), F32),
                   _sds((T, D_MODEL), BF16), _sds((8, LANES), F32)] + [_sds((1, D_MODEL), F32)] * 3,
        scratch_shapes=[slabs] * 2 + [pltpu.VMEM((tm, D_FF), F32), pltpu.SemaphoreType.DMA((4,))],
        compiler_params=_cp("arbitrary"),
    )(h2, x1, mix, tgt, g4, g3, g2, *w_up, *w_down)


def _dw(tag, a, g, ta, tn, shard_cols=False, exchanges=()):
    T, ka = a.shape
    n = g.shape[1]
    tk = min(2 * TM, T)
    nk = T // tk

    def body(a_ref, g_ref, o_ref):
        @pl.when(pl.program_id(2) == 0)
        def _():
            o_ref[...] = jnp.zeros_like(o_ref)

        o_ref[...] += _dot_tn(a_ref[...], g_ref[...])

    if shard_cols:
        per = (n // N_CHIPS) // tn
        out_spec = pl.BlockSpec((None, ta, tn), lambda i, j, k: (j // per, i, j % per))
        out_shape = _sds((N_CHIPS, ka, n // N_CHIPS), F32)
    else:
        out_spec = pl.BlockSpec((ta, tn), lambda i, j, k: (i, j))
        out_shape = _sds((ka, n), F32)
    return _call(
        body, name="dw_" + tag, grid=(ka // ta, n // tn, nk),
        in_specs=[pl.BlockSpec((tk, ta), lambda i, j, k: (k, i)), pl.BlockSpec((tk, tn), lambda i, j, k: (k, j))],
        out_specs=[out_spec], out_shape=[out_shape],
        args=(a, g), sem=("parallel", "parallel", "arbitrary"), exchanges=exchanges)


def _dw_mix(merged, dmix, y_pool, dbp, y_attn, dba, exchanges=()):
    T = merged.shape[0]
    tk = min(2 * TM, T)
    c = D_MODEL // N_CHIPS

    def body(mg_ref, dmix_ref, yp_ref, dbp_ref, ya_ref, dba_ref, out_ref, bp_ref, ba_ref):
        @pl.when(pl.program_id(0) == 0)
        def _():
            for ref in (out_ref, bp_ref, ba_ref):
                ref[...] = jnp.zeros_like(ref)

        out_ref[...] += _dot_tn(mg_ref[...], dmix_ref[...])
        for y_ref, d_ref, o_ref in ((yp_ref, dbp_ref, bp_ref), (ya_ref, dba_ref, ba_ref)):
            res = _dot_tn(y_ref[...], d_ref[...])
            for j in range(N_CHIPS):
                o_ref[j] += res[:, c * j:c * (j + 1)]

    slabs = (N_CHIPS, POOL_WIDTH, c)
    return _call(
        body, name="dw_mix", grid=(T // tk,),
        in_specs=[_rows(tk, D_MODEL), _rows(tk, D_MODEL), _rows(tk, POOL_WIDTH), _rows(tk, D_MODEL),
                  _rows(tk, ATTN_WIDTH), _rows(tk, D_MODEL)],
        out_specs=[_const((D_MODEL, D_MODEL)), _const(slabs), _const(slabs)],
        out_shape=[_sds((D_MODEL, D_MODEL), F32), _sds(slabs, F32), _sds(slabs, F32)],
        args=(merged, dmix, y_pool, dbp, y_attn, dba), sem=("arbitrary",), exchanges=exchanges)


def _merge_bwd(dmix, gate, y_pool, y_attn, w_out, w_bp, w_ba, exchanges=()):
    T = dmix.shape[0]
    tm = min(TM, T)

    def body(dmix_ref, gate_ref, yp_ref, ya_ref, wo_ref, wbp_ref, wba_ref,
             dbp_ref, dba_ref, dgate_ref, dyp_ref, dya_ref):
        dm = _dot_nt(dmix_ref[...], wo_ref[...])
        for j, (y_ref, db_ref, w_ref, dy_ref) in enumerate(
                ((yp_ref, dbp_ref, wbp_ref, dyp_ref), (ya_ref, dba_ref, wba_ref, dya_ref))):
            sl = slice(D_MODEL * j, D_MODEL * (j + 1))
            gt = gate_ref[:, sl].astype(F32)
            db = (dm * gt).astype(BF16)
            db_ref[...] = db
            dgate_ref[:, sl] = (dm * _branch(y_ref[...], w_ref) * gt * (1.0 - gt)).astype(BF16)
            cw = D_MODEL // N_CHIPS
            dy = _dot_nt(db[:, :cw], w_ref[0])
            for c in range(1, N_CHIPS):
                dy = dy + _dot_nt(db[:, cw * c:cw * (c + 1)], w_ref[c])
            dy_ref[...] = dy.astype(dy_ref.dtype)

    return _call(
        body, name="merge_bwd", grid=(T // tm,),
        in_specs=[_rows(tm, D_MODEL), _rows(tm, GATE_WIDTH), _rows(tm, POOL_WIDTH), _rows(tm, ATTN_WIDTH),
                  _const((D_MODEL, D_MODEL)), _const(w_bp.shape), _const(w_ba.shape)],
        out_specs=[_rows(tm, D_MODEL), _rows(tm, D_MODEL), _rows(tm, GATE_WIDTH), _rows(tm, POOL_WIDTH),
                   _rows(tm, ATTN_WIDTH)],
        out_shape=[_sds((T, D_MODEL), BF16), _sds((T, D_MODEL), BF16), _sds((T, GATE_WIDTH), BF16),
                   _sds((T, POOL_WIDTH), F32), _sds((T, ATTN_WIDTH), BF16)],
        args=(dmix, gate, y_pool, y_attn, w_out, w_bp, w_ba), sem=("parallel",), exchanges=exchanges)


def _mixers_bwd(q, k, v, do, sinks, tabs, dyp, diff, w_pool, pool_scale, seq, exchanges=()):
    T = q.shape[0]
    nb = seq // BLOCK
    bl = T // seq
    steps = nb + 1
    tp = T // nb
    nseq = seq // tp
    per = tp // HALO
    last_halo = T // HALO - 1

    def body(sink_ref, q_ref, do_ref, kp_ref, kc_ref, vp_ref, vc_ref, c_ref, a_ref, bt_ref, cp_ref, ap_ref, btp_ref,
             dy_ref, nxt_ref, diff_ref, w_ref, s_ref,
             dq_ref, dk_ref, dv_ref, dsink_ref, du_ref, dw_ref, ds_ref, ck_ref, cv_ref):
        n = pl.program_id(0)

        @pl.when(n == 0)
        def _():
            for ref in (dsink_ref, ck_ref, cv_ref, dw_ref, ds_ref):
                ref[...] = jnp.zeros_like(ref)

        @pl.when(n < nb)
        def _():
            _pool_bwd_tile(n, tp, nseq, dy_ref, nxt_ref, diff_ref, w_ref, s_ref, du_ref, dw_ref, ds_ref)
            valid, lo = _attn_masks(n)
            for b in range(bl):
                kk = jnp.concatenate([kp_ref[b], kc_ref[b]], axis=0)
                vv = jnp.concatenate([vp_ref[b], vc_ref[b]], axis=0)
                dk_acc = jnp.zeros((2 * BLOCK, KV_WIDTH), F32)
                dv_acc = jnp.zeros((2 * BLOCK, KV_WIDTH), F32)
                for h in range(2):
                    qs = _stack_heads(q_ref.at[b], h, lo)
                    dos = _stack_heads(do_ref.at[b], h, lo)
                    pr, ps = _group_probs(qs, kk, valid, _sink_rows(sink_ref, h))
                    dp = _dot_nt(dos, vv)
                    delta = jnp.sum(pr * dp, axis=1, keepdims=True)
                    ds = (pr * (dp - delta)).astype(BF16)
                    dsk = ps * delta
                    for g in range(GROUP):
                        idx = GROUP * h + g
                        dsink_ref[idx:idx + 1, :] += (jnp.zeros((1, LANES), F32)
                                                      - jnp.sum(dsk[BLOCK * g:BLOCK * (g + 1)]))
                    dk_acc = dk_acc + _dot_tn(ds, qs)
                    dv_acc = dv_acc + _dot_tn(pr.astype(BF16), dos)
                    for j, pair in enumerate(_unstack_heads(_dot(ds, kk) * SCALE, h, lo)):
                        sl = slice(LANES * (2 * h + j), LANES * (2 * h + j + 1))
                        dq_ref[b, :, sl] = _rot_bwd(pair, c_ref[...], a_ref[...], bt_ref[...]).astype(BF16)
                fin_k = ck_ref[b] + dk_acc[:BLOCK]
                dk_ref[b] = _rot_bwd(fin_k, cp_ref[...], ap_ref[...], btp_ref[...]).astype(BF16)
                dv_ref[b] = (cv_ref[b] + dv_acc[:BLOCK]).astype(BF16)
                ck_ref[b] = dk_acc[BLOCK:]
                cv_ref[b] = dv_acc[BLOCK:]

        @pl.when(n == nb)
        def _():
            for b in range(bl):
                dk_ref[b] = _rot_bwd(ck_ref[b], cp_ref[...], ap_ref[...], btp_ref[...]).astype(BF16)
                dv_ref[b] = cv_ref[b].astype(BF16)

    cur = lambda n: (0, jnp.minimum(n, nb - 1), 0)
    prv = lambda n: (0, jnp.clip(n - 1, 0, nb - 1), 0)
    tcur = lambda n: (jnp.minimum(n, nb - 1), 0)
    tprv = lambda n: (jnp.clip(n - 1, 0, nb - 1), 0)
    wide = lambda m: pl.BlockSpec((bl, BLOCK, ATTN_WIDTH), m)
    kv = lambda m: pl.BlockSpec((bl, BLOCK, KV_WIDTH), m)
    tab = lambda m: pl.BlockSpec((BLOCK, LANES), m)
    tile = lambda n: (jnp.minimum(n, nb - 1), 0)
    halo = lambda n: (jnp.minimum((jnp.minimum(n, nb - 1) + 1) * per, last_halo), 0)
    rows = pl.BlockSpec((tp, POOL_WIDTH), tile)
    res = _call(
        body, name="mixers_bwd", grid=(steps,),
        in_specs=[pl.BlockSpec(memory_space=pltpu.SMEM), wide(cur), wide(cur), kv(prv), kv(cur), kv(prv), kv(cur),
                  tab(tcur), tab(tcur), tab(tcur), tab(tprv), tab(tprv), tab(tprv),
                  rows, pl.BlockSpec((HALO, POOL_WIDTH), halo), rows, _const((4, POOL_GC, POOL_GC)),
                  _const((1, POOL_WIDTH))],
        out_specs=[wide(cur), kv(prv), kv(prv), _const((8, LANES)), rows, _const((4, POOL_GC, POOL_GC)),
                   _const((1, POOL_WIDTH))],
        out_shape=[_sds((bl, seq, ATTN_WIDTH), BF16), _sds((bl, seq, KV_WIDTH), BF16),
                   _sds((bl, seq, KV_WIDTH), BF16), _sds((8, LANES), F32), _sds((T, POOL_WIDTH), BF16),
                   _sds((4, POOL_GC, POOL_GC), F32), _sds((1, POOL_WIDTH), F32)],
        scratch=[pltpu.VMEM((bl, BLOCK, KV_WIDTH), F32), pltpu.VMEM((bl, BLOCK, KV_WIDTH), F32)],
        args=(sinks, *_by_example(bl, q, do, k, k, v, v), *tabs, *tabs, dyp, dyp, diff, w_pool, pool_scale),
        sem=("arbitrary",), exchanges=exchanges)
    outs, rest = (res if exchanges else (res, None))
    outs = [outs[0].reshape(T, ATTN_WIDTH), outs[1].reshape(T, KV_WIDTH), outs[2].reshape(T, KV_WIDTH), *outs[3:]]
    return (outs, rest) if exchanges else outs


def _pool_bwd_tile(i, tp, nseq, dy_ref, nxt_ref, diff_ref, w_ref, s_ref, du_ref, dw_ref, ds_ref):
    last = (i % nseq) == nseq - 1
    nxt = jnp.where(last, 0.0, nxt_ref[...])
    ext = jnp.concatenate([dy_ref[...], nxt], axis=0) * s_ref[...]
    pos = (i % nseq) * tp + lax.broadcasted_iota(jnp.int32, (tp + HALO, 1), 0)
    for gi, w in enumerate(POOL_WINDOWS):
        sl = slice(POOL_GC * gi, POOL_GC * (gi + 1))
        wg = w_ref[gi].astype(BF16)
        dmx = ext[:, sl].astype(BF16)
        ddiff = _dot_nt(dmx, wg)
        s = ddiff * _inv_count(pos, w)
        sh = 1
        while sh < w:
            s = s + pltpu.roll(s, tp + HALO - sh, 0)
            sh *= 2
        du_ref[:, sl] = (s[:tp] - ddiff[:tp]).astype(BF16)
        dg = diff_ref[:, sl]
        dw_ref[gi] += _dot_tn(dg, dmx[:tp])
        ds_ref[:, sl] += jnp.sum(dy_ref[:, sl] * _dot(dg, wg), axis=0, keepdims=True)


_PARTS = ((0, C_Q), (C_Q, C_K), (C_K, C_V), (C_V, C_G), (C_G, IN_WIDTH))


def _inproj_bwd(parts, x2, dx1, w_in_t, g1, exchanges=()):
    T = x2.shape[0]
    tm = min(TM, T)

    def body(du_ref, dq_ref, dk_ref, dv_ref, dgt_ref, x_ref, dx1_ref, w_ref, g_ref, gx_ref, dg_ref):
        @pl.when(pl.program_id(0) == 0)
        def _():
            dg_ref[...] = jnp.zeros_like(dg_ref)

        dh = jnp.zeros((tm, D_MODEL), F32)
        for (lo, hi), p_ref in zip(_PARTS, (du_ref, dq_ref, dk_ref, dv_ref, dgt_ref)):
            dh = dh + _dot(p_ref[...], w_ref[lo:hi, :])
        dx, dg = _norm_bwd(x_ref[...], g_ref[...], dh)
        gx_ref[...] = dx1_ref[...] + dx
        dg_ref[...] += dg

    return _call(
        body, name="inproj_bwd", grid=(T // tm,),
        in_specs=[_rows(tm, hi - lo) for lo, hi in _PARTS]
        + [_rows(tm, D_MODEL), _rows(tm, D_MODEL), _const((IN_WIDTH, D_MODEL)), _const((1, D_MODEL))],
        out_specs=[_rows(tm, D_MODEL), _const((1, D_MODEL))],
        out_shape=[_sds((T, D_MODEL), F32), _sds((1, D_MODEL), F32)],
        args=(*parts, x2, dx1, w_in_t, g1), sem=("arbitrary",), exchanges=exchanges)


def _dw_in(h, parts, exchanges=()):
    T = h.shape[0]
    tk = min(TM, T)

    def body(h_ref, du_ref, dq_ref, dk_ref, dv_ref, dgt_ref, o_ref, db_ref):
        @pl.when(pl.program_id(0) == 0)
        def _():
            o_ref[...] = jnp.zeros_like(o_ref)
            db_ref[...] = jnp.zeros_like(db_ref)

        hh = h_ref[...]
        for (lo, hi), p_ref in zip(_PARTS, (du_ref, dq_ref, dk_ref, dv_ref, dgt_ref)):
            part = p_ref[...]
            o_ref[lo:hi, :] += _dot_tn(part, hh)
            db_ref[:, lo:hi] += jnp.sum(part.astype(F32), axis=0, keepdims=True)

    return _call(
        body, name="dw_in", grid=(T // tk,),
        in_specs=[_rows(tk, D_MODEL)] + [_rows(tk, hi - lo) for lo, hi in _PARTS],
        out_specs=[_const((IN_WIDTH, D_MODEL)), _const((1, IN_WIDTH))],
        out_shape=[_sds((IN_WIDTH, D_MODEL), F32), _sds((1, IN_WIDTH), F32)],
        args=(h, *parts), sem=("arbitrary",), exchanges=exchanges)


def _row_tile(rows, cap=256, mult=16):
    best = None
    for t in range(mult, min(rows, cap) + 1, mult):
        if rows % t == 0:
            best = t
    if best is None:
        raise ValueError("no row tile for %d rows" % rows)
    return best


def _pair_sum(ids, full, got):
    _, r, c = full.shape
    hr = r // 2
    tr = _row_tile(hr)
    nblk = hr // tr

    def body(ids_ref, a_ref, b_ref, own_ref, sb_ref):
        s = a_ref[...] + b_ref[...]
        sb_ref[...] = s.astype(BF16)

        @pl.when(pl.program_id(1) == ids_ref[0])
        def _():
            own_ref[...] = s

    slab = pl.BlockSpec((None, tr, c), lambda i, j, ids_ref: (j, i, 0))
    return pl.pallas_call(
        body, name="pair_sum_%dx%d" % (r, c),
        grid_spec=pltpu.PrefetchScalarGridSpec(
            num_scalar_prefetch=1, grid=(nblk, N_CHIPS),
            in_specs=[pl.BlockSpec((None, tr, c), lambda i, j, ids_ref: (j, ids_ref[1] * nblk + i, 0)), slab],
            out_specs=[pl.BlockSpec((tr, c), lambda i, j, ids_ref: (i, 0)), slab]),
        out_shape=[_sds((hr, c), F32), _sds((N_CHIPS, hr, c), BF16)],
        compiler_params=_cp("parallel", "arbitrary"),
    )(ids, full, got)


def _pair_sum_small(ids, fulls, gots):
    n = len(fulls)
    dims = [(f.shape[1] // 2, f.shape[2]) for f in fulls]

    def body(ids_ref, *refs):
        ins, outs = refs[:2 * n], refs[2 * n:]
        for k in range(n):
            s = ins[2 * k][...] + ins[2 * k + 1][...]
            outs[2 * k + 1][...] = s.astype(BF16)

            @pl.when(pl.program_id(0) == ids_ref[0])
            def _(k=k, s=s):
                outs[2 * k][...] = s

    in_specs, out_specs, out_shape = [], [], []
    for hr, c in dims:
        slab = pl.BlockSpec((None, hr, c), lambda j, ids_ref: (j, 0, 0))
        in_specs += [pl.BlockSpec((None, hr, c), lambda j, ids_ref: (j, ids_ref[1], 0)), slab]
        out_specs += [pl.BlockSpec((hr, c), lambda j, ids_ref: (0, 0)), slab]
        out_shape += [_sds((hr, c), F32), _sds((N_CHIPS, hr, c), BF16)]
    res = pl.pallas_call(
        body, name="pair_sum_small",
        grid_spec=pltpu.PrefetchScalarGridSpec(num_scalar_prefetch=1, grid=(N_CHIPS,), in_specs=in_specs,
                                               out_specs=out_specs),
        out_shape=out_shape, compiler_params=_cp("arbitrary"),
    )(ids, *[a for pair in zip(fulls, gots) for a in pair])
    return [(res[2 * k], res[2 * k + 1]) for k in range(n)]


def _chip_sum_small(ids, owns, gots):
    n = len(owns)

    def body(ids_ref, *refs):
        ins, outs = refs[:2 * n], refs[2 * n:]
        for k in range(n):
            a, b = ins[2 * k], ins[2 * k + 1]
            outs[k][...] = ((a[...] + b[0].astype(F32)) + b[1].astype(F32)) + b[2].astype(F32)

    in_specs, out_specs, out_shape = [], [], []
    for own in owns:
        hr, c = own.shape
        in_specs += [pl.BlockSpec((hr, c), lambda i, ids_ref: (0, 0)),
                     pl.BlockSpec((3, hr, c), lambda i, ids_ref: (0, 0, 0))]
        out_specs.append(pl.BlockSpec((hr, c), lambda i, ids_ref: (ids_ref[1], 0)))
        out_shape.append(_sds((2 * hr, c), F32))
    return pl.pallas_call(
        body, name="chip_sum_small",
        grid_spec=pltpu.PrefetchScalarGridSpec(num_scalar_prefetch=1, grid=(1,), in_specs=in_specs,
                                               out_specs=out_specs),
        out_shape=out_shape, compiler_params=_cp("arbitrary"),
    )(ids, *[a for pair in zip(owns, gots) for a in pair])


def _chip_sum(ids, own, got):
    hr, c = own.shape
    tr = _row_tile(hr)
    nblk = hr // tr

    def body(ids_ref, a_ref, b_ref, o_ref):
        o_ref[...] = ((a_ref[...] + b_ref[0].astype(F32)) + b_ref[1].astype(F32)) + b_ref[2].astype(F32)

    return pl.pallas_call(
        body, name="chip_sum_%dx%d" % (hr, c),
        grid_spec=pltpu.PrefetchScalarGridSpec(
            num_scalar_prefetch=1, grid=(nblk,),
            in_specs=[pl.BlockSpec((tr, c), lambda i, ids_ref: (i, 0)),
                      pl.BlockSpec((3, tr, c), lambda i, ids_ref: (0, i, 0))],
            out_specs=pl.BlockSpec((tr, c), lambda i, ids_ref: (ids_ref[1] * nblk + i, 0))),
        out_shape=_sds((2 * hr, c), F32),
        compiler_params=_cp("parallel"),
    )(ids, own, got)


def _adamw_math(w, g, m, v):
    nm = ADAM_B1 * m + (1.0 - ADAM_B1) * g
    nv = ADAM_B2 * v + (1.0 - ADAM_B2) * (g * g)
    m_hat = nm / (1.0 - ADAM_B1 ** ADAM_STEP)
    v_hat = nv / (1.0 - ADAM_B2 ** ADAM_STEP)
    return -ADAM_LR * (m_hat / (jnp.sqrt(v_hat) + ADAM_EPS) + ADAM_WD * w), nm, nv


def _adamw(w, g, m, v):
    r, c = w.shape
    tr = _row_tile(r, cap=512, mult=8)

    def body(w_ref, g_ref, m_ref, v_ref, d_ref, nm_ref, nv_ref):
        d_ref[...], nm_ref[...], nv_ref[...] = _adamw_math(w_ref[...], g_ref[...], m_ref[...], v_ref[...])

    spec = _rows(tr, c)
    return pl.pallas_call(
        body, name="adamw_%dx%d" % (r, c), grid=(r // tr,),
        in_specs=[spec] * 4, out_specs=[spec] * 3, out_shape=[_sds((r, c), F32)] * 3,
        compiler_params=_cp("parallel"),
    )(w, g, m, v)


SC_TILES = 32
SC_LANES = 16
SC_ROWS = 8


def _adamw_sparse(w, g, m, v):
    r, c = w.shape
    rows = r // SC_TILES
    step = min(rows, SC_ROWS)

    def body(w_hbm, g_hbm, m_hbm, v_hbm, d_hbm, nm_hbm, nv_hbm, wb, gb, mb, vb):
        tile = lax.axis_index("sc_subcore") * 2 + lax.axis_index("sc_core")

        @pl.loop(0, rows, step=step)
        def _(r0):
            mine = pl.ds(tile * rows + r0, step)
            for src, dst in ((w_hbm, wb), (g_hbm, gb), (m_hbm, mb), (v_hbm, vb)):
                pltpu.sync_copy(src.at[mine], dst)

            @pl.loop(0, step)
            def _(row):
                @pl.loop(0, c, step=SC_LANES)
                def _(i):
                    at = (row, pl.ds(i, SC_LANES))
                    wb[at], mb[at], vb[at] = _adamw_math(wb[at], gb[at], mb[at], vb[at])

            for src, dst in ((wb, d_hbm), (mb, nm_hbm), (vb, nv_hbm)):
                pltpu.sync_copy(src, dst.at[mine])

    return pl.kernel(
        body, name="adamw_sparse_%dx%d" % (r, c), out_type=[_sds((r, c), F32)] * 3,
        mesh=plsc.VectorSubcoreMesh(core_axis_name="sc_core", subcore_axis_name="sc_subcore"),
        scratch_types=[pltpu.VMEM((step, c), F32)] * 4,
    )(w, g, m, v)


_SMALL_NAMES = ("w_pool", "b_in", "g_mix_pre", "g_mix_post", "g_mlp_pre", "g_mlp_post", "pool_scale", "attn_sinks")
B_ROWS = -(-IN_WIDTH // D_MODEL)


def _row_block(rows):
    rows = [jnp.pad(r.astype(F32), ((0, 0), (0, D_MODEL - r.shape[1]))) for r in rows]
    return jnp.pad(jnp.concatenate(rows, axis=0), ((0, 8 - len(rows)), (0, 0)))


def _early_block(dg2, dg3, dg4, dps, dsink, loss):
    tail = jnp.concatenate([jnp.pad(dsink.reshape(1, -1), ((0, 0), (0, LANES - dsink.size))),
                            jnp.pad(loss.reshape(1, 1), ((0, 0), (0, LANES - 1)))], axis=1)
    return _row_block([dg2, dg3, dg4, dps, tail])


def _late_block(db_in, dg1):
    b = jnp.pad(db_in, ((0, 0), (0, B_ROWS * D_MODEL - IN_WIDTH))).reshape(B_ROWS, D_MODEL)
    return _row_block([b[r:r + 1] for r in range(B_ROWS)] + [dg1])


def _small_update(gearly, gmat, glate, w, m, v):
    names = _SMALL_NAMES
    n = len(names)

    def total(ref, rows):
        acc = ref[0:rows, :]
        for d in range(1, N_DEV):
            acc = acc + ref[d * rows:(d + 1) * rows, :]
        return acc

    def body(*refs):
        early_ref, gmat_ref, late_ref = refs[:3]
        w_refs, m_refs, v_refs = refs[3:3 + n], refs[3 + n:3 + 2 * n], refs[3 + 2 * n:3 + 3 * n]
        outs = refs[3 + 3 * n:]
        loss_ref, g_refs, d_refs = outs[0], outs[1:1 + n], outs[1 + n:1 + 2 * n]
        nm_refs, nv_refs = outs[1 + 2 * n:1 + 3 * n], outs[1 + 3 * n:1 + 4 * n]
        early, late = total(early_ref, 8), total(late_ref, 8)
        loss_ref[...] = jnp.sum(early[4:5, LANES:2 * LANES], axis=1, keepdims=True)
        bias = jnp.concatenate([late[r:r + 1, :] for r in range(B_ROWS - 1)]
                               + [late[B_ROWS - 1:B_ROWS, :IN_WIDTH - (B_ROWS - 1) * D_MODEL]], axis=1)
        grad = dict(b_in=bias, g_mix_pre=late[B_ROWS:B_ROWS + 1, :], g_mix_post=early[0:1, :],
                    g_mlp_pre=early[1:2, :], g_mlp_post=early[2:3, :], pool_scale=early[3:4, :POOL_WIDTH],
                    attn_sinks=early[4:5, :N_Q_HEADS])
        for i, name in enumerate(names):
            g = total(gmat_ref, 4 * POOL_GC) if name == "w_pool" else grad[name]
            g_refs[i][...] = g
            d_refs[i][...], nm_refs[i][...], nv_refs[i][...] = _adamw_math(
                w_refs[i][...], g, m_refs[i][...], v_refs[i][...])

    shapes = [_sds(w[k].shape, F32) for k in names]
    res = pl.pallas_call(
        body, name="small_update", out_shape=[_sds((1, 1), F32)] + shapes * 4,
        compiler_params=pltpu.CompilerParams(vmem_limit_bytes=VMEM_MB * 1024 * 1024),
    )(gearly, gmat, glate, *[w[k] for k in names], *[m[k] for k in names], *[v[k] for k in names])
    loss = res[0]
    per = {k: tuple(res[1 + j * n + i] for j in range(4)) for i, k in enumerate(names)}
    return loss, per


_BIG = ("w_in", "w_branch_pool", "w_branch_attn", "w_out", "w_up", "w_down")
_ORDER = ("g_mix_pre", "w_in", "b_in", "w_pool", "pool_scale", "attn_sinks", "w_branch_pool", "w_branch_attn",
          "w_out", "g_mix_post", "g_mlp_pre", "w_up", "w_down", "g_mlp_post")


def _stack_rows(slab):
    return slab.reshape(-1, slab.shape[2])


def _step(x2, tgt, seq, shards, small, ids):
    tabs = _rope_tables(seq)
    g1, g2, g3, g4 = (small[n] for n in ("g_mix_pre", "g_mix_post", "g_mlp_pre", "g_mlp_post"))
    sinks = small["attn_sinks"].reshape(N_Q_HEADS)
    w_pool = small["w_pool"].reshape(4, POOL_GC, POOL_GC)
    pool_scale = small["pool_scale"]

    def whole(shard, slabs):
        return lax.dynamic_update_slice(slabs, shard[None], (ids[0], 0, 0))

    (up_a, up_b, down_a, down_b, *mix_shards), [[in_slab]] = _cast_shards(
        *(shards[n] for n in ("w_up", "w_down", "w_branch_pool", "w_branch_attn", "w_out")),
        exchanges=[_ex_gather([shards["w_in"]])])
    w_in = _stack_rows(whole(shards["w_in"], in_slab))
    (h, u, q, k, v, gate), [mix_slabs] = _inproj(
        x2, g1, w_in, small["b_in"], tabs, seq, exchanges=[_ex_gather(mix_shards)])
    w_bp, w_ba, out_slab = (whole(s, g) for s, g in zip(mix_shards, mix_slabs))
    w_out = _stack_rows(out_slab)
    (y_attn, diff, y_pool), [[got_a, got_b]] = _mixers_fwd(
        q, k, v, sinks, u, w_pool, pool_scale, seq, exchanges=[_ex_gather([up_a, up_b])])
    (merged, mix, x1, h2), [[got_c, got_d]] = _merge_out(
        y_pool, y_attn, gate, x2, w_bp, w_ba, w_out, g2, g3, exchanges=[_ex_gather([down_a, down_b])])
    w_up = (whole(up_a, got_a), whole(up_b, got_b))
    w_down = (whole(down_a, got_c), whole(down_b, got_d))
    act, dff, dup, dx1, dmix, loss_acc, dg4, dg3, dg2 = _mlp_core(h2, x1, mix, tgt, w_up, w_down, g4, g3, g2)

    dw_down = _dw("down", act, dff, 1024, 1024)[0].reshape(N_CHIPS, D_FF // N_CHIPS, D_MODEL)
    (dbp, dba, dgate, dyp, dya), [[got]] = _merge_bwd(
        dmix, gate, y_pool, y_attn, w_out, w_bp, w_ba, exchanges=[_ex_pair([dw_down])])
    ps_down = _pair_sum(ids, dw_down, got)
    (dw_up,), [[got, _]] = _dw("up", h2, dup, 1024, 1024, shard_cols=True, exchanges=[_ex_chip([ps_down[1]])])
    half_down = _chip_sum(ids, ps_down[0], got)
    (dw_out, dw_bp, dw_ba), [[got]] = _dw_mix(merged, dmix, y_pool, dbp, y_attn, dba, exchanges=[_ex_pair([dw_up])])
    ps_up = _pair_sum(ids, dw_up, got)
    dw_mix = [dw_out.reshape(N_CHIPS, D_MODEL // N_CHIPS, D_MODEL), dw_bp, dw_ba]
    (dq, dk, dv, dsink, du, dw_pool, dps), [[got, _], gots, [g_down]] = _mixers_bwd(
        q, k, v, dya, sinks, tabs, dyp, diff, w_pool, pool_scale, seq,
        exchanges=[_ex_chip([ps_up[1]]), _ex_pair(dw_mix), _ex_swap([half_down])])
    half_up = _chip_sum(ids, ps_up[0], got)
    ps_mix = _pair_sum_small(ids, dw_mix, gots)
    parts = (du, dq, dk, dv, dgate)
    early = _early_block(dg2, dg3, dg4, dps, dsink[:, 0], loss_acc[0, 0])
    mat = dw_pool.reshape(4 * POOL_GC, POOL_GC)
    (dw_in_t, db_in), [gots, [gearly, gmat], [g_up]] = _dw_in(
        h, parts, exchanges=[_ex_chip([p[1] for p in ps_mix]), _ex_allgather([early, mat]), _ex_swap([half_up])])
    half_mix = _chip_sum_small(ids, [p[0] for p in ps_mix], gots[:len(ps_mix)])
    dw_in = dw_in_t.reshape(N_CHIPS, IN_WIDTH // N_CHIPS, D_MODEL)
    g_mix, [got] = _alone("swap_mix_pair_in", _ex_swap(half_mix), _ex_pair([dw_in]))
    ps_in = _pair_sum(ids, dw_in, got)
    (gx, dg1), [[got, _]] = _inproj_bwd(parts, x2, dx1, w_in, g1, exchanges=[_ex_chip([ps_in[1]])])
    [g_in], [glate] = _alone("swap_in_allgather", _ex_swap([_chip_sum(ids, ps_in[0], got)]),
                             _ex_allgather([_late_block(db_in, dg1)]))

    grads = dict(w_in=g_in, w_branch_pool=g_mix[1], w_branch_attn=g_mix[2], w_out=g_mix[0], w_up=g_up, w_down=g_down)
    return (gearly, gmat, glate), gx, grads


def kernel(x, g_mix_pre, w_in, b_in, w_pool, pool_scale, attn_sinks, w_branch_pool, w_branch_attn, w_out, g_mix_post, g_mlp_pre, w_up, w_down, g_mlp_post, loss_target, m_g_mix_pre, m_w_in, m_b_in, m_w_pool, m_pool_scale, m_attn_sinks, m_w_branch_pool, m_w_branch_attn, m_w_out, m_g_mix_post, m_g_mlp_pre, m_w_up, m_w_down, m_g_mlp_post, v_g_mix_pre, v_w_in, v_b_in, v_w_pool, v_pool_scale, v_attn_sinks, v_w_branch_pool, v_w_branch_attn, v_w_out, v_g_mix_post, v_g_mlp_pre, v_w_up, v_w_down, v_g_mlp_post):
    weights = dict(g_mix_pre=g_mix_pre, w_in=w_in, b_in=b_in, w_pool=w_pool, pool_scale=pool_scale,
                   attn_sinks=attn_sinks, w_branch_pool=w_branch_pool, w_branch_attn=w_branch_attn, w_out=w_out,
                   g_mix_post=g_mix_post, g_mlp_pre=g_mlp_pre, w_up=w_up, w_down=w_down, g_mlp_post=g_mlp_post)
    mom1 = dict(g_mix_pre=m_g_mix_pre, w_in=m_w_in, b_in=m_b_in, w_pool=m_w_pool, pool_scale=m_pool_scale,
                attn_sinks=m_attn_sinks, w_branch_pool=m_w_branch_pool, w_branch_attn=m_w_branch_attn,
                w_out=m_w_out, g_mix_post=m_g_mix_post, g_mlp_pre=m_g_mlp_pre, w_up=m_w_up, w_down=m_w_down,
                g_mlp_post=m_g_mlp_post)
    mom2 = dict(g_mix_pre=v_g_mix_pre, w_in=v_w_in, b_in=v_b_in, w_pool=v_w_pool, pool_scale=v_pool_scale,
                attn_sinks=v_attn_sinks, w_branch_pool=v_w_branch_pool, w_branch_attn=v_w_branch_attn,
                w_out=v_w_out, g_mix_post=v_g_mix_post, g_mlp_pre=v_g_mlp_pre, w_up=v_w_up, w_down=v_w_down,
                g_mlp_post=v_g_mlp_post)
    b_loc, seq, _ = x.shape
    x2 = x.reshape(b_loc * seq, D_MODEL)
    tgt = loss_target.reshape(b_loc * seq, D_MODEL)
    ids = jnp.stack([2 * lax.axis_index("x") + lax.axis_index("y"), lax.axis_index("c")]).astype(jnp.int32)

    def flat(n, a):
        return a[0].T if n == "w_in" else a[0]

    def unflat(n, a):
        return (a.T if n == "w_in" else a)[None]

    shards = {n: flat(n, weights[n]).astype(BF16) if n == "w_in" else flat(n, weights[n]) for n in _BIG}
    small = {n: weights[n] for n in _ORDER if n not in _BIG}
    (gearly, gmat, glate), gx, grads = _step(x2, tgt, seq, shards, small, ids)

    def two_d(src):
        return {n: src[n].reshape(4 * POOL_GC, POOL_GC) if n == "w_pool" else src[n] for n in _SMALL_NAMES}

    loss, per = _small_update(gearly, gmat, glate, two_d(weights), two_d(mom1), two_d(mom2))
    delta, new_m, new_v = {}, {}, {}
    for n in _SMALL_NAMES:
        grads[n], delta[n], new_m[n], new_v[n] = (a.reshape(weights[n].shape) for a in per[n])
    for n in _BIG:
        update = _adamw if n == "w_in" else _adamw_sparse
        d, nm, nv = update(flat(n, weights[n]), grads[n], flat(n, mom1[n]), flat(n, mom2[n]))
        grads[n] = unflat(n, grads[n])
        delta[n], new_m[n], new_v[n] = unflat(n, d), unflat(n, nm), unflat(n, nv)

    return (loss[0, 0], gx.reshape(x.shape), *[grads[n] for n in _ORDER], *[delta[n] for n in _ORDER],
            *[new_m[n] for n in _ORDER], *[new_v[n] for n in _ORDER])
```

```python
import jax
import jax.numpy as jnp
from jax import lax
from jax.experimental import pallas as pl
from jax.experimental.pallas import tpu as pltpu
from jax.experimental.pallas import tpu_sc as plsc

F32 = jnp.float32
BF16 = jnp.bfloat16

D_MODEL = 1024
POOL_WINDOWS = (2, 4, 8, 16)
POOL_WIDTH = 512
POOL_GC = 128
HALO = 16
HEAD_DIM = 64
N_Q_HEADS = 8
ATTN_WIDTH = 512
KV_WIDTH = 128
BLOCK = 128
NEG_INF = -1e30
ROPE_THETA = 500000.0
ROT_DIM = 16
GATE_WIDTH = 2048
IN_WIDTH = 3328
D_FF = 4096
EPS = 1e-6
SCALE = HEAD_DIM ** -0.5
C_Q, C_K, C_V, C_G = 512, 1024, 1152, 1280

ADAM_LR, ADAM_B1, ADAM_B2, ADAM_EPS, ADAM_WD, ADAM_STEP = 0.001, 0.9, 0.999, 1e-08, 0.01, 10

N_CHIPS = 4
N_DEV = 8
LANES = 128
TM = 512
VMEM_MB = 56

MESH = pl.DeviceIdType.MESH
ANY = pl.BlockSpec(memory_space=pl.ANY)


def _cp(*sem, vmem=VMEM_MB):
    return pltpu.CompilerParams(dimension_semantics=sem, vmem_limit_bytes=vmem * 1024 * 1024)


def _rows(tile, cols):
    return pl.BlockSpec((tile, cols), lambda i: (i, 0))


def _const(shape):
    nd = len(shape)
    return pl.BlockSpec(shape, lambda i: (0,) * nd)


def _sds(shape, dtype):
    return jax.ShapeDtypeStruct(shape, dtype)


def _dot(a, b):
    return jnp.dot(a, b, preferred_element_type=F32)


def _dot_nt(a, b):
    return lax.dot_general(a, b, (((1,), (1,)), ((), ())), preferred_element_type=F32)


def _dot_tn(a, b):
    return lax.dot_general(a, b, (((0,), (0,)), ((), ())), preferred_element_type=F32)


def _rms(x):
    return lax.rsqrt(jnp.mean(x * x, axis=-1, keepdims=True) + EPS)


def _norm_bwd(x, g, dout):
    r = _rms(x)
    n = x * r
    dn = dout * g
    dx = r * (dn - n * jnp.mean(dn * n, axis=-1, keepdims=True))
    return dx, jnp.sum(dout * n, axis=0, keepdims=True)


def _rot_fwd(t, c, a, bt):
    return t * c + pltpu.roll(t, LANES - 8, 1) * a + pltpu.roll(t, 8, 1) * bt


def _rot_bwd(d, c, a, bt):
    return d * c + pltpu.roll(d * a, 8, 1) + pltpu.roll(d * bt, LANES - 8, 1)


def _rope_tables(seq):
    pos = jnp.arange(seq, dtype=F32)
    inv_freq = ROPE_THETA ** (-jnp.arange(0, ROT_DIM, 2, dtype=F32) / ROT_DIM)
    ang = pos[:, None] * inv_freq[None, :]
    cos, sin = jnp.cos(ang), jnp.sin(ang)
    ones = jnp.ones((seq, HEAD_DIM - ROT_DIM), F32)
    zeros8 = jnp.zeros((seq, 8), F32)
    zrest = jnp.zeros((seq, HEAD_DIM - ROT_DIM), F32)
    c = jnp.concatenate([cos, cos, ones], axis=1)
    a = jnp.concatenate([-sin, zeros8, zrest], axis=1)
    bt = jnp.concatenate([zeros8, sin, zrest], axis=1)
    return tuple(jnp.tile(t, (1, 2)) for t in (c, a, bt))


class _Exchange:
    def __init__(self, inputs, out_shapes, sems, start, finish, aliases=None, middle=None):
        self.inputs, self.out_shapes, self.sems = list(inputs), list(out_shapes), list(sems)
        self.start, self.finish, self.aliases = start, finish, dict(aliases or {})
        self.middle = middle


def _call(body, *, name, grid, in_specs, out_specs, out_shape, args, scratch=(), sem=(), exchanges=()):
    in_specs, out_specs, out_shape, scratch = list(in_specs), list(out_specs), list(out_shape), list(scratch)
    if not exchanges:
        return pl.pallas_call(body, name=name, grid=grid, in_specs=in_specs, out_specs=out_specs,
                              out_shape=out_shape, scratch_shapes=scratch, compiler_params=_cp(*sem))(*args)
    n_in, n_out, n_scr = len(in_specs), len(out_specs), len(scratch)
    x_in = [a for ex in exchanges for a in ex.inputs]
    x_out = [s for ex in exchanges for s in ex.out_shapes]
    x_sem = [s for ex in exchanges for s in ex.sems]
    aliases, i_off, o_off = {}, n_in, n_out
    for ex in exchanges:
        for i, o in ex.aliases.items():
            aliases[i_off + i] = o_off + o
        i_off += len(ex.inputs)
        o_off += len(ex.out_shapes)

    def split(flat):
        out, pos = [], 0
        for ex, n in zip(exchanges, flat[1]):
            out.append(flat[0][pos:pos + n])
            pos += n
        return out

    def carrier(*refs):
        pos = 0
        groups = []
        for n in (n_in, len(x_in), n_out, len(x_out), n_scr, len(x_sem)):
            groups.append(refs[pos:pos + n])
            pos += n
        ins, xin, outs, xout, scr, xsem = groups
        xin = split((xin, [len(ex.inputs) for ex in exchanges]))
        xout = split((xout, [len(ex.out_shapes) for ex in exchanges]))
        xsem = split((xsem, [len(ex.sems) for ex in exchanges]))
        first = pl.program_id(0) == 0
        last = pl.program_id(0) == grid[0] - 1
        for d in range(1, len(grid)):
            first = jnp.logical_and(first, pl.program_id(d) == 0)
            last = jnp.logical_and(last, pl.program_id(d) == grid[d] - 1)

        @pl.when(first)
        def _():
            for ex, i, o, s in zip(exchanges, xin, xout, xsem):
                ex.start(i, o, s)

        if any(ex.middle for ex in exchanges):
            half = pl.program_id(0) == 5 * grid[0] // 8
            for d in range(1, len(grid)):
                half = jnp.logical_and(half, pl.program_id(d) == 0)

            @pl.when(half)
            def _():
                for ex, i, o, s in zip(exchanges, xin, xout, xsem):
                    if ex.middle:
                        ex.middle(i, o, s)

        body(*ins, *outs, *scr)

        @pl.when(last)
        def _():
            for ex, i, o, s in zip(exchanges, xin, xout, xsem):
                ex.finish(i, o, s)

    res = pl.pallas_call(
        carrier, name=name, grid=grid, in_specs=in_specs + [ANY] * len(x_in),
        out_specs=out_specs + [ANY] * len(x_out), out_shape=out_shape + x_out,
        scratch_shapes=scratch + x_sem, input_output_aliases=aliases,
        compiler_params=_cp(*(["arbitrary"] * len(grid))),
    )(*args, *x_in)
    return res[:n_out], split((res[n_out:], [len(ex.out_shapes) for ex in exchanges]))


def _alone(name, *exchanges):
    n_in = [len(ex.inputs) for ex in exchanges]
    n_out = [len(ex.out_shapes) for ex in exchanges]
    n_sem = [len(ex.sems) for ex in exchanges]
    aliases, i_off, o_off = {}, 0, 0
    for ex in exchanges:
        for i, o in ex.aliases.items():
            aliases[i_off + i] = o_off + o
        i_off += len(ex.inputs)
        o_off += len(ex.out_shapes)

    def split(flat, counts):
        out, pos = [], 0
        for n in counts:
            out.append(flat[pos:pos + n])
            pos += n
        return out

    def body(*refs):
        ins, outs, sems = split(refs, [sum(n_in), sum(n_out), sum(n_sem)])
        groups = list(zip(exchanges, split(ins, n_in), split(outs, n_out), split(sems, n_sem)))
        for ex, i, o, s in groups:
            ex.start(i, o, s)
        for ex, i, o, s in groups:
            if ex.middle:
                ex.middle(i, o, s)
        for ex, i, o, s in groups:
            ex.finish(i, o, s)

    res = pl.pallas_call(
        body, name=name, in_specs=[ANY] * sum(n_in), out_specs=[ANY] * sum(n_out),
        out_shape=[s for ex in exchanges for s in ex.out_shapes],
        scratch_shapes=[s for ex in exchanges for s in ex.sems], input_output_aliases=aliases,
    )(*[a for ex in exchanges for a in ex.inputs])
    return split(res, n_out)


def _place():
    x, y, c = lax.axis_index("x"), lax.axis_index("y"), lax.axis_index("c")
    chips = [(1 - x, y), (x, 1 - y), (1 - x, 1 - y)]
    return x, y, c, chips


def _remote(src, dst, send, recv, to):
    return pltpu.make_async_remote_copy(src_ref=src, dst_ref=dst, send_sem=send, recv_sem=recv,
                                        device_id=to, device_id_type=MESH)


def _ex_gather(shards):
    nw = len(shards)
    hrs = [s.shape[0] // 2 for s in shards]

    def copies(ins, outs, sems):
        s1, r1, s2, r2, fs, fr = sems
        x, y, c, _ = _place()
        me, xn, yn, dg = (x, y), (1 - x, y), (x, 1 - y), (1 - x, 1 - y)
        nbr = (xn, yn)
        sibling = (x, y, 1 - c)

        def piece(w, chip, core, part=None):
            hr = hrs[w]
            rows = pl.ds(core * hr, hr) if part is None else pl.ds(core * hr + part * (hr // 2), hr // 2)
            return outs[w].at[2 * chip[0] + chip[1], rows]

        def first(w, k):
            return _remote(ins[w].at[pl.ds(c * hrs[w], hrs[w])], piece(w, me, c), s1.at[w, k], r1.at[w, k],
                           (*nbr[k], c))

        def landed(w, k):
            return _remote(piece(w, nbr[k], c), piece(w, nbr[k], c), s1.at[w, k], r1.at[w, k], (*nbr[k], c))

        def onward(w, k):
            return _remote(piece(w, nbr[k], c, k), piece(w, nbr[k], c, k), s2.at[w, k], r2.at[w, k],
                           (*nbr[1 - k], c))

        def arrived(w, k):
            return _remote(piece(w, dg, c, k), piece(w, dg, c, k), s2.at[w, k], r2.at[w, k], (*nbr[1 - k], c))

        def passed(w, j):
            chip = (xn, yn, dg)[j]
            return _remote(piece(w, chip, c), piece(w, chip, c), fs.at[w, j], fr.at[w, j], sibling)

        def handed(w, j):
            chip = (xn, yn, dg)[j]
            return _remote(piece(w, chip, 1 - c), piece(w, chip, 1 - c), fs.at[w, j], fr.at[w, j], sibling)

        return first, landed, onward, arrived, passed, handed

    def start(ins, outs, sems):
        first = copies(ins, outs, sems)[0]
        for w in range(nw):
            for k in range(2):
                first(w, k).start()

    def middle(ins, outs, sems):
        _, landed, onward, _, passed, _ = copies(ins, outs, sems)
        for w in range(nw):
            for k in range(2):
                landed(w, k).wait_recv()
                onward(w, k).start()
                passed(w, k).start()

    def finish(ins, outs, sems):
        first, _, onward, arrived, passed, handed = copies(ins, outs, sems)
        for w in range(nw):
            for k in range(2):
                arrived(w, k).wait_recv()
            passed(w, 2).start()
        for w in range(nw):
            for j in range(3):
                handed(w, j).wait_recv()
        for w in range(nw):
            for k in range(2):
                first(w, k).wait_send()
                onward(w, k).wait_send()
            for j in range(3):
                passed(w, j).wait_send()

    return _Exchange(shards, [_sds((N_CHIPS,) + s.shape, s.dtype) for s in shards],
                     [pltpu.SemaphoreType.DMA((nw, 2))] * 4 + [pltpu.SemaphoreType.DMA((nw, 3))] * 2,
                     start, finish, middle=middle)


def _ex_pair(grads):
    nw = len(grads)

    def copies(ins, outs, sems):
        x, y, c, _ = _place()
        out = []
        for w in range(nw):
            hr = grads[w].shape[1] // 2
            out.append(_remote(ins[w].at[:, pl.ds((1 - c) * hr, hr)], outs[w], sems[0].at[w], sems[1].at[w],
                               (x, y, 1 - c)))
        return out

    def start(ins, outs, sems):
        for cp in copies(ins, outs, sems):
            cp.start()

    def finish(ins, outs, sems):
        for cp in copies(ins, outs, sems):
            cp.wait()

    return _Exchange(grads, [_sds((N_CHIPS, g.shape[1] // 2, g.shape[2]), F32) for g in grads],
                     [pltpu.SemaphoreType.DMA((nw,))] * 2, start, finish)


def _ex_chip(pieces):
    nw = len(pieces)

    def copies(ins, outs, sems):
        x, y, c, chips = _place()
        return [_remote(ins[w].at[2 * cx + cy], outs[w].at[k], sems[0].at[w, k], sems[1].at[w, k], (cx, cy, c))
                for w in range(nw) for k, (cx, cy) in enumerate(chips)]

    def start(ins, outs, sems):
        for cp in copies(ins, outs, sems):
            cp.start()

    def finish(ins, outs, sems):
        for cp in copies(ins, outs, sems):
            cp.wait()

    return _Exchange(pieces, [_sds((3,) + p.shape[1:], BF16) for p in pieces],
                     [pltpu.SemaphoreType.DMA((nw, 3))] * 2, start, finish)


def _ex_swap(fulls):
    nw = len(fulls)

    def start(ins, outs, sems):
        x, y, c, _ = _place()
        for w in range(nw):
            hr = fulls[w].shape[0] // 2
            mine = pl.ds(c * hr, hr)
            _remote(ins[w].at[mine], outs[w].at[mine], sems[0].at[w], sems[1].at[w], (x, y, 1 - c)).start()

    def finish(ins, outs, sems):
        x, y, c, _ = _place()
        for w in range(nw):
            hr = fulls[w].shape[0] // 2
            mine, theirs = pl.ds(c * hr, hr), pl.ds((1 - c) * hr, hr)
            _remote(ins[w].at[mine], outs[w].at[mine], sems[0].at[w], sems[1].at[w], (x, y, 1 - c)).wait_send()
            _remote(ins[w].at[theirs], outs[w].at[theirs], sems[0].at[w], sems[1].at[w], (x, y, 1 - c)).wait_recv()

    return _Exchange(fulls, [_sds(f.shape, F32) for f in fulls], [pltpu.SemaphoreType.DMA((nw,))] * 2,
                     start, finish, aliases={w: w for w in range(nw)})


def _ex_allgather(blocks):
    nb = len(blocks)

    def copies(ins, outs, sems):
        send, recv, lsem = sems
        x, y, c, chips = _place()
        me, sibling = (x, y, c), (x, y, 1 - c)

        def rows(b, px, py, pc):
            m_per = blocks[b].shape[0]
            return outs[b].at[pl.ds((4 * px + 2 * py + pc) * m_per, m_per), :]

        def copy(b, k, blk, to, src=None):
            return _remote(rows(b, *blk) if src is None else src, rows(b, *blk), send.at[b, k], recv.at[b, k], to)

        def mine(b):
            return pltpu.make_async_copy(ins[b], rows(b, *me), lsem.at[b])

        def first(b, k):
            return copy(b, k, me, sibling if k == 0 else (*chips[k - 1], c), src=ins[b])

        def passed(b, j):
            return copy(b, 4 + j, (*chips[j], c), sibling)

        def landed(b, j):
            return copy(b, 1 + j, (*chips[j], c), me)

        def handed(b, k):
            return copy(b, 0, sibling, me) if k == 0 else copy(b, 3 + k, (*chips[k - 1], 1 - c), me)

        return mine, first, passed, landed, handed

    def start(ins, outs, sems):
        mine, first, _, _, _ = copies(ins, outs, sems)
        for b in range(nb):
            mine(b).start()
            for k in range(4):
                first(b, k).start()

    def finish(ins, outs, sems):
        mine, first, passed, landed, handed = copies(ins, outs, sems)
        sent = []
        for b in range(nb):
            for j in range(3):
                landed(b, j).wait_recv()
                cp = passed(b, j)
                cp.start()
                sent.append(cp)
        for b in range(nb):
            for k in range(4):
                handed(b, k).wait_recv()
            for k in range(4):
                first(b, k).wait_send()
        for cp in sent:
            cp.wait_send()
        for b in range(nb):
            mine(b).wait()

    return _Exchange(blocks, [_sds((N_DEV * b.shape[0], b.shape[1]), F32) for b in blocks],
                     [pltpu.SemaphoreType.DMA((nb, 7)), pltpu.SemaphoreType.DMA((nb, 7)), pltpu.SemaphoreType.DMA((nb,))],
                     start, finish)


def _cast_shards(x2, g1, w_up, w_down, w_bp, w_ba, w_out, exchanges=()):
    r, c = w_up.shape
    tr = HALF // 2
    steps = r // tr
    tm = x2.shape[0] // steps

    def body(x_ref, g_ref, up_ref, down_ref, bp_ref, ba_ref, out_ref,
             h_ref, ua_ref, ub_ref, da_ref, db_ref, bpo_ref, bao_ref, outo_ref):
        i = pl.program_id(0)
        x = x_ref[...]
        h_ref[...] = (x * _rms(x) * g_ref[...]).astype(BF16)

        @pl.when(i == 0)
        def _():
            for src, dst in ((bp_ref, bpo_ref), (ba_ref, bao_ref), (out_ref, outo_ref)):
                dst[...] = src[...].astype(BF16)

        @pl.when(i < steps // 2)
        def _():
            ua_ref[...] = up_ref[...].astype(BF16)
            da_ref[...] = down_ref[...].astype(BF16)

        @pl.when(i >= steps // 2)
        def _():
            ub_ref[...] = up_ref[...].astype(BF16)
            db_ref[...] = down_ref[...].astype(BF16)

    rows = _rows(tr, c)
    first = pl.BlockSpec((tr, c), lambda i: (jnp.minimum(i, steps // 2 - 1), 0))
    second = pl.BlockSpec((tr, c), lambda i: (jnp.maximum(i - steps // 2, 0), 0))
    half = _sds((HALF, c), BF16)
    return _call(
        body, name="cast_shards", grid=(steps,),
        in_specs=[_rows(tm, D_MODEL), _const((1, D_MODEL)), rows, rows,
                  _const(w_bp.shape), _const(w_ba.shape), _const(w_out.shape)],
        out_specs=[_rows(tm, D_MODEL), first, second, first, second,
                   _const(w_bp.shape), _const(w_ba.shape), _const(w_out.shape)],
        out_shape=[_sds(x2.shape, BF16), half, half, half, half,
                   _sds(w_bp.shape, BF16), _sds(w_ba.shape, BF16), _sds(w_out.shape, BF16)],
        args=(x2, g1, w_up, w_down, w_bp, w_ba, w_out), sem=("arbitrary",), exchanges=exchanges)


def _inproj(h, w_in_t, b_in, tabs, seq, exchanges=()):
    T = h.shape[0]
    tm = min(TM, seq)
    nseq = seq // tm

    def body(h_ref, w_ref, b_ref, c_ref, a_ref, bt_ref, u_ref, q_ref, k_ref, v_ref, gate_ref):
        h = h_ref[...]

        def proj(lo, hi):
            return _dot_nt(h, w_ref[lo:hi, :]) + b_ref[:, lo:hi]

        c, a, bt = c_ref[...], a_ref[...], bt_ref[...]
        u_ref[...] = proj(0, C_Q)
        q = proj(C_Q, C_K)
        for p in range(4):
            sl = slice(LANES * p, LANES * (p + 1))
            q_ref[:, sl] = (_rot_fwd(q[:, sl], c, a, bt) * SCALE).astype(BF16)
        kv = proj(C_K, C_G)
        k_ref[...] = _rot_fwd(kv[:, :KV_WIDTH], c, a, bt).astype(BF16)
        v_ref[...] = kv[:, KV_WIDTH:].astype(BF16)
        for j in range(2):
            lo = C_G + D_MODEL * j
            gate_ref[:, D_MODEL * j:D_MODEL * (j + 1)] = jax.nn.sigmoid(proj(lo, lo + D_MODEL)).astype(BF16)

    tab = pl.BlockSpec((tm, LANES), lambda i: (i % nseq, 0))
    return _call(
        body, name="inproj", grid=(T // tm,),
        in_specs=[_rows(tm, D_MODEL), _const((IN_WIDTH, D_MODEL)), _const((1, IN_WIDTH)), tab, tab, tab],
        out_specs=[_rows(tm, POOL_WIDTH), _rows(tm, ATTN_WIDTH), _rows(tm, KV_WIDTH),
                   _rows(tm, KV_WIDTH), _rows(tm, GATE_WIDTH)],
        out_shape=[_sds((T, POOL_WIDTH), F32), _sds((T, ATTN_WIDTH), BF16),
                   _sds((T, KV_WIDTH), BF16), _sds((T, KV_WIDTH), BF16), _sds((T, GATE_WIDTH), BF16)],
        args=(h, w_in_t, b_in, *tabs), sem=("parallel",), exchanges=exchanges)


def _inv_count(pos, w):
    return 1.0 / jnp.minimum(pos + 1, w).astype(F32)


def _pool_tile(i, tp, nseq, u_ref, prev_ref, w_ref, s_ref, diff_ref, y_ref):
    first = (i % nseq) == 0
    prev = jnp.where(first, 0.0, prev_ref[...])
    ext = jnp.concatenate([prev, u_ref[...]], axis=0)
    pos = (i % nseq) * tp + lax.broadcasted_iota(jnp.int32, (tp, 1), 0)
    for gi, w in enumerate(POOL_WINDOWS):
        sl = slice(POOL_GC * gi, POOL_GC * (gi + 1))
        xg = ext[:, sl]
        s = xg
        sh = 1
        while sh < w:
            s = s + pltpu.roll(s, sh, 0)
            sh *= 2
        pooled = s[HALO:] * _inv_count(pos, w)
        diff = (pooled - xg[HALO:]).astype(BF16)
        diff_ref[:, sl] = diff
        mixed = _dot(diff, w_ref[gi].astype(BF16))
        y_ref[:, sl] = (mixed * s_ref[:, sl]).astype(BF16)


def _pool_specs(tp):
    per = tp // HALO
    return [_rows(tp, POOL_WIDTH), pl.BlockSpec((HALO, POOL_WIDTH), lambda i: (jnp.maximum(i * per - 1, 0), 0)),
            _const((4, POOL_GC, POOL_GC)), _const((1, POOL_WIDTH))]


GROUP = 4
GROWS = GROUP * BLOCK


def _attn_masks(n):
    qi = lax.broadcasted_iota(jnp.int32, (GROWS, 2 * BLOCK), 0) % BLOCK
    kj = lax.broadcasted_iota(jnp.int32, (GROWS, 2 * BLOCK), 1)
    rel = qi + BLOCK - kj
    valid = (rel >= 0) & (rel < BLOCK) & (kj >= jnp.where(n > 0, 0, BLOCK))
    lo = lax.broadcasted_iota(jnp.int32, (BLOCK, LANES), 1) < HEAD_DIM
    return valid, lo


def _by_example(bl, *arrays):
    return [a.reshape(bl, a.shape[0] // bl, a.shape[1]) for a in arrays]


def _stack_heads(ref, h, lo):
    keep = lo if h == 0 else jnp.logical_not(lo)
    pieces = []
    for p in (2 * h, 2 * h + 1):
        xp = ref[:, LANES * p:LANES * (p + 1)].astype(F32)
        for e in range(2):
            t = xp if e == h else pltpu.roll(xp, HEAD_DIM, 1)
            pieces.append(jnp.where(keep, t, 0.0).astype(BF16))
    return jnp.concatenate(pieces, axis=0)


def _unstack_heads(stacked, h, lo):
    pairs = []
    for j in range(2):
        parts = []
        for e in range(2):
            t = stacked[BLOCK * (2 * j + e):BLOCK * (2 * j + e + 1)]
            parts.append(t if e == h else pltpu.roll(t, HEAD_DIM, 1))
        pairs.append(jnp.where(lo, parts[0], parts[1]))
    return pairs


def _sink_rows(sink_ref, h):
    head = lax.broadcasted_iota(jnp.int32, (GROWS, 1), 0) // BLOCK
    col = jnp.zeros((GROWS, 1), F32) + sink_ref[GROUP * h]
    for g in range(1, GROUP):
        col = jnp.where(head == g, sink_ref[GROUP * h + g], col)
    return col


def _group_probs(qs, kk, valid, sink):
    s = jnp.where(valid, _dot_nt(qs, kk), NEG_INF)
    m = jnp.maximum(jnp.max(s, axis=1, keepdims=True), sink)
    ex = jnp.exp(s - m)
    es = jnp.exp(sink - m)
    inv = 1.0 / (jnp.sum(ex, axis=1, keepdims=True) + es)
    return ex * inv, es * inv


def _mixers_fwd(q, k, v, sinks, u, w_pool, pool_scale, seq, exchanges=()):
    T = q.shape[0]
    nb = seq // BLOCK
    bl = T // seq
    tp = T // nb
    nseq = seq // tp

    def body(sink_ref, q_ref, kp_ref, kc_ref, vp_ref, vc_ref, u_ref, prev_ref, w_ref, s_ref, o_ref, diff_ref, y_ref):
        n = pl.program_id(0)
        valid, lo = _attn_masks(n)
        for b in range(bl):
            kk = jnp.concatenate([kp_ref[b], kc_ref[b]], axis=0)
            vv = jnp.concatenate([vp_ref[b], vc_ref[b]], axis=0)
            for h in range(2):
                qs = _stack_heads(q_ref.at[b], h, lo)
                pr, _ = _group_probs(qs, kk, valid, _sink_rows(sink_ref, h))
                o = _dot(pr.astype(BF16), vv)
                for j, pair in enumerate(_unstack_heads(o, h, lo)):
                    p = 2 * h + j
                    o_ref[b, :, LANES * p:LANES * (p + 1)] = pair.astype(BF16)
        _pool_tile(n, tp, nseq, u_ref, prev_ref, w_ref, s_ref, diff_ref, y_ref)

    cur = lambda n: (0, n, 0)
    prv = lambda n: (0, jnp.maximum(n - 1, 0), 0)
    kv = lambda m: pl.BlockSpec((bl, BLOCK, KV_WIDTH), m)
    res = _call(
        body, name="mixers_fwd", grid=(nb,),
        in_specs=[pl.BlockSpec(memory_space=pltpu.SMEM), pl.BlockSpec((bl, BLOCK, ATTN_WIDTH), cur),
                  kv(prv), kv(cur), kv(prv), kv(cur)] + _pool_specs(tp),
        out_specs=[pl.BlockSpec((bl, BLOCK, ATTN_WIDTH), cur), _rows(tp, POOL_WIDTH), _rows(tp, POOL_WIDTH)],
        out_shape=[_sds((bl, seq, ATTN_WIDTH), BF16), _sds((T, POOL_WIDTH), BF16), _sds((T, POOL_WIDTH), BF16)],
        args=(sinks, *_by_example(bl, q, k, k, v, v), u, u, w_pool, pool_scale), sem=("parallel",),
        exchanges=exchanges)
    outs, rest = res if exchanges else (res, None)
    return [outs[0].reshape(T, ATTN_WIDTH), outs[1], outs[2]], rest


def _branch(y, w_ref):
    return jnp.concatenate([_dot(y, w_ref[j]) for j in range(N_CHIPS)], axis=1)


def _merge_out(y_pool, y_attn, gate, x2, w_bp, w_ba, w_out, g2, g3, exchanges=()):
    T = x2.shape[0]
    tm = min(TM, T)

    def body(yp_ref, ya_ref, gate_ref, x_ref, wbp_ref, wba_ref, wo_ref, g2_ref, g3_ref,
             mg_ref, mix_ref, x1_ref, h2_ref):
        bp, ba = _branch(yp_ref[...], wbp_ref), _branch(ya_ref[...], wba_ref)
        merged = (gate_ref[:, :D_MODEL].astype(F32) * bp + gate_ref[:, D_MODEL:].astype(F32) * ba).astype(BF16)
        mg_ref[...] = merged
        mix = _dot(merged, wo_ref[...])
        mix_ref[...] = mix
        x1 = x_ref[...] + mix * _rms(mix) * g2_ref[...]
        x1_ref[...] = x1
        h2_ref[...] = (x1 * _rms(x1) * g3_ref[...]).astype(BF16)

    return _call(
        body, name="merge_out", grid=(T // tm,),
        in_specs=[_rows(tm, POOL_WIDTH), _rows(tm, ATTN_WIDTH), _rows(tm, GATE_WIDTH), _rows(tm, D_MODEL),
                  _const(w_bp.shape), _const(w_ba.shape), _const((D_MODEL, D_MODEL)),
                  _const((1, D_MODEL)), _const((1, D_MODEL))],
        out_specs=[_rows(tm, D_MODEL)] * 4,
        out_shape=[_sds((T, D_MODEL), BF16), _sds((T, D_MODEL), F32), _sds((T, D_MODEL), F32),
                   _sds((T, D_MODEL), BF16)],
        args=(y_pool, y_attn, gate, x2, w_bp, w_ba, w_out, g2, g3), sem=("parallel",), exchanges=exchanges)


HALF = D_MODEL // 2
TM_MLP = 256


def _mlp_core(h2, x1, mix, tgt, w_up, w_down, g4, g3, g2):
    T = h2.shape[0]
    tm = min(TM_MLP, T)

    def body(h_ref, x1_ref, mix_ref, t_ref, g_ref, g3_ref, g2_ref, ua_hbm, ub_hbm, da_hbm, db_hbm,
             act_ref, dff_ref, dup_ref, dx1_ref, dmix_ref, loss_ref, dg_ref, dg3_ref, dg2_ref,
             wu, wd, relu_scr, sems):
        def weight_copy(i):
            src, dst = ((ua_hbm, wu.at[:, :HALF]), (ub_hbm, wu.at[:, HALF:]),
                        (da_hbm, wd.at[:, :HALF]), (db_hbm, wd.at[:, HALF:]))[i]
            return pltpu.make_async_copy(src, dst, sems.at[i])

        @pl.when(pl.program_id(0) == 0)
        def _():
            for i in range(4):
                weight_copy(i).start()
            loss_ref[...] = jnp.zeros_like(loss_ref)
            for ref in (dg_ref, dg3_ref, dg2_ref):
                ref[...] = jnp.zeros_like(ref)
            weight_copy(0).wait()
            weight_copy(1).wait()

        h = h_ref[...]
        ff = None
        for j in range(N_CHIPS):
            lo = D_MODEL * j
            relu = jnp.maximum(_dot(h, wu[j]), 0.0)
            if j == 0:
                @pl.when(pl.program_id(0) == 0)
                def _():
                    weight_copy(2).wait()
                    weight_copy(3).wait()
            relu_scr[:, lo:lo + D_MODEL] = relu
            act = jnp.square(relu).astype(BF16)
            act_ref[:, lo:lo + D_MODEL] = act
            t = _dot(act, wd[j])
            ff = t if ff is None else ff + t
        g = g_ref[...]
        x1 = x1_ref[...]
        err = x1 + ff * _rms(ff) * g - t_ref[...]
        loss_ref[...] += jnp.sum(err * err) * (0.5 / D_MODEL)
        dy = err * (1.0 / D_MODEL)
        dff, dg = _norm_bwd(ff, g, dy)
        dg_ref[...] += dg
        dff = dff.astype(BF16)
        dff_ref[...] = dff
        dh2 = None
        for j in range(N_CHIPS):
            lo = D_MODEL * j
            dup = (_dot_nt(dff, wd[j]) * (2.0 * relu_scr[:, lo:lo + D_MODEL])).astype(BF16)
            dup_ref[:, lo:lo + D_MODEL] = dup
            t = _dot_nt(dup, wu[j])
            dh2 = t if dh2 is None else dh2 + t
        dx, dg3 = _norm_bwd(x1, g3_ref[...], dh2)
        dx1 = dy + dx
        dx1_ref[...] = dx1
        dg3_ref[...] += dg3
        dmix, dg2 = _norm_bwd(mix_ref[...], g2_ref[...], dx1)
        dmix_ref[...] = dmix.astype(BF16)
        dg2_ref[...] += dg2

    slabs = pltpu.VMEM((N_CHIPS, D_MODEL, D_MODEL), BF16)
    gain = _const((1, D_MODEL))
    return pl.pallas_call(
        body, name="mlp_core", grid=(T // tm,),
        in_specs=[_rows(tm, D_MODEL)] * 4 + [gain] * 3 + [ANY] * 4,
        out_specs=[_rows(tm, D_FF), _rows(tm, D_MODEL), _rows(tm, D_FF), _rows(tm, D_MODEL), _rows(tm, D_MODEL),
                   _const((8, LANES)), gain, gain, gain],
        out_shape=[_sds((T, D_FF), BF16), _sds((T, D_MODEL), BF16), _sds((T, D_FF), BF16), _sds((T, D_MODEL), F32),
                   _sds((T, D_MODEL), BF16), _sds((8, LANES), F32)] + [_sds((1, D_MODEL), F32)] * 3,
        scratch_shapes=[slabs] * 2 + [pltpu.VMEM((tm, D_FF), F32), pltpu.SemaphoreType.DMA((4,))],
        compiler_params=_cp("arbitrary"),
    )(h2, x1, mix, tgt, g4, g3, g2, *w_up, *w_down)


def _dw(tag, a, g, ta, tn, shard_cols=False, exchanges=()):
    T, ka = a.shape
    n = g.shape[1]
    tk = min(2 * TM, T)
    nk = T // tk

    def body(a_ref, g_ref, o_ref):
        @pl.when(pl.program_id(2) == 0)
        def _():
            o_ref[...] = jnp.zeros_like(o_ref)

        o_ref[...] += _dot_tn(a_ref[...], g_ref[...])

    if shard_cols:
        per = (n // N_CHIPS) // tn
        out_spec = pl.BlockSpec((None, ta, tn), lambda i, j, k: (j // per, i, j % per))
        out_shape = _sds((N_CHIPS, ka, n // N_CHIPS), F32)
    else:
        out_spec = pl.BlockSpec((ta, tn), lambda i, j, k: (i, j))
        out_shape = _sds((ka, n), F32)
    return _call(
        body, name="dw_" + tag, grid=(ka // ta, n // tn, nk),
        in_specs=[pl.BlockSpec((tk, ta), lambda i, j, k: (k, i)), pl.BlockSpec((tk, tn), lambda i, j, k: (k, j))],
        out_specs=[out_spec], out_shape=[out_shape],
        args=(a, g), sem=("parallel", "parallel", "arbitrary"), exchanges=exchanges)


def _dw_mix(merged, dmix, y_pool, dbp, y_attn, dba, exchanges=()):
    T = merged.shape[0]
    tk = min(2 * TM, T)
    c = D_MODEL // N_CHIPS

    def body(mg_ref, dmix_ref, yp_ref, dbp_ref, ya_ref, dba_ref, out_ref, bp_ref, ba_ref):
        @pl.when(pl.program_id(0) == 0)
        def _():
            for ref in (out_ref, bp_ref, ba_ref):
                ref[...] = jnp.zeros_like(ref)

        out_ref[...] += _dot_tn(mg_ref[...], dmix_ref[...])
        for y_ref, d_ref, o_ref in ((yp_ref, dbp_ref, bp_ref), (ya_ref, dba_ref, ba_ref)):
            res = _dot_tn(y_ref[...], d_ref[...])
            for j in range(N_CHIPS):
                o_ref[j] += res[:, c * j:c * (j + 1)]

    slabs = (N_CHIPS, POOL_WIDTH, c)
    return _call(
        body, name="dw_mix", grid=(T // tk,),
        in_specs=[_rows(tk, D_MODEL), _rows(tk, D_MODEL), _rows(tk, POOL_WIDTH), _rows(tk, D_MODEL),
                  _rows(tk, ATTN_WIDTH), _rows(tk, D_MODEL)],
        out_specs=[_const((D_MODEL, D_MODEL)), _const(slabs), _const(slabs)],
        out_shape=[_sds((D_MODEL, D_MODEL), F32), _sds(slabs, F32), _sds(slabs, F32)],
        args=(merged, dmix, y_pool, dbp, y_attn, dba), sem=("arbitrary",), exchanges=exchanges)


def _merge_bwd(dmix, gate, y_pool, y_attn, w_out, w_bp, w_ba, exchanges=()):
    T = dmix.shape[0]
    tm = min(TM, T)

    def body(dmix_ref, gate_ref, yp_ref, ya_ref, wo_ref, wbp_ref, wba_ref,
             dbp_ref, dba_ref, dgate_ref, dyp_ref, dya_ref):
        dm = _dot_nt(dmix_ref[...], wo_ref[...])
        for j, (y_ref, db_ref, w_ref, dy_ref) in enumerate(
                ((yp_ref, dbp_ref, wbp_ref, dyp_ref), (ya_ref, dba_ref, wba_ref, dya_ref))):
            sl = slice(D_MODEL * j, D_MODEL * (j + 1))
            gt = gate_ref[:, sl].astype(F32)
            db = (dm * gt).astype(BF16)
            db_ref[...] = db
            dgate_ref[:, sl] = (dm * _branch(y_ref[...], w_ref) * gt * (1.0 - gt)).astype(BF16)
            cw = D_MODEL // N_CHIPS
            dy = _dot_nt(db[:, :cw], w_ref[0])
            for c in range(1, N_CHIPS):
                dy = dy + _dot_nt(db[:, cw * c:cw * (c + 1)], w_ref[c])
            dy_ref[...] = dy.astype(dy_ref.dtype)

    return _call(
        body, name="merge_bwd", grid=(T // tm,),
        in_specs=[_rows(tm, D_MODEL), _rows(tm, GATE_WIDTH), _rows(tm, POOL_WIDTH), _rows(tm, ATTN_WIDTH),
                  _const((D_MODEL, D_MODEL)), _const(w_bp.shape), _const(w_ba.shape)],
        out_specs=[_rows(tm, D_MODEL), _rows(tm, D_MODEL), _rows(tm, GATE_WIDTH), _rows(tm, POOL_WIDTH),
                   _rows(tm, ATTN_WIDTH)],
        out_shape=[_sds((T, D_MODEL), BF16), _sds((T, D_MODEL), BF16), _sds((T, GATE_WIDTH), BF16),
                   _sds((T, POOL_WIDTH), F32), _sds((T, ATTN_WIDTH), BF16)],
        args=(dmix, gate, y_pool, y_attn, w_out, w_bp, w_ba), sem=("parallel",), exchanges=exchanges)


def _mixers_bwd(q, k, v, do, sinks, tabs, dyp, diff, w_pool, pool_scale, seq, exchanges=()):
    T = q.shape[0]
    nb = seq // BLOCK
    bl = T // seq
    steps = nb + 1
    tp = T // nb
    nseq = seq // tp
    per = tp // HALO
    last_halo = T // HALO - 1

    def body(sink_ref, q_ref, do_ref, kp_ref, kc_ref, vp_ref, vc_ref, c_ref, a_ref, bt_ref, cp_ref, ap_ref, btp_ref,
             dy_ref, nxt_ref, diff_ref, w_ref, s_ref,
             dq_ref, dk_ref, dv_ref, dsink_ref, du_ref, dw_ref, ds_ref, ck_ref, cv_ref):
        n = pl.program_id(0)

        @pl.when(n == 0)
        def _():
            for ref in (dsink_ref, ck_ref, cv_ref, dw_ref, ds_ref):
                ref[...] = jnp.zeros_like(ref)

        @pl.when(n < nb)
        def _():
            _pool_bwd_tile(n, tp, nseq, dy_ref, nxt_ref, diff_ref, w_ref, s_ref, du_ref, dw_ref, ds_ref)
            valid, lo = _attn_masks(n)
            for b in range(bl):
                kk = jnp.concatenate([kp_ref[b], kc_ref[b]], axis=0)
                vv = jnp.concatenate([vp_ref[b], vc_ref[b]], axis=0)
                dk_acc = jnp.zeros((2 * BLOCK, KV_WIDTH), F32)
                dv_acc = jnp.zeros((2 * BLOCK, KV_WIDTH), F32)
                for h in range(2):
                    qs = _stack_heads(q_ref.at[b], h, lo)
                    dos = _stack_heads(do_ref.at[b], h, lo)
                    pr, ps = _group_probs(qs, kk, valid, _sink_rows(sink_ref, h))
                    dp = _dot_nt(dos, vv)
                    delta = jnp.sum(pr * dp, axis=1, keepdims=True)
                    ds = (pr * (dp - delta)).astype(BF16)
                    dsk = ps * delta
                    for g in range(GROUP):
                        idx = GROUP * h + g
                        dsink_ref[idx:idx + 1, :] += (jnp.zeros((1, LANES), F32)
                                                      - jnp.sum(dsk[BLOCK * g:BLOCK * (g + 1)]))
                    dk_acc = dk_acc + _dot_tn(ds, qs)
                    dv_acc = dv_acc + _dot_tn(pr.astype(BF16), dos)
                    for j, pair in enumerate(_unstack_heads(_dot(ds, kk) * SCALE, h, lo)):
                        sl = slice(LANES * (2 * h + j), LANES * (2 * h + j + 1))
                        dq_ref[b, :, sl] = _rot_bwd(pair, c_ref[...], a_ref[...], bt_ref[...]).astype(BF16)
                fin_k = ck_ref[b] + dk_acc[:BLOCK]
                dk_ref[b] = _rot_bwd(fin_k, cp_ref[...], ap_ref[...], btp_ref[...]).astype(BF16)
                dv_ref[b] = (cv_ref[b] + dv_acc[:BLOCK]).astype(BF16)
                ck_ref[b] = dk_acc[BLOCK:]
                cv_ref[b] = dv_acc[BLOCK:]

        @pl.when(n == nb)
        def _():
            for b in range(bl):
                dk_ref[b] = _rot_bwd(ck_ref[b], cp_ref[...], ap_ref[...], btp_ref[...]).astype(BF16)
                dv_ref[b] = cv_ref[b].astype(BF16)

    cur = lambda n: (0, jnp.minimum(n, nb - 1), 0)
    prv = lambda n: (0, jnp.clip(n - 1, 0, nb - 1), 0)
    tcur = lambda n: (jnp.minimum(n, nb - 1), 0)
    tprv = lambda n: (jnp.clip(n - 1, 0, nb - 1), 0)
    wide = lambda m: pl.BlockSpec((bl, BLOCK, ATTN_WIDTH), m)
    kv = lambda m: pl.BlockSpec((bl, BLOCK, KV_WIDTH), m)
    tab = lambda m: pl.BlockSpec((BLOCK, LANES), m)
    tile = lambda n: (jnp.minimum(n, nb - 1), 0)
    halo = lambda n: (jnp.minimum((jnp.minimum(n, nb - 1) + 1) * per, last_halo), 0)
    rows = pl.BlockSpec((tp, POOL_WIDTH), tile)
    res = _call(
        body, name="mixers_bwd", grid=(steps,),
        in_specs=[pl.BlockSpec(memory_space=pltpu.SMEM), wide(cur), wide(cur), kv(prv), kv(cur), kv(prv), kv(cur),
                  tab(tcur), tab(tcur), tab(tcur), tab(tprv), tab(tprv), tab(tprv),
                  rows, pl.BlockSpec((HALO, POOL_WIDTH), halo), rows, _const((4, POOL_GC, POOL_GC)),
                  _const((1, POOL_WIDTH))],
        out_specs=[wide(cur), kv(prv), kv(prv), _const((8, LANES)), rows, _const((4, POOL_GC, POOL_GC)),
                   _const((1, POOL_WIDTH))],
        out_shape=[_sds((bl, seq, ATTN_WIDTH), BF16), _sds((bl, seq, KV_WIDTH), BF16),
                   _sds((bl, seq, KV_WIDTH), BF16), _sds((8, LANES), F32), _sds((T, POOL_WIDTH), BF16),
                   _sds((4, POOL_GC, POOL_GC), F32), _sds((1, POOL_WIDTH), F32)],
        scratch=[pltpu.VMEM((bl, BLOCK, KV_WIDTH), F32), pltpu.VMEM((bl, BLOCK, KV_WIDTH), F32)],
        args=(sinks, *_by_example(bl, q, do, k, k, v, v), *tabs, *tabs, dyp, dyp, diff, w_pool, pool_scale),
        sem=("arbitrary",), exchanges=exchanges)
    outs, rest = (res if exchanges else (res, None))
    outs = [outs[0].reshape(T, ATTN_WIDTH), outs[1].reshape(T, KV_WIDTH), outs[2].reshape(T, KV_WIDTH), *outs[3:]]
    return (outs, rest) if exchanges else outs


def _pool_bwd_tile(i, tp, nseq, dy_ref, nxt_ref, diff_ref, w_ref, s_ref, du_ref, dw_ref, ds_ref):
    last = (i % nseq) == nseq - 1
    nxt = jnp.where(last, 0.0, nxt_ref[...])
    ext = jnp.concatenate([dy_ref[...], nxt], axis=0) * s_ref[...]
    pos = (i % nseq) * tp + lax.broadcasted_iota(jnp.int32, (tp + HALO, 1), 0)
    for gi, w in enumerate(POOL_WINDOWS):
        sl = slice(POOL_GC * gi, POOL_GC * (gi + 1))
        wg = w_ref[gi].astype(BF16)
        dmx = ext[:, sl].astype(BF16)
        ddiff = _dot_nt(dmx, wg)
        s = ddiff * _inv_count(pos, w)
        sh = 1
        while sh < w:
            s = s + pltpu.roll(s, tp + HALO - sh, 0)
            sh *= 2
        du_ref[:, sl] = (s[:tp] - ddiff[:tp]).astype(BF16)
        dg = diff_ref[:, sl]
        dw_ref[gi] += _dot_tn(dg, dmx[:tp])
        ds_ref[:, sl] += jnp.sum(dy_ref[:, sl] * _dot(dg, wg), axis=0, keepdims=True)


_PARTS = ((0, C_Q), (C_Q, C_K), (C_K, C_V), (C_V, C_G), (C_G, IN_WIDTH))


def _inproj_bwd(parts, x2, dx1, w_in_t, g1, exchanges=()):
    T = x2.shape[0]
    tm = min(TM, T)

    def body(du_ref, dq_ref, dk_ref, dv_ref, dgt_ref, x_ref, dx1_ref, w_ref, g_ref, gx_ref, dg_ref):
        @pl.when(pl.program_id(0) == 0)
        def _():
            dg_ref[...] = jnp.zeros_like(dg_ref)

        dh = jnp.zeros((tm, D_MODEL), F32)
        for (lo, hi), p_ref in zip(_PARTS, (du_ref, dq_ref, dk_ref, dv_ref, dgt_ref)):
            dh = dh + _dot(p_ref[...], w_ref[lo:hi, :])
        dx, dg = _norm_bwd(x_ref[...], g_ref[...], dh)
        gx_ref[...] = dx1_ref[...] + dx
        dg_ref[...] += dg

    return _call(
        body, name="inproj_bwd", grid=(T // tm,),
        in_specs=[_rows(tm, hi - lo) for lo, hi in _PARTS]
        + [_rows(tm, D_MODEL), _rows(tm, D_MODEL), _const((IN_WIDTH, D_MODEL)), _const((1, D_MODEL))],
        out_specs=[_rows(tm, D_MODEL), _const((1, D_MODEL))],
        out_shape=[_sds((T, D_MODEL), F32), _sds((1, D_MODEL), F32)],
        args=(*parts, x2, dx1, w_in_t, g1), sem=("arbitrary",), exchanges=exchanges)


def _dw_in(h, parts, exchanges=()):
    T = h.shape[0]
    tk = min(TM, T)

    def body(h_ref, du_ref, dq_ref, dk_ref, dv_ref, dgt_ref, o_ref, db_ref):
        @pl.when(pl.program_id(0) == 0)
        def _():
            o_ref[...] = jnp.zeros_like(o_ref)
            db_ref[...] = jnp.zeros_like(db_ref)

        hh = h_ref[...]
        for (lo, hi), p_ref in zip(_PARTS, (du_ref, dq_ref, dk_ref, dv_ref, dgt_ref)):
            part = p_ref[...]
            o_ref[lo:hi, :] += _dot_tn(part, hh)
            db_ref[:, lo:hi] += jnp.sum(part.astype(F32), axis=0, keepdims=True)

    return _call(
        body, name="dw_in", grid=(T // tk,),
        in_specs=[_rows(tk, D_MODEL)] + [_rows(tk, hi - lo) for lo, hi in _PARTS],
        out_specs=[_const((IN_WIDTH, D_MODEL)), _const((1, IN_WIDTH))],
        out_shape=[_sds((IN_WIDTH, D_MODEL), F32), _sds((1, IN_WIDTH), F32)],
        args=(h, *parts), sem=("arbitrary",), exchanges=exchanges)


def _row_tile(rows, cap=256, mult=16):
    best = None
    for t in range(mult, min(rows, cap) + 1, mult):
        if rows % t == 0:
            best = t
    if best is None:
        raise ValueError("no row tile for %d rows" % rows)
    return best


def _pair_sum(ids, full, got):
    _, r, c = full.shape
    hr = r // 2
    tr = _row_tile(hr)
    nblk = hr // tr

    def body(ids_ref, a_ref, b_ref, own_ref, sb_ref):
        s = a_ref[...] + b_ref[...]
        sb_ref[...] = s.astype(BF16)

        @pl.when(pl.program_id(1) == ids_ref[0])
        def _():
            own_ref[...] = s

    slab = pl.BlockSpec((None, tr, c), lambda i, j, ids_ref: (j, i, 0))
    return pl.pallas_call(
        body, name="pair_sum_%dx%d" % (r, c),
        grid_spec=pltpu.PrefetchScalarGridSpec(
            num_scalar_prefetch=1, grid=(nblk, N_CHIPS),
            in_specs=[pl.BlockSpec((None, tr, c), lambda i, j, ids_ref: (j, ids_ref[1] * nblk + i, 0)), slab],
            out_specs=[pl.BlockSpec((tr, c), lambda i, j, ids_ref: (i, 0)), slab]),
        out_shape=[_sds((hr, c), F32), _sds((N_CHIPS, hr, c), BF16)],
        compiler_params=_cp("parallel", "arbitrary"),
    )(ids, full, got)


def _pair_sum_small(ids, fulls, gots):
    n = len(fulls)
    dims = [(f.shape[1] // 2, f.shape[2]) for f in fulls]

    def body(ids_ref, *refs):
        ins, outs = refs[:2 * n], refs[2 * n:]
        for k in range(n):
            s = ins[2 * k][...] + ins[2 * k + 1][...]
            outs[2 * k + 1][...] = s.astype(BF16)

            @pl.when(pl.program_id(0) == ids_ref[0])
            def _(k=k, s=s):
                outs[2 * k][...] = s

    in_specs, out_specs, out_shape = [], [], []
    for hr, c in dims:
        slab = pl.BlockSpec((None, hr, c), lambda j, ids_ref: (j, 0, 0))
        in_specs += [pl.BlockSpec((None, hr, c), lambda j, ids_ref: (j, ids_ref[1], 0)), slab]
        out_specs += [pl.BlockSpec((hr, c), lambda j, ids_ref: (0, 0)), slab]
        out_shape += [_sds((hr, c), F32), _sds((N_CHIPS, hr, c), BF16)]
    res = pl.pallas_call(
        body, name="pair_sum_small",
        grid_spec=pltpu.PrefetchScalarGridSpec(num_scalar_prefetch=1, grid=(N_CHIPS,), in_specs=in_specs,
                                               out_specs=out_specs),
        out_shape=out_shape, compiler_params=_cp("arbitrary"),
    )(ids, *[a for pair in zip(fulls, gots) for a in pair])
    return [(res[2 * k], res[2 * k + 1]) for k in range(n)]


def _chip_sum_small(ids, owns, gots):
    n = len(owns)

    def body(ids_ref, *refs):
        ins, outs = refs[:2 * n], refs[2 * n:]
        for k in range(n):
            a, b = ins[2 * k], ins[2 * k + 1]
            outs[k][...] = ((a[...] + b[0].astype(F32)) + b[1].astype(F32)) + b[2].astype(F32)

    in_specs, out_specs, out_shape = [], [], []
    for own in owns:
        hr, c = own.shape
        in_specs += [pl.BlockSpec((hr, c), lambda i, ids_ref: (0, 0)),
                     pl.BlockSpec((3, hr, c), lambda i, ids_ref: (0, 0, 0))]
        out_specs.append(pl.BlockSpec((hr, c), lambda i, ids_ref: (ids_ref[1], 0)))
        out_shape.append(_sds((2 * hr, c), F32))
    return pl.pallas_call(
        body, name="chip_sum_small",
        grid_spec=pltpu.PrefetchScalarGridSpec(num_scalar_prefetch=1, grid=(1,), in_specs=in_specs,
                                               out_specs=out_specs),
        out_shape=out_shape, compiler_params=_cp("arbitrary"),
    )(ids, *[a for pair in zip(owns, gots) for a in pair])


def _chip_sum(ids, own, got):
    hr, c = own.shape
    tr = _row_tile(hr)
    nblk = hr // tr

    def body(ids_ref, a_ref, b_ref, o_ref):
        o_ref[...] = ((a_ref[...] + b_ref[0].astype(F32)) + b_ref[1].astype(F32)) + b_ref[2].astype(F32)

    return pl.pallas_call(
        body, name="chip_sum_%dx%d" % (hr, c),
        grid_spec=pltpu.PrefetchScalarGridSpec(
            num_scalar_prefetch=1, grid=(nblk,),
            in_specs=[pl.BlockSpec((tr, c), lambda i, ids_ref: (i, 0)),
                      pl.BlockSpec((3, tr, c), lambda i, ids_ref: (0, i, 0))],
            out_specs=pl.BlockSpec((tr, c), lambda i, ids_ref: (ids_ref[1] * nblk + i, 0))),
        out_shape=_sds((2 * hr, c), F32),
        compiler_params=_cp("parallel"),
    )(ids, own, got)


def _adamw_math(w, g, m, v):
    nm = ADAM_B1 * m + (1.0 - ADAM_B1) * g
    nv = ADAM_B2 * v + (1.0 - ADAM_B2) * (g * g)
    m_hat = nm / (1.0 - ADAM_B1 ** ADAM_STEP)
    v_hat = nv / (1.0 - ADAM_B2 ** ADAM_STEP)
    return -ADAM_LR * (m_hat / (jnp.sqrt(v_hat) + ADAM_EPS) + ADAM_WD * w), nm, nv


def _adamw(w, g, m, v):
    r, c = w.shape
    tr = _row_tile(r, cap=512, mult=8)

    def body(w_ref, g_ref, m_ref, v_ref, d_ref, nm_ref, nv_ref):
        d_ref[...], nm_ref[...], nv_ref[...] = _adamw_math(w_ref[...], g_ref[...], m_ref[...], v_ref[...])

    spec = _rows(tr, c)
    return pl.pallas_call(
        body, name="adamw_%dx%d" % (r, c), grid=(r // tr,),
        in_specs=[spec] * 4, out_specs=[spec] * 3, out_shape=[_sds((r, c), F32)] * 3,
        compiler_params=_cp("parallel"),
    )(w, g, m, v)


SC_TILES = 32
SC_LANES = 16
SC_ROWS = 8


def _adamw_sparse(w, g, m, v):
    r, c = w.shape
    rows = r // SC_TILES
    step = min(rows, SC_ROWS)

    def body(w_hbm, g_hbm, m_hbm, v_hbm, d_hbm, nm_hbm, nv_hbm, wb, gb, mb, vb):
        tile = lax.axis_index("sc_subcore") * 2 + lax.axis_index("sc_core")

        @pl.loop(0, rows, step=step)
        def _(r0):
            mine = pl.ds(tile * rows + r0, step)
            for src, dst in ((w_hbm, wb), (g_hbm, gb), (m_hbm, mb), (v_hbm, vb)):
                pltpu.sync_copy(src.at[mine], dst)

            @pl.loop(0, step)
            def _(row):
                @pl.loop(0, c, step=SC_LANES)
                def _(i):
                    at = (row, pl.ds(i, SC_LANES))
                    wb[at], mb[at], vb[at] = _adamw_math(wb[at], gb[at], mb[at], vb[at])

            for src, dst in ((wb, d_hbm), (mb, nm_hbm), (vb, nv_hbm)):
                pltpu.sync_copy(src, dst.at[mine])

    return pl.kernel(
        body, name="adamw_sparse_%dx%d" % (r, c), out_type=[_sds((r, c), F32)] * 3,
        mesh=plsc.VectorSubcoreMesh(core_axis_name="sc_core", subcore_axis_name="sc_subcore"),
        scratch_types=[pltpu.VMEM((step, c), F32)] * 4,
    )(w, g, m, v)


_SMALL_NAMES = ("w_pool", "b_in", "g_mix_pre", "g_mix_post", "g_mlp_pre", "g_mlp_post", "pool_scale", "attn_sinks")
B_ROWS = -(-IN_WIDTH // D_MODEL)


def _row_block(rows):
    rows = [jnp.pad(r.astype(F32), ((0, 0), (0, D_MODEL - r.shape[1]))) for r in rows]
    return jnp.pad(jnp.concatenate(rows, axis=0), ((0, 8 - len(rows)), (0, 0)))


def _early_block(dg2, dg3, dg4, dps, dsink, loss):
    tail = jnp.concatenate([jnp.pad(dsink.reshape(1, -1), ((0, 0), (0, LANES - dsink.size))),
                            jnp.pad(loss.reshape(1, 1), ((0, 0), (0, LANES - 1)))], axis=1)
    return _row_block([dg2, dg3, dg4, dps, tail])


def _late_block(db_in, dg1):
    b = jnp.pad(db_in, ((0, 0), (0, B_ROWS * D_MODEL - IN_WIDTH))).reshape(B_ROWS, D_MODEL)
    return _row_block([b[r:r + 1] for r in range(B_ROWS)] + [dg1])


def _small_update(gearly, gmat, glate, w, m, v):
    names = _SMALL_NAMES
    n = len(names)

    def total(ref, rows):
        acc = ref[0:rows, :]
        for d in range(1, N_DEV):
            acc = acc + ref[d * rows:(d + 1) * rows, :]
        return acc

    def body(*refs):
        early_ref, gmat_ref, late_ref = refs[:3]
        w_refs, m_refs, v_refs = refs[3:3 + n], refs[3 + n:3 + 2 * n], refs[3 + 2 * n:3 + 3 * n]
        outs = refs[3 + 3 * n:]
        loss_ref, g_refs, d_refs = outs[0], outs[1:1 + n], outs[1 + n:1 + 2 * n]
        nm_refs, nv_refs = outs[1 + 2 * n:1 + 3 * n], outs[1 + 3 * n:1 + 4 * n]
        early, late = total(early_ref, 8), total(late_ref, 8)
        loss_ref[...] = jnp.sum(early[4:5, LANES:2 * LANES], axis=1, keepdims=True)
        bias = jnp.concatenate([late[r:r + 1, :] for r in range(B_ROWS - 1)]
                               + [late[B_ROWS - 1:B_ROWS, :IN_WIDTH - (B_ROWS - 1) * D_MODEL]], axis=1)
        grad = dict(b_in=bias, g_mix_pre=late[B_ROWS:B_ROWS + 1, :], g_mix_post=early[0:1, :],
                    g_mlp_pre=early[1:2, :], g_mlp_post=early[2:3, :], pool_scale=early[3:4, :POOL_WIDTH],
                    attn_sinks=early[4:5, :N_Q_HEADS])
        for i, name in enumerate(names):
            g = total(gmat_ref, 4 * POOL_GC) if name == "w_pool" else grad[name]
            g_refs[i][...] = g
            d_refs[i][...], nm_refs[i][...], nv_refs[i][...] = _adamw_math(
                w_refs[i][...], g, m_refs[i][...], v_refs[i][...])

    shapes = [_sds(w[k].shape, F32) for k in names]
    res = pl.pallas_call(
        body, name="small_update", out_shape=[_sds((1, 1), F32)] + shapes * 4,
        compiler_params=pltpu.CompilerParams(vmem_limit_bytes=VMEM_MB * 1024 * 1024),
    )(gearly, gmat, glate, *[w[k] for k in names], *[m[k] for k in names], *[v[k] for k in names])
    loss = res[0]
    per = {k: tuple(res[1 + j * n + i] for j in range(4)) for i, k in enumerate(names)}
    return loss, per


_BIG = ("w_in", "w_branch_pool", "w_branch_attn", "w_out", "w_up", "w_down")
_ORDER = ("g_mix_pre", "w_in", "b_in", "w_pool", "pool_scale", "attn_sinks", "w_branch_pool", "w_branch_attn",
          "w_out", "g_mix_post", "g_mlp_pre", "w_up", "w_down", "g_mlp_post")


def _stack_rows(slab):
    return slab.reshape(-1, slab.shape[2])


def _step(x2, tgt, seq, shards, small, ids):
    tabs = _rope_tables(seq)
    g1, g2, g3, g4 = (small[n] for n in ("g_mix_pre", "g_mix_post", "g_mlp_pre", "g_mlp_post"))
    sinks = small["attn_sinks"].reshape(N_Q_HEADS)
    w_pool = small["w_pool"].reshape(4, POOL_GC, POOL_GC)
    pool_scale = small["pool_scale"]

    def whole(shard, slabs):
        return lax.dynamic_update_slice(slabs, shard[None], (ids[0], 0, 0))

    (h, up_a, up_b, down_a, down_b, *mix_shards), [[in_slab]] = _cast_shards(
        x2, g1, *(shards[n] for n in ("w_up", "w_down", "w_branch_pool", "w_branch_attn", "w_out")),
        exchanges=[_ex_gather([shards["w_in"]])])
    w_in = _stack_rows(whole(shards["w_in"], in_slab))
    (u, q, k, v, gate), [mix_slabs] = _inproj(
        h, w_in, small["b_in"], tabs, seq, exchanges=[_ex_gather(mix_shards)])
    w_bp, w_ba, out_slab = (whole(s, g) for s, g in zip(mix_shards, mix_slabs))
    w_out = _stack_rows(out_slab)
    (y_attn, diff, y_pool), [[got_a, got_b]] = _mixers_fwd(
        q, k, v, sinks, u, w_pool, pool_scale, seq, exchanges=[_ex_gather([up_a, up_b])])
    (merged, mix, x1, h2), [[got_c, got_d]] = _merge_out(
        y_pool, y_attn, gate, x2, w_bp, w_ba, w_out, g2, g3, exchanges=[_ex_gather([down_a, down_b])])
    w_up = (whole(up_a, got_a), whole(up_b, got_b))
    w_down = (whole(down_a, got_c), whole(down_b, got_d))
    act, dff, dup, dx1, dmix, loss_acc, dg4, dg3, dg2 = _mlp_core(h2, x1, mix, tgt, w_up, w_down, g4, g3, g2)

    dw_down = _dw("down", act, dff, 1024, 1024)[0].reshape(N_CHIPS, D_FF // N_CHIPS, D_MODEL)
    (dbp, dba, dgate, dyp, dya), [[got]] = _merge_bwd(
        dmix, gate, y_pool, y_attn, w_out, w_bp, w_ba, exchanges=[_ex_pair([dw_down])])
    ps_down = _pair_sum(ids, dw_down, got)
    (dw_up,), [[got]] = _dw("up", h2, dup, 1024, 1024, shard_cols=True, exchanges=[_ex_chip([ps_down[1]])])
    half_down = _chip_sum(ids, ps_down[0], got)
    (dw_out, dw_bp, dw_ba), [[got]] = _dw_mix(merged, dmix, y_pool, dbp, y_attn, dba, exchanges=[_ex_pair([dw_up])])
    ps_up = _pair_sum(ids, dw_up, got)
    dw_mix = [dw_out.reshape(N_CHIPS, D_MODEL // N_CHIPS, D_MODEL), dw_bp, dw_ba]
    (dq, dk, dv, dsink, du, dw_pool, dps), [[got], gots, [g_down]] = _mixers_bwd(
        q, k, v, dya, sinks, tabs, dyp, diff, w_pool, pool_scale, seq,
        exchanges=[_ex_chip([ps_up[1]]), _ex_pair(dw_mix), _ex_swap([half_down])])
    half_up = _chip_sum(ids, ps_up[0], got)
    ps_mix = _pair_sum_small(ids, dw_mix, gots)
    parts = (du, dq, dk, dv, dgate)
    early = _early_block(dg2, dg3, dg4, dps, dsink[:, 0], loss_acc[0, 0])
    mat = dw_pool.reshape(4 * POOL_GC, POOL_GC)
    (dw_in_t, db_in), [gots, [gearly, gmat], [g_up]] = _dw_in(
        h, parts, exchanges=[_ex_chip([p[1] for p in ps_mix]), _ex_allgather([early, mat]), _ex_swap([half_up])])
    half_mix = _chip_sum_small(ids, [p[0] for p in ps_mix], gots)
    dw_in = dw_in_t.reshape(N_CHIPS, IN_WIDTH // N_CHIPS, D_MODEL)
    g_mix, [got] = _alone("swap_mix_pair_in", _ex_swap(half_mix), _ex_pair([dw_in]))
    ps_in = _pair_sum(ids, dw_in, got)
    (gx, dg1), [[got]] = _inproj_bwd(parts, x2, dx1, w_in, g1, exchanges=[_ex_chip([ps_in[1]])])
    [g_in], [glate] = _alone("swap_in_allgather", _ex_swap([_chip_sum(ids, ps_in[0], got)]),
                             _ex_allgather([_late_block(db_in, dg1)]))

    grads = dict(w_in=g_in, w_branch_pool=g_mix[1], w_branch_attn=g_mix[2], w_out=g_mix[0], w_up=g_up, w_down=g_down)
    return (gearly, gmat, glate), gx, grads


def kernel(x, g_mix_pre, w_in, b_in, w_pool, pool_scale, attn_sinks, w_branch_pool, w_branch_attn, w_out, g_mix_post, g_mlp_pre, w_up, w_down, g_mlp_post, loss_target, m_g_mix_pre, m_w_in, m_b_in, m_w_pool, m_pool_scale, m_attn_sinks, m_w_branch_pool, m_w_branch_attn, m_w_out, m_g_mix_post, m_g_mlp_pre, m_w_up, m_w_down, m_g_mlp_post, v_g_mix_pre, v_w_in, v_b_in, v_w_pool, v_pool_scale, v_attn_sinks, v_w_branch_pool, v_w_branch_attn, v_w_out, v_g_mix_post, v_g_mlp_pre, v_w_up, v_w_down, v_g_mlp_post):
    weights = dict(g_mix_pre=g_mix_pre, w_in=w_in, b_in=b_in, w_pool=w_pool, pool_scale=pool_scale,
                   attn_sinks=attn_sinks, w_branch_pool=w_branch_pool, w_branch_attn=w_branch_attn, w_out=w_out,
                   g_mix_post=g_mix_post, g_mlp_pre=g_mlp_pre, w_up=w_up, w_down=w_down, g_mlp_post=g_mlp_post)
    mom1 = dict(g_mix_pre=m_g_mix_pre, w_in=m_w_in, b_in=m_b_in, w_pool=m_w_pool, pool_scale=m_pool_scale,
                attn_sinks=m_attn_sinks, w_branch_pool=m_w_branch_pool, w_branch_attn=m_w_branch_attn,
                w_out=m_w_out, g_mix_post=m_g_mix_post, g_mlp_pre=m_g_mlp_pre, w_up=m_w_up, w_down=m_w_down,
                g_mlp_post=m_g_mlp_post)
    mom2 = dict(g_mix_pre=v_g_mix_pre, w_in=v_w_in, b_in=v_b_in, w_pool=v_w_pool, pool_scale=v_pool_scale,
                attn_sinks=v_attn_sinks, w_branch_pool=v_w_branch_pool, w_branch_attn=v_w_branch_attn,
                w_out=v_w_out, g_mix_post=v_g_mix_post, g_mlp_pre=v_g_mlp_pre, w_up=v_w_up, w_down=v_w_down,
                g_mlp_post=v_g_mlp_post)
    b_loc, seq, _ = x.shape
    x2 = x.reshape(b_loc * seq, D_MODEL)
    tgt = loss_target.reshape(b_loc * seq, D_MODEL)
    ids = jnp.stack([2 * lax.axis_index("x") + lax.axis_index("y"), lax.axis_index("c")]).astype(jnp.int32)

    def flat(n, a):
        return a[0].T if n == "w_in" else a[0]

    def unflat(n, a):
        return (a.T if n == "w_in" else a)[None]

    shards = {n: flat(n, weights[n]).astype(BF16) if n == "w_in" else flat(n, weights[n]) for n in _BIG}
    small = {n: weights[n] for n in _ORDER if n not in _BIG}
    (gearly, gmat, glate), gx, grads = _step(x2, tgt, seq, shards, small, ids)

    def two_d(src):
        return {n: src[n].reshape(4 * POOL_GC, POOL_GC) if n == "w_pool" else src[n] for n in _SMALL_NAMES}

    loss, per = _small_update(gearly, gmat, glate, two_d(weights), two_d(mom1), two_d(mom2))
    delta, new_m, new_v = {}, {}, {}
    for n in _SMALL_NAMES:
        grads[n], delta[n], new_m[n], new_v[n] = (a.reshape(weights[n].shape) for a in per[n])
    for n in _BIG:
        update = _adamw if n == "w_in" else _adamw_sparse
        d, nm, nv = update(flat(n, weights[n]), grads[n], flat(n, mom1[n]), flat(n, mom2[n]))
        grads[n] = unflat(n, grads[n])
        delta[n], new_m[n], new_v[n] = unflat(n, d), unflat(n, nm), unflat(n, nv)

    return (loss[0, 0], gx.reshape(x.shape), *[grads[n] for n in _ORDER], *[delta[n] for n in _ORDER],
            *[new_m[n] for n in _ORDER], *[new_v[n] for n in _ORDER])
```

```python
import jax
import jax.numpy as jnp
from jax import lax
from jax.experimental import pallas as pl
from jax.experimental.pallas import tpu as pltpu
from jax.experimental.pallas import tpu_sc as plsc

F32 = jnp.float32
BF16 = jnp.bfloat16

D_MODEL = 1024
POOL_WINDOWS = (2, 4, 8, 16)
POOL_WIDTH = 512
POOL_GC = 128
HALO = 16
HEAD_DIM = 64
N_Q_HEADS = 8
ATTN_WIDTH = 512
KV_WIDTH = 128
BLOCK = 128
NEG_INF = -1e30
ROPE_THETA = 500000.0
ROT_DIM = 16
GATE_WIDTH = 2048
IN_WIDTH = 3328
D_FF = 4096
EPS = 1e-6
SCALE = HEAD_DIM ** -0.5
C_Q, C_K, C_V, C_G = 512, 1024, 1152, 1280

ADAM_LR, ADAM_B1, ADAM_B2, ADAM_EPS, ADAM_WD, ADAM_STEP = 0.001, 0.9, 0.999, 1e-08, 0.01, 10

N_CHIPS = 4
N_DEV = 8
LANES = 128
TM = 512
VMEM_MB = 56

MESH = pl.DeviceIdType.MESH
ANY = pl.BlockSpec(memory_space=pl.ANY)


def _cp(*sem, vmem=VMEM_MB):
    return pltpu.CompilerParams(dimension_semantics=sem, vmem_limit_bytes=vmem * 1024 * 1024)


def _rows(tile, cols):
    return pl.BlockSpec((tile, cols), lambda i: (i, 0))


def _const(shape):
    nd = len(shape)
    return pl.BlockSpec(shape, lambda i: (0,) * nd)


def _sds(shape, dtype):
    return jax.ShapeDtypeStruct(shape, dtype)


def _dot(a, b):
    return jnp.dot(a, b, preferred_element_type=F32)


def _dot_nt(a, b):
    return lax.dot_general(a, b, (((1,), (1,)), ((), ())), preferred_element_type=F32)


def _dot_tn(a, b):
    return lax.dot_general(a, b, (((0,), (0,)), ((), ())), preferred_element_type=F32)


def _rms(x):
    return lax.rsqrt(jnp.mean(x * x, axis=-1, keepdims=True) + EPS)


def _norm_bwd(x, g, dout):
    r = _rms(x)
    n = x * r
    dn = dout * g
    dx = r * (dn - n * jnp.mean(dn * n, axis=-1, keepdims=True))
    return dx, jnp.sum(dout * n, axis=0, keepdims=True)


def _rot_fwd(t, c, a, bt):
    return t * c + pltpu.roll(t, LANES - 8, 1) * a + pltpu.roll(t, 8, 1) * bt


def _rot_bwd(d, c, a, bt):
    return d * c + pltpu.roll(d * a, 8, 1) + pltpu.roll(d * bt, LANES - 8, 1)


def _rope_tables(seq):
    pos = jnp.arange(seq, dtype=F32)
    inv_freq = ROPE_THETA ** (-jnp.arange(0, ROT_DIM, 2, dtype=F32) / ROT_DIM)
    ang = pos[:, None] * inv_freq[None, :]
    cos, sin = jnp.cos(ang), jnp.sin(ang)
    ones = jnp.ones((seq, HEAD_DIM - ROT_DIM), F32)
    zeros8 = jnp.zeros((seq, 8), F32)
    zrest = jnp.zeros((seq, HEAD_DIM - ROT_DIM), F32)
    c = jnp.concatenate([cos, cos, ones], axis=1)
    a = jnp.concatenate([-sin, zeros8, zrest], axis=1)
    bt = jnp.concatenate([zeros8, sin, zrest], axis=1)
    return tuple(jnp.tile(t, (1, 2)) for t in (c, a, bt))


class _Exchange:
    def __init__(self, inputs, out_shapes, sems, start, finish, aliases=None, middle=None):
        self.inputs, self.out_shapes, self.sems = list(inputs), list(out_shapes), list(sems)
        self.start, self.finish, self.aliases = start, finish, dict(aliases or {})
        self.middle = middle


def _call(body, *, name, grid, in_specs, out_specs, out_shape, args, scratch=(), sem=(), exchanges=()):
    in_specs, out_specs, out_shape, scratch = list(in_specs), list(out_specs), list(out_shape), list(scratch)
    if not exchanges:
        return pl.pallas_call(body, name=name, grid=grid, in_specs=in_specs, out_specs=out_specs,
                              out_shape=out_shape, scratch_shapes=scratch, compiler_params=_cp(*sem))(*args)
    n_in, n_out, n_scr = len(in_specs), len(out_specs), len(scratch)
    x_in = [a for ex in exchanges for a in ex.inputs]
    x_out = [s for ex in exchanges for s in ex.out_shapes]
    x_sem = [s for ex in exchanges for s in ex.sems]
    aliases, i_off, o_off = {}, n_in, n_out
    for ex in exchanges:
        for i, o in ex.aliases.items():
            aliases[i_off + i] = o_off + o
        i_off += len(ex.inputs)
        o_off += len(ex.out_shapes)

    def split(flat):
        out, pos = [], 0
        for ex, n in zip(exchanges, flat[1]):
            out.append(flat[0][pos:pos + n])
            pos += n
        return out

    def carrier(*refs):
        pos = 0
        groups = []
        for n in (n_in, len(x_in), n_out, len(x_out), n_scr, len(x_sem)):
            groups.append(refs[pos:pos + n])
            pos += n
        ins, xin, outs, xout, scr, xsem = groups
        xin = split((xin, [len(ex.inputs) for ex in exchanges]))
        xout = split((xout, [len(ex.out_shapes) for ex in exchanges]))
        xsem = split((xsem, [len(ex.sems) for ex in exchanges]))
        first = pl.program_id(0) == 0
        last = pl.program_id(0) == grid[0] - 1
        for d in range(1, len(grid)):
            first = jnp.logical_and(first, pl.program_id(d) == 0)
            last = jnp.logical_and(last, pl.program_id(d) == grid[d] - 1)

        @pl.when(first)
        def _():
            for ex, i, o, s in zip(exchanges, xin, xout, xsem):
                ex.start(i, o, s)

        if any(ex.middle for ex in exchanges):
            half = pl.program_id(0) == 5 * grid[0] // 8
            for d in range(1, len(grid)):
                half = jnp.logical_and(half, pl.program_id(d) == 0)

            @pl.when(half)
            def _():
                for ex, i, o, s in zip(exchanges, xin, xout, xsem):
                    if ex.middle:
                        ex.middle(i, o, s)

        body(*ins, *outs, *scr)

        @pl.when(last)
        def _():
            for ex, i, o, s in zip(exchanges, xin, xout, xsem):
                ex.finish(i, o, s)

    res = pl.pallas_call(
        carrier, name=name, grid=grid, in_specs=in_specs + [ANY] * len(x_in),
        out_specs=out_specs + [ANY] * len(x_out), out_shape=out_shape + x_out,
        scratch_shapes=scratch + x_sem, input_output_aliases=aliases,
        compiler_params=_cp(*(["arbitrary"] * len(grid))),
    )(*args, *x_in)
    return res[:n_out], split((res[n_out:], [len(ex.out_shapes) for ex in exchanges]))


def _alone(name, *exchanges):
    n_in = [len(ex.inputs) for ex in exchanges]
    n_out = [len(ex.out_shapes) for ex in exchanges]
    n_sem = [len(ex.sems) for ex in exchanges]
    aliases, i_off, o_off = {}, 0, 0
    for ex in exchanges:
        for i, o in ex.aliases.items():
            aliases[i_off + i] = o_off + o
        i_off += len(ex.inputs)
        o_off += len(ex.out_shapes)

    def split(flat, counts):
        out, pos = [], 0
        for n in counts:
            out.append(flat[pos:pos + n])
            pos += n
        return out

    def body(*refs):
        ins, outs, sems = split(refs, [sum(n_in), sum(n_out), sum(n_sem)])
        groups = list(zip(exchanges, split(ins, n_in), split(outs, n_out), split(sems, n_sem)))
        for ex, i, o, s in groups:
            ex.start(i, o, s)
        for ex, i, o, s in groups:
            if ex.middle:
                ex.middle(i, o, s)
        for ex, i, o, s in groups:
            ex.finish(i, o, s)

    res = pl.pallas_call(
        body, name=name, in_specs=[ANY] * sum(n_in), out_specs=[ANY] * sum(n_out),
        out_shape=[s for ex in exchanges for s in ex.out_shapes],
        scratch_shapes=[s for ex in exchanges for s in ex.sems], input_output_aliases=aliases,
    )(*[a for ex in exchanges for a in ex.inputs])
    return split(res, n_out)


def _place():
    x, y, c = lax.axis_index("x"), lax.axis_index("y"), lax.axis_index("c")
    chips = [(1 - x, y), (x, 1 - y), (1 - x, 1 - y)]
    return x, y, c, chips


def _remote(src, dst, send, recv, to):
    return pltpu.make_async_remote_copy(src_ref=src, dst_ref=dst, send_sem=send, recv_sem=recv,
                                        device_id=to, device_id_type=MESH)


def _ex_gather(shards):
    nw = len(shards)
    hrs = [s.shape[0] // 2 for s in shards]

    def copies(ins, outs, sems):
        s0, r0, s1, r1, s2, r2, fs, fr = sems
        x, y, c, _ = _place()
        me, xn, yn, dg = (x, y), (1 - x, y), (x, 1 - y), (1 - x, 1 - y)
        nbr = (xn, yn)
        sibling = (x, y, 1 - c)

        def piece(w, chip, core, part=None):
            hr = hrs[w]
            rows = pl.ds(core * hr, hr) if part is None else pl.ds(core * hr + part * (hr // 2), hr // 2)
            return outs[w].at[2 * chip[0] + chip[1], rows]

        def first(w, k, lead):
            part = k if lead else 1 - k
            send, recv = (s0, r0) if lead else (s1, r1)
            rows = pl.ds(c * hrs[w] + part * (hrs[w] // 2), hrs[w] // 2)
            return _remote(ins[w].at[rows], piece(w, me, c, part), send.at[w, k], recv.at[w, k], (*nbr[k], c))

        def landed(w, k, lead):
            part = k if lead else 1 - k
            send, recv = (s0, r0) if lead else (s1, r1)
            return _remote(piece(w, nbr[k], c, part), piece(w, nbr[k], c, part), send.at[w, k], recv.at[w, k],
                           (*nbr[k], c))

        def onward(w, k):
            return _remote(piece(w, nbr[k], c, k), piece(w, nbr[k], c, k), s2.at[w, k], r2.at[w, k],
                           (*nbr[1 - k], c))

        def arrived(w, k):
            return _remote(piece(w, dg, c, k), piece(w, dg, c, k), s2.at[w, k], r2.at[w, k], (*nbr[1 - k], c))

        def passed(w, j):
            chip = (xn, yn, dg)[j]
            return _remote(piece(w, chip, c), piece(w, chip, c), fs.at[w, j], fr.at[w, j], sibling)

        def handed(w, j):
            chip = (xn, yn, dg)[j]
            return _remote(piece(w, chip, 1 - c), piece(w, chip, 1 - c), fs.at[w, j], fr.at[w, j], sibling)

        return first, landed, onward, arrived, passed, handed

    def start(ins, outs, sems):
        first = copies(ins, outs, sems)[0]
        for lead in (True, False):
            for w in range(nw):
                for k in range(2):
                    first(w, k, lead).start()

    def middle(ins, outs, sems):
        _, landed, onward, _, passed, _ = copies(ins, outs, sems)
        for w in range(nw):
            for k in range(2):
                landed(w, k, True).wait_recv()
                onward(w, k).start()
        for w in range(nw):
            for k in range(2):
                landed(w, k, False).wait_recv()
                passed(w, k).start()

    def finish(ins, outs, sems):
        first, _, onward, arrived, passed, handed = copies(ins, outs, sems)
        for w in range(nw):
            for k in range(2):
                arrived(w, k).wait_recv()
            passed(w, 2).start()
        for w in range(nw):
            for j in range(3):
                handed(w, j).wait_recv()
        for w in range(nw):
            for k in range(2):
                first(w, k, True).wait_send()
                first(w, k, False).wait_send()
                onward(w, k).wait_send()
            for j in range(3):
                passed(w, j).wait_send()

    return _Exchange(shards, [_sds((N_CHIPS,) + s.shape, s.dtype) for s in shards],
                     [pltpu.SemaphoreType.DMA((nw, 2))] * 6 + [pltpu.SemaphoreType.DMA((nw, 3))] * 2,
                     start, finish, middle=middle)


def _ex_pair(grads):
    nw = len(grads)

    def copies(ins, outs, sems):
        x, y, c, _ = _place()
        out = []
        for w in range(nw):
            hr = grads[w].shape[1] // 2
            out.append(_remote(ins[w].at[:, pl.ds((1 - c) * hr, hr)], outs[w], sems[0].at[w], sems[1].at[w],
                               (x, y, 1 - c)))
        return out

    def start(ins, outs, sems):
        for cp in copies(ins, outs, sems):
            cp.start()

    def finish(ins, outs, sems):
        for cp in copies(ins, outs, sems):
            cp.wait()

    return _Exchange(grads, [_sds((N_CHIPS, g.shape[1] // 2, g.shape[2]), F32) for g in grads],
                     [pltpu.SemaphoreType.DMA((nw,))] * 2, start, finish)


def _ex_chip(pieces):
    nw = len(pieces)

    def copies(ins, outs, sems):
        x, y, c, chips = _place()
        return [_remote(ins[w].at[2 * cx + cy], outs[w].at[k], sems[0].at[w, k], sems[1].at[w, k], (cx, cy, c))
                for w in range(nw) for k, (cx, cy) in enumerate(chips)]

    def start(ins, outs, sems):
        for cp in copies(ins, outs, sems):
            cp.start()

    def finish(ins, outs, sems):
        for cp in copies(ins, outs, sems):
            cp.wait()

    return _Exchange(pieces, [_sds((3,) + p.shape[1:], BF16) for p in pieces],
                     [pltpu.SemaphoreType.DMA((nw, 3))] * 2, start, finish)


def _ex_swap(fulls):
    nw = len(fulls)

    def start(ins, outs, sems):
        x, y, c, _ = _place()
        for w in range(nw):
            hr = fulls[w].shape[0] // 2
            mine = pl.ds(c * hr, hr)
            _remote(ins[w].at[mine], outs[w].at[mine], sems[0].at[w], sems[1].at[w], (x, y, 1 - c)).start()

    def finish(ins, outs, sems):
        x, y, c, _ = _place()
        for w in range(nw):
            hr = fulls[w].shape[0] // 2
            mine, theirs = pl.ds(c * hr, hr), pl.ds((1 - c) * hr, hr)
            _remote(ins[w].at[mine], outs[w].at[mine], sems[0].at[w], sems[1].at[w], (x, y, 1 - c)).wait_send()
            _remote(ins[w].at[theirs], outs[w].at[theirs], sems[0].at[w], sems[1].at[w], (x, y, 1 - c)).wait_recv()

    return _Exchange(fulls, [_sds(f.shape, F32) for f in fulls], [pltpu.SemaphoreType.DMA((nw,))] * 2,
                     start, finish, aliases={w: w for w in range(nw)})


def _ex_allgather(blocks):
    nb = len(blocks)

    def copies(ins, outs, sems):
        send, recv, lsem = sems
        x, y, c, chips = _place()
        me, sibling = (x, y, c), (x, y, 1 - c)

        def rows(b, px, py, pc):
            m_per = blocks[b].shape[0]
            return outs[b].at[pl.ds((4 * px + 2 * py + pc) * m_per, m_per), :]

        def copy(b, k, blk, to, src=None):
            return _remote(rows(b, *blk) if src is None else src, rows(b, *blk), send.at[b, k], recv.at[b, k], to)

        def mine(b):
            return pltpu.make_async_copy(ins[b], rows(b, *me), lsem.at[b])

        def first(b, k):
            return copy(b, k, me, sibling if k == 0 else (*chips[k - 1], c), src=ins[b])

        def passed(b, j):
            return copy(b, 4 + j, (*chips[j], c), sibling)

        def landed(b, j):
            return copy(b, 1 + j, (*chips[j], c), me)

        def handed(b, k):
            return copy(b, 0, sibling, me) if k == 0 else copy(b, 3 + k, (*chips[k - 1], 1 - c), me)

        return mine, first, passed, landed, handed

    def start(ins, outs, sems):
        mine, first, _, _, _ = copies(ins, outs, sems)
        for b in range(nb):
            mine(b).start()
            for k in range(4):
                first(b, k).start()

    def finish(ins, outs, sems):
        mine, first, passed, landed, handed = copies(ins, outs, sems)
        sent = []
        for b in range(nb):
            for j in range(3):
                landed(b, j).wait_recv()
                cp = passed(b, j)
                cp.start()
                sent.append(cp)
        for b in range(nb):
            for k in range(4):
                handed(b, k).wait_recv()
            for k in range(4):
                first(b, k).wait_send()
        for cp in sent:
            cp.wait_send()
        for b in range(nb):
            mine(b).wait()

    return _Exchange(blocks, [_sds((N_DEV * b.shape[0], b.shape[1]), F32) for b in blocks],
                     [pltpu.SemaphoreType.DMA((nb, 7)), pltpu.SemaphoreType.DMA((nb, 7)), pltpu.SemaphoreType.DMA((nb,))],
                     start, finish)


def _cast_shards(w_up, w_down, w_bp, w_ba, w_out, exchanges=()):
    r, c = w_up.shape
    tr = HALF // 2
    steps = r // tr

    def body(up_ref, down_ref, bp_ref, ba_ref, out_ref, ua_ref, ub_ref, da_ref, db_ref, bpo_ref, bao_ref, outo_ref):
        i = pl.program_id(0)

        @pl.when(i == 0)
        def _():
            for src, dst in ((bp_ref, bpo_ref), (ba_ref, bao_ref), (out_ref, outo_ref)):
                dst[...] = src[...].astype(BF16)

        @pl.when(i < steps // 2)
        def _():
            ua_ref[...] = up_ref[...].astype(BF16)
            da_ref[...] = down_ref[...].astype(BF16)

        @pl.when(i >= steps // 2)
        def _():
            ub_ref[...] = up_ref[...].astype(BF16)
            db_ref[...] = down_ref[...].astype(BF16)

    rows = _rows(tr, c)
    first = pl.BlockSpec((tr, c), lambda i: (jnp.minimum(i, steps // 2 - 1), 0))
    second = pl.BlockSpec((tr, c), lambda i: (jnp.maximum(i - steps // 2, 0), 0))
    half = _sds((HALF, c), BF16)
    return _call(
        body, name="cast_shards", grid=(steps,),
        in_specs=[rows, rows, _const(w_bp.shape), _const(w_ba.shape), _const(w_out.shape)],
        out_specs=[first, second, first, second, _const(w_bp.shape), _const(w_ba.shape), _const(w_out.shape)],
        out_shape=[half, half, half, half, _sds(w_bp.shape, BF16), _sds(w_ba.shape, BF16), _sds(w_out.shape, BF16)],
        args=(w_up, w_down, w_bp, w_ba, w_out), sem=("arbitrary",), exchanges=exchanges)


def _inproj(x2, g1, w_in_t, b_in, tabs, seq, exchanges=()):
    T = x2.shape[0]
    tm = min(TM, seq)
    nseq = seq // tm

    def body(x_ref, g_ref, w_ref, b_ref, c_ref, a_ref, bt_ref, h_ref, u_ref, q_ref, k_ref, v_ref, gate_ref):
        x = x_ref[...]
        h = (x * _rms(x) * g_ref[...]).astype(BF16)
        h_ref[...] = h

        def proj(lo, hi):
            return _dot_nt(h, w_ref[lo:hi, :]) + b_ref[:, lo:hi]

        c, a, bt = c_ref[...], a_ref[...], bt_ref[...]
        u_ref[...] = proj(0, C_Q)
        q = proj(C_Q, C_K)
        for p in range(4):
            sl = slice(LANES * p, LANES * (p + 1))
            q_ref[:, sl] = (_rot_fwd(q[:, sl], c, a, bt) * SCALE).astype(BF16)
        kv = proj(C_K, C_G)
        k_ref[...] = _rot_fwd(kv[:, :KV_WIDTH], c, a, bt).astype(BF16)
        v_ref[...] = kv[:, KV_WIDTH:].astype(BF16)
        for j in range(2):
            lo = C_G + D_MODEL * j
            gate_ref[:, D_MODEL * j:D_MODEL * (j + 1)] = jax.nn.sigmoid(proj(lo, lo + D_MODEL)).astype(BF16)

    tab = pl.BlockSpec((tm, LANES), lambda i: (i % nseq, 0))
    return _call(
        body, name="inproj", grid=(T // tm,),
        in_specs=[_rows(tm, D_MODEL), _const((1, D_MODEL)), _const((IN_WIDTH, D_MODEL)), _const((1, IN_WIDTH)),
                  tab, tab, tab],
        out_specs=[_rows(tm, D_MODEL), _rows(tm, POOL_WIDTH), _rows(tm, ATTN_WIDTH), _rows(tm, KV_WIDTH),
                   _rows(tm, KV_WIDTH), _rows(tm, GATE_WIDTH)],
        out_shape=[_sds((T, D_MODEL), BF16), _sds((T, POOL_WIDTH), F32), _sds((T, ATTN_WIDTH), BF16),
                   _sds((T, KV_WIDTH), BF16), _sds((T, KV_WIDTH), BF16), _sds((T, GATE_WIDTH), BF16)],
        args=(x2, g1, w_in_t, b_in, *tabs), sem=("parallel",), exchanges=exchanges)


def _inv_count(pos, w):
    return 1.0 / jnp.minimum(pos + 1, w).astype(F32)


def _pool_tile(i, tp, nseq, u_ref, prev_ref, w_ref, s_ref, diff_ref, y_ref):
    first = (i % nseq) == 0
    prev = jnp.where(first, 0.0, prev_ref[...])
    ext = jnp.concatenate([prev, u_ref[...]], axis=0)
    pos = (i % nseq) * tp + lax.broadcasted_iota(jnp.int32, (tp, 1), 0)
    for gi, w in enumerate(POOL_WINDOWS):
        sl = slice(POOL_GC * gi, POOL_GC * (gi + 1))
        xg = ext[:, sl]
        s = xg
        sh = 1
        while sh < w:
            s = s + pltpu.roll(s, sh, 0)
            sh *= 2
        pooled = s[HALO:] * _inv_count(pos, w)
        diff = (pooled - xg[HALO:]).astype(BF16)
        diff_ref[:, sl] = diff
        mixed = _dot(diff, w_ref[gi].astype(BF16))
        y_ref[:, sl] = (mixed * s_ref[:, sl]).astype(BF16)


def _pool_specs(tp):
    per = tp // HALO
    return [_rows(tp, POOL_WIDTH), pl.BlockSpec((HALO, POOL_WIDTH), lambda i: (jnp.maximum(i * per - 1, 0), 0)),
            _const((4, POOL_GC, POOL_GC)), _const((1, POOL_WIDTH))]


GROUP = 4
GROWS = GROUP * BLOCK


def _attn_masks(n):
    qi = lax.broadcasted_iota(jnp.int32, (GROWS, 2 * BLOCK), 0) % BLOCK
    kj = lax.broadcasted_iota(jnp.int32, (GROWS, 2 * BLOCK), 1)
    rel = qi + BLOCK - kj
    valid = (rel >= 0) & (rel < BLOCK) & (kj >= jnp.where(n > 0, 0, BLOCK))
    lo = lax.broadcasted_iota(jnp.int32, (BLOCK, LANES), 1) < HEAD_DIM
    return valid, lo


def _by_example(bl, *arrays):
    return [a.reshape(bl, a.shape[0] // bl, a.shape[1]) for a in arrays]


def _stack_heads(ref, h, lo):
    keep = lo if h == 0 else jnp.logical_not(lo)
    pieces = []
    for p in (2 * h, 2 * h + 1):
        xp = ref[:, LANES * p:LANES * (p + 1)].astype(F32)
        for e in range(2):
            t = xp if e == h else pltpu.roll(xp, HEAD_DIM, 1)
            pieces.append(jnp.where(keep, t, 0.0).astype(BF16))
    return jnp.concatenate(pieces, axis=0)


def _unstack_heads(stacked, h, lo):
    pairs = []
    for j in range(2):
        parts = []
        for e in range(2):
            t = stacked[BLOCK * (2 * j + e):BLOCK * (2 * j + e + 1)]
            parts.append(t if e == h else pltpu.roll(t, HEAD_DIM, 1))
        pairs.append(jnp.where(lo, parts[0], parts[1]))
    return pairs


def _sink_rows(sink_ref, h):
    head = lax.broadcasted_iota(jnp.int32, (GROWS, 1), 0) // BLOCK
    col = jnp.zeros((GROWS, 1), F32) + sink_ref[GROUP * h]
    for g in range(1, GROUP):
        col = jnp.where(head == g, sink_ref[GROUP * h + g], col)
    return col


def _group_probs(qs, kk, valid, sink):
    s = jnp.where(valid, _dot_nt(qs, kk), NEG_INF)
    m = jnp.maximum(jnp.max(s, axis=1, keepdims=True), sink)
    ex = jnp.exp(s - m)
    es = jnp.exp(sink - m)
    inv = 1.0 / (jnp.sum(ex, axis=1, keepdims=True) + es)
    return ex * inv, es * inv


def _mixers_fwd(q, k, v, sinks, u, w_pool, pool_scale, seq, exchanges=()):
    T = q.shape[0]
    nb = seq // BLOCK
    bl = T // seq
    tp = T // nb
    nseq = seq // tp

    def body(sink_ref, q_ref, kp_ref, kc_ref, vp_ref, vc_ref, u_ref, prev_ref, w_ref, s_ref, o_ref, diff_ref, y_ref):
        n = pl.program_id(0)
        valid, lo = _attn_masks(n)
        for b in range(bl):
            kk = jnp.concatenate([kp_ref[b], kc_ref[b]], axis=0)
            vv = jnp.concatenate([vp_ref[b], vc_ref[b]], axis=0)
            for h in range(2):
                qs = _stack_heads(q_ref.at[b], h, lo)
                pr, _ = _group_probs(qs, kk, valid, _sink_rows(sink_ref, h))
                o = _dot(pr.astype(BF16), vv)
                for j, pair in enumerate(_unstack_heads(o, h, lo)):
                    p = 2 * h + j
                    o_ref[b, :, LANES * p:LANES * (p + 1)] = pair.astype(BF16)
        _pool_tile(n, tp, nseq, u_ref, prev_ref, w_ref, s_ref, diff_ref, y_ref)

    cur = lambda n: (0, n, 0)
    prv = lambda n: (0, jnp.maximum(n - 1, 0), 0)
    kv = lambda m: pl.BlockSpec((bl, BLOCK, KV_WIDTH), m)
    res = _call(
        body, name="mixers_fwd", grid=(nb,),
        in_specs=[pl.BlockSpec(memory_space=pltpu.SMEM), pl.BlockSpec((bl, BLOCK, ATTN_WIDTH), cur),
                  kv(prv), kv(cur), kv(prv), kv(cur)] + _pool_specs(tp),
        out_specs=[pl.BlockSpec((bl, BLOCK, ATTN_WIDTH), cur), _rows(tp, POOL_WIDTH), _rows(tp, POOL_WIDTH)],
        out_shape=[_sds((bl, seq, ATTN_WIDTH), BF16), _sds((T, POOL_WIDTH), BF16), _sds((T, POOL_WIDTH), BF16)],
        args=(sinks, *_by_example(bl, q, k, k, v, v), u, u, w_pool, pool_scale), sem=("parallel",),
        exchanges=exchanges)
    outs, rest = res if exchanges else (res, None)
    return [outs[0].reshape(T, ATTN_WIDTH), outs[1], outs[2]], rest


def _branch(y, w_ref):
    return jnp.concatenate([_dot(y, w_ref[j]) for j in range(N_CHIPS)], axis=1)


def _merge_out(y_pool, y_attn, gate, x2, w_bp, w_ba, w_out, g2, g3, exchanges=()):
    T = x2.shape[0]
    tm = min(TM, T)

    def body(yp_ref, ya_ref, gate_ref, x_ref, wbp_ref, wba_ref, wo_ref, g2_ref, g3_ref,
             mg_ref, mix_ref, x1_ref, h2_ref):
        bp, ba = _branch(yp_ref[...], wbp_ref), _branch(ya_ref[...], wba_ref)
        merged = (gate_ref[:, :D_MODEL].astype(F32) * bp + gate_ref[:, D_MODEL:].astype(F32) * ba).astype(BF16)
        mg_ref[...] = merged
        mix = _dot(merged, wo_ref[...])
        mix_ref[...] = mix
        x1 = x_ref[...] + mix * _rms(mix) * g2_ref[...]
        x1_ref[...] = x1
        h2_ref[...] = (x1 * _rms(x1) * g3_ref[...]).astype(BF16)

    return _call(
        body, name="merge_out", grid=(T // tm,),
        in_specs=[_rows(tm, POOL_WIDTH), _rows(tm, ATTN_WIDTH), _rows(tm, GATE_WIDTH), _rows(tm, D_MODEL),
                  _const(w_bp.shape), _const(w_ba.shape), _const((D_MODEL, D_MODEL)),
                  _const((1, D_MODEL)), _const((1, D_MODEL))],
        out_specs=[_rows(tm, D_MODEL)] * 4,
        out_shape=[_sds((T, D_MODEL), BF16), _sds((T, D_MODEL), F32), _sds((T, D_MODEL), F32),
                   _sds((T, D_MODEL), BF16)],
        args=(y_pool, y_attn, gate, x2, w_bp, w_ba, w_out, g2, g3), sem=("parallel",), exchanges=exchanges)


HALF = D_MODEL // 2
TM_MLP = 256


def _mlp_core(h2, x1, mix, tgt, w_up, w_down, g4, g3, g2):
    T = h2.shape[0]
    tm = min(TM_MLP, T)

    def body(h_ref, x1_ref, mix_ref, t_ref, g_ref, g3_ref, g2_ref, ua_hbm, ub_hbm, da_hbm, db_hbm,
             act_ref, dff_ref, dup_ref, dx1_ref, dmix_ref, loss_ref, dg_ref, dg3_ref, dg2_ref,
             wu, wd, relu_scr, sems):
        def weight_copy(i):
            src, dst = ((ua_hbm, wu.at[:, :HALF]), (ub_hbm, wu.at[:, HALF:]),
                        (da_hbm, wd.at[:, :HALF]), (db_hbm, wd.at[:, HALF:]))[i]
            return pltpu.make_async_copy(src, dst, sems.at[i])

        @pl.when(pl.program_id(0) == 0)
        def _():
            for i in range(4):
                weight_copy(i).start()
            loss_ref[...] = jnp.zeros_like(loss_ref)
            for ref in (dg_ref, dg3_ref, dg2_ref):
                ref[...] = jnp.zeros_like(ref)
            weight_copy(0).wait()
            weight_copy(1).wait()

        h = h_ref[...]
        ff = None
        for j in range(N_CHIPS):
            lo = D_MODEL * j
            relu = jnp.maximum(_dot(h, wu[j]), 0.0)
            if j == 0:
                @pl.when(pl.program_id(0) == 0)
                def _():
                    weight_copy(2).wait()
                    weight_copy(3).wait()
            relu_scr[:, lo:lo + D_MODEL] = relu
            act = jnp.square(relu).astype(BF16)
            act_ref[:, lo:lo + D_MODEL] = act
            t = _dot(act, wd[j])
            ff = t if ff is None else ff + t
        g = g_ref[...]
        x1 = x1_ref[...]
        err = x1 + ff * _rms(ff) * g - t_ref[...]
        loss_ref[...] += jnp.sum(err * err) * (0.5 / D_MODEL)
        dy = err * (1.0 / D_MODEL)
        dff, dg = _norm_bwd(ff, g, dy)
        dg_ref[...] += dg
        dff = dff.astype(BF16)
        dff_ref[...] = dff
        dh2 = None
        for j in range(N_CHIPS):
            lo = D_MODEL * j
            dup = (_dot_nt(dff, wd[j]) * (2.0 * relu_scr[:, lo:lo + D_MODEL])).astype(BF16)
            dup_ref[:, lo:lo + D_MODEL] = dup
            t = _dot_nt(dup, wu[j])
            dh2 = t if dh2 is None else dh2 + t
        dx, dg3 = _norm_bwd(x1, g3_ref[...], dh2)
        dx1 = dy + dx
        dx1_ref[...] = dx1
        dg3_ref[...] += dg3
        dmix, dg2 = _norm_bwd(mix_ref[...], g2_ref[...], dx1)
        dmix_ref[...] = dmix.astype(BF16)
        dg2_ref[...] += dg2

    slabs = pltpu.VMEM((N_CHIPS, D_MODEL, D_MODEL), BF16)
    gain = _const((1, D_MODEL))
    return pl.pallas_call(
        body, name="mlp_core", grid=(T // tm,),
        in_specs=[_rows(tm, D_MODEL)] * 4 + [gain] * 3 + [ANY] * 4,
        out_specs=[_rows(tm, D_FF), _rows(tm, D_MODEL), _rows(tm, D_FF), _rows(tm, D_MODEL), _rows(tm, D_MODEL),
                   _const((8, LANES)), gain, gain, gain],
        out_shape=[_sds((T, D_FF), BF16), _sds((T, D_MODEL), BF16), _sds((T, D_FF), BF16), _sds((T, D_MODEL), F32),
                   _sds((T, D_MODEL), BF16), _sds((8, LANES), F32)] + [_sds((1, D_MODEL), F32)] * 3,
        scratch_shapes=[slabs] * 2 + [pltpu.VMEM((tm, D_FF), F32), pltpu.SemaphoreType.DMA((4,))],
        compiler_params=_cp("arbitrary"),
    )(h2, x1, mix, tgt, g4, g3, g2, *w_up, *w_down)


def _dw(tag, a, g, ta, tn, shard_cols=False, exchanges=()):
    T, ka = a.shape
    n = g.shape[1]
    tk = min(2 * TM, T)
    nk = T // tk

    def body(a_ref, g_ref, o_ref):
        @pl.when(pl.program_id(2) == 0)
        def _():
            o_ref[...] = jnp.zeros_like(o_ref)

        o_ref[...] += _dot_tn(a_ref[...], g_ref[...])

    if shard_cols:
        per = (n // N_CHIPS) // tn
        out_spec = pl.BlockSpec((None, ta, tn), lambda i, j, k: (j // per, i, j % per))
        out_shape = _sds((N_CHIPS, ka, n // N_CHIPS), F32)
    else:
        out_spec = pl.BlockSpec((ta, tn), lambda i, j, k: (i, j))
        out_shape = _sds((ka, n), F32)
    return _call(
        body, name="dw_" + tag, grid=(ka // ta, n // tn, nk),
        in_specs=[pl.BlockSpec((tk, ta), lambda i, j, k: (k, i)), pl.BlockSpec((tk, tn), lambda i, j, k: (k, j))],
        out_specs=[out_spec], out_shape=[out_shape],
        args=(a, g), sem=("parallel", "parallel", "arbitrary"), exchanges=exchanges)


def _dw_mix(merged, dmix, y_pool, dbp, y_attn, dba, exchanges=()):
    T = merged.shape[0]
    tk = min(2 * TM, T)
    c = D_MODEL // N_CHIPS

    def body(mg_ref, dmix_ref, yp_ref, dbp_ref, ya_ref, dba_ref, out_ref, bp_ref, ba_ref):
        @pl.when(pl.program_id(0) == 0)
        def _():
            for ref in (out_ref, bp_ref, ba_ref):
                ref[...] = jnp.zeros_like(ref)

        out_ref[...] += _dot_tn(mg_ref[...], dmix_ref[...])
        for y_ref, d_ref, o_ref in ((yp_ref, dbp_ref, bp_ref), (ya_ref, dba_ref, ba_ref)):
            res = _dot_tn(y_ref[...], d_ref[...])
            for j in range(N_CHIPS):
                o_ref[j] += res[:, c * j:c * (j + 1)]

    slabs = (N_CHIPS, POOL_WIDTH, c)
    return _call(
        body, name="dw_mix", grid=(T // tk,),
        in_specs=[_rows(tk, D_MODEL), _rows(tk, D_MODEL), _rows(tk, POOL_WIDTH), _rows(tk, D_MODEL),
                  _rows(tk, ATTN_WIDTH), _rows(tk, D_MODEL)],
        out_specs=[_const((D_MODEL, D_MODEL)), _const(slabs), _const(slabs)],
        out_shape=[_sds((D_MODEL, D_MODEL), F32), _sds(slabs, F32), _sds(slabs, F32)],
        args=(merged, dmix, y_pool, dbp, y_attn, dba), sem=("arbitrary",), exchanges=exchanges)


def _merge_bwd(dmix, gate, y_pool, y_attn, w_out, w_bp, w_ba, exchanges=()):
    T = dmix.shape[0]
    tm = min(TM, T)

    def body(dmix_ref, gate_ref, yp_ref, ya_ref, wo_ref, wbp_ref, wba_ref,
             dbp_ref, dba_ref, dgate_ref, dyp_ref, dya_ref):
        dm = _dot_nt(dmix_ref[...], wo_ref[...])
        for j, (y_ref, db_ref, w_ref, dy_ref) in enumerate(
                ((yp_ref, dbp_ref, wbp_ref, dyp_ref), (ya_ref, dba_ref, wba_ref, dya_ref))):
            sl = slice(D_MODEL * j, D_MODEL * (j + 1))
            gt = gate_ref[:, sl].astype(F32)
            db = (dm * gt).astype(BF16)
            db_ref[...] = db
            dgate_ref[:, sl] = (dm * _branch(y_ref[...], w_ref) * gt * (1.0 - gt)).astype(BF16)
            cw = D_MODEL // N_CHIPS
            dy = _dot_nt(db[:, :cw], w_ref[0])
            for c in range(1, N_CHIPS):
                dy = dy + _dot_nt(db[:, cw * c:cw * (c + 1)], w_ref[c])
            dy_ref[...] = dy.astype(dy_ref.dtype)

    return _call(
        body, name="merge_bwd", grid=(T // tm,),
        in_specs=[_rows(tm, D_MODEL), _rows(tm, GATE_WIDTH), _rows(tm, POOL_WIDTH), _rows(tm, ATTN_WIDTH),
                  _const((D_MODEL, D_MODEL)), _const(w_bp.shape), _const(w_ba.shape)],
        out_specs=[_rows(tm, D_MODEL), _rows(tm, D_MODEL), _rows(tm, GATE_WIDTH), _rows(tm, POOL_WIDTH),
                   _rows(tm, ATTN_WIDTH)],
        out_shape=[_sds((T, D_MODEL), BF16), _sds((T, D_MODEL), BF16), _sds((T, GATE_WIDTH), BF16),
                   _sds((T, POOL_WIDTH), F32), _sds((T, ATTN_WIDTH), BF16)],
        args=(dmix, gate, y_pool, y_attn, w_out, w_bp, w_ba), sem=("parallel",), exchanges=exchanges)


def _mixers_bwd(q, k, v, do, sinks, tabs, dyp, diff, w_pool, pool_scale, seq, exchanges=()):
    T = q.shape[0]
    nb = seq // BLOCK
    bl = T // seq
    steps = nb + 1
    tp = T // nb
    nseq = seq // tp
    per = tp // HALO
    last_halo = T // HALO - 1

    def body(sink_ref, q_ref, do_ref, kp_ref, kc_ref, vp_ref, vc_ref, c_ref, a_ref, bt_ref, cp_ref, ap_ref, btp_ref,
             dy_ref, nxt_ref, diff_ref, w_ref, s_ref,
             dq_ref, dk_ref, dv_ref, dsink_ref, du_ref, dw_ref, ds_ref, ck_ref, cv_ref):
        n = pl.program_id(0)

        @pl.when(n == 0)
        def _():
            for ref in (dsink_ref, ck_ref, cv_ref, dw_ref, ds_ref):
                ref[...] = jnp.zeros_like(ref)

        @pl.when(n < nb)
        def _():
            _pool_bwd_tile(n, tp, nseq, dy_ref, nxt_ref, diff_ref, w_ref, s_ref, du_ref, dw_ref, ds_ref)
            valid, lo = _attn_masks(n)
            for b in range(bl):
                kk = jnp.concatenate([kp_ref[b], kc_ref[b]], axis=0)
                vv = jnp.concatenate([vp_ref[b], vc_ref[b]], axis=0)
                dk_acc = jnp.zeros((2 * BLOCK, KV_WIDTH), F32)
                dv_acc = jnp.zeros((2 * BLOCK, KV_WIDTH), F32)
                for h in range(2):
                    qs = _stack_heads(q_ref.at[b], h, lo)
                    dos = _stack_heads(do_ref.at[b], h, lo)
                    pr, ps = _group_probs(qs, kk, valid, _sink_rows(sink_ref, h))
                    dp = _dot_nt(dos, vv)
                    delta = jnp.sum(pr * dp, axis=1, keepdims=True)
                    ds = (pr * (dp - delta)).astype(BF16)
                    dsk = ps * delta
                    for g in range(GROUP):
                        idx = GROUP * h + g
                        dsink_ref[idx:idx + 1, :] += (jnp.zeros((1, LANES), F32)
                                                      - jnp.sum(dsk[BLOCK * g:BLOCK * (g + 1)]))
                    dk_acc = dk_acc + _dot_tn(ds, qs)
                    dv_acc = dv_acc + _dot_tn(pr.astype(BF16), dos)
                    for j, pair in enumerate(_unstack_heads(_dot(ds, kk) * SCALE, h, lo)):
                        sl = slice(LANES * (2 * h + j), LANES * (2 * h + j + 1))
                        dq_ref[b, :, sl] = _rot_bwd(pair, c_ref[...], a_ref[...], bt_ref[...]).astype(BF16)
                fin_k = ck_ref[b] + dk_acc[:BLOCK]
                dk_ref[b] = _rot_bwd(fin_k, cp_ref[...], ap_ref[...], btp_ref[...]).astype(BF16)
                dv_ref[b] = (cv_ref[b] + dv_acc[:BLOCK]).astype(BF16)
                ck_ref[b] = dk_acc[BLOCK:]
                cv_ref[b] = dv_acc[BLOCK:]

        @pl.when(n == nb)
        def _():
            for b in range(bl):
                dk_ref[b] = _rot_bwd(ck_ref[b], cp_ref[...], ap_ref[...], btp_ref[...]).astype(BF16)
                dv_ref[b] = cv_ref[b].astype(BF16)

    cur = lambda n: (0, jnp.minimum(n, nb - 1), 0)
    prv = lambda n: (0, jnp.clip(n - 1, 0, nb - 1), 0)
    tcur = lambda n: (jnp.minimum(n, nb - 1), 0)
    tprv = lambda n: (jnp.clip(n - 1, 0, nb - 1), 0)
    wide = lambda m: pl.BlockSpec((bl, BLOCK, ATTN_WIDTH), m)
    kv = lambda m: pl.BlockSpec((bl, BLOCK, KV_WIDTH), m)
    tab = lambda m: pl.BlockSpec((BLOCK, LANES), m)
    tile = lambda n: (jnp.minimum(n, nb - 1), 0)
    halo = lambda n: (jnp.minimum((jnp.minimum(n, nb - 1) + 1) * per, last_halo), 0)
    rows = pl.BlockSpec((tp, POOL_WIDTH), tile)
    res = _call(
        body, name="mixers_bwd", grid=(steps,),
        in_specs=[pl.BlockSpec(memory_space=pltpu.SMEM), wide(cur), wide(cur), kv(prv), kv(cur), kv(prv), kv(cur),
                  tab(tcur), tab(tcur), tab(tcur), tab(tprv), tab(tprv), tab(tprv),
                  rows, pl.BlockSpec((HALO, POOL_WIDTH), halo), rows, _const((4, POOL_GC, POOL_GC)),
                  _const((1, POOL_WIDTH))],
        out_specs=[wide(cur), kv(prv), kv(prv), _const((8, LANES)), rows, _const((4, POOL_GC, POOL_GC)),
                   _const((1, POOL_WIDTH))],
        out_shape=[_sds((bl, seq, ATTN_WIDTH), BF16), _sds((bl, seq, KV_WIDTH), BF16),
                   _sds((bl, seq, KV_WIDTH), BF16), _sds((8, LANES), F32), _sds((T, POOL_WIDTH), BF16),
                   _sds((4, POOL_GC, POOL_GC), F32), _sds((1, POOL_WIDTH), F32)],
        scratch=[pltpu.VMEM((bl, BLOCK, KV_WIDTH), F32), pltpu.VMEM((bl, BLOCK, KV_WIDTH), F32)],
        args=(sinks, *_by_example(bl, q, do, k, k, v, v), *tabs, *tabs, dyp, dyp, diff, w_pool, pool_scale),
        sem=("arbitrary",), exchanges=exchanges)
    outs, rest = (res if exchanges else (res, None))
    outs = [outs[0].reshape(T, ATTN_WIDTH), outs[1].reshape(T, KV_WIDTH), outs[2].reshape(T, KV_WIDTH), *outs[3:]]
    return (outs, rest) if exchanges else outs


def _pool_bwd_tile(i, tp, nseq, dy_ref, nxt_ref, diff_ref, w_ref, s_ref, du_ref, dw_ref, ds_ref):
    last = (i % nseq) == nseq - 1
    nxt = jnp.where(last, 0.0, nxt_ref[...])
    ext = jnp.concatenate([dy_ref[...], nxt], axis=0) * s_ref[...]
    pos = (i % nseq) * tp + lax.broadcasted_iota(jnp.int32, (tp + HALO, 1), 0)
    for gi, w in enumerate(POOL_WINDOWS):
        sl = slice(POOL_GC * gi, POOL_GC * (gi + 1))
        wg = w_ref[gi].astype(BF16)
        dmx = ext[:, sl].astype(BF16)
        ddiff = _dot_nt(dmx, wg)
        s = ddiff * _inv_count(pos, w)
        sh = 1
        while sh < w:
            s = s + pltpu.roll(s, tp + HALO - sh, 0)
            sh *= 2
        du_ref[:, sl] = (s[:tp] - ddiff[:tp]).astype(BF16)
        dg = diff_ref[:, sl]
        dw_ref[gi] += _dot_tn(dg, dmx[:tp])
        ds_ref[:, sl] += jnp.sum(dy_ref[:, sl] * _dot(dg, wg), axis=0, keepdims=True)


_PARTS = ((0, C_Q), (C_Q, C_K), (C_K, C_V), (C_V, C_G), (C_G, IN_WIDTH))


def _inproj_bwd(parts, x2, dx1, w_in_t, g1, exchanges=()):
    T = x2.shape[0]
    tm = min(TM, T)

    def body(du_ref, dq_ref, dk_ref, dv_ref, dgt_ref, x_ref, dx1_ref, w_ref, g_ref, gx_ref, dg_ref):
        @pl.when(pl.program_id(0) == 0)
        def _():
            dg_ref[...] = jnp.zeros_like(dg_ref)

        dh = jnp.zeros((tm, D_MODEL), F32)
        for (lo, hi), p_ref in zip(_PARTS, (du_ref, dq_ref, dk_ref, dv_ref, dgt_ref)):
            dh = dh + _dot(p_ref[...], w_ref[lo:hi, :])
        dx, dg = _norm_bwd(x_ref[...], g_ref[...], dh)
        gx_ref[...] = dx1_ref[...] + dx
        dg_ref[...] += dg

    return _call(
        body, name="inproj_bwd", grid=(T // tm,),
        in_specs=[_rows(tm, hi - lo) for lo, hi in _PARTS]
        + [_rows(tm, D_MODEL), _rows(tm, D_MODEL), _const((IN_WIDTH, D_MODEL)), _const((1, D_MODEL))],
        out_specs=[_rows(tm, D_MODEL), _const((1, D_MODEL))],
        out_shape=[_sds((T, D_MODEL), F32), _sds((1, D_MODEL), F32)],
        args=(*parts, x2, dx1, w_in_t, g1), sem=("arbitrary",), exchanges=exchanges)


def _dw_in(h, parts, exchanges=()):
    T = h.shape[0]
    tk = min(TM, T)

    def body(h_ref, du_ref, dq_ref, dk_ref, dv_ref, dgt_ref, o_ref, db_ref):
        @pl.when(pl.program_id(0) == 0)
        def _():
            o_ref[...] = jnp.zeros_like(o_ref)
            db_ref[...] = jnp.zeros_like(db_ref)

        hh = h_ref[...]
        for (lo, hi), p_ref in zip(_PARTS, (du_ref, dq_ref, dk_ref, dv_ref, dgt_ref)):
            part = p_ref[...]
            o_ref[lo:hi, :] += _dot_tn(part, hh)
            db_ref[:, lo:hi] += jnp.sum(part.astype(F32), axis=0, keepdims=True)

    return _call(
        body, name="dw_in", grid=(T // tk,),
        in_specs=[_rows(tk, D_MODEL)] + [_rows(tk, hi - lo) for lo, hi in _PARTS],
        out_specs=[_const((IN_WIDTH, D_MODEL)), _const((1, IN_WIDTH))],
        out_shape=[_sds((IN_WIDTH, D_MODEL), F32), _sds((1, IN_WIDTH), F32)],
        args=(h, *parts), sem=("arbitrary",), exchanges=exchanges)


def _row_tile(rows, cap=256, mult=16):
    best = None
    for t in range(mult, min(rows, cap) + 1, mult):
        if rows % t == 0:
            best = t
    if best is None:
        raise ValueError("no row tile for %d rows" % rows)
    return best


def _pair_sum(ids, full, got):
    _, r, c = full.shape
    hr = r // 2
    tr = _row_tile(hr)
    nblk = hr // tr

    def body(ids_ref, a_ref, b_ref, own_ref, sb_ref):
        s = a_ref[...] + b_ref[...]
        sb_ref[...] = s.astype(BF16)

        @pl.when(pl.program_id(1) == ids_ref[0])
        def _():
            own_ref[...] = s

    slab = pl.BlockSpec((None, tr, c), lambda i, j, ids_ref: (j, i, 0))
    return pl.pallas_call(
        body, name="pair_sum_%dx%d" % (r, c),
        grid_spec=pltpu.PrefetchScalarGridSpec(
            num_scalar_prefetch=1, grid=(nblk, N_CHIPS),
            in_specs=[pl.BlockSpec((None, tr, c), lambda i, j, ids_ref: (j, ids_ref[1] * nblk + i, 0)), slab],
            out_specs=[pl.BlockSpec((tr, c), lambda i, j, ids_ref: (i, 0)), slab]),
        out_shape=[_sds((hr, c), F32), _sds((N_CHIPS, hr, c), BF16)],
        compiler_params=_cp("parallel", "arbitrary"),
    )(ids, full, got)


def _pair_sum_small(ids, fulls, gots):
    n = len(fulls)
    dims = [(f.shape[1] // 2, f.shape[2]) for f in fulls]

    def body(ids_ref, *refs):
        ins, outs = refs[:2 * n], refs[2 * n:]
        for k in range(n):
            s = ins[2 * k][...] + ins[2 * k + 1][...]
            outs[2 * k + 1][...] = s.astype(BF16)

            @pl.when(pl.program_id(0) == ids_ref[0])
            def _(k=k, s=s):
                outs[2 * k][...] = s

    in_specs, out_specs, out_shape = [], [], []
    for hr, c in dims:
        slab = pl.BlockSpec((None, hr, c), lambda j, ids_ref: (j, 0, 0))
        in_specs += [pl.BlockSpec((None, hr, c), lambda j, ids_ref: (j, ids_ref[1], 0)), slab]
        out_specs += [pl.BlockSpec((hr, c), lambda j, ids_ref: (0, 0)), slab]
        out_shape += [_sds((hr, c), F32), _sds((N_CHIPS, hr, c), BF16)]
    res = pl.pallas_call(
        body, name="pair_sum_small",
        grid_spec=pltpu.PrefetchScalarGridSpec(num_scalar_prefetch=1, grid=(N_CHIPS,), in_specs=in_specs,
                                               out_specs=out_specs),
        out_shape=out_shape, compiler_params=_cp("arbitrary"),
    )(ids, *[a for pair in zip(fulls, gots) for a in pair])
    return [(res[2 * k], res[2 * k + 1]) for k in range(n)]


def _chip_sum_small(ids, owns, gots):
    n = len(owns)

    def body(ids_ref, *refs):
        ins, outs = refs[:2 * n], refs[2 * n:]
        for k in range(n):
            a, b = ins[2 * k], ins[2 * k + 1]
            outs[k][...] = ((a[...] + b[0].astype(F32)) + b[1].astype(F32)) + b[2].astype(F32)

    in_specs, out_specs, out_shape = [], [], []
    for own in owns:
        hr, c = own.shape
        in_specs += [pl.BlockSpec((hr, c), lambda i, ids_ref: (0, 0)),
                     pl.BlockSpec((3, hr, c), lambda i, ids_ref: (0, 0, 0))]
        out_specs.append(pl.BlockSpec((hr, c), lambda i, ids_ref: (ids_ref[1], 0)))
        out_shape.append(_sds((2 * hr, c), F32))
    return pl.pallas_call(
        body, name="chip_sum_small",
        grid_spec=pltpu.PrefetchScalarGridSpec(num_scalar_prefetch=1, grid=(1,), in_specs=in_specs,
                                               out_specs=out_specs),
        out_shape=out_shape, compiler_params=_cp("arbitrary"),
    )(ids, *[a for pair in zip(owns, gots) for a in pair])


def _chip_sum(ids, own, got):
    hr, c = own.shape
    tr = _row_tile(hr)
    nblk = hr // tr

    def body(ids_ref, a_ref, b_ref, o_ref):
        o_ref[...] = ((a_ref[...] + b_ref[0].astype(F32)) + b_ref[1].astype(F32)) + b_ref[2].astype(F32)

    return pl.pallas_call(
        body, name="chip_sum_%dx%d" % (hr, c),
        grid_spec=pltpu.PrefetchScalarGridSpec(
            num_scalar_prefetch=1, grid=(nblk,),
            in_specs=[pl.BlockSpec((tr, c), lambda i, ids_ref: (i, 0)),
                      pl.BlockSpec((3, tr, c), lambda i, ids_ref: (0, i, 0))],
            out_specs=pl.BlockSpec((tr, c), lambda i, ids_ref: (ids_ref[1] * nblk + i, 0))),
        out_shape=_sds((2 * hr, c), F32),
        compiler_params=_cp("parallel"),
    )(ids, own, got)


def _adamw_math(w, g, m, v):
    nm = ADAM_B1 * m + (1.0 - ADAM_B1) * g
    nv = ADAM_B2 * v + (1.0 - ADAM_B2) * (g * g)
    m_hat = nm / (1.0 - ADAM_B1 ** ADAM_STEP)
    v_hat = nv / (1.0 - ADAM_B2 ** ADAM_STEP)
    return -ADAM_LR * (m_hat / (jnp.sqrt(v_hat) + ADAM_EPS) + ADAM_WD * w), nm, nv


def _adamw(w, g, m, v):
    r, c = w.shape
    tr = _row_tile(r, cap=512, mult=8)

    def body(w_ref, g_ref, m_ref, v_ref, d_ref, nm_ref, nv_ref):
        d_ref[...], nm_ref[...], nv_ref[...] = _adamw_math(w_ref[...], g_ref[...], m_ref[...], v_ref[...])

    spec = _rows(tr, c)
    return pl.pallas_call(
        body, name="adamw_%dx%d" % (r, c), grid=(r // tr,),
        in_specs=[spec] * 4, out_specs=[spec] * 3, out_shape=[_sds((r, c), F32)] * 3,
        compiler_params=_cp("parallel"),
    )(w, g, m, v)


SC_TILES = 32
SC_LANES = 16
SC_ROWS = 8


def _adamw_sparse(w, g, m, v):
    r, c = w.shape
    rows = r // SC_TILES
    step = min(rows, SC_ROWS)

    def body(w_hbm, g_hbm, m_hbm, v_hbm, d_hbm, nm_hbm, nv_hbm, wb, gb, mb, vb):
        tile = lax.axis_index("sc_subcore") * 2 + lax.axis_index("sc_core")

        @pl.loop(0, rows, step=step)
        def _(r0):
            mine = pl.ds(tile * rows + r0, step)
            for src, dst in ((w_hbm, wb), (g_hbm, gb), (m_hbm, mb), (v_hbm, vb)):
                pltpu.sync_copy(src.at[mine], dst)

            @pl.loop(0, step)
            def _(row):
                @pl.loop(0, c, step=SC_LANES)
                def _(i):
                    at = (row, pl.ds(i, SC_LANES))
                    wb[at], mb[at], vb[at] = _adamw_math(wb[at], gb[at], mb[at], vb[at])

            for src, dst in ((wb, d_hbm), (mb, nm_hbm), (vb, nv_hbm)):
                pltpu.sync_copy(src, dst.at[mine])

    return pl.kernel(
        body, name="adamw_sparse_%dx%d" % (r, c), out_type=[_sds((r, c), F32)] * 3,
        mesh=plsc.VectorSubcoreMesh(core_axis_name="sc_core", subcore_axis_name="sc_subcore"),
        scratch_types=[pltpu.VMEM((step, c), F32)] * 4,
    )(w, g, m, v)


_SMALL_NAMES = ("w_pool", "b_in", "g_mix_pre", "g_mix_post", "g_mlp_pre", "g_mlp_post", "pool_scale", "attn_sinks")
B_ROWS = -(-IN_WIDTH // D_MODEL)


def _row_block(rows):
    rows = [jnp.pad(r.astype(F32), ((0, 0), (0, D_MODEL - r.shape[1]))) for r in rows]
    return jnp.pad(jnp.concatenate(rows, axis=0), ((0, 8 - len(rows)), (0, 0)))


def _early_block(dg2, dg3, dg4, dps, dsink, loss):
    tail = jnp.concatenate([jnp.pad(dsink.reshape(1, -1), ((0, 0), (0, LANES - dsink.size))),
                            jnp.pad(loss.reshape(1, 1), ((0, 0), (0, LANES - 1)))], axis=1)
    return _row_block([dg2, dg3, dg4, dps, tail])


def _late_block(db_in, dg1):
    b = jnp.pad(db_in, ((0, 0), (0, B_ROWS * D_MODEL - IN_WIDTH))).reshape(B_ROWS, D_MODEL)
    return _row_block([b[r:r + 1] for r in range(B_ROWS)] + [dg1])


def _small_update(gearly, gmat, glate, w, m, v):
    names = _SMALL_NAMES
    n = len(names)

    def total(ref, rows):
        acc = ref[0:rows, :]
        for d in range(1, N_DEV):
            acc = acc + ref[d * rows:(d + 1) * rows, :]
        return acc

    def body(*refs):
        early_ref, gmat_ref, late_ref = refs[:3]
        w_refs, m_refs, v_refs = refs[3:3 + n], refs[3 + n:3 + 2 * n], refs[3 + 2 * n:3 + 3 * n]
        outs = refs[3 + 3 * n:]
        loss_ref, g_refs, d_refs = outs[0], outs[1:1 + n], outs[1 + n:1 + 2 * n]
        nm_refs, nv_refs = outs[1 + 2 * n:1 + 3 * n], outs[1 + 3 * n:1 + 4 * n]
        early, late = total(early_ref, 8), total(late_ref, 8)
        loss_ref[...] = jnp.sum(early[4:5, LANES:2 * LANES], axis=1, keepdims=True)
        bias = jnp.concatenate([late[r:r + 1, :] for r in range(B_ROWS - 1)]
                               + [late[B_ROWS - 1:B_ROWS, :IN_WIDTH - (B_ROWS - 1) * D_MODEL]], axis=1)
        grad = dict(b_in=bias, g_mix_pre=late[B_ROWS:B_ROWS + 1, :], g_mix_post=early[0:1, :],
                    g_mlp_pre=early[1:2, :], g_mlp_post=early[2:3, :], pool_scale=early[3:4, :POOL_WIDTH],
                    attn_sinks=early[4:5, :N_Q_HEADS])
        for i, name in enumerate(names):
            g = total(gmat_ref, 4 * POOL_GC) if name == "w_pool" else grad[name]
            g_refs[i][...] = g
            d_refs[i][...], nm_refs[i][...], nv_refs[i][...] = _adamw_math(
                w_refs[i][...], g, m_refs[i][...], v_refs[i][...])

    shapes = [_sds(w[k].shape, F32) for k in names]
    res = pl.pallas_call(
        body, name="small_update", out_shape=[_sds((1, 1), F32)] + shapes * 4,
        compiler_params=pltpu.CompilerParams(vmem_limit_bytes=VMEM_MB * 1024 * 1024),
    )(gearly, gmat, glate, *[w[k] for k in names], *[m[k] for k in names], *[v[k] for k in names])
    loss = res[0]
    per = {k: tuple(res[1 + j * n + i] for j in range(4)) for i, k in enumerate(names)}
    return loss, per


_BIG = ("w_in", "w_branch_pool", "w_branch_attn", "w_out", "w_up", "w_down")
_ORDER = ("g_mix_pre", "w_in", "b_in", "w_pool", "pool_scale", "attn_sinks", "w_branch_pool", "w_branch_attn",
          "w_out", "g_mix_post", "g_mlp_pre", "w_up", "w_down", "g_mlp_post")


def _stack_rows(slab):
    return slab.reshape(-1, slab.shape[2])


def _step(x2, tgt, seq, shards, small, ids):
    tabs = _rope_tables(seq)
    g1, g2, g3, g4 = (small[n] for n in ("g_mix_pre", "g_mix_post", "g_mlp_pre", "g_mlp_post"))
    sinks = small["attn_sinks"].reshape(N_Q_HEADS)
    w_pool = small["w_pool"].reshape(4, POOL_GC, POOL_GC)
    pool_scale = small["pool_scale"]

    def whole(shard, slabs):
        return lax.dynamic_update_slice(slabs, shard[None], (ids[0], 0, 0))

    (up_a, up_b, down_a, down_b, *mix_shards), [[in_slab]] = _cast_shards(
        *(shards[n] for n in ("w_up", "w_down", "w_branch_pool", "w_branch_attn", "w_out")),
        exchanges=[_ex_gather([shards["w_in"]])])
    w_in = _stack_rows(whole(shards["w_in"], in_slab))
    (h, u, q, k, v, gate), [mix_slabs] = _inproj(
        x2, g1, w_in, small["b_in"], tabs, seq, exchanges=[_ex_gather(mix_shards)])
    w_bp, w_ba, out_slab = (whole(s, g) for s, g in zip(mix_shards, mix_slabs))
    w_out = _stack_rows(out_slab)
    (y_attn, diff, y_pool), [[got_a, got_b]] = _mixers_fwd(
        q, k, v, sinks, u, w_pool, pool_scale, seq, exchanges=[_ex_gather([up_a, up_b])])
    (merged, mix, x1, h2), [[got_c, got_d]] = _merge_out(
        y_pool, y_attn, gate, x2, w_bp, w_ba, w_out, g2, g3, exchanges=[_ex_gather([down_a, down_b])])
    w_up = (whole(up_a, got_a), whole(up_b, got_b))
    w_down = (whole(down_a, got_c), whole(down_b, got_d))
    act, dff, dup, dx1, dmix, loss_acc, dg4, dg3, dg2 = _mlp_core(h2, x1, mix, tgt, w_up, w_down, g4, g3, g2)

    dw_down = _dw("down", act, dff, 1024, 1024)[0].reshape(N_CHIPS, D_FF // N_CHIPS, D_MODEL)
    (dbp, dba, dgate, dyp, dya), [[got]] = _merge_bwd(
        dmix, gate, y_pool, y_attn, w_out, w_bp, w_ba, exchanges=[_ex_pair([dw_down])])
    ps_down = _pair_sum(ids, dw_down, got)
    (dw_up,), [[got]] = _dw("up", h2, dup, 1024, 1024, shard_cols=True, exchanges=[_ex_chip([ps_down[1]])])
    half_down = _chip_sum(ids, ps_down[0], got)
    (dw_out, dw_bp, dw_ba), [[got]] = _dw_mix(merged, dmix, y_pool, dbp, y_attn, dba, exchanges=[_ex_pair([dw_up])])
    ps_up = _pair_sum(ids, dw_up, got)
    dw_mix = [dw_out.reshape(N_CHIPS, D_MODEL // N_CHIPS, D_MODEL), dw_bp, dw_ba]
    (dq, dk, dv, dsink, du, dw_pool, dps), [[got], gots, [g_down]] = _mixers_bwd(
        q, k, v, dya, sinks, tabs, dyp, diff, w_pool, pool_scale, seq,
        exchanges=[_ex_chip([ps_up[1]]), _ex_pair(dw_mix), _ex_swap([half_down])])
    half_up = _chip_sum(ids, ps_up[0], got)
    ps_mix = _pair_sum_small(ids, dw_mix, gots)
    parts = (du, dq, dk, dv, dgate)
    early = _early_block(dg2, dg3, dg4, dps, dsink[:, 0], loss_acc[0, 0])
    mat = dw_pool.reshape(4 * POOL_GC, POOL_GC)
    (dw_in_t, db_in), [gots, [gearly, gmat], [g_up]] = _dw_in(
        h, parts, exchanges=[_ex_chip([p[1] for p in ps_mix]), _ex_allgather([early, mat]), _ex_swap([half_up])])
    half_mix = _chip_sum_small(ids, [p[0] for p in ps_mix], gots)
    dw_in = dw_in_t.reshape(N_CHIPS, IN_WIDTH // N_CHIPS, D_MODEL)
    g_mix, [got] = _alone("swap_mix_pair_in", _ex_swap(half_mix), _ex_pair([dw_in]))
    ps_in = _pair_sum(ids, dw_in, got)
    (gx, dg1), [[got]] = _inproj_bwd(parts, x2, dx1, w_in, g1, exchanges=[_ex_chip([ps_in[1]])])
    [g_in], [glate] = _alone("swap_in_allgather", _ex_swap([_chip_sum(ids, ps_in[0], got)]),
                             _ex_allgather([_late_block(db_in, dg1)]))

    grads = dict(w_in=g_in, w_branch_pool=g_mix[1], w_branch_attn=g_mix[2], w_out=g_mix[0], w_up=g_up, w_down=g_down)
    return (gearly, gmat, glate), gx, grads


def kernel(x, g_mix_pre, w_in, b_in, w_pool, pool_scale, attn_sinks, w_branch_pool, w_branch_attn, w_out, g_mix_post, g_mlp_pre, w_up, w_down, g_mlp_post, loss_target, m_g_mix_pre, m_w_in, m_b_in, m_w_pool, m_pool_scale, m_attn_sinks, m_w_branch_pool, m_w_branch_attn, m_w_out, m_g_mix_post, m_g_mlp_pre, m_w_up, m_w_down, m_g_mlp_post, v_g_mix_pre, v_w_in, v_b_in, v_w_pool, v_pool_scale, v_attn_sinks, v_w_branch_pool, v_w_branch_attn, v_w_out, v_g_mix_post, v_g_mlp_pre, v_w_up, v_w_down, v_g_mlp_post):
    weights = dict(g_mix_pre=g_mix_pre, w_in=w_in, b_in=b_in, w_pool=w_pool, pool_scale=pool_scale,
                   attn_sinks=attn_sinks, w_branch_pool=w_branch_pool, w_branch_attn=w_branch_attn, w_out=w_out,
                   g_mix_post=g_mix_post, g_mlp_pre=g_mlp_pre, w_up=w_up, w_down=w_down, g_mlp_post=g_mlp_post)
    mom1 = dict(g_mix_pre=m_g_mix_pre, w_in=m_w_in, b_in=m_b_in, w_pool=m_w_pool, pool_scale=m_pool_scale,
                attn_sinks=m_attn_sinks, w_branch_pool=m_w_branch_pool, w_branch_attn=m_w_branch_attn,
                w_out=m_w_out, g_mix_post=m_g_mix_post, g_mlp_pre=m_g_mlp_pre, w_up=m_w_up, w_down=m_w_down,
                g_mlp_post=m_g_mlp_post)
    mom2 = dict(g_mix_pre=v_g_mix_pre, w_in=v_w_in, b_in=v_b_in, w_pool=v_w_pool, pool_scale=v_pool_scale,
                attn_sinks=v_attn_sinks, w_branch_pool=v_w_branch_pool, w_branch_attn=v_w_branch_attn,
                w_out=v_w_out, g_mix_post=v_g_mix_post, g_mlp_pre=v_g_mlp_pre, w_up=v_w_up, w_down=v_w_down,
                g_mlp_post=v_g_mlp_post)
    b_loc, seq, _ = x.shape
    x2 = x.reshape(b_loc * seq, D_MODEL)
    tgt = loss_target.reshape(b_loc * seq, D_MODEL)
    ids = jnp.stack([2 * lax.axis_index("x") + lax.axis_index("y"), lax.axis_index("c")]).astype(jnp.int32)

    def flat(n, a):
        return a[0].T if n == "w_in" else a[0]

    def unflat(n, a):
        return (a.T if n == "w_in" else a)[None]

    shards = {n: flat(n, weights[n]).astype(BF16) if n == "w_in" else flat(n, weights[n]) for n in _BIG}
    small = {n: weights[n] for n in _ORDER if n not in _BIG}
    (gearly, gmat, glate), gx, grads = _step(x2, tgt, seq, shards, small, ids)

    def two_d(src):
        return {n: src[n].reshape(4 * POOL_GC, POOL_GC) if n == "w_pool" else src[n] for n in _SMALL_NAMES}

    loss, per = _small_update(gearly, gmat, glate, two_d(weights), two_d(mom1), two_d(mom2))
    delta, new_m, new_v = {}, {}, {}
    for n in _SMALL_NAMES:
        grads[n], delta[n], new_m[n], new_v[n] = (a.reshape(weights[n].shape) for a in per[n])
    for n in _BIG:
        update = _adamw if n == "w_in" else _adamw_sparse
        d, nm, nv = update(flat(n, weights[n]), grads[n], flat(n, mom1[n]), flat(n, mom2[n]))
        grads[n] = unflat(n, grads[n])
        delta[n], new_m[n], new_v[n] = unflat(n, d), unflat(n, nm), unflat(n, nv)

    return (loss[0, 0], gx.reshape(x.shape), *[grads[n] for n in _ORDER], *[delta[n] for n in _ORDER],
            *[new_m[n] for n in _ORDER], *[new_v[n] for n in _ORDER])
```

```python
import jax
import jax.numpy as jnp
from jax import lax
from jax.experimental import pallas as pl
from jax.experimental.pallas import tpu as pltpu
from jax.experimental.pallas import tpu_sc as plsc

F32 = jnp.float32
BF16 = jnp.bfloat16

D_MODEL = 1024
POOL_WINDOWS = (2, 4, 8, 16)
POOL_WIDTH = 512
POOL_GC = 128
HALO = 16
HEAD_DIM = 64
N_Q_HEADS = 8
ATTN_WIDTH = 512
KV_WIDTH = 128
BLOCK = 128
NEG_INF = -1e30
ROPE_THETA = 500000.0
ROT_DIM = 16
GATE_WIDTH = 2048
IN_WIDTH = 3328
D_FF = 4096
EPS = 1e-6
SCALE = HEAD_DIM ** -0.5
C_Q, C_K, C_V, C_G = 512, 1024, 1152, 1280

ADAM_LR, ADAM_B1, ADAM_B2, ADAM_EPS, ADAM_WD, ADAM_STEP = 0.001, 0.9, 0.999, 1e-08, 0.01, 10

N_CHIPS = 4
N_DEV = 8
LANES = 128
TM = 512
VMEM_MB = 56

MESH = pl.DeviceIdType.MESH
ANY = pl.BlockSpec(memory_space=pl.ANY)


def _cp(*sem, vmem=VMEM_MB):
    return pltpu.CompilerParams(dimension_semantics=sem, vmem_limit_bytes=vmem * 1024 * 1024)


def _rows(tile, cols):
    return pl.BlockSpec((tile, cols), lambda i: (i, 0))


def _const(shape):
    nd = len(shape)
    return pl.BlockSpec(shape, lambda i: (0,) * nd)


def _sds(shape, dtype):
    return jax.ShapeDtypeStruct(shape, dtype)


def _dot(a, b):
    return jnp.dot(a, b, preferred_element_type=F32)


def _dot_nt(a, b):
    return lax.dot_general(a, b, (((1,), (1,)), ((), ())), preferred_element_type=F32)


def _dot_tn(a, b):
    return lax.dot_general(a, b, (((0,), (0,)), ((), ())), preferred_element_type=F32)


def _rms(x):
    return lax.rsqrt(jnp.mean(x * x, axis=-1, keepdims=True) + EPS)


def _norm_bwd(x, g, dout):
    r = _rms(x)
    n = x * r
    dn = dout * g
    dx = r * (dn - n * jnp.mean(dn * n, axis=-1, keepdims=True))
    return dx, jnp.sum(dout * n, axis=0, keepdims=True)


def _rot_fwd(t, c, a, bt):
    return t * c + pltpu.roll(t, LANES - 8, 1) * a + pltpu.roll(t, 8, 1) * bt


def _rot_bwd(d, c, a, bt):
    return d * c + pltpu.roll(d * a, 8, 1) + pltpu.roll(d * bt, LANES - 8, 1)


def _rope_tables(seq):
    pos = jnp.arange(seq, dtype=F32)
    inv_freq = ROPE_THETA ** (-jnp.arange(0, ROT_DIM, 2, dtype=F32) / ROT_DIM)
    ang = pos[:, None] * inv_freq[None, :]
    cos, sin = jnp.cos(ang), jnp.sin(ang)
    ones = jnp.ones((seq, HEAD_DIM - ROT_DIM), F32)
    zeros8 = jnp.zeros((seq, 8), F32)
    zrest = jnp.zeros((seq, HEAD_DIM - ROT_DIM), F32)
    c = jnp.concatenate([cos, cos, ones], axis=1)
    a = jnp.concatenate([-sin, zeros8, zrest], axis=1)
    bt = jnp.concatenate([zeros8, sin, zrest], axis=1)
    return tuple(jnp.tile(t, (1, 2)) for t in (c, a, bt))


class _Exchange:
    def __init__(self, inputs, out_shapes, sems, start, finish, aliases=None, rounds=()):
        self.inputs, self.out_shapes, self.sems = list(inputs), list(out_shapes), list(sems)
        self.start, self.finish, self.aliases = start, finish, dict(aliases or {})
        self.rounds = list(rounds)


def _call(body, *, name, grid, in_specs, out_specs, out_shape, args, scratch=(), sem=(), exchanges=()):
    in_specs, out_specs, out_shape, scratch = list(in_specs), list(out_specs), list(out_shape), list(scratch)
    if not exchanges:
        return pl.pallas_call(body, name=name, grid=grid, in_specs=in_specs, out_specs=out_specs,
                              out_shape=out_shape, scratch_shapes=scratch, compiler_params=_cp(*sem))(*args)
    n_in, n_out, n_scr = len(in_specs), len(out_specs), len(scratch)
    x_in = [a for ex in exchanges for a in ex.inputs]
    x_out = [s for ex in exchanges for s in ex.out_shapes]
    x_sem = [s for ex in exchanges for s in ex.sems]
    aliases, i_off, o_off = {}, n_in, n_out
    for ex in exchanges:
        for i, o in ex.aliases.items():
            aliases[i_off + i] = o_off + o
        i_off += len(ex.inputs)
        o_off += len(ex.out_shapes)

    def split(flat):
        out, pos = [], 0
        for ex, n in zip(exchanges, flat[1]):
            out.append(flat[0][pos:pos + n])
            pos += n
        return out

    def carrier(*refs):
        pos = 0
        groups = []
        for n in (n_in, len(x_in), n_out, len(x_out), n_scr, len(x_sem)):
            groups.append(refs[pos:pos + n])
            pos += n
        ins, xin, outs, xout, scr, xsem = groups
        xin = split((xin, [len(ex.inputs) for ex in exchanges]))
        xout = split((xout, [len(ex.out_shapes) for ex in exchanges]))
        xsem = split((xsem, [len(ex.sems) for ex in exchanges]))
        first = pl.program_id(0) == 0
        last = pl.program_id(0) == grid[0] - 1
        for d in range(1, len(grid)):
            first = jnp.logical_and(first, pl.program_id(d) == 0)
            last = jnp.logical_and(last, pl.program_id(d) == grid[d] - 1)

        @pl.when(first)
        def _():
            for ex, i, o, s in zip(exchanges, xin, xout, xsem):
                ex.start(i, o, s)

        for eighths in sorted({e for ex in exchanges for e, _ in ex.rounds}):
            due = pl.program_id(0) == eighths * grid[0] // 8
            for d in range(1, len(grid)):
                due = jnp.logical_and(due, pl.program_id(d) == 0)

            @pl.when(due)
            def _(eighths=eighths):
                for ex, i, o, s in zip(exchanges, xin, xout, xsem):
                    for e, fn in ex.rounds:
                        if e == eighths:
                            fn(i, o, s)

        body(*ins, *outs, *scr)

        @pl.when(last)
        def _():
            for ex, i, o, s in zip(exchanges, xin, xout, xsem):
                ex.finish(i, o, s)

    res = pl.pallas_call(
        carrier, name=name, grid=grid, in_specs=in_specs + [ANY] * len(x_in),
        out_specs=out_specs + [ANY] * len(x_out), out_shape=out_shape + x_out,
        scratch_shapes=scratch + x_sem, input_output_aliases=aliases,
        compiler_params=_cp(*(["arbitrary"] * len(grid))),
    )(*args, *x_in)
    return res[:n_out], split((res[n_out:], [len(ex.out_shapes) for ex in exchanges]))


def _alone(name, *exchanges):
    n_in = [len(ex.inputs) for ex in exchanges]
    n_out = [len(ex.out_shapes) for ex in exchanges]
    n_sem = [len(ex.sems) for ex in exchanges]
    aliases, i_off, o_off = {}, 0, 0
    for ex in exchanges:
        for i, o in ex.aliases.items():
            aliases[i_off + i] = o_off + o
        i_off += len(ex.inputs)
        o_off += len(ex.out_shapes)

    def split(flat, counts):
        out, pos = [], 0
        for n in counts:
            out.append(flat[pos:pos + n])
            pos += n
        return out

    def body(*refs):
        ins, outs, sems = split(refs, [sum(n_in), sum(n_out), sum(n_sem)])
        groups = list(zip(exchanges, split(ins, n_in), split(outs, n_out), split(sems, n_sem)))
        for ex, i, o, s in groups:
            ex.start(i, o, s)
        for ex, i, o, s in groups:
            for _, fn in sorted(ex.rounds, key=lambda r: r[0]):
                fn(i, o, s)
        for ex, i, o, s in groups:
            ex.finish(i, o, s)

    res = pl.pallas_call(
        body, name=name, in_specs=[ANY] * sum(n_in), out_specs=[ANY] * sum(n_out),
        out_shape=[s for ex in exchanges for s in ex.out_shapes],
        scratch_shapes=[s for ex in exchanges for s in ex.sems], input_output_aliases=aliases,
    )(*[a for ex in exchanges for a in ex.inputs])
    return split(res, n_out)


def _place():
    x, y, c = lax.axis_index("x"), lax.axis_index("y"), lax.axis_index("c")
    chips = [(1 - x, y), (x, 1 - y), (1 - x, 1 - y)]
    return x, y, c, chips


def _remote(src, dst, send, recv, to):
    return pltpu.make_async_remote_copy(src_ref=src, dst_ref=dst, send_sem=send, recv_sem=recv,
                                        device_id=to, device_id_type=MESH)


def _ex_gather(shards):
    nw = len(shards)
    hrs = [s.shape[0] // 2 for s in shards]

    def copies(ins, outs, sems):
        s0, r0, s1, r1, s2, r2, fs, fr = sems
        x, y, c, _ = _place()
        me, xn, yn, dg = (x, y), (1 - x, y), (x, 1 - y), (1 - x, 1 - y)
        nbr = (xn, yn)
        sibling = (x, y, 1 - c)

        def piece(w, chip, core, part=None):
            hr = hrs[w]
            rows = pl.ds(core * hr, hr) if part is None else pl.ds(core * hr + part * (hr // 2), hr // 2)
            return outs[w].at[2 * chip[0] + chip[1], rows]

        def first(w, k, lead):
            part = k if lead else 1 - k
            send, recv = (s0, r0) if lead else (s1, r1)
            rows = pl.ds(c * hrs[w] + part * (hrs[w] // 2), hrs[w] // 2)
            return _remote(ins[w].at[rows], piece(w, me, c, part), send.at[w, k], recv.at[w, k], (*nbr[k], c))

        def landed(w, k, lead):
            part = k if lead else 1 - k
            send, recv = (s0, r0) if lead else (s1, r1)
            return _remote(piece(w, nbr[k], c, part), piece(w, nbr[k], c, part), send.at[w, k], recv.at[w, k],
                           (*nbr[k], c))

        def onward(w, k):
            return _remote(piece(w, nbr[k], c, k), piece(w, nbr[k], c, k), s2.at[w, k], r2.at[w, k],
                           (*nbr[1 - k], c))

        def arrived(w, k):
            return _remote(piece(w, dg, c, k), piece(w, dg, c, k), s2.at[w, k], r2.at[w, k], (*nbr[1 - k], c))

        def passed(w, j):
            chip = (xn, yn, dg)[j]
            return _remote(piece(w, chip, c), piece(w, chip, c), fs.at[w, j], fr.at[w, j], sibling)

        def handed(w, j):
            chip = (xn, yn, dg)[j]
            return _remote(piece(w, chip, 1 - c), piece(w, chip, 1 - c), fs.at[w, j], fr.at[w, j], sibling)

        return first, landed, onward, arrived, passed, handed

    def start(ins, outs, sems):
        first = copies(ins, outs, sems)[0]
        for lead in (True, False):
            for w in range(nw):
                for k in range(2):
                    first(w, k, lead).start()

    def pass_on(ins, outs, sems):
        _, landed, onward, _, _, _ = copies(ins, outs, sems)
        for w in range(nw):
            for k in range(2):
                landed(w, k, True).wait_recv()
                onward(w, k).start()

    def hand_over(ins, outs, sems):
        _, landed, _, _, passed, _ = copies(ins, outs, sems)
        for w in range(nw):
            for k in range(2):
                landed(w, k, False).wait_recv()
                passed(w, k).start()

    def finish(ins, outs, sems):
        first, _, onward, arrived, passed, handed = copies(ins, outs, sems)
        for w in range(nw):
            for k in range(2):
                arrived(w, k).wait_recv()
            passed(w, 2).start()
        for w in range(nw):
            for j in range(3):
                handed(w, j).wait_recv()
        for w in range(nw):
            for k in range(2):
                first(w, k, True).wait_send()
                first(w, k, False).wait_send()
                onward(w, k).wait_send()
            for j in range(3):
                passed(w, j).wait_send()

    return _Exchange(shards, [_sds((N_CHIPS,) + s.shape, s.dtype) for s in shards],
                     [pltpu.SemaphoreType.DMA((nw, 2))] * 6 + [pltpu.SemaphoreType.DMA((nw, 3))] * 2,
                     start, finish, rounds=[(3, pass_on), (5, hand_over)])


def _ex_pair(grads):
    nw = len(grads)

    def copies(ins, outs, sems):
        x, y, c, _ = _place()
        out = []
        for w in range(nw):
            hr = grads[w].shape[1] // 2
            out.append(_remote(ins[w].at[:, pl.ds((1 - c) * hr, hr)], outs[w], sems[0].at[w], sems[1].at[w],
                               (x, y, 1 - c)))
        return out

    def start(ins, outs, sems):
        for cp in copies(ins, outs, sems):
            cp.start()

    def finish(ins, outs, sems):
        for cp in copies(ins, outs, sems):
            cp.wait()

    return _Exchange(grads, [_sds((N_CHIPS, g.shape[1] // 2, g.shape[2]), F32) for g in grads],
                     [pltpu.SemaphoreType.DMA((nw,))] * 2, start, finish)


def _ex_chip(pieces):
    nw = len(pieces)

    def copies(ins, outs, sems):
        x, y, c, chips = _place()
        return [_remote(ins[w].at[2 * cx + cy], outs[w].at[k], sems[0].at[w, k], sems[1].at[w, k], (cx, cy, c))
                for w in range(nw) for k, (cx, cy) in enumerate(chips)]

    def start(ins, outs, sems):
        for cp in copies(ins, outs, sems):
            cp.start()

    def finish(ins, outs, sems):
        for cp in copies(ins, outs, sems):
            cp.wait()

    return _Exchange(pieces, [_sds((3,) + p.shape[1:], BF16) for p in pieces],
                     [pltpu.SemaphoreType.DMA((nw, 3))] * 2, start, finish)


def _ex_swap(fulls):
    nw = len(fulls)

    def start(ins, outs, sems):
        x, y, c, _ = _place()
        for w in range(nw):
            hr = fulls[w].shape[0] // 2
            mine = pl.ds(c * hr, hr)
            _remote(ins[w].at[mine], outs[w].at[mine], sems[0].at[w], sems[1].at[w], (x, y, 1 - c)).start()

    def finish(ins, outs, sems):
        x, y, c, _ = _place()
        for w in range(nw):
            hr = fulls[w].shape[0] // 2
            mine, theirs = pl.ds(c * hr, hr), pl.ds((1 - c) * hr, hr)
            _remote(ins[w].at[mine], outs[w].at[mine], sems[0].at[w], sems[1].at[w], (x, y, 1 - c)).wait_send()
            _remote(ins[w].at[theirs], outs[w].at[theirs], sems[0].at[w], sems[1].at[w], (x, y, 1 - c)).wait_recv()

    return _Exchange(fulls, [_sds(f.shape, F32) for f in fulls], [pltpu.SemaphoreType.DMA((nw,))] * 2,
                     start, finish, aliases={w: w for w in range(nw)})


def _ex_allgather(blocks):
    nb = len(blocks)

    def copies(ins, outs, sems):
        send, recv, lsem = sems
        x, y, c, chips = _place()
        me, sibling = (x, y, c), (x, y, 1 - c)

        def rows(b, px, py, pc):
            m_per = blocks[b].shape[0]
            return outs[b].at[pl.ds((4 * px + 2 * py + pc) * m_per, m_per), :]

        def copy(b, k, blk, to, src=None):
            return _remote(rows(b, *blk) if src is None else src, rows(b, *blk), send.at[b, k], recv.at[b, k], to)

        def mine(b):
            return pltpu.make_async_copy(ins[b], rows(b, *me), lsem.at[b])

        def first(b, k):
            return copy(b, k, me, sibling if k == 0 else (*chips[k - 1], c), src=ins[b])

        def passed(b, j):
            return copy(b, 4 + j, (*chips[j], c), sibling)

        def landed(b, j):
            return copy(b, 1 + j, (*chips[j], c), me)

        def handed(b, k):
            return copy(b, 0, sibling, me) if k == 0 else copy(b, 3 + k, (*chips[k - 1], 1 - c), me)

        return mine, first, passed, landed, handed

    def start(ins, outs, sems):
        mine, first, _, _, _ = copies(ins, outs, sems)
        for b in range(nb):
            mine(b).start()
            for k in range(4):
                first(b, k).start()

    def finish(ins, outs, sems):
        mine, first, passed, landed, handed = copies(ins, outs, sems)
        sent = []
        for b in range(nb):
            for j in range(3):
                landed(b, j).wait_recv()
                cp = passed(b, j)
                cp.start()
                sent.append(cp)
        for b in range(nb):
            for k in range(4):
                handed(b, k).wait_recv()
            for k in range(4):
                first(b, k).wait_send()
        for cp in sent:
            cp.wait_send()
        for b in range(nb):
            mine(b).wait()

    return _Exchange(blocks, [_sds((N_DEV * b.shape[0], b.shape[1]), F32) for b in blocks],
                     [pltpu.SemaphoreType.DMA((nb, 7)), pltpu.SemaphoreType.DMA((nb, 7)), pltpu.SemaphoreType.DMA((nb,))],
                     start, finish)


def _cast_shards(w_up, w_down, w_bp, w_ba, w_out, exchanges=()):
    r, c = w_up.shape
    tr = HALF // 2
    steps = r // tr

    def body(up_ref, down_ref, bp_ref, ba_ref, out_ref, ua_ref, ub_ref, da_ref, db_ref, bpo_ref, bao_ref, outo_ref):
        i = pl.program_id(0)

        @pl.when(i == 0)
        def _():
            for src, dst in ((bp_ref, bpo_ref), (ba_ref, bao_ref), (out_ref, outo_ref)):
                dst[...] = src[...].astype(BF16)

        @pl.when(i < steps // 2)
        def _():
            ua_ref[...] = up_ref[...].astype(BF16)
            da_ref[...] = down_ref[...].astype(BF16)

        @pl.when(i >= steps // 2)
        def _():
            ub_ref[...] = up_ref[...].astype(BF16)
            db_ref[...] = down_ref[...].astype(BF16)

    rows = _rows(tr, c)
    first = pl.BlockSpec((tr, c), lambda i: (jnp.minimum(i, steps // 2 - 1), 0))
    second = pl.BlockSpec((tr, c), lambda i: (jnp.maximum(i - steps // 2, 0), 0))
    half = _sds((HALF, c), BF16)
    return _call(
        body, name="cast_shards", grid=(steps,),
        in_specs=[rows, rows, _const(w_bp.shape), _const(w_ba.shape), _const(w_out.shape)],
        out_specs=[first, second, first, second, _const(w_bp.shape), _const(w_ba.shape), _const(w_out.shape)],
        out_shape=[half, half, half, half, _sds(w_bp.shape, BF16), _sds(w_ba.shape, BF16), _sds(w_out.shape, BF16)],
        args=(w_up, w_down, w_bp, w_ba, w_out), sem=("arbitrary",), exchanges=exchanges)


def _inproj(x2, g1, w_in_t, b_in, tabs, seq, exchanges=()):
    T = x2.shape[0]
    tm = min(TM, seq)
    nseq = seq // tm

    def body(x_ref, g_ref, w_ref, b_ref, c_ref, a_ref, bt_ref, h_ref, u_ref, q_ref, k_ref, v_ref, gate_ref):
        x = x_ref[...]
        h = (x * _rms(x) * g_ref[...]).astype(BF16)
        h_ref[...] = h

        def proj(lo, hi):
            return _dot_nt(h, w_ref[lo:hi, :]) + b_ref[:, lo:hi]

        c, a, bt = c_ref[...], a_ref[...], bt_ref[...]
        u_ref[...] = proj(0, C_Q)
        q = proj(C_Q, C_K)
        for p in range(4):
            sl = slice(LANES * p, LANES * (p + 1))
            q_ref[:, sl] = (_rot_fwd(q[:, sl], c, a, bt) * SCALE).astype(BF16)
        kv = proj(C_K, C_G)
        k_ref[...] = _rot_fwd(kv[:, :KV_WIDTH], c, a, bt).astype(BF16)
        v_ref[...] = kv[:, KV_WIDTH:].astype(BF16)
        for j in range(2):
            lo = C_G + D_MODEL * j
            gate_ref[:, D_MODEL * j:D_MODEL * (j + 1)] = jax.nn.sigmoid(proj(lo, lo + D_MODEL)).astype(BF16)

    tab = pl.BlockSpec((tm, LANES), lambda i: (i % nseq, 0))
    return _call(
        body, name="inproj", grid=(T // tm,),
        in_specs=[_rows(tm, D_MODEL), _const((1, D_MODEL)), _const((IN_WIDTH, D_MODEL)), _const((1, IN_WIDTH)),
                  tab, tab, tab],
        out_specs=[_rows(tm, D_MODEL), _rows(tm, POOL_WIDTH), _rows(tm, ATTN_WIDTH), _rows(tm, KV_WIDTH),
                   _rows(tm, KV_WIDTH), _rows(tm, GATE_WIDTH)],
        out_shape=[_sds((T, D_MODEL), BF16), _sds((T, POOL_WIDTH), F32), _sds((T, ATTN_WIDTH), BF16),
                   _sds((T, KV_WIDTH), BF16), _sds((T, KV_WIDTH), BF16), _sds((T, GATE_WIDTH), BF16)],
        args=(x2, g1, w_in_t, b_in, *tabs), sem=("parallel",), exchanges=exchanges)


def _inv_count(pos, w):
    return 1.0 / jnp.minimum(pos + 1, w).astype(F32)


def _pool_tile(i, tp, nseq, u_ref, prev_ref, w_ref, s_ref, diff_ref, y_ref):
    first = (i % nseq) == 0
    prev = jnp.where(first, 0.0, prev_ref[...])
    ext = jnp.concatenate([prev, u_ref[...]], axis=0)
    pos = (i % nseq) * tp + lax.broadcasted_iota(jnp.int32, (tp, 1), 0)
    for gi, w in enumerate(POOL_WINDOWS):
        sl = slice(POOL_GC * gi, POOL_GC * (gi + 1))
        xg = ext[:, sl]
        s = xg
        sh = 1
        while sh < w:
            s = s + pltpu.roll(s, sh, 0)
            sh *= 2
        pooled = s[HALO:] * _inv_count(pos, w)
        diff = (pooled - xg[HALO:]).astype(BF16)
        diff_ref[:, sl] = diff
        mixed = _dot(diff, w_ref[gi].astype(BF16))
        y_ref[:, sl] = (mixed * s_ref[:, sl]).astype(BF16)


def _pool_specs(tp):
    per = tp // HALO
    return [_rows(tp, POOL_WIDTH), pl.BlockSpec((HALO, POOL_WIDTH), lambda i: (jnp.maximum(i * per - 1, 0), 0)),
            _const((4, POOL_GC, POOL_GC)), _const((1, POOL_WIDTH))]


GROUP = 4
GROWS = GROUP * BLOCK


def _attn_masks(n):
    qi = lax.broadcasted_iota(jnp.int32, (GROWS, 2 * BLOCK), 0) % BLOCK
    kj = lax.broadcasted_iota(jnp.int32, (GROWS, 2 * BLOCK), 1)
    rel = qi + BLOCK - kj
    valid = (rel >= 0) & (rel < BLOCK) & (kj >= jnp.where(n > 0, 0, BLOCK))
    lo = lax.broadcasted_iota(jnp.int32, (BLOCK, LANES), 1) < HEAD_DIM
    return valid, lo


def _by_example(bl, *arrays):
    return [a.reshape(bl, a.shape[0] // bl, a.shape[1]) for a in arrays]


def _stack_heads(ref, h, lo):
    keep = lo if h == 0 else jnp.logical_not(lo)
    pieces = []
    for p in (2 * h, 2 * h + 1):
        xp = ref[:, LANES * p:LANES * (p + 1)].astype(F32)
        for e in range(2):
            t = xp if e == h else pltpu.roll(xp, HEAD_DIM, 1)
            pieces.append(jnp.where(keep, t, 0.0).astype(BF16))
    return jnp.concatenate(pieces, axis=0)


def _unstack_heads(stacked, h, lo):
    pairs = []
    for j in range(2):
        parts = []
        for e in range(2):
            t = stacked[BLOCK * (2 * j + e):BLOCK * (2 * j + e + 1)]
            parts.append(t if e == h else pltpu.roll(t, HEAD_DIM, 1))
        pairs.append(jnp.where(lo, parts[0], parts[1]))
    return pairs


def _sink_rows(sink_ref, h):
    head = lax.broadcasted_iota(jnp.int32, (GROWS, 1), 0) // BLOCK
    col = jnp.zeros((GROWS, 1), F32) + sink_ref[GROUP * h]
    for g in range(1, GROUP):
        col = jnp.where(head == g, sink_ref[GROUP * h + g], col)
    return col


def _group_probs(qs, kk, valid, sink):
    s = jnp.where(valid, _dot_nt(qs, kk), NEG_INF)
    m = jnp.maximum(jnp.max(s, axis=1, keepdims=True), sink)
    ex = jnp.exp(s - m)
    es = jnp.exp(sink - m)
    inv = 1.0 / (jnp.sum(ex, axis=1, keepdims=True) + es)
    return ex * inv, es * inv


def _mixers_fwd(q, k, v, sinks, u, w_pool, pool_scale, seq, exchanges=()):
    T = q.shape[0]
    nb = seq // BLOCK
    bl = T // seq
    tp = T // nb
    nseq = seq // tp

    def body(sink_ref, q_ref, kp_ref, kc_ref, vp_ref, vc_ref, u_ref, prev_ref, w_ref, s_ref, o_ref, diff_ref, y_ref):
        n = pl.program_id(0)
        valid, lo = _attn_masks(n)
        for b in range(bl):
            kk = jnp.concatenate([kp_ref[b], kc_ref[b]], axis=0)
            vv = jnp.concatenate([vp_ref[b], vc_ref[b]], axis=0)
            for h in range(2):
                qs = _stack_heads(q_ref.at[b], h, lo)
                pr, _ = _group_probs(qs, kk, valid, _sink_rows(sink_ref, h))
                o = _dot(pr.astype(BF16), vv)
                for j, pair in enumerate(_unstack_heads(o, h, lo)):
                    p = 2 * h + j
                    o_ref[b, :, LANES * p:LANES * (p + 1)] = pair.astype(BF16)
        _pool_tile(n, tp, nseq, u_ref, prev_ref, w_ref, s_ref, diff_ref, y_ref)

    cur = lambda n: (0, n, 0)
    prv = lambda n: (0, jnp.maximum(n - 1, 0), 0)
    kv = lambda m: pl.BlockSpec((bl, BLOCK, KV_WIDTH), m)
    res = _call(
        body, name="mixers_fwd", grid=(nb,),
        in_specs=[pl.BlockSpec(memory_space=pltpu.SMEM), pl.BlockSpec((bl, BLOCK, ATTN_WIDTH), cur),
                  kv(prv), kv(cur), kv(prv), kv(cur)] + _pool_specs(tp),
        out_specs=[pl.BlockSpec((bl, BLOCK, ATTN_WIDTH), cur), _rows(tp, POOL_WIDTH), _rows(tp, POOL_WIDTH)],
        out_shape=[_sds((bl, seq, ATTN_WIDTH), BF16), _sds((T, POOL_WIDTH), BF16), _sds((T, POOL_WIDTH), BF16)],
        args=(sinks, *_by_example(bl, q, k, k, v, v), u, u, w_pool, pool_scale), sem=("parallel",),
        exchanges=exchanges)
    outs, rest = res if exchanges else (res, None)
    return [outs[0].reshape(T, ATTN_WIDTH), outs[1], outs[2]], rest


def _branch(y, w_ref):
    return jnp.concatenate([_dot(y, w_ref[j]) for j in range(N_CHIPS)], axis=1)


def _merge_out(y_pool, y_attn, gate, x2, w_bp, w_ba, w_out, g2, g3, exchanges=()):
    T = x2.shape[0]
    tm = min(TM, T)

    def body(yp_ref, ya_ref, gate_ref, x_ref, wbp_ref, wba_ref, wo_ref, g2_ref, g3_ref,
             mg_ref, mix_ref, x1_ref, h2_ref):
        bp, ba = _branch(yp_ref[...], wbp_ref), _branch(ya_ref[...], wba_ref)
        merged = (gate_ref[:, :D_MODEL].astype(F32) * bp + gate_ref[:, D_MODEL:].astype(F32) * ba).astype(BF16)
        mg_ref[...] = merged
        mix = _dot(merged, wo_ref[...])
        mix_ref[...] = mix
        x1 = x_ref[...] + mix * _rms(mix) * g2_ref[...]
        x1_ref[...] = x1
        h2_ref[...] = (x1 * _rms(x1) * g3_ref[...]).astype(BF16)

    return _call(
        body, name="merge_out", grid=(T // tm,),
        in_specs=[_rows(tm, POOL_WIDTH), _rows(tm, ATTN_WIDTH), _rows(tm, GATE_WIDTH), _rows(tm, D_MODEL),
                  _const(w_bp.shape), _const(w_ba.shape), _const((D_MODEL, D_MODEL)),
                  _const((1, D_MODEL)), _const((1, D_MODEL))],
        out_specs=[_rows(tm, D_MODEL)] * 4,
        out_shape=[_sds((T, D_MODEL), BF16), _sds((T, D_MODEL), F32), _sds((T, D_MODEL), F32),
                   _sds((T, D_MODEL), BF16)],
        args=(y_pool, y_attn, gate, x2, w_bp, w_ba, w_out, g2, g3), sem=("parallel",), exchanges=exchanges)


HALF = D_MODEL // 2
TM_MLP = 256


def _mlp_core(h2, x1, mix, tgt, w_up, w_down, g4, g3, g2):
    T = h2.shape[0]
    tm = min(TM_MLP, T)

    def body(h_ref, x1_ref, mix_ref, t_ref, g_ref, g3_ref, g2_ref, ua_hbm, ub_hbm, da_hbm, db_hbm,
             act_ref, dff_ref, dup_ref, dx1_ref, dmix_ref, loss_ref, dg_ref, dg3_ref, dg2_ref,
             wu, wd, relu_scr, sems):
        def weight_copy(i):
            src, dst = ((ua_hbm, wu.at[:, :HALF]), (ub_hbm, wu.at[:, HALF:]),
                        (da_hbm, wd.at[:, :HALF]), (db_hbm, wd.at[:, HALF:]))[i]
            return pltpu.make_async_copy(src, dst, sems.at[i])

        @pl.when(pl.program_id(0) == 0)
        def _():
            for i in range(4):
                weight_copy(i).start()
            loss_ref[...] = jnp.zeros_like(loss_ref)
            for ref in (dg_ref, dg3_ref, dg2_ref):
                ref[...] = jnp.zeros_like(ref)
            weight_copy(0).wait()
            weight_copy(1).wait()

        h = h_ref[...]
        ff = None
        for j in range(N_CHIPS):
            lo = D_MODEL * j
            relu = jnp.maximum(_dot(h, wu[j]), 0.0)
            if j == 0:
                @pl.when(pl.program_id(0) == 0)
                def _():
                    weight_copy(2).wait()
                    weight_copy(3).wait()
            relu_scr[:, lo:lo + D_MODEL] = relu
            act = jnp.square(relu).astype(BF16)
            act_ref[:, lo:lo + D_MODEL] = act
            t = _dot(act, wd[j])
            ff = t if ff is None else ff + t
        g = g_ref[...]
        x1 = x1_ref[...]
        err = x1 + ff * _rms(ff) * g - t_ref[...]
        loss_ref[...] += jnp.sum(err * err) * (0.5 / D_MODEL)
        dy = err * (1.0 / D_MODEL)
        dff, dg = _norm_bwd(ff, g, dy)
        dg_ref[...] += dg
        dff = dff.astype(BF16)
        dff_ref[...] = dff
        dh2 = None
        for j in range(N_CHIPS):
            lo = D_MODEL * j
            dup = (_dot_nt(dff, wd[j]) * (2.0 * relu_scr[:, lo:lo + D_MODEL])).astype(BF16)
            dup_ref[:, lo:lo + D_MODEL] = dup
            t = _dot_nt(dup, wu[j])
            dh2 = t if dh2 is None else dh2 + t
        dx, dg3 = _norm_bwd(x1, g3_ref[...], dh2)
        dx1 = dy + dx
        dx1_ref[...] = dx1
        dg3_ref[...] += dg3
        dmix, dg2 = _norm_bwd(mix_ref[...], g2_ref[...], dx1)
        dmix_ref[...] = dmix.astype(BF16)
        dg2_ref[...] += dg2

    slabs = pltpu.VMEM((N_CHIPS, D_MODEL, D_MODEL), BF16)
    gain = _const((1, D_MODEL))
    return pl.pallas_call(
        body, name="mlp_core", grid=(T // tm,),
        in_specs=[_rows(tm, D_MODEL)] * 4 + [gain] * 3 + [ANY] * 4,
        out_specs=[_rows(tm, D_FF), _rows(tm, D_MODEL), _rows(tm, D_FF), _rows(tm, D_MODEL), _rows(tm, D_MODEL),
                   _const((8, LANES)), gain, gain, gain],
        out_shape=[_sds((T, D_FF), BF16), _sds((T, D_MODEL), BF16), _sds((T, D_FF), BF16), _sds((T, D_MODEL), F32),
                   _sds((T, D_MODEL), BF16), _sds((8, LANES), F32)] + [_sds((1, D_MODEL), F32)] * 3,
        scratch_shapes=[slabs] * 2 + [pltpu.VMEM((tm, D_FF), F32), pltpu.SemaphoreType.DMA((4,))],
        compiler_params=_cp("arbitrary"),
    )(h2, x1, mix, tgt, g4, g3, g2, *w_up, *w_down)


def _dw(tag, a, g, ta, tn, shard_cols=False, exchanges=()):
    T, ka = a.shape
    n = g.shape[1]
    tk = min(2 * TM, T)
    nk = T // tk

    def body(a_ref, g_ref, o_ref):
        @pl.when(pl.program_id(2) == 0)
        def _():
            o_ref[...] = jnp.zeros_like(o_ref)

        o_ref[...] += _dot_tn(a_ref[...], g_ref[...])

    if shard_cols:
        per = (n // N_CHIPS) // tn
        out_spec = pl.BlockSpec((None, ta, tn), lambda i, j, k: (j // per, i, j % per))
        out_shape = _sds((N_CHIPS, ka, n // N_CHIPS), F32)
    else:
        out_spec = pl.BlockSpec((ta, tn), lambda i, j, k: (i, j))
        out_shape = _sds((ka, n), F32)
    return _call(
        body, name="dw_" + tag, grid=(ka // ta, n // tn, nk),
        in_specs=[pl.BlockSpec((tk, ta), lambda i, j, k: (k, i)), pl.BlockSpec((tk, tn), lambda i, j, k: (k, j))],
        out_specs=[out_spec], out_shape=[out_shape],
        args=(a, g), sem=("parallel", "parallel", "arbitrary"), exchanges=exchanges)


def _dw_mix(merged, dmix, y_pool, dbp, y_attn, dba, exchanges=()):
    T = merged.shape[0]
    tk = min(2 * TM, T)
    c = D_MODEL // N_CHIPS

    def body(mg_ref, dmix_ref, yp_ref, dbp_ref, ya_ref, dba_ref, out_ref, bp_ref, ba_ref):
        @pl.when(pl.program_id(0) == 0)
        def _():
            for ref in (out_ref, bp_ref, ba_ref):
                ref[...] = jnp.zeros_like(ref)

        out_ref[...] += _dot_tn(mg_ref[...], dmix_ref[...])
        for y_ref, d_ref, o_ref in ((yp_ref, dbp_ref, bp_ref), (ya_ref, dba_ref, ba_ref)):
            res = _dot_tn(y_ref[...], d_ref[...])
            for j in range(N_CHIPS):
                o_ref[j] += res[:, c * j:c * (j + 1)]

    slabs = (N_CHIPS, POOL_WIDTH, c)
    return _call(
        body, name="dw_mix", grid=(T // tk,),
        in_specs=[_rows(tk, D_MODEL), _rows(tk, D_MODEL), _rows(tk, POOL_WIDTH), _rows(tk, D_MODEL),
                  _rows(tk, ATTN_WIDTH), _rows(tk, D_MODEL)],
        out_specs=[_const((D_MODEL, D_MODEL)), _const(slabs), _const(slabs)],
        out_shape=[_sds((D_MODEL, D_MODEL), F32), _sds(slabs, F32), _sds(slabs, F32)],
        args=(merged, dmix, y_pool, dbp, y_attn, dba), sem=("arbitrary",), exchanges=exchanges)


def _merge_bwd(dmix, gate, y_pool, y_attn, w_out, w_bp, w_ba, exchanges=()):
    T = dmix.shape[0]
    tm = min(TM, T)

    def body(dmix_ref, gate_ref, yp_ref, ya_ref, wo_ref, wbp_ref, wba_ref,
             dbp_ref, dba_ref, dgate_ref, dyp_ref, dya_ref):
        dm = _dot_nt(dmix_ref[...], wo_ref[...])
        for j, (y_ref, db_ref, w_ref, dy_ref) in enumerate(
                ((yp_ref, dbp_ref, wbp_ref, dyp_ref), (ya_ref, dba_ref, wba_ref, dya_ref))):
            sl = slice(D_MODEL * j, D_MODEL * (j + 1))
            gt = gate_ref[:, sl].astype(F32)
            db = (dm * gt).astype(BF16)
            db_ref[...] = db
            dgate_ref[:, sl] = (dm * _branch(y_ref[...], w_ref) * gt * (1.0 - gt)).astype(BF16)
            cw = D_MODEL // N_CHIPS
            dy = _dot_nt(db[:, :cw], w_ref[0])
            for c in range(1, N_CHIPS):
                dy = dy + _dot_nt(db[:, cw * c:cw * (c + 1)], w_ref[c])
            dy_ref[...] = dy.astype(dy_ref.dtype)

    return _call(
        body, name="merge_bwd", grid=(T // tm,),
        in_specs=[_rows(tm, D_MODEL), _rows(tm, GATE_WIDTH), _rows(tm, POOL_WIDTH), _rows(tm, ATTN_WIDTH),
                  _const((D_MODEL, D_MODEL)), _const(w_bp.shape), _const(w_ba.shape)],
        out_specs=[_rows(tm, D_MODEL), _rows(tm, D_MODEL), _rows(tm, GATE_WIDTH), _rows(tm, POOL_WIDTH),
                   _rows(tm, ATTN_WIDTH)],
        out_shape=[_sds((T, D_MODEL), BF16), _sds((T, D_MODEL), BF16), _sds((T, GATE_WIDTH), BF16),
                   _sds((T, POOL_WIDTH), F32), _sds((T, ATTN_WIDTH), BF16)],
        args=(dmix, gate, y_pool, y_attn, w_out, w_bp, w_ba), sem=("parallel",), exchanges=exchanges)


def _mixers_bwd(q, k, v, do, sinks, tabs, dyp, diff, w_pool, pool_scale, seq, exchanges=()):
    T = q.shape[0]
    nb = seq // BLOCK
    bl = T // seq
    steps = nb + 1
    tp = T // nb
    nseq = seq // tp
    per = tp // HALO
    last_halo = T // HALO - 1

    def body(sink_ref, q_ref, do_ref, kp_ref, kc_ref, vp_ref, vc_ref, c_ref, a_ref, bt_ref, cp_ref, ap_ref, btp_ref,
             dy_ref, nxt_ref, diff_ref, w_ref, s_ref,
             dq_ref, dk_ref, dv_ref, dsink_ref, du_ref, dw_ref, ds_ref, ck_ref, cv_ref):
        n = pl.program_id(0)

        @pl.when(n == 0)
        def _():
            for ref in (dsink_ref, ck_ref, cv_ref, dw_ref, ds_ref):
                ref[...] = jnp.zeros_like(ref)

        @pl.when(n < nb)
        def _():
            _pool_bwd_tile(n, tp, nseq, dy_ref, nxt_ref, diff_ref, w_ref, s_ref, du_ref, dw_ref, ds_ref)
            valid, lo = _attn_masks(n)
            for b in range(bl):
                kk = jnp.concatenate([kp_ref[b], kc_ref[b]], axis=0)
                vv = jnp.concatenate([vp_ref[b], vc_ref[b]], axis=0)
                dk_acc = jnp.zeros((2 * BLOCK, KV_WIDTH), F32)
                dv_acc = jnp.zeros((2 * BLOCK, KV_WIDTH), F32)
                for h in range(2):
                    qs = _stack_heads(q_ref.at[b], h, lo)
                    dos = _stack_heads(do_ref.at[b], h, lo)
                    pr, ps = _group_probs(qs, kk, valid, _sink_rows(sink_ref, h))
                    dp = _dot_nt(dos, vv)
                    delta = jnp.sum(pr * dp, axis=1, keepdims=True)
                    ds = (pr * (dp - delta)).astype(BF16)
                    dsk = ps * delta
                    for g in range(GROUP):
                        idx = GROUP * h + g
                        dsink_ref[idx:idx + 1, :] += (jnp.zeros((1, LANES), F32)
                                                      - jnp.sum(dsk[BLOCK * g:BLOCK * (g + 1)]))
                    dk_acc = dk_acc + _dot_tn(ds, qs)
                    dv_acc = dv_acc + _dot_tn(pr.astype(BF16), dos)
                    for j, pair in enumerate(_unstack_heads(_dot(ds, kk) * SCALE, h, lo)):
                        sl = slice(LANES * (2 * h + j), LANES * (2 * h + j + 1))
                        dq_ref[b, :, sl] = _rot_bwd(pair, c_ref[...], a_ref[...], bt_ref[...]).astype(BF16)
                fin_k = ck_ref[b] + dk_acc[:BLOCK]
                dk_ref[b] = _rot_bwd(fin_k, cp_ref[...], ap_ref[...], btp_ref[...]).astype(BF16)
                dv_ref[b] = (cv_ref[b] + dv_acc[:BLOCK]).astype(BF16)
                ck_ref[b] = dk_acc[BLOCK:]
                cv_ref[b] = dv_acc[BLOCK:]

        @pl.when(n == nb)
        def _():
            for b in range(bl):
                dk_ref[b] = _rot_bwd(ck_ref[b], cp_ref[...], ap_ref[...], btp_ref[...]).astype(BF16)
                dv_ref[b] = cv_ref[b].astype(BF16)

    cur = lambda n: (0, jnp.minimum(n, nb - 1), 0)
    prv = lambda n: (0, jnp.clip(n - 1, 0, nb - 1), 0)
    tcur = lambda n: (jnp.minimum(n, nb - 1), 0)
    tprv = lambda n: (jnp.clip(n - 1, 0, nb - 1), 0)
    wide = lambda m: pl.BlockSpec((bl, BLOCK, ATTN_WIDTH), m)
    kv = lambda m: pl.BlockSpec((bl, BLOCK, KV_WIDTH), m)
    tab = lambda m: pl.BlockSpec((BLOCK, LANES), m)
    tile = lambda n: (jnp.minimum(n, nb - 1), 0)
    halo = lambda n: (jnp.minimum((jnp.minimum(n, nb - 1) + 1) * per, last_halo), 0)
    rows = pl.BlockSpec((tp, POOL_WIDTH), tile)
    res = _call(
        body, name="mixers_bwd", grid=(steps,),
        in_specs=[pl.BlockSpec(memory_space=pltpu.SMEM), wide(cur), wide(cur), kv(prv), kv(cur), kv(prv), kv(cur),
                  tab(tcur), tab(tcur), tab(tcur), tab(tprv), tab(tprv), tab(tprv),
                  rows, pl.BlockSpec((HALO, POOL_WIDTH), halo), rows, _const((4, POOL_GC, POOL_GC)),
                  _const((1, POOL_WIDTH))],
        out_specs=[wide(cur), kv(prv), kv(prv), _const((8, LANES)), rows, _const((4, POOL_GC, POOL_GC)),
                   _const((1, POOL_WIDTH))],
        out_shape=[_sds((bl, seq, ATTN_WIDTH), BF16), _sds((bl, seq, KV_WIDTH), BF16),
                   _sds((bl, seq, KV_WIDTH), BF16), _sds((8, LANES), F32), _sds((T, POOL_WIDTH), BF16),
                   _sds((4, POOL_GC, POOL_GC), F32), _sds((1, POOL_WIDTH), F32)],
        scratch=[pltpu.VMEM((bl, BLOCK, KV_WIDTH), F32), pltpu.VMEM((bl, BLOCK, KV_WIDTH), F32)],
        args=(sinks, *_by_example(bl, q, do, k, k, v, v), *tabs, *tabs, dyp, dyp, diff, w_pool, pool_scale),
        sem=("arbitrary",), exchanges=exchanges)
    outs, rest = (res if exchanges else (res, None))
    outs = [outs[0].reshape(T, ATTN_WIDTH), outs[1].reshape(T, KV_WIDTH), outs[2].reshape(T, KV_WIDTH), *outs[3:]]
    return (outs, rest) if exchanges else outs


def _pool_bwd_tile(i, tp, nseq, dy_ref, nxt_ref, diff_ref, w_ref, s_ref, du_ref, dw_ref, ds_ref):
    last = (i % nseq) == nseq - 1
    nxt = jnp.where(last, 0.0, nxt_ref[...])
    ext = jnp.concatenate([dy_ref[...], nxt], axis=0) * s_ref[...]
    pos = (i % nseq) * tp + lax.broadcasted_iota(jnp.int32, (tp + HALO, 1), 0)
    for gi, w in enumerate(POOL_WINDOWS):
        sl = slice(POOL_GC * gi, POOL_GC * (gi + 1))
        wg = w_ref[gi].astype(BF16)
        dmx = ext[:, sl].astype(BF16)
        ddiff = _dot_nt(dmx, wg)
        s = ddiff * _inv_count(pos, w)
        sh = 1
        while sh < w:
            s = s + pltpu.roll(s, tp + HALO - sh, 0)
            sh *= 2
        du_ref[:, sl] = (s[:tp] - ddiff[:tp]).astype(BF16)
        dg = diff_ref[:, sl]
        dw_ref[gi] += _dot_tn(dg, dmx[:tp])
        ds_ref[:, sl] += jnp.sum(dy_ref[:, sl] * _dot(dg, wg), axis=0, keepdims=True)


_PARTS = ((0, C_Q), (C_Q, C_K), (C_K, C_V), (C_V, C_G), (C_G, IN_WIDTH))


def _inproj_bwd(parts, x2, dx1, w_in_t, g1, exchanges=()):
    T = x2.shape[0]
    tm = min(TM, T)

    def body(du_ref, dq_ref, dk_ref, dv_ref, dgt_ref, x_ref, dx1_ref, w_ref, g_ref, gx_ref, dg_ref):
        @pl.when(pl.program_id(0) == 0)
        def _():
            dg_ref[...] = jnp.zeros_like(dg_ref)

        dh = jnp.zeros((tm, D_MODEL), F32)
        for (lo, hi), p_ref in zip(_PARTS, (du_ref, dq_ref, dk_ref, dv_ref, dgt_ref)):
            dh = dh + _dot(p_ref[...], w_ref[lo:hi, :])
        dx, dg = _norm_bwd(x_ref[...], g_ref[...], dh)
        gx_ref[...] = dx1_ref[...] + dx
        dg_ref[...] += dg

    return _call(
        body, name="inproj_bwd", grid=(T // tm,),
        in_specs=[_rows(tm, hi - lo) for lo, hi in _PARTS]
        + [_rows(tm, D_MODEL), _rows(tm, D_MODEL), _const((IN_WIDTH, D_MODEL)), _const((1, D_MODEL))],
        out_specs=[_rows(tm, D_MODEL), _const((1, D_MODEL))],
        out_shape=[_sds((T, D_MODEL), F32), _sds((1, D_MODEL), F32)],
        args=(*parts, x2, dx1, w_in_t, g1), sem=("arbitrary",), exchanges=exchanges)


def _dw_in(h, parts, exchanges=()):
    T = h.shape[0]
    tk = min(TM, T)

    def body(h_ref, du_ref, dq_ref, dk_ref, dv_ref, dgt_ref, o_ref, db_ref):
        @pl.when(pl.program_id(0) == 0)
        def _():
            o_ref[...] = jnp.zeros_like(o_ref)
            db_ref[...] = jnp.zeros_like(db_ref)

        hh = h_ref[...]
        for (lo, hi), p_ref in zip(_PARTS, (du_ref, dq_ref, dk_ref, dv_ref, dgt_ref)):
            part = p_ref[...]
            o_ref[lo:hi, :] += _dot_tn(part, hh)
            db_ref[:, lo:hi] += jnp.sum(part.astype(F32), axis=0, keepdims=True)

    return _call(
        body, name="dw_in", grid=(T // tk,),
        in_specs=[_rows(tk, D_MODEL)] + [_rows(tk, hi - lo) for lo, hi in _PARTS],
        out_specs=[_const((IN_WIDTH, D_MODEL)), _const((1, IN_WIDTH))],
        out_shape=[_sds((IN_WIDTH, D_MODEL), F32), _sds((1, IN_WIDTH), F32)],
        args=(h, *parts), sem=("arbitrary",), exchanges=exchanges)


def _row_tile(rows, cap=256, mult=16):
    best = None
    for t in range(mult, min(rows, cap) + 1, mult):
        if rows % t == 0:
            best = t
    if best is None:
        raise ValueError("no row tile for %d rows" % rows)
    return best


def _pair_sum(ids, full, got):
    _, r, c = full.shape
    hr = r // 2
    tr = _row_tile(hr)
    nblk = hr // tr

    def body(ids_ref, a_ref, b_ref, own_ref, sb_ref):
        s = a_ref[...] + b_ref[...]
        sb_ref[...] = s.astype(BF16)

        @pl.when(pl.program_id(1) == ids_ref[0])
        def _():
            own_ref[...] = s

    slab = pl.BlockSpec((None, tr, c), lambda i, j, ids_ref: (j, i, 0))
    return pl.pallas_call(
        body, name="pair_sum_%dx%d" % (r, c),
        grid_spec=pltpu.PrefetchScalarGridSpec(
            num_scalar_prefetch=1, grid=(nblk, N_CHIPS),
            in_specs=[pl.BlockSpec((None, tr, c), lambda i, j, ids_ref: (j, ids_ref[1] * nblk + i, 0)), slab],
            out_specs=[pl.BlockSpec((tr, c), lambda i, j, ids_ref: (i, 0)), slab]),
        out_shape=[_sds((hr, c), F32), _sds((N_CHIPS, hr, c), BF16)],
        compiler_params=_cp("parallel", "arbitrary"),
    )(ids, full, got)


def _pair_sum_small(ids, fulls, gots):
    n = len(fulls)
    dims = [(f.shape[1] // 2, f.shape[2]) for f in fulls]

    def body(ids_ref, *refs):
        ins, outs = refs[:2 * n], refs[2 * n:]
        for k in range(n):
            s = ins[2 * k][...] + ins[2 * k + 1][...]
            outs[2 * k + 1][...] = s.astype(BF16)

            @pl.when(pl.program_id(0) == ids_ref[0])
            def _(k=k, s=s):
                outs[2 * k][...] = s

    in_specs, out_specs, out_shape = [], [], []
    for hr, c in dims:
        slab = pl.BlockSpec((None, hr, c), lambda j, ids_ref: (j, 0, 0))
        in_specs += [pl.BlockSpec((None, hr, c), lambda j, ids_ref: (j, ids_ref[1], 0)), slab]
        out_specs += [pl.BlockSpec((hr, c), lambda j, ids_ref: (0, 0)), slab]
        out_shape += [_sds((hr, c), F32), _sds((N_CHIPS, hr, c), BF16)]
    res = pl.pallas_call(
        body, name="pair_sum_small",
        grid_spec=pltpu.PrefetchScalarGridSpec(num_scalar_prefetch=1, grid=(N_CHIPS,), in_specs=in_specs,
                                               out_specs=out_specs),
        out_shape=out_shape, compiler_params=_cp("arbitrary"),
    )(ids, *[a for pair in zip(fulls, gots) for a in pair])
    return [(res[2 * k], res[2 * k + 1]) for k in range(n)]


def _chip_sum_small(ids, owns, gots):
    n = len(owns)

    def body(ids_ref, *refs):
        ins, outs = refs[:2 * n], refs[2 * n:]
        for k in range(n):
            a, b = ins[2 * k], ins[2 * k + 1]
            outs[k][...] = ((a[...] + b[0].astype(F32)) + b[1].astype(F32)) + b[2].astype(F32)

    in_specs, out_specs, out_shape = [], [], []
    for own in owns:
        hr, c = own.shape
        in_specs += [pl.BlockSpec((hr, c), lambda i, ids_ref: (0, 0)),
                     pl.BlockSpec((3, hr, c), lambda i, ids_ref: (0, 0, 0))]
        out_specs.append(pl.BlockSpec((hr, c), lambda i, ids_ref: (ids_ref[1], 0)))
        out_shape.append(_sds((2 * hr, c), F32))
    return pl.pallas_call(
        body, name="chip_sum_small",
        grid_spec=pltpu.PrefetchScalarGridSpec(num_scalar_prefetch=1, grid=(1,), in_specs=in_specs,
                                               out_specs=out_specs),
        out_shape=out_shape, compiler_params=_cp("arbitrary"),
    )(ids, *[a for pair in zip(owns, gots) for a in pair])


def _chip_sum(ids, own, got):
    hr, c = own.shape
    tr = _row_tile(hr)
    nblk = hr // tr

    def body(ids_ref, a_ref, b_ref, o_ref):
        o_ref[...] = ((a_ref[...] + b_ref[0].astype(F32)) + b_ref[1].astype(F32)) + b_ref[2].astype(F32)

    return pl.pallas_call(
        body, name="chip_sum_%dx%d" % (hr, c),
        grid_spec=pltpu.PrefetchScalarGridSpec(
            num_scalar_prefetch=1, grid=(nblk,),
            in_specs=[pl.BlockSpec((tr, c), lambda i, ids_ref: (i, 0)),
                      pl.BlockSpec((3, tr, c), lambda i, ids_ref: (0, i, 0))],
            out_specs=pl.BlockSpec((tr, c), lambda i, ids_ref: (ids_ref[1] * nblk + i, 0))),
        out_shape=_sds((2 * hr, c), F32),
        compiler_params=_cp("parallel"),
    )(ids, own, got)


def _adamw_math(w, g, m, v):
    nm = ADAM_B1 * m + (1.0 - ADAM_B1) * g
    nv = ADAM_B2 * v + (1.0 - ADAM_B2) * (g * g)
    m_hat = nm / (1.0 - ADAM_B1 ** ADAM_STEP)
    v_hat = nv / (1.0 - ADAM_B2 ** ADAM_STEP)
    return -ADAM_LR * (m_hat / (jnp.sqrt(v_hat) + ADAM_EPS) + ADAM_WD * w), nm, nv


def _adamw(w, g, m, v):
    r, c = w.shape
    tr = _row_tile(r, cap=512, mult=8)

    def body(w_ref, g_ref, m_ref, v_ref, d_ref, nm_ref, nv_ref):
        d_ref[...], nm_ref[...], nv_ref[...] = _adamw_math(w_ref[...], g_ref[...], m_ref[...], v_ref[...])

    spec = _rows(tr, c)
    return pl.pallas_call(
        body, name="adamw_%dx%d" % (r, c), grid=(r // tr,),
        in_specs=[spec] * 4, out_specs=[spec] * 3, out_shape=[_sds((r, c), F32)] * 3,
        compiler_params=_cp("parallel"),
    )(w, g, m, v)


SC_TILES = 32
SC_LANES = 16
SC_ROWS = 8


def _adamw_sparse(w, g, m, v):
    r, c = w.shape
    rows = r // SC_TILES
    step = min(rows, SC_ROWS)

    def body(w_hbm, g_hbm, m_hbm, v_hbm, d_hbm, nm_hbm, nv_hbm, wb, gb, mb, vb):
        tile = lax.axis_index("sc_subcore") * 2 + lax.axis_index("sc_core")

        @pl.loop(0, rows, step=step)
        def _(r0):
            mine = pl.ds(tile * rows + r0, step)
            for src, dst in ((w_hbm, wb), (g_hbm, gb), (m_hbm, mb), (v_hbm, vb)):
                pltpu.sync_copy(src.at[mine], dst)

            @pl.loop(0, step)
            def _(row):
                @pl.loop(0, c, step=SC_LANES)
                def _(i):
                    at = (row, pl.ds(i, SC_LANES))
                    wb[at], mb[at], vb[at] = _adamw_math(wb[at], gb[at], mb[at], vb[at])

            for src, dst in ((wb, d_hbm), (mb, nm_hbm), (vb, nv_hbm)):
                pltpu.sync_copy(src, dst.at[mine])

    return pl.kernel(
        body, name="adamw_sparse_%dx%d" % (r, c), out_type=[_sds((r, c), F32)] * 3,
        mesh=plsc.VectorSubcoreMesh(core_axis_name="sc_core", subcore_axis_name="sc_subcore"),
        scratch_types=[pltpu.VMEM((step, c), F32)] * 4,
    )(w, g, m, v)


_SMALL_NAMES = ("w_pool", "b_in", "g_mix_pre", "g_mix_post", "g_mlp_pre", "g_mlp_post", "pool_scale", "attn_sinks")
B_ROWS = -(-IN_WIDTH // D_MODEL)


def _row_block(rows):
    rows = [jnp.pad(r.astype(F32), ((0, 0), (0, D_MODEL - r.shape[1]))) for r in rows]
    return jnp.pad(jnp.concatenate(rows, axis=0), ((0, 8 - len(rows)), (0, 0)))


def _early_block(dg2, dg3, dg4, dps, dsink, loss):
    tail = jnp.concatenate([jnp.pad(dsink.reshape(1, -1), ((0, 0), (0, LANES - dsink.size))),
                            jnp.pad(loss.reshape(1, 1), ((0, 0), (0, LANES - 1)))], axis=1)
    return _row_block([dg2, dg3, dg4, dps, tail])


def _late_block(db_in, dg1):
    b = jnp.pad(db_in, ((0, 0), (0, B_ROWS * D_MODEL - IN_WIDTH))).reshape(B_ROWS, D_MODEL)
    return _row_block([b[r:r + 1] for r in range(B_ROWS)] + [dg1])


def _small_update(gearly, gmat, glate, w, m, v):
    names = _SMALL_NAMES
    n = len(names)

    def total(ref, rows):
        acc = ref[0:rows, :]
        for d in range(1, N_DEV):
            acc = acc + ref[d * rows:(d + 1) * rows, :]
        return acc

    def body(*refs):
        early_ref, gmat_ref, late_ref = refs[:3]
        w_refs, m_refs, v_refs = refs[3:3 + n], refs[3 + n:3 + 2 * n], refs[3 + 2 * n:3 + 3 * n]
        outs = refs[3 + 3 * n:]
        loss_ref, g_refs, d_refs = outs[0], outs[1:1 + n], outs[1 + n:1 + 2 * n]
        nm_refs, nv_refs = outs[1 + 2 * n:1 + 3 * n], outs[1 + 3 * n:1 + 4 * n]
        early, late = total(early_ref, 8), total(late_ref, 8)
        loss_ref[...] = jnp.sum(early[4:5, LANES:2 * LANES], axis=1, keepdims=True)
        bias = jnp.concatenate([late[r:r + 1, :] for r in range(B_ROWS - 1)]
                               + [late[B_ROWS - 1:B_ROWS, :IN_WIDTH - (B_ROWS - 1) * D_MODEL]], axis=1)
        grad = dict(b_in=bias, g_mix_pre=late[B_ROWS:B_ROWS + 1, :], g_mix_post=early[0:1, :],
                    g_mlp_pre=early[1:2, :], g_mlp_post=early[2:3, :], pool_scale=early[3:4, :POOL_WIDTH],
                    attn_sinks=early[4:5, :N_Q_HEADS])
        for i, name in enumerate(names):
            g = total(gmat_ref, 4 * POOL_GC) if name == "w_pool" else grad[name]
            g_refs[i][...] = g
            d_refs[i][...], nm_refs[i][...], nv_refs[i][...] = _adamw_math(
                w_refs[i][...], g, m_refs[i][...], v_refs[i][...])

    shapes = [_sds(w[k].shape, F32) for k in names]
    res = pl.pallas_call(
        body, name="small_update", out_shape=[_sds((1, 1), F32)] + shapes * 4,
        compiler_params=pltpu.CompilerParams(vmem_limit_bytes=VMEM_MB * 1024 * 1024),
    )(gearly, gmat, glate, *[w[k] for k in names], *[m[k] for k in names], *[v[k] for k in names])
    loss = res[0]
    per = {k: tuple(res[1 + j * n + i] for j in range(4)) for i, k in enumerate(names)}
    return loss, per


_BIG = ("w_in", "w_branch_pool", "w_branch_attn", "w_out", "w_up", "w_down")
_ORDER = ("g_mix_pre", "w_in", "b_in", "w_pool", "pool_scale", "attn_sinks", "w_branch_pool", "w_branch_attn",
          "w_out", "g_mix_post", "g_mlp_pre", "w_up", "w_down", "g_mlp_post")


def _stack_rows(slab):
    return slab.reshape(-1, slab.shape[2])


def _step(x2, tgt, seq, shards, small, ids):
    tabs = _rope_tables(seq)
    g1, g2, g3, g4 = (small[n] for n in ("g_mix_pre", "g_mix_post", "g_mlp_pre", "g_mlp_post"))
    sinks = small["attn_sinks"].reshape(N_Q_HEADS)
    w_pool = small["w_pool"].reshape(4, POOL_GC, POOL_GC)
    pool_scale = small["pool_scale"]

    def whole(shard, slabs):
        return lax.dynamic_update_slice(slabs, shard[None], (ids[0], 0, 0))

    (up_a, up_b, down_a, down_b, *mix_shards), [[in_slab]] = _cast_shards(
        *(shards[n] for n in ("w_up", "w_down", "w_branch_pool", "w_branch_attn", "w_out")),
        exchanges=[_ex_gather([shards["w_in"]])])
    w_in = _stack_rows(whole(shards["w_in"], in_slab))
    (h, u, q, k, v, gate), [mix_slabs] = _inproj(
        x2, g1, w_in, small["b_in"], tabs, seq, exchanges=[_ex_gather(mix_shards)])
    w_bp, w_ba, out_slab = (whole(s, g) for s, g in zip(mix_shards, mix_slabs))
    w_out = _stack_rows(out_slab)
    (y_attn, diff, y_pool), [[got_a, got_b]] = _mixers_fwd(
        q, k, v, sinks, u, w_pool, pool_scale, seq, exchanges=[_ex_gather([up_a, up_b])])
    (merged, mix, x1, h2), [[got_c, got_d]] = _merge_out(
        y_pool, y_attn, gate, x2, w_bp, w_ba, w_out, g2, g3, exchanges=[_ex_gather([down_a, down_b])])
    w_up = (whole(up_a, got_a), whole(up_b, got_b))
    w_down = (whole(down_a, got_c), whole(down_b, got_d))
    act, dff, dup, dx1, dmix, loss_acc, dg4, dg3, dg2 = _mlp_core(h2, x1, mix, tgt, w_up, w_down, g4, g3, g2)

    dw_down = _dw("down", act, dff, 1024, 1024)[0].reshape(N_CHIPS, D_FF // N_CHIPS, D_MODEL)
    (dbp, dba, dgate, dyp, dya), [[got]] = _merge_bwd(
        dmix, gate, y_pool, y_attn, w_out, w_bp, w_ba, exchanges=[_ex_pair([dw_down])])
    ps_down = _pair_sum(ids, dw_down, got)
    (dw_up,), [[got]] = _dw("up", h2, dup, 1024, 1024, shard_cols=True, exchanges=[_ex_chip([ps_down[1]])])
    half_down = _chip_sum(ids, ps_down[0], got)
    (dw_out, dw_bp, dw_ba), [[got]] = _dw_mix(merged, dmix, y_pool, dbp, y_attn, dba, exchanges=[_ex_pair([dw_up])])
    ps_up = _pair_sum(ids, dw_up, got)
    dw_mix = [dw_out.reshape(N_CHIPS, D_MODEL // N_CHIPS, D_MODEL), dw_bp, dw_ba]
    (dq, dk, dv, dsink, du, dw_pool, dps), [[got], gots, [g_down]] = _mixers_bwd(
        q, k, v, dya, sinks, tabs, dyp, diff, w_pool, pool_scale, seq,
        exchanges=[_ex_chip([ps_up[1]]), _ex_pair(dw_mix), _ex_swap([half_down])])
    half_up = _chip_sum(ids, ps_up[0], got)
    ps_mix = _pair_sum_small(ids, dw_mix, gots)
    parts = (du, dq, dk, dv, dgate)
    early = _early_block(dg2, dg3, dg4, dps, dsink[:, 0], loss_acc[0, 0])
    mat = dw_pool.reshape(4 * POOL_GC, POOL_GC)
    (dw_in_t, db_in), [gots, [gearly, gmat], [g_up]] = _dw_in(
        h, parts, exchanges=[_ex_chip([p[1] for p in ps_mix]), _ex_allgather([early, mat]), _ex_swap([half_up])])
    half_mix = _chip_sum_small(ids, [p[0] for p in ps_mix], gots)
    dw_in = dw_in_t.reshape(N_CHIPS, IN_WIDTH // N_CHIPS, D_MODEL)
    g_mix, [got] = _alone("swap_mix_pair_in", _ex_swap(half_mix), _ex_pair([dw_in]))
    ps_in = _pair_sum(ids, dw_in, got)
    (gx, dg1), [[got]] = _inproj_bwd(parts, x2, dx1, w_in, g1, exchanges=[_ex_chip([ps_in[1]])])
    [g_in], [glate] = _alone("swap_in_allgather", _ex_swap([_chip_sum(ids, ps_in[0], got)]),
                             _ex_allgather([_late_block(db_in, dg1)]))

    grads = dict(w_in=g_in, w_branch_pool=g_mix[1], w_branch_attn=g_mix[2], w_out=g_mix[0], w_up=g_up, w_down=g_down)
    return (gearly, gmat, glate), gx, grads


def kernel(x, g_mix_pre, w_in, b_in, w_pool, pool_scale, attn_sinks, w_branch_pool, w_branch_attn, w_out, g_mix_post, g_mlp_pre, w_up, w_down, g_mlp_post, loss_target, m_g_mix_pre, m_w_in, m_b_in, m_w_pool, m_pool_scale, m_attn_sinks, m_w_branch_pool, m_w_branch_attn, m_w_out, m_g_mix_post, m_g_mlp_pre, m_w_up, m_w_down, m_g_mlp_post, v_g_mix_pre, v_w_in, v_b_in, v_w_pool, v_pool_scale, v_attn_sinks, v_w_branch_pool, v_w_branch_attn, v_w_out, v_g_mix_post, v_g_mlp_pre, v_w_up, v_w_down, v_g_mlp_post):
    weights = dict(g_mix_pre=g_mix_pre, w_in=w_in, b_in=b_in, w_pool=w_pool, pool_scale=pool_scale,
                   attn_sinks=attn_sinks, w_branch_pool=w_branch_pool, w_branch_attn=w_branch_attn, w_out=w_out,
                   g_mix_post=g_mix_post, g_mlp_pre=g_mlp_pre, w_up=w_up, w_down=w_down, g_mlp_post=g_mlp_post)
    mom1 = dict(g_mix_pre=m_g_mix_pre, w_in=m_w_in, b_in=m_b_in, w_pool=m_w_pool, pool_scale=m_pool_scale,
                attn_sinks=m_attn_sinks, w_branch_pool=m_w_branch_pool, w_branch_attn=m_w_branch_attn,
                w_out=m_w_out, g_mix_post=m_g_mix_post, g_mlp_pre=m_g_mlp_pre, w_up=m_w_up, w_down=m_w_down,
                g_mlp_post=m_g_mlp_post)
    mom2 = dict(g_mix_pre=v_g_mix_pre, w_in=v_w_in, b_in=v_b_in, w_pool=v_w_pool, pool_scale=v_pool_scale,
                attn_sinks=v_attn_sinks, w_branch_pool=v_w_branch_pool, w_branch_attn=v_w_branch_attn,
                w_out=v_w_out, g_mix_post=v_g_mix_post, g_mlp_pre=v_g_mlp_pre, w_up=v_w_up, w_down=v_w_down,
                g_mlp_post=v_g_mlp_post)
    b_loc, seq, _ = x.shape
    x2 = x.reshape(b_loc * seq, D_MODEL)
    tgt = loss_target.reshape(b_loc * seq, D_MODEL)
    ids = jnp.stack([2 * lax.axis_index("x") + lax.axis_index("y"), lax.axis_index("c")]).astype(jnp.int32)

    def flat(n, a):
        return a[0].T if n == "w_in" else a[0]

    def unflat(n, a):
        return (a.T if n == "w_in" else a)[None]

    shards = {n: flat(n, weights[n]).astype(BF16) if n == "w_in" else flat(n, weights[n]) for n in _BIG}
    small = {n: weights[n] for n in _ORDER if n not in _BIG}
    (gearly, gmat, glate), gx, grads = _step(x2, tgt, seq, shards, small, ids)

    def two_d(src):
        return {n: src[n].reshape(4 * POOL_GC, POOL_GC) if n == "w_pool" else src[n] for n in _SMALL_NAMES}

    loss, per = _small_update(gearly, gmat, glate, two_d(weights), two_d(mom1), two_d(mom2))
    delta, new_m, new_v = {}, {}, {}
    for n in _SMALL_NAMES:
        grads[n], delta[n], new_m[n], new_v[n] = (a.reshape(weights[n].shape) for a in per[n])
    for n in _BIG:
        update = _adamw if n == "w_in" else _adamw_sparse
        d, nm, nv = update(flat(n, weights[n]), grads[n], flat(n, mom1[n]), flat(n, mom2[n]))
        grads[n] = unflat(n, grads[n])
        delta[n], new_m[n], new_v[n] = unflat(n, d), unflat(n, nm), unflat(n, nv)

    return (loss[0, 0], gx.reshape(x.shape), *[grads[n] for n in _ORDER], *[delta[n] for n in _ORDER],
            *[new_m[n] for n in _ORDER], *[new_v[n] for n in _ORDER])
```

```python
import jax
import jax.numpy as jnp
from jax import lax
from jax.experimental import pallas as pl
from jax.experimental.pallas import tpu as pltpu
from jax.experimental.pallas import tpu_sc as plsc

F32 = jnp.float32
BF16 = jnp.bfloat16

D_MODEL = 1024
POOL_WINDOWS = (2, 4, 8, 16)
POOL_WIDTH = 512
POOL_GC = 128
HALO = 16
HEAD_DIM = 64
N_Q_HEADS = 8
ATTN_WIDTH = 512
KV_WIDTH = 128
BLOCK = 128
NEG_INF = -1e30
ROPE_THETA = 500000.0
ROT_DIM = 16
GATE_WIDTH = 2048
IN_WIDTH = 3328
D_FF = 4096
EPS = 1e-6
SCALE = HEAD_DIM ** -0.5
C_Q, C_K, C_V, C_G = 512, 1024, 1152, 1280

ADAM_LR, ADAM_B1, ADAM_B2, ADAM_EPS, ADAM_WD, ADAM_STEP = 0.001, 0.9, 0.999, 1e-08, 0.01, 10

N_CHIPS = 4
N_DEV = 8
LANES = 128
TM = 512
VMEM_MB = 56

MESH = pl.DeviceIdType.MESH
ANY = pl.BlockSpec(memory_space=pl.ANY)


def _cp(*sem, vmem=VMEM_MB):
    return pltpu.CompilerParams(dimension_semantics=sem, vmem_limit_bytes=vmem * 1024 * 1024)


def _rows(tile, cols):
    return pl.BlockSpec((tile, cols), lambda i: (i, 0))


def _const(shape):
    nd = len(shape)
    return pl.BlockSpec(shape, lambda i: (0,) * nd)


def _sds(shape, dtype):
    return jax.ShapeDtypeStruct(shape, dtype)


def _dot(a, b):
    return jnp.dot(a, b, preferred_element_type=F32)


def _dot_nt(a, b):
    return lax.dot_general(a, b, (((1,), (1,)), ((), ())), preferred_element_type=F32)


def _dot_tn(a, b):
    return lax.dot_general(a, b, (((0,), (0,)), ((), ())), preferred_element_type=F32)


def _rms(x):
    return lax.rsqrt(jnp.mean(x * x, axis=-1, keepdims=True) + EPS)


def _norm_bwd(x, g, dout):
    r = _rms(x)
    n = x * r
    dn = dout * g
    dx = r * (dn - n * jnp.mean(dn * n, axis=-1, keepdims=True))
    return dx, jnp.sum(dout * n, axis=0, keepdims=True)


def _rot_fwd(t, c, a, bt):
    return t * c + pltpu.roll(t, LANES - 8, 1) * a + pltpu.roll(t, 8, 1) * bt


def _rot_bwd(d, c, a, bt):
    return d * c + pltpu.roll(d * a, 8, 1) + pltpu.roll(d * bt, LANES - 8, 1)


def _rope_tables(seq):
    pos = jnp.arange(seq, dtype=F32)
    inv_freq = ROPE_THETA ** (-jnp.arange(0, ROT_DIM, 2, dtype=F32) / ROT_DIM)
    ang = pos[:, None] * inv_freq[None, :]
    cos, sin = jnp.cos(ang), jnp.sin(ang)
    ones = jnp.ones((seq, HEAD_DIM - ROT_DIM), F32)
    zeros8 = jnp.zeros((seq, 8), F32)
    zrest = jnp.zeros((seq, HEAD_DIM - ROT_DIM), F32)
    c = jnp.concatenate([cos, cos, ones], axis=1)
    a = jnp.concatenate([-sin, zeros8, zrest], axis=1)
    bt = jnp.concatenate([zeros8, sin, zrest], axis=1)
    return tuple(jnp.tile(t, (1, 2)) for t in (c, a, bt))


class _Exchange:
    def __init__(self, inputs, out_shapes, sems, start, finish, aliases=None, middle=None):
        self.inputs, self.out_shapes, self.sems = list(inputs), list(out_shapes), list(sems)
        self.start, self.finish, self.aliases = start, finish, dict(aliases or {})
        self.middle = middle


def _call(body, *, name, grid, in_specs, out_specs, out_shape, args, scratch=(), sem=(), exchanges=()):
    in_specs, out_specs, out_shape, scratch = list(in_specs), list(out_specs), list(out_shape), list(scratch)
    if not exchanges:
        return pl.pallas_call(body, name=name, grid=grid, in_specs=in_specs, out_specs=out_specs,
                              out_shape=out_shape, scratch_shapes=scratch, compiler_params=_cp(*sem))(*args)
    n_in, n_out, n_scr = len(in_specs), len(out_specs), len(scratch)
    x_in = [a for ex in exchanges for a in ex.inputs]
    x_out = [s for ex in exchanges for s in ex.out_shapes]
    x_sem = [s for ex in exchanges for s in ex.sems]
    aliases, i_off, o_off = {}, n_in, n_out
    for ex in exchanges:
        for i, o in ex.aliases.items():
            aliases[i_off + i] = o_off + o
        i_off += len(ex.inputs)
        o_off += len(ex.out_shapes)

    def split(flat):
        out, pos = [], 0
        for ex, n in zip(exchanges, flat[1]):
            out.append(flat[0][pos:pos + n])
            pos += n
        return out

    def carrier(*refs):
        pos = 0
        groups = []
        for n in (n_in, len(x_in), n_out, len(x_out), n_scr, len(x_sem)):
            groups.append(refs[pos:pos + n])
            pos += n
        ins, xin, outs, xout, scr, xsem = groups
        xin = split((xin, [len(ex.inputs) for ex in exchanges]))
        xout = split((xout, [len(ex.out_shapes) for ex in exchanges]))
        xsem = split((xsem, [len(ex.sems) for ex in exchanges]))
        first = pl.program_id(0) == 0
        last = pl.program_id(0) == grid[0] - 1
        for d in range(1, len(grid)):
            first = jnp.logical_and(first, pl.program_id(d) == 0)
            last = jnp.logical_and(last, pl.program_id(d) == grid[d] - 1)

        @pl.when(first)
        def _():
            for ex, i, o, s in zip(exchanges, xin, xout, xsem):
                ex.start(i, o, s)

        if any(ex.middle for ex in exchanges):
            half = pl.program_id(0) == 5 * grid[0] // 8
            for d in range(1, len(grid)):
                half = jnp.logical_and(half, pl.program_id(d) == 0)

            @pl.when(half)
            def _():
                for ex, i, o, s in zip(exchanges, xin, xout, xsem):
                    if ex.middle:
                        ex.middle(i, o, s)

        body(*ins, *outs, *scr)

        @pl.when(last)
        def _():
            for ex, i, o, s in zip(exchanges, xin, xout, xsem):
                ex.finish(i, o, s)

    res = pl.pallas_call(
        carrier, name=name, grid=grid, in_specs=in_specs + [ANY] * len(x_in),
        out_specs=out_specs + [ANY] * len(x_out), out_shape=out_shape + x_out,
        scratch_shapes=scratch + x_sem, input_output_aliases=aliases,
        compiler_params=_cp(*(["arbitrary"] * len(grid))),
    )(*args, *x_in)
    return res[:n_out], split((res[n_out:], [len(ex.out_shapes) for ex in exchanges]))


def _alone(name, *exchanges):
    n_in = [len(ex.inputs) for ex in exchanges]
    n_out = [len(ex.out_shapes) for ex in exchanges]
    n_sem = [len(ex.sems) for ex in exchanges]
    aliases, i_off, o_off = {}, 0, 0
    for ex in exchanges:
        for i, o in ex.aliases.items():
            aliases[i_off + i] = o_off + o
        i_off += len(ex.inputs)
        o_off += len(ex.out_shapes)

    def split(flat, counts):
        out, pos = [], 0
        for n in counts:
            out.append(flat[pos:pos + n])
            pos += n
        return out

    def body(*refs):
        ins, outs, sems = split(refs, [sum(n_in), sum(n_out), sum(n_sem)])
        groups = list(zip(exchanges, split(ins, n_in), split(outs, n_out), split(sems, n_sem)))
        for ex, i, o, s in groups:
            ex.start(i, o, s)
        for ex, i, o, s in groups:
            if ex.middle:
                ex.middle(i, o, s)
        for ex, i, o, s in groups:
            ex.finish(i, o, s)

    res = pl.pallas_call(
        body, name=name, in_specs=[ANY] * sum(n_in), out_specs=[ANY] * sum(n_out),
        out_shape=[s for ex in exchanges for s in ex.out_shapes],
        scratch_shapes=[s for ex in exchanges for s in ex.sems], input_output_aliases=aliases,
    )(*[a for ex in exchanges for a in ex.inputs])
    return split(res, n_out)


def _place():
    x, y, c = lax.axis_index("x"), lax.axis_index("y"), lax.axis_index("c")
    chips = [(1 - x, y), (x, 1 - y), (1 - x, 1 - y)]
    return x, y, c, chips


def _remote(src, dst, send, recv, to):
    return pltpu.make_async_remote_copy(src_ref=src, dst_ref=dst, send_sem=send, recv_sem=recv,
                                        device_id=to, device_id_type=MESH)


def _ex_gather(shards):
    nw = len(shards)
    hrs = [s.shape[0] // 2 for s in shards]

    def copies(ins, outs, sems):
        s0, r0, s1, r1, s2, r2, fs, fr = sems
        x, y, c, _ = _place()
        me, xn, yn, dg = (x, y), (1 - x, y), (x, 1 - y), (1 - x, 1 - y)
        nbr = (xn, yn)
        sibling = (x, y, 1 - c)

        def piece(w, chip, core, part=None):
            hr = hrs[w]
            rows = pl.ds(core * hr, hr) if part is None else pl.ds(core * hr + part * (hr // 2), hr // 2)
            return outs[w].at[2 * chip[0] + chip[1], rows]

        def first(w, k, lead):
            part = k if lead else 1 - k
            send, recv = (s0, r0) if lead else (s1, r1)
            rows = pl.ds(c * hrs[w] + part * (hrs[w] // 2), hrs[w] // 2)
            return _remote(ins[w].at[rows], piece(w, me, c, part), send.at[w, k], recv.at[w, k], (*nbr[k], c))

        def landed(w, k, lead):
            part = k if lead else 1 - k
            send, recv = (s0, r0) if lead else (s1, r1)
            return _remote(piece(w, nbr[k], c, part), piece(w, nbr[k], c, part), send.at[w, k], recv.at[w, k],
                           (*nbr[k], c))

        def onward(w, k):
            return _remote(piece(w, nbr[k], c, k), piece(w, nbr[k], c, k), s2.at[w, k], r2.at[w, k],
                           (*nbr[1 - k], c))

        def arrived(w, k):
            return _remote(piece(w, dg, c, k), piece(w, dg, c, k), s2.at[w, k], r2.at[w, k], (*nbr[1 - k], c))

        def passed(w, j):
            chip = (xn, yn, dg)[j]
            return _remote(piece(w, chip, c), piece(w, chip, c), fs.at[w, j], fr.at[w, j], sibling)

        def handed(w, j):
            chip = (xn, yn, dg)[j]
            return _remote(piece(w, chip, 1 - c), piece(w, chip, 1 - c), fs.at[w, j], fr.at[w, j], sibling)

        return first, landed, onward, arrived, passed, handed

    def start(ins, outs, sems):
        first = copies(ins, outs, sems)[0]
        for lead in (True, False):
            for w in range(nw):
                for k in range(2):
                    first(w, k, lead).start()

    def middle(ins, outs, sems):
        _, landed, onward, _, passed, _ = copies(ins, outs, sems)
        for w in range(nw):
            for k in range(2):
                landed(w, k, True).wait_recv()
                onward(w, k).start()
        for w in range(nw):
            for k in range(2):
                landed(w, k, False).wait_recv()
                passed(w, k).start()

    def finish(ins, outs, sems):
        first, _, onward, arrived, passed, handed = copies(ins, outs, sems)
        for w in range(nw):
            for k in range(2):
                arrived(w, k).wait_recv()
            passed(w, 2).start()
        for w in range(nw):
            for j in range(3):
                handed(w, j).wait_recv()
        for w in range(nw):
            for k in range(2):
                first(w, k, True).wait_send()
                first(w, k, False).wait_send()
                onward(w, k).wait_send()
            for j in range(3):
                passed(w, j).wait_send()

    return _Exchange(shards, [_sds((N_CHIPS,) + s.shape, s.dtype) for s in shards],
                     [pltpu.SemaphoreType.DMA((nw, 2))] * 6 + [pltpu.SemaphoreType.DMA((nw, 3))] * 2,
                     start, finish, middle=middle)


def _ex_pair(grads):
    nw = len(grads)

    def copies(ins, outs, sems):
        x, y, c, _ = _place()
        out = []
        for w in range(nw):
            hr = grads[w].shape[1] // 2
            out.append(_remote(ins[w].at[:, pl.ds((1 - c) * hr, hr)], outs[w], sems[0].at[w], sems[1].at[w],
                               (x, y, 1 - c)))
        return out

    def start(ins, outs, sems):
        for cp in copies(ins, outs, sems):
            cp.start()

    def finish(ins, outs, sems):
        for cp in copies(ins, outs, sems):
            cp.wait()

    return _Exchange(grads, [_sds((N_CHIPS, g.shape[1] // 2, g.shape[2]), F32) for g in grads],
                     [pltpu.SemaphoreType.DMA((nw,))] * 2, start, finish)


def _ex_chip(pieces):
    nw = len(pieces)

    def copies(ins, outs, sems):
        x, y, c, chips = _place()
        return [_remote(ins[w].at[2 * cx + cy], outs[w].at[k], sems[0].at[w, k], sems[1].at[w, k], (cx, cy, c))
                for w in range(nw) for k, (cx, cy) in enumerate(chips)]

    def start(ins, outs, sems):
        for cp in copies(ins, outs, sems):
            cp.start()

    def finish(ins, outs, sems):
        for cp in copies(ins, outs, sems):
            cp.wait()

    return _Exchange(pieces, [_sds((3,) + p.shape[1:], BF16) for p in pieces],
                     [pltpu.SemaphoreType.DMA((nw, 3))] * 2, start, finish)


def _ex_swap(fulls):
    nw = len(fulls)

    def start(ins, outs, sems):
        x, y, c, _ = _place()
        for w in range(nw):
            hr = fulls[w].shape[0] // 2
            mine = pl.ds(c * hr, hr)
            _remote(ins[w].at[mine], outs[w].at[mine], sems[0].at[w], sems[1].at[w], (x, y, 1 - c)).start()

    def finish(ins, outs, sems):
        x, y, c, _ = _place()
        for w in range(nw):
            hr = fulls[w].shape[0] // 2
            mine, theirs = pl.ds(c * hr, hr), pl.ds((1 - c) * hr, hr)
            _remote(ins[w].at[mine], outs[w].at[mine], sems[0].at[w], sems[1].at[w], (x, y, 1 - c)).wait_send()
            _remote(ins[w].at[theirs], outs[w].at[theirs], sems[0].at[w], sems[1].at[w], (x, y, 1 - c)).wait_recv()

    return _Exchange(fulls, [_sds(f.shape, F32) for f in fulls], [pltpu.SemaphoreType.DMA((nw,))] * 2,
                     start, finish, aliases={w: w for w in range(nw)})


def _ex_allgather(blocks):
    nb = len(blocks)

    def copies(ins, outs, sems):
        send, recv, lsem = sems
        x, y, c, chips = _place()
        me, sibling = (x, y, c), (x, y, 1 - c)

        def rows(b, px, py, pc):
            m_per = blocks[b].shape[0]
            return outs[b].at[pl.ds((4 * px + 2 * py + pc) * m_per, m_per), :]

        def copy(b, k, blk, to, src=None):
            return _remote(rows(b, *blk) if src is None else src, rows(b, *blk), send.at[b, k], recv.at[b, k], to)

        def mine(b):
            return pltpu.make_async_copy(ins[b], rows(b, *me), lsem.at[b])

        def first(b, k):
            return copy(b, k, me, sibling if k == 0 else (*chips[k - 1], c), src=ins[b])

        def passed(b, j):
            return copy(b, 4 + j, (*chips[j], c), sibling)

        def landed(b, j):
            return copy(b, 1 + j, (*chips[j], c), me)

        def handed(b, k):
            return copy(b, 0, sibling, me) if k == 0 else copy(b, 3 + k, (*chips[k - 1], 1 - c), me)

        return mine, first, passed, landed, handed

    def start(ins, outs, sems):
        mine, first, _, _, _ = copies(ins, outs, sems)
        for b in range(nb):
            mine(b).start()
            for k in range(4):
                first(b, k).start()

    def finish(ins, outs, sems):
        mine, first, passed, landed, handed = copies(ins, outs, sems)
        sent = []
        for b in range(nb):
            for j in range(3):
                landed(b, j).wait_recv()
                cp = passed(b, j)
                cp.start()
                sent.append(cp)
        for b in range(nb):
            for k in range(4):
                handed(b, k).wait_recv()
            for k in range(4):
                first(b, k).wait_send()
        for cp in sent:
            cp.wait_send()
        for b in range(nb):
            mine(b).wait()

    return _Exchange(blocks, [_sds((N_DEV * b.shape[0], b.shape[1]), F32) for b in blocks],
                     [pltpu.SemaphoreType.DMA((nb, 7)), pltpu.SemaphoreType.DMA((nb, 7)), pltpu.SemaphoreType.DMA((nb,))],
                     start, finish)


def _cast_shards(w_up, w_down, w_bp, w_ba, w_out, exchanges=()):
    r, c = w_up.shape
    tr = HALF // 2
    steps = r // tr

    def body(up_ref, down_ref, bp_ref, ba_ref, out_ref, ua_ref, ub_ref, da_ref, db_ref, bpo_ref, bao_ref, outo_ref):
        i = pl.program_id(0)

        @pl.when(i == 0)
        def _():
            for src, dst in ((bp_ref, bpo_ref), (ba_ref, bao_ref), (out_ref, outo_ref)):
                dst[...] = src[...].astype(BF16)

        @pl.when(i < steps // 2)
        def _():
            ua_ref[...] = up_ref[...].astype(BF16)
            da_ref[...] = down_ref[...].astype(BF16)

        @pl.when(i >= steps // 2)
        def _():
            ub_ref[...] = up_ref[...].astype(BF16)
            db_ref[...] = down_ref[...].astype(BF16)

    rows = _rows(tr, c)
    first = pl.BlockSpec((tr, c), lambda i: (jnp.minimum(i, steps // 2 - 1), 0))
    second = pl.BlockSpec((tr, c), lambda i: (jnp.maximum(i - steps // 2, 0), 0))
    half = _sds((HALF, c), BF16)
    return _call(
        body, name="cast_shards", grid=(steps,),
        in_specs=[rows, rows, _const(w_bp.shape), _const(w_ba.shape), _const(w_out.shape)],
        out_specs=[first, second, first, second, _const(w_bp.shape), _const(w_ba.shape), _const(w_out.shape)],
        out_shape=[half, half, half, half, _sds(w_bp.shape, BF16), _sds(w_ba.shape, BF16), _sds(w_out.shape, BF16)],
        args=(w_up, w_down, w_bp, w_ba, w_out), sem=("arbitrary",), exchanges=exchanges)


def _inproj(x2, g1, w_in_t, b_in, tabs, seq, exchanges=()):
    T = x2.shape[0]
    tm = min(TM, seq)
    nseq = seq // tm

    def body(x_ref, g_ref, w_ref, b_ref, c_ref, a_ref, bt_ref, h_ref, u_ref, q_ref, k_ref, v_ref, gate_ref):
        x = x_ref[...]
        h = (x * _rms(x) * g_ref[...]).astype(BF16)
        h_ref[...] = h

        def proj(lo, hi):
            return _dot_nt(h, w_ref[lo:hi, :]) + b_ref[:, lo:hi]

        c, a, bt = c_ref[...], a_ref[...], bt_ref[...]
        u_ref[...] = proj(0, C_Q)
        q = proj(C_Q, C_K)
        for p in range(4):
            sl = slice(LANES * p, LANES * (p + 1))
            q_ref[:, sl] = (_rot_fwd(q[:, sl], c, a, bt) * SCALE).astype(BF16)
        kv = proj(C_K, C_G)
        k_ref[...] = _rot_fwd(kv[:, :KV_WIDTH], c, a, bt).astype(BF16)
        v_ref[...] = kv[:, KV_WIDTH:].astype(BF16)
        for j in range(2):
            lo = C_G + D_MODEL * j
            gate_ref[:, D_MODEL * j:D_MODEL * (j + 1)] = jax.nn.sigmoid(proj(lo, lo + D_MODEL)).astype(BF16)

    tab = pl.BlockSpec((tm, LANES), lambda i: (i % nseq, 0))
    return _call(
        body, name="inproj", grid=(T // tm,),
        in_specs=[_rows(tm, D_MODEL), _const((1, D_MODEL)), _const((IN_WIDTH, D_MODEL)), _const((1, IN_WIDTH)),
                  tab, tab, tab],
        out_specs=[_rows(tm, D_MODEL), _rows(tm, POOL_WIDTH), _rows(tm, ATTN_WIDTH), _rows(tm, KV_WIDTH),
                   _rows(tm, KV_WIDTH), _rows(tm, GATE_WIDTH)],
        out_shape=[_sds((T, D_MODEL), BF16), _sds((T, POOL_WIDTH), F32), _sds((T, ATTN_WIDTH), BF16),
                   _sds((T, KV_WIDTH), BF16), _sds((T, KV_WIDTH), BF16), _sds((T, GATE_WIDTH), BF16)],
        args=(x2, g1, w_in_t, b_in, *tabs), sem=("parallel",), exchanges=exchanges)


def _inv_count(pos, w):
    return 1.0 / jnp.minimum(pos + 1, w).astype(F32)


def _pool_tile(i, tp, nseq, u_ref, prev_ref, w_ref, s_ref, diff_ref, y_ref):
    first = (i % nseq) == 0
    prev = jnp.where(first, 0.0, prev_ref[...])
    ext = jnp.concatenate([prev, u_ref[...]], axis=0)
    pos = (i % nseq) * tp + lax.broadcasted_iota(jnp.int32, (tp, 1), 0)
    for gi, w in enumerate(POOL_WINDOWS):
        sl = slice(POOL_GC * gi, POOL_GC * (gi + 1))
        xg = ext[:, sl]
        s = xg
        sh = 1
        while sh < w:
            s = s + pltpu.roll(s, sh, 0)
            sh *= 2
        pooled = s[HALO:] * _inv_count(pos, w)
        diff = (pooled - xg[HALO:]).astype(BF16)
        diff_ref[:, sl] = diff
        mixed = _dot(diff, w_ref[gi].astype(BF16))
        y_ref[:, sl] = (mixed * s_ref[:, sl]).astype(BF16)


def _pool_specs(tp):
    per = tp // HALO
    return [_rows(tp, POOL_WIDTH), pl.BlockSpec((HALO, POOL_WIDTH), lambda i: (jnp.maximum(i * per - 1, 0), 0)),
            _const((4, POOL_GC, POOL_GC)), _const((1, POOL_WIDTH))]


GROUP = 4
GROWS = GROUP * BLOCK


def _attn_masks(n):
    qi = lax.broadcasted_iota(jnp.int32, (GROWS, 2 * BLOCK), 0) % BLOCK
    kj = lax.broadcasted_iota(jnp.int32, (GROWS, 2 * BLOCK), 1)
    rel = qi + BLOCK - kj
    valid = (rel >= 0) & (rel < BLOCK) & (kj >= jnp.where(n > 0, 0, BLOCK))
    lo = lax.broadcasted_iota(jnp.int32, (BLOCK, LANES), 1) < HEAD_DIM
    return valid, lo


def _by_example(bl, *arrays):
    return [a.reshape(bl, a.shape[0] // bl, a.shape[1]) for a in arrays]


def _stack_heads(ref, h, lo):
    keep = lo if h == 0 else jnp.logical_not(lo)
    pieces = []
    for p in (2 * h, 2 * h + 1):
        xp = ref[:, LANES * p:LANES * (p + 1)].astype(F32)
        for e in range(2):
            t = xp if e == h else pltpu.roll(xp, HEAD_DIM, 1)
            pieces.append(jnp.where(keep, t, 0.0).astype(BF16))
    return jnp.concatenate(pieces, axis=0)


def _unstack_heads(stacked, h, lo):
    pairs = []
    for j in range(2):
        parts = []
        for e in range(2):
            t = stacked[BLOCK * (2 * j + e):BLOCK * (2 * j + e + 1)]
            parts.append(t if e == h else pltpu.roll(t, HEAD_DIM, 1))
        pairs.append(jnp.where(lo, parts[0], parts[1]))
    return pairs


def _sink_rows(sink_ref, h):
    head = lax.broadcasted_iota(jnp.int32, (GROWS, 1), 0) // BLOCK
    col = jnp.zeros((GROWS, 1), F32) + sink_ref[GROUP * h]
    for g in range(1, GROUP):
        col = jnp.where(head == g, sink_ref[GROUP * h + g], col)
    return col


def _group_probs(qs, kk, valid, sink):
    s = jnp.where(valid, _dot_nt(qs, kk), NEG_INF)
    m = jnp.maximum(jnp.max(s, axis=1, keepdims=True), sink)
    ex = jnp.exp(s - m)
    es = jnp.exp(sink - m)
    inv = 1.0 / (jnp.sum(ex, axis=1, keepdims=True) + es)
    return ex * inv, es * inv


def _mixers_fwd(q, k, v, sinks, u, w_pool, pool_scale, seq, exchanges=()):
    T = q.shape[0]
    nb = seq // BLOCK
    bl = T // seq
    tp = T // nb
    nseq = seq // tp

    def body(sink_ref, q_ref, kp_ref, kc_ref, vp_ref, vc_ref, u_ref, prev_ref, w_ref, s_ref, o_ref, diff_ref, y_ref):
        n = pl.program_id(0)
        valid, lo = _attn_masks(n)
        for b in range(bl):
            kk = jnp.concatenate([kp_ref[b], kc_ref[b]], axis=0)
            vv = jnp.concatenate([vp_ref[b], vc_ref[b]], axis=0)
            for h in range(2):
                qs = _stack_heads(q_ref.at[b], h, lo)
                pr, _ = _group_probs(qs, kk, valid, _sink_rows(sink_ref, h))
                o = _dot(pr.astype(BF16), vv)
                for j, pair in enumerate(_unstack_heads(o, h, lo)):
                    p = 2 * h + j
                    o_ref[b, :, LANES * p:LANES * (p + 1)] = pair.astype(BF16)
        _pool_tile(n, tp, nseq, u_ref, prev_ref, w_ref, s_ref, diff_ref, y_ref)

    cur = lambda n: (0, n, 0)
    prv = lambda n: (0, jnp.maximum(n - 1, 0), 0)
    kv = lambda m: pl.BlockSpec((bl, BLOCK, KV_WIDTH), m)
    res = _call(
        body, name="mixers_fwd", grid=(nb,),
        in_specs=[pl.BlockSpec(memory_space=pltpu.SMEM), pl.BlockSpec((bl, BLOCK, ATTN_WIDTH), cur),
                  kv(prv), kv(cur), kv(prv), kv(cur)] + _pool_specs(tp),
        out_specs=[pl.BlockSpec((bl, BLOCK, ATTN_WIDTH), cur), _rows(tp, POOL_WIDTH), _rows(tp, POOL_WIDTH)],
        out_shape=[_sds((bl, seq, ATTN_WIDTH), BF16), _sds((T, POOL_WIDTH), BF16), _sds((T, POOL_WIDTH), BF16)],
        args=(sinks, *_by_example(bl, q, k, k, v, v), u, u, w_pool, pool_scale), sem=("parallel",),
        exchanges=exchanges)
    outs, rest = res if exchanges else (res, None)
    return [outs[0].reshape(T, ATTN_WIDTH), outs[1], outs[2]], rest


def _branch(y, w_ref):
    return jnp.concatenate([_dot(y, w_ref[j]) for j in range(N_CHIPS)], axis=1)


def _merge_out(y_pool, y_attn, gate, x2, w_bp, w_ba, w_out, g2, g3, exchanges=()):
    T = x2.shape[0]
    tm = min(TM, T)

    def body(yp_ref, ya_ref, gate_ref, x_ref, wbp_ref, wba_ref, wo_ref, g2_ref, g3_ref,
             mg_ref, mix_ref, x1_ref, h2_ref):
        bp, ba = _branch(yp_ref[...], wbp_ref), _branch(ya_ref[...], wba_ref)
        merged = (gate_ref[:, :D_MODEL].astype(F32) * bp + gate_ref[:, D_MODEL:].astype(F32) * ba).astype(BF16)
        mg_ref[...] = merged
        mix = _dot(merged, wo_ref[...])
        mix_ref[...] = mix
        x1 = x_ref[...] + mix * _rms(mix) * g2_ref[...]
        x1_ref[...] = x1
        h2_ref[...] = (x1 * _rms(x1) * g3_ref[...]).astype(BF16)

    return _call(
        body, name="merge_out", grid=(T // tm,),
        in_specs=[_rows(tm, POOL_WIDTH), _rows(tm, ATTN_WIDTH), _rows(tm, GATE_WIDTH), _rows(tm, D_MODEL),
                  _const(w_bp.shape), _const(w_ba.shape), _const((D_MODEL, D_MODEL)),
                  _const((1, D_MODEL)), _const((1, D_MODEL))],
        out_specs=[_rows(tm, D_MODEL)] * 4,
        out_shape=[_sds((T, D_MODEL), BF16), _sds((T, D_MODEL), F32), _sds((T, D_MODEL), F32),
                   _sds((T, D_MODEL), BF16)],
        args=(y_pool, y_attn, gate, x2, w_bp, w_ba, w_out, g2, g3), sem=("parallel",), exchanges=exchanges)


HALF = D_MODEL // 2
TM_MLP = 256


def _mlp_core(h2, x1, mix, tgt, w_up, w_down, g4, g3, g2):
    T = h2.shape[0]
    tm = min(TM_MLP, T)

    def body(h_ref, x1_ref, mix_ref, t_ref, g_ref, g3_ref, g2_ref, ua_hbm, ub_hbm, da_hbm, db_hbm,
             act_ref, dff_ref, dup_ref, dx1_ref, dmix_ref, loss_ref, dg_ref, dg3_ref, dg2_ref,
             wu, wd, relu_scr, sems):
        def weight_copy(i):
            src, dst = ((ua_hbm, wu.at[:, :HALF]), (ub_hbm, wu.at[:, HALF:]),
                        (da_hbm, wd.at[:, :HALF]), (db_hbm, wd.at[:, HALF:]))[i]
            return pltpu.make_async_copy(src, dst, sems.at[i])

        @pl.when(pl.program_id(0) == 0)
        def _():
            for i in range(4):
                weight_copy(i).start()
            loss_ref[...] = jnp.zeros_like(loss_ref)
            for ref in (dg_ref, dg3_ref, dg2_ref):
                ref[...] = jnp.zeros_like(ref)
            weight_copy(0).wait()
            weight_copy(1).wait()

        h = h_ref[...]
        ff = None
        for j in range(N_CHIPS):
            lo = D_MODEL * j
            relu = jnp.maximum(_dot(h, wu[j]), 0.0)
            if j == 0:
                @pl.when(pl.program_id(0) == 0)
                def _():
                    weight_copy(2).wait()
                    weight_copy(3).wait()
            relu_scr[:, lo:lo + D_MODEL] = relu
            act = jnp.square(relu).astype(BF16)
            act_ref[:, lo:lo + D_MODEL] = act
            t = _dot(act, wd[j])
            ff = t if ff is None else ff + t
        g = g_ref[...]
        x1 = x1_ref[...]
        err = x1 + ff * _rms(ff) * g - t_ref[...]
        loss_ref[...] += jnp.sum(err * err) * (0.5 / D_MODEL)
        dy = err * (1.0 / D_MODEL)
        dff, dg = _norm_bwd(ff, g, dy)
        dg_ref[...] += dg
        dff = dff.astype(BF16)
        dff_ref[...] = dff
        dh2 = None
        for j in range(N_CHIPS):
            lo = D_MODEL * j
            dup = (_dot_nt(dff, wd[j]) * (2.0 * relu_scr[:, lo:lo + D_MODEL])).astype(BF16)
            dup_ref[:, lo:lo + D_MODEL] = dup
            t = _dot_nt(dup, wu[j])
            dh2 = t if dh2 is None else dh2 + t
        dx, dg3 = _norm_bwd(x1, g3_ref[...], dh2)
        dx1 = dy + dx
        dx1_ref[...] = dx1
        dg3_ref[...] += dg3
        dmix, dg2 = _norm_bwd(mix_ref[...], g2_ref[...], dx1)
        dmix_ref[...] = dmix.astype(BF16)
        dg2_ref[...] += dg2

    slabs = pltpu.VMEM((N_CHIPS, D_MODEL, D_MODEL), BF16)
    gain = _const((1, D_MODEL))
    return pl.pallas_call(
        body, name="mlp_core", grid=(T // tm,),
        in_specs=[_rows(tm, D_MODEL)] * 4 + [gain] * 3 + [ANY] * 4,
        out_specs=[_rows(tm, D_FF), _rows(tm, D_MODEL), _rows(tm, D_FF), _rows(tm, D_MODEL), _rows(tm, D_MODEL),
                   _const((8, LANES)), gain, gain, gain],
        out_shape=[_sds((T, D_FF), BF16), _sds((T, D_MODEL), BF16), _sds((T, D_FF), BF16), _sds((T, D_MODEL), F32),
                   _sds((T, D_MODEL), BF16), _sds((8, LANES), F32)] + [_sds((1, D_MODEL), F32)] * 3,
        scratch_shapes=[slabs] * 2 + [pltpu.VMEM((tm, D_FF), F32), pltpu.SemaphoreType.DMA((4,))],
        compiler_params=_cp("arbitrary"),
    )(h2, x1, mix, tgt, g4, g3, g2, *w_up, *w_down)


def _dw(tag, a, g, ta, tn, shard_cols=False, exchanges=()):
    T, ka = a.shape
    n = g.shape[1]
    tk = min(2 * TM, T)
    nk = T // tk

    def body(a_ref, g_ref, o_ref):
        @pl.when(pl.program_id(2) == 0)
        def _():
            o_ref[...] = jnp.zeros_like(o_ref)

        o_ref[...] += _dot_tn(a_ref[...], g_ref[...])

    if shard_cols:
        per = (n // N_CHIPS) // tn
        out_spec = pl.BlockSpec((None, ta, tn), lambda i, j, k: (j // per, i, j % per))
        out_shape = _sds((N_CHIPS, ka, n // N_CHIPS), F32)
    else:
        out_spec = pl.BlockSpec((ta, tn), lambda i, j, k: (i, j))
        out_shape = _sds((ka, n), F32)
    return _call(
        body, name="dw_" + tag, grid=(ka // ta, n // tn, nk),
        in_specs=[pl.BlockSpec((tk, ta), lambda i, j, k: (k, i)), pl.BlockSpec((tk, tn), lambda i, j, k: (k, j))],
        out_specs=[out_spec], out_shape=[out_shape],
        args=(a, g), sem=("parallel", "parallel", "arbitrary"), exchanges=exchanges)


def _dw_mix(merged, dmix, y_pool, dbp, y_attn, dba, exchanges=()):
    T = merged.shape[0]
    tk = min(2 * TM, T)
    c = D_MODEL // N_CHIPS

    def body(mg_ref, dmix_ref, yp_ref, dbp_ref, ya_ref, dba_ref, out_ref, bp_ref, ba_ref):
        @pl.when(pl.program_id(0) == 0)
        def _():
            for ref in (out_ref, bp_ref, ba_ref):
                ref[...] = jnp.zeros_like(ref)

        out_ref[...] += _dot_tn(mg_ref[...], dmix_ref[...])
        for y_ref, d_ref, o_ref in ((yp_ref, dbp_ref, bp_ref), (ya_ref, dba_ref, ba_ref)):
            res = _dot_tn(y_ref[...], d_ref[...])
            for j in range(N_CHIPS):
                o_ref[j] += res[:, c * j:c * (j + 1)]

    slabs = (N_CHIPS, POOL_WIDTH, c)
    return _call(
        body, name="dw_mix", grid=(T // tk,),
        in_specs=[_rows(tk, D_MODEL), _rows(tk, D_MODEL), _rows(tk, POOL_WIDTH), _rows(tk, D_MODEL),
                  _rows(tk, ATTN_WIDTH), _rows(tk, D_MODEL)],
        out_specs=[_const((D_MODEL, D_MODEL)), _const(slabs), _const(slabs)],
        out_shape=[_sds((D_MODEL, D_MODEL), F32), _sds(slabs, F32), _sds(slabs, F32)],
        args=(merged, dmix, y_pool, dbp, y_attn, dba), sem=("arbitrary",), exchanges=exchanges)


def _merge_bwd(dmix, gate, y_pool, y_attn, w_out, w_bp, w_ba, exchanges=()):
    T = dmix.shape[0]
    tm = min(TM, T)

    def body(dmix_ref, gate_ref, yp_ref, ya_ref, wo_ref, wbp_ref, wba_ref,
             dbp_ref, dba_ref, dgate_ref, dyp_ref, dya_ref):
        dm = _dot_nt(dmix_ref[...], wo_ref[...])
        for j, (y_ref, db_ref, w_ref, dy_ref) in enumerate(
                ((yp_ref, dbp_ref, wbp_ref, dyp_ref), (ya_ref, dba_ref, wba_ref, dya_ref))):
            sl = slice(D_MODEL * j, D_MODEL * (j + 1))
            gt = gate_ref[:, sl].astype(F32)
            db = (dm * gt).astype(BF16)
            db_ref[...] = db
            dgate_ref[:, sl] = (dm * _branch(y_ref[...], w_ref) * gt * (1.0 - gt)).astype(BF16)
            cw = D_MODEL // N_CHIPS
            dy = _dot_nt(db[:, :cw], w_ref[0])
            for c in range(1, N_CHIPS):
                dy = dy + _dot_nt(db[:, cw * c:cw * (c + 1)], w_ref[c])
            dy_ref[...] = dy.astype(dy_ref.dtype)

    return _call(
        body, name="merge_bwd", grid=(T // tm,),
        in_specs=[_rows(tm, D_MODEL), _rows(tm, GATE_WIDTH), _rows(tm, POOL_WIDTH), _rows(tm, ATTN_WIDTH),
                  _const((D_MODEL, D_MODEL)), _const(w_bp.shape), _const(w_ba.shape)],
        out_specs=[_rows(tm, D_MODEL), _rows(tm, D_MODEL), _rows(tm, GATE_WIDTH), _rows(tm, POOL_WIDTH),
                   _rows(tm, ATTN_WIDTH)],
        out_shape=[_sds((T, D_MODEL), BF16), _sds((T, D_MODEL), BF16), _sds((T, GATE_WIDTH), BF16),
                   _sds((T, POOL_WIDTH), F32), _sds((T, ATTN_WIDTH), BF16)],
        args=(dmix, gate, y_pool, y_attn, w_out, w_bp, w_ba), sem=("parallel",), exchanges=exchanges)


def _mixers_bwd(q, k, v, do, sinks, tabs, dyp, diff, w_pool, pool_scale, seq, exchanges=()):
    T = q.shape[0]
    nb = seq // BLOCK
    bl = T // seq
    steps = nb + 1
    tp = T // nb
    nseq = seq // tp
    per = tp // HALO
    last_halo = T // HALO - 1

    def body(sink_ref, q_ref, do_ref, kp_ref, kc_ref, vp_ref, vc_ref, c_ref, a_ref, bt_ref, cp_ref, ap_ref, btp_ref,
             dy_ref, nxt_ref, diff_ref, w_ref, s_ref,
             dq_ref, dk_ref, dv_ref, dsink_ref, du_ref, dw_ref, ds_ref, ck_ref, cv_ref):
        n = pl.program_id(0)

        @pl.when(n == 0)
        def _():
            for ref in (dsink_ref, ck_ref, cv_ref, dw_ref, ds_ref):
                ref[...] = jnp.zeros_like(ref)

        @pl.when(n < nb)
        def _():
            _pool_bwd_tile(n, tp, nseq, dy_ref, nxt_ref, diff_ref, w_ref, s_ref, du_ref, dw_ref, ds_ref)
            valid, lo = _attn_masks(n)
            for b in range(bl):
                kk = jnp.concatenate([kp_ref[b], kc_ref[b]], axis=0)
                vv = jnp.concatenate([vp_ref[b], vc_ref[b]], axis=0)
                dk_acc = jnp.zeros((2 * BLOCK, KV_WIDTH), F32)
                dv_acc = jnp.zeros((2 * BLOCK, KV_WIDTH), F32)
                for h in range(2):
                    qs = _stack_heads(q_ref.at[b], h, lo)
                    dos = _stack_heads(do_ref.at[b], h, lo)
                    pr, ps = _group_probs(qs, kk, valid, _sink_rows(sink_ref, h))
                    dp = _dot_nt(dos, vv)
                    delta = jnp.sum(pr * dp, axis=1, keepdims=True)
                    ds = (pr * (dp - delta)).astype(BF16)
                    dsk = ps * delta
                    for g in range(GROUP):
                        idx = GROUP * h + g
                        dsink_ref[idx:idx + 1, :] += (jnp.zeros((1, LANES), F32)
                                                      - jnp.sum(dsk[BLOCK * g:BLOCK * (g + 1)]))
                    dk_acc = dk_acc + _dot_tn(ds, qs)
                    dv_acc = dv_acc + _dot_tn(pr.astype(BF16), dos)
                    for j, pair in enumerate(_unstack_heads(_dot(ds, kk) * SCALE, h, lo)):
                        sl = slice(LANES * (2 * h + j), LANES * (2 * h + j + 1))
                        dq_ref[b, :, sl] = _rot_bwd(pair, c_ref[...], a_ref[...], bt_ref[...]).astype(BF16)
                fin_k = ck_ref[b] + dk_acc[:BLOCK]
                dk_ref[b] = _rot_bwd(fin_k, cp_ref[...], ap_ref[...], btp_ref[...]).astype(BF16)
                dv_ref[b] = (cv_ref[b] + dv_acc[:BLOCK]).astype(BF16)
                ck_ref[b] = dk_acc[BLOCK:]
                cv_ref[b] = dv_acc[BLOCK:]

        @pl.when(n == nb)
        def _():
            for b in range(bl):
                dk_ref[b] = _rot_bwd(ck_ref[b], cp_ref[...], ap_ref[...], btp_ref[...]).astype(BF16)
                dv_ref[b] = cv_ref[b].astype(BF16)

    cur = lambda n: (0, jnp.minimum(n, nb - 1), 0)
    prv = lambda n: (0, jnp.clip(n - 1, 0, nb - 1), 0)
    tcur = lambda n: (jnp.minimum(n, nb - 1), 0)
    tprv = lambda n: (jnp.clip(n - 1, 0, nb - 1), 0)
    wide = lambda m: pl.BlockSpec((bl, BLOCK, ATTN_WIDTH), m)
    kv = lambda m: pl.BlockSpec((bl, BLOCK, KV_WIDTH), m)
    tab = lambda m: pl.BlockSpec((BLOCK, LANES), m)
    tile = lambda n: (jnp.minimum(n, nb - 1), 0)
    halo = lambda n: (jnp.minimum((jnp.minimum(n, nb - 1) + 1) * per, last_halo), 0)
    rows = pl.BlockSpec((tp, POOL_WIDTH), tile)
    res = _call(
        body, name="mixers_bwd", grid=(steps,),
        in_specs=[pl.BlockSpec(memory_space=pltpu.SMEM), wide(cur), wide(cur), kv(prv), kv(cur), kv(prv), kv(cur),
                  tab(tcur), tab(tcur), tab(tcur), tab(tprv), tab(tprv), tab(tprv),
                  rows, pl.BlockSpec((HALO, POOL_WIDTH), halo), rows, _const((4, POOL_GC, POOL_GC)),
                  _const((1, POOL_WIDTH))],
        out_specs=[wide(cur), kv(prv), kv(prv), _const((8, LANES)), rows, _const((4, POOL_GC, POOL_GC)),
                   _const((1, POOL_WIDTH))],
        out_shape=[_sds((bl, seq, ATTN_WIDTH), BF16), _sds((bl, seq, KV_WIDTH), BF16),
                   _sds((bl, seq, KV_WIDTH), BF16), _sds((8, LANES), F32), _sds((T, POOL_WIDTH), BF16),
                   _sds((4, POOL_GC, POOL_GC), F32), _sds((1, POOL_WIDTH), F32)],
        scratch=[pltpu.VMEM((bl, BLOCK, KV_WIDTH), F32), pltpu.VMEM((bl, BLOCK, KV_WIDTH), F32)],
        args=(sinks, *_by_example(bl, q, do, k, k, v, v), *tabs, *tabs, dyp, dyp, diff, w_pool, pool_scale),
        sem=("arbitrary",), exchanges=exchanges)
    outs, rest = (res if exchanges else (res, None))
    outs = [outs[0].reshape(T, ATTN_WIDTH), outs[1].reshape(T, KV_WIDTH), outs[2].reshape(T, KV_WIDTH), *outs[3:]]
    return (outs, rest) if exchanges else outs


def _pool_bwd_tile(i, tp, nseq, dy_ref, nxt_ref, diff_ref, w_ref, s_ref, du_ref, dw_ref, ds_ref):
    last = (i % nseq) == nseq - 1
    nxt = jnp.where(last, 0.0, nxt_ref[...])
    ext = jnp.concatenate([dy_ref[...], nxt], axis=0) * s_ref[...]
    pos = (i % nseq) * tp + lax.broadcasted_iota(jnp.int32, (tp + HALO, 1), 0)
    for gi, w in enumerate(POOL_WINDOWS):
        sl = slice(POOL_GC * gi, POOL_GC * (gi + 1))
        wg = w_ref[gi].astype(BF16)
        dmx = ext[:, sl].astype(BF16)
        ddiff = _dot_nt(dmx, wg)
        s = ddiff * _inv_count(pos, w)
        sh = 1
        while sh < w:
            s = s + pltpu.roll(s, tp + HALO - sh, 0)
            sh *= 2
        du_ref[:, sl] = (s[:tp] - ddiff[:tp]).astype(BF16)
        dg = diff_ref[:, sl]
        dw_ref[gi] += _dot_tn(dg, dmx[:tp])
        ds_ref[:, sl] += jnp.sum(dy_ref[:, sl] * _dot(dg, wg), axis=0, keepdims=True)


_PARTS = ((0, C_Q), (C_Q, C_K), (C_K, C_V), (C_V, C_G), (C_G, IN_WIDTH))


def _inproj_bwd(parts, x2, dx1, w_in_t, g1, exchanges=()):
    T = x2.shape[0]
    tm = min(TM, T)

    def body(du_ref, dq_ref, dk_ref, dv_ref, dgt_ref, x_ref, dx1_ref, w_ref, g_ref, gx_ref, dg_ref):
        @pl.when(pl.program_id(0) == 0)
        def _():
            dg_ref[...] = jnp.zeros_like(dg_ref)

        dh = jnp.zeros((tm, D_MODEL), F32)
        for (lo, hi), p_ref in zip(_PARTS, (du_ref, dq_ref, dk_ref, dv_ref, dgt_ref)):
            dh = dh + _dot(p_ref[...], w_ref[lo:hi, :])
        dx, dg = _norm_bwd(x_ref[...], g_ref[...], dh)
        gx_ref[...] = dx1_ref[...] + dx
        dg_ref[...] += dg

    return _call(
        body, name="inproj_bwd", grid=(T // tm,),
        in_specs=[_rows(tm, hi - lo) for lo, hi in _PARTS]
        + [_rows(tm, D_MODEL), _rows(tm, D_MODEL), _const((IN_WIDTH, D_MODEL)), _const((1, D_MODEL))],
        out_specs=[_rows(tm, D_MODEL), _const((1, D_MODEL))],
        out_shape=[_sds((T, D_MODEL), F32), _sds((1, D_MODEL), F32)],
        args=(*parts, x2, dx1, w_in_t, g1), sem=("arbitrary",), exchanges=exchanges)


def _dw_in(h, parts, exchanges=()):
    T = h.shape[0]
    tk = min(2 * TM, T)

    def body(h_ref, du_ref, dq_ref, dk_ref, dv_ref, dgt_ref, o_ref, db_ref):
        @pl.when(pl.program_id(0) == 0)
        def _():
            o_ref[...] = jnp.zeros_like(o_ref)
            db_ref[...] = jnp.zeros_like(db_ref)

        hh = h_ref[...]
        ones = jnp.ones((8, tk), BF16)
        for (lo, hi), p_ref in zip(_PARTS, (du_ref, dq_ref, dk_ref, dv_ref, dgt_ref)):
            for at in range(0, hi - lo, D_MODEL):
                cols = min(D_MODEL, hi - lo - at)
                part = p_ref[:, at:at + cols]
                o_ref[lo + at:lo + at + cols, :] += _dot_tn(part, hh)
                db_ref[:, lo + at:lo + at + cols] += _dot(ones, part)[:1]

    return _call(
        body, name="dw_in", grid=(T // tk,),
        in_specs=[_rows(tk, D_MODEL)] + [_rows(tk, hi - lo) for lo, hi in _PARTS],
        out_specs=[_const((IN_WIDTH, D_MODEL)), _const((1, IN_WIDTH))],
        out_shape=[_sds((IN_WIDTH, D_MODEL), F32), _sds((1, IN_WIDTH), F32)],
        args=(h, *parts), sem=("arbitrary",), exchanges=exchanges)


def _row_tile(rows, cap=256, mult=16):
    best = None
    for t in range(mult, min(rows, cap) + 1, mult):
        if rows % t == 0:
            best = t
    if best is None:
        raise ValueError("no row tile for %d rows" % rows)
    return best


def _pair_sum(ids, full, got):
    _, r, c = full.shape
    hr = r // 2
    tr = _row_tile(hr)
    nblk = hr // tr

    def body(ids_ref, a_ref, b_ref, own_ref, sb_ref):
        s = a_ref[...] + b_ref[...]
        sb_ref[...] = s.astype(BF16)

        @pl.when(pl.program_id(1) == ids_ref[0])
        def _():
            own_ref[...] = s

    slab = pl.BlockSpec((None, tr, c), lambda i, j, ids_ref: (j, i, 0))
    return pl.pallas_call(
        body, name="pair_sum_%dx%d" % (r, c),
        grid_spec=pltpu.PrefetchScalarGridSpec(
            num_scalar_prefetch=1, grid=(nblk, N_CHIPS),
            in_specs=[pl.BlockSpec((None, tr, c), lambda i, j, ids_ref: (j, ids_ref[1] * nblk + i, 0)), slab],
            out_specs=[pl.BlockSpec((tr, c), lambda i, j, ids_ref: (i, 0)), slab]),
        out_shape=[_sds((hr, c), F32), _sds((N_CHIPS, hr, c), BF16)],
        compiler_params=_cp("parallel", "arbitrary"),
    )(ids, full, got)


def _pair_sum_small(ids, fulls, gots):
    n = len(fulls)
    dims = [(f.shape[1] // 2, f.shape[2]) for f in fulls]

    def body(ids_ref, *refs):
        ins, outs = refs[:2 * n], refs[2 * n:]
        for k in range(n):
            s = ins[2 * k][...] + ins[2 * k + 1][...]
            outs[2 * k + 1][...] = s.astype(BF16)

            @pl.when(pl.program_id(0) == ids_ref[0])
            def _(k=k, s=s):
                outs[2 * k][...] = s

    in_specs, out_specs, out_shape = [], [], []
    for hr, c in dims:
        slab = pl.BlockSpec((None, hr, c), lambda j, ids_ref: (j, 0, 0))
        in_specs += [pl.BlockSpec((None, hr, c), lambda j, ids_ref: (j, ids_ref[1], 0)), slab]
        out_specs += [pl.BlockSpec((hr, c), lambda j, ids_ref: (0, 0)), slab]
        out_shape += [_sds((hr, c), F32), _sds((N_CHIPS, hr, c), BF16)]
    res = pl.pallas_call(
        body, name="pair_sum_small",
        grid_spec=pltpu.PrefetchScalarGridSpec(num_scalar_prefetch=1, grid=(N_CHIPS,), in_specs=in_specs,
                                               out_specs=out_specs),
        out_shape=out_shape, compiler_params=_cp("arbitrary"),
    )(ids, *[a for pair in zip(fulls, gots) for a in pair])
    return [(res[2 * k], res[2 * k + 1]) for k in range(n)]


def _chip_sum_small(ids, owns, gots):
    n = len(owns)

    def body(ids_ref, *refs):
        ins, outs = refs[:2 * n], refs[2 * n:]
        for k in range(n):
            a, b = ins[2 * k], ins[2 * k + 1]
            outs[k][...] = ((a[...] + b[0].astype(F32)) + b[1].astype(F32)) + b[2].astype(F32)

    in_specs, out_specs, out_shape = [], [], []
    for own in owns:
        hr, c = own.shape
        in_specs += [pl.BlockSpec((hr, c), lambda i, ids_ref: (0, 0)),
                     pl.BlockSpec((3, hr, c), lambda i, ids_ref: (0, 0, 0))]
        out_specs.append(pl.BlockSpec((hr, c), lambda i, ids_ref: (ids_ref[1], 0)))
        out_shape.append(_sds((2 * hr, c), F32))
    return pl.pallas_call(
        body, name="chip_sum_small",
        grid_spec=pltpu.PrefetchScalarGridSpec(num_scalar_prefetch=1, grid=(1,), in_specs=in_specs,
                                               out_specs=out_specs),
        out_shape=out_shape, compiler_params=_cp("arbitrary"),
    )(ids, *[a for pair in zip(owns, gots) for a in pair])


def _chip_sum(ids, own, got):
    hr, c = own.shape
    tr = _row_tile(hr)
    nblk = hr // tr

    def body(ids_ref, a_ref, b_ref, o_ref):
        o_ref[...] = ((a_ref[...] + b_ref[0].astype(F32)) + b_ref[1].astype(F32)) + b_ref[2].astype(F32)

    return pl.pallas_call(
        body, name="chip_sum_%dx%d" % (hr, c),
        grid_spec=pltpu.PrefetchScalarGridSpec(
            num_scalar_prefetch=1, grid=(nblk,),
            in_specs=[pl.BlockSpec((tr, c), lambda i, ids_ref: (i, 0)),
                      pl.BlockSpec((3, tr, c), lambda i, ids_ref: (0, i, 0))],
            out_specs=pl.BlockSpec((tr, c), lambda i, ids_ref: (ids_ref[1] * nblk + i, 0))),
        out_shape=_sds((2 * hr, c), F32),
        compiler_params=_cp("parallel"),
    )(ids, own, got)


def _adamw_math(w, g, m, v):
    nm = ADAM_B1 * m + (1.0 - ADAM_B1) * g
    nv = ADAM_B2 * v + (1.0 - ADAM_B2) * (g * g)
    m_hat = nm / (1.0 - ADAM_B1 ** ADAM_STEP)
    v_hat = nv / (1.0 - ADAM_B2 ** ADAM_STEP)
    return -ADAM_LR * (m_hat / (jnp.sqrt(v_hat) + ADAM_EPS) + ADAM_WD * w), nm, nv


def _adamw(w, g, m, v):
    r, c = w.shape
    tr = _row_tile(r, cap=512, mult=8)

    def body(w_ref, g_ref, m_ref, v_ref, d_ref, nm_ref, nv_ref):
        d_ref[...], nm_ref[...], nv_ref[...] = _adamw_math(w_ref[...], g_ref[...], m_ref[...], v_ref[...])

    spec = _rows(tr, c)
    return pl.pallas_call(
        body, name="adamw_%dx%d" % (r, c), grid=(r // tr,),
        in_specs=[spec] * 4, out_specs=[spec] * 3, out_shape=[_sds((r, c), F32)] * 3,
        compiler_params=_cp("parallel"),
    )(w, g, m, v)


SC_TILES = 32
SC_LANES = 16
SC_ROWS = 8


def _adamw_sparse(w, g, m, v):
    r, c = w.shape
    rows = r // SC_TILES
    step = min(rows, SC_ROWS)

    def body(w_hbm, g_hbm, m_hbm, v_hbm, d_hbm, nm_hbm, nv_hbm, wb, gb, mb, vb):
        tile = lax.axis_index("sc_subcore") * 2 + lax.axis_index("sc_core")

        @pl.loop(0, rows, step=step)
        def _(r0):
            mine = pl.ds(tile * rows + r0, step)
            for src, dst in ((w_hbm, wb), (g_hbm, gb), (m_hbm, mb), (v_hbm, vb)):
                pltpu.sync_copy(src.at[mine], dst)

            @pl.loop(0, step)
            def _(row):
                @pl.loop(0, c, step=SC_LANES)
                def _(i):
                    at = (row, pl.ds(i, SC_LANES))
                    wb[at], mb[at], vb[at] = _adamw_math(wb[at], gb[at], mb[at], vb[at])

            for src, dst in ((wb, d_hbm), (mb, nm_hbm), (vb, nv_hbm)):
                pltpu.sync_copy(src, dst.at[mine])

    return pl.kernel(
        body, name="adamw_sparse_%dx%d" % (r, c), out_type=[_sds((r, c), F32)] * 3,
        mesh=plsc.VectorSubcoreMesh(core_axis_name="sc_core", subcore_axis_name="sc_subcore"),
        scratch_types=[pltpu.VMEM((step, c), F32)] * 4,
    )(w, g, m, v)


_SMALL_NAMES = ("w_pool", "b_in", "g_mix_pre", "g_mix_post", "g_mlp_pre", "g_mlp_post", "pool_scale", "attn_sinks")
B_ROWS = -(-IN_WIDTH // D_MODEL)


def _row_block(rows):
    rows = [jnp.pad(r.astype(F32), ((0, 0), (0, D_MODEL - r.shape[1]))) for r in rows]
    return jnp.pad(jnp.concatenate(rows, axis=0), ((0, 8 - len(rows)), (0, 0)))


def _early_block(dg2, dg3, dg4, dps, dsink, loss):
    tail = jnp.concatenate([jnp.pad(dsink.reshape(1, -1), ((0, 0), (0, LANES - dsink.size))),
                            jnp.pad(loss.reshape(1, 1), ((0, 0), (0, LANES - 1)))], axis=1)
    return _row_block([dg2, dg3, dg4, dps, tail])


def _late_block(db_in, dg1):
    b = jnp.pad(db_in, ((0, 0), (0, B_ROWS * D_MODEL - IN_WIDTH))).reshape(B_ROWS, D_MODEL)
    return _row_block([b[r:r + 1] for r in range(B_ROWS)] + [dg1])


def _small_update(gearly, gmat, glate, w, m, v):
    names = _SMALL_NAMES
    n = len(names)

    def total(ref, rows):
        acc = ref[0:rows, :]
        for d in range(1, N_DEV):
            acc = acc + ref[d * rows:(d + 1) * rows, :]
        return acc

    def body(*refs):
        early_ref, gmat_ref, late_ref = refs[:3]
        w_refs, m_refs, v_refs = refs[3:3 + n], refs[3 + n:3 + 2 * n], refs[3 + 2 * n:3 + 3 * n]
        outs = refs[3 + 3 * n:]
        loss_ref, g_refs, d_refs = outs[0], outs[1:1 + n], outs[1 + n:1 + 2 * n]
        nm_refs, nv_refs = outs[1 + 2 * n:1 + 3 * n], outs[1 + 3 * n:1 + 4 * n]
        early, late = total(early_ref, 8), total(late_ref, 8)
        loss_ref[...] = jnp.sum(early[4:5, LANES:2 * LANES], axis=1, keepdims=True)
        bias = jnp.concatenate([late[r:r + 1, :] for r in range(B_ROWS - 1)]
                               + [late[B_ROWS - 1:B_ROWS, :IN_WIDTH - (B_ROWS - 1) * D_MODEL]], axis=1)
        grad = dict(b_in=bias, g_mix_pre=late[B_ROWS:B_ROWS + 1, :], g_mix_post=early[0:1, :],
                    g_mlp_pre=early[1:2, :], g_mlp_post=early[2:3, :], pool_scale=early[3:4, :POOL_WIDTH],
                    attn_sinks=early[4:5, :N_Q_HEADS])
        for i, name in enumerate(names):
            g = total(gmat_ref, 4 * POOL_GC) if name == "w_pool" else grad[name]
            g_refs[i][...] = g
            d_refs[i][...], nm_refs[i][...], nv_refs[i][...] = _adamw_math(
                w_refs[i][...], g, m_refs[i][...], v_refs[i][...])

    shapes = [_sds(w[k].shape, F32) for k in names]
    res = pl.pallas_call(
        body, name="small_update", out_shape=[_sds((1, 1), F32)] + shapes * 4,
        compiler_params=pltpu.CompilerParams(vmem_limit_bytes=VMEM_MB * 1024 * 1024),
    )(gearly, gmat, glate, *[w[k] for k in names], *[m[k] for k in names], *[v[k] for k in names])
    loss = res[0]
    per = {k: tuple(res[1 + j * n + i] for j in range(4)) for i, k in enumerate(names)}
    return loss, per


_BIG = ("w_in", "w_branch_pool", "w_branch_attn", "w_out", "w_up", "w_down")
_ORDER = ("g_mix_pre", "w_in", "b_in", "w_pool", "pool_scale", "attn_sinks", "w_branch_pool", "w_branch_attn",
          "w_out", "g_mix_post", "g_mlp_pre", "w_up", "w_down", "g_mlp_post")


def _stack_rows(slab):
    return slab.reshape(-1, slab.shape[2])


def _step(x2, tgt, seq, shards, small, ids):
    tabs = _rope_tables(seq)
    g1, g2, g3, g4 = (small[n] for n in ("g_mix_pre", "g_mix_post", "g_mlp_pre", "g_mlp_post"))
    sinks = small["attn_sinks"].reshape(N_Q_HEADS)
    w_pool = small["w_pool"].reshape(4, POOL_GC, POOL_GC)
    pool_scale = small["pool_scale"]

    def whole(shard, slabs):
        return lax.dynamic_update_slice(slabs, shard[None], (ids[0], 0, 0))

    (up_a, up_b, down_a, down_b, *mix_shards), [[in_slab]] = _cast_shards(
        *(shards[n] for n in ("w_up", "w_down", "w_branch_pool", "w_branch_attn", "w_out")),
        exchanges=[_ex_gather([shards["w_in"]])])
    w_in = _stack_rows(whole(shards["w_in"], in_slab))
    (h, u, q, k, v, gate), [mix_slabs] = _inproj(
        x2, g1, w_in, small["b_in"], tabs, seq, exchanges=[_ex_gather(mix_shards)])
    w_bp, w_ba, out_slab = (whole(s, g) for s, g in zip(mix_shards, mix_slabs))
    w_out = _stack_rows(out_slab)
    (y_attn, diff, y_pool), [[got_a, got_b]] = _mixers_fwd(
        q, k, v, sinks, u, w_pool, pool_scale, seq, exchanges=[_ex_gather([up_a, up_b])])
    (merged, mix, x1, h2), [[got_c, got_d]] = _merge_out(
        y_pool, y_attn, gate, x2, w_bp, w_ba, w_out, g2, g3, exchanges=[_ex_gather([down_a, down_b])])
    w_up = (whole(up_a, got_a), whole(up_b, got_b))
    w_down = (whole(down_a, got_c), whole(down_b, got_d))
    act, dff, dup, dx1, dmix, loss_acc, dg4, dg3, dg2 = _mlp_core(h2, x1, mix, tgt, w_up, w_down, g4, g3, g2)

    dw_down = _dw("down", act, dff, 1024, 1024)[0].reshape(N_CHIPS, D_FF // N_CHIPS, D_MODEL)
    (dbp, dba, dgate, dyp, dya), [[got]] = _merge_bwd(
        dmix, gate, y_pool, y_attn, w_out, w_bp, w_ba, exchanges=[_ex_pair([dw_down])])
    ps_down = _pair_sum(ids, dw_down, got)
    (dw_up,), [[got]] = _dw("up", h2, dup, 1024, 1024, shard_cols=True, exchanges=[_ex_chip([ps_down[1]])])
    half_down = _chip_sum(ids, ps_down[0], got)
    (dw_out, dw_bp, dw_ba), [[got]] = _dw_mix(merged, dmix, y_pool, dbp, y_attn, dba, exchanges=[_ex_pair([dw_up])])
    ps_up = _pair_sum(ids, dw_up, got)
    dw_mix = [dw_out.reshape(N_CHIPS, D_MODEL // N_CHIPS, D_MODEL), dw_bp, dw_ba]
    (dq, dk, dv, dsink, du, dw_pool, dps), [[got], gots, [g_down]] = _mixers_bwd(
        q, k, v, dya, sinks, tabs, dyp, diff, w_pool, pool_scale, seq,
        exchanges=[_ex_chip([ps_up[1]]), _ex_pair(dw_mix), _ex_swap([half_down])])
    half_up = _chip_sum(ids, ps_up[0], got)
    ps_mix = _pair_sum_small(ids, dw_mix, gots)
    parts = (du, dq, dk, dv, dgate)
    early = _early_block(dg2, dg3, dg4, dps, dsink[:, 0], loss_acc[0, 0])
    mat = dw_pool.reshape(4 * POOL_GC, POOL_GC)
    (dw_in_t, db_in), [gots, [gearly, gmat], [g_up]] = _dw_in(
        h, parts, exchanges=[_ex_chip([p[1] for p in ps_mix]), _ex_allgather([early, mat]), _ex_swap([half_up])])
    half_mix = _chip_sum_small(ids, [p[0] for p in ps_mix], gots)
    dw_in = dw_in_t.reshape(N_CHIPS, IN_WIDTH // N_CHIPS, D_MODEL)
    g_mix, [got] = _alone("swap_mix_pair_in", _ex_swap(half_mix), _ex_pair([dw_in]))
    ps_in = _pair_sum(ids, dw_in, got)
    (gx, dg1), [[got]] = _inproj_bwd(parts, x2, dx1, w_in, g1, exchanges=[_ex_chip([ps_in[1]])])
    [g_in], [glate] = _alone("swap_in_allgather", _ex_swap([_chip_sum(ids, ps_in[0], got)]),
                             _ex_allgather([_late_block(db_in, dg1)]))

    grads = dict(w_in=g_in, w_branch_pool=g_mix[1], w_branch_attn=g_mix[2], w_out=g_mix[0], w_up=g_up, w_down=g_down)
    return (gearly, gmat, glate), gx, grads


def kernel(x, g_mix_pre, w_in, b_in, w_pool, pool_scale, attn_sinks, w_branch_pool, w_branch_attn, w_out, g_mix_post, g_mlp_pre, w_up, w_down, g_mlp_post, loss_target, m_g_mix_pre, m_w_in, m_b_in, m_w_pool, m_pool_scale, m_attn_sinks, m_w_branch_pool, m_w_branch_attn, m_w_out, m_g_mix_post, m_g_mlp_pre, m_w_up, m_w_down, m_g_mlp_post, v_g_mix_pre, v_w_in, v_b_in, v_w_pool, v_pool_scale, v_attn_sinks, v_w_branch_pool, v_w_branch_attn, v_w_out, v_g_mix_post, v_g_mlp_pre, v_w_up, v_w_down, v_g_mlp_post):
    weights = dict(g_mix_pre=g_mix_pre, w_in=w_in, b_in=b_in, w_pool=w_pool, pool_scale=pool_scale,
                   attn_sinks=attn_sinks, w_branch_pool=w_branch_pool, w_branch_attn=w_branch_attn, w_out=w_out,
                   g_mix_post=g_mix_post, g_mlp_pre=g_mlp_pre, w_up=w_up, w_down=w_down, g_mlp_post=g_mlp_post)
    mom1 = dict(g_mix_pre=m_g_mix_pre, w_in=m_w_in, b_in=m_b_in, w_pool=m_w_pool, pool_scale=m_pool_scale,
                attn_sinks=m_attn_sinks, w_branch_pool=m_w_branch_pool, w_branch_attn=m_w_branch_attn,
                w_out=m_w_out, g_mix_post=m_g_mix_post, g_mlp_pre=m_g_mlp_pre, w_up=m_w_up, w_down=m_w_down,
                g_mlp_post=m_g_mlp_post)
    mom2 = dict(g_mix_pre=v_g_mix_pre, w_in=v_w_in, b_in=v_b_in, w_pool=v_w_pool, pool_scale=v_pool_scale,
                attn_sinks=v_attn_sinks, w_branch_pool=v_w_branch_pool, w_branch_attn=v_w_branch_attn,
                w_out=v_w_out, g_mix_post=v_g_mix_post, g_mlp_pre=v_g_mlp_pre, w_up=v_w_up, w_down=v_w_down,
                g_mlp_post=v_g_mlp_post)
    b_loc, seq, _ = x.shape
    x2 = x.reshape(b_loc * seq, D_MODEL)
    tgt = loss_target.reshape(b_loc * seq, D_MODEL)
    ids = jnp.stack([2 * lax.axis_index("x") + lax.axis_index("y"), lax.axis_index("c")]).astype(jnp.int32)

    def flat(n, a):
        return a[0].T if n == "w_in" else a[0]

    def unflat(n, a):
        return (a.T if n == "w_in" else a)[None]

    shards = {n: flat(n, weights[n]).astype(BF16) if n == "w_in" else flat(n, weights[n]) for n in _BIG}
    small = {n: weights[n] for n in _ORDER if n not in _BIG}
    (gearly, gmat, glate), gx, grads = _step(x2, tgt, seq, shards, small, ids)

    def two_d(src):
        return {n: src[n].reshape(4 * POOL_GC, POOL_GC) if n == "w_pool" else src[n] for n in _SMALL_NAMES}

    loss, per = _small_update(gearly, gmat, glate, two_d(weights), two_d(mom1), two_d(mom2))
    delta, new_m, new_v = {}, {}, {}
    for n in _SMALL_NAMES:
        grads[n], delta[n], new_m[n], new_v[n] = (a.reshape(weights[n].shape) for a in per[n])
    for n in _BIG:
        update = _adamw if n == "w_in" else _adamw_sparse
        d, nm, nv = update(flat(n, weights[n]), grads[n], flat(n, mom1[n]), flat(n, mom2[n]))
        grads[n] = unflat(n, grads[n])
        delta[n], new_m[n], new_v[n] = unflat(n, d), unflat(n, nm), unflat(n, nv)

    return (loss[0, 0], gx.reshape(x.shape), *[grads[n] for n in _ORDER], *[delta[n] for n in _ORDER],
            *[new_m[n] for n in _ORDER], *[new_v[n] for n in _ORDER])
```

```python
import jax
import jax.numpy as jnp
from jax import lax
from jax.experimental import pallas as pl
from jax.experimental.pallas import tpu as pltpu
from jax.experimental.pallas import tpu_sc as plsc

F32 = jnp.float32
BF16 = jnp.bfloat16

D_MODEL = 1024
POOL_WINDOWS = (2, 4, 8, 16)
POOL_WIDTH = 512
POOL_GC = 128
HALO = 16
HEAD_DIM = 64
N_Q_HEADS = 8
ATTN_WIDTH = 512
KV_WIDTH = 128
BLOCK = 128
NEG_INF = -1e30
ROPE_THETA = 500000.0
ROT_DIM = 16
GATE_WIDTH = 2048
IN_WIDTH = 3328
D_FF = 4096
EPS = 1e-6
SCALE = HEAD_DIM ** -0.5
C_Q, C_K, C_V, C_G = 512, 1024, 1152, 1280

ADAM_LR, ADAM_B1, ADAM_B2, ADAM_EPS, ADAM_WD, ADAM_STEP = 0.001, 0.9, 0.999, 1e-08, 0.01, 10

N_CHIPS = 4
N_DEV = 8
LANES = 128
TM = 512
VMEM_MB = 56

MESH = pl.DeviceIdType.MESH
ANY = pl.BlockSpec(memory_space=pl.ANY)


def _cp(*sem, vmem=VMEM_MB):
    return pltpu.CompilerParams(dimension_semantics=sem, vmem_limit_bytes=vmem * 1024 * 1024)


def _rows(tile, cols):
    return pl.BlockSpec((tile, cols), lambda i: (i, 0))


def _const(shape):
    nd = len(shape)
    return pl.BlockSpec(shape, lambda i: (0,) * nd)


def _sds(shape, dtype):
    return jax.ShapeDtypeStruct(shape, dtype)


def _dot(a, b):
    return jnp.dot(a, b, preferred_element_type=F32)


def _dot_nt(a, b):
    return lax.dot_general(a, b, (((1,), (1,)), ((), ())), preferred_element_type=F32)


def _dot_tn(a, b):
    return lax.dot_general(a, b, (((0,), (0,)), ((), ())), preferred_element_type=F32)


def _rms(x):
    return lax.rsqrt(jnp.mean(x * x, axis=-1, keepdims=True) + EPS)


def _norm_bwd(x, g, dout):
    r = _rms(x)
    n = x * r
    dn = dout * g
    dx = r * (dn - n * jnp.mean(dn * n, axis=-1, keepdims=True))
    return dx, jnp.sum(dout * n, axis=0, keepdims=True)


def _rot_fwd(t, c, a, bt):
    return t * c + pltpu.roll(t, LANES - 8, 1) * a + pltpu.roll(t, 8, 1) * bt


def _rot_bwd(d, c, a, bt):
    return d * c + pltpu.roll(d * a, 8, 1) + pltpu.roll(d * bt, LANES - 8, 1)


def _rope_tables(seq):
    pos = jnp.arange(seq, dtype=F32)
    inv_freq = ROPE_THETA ** (-jnp.arange(0, ROT_DIM, 2, dtype=F32) / ROT_DIM)
    ang = pos[:, None] * inv_freq[None, :]
    cos, sin = jnp.cos(ang), jnp.sin(ang)
    ones = jnp.ones((seq, HEAD_DIM - ROT_DIM), F32)
    zeros8 = jnp.zeros((seq, 8), F32)
    zrest = jnp.zeros((seq, HEAD_DIM - ROT_DIM), F32)
    c = jnp.concatenate([cos, cos, ones], axis=1)
    a = jnp.concatenate([-sin, zeros8, zrest], axis=1)
    bt = jnp.concatenate([zeros8, sin, zrest], axis=1)
    return tuple(jnp.tile(t, (1, 2)) for t in (c, a, bt))


class _Exchange:
    def __init__(self, inputs, out_shapes, sems, start, finish, aliases=None, middle=None):
        self.inputs, self.out_shapes, self.sems = list(inputs), list(out_shapes), list(sems)
        self.start, self.finish, self.aliases = start, finish, dict(aliases or {})
        self.middle = middle


def _call(body, *, name, grid, in_specs, out_specs, out_shape, args, scratch=(), sem=(), exchanges=()):
    in_specs, out_specs, out_shape, scratch = list(in_specs), list(out_specs), list(out_shape), list(scratch)
    if not exchanges:
        return pl.pallas_call(body, name=name, grid=grid, in_specs=in_specs, out_specs=out_specs,
                              out_shape=out_shape, scratch_shapes=scratch, compiler_params=_cp(*sem))(*args)
    n_in, n_out, n_scr = len(in_specs), len(out_specs), len(scratch)
    x_in = [a for ex in exchanges for a in ex.inputs]
    x_out = [s for ex in exchanges for s in ex.out_shapes]
    x_sem = [s for ex in exchanges for s in ex.sems]
    aliases, i_off, o_off = {}, n_in, n_out
    for ex in exchanges:
        for i, o in ex.aliases.items():
            aliases[i_off + i] = o_off + o
        i_off += len(ex.inputs)
        o_off += len(ex.out_shapes)

    def split(flat):
        out, pos = [], 0
        for ex, n in zip(exchanges, flat[1]):
            out.append(flat[0][pos:pos + n])
            pos += n
        return out

    def carrier(*refs):
        pos = 0
        groups = []
        for n in (n_in, len(x_in), n_out, len(x_out), n_scr, len(x_sem)):
            groups.append(refs[pos:pos + n])
            pos += n
        ins, xin, outs, xout, scr, xsem = groups
        xin = split((xin, [len(ex.inputs) for ex in exchanges]))
        xout = split((xout, [len(ex.out_shapes) for ex in exchanges]))
        xsem = split((xsem, [len(ex.sems) for ex in exchanges]))
        first = pl.program_id(0) == 0
        last = pl.program_id(0) == grid[0] - 1
        for d in range(1, len(grid)):
            first = jnp.logical_and(first, pl.program_id(d) == 0)
            last = jnp.logical_and(last, pl.program_id(d) == grid[d] - 1)

        @pl.when(first)
        def _():
            for ex, i, o, s in zip(exchanges, xin, xout, xsem):
                ex.start(i, o, s)

        if any(ex.middle for ex in exchanges):
            half = pl.program_id(0) == 5 * grid[0] // 8
            for d in range(1, len(grid)):
                half = jnp.logical_and(half, pl.program_id(d) == 0)

            @pl.when(half)
            def _():
                for ex, i, o, s in zip(exchanges, xin, xout, xsem):
                    if ex.middle:
                        ex.middle(i, o, s)

        body(*ins, *outs, *scr)

        @pl.when(last)
        def _():
            for ex, i, o, s in zip(exchanges, xin, xout, xsem):
                ex.finish(i, o, s)

    res = pl.pallas_call(
        carrier, name=name, grid=grid, in_specs=in_specs + [ANY] * len(x_in),
        out_specs=out_specs + [ANY] * len(x_out), out_shape=out_shape + x_out,
        scratch_shapes=scratch + x_sem, input_output_aliases=aliases,
        compiler_params=_cp(*(["arbitrary"] * len(grid))),
    )(*args, *x_in)
    return res[:n_out], split((res[n_out:], [len(ex.out_shapes) for ex in exchanges]))


def _alone(name, *exchanges):
    n_in = [len(ex.inputs) for ex in exchanges]
    n_out = [len(ex.out_shapes) for ex in exchanges]
    n_sem = [len(ex.sems) for ex in exchanges]
    aliases, i_off, o_off = {}, 0, 0
    for ex in exchanges:
        for i, o in ex.aliases.items():
            aliases[i_off + i] = o_off + o
        i_off += len(ex.inputs)
        o_off += len(ex.out_shapes)

    def split(flat, counts):
        out, pos = [], 0
        for n in counts:
            out.append(flat[pos:pos + n])
            pos += n
        return out

    def body(*refs):
        ins, outs, sems = split(refs, [sum(n_in), sum(n_out), sum(n_sem)])
        groups = list(zip(exchanges, split(ins, n_in), split(outs, n_out), split(sems, n_sem)))
        for ex, i, o, s in groups:
            ex.start(i, o, s)
        for ex, i, o, s in groups:
            if ex.middle:
                ex.middle(i, o, s)
        for ex, i, o, s in groups:
            ex.finish(i, o, s)

    res = pl.pallas_call(
        body, name=name, in_specs=[ANY] * sum(n_in), out_specs=[ANY] * sum(n_out),
        out_shape=[s for ex in exchanges for s in ex.out_shapes],
        scratch_shapes=[s for ex in exchanges for s in ex.sems], input_output_aliases=aliases,
    )(*[a for ex in exchanges for a in ex.inputs])
    return split(res, n_out)


def _place():
    x, y, c = lax.axis_index("x"), lax.axis_index("y"), lax.axis_index("c")
    chips = [(1 - x, y), (x, 1 - y), (1 - x, 1 - y)]
    return x, y, c, chips


def _remote(src, dst, send, recv, to):
    return pltpu.make_async_remote_copy(src_ref=src, dst_ref=dst, send_sem=send, recv_sem=recv,
                                        device_id=to, device_id_type=MESH)


def _ex_gather(shards):
    nw = len(shards)
    hrs = [s.shape[0] // 2 for s in shards]

    def copies(ins, outs, sems):
        s0, r0, s1, r1, s2, r2, fs, fr = sems
        x, y, c, _ = _place()
        me, xn, yn, dg = (x, y), (1 - x, y), (x, 1 - y), (1 - x, 1 - y)
        nbr = (xn, yn)
        sibling = (x, y, 1 - c)

        def piece(w, chip, core, part=None):
            hr = hrs[w]
            rows = pl.ds(core * hr, hr) if part is None else pl.ds(core * hr + part * (hr // 2), hr // 2)
            return outs[w].at[2 * chip[0] + chip[1], rows]

        def first(w, k, lead):
            part = k if lead else 1 - k
            send, recv = (s0, r0) if lead else (s1, r1)
            rows = pl.ds(c * hrs[w] + part * (hrs[w] // 2), hrs[w] // 2)
            return _remote(ins[w].at[rows], piece(w, me, c, part), send.at[w, k], recv.at[w, k], (*nbr[k], c))

        def landed(w, k, lead):
            part = k if lead else 1 - k
            send, recv = (s0, r0) if lead else (s1, r1)
            return _remote(piece(w, nbr[k], c, part), piece(w, nbr[k], c, part), send.at[w, k], recv.at[w, k],
                           (*nbr[k], c))

        def onward(w, k):
            return _remote(piece(w, nbr[k], c, k), piece(w, nbr[k], c, k), s2.at[w, k], r2.at[w, k],
                           (*nbr[1 - k], c))

        def arrived(w, k):
            return _remote(piece(w, dg, c, k), piece(w, dg, c, k), s2.at[w, k], r2.at[w, k], (*nbr[1 - k], c))

        def passed(w, j):
            chip = (xn, yn, dg)[j]
            return _remote(piece(w, chip, c), piece(w, chip, c), fs.at[w, j], fr.at[w, j], sibling)

        def handed(w, j):
            chip = (xn, yn, dg)[j]
            return _remote(piece(w, chip, 1 - c), piece(w, chip, 1 - c), fs.at[w, j], fr.at[w, j], sibling)

        return first, landed, onward, arrived, passed, handed

    def start(ins, outs, sems):
        first = copies(ins, outs, sems)[0]
        for lead in (True, False):
            for w in range(nw):
                for k in range(2):
                    first(w, k, lead).start()

    def middle(ins, outs, sems):
        _, landed, onward, _, passed, _ = copies(ins, outs, sems)
        for w in range(nw):
            for k in range(2):
                landed(w, k, True).wait_recv()
                onward(w, k).start()
        for w in range(nw):
            for k in range(2):
                landed(w, k, False).wait_recv()
                passed(w, k).start()

    def finish(ins, outs, sems):
        first, _, onward, arrived, passed, handed = copies(ins, outs, sems)
        for w in range(nw):
            for k in range(2):
                arrived(w, k).wait_recv()
            passed(w, 2).start()
        for w in range(nw):
            for j in range(3):
                handed(w, j).wait_recv()
        for w in range(nw):
            for k in range(2):
                first(w, k, True).wait_send()
                first(w, k, False).wait_send()
                onward(w, k).wait_send()
            for j in range(3):
                passed(w, j).wait_send()

    return _Exchange(shards, [_sds((N_CHIPS,) + s.shape, s.dtype) for s in shards],
                     [pltpu.SemaphoreType.DMA((nw, 2))] * 6 + [pltpu.SemaphoreType.DMA((nw, 3))] * 2,
                     start, finish, middle=middle)


def _ex_pair(grads):
    nw = len(grads)

    def copies(ins, outs, sems):
        x, y, c, _ = _place()
        out = []
        for w in range(nw):
            hr = grads[w].shape[1] // 2
            out.append(_remote(ins[w].at[:, pl.ds((1 - c) * hr, hr)], outs[w], sems[0].at[w], sems[1].at[w],
                               (x, y, 1 - c)))
        return out

    def start(ins, outs, sems):
        for cp in copies(ins, outs, sems):
            cp.start()

    def finish(ins, outs, sems):
        for cp in copies(ins, outs, sems):
            cp.wait()

    return _Exchange(grads, [_sds((N_CHIPS, g.shape[1] // 2, g.shape[2]), F32) for g in grads],
                     [pltpu.SemaphoreType.DMA((nw,))] * 2, start, finish)


def _ex_chip(pieces):
    nw = len(pieces)

    def copies(ins, outs, sems):
        x, y, c, chips = _place()
        return [_remote(ins[w].at[2 * cx + cy], outs[w].at[k], sems[0].at[w, k], sems[1].at[w, k], (cx, cy, c))
                for w in range(nw) for k, (cx, cy) in enumerate(chips)]

    def start(ins, outs, sems):
        for cp in copies(ins, outs, sems):
            cp.start()

    def finish(ins, outs, sems):
        for cp in copies(ins, outs, sems):
            cp.wait()

    return _Exchange(pieces, [_sds((3,) + p.shape[1:], BF16) for p in pieces],
                     [pltpu.SemaphoreType.DMA((nw, 3))] * 2, start, finish)


def _ex_swap(fulls):
    nw = len(fulls)

    def start(ins, outs, sems):
        x, y, c, _ = _place()
        for w in range(nw):
            hr = fulls[w].shape[0] // 2
            mine = pl.ds(c * hr, hr)
            _remote(ins[w].at[mine], outs[w].at[mine], sems[0].at[w], sems[1].at[w], (x, y, 1 - c)).start()

    def finish(ins, outs, sems):
        x, y, c, _ = _place()
        for w in range(nw):
            hr = fulls[w].shape[0] // 2
            mine, theirs = pl.ds(c * hr, hr), pl.ds((1 - c) * hr, hr)
            _remote(ins[w].at[mine], outs[w].at[mine], sems[0].at[w], sems[1].at[w], (x, y, 1 - c)).wait_send()
            _remote(ins[w].at[theirs], outs[w].at[theirs], sems[0].at[w], sems[1].at[w], (x, y, 1 - c)).wait_recv()

    return _Exchange(fulls, [_sds(f.shape, F32) for f in fulls], [pltpu.SemaphoreType.DMA((nw,))] * 2,
                     start, finish, aliases={w: w for w in range(nw)})


def _ex_allgather(blocks):
    nb = len(blocks)

    def copies(ins, outs, sems):
        send, recv, lsem = sems
        x, y, c, chips = _place()
        me, sibling = (x, y, c), (x, y, 1 - c)

        def rows(b, px, py, pc):
            m_per = blocks[b].shape[0]
            return outs[b].at[pl.ds((4 * px + 2 * py + pc) * m_per, m_per), :]

        def copy(b, k, blk, to, src=None):
            return _remote(rows(b, *blk) if src is None else src, rows(b, *blk), send.at[b, k], recv.at[b, k], to)

        def mine(b):
            return pltpu.make_async_copy(ins[b], rows(b, *me), lsem.at[b])

        def first(b, k):
            return copy(b, k, me, sibling if k == 0 else (*chips[k - 1], c), src=ins[b])

        def passed(b, j):
            return copy(b, 4 + j, (*chips[j], c), sibling)

        def landed(b, j):
            return copy(b, 1 + j, (*chips[j], c), me)

        def handed(b, k):
            return copy(b, 0, sibling, me) if k == 0 else copy(b, 3 + k, (*chips[k - 1], 1 - c), me)

        return mine, first, passed, landed, handed

    def start(ins, outs, sems):
        mine, first, _, _, _ = copies(ins, outs, sems)
        for b in range(nb):
            mine(b).start()
            for k in range(4):
                first(b, k).start()

    def finish(ins, outs, sems):
        mine, first, passed, landed, handed = copies(ins, outs, sems)
        sent = []
        for b in range(nb):
            for j in range(3):
                landed(b, j).wait_recv()
                cp = passed(b, j)
                cp.start()
                sent.append(cp)
        for b in range(nb):
            for k in range(4):
                handed(b, k).wait_recv()
            for k in range(4):
                first(b, k).wait_send()
        for cp in sent:
            cp.wait_send()
        for b in range(nb):
            mine(b).wait()

    return _Exchange(blocks, [_sds((N_DEV * b.shape[0], b.shape[1]), F32) for b in blocks],
                     [pltpu.SemaphoreType.DMA((nb, 7)), pltpu.SemaphoreType.DMA((nb, 7)), pltpu.SemaphoreType.DMA((nb,))],
                     start, finish)


def _cast_shards(w_up, w_down, w_bp, w_ba, w_out, exchanges=()):
    r, c = w_up.shape
    tr = HALF // 2
    steps = r // tr

    def body(up_ref, down_ref, bp_ref, ba_ref, out_ref, ua_ref, ub_ref, da_ref, db_ref, bpo_ref, bao_ref, outo_ref):
        i = pl.program_id(0)

        @pl.when(i == 0)
        def _():
            for src, dst in ((bp_ref, bpo_ref), (ba_ref, bao_ref), (out_ref, outo_ref)):
                dst[...] = src[...].astype(BF16)

        @pl.when(i < steps // 2)
        def _():
            ua_ref[...] = up_ref[...].astype(BF16)
            da_ref[...] = down_ref[...].astype(BF16)

        @pl.when(i >= steps // 2)
        def _():
            ub_ref[...] = up_ref[...].astype(BF16)
            db_ref[...] = down_ref[...].astype(BF16)

    rows = _rows(tr, c)
    first = pl.BlockSpec((tr, c), lambda i: (jnp.minimum(i, steps // 2 - 1), 0))
    second = pl.BlockSpec((tr, c), lambda i: (jnp.maximum(i - steps // 2, 0), 0))
    half = _sds((HALF, c), BF16)
    return _call(
        body, name="cast_shards", grid=(steps,),
        in_specs=[rows, rows, _const(w_bp.shape), _const(w_ba.shape), _const(w_out.shape)],
        out_specs=[first, second, first, second, _const(w_bp.shape), _const(w_ba.shape), _const(w_out.shape)],
        out_shape=[half, half, half, half, _sds(w_bp.shape, BF16), _sds(w_ba.shape, BF16), _sds(w_out.shape, BF16)],
        args=(w_up, w_down, w_bp, w_ba, w_out), sem=("arbitrary",), exchanges=exchanges)


def _inproj(x2, g1, w_in_t, b_in, tabs, seq, exchanges=()):
    T = x2.shape[0]
    tm = min(TM, seq)
    nseq = seq // tm

    def body(x_ref, g_ref, w_ref, b_ref, c_ref, a_ref, bt_ref, h_ref, u_ref, q_ref, k_ref, v_ref, gate_ref):
        x = x_ref[...]
        h = (x * _rms(x) * g_ref[...]).astype(BF16)
        h_ref[...] = h

        def proj(lo, hi):
            return _dot_nt(h, w_ref[lo:hi, :]) + b_ref[:, lo:hi]

        c, a, bt = c_ref[...], a_ref[...], bt_ref[...]
        u_ref[...] = proj(0, C_Q)
        q = proj(C_Q, C_K)
        for p in range(4):
            sl = slice(LANES * p, LANES * (p + 1))
            q_ref[:, sl] = (_rot_fwd(q[:, sl], c, a, bt) * SCALE).astype(BF16)
        kv = proj(C_K, C_G)
        k_ref[...] = _rot_fwd(kv[:, :KV_WIDTH], c, a, bt).astype(BF16)
        v_ref[...] = kv[:, KV_WIDTH:].astype(BF16)
        for j in range(2):
            lo = C_G + D_MODEL * j
            gate_ref[:, D_MODEL * j:D_MODEL * (j + 1)] = jax.nn.sigmoid(proj(lo, lo + D_MODEL)).astype(BF16)

    tab = pl.BlockSpec((tm, LANES), lambda i: (i % nseq, 0))
    return _call(
        body, name="inproj", grid=(T // tm,),
        in_specs=[_rows(tm, D_MODEL), _const((1, D_MODEL)), _const((IN_WIDTH, D_MODEL)), _const((1, IN_WIDTH)),
                  tab, tab, tab],
        out_specs=[_rows(tm, D_MODEL), _rows(tm, POOL_WIDTH), _rows(tm, ATTN_WIDTH), _rows(tm, KV_WIDTH),
                   _rows(tm, KV_WIDTH), _rows(tm, GATE_WIDTH)],
        out_shape=[_sds((T, D_MODEL), BF16), _sds((T, POOL_WIDTH), F32), _sds((T, ATTN_WIDTH), BF16),
                   _sds((T, KV_WIDTH), BF16), _sds((T, KV_WIDTH), BF16), _sds((T, GATE_WIDTH), BF16)],
        args=(x2, g1, w_in_t, b_in, *tabs), sem=("parallel",), exchanges=exchanges)


def _inv_count(pos, w):
    return 1.0 / jnp.minimum(pos + 1, w).astype(F32)


def _pool_tile(i, tp, nseq, u_ref, prev_ref, w_ref, s_ref, diff_ref, y_ref):
    first = (i % nseq) == 0
    prev = jnp.where(first, 0.0, prev_ref[...])
    ext = jnp.concatenate([prev, u_ref[...]], axis=0)
    pos = (i % nseq) * tp + lax.broadcasted_iota(jnp.int32, (tp, 1), 0)
    for gi, w in enumerate(POOL_WINDOWS):
        sl = slice(POOL_GC * gi, POOL_GC * (gi + 1))
        xg = ext[:, sl]
        s = xg
        sh = 1
        while sh < w:
            s = s + pltpu.roll(s, sh, 0)
            sh *= 2
        pooled = s[HALO:] * _inv_count(pos, w)
        diff = (pooled - xg[HALO:]).astype(BF16)
        diff_ref[:, sl] = diff
        mixed = _dot(diff, w_ref[gi].astype(BF16))
        y_ref[:, sl] = (mixed * s_ref[:, sl]).astype(BF16)


def _pool_specs(tp):
    per = tp // HALO
    return [_rows(tp, POOL_WIDTH), pl.BlockSpec((HALO, POOL_WIDTH), lambda i: (jnp.maximum(i * per - 1, 0), 0)),
            _const((4, POOL_GC, POOL_GC)), _const((1, POOL_WIDTH))]


GROUP = 4
GROWS = GROUP * BLOCK


def _attn_masks(n):
    qi = lax.broadcasted_iota(jnp.int32, (GROWS, 2 * BLOCK), 0) % BLOCK
    kj = lax.broadcasted_iota(jnp.int32, (GROWS, 2 * BLOCK), 1)
    rel = qi + BLOCK - kj
    valid = (rel >= 0) & (rel < BLOCK) & (kj >= jnp.where(n > 0, 0, BLOCK))
    lo = lax.broadcasted_iota(jnp.int32, (BLOCK, LANES), 1) < HEAD_DIM
    return valid, lo


def _by_example(bl, *arrays):
    return [a.reshape(bl, a.shape[0] // bl, a.shape[1]) for a in arrays]


def _stack_heads(ref, h, lo):
    keep = lo if h == 0 else jnp.logical_not(lo)
    pieces = []
    for p in (2 * h, 2 * h + 1):
        xp = ref[:, LANES * p:LANES * (p + 1)].astype(F32)
        for e in range(2):
            t = xp if e == h else pltpu.roll(xp, HEAD_DIM, 1)
            pieces.append(jnp.where(keep, t, 0.0).astype(BF16))
    return jnp.concatenate(pieces, axis=0)


def _unstack_heads(stacked, h, lo):
    pairs = []
    for j in range(2):
        parts = []
        for e in range(2):
            t = stacked[BLOCK * (2 * j + e):BLOCK * (2 * j + e + 1)]
            parts.append(t if e == h else pltpu.roll(t, HEAD_DIM, 1))
        pairs.append(jnp.where(lo, parts[0], parts[1]))
    return pairs


def _sink_rows(sink_ref, h):
    head = lax.broadcasted_iota(jnp.int32, (GROWS, 1), 0) // BLOCK
    col = jnp.zeros((GROWS, 1), F32) + sink_ref[GROUP * h]
    for g in range(1, GROUP):
        col = jnp.where(head == g, sink_ref[GROUP * h + g], col)
    return col


def _group_probs(qs, kk, valid, sink):
    s = jnp.where(valid, _dot_nt(qs, kk), NEG_INF)
    m = jnp.maximum(jnp.max(s, axis=1, keepdims=True), sink)
    ex = jnp.exp(s - m)
    es = jnp.exp(sink - m)
    inv = 1.0 / (jnp.sum(ex, axis=1, keepdims=True) + es)
    return ex * inv, es * inv


def _mixers_fwd(q, k, v, sinks, u, w_pool, pool_scale, seq, exchanges=()):
    T = q.shape[0]
    nb = seq // BLOCK
    bl = T // seq
    tp = T // nb
    nseq = seq // tp

    def body(sink_ref, q_ref, kp_ref, kc_ref, vp_ref, vc_ref, u_ref, prev_ref, w_ref, s_ref, o_ref, diff_ref, y_ref):
        n = pl.program_id(0)
        valid, lo = _attn_masks(n)
        for b in range(bl):
            kk = jnp.concatenate([kp_ref[b], kc_ref[b]], axis=0)
            vv = jnp.concatenate([vp_ref[b], vc_ref[b]], axis=0)
            for h in range(2):
                qs = _stack_heads(q_ref.at[b], h, lo)
                pr, _ = _group_probs(qs, kk, valid, _sink_rows(sink_ref, h))
                o = _dot(pr.astype(BF16), vv)
                for j, pair in enumerate(_unstack_heads(o, h, lo)):
                    p = 2 * h + j
                    o_ref[b, :, LANES * p:LANES * (p + 1)] = pair.astype(BF16)
        _pool_tile(n, tp, nseq, u_ref, prev_ref, w_ref, s_ref, diff_ref, y_ref)

    cur = lambda n: (0, n, 0)
    prv = lambda n: (0, jnp.maximum(n - 1, 0), 0)
    kv = lambda m: pl.BlockSpec((bl, BLOCK, KV_WIDTH), m)
    res = _call(
        body, name="mixers_fwd", grid=(nb,),
        in_specs=[pl.BlockSpec(memory_space=pltpu.SMEM), pl.BlockSpec((bl, BLOCK, ATTN_WIDTH), cur),
                  kv(prv), kv(cur), kv(prv), kv(cur)] + _pool_specs(tp),
        out_specs=[pl.BlockSpec((bl, BLOCK, ATTN_WIDTH), cur), _rows(tp, POOL_WIDTH), _rows(tp, POOL_WIDTH)],
        out_shape=[_sds((bl, seq, ATTN_WIDTH), BF16), _sds((T, POOL_WIDTH), BF16), _sds((T, POOL_WIDTH), BF16)],
        args=(sinks, *_by_example(bl, q, k, k, v, v), u, u, w_pool, pool_scale), sem=("parallel",),
        exchanges=exchanges)
    outs, rest = res if exchanges else (res, None)
    return [outs[0].reshape(T, ATTN_WIDTH), outs[1], outs[2]], rest


def _branch(y, w_ref):
    return jnp.concatenate([_dot(y, w_ref[j]) for j in range(N_CHIPS)], axis=1)


def _merge_out(y_pool, y_attn, gate, x2, w_bp, w_ba, w_out, g2, g3, exchanges=()):
    T = x2.shape[0]
    tm = min(TM, T)

    def body(yp_ref, ya_ref, gate_ref, x_ref, wbp_ref, wba_ref, wo_ref, g2_ref, g3_ref,
             mg_ref, mix_ref, x1_ref, h2_ref):
        bp, ba = _branch(yp_ref[...], wbp_ref), _branch(ya_ref[...], wba_ref)
        merged = (gate_ref[:, :D_MODEL].astype(F32) * bp + gate_ref[:, D_MODEL:].astype(F32) * ba).astype(BF16)
        mg_ref[...] = merged
        mix = _dot(merged, wo_ref[...])
        mix_ref[...] = mix
        x1 = x_ref[...] + mix * _rms(mix) * g2_ref[...]
        x1_ref[...] = x1
        h2_ref[...] = (x1 * _rms(x1) * g3_ref[...]).astype(BF16)

    return _call(
        body, name="merge_out", grid=(T // tm,),
        in_specs=[_rows(tm, POOL_WIDTH), _rows(tm, ATTN_WIDTH), _rows(tm, GATE_WIDTH), _rows(tm, D_MODEL),
                  _const(w_bp.shape), _const(w_ba.shape), _const((D_MODEL, D_MODEL)),
                  _const((1, D_MODEL)), _const((1, D_MODEL))],
        out_specs=[_rows(tm, D_MODEL)] * 4,
        out_shape=[_sds((T, D_MODEL), BF16), _sds((T, D_MODEL), F32), _sds((T, D_MODEL), F32),
                   _sds((T, D_MODEL), BF16)],
        args=(y_pool, y_attn, gate, x2, w_bp, w_ba, w_out, g2, g3), sem=("parallel",), exchanges=exchanges)


HALF = D_MODEL // 2
TM_MLP = 256


def _mlp_core(h2, x1, mix, tgt, w_up, w_down, g4, g3, g2):
    T = h2.shape[0]
    tm = min(TM_MLP, T)

    def body(h_ref, x1_ref, mix_ref, t_ref, g_ref, g3_ref, g2_ref, ua_hbm, ub_hbm, da_hbm, db_hbm,
             act_ref, dff_ref, dup_ref, dx1_ref, dmix_ref, loss_ref, dg_ref, dg3_ref, dg2_ref,
             wu, wd, relu_scr, sems):
        def weight_copy(i):
            src, dst = ((ua_hbm, wu.at[:, :HALF]), (ub_hbm, wu.at[:, HALF:]),
                        (da_hbm, wd.at[:, :HALF]), (db_hbm, wd.at[:, HALF:]))[i]
            return pltpu.make_async_copy(src, dst, sems.at[i])

        @pl.when(pl.program_id(0) == 0)
        def _():
            for i in range(4):
                weight_copy(i).start()
            loss_ref[...] = jnp.zeros_like(loss_ref)
            for ref in (dg_ref, dg3_ref, dg2_ref):
                ref[...] = jnp.zeros_like(ref)
            weight_copy(0).wait()
            weight_copy(1).wait()

        h = h_ref[...]
        ff = None
        for j in range(N_CHIPS):
            lo = D_MODEL * j
            relu = jnp.maximum(_dot(h, wu[j]), 0.0)
            if j == 0:
                @pl.when(pl.program_id(0) == 0)
                def _():
                    weight_copy(2).wait()
                    weight_copy(3).wait()
            relu_scr[:, lo:lo + D_MODEL] = relu
            act = jnp.square(relu).astype(BF16)
            act_ref[:, lo:lo + D_MODEL] = act
            t = _dot(act, wd[j])
            ff = t if ff is None else ff + t
        g = g_ref[...]
        x1 = x1_ref[...]
        err = x1 + ff * _rms(ff) * g - t_ref[...]
        loss_ref[...] += jnp.sum(err * err) * (0.5 / D_MODEL)
        dy = err * (1.0 / D_MODEL)
        dff, dg = _norm_bwd(ff, g, dy)
        dg_ref[...] += dg
        dff = dff.astype(BF16)
        dff_ref[...] = dff
        dh2 = None
        for j in range(N_CHIPS):
            lo = D_MODEL * j
            dup = (_dot_nt(dff, wd[j]) * (2.0 * relu_scr[:, lo:lo + D_MODEL])).astype(BF16)
            dup_ref[:, lo:lo + D_MODEL] = dup
            t = _dot_nt(dup, wu[j])
            dh2 = t if dh2 is None else dh2 + t
        dx, dg3 = _norm_bwd(x1, g3_ref[...], dh2)
        dx1 = dy + dx
        dx1_ref[...] = dx1
        dg3_ref[...] += dg3
        dmix, dg2 = _norm_bwd(mix_ref[...], g2_ref[...], dx1)
        dmix_ref[...] = dmix.astype(BF16)
        dg2_ref[...] += dg2

    slabs = pltpu.VMEM((N_CHIPS, D_MODEL, D_MODEL), BF16)
    gain = _const((1, D_MODEL))
    return pl.pallas_call(
        body, name="mlp_core", grid=(T // tm,),
        in_specs=[_rows(tm, D_MODEL)] * 4 + [gain] * 3 + [ANY] * 4,
        out_specs=[_rows(tm, D_FF), _rows(tm, D_MODEL), _rows(tm, D_FF), _rows(tm, D_MODEL), _rows(tm, D_MODEL),
                   _const((8, LANES)), gain, gain, gain],
        out_shape=[_sds((T, D_FF), BF16), _sds((T, D_MODEL), BF16), _sds((T, D_FF), BF16), _sds((T, D_MODEL), F32),
                   _sds((T, D_MODEL), BF16), _sds((8, LANES), F32)] + [_sds((1, D_MODEL), F32)] * 3,
        scratch_shapes=[slabs] * 2 + [pltpu.VMEM((tm, D_FF), F32), pltpu.SemaphoreType.DMA((4,))],
        compiler_params=_cp("arbitrary"),
    )(h2, x1, mix, tgt, g4, g3, g2, *w_up, *w_down)


def _dw(tag, a, g, ta, tn, shard_cols=False, exchanges=()):
    T, ka = a.shape
    n = g.shape[1]
    tk = min(2 * TM, T)
    nk = T // tk

    def body(a_ref, g_ref, o_ref):
        @pl.when(pl.program_id(2) == 0)
        def _():
            o_ref[...] = jnp.zeros_like(o_ref)

        o_ref[...] += _dot_tn(a_ref[...], g_ref[...])

    if shard_cols:
        per = (n // N_CHIPS) // tn
        out_spec = pl.BlockSpec((None, ta, tn), lambda i, j, k: (j // per, i, j % per))
        out_shape = _sds((N_CHIPS, ka, n // N_CHIPS), F32)
    else:
        out_spec = pl.BlockSpec((ta, tn), lambda i, j, k: (i, j))
        out_shape = _sds((ka, n), F32)
    return _call(
        body, name="dw_" + tag, grid=(ka // ta, n // tn, nk),
        in_specs=[pl.BlockSpec((tk, ta), lambda i, j, k: (k, i)), pl.BlockSpec((tk, tn), lambda i, j, k: (k, j))],
        out_specs=[out_spec], out_shape=[out_shape],
        args=(a, g), sem=("parallel", "parallel", "arbitrary"), exchanges=exchanges)


def _dw_mix(merged, dmix, y_pool, dbp, y_attn, dba, exchanges=()):
    T = merged.shape[0]
    tk = min(2 * TM, T)
    c = D_MODEL // N_CHIPS

    def body(mg_ref, dmix_ref, yp_ref, dbp_ref, ya_ref, dba_ref, out_ref, bp_ref, ba_ref):
        @pl.when(pl.program_id(0) == 0)
        def _():
            for ref in (out_ref, bp_ref, ba_ref):
                ref[...] = jnp.zeros_like(ref)

        out_ref[...] += _dot_tn(mg_ref[...], dmix_ref[...])
        for y_ref, d_ref, o_ref in ((yp_ref, dbp_ref, bp_ref), (ya_ref, dba_ref, ba_ref)):
            res = _dot_tn(y_ref[...], d_ref[...])
            for j in range(N_CHIPS):
                o_ref[j] += res[:, c * j:c * (j + 1)]

    slabs = (N_CHIPS, POOL_WIDTH, c)
    return _call(
        body, name="dw_mix", grid=(T // tk,),
        in_specs=[_rows(tk, D_MODEL), _rows(tk, D_MODEL), _rows(tk, POOL_WIDTH), _rows(tk, D_MODEL),
                  _rows(tk, ATTN_WIDTH), _rows(tk, D_MODEL)],
        out_specs=[_const((D_MODEL, D_MODEL)), _const(slabs), _const(slabs)],
        out_shape=[_sds((D_MODEL, D_MODEL), F32), _sds(slabs, F32), _sds(slabs, F32)],
        args=(merged, dmix, y_pool, dbp, y_attn, dba), sem=("arbitrary",), exchanges=exchanges)


def _merge_bwd(dmix, gate, y_pool, y_attn, w_out, w_bp, w_ba, exchanges=()):
    T = dmix.shape[0]
    tm = min(TM, T)

    def body(dmix_ref, gate_ref, yp_ref, ya_ref, wo_ref, wbp_ref, wba_ref,
             dbp_ref, dba_ref, dgate_ref, dyp_ref, dya_ref):
        dm = _dot_nt(dmix_ref[...], wo_ref[...])
        for j, (y_ref, db_ref, w_ref, dy_ref) in enumerate(
                ((yp_ref, dbp_ref, wbp_ref, dyp_ref), (ya_ref, dba_ref, wba_ref, dya_ref))):
            sl = slice(D_MODEL * j, D_MODEL * (j + 1))
            gt = gate_ref[:, sl].astype(F32)
            db = (dm * gt).astype(BF16)
            db_ref[...] = db
            dgate_ref[:, sl] = (dm * _branch(y_ref[...], w_ref) * gt * (1.0 - gt)).astype(BF16)
            cw = D_MODEL // N_CHIPS
            dy = _dot_nt(db[:, :cw], w_ref[0])
            for c in range(1, N_CHIPS):
                dy = dy + _dot_nt(db[:, cw * c:cw * (c + 1)], w_ref[c])
            dy_ref[...] = dy.astype(dy_ref.dtype)

    return _call(
        body, name="merge_bwd", grid=(T // tm,),
        in_specs=[_rows(tm, D_MODEL), _rows(tm, GATE_WIDTH), _rows(tm, POOL_WIDTH), _rows(tm, ATTN_WIDTH),
                  _const((D_MODEL, D_MODEL)), _const(w_bp.shape), _const(w_ba.shape)],
        out_specs=[_rows(tm, D_MODEL), _rows(tm, D_MODEL), _rows(tm, GATE_WIDTH), _rows(tm, POOL_WIDTH),
                   _rows(tm, ATTN_WIDTH)],
        out_shape=[_sds((T, D_MODEL), BF16), _sds((T, D_MODEL), BF16), _sds((T, GATE_WIDTH), BF16),
                   _sds((T, POOL_WIDTH), F32), _sds((T, ATTN_WIDTH), BF16)],
        args=(dmix, gate, y_pool, y_attn, w_out, w_bp, w_ba), sem=("parallel",), exchanges=exchanges)


def _mixers_bwd(q, k, v, do, sinks, tabs, dyp, diff, w_pool, pool_scale, seq, exchanges=()):
    T = q.shape[0]
    nb = seq // BLOCK
    bl = T // seq
    steps = nb + 1
    tp = T // nb
    nseq = seq // tp
    per = tp // HALO
    last_halo = T // HALO - 1

    def body(sink_ref, q_ref, do_ref, kp_ref, kc_ref, vp_ref, vc_ref, c_ref, a_ref, bt_ref, cp_ref, ap_ref, btp_ref,
             dy_ref, nxt_ref, diff_ref, w_ref, s_ref,
             dq_ref, dk_ref, dv_ref, dsink_ref, du_ref, dw_ref, ds_ref, ck_ref, cv_ref):
        n = pl.program_id(0)

        @pl.when(n == 0)
        def _():
            for ref in (dsink_ref, ck_ref, cv_ref, dw_ref, ds_ref):
                ref[...] = jnp.zeros_like(ref)

        @pl.when(n < nb)
        def _():
            _pool_bwd_tile(n, tp, nseq, dy_ref, nxt_ref, diff_ref, w_ref, s_ref, du_ref, dw_ref, ds_ref)
            valid, lo = _attn_masks(n)
            for b in range(bl):
                kk = jnp.concatenate([kp_ref[b], kc_ref[b]], axis=0)
                vv = jnp.concatenate([vp_ref[b], vc_ref[b]], axis=0)
                dk_acc = jnp.zeros((2 * BLOCK, KV_WIDTH), F32)
                dv_acc = jnp.zeros((2 * BLOCK, KV_WIDTH), F32)
                for h in range(2):
                    qs = _stack_heads(q_ref.at[b], h, lo)
                    dos = _stack_heads(do_ref.at[b], h, lo)
                    pr, ps = _group_probs(qs, kk, valid, _sink_rows(sink_ref, h))
                    dp = _dot_nt(dos, vv)
                    delta = jnp.sum(pr * dp, axis=1, keepdims=True)
                    ds = (pr * (dp - delta)).astype(BF16)
                    dsk = ps * delta
                    for g in range(GROUP):
                        idx = GROUP * h + g
                        dsink_ref[idx:idx + 1, :] += (jnp.zeros((1, LANES), F32)
                                                      - jnp.sum(dsk[BLOCK * g:BLOCK * (g + 1)]))
                    dk_acc = dk_acc + _dot_tn(ds, qs)
                    dv_acc = dv_acc + _dot_tn(pr.astype(BF16), dos)
                    for j, pair in enumerate(_unstack_heads(_dot(ds, kk) * SCALE, h, lo)):
                        sl = slice(LANES * (2 * h + j), LANES * (2 * h + j + 1))
                        dq_ref[b, :, sl] = _rot_bwd(pair, c_ref[...], a_ref[...], bt_ref[...]).astype(BF16)
                fin_k = ck_ref[b] + dk_acc[:BLOCK]
                dk_ref[b] = _rot_bwd(fin_k, cp_ref[...], ap_ref[...], btp_ref[...]).astype(BF16)
                dv_ref[b] = (cv_ref[b] + dv_acc[:BLOCK]).astype(BF16)
                ck_ref[b] = dk_acc[BLOCK:]
                cv_ref[b] = dv_acc[BLOCK:]

        @pl.when(n == nb)
        def _():
            for b in range(bl):
                dk_ref[b] = _rot_bwd(ck_ref[b], cp_ref[...], ap_ref[...], btp_ref[...]).astype(BF16)
                dv_ref[b] = cv_ref[b].astype(BF16)

    cur = lambda n: (0, jnp.minimum(n, nb - 1), 0)
    prv = lambda n: (0, jnp.clip(n - 1, 0, nb - 1), 0)
    tcur = lambda n: (jnp.minimum(n, nb - 1), 0)
    tprv = lambda n: (jnp.clip(n - 1, 0, nb - 1), 0)
    wide = lambda m: pl.BlockSpec((bl, BLOCK, ATTN_WIDTH), m)
    kv = lambda m: pl.BlockSpec((bl, BLOCK, KV_WIDTH), m)
    tab = lambda m: pl.BlockSpec((BLOCK, LANES), m)
    tile = lambda n: (jnp.minimum(n, nb - 1), 0)
    halo = lambda n: (jnp.minimum((jnp.minimum(n, nb - 1) + 1) * per, last_halo), 0)
    rows = pl.BlockSpec((tp, POOL_WIDTH), tile)
    res = _call(
        body, name="mixers_bwd", grid=(steps,),
        in_specs=[pl.BlockSpec(memory_space=pltpu.SMEM), wide(cur), wide(cur), kv(prv), kv(cur), kv(prv), kv(cur),
                  tab(tcur), tab(tcur), tab(tcur), tab(tprv), tab(tprv), tab(tprv),
                  rows, pl.BlockSpec((HALO, POOL_WIDTH), halo), rows, _const((4, POOL_GC, POOL_GC)),
                  _const((1, POOL_WIDTH))],
        out_specs=[wide(cur), kv(prv), kv(prv), _const((8, LANES)), rows, _const((4, POOL_GC, POOL_GC)),
                   _const((1, POOL_WIDTH))],
        out_shape=[_sds((bl, seq, ATTN_WIDTH), BF16), _sds((bl, seq, KV_WIDTH), BF16),
                   _sds((bl, seq, KV_WIDTH), BF16), _sds((8, LANES), F32), _sds((T, POOL_WIDTH), BF16),
                   _sds((4, POOL_GC, POOL_GC), F32), _sds((1, POOL_WIDTH), F32)],
        scratch=[pltpu.VMEM((bl, BLOCK, KV_WIDTH), F32), pltpu.VMEM((bl, BLOCK, KV_WIDTH), F32)],
        args=(sinks, *_by_example(bl, q, do, k, k, v, v), *tabs, *tabs, dyp, dyp, diff, w_pool, pool_scale),
        sem=("arbitrary",), exchanges=exchanges)
    outs, rest = (res if exchanges else (res, None))
    outs = [outs[0].reshape(T, ATTN_WIDTH), outs[1].reshape(T, KV_WIDTH), outs[2].reshape(T, KV_WIDTH), *outs[3:]]
    return (outs, rest) if exchanges else outs


def _pool_bwd_tile(i, tp, nseq, dy_ref, nxt_ref, diff_ref, w_ref, s_ref, du_ref, dw_ref, ds_ref):
    last = (i % nseq) == nseq - 1
    nxt = jnp.where(last, 0.0, nxt_ref[...])
    ext = jnp.concatenate([dy_ref[...], nxt], axis=0) * s_ref[...]
    pos = (i % nseq) * tp + lax.broadcasted_iota(jnp.int32, (tp + HALO, 1), 0)
    for gi, w in enumerate(POOL_WINDOWS):
        sl = slice(POOL_GC * gi, POOL_GC * (gi + 1))
        wg = w_ref[gi].astype(BF16)
        dmx = ext[:, sl].astype(BF16)
        ddiff = _dot_nt(dmx, wg)
        s = ddiff * _inv_count(pos, w)
        sh = 1
        while sh < w:
            s = s + pltpu.roll(s, tp + HALO - sh, 0)
            sh *= 2
        du_ref[:, sl] = (s[:tp] - ddiff[:tp]).astype(BF16)
        dg = diff_ref[:, sl]
        dw_ref[gi] += _dot_tn(dg, dmx[:tp])
        ds_ref[:, sl] += jnp.sum(dy_ref[:, sl] * _dot(dg, wg), axis=0, keepdims=True)


_PARTS = ((0, C_Q), (C_Q, C_K), (C_K, C_V), (C_V, C_G), (C_G, IN_WIDTH))


def _inproj_bwd(parts, x2, dx1, w_in_t, g1, exchanges=()):
    T = x2.shape[0]
    tm = min(TM, T)

    def body(du_ref, dq_ref, dk_ref, dv_ref, dgt_ref, x_ref, dx1_ref, w_ref, g_ref, gx_ref, dg_ref):
        @pl.when(pl.program_id(0) == 0)
        def _():
            dg_ref[...] = jnp.zeros_like(dg_ref)

        dh = jnp.zeros((tm, D_MODEL), F32)
        for (lo, hi), p_ref in zip(_PARTS, (du_ref, dq_ref, dk_ref, dv_ref, dgt_ref)):
            dh = dh + _dot(p_ref[...], w_ref[lo:hi, :])
        dx, dg = _norm_bwd(x_ref[...], g_ref[...], dh)
        gx_ref[...] = dx1_ref[...] + dx
        dg_ref[...] += dg

    return _call(
        body, name="inproj_bwd", grid=(T // tm,),
        in_specs=[_rows(tm, hi - lo) for lo, hi in _PARTS]
        + [_rows(tm, D_MODEL), _rows(tm, D_MODEL), _const((IN_WIDTH, D_MODEL)), _const((1, D_MODEL))],
        out_specs=[_rows(tm, D_MODEL), _const((1, D_MODEL))],
        out_shape=[_sds((T, D_MODEL), F32), _sds((1, D_MODEL), F32)],
        args=(*parts, x2, dx1, w_in_t, g1), sem=("arbitrary",), exchanges=exchanges)


def _dw_in(h, parts, exchanges=()):
    T = h.shape[0]
    tk = min(TM, T)

    def body(h_ref, du_ref, dq_ref, dk_ref, dv_ref, dgt_ref, o_ref, db_ref):
        @pl.when(pl.program_id(0) == 0)
        def _():
            o_ref[...] = jnp.zeros_like(o_ref)
            db_ref[...] = jnp.zeros_like(db_ref)

        hh = h_ref[...]
        ones = jnp.ones((8, tk), BF16)
        for (lo, hi), p_ref in zip(_PARTS, (du_ref, dq_ref, dk_ref, dv_ref, dgt_ref)):
            part = p_ref[...]
            o_ref[lo:hi, :] += _dot_tn(part, hh)
            db_ref[:, lo:hi] += _dot(ones, part)[:1]

    return _call(
        body, name="dw_in", grid=(T // tk,),
        in_specs=[_rows(tk, D_MODEL)] + [_rows(tk, hi - lo) for lo, hi in _PARTS],
        out_specs=[_const((IN_WIDTH, D_MODEL)), _const((1, IN_WIDTH))],
        out_shape=[_sds((IN_WIDTH, D_MODEL), F32), _sds((1, IN_WIDTH), F32)],
        args=(h, *parts), sem=("arbitrary",), exchanges=exchanges)


def _row_tile(rows, cap=256, mult=16):
    best = None
    for t in range(mult, min(rows, cap) + 1, mult):
        if rows % t == 0:
            best = t
    if best is None:
        raise ValueError("no row tile for %d rows" % rows)
    return best


def _pair_sum(ids, full, got):
    _, r, c = full.shape
    hr = r // 2
    tr = _row_tile(hr)
    nblk = hr // tr

    def body(ids_ref, a_ref, b_ref, own_ref, sb_ref):
        s = a_ref[...] + b_ref[...]
        sb_ref[...] = s.astype(BF16)

        @pl.when(pl.program_id(1) == ids_ref[0])
        def _():
            own_ref[...] = s

    slab = pl.BlockSpec((None, tr, c), lambda i, j, ids_ref: (j, i, 0))
    return pl.pallas_call(
        body, name="pair_sum_%dx%d" % (r, c),
        grid_spec=pltpu.PrefetchScalarGridSpec(
            num_scalar_prefetch=1, grid=(nblk, N_CHIPS),
            in_specs=[pl.BlockSpec((None, tr, c), lambda i, j, ids_ref: (j, ids_ref[1] * nblk + i, 0)), slab],
            out_specs=[pl.BlockSpec((tr, c), lambda i, j, ids_ref: (i, 0)), slab]),
        out_shape=[_sds((hr, c), F32), _sds((N_CHIPS, hr, c), BF16)],
        compiler_params=_cp("parallel", "arbitrary"),
    )(ids, full, got)


def _pair_sum_small(ids, fulls, gots):
    n = len(fulls)
    dims = [(f.shape[1] // 2, f.shape[2]) for f in fulls]

    def body(ids_ref, *refs):
        ins, outs = refs[:2 * n], refs[2 * n:]
        for k in range(n):
            s = ins[2 * k][...] + ins[2 * k + 1][...]
            outs[2 * k + 1][...] = s.astype(BF16)

            @pl.when(pl.program_id(0) == ids_ref[0])
            def _(k=k, s=s):
                outs[2 * k][...] = s

    in_specs, out_specs, out_shape = [], [], []
    for hr, c in dims:
        slab = pl.BlockSpec((None, hr, c), lambda j, ids_ref: (j, 0, 0))
        in_specs += [pl.BlockSpec((None, hr, c), lambda j, ids_ref: (j, ids_ref[1], 0)), slab]
        out_specs += [pl.BlockSpec((hr, c), lambda j, ids_ref: (0, 0)), slab]
        out_shape += [_sds((hr, c), F32), _sds((N_CHIPS, hr, c), BF16)]
    res = pl.pallas_call(
        body, name="pair_sum_small",
        grid_spec=pltpu.PrefetchScalarGridSpec(num_scalar_prefetch=1, grid=(N_CHIPS,), in_specs=in_specs,
                                               out_specs=out_specs),
        out_shape=out_shape, compiler_params=_cp("arbitrary"),
    )(ids, *[a for pair in zip(fulls, gots) for a in pair])
    return [(res[2 * k], res[2 * k + 1]) for k in range(n)]


def _chip_sum_small(ids, owns, gots):
    n = len(owns)

    def body(ids_ref, *refs):
        ins, outs = refs[:2 * n], refs[2 * n:]
        for k in range(n):
            a, b = ins[2 * k], ins[2 * k + 1]
            outs[k][...] = ((a[...] + b[0].astype(F32)) + b[1].astype(F32)) + b[2].astype(F32)

    in_specs, out_specs, out_shape = [], [], []
    for own in owns:
        hr, c = own.shape
        in_specs += [pl.BlockSpec((hr, c), lambda i, ids_ref: (0, 0)),
                     pl.BlockSpec((3, hr, c), lambda i, ids_ref: (0, 0, 0))]
        out_specs.append(pl.BlockSpec((hr, c), lambda i, ids_ref: (ids_ref[1], 0)))
        out_shape.append(_sds((2 * hr, c), F32))
    return pl.pallas_call(
        body, name="chip_sum_small",
        grid_spec=pltpu.PrefetchScalarGridSpec(num_scalar_prefetch=1, grid=(1,), in_specs=in_specs,
                                               out_specs=out_specs),
        out_shape=out_shape, compiler_params=_cp("arbitrary"),
    )(ids, *[a for pair in zip(owns, gots) for a in pair])


def _chip_sum(ids, own, got):
    hr, c = own.shape
    tr = _row_tile(hr)
    nblk = hr // tr

    def body(ids_ref, a_ref, b_ref, o_ref):
        o_ref[...] = ((a_ref[...] + b_ref[0].astype(F32)) + b_ref[1].astype(F32)) + b_ref[2].astype(F32)

    return pl.pallas_call(
        body, name="chip_sum_%dx%d" % (hr, c),
        grid_spec=pltpu.PrefetchScalarGridSpec(
            num_scalar_prefetch=1, grid=(nblk,),
            in_specs=[pl.BlockSpec((tr, c), lambda i, ids_ref: (i, 0)),
                      pl.BlockSpec((3, tr, c), lambda i, ids_ref: (0, i, 0))],
            out_specs=pl.BlockSpec((tr, c), lambda i, ids_ref: (ids_ref[1] * nblk + i, 0))),
        out_shape=_sds((2 * hr, c), F32),
        compiler_params=_cp("parallel"),
    )(ids, own, got)


def _adamw_math(w, g, m, v):
    nm = ADAM_B1 * m + (1.0 - ADAM_B1) * g
    nv = ADAM_B2 * v + (1.0 - ADAM_B2) * (g * g)
    m_hat = nm / (1.0 - ADAM_B1 ** ADAM_STEP)
    v_hat = nv / (1.0 - ADAM_B2 ** ADAM_STEP)
    return -ADAM_LR * (m_hat / (jnp.sqrt(v_hat) + ADAM_EPS) + ADAM_WD * w), nm, nv


def _adamw(w, g, m, v):
    r, c = w.shape
    tr = _row_tile(r, cap=512, mult=8)

    def body(w_ref, g_ref, m_ref, v_ref, d_ref, nm_ref, nv_ref):
        d_ref[...], nm_ref[...], nv_ref[...] = _adamw_math(w_ref[...], g_ref[...], m_ref[...], v_ref[...])

    spec = _rows(tr, c)
    return pl.pallas_call(
        body, name="adamw_%dx%d" % (r, c), grid=(r // tr,),
        in_specs=[spec] * 4, out_specs=[spec] * 3, out_shape=[_sds((r, c), F32)] * 3,
        compiler_params=_cp("parallel"),
    )(w, g, m, v)


SC_TILES = 32
SC_LANES = 16
SC_ROWS = 8


def _adamw_sparse(w, g, m, v):
    r, c = w.shape
    rows = r // SC_TILES
    step = min(rows, SC_ROWS)

    def body(w_hbm, g_hbm, m_hbm, v_hbm, d_hbm, nm_hbm, nv_hbm, wb, gb, mb, vb):
        tile = lax.axis_index("sc_subcore") * 2 + lax.axis_index("sc_core")

        @pl.loop(0, rows, step=step)
        def _(r0):
            mine = pl.ds(tile * rows + r0, step)
            for src, dst in ((w_hbm, wb), (g_hbm, gb), (m_hbm, mb), (v_hbm, vb)):
                pltpu.sync_copy(src.at[mine], dst)

            @pl.loop(0, step)
            def _(row):
                @pl.loop(0, c, step=SC_LANES)
                def _(i):
                    at = (row, pl.ds(i, SC_LANES))
                    wb[at], mb[at], vb[at] = _adamw_math(wb[at], gb[at], mb[at], vb[at])

            for src, dst in ((wb, d_hbm), (mb, nm_hbm), (vb, nv_hbm)):
                pltpu.sync_copy(src, dst.at[mine])

    return pl.kernel(
        body, name="adamw_sparse_%dx%d" % (r, c), out_type=[_sds((r, c), F32)] * 3,
        mesh=plsc.VectorSubcoreMesh(core_axis_name="sc_core", subcore_axis_name="sc_subcore"),
        scratch_types=[pltpu.VMEM((step, c), F32)] * 4,
    )(w, g, m, v)


_SMALL_NAMES = ("w_pool", "b_in", "g_mix_pre", "g_mix_post", "g_mlp_pre", "g_mlp_post", "pool_scale", "attn_sinks")
B_ROWS = -(-IN_WIDTH // D_MODEL)


def _row_block(rows):
    rows = [jnp.pad(r.astype(F32), ((0, 0), (0, D_MODEL - r.shape[1]))) for r in rows]
    return jnp.pad(jnp.concatenate(rows, axis=0), ((0, 8 - len(rows)), (0, 0)))


def _early_block(dg2, dg3, dg4, dps, dsink, loss):
    tail = jnp.concatenate([jnp.pad(dsink.reshape(1, -1), ((0, 0), (0, LANES - dsink.size))),
                            jnp.pad(loss.reshape(1, 1), ((0, 0), (0, LANES - 1)))], axis=1)
    return _row_block([dg2, dg3, dg4, dps, tail])


def _late_block(db_in, dg1):
    b = jnp.pad(db_in, ((0, 0), (0, B_ROWS * D_MODEL - IN_WIDTH))).reshape(B_ROWS, D_MODEL)
    return _row_block([b[r:r + 1] for r in range(B_ROWS)] + [dg1])


def _small_update(gearly, gmat, glate, w, m, v):
    names = _SMALL_NAMES
    n = len(names)

    def total(ref, rows):
        acc = ref[0:rows, :]
        for d in range(1, N_DEV):
            acc = acc + ref[d * rows:(d + 1) * rows, :]
        return acc

    def body(*refs):
        early_ref, gmat_ref, late_ref = refs[:3]
        w_refs, m_refs, v_refs = refs[3:3 + n], refs[3 + n:3 + 2 * n], refs[3 + 2 * n:3 + 3 * n]
        outs = refs[3 + 3 * n:]
        loss_ref, g_refs, d_refs = outs[0], outs[1:1 + n], outs[1 + n:1 + 2 * n]
        nm_refs, nv_refs = outs[1 + 2 * n:1 + 3 * n], outs[1 + 3 * n:1 + 4 * n]
        early, late = total(early_ref, 8), total(late_ref, 8)
        loss_ref[...] = jnp.sum(early[4:5, LANES:2 * LANES], axis=1, keepdims=True)
        bias = jnp.concatenate([late[r:r + 1, :] for r in range(B_ROWS - 1)]
                               + [late[B_ROWS - 1:B_ROWS, :IN_WIDTH - (B_ROWS - 1) * D_MODEL]], axis=1)
        grad = dict(b_in=bias, g_mix_pre=late[B_ROWS:B_ROWS + 1, :], g_mix_post=early[0:1, :],
                    g_mlp_pre=early[1:2, :], g_mlp_post=early[2:3, :], pool_scale=early[3:4, :POOL_WIDTH],
                    attn_sinks=early[4:5, :N_Q_HEADS])
        for i, name in enumerate(names):
            g = total(gmat_ref, 4 * POOL_GC) if name == "w_pool" else grad[name]
            g_refs[i][...] = g
            d_refs[i][...], nm_refs[i][...], nv_refs[i][...] = _adamw_math(
                w_refs[i][...], g, m_refs[i][...], v_refs[i][...])

    shapes = [_sds(w[k].shape, F32) for k in names]
    res = pl.pallas_call(
        body, name="small_update", out_shape=[_sds((1, 1), F32)] + shapes * 4,
        compiler_params=pltpu.CompilerParams(vmem_limit_bytes=VMEM_MB * 1024 * 1024),
    )(gearly, gmat, glate, *[w[k] for k in names], *[m[k] for k in names], *[v[k] for k in names])
    loss = res[0]
    per = {k: tuple(res[1 + j * n + i] for j in range(4)) for i, k in enumerate(names)}
    return loss, per


_BIG = ("w_in", "w_branch_pool", "w_branch_attn", "w_out", "w_up", "w_down")
_ORDER = ("g_mix_pre", "w_in", "b_in", "w_pool", "pool_scale", "attn_sinks", "w_branch_pool", "w_branch_attn",
          "w_out", "g_mix_post", "g_mlp_pre", "w_up", "w_down", "g_mlp_post")


def _stack_rows(slab):
    return slab.reshape(-1, slab.shape[2])


def _step(x2, tgt, seq, shards, small, ids):
    tabs = _rope_tables(seq)
    g1, g2, g3, g4 = (small[n] for n in ("g_mix_pre", "g_mix_post", "g_mlp_pre", "g_mlp_post"))
    sinks = small["attn_sinks"].reshape(N_Q_HEADS)
    w_pool = small["w_pool"].reshape(4, POOL_GC, POOL_GC)
    pool_scale = small["pool_scale"]

    def whole(shard, slabs):
        return lax.dynamic_update_slice(slabs, shard[None], (ids[0], 0, 0))

    (up_a, up_b, down_a, down_b, *mix_shards), [[in_slab]] = _cast_shards(
        *(shards[n] for n in ("w_up", "w_down", "w_branch_pool", "w_branch_attn", "w_out")),
        exchanges=[_ex_gather([shards["w_in"]])])
    w_in = _stack_rows(whole(shards["w_in"], in_slab))
    (h, u, q, k, v, gate), [mix_slabs] = _inproj(
        x2, g1, w_in, small["b_in"], tabs, seq, exchanges=[_ex_gather(mix_shards)])
    w_bp, w_ba, out_slab = (whole(s, g) for s, g in zip(mix_shards, mix_slabs))
    w_out = _stack_rows(out_slab)
    (y_attn, diff, y_pool), [[got_a, got_b]] = _mixers_fwd(
        q, k, v, sinks, u, w_pool, pool_scale, seq, exchanges=[_ex_gather([up_a, up_b])])
    (merged, mix, x1, h2), [[got_c, got_d]] = _merge_out(
        y_pool, y_attn, gate, x2, w_bp, w_ba, w_out, g2, g3, exchanges=[_ex_gather([down_a, down_b])])
    w_up = (whole(up_a, got_a), whole(up_b, got_b))
    w_down = (whole(down_a, got_c), whole(down_b, got_d))
    act, dff, dup, dx1, dmix, loss_acc, dg4, dg3, dg2 = _mlp_core(h2, x1, mix, tgt, w_up, w_down, g4, g3, g2)

    dw_down = _dw("down", act, dff, 1024, 1024)[0].reshape(N_CHIPS, D_FF // N_CHIPS, D_MODEL)
    (dbp, dba, dgate, dyp, dya), [[got]] = _merge_bwd(
        dmix, gate, y_pool, y_attn, w_out, w_bp, w_ba, exchanges=[_ex_pair([dw_down])])
    ps_down = _pair_sum(ids, dw_down, got)
    (dw_up,), [[got]] = _dw("up", h2, dup, 1024, 1024, shard_cols=True, exchanges=[_ex_chip([ps_down[1]])])
    half_down = _chip_sum(ids, ps_down[0], got)
    (dw_out, dw_bp, dw_ba), [[got]] = _dw_mix(merged, dmix, y_pool, dbp, y_attn, dba, exchanges=[_ex_pair([dw_up])])
    ps_up = _pair_sum(ids, dw_up, got)
    dw_mix = [dw_out.reshape(N_CHIPS, D_MODEL // N_CHIPS, D_MODEL), dw_bp, dw_ba]
    (dq, dk, dv, dsink, du, dw_pool, dps), [[got], gots, [g_down]] = _mixers_bwd(
        q, k, v, dya, sinks, tabs, dyp, diff, w_pool, pool_scale, seq,
        exchanges=[_ex_chip([ps_up[1]]), _ex_pair(dw_mix), _ex_swap([half_down])])
    half_up = _chip_sum(ids, ps_up[0], got)
    ps_mix = _pair_sum_small(ids, dw_mix, gots)
    parts = (du, dq, dk, dv, dgate)
    early = _early_block(dg2, dg3, dg4, dps, dsink[:, 0], loss_acc[0, 0])
    mat = dw_pool.reshape(4 * POOL_GC, POOL_GC)
    (dw_in_t, db_in), [gots, [gearly, gmat], [g_up]] = _dw_in(
        h, parts, exchanges=[_ex_chip([p[1] for p in ps_mix]), _ex_allgather([early, mat]), _ex_swap([half_up])])
    half_mix = _chip_sum_small(ids, [p[0] for p in ps_mix], gots)
    dw_in = dw_in_t.reshape(N_CHIPS, IN_WIDTH // N_CHIPS, D_MODEL)
    g_mix, [got] = _alone("swap_mix_pair_in", _ex_swap(half_mix), _ex_pair([dw_in]))
    ps_in = _pair_sum(ids, dw_in, got)
    (gx, dg1), [[got]] = _inproj_bwd(parts, x2, dx1, w_in, g1, exchanges=[_ex_chip([ps_in[1]])])
    [g_in], [glate] = _alone("swap_in_allgather", _ex_swap([_chip_sum(ids, ps_in[0], got)]),
                             _ex_allgather([_late_block(db_in, dg1)]))

    grads = dict(w_in=g_in, w_branch_pool=g_mix[1], w_branch_attn=g_mix[2], w_out=g_mix[0], w_up=g_up, w_down=g_down)
    return (gearly, gmat, glate), gx, grads


def kernel(x, g_mix_pre, w_in, b_in, w_pool, pool_scale, attn_sinks, w_branch_pool, w_branch_attn, w_out, g_mix_post, g_mlp_pre, w_up, w_down, g_mlp_post, loss_target, m_g_mix_pre, m_w_in, m_b_in, m_w_pool, m_pool_scale, m_attn_sinks, m_w_branch_pool, m_w_branch_attn, m_w_out, m_g_mix_post, m_g_mlp_pre, m_w_up, m_w_down, m_g_mlp_post, v_g_mix_pre, v_w_in, v_b_in, v_w_pool, v_pool_scale, v_attn_sinks, v_w_branch_pool, v_w_branch_attn, v_w_out, v_g_mix_post, v_g_mlp_pre, v_w_up, v_w_down, v_g_mlp_post):
    weights = dict(g_mix_pre=g_mix_pre, w_in=w_in, b_in=b_in, w_pool=w_pool, pool_scale=pool_scale,
                   attn_sinks=attn_sinks, w_branch_pool=w_branch_pool, w_branch_attn=w_branch_attn, w_out=w_out,
                   g_mix_post=g_mix_post, g_mlp_pre=g_mlp_pre, w_up=w_up, w_down=w_down, g_mlp_post=g_mlp_post)
    mom1 = dict(g_mix_pre=m_g_mix_pre, w_in=m_w_in, b_in=m_b_in, w_pool=m_w_pool, pool_scale=m_pool_scale,
                attn_sinks=m_attn_sinks, w_branch_pool=m_w_branch_pool, w_branch_attn=m_w_branch_attn,
                w_out=m_w_out, g_mix_post=m_g_mix_post, g_mlp_pre=m_g_mlp_pre, w_up=m_w_up, w_down=m_w_down,
                g_mlp_post=m_g_mlp_post)
    mom2 = dict(g_mix_pre=v_g_mix_pre, w_in=v_w_in, b_in=v_b_in, w_pool=v_w_pool, pool_scale=v_pool_scale,
                attn_sinks=v_attn_sinks, w_branch_pool=v_w_branch_pool, w_branch_attn=v_w_branch_attn,
                w_out=v_w_out, g_mix_post=v_g_mix_post, g_mlp_pre=v_g_mlp_pre, w_up=v_w_up, w_down=v_w_down,
                g_mlp_post=v_g_mlp_post)
    b_loc, seq, _ = x.shape
    x2 = x.reshape(b_loc * seq, D_MODEL)
    tgt = loss_target.reshape(b_loc * seq, D_MODEL)
    ids = jnp.stack([2 * lax.axis_index("x") + lax.axis_index("y"), lax.axis_index("c")]).astype(jnp.int32)

    def flat(n, a):
        return a[0].T if n == "w_in" else a[0]

    def unflat(n, a):
        return (a.T if n == "w_in" else a)[None]

    shards = {n: flat(n, weights[n]).astype(BF16) if n == "w_in" else flat(n, weights[n]) for n in _BIG}
    small = {n: weights[n] for n in _ORDER if n not in _BIG}
    (gearly, gmat, glate), gx, grads = _step(x2, tgt, seq, shards, small, ids)

    def two_d(src):
        return {n: src[n].reshape(4 * POOL_GC, POOL_GC) if n == "w_pool" else src[n] for n in _SMALL_NAMES}

    loss, per = _small_update(gearly, gmat, glate, two_d(weights), two_d(mom1), two_d(mom2))
    delta, new_m, new_v = {}, {}, {}
    for n in _SMALL_NAMES:
        grads[n], delta[n], new_m[n], new_v[n] = (a.reshape(weights[n].shape) for a in per[n])
    for n in _BIG:
        update = _adamw if n == "w_in" else _adamw_sparse
        d, nm, nv = update(flat(n, weights[n]), grads[n], flat(n, mom1[n]), flat(n, mom2[n]))
        grads[n] = unflat(n, grads[n])
        delta[n], new_m[n], new_v[n] = unflat(n, d), unflat(n, nm), unflat(n, nv)

    return (loss[0, 0], gx.reshape(x.shape), *[grads[n] for n in _ORDER], *[delta[n] for n in _ORDER],
            *[new_m[n] for n in _ORDER], *[new_v[n] for n in _ORDER])
```

```python
import jax
import jax.numpy as jnp
from jax import lax
from jax.experimental import pallas as pl
from jax.experimental.pallas import tpu as pltpu
from jax.experimental.pallas import tpu_sc as plsc

F32 = jnp.float32
BF16 = jnp.bfloat16

D_MODEL = 1024
POOL_WINDOWS = (2, 4, 8, 16)
POOL_WIDTH = 512
POOL_GC = 128
HALO = 16
HEAD_DIM = 64
N_Q_HEADS = 8
ATTN_WIDTH = 512
KV_WIDTH = 128
BLOCK = 128
NEG_INF = -1e30
ROPE_THETA = 500000.0
ROT_DIM = 16
GATE_WIDTH = 2048
IN_WIDTH = 3328
D_FF = 4096
EPS = 1e-6
SCALE = HEAD_DIM ** -0.5
C_Q, C_K, C_V, C_G = 512, 1024, 1152, 1280

ADAM_LR, ADAM_B1, ADAM_B2, ADAM_EPS, ADAM_WD, ADAM_STEP = 0.001, 0.9, 0.999, 1e-08, 0.01, 10

N_CHIPS = 4
N_DEV = 8
LANES = 128
TM = 512
VMEM_MB = 56

MESH = pl.DeviceIdType.MESH
ANY = pl.BlockSpec(memory_space=pl.ANY)


def _cp(*sem, vmem=VMEM_MB):
    return pltpu.CompilerParams(dimension_semantics=sem, vmem_limit_bytes=vmem * 1024 * 1024)


def _rows(tile, cols):
    return pl.BlockSpec((tile, cols), lambda i: (i, 0))


def _const(shape):
    nd = len(shape)
    return pl.BlockSpec(shape, lambda i: (0,) * nd)


def _sds(shape, dtype):
    return jax.ShapeDtypeStruct(shape, dtype)


def _dot(a, b):
    return jnp.dot(a, b, preferred_element_type=F32)


def _dot_nt(a, b):
    return lax.dot_general(a, b, (((1,), (1,)), ((), ())), preferred_element_type=F32)


def _dot_tn(a, b):
    return lax.dot_general(a, b, (((0,), (0,)), ((), ())), preferred_element_type=F32)


def _rms(x):
    return lax.rsqrt(jnp.mean(x * x, axis=-1, keepdims=True) + EPS)


def _norm_bwd(x, g, dout):
    r = _rms(x)
    n = x * r
    dn = dout * g
    dx = r * (dn - n * jnp.mean(dn * n, axis=-1, keepdims=True))
    return dx, jnp.sum(dout * n, axis=0, keepdims=True)


def _rot_fwd(t, c, a, bt):
    return t * c + pltpu.roll(t, LANES - 8, 1) * a + pltpu.roll(t, 8, 1) * bt


def _rot_bwd(d, c, a, bt):
    return d * c + pltpu.roll(d * a, 8, 1) + pltpu.roll(d * bt, LANES - 8, 1)


def _rope_tables(seq):
    pos = jnp.arange(seq, dtype=F32)
    inv_freq = ROPE_THETA ** (-jnp.arange(0, ROT_DIM, 2, dtype=F32) / ROT_DIM)
    ang = pos[:, None] * inv_freq[None, :]
    cos, sin = jnp.cos(ang), jnp.sin(ang)
    ones = jnp.ones((seq, HEAD_DIM - ROT_DIM), F32)
    zeros8 = jnp.zeros((seq, 8), F32)
    zrest = jnp.zeros((seq, HEAD_DIM - ROT_DIM), F32)
    c = jnp.concatenate([cos, cos, ones], axis=1)
    a = jnp.concatenate([-sin, zeros8, zrest], axis=1)
    bt = jnp.concatenate([zeros8, sin, zrest], axis=1)
    return tuple(jnp.tile(t, (1, 2)) for t in (c, a, bt))


class _Exchange:
    def __init__(self, inputs, out_shapes, sems, start, finish, aliases=None, middle=None):
        self.inputs, self.out_shapes, self.sems = list(inputs), list(out_shapes), list(sems)
        self.start, self.finish, self.aliases = start, finish, dict(aliases or {})
        self.middle = middle


def _call(body, *, name, grid, in_specs, out_specs, out_shape, args, scratch=(), sem=(), exchanges=()):
    in_specs, out_specs, out_shape, scratch = list(in_specs), list(out_specs), list(out_shape), list(scratch)
    if not exchanges:
        return pl.pallas_call(body, name=name, grid=grid, in_specs=in_specs, out_specs=out_specs,
                              out_shape=out_shape, scratch_shapes=scratch, compiler_params=_cp(*sem))(*args)
    n_in, n_out, n_scr = len(in_specs), len(out_specs), len(scratch)
    x_in = [a for ex in exchanges for a in ex.inputs]
    x_out = [s for ex in exchanges for s in ex.out_shapes]
    x_sem = [s for ex in exchanges for s in ex.sems]
    aliases, i_off, o_off = {}, n_in, n_out
    for ex in exchanges:
        for i, o in ex.aliases.items():
            aliases[i_off + i] = o_off + o
        i_off += len(ex.inputs)
        o_off += len(ex.out_shapes)

    def split(flat):
        out, pos = [], 0
        for ex, n in zip(exchanges, flat[1]):
            out.append(flat[0][pos:pos + n])
            pos += n
        return out

    def carrier(*refs):
        pos = 0
        groups = []
        for n in (n_in, len(x_in), n_out, len(x_out), n_scr, len(x_sem)):
            groups.append(refs[pos:pos + n])
            pos += n
        ins, xin, outs, xout, scr, xsem = groups
        xin = split((xin, [len(ex.inputs) for ex in exchanges]))
        xout = split((xout, [len(ex.out_shapes) for ex in exchanges]))
        xsem = split((xsem, [len(ex.sems) for ex in exchanges]))
        first = pl.program_id(0) == 0
        last = pl.program_id(0) == grid[0] - 1
        for d in range(1, len(grid)):
            first = jnp.logical_and(first, pl.program_id(d) == 0)
            last = jnp.logical_and(last, pl.program_id(d) == grid[d] - 1)

        @pl.when(first)
        def _():
            for ex, i, o, s in zip(exchanges, xin, xout, xsem):
                ex.start(i, o, s)

        if any(ex.middle for ex in exchanges):
            half = pl.program_id(0) == 5 * grid[0] // 8
            for d in range(1, len(grid)):
                half = jnp.logical_and(half, pl.program_id(d) == 0)

            @pl.when(half)
            def _():
                for ex, i, o, s in zip(exchanges, xin, xout, xsem):
                    if ex.middle:
                        ex.middle(i, o, s)

        body(*ins, *outs, *scr)

        @pl.when(last)
        def _():
            for ex, i, o, s in zip(exchanges, xin, xout, xsem):
                ex.finish(i, o, s)

    res = pl.pallas_call(
        carrier, name=name, grid=grid, in_specs=in_specs + [ANY] * len(x_in),
        out_specs=out_specs + [ANY] * len(x_out), out_shape=out_shape + x_out,
        scratch_shapes=scratch + x_sem, input_output_aliases=aliases,
        compiler_params=_cp(*(["arbitrary"] * len(grid))),
    )(*args, *x_in)
    return res[:n_out], split((res[n_out:], [len(ex.out_shapes) for ex in exchanges]))


def _alone(name, *exchanges):
    n_in = [len(ex.inputs) for ex in exchanges]
    n_out = [len(ex.out_shapes) for ex in exchanges]
    n_sem = [len(ex.sems) for ex in exchanges]
    aliases, i_off, o_off = {}, 0, 0
    for ex in exchanges:
        for i, o in ex.aliases.items():
            aliases[i_off + i] = o_off + o
        i_off += len(ex.inputs)
        o_off += len(ex.out_shapes)

    def split(flat, counts):
        out, pos = [], 0
        for n in counts:
            out.append(flat[pos:pos + n])
            pos += n
        return out

    def body(*refs):
        ins, outs, sems = split(refs, [sum(n_in), sum(n_out), sum(n_sem)])
        groups = list(zip(exchanges, split(ins, n_in), split(outs, n_out), split(sems, n_sem)))
        for ex, i, o, s in groups:
            ex.start(i, o, s)
        for ex, i, o, s in groups:
            if ex.middle:
                ex.middle(i, o, s)
        for ex, i, o, s in groups:
            ex.finish(i, o, s)

    res = pl.pallas_call(
        body, name=name, in_specs=[ANY] * sum(n_in), out_specs=[ANY] * sum(n_out),
        out_shape=[s for ex in exchanges for s in ex.out_shapes],
        scratch_shapes=[s for ex in exchanges for s in ex.sems], input_output_aliases=aliases,
    )(*[a for ex in exchanges for a in ex.inputs])
    return split(res, n_out)


def _place():
    x, y, c = lax.axis_index("x"), lax.axis_index("y"), lax.axis_index("c")
    chips = [(1 - x, y), (x, 1 - y), (1 - x, 1 - y)]
    return x, y, c, chips


def _remote(src, dst, send, recv, to):
    return pltpu.make_async_remote_copy(src_ref=src, dst_ref=dst, send_sem=send, recv_sem=recv,
                                        device_id=to, device_id_type=MESH)


def _ex_gather(shards):
    nw = len(shards)
    hrs = [s.shape[0] // 2 for s in shards]

    def copies(ins, outs, sems):
        s0, r0, s1, r1, s2, r2, fs, fr = sems
        x, y, c, _ = _place()
        me, xn, yn, dg = (x, y), (1 - x, y), (x, 1 - y), (1 - x, 1 - y)
        nbr = (xn, yn)
        sibling = (x, y, 1 - c)

        def piece(w, chip, core, part=None):
            hr = hrs[w]
            rows = pl.ds(core * hr, hr) if part is None else pl.ds(core * hr + part * (hr // 2), hr // 2)
            return outs[w].at[2 * chip[0] + chip[1], rows]

        def first(w, k, lead):
            part = k if lead else 1 - k
            send, recv = (s0, r0) if lead else (s1, r1)
            rows = pl.ds(c * hrs[w] + part * (hrs[w] // 2), hrs[w] // 2)
            return _remote(ins[w].at[rows], piece(w, me, c, part), send.at[w, k], recv.at[w, k], (*nbr[k], c))

        def landed(w, k, lead):
            part = k if lead else 1 - k
            send, recv = (s0, r0) if lead else (s1, r1)
            return _remote(piece(w, nbr[k], c, part), piece(w, nbr[k], c, part), send.at[w, k], recv.at[w, k],
                           (*nbr[k], c))

        def onward(w, k):
            return _remote(piece(w, nbr[k], c, k), piece(w, nbr[k], c, k), s2.at[w, k], r2.at[w, k],
                           (*nbr[1 - k], c))

        def arrived(w, k):
            return _remote(piece(w, dg, c, k), piece(w, dg, c, k), s2.at[w, k], r2.at[w, k], (*nbr[1 - k], c))

        def passed(w, j):
            chip = (xn, yn, dg)[j]
            return _remote(piece(w, chip, c), piece(w, chip, c), fs.at[w, j], fr.at[w, j], sibling)

        def handed(w, j):
            chip = (xn, yn, dg)[j]
            return _remote(piece(w, chip, 1 - c), piece(w, chip, 1 - c), fs.at[w, j], fr.at[w, j], sibling)

        return first, landed, onward, arrived, passed, handed

    def start(ins, outs, sems):
        first = copies(ins, outs, sems)[0]
        for lead in (True, False):
            for w in range(nw):
                for k in range(2):
                    first(w, k, lead).start()

    def middle(ins, outs, sems):
        _, landed, onward, _, passed, _ = copies(ins, outs, sems)
        for w in range(nw):
            for k in range(2):
                landed(w, k, True).wait_recv()
                onward(w, k).start()
        for w in range(nw):
            for k in range(2):
                landed(w, k, False).wait_recv()
                passed(w, k).start()

    def finish(ins, outs, sems):
        first, _, onward, arrived, passed, handed = copies(ins, outs, sems)
        for w in range(nw):
            for k in range(2):
                arrived(w, k).wait_recv()
            passed(w, 2).start()
        for w in range(nw):
            for j in range(3):
                handed(w, j).wait_recv()
        for w in range(nw):
            for k in range(2):
                first(w, k, True).wait_send()
                first(w, k, False).wait_send()
                onward(w, k).wait_send()
            for j in range(3):
                passed(w, j).wait_send()

    return _Exchange(shards, [_sds((N_CHIPS,) + s.shape, s.dtype) for s in shards],
                     [pltpu.SemaphoreType.DMA((nw, 2))] * 6 + [pltpu.SemaphoreType.DMA((nw, 3))] * 2,
                     start, finish, middle=middle)


def _ex_pair(grads):
    nw = len(grads)

    def copies(ins, outs, sems):
        x, y, c, _ = _place()
        out = []
        for w in range(nw):
            hr = grads[w].shape[1] // 2
            out.append(_remote(ins[w].at[:, pl.ds((1 - c) * hr, hr)], outs[w], sems[0].at[w], sems[1].at[w],
                               (x, y, 1 - c)))
        return out

    def start(ins, outs, sems):
        for cp in copies(ins, outs, sems):
            cp.start()

    def finish(ins, outs, sems):
        for cp in copies(ins, outs, sems):
            cp.wait()

    return _Exchange(grads, [_sds((N_CHIPS, g.shape[1] // 2, g.shape[2]), F32) for g in grads],
                     [pltpu.SemaphoreType.DMA((nw,))] * 2, start, finish)


def _ex_chip(pieces):
    nw = len(pieces)

    def copies(ins, outs, sems):
        x, y, c, chips = _place()
        return [_remote(ins[w].at[2 * cx + cy], outs[w].at[k], sems[0].at[w, k], sems[1].at[w, k], (cx, cy, c))
                for w in range(nw) for k, (cx, cy) in enumerate(chips)]

    def start(ins, outs, sems):
        for cp in copies(ins, outs, sems):
            cp.start()

    def finish(ins, outs, sems):
        for cp in copies(ins, outs, sems):
            cp.wait()

    return _Exchange(pieces, [_sds((3,) + p.shape[1:], BF16) for p in pieces],
                     [pltpu.SemaphoreType.DMA((nw, 3))] * 2, start, finish)


def _ex_swap(fulls):
    nw = len(fulls)

    def start(ins, outs, sems):
        x, y, c, _ = _place()
        for w in range(nw):
            hr = fulls[w].shape[0] // 2
            mine = pl.ds(c * hr, hr)
            _remote(ins[w].at[mine], outs[w].at[mine], sems[0].at[w], sems[1].at[w], (x, y, 1 - c)).start()

    def finish(ins, outs, sems):
        x, y, c, _ = _place()
        for w in range(nw):
            hr = fulls[w].shape[0] // 2
            mine, theirs = pl.ds(c * hr, hr), pl.ds((1 - c) * hr, hr)
            _remote(ins[w].at[mine], outs[w].at[mine], sems[0].at[w], sems[1].at[w], (x, y, 1 - c)).wait_send()
            _remote(ins[w].at[theirs], outs[w].at[theirs], sems[0].at[w], sems[1].at[w], (x, y, 1 - c)).wait_recv()

    return _Exchange(fulls, [_sds(f.shape, F32) for f in fulls], [pltpu.SemaphoreType.DMA((nw,))] * 2,
                     start, finish, aliases={w: w for w in range(nw)})


def _ex_allgather(blocks):
    nb = len(blocks)

    def copies(ins, outs, sems):
        send, recv, lsem = sems
        x, y, c, chips = _place()
        me, sibling = (x, y, c), (x, y, 1 - c)

        def rows(b, px, py, pc):
            m_per = blocks[b].shape[0]
            return outs[b].at[pl.ds((4 * px + 2 * py + pc) * m_per, m_per), :]

        def copy(b, k, blk, to, src=None):
            return _remote(rows(b, *blk) if src is None else src, rows(b, *blk), send.at[b, k], recv.at[b, k], to)

        def mine(b):
            return pltpu.make_async_copy(ins[b], rows(b, *me), lsem.at[b])

        def first(b, k):
            return copy(b, k, me, sibling if k == 0 else (*chips[k - 1], c), src=ins[b])

        def passed(b, j):
            return copy(b, 4 + j, (*chips[j], c), sibling)

        def landed(b, j):
            return copy(b, 1 + j, (*chips[j], c), me)

        def handed(b, k):
            return copy(b, 0, sibling, me) if k == 0 else copy(b, 3 + k, (*chips[k - 1], 1 - c), me)

        return mine, first, passed, landed, handed

    def start(ins, outs, sems):
        mine, first, _, _, _ = copies(ins, outs, sems)
        for b in range(nb):
            mine(b).start()
            for k in range(4):
                first(b, k).start()

    def finish(ins, outs, sems):
        mine, first, passed, landed, handed = copies(ins, outs, sems)
        sent = []
        for b in range(nb):
            for j in range(3):
                landed(b, j).wait_recv()
                cp = passed(b, j)
                cp.start()
                sent.append(cp)
        for b in range(nb):
            for k in range(4):
                handed(b, k).wait_recv()
            for k in range(4):
                first(b, k).wait_send()
        for cp in sent:
            cp.wait_send()
        for b in range(nb):
            mine(b).wait()

    return _Exchange(blocks, [_sds((N_DEV * b.shape[0], b.shape[1]), F32) for b in blocks],
                     [pltpu.SemaphoreType.DMA((nb, 7)), pltpu.SemaphoreType.DMA((nb, 7)), pltpu.SemaphoreType.DMA((nb,))],
                     start, finish)


def _cast_shards(w_up, w_down, w_bp, w_ba, w_out, exchanges=()):
    r, c = w_up.shape
    tr = HALF // 2
    steps = r // tr

    def body(up_ref, down_ref, bp_ref, ba_ref, out_ref, ua_ref, ub_ref, da_ref, db_ref, bpo_ref, bao_ref, outo_ref):
        i = pl.program_id(0)

        @pl.when(i == 0)
        def _():
            for src, dst in ((bp_ref, bpo_ref), (ba_ref, bao_ref), (out_ref, outo_ref)):
                dst[...] = src[...].astype(BF16)

        @pl.when(i < steps // 2)
        def _():
            ua_ref[...] = up_ref[...].astype(BF16)
            da_ref[...] = down_ref[...].astype(BF16)

        @pl.when(i >= steps // 2)
        def _():
            ub_ref[...] = up_ref[...].astype(BF16)
            db_ref[...] = down_ref[...].astype(BF16)

    rows = _rows(tr, c)
    first = pl.BlockSpec((tr, c), lambda i: (jnp.minimum(i, steps // 2 - 1), 0))
    second = pl.BlockSpec((tr, c), lambda i: (jnp.maximum(i - steps // 2, 0), 0))
    half = _sds((HALF, c), BF16)
    return _call(
        body, name="cast_shards", grid=(steps,),
        in_specs=[rows, rows, _const(w_bp.shape), _const(w_ba.shape), _const(w_out.shape)],
        out_specs=[first, second, first, second, _const(w_bp.shape), _const(w_ba.shape), _const(w_out.shape)],
        out_shape=[half, half, half, half, _sds(w_bp.shape, BF16), _sds(w_ba.shape, BF16), _sds(w_out.shape, BF16)],
        args=(w_up, w_down, w_bp, w_ba, w_out), sem=("arbitrary",), exchanges=exchanges)


def _inproj(x2, g1, w_in_t, b_in, tabs, seq, exchanges=()):
    T = x2.shape[0]
    tm = min(TM, seq)
    nseq = seq // tm

    def body(x_ref, g_ref, w_ref, b_ref, c_ref, a_ref, bt_ref, h_ref, u_ref, q_ref, k_ref, v_ref, gate_ref):
        x = x_ref[...]
        h = (x * _rms(x) * g_ref[...]).astype(BF16)
        h_ref[...] = h

        def proj(lo, hi):
            return _dot_nt(h, w_ref[lo:hi, :]) + b_ref[:, lo:hi]

        c, a, bt = c_ref[...], a_ref[...], bt_ref[...]
        u_ref[...] = proj(0, C_Q)
        q = proj(C_Q, C_K)
        for p in range(4):
            sl = slice(LANES * p, LANES * (p + 1))
            q_ref[:, sl] = (_rot_fwd(q[:, sl], c, a, bt) * SCALE).astype(BF16)
        kv = proj(C_K, C_G)
        k_ref[...] = _rot_fwd(kv[:, :KV_WIDTH], c, a, bt).astype(BF16)
        v_ref[...] = kv[:, KV_WIDTH:].astype(BF16)
        for j in range(2):
            lo = C_G + D_MODEL * j
            gate_ref[:, D_MODEL * j:D_MODEL * (j + 1)] = jax.nn.sigmoid(proj(lo, lo + D_MODEL)).astype(BF16)

    tab = pl.BlockSpec((tm, LANES), lambda i: (i % nseq, 0))
    return _call(
        body, name="inproj", grid=(T // tm,),
        in_specs=[_rows(tm, D_MODEL), _const((1, D_MODEL)), _const((IN_WIDTH, D_MODEL)), _const((1, IN_WIDTH)),
                  tab, tab, tab],
        out_specs=[_rows(tm, D_MODEL), _rows(tm, POOL_WIDTH), _rows(tm, ATTN_WIDTH), _rows(tm, KV_WIDTH),
                   _rows(tm, KV_WIDTH), _rows(tm, GATE_WIDTH)],
        out_shape=[_sds((T, D_MODEL), BF16), _sds((T, POOL_WIDTH), F32), _sds((T, ATTN_WIDTH), BF16),
                   _sds((T, KV_WIDTH), BF16), _sds((T, KV_WIDTH), BF16), _sds((T, GATE_WIDTH), BF16)],
        args=(x2, g1, w_in_t, b_in, *tabs), sem=("parallel",), exchanges=exchanges)


def _inv_count(pos, w):
    return 1.0 / jnp.minimum(pos + 1, w).astype(F32)


def _pool_tile(i, tp, nseq, u_ref, prev_ref, w_ref, s_ref, diff_ref, y_ref):
    first = (i % nseq) == 0
    prev = jnp.where(first, 0.0, prev_ref[...])
    ext = jnp.concatenate([prev, u_ref[...]], axis=0)
    pos = (i % nseq) * tp + lax.broadcasted_iota(jnp.int32, (tp, 1), 0)
    for gi, w in enumerate(POOL_WINDOWS):
        sl = slice(POOL_GC * gi, POOL_GC * (gi + 1))
        xg = ext[:, sl]
        s = xg
        sh = 1
        while sh < w:
            s = s + pltpu.roll(s, sh, 0)
            sh *= 2
        pooled = s[HALO:] * _inv_count(pos, w)
        diff = (pooled - xg[HALO:]).astype(BF16)
        diff_ref[:, sl] = diff
        mixed = _dot(diff, w_ref[gi].astype(BF16))
        y_ref[:, sl] = (mixed * s_ref[:, sl]).astype(BF16)


def _pool_specs(tp):
    per = tp // HALO
    return [_rows(tp, POOL_WIDTH), pl.BlockSpec((HALO, POOL_WIDTH), lambda i: (jnp.maximum(i * per - 1, 0), 0)),
            _const((4, POOL_GC, POOL_GC)), _const((1, POOL_WIDTH))]


GROUP = 4
GROWS = GROUP * BLOCK


def _attn_masks(n):
    qi = lax.broadcasted_iota(jnp.int32, (GROWS, 2 * BLOCK), 0) % BLOCK
    kj = lax.broadcasted_iota(jnp.int32, (GROWS, 2 * BLOCK), 1)
    rel = qi + BLOCK - kj
    valid = (rel >= 0) & (rel < BLOCK) & (kj >= jnp.where(n > 0, 0, BLOCK))
    lo = lax.broadcasted_iota(jnp.int32, (BLOCK, LANES), 1) < HEAD_DIM
    return valid, lo


def _by_example(bl, *arrays):
    return [a.reshape(bl, a.shape[0] // bl, a.shape[1]) for a in arrays]


def _stack_heads(ref, h, lo):
    keep = lo if h == 0 else jnp.logical_not(lo)
    pieces = []
    for p in (2 * h, 2 * h + 1):
        xp = ref[:, LANES * p:LANES * (p + 1)].astype(F32)
        for e in range(2):
            t = xp if e == h else pltpu.roll(xp, HEAD_DIM, 1)
            pieces.append(jnp.where(keep, t, 0.0).astype(BF16))
    return jnp.concatenate(pieces, axis=0)


def _unstack_heads(stacked, h, lo):
    pairs = []
    for j in range(2):
        parts = []
        for e in range(2):
            t = stacked[BLOCK * (2 * j + e):BLOCK * (2 * j + e + 1)]
            parts.append(t if e == h else pltpu.roll(t, HEAD_DIM, 1))
        pairs.append(jnp.where(lo, parts[0], parts[1]))
    return pairs


def _sink_rows(sink_ref, h):
    head = lax.broadcasted_iota(jnp.int32, (GROWS, 1), 0) // BLOCK
    col = jnp.zeros((GROWS, 1), F32) + sink_ref[GROUP * h]
    for g in range(1, GROUP):
        col = jnp.where(head == g, sink_ref[GROUP * h + g], col)
    return col


def _group_probs(qs, kk, valid, sink):
    s = jnp.where(valid, _dot_nt(qs, kk), NEG_INF)
    m = jnp.maximum(jnp.max(s, axis=1, keepdims=True), sink)
    ex = jnp.exp(s - m)
    es = jnp.exp(sink - m)
    inv = 1.0 / (jnp.sum(ex, axis=1, keepdims=True) + es)
    return ex * inv, es * inv


def _mixers_fwd(q, k, v, sinks, u, w_pool, pool_scale, seq, exchanges=()):
    T = q.shape[0]
    nb = seq // BLOCK
    bl = T // seq
    tp = T // nb
    nseq = seq // tp

    def body(sink_ref, q_ref, kp_ref, kc_ref, vp_ref, vc_ref, u_ref, prev_ref, w_ref, s_ref, o_ref, diff_ref, y_ref):
        n = pl.program_id(0)
        valid, lo = _attn_masks(n)
        for b in range(bl):
            kk = jnp.concatenate([kp_ref[b], kc_ref[b]], axis=0)
            vv = jnp.concatenate([vp_ref[b], vc_ref[b]], axis=0)
            for h in range(2):
                qs = _stack_heads(q_ref.at[b], h, lo)
                pr, _ = _group_probs(qs, kk, valid, _sink_rows(sink_ref, h))
                o = _dot(pr.astype(BF16), vv)
                for j, pair in enumerate(_unstack_heads(o, h, lo)):
                    p = 2 * h + j
                    o_ref[b, :, LANES * p:LANES * (p + 1)] = pair.astype(BF16)
        _pool_tile(n, tp, nseq, u_ref, prev_ref, w_ref, s_ref, diff_ref, y_ref)

    cur = lambda n: (0, n, 0)
    prv = lambda n: (0, jnp.maximum(n - 1, 0), 0)
    kv = lambda m: pl.BlockSpec((bl, BLOCK, KV_WIDTH), m)
    res = _call(
        body, name="mixers_fwd", grid=(nb,),
        in_specs=[pl.BlockSpec(memory_space=pltpu.SMEM), pl.BlockSpec((bl, BLOCK, ATTN_WIDTH), cur),
                  kv(prv), kv(cur), kv(prv), kv(cur)] + _pool_specs(tp),
        out_specs=[pl.BlockSpec((bl, BLOCK, ATTN_WIDTH), cur), _rows(tp, POOL_WIDTH), _rows(tp, POOL_WIDTH)],
        out_shape=[_sds((bl, seq, ATTN_WIDTH), BF16), _sds((T, POOL_WIDTH), BF16), _sds((T, POOL_WIDTH), BF16)],
        args=(sinks, *_by_example(bl, q, k, k, v, v), u, u, w_pool, pool_scale), sem=("parallel",),
        exchanges=exchanges)
    outs, rest = res if exchanges else (res, None)
    return [outs[0].reshape(T, ATTN_WIDTH), outs[1], outs[2]], rest


def _branch(y, w_ref):
    return jnp.concatenate([_dot(y, w_ref[j]) for j in range(N_CHIPS)], axis=1)


def _merge_out(y_pool, y_attn, gate, x2, w_bp, w_ba, w_out, g2, g3, exchanges=()):
    T = x2.shape[0]
    tm = min(TM, T)

    def body(yp_ref, ya_ref, gate_ref, x_ref, wbp_ref, wba_ref, wo_ref, g2_ref, g3_ref,
             mg_ref, mix_ref, x1_ref, h2_ref):
        bp, ba = _branch(yp_ref[...], wbp_ref), _branch(ya_ref[...], wba_ref)
        merged = (gate_ref[:, :D_MODEL].astype(F32) * bp + gate_ref[:, D_MODEL:].astype(F32) * ba).astype(BF16)
        mg_ref[...] = merged
        mix = _dot(merged, wo_ref[...])
        mix_ref[...] = mix
        x1 = x_ref[...] + mix * _rms(mix) * g2_ref[...]
        x1_ref[...] = x1
        h2_ref[...] = (x1 * _rms(x1) * g3_ref[...]).astype(BF16)

    return _call(
        body, name="merge_out", grid=(T // tm,),
        in_specs=[_rows(tm, POOL_WIDTH), _rows(tm, ATTN_WIDTH), _rows(tm, GATE_WIDTH), _rows(tm, D_MODEL),
                  _const(w_bp.shape), _const(w_ba.shape), _const((D_MODEL, D_MODEL)),
                  _const((1, D_MODEL)), _const((1, D_MODEL))],
        out_specs=[_rows(tm, D_MODEL)] * 4,
        out_shape=[_sds((T, D_MODEL), BF16), _sds((T, D_MODEL), F32), _sds((T, D_MODEL), F32),
                   _sds((T, D_MODEL), BF16)],
        args=(y_pool, y_attn, gate, x2, w_bp, w_ba, w_out, g2, g3), sem=("parallel",), exchanges=exchanges)


HALF = D_MODEL // 2
TM_MLP = 256


def _mlp_core(h2, x1, mix, tgt, w_up, w_down, g4, g3, g2):
    T = h2.shape[0]
    tm = min(TM_MLP, T)

    def body(h_ref, x1_ref, mix_ref, t_ref, g_ref, g3_ref, g2_ref, ua_hbm, ub_hbm, da_hbm, db_hbm,
             act_ref, dff_ref, dup_ref, dx1_ref, dmix_ref, loss_ref, dg_ref, dg3_ref, dg2_ref,
             wu, wd, relu_scr, sems):
        def weight_copy(i):
            src, dst = ((ua_hbm, wu.at[:, :HALF]), (ub_hbm, wu.at[:, HALF:]),
                        (da_hbm, wd.at[:, :HALF]), (db_hbm, wd.at[:, HALF:]))[i]
            return pltpu.make_async_copy(src, dst, sems.at[i])

        @pl.when(pl.program_id(0) == 0)
        def _():
            for i in range(4):
                weight_copy(i).start()
            loss_ref[...] = jnp.zeros_like(loss_ref)
            for ref in (dg_ref, dg3_ref, dg2_ref):
                ref[...] = jnp.zeros_like(ref)
            weight_copy(0).wait()
            weight_copy(1).wait()

        h = h_ref[...]
        ff = None
        for j in range(N_CHIPS):
            lo = D_MODEL * j
            relu = jnp.maximum(_dot(h, wu[j]), 0.0)
            if j == 0:
                @pl.when(pl.program_id(0) == 0)
                def _():
                    weight_copy(2).wait()
                    weight_copy(3).wait()
            relu_scr[:, lo:lo + D_MODEL] = relu
            act = jnp.square(relu).astype(BF16)
            act_ref[:, lo:lo + D_MODEL] = act
            t = _dot(act, wd[j])
            ff = t if ff is None else ff + t
        g = g_ref[...]
        x1 = x1_ref[...]
        err = x1 + ff * _rms(ff) * g - t_ref[...]
        loss_ref[...] += jnp.sum(err * err) * (0.5 / D_MODEL)
        dy = err * (1.0 / D_MODEL)
        dff, dg = _norm_bwd(ff, g, dy)
        dg_ref[...] += dg
        dff = dff.astype(BF16)
        dff_ref[...] = dff
        dh2 = None
        for j in range(N_CHIPS):
            lo = D_MODEL * j
            dup = (_dot_nt(dff, wd[j]) * (2.0 * relu_scr[:, lo:lo + D_MODEL])).astype(BF16)
            dup_ref[:, lo:lo + D_MODEL] = dup
            t = _dot_nt(dup, wu[j])
            dh2 = t if dh2 is None else dh2 + t
        dx, dg3 = _norm_bwd(x1, g3_ref[...], dh2)
        dx1 = dy + dx
        dx1_ref[...] = dx1
        dg3_ref[...] += dg3
        dmix, dg2 = _norm_bwd(mix_ref[...], g2_ref[...], dx1)
        dmix_ref[...] = dmix.astype(BF16)
        dg2_ref[...] += dg2

    slabs = pltpu.VMEM((N_CHIPS, D_MODEL, D_MODEL), BF16)
    gain = _const((1, D_MODEL))
    return pl.pallas_call(
        body, name="mlp_core", grid=(T // tm,),
        in_specs=[_rows(tm, D_MODEL)] * 4 + [gain] * 3 + [ANY] * 4,
        out_specs=[_rows(tm, D_FF), _rows(tm, D_MODEL), _rows(tm, D_FF), _rows(tm, D_MODEL), _rows(tm, D_MODEL),
                   _const((8, LANES)), gain, gain, gain],
        out_shape=[_sds((T, D_FF), BF16), _sds((T, D_MODEL), BF16), _sds((T, D_FF), BF16), _sds((T, D_MODEL), F32),
                   _sds((T, D_MODEL), BF16), _sds((8, LANES), F32)] + [_sds((1, D_MODEL), F32)] * 3,
        scratch_shapes=[slabs] * 2 + [pltpu.VMEM((tm, D_FF), F32), pltpu.SemaphoreType.DMA((4,))],
        compiler_params=_cp("arbitrary"),
    )(h2, x1, mix, tgt, g4, g3, g2, *w_up, *w_down)


def _dw(tag, a, g, ta, tn, shard_cols=False, exchanges=()):
    T, ka = a.shape
    n = g.shape[1]
    tk = min(2 * TM, T)
    nk = T // tk

    def body(a_ref, g_ref, o_ref):
        @pl.when(pl.program_id(2) == 0)
        def _():
            o_ref[...] = jnp.zeros_like(o_ref)

        o_ref[...] += _dot_tn(a_ref[...], g_ref[...])

    if shard_cols:
        per = (n // N_CHIPS) // tn
        out_spec = pl.BlockSpec((None, ta, tn), lambda i, j, k: (j // per, i, j % per))
        out_shape = _sds((N_CHIPS, ka, n // N_CHIPS), F32)
    else:
        out_spec = pl.BlockSpec((ta, tn), lambda i, j, k: (i, j))
        out_shape = _sds((ka, n), F32)
    return _call(
        body, name="dw_" + tag, grid=(ka // ta, n // tn, nk),
        in_specs=[pl.BlockSpec((tk, ta), lambda i, j, k: (k, i)), pl.BlockSpec((tk, tn), lambda i, j, k: (k, j))],
        out_specs=[out_spec], out_shape=[out_shape],
        args=(a, g), sem=("parallel", "parallel", "arbitrary"), exchanges=exchanges)


def _dw_mix(merged, dmix, y_pool, dbp, y_attn, dba, exchanges=()):
    T = merged.shape[0]
    tk = min(2 * TM, T)
    c = D_MODEL // N_CHIPS

    def body(mg_ref, dmix_ref, yp_ref, dbp_ref, ya_ref, dba_ref, out_ref, bp_ref, ba_ref):
        @pl.when(pl.program_id(0) == 0)
        def _():
            for ref in (out_ref, bp_ref, ba_ref):
                ref[...] = jnp.zeros_like(ref)

        out_ref[...] += _dot_tn(mg_ref[...], dmix_ref[...])
        for y_ref, d_ref, o_ref in ((yp_ref, dbp_ref, bp_ref), (ya_ref, dba_ref, ba_ref)):
            res = _dot_tn(y_ref[...], d_ref[...])
            for j in range(N_CHIPS):
                o_ref[j] += res[:, c * j:c * (j + 1)]

    slabs = (N_CHIPS, POOL_WIDTH, c)
    return _call(
        body, name="dw_mix", grid=(T // tk,),
        in_specs=[_rows(tk, D_MODEL), _rows(tk, D_MODEL), _rows(tk, POOL_WIDTH), _rows(tk, D_MODEL),
                  _rows(tk, ATTN_WIDTH), _rows(tk, D_MODEL)],
        out_specs=[_const((D_MODEL, D_MODEL)), _const(slabs), _const(slabs)],
        out_shape=[_sds((D_MODEL, D_MODEL), F32), _sds(slabs, F32), _sds(slabs, F32)],
        args=(merged, dmix, y_pool, dbp, y_attn, dba), sem=("arbitrary",), exchanges=exchanges)


def _merge_bwd(dmix, gate, y_pool, y_attn, w_out, w_bp, w_ba, exchanges=()):
    T = dmix.shape[0]
    tm = min(TM, T)

    def body(dmix_ref, gate_ref, yp_ref, ya_ref, wo_ref, wbp_ref, wba_ref,
             dbp_ref, dba_ref, dgate_ref, dyp_ref, dya_ref):
        dm = _dot_nt(dmix_ref[...], wo_ref[...])
        for j, (y_ref, db_ref, w_ref, dy_ref) in enumerate(
                ((yp_ref, dbp_ref, wbp_ref, dyp_ref), (ya_ref, dba_ref, wba_ref, dya_ref))):
            sl = slice(D_MODEL * j, D_MODEL * (j + 1))
            gt = gate_ref[:, sl].astype(F32)
            db = (dm * gt).astype(BF16)
            db_ref[...] = db
            dgate_ref[:, sl] = (dm * _branch(y_ref[...], w_ref) * gt * (1.0 - gt)).astype(BF16)
            cw = D_MODEL // N_CHIPS
            dy = _dot_nt(db[:, :cw], w_ref[0])
            for c in range(1, N_CHIPS):
                dy = dy + _dot_nt(db[:, cw * c:cw * (c + 1)], w_ref[c])
            dy_ref[...] = dy.astype(dy_ref.dtype)

    return _call(
        body, name="merge_bwd", grid=(T // tm,),
        in_specs=[_rows(tm, D_MODEL), _rows(tm, GATE_WIDTH), _rows(tm, POOL_WIDTH), _rows(tm, ATTN_WIDTH),
                  _const((D_MODEL, D_MODEL)), _const(w_bp.shape), _const(w_ba.shape)],
        out_specs=[_rows(tm, D_MODEL), _rows(tm, D_MODEL), _rows(tm, GATE_WIDTH), _rows(tm, POOL_WIDTH),
                   _rows(tm, ATTN_WIDTH)],
        out_shape=[_sds((T, D_MODEL), BF16), _sds((T, D_MODEL), BF16), _sds((T, GATE_WIDTH), BF16),
                   _sds((T, POOL_WIDTH), F32), _sds((T, ATTN_WIDTH), BF16)],
        args=(dmix, gate, y_pool, y_attn, w_out, w_bp, w_ba), sem=("parallel",), exchanges=exchanges)


def _mixers_bwd(q, k, v, do, sinks, tabs, dyp, diff, w_pool, pool_scale, seq, exchanges=()):
    T = q.shape[0]
    nb = seq // BLOCK
    bl = T // seq
    steps = nb + 1
    tp = T // nb
    nseq = seq // tp
    per = tp // HALO
    last_halo = T // HALO - 1

    def body(sink_ref, q_ref, do_ref, kp_ref, kc_ref, vp_ref, vc_ref, c_ref, a_ref, bt_ref, cp_ref, ap_ref, btp_ref,
             dy_ref, nxt_ref, diff_ref, w_ref, s_ref,
             dq_ref, dk_ref, dv_ref, dsink_ref, du_ref, dw_ref, ds_ref, ck_ref, cv_ref):
        n = pl.program_id(0)

        @pl.when(n == 0)
        def _():
            for ref in (dsink_ref, ck_ref, cv_ref, dw_ref, ds_ref):
                ref[...] = jnp.zeros_like(ref)

        @pl.when(n < nb)
        def _():
            _pool_bwd_tile(n, tp, nseq, dy_ref, nxt_ref, diff_ref, w_ref, s_ref, du_ref, dw_ref, ds_ref)
            valid, lo = _attn_masks(n)
            for b in range(bl):
                kk = jnp.concatenate([kp_ref[b], kc_ref[b]], axis=0)
                vv = jnp.concatenate([vp_ref[b], vc_ref[b]], axis=0)
                dk_acc = jnp.zeros((2 * BLOCK, KV_WIDTH), F32)
                dv_acc = jnp.zeros((2 * BLOCK, KV_WIDTH), F32)
                for h in range(2):
                    qs = _stack_heads(q_ref.at[b], h, lo)
                    dos = _stack_heads(do_ref.at[b], h, lo)
                    pr, ps = _group_probs(qs, kk, valid, _sink_rows(sink_ref, h))
                    dp = _dot_nt(dos, vv)
                    delta = jnp.sum(pr * dp, axis=1, keepdims=True)
                    ds = (pr * (dp - delta)).astype(BF16)
                    dsk = ps * delta
                    for g in range(GROUP):
                        idx = GROUP * h + g
                        dsink_ref[idx:idx + 1, :] += (jnp.zeros((1, LANES), F32)
                                                      - jnp.sum(dsk[BLOCK * g:BLOCK * (g + 1)]))
                    dk_acc = dk_acc + _dot_tn(ds, qs)
                    dv_acc = dv_acc + _dot_tn(pr.astype(BF16), dos)
                    for j, pair in enumerate(_unstack_heads(_dot(ds, kk) * SCALE, h, lo)):
                        sl = slice(LANES * (2 * h + j), LANES * (2 * h + j + 1))
                        dq_ref[b, :, sl] = _rot_bwd(pair, c_ref[...], a_ref[...], bt_ref[...]).astype(BF16)
                fin_k = ck_ref[b] + dk_acc[:BLOCK]
                dk_ref[b] = _rot_bwd(fin_k, cp_ref[...], ap_ref[...], btp_ref[...]).astype(BF16)
                dv_ref[b] = (cv_ref[b] + dv_acc[:BLOCK]).astype(BF16)
                ck_ref[b] = dk_acc[BLOCK:]
                cv_ref[b] = dv_acc[BLOCK:]

        @pl.when(n == nb)
        def _():
            for b in range(bl):
                dk_ref[b] = _rot_bwd(ck_ref[b], cp_ref[...], ap_ref[...], btp_ref[...]).astype(BF16)
                dv_ref[b] = cv_ref[b].astype(BF16)

    cur = lambda n: (0, jnp.minimum(n, nb - 1), 0)
    prv = lambda n: (0, jnp.clip(n - 1, 0, nb - 1), 0)
    tcur = lambda n: (jnp.minimum(n, nb - 1), 0)
    tprv = lambda n: (jnp.clip(n - 1, 0, nb - 1), 0)
    wide = lambda m: pl.BlockSpec((bl, BLOCK, ATTN_WIDTH), m)
    kv = lambda m: pl.BlockSpec((bl, BLOCK, KV_WIDTH), m)
    tab = lambda m: pl.BlockSpec((BLOCK, LANES), m)
    tile = lambda n: (jnp.minimum(n, nb - 1), 0)
    halo = lambda n: (jnp.minimum((jnp.minimum(n, nb - 1) + 1) * per, last_halo), 0)
    rows = pl.BlockSpec((tp, POOL_WIDTH), tile)
    res = _call(
        body, name="mixers_bwd", grid=(steps,),
        in_specs=[pl.BlockSpec(memory_space=pltpu.SMEM), wide(cur), wide(cur), kv(prv), kv(cur), kv(prv), kv(cur),
                  tab(tcur), tab(tcur), tab(tcur), tab(tprv), tab(tprv), tab(tprv),
                  rows, pl.BlockSpec((HALO, POOL_WIDTH), halo), rows, _const((4, POOL_GC, POOL_GC)),
                  _const((1, POOL_WIDTH))],
        out_specs=[wide(cur), kv(prv), kv(prv), _const((8, LANES)), rows, _const((4, POOL_GC, POOL_GC)),
                   _const((1, POOL_WIDTH))],
        out_shape=[_sds((bl, seq, ATTN_WIDTH), BF16), _sds((bl, seq, KV_WIDTH), BF16),
                   _sds((bl, seq, KV_WIDTH), BF16), _sds((8, LANES), F32), _sds((T, POOL_WIDTH), BF16),
                   _sds((4, POOL_GC, POOL_GC), F32), _sds((1, POOL_WIDTH), F32)],
        scratch=[pltpu.VMEM((bl, BLOCK, KV_WIDTH), F32), pltpu.VMEM((bl, BLOCK, KV_WIDTH), F32)],
        args=(sinks, *_by_example(bl, q, do, k, k, v, v), *tabs, *tabs, dyp, dyp, diff, w_pool, pool_scale),
        sem=("arbitrary",), exchanges=exchanges)
    outs, rest = (res if exchanges else (res, None))
    outs = [outs[0].reshape(T, ATTN_WIDTH), outs[1].reshape(T, KV_WIDTH), outs[2].reshape(T, KV_WIDTH), *outs[3:]]
    return (outs, rest) if exchanges else outs


def _pool_bwd_tile(i, tp, nseq, dy_ref, nxt_ref, diff_ref, w_ref, s_ref, du_ref, dw_ref, ds_ref):
    last = (i % nseq) == nseq - 1
    nxt = jnp.where(last, 0.0, nxt_ref[...])
    ext = jnp.concatenate([dy_ref[...], nxt], axis=0) * s_ref[...]
    pos = (i % nseq) * tp + lax.broadcasted_iota(jnp.int32, (tp + HALO, 1), 0)
    for gi, w in enumerate(POOL_WINDOWS):
        sl = slice(POOL_GC * gi, POOL_GC * (gi + 1))
        wg = w_ref[gi].astype(BF16)
        dmx = ext[:, sl].astype(BF16)
        ddiff = _dot_nt(dmx, wg)
        s = ddiff * _inv_count(pos, w)
        sh = 1
        while sh < w:
            s = s + pltpu.roll(s, tp + HALO - sh, 0)
            sh *= 2
        du_ref[:, sl] = (s[:tp] - ddiff[:tp]).astype(BF16)
        dg = diff_ref[:, sl]
        dw_ref[gi] += _dot_tn(dg, dmx[:tp])
        ds_ref[:, sl] += jnp.sum(dy_ref[:, sl] * _dot(dg, wg), axis=0, keepdims=True)


_PARTS = ((0, C_Q), (C_Q, C_K), (C_K, C_V), (C_V, C_G), (C_G, IN_WIDTH))


def _inproj_bwd(parts, x2, dx1, w_in_t, g1, exchanges=()):
    T = x2.shape[0]
    tm = min(TM, T)

    def body(du_ref, dq_ref, dk_ref, dv_ref, dgt_ref, x_ref, dx1_ref, w_ref, g_ref, gx_ref, dg_ref):
        @pl.when(pl.program_id(0) == 0)
        def _():
            dg_ref[...] = jnp.zeros_like(dg_ref)

        dh = jnp.zeros((tm, D_MODEL), F32)
        for (lo, hi), p_ref in zip(_PARTS, (du_ref, dq_ref, dk_ref, dv_ref, dgt_ref)):
            dh = dh + _dot(p_ref[...], w_ref[lo:hi, :])
        dx, dg = _norm_bwd(x_ref[...], g_ref[...], dh)
        gx_ref[...] = dx1_ref[...] + dx
        dg_ref[...] += dg

    return _call(
        body, name="inproj_bwd", grid=(T // tm,),
        in_specs=[_rows(tm, hi - lo) for lo, hi in _PARTS]
        + [_rows(tm, D_MODEL), _rows(tm, D_MODEL), _const((IN_WIDTH, D_MODEL)), _const((1, D_MODEL))],
        out_specs=[_rows(tm, D_MODEL), _const((1, D_MODEL))],
        out_shape=[_sds((T, D_MODEL), F32), _sds((1, D_MODEL), F32)],
        args=(*parts, x2, dx1, w_in_t, g1), sem=("arbitrary",), exchanges=exchanges)


def _dw_in(h, parts, exchanges=()):
    T = h.shape[0]
    tk = min(TM, T)

    def body(h_ref, du_ref, dq_ref, dk_ref, dv_ref, dgt_ref, o_ref, db_ref):
        @pl.when(pl.program_id(0) == 0)
        def _():
            o_ref[...] = jnp.zeros_like(o_ref)
            db_ref[...] = jnp.zeros_like(db_ref)

        hh = h_ref[...]
        ones = jnp.ones((8, tk), BF16)
        for (lo, hi), p_ref in zip(_PARTS, (du_ref, dq_ref, dk_ref, dv_ref, dgt_ref)):
            part = p_ref[...]
            o_ref[lo:hi, :] += _dot_tn(part, hh)
            db_ref[:, lo:hi] += _dot(ones, part)[:1]

    return _call(
        body, name="dw_in", grid=(T // tk,),
        in_specs=[_rows(tk, D_MODEL)] + [_rows(tk, hi - lo) for lo, hi in _PARTS],
        out_specs=[_const((IN_WIDTH, D_MODEL)), _const((1, IN_WIDTH))],
        out_shape=[_sds((IN_WIDTH, D_MODEL), F32), _sds((1, IN_WIDTH), F32)],
        args=(h, *parts), sem=("arbitrary",), exchanges=exchanges)


def _row_tile(rows, cap=256, mult=16):
    best = None
    for t in range(mult, min(rows, cap) + 1, mult):
        if rows % t == 0:
            best = t
    if best is None:
        raise ValueError("no row tile for %d rows" % rows)
    return best


def _pair_sum(ids, full, got):
    _, r, c = full.shape
    hr = r // 2
    tr = _row_tile(hr)
    nblk = hr // tr

    def body(ids_ref, a_ref, b_ref, own_ref, sb_ref):
        s = a_ref[...] + b_ref[...]
        sb_ref[...] = s.astype(BF16)

        @pl.when(pl.program_id(1) == ids_ref[0])
        def _():
            own_ref[...] = s

    slab = pl.BlockSpec((None, tr, c), lambda i, j, ids_ref: (j, i, 0))
    return pl.pallas_call(
        body, name="pair_sum_%dx%d" % (r, c),
        grid_spec=pltpu.PrefetchScalarGridSpec(
            num_scalar_prefetch=1, grid=(nblk, N_CHIPS),
            in_specs=[pl.BlockSpec((None, tr, c), lambda i, j, ids_ref: (j, ids_ref[1] * nblk + i, 0)), slab],
            out_specs=[pl.BlockSpec((tr, c), lambda i, j, ids_ref: (i, 0)), slab]),
        out_shape=[_sds((hr, c), F32), _sds((N_CHIPS, hr, c), BF16)],
        compiler_params=_cp("parallel", "arbitrary"),
    )(ids, full, got)


def _pair_sum_small(ids, fulls, gots):
    n = len(fulls)
    dims = [(f.shape[1] // 2, f.shape[2]) for f in fulls]

    def body(ids_ref, *refs):
        ins, outs = refs[:2 * n], refs[2 * n:]
        for k in range(n):
            s = ins[2 * k][...] + ins[2 * k + 1][...]
            outs[2 * k + 1][...] = s.astype(BF16)

            @pl.when(pl.program_id(0) == ids_ref[0])
            def _(k=k, s=s):
                outs[2 * k][...] = s

    in_specs, out_specs, out_shape = [], [], []
    for hr, c in dims:
        slab = pl.BlockSpec((None, hr, c), lambda j, ids_ref: (j, 0, 0))
        in_specs += [pl.BlockSpec((None, hr, c), lambda j, ids_ref: (j, ids_ref[1], 0)), slab]
        out_specs += [pl.BlockSpec((hr, c), lambda j, ids_ref: (0, 0)), slab]
        out_shape += [_sds((hr, c), F32), _sds((N_CHIPS, hr, c), BF16)]
    res = pl.pallas_call(
        body, name="pair_sum_small",
        grid_spec=pltpu.PrefetchScalarGridSpec(num_scalar_prefetch=1, grid=(N_CHIPS,), in_specs=in_specs,
                                               out_specs=out_specs),
        out_shape=out_shape, compiler_params=_cp("arbitrary"),
    )(ids, *[a for pair in zip(fulls, gots) for a in pair])
    return [(res[2 * k], res[2 * k + 1]) for k in range(n)]


def _chip_sum_small(ids, owns, gots):
    n = len(owns)

    def body(ids_ref, *refs):
        ins, outs = refs[:2 * n], refs[2 * n:]
        for k in range(n):
            a, b = ins[2 * k], ins[2 * k + 1]
            outs[k][...] = ((a[...] + b[0].astype(F32)) + b[1].astype(F32)) + b[2].astype(F32)

    in_specs, out_specs, out_shape = [], [], []
    for own in owns:
        hr, c = own.shape
        in_specs += [pl.BlockSpec((hr, c), lambda i, ids_ref: (0, 0)),
                     pl.BlockSpec((3, hr, c), lambda i, ids_ref: (0, 0, 0))]
        out_specs.append(pl.BlockSpec((hr, c), lambda i, ids_ref: (ids_ref[1], 0)))
        out_shape.append(_sds((2 * hr, c), F32))
    return pl.pallas_call(
        body, name="chip_sum_small",
        grid_spec=pltpu.PrefetchScalarGridSpec(num_scalar_prefetch=1, grid=(1,), in_specs=in_specs,
                                               out_specs=out_specs),
        out_shape=out_shape, compiler_params=_cp("arbitrary"),
    )(ids, *[a for pair in zip(owns, gots) for a in pair])


def _chip_sum(ids, own, got):
    hr, c = own.shape
    tr = _row_tile(hr)
    nblk = hr // tr

    def body(ids_ref, a_ref, b_ref, o_ref):
        o_ref[...] = ((a_ref[...] + b_ref[0].astype(F32)) + b_ref[1].astype(F32)) + b_ref[2].astype(F32)

    return pl.pallas_call(
        body, name="chip_sum_%dx%d" % (hr, c),
        grid_spec=pltpu.PrefetchScalarGridSpec(
            num_scalar_prefetch=1, grid=(nblk,),
            in_specs=[pl.BlockSpec((tr, c), lambda i, ids_ref: (i, 0)),
                      pl.BlockSpec((3, tr, c), lambda i, ids_ref: (0, i, 0))],
            out_specs=pl.BlockSpec((tr, c), lambda i, ids_ref: (ids_ref[1] * nblk + i, 0))),
        out_shape=_sds((2 * hr, c), F32),
        compiler_params=_cp("parallel"),
    )(ids, own, got)


def _adamw_math(w, g, m, v):
    nm = ADAM_B1 * m + (1.0 - ADAM_B1) * g
    nv = ADAM_B2 * v + (1.0 - ADAM_B2) * (g * g)
    m_hat = nm / (1.0 - ADAM_B1 ** ADAM_STEP)
    v_hat = nv / (1.0 - ADAM_B2 ** ADAM_STEP)
    return -ADAM_LR * (m_hat / (jnp.sqrt(v_hat) + ADAM_EPS) + ADAM_WD * w), nm, nv


def _adamw(w, g, m, v):
    r, c = w.shape
    tr = _row_tile(r, cap=512, mult=8)

    def body(w_ref, g_ref, m_ref, v_ref, d_ref, nm_ref, nv_ref):
        d_ref[...], nm_ref[...], nv_ref[...] = _adamw_math(w_ref[...], g_ref[...], m_ref[...], v_ref[...])

    spec = _rows(tr, c)
    return pl.pallas_call(
        body, name="adamw_%dx%d" % (r, c), grid=(r // tr,),
        in_specs=[spec] * 4, out_specs=[spec] * 3, out_shape=[_sds((r, c), F32)] * 3,
        compiler_params=_cp("parallel"),
    )(w, g, m, v)


SC_TILES = 32
SC_LANES = 16
SC_ROWS = 8


def _adamw_sparse(w, g, m, v):
    r, c = w.shape
    rows = r // SC_TILES
    step = min(rows, SC_ROWS)

    def body(w_hbm, g_hbm, m_hbm, v_hbm, d_hbm, nm_hbm, nv_hbm, wb, gb, mb, vb):
        tile = lax.axis_index("sc_subcore") * 2 + lax.axis_index("sc_core")

        @pl.loop(0, rows, step=step)
        def _(r0):
            mine = pl.ds(tile * rows + r0, step)
            for src, dst in ((w_hbm, wb), (g_hbm, gb), (m_hbm, mb), (v_hbm, vb)):
                pltpu.sync_copy(src.at[mine], dst)

            @pl.loop(0, step)
            def _(row):
                @pl.loop(0, c, step=SC_LANES)
                def _(i):
                    at = (row, pl.ds(i, SC_LANES))
                    wb[at], mb[at], vb[at] = _adamw_math(wb[at], gb[at], mb[at], vb[at])

            for src, dst in ((wb, d_hbm), (mb, nm_hbm), (vb, nv_hbm)):
                pltpu.sync_copy(src, dst.at[mine])

    return pl.kernel(
        body, name="adamw_sparse_%dx%d" % (r, c), out_type=[_sds((r, c), F32)] * 3,
        mesh=plsc.VectorSubcoreMesh(core_axis_name="sc_core", subcore_axis_name="sc_subcore"),
        scratch_types=[pltpu.VMEM((step, c), F32)] * 4,
    )(w, g, m, v)


_SMALL_NAMES = ("w_pool", "b_in", "g_mix_pre", "g_mix_post", "g_mlp_pre", "g_mlp_post", "pool_scale", "attn_sinks")
B_ROWS = -(-IN_WIDTH // D_MODEL)


def _row_block(rows):
    rows = [jnp.pad(r.astype(F32), ((0, 0), (0, D_MODEL - r.shape[1]))) for r in rows]
    return jnp.pad(jnp.concatenate(rows, axis=0), ((0, 8 - len(rows)), (0, 0)))


def _early_block(dg2, dg3, dg4, dps, dsink, loss):
    tail = jnp.concatenate([jnp.pad(dsink.reshape(1, -1), ((0, 0), (0, LANES - dsink.size))),
                            jnp.pad(loss.reshape(1, 1), ((0, 0), (0, LANES - 1)))], axis=1)
    return _row_block([dg2, dg3, dg4, dps, tail])


def _late_block(db_in, dg1):
    b = jnp.pad(db_in, ((0, 0), (0, B_ROWS * D_MODEL - IN_WIDTH))).reshape(B_ROWS, D_MODEL)
    return _row_block([b[r:r + 1] for r in range(B_ROWS)] + [dg1])


def _small_update(gearly, gmat, glate, w, m, v):
    names = _SMALL_NAMES
    n = len(names)

    def total(ref, rows):
        acc = ref[0:rows, :]
        for d in range(1, N_DEV):
            acc = acc + ref[d * rows:(d + 1) * rows, :]
        return acc

    def body(*refs):
        early_ref, gmat_ref, late_ref = refs[:3]
        w_refs, m_refs, v_refs = refs[3:3 + n], refs[3 + n:3 + 2 * n], refs[3 + 2 * n:3 + 3 * n]
        outs = refs[3 + 3 * n:]
        loss_ref, g_refs, d_refs = outs[0], outs[1:1 + n], outs[1 + n:1 + 2 * n]
        nm_refs, nv_refs = outs[1 + 2 * n:1 + 3 * n], outs[1 + 3 * n:1 + 4 * n]
        early, late = total(early_ref, 8), total(late_ref, 8)
        loss_ref[...] = jnp.sum(early[4:5, LANES:2 * LANES], axis=1, keepdims=True)
        bias = jnp.concatenate([late[r:r + 1, :] for r in range(B_ROWS - 1)]
                               + [late[B_ROWS - 1:B_ROWS, :IN_WIDTH - (B_ROWS - 1) * D_MODEL]], axis=1)
        grad = dict(b_in=bias, g_mix_pre=late[B_ROWS:B_ROWS + 1, :], g_mix_post=early[0:1, :],
                    g_mlp_pre=early[1:2, :], g_mlp_post=early[2:3, :], pool_scale=early[3:4, :POOL_WIDTH],
                    attn_sinks=early[4:5, :N_Q_HEADS])
        for i, name in enumerate(names):
            g = total(gmat_ref, 4 * POOL_GC) if name == "w_pool" else grad[name]
            g_refs[i][...] = g
            d_refs[i][...], nm_refs[i][...], nv_refs[i][...] = _adamw_math(
                w_refs[i][...], g, m_refs[i][...], v_refs[i][...])

    shapes = [_sds(w[k].shape, F32) for k in names]
    res = pl.pallas_call(
        body, name="small_update", out_shape=[_sds((1, 1), F32)] + shapes * 4,
        compiler_params=pltpu.CompilerParams(vmem_limit_bytes=VMEM_MB * 1024 * 1024),
    )(gearly, gmat, glate, *[w[k] for k in names], *[m[k] for k in names], *[v[k] for k in names])
    loss = res[0]
    per = {k: tuple(res[1 + j * n + i] for j in range(4)) for i, k in enumerate(names)}
    return loss, per


_BIG = ("w_in", "w_branch_pool", "w_branch_attn", "w_out", "w_up", "w_down")
_ORDER = ("g_mix_pre", "w_in", "b_in", "w_pool", "pool_scale", "attn_sinks", "w_branch_pool", "w_branch_attn",
          "w_out", "g_mix_post", "g_mlp_pre", "w_up", "w_down", "g_mlp_post")


def _stack_rows(slab):
    return slab.reshape(-1, slab.shape[2])


def _step(x2, tgt, seq, shards, small, ids):
    tabs = _rope_tables(seq)
    g1, g2, g3, g4 = (small[n] for n in ("g_mix_pre", "g_mix_post", "g_mlp_pre", "g_mlp_post"))
    sinks = small["attn_sinks"].reshape(N_Q_HEADS)
    w_pool = small["w_pool"].reshape(4, POOL_GC, POOL_GC)
    pool_scale = small["pool_scale"]

    def whole(shard, slabs):
        return lax.dynamic_update_slice(slabs, shard[None], (ids[0], 0, 0))

    (up_a, up_b, down_a, down_b, *mix_shards), [[in_slab]] = _cast_shards(
        *(shards[n] for n in ("w_up", "w_down", "w_branch_pool", "w_branch_attn", "w_out")),
        exchanges=[_ex_gather([shards["w_in"]])])
    w_in = _stack_rows(whole(shards["w_in"], in_slab))
    (h, u, q, k, v, gate), [mix_slabs] = _inproj(
        x2, g1, w_in, small["b_in"], tabs, seq, exchanges=[_ex_gather(mix_shards)])
    w_bp, w_ba, out_slab = (whole(s, g) for s, g in zip(mix_shards, mix_slabs))
    w_out = _stack_rows(out_slab)
    (y_attn, diff, y_pool), [[got_a, got_b]] = _mixers_fwd(
        q, k, v, sinks, u, w_pool, pool_scale, seq, exchanges=[_ex_gather([up_a, up_b])])
    (merged, mix, x1, h2), [[got_c, got_d]] = _merge_out(
        y_pool, y_attn, gate, x2, w_bp, w_ba, w_out, g2, g3, exchanges=[_ex_gather([down_a, down_b])])
    w_up = (whole(up_a, got_a), whole(up_b, got_b))
    w_down = (whole(down_a, got_c), whole(down_b, got_d))
    act, dff, dup, dx1, dmix, loss_acc, dg4, dg3, dg2 = _mlp_core(h2, x1, mix, tgt, w_up, w_down, g4, g3, g2)

    dw_down = _dw("down", act, dff, 2048, 1024)[0].reshape(N_CHIPS, D_FF // N_CHIPS, D_MODEL)
    (dbp, dba, dgate, dyp, dya), [[got]] = _merge_bwd(
        dmix, gate, y_pool, y_attn, w_out, w_bp, w_ba, exchanges=[_ex_pair([dw_down])])
    ps_down = _pair_sum(ids, dw_down, got)
    (dw_up,), [[got]] = _dw("up", h2, dup, 1024, 1024, shard_cols=True, exchanges=[_ex_chip([ps_down[1]])])
    half_down = _chip_sum(ids, ps_down[0], got)
    (dw_out, dw_bp, dw_ba), [[got]] = _dw_mix(merged, dmix, y_pool, dbp, y_attn, dba, exchanges=[_ex_pair([dw_up])])
    ps_up = _pair_sum(ids, dw_up, got)
    dw_mix = [dw_out.reshape(N_CHIPS, D_MODEL // N_CHIPS, D_MODEL), dw_bp, dw_ba]
    (dq, dk, dv, dsink, du, dw_pool, dps), [[got], gots, [g_down]] = _mixers_bwd(
        q, k, v, dya, sinks, tabs, dyp, diff, w_pool, pool_scale, seq,
        exchanges=[_ex_chip([ps_up[1]]), _ex_pair(dw_mix), _ex_swap([half_down])])
    half_up = _chip_sum(ids, ps_up[0], got)
    ps_mix = _pair_sum_small(ids, dw_mix, gots)
    parts = (du, dq, dk, dv, dgate)
    early = _early_block(dg2, dg3, dg4, dps, dsink[:, 0], loss_acc[0, 0])
    mat = dw_pool.reshape(4 * POOL_GC, POOL_GC)
    (dw_in_t, db_in), [gots, [gearly, gmat], [g_up]] = _dw_in(
        h, parts, exchanges=[_ex_chip([p[1] for p in ps_mix]), _ex_allgather([early, mat]), _ex_swap([half_up])])
    half_mix = _chip_sum_small(ids, [p[0] for p in ps_mix], gots)
    dw_in = dw_in_t.reshape(N_CHIPS, IN_WIDTH // N_CHIPS, D_MODEL)
    g_mix, [got] = _alone("swap_mix_pair_in", _ex_swap(half_mix), _ex_pair([dw_in]))
    ps_in = _pair_sum(ids, dw_in, got)
    (gx, dg1), [[got]] = _inproj_bwd(parts, x2, dx1, w_in, g1, exchanges=[_ex_chip([ps_in[1]])])
    [g_in], [glate] = _alone("swap_in_allgather", _ex_swap([_chip_sum(ids, ps_in[0], got)]),
                             _ex_allgather([_late_block(db_in, dg1)]))

    grads = dict(w_in=g_in, w_branch_pool=g_mix[1], w_branch_attn=g_mix[2], w_out=g_mix[0], w_up=g_up, w_down=g_down)
    return (gearly, gmat, glate), gx, grads


def kernel(x, g_mix_pre, w_in, b_in, w_pool, pool_scale, attn_sinks, w_branch_pool, w_branch_attn, w_out, g_mix_post, g_mlp_pre, w_up, w_down, g_mlp_post, loss_target, m_g_mix_pre, m_w_in, m_b_in, m_w_pool, m_pool_scale, m_attn_sinks, m_w_branch_pool, m_w_branch_attn, m_w_out, m_g_mix_post, m_g_mlp_pre, m_w_up, m_w_down, m_g_mlp_post, v_g_mix_pre, v_w_in, v_b_in, v_w_pool, v_pool_scale, v_attn_sinks, v_w_branch_pool, v_w_branch_attn, v_w_out, v_g_mix_post, v_g_mlp_pre, v_w_up, v_w_down, v_g_mlp_post):
    weights = dict(g_mix_pre=g_mix_pre, w_in=w_in, b_in=b_in, w_pool=w_pool, pool_scale=pool_scale,
                   attn_sinks=attn_sinks, w_branch_pool=w_branch_pool, w_branch_attn=w_branch_attn, w_out=w_out,
                   g_mix_post=g_mix_post, g_mlp_pre=g_mlp_pre, w_up=w_up, w_down=w_down, g_mlp_post=g_mlp_post)
    mom1 = dict(g_mix_pre=m_g_mix_pre, w_in=m_w_in, b_in=m_b_in, w_pool=m_w_pool, pool_scale=m_pool_scale,
                attn_sinks=m_attn_sinks, w_branch_pool=m_w_branch_pool, w_branch_attn=m_w_branch_attn,
                w_out=m_w_out, g_mix_post=m_g_mix_post, g_mlp_pre=m_g_mlp_pre, w_up=m_w_up, w_down=m_w_down,
                g_mlp_post=m_g_mlp_post)
    mom2 = dict(g_mix_pre=v_g_mix_pre, w_in=v_w_in, b_in=v_b_in, w_pool=v_w_pool, pool_scale=v_pool_scale,
                attn_sinks=v_attn_sinks, w_branch_pool=v_w_branch_pool, w_branch_attn=v_w_branch_attn,
                w_out=v_w_out, g_mix_post=v_g_mix_post, g_mlp_pre=v_g_mlp_pre, w_up=v_w_up, w_down=v_w_down,
                g_mlp_post=v_g_mlp_post)
    b_loc, seq, _ = x.shape
    x2 = x.reshape(b_loc * seq, D_MODEL)
    tgt = loss_target.reshape(b_loc * seq, D_MODEL)
    ids = jnp.stack([2 * lax.axis_index("x") + lax.axis_index("y"), lax.axis_index("c")]).astype(jnp.int32)

    def flat(n, a):
        return a[0].T if n == "w_in" else a[0]

    def unflat(n, a):
        return (a.T if n == "w_in" else a)[None]

    shards = {n: flat(n, weights[n]).astype(BF16) if n == "w_in" else flat(n, weights[n]) for n in _BIG}
    small = {n: weights[n] for n in _ORDER if n not in _BIG}
    (gearly, gmat, glate), gx, grads = _step(x2, tgt, seq, shards, small, ids)

    def two_d(src):
        return {n: src[n].reshape(4 * POOL_GC, POOL_GC) if n == "w_pool" else src[n] for n in _SMALL_NAMES}

    loss, per = _small_update(gearly, gmat, glate, two_d(weights), two_d(mom1), two_d(mom2))
    delta, new_m, new_v = {}, {}, {}
    for n in _SMALL_NAMES:
        grads[n], delta[n], new_m[n], new_v[n] = (a.reshape(weights[n].shape) for a in per[n])
    for n in _BIG:
        update = _adamw if n == "w_in" else _adamw_sparse
        d, nm, nv = update(flat(n, weights[n]), grads[n], flat(n, mom1[n]), flat(n, mom2[n]))
        grads[n] = unflat(n, grads[n])
        delta[n], new_m[n], new_v[n] = unflat(n, d), unflat(n, nm), unflat(n, nv)

    return (loss[0, 0], gx.reshape(x.shape), *[grads[n] for n in _ORDER], *[delta[n] for n in _ORDER],
            *[new_m[n] for n in _ORDER], *[new_v[n] for n in _ORDER])
```

```python
import jax
import jax.numpy as jnp
from jax import lax
from jax.experimental import pallas as pl
from jax.experimental.pallas import tpu as pltpu
from jax.experimental.pallas import tpu_sc as plsc

F32 = jnp.float32
BF16 = jnp.bfloat16

D_MODEL = 1024
POOL_WINDOWS = (2, 4, 8, 16)
POOL_WIDTH = 512
POOL_GC = 128
HALO = 16
HEAD_DIM = 64
N_Q_HEADS = 8
ATTN_WIDTH = 512
KV_WIDTH = 128
BLOCK = 128
NEG_INF = -1e30
ROPE_THETA = 500000.0
ROT_DIM = 16
GATE_WIDTH = 2048
IN_WIDTH = 3328
D_FF = 4096
EPS = 1e-6
SCALE = HEAD_DIM ** -0.5
C_Q, C_K, C_V, C_G = 512, 1024, 1152, 1280

ADAM_LR, ADAM_B1, ADAM_B2, ADAM_EPS, ADAM_WD, ADAM_STEP = 0.001, 0.9, 0.999, 1e-08, 0.01, 10

N_CHIPS = 4
N_DEV = 8
LANES = 128
TM = 512
VMEM_MB = 56

MESH = pl.DeviceIdType.MESH
ANY = pl.BlockSpec(memory_space=pl.ANY)


def _cp(*sem, vmem=VMEM_MB):
    return pltpu.CompilerParams(dimension_semantics=sem, vmem_limit_bytes=vmem * 1024 * 1024)


def _rows(tile, cols):
    return pl.BlockSpec((tile, cols), lambda i: (i, 0))


def _const(shape):
    nd = len(shape)
    return pl.BlockSpec(shape, lambda i: (0,) * nd)


def _sds(shape, dtype):
    return jax.ShapeDtypeStruct(shape, dtype)


def _dot(a, b):
    return jnp.dot(a, b, preferred_element_type=F32)


def _dot_nt(a, b):
    return lax.dot_general(a, b, (((1,), (1,)), ((), ())), preferred_element_type=F32)


def _dot_tn(a, b):
    return lax.dot_general(a, b, (((0,), (0,)), ((), ())), preferred_element_type=F32)


def _rms(x):
    return lax.rsqrt(jnp.mean(x * x, axis=-1, keepdims=True) + EPS)


def _norm_bwd(x, g, dout):
    r = _rms(x)
    n = x * r
    dn = dout * g
    dx = r * (dn - n * jnp.mean(dn * n, axis=-1, keepdims=True))
    return dx, jnp.sum(dout * n, axis=0, keepdims=True)


def _rot_fwd(t, c, a, bt):
    return t * c + pltpu.roll(t, LANES - 8, 1) * a + pltpu.roll(t, 8, 1) * bt


def _rot_bwd(d, c, a, bt):
    return d * c + pltpu.roll(d * a, 8, 1) + pltpu.roll(d * bt, LANES - 8, 1)


def _rope_tables(seq):
    pos = jnp.arange(seq, dtype=F32)
    inv_freq = ROPE_THETA ** (-jnp.arange(0, ROT_DIM, 2, dtype=F32) / ROT_DIM)
    ang = pos[:, None] * inv_freq[None, :]
    cos, sin = jnp.cos(ang), jnp.sin(ang)
    ones = jnp.ones((seq, HEAD_DIM - ROT_DIM), F32)
    zeros8 = jnp.zeros((seq, 8), F32)
    zrest = jnp.zeros((seq, HEAD_DIM - ROT_DIM), F32)
    c = jnp.concatenate([cos, cos, ones], axis=1)
    a = jnp.concatenate([-sin, zeros8, zrest], axis=1)
    bt = jnp.concatenate([zeros8, sin, zrest], axis=1)
    return tuple(jnp.tile(t, (1, 2)) for t in (c, a, bt))


class _Exchange:
    def __init__(self, inputs, out_shapes, sems, start, finish, aliases=None, middle=None):
        self.inputs, self.out_shapes, self.sems = list(inputs), list(out_shapes), list(sems)
        self.start, self.finish, self.aliases = start, finish, dict(aliases or {})
        self.middle = middle


def _call(body, *, name, grid, in_specs, out_specs, out_shape, args, scratch=(), sem=(), exchanges=()):
    in_specs, out_specs, out_shape, scratch = list(in_specs), list(out_specs), list(out_shape), list(scratch)
    if not exchanges:
        return pl.pallas_call(body, name=name, grid=grid, in_specs=in_specs, out_specs=out_specs,
                              out_shape=out_shape, scratch_shapes=scratch, compiler_params=_cp(*sem))(*args)
    n_in, n_out, n_scr = len(in_specs), len(out_specs), len(scratch)
    x_in = [a for ex in exchanges for a in ex.inputs]
    x_out = [s for ex in exchanges for s in ex.out_shapes]
    x_sem = [s for ex in exchanges for s in ex.sems]
    aliases, i_off, o_off = {}, n_in, n_out
    for ex in exchanges:
        for i, o in ex.aliases.items():
            aliases[i_off + i] = o_off + o
        i_off += len(ex.inputs)
        o_off += len(ex.out_shapes)

    def split(flat):
        out, pos = [], 0
        for ex, n in zip(exchanges, flat[1]):
            out.append(flat[0][pos:pos + n])
            pos += n
        return out

    def carrier(*refs):
        pos = 0
        groups = []
        for n in (n_in, len(x_in), n_out, len(x_out), n_scr, len(x_sem)):
            groups.append(refs[pos:pos + n])
            pos += n
        ins, xin, outs, xout, scr, xsem = groups
        xin = split((xin, [len(ex.inputs) for ex in exchanges]))
        xout = split((xout, [len(ex.out_shapes) for ex in exchanges]))
        xsem = split((xsem, [len(ex.sems) for ex in exchanges]))
        first = pl.program_id(0) == 0
        last = pl.program_id(0) == grid[0] - 1
        for d in range(1, len(grid)):
            first = jnp.logical_and(first, pl.program_id(d) == 0)
            last = jnp.logical_and(last, pl.program_id(d) == grid[d] - 1)

        @pl.when(first)
        def _():
            for ex, i, o, s in zip(exchanges, xin, xout, xsem):
                ex.start(i, o, s)

        if any(ex.middle for ex in exchanges):
            half = pl.program_id(0) == 5 * grid[0] // 8
            for d in range(1, len(grid)):
                half = jnp.logical_and(half, pl.program_id(d) == 0)

            @pl.when(half)
            def _():
                for ex, i, o, s in zip(exchanges, xin, xout, xsem):
                    if ex.middle:
                        ex.middle(i, o, s)

        body(*ins, *outs, *scr)

        @pl.when(last)
        def _():
            for ex, i, o, s in zip(exchanges, xin, xout, xsem):
                ex.finish(i, o, s)

    res = pl.pallas_call(
        carrier, name=name, grid=grid, in_specs=in_specs + [ANY] * len(x_in),
        out_specs=out_specs + [ANY] * len(x_out), out_shape=out_shape + x_out,
        scratch_shapes=scratch + x_sem, input_output_aliases=aliases,
        compiler_params=_cp(*(["arbitrary"] * len(grid))),
    )(*args, *x_in)
    return res[:n_out], split((res[n_out:], [len(ex.out_shapes) for ex in exchanges]))


def _alone(name, *exchanges):
    n_in = [len(ex.inputs) for ex in exchanges]
    n_out = [len(ex.out_shapes) for ex in exchanges]
    n_sem = [len(ex.sems) for ex in exchanges]
    aliases, i_off, o_off = {}, 0, 0
    for ex in exchanges:
        for i, o in ex.aliases.items():
            aliases[i_off + i] = o_off + o
        i_off += len(ex.inputs)
        o_off += len(ex.out_shapes)

    def split(flat, counts):
        out, pos = [], 0
        for n in counts:
            out.append(flat[pos:pos + n])
            pos += n
        return out

    def body(*refs):
        ins, outs, sems = split(refs, [sum(n_in), sum(n_out), sum(n_sem)])
        groups = list(zip(exchanges, split(ins, n_in), split(outs, n_out), split(sems, n_sem)))
        for ex, i, o, s in groups:
            ex.start(i, o, s)
        for ex, i, o, s in groups:
            if ex.middle:
                ex.middle(i, o, s)
        for ex, i, o, s in groups:
            ex.finish(i, o, s)

    res = pl.pallas_call(
        body, name=name, in_specs=[ANY] * sum(n_in), out_specs=[ANY] * sum(n_out),
        out_shape=[s for ex in exchanges for s in ex.out_shapes],
        scratch_shapes=[s for ex in exchanges for s in ex.sems], input_output_aliases=aliases,
    )(*[a for ex in exchanges for a in ex.inputs])
    return split(res, n_out)


def _place():
    x, y, c = lax.axis_index("x"), lax.axis_index("y"), lax.axis_index("c")
    chips = [(1 - x, y), (x, 1 - y), (1 - x, 1 - y)]
    return x, y, c, chips


def _remote(src, dst, send, recv, to):
    return pltpu.make_async_remote_copy(src_ref=src, dst_ref=dst, send_sem=send, recv_sem=recv,
                                        device_id=to, device_id_type=MESH)


def _ex_gather(shards):
    nw = len(shards)
    hrs = [s.shape[0] // 2 for s in shards]

    def copies(ins, outs, sems):
        s0, r0, s1, r1, s2, r2, fs, fr = sems
        x, y, c, _ = _place()
        me, xn, yn, dg = (x, y), (1 - x, y), (x, 1 - y), (1 - x, 1 - y)
        nbr = (xn, yn)
        sibling = (x, y, 1 - c)

        def piece(w, chip, core, part=None):
            hr = hrs[w]
            rows = pl.ds(core * hr, hr) if part is None else pl.ds(core * hr + part * (hr // 2), hr // 2)
            return outs[w].at[2 * chip[0] + chip[1], rows]

        def first(w, k, lead):
            part = k if lead else 1 - k
            send, recv = (s0, r0) if lead else (s1, r1)
            rows = pl.ds(c * hrs[w] + part * (hrs[w] // 2), hrs[w] // 2)
            return _remote(ins[w].at[rows], piece(w, me, c, part), send.at[w, k], recv.at[w, k], (*nbr[k], c))

        def landed(w, k, lead):
            part = k if lead else 1 - k
            send, recv = (s0, r0) if lead else (s1, r1)
            return _remote(piece(w, nbr[k], c, part), piece(w, nbr[k], c, part), send.at[w, k], recv.at[w, k],
                           (*nbr[k], c))

        def onward(w, k):
            return _remote(piece(w, nbr[k], c, k), piece(w, nbr[k], c, k), s2.at[w, k], r2.at[w, k],
                           (*nbr[1 - k], c))

        def arrived(w, k):
            return _remote(piece(w, dg, c, k), piece(w, dg, c, k), s2.at[w, k], r2.at[w, k], (*nbr[1 - k], c))

        def passed(w, j):
            chip = (xn, yn, dg)[j]
            return _remote(piece(w, chip, c), piece(w, chip, c), fs.at[w, j], fr.at[w, j], sibling)

        def handed(w, j):
            chip = (xn, yn, dg)[j]
            return _remote(piece(w, chip, 1 - c), piece(w, chip, 1 - c), fs.at[w, j], fr.at[w, j], sibling)

        return first, landed, onward, arrived, passed, handed

    def start(ins, outs, sems):
        first = copies(ins, outs, sems)[0]
        for lead in (True, False):
            for w in range(nw):
                for k in range(2):
                    first(w, k, lead).start()

    def middle(ins, outs, sems):
        _, landed, onward, _, passed, _ = copies(ins, outs, sems)
        for w in range(nw):
            for k in range(2):
                landed(w, k, True).wait_recv()
                onward(w, k).start()
        for w in range(nw):
            for k in range(2):
                landed(w, k, False).wait_recv()
                passed(w, k).start()

    def finish(ins, outs, sems):
        first, _, onward, arrived, passed, handed = copies(ins, outs, sems)
        for w in range(nw):
            for k in range(2):
                arrived(w, k).wait_recv()
            passed(w, 2).start()
        for w in range(nw):
            for j in range(3):
                handed(w, j).wait_recv()
        for w in range(nw):
            for k in range(2):
                first(w, k, True).wait_send()
                first(w, k, False).wait_send()
                onward(w, k).wait_send()
            for j in range(3):
                passed(w, j).wait_send()

    return _Exchange(shards, [_sds((N_CHIPS,) + s.shape, s.dtype) for s in shards],
                     [pltpu.SemaphoreType.DMA((nw, 2))] * 6 + [pltpu.SemaphoreType.DMA((nw, 3))] * 2,
                     start, finish, middle=middle)


def _ex_pair(grads):
    nw = len(grads)

    def copies(ins, outs, sems):
        x, y, c, _ = _place()
        out = []
        for w in range(nw):
            hr = grads[w].shape[1] // 2
            out.append(_remote(ins[w].at[:, pl.ds((1 - c) * hr, hr)], outs[w], sems[0].at[w], sems[1].at[w],
                               (x, y, 1 - c)))
        return out

    def start(ins, outs, sems):
        for cp in copies(ins, outs, sems):
            cp.start()

    def finish(ins, outs, sems):
        for cp in copies(ins, outs, sems):
            cp.wait()

    return _Exchange(grads, [_sds((N_CHIPS, g.shape[1] // 2, g.shape[2]), F32) for g in grads],
                     [pltpu.SemaphoreType.DMA((nw,))] * 2, start, finish)


def _ex_chip(pieces):
    nw = len(pieces)

    def copies(ins, outs, sems):
        x, y, c, chips = _place()
        return [_remote(ins[w].at[2 * cx + cy], outs[w].at[k], sems[0].at[w, k], sems[1].at[w, k], (cx, cy, c))
                for w in range(nw) for k, (cx, cy) in enumerate(chips)]

    def start(ins, outs, sems):
        for cp in copies(ins, outs, sems):
            cp.start()

    def finish(ins, outs, sems):
        for cp in copies(ins, outs, sems):
            cp.wait()

    return _Exchange(pieces, [_sds((3,) + p.shape[1:], BF16) for p in pieces],
                     [pltpu.SemaphoreType.DMA((nw, 3))] * 2, start, finish)


def _ex_swap(fulls):
    nw = len(fulls)

    def start(ins, outs, sems):
        x, y, c, _ = _place()
        for w in range(nw):
            hr = fulls[w].shape[0] // 2
            mine = pl.ds(c * hr, hr)
            _remote(ins[w].at[mine], outs[w].at[mine], sems[0].at[w], sems[1].at[w], (x, y, 1 - c)).start()

    def finish(ins, outs, sems):
        x, y, c, _ = _place()
        for w in range(nw):
            hr = fulls[w].shape[0] // 2
            mine, theirs = pl.ds(c * hr, hr), pl.ds((1 - c) * hr, hr)
            _remote(ins[w].at[mine], outs[w].at[mine], sems[0].at[w], sems[1].at[w], (x, y, 1 - c)).wait_send()
            _remote(ins[w].at[theirs], outs[w].at[theirs], sems[0].at[w], sems[1].at[w], (x, y, 1 - c)).wait_recv()

    return _Exchange(fulls, [_sds(f.shape, F32) for f in fulls], [pltpu.SemaphoreType.DMA((nw,))] * 2,
                     start, finish, aliases={w: w for w in range(nw)})


def _ex_allgather(blocks):
    nb = len(blocks)

    def copies(ins, outs, sems):
        send, recv, lsem = sems
        x, y, c, chips = _place()
        me, sibling = (x, y, c), (x, y, 1 - c)

        def rows(b, px, py, pc):
            m_per = blocks[b].shape[0]
            return outs[b].at[pl.ds((4 * px + 2 * py + pc) * m_per, m_per), :]

        def copy(b, k, blk, to, src=None):
            return _remote(rows(b, *blk) if src is None else src, rows(b, *blk), send.at[b, k], recv.at[b, k], to)

        def mine(b):
            return pltpu.make_async_copy(ins[b], rows(b, *me), lsem.at[b])

        def first(b, k):
            return copy(b, k, me, sibling if k == 0 else (*chips[k - 1], c), src=ins[b])

        def passed(b, j):
            return copy(b, 4 + j, (*chips[j], c), sibling)

        def landed(b, j):
            return copy(b, 1 + j, (*chips[j], c), me)

        def handed(b, k):
            return copy(b, 0, sibling, me) if k == 0 else copy(b, 3 + k, (*chips[k - 1], 1 - c), me)

        return mine, first, passed, landed, handed

    def start(ins, outs, sems):
        mine, first, _, _, _ = copies(ins, outs, sems)
        for b in range(nb):
            mine(b).start()
            for k in range(4):
                first(b, k).start()

    def finish(ins, outs, sems):
        mine, first, passed, landed, handed = copies(ins, outs, sems)
        sent = []
        for b in range(nb):
            for j in range(3):
                landed(b, j).wait_recv()
                cp = passed(b, j)
                cp.start()
                sent.append(cp)
        for b in range(nb):
            for k in range(4):
                handed(b, k).wait_recv()
            for k in range(4):
                first(b, k).wait_send()
        for cp in sent:
            cp.wait_send()
        for b in range(nb):
            mine(b).wait()

    return _Exchange(blocks, [_sds((N_DEV * b.shape[0], b.shape[1]), F32) for b in blocks],
                     [pltpu.SemaphoreType.DMA((nb, 7)), pltpu.SemaphoreType.DMA((nb, 7)), pltpu.SemaphoreType.DMA((nb,))],
                     start, finish)


def _cast_shards(w_up, w_down, w_bp, w_ba, w_out, exchanges=()):
    r, c = w_up.shape
    tr = HALF // 2
    steps = r // tr

    def body(up_ref, down_ref, bp_ref, ba_ref, out_ref, ua_ref, ub_ref, da_ref, db_ref, bpo_ref, bao_ref, outo_ref):
        i = pl.program_id(0)

        @pl.when(i == 0)
        def _():
            for src, dst in ((bp_ref, bpo_ref), (ba_ref, bao_ref), (out_ref, outo_ref)):
                dst[...] = src[...].astype(BF16)

        @pl.when(i < steps // 2)
        def _():
            ua_ref[...] = up_ref[...].astype(BF16)
            da_ref[...] = down_ref[...].astype(BF16)

        @pl.when(i >= steps // 2)
        def _():
            ub_ref[...] = up_ref[...].astype(BF16)
            db_ref[...] = down_ref[...].astype(BF16)

    rows = _rows(tr, c)
    first = pl.BlockSpec((tr, c), lambda i: (jnp.minimum(i, steps // 2 - 1), 0))
    second = pl.BlockSpec((tr, c), lambda i: (jnp.maximum(i - steps // 2, 0), 0))
    half = _sds((HALF, c), BF16)
    return _call(
        body, name="cast_shards", grid=(steps,),
        in_specs=[rows, rows, _const(w_bp.shape), _const(w_ba.shape), _const(w_out.shape)],
        out_specs=[first, second, first, second, _const(w_bp.shape), _const(w_ba.shape), _const(w_out.shape)],
        out_shape=[half, half, half, half, _sds(w_bp.shape, BF16), _sds(w_ba.shape, BF16), _sds(w_out.shape, BF16)],
        args=(w_up, w_down, w_bp, w_ba, w_out), sem=("arbitrary",), exchanges=exchanges)


def _inproj(x2, g1, w_in_t, b_in, tabs, seq, exchanges=()):
    T = x2.shape[0]
    tm = min(TM, seq)
    nseq = seq // tm

    def body(x_ref, g_ref, w_ref, b_ref, c_ref, a_ref, bt_ref, h_ref, u_ref, q_ref, k_ref, v_ref, gate_ref):
        x = x_ref[...]
        h = (x * _rms(x) * g_ref[...]).astype(BF16)
        h_ref[...] = h

        def proj(lo, hi):
            return _dot_nt(h, w_ref[lo:hi, :]) + b_ref[:, lo:hi]

        c, a, bt = c_ref[...], a_ref[...], bt_ref[...]
        u_ref[...] = proj(0, C_Q)
        q = proj(C_Q, C_K)
        for p in range(4):
            sl = slice(LANES * p, LANES * (p + 1))
            q_ref[:, sl] = (_rot_fwd(q[:, sl], c, a, bt) * SCALE).astype(BF16)
        kv = proj(C_K, C_G)
        k_ref[...] = _rot_fwd(kv[:, :KV_WIDTH], c, a, bt).astype(BF16)
        v_ref[...] = kv[:, KV_WIDTH:].astype(BF16)
        for j in range(2):
            lo = C_G + D_MODEL * j
            gate_ref[:, D_MODEL * j:D_MODEL * (j + 1)] = jax.nn.sigmoid(proj(lo, lo + D_MODEL)).astype(BF16)

    tab = pl.BlockSpec((tm, LANES), lambda i: (i % nseq, 0))
    return _call(
        body, name="inproj", grid=(T // tm,),
        in_specs=[_rows(tm, D_MODEL), _const((1, D_MODEL)), _const((IN_WIDTH, D_MODEL)), _const((1, IN_WIDTH)),
                  tab, tab, tab],
        out_specs=[_rows(tm, D_MODEL), _rows(tm, POOL_WIDTH), _rows(tm, ATTN_WIDTH), _rows(tm, KV_WIDTH),
                   _rows(tm, KV_WIDTH), _rows(tm, GATE_WIDTH)],
        out_shape=[_sds((T, D_MODEL), BF16), _sds((T, POOL_WIDTH), F32), _sds((T, ATTN_WIDTH), BF16),
                   _sds((T, KV_WIDTH), BF16), _sds((T, KV_WIDTH), BF16), _sds((T, GATE_WIDTH), BF16)],
        args=(x2, g1, w_in_t, b_in, *tabs), sem=("parallel",), exchanges=exchanges)


def _inv_count(pos, w):
    return 1.0 / jnp.minimum(pos + 1, w).astype(F32)


def _pool_tile(i, tp, nseq, u_ref, prev_ref, w_ref, s_ref, diff_ref, y_ref):
    first = (i % nseq) == 0
    prev = jnp.where(first, 0.0, prev_ref[...])
    ext = jnp.concatenate([prev, u_ref[...]], axis=0)
    pos = (i % nseq) * tp + lax.broadcasted_iota(jnp.int32, (tp, 1), 0)
    for gi, w in enumerate(POOL_WINDOWS):
        sl = slice(POOL_GC * gi, POOL_GC * (gi + 1))
        xg = ext[:, sl]
        s = xg
        sh = 1
        while sh < w:
            s = s + pltpu.roll(s, sh, 0)
            sh *= 2
        pooled = s[HALO:] * _inv_count(pos, w)
        diff = (pooled - xg[HALO:]).astype(BF16)
        diff_ref[:, sl] = diff
        mixed = _dot(diff, w_ref[gi].astype(BF16))
        y_ref[:, sl] = (mixed * s_ref[:, sl]).astype(BF16)


def _pool_specs(tp):
    per = tp // HALO
    return [_rows(tp, POOL_WIDTH), pl.BlockSpec((HALO, POOL_WIDTH), lambda i: (jnp.maximum(i * per - 1, 0), 0)),
            _const((4, POOL_GC, POOL_GC)), _const((1, POOL_WIDTH))]


GROUP = 4
GROWS = GROUP * BLOCK


def _attn_masks(n):
    qi = lax.broadcasted_iota(jnp.int32, (GROWS, 2 * BLOCK), 0) % BLOCK
    kj = lax.broadcasted_iota(jnp.int32, (GROWS, 2 * BLOCK), 1)
    rel = qi + BLOCK - kj
    valid = (rel >= 0) & (rel < BLOCK) & (kj >= jnp.where(n > 0, 0, BLOCK))
    lo = lax.broadcasted_iota(jnp.int32, (BLOCK, LANES), 1) < HEAD_DIM
    return valid, lo


def _by_example(bl, *arrays):
    return [a.reshape(bl, a.shape[0] // bl, a.shape[1]) for a in arrays]


def _stack_heads(ref, h, lo):
    keep = lo if h == 0 else jnp.logical_not(lo)
    pieces = []
    for p in (2 * h, 2 * h + 1):
        xp = ref[:, LANES * p:LANES * (p + 1)].astype(F32)
        for e in range(2):
            t = xp if e == h else pltpu.roll(xp, HEAD_DIM, 1)
            pieces.append(jnp.where(keep, t, 0.0).astype(BF16))
    return jnp.concatenate(pieces, axis=0)


def _unstack_heads(stacked, h, lo):
    pairs = []
    for j in range(2):
        parts = []
        for e in range(2):
            t = stacked[BLOCK * (2 * j + e):BLOCK * (2 * j + e + 1)]
            parts.append(t if e == h else pltpu.roll(t, HEAD_DIM, 1))
        pairs.append(jnp.where(lo, parts[0], parts[1]))
    return pairs


def _sink_rows(sink_ref, h):
    head = lax.broadcasted_iota(jnp.int32, (GROWS, 1), 0) // BLOCK
    col = jnp.zeros((GROWS, 1), F32) + sink_ref[GROUP * h]
    for g in range(1, GROUP):
        col = jnp.where(head == g, sink_ref[GROUP * h + g], col)
    return col


def _group_probs(qs, kk, valid, sink):
    s = jnp.where(valid, _dot_nt(qs, kk), NEG_INF)
    m = jnp.maximum(jnp.max(s, axis=1, keepdims=True), sink)
    ex = jnp.exp(s - m)
    es = jnp.exp(sink - m)
    inv = 1.0 / (jnp.sum(ex, axis=1, keepdims=True) + es)
    return ex * inv, es * inv


def _mixers_fwd(q, k, v, sinks, u, w_pool, pool_scale, seq, exchanges=()):
    T = q.shape[0]
    nb = seq // BLOCK
    bl = T // seq
    tp = T // nb
    nseq = seq // tp

    def body(sink_ref, q_ref, kp_ref, kc_ref, vp_ref, vc_ref, u_ref, prev_ref, w_ref, s_ref, o_ref, diff_ref, y_ref):
        n = pl.program_id(0)
        valid, lo = _attn_masks(n)
        for b in range(bl):
            kk = jnp.concatenate([kp_ref[b], kc_ref[b]], axis=0)
            vv = jnp.concatenate([vp_ref[b], vc_ref[b]], axis=0)
            for h in range(2):
                qs = _stack_heads(q_ref.at[b], h, lo)
                pr, _ = _group_probs(qs, kk, valid, _sink_rows(sink_ref, h))
                o = _dot(pr.astype(BF16), vv)
                for j, pair in enumerate(_unstack_heads(o, h, lo)):
                    p = 2 * h + j
                    o_ref[b, :, LANES * p:LANES * (p + 1)] = pair.astype(BF16)
        _pool_tile(n, tp, nseq, u_ref, prev_ref, w_ref, s_ref, diff_ref, y_ref)

    cur = lambda n: (0, n, 0)
    prv = lambda n: (0, jnp.maximum(n - 1, 0), 0)
    kv = lambda m: pl.BlockSpec((bl, BLOCK, KV_WIDTH), m)
    res = _call(
        body, name="mixers_fwd", grid=(nb,),
        in_specs=[pl.BlockSpec(memory_space=pltpu.SMEM), pl.BlockSpec((bl, BLOCK, ATTN_WIDTH), cur),
                  kv(prv), kv(cur), kv(prv), kv(cur)] + _pool_specs(tp),
        out_specs=[pl.BlockSpec((bl, BLOCK, ATTN_WIDTH), cur), _rows(tp, POOL_WIDTH), _rows(tp, POOL_WIDTH)],
        out_shape=[_sds((bl, seq, ATTN_WIDTH), BF16), _sds((T, POOL_WIDTH), BF16), _sds((T, POOL_WIDTH), BF16)],
        args=(sinks, *_by_example(bl, q, k, k, v, v), u, u, w_pool, pool_scale), sem=("parallel",),
        exchanges=exchanges)
    outs, rest = res if exchanges else (res, None)
    return [outs[0].reshape(T, ATTN_WIDTH), outs[1], outs[2]], rest


def _branch(y, w_ref):
    return jnp.concatenate([_dot(y, w_ref[j]) for j in range(N_CHIPS)], axis=1)


def _merge_out(y_pool, y_attn, gate, x2, w_bp, w_ba, w_out, g2, g3, exchanges=()):
    T = x2.shape[0]
    tm = min(TM, T)

    def body(yp_ref, ya_ref, gate_ref, x_ref, wbp_ref, wba_ref, wo_ref, g2_ref, g3_ref,
             mg_ref, mix_ref, x1_ref, h2_ref):
        bp, ba = _branch(yp_ref[...], wbp_ref), _branch(ya_ref[...], wba_ref)
        merged = (gate_ref[:, :D_MODEL].astype(F32) * bp + gate_ref[:, D_MODEL:].astype(F32) * ba).astype(BF16)
        mg_ref[...] = merged
        mix = _dot(merged, wo_ref[...])
        mix_ref[...] = mix
        x1 = x_ref[...] + mix * _rms(mix) * g2_ref[...]
        x1_ref[...] = x1
        h2_ref[...] = (x1 * _rms(x1) * g3_ref[...]).astype(BF16)

    return _call(
        body, name="merge_out", grid=(T // tm,),
        in_specs=[_rows(tm, POOL_WIDTH), _rows(tm, ATTN_WIDTH), _rows(tm, GATE_WIDTH), _rows(tm, D_MODEL),
                  _const(w_bp.shape), _const(w_ba.shape), _const((D_MODEL, D_MODEL)),
                  _const((1, D_MODEL)), _const((1, D_MODEL))],
        out_specs=[_rows(tm, D_MODEL)] * 4,
        out_shape=[_sds((T, D_MODEL), BF16), _sds((T, D_MODEL), F32), _sds((T, D_MODEL), F32),
                   _sds((T, D_MODEL), BF16)],
        args=(y_pool, y_attn, gate, x2, w_bp, w_ba, w_out, g2, g3), sem=("parallel",), exchanges=exchanges)


HALF = D_MODEL // 2
TM_MLP = 256


def _mlp_core(h2, x1, mix, tgt, w_up, w_down, g4, g3, g2):
    T = h2.shape[0]
    tm = min(TM_MLP, T)

    def body(h_ref, x1_ref, mix_ref, t_ref, g_ref, g3_ref, g2_ref, ua_hbm, ub_hbm, da_hbm, db_hbm,
             act_ref, dff_ref, dup_ref, dx1_ref, dmix_ref, loss_ref, dg_ref, dg3_ref, dg2_ref,
             wu, wd, relu_scr, sems):
        def weight_copy(i):
            src, dst = ((ua_hbm, wu.at[:, :HALF]), (ub_hbm, wu.at[:, HALF:]),
                        (da_hbm, wd.at[:, :HALF]), (db_hbm, wd.at[:, HALF:]))[i]
            return pltpu.make_async_copy(src, dst, sems.at[i])

        @pl.when(pl.program_id(0) == 0)
        def _():
            for i in range(4):
                weight_copy(i).start()
            loss_ref[...] = jnp.zeros_like(loss_ref)
            for ref in (dg_ref, dg3_ref, dg2_ref):
                ref[...] = jnp.zeros_like(ref)
            weight_copy(0).wait()
            weight_copy(1).wait()

        h = h_ref[...]
        ff = None
        for j in range(N_CHIPS):
            lo = D_MODEL * j
            relu = jnp.maximum(_dot(h, wu[j]), 0.0)
            if j == 0:
                @pl.when(pl.program_id(0) == 0)
                def _():
                    weight_copy(2).wait()
                    weight_copy(3).wait()
            relu_scr[:, lo:lo + D_MODEL] = relu
            act = jnp.square(relu).astype(BF16)
            act_ref[:, lo:lo + D_MODEL] = act
            t = _dot(act, wd[j])
            ff = t if ff is None else ff + t
        g = g_ref[...]
        x1 = x1_ref[...]
        err = x1 + ff * _rms(ff) * g - t_ref[...]
        loss_ref[...] += jnp.sum(err * err) * (0.5 / D_MODEL)
        dy = err * (1.0 / D_MODEL)
        dff, dg = _norm_bwd(ff, g, dy)
        dg_ref[...] += dg
        dff = dff.astype(BF16)
        dff_ref[...] = dff
        dh2 = None
        for j in range(N_CHIPS):
            lo = D_MODEL * j
            dup = (_dot_nt(dff, wd[j]) * (2.0 * relu_scr[:, lo:lo + D_MODEL])).astype(BF16)
            dup_ref[:, lo:lo + D_MODEL] = dup
            t = _dot_nt(dup, wu[j])
            dh2 = t if dh2 is None else dh2 + t
        dx, dg3 = _norm_bwd(x1, g3_ref[...], dh2)
        dx1 = dy + dx
        dx1_ref[...] = dx1
        dg3_ref[...] += dg3
        dmix, dg2 = _norm_bwd(mix_ref[...], g2_ref[...], dx1)
        dmix_ref[...] = dmix.astype(BF16)
        dg2_ref[...] += dg2

    slabs = pltpu.VMEM((N_CHIPS, D_MODEL, D_MODEL), BF16)
    gain = _const((1, D_MODEL))
    return pl.pallas_call(
        body, name="mlp_core", grid=(T // tm,),
        in_specs=[_rows(tm, D_MODEL)] * 4 + [gain] * 3 + [ANY] * 4,
        out_specs=[_rows(tm, D_FF), _rows(tm, D_MODEL), _rows(tm, D_FF), _rows(tm, D_MODEL), _rows(tm, D_MODEL),
                   _const((8, LANES)), gain, gain, gain],
        out_shape=[_sds((T, D_FF), BF16), _sds((T, D_MODEL), BF16), _sds((T, D_FF), BF16), _sds((T, D_MODEL), F32),
                   _sds((T, D_MODEL), BF16), _sds((8, LANES), F32)] + [_sds((1, D_MODEL), F32)] * 3,
        scratch_shapes=[slabs] * 2 + [pltpu.VMEM((tm, D_FF), F32), pltpu.SemaphoreType.DMA((4,))],
        compiler_params=_cp("arbitrary"),
    )(h2, x1, mix, tgt, g4, g3, g2, *w_up, *w_down)


def _dw(tag, a, g, ta, tn, shard_cols=False, exchanges=()):
    T, ka = a.shape
    n = g.shape[1]
    tk = min(2 * TM, T)
    nk = T // tk
    slab = n // N_CHIPS

    def body(a_ref, g_ref, o_ref):
        @pl.when(pl.program_id(2) == 0)
        def _():
            o_ref[...] = jnp.zeros_like(o_ref)

        prod = _dot_tn(a_ref[...], g_ref[...])
        if shard_cols:
            for s in range(tn // slab):
                o_ref[s] += prod[:, slab * s:slab * (s + 1)]
        else:
            o_ref[...] += prod

    if shard_cols:
        out_spec = pl.BlockSpec((tn // slab, ta, slab), lambda i, j, k: (j, i, 0))
        out_shape = _sds((N_CHIPS, ka, slab), F32)
    else:
        out_spec = pl.BlockSpec((ta, tn), lambda i, j, k: (i, j))
        out_shape = _sds((ka, n), F32)
    return _call(
        body, name="dw_" + tag, grid=(ka // ta, n // tn, nk),
        in_specs=[pl.BlockSpec((tk, ta), lambda i, j, k: (k, i)), pl.BlockSpec((tk, tn), lambda i, j, k: (k, j))],
        out_specs=[out_spec], out_shape=[out_shape],
        args=(a, g), sem=("parallel", "parallel", "arbitrary"), exchanges=exchanges)


def _dw_mix(merged, dmix, y_pool, dbp, y_attn, dba, exchanges=()):
    T = merged.shape[0]
    tk = min(2 * TM, T)
    c = D_MODEL // N_CHIPS

    def body(mg_ref, dmix_ref, yp_ref, dbp_ref, ya_ref, dba_ref, out_ref, bp_ref, ba_ref):
        @pl.when(pl.program_id(0) == 0)
        def _():
            for ref in (out_ref, bp_ref, ba_ref):
                ref[...] = jnp.zeros_like(ref)

        out_ref[...] += _dot_tn(mg_ref[...], dmix_ref[...])
        for y_ref, d_ref, o_ref in ((yp_ref, dbp_ref, bp_ref), (ya_ref, dba_ref, ba_ref)):
            res = _dot_tn(y_ref[...], d_ref[...])
            for j in range(N_CHIPS):
                o_ref[j] += res[:, c * j:c * (j + 1)]

    slabs = (N_CHIPS, POOL_WIDTH, c)
    return _call(
        body, name="dw_mix", grid=(T // tk,),
        in_specs=[_rows(tk, D_MODEL), _rows(tk, D_MODEL), _rows(tk, POOL_WIDTH), _rows(tk, D_MODEL),
                  _rows(tk, ATTN_WIDTH), _rows(tk, D_MODEL)],
        out_specs=[_const((D_MODEL, D_MODEL)), _const(slabs), _const(slabs)],
        out_shape=[_sds((D_MODEL, D_MODEL), F32), _sds(slabs, F32), _sds(slabs, F32)],
        args=(merged, dmix, y_pool, dbp, y_attn, dba), sem=("arbitrary",), exchanges=exchanges)


def _merge_bwd(dmix, gate, y_pool, y_attn, w_out, w_bp, w_ba, exchanges=()):
    T = dmix.shape[0]
    tm = min(TM, T)

    def body(dmix_ref, gate_ref, yp_ref, ya_ref, wo_ref, wbp_ref, wba_ref,
             dbp_ref, dba_ref, dgate_ref, dyp_ref, dya_ref):
        dm = _dot_nt(dmix_ref[...], wo_ref[...])
        for j, (y_ref, db_ref, w_ref, dy_ref) in enumerate(
                ((yp_ref, dbp_ref, wbp_ref, dyp_ref), (ya_ref, dba_ref, wba_ref, dya_ref))):
            sl = slice(D_MODEL * j, D_MODEL * (j + 1))
            gt = gate_ref[:, sl].astype(F32)
            db = (dm * gt).astype(BF16)
            db_ref[...] = db
            dgate_ref[:, sl] = (dm * _branch(y_ref[...], w_ref) * gt * (1.0 - gt)).astype(BF16)
            cw = D_MODEL // N_CHIPS
            dy = _dot_nt(db[:, :cw], w_ref[0])
            for c in range(1, N_CHIPS):
                dy = dy + _dot_nt(db[:, cw * c:cw * (c + 1)], w_ref[c])
            dy_ref[...] = dy.astype(dy_ref.dtype)

    return _call(
        body, name="merge_bwd", grid=(T // tm,),
        in_specs=[_rows(tm, D_MODEL), _rows(tm, GATE_WIDTH), _rows(tm, POOL_WIDTH), _rows(tm, ATTN_WIDTH),
                  _const((D_MODEL, D_MODEL)), _const(w_bp.shape), _const(w_ba.shape)],
        out_specs=[_rows(tm, D_MODEL), _rows(tm, D_MODEL), _rows(tm, GATE_WIDTH), _rows(tm, POOL_WIDTH),
                   _rows(tm, ATTN_WIDTH)],
        out_shape=[_sds((T, D_MODEL), BF16), _sds((T, D_MODEL), BF16), _sds((T, GATE_WIDTH), BF16),
                   _sds((T, POOL_WIDTH), F32), _sds((T, ATTN_WIDTH), BF16)],
        args=(dmix, gate, y_pool, y_attn, w_out, w_bp, w_ba), sem=("parallel",), exchanges=exchanges)


def _mixers_bwd(q, k, v, do, sinks, tabs, dyp, diff, w_pool, pool_scale, seq, exchanges=()):
    T = q.shape[0]
    nb = seq // BLOCK
    bl = T // seq
    steps = nb + 1
    tp = T // nb
    nseq = seq // tp
    per = tp // HALO
    last_halo = T // HALO - 1

    def body(sink_ref, q_ref, do_ref, kp_ref, kc_ref, vp_ref, vc_ref, c_ref, a_ref, bt_ref, cp_ref, ap_ref, btp_ref,
             dy_ref, nxt_ref, diff_ref, w_ref, s_ref,
             dq_ref, dk_ref, dv_ref, dsink_ref, du_ref, dw_ref, ds_ref, ck_ref, cv_ref):
        n = pl.program_id(0)

        @pl.when(n == 0)
        def _():
            for ref in (dsink_ref, ck_ref, cv_ref, dw_ref, ds_ref):
                ref[...] = jnp.zeros_like(ref)

        @pl.when(n < nb)
        def _():
            _pool_bwd_tile(n, tp, nseq, dy_ref, nxt_ref, diff_ref, w_ref, s_ref, du_ref, dw_ref, ds_ref)
            valid, lo = _attn_masks(n)
            for b in range(bl):
                kk = jnp.concatenate([kp_ref[b], kc_ref[b]], axis=0)
                vv = jnp.concatenate([vp_ref[b], vc_ref[b]], axis=0)
                dk_acc = jnp.zeros((2 * BLOCK, KV_WIDTH), F32)
                dv_acc = jnp.zeros((2 * BLOCK, KV_WIDTH), F32)
                for h in range(2):
                    qs = _stack_heads(q_ref.at[b], h, lo)
                    dos = _stack_heads(do_ref.at[b], h, lo)
                    pr, ps = _group_probs(qs, kk, valid, _sink_rows(sink_ref, h))
                    dp = _dot_nt(dos, vv)
                    delta = jnp.sum(pr * dp, axis=1, keepdims=True)
                    ds = (pr * (dp - delta)).astype(BF16)
                    dsk = ps * delta
                    for g in range(GROUP):
                        idx = GROUP * h + g
                        dsink_ref[idx:idx + 1, :] += (jnp.zeros((1, LANES), F32)
                                                      - jnp.sum(dsk[BLOCK * g:BLOCK * (g + 1)]))
                    dk_acc = dk_acc + _dot_tn(ds, qs)
                    dv_acc = dv_acc + _dot_tn(pr.astype(BF16), dos)
                    for j, pair in enumerate(_unstack_heads(_dot(ds, kk) * SCALE, h, lo)):
                        sl = slice(LANES * (2 * h + j), LANES * (2 * h + j + 1))
                        dq_ref[b, :, sl] = _rot_bwd(pair, c_ref[...], a_ref[...], bt_ref[...]).astype(BF16)
                fin_k = ck_ref[b] + dk_acc[:BLOCK]
                dk_ref[b] = _rot_bwd(fin_k, cp_ref[...], ap_ref[...], btp_ref[...]).astype(BF16)
                dv_ref[b] = (cv_ref[b] + dv_acc[:BLOCK]).astype(BF16)
                ck_ref[b] = dk_acc[BLOCK:]
                cv_ref[b] = dv_acc[BLOCK:]

        @pl.when(n == nb)
        def _():
            for b in range(bl):
                dk_ref[b] = _rot_bwd(ck_ref[b], cp_ref[...], ap_ref[...], btp_ref[...]).astype(BF16)
                dv_ref[b] = cv_ref[b].astype(BF16)

    cur = lambda n: (0, jnp.minimum(n, nb - 1), 0)
    prv = lambda n: (0, jnp.clip(n - 1, 0, nb - 1), 0)
    tcur = lambda n: (jnp.minimum(n, nb - 1), 0)
    tprv = lambda n: (jnp.clip(n - 1, 0, nb - 1), 0)
    wide = lambda m: pl.BlockSpec((bl, BLOCK, ATTN_WIDTH), m)
    kv = lambda m: pl.BlockSpec((bl, BLOCK, KV_WIDTH), m)
    tab = lambda m: pl.BlockSpec((BLOCK, LANES), m)
    tile = lambda n: (jnp.minimum(n, nb - 1), 0)
    halo = lambda n: (jnp.minimum((jnp.minimum(n, nb - 1) + 1) * per, last_halo), 0)
    rows = pl.BlockSpec((tp, POOL_WIDTH), tile)
    res = _call(
        body, name="mixers_bwd", grid=(steps,),
        in_specs=[pl.BlockSpec(memory_space=pltpu.SMEM), wide(cur), wide(cur), kv(prv), kv(cur), kv(prv), kv(cur),
                  tab(tcur), tab(tcur), tab(tcur), tab(tprv), tab(tprv), tab(tprv),
                  rows, pl.BlockSpec((HALO, POOL_WIDTH), halo), rows, _const((4, POOL_GC, POOL_GC)),
                  _const((1, POOL_WIDTH))],
        out_specs=[wide(cur), kv(prv), kv(prv), _const((8, LANES)), rows, _const((4, POOL_GC, POOL_GC)),
                   _const((1, POOL_WIDTH))],
        out_shape=[_sds((bl, seq, ATTN_WIDTH), BF16), _sds((bl, seq, KV_WIDTH), BF16),
                   _sds((bl, seq, KV_WIDTH), BF16), _sds((8, LANES), F32), _sds((T, POOL_WIDTH), BF16),
                   _sds((4, POOL_GC, POOL_GC), F32), _sds((1, POOL_WIDTH), F32)],
        scratch=[pltpu.VMEM((bl, BLOCK, KV_WIDTH), F32), pltpu.VMEM((bl, BLOCK, KV_WIDTH), F32)],
        args=(sinks, *_by_example(bl, q, do, k, k, v, v), *tabs, *tabs, dyp, dyp, diff, w_pool, pool_scale),
        sem=("arbitrary",), exchanges=exchanges)
    outs, rest = (res if exchanges else (res, None))
    outs = [outs[0].reshape(T, ATTN_WIDTH), outs[1].reshape(T, KV_WIDTH), outs[2].reshape(T, KV_WIDTH), *outs[3:]]
    return (outs, rest) if exchanges else outs


def _pool_bwd_tile(i, tp, nseq, dy_ref, nxt_ref, diff_ref, w_ref, s_ref, du_ref, dw_ref, ds_ref):
    last = (i % nseq) == nseq - 1
    nxt = jnp.where(last, 0.0, nxt_ref[...])
    ext = jnp.concatenate([dy_ref[...], nxt], axis=0) * s_ref[...]
    pos = (i % nseq) * tp + lax.broadcasted_iota(jnp.int32, (tp + HALO, 1), 0)
    for gi, w in enumerate(POOL_WINDOWS):
        sl = slice(POOL_GC * gi, POOL_GC * (gi + 1))
        wg = w_ref[gi].astype(BF16)
        dmx = ext[:, sl].astype(BF16)
        ddiff = _dot_nt(dmx, wg)
        s = ddiff * _inv_count(pos, w)
        sh = 1
        while sh < w:
            s = s + pltpu.roll(s, tp + HALO - sh, 0)
            sh *= 2
        du_ref[:, sl] = (s[:tp] - ddiff[:tp]).astype(BF16)
        dg = diff_ref[:, sl]
        dw_ref[gi] += _dot_tn(dg, dmx[:tp])
        ds_ref[:, sl] += jnp.sum(dy_ref[:, sl] * _dot(dg, wg), axis=0, keepdims=True)


_PARTS = ((0, C_Q), (C_Q, C_K), (C_K, C_V), (C_V, C_G), (C_G, IN_WIDTH))


def _inproj_bwd(parts, x2, dx1, w_in_t, g1, exchanges=()):
    T = x2.shape[0]
    tm = min(TM, T)

    def body(du_ref, dq_ref, dk_ref, dv_ref, dgt_ref, x_ref, dx1_ref, w_ref, g_ref, gx_ref, dg_ref):
        @pl.when(pl.program_id(0) == 0)
        def _():
            dg_ref[...] = jnp.zeros_like(dg_ref)

        dh = jnp.zeros((tm, D_MODEL), F32)
        for (lo, hi), p_ref in zip(_PARTS, (du_ref, dq_ref, dk_ref, dv_ref, dgt_ref)):
            dh = dh + _dot(p_ref[...], w_ref[lo:hi, :])
        dx, dg = _norm_bwd(x_ref[...], g_ref[...], dh)
        gx_ref[...] = dx1_ref[...] + dx
        dg_ref[...] += dg

    return _call(
        body, name="inproj_bwd", grid=(T // tm,),
        in_specs=[_rows(tm, hi - lo) for lo, hi in _PARTS]
        + [_rows(tm, D_MODEL), _rows(tm, D_MODEL), _const((IN_WIDTH, D_MODEL)), _const((1, D_MODEL))],
        out_specs=[_rows(tm, D_MODEL), _const((1, D_MODEL))],
        out_shape=[_sds((T, D_MODEL), F32), _sds((1, D_MODEL), F32)],
        args=(*parts, x2, dx1, w_in_t, g1), sem=("arbitrary",), exchanges=exchanges)


def _dw_in(h, parts, exchanges=()):
    T = h.shape[0]
    tk = min(TM, T)

    def body(h_ref, du_ref, dq_ref, dk_ref, dv_ref, dgt_ref, o_ref, db_ref):
        @pl.when(pl.program_id(0) == 0)
        def _():
            o_ref[...] = jnp.zeros_like(o_ref)
            db_ref[...] = jnp.zeros_like(db_ref)

        hh = h_ref[...]
        ones = jnp.ones((8, tk), BF16)
        for (lo, hi), p_ref in zip(_PARTS, (du_ref, dq_ref, dk_ref, dv_ref, dgt_ref)):
            part = p_ref[...]
            o_ref[lo:hi, :] += _dot_tn(part, hh)
            db_ref[:, lo:hi] += _dot(ones, part)[:1]

    return _call(
        body, name="dw_in", grid=(T // tk,),
        in_specs=[_rows(tk, D_MODEL)] + [_rows(tk, hi - lo) for lo, hi in _PARTS],
        out_specs=[_const((IN_WIDTH, D_MODEL)), _const((1, IN_WIDTH))],
        out_shape=[_sds((IN_WIDTH, D_MODEL), F32), _sds((1, IN_WIDTH), F32)],
        args=(h, *parts), sem=("arbitrary",), exchanges=exchanges)


def _row_tile(rows, cap=256, mult=16):
    best = None
    for t in range(mult, min(rows, cap) + 1, mult):
        if rows % t == 0:
            best = t
    if best is None:
        raise ValueError("no row tile for %d rows" % rows)
    return best


def _pair_sum(ids, full, got):
    _, r, c = full.shape
    hr = r // 2
    tr = _row_tile(hr)
    nblk = hr // tr

    def body(ids_ref, a_ref, b_ref, own_ref, sb_ref):
        s = a_ref[...] + b_ref[...]
        sb_ref[...] = s.astype(BF16)

        @pl.when(pl.program_id(1) == ids_ref[0])
        def _():
            own_ref[...] = s

    slab = pl.BlockSpec((None, tr, c), lambda i, j, ids_ref: (j, i, 0))
    return pl.pallas_call(
        body, name="pair_sum_%dx%d" % (r, c),
        grid_spec=pltpu.PrefetchScalarGridSpec(
            num_scalar_prefetch=1, grid=(nblk, N_CHIPS),
            in_specs=[pl.BlockSpec((None, tr, c), lambda i, j, ids_ref: (j, ids_ref[1] * nblk + i, 0)), slab],
            out_specs=[pl.BlockSpec((tr, c), lambda i, j, ids_ref: (i, 0)), slab]),
        out_shape=[_sds((hr, c), F32), _sds((N_CHIPS, hr, c), BF16)],
        compiler_params=_cp("parallel", "arbitrary"),
    )(ids, full, got)


def _pair_sum_small(ids, fulls, gots):
    n = len(fulls)
    dims = [(f.shape[1] // 2, f.shape[2]) for f in fulls]

    def body(ids_ref, *refs):
        ins, outs = refs[:2 * n], refs[2 * n:]
        for k in range(n):
            s = ins[2 * k][...] + ins[2 * k + 1][...]
            outs[2 * k + 1][...] = s.astype(BF16)

            @pl.when(pl.program_id(0) == ids_ref[0])
            def _(k=k, s=s):
                outs[2 * k][...] = s

    in_specs, out_specs, out_shape = [], [], []
    for hr, c in dims:
        slab = pl.BlockSpec((None, hr, c), lambda j, ids_ref: (j, 0, 0))
        in_specs += [pl.BlockSpec((None, hr, c), lambda j, ids_ref: (j, ids_ref[1], 0)), slab]
        out_specs += [pl.BlockSpec((hr, c), lambda j, ids_ref: (0, 0)), slab]
        out_shape += [_sds((hr, c), F32), _sds((N_CHIPS, hr, c), BF16)]
    res = pl.pallas_call(
        body, name="pair_sum_small",
        grid_spec=pltpu.PrefetchScalarGridSpec(num_scalar_prefetch=1, grid=(N_CHIPS,), in_specs=in_specs,
                                               out_specs=out_specs),
        out_shape=out_shape, compiler_params=_cp("arbitrary"),
    )(ids, *[a for pair in zip(fulls, gots) for a in pair])
    return [(res[2 * k], res[2 * k + 1]) for k in range(n)]


def _chip_sum_small(ids, owns, gots):
    n = len(owns)

    def body(ids_ref, *refs):
        ins, outs = refs[:2 * n], refs[2 * n:]
        for k in range(n):
            a, b = ins[2 * k], ins[2 * k + 1]
            outs[k][...] = ((a[...] + b[0].astype(F32)) + b[1].astype(F32)) + b[2].astype(F32)

    in_specs, out_specs, out_shape = [], [], []
    for own in owns:
        hr, c = own.shape
        in_specs += [pl.BlockSpec((hr, c), lambda i, ids_ref: (0, 0)),
                     pl.BlockSpec((3, hr, c), lambda i, ids_ref: (0, 0, 0))]
        out_specs.append(pl.BlockSpec((hr, c), lambda i, ids_ref: (ids_ref[1], 0)))
        out_shape.append(_sds((2 * hr, c), F32))
    return pl.pallas_call(
        body, name="chip_sum_small",
        grid_spec=pltpu.PrefetchScalarGridSpec(num_scalar_prefetch=1, grid=(1,), in_specs=in_specs,
                                               out_specs=out_specs),
        out_shape=out_shape, compiler_params=_cp("arbitrary"),
    )(ids, *[a for pair in zip(owns, gots) for a in pair])


def _chip_sum(ids, own, got):
    hr, c = own.shape
    tr = _row_tile(hr)
    nblk = hr // tr

    def body(ids_ref, a_ref, b_ref, o_ref):
        o_ref[...] = ((a_ref[...] + b_ref[0].astype(F32)) + b_ref[1].astype(F32)) + b_ref[2].astype(F32)

    return pl.pallas_call(
        body, name="chip_sum_%dx%d" % (hr, c),
        grid_spec=pltpu.PrefetchScalarGridSpec(
            num_scalar_prefetch=1, grid=(nblk,),
            in_specs=[pl.BlockSpec((tr, c), lambda i, ids_ref: (i, 0)),
                      pl.BlockSpec((3, tr, c), lambda i, ids_ref: (0, i, 0))],
            out_specs=pl.BlockSpec((tr, c), lambda i, ids_ref: (ids_ref[1] * nblk + i, 0))),
        out_shape=_sds((2 * hr, c), F32),
        compiler_params=_cp("parallel"),
    )(ids, own, got)


def _adamw_math(w, g, m, v):
    nm = ADAM_B1 * m + (1.0 - ADAM_B1) * g
    nv = ADAM_B2 * v + (1.0 - ADAM_B2) * (g * g)
    m_hat = nm / (1.0 - ADAM_B1 ** ADAM_STEP)
    v_hat = nv / (1.0 - ADAM_B2 ** ADAM_STEP)
    return -ADAM_LR * (m_hat / (jnp.sqrt(v_hat) + ADAM_EPS) + ADAM_WD * w), nm, nv


def _adamw(w, g, m, v):
    r, c = w.shape
    tr = _row_tile(r, cap=512, mult=8)

    def body(w_ref, g_ref, m_ref, v_ref, d_ref, nm_ref, nv_ref):
        d_ref[...], nm_ref[...], nv_ref[...] = _adamw_math(w_ref[...], g_ref[...], m_ref[...], v_ref[...])

    spec = _rows(tr, c)
    return pl.pallas_call(
        body, name="adamw_%dx%d" % (r, c), grid=(r // tr,),
        in_specs=[spec] * 4, out_specs=[spec] * 3, out_shape=[_sds((r, c), F32)] * 3,
        compiler_params=_cp("parallel"),
    )(w, g, m, v)


SC_TILES = 32
SC_LANES = 16
SC_ROWS = 8


def _adamw_sparse(w, g, m, v):
    r, c = w.shape
    rows = r // SC_TILES
    step = min(rows, SC_ROWS)

    def body(w_hbm, g_hbm, m_hbm, v_hbm, d_hbm, nm_hbm, nv_hbm, wb, gb, mb, vb):
        tile = lax.axis_index("sc_subcore") * 2 + lax.axis_index("sc_core")

        @pl.loop(0, rows, step=step)
        def _(r0):
            mine = pl.ds(tile * rows + r0, step)
            for src, dst in ((w_hbm, wb), (g_hbm, gb), (m_hbm, mb), (v_hbm, vb)):
                pltpu.sync_copy(src.at[mine], dst)

            @pl.loop(0, step)
            def _(row):
                @pl.loop(0, c, step=SC_LANES)
                def _(i):
                    at = (row, pl.ds(i, SC_LANES))
                    wb[at], mb[at], vb[at] = _adamw_math(wb[at], gb[at], mb[at], vb[at])

            for src, dst in ((wb, d_hbm), (mb, nm_hbm), (vb, nv_hbm)):
                pltpu.sync_copy(src, dst.at[mine])

    return pl.kernel(
        body, name="adamw_sparse_%dx%d" % (r, c), out_type=[_sds((r, c), F32)] * 3,
        mesh=plsc.VectorSubcoreMesh(core_axis_name="sc_core", subcore_axis_name="sc_subcore"),
        scratch_types=[pltpu.VMEM((step, c), F32)] * 4,
    )(w, g, m, v)


_SMALL_NAMES = ("w_pool", "b_in", "g_mix_pre", "g_mix_post", "g_mlp_pre", "g_mlp_post", "pool_scale", "attn_sinks")
B_ROWS = -(-IN_WIDTH // D_MODEL)


def _row_block(rows):
    rows = [jnp.pad(r.astype(F32), ((0, 0), (0, D_MODEL - r.shape[1]))) for r in rows]
    return jnp.pad(jnp.concatenate(rows, axis=0), ((0, 8 - len(rows)), (0, 0)))


def _early_block(dg2, dg3, dg4, dps, dsink, loss):
    tail = jnp.concatenate([jnp.pad(dsink.reshape(1, -1), ((0, 0), (0, LANES - dsink.size))),
                            jnp.pad(loss.reshape(1, 1), ((0, 0), (0, LANES - 1)))], axis=1)
    return _row_block([dg2, dg3, dg4, dps, tail])


def _late_block(db_in, dg1):
    b = jnp.pad(db_in, ((0, 0), (0, B_ROWS * D_MODEL - IN_WIDTH))).reshape(B_ROWS, D_MODEL)
    return _row_block([b[r:r + 1] for r in range(B_ROWS)] + [dg1])


def _small_update(gearly, gmat, glate, w, m, v):
    names = _SMALL_NAMES
    n = len(names)

    def total(ref, rows):
        acc = ref[0:rows, :]
        for d in range(1, N_DEV):
            acc = acc + ref[d * rows:(d + 1) * rows, :]
        return acc

    def body(*refs):
        early_ref, gmat_ref, late_ref = refs[:3]
        w_refs, m_refs, v_refs = refs[3:3 + n], refs[3 + n:3 + 2 * n], refs[3 + 2 * n:3 + 3 * n]
        outs = refs[3 + 3 * n:]
        loss_ref, g_refs, d_refs = outs[0], outs[1:1 + n], outs[1 + n:1 + 2 * n]
        nm_refs, nv_refs = outs[1 + 2 * n:1 + 3 * n], outs[1 + 3 * n:1 + 4 * n]
        early, late = total(early_ref, 8), total(late_ref, 8)
        loss_ref[...] = jnp.sum(early[4:5, LANES:2 * LANES], axis=1, keepdims=True)
        bias = jnp.concatenate([late[r:r + 1, :] for r in range(B_ROWS - 1)]
                               + [late[B_ROWS - 1:B_ROWS, :IN_WIDTH - (B_ROWS - 1) * D_MODEL]], axis=1)
        grad = dict(b_in=bias, g_mix_pre=late[B_ROWS:B_ROWS + 1, :], g_mix_post=early[0:1, :],
                    g_mlp_pre=early[1:2, :], g_mlp_post=early[2:3, :], pool_scale=early[3:4, :POOL_WIDTH],
                    attn_sinks=early[4:5, :N_Q_HEADS])
        for i, name in enumerate(names):
            g = total(gmat_ref, 4 * POOL_GC) if name == "w_pool" else grad[name]
            g_refs[i][...] = g
            d_refs[i][...], nm_refs[i][...], nv_refs[i][...] = _adamw_math(
                w_refs[i][...], g, m_refs[i][...], v_refs[i][...])

    shapes = [_sds(w[k].shape, F32) for k in names]
    res = pl.pallas_call(
        body, name="small_update", out_shape=[_sds((1, 1), F32)] + shapes * 4,
        compiler_params=pltpu.CompilerParams(vmem_limit_bytes=VMEM_MB * 1024 * 1024),
    )(gearly, gmat, glate, *[w[k] for k in names], *[m[k] for k in names], *[v[k] for k in names])
    loss = res[0]
    per = {k: tuple(res[1 + j * n + i] for j in range(4)) for i, k in enumerate(names)}
    return loss, per


_BIG = ("w_in", "w_branch_pool", "w_branch_attn", "w_out", "w_up", "w_down")
_ORDER = ("g_mix_pre", "w_in", "b_in", "w_pool", "pool_scale", "attn_sinks", "w_branch_pool", "w_branch_attn",
          "w_out", "g_mix_post", "g_mlp_pre", "w_up", "w_down", "g_mlp_post")


def _stack_rows(slab):
    return slab.reshape(-1, slab.shape[2])


def _step(x2, tgt, seq, shards, small, ids):
    tabs = _rope_tables(seq)
    g1, g2, g3, g4 = (small[n] for n in ("g_mix_pre", "g_mix_post", "g_mlp_pre", "g_mlp_post"))
    sinks = small["attn_sinks"].reshape(N_Q_HEADS)
    w_pool = small["w_pool"].reshape(4, POOL_GC, POOL_GC)
    pool_scale = small["pool_scale"]

    def whole(shard, slabs):
        return lax.dynamic_update_slice(slabs, shard[None], (ids[0], 0, 0))

    (up_a, up_b, down_a, down_b, *mix_shards), [[in_slab]] = _cast_shards(
        *(shards[n] for n in ("w_up", "w_down", "w_branch_pool", "w_branch_attn", "w_out")),
        exchanges=[_ex_gather([shards["w_in"]])])
    w_in = _stack_rows(whole(shards["w_in"], in_slab))
    (h, u, q, k, v, gate), [mix_slabs] = _inproj(
        x2, g1, w_in, small["b_in"], tabs, seq, exchanges=[_ex_gather(mix_shards)])
    w_bp, w_ba, out_slab = (whole(s, g) for s, g in zip(mix_shards, mix_slabs))
    w_out = _stack_rows(out_slab)
    (y_attn, diff, y_pool), [[got_a, got_b]] = _mixers_fwd(
        q, k, v, sinks, u, w_pool, pool_scale, seq, exchanges=[_ex_gather([up_a, up_b])])
    (merged, mix, x1, h2), [[got_c, got_d]] = _merge_out(
        y_pool, y_attn, gate, x2, w_bp, w_ba, w_out, g2, g3, exchanges=[_ex_gather([down_a, down_b])])
    w_up = (whole(up_a, got_a), whole(up_b, got_b))
    w_down = (whole(down_a, got_c), whole(down_b, got_d))
    act, dff, dup, dx1, dmix, loss_acc, dg4, dg3, dg2 = _mlp_core(h2, x1, mix, tgt, w_up, w_down, g4, g3, g2)

    dw_down = _dw("down", act, dff, 2048, 1024)[0].reshape(N_CHIPS, D_FF // N_CHIPS, D_MODEL)
    (dbp, dba, dgate, dyp, dya), [[got]] = _merge_bwd(
        dmix, gate, y_pool, y_attn, w_out, w_bp, w_ba, exchanges=[_ex_pair([dw_down])])
    ps_down = _pair_sum(ids, dw_down, got)
    (dw_up,), [[got]] = _dw("up", h2, dup, 1024, 2048, shard_cols=True, exchanges=[_ex_chip([ps_down[1]])])
    half_down = _chip_sum(ids, ps_down[0], got)
    (dw_out, dw_bp, dw_ba), [[got]] = _dw_mix(merged, dmix, y_pool, dbp, y_attn, dba, exchanges=[_ex_pair([dw_up])])
    ps_up = _pair_sum(ids, dw_up, got)
    dw_mix = [dw_out.reshape(N_CHIPS, D_MODEL // N_CHIPS, D_MODEL), dw_bp, dw_ba]
    (dq, dk, dv, dsink, du, dw_pool, dps), [[got], gots, [g_down]] = _mixers_bwd(
        q, k, v, dya, sinks, tabs, dyp, diff, w_pool, pool_scale, seq,
        exchanges=[_ex_chip([ps_up[1]]), _ex_pair(dw_mix), _ex_swap([half_down])])
    half_up = _chip_sum(ids, ps_up[0], got)
    ps_mix = _pair_sum_small(ids, dw_mix, gots)
    parts = (du, dq, dk, dv, dgate)
    early = _early_block(dg2, dg3, dg4, dps, dsink[:, 0], loss_acc[0, 0])
    mat = dw_pool.reshape(4 * POOL_GC, POOL_GC)
    (dw_in_t, db_in), [gots, [gearly, gmat], [g_up]] = _dw_in(
        h, parts, exchanges=[_ex_chip([p[1] for p in ps_mix]), _ex_allgather([early, mat]), _ex_swap([half_up])])
    half_mix = _chip_sum_small(ids, [p[0] for p in ps_mix], gots)
    dw_in = dw_in_t.reshape(N_CHIPS, IN_WIDTH // N_CHIPS, D_MODEL)
    g_mix, [got] = _alone("swap_mix_pair_in", _ex_swap(half_mix), _ex_pair([dw_in]))
    ps_in = _pair_sum(ids, dw_in, got)
    (gx, dg1), [[got]] = _inproj_bwd(parts, x2, dx1, w_in, g1, exchanges=[_ex_chip([ps_in[1]])])
    [g_in], [glate] = _alone("swap_in_allgather", _ex_swap([_chip_sum(ids, ps_in[0], got)]),
                             _ex_allgather([_late_block(db_in, dg1)]))

    grads = dict(w_in=g_in, w_branch_pool=g_mix[1], w_branch_attn=g_mix[2], w_out=g_mix[0], w_up=g_up, w_down=g_down)
    return (gearly, gmat, glate), gx, grads


def kernel(x, g_mix_pre, w_in, b_in, w_pool, pool_scale, attn_sinks, w_branch_pool, w_branch_attn, w_out, g_mix_post, g_mlp_pre, w_up, w_down, g_mlp_post, loss_target, m_g_mix_pre, m_w_in, m_b_in, m_w_pool, m_pool_scale, m_attn_sinks, m_w_branch_pool, m_w_branch_attn, m_w_out, m_g_mix_post, m_g_mlp_pre, m_w_up, m_w_down, m_g_mlp_post, v_g_mix_pre, v_w_in, v_b_in, v_w_pool, v_pool_scale, v_attn_sinks, v_w_branch_pool, v_w_branch_attn, v_w_out, v_g_mix_post, v_g_mlp_pre, v_w_up, v_w_down, v_g_mlp_post):
    weights = dict(g_mix_pre=g_mix_pre, w_in=w_in, b_in=b_in, w_pool=w_pool, pool_scale=pool_scale,
                   attn_sinks=attn_sinks, w_branch_pool=w_branch_pool, w_branch_attn=w_branch_attn, w_out=w_out,
                   g_mix_post=g_mix_post, g_mlp_pre=g_mlp_pre, w_up=w_up, w_down=w_down, g_mlp_post=g_mlp_post)
    mom1 = dict(g_mix_pre=m_g_mix_pre, w_in=m_w_in, b_in=m_b_in, w_pool=m_w_pool, pool_scale=m_pool_scale,
                attn_sinks=m_attn_sinks, w_branch_pool=m_w_branch_pool, w_branch_attn=m_w_branch_attn,
                w_out=m_w_out, g_mix_post=m_g_mix_post, g_mlp_pre=m_g_mlp_pre, w_up=m_w_up, w_down=m_w_down,
                g_mlp_post=m_g_mlp_post)
    mom2 = dict(g_mix_pre=v_g_mix_pre, w_in=v_w_in, b_in=v_b_in, w_pool=v_w_pool, pool_scale=v_pool_scale,
                attn_sinks=v_attn_sinks, w_branch_pool=v_w_branch_pool, w_branch_attn=v_w_branch_attn,
                w_out=v_w_out, g_mix_post=v_g_mix_post, g_mlp_pre=v_g_mlp_pre, w_up=v_w_up, w_down=v_w_down,
                g_mlp_post=v_g_mlp_post)
    b_loc, seq, _ = x.shape
    x2 = x.reshape(b_loc * seq, D_MODEL)
    tgt = loss_target.reshape(b_loc * seq, D_MODEL)
    ids = jnp.stack([2 * lax.axis_index("x") + lax.axis_index("y"), lax.axis_index("c")]).astype(jnp.int32)

    def flat(n, a):
        return a[0].T if n == "w_in" else a[0]

    def unflat(n, a):
        return (a.T if n == "w_in" else a)[None]

    shards = {n: flat(n, weights[n]).astype(BF16) if n == "w_in" else flat(n, weights[n]) for n in _BIG}
    small = {n: weights[n] for n in _ORDER if n not in _BIG}
    (gearly, gmat, glate), gx, grads = _step(x2, tgt, seq, shards, small, ids)

    def two_d(src):
        return {n: src[n].reshape(4 * POOL_GC, POOL_GC) if n == "w_pool" else src[n] for n in _SMALL_NAMES}

    loss, per = _small_update(gearly, gmat, glate, two_d(weights), two_d(mom1), two_d(mom2))
    delta, new_m, new_v = {}, {}, {}
    for n in _SMALL_NAMES:
        grads[n], delta[n], new_m[n], new_v[n] = (a.reshape(weights[n].shape) for a in per[n])
    for n in _BIG:
        update = _adamw if n == "w_in" else _adamw_sparse
        d, nm, nv = update(flat(n, weights[n]), grads[n], flat(n, mom1[n]), flat(n, mom2[n]))
        grads[n] = unflat(n, grads[n])
        delta[n], new_m[n], new_v[n] = unflat(n, d), unflat(n, nm), unflat(n, nv)

    return (loss[0, 0], gx.reshape(x.shape), *[grads[n] for n in _ORDER], *[delta[n] for n in _ORDER],
            *[new_m[n] for n in _ORDER], *[new_v[n] for n in _ORDER])
```

```python
import jax
import jax.numpy as jnp
from jax import lax
from jax.experimental import pallas as pl
from jax.experimental.pallas import tpu as pltpu
from jax.experimental.pallas import tpu_sc as plsc

F32 = jnp.float32
BF16 = jnp.bfloat16

D_MODEL = 1024
POOL_WINDOWS = (2, 4, 8, 16)
POOL_WIDTH = 512
POOL_GC = 128
HALO = 16
HEAD_DIM = 64
N_Q_HEADS = 8
ATTN_WIDTH = 512
KV_WIDTH = 128
BLOCK = 128
NEG_INF = -1e30
ROPE_THETA = 500000.0
ROT_DIM = 16
GATE_WIDTH = 2048
IN_WIDTH = 3328
D_FF = 4096
EPS = 1e-6
SCALE = HEAD_DIM ** -0.5
C_Q, C_K, C_V, C_G = 512, 1024, 1152, 1280

ADAM_LR, ADAM_B1, ADAM_B2, ADAM_EPS, ADAM_WD, ADAM_STEP = 0.001, 0.9, 0.999, 1e-08, 0.01, 10

N_CHIPS = 4
N_DEV = 8
LANES = 128
TM = 512
VMEM_MB = 56

MESH = pl.DeviceIdType.MESH
ANY = pl.BlockSpec(memory_space=pl.ANY)


def _cp(*sem, vmem=VMEM_MB):
    return pltpu.CompilerParams(dimension_semantics=sem, vmem_limit_bytes=vmem * 1024 * 1024)


def _rows(tile, cols):
    return pl.BlockSpec((tile, cols), lambda i: (i, 0))


def _const(shape):
    nd = len(shape)
    return pl.BlockSpec(shape, lambda i: (0,) * nd)


def _sds(shape, dtype):
    return jax.ShapeDtypeStruct(shape, dtype)


def _dot(a, b):
    return jnp.dot(a, b, preferred_element_type=F32)


def _dot_nt(a, b):
    return lax.dot_general(a, b, (((1,), (1,)), ((), ())), preferred_element_type=F32)


def _dot_tn(a, b):
    return lax.dot_general(a, b, (((0,), (0,)), ((), ())), preferred_element_type=F32)


def _rms(x):
    return lax.rsqrt(jnp.mean(x * x, axis=-1, keepdims=True) + EPS)


def _norm_bwd(x, g, dout):
    r = _rms(x)
    n = x * r
    dn = dout * g
    dx = r * (dn - n * jnp.mean(dn * n, axis=-1, keepdims=True))
    return dx, jnp.sum(dout * n, axis=0, keepdims=True)


def _rot_fwd(t, c, a, bt):
    return t * c + pltpu.roll(t, LANES - 8, 1) * a + pltpu.roll(t, 8, 1) * bt


def _rot_bwd(d, c, a, bt):
    return d * c + pltpu.roll(d * a, 8, 1) + pltpu.roll(d * bt, LANES - 8, 1)


def _rope_tables(seq):
    pos = jnp.arange(seq, dtype=F32)
    inv_freq = ROPE_THETA ** (-jnp.arange(0, ROT_DIM, 2, dtype=F32) / ROT_DIM)
    ang = pos[:, None] * inv_freq[None, :]
    cos, sin = jnp.cos(ang), jnp.sin(ang)
    ones = jnp.ones((seq, HEAD_DIM - ROT_DIM), F32)
    zeros8 = jnp.zeros((seq, 8), F32)
    zrest = jnp.zeros((seq, HEAD_DIM - ROT_DIM), F32)
    c = jnp.concatenate([cos, cos, ones], axis=1)
    a = jnp.concatenate([-sin, zeros8, zrest], axis=1)
    bt = jnp.concatenate([zeros8, sin, zrest], axis=1)
    return tuple(jnp.tile(t, (1, 2)) for t in (c, a, bt))


class _Exchange:
    def __init__(self, inputs, out_shapes, sems, start, finish, aliases=None, middle=None):
        self.inputs, self.out_shapes, self.sems = list(inputs), list(out_shapes), list(sems)
        self.start, self.finish, self.aliases = start, finish, dict(aliases or {})
        self.middle = middle


def _call(body, *, name, grid, in_specs, out_specs, out_shape, args, scratch=(), sem=(), exchanges=()):
    in_specs, out_specs, out_shape, scratch = list(in_specs), list(out_specs), list(out_shape), list(scratch)
    if not exchanges:
        return pl.pallas_call(body, name=name, grid=grid, in_specs=in_specs, out_specs=out_specs,
                              out_shape=out_shape, scratch_shapes=scratch, compiler_params=_cp(*sem))(*args)
    n_in, n_out, n_scr = len(in_specs), len(out_specs), len(scratch)
    x_in = [a for ex in exchanges for a in ex.inputs]
    x_out = [s for ex in exchanges for s in ex.out_shapes]
    x_sem = [s for ex in exchanges for s in ex.sems]
    aliases, i_off, o_off = {}, n_in, n_out
    for ex in exchanges:
        for i, o in ex.aliases.items():
            aliases[i_off + i] = o_off + o
        i_off += len(ex.inputs)
        o_off += len(ex.out_shapes)

    def split(flat):
        out, pos = [], 0
        for ex, n in zip(exchanges, flat[1]):
            out.append(flat[0][pos:pos + n])
            pos += n
        return out

    def carrier(*refs):
        pos = 0
        groups = []
        for n in (n_in, len(x_in), n_out, len(x_out), n_scr, len(x_sem)):
            groups.append(refs[pos:pos + n])
            pos += n
        ins, xin, outs, xout, scr, xsem = groups
        xin = split((xin, [len(ex.inputs) for ex in exchanges]))
        xout = split((xout, [len(ex.out_shapes) for ex in exchanges]))
        xsem = split((xsem, [len(ex.sems) for ex in exchanges]))
        first = pl.program_id(0) == 0
        last = pl.program_id(0) == grid[0] - 1
        for d in range(1, len(grid)):
            first = jnp.logical_and(first, pl.program_id(d) == 0)
            last = jnp.logical_and(last, pl.program_id(d) == grid[d] - 1)

        @pl.when(first)
        def _():
            for ex, i, o, s in zip(exchanges, xin, xout, xsem):
                ex.start(i, o, s)

        if any(ex.middle for ex in exchanges):
            half = pl.program_id(0) == 5 * grid[0] // 8
            for d in range(1, len(grid)):
                half = jnp.logical_and(half, pl.program_id(d) == 0)

            @pl.when(half)
            def _():
                for ex, i, o, s in zip(exchanges, xin, xout, xsem):
                    if ex.middle:
                        ex.middle(i, o, s)

        body(*ins, *outs, *scr)

        @pl.when(last)
        def _():
            for ex, i, o, s in zip(exchanges, xin, xout, xsem):
                ex.finish(i, o, s)

    res = pl.pallas_call(
        carrier, name=name, grid=grid, in_specs=in_specs + [ANY] * len(x_in),
        out_specs=out_specs + [ANY] * len(x_out), out_shape=out_shape + x_out,
        scratch_shapes=scratch + x_sem, input_output_aliases=aliases,
        compiler_params=_cp(*(["arbitrary"] * len(grid))),
    )(*args, *x_in)
    return res[:n_out], split((res[n_out:], [len(ex.out_shapes) for ex in exchanges]))


def _alone(name, *exchanges):
    n_in = [len(ex.inputs) for ex in exchanges]
    n_out = [len(ex.out_shapes) for ex in exchanges]
    n_sem = [len(ex.sems) for ex in exchanges]
    aliases, i_off, o_off = {}, 0, 0
    for ex in exchanges:
        for i, o in ex.aliases.items():
            aliases[i_off + i] = o_off + o
        i_off += len(ex.inputs)
        o_off += len(ex.out_shapes)

    def split(flat, counts):
        out, pos = [], 0
        for n in counts:
            out.append(flat[pos:pos + n])
            pos += n
        return out

    def body(*refs):
        ins, outs, sems = split(refs, [sum(n_in), sum(n_out), sum(n_sem)])
        groups = list(zip(exchanges, split(ins, n_in), split(outs, n_out), split(sems, n_sem)))
        for ex, i, o, s in groups:
            ex.start(i, o, s)
        for ex, i, o, s in groups:
            if ex.middle:
                ex.middle(i, o, s)
        for ex, i, o, s in groups:
            ex.finish(i, o, s)

    res = pl.pallas_call(
        body, name=name, in_specs=[ANY] * sum(n_in), out_specs=[ANY] * sum(n_out),
        out_shape=[s for ex in exchanges for s in ex.out_shapes],
        scratch_shapes=[s for ex in exchanges for s in ex.sems], input_output_aliases=aliases,
    )(*[a for ex in exchanges for a in ex.inputs])
    return split(res, n_out)


def _place():
    x, y, c = lax.axis_index("x"), lax.axis_index("y"), lax.axis_index("c")
    chips = [(1 - x, y), (x, 1 - y), (1 - x, 1 - y)]
    return x, y, c, chips


def _remote(src, dst, send, recv, to):
    return pltpu.make_async_remote_copy(src_ref=src, dst_ref=dst, send_sem=send, recv_sem=recv,
                                        device_id=to, device_id_type=MESH)


def _ex_gather(shards):
    nw = len(shards)
    hrs = [s.shape[0] // 2 for s in shards]

    def copies(ins, outs, sems):
        s0, r0, s1, r1, s2, r2, fs, fr = sems
        x, y, c, _ = _place()
        me, xn, yn, dg = (x, y), (1 - x, y), (x, 1 - y), (1 - x, 1 - y)
        nbr = (xn, yn)
        sibling = (x, y, 1 - c)

        def piece(w, chip, core, part=None):
            hr = hrs[w]
            rows = pl.ds(core * hr, hr) if part is None else pl.ds(core * hr + part * (hr // 2), hr // 2)
            return outs[w].at[2 * chip[0] + chip[1], rows]

        def first(w, k, lead):
            part = k if lead else 1 - k
            send, recv = (s0, r0) if lead else (s1, r1)
            rows = pl.ds(c * hrs[w] + part * (hrs[w] // 2), hrs[w] // 2)
            return _remote(ins[w].at[rows], piece(w, me, c, part), send.at[w, k], recv.at[w, k], (*nbr[k], c))

        def landed(w, k, lead):
            part = k if lead else 1 - k
            send, recv = (s0, r0) if lead else (s1, r1)
            return _remote(piece(w, nbr[k], c, part), piece(w, nbr[k], c, part), send.at[w, k], recv.at[w, k],
                           (*nbr[k], c))

        def onward(w, k):
            return _remote(piece(w, nbr[k], c, k), piece(w, nbr[k], c, k), s2.at[w, k], r2.at[w, k],
                           (*nbr[1 - k], c))

        def arrived(w, k):
            return _remote(piece(w, dg, c, k), piece(w, dg, c, k), s2.at[w, k], r2.at[w, k], (*nbr[1 - k], c))

        def passed(w, j):
            chip = (xn, yn, dg)[j]
            return _remote(piece(w, chip, c), piece(w, chip, c), fs.at[w, j], fr.at[w, j], sibling)

        def handed(w, j):
            chip = (xn, yn, dg)[j]
            return _remote(piece(w, chip, 1 - c), piece(w, chip, 1 - c), fs.at[w, j], fr.at[w, j], sibling)

        return first, landed, onward, arrived, passed, handed

    def start(ins, outs, sems):
        first = copies(ins, outs, sems)[0]
        for lead in (True, False):
            for w in range(nw):
                for k in range(2):
                    first(w, k, lead).start()

    def middle(ins, outs, sems):
        _, landed, onward, _, passed, _ = copies(ins, outs, sems)
        for w in range(nw):
            for k in range(2):
                landed(w, k, True).wait_recv()
                onward(w, k).start()
        for w in range(nw):
            for k in range(2):
                landed(w, k, False).wait_recv()
                passed(w, k).start()

    def finish(ins, outs, sems):
        first, _, onward, arrived, passed, handed = copies(ins, outs, sems)
        for w in range(nw):
            for k in range(2):
                arrived(w, k).wait_recv()
            passed(w, 2).start()
        for w in range(nw):
            for j in range(3):
                handed(w, j).wait_recv()
        for w in range(nw):
            for k in range(2):
                first(w, k, True).wait_send()
                first(w, k, False).wait_send()
                onward(w, k).wait_send()
            for j in range(3):
                passed(w, j).wait_send()

    return _Exchange(shards, [_sds((N_CHIPS,) + s.shape, s.dtype) for s in shards],
                     [pltpu.SemaphoreType.DMA((nw, 2))] * 6 + [pltpu.SemaphoreType.DMA((nw, 3))] * 2,
                     start, finish, middle=middle)


def _ex_pair(grads):
    nw = len(grads)

    def copies(ins, outs, sems):
        x, y, c, _ = _place()
        out = []
        for w in range(nw):
            hr = grads[w].shape[1] // 2
            out.append(_remote(ins[w].at[:, pl.ds((1 - c) * hr, hr)], outs[w], sems[0].at[w], sems[1].at[w],
                               (x, y, 1 - c)))
        return out

    def start(ins, outs, sems):
        for cp in copies(ins, outs, sems):
            cp.start()

    def finish(ins, outs, sems):
        for cp in copies(ins, outs, sems):
            cp.wait()

    return _Exchange(grads, [_sds((N_CHIPS, g.shape[1] // 2, g.shape[2]), F32) for g in grads],
                     [pltpu.SemaphoreType.DMA((nw,))] * 2, start, finish)


def _ex_chip(pieces):
    nw = len(pieces)

    def copies(ins, outs, sems):
        x, y, c, chips = _place()
        return [_remote(ins[w].at[2 * cx + cy], outs[w].at[k], sems[0].at[w, k], sems[1].at[w, k], (cx, cy, c))
                for w in range(nw) for k, (cx, cy) in enumerate(chips)]

    def start(ins, outs, sems):
        for cp in copies(ins, outs, sems):
            cp.start()

    def finish(ins, outs, sems):
        for cp in copies(ins, outs, sems):
            cp.wait()

    return _Exchange(pieces, [_sds((3,) + p.shape[1:], BF16) for p in pieces],
                     [pltpu.SemaphoreType.DMA((nw, 3))] * 2, start, finish)


def _ex_swap(fulls):
    nw = len(fulls)

    def start(ins, outs, sems):
        x, y, c, _ = _place()
        for w in range(nw):
            hr = fulls[w].shape[0] // 2
            mine = pl.ds(c * hr, hr)
            _remote(ins[w].at[mine], outs[w].at[mine], sems[0].at[w], sems[1].at[w], (x, y, 1 - c)).start()

    def finish(ins, outs, sems):
        x, y, c, _ = _place()
        for w in range(nw):
            hr = fulls[w].shape[0] // 2
            mine, theirs = pl.ds(c * hr, hr), pl.ds((1 - c) * hr, hr)
            _remote(ins[w].at[mine], outs[w].at[mine], sems[0].at[w], sems[1].at[w], (x, y, 1 - c)).wait_send()
            _remote(ins[w].at[theirs], outs[w].at[theirs], sems[0].at[w], sems[1].at[w], (x, y, 1 - c)).wait_recv()

    return _Exchange(fulls, [_sds(f.shape, F32) for f in fulls], [pltpu.SemaphoreType.DMA((nw,))] * 2,
                     start, finish, aliases={w: w for w in range(nw)})


def _ex_allgather(blocks):
    nb = len(blocks)

    def copies(ins, outs, sems):
        send, recv, lsem = sems
        x, y, c, chips = _place()
        me, sibling = (x, y, c), (x, y, 1 - c)

        def rows(b, px, py, pc):
            m_per = blocks[b].shape[0]
            return outs[b].at[pl.ds((4 * px + 2 * py + pc) * m_per, m_per), :]

        def copy(b, k, blk, to, src=None):
            return _remote(rows(b, *blk) if src is None else src, rows(b, *blk), send.at[b, k], recv.at[b, k], to)

        def mine(b):
            return pltpu.make_async_copy(ins[b], rows(b, *me), lsem.at[b])

        def first(b, k):
            return copy(b, k, me, sibling if k == 0 else (*chips[k - 1], c), src=ins[b])

        def passed(b, j):
            return copy(b, 4 + j, (*chips[j], c), sibling)

        def landed(b, j):
            return copy(b, 1 + j, (*chips[j], c), me)

        def handed(b, k):
            return copy(b, 0, sibling, me) if k == 0 else copy(b, 3 + k, (*chips[k - 1], 1 - c), me)

        return mine, first, passed, landed, handed

    def start(ins, outs, sems):
        mine, first, _, _, _ = copies(ins, outs, sems)
        for b in range(nb):
            mine(b).start()
            for k in range(4):
                first(b, k).start()

    def finish(ins, outs, sems):
        mine, first, passed, landed, handed = copies(ins, outs, sems)
        sent = []
        for b in range(nb):
            for j in range(3):
                landed(b, j).wait_recv()
                cp = passed(b, j)
                cp.start()
                sent.append(cp)
        for b in range(nb):
            for k in range(4):
                handed(b, k).wait_recv()
            for k in range(4):
                first(b, k).wait_send()
        for cp in sent:
            cp.wait_send()
        for b in range(nb):
            mine(b).wait()

    return _Exchange(blocks, [_sds((N_DEV * b.shape[0], b.shape[1]), F32) for b in blocks],
                     [pltpu.SemaphoreType.DMA((nb, 7)), pltpu.SemaphoreType.DMA((nb, 7)), pltpu.SemaphoreType.DMA((nb,))],
                     start, finish)


def _cast_shards(w_up, w_down, w_bp, w_ba, w_out, exchanges=()):
    r, c = w_up.shape
    tr = HALF // 2
    steps = r // tr

    def body(up_ref, down_ref, bp_ref, ba_ref, out_ref, ua_ref, ub_ref, da_ref, db_ref, bpo_ref, bao_ref, outo_ref):
        i = pl.program_id(0)

        @pl.when(i == 0)
        def _():
            for src, dst in ((bp_ref, bpo_ref), (ba_ref, bao_ref), (out_ref, outo_ref)):
                dst[...] = src[...].astype(BF16)

        @pl.when(i < steps // 2)
        def _():
            ua_ref[...] = up_ref[...].astype(BF16)
            da_ref[...] = down_ref[...].astype(BF16)

        @pl.when(i >= steps // 2)
        def _():
            ub_ref[...] = up_ref[...].astype(BF16)
            db_ref[...] = down_ref[...].astype(BF16)

    rows = _rows(tr, c)
    first = pl.BlockSpec((tr, c), lambda i: (jnp.minimum(i, steps // 2 - 1), 0))
    second = pl.BlockSpec((tr, c), lambda i: (jnp.maximum(i - steps // 2, 0), 0))
    half = _sds((HALF, c), BF16)
    return _call(
        body, name="cast_shards", grid=(steps,),
        in_specs=[rows, rows, _const(w_bp.shape), _const(w_ba.shape), _const(w_out.shape)],
        out_specs=[first, second, first, second, _const(w_bp.shape), _const(w_ba.shape), _const(w_out.shape)],
        out_shape=[half, half, half, half, _sds(w_bp.shape, BF16), _sds(w_ba.shape, BF16), _sds(w_out.shape, BF16)],
        args=(w_up, w_down, w_bp, w_ba, w_out), sem=("arbitrary",), exchanges=exchanges)


def _inproj(x2, g1, w_in_t, b_in, tabs, seq, exchanges=()):
    T = x2.shape[0]
    tm = min(TM, seq)
    nseq = seq // tm

    def body(x_ref, g_ref, w_ref, b_ref, c_ref, a_ref, bt_ref, h_ref, u_ref, q_ref, k_ref, v_ref, gate_ref):
        x = x_ref[...]
        h = (x * _rms(x) * g_ref[...]).astype(BF16)
        h_ref[...] = h

        def proj(lo, hi):
            return _dot_nt(h, w_ref[lo:hi, :]) + b_ref[:, lo:hi]

        c, a, bt = c_ref[...], a_ref[...], bt_ref[...]
        u_ref[...] = proj(0, C_Q)
        q = proj(C_Q, C_K)
        for p in range(4):
            sl = slice(LANES * p, LANES * (p + 1))
            q_ref[:, sl] = (_rot_fwd(q[:, sl], c, a, bt) * SCALE).astype(BF16)
        kv = proj(C_K, C_G)
        k_ref[...] = _rot_fwd(kv[:, :KV_WIDTH], c, a, bt).astype(BF16)
        v_ref[...] = kv[:, KV_WIDTH:].astype(BF16)
        for j in range(2):
            lo = C_G + D_MODEL * j
            gate_ref[:, D_MODEL * j:D_MODEL * (j + 1)] = jax.nn.sigmoid(proj(lo, lo + D_MODEL)).astype(BF16)

    tab = pl.BlockSpec((tm, LANES), lambda i: (i % nseq, 0))
    return _call(
        body, name="inproj", grid=(T // tm,),
        in_specs=[_rows(tm, D_MODEL), _const((1, D_MODEL)), _const((IN_WIDTH, D_MODEL)), _const((1, IN_WIDTH)),
                  tab, tab, tab],
        out_specs=[_rows(tm, D_MODEL), _rows(tm, POOL_WIDTH), _rows(tm, ATTN_WIDTH), _rows(tm, KV_WIDTH),
                   _rows(tm, KV_WIDTH), _rows(tm, GATE_WIDTH)],
        out_shape=[_sds((T, D_MODEL), BF16), _sds((T, POOL_WIDTH), F32), _sds((T, ATTN_WIDTH), BF16),
                   _sds((T, KV_WIDTH), BF16), _sds((T, KV_WIDTH), BF16), _sds((T, GATE_WIDTH), BF16)],
        args=(x2, g1, w_in_t, b_in, *tabs), sem=("parallel",), exchanges=exchanges)


def _inv_count(pos, w):
    return 1.0 / jnp.minimum(pos + 1, w).astype(F32)


def _pool_tile(i, tp, nseq, u_ref, prev_ref, w_ref, s_ref, diff_ref, y_ref):
    first = (i % nseq) == 0
    prev = jnp.where(first, 0.0, prev_ref[...])
    ext = jnp.concatenate([prev, u_ref[...]], axis=0)
    pos = (i % nseq) * tp + lax.broadcasted_iota(jnp.int32, (tp, 1), 0)
    for gi, w in enumerate(POOL_WINDOWS):
        sl = slice(POOL_GC * gi, POOL_GC * (gi + 1))
        xg = ext[:, sl]
        s = xg
        sh = 1
        while sh < w:
            s = s + pltpu.roll(s, sh, 0)
            sh *= 2
        pooled = s[HALO:] * _inv_count(pos, w)
        diff = (pooled - xg[HALO:]).astype(BF16)
        diff_ref[:, sl] = diff
        mixed = _dot(diff, w_ref[gi].astype(BF16))
        y_ref[:, sl] = (mixed * s_ref[:, sl]).astype(BF16)


def _pool_specs(tp):
    per = tp // HALO
    return [_rows(tp, POOL_WIDTH), pl.BlockSpec((HALO, POOL_WIDTH), lambda i: (jnp.maximum(i * per - 1, 0), 0)),
            _const((4, POOL_GC, POOL_GC)), _const((1, POOL_WIDTH))]


GROUP = 4
GROWS = GROUP * BLOCK


def _attn_masks(n):
    qi = lax.broadcasted_iota(jnp.int32, (GROWS, 2 * BLOCK), 0) % BLOCK
    kj = lax.broadcasted_iota(jnp.int32, (GROWS, 2 * BLOCK), 1)
    rel = qi + BLOCK - kj
    valid = (rel >= 0) & (rel < BLOCK) & (kj >= jnp.where(n > 0, 0, BLOCK))
    lo = lax.broadcasted_iota(jnp.int32, (BLOCK, LANES), 1) < HEAD_DIM
    return valid, lo


def _by_example(bl, *arrays):
    return [a.reshape(bl, a.shape[0] // bl, a.shape[1]) for a in arrays]


def _stack_heads(ref, h, lo):
    keep = lo if h == 0 else jnp.logical_not(lo)
    pieces = []
    for p in (2 * h, 2 * h + 1):
        xp = ref[:, LANES * p:LANES * (p + 1)].astype(F32)
        for e in range(2):
            t = xp if e == h else pltpu.roll(xp, HEAD_DIM, 1)
            pieces.append(jnp.where(keep, t, 0.0).astype(BF16))
    return jnp.concatenate(pieces, axis=0)


def _unstack_heads(stacked, h, lo):
    pairs = []
    for j in range(2):
        parts = []
        for e in range(2):
            t = stacked[BLOCK * (2 * j + e):BLOCK * (2 * j + e + 1)]
            parts.append(t if e == h else pltpu.roll(t, HEAD_DIM, 1))
        pairs.append(jnp.where(lo, parts[0], parts[1]))
    return pairs


def _sink_rows(sink_ref, h):
    head = lax.broadcasted_iota(jnp.int32, (GROWS, 1), 0) // BLOCK
    col = jnp.zeros((GROWS, 1), F32) + sink_ref[GROUP * h]
    for g in range(1, GROUP):
        col = jnp.where(head == g, sink_ref[GROUP * h + g], col)
    return col


def _group_probs(qs, kk, valid, sink):
    s = jnp.where(valid, _dot_nt(qs, kk), NEG_INF)
    m = jnp.maximum(jnp.max(s, axis=1, keepdims=True), sink)
    ex = jnp.exp(s - m)
    es = jnp.exp(sink - m)
    inv = 1.0 / (jnp.sum(ex, axis=1, keepdims=True) + es)
    return ex * inv, es * inv


def _mixers_fwd(q, k, v, sinks, u, w_pool, pool_scale, seq, exchanges=()):
    T = q.shape[0]
    nb = seq // BLOCK
    bl = T // seq
    tp = T // nb
    nseq = seq // tp

    def body(sink_ref, q_ref, kp_ref, kc_ref, vp_ref, vc_ref, u_ref, prev_ref, w_ref, s_ref, o_ref, diff_ref, y_ref):
        n = pl.program_id(0)
        valid, lo = _attn_masks(n)
        for b in range(bl):
            kk = jnp.concatenate([kp_ref[b], kc_ref[b]], axis=0)
            vv = jnp.concatenate([vp_ref[b], vc_ref[b]], axis=0)
            for h in range(2):
                qs = _stack_heads(q_ref.at[b], h, lo)
                pr, _ = _group_probs(qs, kk, valid, _sink_rows(sink_ref, h))
                o = _dot(pr.astype(BF16), vv)
                for j, pair in enumerate(_unstack_heads(o, h, lo)):
                    p = 2 * h + j
                    o_ref[b, :, LANES * p:LANES * (p + 1)] = pair.astype(BF16)
        _pool_tile(n, tp, nseq, u_ref, prev_ref, w_ref, s_ref, diff_ref, y_ref)

    cur = lambda n: (0, n, 0)
    prv = lambda n: (0, jnp.maximum(n - 1, 0), 0)
    kv = lambda m: pl.BlockSpec((bl, BLOCK, KV_WIDTH), m)
    res = _call(
        body, name="mixers_fwd", grid=(nb,),
        in_specs=[pl.BlockSpec(memory_space=pltpu.SMEM), pl.BlockSpec((bl, BLOCK, ATTN_WIDTH), cur),
                  kv(prv), kv(cur), kv(prv), kv(cur)] + _pool_specs(tp),
        out_specs=[pl.BlockSpec((bl, BLOCK, ATTN_WIDTH), cur), _rows(tp, POOL_WIDTH), _rows(tp, POOL_WIDTH)],
        out_shape=[_sds((bl, seq, ATTN_WIDTH), BF16), _sds((T, POOL_WIDTH), BF16), _sds((T, POOL_WIDTH), BF16)],
        args=(sinks, *_by_example(bl, q, k, k, v, v), u, u, w_pool, pool_scale), sem=("parallel",),
        exchanges=exchanges)
    outs, rest = res if exchanges else (res, None)
    return [outs[0].reshape(T, ATTN_WIDTH), outs[1], outs[2]], rest


def _branch(y, w_ref):
    return jnp.concatenate([_dot(y, w_ref[j]) for j in range(N_CHIPS)], axis=1)


def _merge_out(y_pool, y_attn, gate, x2, w_bp, w_ba, w_out, g2, g3, exchanges=()):
    T = x2.shape[0]
    tm = min(TM, T)

    def body(yp_ref, ya_ref, gate_ref, x_ref, wbp_ref, wba_ref, wo_ref, g2_ref, g3_ref,
             mg_ref, mix_ref, x1_ref, h2_ref):
        bp, ba = _branch(yp_ref[...], wbp_ref), _branch(ya_ref[...], wba_ref)
        merged = (gate_ref[:, :D_MODEL].astype(F32) * bp + gate_ref[:, D_MODEL:].astype(F32) * ba).astype(BF16)
        mg_ref[...] = merged
        mix = _dot(merged, wo_ref[...])
        mix_ref[...] = mix
        x1 = x_ref[...] + mix * _rms(mix) * g2_ref[...]
        x1_ref[...] = x1
        h2_ref[...] = (x1 * _rms(x1) * g3_ref[...]).astype(BF16)

    return _call(
        body, name="merge_out", grid=(T // tm,),
        in_specs=[_rows(tm, POOL_WIDTH), _rows(tm, ATTN_WIDTH), _rows(tm, GATE_WIDTH), _rows(tm, D_MODEL),
                  _const(w_bp.shape), _const(w_ba.shape), _const((D_MODEL, D_MODEL)),
                  _const((1, D_MODEL)), _const((1, D_MODEL))],
        out_specs=[_rows(tm, D_MODEL)] * 4,
        out_shape=[_sds((T, D_MODEL), BF16), _sds((T, D_MODEL), F32), _sds((T, D_MODEL), F32),
                   _sds((T, D_MODEL), BF16)],
        args=(y_pool, y_attn, gate, x2, w_bp, w_ba, w_out, g2, g3), sem=("parallel",), exchanges=exchanges)


HALF = D_MODEL // 2
TM_MLP = 256


def _mlp_core(h2, x1, mix, tgt, w_up, w_down, g4, g3, g2):
    T = h2.shape[0]
    tm = min(TM_MLP, T)

    def body(h_ref, x1_ref, mix_ref, t_ref, g_ref, g3_ref, g2_ref, ua_hbm, ub_hbm, da_hbm, db_hbm,
             act_ref, dff_ref, dup_ref, dx1_ref, dmix_ref, loss_ref, dg_ref, dg3_ref, dg2_ref,
             wu, wd, relu_scr, sems):
        def weight_copy(i):
            src, dst = ((ua_hbm, wu.at[:, :HALF]), (ub_hbm, wu.at[:, HALF:]),
                        (da_hbm, wd.at[:, :HALF]), (db_hbm, wd.at[:, HALF:]))[i]
            return pltpu.make_async_copy(src, dst, sems.at[i])

        @pl.when(pl.program_id(0) == 0)
        def _():
            for i in range(4):
                weight_copy(i).start()
            loss_ref[...] = jnp.zeros_like(loss_ref)
            for ref in (dg_ref, dg3_ref, dg2_ref):
                ref[...] = jnp.zeros_like(ref)
            weight_copy(0).wait()
            weight_copy(1).wait()

        h = h_ref[...]
        ff = None
        for j in range(N_CHIPS):
            lo = D_MODEL * j
            relu = jnp.maximum(_dot(h, wu[j]), 0.0)
            if j == 0:
                @pl.when(pl.program_id(0) == 0)
                def _():
                    weight_copy(2).wait()
                    weight_copy(3).wait()
            relu_scr[:, lo:lo + D_MODEL] = relu
            act = jnp.square(relu).astype(BF16)
            act_ref[:, lo:lo + D_MODEL] = act
            t = _dot(act, wd[j])
            ff = t if ff is None else ff + t
        g = g_ref[...]
        x1 = x1_ref[...]
        err = x1 + ff * _rms(ff) * g - t_ref[...]
        loss_ref[...] += jnp.sum(err * err) * (0.5 / D_MODEL)
        dy = err * (1.0 / D_MODEL)
        dff, dg = _norm_bwd(ff, g, dy)
        dg_ref[...] += dg
        dff = dff.astype(BF16)
        dff_ref[...] = dff
        dh2 = None
        for j in range(N_CHIPS):
            lo = D_MODEL * j
            dup = (_dot_nt(dff, wd[j]) * (2.0 * relu_scr[:, lo:lo + D_MODEL])).astype(BF16)
            dup_ref[:, lo:lo + D_MODEL] = dup
            t = _dot_nt(dup, wu[j])
            dh2 = t if dh2 is None else dh2 + t
        dx, dg3 = _norm_bwd(x1, g3_ref[...], dh2)
        dx1 = dy + dx
        dx1_ref[...] = dx1
        dg3_ref[...] += dg3
        dmix, dg2 = _norm_bwd(mix_ref[...], g2_ref[...], dx1)
        dmix_ref[...] = dmix.astype(BF16)
        dg2_ref[...] += dg2

    slabs = pltpu.VMEM((N_CHIPS, D_MODEL, D_MODEL), BF16)
    gain = _const((1, D_MODEL))
    return pl.pallas_call(
        body, name="mlp_core", grid=(T // tm,),
        in_specs=[_rows(tm, D_MODEL)] * 4 + [gain] * 3 + [ANY] * 4,
        out_specs=[_rows(tm, D_FF), _rows(tm, D_MODEL), _rows(tm, D_FF), _rows(tm, D_MODEL), _rows(tm, D_MODEL),
                   _const((8, LANES)), gain, gain, gain],
        out_shape=[_sds((T, D_FF), BF16), _sds((T, D_MODEL), BF16), _sds((T, D_FF), BF16), _sds((T, D_MODEL), F32),
                   _sds((T, D_MODEL), BF16), _sds((8, LANES), F32)] + [_sds((1, D_MODEL), F32)] * 3,
        scratch_shapes=[slabs] * 2 + [pltpu.VMEM((tm, D_FF), F32), pltpu.SemaphoreType.DMA((4,))],
        compiler_params=_cp("arbitrary"),
    )(h2, x1, mix, tgt, g4, g3, g2, *w_up, *w_down)


def _dw(tag, a, g, ta, tn, shard_cols=False, exchanges=()):
    T, ka = a.shape
    n = g.shape[1]
    tk = min(2 * TM, T)
    nk = T // tk

    def body(a_ref, g_ref, o_ref):
        @pl.when(pl.program_id(2) == 0)
        def _():
            o_ref[...] = jnp.zeros_like(o_ref)

        o_ref[...] += _dot_tn(a_ref[...], g_ref[...])

    if shard_cols:
        per = (n // N_CHIPS) // tn
        out_spec = pl.BlockSpec((None, ta, tn), lambda i, j, k: (j // per, i, j % per))
        out_shape = _sds((N_CHIPS, ka, n // N_CHIPS), F32)
    else:
        out_spec = pl.BlockSpec((ta, tn), lambda i, j, k: (i, j))
        out_shape = _sds((ka, n), F32)
    return _call(
        body, name="dw_" + tag, grid=(ka // ta, n // tn, nk),
        in_specs=[pl.BlockSpec((tk, ta), lambda i, j, k: (k, i)), pl.BlockSpec((tk, tn), lambda i, j, k: (k, j))],
        out_specs=[out_spec], out_shape=[out_shape],
        args=(a, g), sem=("parallel", "parallel", "arbitrary"), exchanges=exchanges)


def _dw_mix(merged, dmix, y_pool, dbp, y_attn, dba, exchanges=()):
    T = merged.shape[0]
    tk = min(2 * TM, T)
    c = D_MODEL // N_CHIPS

    def body(mg_ref, dmix_ref, yp_ref, dbp_ref, ya_ref, dba_ref, out_ref, bp_ref, ba_ref):
        @pl.when(pl.program_id(0) == 0)
        def _():
            for ref in (out_ref, bp_ref, ba_ref):
                ref[...] = jnp.zeros_like(ref)

        out_ref[...] += _dot_tn(mg_ref[...], dmix_ref[...])
        for y_ref, d_ref, o_ref in ((yp_ref, dbp_ref, bp_ref), (ya_ref, dba_ref, ba_ref)):
            res = _dot_tn(y_ref[...], d_ref[...])
            for j in range(N_CHIPS):
                o_ref[j] += res[:, c * j:c * (j + 1)]

    slabs = (N_CHIPS, POOL_WIDTH, c)
    return _call(
        body, name="dw_mix", grid=(T // tk,),
        in_specs=[_rows(tk, D_MODEL), _rows(tk, D_MODEL), _rows(tk, POOL_WIDTH), _rows(tk, D_MODEL),
                  _rows(tk, ATTN_WIDTH), _rows(tk, D_MODEL)],
        out_specs=[_const((D_MODEL, D_MODEL)), _const(slabs), _const(slabs)],
        out_shape=[_sds((D_MODEL, D_MODEL), F32), _sds(slabs, F32), _sds(slabs, F32)],
        args=(merged, dmix, y_pool, dbp, y_attn, dba), sem=("arbitrary",), exchanges=exchanges)


def _merge_bwd(dmix, gate, y_pool, y_attn, w_out, w_bp, w_ba, exchanges=()):
    T = dmix.shape[0]
    tm = min(TM, T)

    def body(dmix_ref, gate_ref, yp_ref, ya_ref, wo_ref, wbp_ref, wba_ref,
             dbp_ref, dba_ref, dgate_ref, dyp_ref, dya_ref):
        dm = _dot_nt(dmix_ref[...], wo_ref[...])
        for j, (y_ref, db_ref, w_ref, dy_ref) in enumerate(
                ((yp_ref, dbp_ref, wbp_ref, dyp_ref), (ya_ref, dba_ref, wba_ref, dya_ref))):
            sl = slice(D_MODEL * j, D_MODEL * (j + 1))
            gt = gate_ref[:, sl].astype(F32)
            db = (dm * gt).astype(BF16)
            db_ref[...] = db
            dgate_ref[:, sl] = (dm * _branch(y_ref[...], w_ref) * gt * (1.0 - gt)).astype(BF16)
            cw = D_MODEL // N_CHIPS
            dy = _dot_nt(db[:, :cw], w_ref[0])
            for c in range(1, N_CHIPS):
                dy = dy + _dot_nt(db[:, cw * c:cw * (c + 1)], w_ref[c])
            dy_ref[...] = dy.astype(dy_ref.dtype)

    return _call(
        body, name="merge_bwd", grid=(T // tm,),
        in_specs=[_rows(tm, D_MODEL), _rows(tm, GATE_WIDTH), _rows(tm, POOL_WIDTH), _rows(tm, ATTN_WIDTH),
                  _const((D_MODEL, D_MODEL)), _const(w_bp.shape), _const(w_ba.shape)],
        out_specs=[_rows(tm, D_MODEL), _rows(tm, D_MODEL), _rows(tm, GATE_WIDTH), _rows(tm, POOL_WIDTH),
                   _rows(tm, ATTN_WIDTH)],
        out_shape=[_sds((T, D_MODEL), BF16), _sds((T, D_MODEL), BF16), _sds((T, GATE_WIDTH), BF16),
                   _sds((T, POOL_WIDTH), F32), _sds((T, ATTN_WIDTH), BF16)],
        args=(dmix, gate, y_pool, y_attn, w_out, w_bp, w_ba), sem=("parallel",), exchanges=exchanges)


def _mixers_bwd(q, k, v, do, sinks, tabs, dyp, diff, w_pool, pool_scale, seq, exchanges=()):
    T = q.shape[0]
    nb = seq // BLOCK
    bl = T // seq
    steps = nb + 1
    tp = T // nb
    nseq = seq // tp
    per = tp // HALO
    last_halo = T // HALO - 1

    def body(sink_ref, q_ref, do_ref, kp_ref, kc_ref, vp_ref, vc_ref, c_ref, a_ref, bt_ref, cp_ref, ap_ref, btp_ref,
             dy_ref, nxt_ref, diff_ref, w_ref, s_ref,
             dq_ref, dk_ref, dv_ref, dsink_ref, du_ref, dw_ref, ds_ref, ck_ref, cv_ref):
        n = pl.program_id(0)

        @pl.when(n == 0)
        def _():
            for ref in (dsink_ref, ck_ref, cv_ref, dw_ref, ds_ref):
                ref[...] = jnp.zeros_like(ref)

        @pl.when(n < nb)
        def _():
            _pool_bwd_tile(n, tp, nseq, dy_ref, nxt_ref, diff_ref, w_ref, s_ref, du_ref, dw_ref, ds_ref)
            valid, lo = _attn_masks(n)
            for b in range(bl):
                kk = jnp.concatenate([kp_ref[b], kc_ref[b]], axis=0)
                vv = jnp.concatenate([vp_ref[b], vc_ref[b]], axis=0)
                dk_acc = jnp.zeros((2 * BLOCK, KV_WIDTH), F32)
                dv_acc = jnp.zeros((2 * BLOCK, KV_WIDTH), F32)
                for h in range(2):
                    qs = _stack_heads(q_ref.at[b], h, lo)
                    dos = _stack_heads(do_ref.at[b], h, lo)
                    pr, ps = _group_probs(qs, kk, valid, _sink_rows(sink_ref, h))
                    dp = _dot_nt(dos, vv)
                    delta = jnp.sum(pr * dp, axis=1, keepdims=True)
                    ds = (pr * (dp - delta)).astype(BF16)
                    dsk = ps * delta
                    for g in range(GROUP):
                        idx = GROUP * h + g
                        dsink_ref[idx:idx + 1, :] += (jnp.zeros((1, LANES), F32)
                                                      - jnp.sum(dsk[BLOCK * g:BLOCK * (g + 1)]))
                    dk_acc = dk_acc + _dot_tn(ds, qs)
                    dv_acc = dv_acc + _dot_tn(pr.astype(BF16), dos)
                    for j, pair in enumerate(_unstack_heads(_dot(ds, kk) * SCALE, h, lo)):
                        sl = slice(LANES * (2 * h + j), LANES * (2 * h + j + 1))
                        dq_ref[b, :, sl] = _rot_bwd(pair, c_ref[...], a_ref[...], bt_ref[...]).astype(BF16)
                fin_k = ck_ref[b] + dk_acc[:BLOCK]
                dk_ref[b] = _rot_bwd(fin_k, cp_ref[...], ap_ref[...], btp_ref[...]).astype(BF16)
                dv_ref[b] = (cv_ref[b] + dv_acc[:BLOCK]).astype(BF16)
                ck_ref[b] = dk_acc[BLOCK:]
                cv_ref[b] = dv_acc[BLOCK:]

        @pl.when(n == nb)
        def _():
            for b in range(bl):
                dk_ref[b] = _rot_bwd(ck_ref[b], cp_ref[...], ap_ref[...], btp_ref[...]).astype(BF16)
                dv_ref[b] = cv_ref[b].astype(BF16)

    cur = lambda n: (0, jnp.minimum(n, nb - 1), 0)
    prv = lambda n: (0, jnp.clip(n - 1, 0, nb - 1), 0)
    tcur = lambda n: (jnp.minimum(n, nb - 1), 0)
    tprv = lambda n: (jnp.clip(n - 1, 0, nb - 1), 0)
    wide = lambda m: pl.BlockSpec((bl, BLOCK, ATTN_WIDTH), m)
    kv = lambda m: pl.BlockSpec((bl, BLOCK, KV_WIDTH), m)
    tab = lambda m: pl.BlockSpec((BLOCK, LANES), m)
    tile = lambda n: (jnp.minimum(n, nb - 1), 0)
    halo = lambda n: (jnp.minimum((jnp.minimum(n, nb - 1) + 1) * per, last_halo), 0)
    rows = pl.BlockSpec((tp, POOL_WIDTH), tile)
    res = _call(
        body, name="mixers_bwd", grid=(steps,),
        in_specs=[pl.BlockSpec(memory_space=pltpu.SMEM), wide(cur), wide(cur), kv(prv), kv(cur), kv(prv), kv(cur),
                  tab(tcur), tab(tcur), tab(tcur), tab(tprv), tab(tprv), tab(tprv),
                  rows, pl.BlockSpec((HALO, POOL_WIDTH), halo), rows, _const((4, POOL_GC, POOL_GC)),
                  _const((1, POOL_WIDTH))],
        out_specs=[wide(cur), kv(prv), kv(prv), _const((8, LANES)), rows, _const((4, POOL_GC, POOL_GC)),
                   _const((1, POOL_WIDTH))],
        out_shape=[_sds((bl, seq, ATTN_WIDTH), BF16), _sds((bl, seq, KV_WIDTH), BF16),
                   _sds((bl, seq, KV_WIDTH), BF16), _sds((8, LANES), F32), _sds((T, POOL_WIDTH), BF16),
                   _sds((4, POOL_GC, POOL_GC), F32), _sds((1, POOL_WIDTH), F32)],
        scratch=[pltpu.VMEM((bl, BLOCK, KV_WIDTH), F32), pltpu.VMEM((bl, BLOCK, KV_WIDTH), F32)],
        args=(sinks, *_by_example(bl, q, do, k, k, v, v), *tabs, *tabs, dyp, dyp, diff, w_pool, pool_scale),
        sem=("arbitrary",), exchanges=exchanges)
    outs, rest = (res if exchanges else (res, None))
    outs = [outs[0].reshape(T, ATTN_WIDTH), outs[1].reshape(T, KV_WIDTH), outs[2].reshape(T, KV_WIDTH), *outs[3:]]
    return (outs, rest) if exchanges else outs


def _pool_bwd_tile(i, tp, nseq, dy_ref, nxt_ref, diff_ref, w_ref, s_ref, du_ref, dw_ref, ds_ref):
    last = (i % nseq) == nseq - 1
    nxt = jnp.where(last, 0.0, nxt_ref[...])
    ext = jnp.concatenate([dy_ref[...], nxt], axis=0) * s_ref[...]
    pos = (i % nseq) * tp + lax.broadcasted_iota(jnp.int32, (tp + HALO, 1), 0)
    for gi, w in enumerate(POOL_WINDOWS):
        sl = slice(POOL_GC * gi, POOL_GC * (gi + 1))
        wg = w_ref[gi].astype(BF16)
        dmx = ext[:, sl].astype(BF16)
        ddiff = _dot_nt(dmx, wg)
        s = ddiff * _inv_count(pos, w)
        sh = 1
        while sh < w:
            s = s + pltpu.roll(s, tp + HALO - sh, 0)
            sh *= 2
        du_ref[:, sl] = (s[:tp] - ddiff[:tp]).astype(BF16)
        dg = diff_ref[:, sl]
        dw_ref[gi] += _dot_tn(dg, dmx[:tp])
        ds_ref[:, sl] += jnp.sum(dy_ref[:, sl] * _dot(dg, wg), axis=0, keepdims=True)


_PARTS = ((0, C_Q), (C_Q, C_K), (C_K, C_V), (C_V, C_G), (C_G, IN_WIDTH))


def _inproj_bwd(parts, x2, dx1, w_in_t, g1, exchanges=()):
    T = x2.shape[0]
    tm = min(TM, T)

    def body(du_ref, dq_ref, dk_ref, dv_ref, dgt_ref, x_ref, dx1_ref, w_ref, g_ref, gx_ref, dg_ref):
        @pl.when(pl.program_id(0) == 0)
        def _():
            dg_ref[...] = jnp.zeros_like(dg_ref)

        dh = jnp.zeros((tm, D_MODEL), F32)
        for (lo, hi), p_ref in zip(_PARTS, (du_ref, dq_ref, dk_ref, dv_ref, dgt_ref)):
            dh = dh + _dot(p_ref[...], w_ref[lo:hi, :])
        dx, dg = _norm_bwd(x_ref[...], g_ref[...], dh)
        gx_ref[...] = dx1_ref[...] + dx
        dg_ref[...] += dg

    return _call(
        body, name="inproj_bwd", grid=(T // tm,),
        in_specs=[_rows(tm, hi - lo) for lo, hi in _PARTS]
        + [_rows(tm, D_MODEL), _rows(tm, D_MODEL), _const((IN_WIDTH, D_MODEL)), _const((1, D_MODEL))],
        out_specs=[_rows(tm, D_MODEL), _const((1, D_MODEL))],
        out_shape=[_sds((T, D_MODEL), F32), _sds((1, D_MODEL), F32)],
        args=(*parts, x2, dx1, w_in_t, g1), sem=("arbitrary",), exchanges=exchanges)


def _dw_in(h, parts, exchanges=()):
    T = h.shape[0]
    tk = min(TM, T)

    def body(h_ref, du_ref, dq_ref, dk_ref, dv_ref, dgt_ref, o_ref, db_ref):
        @pl.when(pl.program_id(0) == 0)
        def _():
            o_ref[...] = jnp.zeros_like(o_ref)
            db_ref[...] = jnp.zeros_like(db_ref)

        hh = h_ref[...]
        ones = jnp.ones((8, tk), BF16)
        for (lo, hi), p_ref in zip(_PARTS, (du_ref, dq_ref, dk_ref, dv_ref, dgt_ref)):
            part = p_ref[...]
            o_ref[lo:hi, :] += _dot_tn(part, hh)
            db_ref[:, lo:hi] += _dot(ones, part)[:1]

    return _call(
        body, name="dw_in", grid=(T // tk,),
        in_specs=[_rows(tk, D_MODEL)] + [_rows(tk, hi - lo) for lo, hi in _PARTS],
        out_specs=[_const((IN_WIDTH, D_MODEL)), _const((1, IN_WIDTH))],
        out_shape=[_sds((IN_WIDTH, D_MODEL), F32), _sds((1, IN_WIDTH), F32)],
        args=(h, *parts), sem=("arbitrary",), exchanges=exchanges)


def _row_tile(rows, cap=256, mult=16):
    best = None
    for t in range(mult, min(rows, cap) + 1, mult):
        if rows % t == 0:
            best = t
    if best is None:
        raise ValueError("no row tile for %d rows" % rows)
    return best


def _pair_sum(ids, full, got):
    _, r, c = full.shape
    hr = r // 2
    tr = _row_tile(hr, cap=512)
    nblk = hr // tr

    def body(ids_ref, a_ref, b_ref, own_ref, sb_ref):
        s = a_ref[...] + b_ref[...]
        sb_ref[...] = s.astype(BF16)

        @pl.when(pl.program_id(1) == ids_ref[0])
        def _():
            own_ref[...] = s

    slab = pl.BlockSpec((None, tr, c), lambda i, j, ids_ref: (j, i, 0))
    return pl.pallas_call(
        body, name="pair_sum_%dx%d" % (r, c),
        grid_spec=pltpu.PrefetchScalarGridSpec(
            num_scalar_prefetch=1, grid=(nblk, N_CHIPS),
            in_specs=[pl.BlockSpec((None, tr, c), lambda i, j, ids_ref: (j, ids_ref[1] * nblk + i, 0)), slab],
            out_specs=[pl.BlockSpec((tr, c), lambda i, j, ids_ref: (i, 0)), slab]),
        out_shape=[_sds((hr, c), F32), _sds((N_CHIPS, hr, c), BF16)],
        compiler_params=_cp("parallel", "arbitrary"),
    )(ids, full, got)


def _pair_sum_small(ids, fulls, gots):
    n = len(fulls)
    dims = [(f.shape[1] // 2, f.shape[2]) for f in fulls]

    def body(ids_ref, *refs):
        ins, outs = refs[:2 * n], refs[2 * n:]
        for k in range(n):
            s = ins[2 * k][...] + ins[2 * k + 1][...]
            outs[2 * k + 1][...] = s.astype(BF16)

            @pl.when(pl.program_id(0) == ids_ref[0])
            def _(k=k, s=s):
                outs[2 * k][...] = s

    in_specs, out_specs, out_shape = [], [], []
    for hr, c in dims:
        slab = pl.BlockSpec((None, hr, c), lambda j, ids_ref: (j, 0, 0))
        in_specs += [pl.BlockSpec((None, hr, c), lambda j, ids_ref: (j, ids_ref[1], 0)), slab]
        out_specs += [pl.BlockSpec((hr, c), lambda j, ids_ref: (0, 0)), slab]
        out_shape += [_sds((hr, c), F32), _sds((N_CHIPS, hr, c), BF16)]
    res = pl.pallas_call(
        body, name="pair_sum_small",
        grid_spec=pltpu.PrefetchScalarGridSpec(num_scalar_prefetch=1, grid=(N_CHIPS,), in_specs=in_specs,
                                               out_specs=out_specs),
        out_shape=out_shape, compiler_params=_cp("arbitrary"),
    )(ids, *[a for pair in zip(fulls, gots) for a in pair])
    return [(res[2 * k], res[2 * k + 1]) for k in range(n)]


def _chip_sum_small(ids, owns, gots):
    n = len(owns)

    def body(ids_ref, *refs):
        ins, outs = refs[:2 * n], refs[2 * n:]
        for k in range(n):
            a, b = ins[2 * k], ins[2 * k + 1]
            outs[k][...] = ((a[...] + b[0].astype(F32)) + b[1].astype(F32)) + b[2].astype(F32)

    in_specs, out_specs, out_shape = [], [], []
    for own in owns:
        hr, c = own.shape
        in_specs += [pl.BlockSpec((hr, c), lambda i, ids_ref: (0, 0)),
                     pl.BlockSpec((3, hr, c), lambda i, ids_ref: (0, 0, 0))]
        out_specs.append(pl.BlockSpec((hr, c), lambda i, ids_ref: (ids_ref[1], 0)))
        out_shape.append(_sds((2 * hr, c), F32))
    return pl.pallas_call(
        body, name="chip_sum_small",
        grid_spec=pltpu.PrefetchScalarGridSpec(num_scalar_prefetch=1, grid=(1,), in_specs=in_specs,
                                               out_specs=out_specs),
        out_shape=out_shape, compiler_params=_cp("arbitrary"),
    )(ids, *[a for pair in zip(owns, gots) for a in pair])


def _chip_sum(ids, own, got):
    hr, c = own.shape
    tr = _row_tile(hr)
    nblk = hr // tr

    def body(ids_ref, a_ref, b_ref, o_ref):
        o_ref[...] = ((a_ref[...] + b_ref[0].astype(F32)) + b_ref[1].astype(F32)) + b_ref[2].astype(F32)

    return pl.pallas_call(
        body, name="chip_sum_%dx%d" % (hr, c),
        grid_spec=pltpu.PrefetchScalarGridSpec(
            num_scalar_prefetch=1, grid=(nblk,),
            in_specs=[pl.BlockSpec((tr, c), lambda i, ids_ref: (i, 0)),
                      pl.BlockSpec((3, tr, c), lambda i, ids_ref: (0, i, 0))],
            out_specs=pl.BlockSpec((tr, c), lambda i, ids_ref: (ids_ref[1] * nblk + i, 0))),
        out_shape=_sds((2 * hr, c), F32),
        compiler_params=_cp("parallel"),
    )(ids, own, got)


def _adamw_math(w, g, m, v):
    nm = ADAM_B1 * m + (1.0 - ADAM_B1) * g
    nv = ADAM_B2 * v + (1.0 - ADAM_B2) * (g * g)
    m_hat = nm / (1.0 - ADAM_B1 ** ADAM_STEP)
    v_hat = nv / (1.0 - ADAM_B2 ** ADAM_STEP)
    return -ADAM_LR * (m_hat / (jnp.sqrt(v_hat) + ADAM_EPS) + ADAM_WD * w), nm, nv


def _adamw(w, g, m, v):
    r, c = w.shape
    tr = _row_tile(r, cap=512, mult=8)

    def body(w_ref, g_ref, m_ref, v_ref, d_ref, nm_ref, nv_ref):
        d_ref[...], nm_ref[...], nv_ref[...] = _adamw_math(w_ref[...], g_ref[...], m_ref[...], v_ref[...])

    spec = _rows(tr, c)
    return pl.pallas_call(
        body, name="adamw_%dx%d" % (r, c), grid=(r // tr,),
        in_specs=[spec] * 4, out_specs=[spec] * 3, out_shape=[_sds((r, c), F32)] * 3,
        compiler_params=_cp("parallel"),
    )(w, g, m, v)


SC_TILES = 32
SC_LANES = 16
SC_ROWS = 8


def _adamw_sparse(w, g, m, v):
    r, c = w.shape
    rows = r // SC_TILES
    step = min(rows, SC_ROWS)

    def body(w_hbm, g_hbm, m_hbm, v_hbm, d_hbm, nm_hbm, nv_hbm, wb, gb, mb, vb):
        tile = lax.axis_index("sc_subcore") * 2 + lax.axis_index("sc_core")

        @pl.loop(0, rows, step=step)
        def _(r0):
            mine = pl.ds(tile * rows + r0, step)
            for src, dst in ((w_hbm, wb), (g_hbm, gb), (m_hbm, mb), (v_hbm, vb)):
                pltpu.sync_copy(src.at[mine], dst)

            @pl.loop(0, step)
            def _(row):
                @pl.loop(0, c, step=SC_LANES)
                def _(i):
                    at = (row, pl.ds(i, SC_LANES))
                    wb[at], mb[at], vb[at] = _adamw_math(wb[at], gb[at], mb[at], vb[at])

            for src, dst in ((wb, d_hbm), (mb, nm_hbm), (vb, nv_hbm)):
                pltpu.sync_copy(src, dst.at[mine])

    return pl.kernel(
        body, name="adamw_sparse_%dx%d" % (r, c), out_type=[_sds((r, c), F32)] * 3,
        mesh=plsc.VectorSubcoreMesh(core_axis_name="sc_core", subcore_axis_name="sc_subcore"),
        scratch_types=[pltpu.VMEM((step, c), F32)] * 4,
    )(w, g, m, v)


_SMALL_NAMES = ("w_pool", "b_in", "g_mix_pre", "g_mix_post", "g_mlp_pre", "g_mlp_post", "pool_scale", "attn_sinks")
B_ROWS = -(-IN_WIDTH // D_MODEL)


def _row_block(rows):
    rows = [jnp.pad(r.astype(F32), ((0, 0), (0, D_MODEL - r.shape[1]))) for r in rows]
    return jnp.pad(jnp.concatenate(rows, axis=0), ((0, 8 - len(rows)), (0, 0)))


def _early_block(dg2, dg3, dg4, dps, dsink, loss):
    tail = jnp.concatenate([jnp.pad(dsink.reshape(1, -1), ((0, 0), (0, LANES - dsink.size))),
                            jnp.pad(loss.reshape(1, 1), ((0, 0), (0, LANES - 1)))], axis=1)
    return _row_block([dg2, dg3, dg4, dps, tail])


def _late_block(db_in, dg1):
    b = jnp.pad(db_in, ((0, 0), (0, B_ROWS * D_MODEL - IN_WIDTH))).reshape(B_ROWS, D_MODEL)
    return _row_block([b[r:r + 1] for r in range(B_ROWS)] + [dg1])


def _small_update(gearly, gmat, glate, w, m, v):
    names = _SMALL_NAMES
    n = len(names)

    def total(ref, rows):
        acc = ref[0:rows, :]
        for d in range(1, N_DEV):
            acc = acc + ref[d * rows:(d + 1) * rows, :]
        return acc

    def body(*refs):
        early_ref, gmat_ref, late_ref = refs[:3]
        w_refs, m_refs, v_refs = refs[3:3 + n], refs[3 + n:3 + 2 * n], refs[3 + 2 * n:3 + 3 * n]
        outs = refs[3 + 3 * n:]
        loss_ref, g_refs, d_refs = outs[0], outs[1:1 + n], outs[1 + n:1 + 2 * n]
        nm_refs, nv_refs = outs[1 + 2 * n:1 + 3 * n], outs[1 + 3 * n:1 + 4 * n]
        early, late = total(early_ref, 8), total(late_ref, 8)
        loss_ref[...] = jnp.sum(early[4:5, LANES:2 * LANES], axis=1, keepdims=True)
        bias = jnp.concatenate([late[r:r + 1, :] for r in range(B_ROWS - 1)]
                               + [late[B_ROWS - 1:B_ROWS, :IN_WIDTH - (B_ROWS - 1) * D_MODEL]], axis=1)
        grad = dict(b_in=bias, g_mix_pre=late[B_ROWS:B_ROWS + 1, :], g_mix_post=early[0:1, :],
                    g_mlp_pre=early[1:2, :], g_mlp_post=early[2:3, :], pool_scale=early[3:4, :POOL_WIDTH],
                    attn_sinks=early[4:5, :N_Q_HEADS])
        for i, name in enumerate(names):
            g = total(gmat_ref, 4 * POOL_GC) if name == "w_pool" else grad[name]
            g_refs[i][...] = g
            d_refs[i][...], nm_refs[i][...], nv_refs[i][...] = _adamw_math(
                w_refs[i][...], g, m_refs[i][...], v_refs[i][...])

    shapes = [_sds(w[k].shape, F32) for k in names]
    res = pl.pallas_call(
        body, name="small_update", out_shape=[_sds((1, 1), F32)] + shapes * 4,
        compiler_params=pltpu.CompilerParams(vmem_limit_bytes=VMEM_MB * 1024 * 1024),
    )(gearly, gmat, glate, *[w[k] for k in names], *[m[k] for k in names], *[v[k] for k in names])
    loss = res[0]
    per = {k: tuple(res[1 + j * n + i] for j in range(4)) for i, k in enumerate(names)}
    return loss, per


_BIG = ("w_in", "w_branch_pool", "w_branch_attn", "w_out", "w_up", "w_down")
_ORDER = ("g_mix_pre", "w_in", "b_in", "w_pool", "pool_scale", "attn_sinks", "w_branch_pool", "w_branch_attn",
          "w_out", "g_mix_post", "g_mlp_pre", "w_up", "w_down", "g_mlp_post")


def _stack_rows(slab):
    return slab.reshape(-1, slab.shape[2])


def _step(x2, tgt, seq, shards, small, ids):
    tabs = _rope_tables(seq)
    g1, g2, g3, g4 = (small[n] for n in ("g_mix_pre", "g_mix_post", "g_mlp_pre", "g_mlp_post"))
    sinks = small["attn_sinks"].reshape(N_Q_HEADS)
    w_pool = small["w_pool"].reshape(4, POOL_GC, POOL_GC)
    pool_scale = small["pool_scale"]

    def whole(shard, slabs):
        return lax.dynamic_update_slice(slabs, shard[None], (ids[0], 0, 0))

    (up_a, up_b, down_a, down_b, *mix_shards), [[in_slab]] = _cast_shards(
        *(shards[n] for n in ("w_up", "w_down", "w_branch_pool", "w_branch_attn", "w_out")),
        exchanges=[_ex_gather([shards["w_in"]])])
    w_in = _stack_rows(whole(shards["w_in"], in_slab))
    (h, u, q, k, v, gate), [mix_slabs] = _inproj(
        x2, g1, w_in, small["b_in"], tabs, seq, exchanges=[_ex_gather(mix_shards)])
    w_bp, w_ba, out_slab = (whole(s, g) for s, g in zip(mix_shards, mix_slabs))
    w_out = _stack_rows(out_slab)
    (y_attn, diff, y_pool), [[got_a, got_b]] = _mixers_fwd(
        q, k, v, sinks, u, w_pool, pool_scale, seq, exchanges=[_ex_gather([up_a, up_b])])
    (merged, mix, x1, h2), [[got_c, got_d]] = _merge_out(
        y_pool, y_attn, gate, x2, w_bp, w_ba, w_out, g2, g3, exchanges=[_ex_gather([down_a, down_b])])
    w_up = (whole(up_a, got_a), whole(up_b, got_b))
    w_down = (whole(down_a, got_c), whole(down_b, got_d))
    act, dff, dup, dx1, dmix, loss_acc, dg4, dg3, dg2 = _mlp_core(h2, x1, mix, tgt, w_up, w_down, g4, g3, g2)

    dw_down = _dw("down", act, dff, 2048, 1024)[0].reshape(N_CHIPS, D_FF // N_CHIPS, D_MODEL)
    (dbp, dba, dgate, dyp, dya), [[got]] = _merge_bwd(
        dmix, gate, y_pool, y_attn, w_out, w_bp, w_ba, exchanges=[_ex_pair([dw_down])])
    ps_down = _pair_sum(ids, dw_down, got)
    (dw_up,), [[got]] = _dw("up", h2, dup, 1024, 1024, shard_cols=True, exchanges=[_ex_chip([ps_down[1]])])
    half_down = _chip_sum(ids, ps_down[0], got)
    (dw_out, dw_bp, dw_ba), [[got]] = _dw_mix(merged, dmix, y_pool, dbp, y_attn, dba, exchanges=[_ex_pair([dw_up])])
    ps_up = _pair_sum(ids, dw_up, got)
    dw_mix = [dw_out.reshape(N_CHIPS, D_MODEL // N_CHIPS, D_MODEL), dw_bp, dw_ba]
    (dq, dk, dv, dsink, du, dw_pool, dps), [[got], gots, [g_down]] = _mixers_bwd(
        q, k, v, dya, sinks, tabs, dyp, diff, w_pool, pool_scale, seq,
        exchanges=[_ex_chip([ps_up[1]]), _ex_pair(dw_mix), _ex_swap([half_down])])
    half_up = _chip_sum(ids, ps_up[0], got)
    ps_mix = _pair_sum_small(ids, dw_mix, gots)
    parts = (du, dq, dk, dv, dgate)
    early = _early_block(dg2, dg3, dg4, dps, dsink[:, 0], loss_acc[0, 0])
    mat = dw_pool.reshape(4 * POOL_GC, POOL_GC)
    (dw_in_t, db_in), [gots, [gearly, gmat], [g_up]] = _dw_in(
        h, parts, exchanges=[_ex_chip([p[1] for p in ps_mix]), _ex_allgather([early, mat]), _ex_swap([half_up])])
    half_mix = _chip_sum_small(ids, [p[0] for p in ps_mix], gots)
    dw_in = dw_in_t.reshape(N_CHIPS, IN_WIDTH // N_CHIPS, D_MODEL)
    g_mix, [got] = _alone("swap_mix_pair_in", _ex_swap(half_mix), _ex_pair([dw_in]))
    ps_in = _pair_sum(ids, dw_in, got)
    (gx, dg1), [[got]] = _inproj_bwd(parts, x2, dx1, w_in, g1, exchanges=[_ex_chip([ps_in[1]])])
    [g_in], [glate] = _alone("swap_in_allgather", _ex_swap([_chip_sum(ids, ps_in[0], got)]),
                             _ex_allgather([_late_block(db_in, dg1)]))

    grads = dict(w_in=g_in, w_branch_pool=g_mix[1], w_branch_attn=g_mix[2], w_out=g_mix[0], w_up=g_up, w_down=g_down)
    return (gearly, gmat, glate), gx, grads


def kernel(x, g_mix_pre, w_in, b_in, w_pool, pool_scale, attn_sinks, w_branch_pool, w_branch_attn, w_out, g_mix_post, g_mlp_pre, w_up, w_down, g_mlp_post, loss_target, m_g_mix_pre, m_w_in, m_b_in, m_w_pool, m_pool_scale, m_attn_sinks, m_w_branch_pool, m_w_branch_attn, m_w_out, m_g_mix_post, m_g_mlp_pre, m_w_up, m_w_down, m_g_mlp_post, v_g_mix_pre, v_w_in, v_b_in, v_w_pool, v_pool_scale, v_attn_sinks, v_w_branch_pool, v_w_branch_attn, v_w_out, v_g_mix_post, v_g_mlp_pre, v_w_up, v_w_down, v_g_mlp_post):
    weights = dict(g_mix_pre=g_mix_pre, w_in=w_in, b_in=b_in, w_pool=w_pool, pool_scale=pool_scale,
                   attn_sinks=attn_sinks, w_branch_pool=w_branch_pool, w_branch_attn=w_branch_attn, w_out=w_out,
                   g_mix_post=g_mix_post, g_mlp_pre=g_mlp_pre, w_up=w_up, w_down=w_down, g_mlp_post=g_mlp_post)
    mom1 = dict(g_mix_pre=m_g_mix_pre, w_in=m_w_in, b_in=m_b_in, w_pool=m_w_pool, pool_scale=m_pool_scale,
                attn_sinks=m_attn_sinks, w_branch_pool=m_w_branch_pool, w_branch_attn=m_w_branch_attn,
                w_out=m_w_out, g_mix_post=m_g_mix_post, g_mlp_pre=m_g_mlp_pre, w_up=m_w_up, w_down=m_w_down,
                g_mlp_post=m_g_mlp_post)
    mom2 = dict(g_mix_pre=v_g_mix_pre, w_in=v_w_in, b_in=v_b_in, w_pool=v_w_pool, pool_scale=v_pool_scale,
                attn_sinks=v_attn_sinks, w_branch_pool=v_w_branch_pool, w_branch_attn=v_w_branch_attn,
                w_out=v_w_out, g_mix_post=v_g_mix_post, g_mlp_pre=v_g_mlp_pre, w_up=v_w_up, w_down=v_w_down,
                g_mlp_post=v_g_mlp_post)
    b_loc, seq, _ = x.shape
    x2 = x.reshape(b_loc * seq, D_MODEL)
    tgt = loss_target.reshape(b_loc * seq, D_MODEL)
    ids = jnp.stack([2 * lax.axis_index("x") + lax.axis_index("y"), lax.axis_index("c")]).astype(jnp.int32)

    def flat(n, a):
        return a[0].T if n == "w_in" else a[0]

    def unflat(n, a):
        return (a.T if n == "w_in" else a)[None]

    shards = {n: flat(n, weights[n]).astype(BF16) if n == "w_in" else flat(n, weights[n]) for n in _BIG}
    small = {n: weights[n] for n in _ORDER if n not in _BIG}
    (gearly, gmat, glate), gx, grads = _step(x2, tgt, seq, shards, small, ids)

    def two_d(src):
        return {n: src[n].reshape(4 * POOL_GC, POOL_GC) if n == "w_pool" else src[n] for n in _SMALL_NAMES}

    loss, per = _small_update(gearly, gmat, glate, two_d(weights), two_d(mom1), two_d(mom2))
    delta, new_m, new_v = {}, {}, {}
    for n in _SMALL_NAMES:
        grads[n], delta[n], new_m[n], new_v[n] = (a.reshape(weights[n].shape) for a in per[n])
    for n in _BIG:
        update = _adamw if n == "w_in" else _adamw_sparse
        d, nm, nv = update(flat(n, weights[n]), grads[n], flat(n, mom1[n]), flat(n, mom2[n]))
        grads[n] = unflat(n, grads[n])
        delta[n], new_m[n], new_v[n] = unflat(n, d), unflat(n, nm), unflat(n, nv)

    return (loss[0, 0], gx.reshape(x.shape), *[grads[n] for n in _ORDER], *[delta[n] for n in _ORDER],
            *[new_m[n] for n in _ORDER], *[new_v[n] for n in _ORDER])
```

```python
import jax
import jax.numpy as jnp
from jax import lax
from jax.experimental import pallas as pl
from jax.experimental.pallas import tpu as pltpu
from jax.experimental.pallas import tpu_sc as plsc

F32 = jnp.float32
BF16 = jnp.bfloat16

D_MODEL = 1024
POOL_WINDOWS = (2, 4, 8, 16)
POOL_WIDTH = 512
POOL_GC = 128
HALO = 16
HEAD_DIM = 64
N_Q_HEADS = 8
ATTN_WIDTH = 512
KV_WIDTH = 128
BLOCK = 128
NEG_INF = -1e30
ROPE_THETA = 500000.0
ROT_DIM = 16
GATE_WIDTH = 2048
IN_WIDTH = 3328
D_FF = 4096
EPS = 1e-6
SCALE = HEAD_DIM ** -0.5
C_Q, C_K, C_V, C_G = 512, 1024, 1152, 1280

ADAM_LR, ADAM_B1, ADAM_B2, ADAM_EPS, ADAM_WD, ADAM_STEP = 0.001, 0.9, 0.999, 1e-08, 0.01, 10

N_CHIPS = 4
N_DEV = 8
LANES = 128
TM = 512
VMEM_MB = 56

MESH = pl.DeviceIdType.MESH
ANY = pl.BlockSpec(memory_space=pl.ANY)


def _cp(*sem, vmem=VMEM_MB):
    return pltpu.CompilerParams(dimension_semantics=sem, vmem_limit_bytes=vmem * 1024 * 1024)


def _rows(tile, cols):
    return pl.BlockSpec((tile, cols), lambda i: (i, 0))


def _const(shape):
    nd = len(shape)
    return pl.BlockSpec(shape, lambda i: (0,) * nd)


def _sds(shape, dtype):
    return jax.ShapeDtypeStruct(shape, dtype)


def _dot(a, b):
    return jnp.dot(a, b, preferred_element_type=F32)


def _dot_nt(a, b):
    return lax.dot_general(a, b, (((1,), (1,)), ((), ())), preferred_element_type=F32)


def _dot_tn(a, b):
    return lax.dot_general(a, b, (((0,), (0,)), ((), ())), preferred_element_type=F32)


def _rms(x):
    return lax.rsqrt(jnp.mean(x * x, axis=-1, keepdims=True) + EPS)


def _norm_bwd(x, g, dout):
    r = _rms(x)
    n = x * r
    dn = dout * g
    dx = r * (dn - n * jnp.mean(dn * n, axis=-1, keepdims=True))
    return dx, jnp.sum(dout * n, axis=0, keepdims=True)


def _rot_fwd(t, c, a, bt):
    return t * c + pltpu.roll(t, LANES - 8, 1) * a + pltpu.roll(t, 8, 1) * bt


def _rot_bwd(d, c, a, bt):
    return d * c + pltpu.roll(d * a, 8, 1) + pltpu.roll(d * bt, LANES - 8, 1)


def _rope_tables(seq):
    pos = jnp.arange(seq, dtype=F32)
    inv_freq = ROPE_THETA ** (-jnp.arange(0, ROT_DIM, 2, dtype=F32) / ROT_DIM)
    ang = pos[:, None] * inv_freq[None, :]
    cos, sin = jnp.cos(ang), jnp.sin(ang)
    ones = jnp.ones((seq, HEAD_DIM - ROT_DIM), F32)
    zeros8 = jnp.zeros((seq, 8), F32)
    zrest = jnp.zeros((seq, HEAD_DIM - ROT_DIM), F32)
    c = jnp.concatenate([cos, cos, ones], axis=1)
    a = jnp.concatenate([-sin, zeros8, zrest], axis=1)
    bt = jnp.concatenate([zeros8, sin, zrest], axis=1)
    return tuple(jnp.tile(t, (1, 2)) for t in (c, a, bt))


class _Exchange:
    def __init__(self, inputs, out_shapes, sems, start, finish, aliases=None, middle=None):
        self.inputs, self.out_shapes, self.sems = list(inputs), list(out_shapes), list(sems)
        self.start, self.finish, self.aliases = start, finish, dict(aliases or {})
        self.middle = middle


def _call(body, *, name, grid, in_specs, out_specs, out_shape, args, scratch=(), sem=(), exchanges=()):
    in_specs, out_specs, out_shape, scratch = list(in_specs), list(out_specs), list(out_shape), list(scratch)
    if not exchanges:
        return pl.pallas_call(body, name=name, grid=grid, in_specs=in_specs, out_specs=out_specs,
                              out_shape=out_shape, scratch_shapes=scratch, compiler_params=_cp(*sem))(*args)
    n_in, n_out, n_scr = len(in_specs), len(out_specs), len(scratch)
    x_in = [a for ex in exchanges for a in ex.inputs]
    x_out = [s for ex in exchanges for s in ex.out_shapes]
    x_sem = [s for ex in exchanges for s in ex.sems]
    aliases, i_off, o_off = {}, n_in, n_out
    for ex in exchanges:
        for i, o in ex.aliases.items():
            aliases[i_off + i] = o_off + o
        i_off += len(ex.inputs)
        o_off += len(ex.out_shapes)

    def split(flat):
        out, pos = [], 0
        for ex, n in zip(exchanges, flat[1]):
            out.append(flat[0][pos:pos + n])
            pos += n
        return out

    def carrier(*refs):
        pos = 0
        groups = []
        for n in (n_in, len(x_in), n_out, len(x_out), n_scr, len(x_sem)):
            groups.append(refs[pos:pos + n])
            pos += n
        ins, xin, outs, xout, scr, xsem = groups
        xin = split((xin, [len(ex.inputs) for ex in exchanges]))
        xout = split((xout, [len(ex.out_shapes) for ex in exchanges]))
        xsem = split((xsem, [len(ex.sems) for ex in exchanges]))
        first = pl.program_id(0) == 0
        last = pl.program_id(0) == grid[0] - 1
        for d in range(1, len(grid)):
            first = jnp.logical_and(first, pl.program_id(d) == 0)
            last = jnp.logical_and(last, pl.program_id(d) == grid[d] - 1)

        @pl.when(first)
        def _():
            for ex, i, o, s in zip(exchanges, xin, xout, xsem):
                ex.start(i, o, s)

        if any(ex.middle for ex in exchanges):
            half = pl.program_id(0) == 5 * grid[0] // 8
            for d in range(1, len(grid)):
                half = jnp.logical_and(half, pl.program_id(d) == 0)

            @pl.when(half)
            def _():
                for ex, i, o, s in zip(exchanges, xin, xout, xsem):
                    if ex.middle:
                        ex.middle(i, o, s)

        body(*ins, *outs, *scr)

        @pl.when(last)
        def _():
            for ex, i, o, s in zip(exchanges, xin, xout, xsem):
                ex.finish(i, o, s)

    res = pl.pallas_call(
        carrier, name=name, grid=grid, in_specs=in_specs + [ANY] * len(x_in),
        out_specs=out_specs + [ANY] * len(x_out), out_shape=out_shape + x_out,
        scratch_shapes=scratch + x_sem, input_output_aliases=aliases,
        compiler_params=_cp(*(["arbitrary"] * len(grid))),
    )(*args, *x_in)
    return res[:n_out], split((res[n_out:], [len(ex.out_shapes) for ex in exchanges]))


def _alone(name, *exchanges):
    n_in = [len(ex.inputs) for ex in exchanges]
    n_out = [len(ex.out_shapes) for ex in exchanges]
    n_sem = [len(ex.sems) for ex in exchanges]
    aliases, i_off, o_off = {}, 0, 0
    for ex in exchanges:
        for i, o in ex.aliases.items():
            aliases[i_off + i] = o_off + o
        i_off += len(ex.inputs)
        o_off += len(ex.out_shapes)

    def split(flat, counts):
        out, pos = [], 0
        for n in counts:
            out.append(flat[pos:pos + n])
            pos += n
        return out

    def body(*refs):
        ins, outs, sems = split(refs, [sum(n_in), sum(n_out), sum(n_sem)])
        groups = list(zip(exchanges, split(ins, n_in), split(outs, n_out), split(sems, n_sem)))
        for ex, i, o, s in groups:
            ex.start(i, o, s)
        for ex, i, o, s in groups:
            if ex.middle:
                ex.middle(i, o, s)
        for ex, i, o, s in groups:
            ex.finish(i, o, s)

    res = pl.pallas_call(
        body, name=name, in_specs=[ANY] * sum(n_in), out_specs=[ANY] * sum(n_out),
        out_shape=[s for ex in exchanges for s in ex.out_shapes],
        scratch_shapes=[s for ex in exchanges for s in ex.sems], input_output_aliases=aliases,
    )(*[a for ex in exchanges for a in ex.inputs])
    return split(res, n_out)


def _place():
    x, y, c = lax.axis_index("x"), lax.axis_index("y"), lax.axis_index("c")
    chips = [(1 - x, y), (x, 1 - y), (1 - x, 1 - y)]
    return x, y, c, chips


def _remote(src, dst, send, recv, to):
    return pltpu.make_async_remote_copy(src_ref=src, dst_ref=dst, send_sem=send, recv_sem=recv,
                                        device_id=to, device_id_type=MESH)


def _ex_gather(shards):
    nw = len(shards)
    hrs = [s.shape[0] // 2 for s in shards]

    def copies(ins, outs, sems):
        s0, r0, s1, r1, s2, r2, fs, fr = sems
        x, y, c, _ = _place()
        me, xn, yn, dg = (x, y), (1 - x, y), (x, 1 - y), (1 - x, 1 - y)
        nbr = (xn, yn)
        sibling = (x, y, 1 - c)

        def piece(w, chip, core, part=None):
            hr = hrs[w]
            rows = pl.ds(core * hr, hr) if part is None else pl.ds(core * hr + part * (hr // 2), hr // 2)
            return outs[w].at[2 * chip[0] + chip[1], rows]

        def first(w, k, lead):
            part = k if lead else 1 - k
            send, recv = (s0, r0) if lead else (s1, r1)
            rows = pl.ds(c * hrs[w] + part * (hrs[w] // 2), hrs[w] // 2)
            return _remote(ins[w].at[rows], piece(w, me, c, part), send.at[w, k], recv.at[w, k], (*nbr[k], c))

        def landed(w, k, lead):
            part = k if lead else 1 - k
            send, recv = (s0, r0) if lead else (s1, r1)
            return _remote(piece(w, nbr[k], c, part), piece(w, nbr[k], c, part), send.at[w, k], recv.at[w, k],
                           (*nbr[k], c))

        def onward(w, k):
            return _remote(piece(w, nbr[k], c, k), piece(w, nbr[k], c, k), s2.at[w, k], r2.at[w, k],
                           (*nbr[1 - k], c))

        def arrived(w, k):
            return _remote(piece(w, dg, c, k), piece(w, dg, c, k), s2.at[w, k], r2.at[w, k], (*nbr[1 - k], c))

        def passed(w, j):
            chip = (xn, yn, dg)[j]
            return _remote(piece(w, chip, c), piece(w, chip, c), fs.at[w, j], fr.at[w, j], sibling)

        def handed(w, j):
            chip = (xn, yn, dg)[j]
            return _remote(piece(w, chip, 1 - c), piece(w, chip, 1 - c), fs.at[w, j], fr.at[w, j], sibling)

        return first, landed, onward, arrived, passed, handed

    def start(ins, outs, sems):
        first = copies(ins, outs, sems)[0]
        for lead in (True, False):
            for w in range(nw):
                for k in range(2):
                    first(w, k, lead).start()

    def middle(ins, outs, sems):
        _, landed, onward, _, passed, _ = copies(ins, outs, sems)
        for w in range(nw):
            for k in range(2):
                landed(w, k, True).wait_recv()
                onward(w, k).start()
        for w in range(nw):
            for k in range(2):
                landed(w, k, False).wait_recv()
                passed(w, k).start()

    def finish(ins, outs, sems):
        first, _, onward, arrived, passed, handed = copies(ins, outs, sems)
        for w in range(nw):
            for k in range(2):
                arrived(w, k).wait_recv()
            passed(w, 2).start()
        for w in range(nw):
            for j in range(3):
                handed(w, j).wait_recv()
        for w in range(nw):
            for k in range(2):
                first(w, k, True).wait_send()
                first(w, k, False).wait_send()
                onward(w, k).wait_send()
            for j in range(3):
                passed(w, j).wait_send()

    return _Exchange(shards, [_sds((N_CHIPS,) + s.shape, s.dtype) for s in shards],
                     [pltpu.SemaphoreType.DMA((nw, 2))] * 6 + [pltpu.SemaphoreType.DMA((nw, 3))] * 2,
                     start, finish, middle=middle)


def _ex_pair(grads):
    nw = len(grads)

    def copies(ins, outs, sems):
        x, y, c, _ = _place()
        out = []
        for w in range(nw):
            hr = grads[w].shape[1] // 2
            out.append(_remote(ins[w].at[:, pl.ds((1 - c) * hr, hr)], outs[w], sems[0].at[w], sems[1].at[w],
                               (x, y, 1 - c)))
        return out

    def start(ins, outs, sems):
        for cp in copies(ins, outs, sems):
            cp.start()

    def finish(ins, outs, sems):
        for cp in copies(ins, outs, sems):
            cp.wait()

    return _Exchange(grads, [_sds((N_CHIPS, g.shape[1] // 2, g.shape[2]), F32) for g in grads],
                     [pltpu.SemaphoreType.DMA((nw,))] * 2, start, finish)


def _ex_chip(pieces):
    nw = len(pieces)

    def copies(ins, outs, sems):
        x, y, c, chips = _place()
        return [_remote(ins[w].at[2 * cx + cy], outs[w].at[k], sems[0].at[w, k], sems[1].at[w, k], (cx, cy, c))
                for w in range(nw) for k, (cx, cy) in enumerate(chips)]

    def start(ins, outs, sems):
        for cp in copies(ins, outs, sems):
            cp.start()

    def finish(ins, outs, sems):
        for cp in copies(ins, outs, sems):
            cp.wait()

    return _Exchange(pieces, [_sds((3,) + p.shape[1:], BF16) for p in pieces],
                     [pltpu.SemaphoreType.DMA((nw, 3))] * 2, start, finish)


def _ex_swap(fulls):
    nw = len(fulls)

    def start(ins, outs, sems):
        x, y, c, _ = _place()
        for w in range(nw):
            hr = fulls[w].shape[0] // 2
            mine = pl.ds(c * hr, hr)
            _remote(ins[w].at[mine], outs[w].at[mine], sems[0].at[w], sems[1].at[w], (x, y, 1 - c)).start()

    def finish(ins, outs, sems):
        x, y, c, _ = _place()
        for w in range(nw):
            hr = fulls[w].shape[0] // 2
            mine, theirs = pl.ds(c * hr, hr), pl.ds((1 - c) * hr, hr)
            _remote(ins[w].at[mine], outs[w].at[mine], sems[0].at[w], sems[1].at[w], (x, y, 1 - c)).wait_send()
            _remote(ins[w].at[theirs], outs[w].at[theirs], sems[0].at[w], sems[1].at[w], (x, y, 1 - c)).wait_recv()

    return _Exchange(fulls, [_sds(f.shape, F32) for f in fulls], [pltpu.SemaphoreType.DMA((nw,))] * 2,
                     start, finish, aliases={w: w for w in range(nw)})


def _ex_allgather(blocks):
    nb = len(blocks)

    def copies(ins, outs, sems):
        send, recv, lsem = sems
        x, y, c, chips = _place()
        me, sibling = (x, y, c), (x, y, 1 - c)

        def rows(b, px, py, pc):
            m_per = blocks[b].shape[0]
            return outs[b].at[pl.ds((4 * px + 2 * py + pc) * m_per, m_per), :]

        def copy(b, k, blk, to, src=None):
            return _remote(rows(b, *blk) if src is None else src, rows(b, *blk), send.at[b, k], recv.at[b, k], to)

        def mine(b):
            return pltpu.make_async_copy(ins[b], rows(b, *me), lsem.at[b])

        def first(b, k):
            return copy(b, k, me, sibling if k == 0 else (*chips[k - 1], c), src=ins[b])

        def passed(b, j):
            return copy(b, 4 + j, (*chips[j], c), sibling)

        def landed(b, j):
            return copy(b, 1 + j, (*chips[j], c), me)

        def handed(b, k):
            return copy(b, 0, sibling, me) if k == 0 else copy(b, 3 + k, (*chips[k - 1], 1 - c), me)

        return mine, first, passed, landed, handed

    def start(ins, outs, sems):
        mine, first, _, _, _ = copies(ins, outs, sems)
        for b in range(nb):
            mine(b).start()
            for k in range(4):
                first(b, k).start()

    def finish(ins, outs, sems):
        mine, first, passed, landed, handed = copies(ins, outs, sems)
        sent = []
        for b in range(nb):
            for j in range(3):
                landed(b, j).wait_recv()
                cp = passed(b, j)
                cp.start()
                sent.append(cp)
        for b in range(nb):
            for k in range(4):
                handed(b, k).wait_recv()
            for k in range(4):
                first(b, k).wait_send()
        for cp in sent:
            cp.wait_send()
        for b in range(nb):
            mine(b).wait()

    return _Exchange(blocks, [_sds((N_DEV * b.shape[0], b.shape[1]), F32) for b in blocks],
                     [pltpu.SemaphoreType.DMA((nb, 7)), pltpu.SemaphoreType.DMA((nb, 7)), pltpu.SemaphoreType.DMA((nb,))],
                     start, finish)


def _cast_shards(w_up, w_down, w_bp, w_ba, w_out, exchanges=()):
    r, c = w_up.shape
    tr = HALF // 2
    steps = r // tr

    def body(up_ref, down_ref, bp_ref, ba_ref, out_ref, ua_ref, ub_ref, da_ref, db_ref, bpo_ref, bao_ref, outo_ref):
        i = pl.program_id(0)

        @pl.when(i == 0)
        def _():
            for src, dst in ((bp_ref, bpo_ref), (ba_ref, bao_ref), (out_ref, outo_ref)):
                dst[...] = src[...].astype(BF16)

        @pl.when(i < steps // 2)
        def _():
            ua_ref[...] = up_ref[...].astype(BF16)
            da_ref[...] = down_ref[...].astype(BF16)

        @pl.when(i >= steps // 2)
        def _():
            ub_ref[...] = up_ref[...].astype(BF16)
            db_ref[...] = down_ref[...].astype(BF16)

    rows = _rows(tr, c)
    first = pl.BlockSpec((tr, c), lambda i: (jnp.minimum(i, steps // 2 - 1), 0))
    second = pl.BlockSpec((tr, c), lambda i: (jnp.maximum(i - steps // 2, 0), 0))
    half = _sds((HALF, c), BF16)
    return _call(
        body, name="cast_shards", grid=(steps,),
        in_specs=[rows, rows, _const(w_bp.shape), _const(w_ba.shape), _const(w_out.shape)],
        out_specs=[first, second, first, second, _const(w_bp.shape), _const(w_ba.shape), _const(w_out.shape)],
        out_shape=[half, half, half, half, _sds(w_bp.shape, BF16), _sds(w_ba.shape, BF16), _sds(w_out.shape, BF16)],
        args=(w_up, w_down, w_bp, w_ba, w_out), sem=("arbitrary",), exchanges=exchanges)


def _inproj(x2, g1, w_in_t, b_in, tabs, seq, exchanges=()):
    T = x2.shape[0]
    tm = min(TM, seq)
    nseq = seq // tm

    def body(x_ref, g_ref, w_ref, b_ref, c_ref, a_ref, bt_ref, h_ref, u_ref, q_ref, k_ref, v_ref, gate_ref):
        x = x_ref[...]
        h = (x * _rms(x) * g_ref[...]).astype(BF16)
        h_ref[...] = h

        def proj(lo, hi):
            return _dot_nt(h, w_ref[lo:hi, :]) + b_ref[:, lo:hi]

        c, a, bt = c_ref[...], a_ref[...], bt_ref[...]
        u_ref[...] = proj(0, C_Q)
        q = proj(C_Q, C_K)
        for p in range(4):
            sl = slice(LANES * p, LANES * (p + 1))
            q_ref[:, sl] = (_rot_fwd(q[:, sl], c, a, bt) * SCALE).astype(BF16)
        kv = proj(C_K, C_G)
        k_ref[...] = _rot_fwd(kv[:, :KV_WIDTH], c, a, bt).astype(BF16)
        v_ref[...] = kv[:, KV_WIDTH:].astype(BF16)
        for j in range(2):
            lo = C_G + D_MODEL * j
            gate_ref[:, D_MODEL * j:D_MODEL * (j + 1)] = jax.nn.sigmoid(proj(lo, lo + D_MODEL)).astype(BF16)

    tab = pl.BlockSpec((tm, LANES), lambda i: (i % nseq, 0))
    return _call(
        body, name="inproj", grid=(T // tm,),
        in_specs=[_rows(tm, D_MODEL), _const((1, D_MODEL)), _const((IN_WIDTH, D_MODEL)), _const((1, IN_WIDTH)),
                  tab, tab, tab],
        out_specs=[_rows(tm, D_MODEL), _rows(tm, POOL_WIDTH), _rows(tm, ATTN_WIDTH), _rows(tm, KV_WIDTH),
                   _rows(tm, KV_WIDTH), _rows(tm, GATE_WIDTH)],
        out_shape=[_sds((T, D_MODEL), BF16), _sds((T, POOL_WIDTH), F32), _sds((T, ATTN_WIDTH), BF16),
                   _sds((T, KV_WIDTH), BF16), _sds((T, KV_WIDTH), BF16), _sds((T, GATE_WIDTH), BF16)],
        args=(x2, g1, w_in_t, b_in, *tabs), sem=("parallel",), exchanges=exchanges)


def _inv_count(pos, w):
    return 1.0 / jnp.minimum(pos + 1, w).astype(F32)


def _pool_tile(i, tp, nseq, u_ref, prev_ref, w_ref, s_ref, diff_ref, y_ref):
    first = (i % nseq) == 0
    prev = jnp.where(first, 0.0, prev_ref[...])
    ext = jnp.concatenate([prev, u_ref[...]], axis=0)
    pos = (i % nseq) * tp + lax.broadcasted_iota(jnp.int32, (tp, 1), 0)
    for gi, w in enumerate(POOL_WINDOWS):
        sl = slice(POOL_GC * gi, POOL_GC * (gi + 1))
        xg = ext[:, sl]
        s = xg
        sh = 1
        while sh < w:
            s = s + pltpu.roll(s, sh, 0)
            sh *= 2
        pooled = s[HALO:] * _inv_count(pos, w)
        diff = (pooled - xg[HALO:]).astype(BF16)
        diff_ref[:, sl] = diff
        mixed = _dot(diff, w_ref[gi].astype(BF16))
        y_ref[:, sl] = (mixed * s_ref[:, sl]).astype(BF16)


def _pool_specs(tp):
    per = tp // HALO
    return [_rows(tp, POOL_WIDTH), pl.BlockSpec((HALO, POOL_WIDTH), lambda i: (jnp.maximum(i * per - 1, 0), 0)),
            _const((4, POOL_GC, POOL_GC)), _const((1, POOL_WIDTH))]


GROUP = 4
GROWS = GROUP * BLOCK


def _attn_masks(n):
    qi = lax.broadcasted_iota(jnp.int32, (GROWS, 2 * BLOCK), 0) % BLOCK
    kj = lax.broadcasted_iota(jnp.int32, (GROWS, 2 * BLOCK), 1)
    rel = qi + BLOCK - kj
    valid = (rel >= 0) & (rel < BLOCK) & (kj >= jnp.where(n > 0, 0, BLOCK))
    lo = lax.broadcasted_iota(jnp.int32, (BLOCK, LANES), 1) < HEAD_DIM
    return valid, lo


def _by_example(bl, *arrays):
    return [a.reshape(bl, a.shape[0] // bl, a.shape[1]) for a in arrays]


def _stack_heads(ref, h, lo):
    keep = lo if h == 0 else jnp.logical_not(lo)
    pieces = []
    for p in (2 * h, 2 * h + 1):
        xp = ref[:, LANES * p:LANES * (p + 1)].astype(F32)
        for e in range(2):
            t = xp if e == h else pltpu.roll(xp, HEAD_DIM, 1)
            pieces.append(jnp.where(keep, t, 0.0).astype(BF16))
    return jnp.concatenate(pieces, axis=0)


def _unstack_heads(stacked, h, lo):
    pairs = []
    for j in range(2):
        parts = []
        for e in range(2):
            t = stacked[BLOCK * (2 * j + e):BLOCK * (2 * j + e + 1)]
            parts.append(t if e == h else pltpu.roll(t, HEAD_DIM, 1))
        pairs.append(jnp.where(lo, parts[0], parts[1]))
    return pairs


def _sink_rows(sink_ref, h):
    head = lax.broadcasted_iota(jnp.int32, (GROWS, 1), 0) // BLOCK
    col = jnp.zeros((GROWS, 1), F32) + sink_ref[GROUP * h]
    for g in range(1, GROUP):
        col = jnp.where(head == g, sink_ref[GROUP * h + g], col)
    return col


def _group_probs(qs, kk, valid, sink):
    s = jnp.where(valid, _dot_nt(qs, kk), NEG_INF)
    m = jnp.maximum(jnp.max(s, axis=1, keepdims=True), sink)
    ex = jnp.exp(s - m)
    es = jnp.exp(sink - m)
    inv = 1.0 / (jnp.sum(ex, axis=1, keepdims=True) + es)
    return ex * inv, es * inv


def _mixers_fwd(q, k, v, sinks, u, w_pool, pool_scale, seq, exchanges=()):
    T = q.shape[0]
    nb = seq // BLOCK
    bl = T // seq
    tp = T // nb
    nseq = seq // tp

    def body(sink_ref, q_ref, kp_ref, kc_ref, vp_ref, vc_ref, u_ref, prev_ref, w_ref, s_ref, o_ref, diff_ref, y_ref):
        n = pl.program_id(0)
        valid, lo = _attn_masks(n)
        for b in range(bl):
            kk = jnp.concatenate([kp_ref[b], kc_ref[b]], axis=0)
            vv = jnp.concatenate([vp_ref[b], vc_ref[b]], axis=0)
            for h in range(2):
                qs = _stack_heads(q_ref.at[b], h, lo)
                pr, _ = _group_probs(qs, kk, valid, _sink_rows(sink_ref, h))
                o = _dot(pr.astype(BF16), vv)
                for j, pair in enumerate(_unstack_heads(o, h, lo)):
                    p = 2 * h + j
                    o_ref[b, :, LANES * p:LANES * (p + 1)] = pair.astype(BF16)
        _pool_tile(n, tp, nseq, u_ref, prev_ref, w_ref, s_ref, diff_ref, y_ref)

    cur = lambda n: (0, n, 0)
    prv = lambda n: (0, jnp.maximum(n - 1, 0), 0)
    kv = lambda m: pl.BlockSpec((bl, BLOCK, KV_WIDTH), m)
    res = _call(
        body, name="mixers_fwd", grid=(nb,),
        in_specs=[pl.BlockSpec(memory_space=pltpu.SMEM), pl.BlockSpec((bl, BLOCK, ATTN_WIDTH), cur),
                  kv(prv), kv(cur), kv(prv), kv(cur)] + _pool_specs(tp),
        out_specs=[pl.BlockSpec((bl, BLOCK, ATTN_WIDTH), cur), _rows(tp, POOL_WIDTH), _rows(tp, POOL_WIDTH)],
        out_shape=[_sds((bl, seq, ATTN_WIDTH), BF16), _sds((T, POOL_WIDTH), BF16), _sds((T, POOL_WIDTH), BF16)],
        args=(sinks, *_by_example(bl, q, k, k, v, v), u, u, w_pool, pool_scale), sem=("parallel",),
        exchanges=exchanges)
    outs, rest = res if exchanges else (res, None)
    return [outs[0].reshape(T, ATTN_WIDTH), outs[1], outs[2]], rest


def _branch(y, w_ref):
    return jnp.concatenate([_dot(y, w_ref[j]) for j in range(N_CHIPS)], axis=1)


def _merge_out(y_pool, y_attn, gate, x2, w_bp, w_ba, w_out, g2, g3, exchanges=()):
    T = x2.shape[0]
    tm = min(TM, T)

    def body(yp_ref, ya_ref, gate_ref, x_ref, wbp_ref, wba_ref, wo_ref, g2_ref, g3_ref,
             mg_ref, mix_ref, x1_ref, h2_ref):
        bp, ba = _branch(yp_ref[...], wbp_ref), _branch(ya_ref[...], wba_ref)
        merged = (gate_ref[:, :D_MODEL].astype(F32) * bp + gate_ref[:, D_MODEL:].astype(F32) * ba).astype(BF16)
        mg_ref[...] = merged
        mix = _dot(merged, wo_ref[...])
        mix_ref[...] = mix
        x1 = x_ref[...] + mix * _rms(mix) * g2_ref[...]
        x1_ref[...] = x1
        h2_ref[...] = (x1 * _rms(x1) * g3_ref[...]).astype(BF16)

    return _call(
        body, name="merge_out", grid=(T // tm,),
        in_specs=[_rows(tm, POOL_WIDTH), _rows(tm, ATTN_WIDTH), _rows(tm, GATE_WIDTH), _rows(tm, D_MODEL),
                  _const(w_bp.shape), _const(w_ba.shape), _const((D_MODEL, D_MODEL)),
                  _const((1, D_MODEL)), _const((1, D_MODEL))],
        out_specs=[_rows(tm, D_MODEL)] * 4,
        out_shape=[_sds((T, D_MODEL), BF16), _sds((T, D_MODEL), F32), _sds((T, D_MODEL), F32),
                   _sds((T, D_MODEL), BF16)],
        args=(y_pool, y_attn, gate, x2, w_bp, w_ba, w_out, g2, g3), sem=("parallel",), exchanges=exchanges)


HALF = D_MODEL // 2
TM_MLP = 256


def _mlp_core(h2, x1, mix, tgt, w_up, w_down, g4, g3, g2):
    T = h2.shape[0]
    tm = min(TM_MLP, T)

    def body(h_ref, x1_ref, mix_ref, t_ref, g_ref, g3_ref, g2_ref, ua_hbm, ub_hbm, da_hbm, db_hbm,
             act_ref, dff_ref, dup_ref, dx1_ref, dmix_ref, loss_ref, dg_ref, dg3_ref, dg2_ref,
             wu, wd, relu_scr, sems):
        def weight_copy(i):
            src, dst = ((ua_hbm, wu.at[:, :HALF]), (ub_hbm, wu.at[:, HALF:]),
                        (da_hbm, wd.at[:, :HALF]), (db_hbm, wd.at[:, HALF:]))[i]
            return pltpu.make_async_copy(src, dst, sems.at[i])

        @pl.when(pl.program_id(0) == 0)
        def _():
            for i in range(4):
                weight_copy(i).start()
            loss_ref[...] = jnp.zeros_like(loss_ref)
            for ref in (dg_ref, dg3_ref, dg2_ref):
                ref[...] = jnp.zeros_like(ref)
            weight_copy(0).wait()
            weight_copy(1).wait()

        h = h_ref[...]
        ff = None
        for j in range(N_CHIPS):
            lo = D_MODEL * j
            relu = jnp.maximum(_dot(h, wu[j]), 0.0)
            if j == 0:
                @pl.when(pl.program_id(0) == 0)
                def _():
                    weight_copy(2).wait()
                    weight_copy(3).wait()
            relu_scr[:, lo:lo + D_MODEL] = relu
            act = jnp.square(relu).astype(BF16)
            act_ref[:, lo:lo + D_MODEL] = act
            t = _dot(act, wd[j])
            ff = t if ff is None else ff + t
        g = g_ref[...]
        x1 = x1_ref[...]
        err = x1 + ff * _rms(ff) * g - t_ref[...]
        loss_ref[...] += jnp.sum(err * err) * (0.5 / D_MODEL)
        dy = err * (1.0 / D_MODEL)
        dff, dg = _norm_bwd(ff, g, dy)
        dg_ref[...] += dg
        dff = dff.astype(BF16)
        dff_ref[...] = dff
        dh2 = None
        for j in range(N_CHIPS):
            lo = D_MODEL * j
            dup = (_dot_nt(dff, wd[j]) * (2.0 * relu_scr[:, lo:lo + D_MODEL])).astype(BF16)
            dup_ref[:, lo:lo + D_MODEL] = dup
            t = _dot_nt(dup, wu[j])
            dh2 = t if dh2 is None else dh2 + t
        dx, dg3 = _norm_bwd(x1, g3_ref[...], dh2)
        dx1 = dy + dx
        dx1_ref[...] = dx1
        dg3_ref[...] += dg3
        dmix, dg2 = _norm_bwd(mix_ref[...], g2_ref[...], dx1)
        dmix_ref[...] = dmix.astype(BF16)
        dg2_ref[...] += dg2

    slabs = pltpu.VMEM((N_CHIPS, D_MODEL, D_MODEL), BF16)
    gain = _const((1, D_MODEL))
    return pl.pallas_call(
        body, name="mlp_core", grid=(T // tm,),
        in_specs=[_rows(tm, D_MODEL)] * 4 + [gain] * 3 + [ANY] * 4,
        out_specs=[_rows(tm, D_FF), _rows(tm, D_MODEL), _rows(tm, D_FF), _rows(tm, D_MODEL), _rows(tm, D_MODEL),
                   _const((8, LANES)), gain, gain, gain],
        out_shape=[_sds((T, D_FF), BF16), _sds((T, D_MODEL), BF16), _sds((T, D_FF), BF16), _sds((T, D_MODEL), F32),
                   _sds((T, D_MODEL), BF16), _sds((8, LANES), F32)] + [_sds((1, D_MODEL), F32)] * 3,
        scratch_shapes=[slabs] * 2 + [pltpu.VMEM((tm, D_FF), F32), pltpu.SemaphoreType.DMA((4,))],
        compiler_params=_cp("arbitrary"),
    )(h2, x1, mix, tgt, g4, g3, g2, *w_up, *w_down)


def _dw(tag, a, g, ta, tn, shard_cols=False, exchanges=()):
    T, ka = a.shape
    n = g.shape[1]
    tk = min(2 * TM, T)
    nk = T // tk

    def body(a_ref, g_ref, o_ref):
        @pl.when(pl.program_id(2) == 0)
        def _():
            o_ref[...] = jnp.zeros_like(o_ref)

        o_ref[...] += _dot_tn(a_ref[...], g_ref[...])

    if shard_cols:
        per = (n // N_CHIPS) // tn
        out_spec = pl.BlockSpec((None, ta, tn), lambda i, j, k: (j // per, i, j % per))
        out_shape = _sds((N_CHIPS, ka, n // N_CHIPS), F32)
    else:
        out_spec = pl.BlockSpec((ta, tn), lambda i, j, k: (i, j))
        out_shape = _sds((ka, n), F32)
    return _call(
        body, name="dw_" + tag, grid=(ka // ta, n // tn, nk),
        in_specs=[pl.BlockSpec((tk, ta), lambda i, j, k: (k, i)), pl.BlockSpec((tk, tn), lambda i, j, k: (k, j))],
        out_specs=[out_spec], out_shape=[out_shape],
        args=(a, g), sem=("parallel", "parallel", "arbitrary"), exchanges=exchanges)


def _dw_mix(merged, dmix, y_pool, dbp, y_attn, dba, exchanges=()):
    T = merged.shape[0]
    tk = min(2 * TM, T)
    c = D_MODEL // N_CHIPS

    def body(mg_ref, dmix_ref, yp_ref, dbp_ref, ya_ref, dba_ref, out_ref, bp_ref, ba_ref):
        @pl.when(pl.program_id(0) == 0)
        def _():
            for ref in (out_ref, bp_ref, ba_ref):
                ref[...] = jnp.zeros_like(ref)

        out_ref[...] += _dot_tn(mg_ref[...], dmix_ref[...])
        for y_ref, d_ref, o_ref in ((yp_ref, dbp_ref, bp_ref), (ya_ref, dba_ref, ba_ref)):
            res = _dot_tn(y_ref[...], d_ref[...])
            for j in range(N_CHIPS):
                o_ref[j] += res[:, c * j:c * (j + 1)]

    slabs = (N_CHIPS, POOL_WIDTH, c)
    return _call(
        body, name="dw_mix", grid=(T // tk,),
        in_specs=[_rows(tk, D_MODEL), _rows(tk, D_MODEL), _rows(tk, POOL_WIDTH), _rows(tk, D_MODEL),
                  _rows(tk, ATTN_WIDTH), _rows(tk, D_MODEL)],
        out_specs=[_const((D_MODEL, D_MODEL)), _const(slabs), _const(slabs)],
        out_shape=[_sds((D_MODEL, D_MODEL), F32), _sds(slabs, F32), _sds(slabs, F32)],
        args=(merged, dmix, y_pool, dbp, y_attn, dba), sem=("arbitrary",), exchanges=exchanges)


def _merge_bwd(dmix, gate, y_pool, y_attn, w_out, w_bp, w_ba, exchanges=()):
    T = dmix.shape[0]
    tm = min(TM, T)

    def body(dmix_ref, gate_ref, yp_ref, ya_ref, wo_ref, wbp_ref, wba_ref,
             dbp_ref, dba_ref, dgate_ref, dyp_ref, dya_ref):
        dm = _dot_nt(dmix_ref[...], wo_ref[...])
        for j, (y_ref, db_ref, w_ref, dy_ref) in enumerate(
                ((yp_ref, dbp_ref, wbp_ref, dyp_ref), (ya_ref, dba_ref, wba_ref, dya_ref))):
            sl = slice(D_MODEL * j, D_MODEL * (j + 1))
            gt = gate_ref[:, sl].astype(F32)
            db = (dm * gt).astype(BF16)
            db_ref[...] = db
            dgate_ref[:, sl] = (dm * _branch(y_ref[...], w_ref) * gt * (1.0 - gt)).astype(BF16)
            cw = D_MODEL // N_CHIPS
            dy = _dot_nt(db[:, :cw], w_ref[0])
            for c in range(1, N_CHIPS):
                dy = dy + _dot_nt(db[:, cw * c:cw * (c + 1)], w_ref[c])
            dy_ref[...] = dy.astype(dy_ref.dtype)

    return _call(
        body, name="merge_bwd", grid=(T // tm,),
        in_specs=[_rows(tm, D_MODEL), _rows(tm, GATE_WIDTH), _rows(tm, POOL_WIDTH), _rows(tm, ATTN_WIDTH),
                  _const((D_MODEL, D_MODEL)), _const(w_bp.shape), _const(w_ba.shape)],
        out_specs=[_rows(tm, D_MODEL), _rows(tm, D_MODEL), _rows(tm, GATE_WIDTH), _rows(tm, POOL_WIDTH),
                   _rows(tm, ATTN_WIDTH)],
        out_shape=[_sds((T, D_MODEL), BF16), _sds((T, D_MODEL), BF16), _sds((T, GATE_WIDTH), BF16),
                   _sds((T, POOL_WIDTH), F32), _sds((T, ATTN_WIDTH), BF16)],
        args=(dmix, gate, y_pool, y_attn, w_out, w_bp, w_ba), sem=("parallel",), exchanges=exchanges)


def _mixers_bwd(q, k, v, do, sinks, tabs, dyp, diff, w_pool, pool_scale, seq, exchanges=()):
    T = q.shape[0]
    nb = seq // BLOCK
    bl = T // seq
    steps = nb + 1
    tp = T // nb
    nseq = seq // tp
    per = tp // HALO
    last_halo = T // HALO - 1

    def body(sink_ref, q_ref, do_ref, kp_ref, kc_ref, vp_ref, vc_ref, c_ref, a_ref, bt_ref, cp_ref, ap_ref, btp_ref,
             dy_ref, nxt_ref, diff_ref, w_ref, s_ref,
             dq_ref, dk_ref, dv_ref, dsink_ref, du_ref, dw_ref, ds_ref, ck_ref, cv_ref):
        n = pl.program_id(0)

        @pl.when(n == 0)
        def _():
            for ref in (dsink_ref, ck_ref, cv_ref, dw_ref, ds_ref):
                ref[...] = jnp.zeros_like(ref)

        @pl.when(n < nb)
        def _():
            _pool_bwd_tile(n, tp, nseq, dy_ref, nxt_ref, diff_ref, w_ref, s_ref, du_ref, dw_ref, ds_ref)
            valid, lo = _attn_masks(n)
            for b in range(bl):
                kk = jnp.concatenate([kp_ref[b], kc_ref[b]], axis=0)
                vv = jnp.concatenate([vp_ref[b], vc_ref[b]], axis=0)
                dk_acc = jnp.zeros((2 * BLOCK, KV_WIDTH), F32)
                dv_acc = jnp.zeros((2 * BLOCK, KV_WIDTH), F32)
                for h in range(2):
                    qs = _stack_heads(q_ref.at[b], h, lo)
                    dos = _stack_heads(do_ref.at[b], h, lo)
                    pr, ps = _group_probs(qs, kk, valid, _sink_rows(sink_ref, h))
                    dp = _dot_nt(dos, vv)
                    delta = jnp.sum(pr * dp, axis=1, keepdims=True)
                    ds = (pr * (dp - delta)).astype(BF16)
                    dsk = ps * delta
                    for g in range(GROUP):
                        idx = GROUP * h + g
                        dsink_ref[idx:idx + 1, :] += (jnp.zeros((1, LANES), F32)
                                                      - jnp.sum(dsk[BLOCK * g:BLOCK * (g + 1)]))
                    dk_acc = dk_acc + _dot_tn(ds, qs)
                    dv_acc = dv_acc + _dot_tn(pr.astype(BF16), dos)
                    for j, pair in enumerate(_unstack_heads(_dot(ds, kk) * SCALE, h, lo)):
                        sl = slice(LANES * (2 * h + j), LANES * (2 * h + j + 1))
                        dq_ref[b, :, sl] = _rot_bwd(pair, c_ref[...], a_ref[...], bt_ref[...]).astype(BF16)
                fin_k = ck_ref[b] + dk_acc[:BLOCK]
                dk_ref[b] = _rot_bwd(fin_k, cp_ref[...], ap_ref[...], btp_ref[...]).astype(BF16)
                dv_ref[b] = (cv_ref[b] + dv_acc[:BLOCK]).astype(BF16)
                ck_ref[b] = dk_acc[BLOCK:]
                cv_ref[b] = dv_acc[BLOCK:]

        @pl.when(n == nb)
        def _():
            for b in range(bl):
                dk_ref[b] = _rot_bwd(ck_ref[b], cp_ref[...], ap_ref[...], btp_ref[...]).astype(BF16)
                dv_ref[b] = cv_ref[b].astype(BF16)

    cur = lambda n: (0, jnp.minimum(n, nb - 1), 0)
    prv = lambda n: (0, jnp.clip(n - 1, 0, nb - 1), 0)
    tcur = lambda n: (jnp.minimum(n, nb - 1), 0)
    tprv = lambda n: (jnp.clip(n - 1, 0, nb - 1), 0)
    wide = lambda m: pl.BlockSpec((bl, BLOCK, ATTN_WIDTH), m)
    kv = lambda m: pl.BlockSpec((bl, BLOCK, KV_WIDTH), m)
    tab = lambda m: pl.BlockSpec((BLOCK, LANES), m)
    tile = lambda n: (jnp.minimum(n, nb - 1), 0)
    halo = lambda n: (jnp.minimum((jnp.minimum(n, nb - 1) + 1) * per, last_halo), 0)
    rows = pl.BlockSpec((tp, POOL_WIDTH), tile)
    res = _call(
        body, name="mixers_bwd", grid=(steps,),
        in_specs=[pl.BlockSpec(memory_space=pltpu.SMEM), wide(cur), wide(cur), kv(prv), kv(cur), kv(prv), kv(cur),
                  tab(tcur), tab(tcur), tab(tcur), tab(tprv), tab(tprv), tab(tprv),
                  rows, pl.BlockSpec((HALO, POOL_WIDTH), halo), rows, _const((4, POOL_GC, POOL_GC)),
                  _const((1, POOL_WIDTH))],
        out_specs=[wide(cur), kv(prv), kv(prv), _const((8, LANES)), rows, _const((4, POOL_GC, POOL_GC)),
                   _const((1, POOL_WIDTH))],
        out_shape=[_sds((bl, seq, ATTN_WIDTH), BF16), _sds((bl, seq, KV_WIDTH), BF16),
                   _sds((bl, seq, KV_WIDTH), BF16), _sds((8, LANES), F32), _sds((T, POOL_WIDTH), BF16),
                   _sds((4, POOL_GC, POOL_GC), F32), _sds((1, POOL_WIDTH), F32)],
        scratch=[pltpu.VMEM((bl, BLOCK, KV_WIDTH), F32), pltpu.VMEM((bl, BLOCK, KV_WIDTH), F32)],
        args=(sinks, *_by_example(bl, q, do, k, k, v, v), *tabs, *tabs, dyp, dyp, diff, w_pool, pool_scale),
        sem=("arbitrary",), exchanges=exchanges)
    outs, rest = (res if exchanges else (res, None))
    outs = [outs[0].reshape(T, ATTN_WIDTH), outs[1].reshape(T, KV_WIDTH), outs[2].reshape(T, KV_WIDTH), *outs[3:]]
    return (outs, rest) if exchanges else outs


def _pool_bwd_tile(i, tp, nseq, dy_ref, nxt_ref, diff_ref, w_ref, s_ref, du_ref, dw_ref, ds_ref):
    last = (i % nseq) == nseq - 1
    nxt = jnp.where(last, 0.0, nxt_ref[...])
    ext = jnp.concatenate([dy_ref[...], nxt], axis=0) * s_ref[...]
    pos = (i % nseq) * tp + lax.broadcasted_iota(jnp.int32, (tp + HALO, 1), 0)
    for gi, w in enumerate(POOL_WINDOWS):
        sl = slice(POOL_GC * gi, POOL_GC * (gi + 1))
        wg = w_ref[gi].astype(BF16)
        dmx = ext[:, sl].astype(BF16)
        ddiff = _dot_nt(dmx, wg)
        s = ddiff * _inv_count(pos, w)
        sh = 1
        while sh < w:
            s = s + pltpu.roll(s, tp + HALO - sh, 0)
            sh *= 2
        du_ref[:, sl] = (s[:tp] - ddiff[:tp]).astype(BF16)
        dg = diff_ref[:, sl]
        dw_ref[gi] += _dot_tn(dg, dmx[:tp])
        ds_ref[:, sl] += jnp.sum(dy_ref[:, sl] * _dot(dg, wg), axis=0, keepdims=True)


_PARTS = ((0, C_Q), (C_Q, C_K), (C_K, C_V), (C_V, C_G), (C_G, IN_WIDTH))


def _inproj_bwd(parts, x2, dx1, w_in_t, g1, exchanges=()):
    T = x2.shape[0]
    tm = min(TM, T)

    def body(du_ref, dq_ref, dk_ref, dv_ref, dgt_ref, x_ref, dx1_ref, w_ref, g_ref, gx_ref, dg_ref):
        @pl.when(pl.program_id(0) == 0)
        def _():
            dg_ref[...] = jnp.zeros_like(dg_ref)

        dh = jnp.zeros((tm, D_MODEL), F32)
        for (lo, hi), p_ref in zip(_PARTS, (du_ref, dq_ref, dk_ref, dv_ref, dgt_ref)):
            dh = dh + _dot(p_ref[...], w_ref[lo:hi, :])
        dx, dg = _norm_bwd(x_ref[...], g_ref[...], dh)
        gx_ref[...] = dx1_ref[...] + dx
        dg_ref[...] += dg

    return _call(
        body, name="inproj_bwd", grid=(T // tm,),
        in_specs=[_rows(tm, hi - lo) for lo, hi in _PARTS]
        + [_rows(tm, D_MODEL), _rows(tm, D_MODEL), _const((IN_WIDTH, D_MODEL)), _const((1, D_MODEL))],
        out_specs=[_rows(tm, D_MODEL), _const((1, D_MODEL))],
        out_shape=[_sds((T, D_MODEL), F32), _sds((1, D_MODEL), F32)],
        args=(*parts, x2, dx1, w_in_t, g1), sem=("arbitrary",), exchanges=exchanges)


def _dw_in(h, parts, exchanges=()):
    T = h.shape[0]
    tk = min(TM, T)

    def body(h_ref, du_ref, dq_ref, dk_ref, dv_ref, dgt_ref, o_ref, db_ref):
        @pl.when(pl.program_id(0) == 0)
        def _():
            o_ref[...] = jnp.zeros_like(o_ref)
            db_ref[...] = jnp.zeros_like(db_ref)

        hh = h_ref[...]
        ones = jnp.ones((8, tk), BF16)
        for (lo, hi), p_ref in zip(_PARTS, (du_ref, dq_ref, dk_ref, dv_ref, dgt_ref)):
            part = p_ref[...]
            o_ref[lo:hi, :] += _dot_tn(part, hh)
            db_ref[:, lo:hi] += _dot(ones, part)[:1]

    return _call(
        body, name="dw_in", grid=(T // tk,),
        in_specs=[_rows(tk, D_MODEL)] + [_rows(tk, hi - lo) for lo, hi in _PARTS],
        out_specs=[_const((IN_WIDTH, D_MODEL)), _const((1, IN_WIDTH))],
        out_shape=[_sds((IN_WIDTH, D_MODEL), F32), _sds((1, IN_WIDTH), F32)],
        args=(h, *parts), sem=("arbitrary",), exchanges=exchanges)


def _row_tile(rows, cap=256, mult=16):
    best = None
    for t in range(mult, min(rows, cap) + 1, mult):
        if rows % t == 0:
            best = t
    if best is None:
        raise ValueError("no row tile for %d rows" % rows)
    return best


def _pair_sum(ids, full, got):
    _, r, c = full.shape
    hr = r // 2
    tr = _row_tile(hr, cap=512)
    nblk = hr // tr

    def body(ids_ref, a_ref, b_ref, own_ref, sb_ref):
        s = a_ref[...] + b_ref[...]
        sb_ref[...] = s.astype(BF16)

        @pl.when(pl.program_id(1) == ids_ref[0])
        def _():
            own_ref[...] = s

    slab = pl.BlockSpec((None, tr, c), lambda i, j, ids_ref: (j, i, 0))
    return pl.pallas_call(
        body, name="pair_sum_%dx%d" % (r, c),
        grid_spec=pltpu.PrefetchScalarGridSpec(
            num_scalar_prefetch=1, grid=(nblk, N_CHIPS),
            in_specs=[pl.BlockSpec((None, tr, c), lambda i, j, ids_ref: (j, ids_ref[1] * nblk + i, 0)), slab],
            out_specs=[pl.BlockSpec((tr, c), lambda i, j, ids_ref: (i, 0)), slab]),
        out_shape=[_sds((hr, c), F32), _sds((N_CHIPS, hr, c), BF16)],
        compiler_params=_cp("parallel", "arbitrary"),
    )(ids, full, got)


def _pair_sum_small(ids, fulls, gots):
    n = len(fulls)
    dims = [(f.shape[1] // 2, f.shape[2]) for f in fulls]

    def body(ids_ref, *refs):
        ins, outs = refs[:2 * n], refs[2 * n:]
        for k in range(n):
            s = ins[2 * k][...] + ins[2 * k + 1][...]
            outs[2 * k + 1][...] = s.astype(BF16)

            @pl.when(pl.program_id(0) == ids_ref[0])
            def _(k=k, s=s):
                outs[2 * k][...] = s

    in_specs, out_specs, out_shape = [], [], []
    for hr, c in dims:
        slab = pl.BlockSpec((None, hr, c), lambda j, ids_ref: (j, 0, 0))
        in_specs += [pl.BlockSpec((None, hr, c), lambda j, ids_ref: (j, ids_ref[1], 0)), slab]
        out_specs += [pl.BlockSpec((hr, c), lambda j, ids_ref: (0, 0)), slab]
        out_shape += [_sds((hr, c), F32), _sds((N_CHIPS, hr, c), BF16)]
    res = pl.pallas_call(
        body, name="pair_sum_small",
        grid_spec=pltpu.PrefetchScalarGridSpec(num_scalar_prefetch=1, grid=(N_CHIPS,), in_specs=in_specs,
                                               out_specs=out_specs),
        out_shape=out_shape, compiler_params=_cp("arbitrary"),
    )(ids, *[a for pair in zip(fulls, gots) for a in pair])
    return [(res[2 * k], res[2 * k + 1]) for k in range(n)]


def _chip_sum_small(ids, owns, gots):
    n = len(owns)

    def body(ids_ref, *refs):
        ins, outs = refs[:2 * n], refs[2 * n:]
        for k in range(n):
            a, b = ins[2 * k], ins[2 * k + 1]
            outs[k][...] = ((a[...] + b[0].astype(F32)) + b[1].astype(F32)) + b[2].astype(F32)

    in_specs, out_specs, out_shape = [], [], []
    for own in owns:
        hr, c = own.shape
        in_specs += [pl.BlockSpec((hr, c), lambda i, ids_ref: (0, 0)),
                     pl.BlockSpec((3, hr, c), lambda i, ids_ref: (0, 0, 0))]
        out_specs.append(pl.BlockSpec((hr, c), lambda i, ids_ref: (ids_ref[1], 0)))
        out_shape.append(_sds((2 * hr, c), F32))
    return pl.pallas_call(
        body, name="chip_sum_small",
        grid_spec=pltpu.PrefetchScalarGridSpec(num_scalar_prefetch=1, grid=(1,), in_specs=in_specs,
                                               out_specs=out_specs),
        out_shape=out_shape, compiler_params=_cp("arbitrary"),
    )(ids, *[a for pair in zip(owns, gots) for a in pair])


def _chip_sum(ids, own, got):
    hr, c = own.shape
    tr = _row_tile(hr, cap=512)
    nblk = hr // tr

    def body(ids_ref, a_ref, b_ref, o_ref):
        o_ref[...] = ((a_ref[...] + b_ref[0].astype(F32)) + b_ref[1].astype(F32)) + b_ref[2].astype(F32)

    return pl.pallas_call(
        body, name="chip_sum_%dx%d" % (hr, c),
        grid_spec=pltpu.PrefetchScalarGridSpec(
            num_scalar_prefetch=1, grid=(nblk,),
            in_specs=[pl.BlockSpec((tr, c), lambda i, ids_ref: (i, 0)),
                      pl.BlockSpec((3, tr, c), lambda i, ids_ref: (0, i, 0))],
            out_specs=pl.BlockSpec((tr, c), lambda i, ids_ref: (ids_ref[1] * nblk + i, 0))),
        out_shape=_sds((2 * hr, c), F32),
        compiler_params=_cp("parallel"),
    )(ids, own, got)


def _adamw_math(w, g, m, v):
    nm = ADAM_B1 * m + (1.0 - ADAM_B1) * g
    nv = ADAM_B2 * v + (1.0 - ADAM_B2) * (g * g)
    m_hat = nm / (1.0 - ADAM_B1 ** ADAM_STEP)
    v_hat = nv / (1.0 - ADAM_B2 ** ADAM_STEP)
    return -ADAM_LR * (m_hat / (jnp.sqrt(v_hat) + ADAM_EPS) + ADAM_WD * w), nm, nv


def _adamw(w, g, m, v):
    r, c = w.shape
    tr = _row_tile(r, cap=512, mult=8)

    def body(w_ref, g_ref, m_ref, v_ref, d_ref, nm_ref, nv_ref):
        d_ref[...], nm_ref[...], nv_ref[...] = _adamw_math(w_ref[...], g_ref[...], m_ref[...], v_ref[...])

    spec = _rows(tr, c)
    return pl.pallas_call(
        body, name="adamw_%dx%d" % (r, c), grid=(r // tr,),
        in_specs=[spec] * 4, out_specs=[spec] * 3, out_shape=[_sds((r, c), F32)] * 3,
        compiler_params=_cp("parallel"),
    )(w, g, m, v)


SC_TILES = 32
SC_LANES = 16
SC_ROWS = 8


def _adamw_sparse(w, g, m, v):
    r, c = w.shape
    rows = r // SC_TILES
    step = min(rows, SC_ROWS)

    def body(w_hbm, g_hbm, m_hbm, v_hbm, d_hbm, nm_hbm, nv_hbm, wb, gb, mb, vb):
        tile = lax.axis_index("sc_subcore") * 2 + lax.axis_index("sc_core")

        @pl.loop(0, rows, step=step)
        def _(r0):
            mine = pl.ds(tile * rows + r0, step)
            for src, dst in ((w_hbm, wb), (g_hbm, gb), (m_hbm, mb), (v_hbm, vb)):
                pltpu.sync_copy(src.at[mine], dst)

            @pl.loop(0, step)
            def _(row):
                @pl.loop(0, c, step=SC_LANES)
                def _(i):
                    at = (row, pl.ds(i, SC_LANES))
                    wb[at], mb[at], vb[at] = _adamw_math(wb[at], gb[at], mb[at], vb[at])

            for src, dst in ((wb, d_hbm), (mb, nm_hbm), (vb, nv_hbm)):
                pltpu.sync_copy(src, dst.at[mine])

    return pl.kernel(
        body, name="adamw_sparse_%dx%d" % (r, c), out_type=[_sds((r, c), F32)] * 3,
        mesh=plsc.VectorSubcoreMesh(core_axis_name="sc_core", subcore_axis_name="sc_subcore"),
        scratch_types=[pltpu.VMEM((step, c), F32)] * 4,
    )(w, g, m, v)


_SMALL_NAMES = ("w_pool", "b_in", "g_mix_pre", "g_mix_post", "g_mlp_pre", "g_mlp_post", "pool_scale", "attn_sinks")
B_ROWS = -(-IN_WIDTH // D_MODEL)


def _row_block(rows):
    rows = [jnp.pad(r.astype(F32), ((0, 0), (0, D_MODEL - r.shape[1]))) for r in rows]
    return jnp.pad(jnp.concatenate(rows, axis=0), ((0, 8 - len(rows)), (0, 0)))


def _early_block(dg2, dg3, dg4, dps, dsink, loss):
    tail = jnp.concatenate([jnp.pad(dsink.reshape(1, -1), ((0, 0), (0, LANES - dsink.size))),
                            jnp.pad(loss.reshape(1, 1), ((0, 0), (0, LANES - 1)))], axis=1)
    return _row_block([dg2, dg3, dg4, dps, tail])


def _late_block(db_in, dg1):
    b = jnp.pad(db_in, ((0, 0), (0, B_ROWS * D_MODEL - IN_WIDTH))).reshape(B_ROWS, D_MODEL)
    return _row_block([b[r:r + 1] for r in range(B_ROWS)] + [dg1])


def _small_update(gearly, gmat, glate, w, m, v):
    names = _SMALL_NAMES
    n = len(names)

    def total(ref, rows):
        acc = ref[0:rows, :]
        for d in range(1, N_DEV):
            acc = acc + ref[d * rows:(d + 1) * rows, :]
        return acc

    def body(*refs):
        early_ref, gmat_ref, late_ref = refs[:3]
        w_refs, m_refs, v_refs = refs[3:3 + n], refs[3 + n:3 + 2 * n], refs[3 + 2 * n:3 + 3 * n]
        outs = refs[3 + 3 * n:]
        loss_ref, g_refs, d_refs = outs[0], outs[1:1 + n], outs[1 + n:1 + 2 * n]
        nm_refs, nv_refs = outs[1 + 2 * n:1 + 3 * n], outs[1 + 3 * n:1 + 4 * n]
        early, late = total(early_ref, 8), total(late_ref, 8)
        loss_ref[...] = jnp.sum(early[4:5, LANES:2 * LANES], axis=1, keepdims=True)
        bias = jnp.concatenate([late[r:r + 1, :] for r in range(B_ROWS - 1)]
                               + [late[B_ROWS - 1:B_ROWS, :IN_WIDTH - (B_ROWS - 1) * D_MODEL]], axis=1)
        grad = dict(b_in=bias, g_mix_pre=late[B_ROWS:B_ROWS + 1, :], g_mix_post=early[0:1, :],
                    g_mlp_pre=early[1:2, :], g_mlp_post=early[2:3, :], pool_scale=early[3:4, :POOL_WIDTH],
                    attn_sinks=early[4:5, :N_Q_HEADS])
        for i, name in enumerate(names):
            g = total(gmat_ref, 4 * POOL_GC) if name == "w_pool" else grad[name]
            g_refs[i][...] = g
            d_refs[i][...], nm_refs[i][...], nv_refs[i][...] = _adamw_math(
                w_refs[i][...], g, m_refs[i][...], v_refs[i][...])

    shapes = [_sds(w[k].shape, F32) for k in names]
    res = pl.pallas_call(
        body, name="small_update", out_shape=[_sds((1, 1), F32)] + shapes * 4,
        compiler_params=pltpu.CompilerParams(vmem_limit_bytes=VMEM_MB * 1024 * 1024),
    )(gearly, gmat, glate, *[w[k] for k in names], *[m[k] for k in names], *[v[k] for k in names])
    loss = res[0]
    per = {k: tuple(res[1 + j * n + i] for j in range(4)) for i, k in enumerate(names)}
    return loss, per


_BIG = ("w_in", "w_branch_pool", "w_branch_attn", "w_out", "w_up", "w_down")
_ORDER = ("g_mix_pre", "w_in", "b_in", "w_pool", "pool_scale", "attn_sinks", "w_branch_pool", "w_branch_attn",
          "w_out", "g_mix_post", "g_mlp_pre", "w_up", "w_down", "g_mlp_post")


def _stack_rows(slab):
    return slab.reshape(-1, slab.shape[2])


def _step(x2, tgt, seq, shards, small, ids):
    tabs = _rope_tables(seq)
    g1, g2, g3, g4 = (small[n] for n in ("g_mix_pre", "g_mix_post", "g_mlp_pre", "g_mlp_post"))
    sinks = small["attn_sinks"].reshape(N_Q_HEADS)
    w_pool = small["w_pool"].reshape(4, POOL_GC, POOL_GC)
    pool_scale = small["pool_scale"]

    def whole(shard, slabs):
        return lax.dynamic_update_slice(slabs, shard[None], (ids[0], 0, 0))

    (up_a, up_b, down_a, down_b, *mix_shards), [[in_slab]] = _cast_shards(
        *(shards[n] for n in ("w_up", "w_down", "w_branch_pool", "w_branch_attn", "w_out")),
        exchanges=[_ex_gather([shards["w_in"]])])
    w_in = _stack_rows(whole(shards["w_in"], in_slab))
    (h, u, q, k, v, gate), [mix_slabs] = _inproj(
        x2, g1, w_in, small["b_in"], tabs, seq, exchanges=[_ex_gather(mix_shards)])
    w_bp, w_ba, out_slab = (whole(s, g) for s, g in zip(mix_shards, mix_slabs))
    w_out = _stack_rows(out_slab)
    (y_attn, diff, y_pool), [[got_a, got_b]] = _mixers_fwd(
        q, k, v, sinks, u, w_pool, pool_scale, seq, exchanges=[_ex_gather([up_a, up_b])])
    (merged, mix, x1, h2), [[got_c, got_d]] = _merge_out(
        y_pool, y_attn, gate, x2, w_bp, w_ba, w_out, g2, g3, exchanges=[_ex_gather([down_a, down_b])])
    w_up = (whole(up_a, got_a), whole(up_b, got_b))
    w_down = (whole(down_a, got_c), whole(down_b, got_d))
    act, dff, dup, dx1, dmix, loss_acc, dg4, dg3, dg2 = _mlp_core(h2, x1, mix, tgt, w_up, w_down, g4, g3, g2)

    dw_down = _dw("down", act, dff, 2048, 1024)[0].reshape(N_CHIPS, D_FF // N_CHIPS, D_MODEL)
    (dbp, dba, dgate, dyp, dya), [[got]] = _merge_bwd(
        dmix, gate, y_pool, y_attn, w_out, w_bp, w_ba, exchanges=[_ex_pair([dw_down])])
    ps_down = _pair_sum(ids, dw_down, got)
    (dw_up,), [[got]] = _dw("up", h2, dup, 1024, 1024, shard_cols=True, exchanges=[_ex_chip([ps_down[1]])])
    half_down = _chip_sum(ids, ps_down[0], got)
    (dw_out, dw_bp, dw_ba), [[got]] = _dw_mix(merged, dmix, y_pool, dbp, y_attn, dba, exchanges=[_ex_pair([dw_up])])
    ps_up = _pair_sum(ids, dw_up, got)
    dw_mix = [dw_out.reshape(N_CHIPS, D_MODEL // N_CHIPS, D_MODEL), dw_bp, dw_ba]
    (dq, dk, dv, dsink, du, dw_pool, dps), [[got], gots, [g_down]] = _mixers_bwd(
        q, k, v, dya, sinks, tabs, dyp, diff, w_pool, pool_scale, seq,
        exchanges=[_ex_chip([ps_up[1]]), _ex_pair(dw_mix), _ex_swap([half_down])])
    half_up = _chip_sum(ids, ps_up[0], got)
    ps_mix = _pair_sum_small(ids, dw_mix, gots)
    parts = (du, dq, dk, dv, dgate)
    early = _early_block(dg2, dg3, dg4, dps, dsink[:, 0], loss_acc[0, 0])
    mat = dw_pool.reshape(4 * POOL_GC, POOL_GC)
    (dw_in_t, db_in), [gots, [gearly, gmat], [g_up]] = _dw_in(
        h, parts, exchanges=[_ex_chip([p[1] for p in ps_mix]), _ex_allgather([early, mat]), _ex_swap([half_up])])
    half_mix = _chip_sum_small(ids, [p[0] for p in ps_mix], gots)
    dw_in = dw_in_t.reshape(N_CHIPS, IN_WIDTH // N_CHIPS, D_MODEL)
    g_mix, [got] = _alone("swap_mix_pair_in", _ex_swap(half_mix), _ex_pair([dw_in]))
    ps_in = _pair_sum(ids, dw_in, got)
    (gx, dg1), [[got]] = _inproj_bwd(parts, x2, dx1, w_in, g1, exchanges=[_ex_chip([ps_in[1]])])
    [g_in], [glate] = _alone("swap_in_allgather", _ex_swap([_chip_sum(ids, ps_in[0], got)]),
                             _ex_allgather([_late_block(db_in, dg1)]))

    grads = dict(w_in=g_in, w_branch_pool=g_mix[1], w_branch_attn=g_mix[2], w_out=g_mix[0], w_up=g_up, w_down=g_down)
    return (gearly, gmat, glate), gx, grads


def kernel(x, g_mix_pre, w_in, b_in, w_pool, pool_scale, attn_sinks, w_branch_pool, w_branch_attn, w_out, g_mix_post, g_mlp_pre, w_up, w_down, g_mlp_post, loss_target, m_g_mix_pre, m_w_in, m_b_in, m_w_pool, m_pool_scale, m_attn_sinks, m_w_branch_pool, m_w_branch_attn, m_w_out, m_g_mix_post, m_g_mlp_pre, m_w_up, m_w_down, m_g_mlp_post, v_g_mix_pre, v_w_in, v_b_in, v_w_pool, v_pool_scale, v_attn_sinks, v_w_branch_pool, v_w_branch_attn, v_w_out, v_g_mix_post, v_g_mlp_pre, v_w_up, v_w_down, v_g_mlp_post):
    weights = dict(g_mix_pre=g_mix_pre, w_in=w_in, b_in=b_in, w_pool=w_pool, pool_scale=pool_scale,
                   attn_sinks=attn_sinks, w_branch_pool=w_branch_pool, w_branch_attn=w_branch_attn, w_out=w_out,
                   g_mix_post=g_mix_post, g_mlp_pre=g_mlp_pre, w_up=w_up, w_down=w_down, g_mlp_post=g_mlp_post)
    mom1 = dict(g_mix_pre=m_g_mix_pre, w_in=m_w_in, b_in=m_b_in, w_pool=m_w_pool, pool_scale=m_pool_scale,
                attn_sinks=m_attn_sinks, w_branch_pool=m_w_branch_pool, w_branch_attn=m_w_branch_attn,
                w_out=m_w_out, g_mix_post=m_g_mix_post, g_mlp_pre=m_g_mlp_pre, w_up=m_w_up, w_down=m_w_down,
                g_mlp_post=m_g_mlp_post)
    mom2 = dict(g_mix_pre=v_g_mix_pre, w_in=v_w_in, b_in=v_b_in, w_pool=v_w_pool, pool_scale=v_pool_scale,
                attn_sinks=v_attn_sinks, w_branch_pool=v_w_branch_pool, w_branch_attn=v_w_branch_attn,
                w_out=v_w_out, g_mix_post=v_g_mix_post, g_mlp_pre=v_g_mlp_pre, w_up=v_w_up, w_down=v_w_down,
                g_mlp_post=v_g_mlp_post)
    b_loc, seq, _ = x.shape
    x2 = x.reshape(b_loc * seq, D_MODEL)
    tgt = loss_target.reshape(b_loc * seq, D_MODEL)
    ids = jnp.stack([2 * lax.axis_index("x") + lax.axis_index("y"), lax.axis_index("c")]).astype(jnp.int32)

    def flat(n, a):
        return a[0].T if n == "w_in" else a[0]

    def unflat(n, a):
        return (a.T if n == "w_in" else a)[None]

    shards = {n: flat(n, weights[n]).astype(BF16) if n == "w_in" else flat(n, weights[n]) for n in _BIG}
    small = {n: weights[n] for n in _ORDER if n not in _BIG}
    (gearly, gmat, glate), gx, grads = _step(x2, tgt, seq, shards, small, ids)

    def two_d(src):
        return {n: src[n].reshape(4 * POOL_GC, POOL_GC) if n == "w_pool" else src[n] for n in _SMALL_NAMES}

    loss, per = _small_update(gearly, gmat, glate, two_d(weights), two_d(mom1), two_d(mom2))
    delta, new_m, new_v = {}, {}, {}
    for n in _SMALL_NAMES:
        grads[n], delta[n], new_m[n], new_v[n] = (a.reshape(weights[n].shape) for a in per[n])
    for n in _BIG:
        update = _adamw if n == "w_in" else _adamw_sparse
        d, nm, nv = update(flat(n, weights[n]), grads[n], flat(n, mom1[n]), flat(n, mom2[n]))
        grads[n] = unflat(n, grads[n])
        delta[n], new_m[n], new_v[n] = unflat(n, d), unflat(n, nm), unflat(n, nv)

    return (loss[0, 0], gx.reshape(x.shape), *[grads[n] for n in _ORDER], *[delta[n] for n in _ORDER],
            *[new_m[n] for n in _ORDER], *[new_v[n] for n in _ORDER])
```

```python
import jax
import jax.numpy as jnp
from jax import lax
from jax.experimental import pallas as pl
from jax.experimental.pallas import tpu as pltpu
from jax.experimental.pallas import tpu_sc as plsc

F32 = jnp.float32
BF16 = jnp.bfloat16

D_MODEL = 1024
POOL_WINDOWS = (2, 4, 8, 16)
POOL_WIDTH = 512
POOL_GC = 128
HALO = 16
HEAD_DIM = 64
N_Q_HEADS = 8
ATTN_WIDTH = 512
KV_WIDTH = 128
BLOCK = 128
NEG_INF = -1e30
ROPE_THETA = 500000.0
ROT_DIM = 16
GATE_WIDTH = 2048
IN_WIDTH = 3328
D_FF = 4096
EPS = 1e-6
SCALE = HEAD_DIM ** -0.5
C_Q, C_K, C_V, C_G = 512, 1024, 1152, 1280

ADAM_LR, ADAM_B1, ADAM_B2, ADAM_EPS, ADAM_WD, ADAM_STEP = 0.001, 0.9, 0.999, 1e-08, 0.01, 10

N_CHIPS = 4
N_DEV = 8
LANES = 128
TM = 512
VMEM_MB = 56

MESH = pl.DeviceIdType.MESH
ANY = pl.BlockSpec(memory_space=pl.ANY)


def _cp(*sem, vmem=VMEM_MB):
    return pltpu.CompilerParams(dimension_semantics=sem, vmem_limit_bytes=vmem * 1024 * 1024)


def _rows(tile, cols):
    return pl.BlockSpec((tile, cols), lambda i: (i, 0))


def _const(shape):
    nd = len(shape)
    return pl.BlockSpec(shape, lambda i: (0,) * nd)


def _sds(shape, dtype):
    return jax.ShapeDtypeStruct(shape, dtype)


def _dot(a, b):
    return jnp.dot(a, b, preferred_element_type=F32)


def _dot_nt(a, b):
    return lax.dot_general(a, b, (((1,), (1,)), ((), ())), preferred_element_type=F32)


def _dot_tn(a, b):
    return lax.dot_general(a, b, (((0,), (0,)), ((), ())), preferred_element_type=F32)


def _rms(x):
    return lax.rsqrt(jnp.mean(x * x, axis=-1, keepdims=True) + EPS)


def _norm_bwd(x, g, dout):
    r = _rms(x)
    n = x * r
    dn = dout * g
    dx = r * (dn - n * jnp.mean(dn * n, axis=-1, keepdims=True))
    return dx, jnp.sum(dout * n, axis=0, keepdims=True)


def _rot_fwd(t, c, a, bt):
    return t * c + pltpu.roll(t, LANES - 8, 1) * a + pltpu.roll(t, 8, 1) * bt


def _rot_bwd(d, c, a, bt):
    return d * c + pltpu.roll(d * a, 8, 1) + pltpu.roll(d * bt, LANES - 8, 1)


def _rope_tables(seq):
    pos = jnp.arange(seq, dtype=F32)
    inv_freq = ROPE_THETA ** (-jnp.arange(0, ROT_DIM, 2, dtype=F32) / ROT_DIM)
    ang = pos[:, None] * inv_freq[None, :]
    cos, sin = jnp.cos(ang), jnp.sin(ang)
    ones = jnp.ones((seq, HEAD_DIM - ROT_DIM), F32)
    zeros8 = jnp.zeros((seq, 8), F32)
    zrest = jnp.zeros((seq, HEAD_DIM - ROT_DIM), F32)
    c = jnp.concatenate([cos, cos, ones], axis=1)
    a = jnp.concatenate([-sin, zeros8, zrest], axis=1)
    bt = jnp.concatenate([zeros8, sin, zrest], axis=1)
    return tuple(jnp.tile(t, (1, 2)) for t in (c, a, bt))


class _Exchange:
    def __init__(self, inputs, out_shapes, sems, start, finish, aliases=None, middle=None):
        self.inputs, self.out_shapes, self.sems = list(inputs), list(out_shapes), list(sems)
        self.start, self.finish, self.aliases = start, finish, dict(aliases or {})
        self.middle = middle


def _call(body, *, name, grid, in_specs, out_specs, out_shape, args, scratch=(), sem=(), exchanges=()):
    in_specs, out_specs, out_shape, scratch = list(in_specs), list(out_specs), list(out_shape), list(scratch)
    if not exchanges:
        return pl.pallas_call(body, name=name, grid=grid, in_specs=in_specs, out_specs=out_specs,
                              out_shape=out_shape, scratch_shapes=scratch, compiler_params=_cp(*sem))(*args)
    n_in, n_out, n_scr = len(in_specs), len(out_specs), len(scratch)
    x_in = [a for ex in exchanges for a in ex.inputs]
    x_out = [s for ex in exchanges for s in ex.out_shapes]
    x_sem = [s for ex in exchanges for s in ex.sems]
    aliases, i_off, o_off = {}, n_in, n_out
    for ex in exchanges:
        for i, o in ex.aliases.items():
            aliases[i_off + i] = o_off + o
        i_off += len(ex.inputs)
        o_off += len(ex.out_shapes)

    def split(flat):
        out, pos = [], 0
        for ex, n in zip(exchanges, flat[1]):
            out.append(flat[0][pos:pos + n])
            pos += n
        return out

    def carrier(*refs):
        pos = 0
        groups = []
        for n in (n_in, len(x_in), n_out, len(x_out), n_scr, len(x_sem)):
            groups.append(refs[pos:pos + n])
            pos += n
        ins, xin, outs, xout, scr, xsem = groups
        xin = split((xin, [len(ex.inputs) for ex in exchanges]))
        xout = split((xout, [len(ex.out_shapes) for ex in exchanges]))
        xsem = split((xsem, [len(ex.sems) for ex in exchanges]))
        first = pl.program_id(0) == 0
        last = pl.program_id(0) == grid[0] - 1
        for d in range(1, len(grid)):
            first = jnp.logical_and(first, pl.program_id(d) == 0)
            last = jnp.logical_and(last, pl.program_id(d) == grid[d] - 1)

        @pl.when(first)
        def _():
            for ex, i, o, s in zip(exchanges, xin, xout, xsem):
                ex.start(i, o, s)

        if any(ex.middle for ex in exchanges):
            half = pl.program_id(0) == 5 * grid[0] // 8
            for d in range(1, len(grid)):
                half = jnp.logical_and(half, pl.program_id(d) == 0)

            @pl.when(half)
            def _():
                for ex, i, o, s in zip(exchanges, xin, xout, xsem):
                    if ex.middle:
                        ex.middle(i, o, s)

        body(*ins, *outs, *scr)

        @pl.when(last)
        def _():
            for ex, i, o, s in zip(exchanges, xin, xout, xsem):
                ex.finish(i, o, s)

    res = pl.pallas_call(
        carrier, name=name, grid=grid, in_specs=in_specs + [ANY] * len(x_in),
        out_specs=out_specs + [ANY] * len(x_out), out_shape=out_shape + x_out,
        scratch_shapes=scratch + x_sem, input_output_aliases=aliases,
        compiler_params=_cp(*(["arbitrary"] * len(grid))),
    )(*args, *x_in)
    return res[:n_out], split((res[n_out:], [len(ex.out_shapes) for ex in exchanges]))


def _alone(name, *exchanges):
    n_in = [len(ex.inputs) for ex in exchanges]
    n_out = [len(ex.out_shapes) for ex in exchanges]
    n_sem = [len(ex.sems) for ex in exchanges]
    aliases, i_off, o_off = {}, 0, 0
    for ex in exchanges:
        for i, o in ex.aliases.items():
            aliases[i_off + i] = o_off + o
        i_off += len(ex.inputs)
        o_off += len(ex.out_shapes)

    def split(flat, counts):
        out, pos = [], 0
        for n in counts:
            out.append(flat[pos:pos + n])
            pos += n
        return out

    def body(*refs):
        ins, outs, sems = split(refs, [sum(n_in), sum(n_out), sum(n_sem)])
        groups = list(zip(exchanges, split(ins, n_in), split(outs, n_out), split(sems, n_sem)))
        for ex, i, o, s in groups:
            ex.start(i, o, s)
        for ex, i, o, s in groups:
            if ex.middle:
                ex.middle(i, o, s)
        for ex, i, o, s in groups:
            ex.finish(i, o, s)

    res = pl.pallas_call(
        body, name=name, in_specs=[ANY] * sum(n_in), out_specs=[ANY] * sum(n_out),
        out_shape=[s for ex in exchanges for s in ex.out_shapes],
        scratch_shapes=[s for ex in exchanges for s in ex.sems], input_output_aliases=aliases,
    )(*[a for ex in exchanges for a in ex.inputs])
    return split(res, n_out)


def _place():
    x, y, c = lax.axis_index("x"), lax.axis_index("y"), lax.axis_index("c")
    chips = [(1 - x, y), (x, 1 - y), (1 - x, 1 - y)]
    return x, y, c, chips


def _remote(src, dst, send, recv, to):
    return pltpu.make_async_remote_copy(src_ref=src, dst_ref=dst, send_sem=send, recv_sem=recv,
                                        device_id=to, device_id_type=MESH)


def _ex_gather(shards):
    nw = len(shards)
    hrs = [s.shape[0] // 2 for s in shards]

    def copies(ins, outs, sems):
        s0, r0, s1, r1, s2, r2, fs, fr = sems
        x, y, c, _ = _place()
        me, xn, yn, dg = (x, y), (1 - x, y), (x, 1 - y), (1 - x, 1 - y)
        nbr = (xn, yn)
        sibling = (x, y, 1 - c)

        def piece(w, chip, core, part=None):
            hr = hrs[w]
            rows = pl.ds(core * hr, hr) if part is None else pl.ds(core * hr + part * (hr // 2), hr // 2)
            return outs[w].at[2 * chip[0] + chip[1], rows]

        def first(w, k, lead):
            part = k if lead else 1 - k
            send, recv = (s0, r0) if lead else (s1, r1)
            rows = pl.ds(c * hrs[w] + part * (hrs[w] // 2), hrs[w] // 2)
            return _remote(ins[w].at[rows], piece(w, me, c, part), send.at[w, k], recv.at[w, k], (*nbr[k], c))

        def landed(w, k, lead):
            part = k if lead else 1 - k
            send, recv = (s0, r0) if lead else (s1, r1)
            return _remote(piece(w, nbr[k], c, part), piece(w, nbr[k], c, part), send.at[w, k], recv.at[w, k],
                           (*nbr[k], c))

        def onward(w, k):
            return _remote(piece(w, nbr[k], c, k), piece(w, nbr[k], c, k), s2.at[w, k], r2.at[w, k],
                           (*nbr[1 - k], c))

        def arrived(w, k):
            return _remote(piece(w, dg, c, k), piece(w, dg, c, k), s2.at[w, k], r2.at[w, k], (*nbr[1 - k], c))

        def passed(w, j):
            chip = (xn, yn, dg)[j]
            return _remote(piece(w, chip, c), piece(w, chip, c), fs.at[w, j], fr.at[w, j], sibling)

        def handed(w, j):
            chip = (xn, yn, dg)[j]
            return _remote(piece(w, chip, 1 - c), piece(w, chip, 1 - c), fs.at[w, j], fr.at[w, j], sibling)

        return first, landed, onward, arrived, passed, handed

    def start(ins, outs, sems):
        first = copies(ins, outs, sems)[0]
        for lead in (True, False):
            for w in range(nw):
                for k in range(2):
                    first(w, k, lead).start()

    def middle(ins, outs, sems):
        _, landed, onward, _, passed, _ = copies(ins, outs, sems)
        for w in range(nw):
            for k in range(2):
                landed(w, k, True).wait_recv()
                onward(w, k).start()
        for w in range(nw):
            for k in range(2):
                landed(w, k, False).wait_recv()
                passed(w, k).start()

    def finish(ins, outs, sems):
        first, _, onward, arrived, passed, handed = copies(ins, outs, sems)
        for w in range(nw):
            for k in range(2):
                arrived(w, k).wait_recv()
            passed(w, 2).start()
        for w in range(nw):
            for j in range(3):
                handed(w, j).wait_recv()
        for w in range(nw):
            for k in range(2):
                first(w, k, True).wait_send()
                first(w, k, False).wait_send()
                onward(w, k).wait_send()
            for j in range(3):
                passed(w, j).wait_send()

    return _Exchange(shards, [_sds((N_CHIPS,) + s.shape, s.dtype) for s in shards],
                     [pltpu.SemaphoreType.DMA((nw, 2))] * 6 + [pltpu.SemaphoreType.DMA((nw, 3))] * 2,
                     start, finish, middle=middle)


def _ex_pair(grads):
    nw = len(grads)

    def copies(ins, outs, sems):
        x, y, c, _ = _place()
        out = []
        for w in range(nw):
            hr = grads[w].shape[1] // 2
            out.append(_remote(ins[w].at[:, pl.ds((1 - c) * hr, hr)], outs[w], sems[0].at[w], sems[1].at[w],
                               (x, y, 1 - c)))
        return out

    def start(ins, outs, sems):
        for cp in copies(ins, outs, sems):
            cp.start()

    def finish(ins, outs, sems):
        for cp in copies(ins, outs, sems):
            cp.wait()

    return _Exchange(grads, [_sds((N_CHIPS, g.shape[1] // 2, g.shape[2]), F32) for g in grads],
                     [pltpu.SemaphoreType.DMA((nw,))] * 2, start, finish)


def _ex_chip(pieces):
    nw = len(pieces)

    def copies(ins, outs, sems):
        x, y, c, chips = _place()
        return [_remote(ins[w].at[2 * cx + cy], outs[w].at[k], sems[0].at[w, k], sems[1].at[w, k], (cx, cy, c))
                for w in range(nw) for k, (cx, cy) in enumerate(chips)]

    def start(ins, outs, sems):
        for cp in copies(ins, outs, sems):
            cp.start()

    def finish(ins, outs, sems):
        for cp in copies(ins, outs, sems):
            cp.wait()

    return _Exchange(pieces, [_sds((3,) + p.shape[1:], BF16) for p in pieces],
                     [pltpu.SemaphoreType.DMA((nw, 3))] * 2, start, finish)


def _ex_swap(fulls):
    nw = len(fulls)

    def start(ins, outs, sems):
        x, y, c, _ = _place()
        for w in range(nw):
            hr = fulls[w].shape[0] // 2
            mine = pl.ds(c * hr, hr)
            _remote(ins[w].at[mine], outs[w].at[mine], sems[0].at[w], sems[1].at[w], (x, y, 1 - c)).start()

    def finish(ins, outs, sems):
        x, y, c, _ = _place()
        for w in range(nw):
            hr = fulls[w].shape[0] // 2
            mine, theirs = pl.ds(c * hr, hr), pl.ds((1 - c) * hr, hr)
            _remote(ins[w].at[mine], outs[w].at[mine], sems[0].at[w], sems[1].at[w], (x, y, 1 - c)).wait_send()
            _remote(ins[w].at[theirs], outs[w].at[theirs], sems[0].at[w], sems[1].at[w], (x, y, 1 - c)).wait_recv()

    return _Exchange(fulls, [_sds(f.shape, F32) for f in fulls], [pltpu.SemaphoreType.DMA((nw,))] * 2,
                     start, finish, aliases={w: w for w in range(nw)})


def _ex_allgather(blocks):
    nb = len(blocks)

    def copies(ins, outs, sems):
        send, recv, lsem = sems
        x, y, c, chips = _place()
        me, sibling = (x, y, c), (x, y, 1 - c)

        def rows(b, px, py, pc):
            m_per = blocks[b].shape[0]
            return outs[b].at[pl.ds((4 * px + 2 * py + pc) * m_per, m_per), :]

        def copy(b, k, blk, to, src=None):
            return _remote(rows(b, *blk) if src is None else src, rows(b, *blk), send.at[b, k], recv.at[b, k], to)

        def mine(b):
            return pltpu.make_async_copy(ins[b], rows(b, *me), lsem.at[b])

        def first(b, k):
            return copy(b, k, me, sibling if k == 0 else (*chips[k - 1], c), src=ins[b])

        def passed(b, j):
            return copy(b, 4 + j, (*chips[j], c), sibling)

        def landed(b, j):
            return copy(b, 1 + j, (*chips[j], c), me)

        def handed(b, k):
            return copy(b, 0, sibling, me) if k == 0 else copy(b, 3 + k, (*chips[k - 1], 1 - c), me)

        return mine, first, passed, landed, handed

    def start(ins, outs, sems):
        mine, first, _, _, _ = copies(ins, outs, sems)
        for b in range(nb):
            mine(b).start()
            for k in range(4):
                first(b, k).start()

    def finish(ins, outs, sems):
        mine, first, passed, landed, handed = copies(ins, outs, sems)
        sent = []
        for b in range(nb):
            for j in range(3):
                landed(b, j).wait_recv()
                cp = passed(b, j)
                cp.start()
                sent.append(cp)
        for b in range(nb):
            for k in range(4):
                handed(b, k).wait_recv()
            for k in range(4):
                first(b, k).wait_send()
        for cp in sent:
            cp.wait_send()
        for b in range(nb):
            mine(b).wait()

    return _Exchange(blocks, [_sds((N_DEV * b.shape[0], b.shape[1]), F32) for b in blocks],
                     [pltpu.SemaphoreType.DMA((nb, 7)), pltpu.SemaphoreType.DMA((nb, 7)), pltpu.SemaphoreType.DMA((nb,))],
                     start, finish)


def _cast_shards(w_up, w_down, w_bp, w_ba, w_out, exchanges=()):
    r, c = w_up.shape
    tr = HALF // 2
    steps = r // tr

    def body(up_ref, down_ref, bp_ref, ba_ref, out_ref, ua_ref, ub_ref, da_ref, db_ref, bpo_ref, bao_ref, outo_ref):
        i = pl.program_id(0)

        @pl.when(i == 0)
        def _():
            for src, dst in ((bp_ref, bpo_ref), (ba_ref, bao_ref), (out_ref, outo_ref)):
                dst[...] = src[...].astype(BF16)

        @pl.when(i < steps // 2)
        def _():
            ua_ref[...] = up_ref[...].astype(BF16)
            da_ref[...] = down_ref[...].astype(BF16)

        @pl.when(i >= steps // 2)
        def _():
            ub_ref[...] = up_ref[...].astype(BF16)
            db_ref[...] = down_ref[...].astype(BF16)

    rows = _rows(tr, c)
    first = pl.BlockSpec((tr, c), lambda i: (jnp.minimum(i, steps // 2 - 1), 0))
    second = pl.BlockSpec((tr, c), lambda i: (jnp.maximum(i - steps // 2, 0), 0))
    half = _sds((HALF, c), BF16)
    return _call(
        body, name="cast_shards", grid=(steps,),
        in_specs=[rows, rows, _const(w_bp.shape), _const(w_ba.shape), _const(w_out.shape)],
        out_specs=[first, second, first, second, _const(w_bp.shape), _const(w_ba.shape), _const(w_out.shape)],
        out_shape=[half, half, half, half, _sds(w_bp.shape, BF16), _sds(w_ba.shape, BF16), _sds(w_out.shape, BF16)],
        args=(w_up, w_down, w_bp, w_ba, w_out), sem=("arbitrary",), exchanges=exchanges)


def _inproj(x2, g1, w_in_t, b_in, tabs, seq, exchanges=()):
    T = x2.shape[0]
    tm = min(TM, seq)
    nseq = seq // tm

    def body(x_ref, g_ref, w_ref, b_ref, c_ref, a_ref, bt_ref, h_ref, u_ref, q_ref, k_ref, v_ref, gate_ref):
        x = x_ref[...]
        h = (x * _rms(x) * g_ref[...]).astype(BF16)
        h_ref[...] = h

        def proj(lo, hi):
            return _dot_nt(h, w_ref[lo:hi, :]) + b_ref[:, lo:hi]

        c, a, bt = c_ref[...], a_ref[...], bt_ref[...]
        u_ref[...] = proj(0, C_Q)
        q = proj(C_Q, C_K)
        for p in range(4):
            sl = slice(LANES * p, LANES * (p + 1))
            q_ref[:, sl] = (_rot_fwd(q[:, sl], c, a, bt) * SCALE).astype(BF16)
        kv = proj(C_K, C_G)
        k_ref[...] = _rot_fwd(kv[:, :KV_WIDTH], c, a, bt).astype(BF16)
        v_ref[...] = kv[:, KV_WIDTH:].astype(BF16)
        for j in range(2):
            lo = C_G + D_MODEL * j
            gate_ref[:, D_MODEL * j:D_MODEL * (j + 1)] = jax.nn.sigmoid(proj(lo, lo + D_MODEL)).astype(BF16)

    tab = pl.BlockSpec((tm, LANES), lambda i: (i % nseq, 0))
    return _call(
        body, name="inproj", grid=(T // tm,),
        in_specs=[_rows(tm, D_MODEL), _const((1, D_MODEL)), _const((IN_WIDTH, D_MODEL)), _const((1, IN_WIDTH)),
                  tab, tab, tab],
        out_specs=[_rows(tm, D_MODEL), _rows(tm, POOL_WIDTH), _rows(tm, ATTN_WIDTH), _rows(tm, KV_WIDTH),
                   _rows(tm, KV_WIDTH), _rows(tm, GATE_WIDTH)],
        out_shape=[_sds((T, D_MODEL), BF16), _sds((T, POOL_WIDTH), F32), _sds((T, ATTN_WIDTH), BF16),
                   _sds((T, KV_WIDTH), BF16), _sds((T, KV_WIDTH), BF16), _sds((T, GATE_WIDTH), BF16)],
        args=(x2, g1, w_in_t, b_in, *tabs), sem=("parallel",), exchanges=exchanges)


def _inv_count(pos, w):
    return 1.0 / jnp.minimum(pos + 1, w).astype(F32)


def _pool_tile(i, tp, nseq, u_ref, prev_ref, w_ref, s_ref, diff_ref, y_ref):
    first = (i % nseq) == 0
    prev = jnp.where(first, 0.0, prev_ref[...])
    ext = jnp.concatenate([prev, u_ref[...]], axis=0)
    pos = (i % nseq) * tp + lax.broadcasted_iota(jnp.int32, (tp, 1), 0)
    for gi, w in enumerate(POOL_WINDOWS):
        sl = slice(POOL_GC * gi, POOL_GC * (gi + 1))
        xg = ext[:, sl]
        s = xg
        sh = 1
        while sh < w:
            s = s + pltpu.roll(s, sh, 0)
            sh *= 2
        pooled = s[HALO:] * _inv_count(pos, w)
        diff = (pooled - xg[HALO:]).astype(BF16)
        diff_ref[:, sl] = diff
        mixed = _dot(diff, w_ref[gi].astype(BF16))
        y_ref[:, sl] = (mixed * s_ref[:, sl]).astype(BF16)


def _pool_specs(tp):
    per = tp // HALO
    return [_rows(tp, POOL_WIDTH), pl.BlockSpec((HALO, POOL_WIDTH), lambda i: (jnp.maximum(i * per - 1, 0), 0)),
            _const((4, POOL_GC, POOL_GC)), _const((1, POOL_WIDTH))]


GROUP = 4
GROWS = GROUP * BLOCK


def _attn_masks(n):
    qi = lax.broadcasted_iota(jnp.int32, (GROWS, 2 * BLOCK), 0) % BLOCK
    kj = lax.broadcasted_iota(jnp.int32, (GROWS, 2 * BLOCK), 1)
    rel = qi + BLOCK - kj
    valid = (rel >= 0) & (rel < BLOCK) & (kj >= jnp.where(n > 0, 0, BLOCK))
    lo = lax.broadcasted_iota(jnp.int32, (BLOCK, LANES), 1) < HEAD_DIM
    return valid, lo


def _by_example(bl, *arrays):
    return [a.reshape(bl, a.shape[0] // bl, a.shape[1]) for a in arrays]


def _stack_heads(ref, h, lo):
    keep = lo if h == 0 else jnp.logical_not(lo)
    pieces = []
    for p in (2 * h, 2 * h + 1):
        xp = ref[:, LANES * p:LANES * (p + 1)].astype(F32)
        for e in range(2):
            t = xp if e == h else pltpu.roll(xp, HEAD_DIM, 1)
            pieces.append(jnp.where(keep, t, 0.0).astype(BF16))
    return jnp.concatenate(pieces, axis=0)


def _unstack_heads(stacked, h, lo):
    pairs = []
    for j in range(2):
        parts = []
        for e in range(2):
            t = stacked[BLOCK * (2 * j + e):BLOCK * (2 * j + e + 1)]
            parts.append(t if e == h else pltpu.roll(t, HEAD_DIM, 1))
        pairs.append(jnp.where(lo, parts[0], parts[1]))
    return pairs


def _sink_rows(sink_ref, h):
    head = lax.broadcasted_iota(jnp.int32, (GROWS, 1), 0) // BLOCK
    col = jnp.zeros((GROWS, 1), F32) + sink_ref[GROUP * h]
    for g in range(1, GROUP):
        col = jnp.where(head == g, sink_ref[GROUP * h + g], col)
    return col


def _group_probs(qs, kk, valid, sink):
    s = jnp.where(valid, _dot_nt(qs, kk), NEG_INF)
    m = jnp.maximum(jnp.max(s, axis=1, keepdims=True), sink)
    ex = jnp.exp(s - m)
    es = jnp.exp(sink - m)
    inv = 1.0 / (jnp.sum(ex, axis=1, keepdims=True) + es)
    return ex * inv, es * inv


def _mixers_fwd(q, k, v, sinks, u, w_pool, pool_scale, seq, exchanges=()):
    T = q.shape[0]
    nb = seq // BLOCK
    bl = T // seq
    tp = T // nb
    nseq = seq // tp

    def body(sink_ref, q_ref, kp_ref, kc_ref, vp_ref, vc_ref, u_ref, prev_ref, w_ref, s_ref, o_ref, diff_ref, y_ref):
        n = pl.program_id(0)
        valid, lo = _attn_masks(n)
        for b in range(bl):
            kk = jnp.concatenate([kp_ref[b], kc_ref[b]], axis=0)
            vv = jnp.concatenate([vp_ref[b], vc_ref[b]], axis=0)
            for h in range(2):
                qs = _stack_heads(q_ref.at[b], h, lo)
                pr, _ = _group_probs(qs, kk, valid, _sink_rows(sink_ref, h))
                o = _dot(pr.astype(BF16), vv)
                for j, pair in enumerate(_unstack_heads(o, h, lo)):
                    p = 2 * h + j
                    o_ref[b, :, LANES * p:LANES * (p + 1)] = pair.astype(BF16)
        _pool_tile(n, tp, nseq, u_ref, prev_ref, w_ref, s_ref, diff_ref, y_ref)

    cur = lambda n: (0, n, 0)
    prv = lambda n: (0, jnp.maximum(n - 1, 0), 0)
    kv = lambda m: pl.BlockSpec((bl, BLOCK, KV_WIDTH), m)
    res = _call(
        body, name="mixers_fwd", grid=(nb,),
        in_specs=[pl.BlockSpec(memory_space=pltpu.SMEM), pl.BlockSpec((bl, BLOCK, ATTN_WIDTH), cur),
                  kv(prv), kv(cur), kv(prv), kv(cur)] + _pool_specs(tp),
        out_specs=[pl.BlockSpec((bl, BLOCK, ATTN_WIDTH), cur), _rows(tp, POOL_WIDTH), _rows(tp, POOL_WIDTH)],
        out_shape=[_sds((bl, seq, ATTN_WIDTH), BF16), _sds((T, POOL_WIDTH), BF16), _sds((T, POOL_WIDTH), BF16)],
        args=(sinks, *_by_example(bl, q, k, k, v, v), u, u, w_pool, pool_scale), sem=("parallel",),
        exchanges=exchanges)
    outs, rest = res if exchanges else (res, None)
    return [outs[0].reshape(T, ATTN_WIDTH), outs[1], outs[2]], rest


def _branch(y, w_ref):
    return jnp.concatenate([_dot(y, w_ref[j]) for j in range(N_CHIPS)], axis=1)


def _merge_out(y_pool, y_attn, gate, x2, w_bp, w_ba, w_out, g2, g3, exchanges=()):
    T = x2.shape[0]
    tm = min(TM, T)

    def body(yp_ref, ya_ref, gate_ref, x_ref, wbp_ref, wba_ref, wo_ref, g2_ref, g3_ref,
             mg_ref, mix_ref, x1_ref, h2_ref):
        bp, ba = _branch(yp_ref[...], wbp_ref), _branch(ya_ref[...], wba_ref)
        merged = (gate_ref[:, :D_MODEL].astype(F32) * bp + gate_ref[:, D_MODEL:].astype(F32) * ba).astype(BF16)
        mg_ref[...] = merged
        mix = _dot(merged, wo_ref[...])
        mix_ref[...] = mix
        x1 = x_ref[...] + mix * _rms(mix) * g2_ref[...]
        x1_ref[...] = x1
        h2_ref[...] = (x1 * _rms(x1) * g3_ref[...]).astype(BF16)

    return _call(
        body, name="merge_out", grid=(T // tm,),
        in_specs=[_rows(tm, POOL_WIDTH), _rows(tm, ATTN_WIDTH), _rows(tm, GATE_WIDTH), _rows(tm, D_MODEL),
                  _const(w_bp.shape), _const(w_ba.shape), _const((D_MODEL, D_MODEL)),
                  _const((1, D_MODEL)), _const((1, D_MODEL))],
        out_specs=[_rows(tm, D_MODEL)] * 4,
        out_shape=[_sds((T, D_MODEL), BF16), _sds((T, D_MODEL), F32), _sds((T, D_MODEL), F32),
                   _sds((T, D_MODEL), BF16)],
        args=(y_pool, y_attn, gate, x2, w_bp, w_ba, w_out, g2, g3), sem=("parallel",), exchanges=exchanges)


HALF = D_MODEL // 2
TM_MLP = 256


def _mlp_core(h2, x1, mix, tgt, w_up, w_down, g4, g3, g2):
    T = h2.shape[0]
    tm = min(TM_MLP, T)

    def body(h_ref, x1_ref, mix_ref, t_ref, g_ref, g3_ref, g2_ref, ua_hbm, ub_hbm, da_hbm, db_hbm,
             act_ref, dff_ref, dup_ref, dx1_ref, dmix_ref, loss_ref, dg_ref, dg3_ref, dg2_ref,
             wu, wd, relu_scr, sems):
        def weight_copy(i):
            src, dst = ((ua_hbm, wu.at[:, :HALF]), (ub_hbm, wu.at[:, HALF:]),
                        (da_hbm, wd.at[:, :HALF]), (db_hbm, wd.at[:, HALF:]))[i]
            return pltpu.make_async_copy(src, dst, sems.at[i])

        @pl.when(pl.program_id(0) == 0)
        def _():
            for i in range(4):
                weight_copy(i).start()
            loss_ref[...] = jnp.zeros_like(loss_ref)
            for ref in (dg_ref, dg3_ref, dg2_ref):
                ref[...] = jnp.zeros_like(ref)
            weight_copy(0).wait()
            weight_copy(1).wait()

        h = h_ref[...]
        ff = None
        for j in range(N_CHIPS):
            lo = D_MODEL * j
            relu = jnp.maximum(_dot(h, wu[j]), 0.0)
            if j == 0:
                @pl.when(pl.program_id(0) == 0)
                def _():
                    weight_copy(2).wait()
                    weight_copy(3).wait()
            relu_scr[:, lo:lo + D_MODEL] = relu
            act = jnp.square(relu).astype(BF16)
            act_ref[:, lo:lo + D_MODEL] = act
            t = _dot(act, wd[j])
            ff = t if ff is None else ff + t
        g = g_ref[...]
        x1 = x1_ref[...]
        err = x1 + ff * _rms(ff) * g - t_ref[...]
        loss_ref[...] += jnp.sum(err * err) * (0.5 / D_MODEL)
        dy = err * (1.0 / D_MODEL)
        dff, dg = _norm_bwd(ff, g, dy)
        dg_ref[...] += dg
        dff = dff.astype(BF16)
        dff_ref[...] = dff
        dh2 = None
        for j in range(N_CHIPS):
            lo = D_MODEL * j
            dup = (_dot_nt(dff, wd[j]) * (2.0 * relu_scr[:, lo:lo + D_MODEL])).astype(BF16)
            dup_ref[:, lo:lo + D_MODEL] = dup
            t = _dot_nt(dup, wu[j])
            dh2 = t if dh2 is None else dh2 + t
        dx, dg3 = _norm_bwd(x1, g3_ref[...], dh2)
        dx1 = dy + dx
        dx1_ref[...] = dx1
        dg3_ref[...] += dg3
        dmix, dg2 = _norm_bwd(mix_ref[...], g2_ref[...], dx1)
        dmix_ref[...] = dmix.astype(BF16)
        dg2_ref[...] += dg2

    slabs = pltpu.VMEM((N_CHIPS, D_MODEL, D_MODEL), BF16)
    gain = _const((1, D_MODEL))
    return pl.pallas_call(
        body, name="mlp_core", grid=(T // tm,),
        in_specs=[_rows(tm, D_MODEL)] * 4 + [gain] * 3 + [ANY] * 4,
        out_specs=[_rows(tm, D_FF), _rows(tm, D_MODEL), _rows(tm, D_FF), _rows(tm, D_MODEL), _rows(tm, D_MODEL),
                   _const((8, LANES)), gain, gain, gain],
        out_shape=[_sds((T, D_FF), BF16), _sds((T, D_MODEL), BF16), _sds((T, D_FF), BF16), _sds((T, D_MODEL), F32),
                   _sds((T, D_MODEL), BF16), _sds((8, LANES), F32)] + [_sds((1, D_MODEL), F32)] * 3,
        scratch_shapes=[slabs] * 2 + [pltpu.VMEM((tm, D_FF), F32), pltpu.SemaphoreType.DMA((4,))],
        compiler_params=_cp("arbitrary"),
    )(h2, x1, mix, tgt, g4, g3, g2, *w_up, *w_down)


def _dw(tag, a, g, ta, tn, tk=2 * TM, shard_cols=False, exchanges=()):
    T, ka = a.shape
    n = g.shape[1]
    tk = min(tk, T)
    nk = T // tk

    def body(a_ref, g_ref, o_ref):
        @pl.when(pl.program_id(2) == 0)
        def _():
            o_ref[...] = jnp.zeros_like(o_ref)

        o_ref[...] += _dot_tn(a_ref[...], g_ref[...])

    if shard_cols:
        per = (n // N_CHIPS) // tn
        out_spec = pl.BlockSpec((None, ta, tn), lambda i, j, k: (j // per, i, j % per))
        out_shape = _sds((N_CHIPS, ka, n // N_CHIPS), F32)
    else:
        out_spec = pl.BlockSpec((ta, tn), lambda i, j, k: (i, j))
        out_shape = _sds((ka, n), F32)
    return _call(
        body, name="dw_" + tag, grid=(ka // ta, n // tn, nk),
        in_specs=[pl.BlockSpec((tk, ta), lambda i, j, k: (k, i)), pl.BlockSpec((tk, tn), lambda i, j, k: (k, j))],
        out_specs=[out_spec], out_shape=[out_shape],
        args=(a, g), sem=("parallel", "parallel", "arbitrary"), exchanges=exchanges)


def _dw_mix(merged, dmix, y_pool, dbp, y_attn, dba, exchanges=()):
    T = merged.shape[0]
    tk = min(2 * TM, T)
    c = D_MODEL // N_CHIPS

    def body(mg_ref, dmix_ref, yp_ref, dbp_ref, ya_ref, dba_ref, out_ref, bp_ref, ba_ref):
        @pl.when(pl.program_id(0) == 0)
        def _():
            for ref in (out_ref, bp_ref, ba_ref):
                ref[...] = jnp.zeros_like(ref)

        out_ref[...] += _dot_tn(mg_ref[...], dmix_ref[...])
        for y_ref, d_ref, o_ref in ((yp_ref, dbp_ref, bp_ref), (ya_ref, dba_ref, ba_ref)):
            res = _dot_tn(y_ref[...], d_ref[...])
            for j in range(N_CHIPS):
                o_ref[j] += res[:, c * j:c * (j + 1)]

    slabs = (N_CHIPS, POOL_WIDTH, c)
    return _call(
        body, name="dw_mix", grid=(T // tk,),
        in_specs=[_rows(tk, D_MODEL), _rows(tk, D_MODEL), _rows(tk, POOL_WIDTH), _rows(tk, D_MODEL),
                  _rows(tk, ATTN_WIDTH), _rows(tk, D_MODEL)],
        out_specs=[_const((D_MODEL, D_MODEL)), _const(slabs), _const(slabs)],
        out_shape=[_sds((D_MODEL, D_MODEL), F32), _sds(slabs, F32), _sds(slabs, F32)],
        args=(merged, dmix, y_pool, dbp, y_attn, dba), sem=("arbitrary",), exchanges=exchanges)


def _merge_bwd(dmix, gate, y_pool, y_attn, w_out, w_bp, w_ba, exchanges=()):
    T = dmix.shape[0]
    tm = min(TM, T)

    def body(dmix_ref, gate_ref, yp_ref, ya_ref, wo_ref, wbp_ref, wba_ref,
             dbp_ref, dba_ref, dgate_ref, dyp_ref, dya_ref):
        dm = _dot_nt(dmix_ref[...], wo_ref[...])
        for j, (y_ref, db_ref, w_ref, dy_ref) in enumerate(
                ((yp_ref, dbp_ref, wbp_ref, dyp_ref), (ya_ref, dba_ref, wba_ref, dya_ref))):
            sl = slice(D_MODEL * j, D_MODEL * (j + 1))
            gt = gate_ref[:, sl].astype(F32)
            db = (dm * gt).astype(BF16)
            db_ref[...] = db
            dgate_ref[:, sl] = (dm * _branch(y_ref[...], w_ref) * gt * (1.0 - gt)).astype(BF16)
            cw = D_MODEL // N_CHIPS
            dy = _dot_nt(db[:, :cw], w_ref[0])
            for c in range(1, N_CHIPS):
                dy = dy + _dot_nt(db[:, cw * c:cw * (c + 1)], w_ref[c])
            dy_ref[...] = dy.astype(dy_ref.dtype)

    return _call(
        body, name="merge_bwd", grid=(T // tm,),
        in_specs=[_rows(tm, D_MODEL), _rows(tm, GATE_WIDTH), _rows(tm, POOL_WIDTH), _rows(tm, ATTN_WIDTH),
                  _const((D_MODEL, D_MODEL)), _const(w_bp.shape), _const(w_ba.shape)],
        out_specs=[_rows(tm, D_MODEL), _rows(tm, D_MODEL), _rows(tm, GATE_WIDTH), _rows(tm, POOL_WIDTH),
                   _rows(tm, ATTN_WIDTH)],
        out_shape=[_sds((T, D_MODEL), BF16), _sds((T, D_MODEL), BF16), _sds((T, GATE_WIDTH), BF16),
                   _sds((T, POOL_WIDTH), F32), _sds((T, ATTN_WIDTH), BF16)],
        args=(dmix, gate, y_pool, y_attn, w_out, w_bp, w_ba), sem=("parallel",), exchanges=exchanges)


def _mixers_bwd(q, k, v, do, sinks, tabs, dyp, diff, w_pool, pool_scale, seq, exchanges=()):
    T = q.shape[0]
    nb = seq // BLOCK
    bl = T // seq
    steps = nb + 1
    tp = T // nb
    nseq = seq // tp
    per = tp // HALO
    last_halo = T // HALO - 1

    def body(sink_ref, q_ref, do_ref, kp_ref, kc_ref, vp_ref, vc_ref, c_ref, a_ref, bt_ref, cp_ref, ap_ref, btp_ref,
             dy_ref, nxt_ref, diff_ref, w_ref, s_ref,
             dq_ref, dk_ref, dv_ref, dsink_ref, du_ref, dw_ref, ds_ref, ck_ref, cv_ref):
        n = pl.program_id(0)

        @pl.when(n == 0)
        def _():
            for ref in (dsink_ref, ck_ref, cv_ref, dw_ref, ds_ref):
                ref[...] = jnp.zeros_like(ref)

        @pl.when(n < nb)
        def _():
            _pool_bwd_tile(n, tp, nseq, dy_ref, nxt_ref, diff_ref, w_ref, s_ref, du_ref, dw_ref, ds_ref)
            valid, lo = _attn_masks(n)
            for b in range(bl):
                kk = jnp.concatenate([kp_ref[b], kc_ref[b]], axis=0)
                vv = jnp.concatenate([vp_ref[b], vc_ref[b]], axis=0)
                dk_acc = jnp.zeros((2 * BLOCK, KV_WIDTH), F32)
                dv_acc = jnp.zeros((2 * BLOCK, KV_WIDTH), F32)
                for h in range(2):
                    qs = _stack_heads(q_ref.at[b], h, lo)
                    dos = _stack_heads(do_ref.at[b], h, lo)
                    pr, ps = _group_probs(qs, kk, valid, _sink_rows(sink_ref, h))
                    dp = _dot_nt(dos, vv)
                    delta = jnp.sum(pr * dp, axis=1, keepdims=True)
                    ds = (pr * (dp - delta)).astype(BF16)
                    dsk = ps * delta
                    for g in range(GROUP):
                        idx = GROUP * h + g
                        dsink_ref[idx:idx + 1, :] += (jnp.zeros((1, LANES), F32)
                                                      - jnp.sum(dsk[BLOCK * g:BLOCK * (g + 1)]))
                    dk_acc = dk_acc + _dot_tn(ds, qs)
                    dv_acc = dv_acc + _dot_tn(pr.astype(BF16), dos)
                    for j, pair in enumerate(_unstack_heads(_dot(ds, kk) * SCALE, h, lo)):
                        sl = slice(LANES * (2 * h + j), LANES * (2 * h + j + 1))
                        dq_ref[b, :, sl] = _rot_bwd(pair, c_ref[...], a_ref[...], bt_ref[...]).astype(BF16)
                fin_k = ck_ref[b] + dk_acc[:BLOCK]
                dk_ref[b] = _rot_bwd(fin_k, cp_ref[...], ap_ref[...], btp_ref[...]).astype(BF16)
                dv_ref[b] = (cv_ref[b] + dv_acc[:BLOCK]).astype(BF16)
                ck_ref[b] = dk_acc[BLOCK:]
                cv_ref[b] = dv_acc[BLOCK:]

        @pl.when(n == nb)
        def _():
            for b in range(bl):
                dk_ref[b] = _rot_bwd(ck_ref[b], cp_ref[...], ap_ref[...], btp_ref[...]).astype(BF16)
                dv_ref[b] = cv_ref[b].astype(BF16)

    cur = lambda n: (0, jnp.minimum(n, nb - 1), 0)
    prv = lambda n: (0, jnp.clip(n - 1, 0, nb - 1), 0)
    tcur = lambda n: (jnp.minimum(n, nb - 1), 0)
    tprv = lambda n: (jnp.clip(n - 1, 0, nb - 1), 0)
    wide = lambda m: pl.BlockSpec((bl, BLOCK, ATTN_WIDTH), m)
    kv = lambda m: pl.BlockSpec((bl, BLOCK, KV_WIDTH), m)
    tab = lambda m: pl.BlockSpec((BLOCK, LANES), m)
    tile = lambda n: (jnp.minimum(n, nb - 1), 0)
    halo = lambda n: (jnp.minimum((jnp.minimum(n, nb - 1) + 1) * per, last_halo), 0)
    rows = pl.BlockSpec((tp, POOL_WIDTH), tile)
    res = _call(
        body, name="mixers_bwd", grid=(steps,),
        in_specs=[pl.BlockSpec(memory_space=pltpu.SMEM), wide(cur), wide(cur), kv(prv), kv(cur), kv(prv), kv(cur),
                  tab(tcur), tab(tcur), tab(tcur), tab(tprv), tab(tprv), tab(tprv),
                  rows, pl.BlockSpec((HALO, POOL_WIDTH), halo), rows, _const((4, POOL_GC, POOL_GC)),
                  _const((1, POOL_WIDTH))],
        out_specs=[wide(cur), kv(prv), kv(prv), _const((8, LANES)), rows, _const((4, POOL_GC, POOL_GC)),
                   _const((1, POOL_WIDTH))],
        out_shape=[_sds((bl, seq, ATTN_WIDTH), BF16), _sds((bl, seq, KV_WIDTH), BF16),
                   _sds((bl, seq, KV_WIDTH), BF16), _sds((8, LANES), F32), _sds((T, POOL_WIDTH), BF16),
                   _sds((4, POOL_GC, POOL_GC), F32), _sds((1, POOL_WIDTH), F32)],
        scratch=[pltpu.VMEM((bl, BLOCK, KV_WIDTH), F32), pltpu.VMEM((bl, BLOCK, KV_WIDTH), F32)],
        args=(sinks, *_by_example(bl, q, do, k, k, v, v), *tabs, *tabs, dyp, dyp, diff, w_pool, pool_scale),
        sem=("arbitrary",), exchanges=exchanges)
    outs, rest = (res if exchanges else (res, None))
    outs = [outs[0].reshape(T, ATTN_WIDTH), outs[1].reshape(T, KV_WIDTH), outs[2].reshape(T, KV_WIDTH), *outs[3:]]
    return (outs, rest) if exchanges else outs


def _pool_bwd_tile(i, tp, nseq, dy_ref, nxt_ref, diff_ref, w_ref, s_ref, du_ref, dw_ref, ds_ref):
    last = (i % nseq) == nseq - 1
    nxt = jnp.where(last, 0.0, nxt_ref[...])
    ext = jnp.concatenate([dy_ref[...], nxt], axis=0) * s_ref[...]
    pos = (i % nseq) * tp + lax.broadcasted_iota(jnp.int32, (tp + HALO, 1), 0)
    for gi, w in enumerate(POOL_WINDOWS):
        sl = slice(POOL_GC * gi, POOL_GC * (gi + 1))
        wg = w_ref[gi].astype(BF16)
        dmx = ext[:, sl].astype(BF16)
        ddiff = _dot_nt(dmx, wg)
        s = ddiff * _inv_count(pos, w)
        sh = 1
        while sh < w:
            s = s + pltpu.roll(s, tp + HALO - sh, 0)
            sh *= 2
        du_ref[:, sl] = (s[:tp] - ddiff[:tp]).astype(BF16)
        dg = diff_ref[:, sl]
        dw_ref[gi] += _dot_tn(dg, dmx[:tp])
        ds_ref[:, sl] += jnp.sum(dy_ref[:, sl] * _dot(dg, wg), axis=0, keepdims=True)


_PARTS = ((0, C_Q), (C_Q, C_K), (C_K, C_V), (C_V, C_G), (C_G, IN_WIDTH))


def _inproj_bwd(parts, x2, dx1, w_in_t, g1, exchanges=()):
    T = x2.shape[0]
    tm = min(TM, T)

    def body(du_ref, dq_ref, dk_ref, dv_ref, dgt_ref, x_ref, dx1_ref, w_ref, g_ref, gx_ref, dg_ref):
        @pl.when(pl.program_id(0) == 0)
        def _():
            dg_ref[...] = jnp.zeros_like(dg_ref)

        dh = jnp.zeros((tm, D_MODEL), F32)
        for (lo, hi), p_ref in zip(_PARTS, (du_ref, dq_ref, dk_ref, dv_ref, dgt_ref)):
            dh = dh + _dot(p_ref[...], w_ref[lo:hi, :])
        dx, dg = _norm_bwd(x_ref[...], g_ref[...], dh)
        gx_ref[...] = dx1_ref[...] + dx
        dg_ref[...] += dg

    return _call(
        body, name="inproj_bwd", grid=(T // tm,),
        in_specs=[_rows(tm, hi - lo) for lo, hi in _PARTS]
        + [_rows(tm, D_MODEL), _rows(tm, D_MODEL), _const((IN_WIDTH, D_MODEL)), _const((1, D_MODEL))],
        out_specs=[_rows(tm, D_MODEL), _const((1, D_MODEL))],
        out_shape=[_sds((T, D_MODEL), F32), _sds((1, D_MODEL), F32)],
        args=(*parts, x2, dx1, w_in_t, g1), sem=("arbitrary",), exchanges=exchanges)


def _dw_in(h, parts, exchanges=()):
    T = h.shape[0]
    tk = min(TM, T)

    def body(h_ref, du_ref, dq_ref, dk_ref, dv_ref, dgt_ref, o_ref, db_ref):
        @pl.when(pl.program_id(0) == 0)
        def _():
            o_ref[...] = jnp.zeros_like(o_ref)
            db_ref[...] = jnp.zeros_like(db_ref)

        hh = h_ref[...]
        ones = jnp.ones((8, tk), BF16)
        for (lo, hi), p_ref in zip(_PARTS, (du_ref, dq_ref, dk_ref, dv_ref, dgt_ref)):
            part = p_ref[...]
            o_ref[lo:hi, :] += _dot_tn(part, hh)
            db_ref[:, lo:hi] += _dot(ones, part)[:1]

    return _call(
        body, name="dw_in", grid=(T // tk,),
        in_specs=[_rows(tk, D_MODEL)] + [_rows(tk, hi - lo) for lo, hi in _PARTS],
        out_specs=[_const((IN_WIDTH, D_MODEL)), _const((1, IN_WIDTH))],
        out_shape=[_sds((IN_WIDTH, D_MODEL), F32), _sds((1, IN_WIDTH), F32)],
        args=(h, *parts), sem=("arbitrary",), exchanges=exchanges)


def _row_tile(rows, cap=256, mult=16):
    best = None
    for t in range(mult, min(rows, cap) + 1, mult):
        if rows % t == 0:
            best = t
    if best is None:
        raise ValueError("no row tile for %d rows" % rows)
    return best


def _pair_sum(ids, full, got):
    _, r, c = full.shape
    hr = r // 2
    tr = _row_tile(hr, cap=512)
    nblk = hr // tr

    def body(ids_ref, a_ref, b_ref, own_ref, sb_ref):
        s = a_ref[...] + b_ref[...]
        sb_ref[...] = s.astype(BF16)

        @pl.when(pl.program_id(1) == ids_ref[0])
        def _():
            own_ref[...] = s

    slab = pl.BlockSpec((None, tr, c), lambda i, j, ids_ref: (j, i, 0))
    return pl.pallas_call(
        body, name="pair_sum_%dx%d" % (r, c),
        grid_spec=pltpu.PrefetchScalarGridSpec(
            num_scalar_prefetch=1, grid=(nblk, N_CHIPS),
            in_specs=[pl.BlockSpec((None, tr, c), lambda i, j, ids_ref: (j, ids_ref[1] * nblk + i, 0)), slab],
            out_specs=[pl.BlockSpec((tr, c), lambda i, j, ids_ref: (i, 0)), slab]),
        out_shape=[_sds((hr, c), F32), _sds((N_CHIPS, hr, c), BF16)],
        compiler_params=_cp("parallel", "arbitrary"),
    )(ids, full, got)


def _pair_sum_small(ids, fulls, gots):
    n = len(fulls)
    dims = [(f.shape[1] // 2, f.shape[2]) for f in fulls]

    def body(ids_ref, *refs):
        ins, outs = refs[:2 * n], refs[2 * n:]
        for k in range(n):
            s = ins[2 * k][...] + ins[2 * k + 1][...]
            outs[2 * k + 1][...] = s.astype(BF16)

            @pl.when(pl.program_id(0) == ids_ref[0])
            def _(k=k, s=s):
                outs[2 * k][...] = s

    in_specs, out_specs, out_shape = [], [], []
    for hr, c in dims:
        slab = pl.BlockSpec((None, hr, c), lambda j, ids_ref: (j, 0, 0))
        in_specs += [pl.BlockSpec((None, hr, c), lambda j, ids_ref: (j, ids_ref[1], 0)), slab]
        out_specs += [pl.BlockSpec((hr, c), lambda j, ids_ref: (0, 0)), slab]
        out_shape += [_sds((hr, c), F32), _sds((N_CHIPS, hr, c), BF16)]
    res = pl.pallas_call(
        body, name="pair_sum_small",
        grid_spec=pltpu.PrefetchScalarGridSpec(num_scalar_prefetch=1, grid=(N_CHIPS,), in_specs=in_specs,
                                               out_specs=out_specs),
        out_shape=out_shape, compiler_params=_cp("arbitrary"),
    )(ids, *[a for pair in zip(fulls, gots) for a in pair])
    return [(res[2 * k], res[2 * k + 1]) for k in range(n)]


def _chip_sum_small(ids, owns, gots):
    n = len(owns)

    def body(ids_ref, *refs):
        ins, outs = refs[:2 * n], refs[2 * n:]
        for k in range(n):
            a, b = ins[2 * k], ins[2 * k + 1]
            outs[k][...] = ((a[...] + b[0].astype(F32)) + b[1].astype(F32)) + b[2].astype(F32)

    in_specs, out_specs, out_shape = [], [], []
    for own in owns:
        hr, c = own.shape
        in_specs += [pl.BlockSpec((hr, c), lambda i, ids_ref: (0, 0)),
                     pl.BlockSpec((3, hr, c), lambda i, ids_ref: (0, 0, 0))]
        out_specs.append(pl.BlockSpec((hr, c), lambda i, ids_ref: (ids_ref[1], 0)))
        out_shape.append(_sds((2 * hr, c), F32))
    return pl.pallas_call(
        body, name="chip_sum_small",
        grid_spec=pltpu.PrefetchScalarGridSpec(num_scalar_prefetch=1, grid=(1,), in_specs=in_specs,
                                               out_specs=out_specs),
        out_shape=out_shape, compiler_params=_cp("arbitrary"),
    )(ids, *[a for pair in zip(owns, gots) for a in pair])


def _chip_sum(ids, own, got):
    hr, c = own.shape
    tr = _row_tile(hr)
    nblk = hr // tr

    def body(ids_ref, a_ref, b_ref, o_ref):
        o_ref[...] = ((a_ref[...] + b_ref[0].astype(F32)) + b_ref[1].astype(F32)) + b_ref[2].astype(F32)

    return pl.pallas_call(
        body, name="chip_sum_%dx%d" % (hr, c),
        grid_spec=pltpu.PrefetchScalarGridSpec(
            num_scalar_prefetch=1, grid=(nblk,),
            in_specs=[pl.BlockSpec((tr, c), lambda i, ids_ref: (i, 0)),
                      pl.BlockSpec((3, tr, c), lambda i, ids_ref: (0, i, 0))],
            out_specs=pl.BlockSpec((tr, c), lambda i, ids_ref: (ids_ref[1] * nblk + i, 0))),
        out_shape=_sds((2 * hr, c), F32),
        compiler_params=_cp("parallel"),
    )(ids, own, got)


def _adamw_math(w, g, m, v):
    nm = ADAM_B1 * m + (1.0 - ADAM_B1) * g
    nv = ADAM_B2 * v + (1.0 - ADAM_B2) * (g * g)
    m_hat = nm / (1.0 - ADAM_B1 ** ADAM_STEP)
    v_hat = nv / (1.0 - ADAM_B2 ** ADAM_STEP)
    return -ADAM_LR * (m_hat / (jnp.sqrt(v_hat) + ADAM_EPS) + ADAM_WD * w), nm, nv


def _adamw(w, g, m, v):
    r, c = w.shape
    tr = _row_tile(r, cap=512, mult=8)

    def body(w_ref, g_ref, m_ref, v_ref, d_ref, nm_ref, nv_ref):
        d_ref[...], nm_ref[...], nv_ref[...] = _adamw_math(w_ref[...], g_ref[...], m_ref[...], v_ref[...])

    spec = _rows(tr, c)
    return pl.pallas_call(
        body, name="adamw_%dx%d" % (r, c), grid=(r // tr,),
        in_specs=[spec] * 4, out_specs=[spec] * 3, out_shape=[_sds((r, c), F32)] * 3,
        compiler_params=_cp("parallel"),
    )(w, g, m, v)


SC_TILES = 32
SC_LANES = 16
SC_ROWS = 8


def _adamw_sparse(w, g, m, v):
    r, c = w.shape
    rows = r // SC_TILES
    step = min(rows, SC_ROWS)

    def body(w_hbm, g_hbm, m_hbm, v_hbm, d_hbm, nm_hbm, nv_hbm, wb, gb, mb, vb):
        tile = lax.axis_index("sc_subcore") * 2 + lax.axis_index("sc_core")

        @pl.loop(0, rows, step=step)
        def _(r0):
            mine = pl.ds(tile * rows + r0, step)
            for src, dst in ((w_hbm, wb), (g_hbm, gb), (m_hbm, mb), (v_hbm, vb)):
                pltpu.sync_copy(src.at[mine], dst)

            @pl.loop(0, step)
            def _(row):
                @pl.loop(0, c, step=SC_LANES)
                def _(i):
                    at = (row, pl.ds(i, SC_LANES))
                    wb[at], mb[at], vb[at] = _adamw_math(wb[at], gb[at], mb[at], vb[at])

            for src, dst in ((wb, d_hbm), (mb, nm_hbm), (vb, nv_hbm)):
                pltpu.sync_copy(src, dst.at[mine])

    return pl.kernel(
        body, name="adamw_sparse_%dx%d" % (r, c), out_type=[_sds((r, c), F32)] * 3,
        mesh=plsc.VectorSubcoreMesh(core_axis_name="sc_core", subcore_axis_name="sc_subcore"),
        scratch_types=[pltpu.VMEM((step, c), F32)] * 4,
    )(w, g, m, v)


_SMALL_NAMES = ("w_pool", "b_in", "g_mix_pre", "g_mix_post", "g_mlp_pre", "g_mlp_post", "pool_scale", "attn_sinks")
B_ROWS = -(-IN_WIDTH // D_MODEL)


def _row_block(rows):
    rows = [jnp.pad(r.astype(F32), ((0, 0), (0, D_MODEL - r.shape[1]))) for r in rows]
    return jnp.pad(jnp.concatenate(rows, axis=0), ((0, 8 - len(rows)), (0, 0)))


def _early_block(dg2, dg3, dg4, dps, dsink, loss):
    tail = jnp.concatenate([jnp.pad(dsink.reshape(1, -1), ((0, 0), (0, LANES - dsink.size))),
                            jnp.pad(loss.reshape(1, 1), ((0, 0), (0, LANES - 1)))], axis=1)
    return _row_block([dg2, dg3, dg4, dps, tail])


def _late_block(db_in, dg1):
    b = jnp.pad(db_in, ((0, 0), (0, B_ROWS * D_MODEL - IN_WIDTH))).reshape(B_ROWS, D_MODEL)
    return _row_block([b[r:r + 1] for r in range(B_ROWS)] + [dg1])


def _small_update(gearly, gmat, glate, w, m, v):
    names = _SMALL_NAMES
    n = len(names)

    def total(ref, rows):
        acc = ref[0:rows, :]
        for d in range(1, N_DEV):
            acc = acc + ref[d * rows:(d + 1) * rows, :]
        return acc

    def body(*refs):
        early_ref, gmat_ref, late_ref = refs[:3]
        w_refs, m_refs, v_refs = refs[3:3 + n], refs[3 + n:3 + 2 * n], refs[3 + 2 * n:3 + 3 * n]
        outs = refs[3 + 3 * n:]
        loss_ref, g_refs, d_refs = outs[0], outs[1:1 + n], outs[1 + n:1 + 2 * n]
        nm_refs, nv_refs = outs[1 + 2 * n:1 + 3 * n], outs[1 + 3 * n:1 + 4 * n]
        early, late = total(early_ref, 8), total(late_ref, 8)
        loss_ref[...] = jnp.sum(early[4:5, LANES:2 * LANES], axis=1, keepdims=True)
        bias = jnp.concatenate([late[r:r + 1, :] for r in range(B_ROWS - 1)]
                               + [late[B_ROWS - 1:B_ROWS, :IN_WIDTH - (B_ROWS - 1) * D_MODEL]], axis=1)
        grad = dict(b_in=bias, g_mix_pre=late[B_ROWS:B_ROWS + 1, :], g_mix_post=early[0:1, :],
                    g_mlp_pre=early[1:2, :], g_mlp_post=early[2:3, :], pool_scale=early[3:4, :POOL_WIDTH],
                    attn_sinks=early[4:5, :N_Q_HEADS])
        for i, name in enumerate(names):
            g = total(gmat_ref, 4 * POOL_GC) if name == "w_pool" else grad[name]
            g_refs[i][...] = g
            d_refs[i][...], nm_refs[i][...], nv_refs[i][...] = _adamw_math(
                w_refs[i][...], g, m_refs[i][...], v_refs[i][...])

    shapes = [_sds(w[k].shape, F32) for k in names]
    res = pl.pallas_call(
        body, name="small_update", out_shape=[_sds((1, 1), F32)] + shapes * 4,
        compiler_params=pltpu.CompilerParams(vmem_limit_bytes=VMEM_MB * 1024 * 1024),
    )(gearly, gmat, glate, *[w[k] for k in names], *[m[k] for k in names], *[v[k] for k in names])
    loss = res[0]
    per = {k: tuple(res[1 + j * n + i] for j in range(4)) for i, k in enumerate(names)}
    return loss, per


_BIG = ("w_in", "w_branch_pool", "w_branch_attn", "w_out", "w_up", "w_down")
_ORDER = ("g_mix_pre", "w_in", "b_in", "w_pool", "pool_scale", "attn_sinks", "w_branch_pool", "w_branch_attn",
          "w_out", "g_mix_post", "g_mlp_pre", "w_up", "w_down", "g_mlp_post")


def _stack_rows(slab):
    return slab.reshape(-1, slab.shape[2])


def _step(x2, tgt, seq, shards, small, ids):
    tabs = _rope_tables(seq)
    g1, g2, g3, g4 = (small[n] for n in ("g_mix_pre", "g_mix_post", "g_mlp_pre", "g_mlp_post"))
    sinks = small["attn_sinks"].reshape(N_Q_HEADS)
    w_pool = small["w_pool"].reshape(4, POOL_GC, POOL_GC)
    pool_scale = small["pool_scale"]

    def whole(shard, slabs):
        return lax.dynamic_update_slice(slabs, shard[None], (ids[0], 0, 0))

    (up_a, up_b, down_a, down_b, *mix_shards), [[in_slab]] = _cast_shards(
        *(shards[n] for n in ("w_up", "w_down", "w_branch_pool", "w_branch_attn", "w_out")),
        exchanges=[_ex_gather([shards["w_in"]])])
    w_in = _stack_rows(whole(shards["w_in"], in_slab))
    (h, u, q, k, v, gate), [mix_slabs] = _inproj(
        x2, g1, w_in, small["b_in"], tabs, seq, exchanges=[_ex_gather(mix_shards)])
    w_bp, w_ba, out_slab = (whole(s, g) for s, g in zip(mix_shards, mix_slabs))
    w_out = _stack_rows(out_slab)
    (y_attn, diff, y_pool), [[got_a, got_b]] = _mixers_fwd(
        q, k, v, sinks, u, w_pool, pool_scale, seq, exchanges=[_ex_gather([up_a, up_b])])
    (merged, mix, x1, h2), [[got_c, got_d]] = _merge_out(
        y_pool, y_attn, gate, x2, w_bp, w_ba, w_out, g2, g3, exchanges=[_ex_gather([down_a, down_b])])
    w_up = (whole(up_a, got_a), whole(up_b, got_b))
    w_down = (whole(down_a, got_c), whole(down_b, got_d))
    act, dff, dup, dx1, dmix, loss_acc, dg4, dg3, dg2 = _mlp_core(h2, x1, mix, tgt, w_up, w_down, g4, g3, g2)

    dw_down = _dw("down", act, dff, 2048, 1024)[0].reshape(N_CHIPS, D_FF // N_CHIPS, D_MODEL)
    (dbp, dba, dgate, dyp, dya), [[got]] = _merge_bwd(
        dmix, gate, y_pool, y_attn, w_out, w_bp, w_ba, exchanges=[_ex_pair([dw_down])])
    ps_down = _pair_sum(ids, dw_down, got)
    (dw_up,), [[got]] = _dw("up", h2, dup, 1024, 1024, tk=4 * TM, shard_cols=True,
                            exchanges=[_ex_chip([ps_down[1]])])
    half_down = _chip_sum(ids, ps_down[0], got)
    (dw_out, dw_bp, dw_ba), [[got]] = _dw_mix(merged, dmix, y_pool, dbp, y_attn, dba, exchanges=[_ex_pair([dw_up])])
    ps_up = _pair_sum(ids, dw_up, got)
    dw_mix = [dw_out.reshape(N_CHIPS, D_MODEL // N_CHIPS, D_MODEL), dw_bp, dw_ba]
    (dq, dk, dv, dsink, du, dw_pool, dps), [[got], gots, [g_down]] = _mixers_bwd(
        q, k, v, dya, sinks, tabs, dyp, diff, w_pool, pool_scale, seq,
        exchanges=[_ex_chip([ps_up[1]]), _ex_pair(dw_mix), _ex_swap([half_down])])
    half_up = _chip_sum(ids, ps_up[0], got)
    ps_mix = _pair_sum_small(ids, dw_mix, gots)
    parts = (du, dq, dk, dv, dgate)
    early = _early_block(dg2, dg3, dg4, dps, dsink[:, 0], loss_acc[0, 0])
    mat = dw_pool.reshape(4 * POOL_GC, POOL_GC)
    (dw_in_t, db_in), [gots, [gearly, gmat], [g_up]] = _dw_in(
        h, parts, exchanges=[_ex_chip([p[1] for p in ps_mix]), _ex_allgather([early, mat]), _ex_swap([half_up])])
    half_mix = _chip_sum_small(ids, [p[0] for p in ps_mix], gots)
    dw_in = dw_in_t.reshape(N_CHIPS, IN_WIDTH // N_CHIPS, D_MODEL)
    g_mix, [got] = _alone("swap_mix_pair_in", _ex_swap(half_mix), _ex_pair([dw_in]))
    ps_in = _pair_sum(ids, dw_in, got)
    (gx, dg1), [[got]] = _inproj_bwd(parts, x2, dx1, w_in, g1, exchanges=[_ex_chip([ps_in[1]])])
    [g_in], [glate] = _alone("swap_in_allgather", _ex_swap([_chip_sum(ids, ps_in[0], got)]),
                             _ex_allgather([_late_block(db_in, dg1)]))

    grads = dict(w_in=g_in, w_branch_pool=g_mix[1], w_branch_attn=g_mix[2], w_out=g_mix[0], w_up=g_up, w_down=g_down)
    return (gearly, gmat, glate), gx, grads


def kernel(x, g_mix_pre, w_in, b_in, w_pool, pool_scale, attn_sinks, w_branch_pool, w_branch_attn, w_out, g_mix_post, g_mlp_pre, w_up, w_down, g_mlp_post, loss_target, m_g_mix_pre, m_w_in, m_b_in, m_w_pool, m_pool_scale, m_attn_sinks, m_w_branch_pool, m_w_branch_attn, m_w_out, m_g_mix_post, m_g_mlp_pre, m_w_up, m_w_down, m_g_mlp_post, v_g_mix_pre, v_w_in, v_b_in, v_w_pool, v_pool_scale, v_attn_sinks, v_w_branch_pool, v_w_branch_attn, v_w_out, v_g_mix_post, v_g_mlp_pre, v_w_up, v_w_down, v_g_mlp_post):
    weights = dict(g_mix_pre=g_mix_pre, w_in=w_in, b_in=b_in, w_pool=w_pool, pool_scale=pool_scale,
                   attn_sinks=attn_sinks, w_branch_pool=w_branch_pool, w_branch_attn=w_branch_attn, w_out=w_out,
                   g_mix_post=g_mix_post, g_mlp_pre=g_mlp_pre, w_up=w_up, w_down=w_down, g_mlp_post=g_mlp_post)
    mom1 = dict(g_mix_pre=m_g_mix_pre, w_in=m_w_in, b_in=m_b_in, w_pool=m_w_pool, pool_scale=m_pool_scale,
                attn_sinks=m_attn_sinks, w_branch_pool=m_w_branch_pool, w_branch_attn=m_w_branch_attn,
                w_out=m_w_out, g_mix_post=m_g_mix_post, g_mlp_pre=m_g_mlp_pre, w_up=m_w_up, w_down=m_w_down,
                g_mlp_post=m_g_mlp_post)
    mom2 = dict(g_mix_pre=v_g_mix_pre, w_in=v_w_in, b_in=v_b_in, w_pool=v_w_pool, pool_scale=v_pool_scale,
                attn_sinks=v_attn_sinks, w_branch_pool=v_w_branch_pool, w_branch_attn=v_w_branch_attn,
                w_out=v_w_out, g_mix_post=v_g_mix_post, g_mlp_pre=v_g_mlp_pre, w_up=v_w_up, w_down=v_w_down,
                g_mlp_post=v_g_mlp_post)
    b_loc, seq, _ = x.shape
    x2 = x.reshape(b_loc * seq, D_MODEL)
    tgt = loss_target.reshape(b_loc * seq, D_MODEL)
    ids = jnp.stack([2 * lax.axis_index("x") + lax.axis_index("y"), lax.axis_index("c")]).astype(jnp.int32)

    def flat(n, a):
        return a[0].T if n == "w_in" else a[0]

    def unflat(n, a):
        return (a.T if n == "w_in" else a)[None]

    shards = {n: flat(n, weights[n]).astype(BF16) if n == "w_in" else flat(n, weights[n]) for n in _BIG}
    small = {n: weights[n] for n in _ORDER if n not in _BIG}
    (gearly, gmat, glate), gx, grads = _step(x2, tgt, seq, shards, small, ids)

    def two_d(src):
        return {n: src[n].reshape(4 * POOL_GC, POOL_GC) if n == "w_pool" else src[n] for n in _SMALL_NAMES}

    loss, per = _small_update(gearly, gmat, glate, two_d(weights), two_d(mom1), two_d(mom2))
    delta, new_m, new_v = {}, {}, {}
    for n in _SMALL_NAMES:
        grads[n], delta[n], new_m[n], new_v[n] = (a.reshape(weights[n].shape) for a in per[n])
    for n in _BIG:
        update = _adamw if n == "w_in" else _adamw_sparse
        d, nm, nv = update(flat(n, weights[n]), grads[n], flat(n, mom1[n]), flat(n, mom2[n]))
        grads[n] = unflat(n, grads[n])
        delta[n], new_m[n], new_v[n] = unflat(n, d), unflat(n, nm), unflat(n, nv)

    return (loss[0, 0], gx.reshape(x.shape), *[grads[n] for n in _ORDER], *[delta[n] for n in _ORDER],
            *[new_m[n] for n in _ORDER], *[new_v[n] for n in _ORDER])
```

```python
import jax
import jax.numpy as jnp
from jax import lax
from jax.experimental import pallas as pl
from jax.experimental.pallas import tpu as pltpu
from jax.experimental.pallas import tpu_sc as plsc

F32 = jnp.float32
BF16 = jnp.bfloat16

D_MODEL = 1024
POOL_WINDOWS = (2, 4, 8, 16)
POOL_WIDTH = 512
POOL_GC = 128
HALO = 16
HEAD_DIM = 64
N_Q_HEADS = 8
ATTN_WIDTH = 512
KV_WIDTH = 128
BLOCK = 128
NEG_INF = -1e30
ROPE_THETA = 500000.0
ROT_DIM = 16
GATE_WIDTH = 2048
IN_WIDTH = 3328
D_FF = 4096
EPS = 1e-6
SCALE = HEAD_DIM ** -0.5
C_Q, C_K, C_V, C_G = 512, 1024, 1152, 1280

ADAM_LR, ADAM_B1, ADAM_B2, ADAM_EPS, ADAM_WD, ADAM_STEP = 0.001, 0.9, 0.999, 1e-08, 0.01, 10

N_CHIPS = 4
N_DEV = 8
LANES = 128
TM = 512
VMEM_MB = 56

MESH = pl.DeviceIdType.MESH
ANY = pl.BlockSpec(memory_space=pl.ANY)


def _cp(*sem, vmem=VMEM_MB):
    return pltpu.CompilerParams(dimension_semantics=sem, vmem_limit_bytes=vmem * 1024 * 1024)


def _rows(tile, cols):
    return pl.BlockSpec((tile, cols), lambda i: (i, 0))


def _const(shape):
    nd = len(shape)
    return pl.BlockSpec(shape, lambda i: (0,) * nd)


def _sds(shape, dtype):
    return jax.ShapeDtypeStruct(shape, dtype)


def _dot(a, b):
    return jnp.dot(a, b, preferred_element_type=F32)


def _dot_nt(a, b):
    return lax.dot_general(a, b, (((1,), (1,)), ((), ())), preferred_element_type=F32)


def _dot_tn(a, b):
    return lax.dot_general(a, b, (((0,), (0,)), ((), ())), preferred_element_type=F32)


def _rms(x):
    return lax.rsqrt(jnp.mean(x * x, axis=-1, keepdims=True) + EPS)


def _norm_bwd(x, g, dout):
    r = _rms(x)
    n = x * r
    dn = dout * g
    dx = r * (dn - n * jnp.mean(dn * n, axis=-1, keepdims=True))
    return dx, jnp.sum(dout * n, axis=0, keepdims=True)


def _rot_fwd(t, c, a, bt):
    return t * c + pltpu.roll(t, LANES - 8, 1) * a + pltpu.roll(t, 8, 1) * bt


def _rot_bwd(d, c, a, bt):
    return d * c + pltpu.roll(d * a, 8, 1) + pltpu.roll(d * bt, LANES - 8, 1)


def _rope_tables(seq):
    pos = jnp.arange(seq, dtype=F32)
    inv_freq = ROPE_THETA ** (-jnp.arange(0, ROT_DIM, 2, dtype=F32) / ROT_DIM)
    ang = pos[:, None] * inv_freq[None, :]
    cos, sin = jnp.cos(ang), jnp.sin(ang)
    ones = jnp.ones((seq, HEAD_DIM - ROT_DIM), F32)
    zeros8 = jnp.zeros((seq, 8), F32)
    zrest = jnp.zeros((seq, HEAD_DIM - ROT_DIM), F32)
    c = jnp.concatenate([cos, cos, ones], axis=1)
    a = jnp.concatenate([-sin, zeros8, zrest], axis=1)
    bt = jnp.concatenate([zeros8, sin, zrest], axis=1)
    return tuple(jnp.tile(t, (1, 2)) for t in (c, a, bt))


class _Exchange:
    def __init__(self, inputs, out_shapes, sems, start, finish, aliases=None, middle=None):
        self.inputs, self.out_shapes, self.sems = list(inputs), list(out_shapes), list(sems)
        self.start, self.finish, self.aliases = start, finish, dict(aliases or {})
        self.middle = middle


def _call(body, *, name, grid, in_specs, out_specs, out_shape, args, scratch=(), sem=(), exchanges=()):
    in_specs, out_specs, out_shape, scratch = list(in_specs), list(out_specs), list(out_shape), list(scratch)
    if not exchanges:
        return pl.pallas_call(body, name=name, grid=grid, in_specs=in_specs, out_specs=out_specs,
                              out_shape=out_shape, scratch_shapes=scratch, compiler_params=_cp(*sem))(*args)
    n_in, n_out, n_scr = len(in_specs), len(out_specs), len(scratch)
    x_in = [a for ex in exchanges for a in ex.inputs]
    x_out = [s for ex in exchanges for s in ex.out_shapes]
    x_sem = [s for ex in exchanges for s in ex.sems]
    aliases, i_off, o_off = {}, n_in, n_out
    for ex in exchanges:
        for i, o in ex.aliases.items():
            aliases[i_off + i] = o_off + o
        i_off += len(ex.inputs)
        o_off += len(ex.out_shapes)

    def split(flat):
        out, pos = [], 0
        for ex, n in zip(exchanges, flat[1]):
            out.append(flat[0][pos:pos + n])
            pos += n
        return out

    def carrier(*refs):
        pos = 0
        groups = []
        for n in (n_in, len(x_in), n_out, len(x_out), n_scr, len(x_sem)):
            groups.append(refs[pos:pos + n])
            pos += n
        ins, xin, outs, xout, scr, xsem = groups
        xin = split((xin, [len(ex.inputs) for ex in exchanges]))
        xout = split((xout, [len(ex.out_shapes) for ex in exchanges]))
        xsem = split((xsem, [len(ex.sems) for ex in exchanges]))
        first = pl.program_id(0) == 0
        last = pl.program_id(0) == grid[0] - 1
        for d in range(1, len(grid)):
            first = jnp.logical_and(first, pl.program_id(d) == 0)
            last = jnp.logical_and(last, pl.program_id(d) == grid[d] - 1)

        @pl.when(first)
        def _():
            for ex, i, o, s in zip(exchanges, xin, xout, xsem):
                ex.start(i, o, s)

        if any(ex.middle for ex in exchanges):
            half = pl.program_id(0) == 5 * grid[0] // 8
            for d in range(1, len(grid)):
                half = jnp.logical_and(half, pl.program_id(d) == 0)

            @pl.when(half)
            def _():
                for ex, i, o, s in zip(exchanges, xin, xout, xsem):
                    if ex.middle:
                        ex.middle(i, o, s)

        body(*ins, *outs, *scr)

        @pl.when(last)
        def _():
            for ex, i, o, s in zip(exchanges, xin, xout, xsem):
                ex.finish(i, o, s)

    res = pl.pallas_call(
        carrier, name=name, grid=grid, in_specs=in_specs + [ANY] * len(x_in),
        out_specs=out_specs + [ANY] * len(x_out), out_shape=out_shape + x_out,
        scratch_shapes=scratch + x_sem, input_output_aliases=aliases,
        compiler_params=_cp(*(["arbitrary"] * len(grid))),
    )(*args, *x_in)
    return res[:n_out], split((res[n_out:], [len(ex.out_shapes) for ex in exchanges]))


def _alone(name, *exchanges):
    n_in = [len(ex.inputs) for ex in exchanges]
    n_out = [len(ex.out_shapes) for ex in exchanges]
    n_sem = [len(ex.sems) for ex in exchanges]
    aliases, i_off, o_off = {}, 0, 0
    for ex in exchanges:
        for i, o in ex.aliases.items():
            aliases[i_off + i] = o_off + o
        i_off += len(ex.inputs)
        o_off += len(ex.out_shapes)

    def split(flat, counts):
        out, pos = [], 0
        for n in counts:
            out.append(flat[pos:pos + n])
            pos += n
        return out

    def body(*refs):
        ins, outs, sems = split(refs, [sum(n_in), sum(n_out), sum(n_sem)])
        groups = list(zip(exchanges, split(ins, n_in), split(outs, n_out), split(sems, n_sem)))
        for ex, i, o, s in groups:
            ex.start(i, o, s)
        for ex, i, o, s in groups:
            if ex.middle:
                ex.middle(i, o, s)
        for ex, i, o, s in groups:
            ex.finish(i, o, s)

    res = pl.pallas_call(
        body, name=name, in_specs=[ANY] * sum(n_in), out_specs=[ANY] * sum(n_out),
        out_shape=[s for ex in exchanges for s in ex.out_shapes],
        scratch_shapes=[s for ex in exchanges for s in ex.sems], input_output_aliases=aliases,
    )(*[a for ex in exchanges for a in ex.inputs])
    return split(res, n_out)


def _place():
    x, y, c = lax.axis_index("x"), lax.axis_index("y"), lax.axis_index("c")
    chips = [(1 - x, y), (x, 1 - y), (1 - x, 1 - y)]
    return x, y, c, chips


def _remote(src, dst, send, recv, to):
    return pltpu.make_async_remote_copy(src_ref=src, dst_ref=dst, send_sem=send, recv_sem=recv,
                                        device_id=to, device_id_type=MESH)


def _ex_gather(shards):
    nw = len(shards)
    hrs = [s.shape[0] // 2 for s in shards]

    def copies(ins, outs, sems):
        s0, r0, s1, r1, s2, r2, fs, fr = sems
        x, y, c, _ = _place()
        me, xn, yn, dg = (x, y), (1 - x, y), (x, 1 - y), (1 - x, 1 - y)
        nbr = (xn, yn)
        sibling = (x, y, 1 - c)

        def piece(w, chip, core, part=None):
            hr = hrs[w]
            rows = pl.ds(core * hr, hr) if part is None else pl.ds(core * hr + part * (hr // 2), hr // 2)
            return outs[w].at[2 * chip[0] + chip[1], rows]

        def first(w, k, lead):
            part = k if lead else 1 - k
            send, recv = (s0, r0) if lead else (s1, r1)
            rows = pl.ds(c * hrs[w] + part * (hrs[w] // 2), hrs[w] // 2)
            return _remote(ins[w].at[rows], piece(w, me, c, part), send.at[w, k], recv.at[w, k], (*nbr[k], c))

        def landed(w, k, lead):
            part = k if lead else 1 - k
            send, recv = (s0, r0) if lead else (s1, r1)
            return _remote(piece(w, nbr[k], c, part), piece(w, nbr[k], c, part), send.at[w, k], recv.at[w, k],
                           (*nbr[k], c))

        def onward(w, k):
            return _remote(piece(w, nbr[k], c, k), piece(w, nbr[k], c, k), s2.at[w, k], r2.at[w, k],
                           (*nbr[1 - k], c))

        def arrived(w, k):
            return _remote(piece(w, dg, c, k), piece(w, dg, c, k), s2.at[w, k], r2.at[w, k], (*nbr[1 - k], c))

        def passed(w, j):
            chip = (xn, yn, dg)[j]
            return _remote(piece(w, chip, c), piece(w, chip, c), fs.at[w, j], fr.at[w, j], sibling)

        def handed(w, j):
            chip = (xn, yn, dg)[j]
            return _remote(piece(w, chip, 1 - c), piece(w, chip, 1 - c), fs.at[w, j], fr.at[w, j], sibling)

        return first, landed, onward, arrived, passed, handed

    def start(ins, outs, sems):
        first = copies(ins, outs, sems)[0]
        for lead in (True, False):
            for w in range(nw):
                for k in range(2):
                    first(w, k, lead).start()

    def middle(ins, outs, sems):
        _, landed, onward, _, passed, _ = copies(ins, outs, sems)
        for w in range(nw):
            for k in range(2):
                landed(w, k, True).wait_recv()
                onward(w, k).start()
        for w in range(nw):
            for k in range(2):
                landed(w, k, False).wait_recv()
                passed(w, k).start()

    def finish(ins, outs, sems):
        first, _, onward, arrived, passed, handed = copies(ins, outs, sems)
        for w in range(nw):
            for k in range(2):
                arrived(w, k).wait_recv()
            passed(w, 2).start()
        for w in range(nw):
            for j in range(3):
                handed(w, j).wait_recv()
        for w in range(nw):
            for k in range(2):
                first(w, k, True).wait_send()
                first(w, k, False).wait_send()
                onward(w, k).wait_send()
            for j in range(3):
                passed(w, j).wait_send()

    return _Exchange(shards, [_sds((N_CHIPS,) + s.shape, s.dtype) for s in shards],
                     [pltpu.SemaphoreType.DMA((nw, 2))] * 6 + [pltpu.SemaphoreType.DMA((nw, 3))] * 2,
                     start, finish, middle=middle)


def _ex_pair(grads):
    nw = len(grads)

    def copies(ins, outs, sems):
        x, y, c, _ = _place()
        out = []
        for w in range(nw):
            hr = grads[w].shape[1] // 2
            out.append(_remote(ins[w].at[:, pl.ds((1 - c) * hr, hr)], outs[w], sems[0].at[w], sems[1].at[w],
                               (x, y, 1 - c)))
        return out

    def start(ins, outs, sems):
        for cp in copies(ins, outs, sems):
            cp.start()

    def finish(ins, outs, sems):
        for cp in copies(ins, outs, sems):
            cp.wait()

    return _Exchange(grads, [_sds((N_CHIPS, g.shape[1] // 2, g.shape[2]), F32) for g in grads],
                     [pltpu.SemaphoreType.DMA((nw,))] * 2, start, finish)


def _ex_chip(pieces):
    nw = len(pieces)

    def copies(ins, outs, sems):
        x, y, c, chips = _place()
        return [_remote(ins[w].at[2 * cx + cy], outs[w].at[k], sems[0].at[w, k], sems[1].at[w, k], (cx, cy, c))
                for w in range(nw) for k, (cx, cy) in enumerate(chips)]

    def start(ins, outs, sems):
        for cp in copies(ins, outs, sems):
            cp.start()

    def finish(ins, outs, sems):
        for cp in copies(ins, outs, sems):
            cp.wait()

    return _Exchange(pieces, [_sds((3,) + p.shape[1:], BF16) for p in pieces],
                     [pltpu.SemaphoreType.DMA((nw, 3))] * 2, start, finish)


def _ex_swap(fulls):
    nw = len(fulls)

    def start(ins, outs, sems):
        x, y, c, _ = _place()
        for w in range(nw):
            hr = fulls[w].shape[0] // 2
            mine = pl.ds(c * hr, hr)
            _remote(ins[w].at[mine], outs[w].at[mine], sems[0].at[w], sems[1].at[w], (x, y, 1 - c)).start()

    def finish(ins, outs, sems):
        x, y, c, _ = _place()
        for w in range(nw):
            hr = fulls[w].shape[0] // 2
            mine, theirs = pl.ds(c * hr, hr), pl.ds((1 - c) * hr, hr)
            _remote(ins[w].at[mine], outs[w].at[mine], sems[0].at[w], sems[1].at[w], (x, y, 1 - c)).wait_send()
            _remote(ins[w].at[theirs], outs[w].at[theirs], sems[0].at[w], sems[1].at[w], (x, y, 1 - c)).wait_recv()

    return _Exchange(fulls, [_sds(f.shape, F32) for f in fulls], [pltpu.SemaphoreType.DMA((nw,))] * 2,
                     start, finish, aliases={w: w for w in range(nw)})


def _ex_allgather(blocks):
    nb = len(blocks)

    def copies(ins, outs, sems):
        send, recv, lsem = sems
        x, y, c, chips = _place()
        me, sibling = (x, y, c), (x, y, 1 - c)

        def rows(b, px, py, pc):
            m_per = blocks[b].shape[0]
            return outs[b].at[pl.ds((4 * px + 2 * py + pc) * m_per, m_per), :]

        def copy(b, k, blk, to, src=None):
            return _remote(rows(b, *blk) if src is None else src, rows(b, *blk), send.at[b, k], recv.at[b, k], to)

        def mine(b):
            return pltpu.make_async_copy(ins[b], rows(b, *me), lsem.at[b])

        def first(b, k):
            return copy(b, k, me, sibling if k == 0 else (*chips[k - 1], c), src=ins[b])

        def passed(b, j):
            return copy(b, 4 + j, (*chips[j], c), sibling)

        def landed(b, j):
            return copy(b, 1 + j, (*chips[j], c), me)

        def handed(b, k):
            return copy(b, 0, sibling, me) if k == 0 else copy(b, 3 + k, (*chips[k - 1], 1 - c), me)

        return mine, first, passed, landed, handed

    def start(ins, outs, sems):
        mine, first, _, _, _ = copies(ins, outs, sems)
        for b in range(nb):
            mine(b).start()
            for k in range(4):
                first(b, k).start()

    def finish(ins, outs, sems):
        mine, first, passed, landed, handed = copies(ins, outs, sems)
        sent = []
        for b in range(nb):
            for j in range(3):
                landed(b, j).wait_recv()
                cp = passed(b, j)
                cp.start()
                sent.append(cp)
        for b in range(nb):
            for k in range(4):
                handed(b, k).wait_recv()
            for k in range(4):
                first(b, k).wait_send()
        for cp in sent:
            cp.wait_send()
        for b in range(nb):
            mine(b).wait()

    return _Exchange(blocks, [_sds((N_DEV * b.shape[0], b.shape[1]), F32) for b in blocks],
                     [pltpu.SemaphoreType.DMA((nb, 7)), pltpu.SemaphoreType.DMA((nb, 7)), pltpu.SemaphoreType.DMA((nb,))],
                     start, finish)


def _cast_shards(w_up, w_down, w_bp, w_ba, w_out, exchanges=()):
    r, c = w_up.shape
    tr = HALF // 2
    steps = r // tr

    def body(up_ref, down_ref, bp_ref, ba_ref, out_ref, ua_ref, ub_ref, da_ref, db_ref, bpo_ref, bao_ref, outo_ref):
        i = pl.program_id(0)

        @pl.when(i == 0)
        def _():
            for src, dst in ((bp_ref, bpo_ref), (ba_ref, bao_ref), (out_ref, outo_ref)):
                dst[...] = src[...].astype(BF16)

        @pl.when(i < steps // 2)
        def _():
            ua_ref[...] = up_ref[...].astype(BF16)
            da_ref[...] = down_ref[...].astype(BF16)

        @pl.when(i >= steps // 2)
        def _():
            ub_ref[...] = up_ref[...].astype(BF16)
            db_ref[...] = down_ref[...].astype(BF16)

    rows = _rows(tr, c)
    first = pl.BlockSpec((tr, c), lambda i: (jnp.minimum(i, steps // 2 - 1), 0))
    second = pl.BlockSpec((tr, c), lambda i: (jnp.maximum(i - steps // 2, 0), 0))
    half = _sds((HALF, c), BF16)
    return _call(
        body, name="cast_shards", grid=(steps,),
        in_specs=[rows, rows, _const(w_bp.shape), _const(w_ba.shape), _const(w_out.shape)],
        out_specs=[first, second, first, second, _const(w_bp.shape), _const(w_ba.shape), _const(w_out.shape)],
        out_shape=[half, half, half, half, _sds(w_bp.shape, BF16), _sds(w_ba.shape, BF16), _sds(w_out.shape, BF16)],
        args=(w_up, w_down, w_bp, w_ba, w_out), sem=("arbitrary",), exchanges=exchanges)


def _inproj(x2, g1, w_in_t, b_in, tabs, seq, exchanges=()):
    T = x2.shape[0]
    tm = min(TM, seq)
    nseq = seq // tm

    def body(x_ref, g_ref, w_ref, b_ref, c_ref, a_ref, bt_ref, h_ref, u_ref, q_ref, k_ref, v_ref, gate_ref):
        x = x_ref[...]
        h = (x * _rms(x) * g_ref[...]).astype(BF16)
        h_ref[...] = h

        def proj(lo, hi):
            return _dot_nt(h, w_ref[lo:hi, :]) + b_ref[:, lo:hi]

        c, a, bt = c_ref[...], a_ref[...], bt_ref[...]
        u_ref[...] = proj(0, C_Q)
        q = proj(C_Q, C_K)
        for p in range(4):
            sl = slice(LANES * p, LANES * (p + 1))
            q_ref[:, sl] = (_rot_fwd(q[:, sl], c, a, bt) * SCALE).astype(BF16)
        kv = proj(C_K, C_G)
        k_ref[...] = _rot_fwd(kv[:, :KV_WIDTH], c, a, bt).astype(BF16)
        v_ref[...] = kv[:, KV_WIDTH:].astype(BF16)
        for j in range(2):
            lo = C_G + D_MODEL * j
            gate_ref[:, D_MODEL * j:D_MODEL * (j + 1)] = jax.nn.sigmoid(proj(lo, lo + D_MODEL)).astype(BF16)

    tab = pl.BlockSpec((tm, LANES), lambda i: (i % nseq, 0))
    return _call(
        body, name="inproj", grid=(T // tm,),
        in_specs=[_rows(tm, D_MODEL), _const((1, D_MODEL)), _const((IN_WIDTH, D_MODEL)), _const((1, IN_WIDTH)),
                  tab, tab, tab],
        out_specs=[_rows(tm, D_MODEL), _rows(tm, POOL_WIDTH), _rows(tm, ATTN_WIDTH), _rows(tm, KV_WIDTH),
                   _rows(tm, KV_WIDTH), _rows(tm, GATE_WIDTH)],
        out_shape=[_sds((T, D_MODEL), BF16), _sds((T, POOL_WIDTH), F32), _sds((T, ATTN_WIDTH), BF16),
                   _sds((T, KV_WIDTH), BF16), _sds((T, KV_WIDTH), BF16), _sds((T, GATE_WIDTH), BF16)],
        args=(x2, g1, w_in_t, b_in, *tabs), sem=("parallel",), exchanges=exchanges)


def _inv_count(pos, w):
    return 1.0 / jnp.minimum(pos + 1, w).astype(F32)


def _pool_tile(i, tp, nseq, u_ref, prev_ref, w_ref, s_ref, diff_ref, y_ref):
    first = (i % nseq) == 0
    prev = jnp.where(first, 0.0, prev_ref[...])
    ext = jnp.concatenate([prev, u_ref[...]], axis=0)
    pos = (i % nseq) * tp + lax.broadcasted_iota(jnp.int32, (tp, 1), 0)
    for gi, w in enumerate(POOL_WINDOWS):
        sl = slice(POOL_GC * gi, POOL_GC * (gi + 1))
        xg = ext[:, sl]
        s = xg
        sh = 1
        while sh < w:
            s = s + pltpu.roll(s, sh, 0)
            sh *= 2
        pooled = s[HALO:] * _inv_count(pos, w)
        diff = (pooled - xg[HALO:]).astype(BF16)
        diff_ref[:, sl] = diff
        mixed = _dot(diff, w_ref[gi].astype(BF16))
        y_ref[:, sl] = (mixed * s_ref[:, sl]).astype(BF16)


def _pool_specs(tp):
    per = tp // HALO
    return [_rows(tp, POOL_WIDTH), pl.BlockSpec((HALO, POOL_WIDTH), lambda i: (jnp.maximum(i * per - 1, 0), 0)),
            _const((4, POOL_GC, POOL_GC)), _const((1, POOL_WIDTH))]


GROUP = 4
GROWS = GROUP * BLOCK


def _attn_masks(n):
    qi = lax.broadcasted_iota(jnp.int32, (GROWS, 2 * BLOCK), 0) % BLOCK
    kj = lax.broadcasted_iota(jnp.int32, (GROWS, 2 * BLOCK), 1)
    rel = qi + BLOCK - kj
    valid = (rel >= 0) & (rel < BLOCK) & (kj >= jnp.where(n > 0, 0, BLOCK))
    lo = lax.broadcasted_iota(jnp.int32, (BLOCK, LANES), 1) < HEAD_DIM
    return valid, lo


def _by_example(bl, *arrays):
    return [a.reshape(bl, a.shape[0] // bl, a.shape[1]) for a in arrays]


def _stack_heads(ref, h, lo):
    keep = lo if h == 0 else jnp.logical_not(lo)
    pieces = []
    for p in (2 * h, 2 * h + 1):
        xp = ref[:, LANES * p:LANES * (p + 1)].astype(F32)
        for e in range(2):
            t = xp if e == h else pltpu.roll(xp, HEAD_DIM, 1)
            pieces.append(jnp.where(keep, t, 0.0).astype(BF16))
    return jnp.concatenate(pieces, axis=0)


def _unstack_heads(stacked, h, lo):
    pairs = []
    for j in range(2):
        parts = []
        for e in range(2):
            t = stacked[BLOCK * (2 * j + e):BLOCK * (2 * j + e + 1)]
            parts.append(t if e == h else pltpu.roll(t, HEAD_DIM, 1))
        pairs.append(jnp.where(lo, parts[0], parts[1]))
    return pairs


def _sink_rows(sink_ref, h):
    head = lax.broadcasted_iota(jnp.int32, (GROWS, 1), 0) // BLOCK
    col = jnp.zeros((GROWS, 1), F32) + sink_ref[GROUP * h]
    for g in range(1, GROUP):
        col = jnp.where(head == g, sink_ref[GROUP * h + g], col)
    return col


def _group_probs(qs, kk, valid, sink):
    s = jnp.where(valid, _dot_nt(qs, kk), NEG_INF)
    m = jnp.maximum(jnp.max(s, axis=1, keepdims=True), sink)
    ex = jnp.exp(s - m)
    es = jnp.exp(sink - m)
    inv = 1.0 / (jnp.sum(ex, axis=1, keepdims=True) + es)
    return ex * inv, es * inv


def _mixers_fwd(q, k, v, sinks, u, w_pool, pool_scale, seq, exchanges=()):
    T = q.shape[0]
    nb = seq // BLOCK
    bl = T // seq
    tp = T // nb
    nseq = seq // tp

    def body(sink_ref, q_ref, kp_ref, kc_ref, vp_ref, vc_ref, u_ref, prev_ref, w_ref, s_ref, o_ref, diff_ref, y_ref):
        n = pl.program_id(0)
        valid, lo = _attn_masks(n)
        for b in range(bl):
            kk = jnp.concatenate([kp_ref[b], kc_ref[b]], axis=0)
            vv = jnp.concatenate([vp_ref[b], vc_ref[b]], axis=0)
            for h in range(2):
                qs = _stack_heads(q_ref.at[b], h, lo)
                pr, _ = _group_probs(qs, kk, valid, _sink_rows(sink_ref, h))
                o = _dot(pr.astype(BF16), vv)
                for j, pair in enumerate(_unstack_heads(o, h, lo)):
                    p = 2 * h + j
                    o_ref[b, :, LANES * p:LANES * (p + 1)] = pair.astype(BF16)
        _pool_tile(n, tp, nseq, u_ref, prev_ref, w_ref, s_ref, diff_ref, y_ref)

    cur = lambda n: (0, n, 0)
    prv = lambda n: (0, jnp.maximum(n - 1, 0), 0)
    kv = lambda m: pl.BlockSpec((bl, BLOCK, KV_WIDTH), m)
    res = _call(
        body, name="mixers_fwd", grid=(nb,),
        in_specs=[pl.BlockSpec(memory_space=pltpu.SMEM), pl.BlockSpec((bl, BLOCK, ATTN_WIDTH), cur),
                  kv(prv), kv(cur), kv(prv), kv(cur)] + _pool_specs(tp),
        out_specs=[pl.BlockSpec((bl, BLOCK, ATTN_WIDTH), cur), _rows(tp, POOL_WIDTH), _rows(tp, POOL_WIDTH)],
        out_shape=[_sds((bl, seq, ATTN_WIDTH), BF16), _sds((T, POOL_WIDTH), BF16), _sds((T, POOL_WIDTH), BF16)],
        args=(sinks, *_by_example(bl, q, k, k, v, v), u, u, w_pool, pool_scale), sem=("parallel",),
        exchanges=exchanges)
    outs, rest = res if exchanges else (res, None)
    return [outs[0].reshape(T, ATTN_WIDTH), outs[1], outs[2]], rest


def _branch(y, w_ref):
    return jnp.concatenate([_dot(y, w_ref[j]) for j in range(N_CHIPS)], axis=1)


def _merge_out(y_pool, y_attn, gate, x2, w_bp, w_ba, w_out, g2, g3, exchanges=()):
    T = x2.shape[0]
    tm = min(TM, T)

    def body(yp_ref, ya_ref, gate_ref, x_ref, wbp_ref, wba_ref, wo_ref, g2_ref, g3_ref,
             mg_ref, mix_ref, x1_ref, h2_ref):
        bp, ba = _branch(yp_ref[...], wbp_ref), _branch(ya_ref[...], wba_ref)
        merged = (gate_ref[:, :D_MODEL].astype(F32) * bp + gate_ref[:, D_MODEL:].astype(F32) * ba).astype(BF16)
        mg_ref[...] = merged
        mix = _dot(merged, wo_ref[...])
        mix_ref[...] = mix
        x1 = x_ref[...] + mix * _rms(mix) * g2_ref[...]
        x1_ref[...] = x1
        h2_ref[...] = (x1 * _rms(x1) * g3_ref[...]).astype(BF16)

    return _call(
        body, name="merge_out", grid=(T // tm,),
        in_specs=[_rows(tm, POOL_WIDTH), _rows(tm, ATTN_WIDTH), _rows(tm, GATE_WIDTH), _rows(tm, D_MODEL),
                  _const(w_bp.shape), _const(w_ba.shape), _const((D_MODEL, D_MODEL)),
                  _const((1, D_MODEL)), _const((1, D_MODEL))],
        out_specs=[_rows(tm, D_MODEL)] * 4,
        out_shape=[_sds((T, D_MODEL), BF16), _sds((T, D_MODEL), F32), _sds((T, D_MODEL), F32),
                   _sds((T, D_MODEL), BF16)],
        args=(y_pool, y_attn, gate, x2, w_bp, w_ba, w_out, g2, g3), sem=("parallel",), exchanges=exchanges)


HALF = D_MODEL // 2
TM_MLP = 256


def _mlp_core(h2, x1, mix, tgt, w_up, w_down, g4, g3, g2):
    T = h2.shape[0]
    tm = min(TM_MLP, T)

    def body(h_ref, x1_ref, mix_ref, t_ref, g_ref, g3_ref, g2_ref, ua_hbm, ub_hbm, da_hbm, db_hbm,
             act_ref, dff_ref, dup_ref, dx1_ref, dmix_ref, loss_ref, dg_ref, dg3_ref, dg2_ref,
             wu, wd, relu_scr, sems):
        def weight_copy(i):
            src, dst = ((ua_hbm, wu.at[:, :HALF]), (ub_hbm, wu.at[:, HALF:]),
                        (da_hbm, wd.at[:, :HALF]), (db_hbm, wd.at[:, HALF:]))[i]
            return pltpu.make_async_copy(src, dst, sems.at[i])

        @pl.when(pl.program_id(0) == 0)
        def _():
            for i in range(4):
                weight_copy(i).start()
            loss_ref[...] = jnp.zeros_like(loss_ref)
            for ref in (dg_ref, dg3_ref, dg2_ref):
                ref[...] = jnp.zeros_like(ref)
            weight_copy(0).wait()
            weight_copy(1).wait()

        h = h_ref[...]
        ff = None
        for j in range(N_CHIPS):
            lo = D_MODEL * j
            relu = jnp.maximum(_dot(h, wu[j]), 0.0)
            if j == 0:
                @pl.when(pl.program_id(0) == 0)
                def _():
                    weight_copy(2).wait()
                    weight_copy(3).wait()
            relu_scr[:, lo:lo + D_MODEL] = relu
            act = jnp.square(relu).astype(BF16)
            act_ref[:, lo:lo + D_MODEL] = act
            t = _dot(act, wd[j])
            ff = t if ff is None else ff + t
        g = g_ref[...]
        x1 = x1_ref[...]
        err = x1 + ff * _rms(ff) * g - t_ref[...]
        loss_ref[...] += jnp.sum(err * err) * (0.5 / D_MODEL)
        dy = err * (1.0 / D_MODEL)
        dff, dg = _norm_bwd(ff, g, dy)
        dg_ref[...] += dg
        dff = dff.astype(BF16)
        dff_ref[...] = dff
        dh2 = None
        for j in range(N_CHIPS):
            lo = D_MODEL * j
            dup = (_dot_nt(dff, wd[j]) * (2.0 * relu_scr[:, lo:lo + D_MODEL])).astype(BF16)
            dup_ref[:, lo:lo + D_MODEL] = dup
            t = _dot_nt(dup, wu[j])
            dh2 = t if dh2 is None else dh2 + t
        dx, dg3 = _norm_bwd(x1, g3_ref[...], dh2)
        dx1 = dy + dx
        dx1_ref[...] = dx1
        dg3_ref[...] += dg3
        dmix, dg2 = _norm_bwd(mix_ref[...], g2_ref[...], dx1)
        dmix_ref[...] = dmix.astype(BF16)
        dg2_ref[...] += dg2

    slabs = pltpu.VMEM((N_CHIPS, D_MODEL, D_MODEL), BF16)
    gain = _const((1, D_MODEL))
    return pl.pallas_call(
        body, name="mlp_core", grid=(T // tm,),
        in_specs=[_rows(tm, D_MODEL)] * 4 + [gain] * 3 + [ANY] * 4,
        out_specs=[_rows(tm, D_FF), _rows(tm, D_MODEL), _rows(tm, D_FF), _rows(tm, D_MODEL), _rows(tm, D_MODEL),
                   _const((8, LANES)), gain, gain, gain],
        out_shape=[_sds((T, D_FF), BF16), _sds((T, D_MODEL), BF16), _sds((T, D_FF), BF16), _sds((T, D_MODEL), F32),
                   _sds((T, D_MODEL), BF16), _sds((8, LANES), F32)] + [_sds((1, D_MODEL), F32)] * 3,
        scratch_shapes=[slabs] * 2 + [pltpu.VMEM((tm, D_FF), F32), pltpu.SemaphoreType.DMA((4,))],
        compiler_params=_cp("arbitrary"),
    )(h2, x1, mix, tgt, g4, g3, g2, *w_up, *w_down)


def _dw(tag, a, g, ta, tn, shard_cols=False, exchanges=()):
    T, ka = a.shape
    n = g.shape[1]
    tk = min(2 * TM, T)
    nk = T // tk

    def body(a_ref, g_ref, o_ref):
        @pl.when(pl.program_id(2) == 0)
        def _():
            o_ref[...] = jnp.zeros_like(o_ref)

        o_ref[...] += _dot_tn(a_ref[...], g_ref[...])

    if shard_cols:
        per = (n // N_CHIPS) // tn
        out_spec = pl.BlockSpec((None, ta, tn), lambda i, j, k: (j // per, i, j % per))
        out_shape = _sds((N_CHIPS, ka, n // N_CHIPS), F32)
    else:
        out_spec = pl.BlockSpec((ta, tn), lambda i, j, k: (i, j))
        out_shape = _sds((ka, n), F32)
    return _call(
        body, name="dw_" + tag, grid=(ka // ta, n // tn, nk),
        in_specs=[pl.BlockSpec((tk, ta), lambda i, j, k: (k, i)), pl.BlockSpec((tk, tn), lambda i, j, k: (k, j))],
        out_specs=[out_spec], out_shape=[out_shape],
        args=(a, g), sem=("parallel", "parallel", "arbitrary"), exchanges=exchanges)


def _dw_mix(merged, dmix, y_pool, dbp, y_attn, dba, exchanges=()):
    T = merged.shape[0]
    tk = min(2 * TM, T)
    c = D_MODEL // N_CHIPS

    def body(mg_ref, dmix_ref, yp_ref, dbp_ref, ya_ref, dba_ref, out_ref, bp_ref, ba_ref):
        @pl.when(pl.program_id(0) == 0)
        def _():
            for ref in (out_ref, bp_ref, ba_ref):
                ref[...] = jnp.zeros_like(ref)

        out_ref[...] += _dot_tn(mg_ref[...], dmix_ref[...])
        for y_ref, d_ref, o_ref in ((yp_ref, dbp_ref, bp_ref), (ya_ref, dba_ref, ba_ref)):
            res = _dot_tn(y_ref[...], d_ref[...])
            for j in range(N_CHIPS):
                o_ref[j] += res[:, c * j:c * (j + 1)]

    slabs = (N_CHIPS, POOL_WIDTH, c)
    return _call(
        body, name="dw_mix", grid=(T // tk,),
        in_specs=[_rows(tk, D_MODEL), _rows(tk, D_MODEL), _rows(tk, POOL_WIDTH), _rows(tk, D_MODEL),
                  _rows(tk, ATTN_WIDTH), _rows(tk, D_MODEL)],
        out_specs=[_const((D_MODEL, D_MODEL)), _const(slabs), _const(slabs)],
        out_shape=[_sds((D_MODEL, D_MODEL), F32), _sds(slabs, F32), _sds(slabs, F32)],
        args=(merged, dmix, y_pool, dbp, y_attn, dba), sem=("arbitrary",), exchanges=exchanges)


def _merge_bwd(dmix, gate, y_pool, y_attn, w_out, w_bp, w_ba, exchanges=()):
    T = dmix.shape[0]
    tm = min(TM, T)

    def body(dmix_ref, gate_ref, yp_ref, ya_ref, wo_ref, wbp_ref, wba_ref,
             dbp_ref, dba_ref, dgate_ref, dyp_ref, dya_ref):
        dm = _dot_nt(dmix_ref[...], wo_ref[...])
        for j, (y_ref, db_ref, w_ref, dy_ref) in enumerate(
                ((yp_ref, dbp_ref, wbp_ref, dyp_ref), (ya_ref, dba_ref, wba_ref, dya_ref))):
            sl = slice(D_MODEL * j, D_MODEL * (j + 1))
            gt = gate_ref[:, sl].astype(F32)
            db = (dm * gt).astype(BF16)
            db_ref[...] = db
            dgate_ref[:, sl] = (dm * _branch(y_ref[...], w_ref) * gt * (1.0 - gt)).astype(BF16)
            cw = D_MODEL // N_CHIPS
            dy = _dot_nt(db[:, :cw], w_ref[0])
            for c in range(1, N_CHIPS):
                dy = dy + _dot_nt(db[:, cw * c:cw * (c + 1)], w_ref[c])
            dy_ref[...] = dy.astype(dy_ref.dtype)

    return _call(
        body, name="merge_bwd", grid=(T // tm,),
        in_specs=[_rows(tm, D_MODEL), _rows(tm, GATE_WIDTH), _rows(tm, POOL_WIDTH), _rows(tm, ATTN_WIDTH),
                  _const((D_MODEL, D_MODEL)), _const(w_bp.shape), _const(w_ba.shape)],
        out_specs=[_rows(tm, D_MODEL), _rows(tm, D_MODEL), _rows(tm, GATE_WIDTH), _rows(tm, POOL_WIDTH),
                   _rows(tm, ATTN_WIDTH)],
        out_shape=[_sds((T, D_MODEL), BF16), _sds((T, D_MODEL), BF16), _sds((T, GATE_WIDTH), BF16),
                   _sds((T, POOL_WIDTH), F32), _sds((T, ATTN_WIDTH), BF16)],
        args=(dmix, gate, y_pool, y_attn, w_out, w_bp, w_ba), sem=("parallel",), exchanges=exchanges)


def _mixers_bwd(q, k, v, do, sinks, tabs, dyp, diff, w_pool, pool_scale, seq, exchanges=()):
    T = q.shape[0]
    nb = seq // BLOCK
    bl = T // seq
    steps = nb + 1
    tp = T // nb
    nseq = seq // tp
    per = tp // HALO
    last_halo = T // HALO - 1

    def body(sink_ref, q_ref, do_ref, kp_ref, kc_ref, vp_ref, vc_ref, c_ref, a_ref, bt_ref, cp_ref, ap_ref, btp_ref,
             dy_ref, nxt_ref, diff_ref, w_ref, s_ref,
             dq_ref, dk_ref, dv_ref, dsink_ref, du_ref, dw_ref, ds_ref, ck_ref, cv_ref):
        n = pl.program_id(0)

        @pl.when(n == 0)
        def _():
            for ref in (dsink_ref, ck_ref, cv_ref, dw_ref, ds_ref):
                ref[...] = jnp.zeros_like(ref)

        @pl.when(n < nb)
        def _():
            _pool_bwd_tile(n, tp, nseq, dy_ref, nxt_ref, diff_ref, w_ref, s_ref, du_ref, dw_ref, ds_ref)
            valid, lo = _attn_masks(n)
            for b in range(bl):
                kk = jnp.concatenate([kp_ref[b], kc_ref[b]], axis=0)
                vv = jnp.concatenate([vp_ref[b], vc_ref[b]], axis=0)
                dk_acc = jnp.zeros((2 * BLOCK, KV_WIDTH), F32)
                dv_acc = jnp.zeros((2 * BLOCK, KV_WIDTH), F32)
                for h in range(2):
                    qs = _stack_heads(q_ref.at[b], h, lo)
                    dos = _stack_heads(do_ref.at[b], h, lo)
                    pr, ps = _group_probs(qs, kk, valid, _sink_rows(sink_ref, h))
                    dp = _dot_nt(dos, vv)
                    delta = jnp.sum(pr * dp, axis=1, keepdims=True)
                    ds = (pr * (dp - delta)).astype(BF16)
                    dsk = ps * delta
                    for g in range(GROUP):
                        idx = GROUP * h + g
                        dsink_ref[idx:idx + 1, :] += (jnp.zeros((1, LANES), F32)
                                                      - jnp.sum(dsk[BLOCK * g:BLOCK * (g + 1)]))
                    dk_acc = dk_acc + _dot_tn(ds, qs)
                    dv_acc = dv_acc + _dot_tn(pr.astype(BF16), dos)
                    for j, pair in enumerate(_unstack_heads(_dot(ds, kk) * SCALE, h, lo)):
                        sl = slice(LANES * (2 * h + j), LANES * (2 * h + j + 1))
                        dq_ref[b, :, sl] = _rot_bwd(pair, c_ref[...], a_ref[...], bt_ref[...]).astype(BF16)
                fin_k = ck_ref[b] + dk_acc[:BLOCK]
                dk_ref[b] = _rot_bwd(fin_k, cp_ref[...], ap_ref[...], btp_ref[...]).astype(BF16)
                dv_ref[b] = (cv_ref[b] + dv_acc[:BLOCK]).astype(BF16)
                ck_ref[b] = dk_acc[BLOCK:]
                cv_ref[b] = dv_acc[BLOCK:]

        @pl.when(n == nb)
        def _():
            for b in range(bl):
                dk_ref[b] = _rot_bwd(ck_ref[b], cp_ref[...], ap_ref[...], btp_ref[...]).astype(BF16)
                dv_ref[b] = cv_ref[b].astype(BF16)

    cur = lambda n: (0, jnp.minimum(n, nb - 1), 0)
    prv = lambda n: (0, jnp.clip(n - 1, 0, nb - 1), 0)
    tcur = lambda n: (jnp.minimum(n, nb - 1), 0)
    tprv = lambda n: (jnp.clip(n - 1, 0, nb - 1), 0)
    wide = lambda m: pl.BlockSpec((bl, BLOCK, ATTN_WIDTH), m)
    kv = lambda m: pl.BlockSpec((bl, BLOCK, KV_WIDTH), m)
    tab = lambda m: pl.BlockSpec((BLOCK, LANES), m)
    tile = lambda n: (jnp.minimum(n, nb - 1), 0)
    halo = lambda n: (jnp.minimum((jnp.minimum(n, nb - 1) + 1) * per, last_halo), 0)
    rows = pl.BlockSpec((tp, POOL_WIDTH), tile)
    res = _call(
        body, name="mixers_bwd", grid=(steps,),
        in_specs=[pl.BlockSpec(memory_space=pltpu.SMEM), wide(cur), wide(cur), kv(prv), kv(cur), kv(prv), kv(cur),
                  tab(tcur), tab(tcur), tab(tcur), tab(tprv), tab(tprv), tab(tprv),
                  rows, pl.BlockSpec((HALO, POOL_WIDTH), halo), rows, _const((4, POOL_GC, POOL_GC)),
                  _const((1, POOL_WIDTH))],
        out_specs=[wide(cur), kv(prv), kv(prv), _const((8, LANES)), rows, _const((4, POOL_GC, POOL_GC)),
                   _const((1, POOL_WIDTH))],
        out_shape=[_sds((bl, seq, ATTN_WIDTH), BF16), _sds((bl, seq, KV_WIDTH), BF16),
                   _sds((bl, seq, KV_WIDTH), BF16), _sds((8, LANES), F32), _sds((T, POOL_WIDTH), BF16),
                   _sds((4, POOL_GC, POOL_GC), F32), _sds((1, POOL_WIDTH), F32)],
        scratch=[pltpu.VMEM((bl, BLOCK, KV_WIDTH), F32), pltpu.VMEM((bl, BLOCK, KV_WIDTH), F32)],
        args=(sinks, *_by_example(bl, q, do, k, k, v, v), *tabs, *tabs, dyp, dyp, diff, w_pool, pool_scale),
        sem=("arbitrary",), exchanges=exchanges)
    outs, rest = (res if exchanges else (res, None))
    outs = [outs[0].reshape(T, ATTN_WIDTH), outs[1].reshape(T, KV_WIDTH), outs[2].reshape(T, KV_WIDTH), *outs[3:]]
    return (outs, rest) if exchanges else outs


def _pool_bwd_tile(i, tp, nseq, dy_ref, nxt_ref, diff_ref, w_ref, s_ref, du_ref, dw_ref, ds_ref):
    last = (i % nseq) == nseq - 1
    nxt = jnp.where(last, 0.0, nxt_ref[...])
    ext = jnp.concatenate([dy_ref[...], nxt], axis=0) * s_ref[...]
    pos = (i % nseq) * tp + lax.broadcasted_iota(jnp.int32, (tp + HALO, 1), 0)
    for gi, w in enumerate(POOL_WINDOWS):
        sl = slice(POOL_GC * gi, POOL_GC * (gi + 1))
        wg = w_ref[gi].astype(BF16)
        dmx = ext[:, sl].astype(BF16)
        ddiff = _dot_nt(dmx, wg)
        s = ddiff * _inv_count(pos, w)
        sh = 1
        while sh < w:
            s = s + pltpu.roll(s, tp + HALO - sh, 0)
            sh *= 2
        du_ref[:, sl] = (s[:tp] - ddiff[:tp]).astype(BF16)
        dg = diff_ref[:, sl]
        dw_ref[gi] += _dot_tn(dg, dmx[:tp])
        ds_ref[:, sl] += jnp.sum(dy_ref[:, sl] * _dot(dg, wg), axis=0, keepdims=True)


_PARTS = ((0, C_Q), (C_Q, C_K), (C_K, C_V), (C_V, C_G), (C_G, IN_WIDTH))


def _inproj_bwd(parts, x2, dx1, w_in_t, g1, exchanges=()):
    T = x2.shape[0]
    tm = min(TM, T)

    def body(du_ref, dq_ref, dk_ref, dv_ref, dgt_ref, x_ref, dx1_ref, w_ref, g_ref, gx_ref, dg_ref):
        @pl.when(pl.program_id(0) == 0)
        def _():
            dg_ref[...] = jnp.zeros_like(dg_ref)

        dh = jnp.zeros((tm, D_MODEL), F32)
        for (lo, hi), p_ref in zip(_PARTS, (du_ref, dq_ref, dk_ref, dv_ref, dgt_ref)):
            dh = dh + _dot(p_ref[...], w_ref[lo:hi, :])
        dx, dg = _norm_bwd(x_ref[...], g_ref[...], dh)
        gx_ref[...] = dx1_ref[...] + dx
        dg_ref[...] += dg

    return _call(
        body, name="inproj_bwd", grid=(T // tm,),
        in_specs=[_rows(tm, hi - lo) for lo, hi in _PARTS]
        + [_rows(tm, D_MODEL), _rows(tm, D_MODEL), _const((IN_WIDTH, D_MODEL)), _const((1, D_MODEL))],
        out_specs=[_rows(tm, D_MODEL), _const((1, D_MODEL))],
        out_shape=[_sds((T, D_MODEL), F32), _sds((1, D_MODEL), F32)],
        args=(*parts, x2, dx1, w_in_t, g1), sem=("arbitrary",), exchanges=exchanges)


def _dw_in(h, parts, exchanges=()):
    T = h.shape[0]
    tk = min(TM, T)

    def body(h_ref, du_ref, dq_ref, dk_ref, dv_ref, dgt_ref, o_ref, db_ref):
        @pl.when(pl.program_id(0) == 0)
        def _():
            o_ref[...] = jnp.zeros_like(o_ref)
            db_ref[...] = jnp.zeros_like(db_ref)

        hh = h_ref[...]
        ones = jnp.ones((8, tk), BF16)
        for (lo, hi), p_ref in zip(_PARTS, (du_ref, dq_ref, dk_ref, dv_ref, dgt_ref)):
            part = p_ref[...]
            o_ref[lo:hi, :] += _dot_tn(part, hh)
            db_ref[:, lo:hi] += _dot(ones, part)[:1]

    return _call(
        body, name="dw_in", grid=(T // tk,),
        in_specs=[_rows(tk, D_MODEL)] + [_rows(tk, hi - lo) for lo, hi in _PARTS],
        out_specs=[_const((IN_WIDTH, D_MODEL)), _const((1, IN_WIDTH))],
        out_shape=[_sds((IN_WIDTH, D_MODEL), F32), _sds((1, IN_WIDTH), F32)],
        args=(h, *parts), sem=("arbitrary",), exchanges=exchanges)


def _row_tile(rows, cap=256, mult=16):
    best = None
    for t in range(mult, min(rows, cap) + 1, mult):
        if rows % t == 0:
            best = t
    if best is None:
        raise ValueError("no row tile for %d rows" % rows)
    return best


def _pair_sum(ids, full, got):
    _, r, c = full.shape
    hr = r // 2
    tr = _row_tile(hr, cap=512)
    nblk = hr // tr

    def body(ids_ref, a_ref, b_ref, own_ref, sb_ref):
        s = a_ref[...] + b_ref[...]
        sb_ref[...] = s.astype(BF16)

        @pl.when(pl.program_id(1) == ids_ref[0])
        def _():
            own_ref[...] = s

    slab = pl.BlockSpec((None, tr, c), lambda i, j, ids_ref: (j, i, 0))
    return pl.pallas_call(
        body, name="pair_sum_%dx%d" % (r, c),
        grid_spec=pltpu.PrefetchScalarGridSpec(
            num_scalar_prefetch=1, grid=(nblk, N_CHIPS),
            in_specs=[pl.BlockSpec((None, tr, c), lambda i, j, ids_ref: (j, ids_ref[1] * nblk + i, 0)), slab],
            out_specs=[pl.BlockSpec((tr, c), lambda i, j, ids_ref: (i, 0)), slab]),
        out_shape=[_sds((hr, c), F32), _sds((N_CHIPS, hr, c), BF16)],
        compiler_params=_cp("parallel", "arbitrary"),
    )(ids, full, got)


def _pair_sum_small(ids, fulls, gots):
    n = len(fulls)
    dims = [(f.shape[1] // 2, f.shape[2]) for f in fulls]

    def body(ids_ref, *refs):
        ins, outs = refs[:2 * n], refs[2 * n:]
        for k in range(n):
            a, b = ins[2 * k], ins[2 * k + 1]
            outs[2 * k + 1][...] = (a[...] + b[...]).astype(BF16)
            outs[2 * k][...] = a[ids_ref[0]] + b[ids_ref[0]]

    in_specs, out_specs, out_shape = [], [], []
    for hr, c in dims:
        slabs = pl.BlockSpec((N_CHIPS, hr, c), lambda j, ids_ref: (0, 0, 0))
        in_specs += [pl.BlockSpec((N_CHIPS, hr, c), lambda j, ids_ref: (0, ids_ref[1], 0)), slabs]
        out_specs += [pl.BlockSpec((hr, c), lambda j, ids_ref: (0, 0)), slabs]
        out_shape += [_sds((hr, c), F32), _sds((N_CHIPS, hr, c), BF16)]
    res = pl.pallas_call(
        body, name="pair_sum_small",
        grid_spec=pltpu.PrefetchScalarGridSpec(num_scalar_prefetch=1, grid=(1,), in_specs=in_specs,
                                               out_specs=out_specs),
        out_shape=out_shape, compiler_params=_cp("arbitrary"),
    )(ids, *[a for pair in zip(fulls, gots) for a in pair])
    return [(res[2 * k], res[2 * k + 1]) for k in range(n)]


def _chip_sum_small(ids, owns, gots):
    n = len(owns)

    def body(ids_ref, *refs):
        ins, outs = refs[:2 * n], refs[2 * n:]
        for k in range(n):
            a, b = ins[2 * k], ins[2 * k + 1]
            outs[k][...] = ((a[...] + b[0].astype(F32)) + b[1].astype(F32)) + b[2].astype(F32)

    in_specs, out_specs, out_shape = [], [], []
    for own in owns:
        hr, c = own.shape
        in_specs += [pl.BlockSpec((hr, c), lambda i, ids_ref: (0, 0)),
                     pl.BlockSpec((3, hr, c), lambda i, ids_ref: (0, 0, 0))]
        out_specs.append(pl.BlockSpec((hr, c), lambda i, ids_ref: (ids_ref[1], 0)))
        out_shape.append(_sds((2 * hr, c), F32))
    return pl.pallas_call(
        body, name="chip_sum_small",
        grid_spec=pltpu.PrefetchScalarGridSpec(num_scalar_prefetch=1, grid=(1,), in_specs=in_specs,
                                               out_specs=out_specs),
        out_shape=out_shape, compiler_params=_cp("arbitrary"),
    )(ids, *[a for pair in zip(owns, gots) for a in pair])


def _chip_sum(ids, own, got):
    hr, c = own.shape
    tr = _row_tile(hr)
    nblk = hr // tr

    def body(ids_ref, a_ref, b_ref, o_ref):
        o_ref[...] = ((a_ref[...] + b_ref[0].astype(F32)) + b_ref[1].astype(F32)) + b_ref[2].astype(F32)

    return pl.pallas_call(
        body, name="chip_sum_%dx%d" % (hr, c),
        grid_spec=pltpu.PrefetchScalarGridSpec(
            num_scalar_prefetch=1, grid=(nblk,),
            in_specs=[pl.BlockSpec((tr, c), lambda i, ids_ref: (i, 0)),
                      pl.BlockSpec((3, tr, c), lambda i, ids_ref: (0, i, 0))],
            out_specs=pl.BlockSpec((tr, c), lambda i, ids_ref: (ids_ref[1] * nblk + i, 0))),
        out_shape=_sds((2 * hr, c), F32),
        compiler_params=_cp("parallel"),
    )(ids, own, got)


def _adamw_math(w, g, m, v):
    nm = ADAM_B1 * m + (1.0 - ADAM_B1) * g
    nv = ADAM_B2 * v + (1.0 - ADAM_B2) * (g * g)
    m_hat = nm / (1.0 - ADAM_B1 ** ADAM_STEP)
    v_hat = nv / (1.0 - ADAM_B2 ** ADAM_STEP)
    return -ADAM_LR * (m_hat / (jnp.sqrt(v_hat) + ADAM_EPS) + ADAM_WD * w), nm, nv


def _adamw(w, g, m, v):
    r, c = w.shape
    tr = _row_tile(r, cap=512, mult=8)

    def body(w_ref, g_ref, m_ref, v_ref, d_ref, nm_ref, nv_ref):
        d_ref[...], nm_ref[...], nv_ref[...] = _adamw_math(w_ref[...], g_ref[...], m_ref[...], v_ref[...])

    spec = _rows(tr, c)
    return pl.pallas_call(
        body, name="adamw_%dx%d" % (r, c), grid=(r // tr,),
        in_specs=[spec] * 4, out_specs=[spec] * 3, out_shape=[_sds((r, c), F32)] * 3,
        compiler_params=_cp("parallel"),
    )(w, g, m, v)


SC_TILES = 32
SC_LANES = 16
SC_ROWS = 8


def _adamw_sparse(w, g, m, v):
    r, c = w.shape
    rows = r // SC_TILES
    step = min(rows, SC_ROWS)

    def body(w_hbm, g_hbm, m_hbm, v_hbm, d_hbm, nm_hbm, nv_hbm, wb, gb, mb, vb):
        tile = lax.axis_index("sc_subcore") * 2 + lax.axis_index("sc_core")

        @pl.loop(0, rows, step=step)
        def _(r0):
            mine = pl.ds(tile * rows + r0, step)
            for src, dst in ((w_hbm, wb), (g_hbm, gb), (m_hbm, mb), (v_hbm, vb)):
                pltpu.sync_copy(src.at[mine], dst)

            @pl.loop(0, step)
            def _(row):
                @pl.loop(0, c, step=SC_LANES)
                def _(i):
                    at = (row, pl.ds(i, SC_LANES))
                    wb[at], mb[at], vb[at] = _adamw_math(wb[at], gb[at], mb[at], vb[at])

            for src, dst in ((wb, d_hbm), (mb, nm_hbm), (vb, nv_hbm)):
                pltpu.sync_copy(src, dst.at[mine])

    return pl.kernel(
        body, name="adamw_sparse_%dx%d" % (r, c), out_type=[_sds((r, c), F32)] * 3,
        mesh=plsc.VectorSubcoreMesh(core_axis_name="sc_core", subcore_axis_name="sc_subcore"),
        scratch_types=[pltpu.VMEM((step, c), F32)] * 4,
    )(w, g, m, v)


_SMALL_NAMES = ("w_pool", "b_in", "g_mix_pre", "g_mix_post", "g_mlp_pre", "g_mlp_post", "pool_scale", "attn_sinks")
B_ROWS = -(-IN_WIDTH // D_MODEL)


def _row_block(rows):
    rows = [jnp.pad(r.astype(F32), ((0, 0), (0, D_MODEL - r.shape[1]))) for r in rows]
    return jnp.pad(jnp.concatenate(rows, axis=0), ((0, 8 - len(rows)), (0, 0)))


def _early_block(dg2, dg3, dg4, dps, dsink, loss):
    tail = jnp.concatenate([jnp.pad(dsink.reshape(1, -1), ((0, 0), (0, LANES - dsink.size))),
                            jnp.pad(loss.reshape(1, 1), ((0, 0), (0, LANES - 1)))], axis=1)
    return _row_block([dg2, dg3, dg4, dps, tail])


def _late_block(db_in, dg1):
    b = jnp.pad(db_in, ((0, 0), (0, B_ROWS * D_MODEL - IN_WIDTH))).reshape(B_ROWS, D_MODEL)
    return _row_block([b[r:r + 1] for r in range(B_ROWS)] + [dg1])


def _small_update(gearly, gmat, glate, w, m, v):
    names = _SMALL_NAMES
    n = len(names)

    def total(ref, rows):
        acc = ref[0:rows, :]
        for d in range(1, N_DEV):
            acc = acc + ref[d * rows:(d + 1) * rows, :]
        return acc

    def body(*refs):
        early_ref, gmat_ref, late_ref = refs[:3]
        w_refs, m_refs, v_refs = refs[3:3 + n], refs[3 + n:3 + 2 * n], refs[3 + 2 * n:3 + 3 * n]
        outs = refs[3 + 3 * n:]
        loss_ref, g_refs, d_refs = outs[0], outs[1:1 + n], outs[1 + n:1 + 2 * n]
        nm_refs, nv_refs = outs[1 + 2 * n:1 + 3 * n], outs[1 + 3 * n:1 + 4 * n]
        early, late = total(early_ref, 8), total(late_ref, 8)
        loss_ref[...] = jnp.sum(early[4:5, LANES:2 * LANES], axis=1, keepdims=True)
        bias = jnp.concatenate([late[r:r + 1, :] for r in range(B_ROWS - 1)]
                               + [late[B_ROWS - 1:B_ROWS, :IN_WIDTH - (B_ROWS - 1) * D_MODEL]], axis=1)
        grad = dict(b_in=bias, g_mix_pre=late[B_ROWS:B_ROWS + 1, :], g_mix_post=early[0:1, :],
                    g_mlp_pre=early[1:2, :], g_mlp_post=early[2:3, :], pool_scale=early[3:4, :POOL_WIDTH],
                    attn_sinks=early[4:5, :N_Q_HEADS])
        for i, name in enumerate(names):
            g = total(gmat_ref, 4 * POOL_GC) if name == "w_pool" else grad[name]
            g_refs[i][...] = g
            d_refs[i][...], nm_refs[i][...], nv_refs[i][...] = _adamw_math(
                w_refs[i][...], g, m_refs[i][...], v_refs[i][...])

    shapes = [_sds(w[k].shape, F32) for k in names]
    res = pl.pallas_call(
        body, name="small_update", out_shape=[_sds((1, 1), F32)] + shapes * 4,
        compiler_params=pltpu.CompilerParams(vmem_limit_bytes=VMEM_MB * 1024 * 1024),
    )(gearly, gmat, glate, *[w[k] for k in names], *[m[k] for k in names], *[v[k] for k in names])
    loss = res[0]
    per = {k: tuple(res[1 + j * n + i] for j in range(4)) for i, k in enumerate(names)}
    return loss, per


_BIG = ("w_in", "w_branch_pool", "w_branch_attn", "w_out", "w_up", "w_down")
_ORDER = ("g_mix_pre", "w_in", "b_in", "w_pool", "pool_scale", "attn_sinks", "w_branch_pool", "w_branch_attn",
          "w_out", "g_mix_post", "g_mlp_pre", "w_up", "w_down", "g_mlp_post")


def _stack_rows(slab):
    return slab.reshape(-1, slab.shape[2])


def _step(x2, tgt, seq, shards, small, ids):
    tabs = _rope_tables(seq)
    g1, g2, g3, g4 = (small[n] for n in ("g_mix_pre", "g_mix_post", "g_mlp_pre", "g_mlp_post"))
    sinks = small["attn_sinks"].reshape(N_Q_HEADS)
    w_pool = small["w_pool"].reshape(4, POOL_GC, POOL_GC)
    pool_scale = small["pool_scale"]

    def whole(shard, slabs):
        return lax.dynamic_update_slice(slabs, shard[None], (ids[0], 0, 0))

    (up_a, up_b, down_a, down_b, *mix_shards), [[in_slab]] = _cast_shards(
        *(shards[n] for n in ("w_up", "w_down", "w_branch_pool", "w_branch_attn", "w_out")),
        exchanges=[_ex_gather([shards["w_in"]])])
    w_in = _stack_rows(whole(shards["w_in"], in_slab))
    (h, u, q, k, v, gate), [mix_slabs] = _inproj(
        x2, g1, w_in, small["b_in"], tabs, seq, exchanges=[_ex_gather(mix_shards)])
    w_bp, w_ba, out_slab = (whole(s, g) for s, g in zip(mix_shards, mix_slabs))
    w_out = _stack_rows(out_slab)
    (y_attn, diff, y_pool), [[got_a, got_b]] = _mixers_fwd(
        q, k, v, sinks, u, w_pool, pool_scale, seq, exchanges=[_ex_gather([up_a, up_b])])
    (merged, mix, x1, h2), [[got_c, got_d]] = _merge_out(
        y_pool, y_attn, gate, x2, w_bp, w_ba, w_out, g2, g3, exchanges=[_ex_gather([down_a, down_b])])
    w_up = (whole(up_a, got_a), whole(up_b, got_b))
    w_down = (whole(down_a, got_c), whole(down_b, got_d))
    act, dff, dup, dx1, dmix, loss_acc, dg4, dg3, dg2 = _mlp_core(h2, x1, mix, tgt, w_up, w_down, g4, g3, g2)

    dw_down = _dw("down", act, dff, 2048, 1024)[0].reshape(N_CHIPS, D_FF // N_CHIPS, D_MODEL)
    (dbp, dba, dgate, dyp, dya), [[got]] = _merge_bwd(
        dmix, gate, y_pool, y_attn, w_out, w_bp, w_ba, exchanges=[_ex_pair([dw_down])])
    ps_down = _pair_sum(ids, dw_down, got)
    (dw_up,), [[got]] = _dw("up", h2, dup, 1024, 1024, shard_cols=True, exchanges=[_ex_chip([ps_down[1]])])
    half_down = _chip_sum(ids, ps_down[0], got)
    (dw_out, dw_bp, dw_ba), [[got]] = _dw_mix(merged, dmix, y_pool, dbp, y_attn, dba, exchanges=[_ex_pair([dw_up])])
    ps_up = _pair_sum(ids, dw_up, got)
    dw_mix = [dw_out.reshape(N_CHIPS, D_MODEL // N_CHIPS, D_MODEL), dw_bp, dw_ba]
    (dq, dk, dv, dsink, du, dw_pool, dps), [[got], gots, [g_down]] = _mixers_bwd(
        q, k, v, dya, sinks, tabs, dyp, diff, w_pool, pool_scale, seq,
        exchanges=[_ex_chip([ps_up[1]]), _ex_pair(dw_mix), _ex_swap([half_down])])
    half_up = _chip_sum(ids, ps_up[0], got)
    ps_mix = _pair_sum_small(ids, dw_mix, gots)
    parts = (du, dq, dk, dv, dgate)
    early = _early_block(dg2, dg3, dg4, dps, dsink[:, 0], loss_acc[0, 0])
    mat = dw_pool.reshape(4 * POOL_GC, POOL_GC)
    (dw_in_t, db_in), [gots, [gearly, gmat], [g_up]] = _dw_in(
        h, parts, exchanges=[_ex_chip([p[1] for p in ps_mix]), _ex_allgather([early, mat]), _ex_swap([half_up])])
    half_mix = _chip_sum_small(ids, [p[0] for p in ps_mix], gots)
    dw_in = dw_in_t.reshape(N_CHIPS, IN_WIDTH // N_CHIPS, D_MODEL)
    g_mix, [got] = _alone("swap_mix_pair_in", _ex_swap(half_mix), _ex_pair([dw_in]))
    ps_in = _pair_sum(ids, dw_in, got)
    (gx, dg1), [[got]] = _inproj_bwd(parts, x2, dx1, w_in, g1, exchanges=[_ex_chip([ps_in[1]])])
    [g_in], [glate] = _alone("swap_in_allgather", _ex_swap([_chip_sum(ids, ps_in[0], got)]),
                             _ex_allgather([_late_block(db_in, dg1)]))

    grads = dict(w_in=g_in, w_branch_pool=g_mix[1], w_branch_attn=g_mix[2], w_out=g_mix[0], w_up=g_up, w_down=g_down)
    return (gearly, gmat, glate), gx, grads


def kernel(x, g_mix_pre, w_in, b_in, w_pool, pool_scale, attn_sinks, w_branch_pool, w_branch_attn, w_out, g_mix_post, g_mlp_pre, w_up, w_down, g_mlp_post, loss_target, m_g_mix_pre, m_w_in, m_b_in, m_w_pool, m_pool_scale, m_attn_sinks, m_w_branch_pool, m_w_branch_attn, m_w_out, m_g_mix_post, m_g_mlp_pre, m_w_up, m_w_down, m_g_mlp_post, v_g_mix_pre, v_w_in, v_b_in, v_w_pool, v_pool_scale, v_attn_sinks, v_w_branch_pool, v_w_branch_attn, v_w_out, v_g_mix_post, v_g_mlp_pre, v_w_up, v_w_down, v_g_mlp_post):
    weights = dict(g_mix_pre=g_mix_pre, w_in=w_in, b_in=b_in, w_pool=w_pool, pool_scale=pool_scale,
                   attn_sinks=attn_sinks, w_branch_pool=w_branch_pool, w_branch_attn=w_branch_attn, w_out=w_out,
                   g_mix_post=g_mix_post, g_mlp_pre=g_mlp_pre, w_up=w_up, w_down=w_down, g_mlp_post=g_mlp_post)
    mom1 = dict(g_mix_pre=m_g_mix_pre, w_in=m_w_in, b_in=m_b_in, w_pool=m_w_pool, pool_scale=m_pool_scale,
                attn_sinks=m_attn_sinks, w_branch_pool=m_w_branch_pool, w_branch_attn=m_w_branch_attn,
                w_out=m_w_out, g_mix_post=m_g_mix_post, g_mlp_pre=m_g_mlp_pre, w_up=m_w_up, w_down=m_w_down,
                g_mlp_post=m_g_mlp_post)
    mom2 = dict(g_mix_pre=v_g_mix_pre, w_in=v_w_in, b_in=v_b_in, w_pool=v_w_pool, pool_scale=v_pool_scale,
                attn_sinks=v_attn_sinks, w_branch_pool=v_w_branch_pool, w_branch_attn=v_w_branch_attn,
                w_out=v_w_out, g_mix_post=v_g_mix_post, g_mlp_pre=v_g_mlp_pre, w_up=v_w_up, w_down=v_w_down,
                g_mlp_post=v_g_mlp_post)
    b_loc, seq, _ = x.shape
    x2 = x.reshape(b_loc * seq, D_MODEL)
    tgt = loss_target.reshape(b_loc * seq, D_MODEL)
    ids = jnp.stack([2 * lax.axis_index("x") + lax.axis_index("y"), lax.axis_index("c")]).astype(jnp.int32)

    def flat(n, a):
        return a[0].T if n == "w_in" else a[0]

    def unflat(n, a):
        return (a.T if n == "w_in" else a)[None]

    shards = {n: flat(n, weights[n]).astype(BF16) if n == "w_in" else flat(n, weights[n]) for n in _BIG}
    small = {n: weights[n] for n in _ORDER if n not in _BIG}
    (gearly, gmat, glate), gx, grads = _step(x2, tgt, seq, shards, small, ids)

    def two_d(src):
        return {n: src[n].reshape(4 * POOL_GC, POOL_GC) if n == "w_pool" else src[n] for n in _SMALL_NAMES}

    loss, per = _small_update(gearly, gmat, glate, two_d(weights), two_d(mom1), two_d(mom2))
    delta, new_m, new_v = {}, {}, {}
    for n in _SMALL_NAMES:
        grads[n], delta[n], new_m[n], new_v[n] = (a.reshape(weights[n].shape) for a in per[n])
    for n in _BIG:
        update = _adamw if n == "w_in" else _adamw_sparse
        d, nm, nv = update(flat(n, weights[n]), grads[n], flat(n, mom1[n]), flat(n, mom2[n]))
        grads[n] = unflat(n, grads[n])
        delta[n], new_m[n], new_v[n] = unflat(n, d), unflat(n, nm), unflat(n, nv)

    return (loss[0, 0], gx.reshape(x.shape), *[grads[n] for n in _ORDER], *[delta[n] for n in _ORDER],
            *[new_m[n] for n in _ORDER], *[new_v[n] for n in _ORDER])
```

```python
import jax
import jax.numpy as jnp
from jax import lax
from jax.experimental import pallas as pl
from jax.experimental.pallas import tpu as pltpu
from jax.experimental.pallas import tpu_sc as plsc

F32 = jnp.float32
BF16 = jnp.bfloat16

D_MODEL = 1024
POOL_WINDOWS = (2, 4, 8, 16)
POOL_WIDTH = 512
POOL_GC = 128
HALO = 16
HEAD_DIM = 64
N_Q_HEADS = 8
ATTN_WIDTH = 512
KV_WIDTH = 128
BLOCK = 128
NEG_INF = -1e30
ROPE_THETA = 500000.0
ROT_DIM = 16
GATE_WIDTH = 2048
IN_WIDTH = 3328
D_FF = 4096
EPS = 1e-6
SCALE = HEAD_DIM ** -0.5
C_Q, C_K, C_V, C_G = 512, 1024, 1152, 1280

ADAM_LR, ADAM_B1, ADAM_B2, ADAM_EPS, ADAM_WD, ADAM_STEP = 0.001, 0.9, 0.999, 1e-08, 0.01, 10

N_CHIPS = 4
N_DEV = 8
LANES = 128
TM = 512
VMEM_MB = 56

MESH = pl.DeviceIdType.MESH
ANY = pl.BlockSpec(memory_space=pl.ANY)


def _cp(*sem, vmem=VMEM_MB):
    return pltpu.CompilerParams(dimension_semantics=sem, vmem_limit_bytes=vmem * 1024 * 1024)


def _rows(tile, cols):
    return pl.BlockSpec((tile, cols), lambda i: (i, 0))


def _const(shape):
    nd = len(shape)
    return pl.BlockSpec(shape, lambda i: (0,) * nd)


def _sds(shape, dtype):
    return jax.ShapeDtypeStruct(shape, dtype)


def _dot(a, b):
    return jnp.dot(a, b, preferred_element_type=F32)


def _dot_nt(a, b):
    return lax.dot_general(a, b, (((1,), (1,)), ((), ())), preferred_element_type=F32)


def _dot_tn(a, b):
    return lax.dot_general(a, b, (((0,), (0,)), ((), ())), preferred_element_type=F32)


def _rms(x):
    return lax.rsqrt(jnp.mean(x * x, axis=-1, keepdims=True) + EPS)


def _norm_bwd(x, g, dout):
    r = _rms(x)
    n = x * r
    dn = dout * g
    dx = r * (dn - n * jnp.mean(dn * n, axis=-1, keepdims=True))
    return dx, jnp.sum(dout * n, axis=0, keepdims=True)


def _rot_fwd(t, c, a, bt):
    return t * c + pltpu.roll(t, LANES - 8, 1) * a + pltpu.roll(t, 8, 1) * bt


def _rot_bwd(d, c, a, bt):
    return d * c + pltpu.roll(d * a, 8, 1) + pltpu.roll(d * bt, LANES - 8, 1)


def _rope_tables(seq):
    pos = jnp.arange(seq, dtype=F32)
    inv_freq = ROPE_THETA ** (-jnp.arange(0, ROT_DIM, 2, dtype=F32) / ROT_DIM)
    ang = pos[:, None] * inv_freq[None, :]
    cos, sin = jnp.cos(ang), jnp.sin(ang)
    ones = jnp.ones((seq, HEAD_DIM - ROT_DIM), F32)
    zeros8 = jnp.zeros((seq, 8), F32)
    zrest = jnp.zeros((seq, HEAD_DIM - ROT_DIM), F32)
    c = jnp.concatenate([cos, cos, ones], axis=1)
    a = jnp.concatenate([-sin, zeros8, zrest], axis=1)
    bt = jnp.concatenate([zeros8, sin, zrest], axis=1)
    return tuple(jnp.tile(t, (1, 2)) for t in (c, a, bt))


class _Exchange:
    def __init__(self, inputs, out_shapes, sems, start, finish, aliases=None, middle=None):
        self.inputs, self.out_shapes, self.sems = list(inputs), list(out_shapes), list(sems)
        self.start, self.finish, self.aliases = start, finish, dict(aliases or {})
        self.middle = middle


def _call(body, *, name, grid, in_specs, out_specs, out_shape, args, scratch=(), sem=(), exchanges=()):
    in_specs, out_specs, out_shape, scratch = list(in_specs), list(out_specs), list(out_shape), list(scratch)
    if not exchanges:
        return pl.pallas_call(body, name=name, grid=grid, in_specs=in_specs, out_specs=out_specs,
                              out_shape=out_shape, scratch_shapes=scratch, compiler_params=_cp(*sem))(*args)
    n_in, n_out, n_scr = len(in_specs), len(out_specs), len(scratch)
    x_in = [a for ex in exchanges for a in ex.inputs]
    x_out = [s for ex in exchanges for s in ex.out_shapes]
    x_sem = [s for ex in exchanges for s in ex.sems]
    aliases, i_off, o_off = {}, n_in, n_out
    for ex in exchanges:
        for i, o in ex.aliases.items():
            aliases[i_off + i] = o_off + o
        i_off += len(ex.inputs)
        o_off += len(ex.out_shapes)

    def split(flat):
        out, pos = [], 0
        for ex, n in zip(exchanges, flat[1]):
            out.append(flat[0][pos:pos + n])
            pos += n
        return out

    def carrier(*refs):
        pos = 0
        groups = []
        for n in (n_in, len(x_in), n_out, len(x_out), n_scr, len(x_sem)):
            groups.append(refs[pos:pos + n])
            pos += n
        ins, xin, outs, xout, scr, xsem = groups
        xin = split((xin, [len(ex.inputs) for ex in exchanges]))
        xout = split((xout, [len(ex.out_shapes) for ex in exchanges]))
        xsem = split((xsem, [len(ex.sems) for ex in exchanges]))
        first = pl.program_id(0) == 0
        last = pl.program_id(0) == grid[0] - 1
        for d in range(1, len(grid)):
            first = jnp.logical_and(first, pl.program_id(d) == 0)
            last = jnp.logical_and(last, pl.program_id(d) == grid[d] - 1)

        @pl.when(first)
        def _():
            for ex, i, o, s in zip(exchanges, xin, xout, xsem):
                ex.start(i, o, s)

        if any(ex.middle for ex in exchanges):
            half = pl.program_id(0) == 5 * grid[0] // 8
            for d in range(1, len(grid)):
                half = jnp.logical_and(half, pl.program_id(d) == 0)

            @pl.when(half)
            def _():
                for ex, i, o, s in zip(exchanges, xin, xout, xsem):
                    if ex.middle:
                        ex.middle(i, o, s)

        body(*ins, *outs, *scr)

        @pl.when(last)
        def _():
            for ex, i, o, s in zip(exchanges, xin, xout, xsem):
                ex.finish(i, o, s)

    res = pl.pallas_call(
        carrier, name=name, grid=grid, in_specs=in_specs + [ANY] * len(x_in),
        out_specs=out_specs + [ANY] * len(x_out), out_shape=out_shape + x_out,
        scratch_shapes=scratch + x_sem, input_output_aliases=aliases,
        compiler_params=_cp(*(["arbitrary"] * len(grid))),
    )(*args, *x_in)
    return res[:n_out], split((res[n_out:], [len(ex.out_shapes) for ex in exchanges]))


def _alone(name, *exchanges):
    n_in = [len(ex.inputs) for ex in exchanges]
    n_out = [len(ex.out_shapes) for ex in exchanges]
    n_sem = [len(ex.sems) for ex in exchanges]
    aliases, i_off, o_off = {}, 0, 0
    for ex in exchanges:
        for i, o in ex.aliases.items():
            aliases[i_off + i] = o_off + o
        i_off += len(ex.inputs)
        o_off += len(ex.out_shapes)

    def split(flat, counts):
        out, pos = [], 0
        for n in counts:
            out.append(flat[pos:pos + n])
            pos += n
        return out

    def body(*refs):
        ins, outs, sems = split(refs, [sum(n_in), sum(n_out), sum(n_sem)])
        groups = list(zip(exchanges, split(ins, n_in), split(outs, n_out), split(sems, n_sem)))
        for ex, i, o, s in groups:
            ex.start(i, o, s)
        for ex, i, o, s in groups:
            if ex.middle:
                ex.middle(i, o, s)
        for ex, i, o, s in groups:
            ex.finish(i, o, s)

    res = pl.pallas_call(
        body, name=name, in_specs=[ANY] * sum(n_in), out_specs=[ANY] * sum(n_out),
        out_shape=[s for ex in exchanges for s in ex.out_shapes],
        scratch_shapes=[s for ex in exchanges for s in ex.sems], input_output_aliases=aliases,
    )(*[a for ex in exchanges for a in ex.inputs])
    return split(res, n_out)


def _place():
    x, y, c = lax.axis_index("x"), lax.axis_index("y"), lax.axis_index("c")
    chips = [(1 - x, y), (x, 1 - y), (1 - x, 1 - y)]
    return x, y, c, chips


def _remote(src, dst, send, recv, to):
    return pltpu.make_async_remote_copy(src_ref=src, dst_ref=dst, send_sem=send, recv_sem=recv,
                                        device_id=to, device_id_type=MESH)


def _ex_gather(shards):
    nw = len(shards)
    hrs = [s.shape[0] // 2 for s in shards]

    def copies(ins, outs, sems):
        s0, r0, s1, r1, s2, r2, fs, fr = sems
        x, y, c, _ = _place()
        me, xn, yn, dg = (x, y), (1 - x, y), (x, 1 - y), (1 - x, 1 - y)
        nbr = (xn, yn)
        sibling = (x, y, 1 - c)

        def piece(w, chip, core, part=None):
            hr = hrs[w]
            rows = pl.ds(core * hr, hr) if part is None else pl.ds(core * hr + part * (hr // 2), hr // 2)
            return outs[w].at[2 * chip[0] + chip[1], rows]

        def first(w, k, lead):
            part = k if lead else 1 - k
            send, recv = (s0, r0) if lead else (s1, r1)
            rows = pl.ds(c * hrs[w] + part * (hrs[w] // 2), hrs[w] // 2)
            return _remote(ins[w].at[rows], piece(w, me, c, part), send.at[w, k], recv.at[w, k], (*nbr[k], c))

        def landed(w, k, lead):
            part = k if lead else 1 - k
            send, recv = (s0, r0) if lead else (s1, r1)
            return _remote(piece(w, nbr[k], c, part), piece(w, nbr[k], c, part), send.at[w, k], recv.at[w, k],
                           (*nbr[k], c))

        def onward(w, k):
            return _remote(piece(w, nbr[k], c, k), piece(w, nbr[k], c, k), s2.at[w, k], r2.at[w, k],
                           (*nbr[1 - k], c))

        def arrived(w, k):
            return _remote(piece(w, dg, c, k), piece(w, dg, c, k), s2.at[w, k], r2.at[w, k], (*nbr[1 - k], c))

        def passed(w, j):
            chip = (xn, yn, dg)[j]
            return _remote(piece(w, chip, c), piece(w, chip, c), fs.at[w, j], fr.at[w, j], sibling)

        def handed(w, j):
            chip = (xn, yn, dg)[j]
            return _remote(piece(w, chip, 1 - c), piece(w, chip, 1 - c), fs.at[w, j], fr.at[w, j], sibling)

        return first, landed, onward, arrived, passed, handed

    def start(ins, outs, sems):
        first = copies(ins, outs, sems)[0]
        for lead in (True, False):
            for w in range(nw):
                for k in range(2):
                    first(w, k, lead).start()

    def middle(ins, outs, sems):
        _, landed, onward, _, passed, _ = copies(ins, outs, sems)
        for w in range(nw):
            for k in range(2):
                landed(w, k, True).wait_recv()
                onward(w, k).start()
        for w in range(nw):
            for k in range(2):
                landed(w, k, False).wait_recv()
                passed(w, k).start()

    def finish(ins, outs, sems):
        first, _, onward, arrived, passed, handed = copies(ins, outs, sems)
        for w in range(nw):
            for k in range(2):
                arrived(w, k).wait_recv()
            passed(w, 2).start()
        for w in range(nw):
            for j in range(3):
                handed(w, j).wait_recv()
        for w in range(nw):
            for k in range(2):
                first(w, k, True).wait_send()
                first(w, k, False).wait_send()
                onward(w, k).wait_send()
            for j in range(3):
                passed(w, j).wait_send()

    return _Exchange(shards, [_sds((N_CHIPS,) + s.shape, s.dtype) for s in shards],
                     [pltpu.SemaphoreType.DMA((nw, 2))] * 6 + [pltpu.SemaphoreType.DMA((nw, 3))] * 2,
                     start, finish, middle=middle)


def _ex_pair(grads):
    nw = len(grads)

    def copies(ins, outs, sems):
        x, y, c, _ = _place()
        out = []
        for w in range(nw):
            hr = grads[w].shape[1] // 2
            out.append(_remote(ins[w].at[:, pl.ds((1 - c) * hr, hr)], outs[w], sems[0].at[w], sems[1].at[w],
                               (x, y, 1 - c)))
        return out

    def start(ins, outs, sems):
        for cp in copies(ins, outs, sems):
            cp.start()

    def finish(ins, outs, sems):
        for cp in copies(ins, outs, sems):
            cp.wait()

    return _Exchange(grads, [_sds((N_CHIPS, g.shape[1] // 2, g.shape[2]), F32) for g in grads],
                     [pltpu.SemaphoreType.DMA((nw,))] * 2, start, finish)


def _ex_chip(pieces):
    nw = len(pieces)

    def copies(ins, outs, sems):
        x, y, c, chips = _place()
        return [_remote(ins[w].at[2 * cx + cy], outs[w].at[k], sems[0].at[w, k], sems[1].at[w, k], (cx, cy, c))
                for w in range(nw) for k, (cx, cy) in enumerate(chips)]

    def start(ins, outs, sems):
        for cp in copies(ins, outs, sems):
            cp.start()

    def finish(ins, outs, sems):
        for cp in copies(ins, outs, sems):
            cp.wait()

    return _Exchange(pieces, [_sds((3,) + p.shape[1:], BF16) for p in pieces],
                     [pltpu.SemaphoreType.DMA((nw, 3))] * 2, start, finish)


def _ex_swap(fulls):
    nw = len(fulls)

    def start(ins, outs, sems):
        x, y, c, _ = _place()
        for w in range(nw):
            hr = fulls[w].shape[0] // 2
            mine = pl.ds(c * hr, hr)
            _remote(ins[w].at[mine], outs[w].at[mine], sems[0].at[w], sems[1].at[w], (x, y, 1 - c)).start()

    def finish(ins, outs, sems):
        x, y, c, _ = _place()
        for w in range(nw):
            hr = fulls[w].shape[0] // 2
            mine, theirs = pl.ds(c * hr, hr), pl.ds((1 - c) * hr, hr)
            _remote(ins[w].at[mine], outs[w].at[mine], sems[0].at[w], sems[1].at[w], (x, y, 1 - c)).wait_send()
            _remote(ins[w].at[theirs], outs[w].at[theirs], sems[0].at[w], sems[1].at[w], (x, y, 1 - c)).wait_recv()

    return _Exchange(fulls, [_sds(f.shape, F32) for f in fulls], [pltpu.SemaphoreType.DMA((nw,))] * 2,
                     start, finish, aliases={w: w for w in range(nw)})


def _ex_allgather(blocks):
    nb = len(blocks)

    def copies(ins, outs, sems):
        send, recv, lsem = sems
        x, y, c, chips = _place()
        me, sibling = (x, y, c), (x, y, 1 - c)

        def rows(b, px, py, pc):
            m_per = blocks[b].shape[0]
            return outs[b].at[pl.ds((4 * px + 2 * py + pc) * m_per, m_per), :]

        def copy(b, k, blk, to, src=None):
            return _remote(rows(b, *blk) if src is None else src, rows(b, *blk), send.at[b, k], recv.at[b, k], to)

        def mine(b):
            return pltpu.make_async_copy(ins[b], rows(b, *me), lsem.at[b])

        def first(b, k):
            return copy(b, k, me, sibling if k == 0 else (*chips[k - 1], c), src=ins[b])

        def passed(b, j):
            return copy(b, 4 + j, (*chips[j], c), sibling)

        def landed(b, j):
            return copy(b, 1 + j, (*chips[j], c), me)

        def handed(b, k):
            return copy(b, 0, sibling, me) if k == 0 else copy(b, 3 + k, (*chips[k - 1], 1 - c), me)

        return mine, first, passed, landed, handed

    def start(ins, outs, sems):
        mine, first, _, _, _ = copies(ins, outs, sems)
        for b in range(nb):
            mine(b).start()
            for k in range(4):
                first(b, k).start()

    def finish(ins, outs, sems):
        mine, first, passed, landed, handed = copies(ins, outs, sems)
        sent = []
        for b in range(nb):
            for j in range(3):
                landed(b, j).wait_recv()
                cp = passed(b, j)
                cp.start()
                sent.append(cp)
        for b in range(nb):
            for k in range(4):
                handed(b, k).wait_recv()
            for k in range(4):
                first(b, k).wait_send()
        for cp in sent:
            cp.wait_send()
        for b in range(nb):
            mine(b).wait()

    return _Exchange(blocks, [_sds((N_DEV * b.shape[0], b.shape[1]), F32) for b in blocks],
                     [pltpu.SemaphoreType.DMA((nb, 7)), pltpu.SemaphoreType.DMA((nb, 7)), pltpu.SemaphoreType.DMA((nb,))],
                     start, finish)


def _cast_shards(w_up, w_down, w_bp, w_ba, w_out, exchanges=()):
    r, c = w_up.shape
    tr = HALF // 2
    steps = r // tr

    def body(up_ref, down_ref, bp_ref, ba_ref, out_ref, ua_ref, ub_ref, da_ref, db_ref, bpo_ref, bao_ref, outo_ref):
        i = pl.program_id(0)

        @pl.when(i == 0)
        def _():
            for src, dst in ((bp_ref, bpo_ref), (ba_ref, bao_ref), (out_ref, outo_ref)):
                dst[...] = src[...].astype(BF16)

        @pl.when(i < steps // 2)
        def _():
            ua_ref[...] = up_ref[...].astype(BF16)
            da_ref[...] = down_ref[...].astype(BF16)

        @pl.when(i >= steps // 2)
        def _():
            ub_ref[...] = up_ref[...].astype(BF16)
            db_ref[...] = down_ref[...].astype(BF16)

    rows = _rows(tr, c)
    first = pl.BlockSpec((tr, c), lambda i: (jnp.minimum(i, steps // 2 - 1), 0))
    second = pl.BlockSpec((tr, c), lambda i: (jnp.maximum(i - steps // 2, 0), 0))
    half = _sds((HALF, c), BF16)
    return _call(
        body, name="cast_shards", grid=(steps,),
        in_specs=[rows, rows, _const(w_bp.shape), _const(w_ba.shape), _const(w_out.shape)],
        out_specs=[first, second, first, second, _const(w_bp.shape), _const(w_ba.shape), _const(w_out.shape)],
        out_shape=[half, half, half, half, _sds(w_bp.shape, BF16), _sds(w_ba.shape, BF16), _sds(w_out.shape, BF16)],
        args=(w_up, w_down, w_bp, w_ba, w_out), sem=("arbitrary",), exchanges=exchanges)


def _inproj(x2, g1, w_in_t, b_in, tabs, seq, exchanges=()):
    T = x2.shape[0]
    tm = min(TM, seq)
    nseq = seq // tm

    def body(x_ref, g_ref, w_ref, b_ref, c_ref, a_ref, bt_ref, h_ref, u_ref, q_ref, k_ref, v_ref, gate_ref):
        x = x_ref[...]
        h = (x * _rms(x) * g_ref[...]).astype(BF16)
        h_ref[...] = h

        def proj(lo, hi):
            return _dot_nt(h, w_ref[lo:hi, :]) + b_ref[:, lo:hi]

        c, a, bt = c_ref[...], a_ref[...], bt_ref[...]
        u_ref[...] = proj(0, C_Q)
        q = proj(C_Q, C_K)
        for p in range(4):
            sl = slice(LANES * p, LANES * (p + 1))
            q_ref[:, sl] = (_rot_fwd(q[:, sl], c, a, bt) * SCALE).astype(BF16)
        kv = proj(C_K, C_G)
        k_ref[...] = _rot_fwd(kv[:, :KV_WIDTH], c, a, bt).astype(BF16)
        v_ref[...] = kv[:, KV_WIDTH:].astype(BF16)
        for j in range(2):
            lo = C_G + D_MODEL * j
            gate_ref[:, D_MODEL * j:D_MODEL * (j + 1)] = jax.nn.sigmoid(proj(lo, lo + D_MODEL)).astype(BF16)

    tab = pl.BlockSpec((tm, LANES), lambda i: (i % nseq, 0))
    return _call(
        body, name="inproj", grid=(T // tm,),
        in_specs=[_rows(tm, D_MODEL), _const((1, D_MODEL)), _const((IN_WIDTH, D_MODEL)), _const((1, IN_WIDTH)),
                  tab, tab, tab],
        out_specs=[_rows(tm, D_MODEL), _rows(tm, POOL_WIDTH), _rows(tm, ATTN_WIDTH), _rows(tm, KV_WIDTH),
                   _rows(tm, KV_WIDTH), _rows(tm, GATE_WIDTH)],
        out_shape=[_sds((T, D_MODEL), BF16), _sds((T, POOL_WIDTH), F32), _sds((T, ATTN_WIDTH), BF16),
                   _sds((T, KV_WIDTH), BF16), _sds((T, KV_WIDTH), BF16), _sds((T, GATE_WIDTH), BF16)],
        args=(x2, g1, w_in_t, b_in, *tabs), sem=("parallel",), exchanges=exchanges)


def _inv_count(pos, w):
    return 1.0 / jnp.minimum(pos + 1, w).astype(F32)


def _pool_tile(i, tp, nseq, u_ref, prev_ref, w_ref, s_ref, diff_ref, y_ref):
    first = (i % nseq) == 0
    prev = jnp.where(first, 0.0, prev_ref[...])
    ext = jnp.concatenate([prev, u_ref[...]], axis=0)
    pos = (i % nseq) * tp + lax.broadcasted_iota(jnp.int32, (tp, 1), 0)
    for gi, w in enumerate(POOL_WINDOWS):
        sl = slice(POOL_GC * gi, POOL_GC * (gi + 1))
        xg = ext[:, sl]
        s = xg
        sh = 1
        while sh < w:
            s = s + pltpu.roll(s, sh, 0)
            sh *= 2
        pooled = s[HALO:] * _inv_count(pos, w)
        diff = (pooled - xg[HALO:]).astype(BF16)
        diff_ref[:, sl] = diff
        mixed = _dot(diff, w_ref[gi].astype(BF16))
        y_ref[:, sl] = (mixed * s_ref[:, sl]).astype(BF16)


def _pool_specs(tp):
    per = tp // HALO
    return [_rows(tp, POOL_WIDTH), pl.BlockSpec((HALO, POOL_WIDTH), lambda i: (jnp.maximum(i * per - 1, 0), 0)),
            _const((4, POOL_GC, POOL_GC)), _const((1, POOL_WIDTH))]


GROUP = 4
GROWS = GROUP * BLOCK


def _attn_masks(n):
    qi = lax.broadcasted_iota(jnp.int32, (GROWS, 2 * BLOCK), 0) % BLOCK
    kj = lax.broadcasted_iota(jnp.int32, (GROWS, 2 * BLOCK), 1)
    rel = qi + BLOCK - kj
    valid = (rel >= 0) & (rel < BLOCK) & (kj >= jnp.where(n > 0, 0, BLOCK))
    lo = lax.broadcasted_iota(jnp.int32, (BLOCK, LANES), 1) < HEAD_DIM
    return valid, lo


def _by_example(bl, *arrays):
    return [a.reshape(bl, a.shape[0] // bl, a.shape[1]) for a in arrays]


def _stack_heads(ref, h, lo):
    keep = lo if h == 0 else jnp.logical_not(lo)
    pieces = []
    for p in (2 * h, 2 * h + 1):
        xp = ref[:, LANES * p:LANES * (p + 1)].astype(F32)
        for e in range(2):
            t = xp if e == h else pltpu.roll(xp, HEAD_DIM, 1)
            pieces.append(jnp.where(keep, t, 0.0).astype(BF16))
    return jnp.concatenate(pieces, axis=0)


def _unstack_heads(stacked, h, lo):
    pairs = []
    for j in range(2):
        parts = []
        for e in range(2):
            t = stacked[BLOCK * (2 * j + e):BLOCK * (2 * j + e + 1)]
            parts.append(t if e == h else pltpu.roll(t, HEAD_DIM, 1))
        pairs.append(jnp.where(lo, parts[0], parts[1]))
    return pairs


def _sink_rows(sink_ref, h):
    head = lax.broadcasted_iota(jnp.int32, (GROWS, 1), 0) // BLOCK
    col = jnp.zeros((GROWS, 1), F32) + sink_ref[GROUP * h]
    for g in range(1, GROUP):
        col = jnp.where(head == g, sink_ref[GROUP * h + g], col)
    return col


def _group_probs(qs, kk, valid, sink):
    s = jnp.where(valid, _dot_nt(qs, kk), NEG_INF)
    m = jnp.maximum(jnp.max(s, axis=1, keepdims=True), sink)
    ex = jnp.exp(s - m)
    es = jnp.exp(sink - m)
    inv = 1.0 / (jnp.sum(ex, axis=1, keepdims=True) + es)
    return ex * inv, es * inv


def _mixers_fwd(q, k, v, sinks, u, w_pool, pool_scale, seq, exchanges=()):
    T = q.shape[0]
    nb = seq // BLOCK
    bl = T // seq
    tp = T // nb
    nseq = seq // tp

    def body(sink_ref, q_ref, kp_ref, kc_ref, vp_ref, vc_ref, u_ref, prev_ref, w_ref, s_ref, o_ref, diff_ref, y_ref):
        n = pl.program_id(0)
        valid, lo = _attn_masks(n)
        for b in range(bl):
            kk = jnp.concatenate([kp_ref[b], kc_ref[b]], axis=0)
            vv = jnp.concatenate([vp_ref[b], vc_ref[b]], axis=0)
            for h in range(2):
                qs = _stack_heads(q_ref.at[b], h, lo)
                pr, _ = _group_probs(qs, kk, valid, _sink_rows(sink_ref, h))
                o = _dot(pr.astype(BF16), vv)
                for j, pair in enumerate(_unstack_heads(o, h, lo)):
                    p = 2 * h + j
                    o_ref[b, :, LANES * p:LANES * (p + 1)] = pair.astype(BF16)
        _pool_tile(n, tp, nseq, u_ref, prev_ref, w_ref, s_ref, diff_ref, y_ref)

    cur = lambda n: (0, n, 0)
    prv = lambda n: (0, jnp.maximum(n - 1, 0), 0)
    kv = lambda m: pl.BlockSpec((bl, BLOCK, KV_WIDTH), m)
    res = _call(
        body, name="mixers_fwd", grid=(nb,),
        in_specs=[pl.BlockSpec(memory_space=pltpu.SMEM), pl.BlockSpec((bl, BLOCK, ATTN_WIDTH), cur),
                  kv(prv), kv(cur), kv(prv), kv(cur)] + _pool_specs(tp),
        out_specs=[pl.BlockSpec((bl, BLOCK, ATTN_WIDTH), cur), _rows(tp, POOL_WIDTH), _rows(tp, POOL_WIDTH)],
        out_shape=[_sds((bl, seq, ATTN_WIDTH), BF16), _sds((T, POOL_WIDTH), BF16), _sds((T, POOL_WIDTH), BF16)],
        args=(sinks, *_by_example(bl, q, k, k, v, v), u, u, w_pool, pool_scale), sem=("parallel",),
        exchanges=exchanges)
    outs, rest = res if exchanges else (res, None)
    return [outs[0].reshape(T, ATTN_WIDTH), outs[1], outs[2]], rest


def _branch(y, w_ref):
    return jnp.concatenate([_dot(y, w_ref[j]) for j in range(N_CHIPS)], axis=1)


def _merge_out(y_pool, y_attn, gate, x2, w_bp, w_ba, w_out, g2, g3, exchanges=()):
    T = x2.shape[0]
    tm = min(TM, T)

    def body(yp_ref, ya_ref, gate_ref, x_ref, wbp_ref, wba_ref, wo_ref, g2_ref, g3_ref,
             mg_ref, mix_ref, x1_ref, h2_ref):
        bp, ba = _branch(yp_ref[...], wbp_ref), _branch(ya_ref[...], wba_ref)
        merged = (gate_ref[:, :D_MODEL].astype(F32) * bp + gate_ref[:, D_MODEL:].astype(F32) * ba).astype(BF16)
        mg_ref[...] = merged
        mix = _dot(merged, wo_ref[...])
        mix_ref[...] = mix
        x1 = x_ref[...] + mix * _rms(mix) * g2_ref[...]
        x1_ref[...] = x1
        h2_ref[...] = (x1 * _rms(x1) * g3_ref[...]).astype(BF16)

    return _call(
        body, name="merge_out", grid=(T // tm,),
        in_specs=[_rows(tm, POOL_WIDTH), _rows(tm, ATTN_WIDTH), _rows(tm, GATE_WIDTH), _rows(tm, D_MODEL),
                  _const(w_bp.shape), _const(w_ba.shape), _const((D_MODEL, D_MODEL)),
                  _const((1, D_MODEL)), _const((1, D_MODEL))],
        out_specs=[_rows(tm, D_MODEL)] * 4,
        out_shape=[_sds((T, D_MODEL), BF16), _sds((T, D_MODEL), F32), _sds((T, D_MODEL), F32),
                   _sds((T, D_MODEL), BF16)],
        args=(y_pool, y_attn, gate, x2, w_bp, w_ba, w_out, g2, g3), sem=("parallel",), exchanges=exchanges)


HALF = D_MODEL // 2
TM_MLP = 256


def _mlp_core(h2, x1, mix, tgt, w_up, w_down, g4, g3, g2):
    T = h2.shape[0]
    tm = min(TM_MLP, T)

    def body(h_ref, x1_ref, mix_ref, t_ref, g_ref, g3_ref, g2_ref, ua_hbm, ub_hbm, da_hbm, db_hbm,
             act_ref, dff_ref, dup_ref, dx1_ref, dmix_ref, loss_ref, dg_ref, dg3_ref, dg2_ref,
             wu, wd, relu_scr, sems):
        def weight_copy(i):
            src, dst = ((ua_hbm, wu.at[:, :HALF]), (ub_hbm, wu.at[:, HALF:]),
                        (da_hbm, wd.at[:, :HALF]), (db_hbm, wd.at[:, HALF:]))[i]
            return pltpu.make_async_copy(src, dst, sems.at[i])

        @pl.when(pl.program_id(0) == 0)
        def _():
            for i in range(4):
                weight_copy(i).start()
            loss_ref[...] = jnp.zeros_like(loss_ref)
            for ref in (dg_ref, dg3_ref, dg2_ref):
                ref[...] = jnp.zeros_like(ref)
            for i in range(4):
                weight_copy(i).wait()

        h = h_ref[...]
        ff = None
        for j in range(N_CHIPS):
            lo = D_MODEL * j
            relu = jnp.maximum(_dot(h, wu[j]), 0.0)
            relu_scr[:, lo:lo + D_MODEL] = relu
            act = jnp.square(relu).astype(BF16)
            act_ref[:, lo:lo + D_MODEL] = act
            t = _dot(act, wd[j])
            ff = t if ff is None else ff + t
        g = g_ref[...]
        x1 = x1_ref[...]
        err = x1 + ff * _rms(ff) * g - t_ref[...]
        loss_ref[...] += jnp.sum(err * err) * (0.5 / D_MODEL)
        dy = err * (1.0 / D_MODEL)
        dff, dg = _norm_bwd(ff, g, dy)
        dg_ref[...] += dg
        dff = dff.astype(BF16)
        dff_ref[...] = dff
        dh2 = None
        for j in range(N_CHIPS):
            lo = D_MODEL * j
            dup = (_dot_nt(dff, wd[j]) * (2.0 * relu_scr[:, lo:lo + D_MODEL])).astype(BF16)
            dup_ref[:, lo:lo + D_MODEL] = dup
            t = _dot_nt(dup, wu[j])
            dh2 = t if dh2 is None else dh2 + t
        dx, dg3 = _norm_bwd(x1, g3_ref[...], dh2)
        dx1 = dy + dx
        dx1_ref[...] = dx1
        dg3_ref[...] += dg3
        dmix, dg2 = _norm_bwd(mix_ref[...], g2_ref[...], dx1)
        dmix_ref[...] = dmix.astype(BF16)
        dg2_ref[...] += dg2

    slabs = pltpu.VMEM((N_CHIPS, D_MODEL, D_MODEL), BF16)
    gain = _const((1, D_MODEL))
    return pl.pallas_call(
        body, name="mlp_core", grid=(T // tm,),
        in_specs=[_rows(tm, D_MODEL)] * 4 + [gain] * 3 + [ANY] * 4,
        out_specs=[_rows(tm, D_FF), _rows(tm, D_MODEL), _rows(tm, D_FF), _rows(tm, D_MODEL), _rows(tm, D_MODEL),
                   _const((8, LANES)), gain, gain, gain],
        out_shape=[_sds((T, D_FF), BF16), _sds((T, D_MODEL), BF16), _sds((T, D_FF), BF16), _sds((T, D_MODEL), F32),
                   _sds((T, D_MODEL), BF16), _sds((8, LANES), F32)] + [_sds((1, D_MODEL), F32)] * 3,
        scratch_shapes=[slabs] * 2 + [pltpu.VMEM((tm, D_FF), F32), pltpu.SemaphoreType.DMA((4,))],
        compiler_params=_cp("arbitrary"),
    )(h2, x1, mix, tgt, g4, g3, g2, *w_up, *w_down)


def _dw(tag, a, g, ta, tn, shard_cols=False, exchanges=()):
    T, ka = a.shape
    n = g.shape[1]
    tk = min(2 * TM, T)
    nk = T // tk

    def body(a_ref, g_ref, o_ref):
        @pl.when(pl.program_id(2) == 0)
        def _():
            o_ref[...] = jnp.zeros_like(o_ref)

        o_ref[...] += _dot_tn(a_ref[...], g_ref[...])

    if shard_cols:
        per = (n // N_CHIPS) // tn
        out_spec = pl.BlockSpec((None, ta, tn), lambda i, j, k: (j // per, i, j % per))
        out_shape = _sds((N_CHIPS, ka, n // N_CHIPS), F32)
    else:
        out_spec = pl.BlockSpec((ta, tn), lambda i, j, k: (i, j))
        out_shape = _sds((ka, n), F32)
    return _call(
        body, name="dw_" + tag, grid=(ka // ta, n // tn, nk),
        in_specs=[pl.BlockSpec((tk, ta), lambda i, j, k: (k, i)), pl.BlockSpec((tk, tn), lambda i, j, k: (k, j))],
        out_specs=[out_spec], out_shape=[out_shape],
        args=(a, g), sem=("parallel", "parallel", "arbitrary"), exchanges=exchanges)


def _dw_mix(merged, dmix, y_pool, dbp, y_attn, dba, exchanges=()):
    T = merged.shape[0]
    tk = min(2 * TM, T)
    c = D_MODEL // N_CHIPS

    def body(mg_ref, dmix_ref, yp_ref, dbp_ref, ya_ref, dba_ref, out_ref, bp_ref, ba_ref):
        @pl.when(pl.program_id(0) == 0)
        def _():
            for ref in (out_ref, bp_ref, ba_ref):
                ref[...] = jnp.zeros_like(ref)

        out_ref[...] += _dot_tn(mg_ref[...], dmix_ref[...])
        for y_ref, d_ref, o_ref in ((yp_ref, dbp_ref, bp_ref), (ya_ref, dba_ref, ba_ref)):
            res = _dot_tn(y_ref[...], d_ref[...])
            for j in range(N_CHIPS):
                o_ref[j] += res[:, c * j:c * (j + 1)]

    slabs = (N_CHIPS, POOL_WIDTH, c)
    return _call(
        body, name="dw_mix", grid=(T // tk,),
        in_specs=[_rows(tk, D_MODEL), _rows(tk, D_MODEL), _rows(tk, POOL_WIDTH), _rows(tk, D_MODEL),
                  _rows(tk, ATTN_WIDTH), _rows(tk, D_MODEL)],
        out_specs=[_const((D_MODEL, D_MODEL)), _const(slabs), _const(slabs)],
        out_shape=[_sds((D_MODEL, D_MODEL), F32), _sds(slabs, F32), _sds(slabs, F32)],
        args=(merged, dmix, y_pool, dbp, y_attn, dba), sem=("arbitrary",), exchanges=exchanges)


def _merge_bwd(dmix, gate, y_pool, y_attn, w_out, w_bp, w_ba, exchanges=()):
    T = dmix.shape[0]
    tm = min(TM, T)

    def body(dmix_ref, gate_ref, yp_ref, ya_ref, wo_ref, wbp_ref, wba_ref,
             dbp_ref, dba_ref, dgate_ref, dyp_ref, dya_ref):
        dm = _dot_nt(dmix_ref[...], wo_ref[...])
        for j, (y_ref, db_ref, w_ref, dy_ref) in enumerate(
                ((yp_ref, dbp_ref, wbp_ref, dyp_ref), (ya_ref, dba_ref, wba_ref, dya_ref))):
            sl = slice(D_MODEL * j, D_MODEL * (j + 1))
            gt = gate_ref[:, sl].astype(F32)
            db = (dm * gt).astype(BF16)
            db_ref[...] = db
            dgate_ref[:, sl] = (dm * _branch(y_ref[...], w_ref) * gt * (1.0 - gt)).astype(BF16)
            cw = D_MODEL // N_CHIPS
            dy = _dot_nt(db[:, :cw], w_ref[0])
            for c in range(1, N_CHIPS):
                dy = dy + _dot_nt(db[:, cw * c:cw * (c + 1)], w_ref[c])
            dy_ref[...] = dy.astype(dy_ref.dtype)

    return _call(
        body, name="merge_bwd", grid=(T // tm,),
        in_specs=[_rows(tm, D_MODEL), _rows(tm, GATE_WIDTH), _rows(tm, POOL_WIDTH), _rows(tm, ATTN_WIDTH),
                  _const((D_MODEL, D_MODEL)), _const(w_bp.shape), _const(w_ba.shape)],
        out_specs=[_rows(tm, D_MODEL), _rows(tm, D_MODEL), _rows(tm, GATE_WIDTH), _rows(tm, POOL_WIDTH),
                   _rows(tm, ATTN_WIDTH)],
        out_shape=[_sds((T, D_MODEL), BF16), _sds((T, D_MODEL), BF16), _sds((T, GATE_WIDTH), BF16),
                   _sds((T, POOL_WIDTH), F32), _sds((T, ATTN_WIDTH), BF16)],
        args=(dmix, gate, y_pool, y_attn, w_out, w_bp, w_ba), sem=("parallel",), exchanges=exchanges)


def _mixers_bwd(q, k, v, do, sinks, tabs, dyp, diff, w_pool, pool_scale, seq, exchanges=()):
    T = q.shape[0]
    nb = seq // BLOCK
    bl = T // seq
    steps = nb + 1
    tp = T // nb
    nseq = seq // tp
    per = tp // HALO
    last_halo = T // HALO - 1

    def body(sink_ref, q_ref, do_ref, kp_ref, kc_ref, vp_ref, vc_ref, c_ref, a_ref, bt_ref, cp_ref, ap_ref, btp_ref,
             dy_ref, nxt_ref, diff_ref, w_ref, s_ref,
             dq_ref, dk_ref, dv_ref, dsink_ref, du_ref, dw_ref, ds_ref, ck_ref, cv_ref):
        n = pl.program_id(0)

        @pl.when(n == 0)
        def _():
            for ref in (dsink_ref, ck_ref, cv_ref, dw_ref, ds_ref):
                ref[...] = jnp.zeros_like(ref)

        @pl.when(n < nb)
        def _():
            _pool_bwd_tile(n, tp, nseq, dy_ref, nxt_ref, diff_ref, w_ref, s_ref, du_ref, dw_ref, ds_ref)
            valid, lo = _attn_masks(n)
            for b in range(bl):
                kk = jnp.concatenate([kp_ref[b], kc_ref[b]], axis=0)
                vv = jnp.concatenate([vp_ref[b], vc_ref[b]], axis=0)
                dk_acc = jnp.zeros((2 * BLOCK, KV_WIDTH), F32)
                dv_acc = jnp.zeros((2 * BLOCK, KV_WIDTH), F32)
                for h in range(2):
                    qs = _stack_heads(q_ref.at[b], h, lo)
                    dos = _stack_heads(do_ref.at[b], h, lo)
                    pr, ps = _group_probs(qs, kk, valid, _sink_rows(sink_ref, h))
                    dp = _dot_nt(dos, vv)
                    delta = jnp.sum(pr * dp, axis=1, keepdims=True)
                    ds = (pr * (dp - delta)).astype(BF16)
                    dsk = ps * delta
                    for g in range(GROUP):
                        idx = GROUP * h + g
                        dsink_ref[idx:idx + 1, :] += (jnp.zeros((1, LANES), F32)
                                                      - jnp.sum(dsk[BLOCK * g:BLOCK * (g + 1)]))
                    dk_acc = dk_acc + _dot_tn(ds, qs)
                    dv_acc = dv_acc + _dot_tn(pr.astype(BF16), dos)
                    for j, pair in enumerate(_unstack_heads(_dot(ds, kk) * SCALE, h, lo)):
                        sl = slice(LANES * (2 * h + j), LANES * (2 * h + j + 1))
                        dq_ref[b, :, sl] = _rot_bwd(pair, c_ref[...], a_ref[...], bt_ref[...]).astype(BF16)
                fin_k = ck_ref[b] + dk_acc[:BLOCK]
                dk_ref[b] = _rot_bwd(fin_k, cp_ref[...], ap_ref[...], btp_ref[...]).astype(BF16)
                dv_ref[b] = (cv_ref[b] + dv_acc[:BLOCK]).astype(BF16)
                ck_ref[b] = dk_acc[BLOCK:]
                cv_ref[b] = dv_acc[BLOCK:]

        @pl.when(n == nb)
        def _():
            for b in range(bl):
                dk_ref[b] = _rot_bwd(ck_ref[b], cp_ref[...], ap_ref[...], btp_ref[...]).astype(BF16)
                dv_ref[b] = cv_ref[b].astype(BF16)

    cur = lambda n: (0, jnp.minimum(n, nb - 1), 0)
    prv = lambda n: (0, jnp.clip(n - 1, 0, nb - 1), 0)
    tcur = lambda n: (jnp.minimum(n, nb - 1), 0)
    tprv = lambda n: (jnp.clip(n - 1, 0, nb - 1), 0)
    wide = lambda m: pl.BlockSpec((bl, BLOCK, ATTN_WIDTH), m)
    kv = lambda m: pl.BlockSpec((bl, BLOCK, KV_WIDTH), m)
    tab = lambda m: pl.BlockSpec((BLOCK, LANES), m)
    tile = lambda n: (jnp.minimum(n, nb - 1), 0)
    halo = lambda n: (jnp.minimum((jnp.minimum(n, nb - 1) + 1) * per, last_halo), 0)
    rows = pl.BlockSpec((tp, POOL_WIDTH), tile)
    res = _call(
        body, name="mixers_bwd", grid=(steps,),
        in_specs=[pl.BlockSpec(memory_space=pltpu.SMEM), wide(cur), wide(cur), kv(prv), kv(cur), kv(prv), kv(cur),
                  tab(tcur), tab(tcur), tab(tcur), tab(tprv), tab(tprv), tab(tprv),
                  rows, pl.BlockSpec((HALO, POOL_WIDTH), halo), rows, _const((4, POOL_GC, POOL_GC)),
                  _const((1, POOL_WIDTH))],
        out_specs=[wide(cur), kv(prv), kv(prv), _const((8, LANES)), rows, _const((4, POOL_GC, POOL_GC)),
                   _const((1, POOL_WIDTH))],
        out_shape=[_sds((bl, seq, ATTN_WIDTH), BF16), _sds((bl, seq, KV_WIDTH), BF16),
                   _sds((bl, seq, KV_WIDTH), BF16), _sds((8, LANES), F32), _sds((T, POOL_WIDTH), BF16),
                   _sds((4, POOL_GC, POOL_GC), F32), _sds((1, POOL_WIDTH), F32)],
        scratch=[pltpu.VMEM((bl, BLOCK, KV_WIDTH), F32), pltpu.VMEM((bl, BLOCK, KV_WIDTH), F32)],
        args=(sinks, *_by_example(bl, q, do, k, k, v, v), *tabs, *tabs, dyp, dyp, diff, w_pool, pool_scale),
        sem=("arbitrary",), exchanges=exchanges)
    outs, rest = (res if exchanges else (res, None))
    outs = [outs[0].reshape(T, ATTN_WIDTH), outs[1].reshape(T, KV_WIDTH), outs[2].reshape(T, KV_WIDTH), *outs[3:]]
    return (outs, rest) if exchanges else outs


def _pool_bwd_tile(i, tp, nseq, dy_ref, nxt_ref, diff_ref, w_ref, s_ref, du_ref, dw_ref, ds_ref):
    last = (i % nseq) == nseq - 1
    nxt = jnp.where(last, 0.0, nxt_ref[...])
    ext = jnp.concatenate([dy_ref[...], nxt], axis=0) * s_ref[...]
    pos = (i % nseq) * tp + lax.broadcasted_iota(jnp.int32, (tp + HALO, 1), 0)
    for gi, w in enumerate(POOL_WINDOWS):
        sl = slice(POOL_GC * gi, POOL_GC * (gi + 1))
        wg = w_ref[gi].astype(BF16)
        dmx = ext[:, sl].astype(BF16)
        ddiff = _dot_nt(dmx, wg)
        s = ddiff * _inv_count(pos, w)
        sh = 1
        while sh < w:
            s = s + pltpu.roll(s, tp + HALO - sh, 0)
            sh *= 2
        du_ref[:, sl] = (s[:tp] - ddiff[:tp]).astype(BF16)
        dg = diff_ref[:, sl]
        dw_ref[gi] += _dot_tn(dg, dmx[:tp])
        ds_ref[:, sl] += jnp.sum(dy_ref[:, sl] * _dot(dg, wg), axis=0, keepdims=True)


_PARTS = ((0, C_Q), (C_Q, C_K), (C_K, C_V), (C_V, C_G), (C_G, IN_WIDTH))


def _inproj_bwd(parts, x2, dx1, w_in_t, g1, exchanges=()):
    T = x2.shape[0]
    tm = min(TM, T)

    def body(du_ref, dq_ref, dk_ref, dv_ref, dgt_ref, x_ref, dx1_ref, w_ref, g_ref, gx_ref, dg_ref):
        @pl.when(pl.program_id(0) == 0)
        def _():
            dg_ref[...] = jnp.zeros_like(dg_ref)

        dh = jnp.zeros((tm, D_MODEL), F32)
        for (lo, hi), p_ref in zip(_PARTS, (du_ref, dq_ref, dk_ref, dv_ref, dgt_ref)):
            dh = dh + _dot(p_ref[...], w_ref[lo:hi, :])
        dx, dg = _norm_bwd(x_ref[...], g_ref[...], dh)
        gx_ref[...] = dx1_ref[...] + dx
        dg_ref[...] += dg

    return _call(
        body, name="inproj_bwd", grid=(T // tm,),
        in_specs=[_rows(tm, hi - lo) for lo, hi in _PARTS]
        + [_rows(tm, D_MODEL), _rows(tm, D_MODEL), _const((IN_WIDTH, D_MODEL)), _const((1, D_MODEL))],
        out_specs=[_rows(tm, D_MODEL), _const((1, D_MODEL))],
        out_shape=[_sds((T, D_MODEL), F32), _sds((1, D_MODEL), F32)],
        args=(*parts, x2, dx1, w_in_t, g1), sem=("arbitrary",), exchanges=exchanges)


def _dw_in(h, parts, exchanges=()):
    T = h.shape[0]
    tk = min(TM, T)

    def body(h_ref, du_ref, dq_ref, dk_ref, dv_ref, dgt_ref, o_ref, db_ref):
        @pl.when(pl.program_id(0) == 0)
        def _():
            o_ref[...] = jnp.zeros_like(o_ref)
            db_ref[...] = jnp.zeros_like(db_ref)

        hh = h_ref[...]
        ones = jnp.ones((8, tk), BF16)
        for (lo, hi), p_ref in zip(_PARTS, (du_ref, dq_ref, dk_ref, dv_ref, dgt_ref)):
            part = p_ref[...]
            o_ref[lo:hi, :] += _dot_tn(part, hh)
            db_ref[:, lo:hi] += _dot(ones, part)[:1]

    return _call(
        body, name="dw_in", grid=(T // tk,),
        in_specs=[_rows(tk, D_MODEL)] + [_rows(tk, hi - lo) for lo, hi in _PARTS],
        out_specs=[_const((IN_WIDTH, D_MODEL)), _const((1, IN_WIDTH))],
        out_shape=[_sds((IN_WIDTH, D_MODEL), F32), _sds((1, IN_WIDTH), F32)],
        args=(h, *parts), sem=("arbitrary",), exchanges=exchanges)


def _row_tile(rows, cap=256, mult=16):
    best = None
    for t in range(mult, min(rows, cap) + 1, mult):
        if rows % t == 0:
            best = t
    if best is None:
        raise ValueError("no row tile for %d rows" % rows)
    return best


def _pair_sum(ids, full, got):
    _, r, c = full.shape
    hr = r // 2
    tr = _row_tile(hr, cap=512)
    nblk = hr // tr

    def body(ids_ref, a_ref, b_ref, own_ref, sb_ref):
        s = a_ref[...] + b_ref[...]
        sb_ref[...] = s.astype(BF16)

        @pl.when(pl.program_id(1) == ids_ref[0])
        def _():
            own_ref[...] = s

    slab = pl.BlockSpec((None, tr, c), lambda i, j, ids_ref: (j, i, 0))
    return pl.pallas_call(
        body, name="pair_sum_%dx%d" % (r, c),
        grid_spec=pltpu.PrefetchScalarGridSpec(
            num_scalar_prefetch=1, grid=(nblk, N_CHIPS),
            in_specs=[pl.BlockSpec((None, tr, c), lambda i, j, ids_ref: (j, ids_ref[1] * nblk + i, 0)), slab],
            out_specs=[pl.BlockSpec((tr, c), lambda i, j, ids_ref: (i, 0)), slab]),
        out_shape=[_sds((hr, c), F32), _sds((N_CHIPS, hr, c), BF16)],
        compiler_params=_cp("parallel", "arbitrary"),
    )(ids, full, got)


def _pair_sum_small(ids, fulls, gots):
    n = len(fulls)
    dims = [(f.shape[1] // 2, f.shape[2]) for f in fulls]

    def body(ids_ref, *refs):
        ins, outs = refs[:2 * n], refs[2 * n:]
        for k in range(n):
            s = ins[2 * k][...] + ins[2 * k + 1][...]
            outs[2 * k + 1][...] = s.astype(BF16)

            @pl.when(pl.program_id(0) == ids_ref[0])
            def _(k=k, s=s):
                outs[2 * k][...] = s

    in_specs, out_specs, out_shape = [], [], []
    for hr, c in dims:
        slab = pl.BlockSpec((None, hr, c), lambda j, ids_ref: (j, 0, 0))
        in_specs += [pl.BlockSpec((None, hr, c), lambda j, ids_ref: (j, ids_ref[1], 0)), slab]
        out_specs += [pl.BlockSpec((hr, c), lambda j, ids_ref: (0, 0)), slab]
        out_shape += [_sds((hr, c), F32), _sds((N_CHIPS, hr, c), BF16)]
    res = pl.pallas_call(
        body, name="pair_sum_small",
        grid_spec=pltpu.PrefetchScalarGridSpec(num_scalar_prefetch=1, grid=(N_CHIPS,), in_specs=in_specs,
                                               out_specs=out_specs),
        out_shape=out_shape, compiler_params=_cp("arbitrary"),
    )(ids, *[a for pair in zip(fulls, gots) for a in pair])
    return [(res[2 * k], res[2 * k + 1]) for k in range(n)]


def _chip_sum_small(ids, owns, gots):
    n = len(owns)

    def body(ids_ref, *refs):
        ins, outs = refs[:2 * n], refs[2 * n:]
        for k in range(n):
            a, b = ins[2 * k], ins[2 * k + 1]
            outs[k][...] = ((a[...] + b[0].astype(F32)) + b[1].astype(F32)) + b[2].astype(F32)

    in_specs, out_specs, out_shape = [], [], []
    for own in owns:
        hr, c = own.shape
        in_specs += [pl.BlockSpec((hr, c), lambda i, ids_ref: (0, 0)),
                     pl.BlockSpec((3, hr, c), lambda i, ids_ref: (0, 0, 0))]
        out_specs.append(pl.BlockSpec((hr, c), lambda i, ids_ref: (ids_ref[1], 0)))
        out_shape.append(_sds((2 * hr, c), F32))
    return pl.pallas_call(
        body, name="chip_sum_small",
        grid_spec=pltpu.PrefetchScalarGridSpec(num_scalar_prefetch=1, grid=(1,), in_specs=in_specs,
                                               out_specs=out_specs),
        out_shape=out_shape, compiler_params=_cp("arbitrary"),
    )(ids, *[a for pair in zip(owns, gots) for a in pair])


def _chip_sum(ids, own, got):
    hr, c = own.shape
    tr = _row_tile(hr)
    nblk = hr // tr

    def body(ids_ref, a_ref, b_ref, o_ref):
        o_ref[...] = ((a_ref[...] + b_ref[0].astype(F32)) + b_ref[1].astype(F32)) + b_ref[2].astype(F32)

    return pl.pallas_call(
        body, name="chip_sum_%dx%d" % (hr, c),
        grid_spec=pltpu.PrefetchScalarGridSpec(
            num_scalar_prefetch=1, grid=(nblk,),
            in_specs=[pl.BlockSpec((tr, c), lambda i, ids_ref: (i, 0)),
                      pl.BlockSpec((3, tr, c), lambda i, ids_ref: (0, i, 0))],
            out_specs=pl.BlockSpec((tr, c), lambda i, ids_ref: (ids_ref[1] * nblk + i, 0))),
        out_shape=_sds((2 * hr, c), F32),
        compiler_params=_cp("parallel"),
    )(ids, own, got)


def _adamw_math(w, g, m, v):
    nm = ADAM_B1 * m + (1.0 - ADAM_B1) * g
    nv = ADAM_B2 * v + (1.0 - ADAM_B2) * (g * g)
    m_hat = nm / (1.0 - ADAM_B1 ** ADAM_STEP)
    v_hat = nv / (1.0 - ADAM_B2 ** ADAM_STEP)
    return -ADAM_LR * (m_hat / (jnp.sqrt(v_hat) + ADAM_EPS) + ADAM_WD * w), nm, nv


def _adamw(w, g, m, v):
    r, c = w.shape
    tr = _row_tile(r, cap=512, mult=8)

    def body(w_ref, g_ref, m_ref, v_ref, d_ref, nm_ref, nv_ref):
        d_ref[...], nm_ref[...], nv_ref[...] = _adamw_math(w_ref[...], g_ref[...], m_ref[...], v_ref[...])

    spec = _rows(tr, c)
    return pl.pallas_call(
        body, name="adamw_%dx%d" % (r, c), grid=(r // tr,),
        in_specs=[spec] * 4, out_specs=[spec] * 3, out_shape=[_sds((r, c), F32)] * 3,
        compiler_params=_cp("parallel"),
    )(w, g, m, v)


SC_TILES = 32
SC_LANES = 16
SC_ROWS = 8


def _adamw_sparse(w, g, m, v):
    r, c = w.shape
    rows = r // SC_TILES
    step = min(rows, SC_ROWS)

    def body(w_hbm, g_hbm, m_hbm, v_hbm, d_hbm, nm_hbm, nv_hbm, wb, gb, mb, vb):
        tile = lax.axis_index("sc_subcore") * 2 + lax.axis_index("sc_core")

        @pl.loop(0, rows, step=step)
        def _(r0):
            mine = pl.ds(tile * rows + r0, step)
            for src, dst in ((w_hbm, wb), (g_hbm, gb), (m_hbm, mb), (v_hbm, vb)):
                pltpu.sync_copy(src.at[mine], dst)

            @pl.loop(0, step)
            def _(row):
                @pl.loop(0, c, step=SC_LANES)
                def _(i):
                    at = (row, pl.ds(i, SC_LANES))
                    wb[at], mb[at], vb[at] = _adamw_math(wb[at], gb[at], mb[at], vb[at])

            for src, dst in ((wb, d_hbm), (mb, nm_hbm), (vb, nv_hbm)):
                pltpu.sync_copy(src, dst.at[mine])

    return pl.kernel(
        body, name="adamw_sparse_%dx%d" % (r, c), out_type=[_sds((r, c), F32)] * 3,
        mesh=plsc.VectorSubcoreMesh(core_axis_name="sc_core", subcore_axis_name="sc_subcore"),
        scratch_types=[pltpu.VMEM((step, c), F32)] * 4,
    )(w, g, m, v)


_SMALL_NAMES = ("w_pool", "b_in", "g_mix_pre", "g_mix_post", "g_mlp_pre", "g_mlp_post", "pool_scale", "attn_sinks")
B_ROWS = -(-IN_WIDTH // D_MODEL)


def _row_block(rows):
    rows = [jnp.pad(r.astype(F32), ((0, 0), (0, D_MODEL - r.shape[1]))) for r in rows]
    return jnp.pad(jnp.concatenate(rows, axis=0), ((0, 8 - len(rows)), (0, 0)))


def _early_block(dg2, dg3, dg4, dps, dsink, loss):
    tail = jnp.concatenate([jnp.pad(dsink.reshape(1, -1), ((0, 0), (0, LANES - dsink.size))),
                            jnp.pad(loss.reshape(1, 1), ((0, 0), (0, LANES - 1)))], axis=1)
    return _row_block([dg2, dg3, dg4, dps, tail])


def _late_block(db_in, dg1):
    b = jnp.pad(db_in, ((0, 0), (0, B_ROWS * D_MODEL - IN_WIDTH))).reshape(B_ROWS, D_MODEL)
    return _row_block([b[r:r + 1] for r in range(B_ROWS)] + [dg1])


def _small_update(gearly, gmat, glate, w, m, v):
    names = _SMALL_NAMES
    n = len(names)

    def total(ref, rows):
        acc = ref[0:rows, :]
        for d in range(1, N_DEV):
            acc = acc + ref[d * rows:(d + 1) * rows, :]
        return acc

    def body(*refs):
        early_ref, gmat_ref, late_ref = refs[:3]
        w_refs, m_refs, v_refs = refs[3:3 + n], refs[3 + n:3 + 2 * n], refs[3 + 2 * n:3 + 3 * n]
        outs = refs[3 + 3 * n:]
        loss_ref, g_refs, d_refs = outs[0], outs[1:1 + n], outs[1 + n:1 + 2 * n]
        nm_refs, nv_refs = outs[1 + 2 * n:1 + 3 * n], outs[1 + 3 * n:1 + 4 * n]
        early, late = total(early_ref, 8), total(late_ref, 8)
        loss_ref[...] = jnp.sum(early[4:5, LANES:2 * LANES], axis=1, keepdims=True)
        bias = jnp.concatenate([late[r:r + 1, :] for r in range(B_ROWS - 1)]
                               + [late[B_ROWS - 1:B_ROWS, :IN_WIDTH - (B_ROWS - 1) * D_MODEL]], axis=1)
        grad = dict(b_in=bias, g_mix_pre=late[B_ROWS:B_ROWS + 1, :], g_mix_post=early[0:1, :],
                    g_mlp_pre=early[1:2, :], g_mlp_post=early[2:3, :], pool_scale=early[3:4, :POOL_WIDTH],
                    attn_sinks=early[4:5, :N_Q_HEADS])
        for i, name in enumerate(names):
            g = total(gmat_ref, 4 * POOL_GC) if name == "w_pool" else grad[name]
            g_refs[i][...] = g
            d_refs[i][...], nm_refs[i][...], nv_refs[i][...] = _adamw_math(
                w_refs[i][...], g, m_refs[i][...], v_refs[i][...])

    shapes = [_sds(w[k].shape, F32) for k in names]
    res = pl.pallas_call(
        body, name="small_update", out_shape=[_sds((1, 1), F32)] + shapes * 4,
        compiler_params=pltpu.CompilerParams(vmem_limit_bytes=VMEM_MB * 1024 * 1024),
    )(gearly, gmat, glate, *[w[k] for k in names], *[m[k] for k in names], *[v[k] for k in names])
    loss = res[0]
    per = {k: tuple(res[1 + j * n + i] for j in range(4)) for i, k in enumerate(names)}
    return loss, per


_BIG = ("w_in", "w_branch_pool", "w_branch_attn", "w_out", "w_up", "w_down")
_ORDER = ("g_mix_pre", "w_in", "b_in", "w_pool", "pool_scale", "attn_sinks", "w_branch_pool", "w_branch_attn",
          "w_out", "g_mix_post", "g_mlp_pre", "w_up", "w_down", "g_mlp_post")


def _stack_rows(slab):
    return slab.reshape(-1, slab.shape[2])


def _step(x2, tgt, seq, shards, small, ids):
    tabs = _rope_tables(seq)
    g1, g2, g3, g4 = (small[n] for n in ("g_mix_pre", "g_mix_post", "g_mlp_pre", "g_mlp_post"))
    sinks = small["attn_sinks"].reshape(N_Q_HEADS)
    w_pool = small["w_pool"].reshape(4, POOL_GC, POOL_GC)
    pool_scale = small["pool_scale"]

    def whole(shard, slabs):
        return lax.dynamic_update_slice(slabs, shard[None], (ids[0], 0, 0))

    (up_a, up_b, down_a, down_b, *mix_shards), [[in_slab]] = _cast_shards(
        *(shards[n] for n in ("w_up", "w_down", "w_branch_pool", "w_branch_attn", "w_out")),
        exchanges=[_ex_gather([shards["w_in"]])])
    w_in = _stack_rows(whole(shards["w_in"], in_slab))
    (h, u, q, k, v, gate), [mix_slabs] = _inproj(
        x2, g1, w_in, small["b_in"], tabs, seq, exchanges=[_ex_gather(mix_shards)])
    w_bp, w_ba, out_slab = (whole(s, g) for s, g in zip(mix_shards, mix_slabs))
    w_out = _stack_rows(out_slab)
    (y_attn, diff, y_pool), [[got_a, got_b]] = _mixers_fwd(
        q, k, v, sinks, u, w_pool, pool_scale, seq, exchanges=[_ex_gather([up_a, up_b])])
    (merged, mix, x1, h2), [[got_c, got_d]] = _merge_out(
        y_pool, y_attn, gate, x2, w_bp, w_ba, w_out, g2, g3, exchanges=[_ex_gather([down_a, down_b])])
    w_up = (whole(up_a, got_a), whole(up_b, got_b))
    w_down = (whole(down_a, got_c), whole(down_b, got_d))
    act, dff, dup, dx1, dmix, loss_acc, dg4, dg3, dg2 = _mlp_core(h2, x1, mix, tgt, w_up, w_down, g4, g3, g2)

    dw_down = _dw("down", act, dff, 2048, 1024)[0].reshape(N_CHIPS, D_FF // N_CHIPS, D_MODEL)
    (dbp, dba, dgate, dyp, dya), [[got]] = _merge_bwd(
        dmix, gate, y_pool, y_attn, w_out, w_bp, w_ba, exchanges=[_ex_pair([dw_down])])
    ps_down = _pair_sum(ids, dw_down, got)
    (dw_up,), [[got]] = _dw("up", h2, dup, 1024, 1024, shard_cols=True, exchanges=[_ex_chip([ps_down[1]])])
    half_down = _chip_sum(ids, ps_down[0], got)
    (dw_out, dw_bp, dw_ba), [[got]] = _dw_mix(merged, dmix, y_pool, dbp, y_attn, dba, exchanges=[_ex_pair([dw_up])])
    ps_up = _pair_sum(ids, dw_up, got)
    dw_mix = [dw_out.reshape(N_CHIPS, D_MODEL // N_CHIPS, D_MODEL), dw_bp, dw_ba]
    (dq, dk, dv, dsink, du, dw_pool, dps), [[got], gots, [g_down]] = _mixers_bwd(
        q, k, v, dya, sinks, tabs, dyp, diff, w_pool, pool_scale, seq,
        exchanges=[_ex_chip([ps_up[1]]), _ex_pair(dw_mix), _ex_swap([half_down])])
    half_up = _chip_sum(ids, ps_up[0], got)
    ps_mix = _pair_sum_small(ids, dw_mix, gots)
    parts = (du, dq, dk, dv, dgate)
    early = _early_block(dg2, dg3, dg4, dps, dsink[:, 0], loss_acc[0, 0])
    mat = dw_pool.reshape(4 * POOL_GC, POOL_GC)
    (dw_in_t, db_in), [gots, [gearly, gmat], [g_up]] = _dw_in(
        h, parts, exchanges=[_ex_chip([p[1] for p in ps_mix]), _ex_allgather([early, mat]), _ex_swap([half_up])])
    half_mix = _chip_sum_small(ids, [p[0] for p in ps_mix], gots)
    dw_in = dw_in_t.reshape(N_CHIPS, IN_WIDTH // N_CHIPS, D_MODEL)
    g_mix, [got] = _alone("swap_mix_pair_in", _ex_swap(half_mix), _ex_pair([dw_in]))
    ps_in = _pair_sum(ids, dw_in, got)
    (gx, dg1), [[got]] = _inproj_bwd(parts, x2, dx1, w_in, g1, exchanges=[_ex_chip([ps_in[1]])])
    [g_in], [glate] = _alone("swap_in_allgather", _ex_swap([_chip_sum(ids, ps_in[0], got)]),
                             _ex_allgather([_late_block(db_in, dg1)]))

    grads = dict(w_in=g_in, w_branch_pool=g_mix[1], w_branch_attn=g_mix[2], w_out=g_mix[0], w_up=g_up, w_down=g_down)
    return (gearly, gmat, glate), gx, grads


def kernel(x, g_mix_pre, w_in, b_in, w_pool, pool_scale, attn_sinks, w_branch_pool, w_branch_attn, w_out, g_mix_post, g_mlp_pre, w_up, w_down, g_mlp_post, loss_target, m_g_mix_pre, m_w_in, m_b_in, m_w_pool, m_pool_scale, m_attn_sinks, m_w_branch_pool, m_w_branch_attn, m_w_out, m_g_mix_post, m_g_mlp_pre, m_w_up, m_w_down, m_g_mlp_post, v_g_mix_pre, v_w_in, v_b_in, v_w_pool, v_pool_scale, v_attn_sinks, v_w_branch_pool, v_w_branch_attn, v_w_out, v_g_mix_post, v_g_mlp_pre, v_w_up, v_w_down, v_g_mlp_post):
    weights = dict(g_mix_pre=g_mix_pre, w_in=w_in, b_in=b_in, w_pool=w_pool, pool_scale=pool_scale,
                   attn_sinks=attn_sinks, w_branch_pool=w_branch_pool, w_branch_attn=w_branch_attn, w_out=w_out,
                   g_mix_post=g_mix_post, g_mlp_pre=g_mlp_pre, w_up=w_up, w_down=w_down, g_mlp_post=g_mlp_post)
    mom1 = dict(g_mix_pre=m_g_mix_pre, w_in=m_w_in, b_in=m_b_in, w_pool=m_w_pool, pool_scale=m_pool_scale,
                attn_sinks=m_attn_sinks, w_branch_pool=m_w_branch_pool, w_branch_attn=m_w_branch_attn,
                w_out=m_w_out, g_mix_post=m_g_mix_post, g_mlp_pre=m_g_mlp_pre, w_up=m_w_up, w_down=m_w_down,
                g_mlp_post=m_g_mlp_post)
    mom2 = dict(g_mix_pre=v_g_mix_pre, w_in=v_w_in, b_in=v_b_in, w_pool=v_w_pool, pool_scale=v_pool_scale,
                attn_sinks=v_attn_sinks, w_branch_pool=v_w_branch_pool, w_branch_attn=v_w_branch_attn,
                w_out=v_w_out, g_mix_post=v_g_mix_post, g_mlp_pre=v_g_mlp_pre, w_up=v_w_up, w_down=v_w_down,
                g_mlp_post=v_g_mlp_post)
    b_loc, seq, _ = x.shape
    x2 = x.reshape(b_loc * seq, D_MODEL)
    tgt = loss_target.reshape(b_loc * seq, D_MODEL)
    ids = jnp.stack([2 * lax.axis_index("x") + lax.axis_index("y"), lax.axis_index("c")]).astype(jnp.int32)

    def flat(n, a):
        return a[0].T if n == "w_in" else a[0]

    def unflat(n, a):
        return (a.T if n == "w_in" else a)[None]

    shards = {n: flat(n, weights[n]).astype(BF16) if n == "w_in" else flat(n, weights[n]) for n in _BIG}
    small = {n: weights[n] for n in _ORDER if n not in _BIG}
    (gearly, gmat, glate), gx, grads = _step(x2, tgt, seq, shards, small, ids)

    def two_d(src):
        return {n: src[n].reshape(4 * POOL_GC, POOL_GC) if n == "w_pool" else src[n] for n in _SMALL_NAMES}

    loss, per = _small_update(gearly, gmat, glate, two_d(weights), two_d(mom1), two_d(mom2))
    delta, new_m, new_v = {}, {}, {}
    for n in _SMALL_NAMES:
        grads[n], delta[n], new_m[n], new_v[n] = (a.reshape(weights[n].shape) for a in per[n])
    for n in _BIG:
        update = _adamw if n == "w_in" else _adamw_sparse
        d, nm, nv = update(flat(n, weights[n]), grads[n], flat(n, mom1[n]), flat(n, mom2[n]))
        grads[n] = unflat(n, grads[n])
        delta[n], new_m[n], new_v[n] = unflat(n, d), unflat(n, nm), unflat(n, nv)

    return (loss[0, 0], gx.reshape(x.shape), *[grads[n] for n in _ORDER], *[delta[n] for n in _ORDER],
            *[new_m[n] for n in _ORDER], *[new_v[n] for n in _ORDER])
```
